```python
import math
import jax, jax.numpy as jnp
from jax import lax
import numpy as np

D_MODEL = 1024
BATCH = 16
SEQ = 2048
DEPTH = 1

CHUNK = 64
RET_HEADS = 4
RET_KEY_DIM = 128
RET_VAL_DIM = 256
RET_QK = RET_HEADS * RET_KEY_DIM
RET_V = RET_HEADS * RET_VAL_DIM
ATT_HEADS = 8
ATT_HEAD_DIM = 64
ATT_W = ATT_HEADS * ATT_HEAD_DIM
BAND_CHUNKS = 8
BAND = (BAND_CHUNKS + 1) * CHUNK
MAX_REL = 256
N_REL = CHUNK + MAX_REL
D_FF = -(-8 * D_MODEL // (3 * 256)) * 256
ROPE_BASE = 10000.0
EPS = 1e-6
NEG_INF = -1e30
IN_SIZES = (RET_QK, RET_QK, RET_V, RET_V, ATT_W, ATT_W, ATT_W, 2 * D_MODEL)
IN_SPLITS = tuple(int(s) for s in np.cumsum(IN_SIZES)[:-1])
N_IN = int(sum(IN_SIZES))

kernel_name = "hybrid_retention_chunkattn_gated_block"


def rmsnorm(x, g):
    xf = x.astype(jnp.float32)
    y = xf * lax.rsqrt(jnp.mean(xf * xf, axis=-1, keepdims=True) + EPS)
    return (y * g.astype(jnp.float32)).astype(x.dtype)


def rotary(x, pos):
    d = x.shape[-1]
    freqs = ROPE_BASE ** (-jnp.arange(0, d, 2, dtype=jnp.float32) / d)
    ang = pos[:, None] * freqs[None, :]
    cos = jnp.cos(ang)[None, :, None, :].astype(x.dtype)
    sin = jnp.sin(ang)[None, :, None, :].astype(x.dtype)
    x1, x2 = x[..., : d // 2], x[..., d // 2:]
    return jnp.concatenate([x1 * cos - x2 * sin, x1 * sin + x2 * cos], axis=-1)


def retention(q, k, v):
    B, S, H, dk = q.shape
    dv = v.shape[-1]
    nc = S // CHUNK
    dt = q.dtype
    log_g = jnp.log(1.0 - 2.0 ** (-5.0 - jnp.arange(H, dtype=jnp.float32)))
    p = jnp.arange(CHUNK, dtype=jnp.float32)
    intra = jnp.exp(log_g[:, None, None] * jnp.abs(p[:, None] - p[None, :])).astype(dt)
    q_dec = jnp.exp(log_g[None, :] * (p[:, None] + 1.0)).astype(dt)[None, :, :, None]
    k_dec = jnp.exp(log_g[None, :] * (CHUNK - 1.0 - p[:, None])).astype(dt)[None, :, :, None]
    chunk_dec = jnp.exp(log_g * CHUNK).astype(dt)[None, :, None, None]
    k = k * jnp.asarray(RET_KEY_DIM ** -0.5, dt)

    def to_chunks(t):
        return t.reshape(B, nc, CHUNK, H, t.shape[-1]).transpose(1, 0, 2, 3, 4)

    def step(state, inp):
        qi, ki, vi = inp
        s = jnp.einsum('bnhd,bmhd->bhnm', qi, ki) * intra[None]
        o = jnp.einsum('bhnm,bmhe->bnhe', s, vi)
        o = o + jnp.einsum('bnhd,bhde->bnhe', qi * q_dec, state)
        new_state = state * chunk_dec + jnp.einsum('bmhd,bmhe->bhde', ki * k_dec, vi)
        return new_state, o

    state0 = jnp.zeros((B, H, dk, dv), dt)
    _, o = lax.scan(step, state0, (to_chunks(q), to_chunks(k), to_chunks(v)))
    return o.transpose(1, 0, 2, 3, 4).reshape(B, S, H, dv)


def head_groupnorm(o):
    of = o.astype(jnp.float32)
    mu = jnp.mean(of, axis=-1, keepdims=True)
    var = jnp.mean(jnp.square(of - mu), axis=-1, keepdims=True)
    return ((of - mu) * lax.rsqrt(var + EPS)).astype(o.dtype)


def chunk_band_attention(q, k, v, rel_bias):
    B, S, H, dh = q.shape
    nc = S // CHUNK
    pad = BAND_CHUNKS * CHUNK
    kp = jnp.pad(k, ((0, 0), (pad, 0), (0, 0), (0, 0)))
    vp = jnp.pad(v, ((0, 0), (pad, 0), (0, 0), (0, 0)))
    n = jnp.arange(CHUNK)
    j = jnp.arange(BAND)
    rel = (pad + n[:, None]) - j[None, :]
    idx = jnp.clip(rel, -(CHUNK - 1), MAX_REL) + (CHUNK - 1)
    bias = rel_bias.astype(jnp.float32)[:, idx]
    scale = ATT_HEAD_DIM ** -0.5
    qc = q.reshape(B, nc, CHUNK, H, dh).transpose(1, 0, 2, 3, 4)

    def one_chunk(args):
        i, qi = args
        ki = lax.dynamic_slice_in_dim(kp, i * CHUNK, BAND, axis=1)
        vi = lax.dynamic_slice_in_dim(vp, i * CHUNK, BAND, axis=1)
        s = jnp.einsum('bnhd,bmhd->bhnm', qi, ki).astype(jnp.float32) * scale + bias[None]
        valid = j >= (BAND_CHUNKS - i) * CHUNK
        s = jnp.where(valid[None, None, None, :], s, NEG_INF)
        pr = jax.nn.softmax(s, axis=-1).astype(vi.dtype)
        return jnp.einsum('bhnm,bmhd->bnhd', pr, vi)

    o = lax.map(one_chunk, (jnp.arange(nc), qc))
    return o.transpose(1, 0, 2, 3, 4).reshape(B, S, H * dh)


def _fwd_setup_inputs(seed: int = 0) -> dict:
    key = jax.random.key(seed)
    ks = jax.random.split(key, 16)
    f32 = jnp.float32

    def w(k, shape, fan_in):
        return jax.random.normal(k, shape, f32) * (fan_in ** -0.5)

    return {
        "x": jax.random.normal(ks[0], (BATCH, SEQ, D_MODEL), f32),
        "norm_mix": 1.0 + 0.05 * jax.random.normal(ks[1], (DEPTH, D_MODEL), f32),
        "w_in": w(ks[2], (DEPTH, D_MODEL, N_IN), D_MODEL),
        "b_gate": 0.02 * jax.random.normal(ks[3], (DEPTH, 2 * D_MODEL), f32),
        "rel_bias": 0.1 * jax.random.normal(ks[4], (DEPTH, ATT_HEADS, N_REL), f32),
        "w_ret_out": w(ks[5], (DEPTH, RET_V, D_MODEL), RET_V),
        "w_att_out": w(ks[6], (DEPTH, ATT_W, D_MODEL), ATT_W),
        "w_out": w(ks[7], (DEPTH, D_MODEL, D_MODEL), D_MODEL),
        "norm_ffn": 1.0 + 0.05 * jax.random.normal(ks[8], (DEPTH, D_MODEL), f32),
        "w_ffn_gate": w(ks[9], (DEPTH, D_MODEL, D_FF), D_MODEL),
        "w_ffn_up": w(ks[10], (DEPTH, D_MODEL, D_FF), D_MODEL),
        "w_ffn_down": w(ks[11], (DEPTH, D_FF, D_MODEL), D_FF),
        "norm_final": 1.0 + 0.05 * jax.random.normal(ks[12], (D_MODEL,), f32),
    }


def _fwd_reference(x, norm_mix, w_in, b_gate, rel_bias, w_ret_out, w_att_out, w_out,
              norm_ffn, w_ffn_gate, w_ffn_up, w_ffn_down, norm_final):
    B, S, _ = x.shape
    pos = jnp.arange(S, dtype=jnp.float32)
    h = x
    for l in range(DEPTH):
        xn = rmsnorm(h, norm_mix[l])
        proj = xn @ w_in[l]
        rq, rk, rv, rg, aq, ak, av, gl = jnp.split(proj, IN_SPLITS, axis=-1)
        rq = rotary(rq.reshape(B, S, RET_HEADS, RET_KEY_DIM), pos)
        rk = rotary(rk.reshape(B, S, RET_HEADS, RET_KEY_DIM), pos)
        rv = rv.reshape(B, S, RET_HEADS, RET_VAL_DIM)
        ro = head_groupnorm(retention(rq, rk, rv)).reshape(B, S, RET_V)
        y_ret = (jax.nn.silu(rg) * ro) @ w_ret_out[l]
        ao = chunk_band_attention(aq.reshape(B, S, ATT_HEADS, ATT_HEAD_DIM),
                                  ak.reshape(B, S, ATT_HEADS, ATT_HEAD_DIM),
                                  av.reshape(B, S, ATT_HEADS, ATT_HEAD_DIM),
                                  rel_bias[l])
        y_att = ao @ w_att_out[l]
        gates = jax.nn.sigmoid(gl + b_gate[l])
        g_ret, g_att = gates[..., :D_MODEL], gates[..., D_MODEL:]
        h = h + (g_ret * y_ret + g_att * y_att) @ w_out[l]
        hn = rmsnorm(h, norm_ffn[l])
        h = h + (jax.nn.silu(hn @ w_ffn_gate[l]) * (hn @ w_ffn_up[l])) @ w_ffn_down[l]
    return rmsnorm(h, norm_final)


import jax as _jax
import jax.numpy as _jnp

TWIN_FORMAT = 'train_step'
FWD_PARAMS = ['x', 'norm_mix', 'w_in', 'b_gate', 'rel_bias', 'w_ret_out', 'w_att_out', 'w_out', 'norm_ffn', 'w_ffn_gate', 'w_ffn_up', 'w_ffn_down', 'norm_final']
TWIN_WEIGHTS = ['norm_mix', 'w_in', 'b_gate', 'rel_bias', 'w_ret_out', 'w_att_out', 'w_out', 'norm_ffn', 'w_ffn_gate', 'w_ffn_up', 'w_ffn_down', 'norm_final']
TWIN_DIFF_INPUT = 'x'
TWIN_INPUTS = ['x', 'norm_mix', 'w_in', 'b_gate', 'rel_bias', 'w_ret_out', 'w_att_out', 'w_out', 'norm_ffn', 'w_ffn_gate', 'w_ffn_up', 'w_ffn_down', 'norm_final', 'loss_target', 'm_norm_mix', 'm_w_in', 'm_b_gate', 'm_rel_bias', 'm_w_ret_out', 'm_w_att_out', 'm_w_out', 'm_norm_ffn', 'm_w_ffn_gate', 'm_w_ffn_up', 'm_w_ffn_down', 'm_norm_final', 'v_norm_mix', 'v_w_in', 'v_b_gate', 'v_rel_bias', 'v_w_ret_out', 'v_w_att_out', 'v_w_out', 'v_norm_ffn', 'v_w_ffn_gate', 'v_w_ffn_up', 'v_w_ffn_down', 'v_norm_final']
TWIN_OUTPUTS = ['loss', 'grad_x', 'grad_norm_mix', 'grad_w_in', 'grad_b_gate', 'grad_rel_bias', 'grad_w_ret_out', 'grad_w_att_out', 'grad_w_out', 'grad_norm_ffn', 'grad_w_ffn_gate', 'grad_w_ffn_up', 'grad_w_ffn_down', 'grad_norm_final', 'delta_norm_mix', 'delta_w_in', 'delta_b_gate', 'delta_rel_bias', 'delta_w_ret_out', 'delta_w_att_out', 'delta_w_out', 'delta_norm_ffn', 'delta_w_ffn_gate', 'delta_w_ffn_up', 'delta_w_ffn_down', 'delta_norm_final', 'new_m_norm_mix', 'new_m_w_in', 'new_m_b_gate', 'new_m_rel_bias', 'new_m_w_ret_out', 'new_m_w_att_out', 'new_m_w_out', 'new_m_norm_ffn', 'new_m_w_ffn_gate', 'new_m_w_ffn_up', 'new_m_w_ffn_down', 'new_m_norm_final', 'new_v_norm_mix', 'new_v_w_in', 'new_v_b_gate', 'new_v_rel_bias', 'new_v_w_ret_out', 'new_v_w_att_out', 'new_v_w_out', 'new_v_norm_ffn', 'new_v_w_ffn_gate', 'new_v_w_ffn_up', 'new_v_w_ffn_down', 'new_v_norm_final']
TWIN_LEAF_KINDS = {'loss': 'loss', 'grad_x': 'grad_x', 'grad_norm_mix': 'grad_w', 'grad_w_in': 'grad_w', 'grad_b_gate': 'grad_w', 'grad_rel_bias': 'grad_w', 'grad_w_ret_out': 'grad_w', 'grad_w_att_out': 'grad_w', 'grad_w_out': 'grad_w', 'grad_norm_ffn': 'grad_w', 'grad_w_ffn_gate': 'grad_w', 'grad_w_ffn_up': 'grad_w', 'grad_w_ffn_down': 'grad_w', 'grad_norm_final': 'grad_w', 'delta_norm_mix': 'delta_w', 'delta_w_in': 'delta_w', 'delta_b_gate': 'delta_w', 'delta_rel_bias': 'delta_w', 'delta_w_ret_out': 'delta_w', 'delta_w_att_out': 'delta_w', 'delta_w_out': 'delta_w', 'delta_norm_ffn': 'delta_w', 'delta_w_ffn_gate': 'delta_w', 'delta_w_ffn_up': 'delta_w', 'delta_w_ffn_down': 'delta_w', 'delta_norm_final': 'delta_w', 'new_m_norm_mix': 'new_m', 'new_m_w_in': 'new_m', 'new_m_b_gate': 'new_m', 'new_m_rel_bias': 'new_m', 'new_m_w_ret_out': 'new_m', 'new_m_w_att_out': 'new_m', 'new_m_w_out': 'new_m', 'new_m_norm_ffn': 'new_m', 'new_m_w_ffn_gate': 'new_m', 'new_m_w_ffn_up': 'new_m', 'new_m_w_ffn_down': 'new_m', 'new_m_norm_final': 'new_m', 'new_v_norm_mix': 'new_v', 'new_v_w_in': 'new_v', 'new_v_b_gate': 'new_v', 'new_v_rel_bias': 'new_v', 'new_v_w_ret_out': 'new_v', 'new_v_w_att_out': 'new_v', 'new_v_w_out': 'new_v', 'new_v_norm_ffn': 'new_v', 'new_v_w_ffn_gate': 'new_v', 'new_v_w_ffn_up': 'new_v', 'new_v_w_ffn_down': 'new_v', 'new_v_norm_final': 'new_v'}


def _forward(args):
    return _fwd_reference(*[args[k] for k in FWD_PARAMS])


def _output_shape():
    out = _jax.eval_shape(lambda: _forward(_fwd_setup_inputs(0)))
    return out.shape, out.dtype

N_MICROBATCH = 1
ADAM_LR = 0.001
ADAM_B1 = 0.9
ADAM_B2 = 0.999
ADAM_EPS = 1e-08
ADAM_WD = 0.01
ADAM_STEP = 10
PER_EXAMPLE_BATCH_AXIS = {'x': 0, 'loss_target': 0}
SHARED_INPUTS = []
_WEIGHT_DTYPES = {'norm_mix': _jnp.float32, 'w_in': _jnp.float32, 'b_gate': _jnp.float32, 'rel_bias': _jnp.float32, 'w_ret_out': _jnp.float32, 'w_att_out': _jnp.float32, 'w_out': _jnp.float32, 'norm_ffn': _jnp.float32, 'w_ffn_gate': _jnp.float32, 'w_ffn_up': _jnp.float32, 'w_ffn_down': _jnp.float32, 'norm_final': _jnp.float32}
MOMENT_SCALE = {'norm_mix': 1.338363e-01, 'w_in': 5.104822e-02, 'b_gate': 1.761824e-02, 'rel_bias': 9.052288e-03, 'w_ret_out': 6.210010e-02, 'w_att_out': 1.518506e-02, 'w_out': 6.319589e-02, 'norm_ffn': 1.274182e-01, 'w_ffn_gate': 5.491259e-02, 'w_ffn_up': 5.341878e-02, 'w_ffn_down': 8.902008e-02, 'norm_final': 3.213040e+01}


def _to_microbatches(a, axis):
    t = _jnp.moveaxis(a, axis, 0)
    t = t.reshape((N_MICROBATCH, t.shape[0] // N_MICROBATCH) + t.shape[1:])
    return _jnp.moveaxis(t, 1, axis + 1)


def setup_inputs(seed: int = 0) -> dict:
    inp = _fwd_setup_inputs(seed)
    key = _jax.random.fold_in(_jax.random.key(seed), 7919)
    shape, _ = _output_shape()
    out = dict(inp)
    out["loss_target"] = _jax.random.normal(_jax.random.fold_in(key, 0), shape, _jnp.float32)
    for i, name in enumerate(TWIN_WEIGHTS):
        w = inp[name].astype(_jnp.float32)
        if MOMENT_SCALE is None:
            s = _jnp.sqrt(_jnp.mean(_jnp.square(w)) + 1e-30)
        else:
            s = MOMENT_SCALE[name]
        km, kv = _jax.random.split(_jax.random.fold_in(key, i + 1))
        out[name] = w
        out["m_" + name] = s * _jax.random.normal(km, w.shape, _jnp.float32)
        out["v_" + name] = (s * s) * _jax.random.uniform(kv, w.shape, _jnp.float32, 0.5, 1.5)
    if N_MICROBATCH > 1:
        for name, axis in PER_EXAMPLE_BATCH_AXIS.items():
            out[name] = _to_microbatches(out[name], axis)
    return {'x': out['x'], 'norm_mix': out['norm_mix'], 'w_in': out['w_in'], 'b_gate': out['b_gate'], 'rel_bias': out['rel_bias'], 'w_ret_out': out['w_ret_out'], 'w_att_out': out['w_att_out'], 'w_out': out['w_out'], 'norm_ffn': out['norm_ffn'], 'w_ffn_gate': out['w_ffn_gate'], 'w_ffn_up': out['w_ffn_up'], 'w_ffn_down': out['w_ffn_down'], 'norm_final': out['norm_final'], 'loss_target': out['loss_target'], 'm_norm_mix': out['m_norm_mix'], 'm_w_in': out['m_w_in'], 'm_b_gate': out['m_b_gate'], 'm_rel_bias': out['m_rel_bias'], 'm_w_ret_out': out['m_w_ret_out'], 'm_w_att_out': out['m_w_att_out'], 'm_w_out': out['m_w_out'], 'm_norm_ffn': out['m_norm_ffn'], 'm_w_ffn_gate': out['m_w_ffn_gate'], 'm_w_ffn_up': out['m_w_ffn_up'], 'm_w_ffn_down': out['m_w_ffn_down'], 'm_norm_final': out['m_norm_final'], 'v_norm_mix': out['v_norm_mix'], 'v_w_in': out['v_w_in'], 'v_b_gate': out['v_b_gate'], 'v_rel_bias': out['v_rel_bias'], 'v_w_ret_out': out['v_w_ret_out'], 'v_w_att_out': out['v_w_att_out'], 'v_w_out': out['v_w_out'], 'v_norm_ffn': out['v_norm_ffn'], 'v_w_ffn_gate': out['v_w_ffn_gate'], 'v_w_ffn_up': out['v_w_ffn_up'], 'v_w_ffn_down': out['v_w_ffn_down'], 'v_norm_final': out['v_norm_final']}


def _loss(weights, diff, rest, loss_target):
    with _jax.named_scope("forward"):
        args = {**rest, TWIN_DIFF_INPUT: diff, **{k: w.astype(_WEIGHT_DTYPES[k]) for k, w in weights.items()}}
        y = _forward(args)
    with _jax.named_scope("loss_head"):
        err = _jnp.square(y.astype(_jnp.float32) - loss_target)
        return 0.5 * _jnp.sum(_jnp.mean(err, axis=-1)) if err.ndim else 0.5 * err


def _adamw(w, g, m, v):
    m = ADAM_B1 * m + (1.0 - ADAM_B1) * g
    v = ADAM_B2 * v + (1.0 - ADAM_B2) * _jnp.square(g)
    m_hat = m / (1.0 - ADAM_B1 ** ADAM_STEP)
    v_hat = v / (1.0 - ADAM_B2 ** ADAM_STEP)
    delta = -ADAM_LR * (m_hat / (_jnp.sqrt(v_hat) + ADAM_EPS) + ADAM_WD * w)
    return delta, m, v


def reference(x, norm_mix, w_in, b_gate, rel_bias, w_ret_out, w_att_out, w_out, norm_ffn, w_ffn_gate, w_ffn_up, w_ffn_down, norm_final, loss_target, m_norm_mix, m_w_in, m_b_gate, m_rel_bias, m_w_ret_out, m_w_att_out, m_w_out, m_norm_ffn, m_w_ffn_gate, m_w_ffn_up, m_w_ffn_down, m_norm_final, v_norm_mix, v_w_in, v_b_gate, v_rel_bias, v_w_ret_out, v_w_att_out, v_w_out, v_norm_ffn, v_w_ffn_gate, v_w_ffn_up, v_w_ffn_down, v_norm_final):
    given = dict(x=x, norm_mix=norm_mix, w_in=w_in, b_gate=b_gate, rel_bias=rel_bias, w_ret_out=w_ret_out, w_att_out=w_att_out, w_out=w_out, norm_ffn=norm_ffn, w_ffn_gate=w_ffn_gate, w_ffn_up=w_ffn_up, w_ffn_down=w_ffn_down, norm_final=norm_final, loss_target=loss_target, m_norm_mix=m_norm_mix, m_w_in=m_w_in, m_b_gate=m_b_gate, m_rel_bias=m_rel_bias, m_w_ret_out=m_w_ret_out, m_w_att_out=m_w_att_out, m_w_out=m_w_out, m_norm_ffn=m_norm_ffn, m_w_ffn_gate=m_w_ffn_gate, m_w_ffn_up=m_w_ffn_up, m_w_ffn_down=m_w_ffn_down, m_norm_final=m_norm_final, v_norm_mix=v_norm_mix, v_w_in=v_w_in, v_b_gate=v_b_gate, v_rel_bias=v_rel_bias, v_w_ret_out=v_w_ret_out, v_w_att_out=v_w_att_out, v_w_out=v_w_out, v_norm_ffn=v_norm_ffn, v_w_ffn_gate=v_w_ffn_gate, v_w_ffn_up=v_w_ffn_up, v_w_ffn_down=v_w_ffn_down, v_norm_final=v_norm_final)
    weights = {n: given[n] for n in TWIN_WEIGHTS}
    shared = {n: given[n] for n in SHARED_INPUTS}
    per_example = {n: given[n] for n in ['x']}
    grad_fn = _jax.value_and_grad(_loss, argnums=(0, 1))

    def one_microbatch(ex, loss_target):
        ex = dict(ex)
        diff = ex.pop(TWIN_DIFF_INPUT)
        return grad_fn(weights, diff, {**shared, **ex}, loss_target)

    if N_MICROBATCH == 1:
        loss, (grad_w, grad_x) = one_microbatch(per_example, given["loss_target"])
    else:
        def body(carry, xs):
            loss_sum, grad_sum = carry
            l_k, (gw_k, gx_k) = one_microbatch(xs[0], xs[1])
            with _jax.named_scope("update"):
                return (loss_sum + l_k, _jax.tree.map(_jnp.add, grad_sum, gw_k)), gx_k

        init = (_jnp.zeros((), _jnp.float32), _jax.tree.map(_jnp.zeros_like, weights))
        (loss, grad_w), grad_x = _jax.lax.scan(body, init, (per_example, given["loss_target"]))
    with _jax.named_scope("update"):
        delta_w, new_m, new_v = {}, {}, {}
        for n in TWIN_WEIGHTS:
            delta_w[n], new_m[n], new_v[n] = _adamw(weights[n], grad_w[n], given["m_" + n], given["v_" + n])
    return (loss, grad_x, *[grad_w[n] for n in TWIN_WEIGHTS], *[delta_w[n] for n in TWIN_WEIGHTS],
            *[new_m[n] for n in TWIN_WEIGHTS], *[new_v[n] for n in TWIN_WEIGHTS])
```

```python
import functools
import math

import numpy as np
import jax
import jax.numpy as jnp
from jax import lax
from jax.experimental import pallas as pl
from jax.experimental.pallas import tpu as pltpu

f32 = jnp.float32
bf16 = jnp.bfloat16

D_MODEL = 1024
CHUNK = 64
RET_HEADS = 4
RET_KEY_DIM = 128
RET_VAL_DIM = 256
ATT_HEADS = 8
ATT_HEAD_DIM = 64
ATT_W = ATT_HEADS * ATT_HEAD_DIM
BAND_CHUNKS = 8
PAD = BAND_CHUNKS * CHUNK
MAX_REL = 256
N_REL = CHUNK + MAX_REL
D_FF = 2816
N_IN = 6656
ROPE_BASE = 10000.0
EPS = 1e-6
NEG_INF = -1e30
C_RQ, C_RK, C_RV, C_RG, C_AQ, C_AK, C_AV, C_GL = 0, 512, 1024, 2048, 3072, 3584, 4096, 4608

ADAM_LR, ADAM_B1, ADAM_B2, ADAM_EPS, ADAM_WD, ADAM_STEP = 0.001, 0.9, 0.999, 1e-08, 0.01, 10

N_CHIPS = 4
N_DEV = 8
PACK_W = 1024
QBLK = 256
KWIN = PAD + QBLK
TOEP = 1024
VMEM_LIMIT = 56 * 1024 * 1024
MESH = pl.DeviceIdType.MESH

BIG = (
    ("w_in", 1), ("w_ret_out", 0), ("w_att_out", 1), ("w_out", 0), ("w_ffn_gate", 1), ("w_ffn_up", 1), ("w_ffn_down", 0))
WEIGHTS = ("norm_mix", "w_in", "b_gate", "rel_bias", "w_ret_out", "w_att_out", "w_out", "norm_ffn", "w_ffn_gate",
           "w_ffn_up", "w_ffn_down", "norm_final")
SMALL = ("norm_mix", "b_gate", "rel_bias", "norm_ffn", "norm_final")


def _dot(a, b):
    return lax.dot_general(a, b, (((1,), (0,)), ((), ())), preferred_element_type=f32)


def _dot_nt(a, b):
    return lax.dot_general(a, b, (((1,), (1,)), ((), ())), preferred_element_type=f32)


def _dot_tn(a, b):
    return lax.dot_general(a, b, (((0,), (0,)), ((), ())), preferred_element_type=f32)


def _sig(x):
    return 1.0 / (1.0 + jnp.exp(-x))


def _tile(n, pref, mult):
    best = None
    for t in range(mult, min(n, pref) + 1, mult):
        if n % t == 0:
            best = t
    return best if best is not None else n


def _params(sem, vmem=VMEM_LIMIT):
    return pltpu.CompilerParams(dimension_semantics=sem, vmem_limit_bytes=vmem)


def _in_proj(x2, gamma, w):
    T, D = x2.shape
    N = w.shape[1]
    tm, tn = _tile(T, 512, 8), _tile(N, 1664, 128)

    def body(x_ref, g_ref, w_ref, xn_ref, p_ref, xs_ref):
        @pl.when(pl.program_id(1) == 0)
        def _():
            x = x_ref[...]
            r = lax.rsqrt(jnp.mean(x * x, axis=-1, keepdims=True) + EPS)
            xn = (x * r * g_ref[...]).astype(bf16)
            xs_ref[...] = xn
            xn_ref[...] = xn

        p_ref[...] = _dot(xs_ref[...], w_ref[...]).astype(bf16)

    return pl.pallas_call(
        body, name="in_proj", grid=(T // tm, N // tn),
        in_specs=[pl.BlockSpec((tm, D), lambda i, j: (i, 0)), pl.BlockSpec((1, D), lambda i, j: (0, 0)),
                  pl.BlockSpec((D, tn), lambda i, j: (0, j))],
        out_specs=[pl.BlockSpec((tm, D), lambda i, j: (i, 0)), pl.BlockSpec((tm, tn), lambda i, j: (i, j))],
        out_shape=[jax.ShapeDtypeStruct((T, D), bf16), jax.ShapeDtypeStruct((T, N), bf16)],
        scratch_shapes=[pltpu.VMEM((tm, D), bf16)],
        compiler_params=_params(("parallel", "arbitrary")),
    )(x2, gamma, w)


def _rope_tables(S):
    d = RET_KEY_DIM
    freqs = ROPE_BASE ** (-jnp.arange(0, d, 2, dtype=f32) / d)
    ang = jnp.arange(S, dtype=f32)[:, None] * freqs[None, :]
    cos, sin = jnp.cos(ang), jnp.sin(ang)
    return jnp.concatenate([cos, cos], axis=1), jnp.concatenate([-sin, sin], axis=1)


def _decay_tables():
    H = RET_HEADS
    log_g = jnp.log(1.0 - 2.0 ** (-5.0 - jnp.arange(H, dtype=f32)))
    p = jnp.arange(CHUNK, dtype=f32)
    intra = jnp.exp(log_g[:, None, None] * jnp.abs(p[:, None] - p[None, :]))
    q_dec = jnp.exp(log_g[:, None] * (p[None, :] + 1.0))
    k_dec = jnp.exp(log_g[:, None] * (CHUNK - 1.0 - p[None, :]))
    c_dec = jnp.exp(log_g * CHUNK)
    q_dec = jnp.broadcast_to(q_dec[:, :, None], (H, CHUNK, RET_KEY_DIM))
    k_dec = jnp.broadcast_to(k_dec[:, :, None], (H, CHUNK, RET_KEY_DIM))
    c_dec = jnp.broadcast_to(c_dec[:, None, None], (H, 1, RET_VAL_DIM))
    return intra, q_dec, k_dec, c_dec


K_SCALE = RET_KEY_DIM ** -0.5


def _ret_tables_specs():
    return [pl.BlockSpec((1, CHUNK, CHUNK), lambda b, h: (h, 0, 0)),
            pl.BlockSpec((1, CHUNK, RET_KEY_DIM), lambda b, h: (h, 0, 0)),
            pl.BlockSpec((1, CHUNK, RET_KEY_DIM), lambda b, h: (h, 0, 0)),
            pl.BlockSpec((1, 1, RET_VAL_DIM), lambda b, h: (h, 0, 0))]


def _ret_fwd(proj, B, S, rope, decay):
    T = B * S
    nc = S // CHUNK
    H, dk, dv = RET_HEADS, RET_KEY_DIM, RET_VAL_DIM

    def body(q_ref, k_ref, v_ref, g_ref, cos_ref, sin_ref, intra_ref, qd_ref, kd_ref, cd_ref,
             qr_ref, kr_ref, o_ref, u_ref, st_ref):
        cos, sn = cos_ref[...], sin_ref[...]
        q = q_ref[...].astype(f32)
        k = k_ref[...].astype(f32)
        qr_ref[...] = (q * cos + pltpu.roll(q, dk // 2, 1) * sn).astype(bf16)
        kr_ref[...] = ((k * cos + pltpu.roll(k, dk // 2, 1) * sn) * K_SCALE).astype(bf16)
        intra, qd, kd, cd = intra_ref[0], qd_ref[0], kd_ref[0], cd_ref[0]

        def step(i, state):
            r = pl.ds(pl.multiple_of(i * CHUNK, CHUNK), CHUNK)
            qi, ki, vi = qr_ref[r, :], kr_ref[r, :], v_ref[r, :]
            sb = state.astype(bf16)
            st_ref[0, 0, i] = sb
            s = (_dot_nt(qi, ki) * intra).astype(bf16)
            o = _dot(s, vi) + _dot((qi.astype(f32) * qd).astype(bf16), sb)
            new_state = state * cd + _dot_tn((ki.astype(f32) * kd).astype(bf16), vi)
            mu = jnp.mean(o, axis=-1, keepdims=True)
            xc = o - mu
            var = jnp.mean(xc * xc, axis=-1, keepdims=True)
            oh = xc * lax.rsqrt(var + EPS)
            g = g_ref[r, :].astype(f32)
            o_ref[r, :] = o.astype(bf16)
            u_ref[r, :] = (g * _sig(g) * oh).astype(bf16)
            return new_state

        lax.fori_loop(0, nc, step, jnp.zeros((dk, dv), f32))

    qb, vb = C_RQ // dk, C_RV // dv
    return pl.pallas_call(
        body, name="ret_fwd", grid=(B, H),
        in_specs=[pl.BlockSpec((S, dk), lambda b, h: (b, C_RQ // dk + h)),
                  pl.BlockSpec((S, dk), lambda b, h: (b, C_RK // dk + h)),
                  pl.BlockSpec((S, dv), lambda b, h: (b, C_RV // dv + h)),
                  pl.BlockSpec((S, dv), lambda b, h: (b, C_RG // dv + h)),
                  pl.BlockSpec((S, dk), lambda b, h: (0, 0)), pl.BlockSpec((S, dk), lambda b, h: (0, 0)),
                  *_ret_tables_specs()],
        out_specs=[pl.BlockSpec((S, dk), lambda b, h: (b, h)), pl.BlockSpec((S, dk), lambda b, h: (b, h)),
                   pl.BlockSpec((S, dv), lambda b, h: (b, h)), pl.BlockSpec((S, dv), lambda b, h: (b, h)),
                   pl.BlockSpec((1, 1, nc, dk, dv), lambda b, h: (b, h, 0, 0, 0))],
        out_shape=[jax.ShapeDtypeStruct((T, H * dk), bf16), jax.ShapeDtypeStruct((T, H * dk), bf16),
                   jax.ShapeDtypeStruct((T, H * dv), bf16), jax.ShapeDtypeStruct((T, H * dv), bf16),
                   jax.ShapeDtypeStruct((B, H, nc, dk, dv), bf16)],
        compiler_params=_params(("parallel", "parallel")),
    )(proj, proj, proj, proj, *rope, *decay)


def _bias_rows(rb):
    last = rb[:, N_REL - 1:]
    return jnp.concatenate([
        jnp.broadcast_to(last, (ATT_HEADS, PAD - MAX_REL + 1)),
        jnp.flip(rb[:, :N_REL - 1], axis=1),
        jnp.broadcast_to(rb[:, :1], (ATT_HEADS, KWIN - PAD - CHUNK)),
        jnp.broadcast_to(last, (ATT_HEADS, TOEP - KWIN)),
    ], axis=1)


def _build_bias(t_ref, bias_ref):
    row = lax.broadcasted_iota(jnp.int32, (QBLK, KWIN), 0) // CHUNK
    col = lax.broadcasted_iota(jnp.int32, (QBLK, KWIN), 1) // CHUNK
    delta = BAND_CHUNKS + row - col
    vis = (delta >= 0) & (delta <= BAND_CHUNKS)
    for h in range(ATT_HEADS):
        t = jnp.broadcast_to(t_ref[h:h + 1, :], (QBLK, TOEP))
        rolled = pltpu.roll(t, 0, 1, stride=1, stride_axis=0)
        bias_ref[h] = jnp.where(vis, rolled[:, :KWIN], NEG_INF)


def _att_probs(qh, kh, bias, pre):
    s = _dot_nt(qh, kh) * (ATT_HEAD_DIM ** -0.5) + bias
    s = jnp.where(pre, NEG_INF, s)
    m = jnp.max(s, axis=-1, keepdims=True)
    p = jnp.exp(s - m)
    return p * (1.0 / jnp.sum(p, axis=-1, keepdims=True))


def _att_fwd(proj, kp, vp, trows, B, S):
    T = B * S
    nq = S // QBLK
    dh = ATT_HEAD_DIM

    def body(q_ref, kp_ref, vp_ref, t_ref, o_ref, bias_ref):
        @pl.when((pl.program_id(0) == 0) & (pl.program_id(1) == 0))
        def _():
            _build_bias(t_ref, bias_ref)

        q0 = pl.multiple_of(pl.program_id(1) * QBLK, QBLK)
        kw = kp_ref[0, pl.ds(q0, KWIN), :]
        vw = vp_ref[0, pl.ds(q0, KWIN), :]
        pre = lax.broadcasted_iota(jnp.int32, (QBLK, KWIN), 1) < PAD - q0
        outs = []
        for h in range(ATT_HEADS):
            hs = slice(h * dh, (h + 1) * dh)
            pr = _att_probs(q_ref[:, hs], kw[:, hs], bias_ref[h], pre)
            outs.append(_dot(pr.astype(bf16), vw[:, hs]))
        o_ref[...] = jnp.concatenate(outs, axis=1).astype(bf16)

    return pl.pallas_call(
        body, name="att_fwd", grid=(B, nq),
        in_specs=[pl.BlockSpec((QBLK, ATT_W), lambda b, i: (b * nq + i, C_AQ // ATT_W)),
                  pl.BlockSpec((1, PAD + S, ATT_W), lambda b, i: (b, 0, 0)),
                  pl.BlockSpec((1, PAD + S, ATT_W), lambda b, i: (b, 0, 0)),
                  pl.BlockSpec((ATT_HEADS, TOEP), lambda b, i: (0, 0))],
        out_specs=pl.BlockSpec((QBLK, ATT_W), lambda b, i: (b * nq + i, 0)),
        out_shape=jax.ShapeDtypeStruct((T, ATT_W), bf16),
        scratch_shapes=[pltpu.VMEM((ATT_HEADS, QBLK, KWIN), f32)],
        compiler_params=_params(("arbitrary", "arbitrary")),
    )(proj, kp, vp, trows)


def _gl_specs(tm):
    w = 512
    return [pl.BlockSpec((tm, w), functools.partial(lambda i, j: (i, C_GL // 512 + j), j=j)) for j in range(4)]


def _gates(gl_refs, bg_ref):
    gl = jnp.concatenate([r[...] for r in gl_refs], axis=1).astype(f32) + bg_ref[...]
    g = _sig(gl)
    return g[:, :D_MODEL], g[:, D_MODEL:]


def _mix_fwd(x2, proj, u, ao, b_gate, w_ro, w_ao, w_out):
    T, D = x2.shape
    tm = _tile(T, 512, 8)

    def body(x_ref, u_ref, ao_ref, g0, g1, g2, g3, bg_ref, wro_ref, wao_ref, wo_ref, h1_ref, yr_ref, ya_ref):
        yr = _dot(u_ref[...], wro_ref[...])
        ya = _dot(ao_ref[...], wao_ref[...])
        gr, ga = _gates((g0, g1, g2, g3), bg_ref)
        mix = gr * yr + ga * ya
        h1_ref[...] = x_ref[...] + _dot(mix.astype(bf16), wo_ref[...])
        yr_ref[...] = yr.astype(bf16)
        ya_ref[...] = ya.astype(bf16)

    full = lambda a: pl.BlockSpec(a.shape, lambda i: (0, 0))
    row = lambda n: pl.BlockSpec((tm, n), lambda i: (i, 0))
    return pl.pallas_call(
        body, name="mix_fwd", grid=(T // tm,),
        in_specs=[row(D), row(D), row(ATT_W), *_gl_specs(tm), full(b_gate), full(w_ro), full(w_ao), full(w_out)],
        out_specs=[row(D), row(D), row(D)],
        out_shape=[jax.ShapeDtypeStruct((T, D), f32), jax.ShapeDtypeStruct((T, D), bf16),
                   jax.ShapeDtypeStruct((T, D), bf16)],
        compiler_params=_params(("parallel",)),
    )(x2, u, ao, proj, proj, proj, proj, b_gate, w_ro, w_ao, w_out)


def _ffn_fwd(h1, g_ffn, wg, wu, wd, g_fin, target):
    T, D = h1.shape
    F = wg.shape[1]
    tm, tf = _tile(T, 512, 8), _tile(F, 256, 128)
    nf = F // tf

    def body(h1_ref, g_ref, wg_ref, wu_ref, wd_ref, gf_ref, tg_ref, hn_ref, a_ref, b_ref, f_ref, dh2_ref, part_ref,
             hs_ref, acc_ref):
        j = pl.program_id(1)

        @pl.when(j == 0)
        def _():
            h = h1_ref[...]
            r = lax.rsqrt(jnp.mean(h * h, axis=-1, keepdims=True) + EPS)
            hn = (h * r * g_ref[...]).astype(bf16)
            hs_ref[...] = hn
            hn_ref[...] = hn
            acc_ref[...] = jnp.zeros_like(acc_ref)

        hn = hs_ref[...]
        a = _dot(hn, wg_ref[...])
        b = _dot(hn, wu_ref[...])
        f = ((a * _sig(a)) * b).astype(bf16)
        a_ref[...] = a.astype(bf16)
        b_ref[...] = b.astype(bf16)
        f_ref[...] = f
        acc_ref[...] += _dot(f, wd_ref[...])

        @pl.when(j == nf - 1)
        def _():
            h2 = h1_ref[...] + acc_ref[...]
            r = lax.rsqrt(jnp.mean(h2 * h2, axis=-1, keepdims=True) + EPS)
            n = h2 * r
            gf = gf_ref[...]
            e = n * gf - tg_ref[...]
            dy = e * (1.0 / D)
            dn = dy * gf
            dh2_ref[...] = r * (dn - n * jnp.mean(dn * n, axis=-1, keepdims=True))
            part_ref[...] = jnp.zeros_like(part_ref)
            part_ref[0:1, :] = jnp.sum(dy * n, axis=0, keepdims=True)
            part_ref[1:2, :] = (0.5 / D) * jnp.sum(e * e, axis=0, keepdims=True)

    row = lambda n: pl.BlockSpec((tm, n), lambda i, j: (i, 0))
    vec = pl.BlockSpec((1, D), lambda i, j: (0, 0))
    col = pl.BlockSpec((tm, tf), lambda i, j: (i, j))
    return pl.pallas_call(
        body, name="ffn_fwd", grid=(T // tm, nf),
        in_specs=[row(D), vec, pl.BlockSpec((D, tf), lambda i, j: (0, j)), pl.BlockSpec((D, tf), lambda i, j: (0, j)),
                  pl.BlockSpec((tf, D), lambda i, j: (j, 0)), vec, row(D)],
        out_specs=[row(D), col, col, col, row(D), pl.BlockSpec((8, D), lambda i, j: (i, 0))],
        out_shape=[jax.ShapeDtypeStruct((T, D), bf16), jax.ShapeDtypeStruct((T, F), bf16),
                   jax.ShapeDtypeStruct((T, F), bf16), jax.ShapeDtypeStruct((T, F), bf16),
                   jax.ShapeDtypeStruct((T, D), f32), jax.ShapeDtypeStruct((T // tm * 8, D), f32)],
        scratch_shapes=[pltpu.VMEM((tm, D), bf16), pltpu.VMEM((tm, D), f32)],
        compiler_params=_params(("parallel", "arbitrary")),
    )(h1, g_ffn, wg, wu, wd, g_fin, target)


def _ffn_bwd(dh2, h1, g_ffn, a, b, wg, wu, wd):
    T, D = h1.shape
    F = wg.shape[1]
    tm, tf = _tile(T, 512, 8), _tile(F, 256, 128)
    nf = F // tf

    def body(dh2_ref, h1_ref, g_ref, a_ref, b_ref, wg_ref, wu_ref, wd_ref, da_ref, db_ref, dh1_ref, part_ref,
             ds_ref, acc_ref):
        j = pl.program_id(1)

        @pl.when(j == 0)
        def _():
            ds_ref[...] = dh2_ref[...].astype(bf16)
            acc_ref[...] = jnp.zeros_like(acc_ref)

        df = _dot_nt(ds_ref[...], wd_ref[...])
        av = a_ref[...].astype(f32)
        sg = _sig(av)
        db = (df * (av * sg)).astype(bf16)
        da = (df * b_ref[...].astype(f32) * (sg * (1.0 + av * (1.0 - sg)))).astype(bf16)
        da_ref[...] = da
        db_ref[...] = db
        acc_ref[...] += _dot_nt(da, wg_ref[...]) + _dot_nt(db, wu_ref[...])

        @pl.when(j == nf - 1)
        def _():
            h = h1_ref[...]
            r = lax.rsqrt(jnp.mean(h * h, axis=-1, keepdims=True) + EPS)
            n = h * r
            dhn = acc_ref[...]
            dn = dhn * g_ref[...]
            dh1_ref[...] = dh2_ref[...] + r * (dn - n * jnp.mean(dn * n, axis=-1, keepdims=True))
            part_ref[...] = jnp.zeros_like(part_ref)
            part_ref[0:1, :] = jnp.sum(dhn * n, axis=0, keepdims=True)

    row = lambda n: pl.BlockSpec((tm, n), lambda i, j: (i, 0))
    col = pl.BlockSpec((tm, tf), lambda i, j: (i, j))
    return pl.pallas_call(
        body, name="ffn_bwd", grid=(T // tm, nf),
        in_specs=[row(D), row(D), pl.BlockSpec((1, D), lambda i, j: (0, 0)), col, col,
                  pl.BlockSpec((D, tf), lambda i, j: (0, j)), pl.BlockSpec((D, tf), lambda i, j: (0, j)),
                  pl.BlockSpec((tf, D), lambda i, j: (j, 0))],
        out_specs=[col, col, row(D), pl.BlockSpec((8, D), lambda i, j: (i, 0))],
        out_shape=[jax.ShapeDtypeStruct((T, F), bf16), jax.ShapeDtypeStruct((T, F), bf16),
                   jax.ShapeDtypeStruct((T, D), f32), jax.ShapeDtypeStruct((T // tm * 8, D), f32)],
        scratch_shapes=[pltpu.VMEM((tm, D), bf16), pltpu.VMEM((tm, D), f32)],
        compiler_params=_params(("parallel", "arbitrary")),
    )(dh2, h1, g_ffn, a, b, wg, wu, wd)


def _mix_bwd(dh1, proj, yr, ya, b_gate, w_ro, w_ao, w_out):
    T, D = dh1.shape
    tm = _tile(T, 512, 8)

    def body(dh1_ref, g0, g1, g2, g3, bg_ref, yr_ref, ya_ref, wro_ref, wao_ref, wo_ref,
             du_ref, dao_ref, dgl_ref, mix_ref, dyr_ref, dya_ref, part_ref):
        dmix = _dot_nt(dh1_ref[...].astype(bf16), wo_ref[...])
        gr, ga = _gates((g0, g1, g2, g3), bg_ref)
        yr = yr_ref[...].astype(f32)
        ya = ya_ref[...].astype(f32)
        dyr = (dmix * gr).astype(bf16)
        dya = (dmix * ga).astype(bf16)
        dgl = jnp.concatenate([dmix * yr * gr * (1.0 - gr), dmix * ya * ga * (1.0 - ga)], axis=1)
        du_ref[...] = _dot_nt(dyr, wro_ref[...]).astype(bf16)
        dao_ref[...] = _dot_nt(dya, wao_ref[...]).astype(bf16)
        dgl_ref[...] = dgl.astype(bf16)
        mix_ref[...] = (gr * yr + ga * ya).astype(bf16)
        dyr_ref[...] = dyr
        dya_ref[...] = dya
        part_ref[...] = jnp.zeros_like(part_ref)
        part_ref[0:1, :] = jnp.sum(dgl, axis=0, keepdims=True)

    full = lambda a: pl.BlockSpec(a.shape, lambda i: (0, 0))
    row = lambda n: pl.BlockSpec((tm, n), lambda i: (i, 0))
    return pl.pallas_call(
        body, name="mix_bwd", grid=(T // tm,),
        in_specs=[row(D), *_gl_specs(tm), full(b_gate), row(D), row(D), full(w_ro), full(w_ao), full(w_out)],
        out_specs=[row(D), row(ATT_W), row(2 * D), row(D), row(D), row(D), pl.BlockSpec((8, 2 * D), lambda i: (i, 0))],
        out_shape=[jax.ShapeDtypeStruct((T, D), bf16), jax.ShapeDtypeStruct((T, ATT_W), bf16),
                   jax.ShapeDtypeStruct((T, 2 * D), bf16), jax.ShapeDtypeStruct((T, D), bf16),
                   jax.ShapeDtypeStruct((T, D), bf16), jax.ShapeDtypeStruct((T, D), bf16),
                   jax.ShapeDtypeStruct((T // tm * 8, 2 * D), f32)],
        compiler_params=_params(("parallel",)),
    )(dh1, proj, proj, proj, proj, b_gate, yr, ya, w_ro, w_ao, w_out)


def _ret_bwd(proj, qr, kr, o, states, du, B, S, rope, decay):
    T = B * S
    nc = S // CHUNK
    H, dk, dv = RET_HEADS, RET_KEY_DIM, RET_VAL_DIM

    def body(qr_ref, kr_ref, v_ref, g_ref, o_ref, st_ref, du_ref, cos_ref, sin_ref, intra_ref, qd_ref, kd_ref, cd_ref,
             dq_ref, dk_ref, dv_ref, dg_ref, dqs_ref, dks_ref):
        intra, qd, kd, cd = intra_ref[0], qd_ref[0], kd_ref[0], cd_ref[0]

        def step(t, dstate):
            i = nc - 1 - t
            r = pl.ds(pl.multiple_of(i * CHUNK, CHUNK), CHUNK)
            qi, ki, vi = qr_ref[r, :], kr_ref[r, :], v_ref[r, :]
            si = st_ref[0, 0, i]
            o = o_ref[r, :].astype(f32)
            mu = jnp.mean(o, axis=-1, keepdims=True)
            xc = o - mu
            rstd = lax.rsqrt(jnp.mean(xc * xc, axis=-1, keepdims=True) + EPS)
            oh = xc * rstd
            g = g_ref[r, :].astype(f32)
            sg = _sig(g)
            dui = du_ref[r, :].astype(f32)
            dg_ref[r, :] = (dui * oh * (sg * (1.0 + g * (1.0 - sg)))).astype(bf16)
            doh = dui * (g * sg)
            do = rstd * (doh - jnp.mean(doh, axis=-1, keepdims=True)
                         - oh * jnp.mean(doh * oh, axis=-1, keepdims=True))
            dob = do.astype(bf16)
            p = (_dot_nt(qi, ki) * intra).astype(bf16)
            dsb = dstate.astype(bf16)
            kt = (ki.astype(f32) * kd).astype(bf16)
            qt = (qi.astype(f32) * qd).astype(bf16)
            dv_ref[r, :] = (_dot_tn(p, dob) + _dot(kt, dsb)).astype(bf16)
            da = (_dot_nt(dob, vi) * intra).astype(bf16)
            dqs_ref[r, :] = _dot(da, ki) + _dot_nt(dob, si) * qd
            dks_ref[r, :] = (_dot_tn(da, qi) + _dot_nt(vi, dsb) * kd) * K_SCALE
            return dstate * cd + _dot_tn(qt, dob)

        lax.fori_loop(0, nc, step, jnp.zeros((dk, dv), f32))
        cos, snb = cos_ref[...], -sin_ref[...]
        dq = dqs_ref[...]
        dkk = dks_ref[...]
        dq_ref[...] = (dq * cos + pltpu.roll(dq, dk // 2, 1) * snb).astype(bf16)
        dk_ref[...] = (dkk * cos + pltpu.roll(dkk, dk // 2, 1) * snb).astype(bf16)

    hk = lambda b, h: (b, h)
    return pl.pallas_call(
        body, name="ret_bwd", grid=(B, H),
        in_specs=[pl.BlockSpec((S, dk), hk), pl.BlockSpec((S, dk), hk),
                  pl.BlockSpec((S, dv), lambda b, h: (b, C_RV // dv + h)),
                  pl.BlockSpec((S, dv), lambda b, h: (b, C_RG // dv + h)),
                  pl.BlockSpec((S, dv), hk),
                  pl.BlockSpec((1, 1, nc, dk, dv), lambda b, h: (b, h, 0, 0, 0)),
                  pl.BlockSpec((S, dv), hk),
                  pl.BlockSpec((S, dk), lambda b, h: (0, 0)), pl.BlockSpec((S, dk), lambda b, h: (0, 0)),
                  *_ret_tables_specs()],
        out_specs=[pl.BlockSpec((S, dk), hk), pl.BlockSpec((S, dk), hk), pl.BlockSpec((S, dv), hk),
                   pl.BlockSpec((S, dv), hk)],
        out_shape=[jax.ShapeDtypeStruct((T, H * dk), bf16), jax.ShapeDtypeStruct((T, H * dk), bf16),
                   jax.ShapeDtypeStruct((T, H * dv), bf16), jax.ShapeDtypeStruct((T, H * dv), bf16)],
        scratch_shapes=[pltpu.VMEM((S, dk), f32), pltpu.VMEM((S, dk), f32)],
        compiler_params=_params(("parallel", "parallel")),
    )(qr, kr, proj, proj, o, states, du, *rope, *decay)


def _att_bwd(proj, kp, vp, dao, trows, B, S):
    T = B * S
    nq = S // QBLK
    dh = ATT_HEAD_DIM
    scale = ATT_HEAD_DIM ** -0.5

    def body(q_ref, kp_ref, vp_ref, do_ref, t_ref, dq_ref, dk_ref, dv_ref, vec_ref, bias_ref, dbias_ref, dka_ref, dva_ref):
        b, i = pl.program_id(0), pl.program_id(1)

        @pl.when((b == 0) & (i == 0))
        def _():
            _build_bias(t_ref, bias_ref)
            dbias_ref[...] = jnp.zeros_like(dbias_ref)

        @pl.when(i == 0)
        def _():
            dka_ref[...] = jnp.zeros_like(dka_ref)
            dva_ref[...] = jnp.zeros_like(dva_ref)

        q0 = pl.multiple_of(i * QBLK, QBLK)
        win = pl.ds(q0, KWIN)
        kw = kp_ref[0, win, :]
        vw = vp_ref[0, win, :]
        pre = lax.broadcasted_iota(jnp.int32, (QBLK, KWIN), 1) < PAD - q0
        dqs, dks, dvs = [], [], []
        for h in range(ATT_HEADS):
            hs = slice(h * dh, (h + 1) * dh)
            qh, kh, vh, doh = q_ref[:, hs], kw[:, hs], vw[:, hs], do_ref[:, hs]
            pr = _att_probs(qh, kh, bias_ref[h], pre)
            dp = _dot_nt(doh, vh)
            ds = pr * (dp - jnp.sum(pr * dp, axis=-1, keepdims=True))
            dbias_ref[h] += ds
            dsb = (ds * scale).astype(bf16)
            dqs.append(_dot(dsb, kh))
            dks.append(_dot_tn(dsb, qh))
            dvs.append(_dot_tn(pr.astype(bf16), doh))
        dq_ref[...] = jnp.concatenate(dqs, axis=1).astype(bf16)
        dka_ref[win, :] += jnp.concatenate(dks, axis=1)
        dva_ref[win, :] += jnp.concatenate(dvs, axis=1)

        @pl.when(i == nq - 1)
        def _():
            dk_ref[0] = dka_ref[...].astype(bf16)
            dv_ref[0] = dva_ref[...].astype(bf16)

        @pl.when((b == B - 1) & (i == nq - 1))
        def _():
            rr = lax.broadcasted_iota(jnp.int32, (QBLK, QBLK), 0)
            cc = lax.broadcasted_iota(jnp.int32, (QBLK, QBLK), 1)
            flip = jnp.where(rr + cc == QBLK - 1, 1.0, 0.0).astype(bf16)
            for h in range(ATT_HEADS):
                d = dbias_ref[h]
                hi = d.astype(bf16)
                lo = (d - hi.astype(f32)).astype(bf16)
                rev = _dot(flip, hi) + _dot(flip, lo)
                wide = jnp.concatenate([rev, jnp.zeros((QBLK, TOEP - KWIN), f32)], axis=1)
                rolled = pltpu.roll(wide, 0, 1, stride=1, stride_axis=0)
                vec_ref[h:h + 1, :] = jnp.sum(rolled, axis=0, keepdims=True)

    qspec = lambda c: pl.BlockSpec((QBLK, ATT_W), lambda b, i: (b * nq + i, c))
    kspec = pl.BlockSpec((1, PAD + S, ATT_W), lambda b, i: (b, 0, 0))
    return pl.pallas_call(
        body, name="att_bwd", grid=(B, nq),
        in_specs=[qspec(C_AQ // ATT_W), kspec, kspec, qspec(0), pl.BlockSpec((ATT_HEADS, TOEP), lambda b, i: (0, 0))],
        out_specs=[qspec(0), kspec, kspec, pl.BlockSpec((ATT_HEADS, TOEP), lambda b, i: (0, 0))],
        out_shape=[jax.ShapeDtypeStruct((T, ATT_W), bf16), jax.ShapeDtypeStruct((B, PAD + S, ATT_W), bf16),
                   jax.ShapeDtypeStruct((B, PAD + S, ATT_W), bf16), jax.ShapeDtypeStruct((ATT_HEADS, TOEP), f32)],
        scratch_shapes=[pltpu.VMEM((ATT_HEADS, QBLK, KWIN), f32), pltpu.VMEM((ATT_HEADS, QBLK, KWIN), f32),
                        pltpu.VMEM((PAD + S, ATT_W), f32), pltpu.VMEM((PAD + S, ATT_W), f32)],
        compiler_params=_params(("arbitrary", "arbitrary")),
    )(proj, kp, vp, dao, trows)


def _in_proj_bwd(dproj, w_in, x2, gamma, dh1):
    T, D = x2.shape
    N = w_in.shape[1]
    tm, tk = _tile(T, 512, 8), _tile(N, 1664, 128)
    nk = N // tk

    def body(dp_ref, w_ref, x_ref, g_ref, dh1_ref, dx_ref, part_ref, acc_ref):
        j = pl.program_id(1)

        @pl.when(j == 0)
        def _():
            acc_ref[...] = jnp.zeros_like(acc_ref)

        acc_ref[...] += _dot_nt(dp_ref[...], w_ref[...])

        @pl.when(j == nk - 1)
        def _():
            x = x_ref[...]
            r = lax.rsqrt(jnp.mean(x * x, axis=-1, keepdims=True) + EPS)
            n = x * r
            dxn = acc_ref[...]
            dn = dxn * g_ref[...]
            dx_ref[...] = dh1_ref[...] + r * (dn - n * jnp.mean(dn * n, axis=-1, keepdims=True))
            part_ref[...] = jnp.zeros_like(part_ref)
            part_ref[0:1, :] = jnp.sum(dxn * n, axis=0, keepdims=True)

    row = lambda n: pl.BlockSpec((tm, n), lambda i, j: (i, 0))
    return pl.pallas_call(
        body, name="in_proj_bwd", grid=(T // tm, nk),
        in_specs=[pl.BlockSpec((tm, tk), lambda i, j: (i, j)), pl.BlockSpec((D, tk), lambda i, j: (0, j)), row(D),
                  pl.BlockSpec((1, D), lambda i, j: (0, 0)), row(D)],
        out_specs=[row(D), pl.BlockSpec((8, D), lambda i, j: (i, 0))],
        out_shape=[jax.ShapeDtypeStruct((T, D), f32), jax.ShapeDtypeStruct((T // tm * 8, D), f32)],
        scratch_shapes=[pltpu.VMEM((tm, D), f32)],
        compiler_params=_params(("parallel", "arbitrary")),
    )(dproj, w_in, x2, gamma, dh1)


def _wgrad(a, b, name):
    T, K = a.shape
    N = b.shape[1]
    tt, tk, tn = _tile(T, 512, 16), _tile(K, 1408, 128), _tile(N, 1664, 128)
    nt = T // tt

    def body(a_ref, b_ref, o_ref):
        @pl.when(pl.program_id(2) == 0)
        def _():
            o_ref[...] = jnp.zeros_like(o_ref)

        o_ref[...] += _dot_tn(a_ref[...].astype(bf16), b_ref[...].astype(bf16))

    return pl.pallas_call(
        body, name=name, grid=(K // tk, N // tn, nt),
        in_specs=[pl.BlockSpec((tt, tk), lambda k, n, t: (t, k)), pl.BlockSpec((tt, tn), lambda k, n, t: (t, n))],
        out_specs=pl.BlockSpec((tk, tn), lambda k, n, t: (k, n)),
        out_shape=jax.ShapeDtypeStruct((K, N), f32),
        compiler_params=_params(("parallel", "parallel", "arbitrary")),
    )(a, b)


def _adamw(w, g, m, v, name):
    R, C = w.shape
    tr = _tile(R, max(8, (1 << 18) // C // 8 * 8), 8)

    def body(w_ref, g_ref, m_ref, v_ref, d_ref, mo_ref, vo_ref):
        g_ = g_ref[...]
        m_ = ADAM_B1 * m_ref[...] + (1.0 - ADAM_B1) * g_
        v_ = ADAM_B2 * v_ref[...] + (1.0 - ADAM_B2) * (g_ * g_)
        m_hat = m_ / (1.0 - ADAM_B1 ** ADAM_STEP)
        v_hat = v_ / (1.0 - ADAM_B2 ** ADAM_STEP)
        d_ref[...] = -ADAM_LR * (m_hat / (jnp.sqrt(v_hat) + ADAM_EPS) + ADAM_WD * w_ref[...])
        mo_ref[...] = m_
        vo_ref[...] = v_

    spec = pl.BlockSpec((tr, C), lambda i: (i, 0))
    return pl.pallas_call(
        body, name=name, grid=(R // tr,), in_specs=[spec] * 4, out_specs=[spec] * 3,
        out_shape=[jax.ShapeDtypeStruct((R, C), f32)] * 3,
        compiler_params=_params(("parallel",)),
    )(w, g, m, v)


def _place():
    return lax.axis_index("x"), lax.axis_index("y"), lax.axis_index("c")


def _other_chips(x, y):
    chips = [(1 - x, y), (x, 1 - y), (1 - x, 1 - y)]
    return chips, [2 * cx + cy for cx, cy in chips]


def _exchange_small(blk, name, reduce):
    R, C = blk.shape

    def body(x_ref, out_ref, *rest):
        if reduce:
            all_ref, send_sems, recv_sems = rest
        else:
            all_ref = out_ref
            send_sems, recv_sems = rest
        x, y, c = _place()
        me = 4 * x + 2 * y + c
        all_ref[me] = x_ref[...]
        copies = []
        for k in range(1, N_DEV):
            peer = tuple(1 - p if (k >> s) & 1 else p for p, s in ((x, 2), (y, 1), (c, 0)))
            cp = pltpu.make_async_remote_copy(src_ref=x_ref, dst_ref=all_ref.at[me], send_sem=send_sems.at[k - 1],
                                              recv_sem=recv_sems.at[k - 1], device_id=peer, device_id_type=MESH)
            cp.start()
            copies.append(cp)
        for cp in copies:
            cp.wait()
        if reduce:
            tot = all_ref[0]
            for d in range(1, N_DEV):
                tot = tot + all_ref[d]
            out_ref[...] = tot

    vm = pl.BlockSpec(memory_space=pltpu.VMEM)
    scratch = [pltpu.SemaphoreType.DMA((N_DEV - 1,)), pltpu.SemaphoreType.DMA((N_DEV - 1,))]
    if reduce:
        scratch = [pltpu.VMEM((N_DEV, R, C), f32)] + scratch
    return pl.pallas_call(
        body, name=name, in_specs=[vm], out_specs=vm,
        out_shape=jax.ShapeDtypeStruct((R, C) if reduce else (N_DEV, R, C), f32),
        scratch_shapes=scratch,
    )(blk)


def _gather_weights(wpack):
    NR, W = wpack.shape
    half = NR // 2

    def body(w_ref, out_ref, send_sems, recv_sems, local_sem):
        x, y, c = _place()
        k_me = 2 * x + y
        mine, theirs = pl.ds(c * half, half), pl.ds((1 - c) * half, half)
        chips, ks = _other_chips(x, y)
        own = pltpu.make_async_copy(w_ref, out_ref.at[k_me], local_sem)
        own.start()

        def copy(n, src, dst, to):
            return pltpu.make_async_remote_copy(src_ref=src, dst_ref=dst, send_sem=send_sems.at[n],
                                                recv_sem=recv_sems.at[n], device_id=to, device_id_type=MESH)

        first = [copy(j, w_ref.at[mine], out_ref.at[k_me, mine], (*chips[j], c)) for j in range(3)]
        for cp in first:
            cp.start()
        passed = [copy(3 + j, out_ref.at[ks[j], mine], out_ref.at[ks[j], mine], (x, y, 1 - c)) for j in range(3)]
        for j in range(3):
            copy(j, w_ref.at[mine], out_ref.at[ks[j], mine], (x, y, c)).wait_recv()
            passed[j].start()
        for j in range(3):
            copy(3 + j, w_ref.at[mine], out_ref.at[ks[j], theirs], (x, y, c)).wait_recv()
        for cp in first + passed:
            cp.wait_send()
        own.wait()

    anyspace = pl.BlockSpec(memory_space=pl.ANY)
    return pl.pallas_call(
        body, name="gather_weights", in_specs=[anyspace], out_specs=anyspace,
        out_shape=jax.ShapeDtypeStruct((N_CHIPS, NR, W), wpack.dtype),
        scratch_shapes=[pltpu.SemaphoreType.DMA((6,)), pltpu.SemaphoreType.DMA((6,)), pltpu.SemaphoreType.DMA],
    )(wpack)


def _rs_sibling(gpack):
    _, NR, W = gpack.shape
    half = NR // 2

    def body(g_ref, out_ref, send_sem, recv_sem):
        x, y, c = _place()
        cp = pltpu.make_async_remote_copy(src_ref=g_ref.at[:, pl.ds((1 - c) * half, half)], dst_ref=out_ref,
                                          send_sem=send_sem, recv_sem=recv_sem, device_id=(x, y, 1 - c),
                                          device_id_type=MESH)
        cp.start()
        cp.wait()

    anyspace = pl.BlockSpec(memory_space=pl.ANY)
    return pl.pallas_call(
        body, name="rs_sibling", in_specs=[anyspace], out_specs=anyspace,
        out_shape=jax.ShapeDtypeStruct((N_CHIPS, half, W), gpack.dtype),
        scratch_shapes=[pltpu.SemaphoreType.DMA, pltpu.SemaphoreType.DMA],
    )(gpack)


def _rs_add_sibling(cidx, gpack, got):
    _, NR, W = gpack.shape
    half = NR // 2
    tr = _tile(half, 768, 16)
    nr = half // tr

    def body(c_ref, a_ref, b_ref, o_ref):
        o_ref[...] = (a_ref[...].astype(f32) + b_ref[...].astype(f32)).astype(o_ref.dtype)

    return pl.pallas_call(
        body, name="rs_add_sibling",
        grid_spec=pltpu.PrefetchScalarGridSpec(
            num_scalar_prefetch=1, grid=(N_CHIPS, nr),
            in_specs=[pl.BlockSpec((1, tr, W), lambda k, r, c_ref: (k, c_ref[0] * nr + r, 0)),
                      pl.BlockSpec((1, tr, W), lambda k, r, c_ref: (k, r, 0))],
            out_specs=pl.BlockSpec((1, tr, W), lambda k, r, c_ref: (k, r, 0))),
        out_shape=jax.ShapeDtypeStruct((N_CHIPS, half, W), bf16),
        compiler_params=_params(("parallel", "parallel")),
    )(cidx, gpack, got)


def _rs_chips(part):
    _, half, W = part.shape

    def body(p_ref, out_ref, send_sems, recv_sems, local_sem):
        x, y, c = _place()
        k_me = 2 * x + y
        chips, ks = _other_chips(x, y)
        own = pltpu.make_async_copy(p_ref.at[k_me], out_ref.at[k_me], local_sem)
        own.start()
        copies = [pltpu.make_async_remote_copy(src_ref=p_ref.at[ks[j]], dst_ref=out_ref.at[k_me],
                                               send_sem=send_sems.at[j], recv_sem=recv_sems.at[j],
                                               device_id=(*chips[j], c), device_id_type=MESH) for j in range(3)]
        for cp in copies:
            cp.start()
        for j in range(3):
            pltpu.make_async_remote_copy(src_ref=p_ref.at[ks[j]], dst_ref=out_ref.at[ks[j]], send_sem=send_sems.at[j],
                                         recv_sem=recv_sems.at[j], device_id=(x, y, c), device_id_type=MESH).wait_recv()
        for cp in copies:
            cp.wait_send()
        own.wait()

    anyspace = pl.BlockSpec(memory_space=pl.ANY)
    return pl.pallas_call(
        body, name="rs_chips", in_specs=[anyspace], out_specs=anyspace,
        out_shape=jax.ShapeDtypeStruct(part.shape, part.dtype),
        scratch_shapes=[pltpu.SemaphoreType.DMA((3,)), pltpu.SemaphoreType.DMA((3,)), pltpu.SemaphoreType.DMA],
    )(part)


def _rs_add_chips(parts):
    _, half, W = parts.shape
    tr = _tile(half, 768, 16)

    def body(p_ref, o_ref):
        tot = p_ref[0].astype(f32)
        for k in range(1, N_CHIPS):
            tot = tot + p_ref[k].astype(f32)
        o_ref[...] = tot

    return pl.pallas_call(
        body, name="rs_add_chips", grid=(half // tr,),
        in_specs=[pl.BlockSpec((N_CHIPS, tr, W), lambda r: (0, r, 0))],
        out_specs=pl.BlockSpec((tr, W), lambda r: (r, 0)),
        out_shape=jax.ShapeDtypeStruct((half, W), f32),
        compiler_params=_params(("parallel",)),
    )(parts)


def _rs_share(mine):
    half, W = mine.shape

    def body(m_ref, out_ref, send_sem, recv_sem, local_sem):
        x, y, c = _place()
        own = pltpu.make_async_copy(m_ref, out_ref.at[c], local_sem)
        own.start()
        cp = pltpu.make_async_remote_copy(src_ref=m_ref, dst_ref=out_ref.at[c], send_sem=send_sem, recv_sem=recv_sem,
                                          device_id=(x, y, 1 - c), device_id_type=MESH)
        cp.start()
        cp.wait_send()
        pltpu.make_async_remote_copy(src_ref=m_ref, dst_ref=out_ref.at[1 - c], send_sem=send_sem, recv_sem=recv_sem,
                                     device_id=(x, y, c), device_id_type=MESH).wait_recv()
        own.wait()

    anyspace = pl.BlockSpec(memory_space=pl.ANY)
    return pl.pallas_call(
        body, name="rs_share", in_specs=[anyspace], out_specs=anyspace,
        out_shape=jax.ShapeDtypeStruct((2, half, W), mine.dtype),
        scratch_shapes=[pltpu.SemaphoreType.DMA, pltpu.SemaphoreType.DMA, pltpu.SemaphoreType.DMA],
    )(mine)


def _local_step(x, target, norm_mix, b_gate, rb_full, norm_ffn, norm_final, wb):
    B, S, D = x.shape
    T = B * S
    x2 = x.reshape(T, D)
    tg2 = target.reshape(T, D)
    rope, decay = _rope_tables(S), _decay_tables()
    trows = _bias_rows(rb_full)
    g_fin = norm_final.reshape(1, D)

    xn, proj = _in_proj(x2, norm_mix, wb["w_in"])
    qr, kr, o, u, states = _ret_fwd(proj, B, S, rope, decay)
    pad3 = lambda a: jnp.pad(a.reshape(B, S, ATT_W), ((0, 0), (PAD, 0), (0, 0)))
    kp, vp = pad3(proj[:, C_AK:C_AK + ATT_W]), pad3(proj[:, C_AV:C_AV + ATT_W])
    ao = _att_fwd(proj, kp, vp, trows, B, S)
    h1, yr, ya = _mix_fwd(x2, proj, u, ao, b_gate, wb["w_ret_out"], wb["w_att_out"], wb["w_out"])
    hn, a, b, f, dh2, part_fin = _ffn_fwd(h1, norm_ffn, wb["w_ffn_gate"], wb["w_ffn_up"], wb["w_ffn_down"], g_fin, tg2)

    da, db, dh1, part_ffn = _ffn_bwd(dh2, h1, norm_ffn, a, b, wb["w_ffn_gate"], wb["w_ffn_up"], wb["w_ffn_down"])
    du, dao, dgl, mix, dyr, dya, part_bg = _mix_bwd(dh1, proj, yr, ya, b_gate, wb["w_ret_out"], wb["w_att_out"],
                                                    wb["w_out"])
    drq, drk, drv, drg = _ret_bwd(proj, qr, kr, o, states, du, B, S, rope, decay)
    daq, dkp, dvp, dvec = _att_bwd(proj, kp, vp, dao, trows, B, S)
    unpad = lambda a3: a3[:, PAD:, :].reshape(T, ATT_W)
    dproj = jnp.concatenate([drq, drk, drv, drg, daq, unpad(dkp), unpad(dvp), dgl], axis=1)
    gx, part_mix = _in_proj_bwd(dproj, wb["w_in"], x2, norm_mix, dh1)

    gbig = {
        "w_in": _wgrad(xn, dproj, "wgrad_in"),
        "w_ret_out": _wgrad(u, dyr, "wgrad_ret_out"),
        "w_att_out": _wgrad(ao, dya, "wgrad_att_out"),
        "w_out": _wgrad(mix, dh1, "wgrad_out"),
        "w_ffn_gate": _wgrad(hn, da, "wgrad_ffn_gate"),
        "w_ffn_up": _wgrad(hn, db, "wgrad_ffn_up"),
        "w_ffn_down": _wgrad(f, dh2, "wgrad_ffn_down"),
    }
    rows = lambda p, r: p.reshape(-1, 8, p.shape[-1])[:, r, :].sum(axis=0)
    lo = KWIN - 1 - (MAX_REL - 1)
    drb = jnp.concatenate([jnp.flip(dvec[:, lo:lo + N_REL - 1], axis=1), dvec[:, :lo].sum(axis=1, keepdims=True)], axis=1)
    gsmall = {
        "norm_mix": rows(part_mix, 0), "b_gate": rows(part_bg, 0), "rel_bias": drb, "norm_ffn": rows(part_ffn, 0),
        "norm_final": rows(part_fin, 0),
    }
    return rows(part_fin, 1), gx.reshape(B, S, D), gbig, gsmall


def _split4(a, axis):
    return jnp.split(a, N_CHIPS, axis=axis)


def _pack_rows(arrs, dtype):
    return jnp.concatenate([a.reshape(-1, PACK_W).astype(dtype) for a in arrs], axis=0)


def _unpack_rows(pack, shapes):
    out, off = [], 0
    for shp in shapes:
        n = shp[0] * shp[1] // PACK_W
        out.append(pack[off:off + n].reshape(shp))
        off += n
    return out


SMALL_ROWS = 16


def _pack_small(gs, loss_lanes):
    D = D_MODEL
    rb = jnp.pad(gs["rel_bias"].reshape(-1), (0, 3 * D - ATT_HEADS * N_REL)).reshape(3, D)
    rows = [gs["norm_mix"].reshape(1, D), gs["b_gate"].reshape(2, D), gs["norm_ffn"].reshape(1, D),
            gs["norm_final"].reshape(1, D), rb, loss_lanes.reshape(1, D)]
    used = sum(r.shape[0] for r in rows)
    return jnp.concatenate(rows + [jnp.zeros((SMALL_ROWS - used, D), f32)], axis=0)


def kernel(x, norm_mix, w_in, b_gate, rel_bias, w_ret_out, w_att_out, w_out, norm_ffn, w_ffn_gate, w_ffn_up, w_ffn_down, norm_final, loss_target, m_norm_mix, m_w_in, m_b_gate, m_rel_bias, m_w_ret_out, m_w_att_out, m_w_out, m_norm_ffn, m_w_ffn_gate, m_w_ffn_up, m_w_ffn_down, m_norm_final, v_norm_mix, v_w_in, v_b_gate, v_rel_bias, v_w_ret_out, v_w_att_out, v_w_out, v_norm_ffn, v_w_ffn_gate, v_w_ffn_up, v_w_ffn_down, v_norm_final):
    w = dict(norm_mix=norm_mix, w_in=w_in, b_gate=b_gate, rel_bias=rel_bias, w_ret_out=w_ret_out, w_att_out=w_att_out,
             w_out=w_out, norm_ffn=norm_ffn, w_ffn_gate=w_ffn_gate, w_ffn_up=w_ffn_up, w_ffn_down=w_ffn_down,
             norm_final=norm_final)
    m = dict(norm_mix=m_norm_mix, w_in=m_w_in, b_gate=m_b_gate, rel_bias=m_rel_bias, w_ret_out=m_w_ret_out,
             w_att_out=m_w_att_out, w_out=m_w_out, norm_ffn=m_norm_ffn, w_ffn_gate=m_w_ffn_gate, w_ffn_up=m_w_ffn_up,
             w_ffn_down=m_w_ffn_down, norm_final=m_norm_final)
    v = dict(norm_mix=v_norm_mix, w_in=v_w_in, b_gate=v_b_gate, rel_bias=v_rel_bias, w_ret_out=v_w_ret_out,
             w_att_out=v_w_att_out, w_out=v_w_out, norm_ffn=v_norm_ffn, w_ffn_gate=v_w_ffn_gate, w_ffn_up=v_w_ffn_up,
             w_ffn_down=v_w_ffn_down, norm_final=v_norm_final)
    xi, yi, ci = _place()
    k_me = 2 * xi + yi

    shard2 = {n: w[n][0] for n, _ in BIG}
    shard_shapes = [shard2[n].shape for n, _ in BIG]
    gathered = _gather_weights(_pack_rows([shard2[n] for n, _ in BIG], bf16))
    per_chip = [_unpack_rows(gathered[k], shard_shapes) for k in range(N_CHIPS)]
    wb = {n: jnp.concatenate([per_chip[k][i] for k in range(N_CHIPS)], axis=ax) for i, (n, ax) in enumerate(BIG)}
    nrel_loc = rel_bias.shape[-1]
    rb_all = _exchange_small(jnp.pad(rel_bias[0], ((0, 0), (0, 128 - nrel_loc))), "gather_rel_bias", False)
    rb_full = jnp.concatenate([rb_all[2 * k, :, :nrel_loc] for k in range(N_CHIPS)], axis=1)

    loss_lanes, grad_x, gbig, gsmall = _local_step(x, loss_target, norm_mix, b_gate, rb_full, norm_ffn, norm_final, wb)

    small = _exchange_small(_pack_small(gsmall, loss_lanes), "reduce_small", True)
    D = D_MODEL
    loss = jnp.sum(small[8])
    drb_full = small[5:8].reshape(-1)[:ATT_HEADS * N_REL].reshape(ATT_HEADS, N_REL)
    g = {
        "norm_mix": small[0:1], "b_gate": small[1:3].reshape(1, 2 * D), "norm_ffn": small[3:4], "norm_final": small[4],
        "rel_bias": lax.dynamic_slice_in_dim(drb_full, k_me * nrel_loc, nrel_loc, axis=1)[None],
    }

    gpack = jnp.stack([_pack_rows([_split4(gbig[n], ax)[k] for n, ax in BIG], bf16) for k in range(N_CHIPS)], axis=0)
    got = _rs_sibling(gpack)
    part = _rs_add_sibling(ci.reshape(1).astype(jnp.int32), gpack, got)
    parts = _rs_chips(part)
    halves = _rs_share(_rs_add_chips(parts))
    for (n, _), gs in zip(BIG, _unpack_rows(halves.reshape(-1, PACK_W), shard_shapes)):
        g[n] = gs[None]

    delta, new_m, new_v = {}, {}, {}
    for n, _ in BIG:
        d_, m_, v_ = _adamw(w[n][0], g[n][0], m[n][0], v[n][0], "adamw_" + n)
        delta[n], new_m[n], new_v[n] = d_[None], m_[None], v_[None]
    flat = lambda d: jnp.concatenate([d[n].reshape(-1) for n in SMALL])
    n_small = sum(int(np.prod(w[n].shape)) for n in SMALL)
    n_pad = -n_small % 1024
    packs = [jnp.pad(flat(d), (0, n_pad)).reshape(-1, 128) for d in (w, g, m, v)]
    outs = _adamw(*packs, "adamw_small")
    for res, dst in zip(outs, (delta, new_m, new_v)):
        off = 0
        fl = res.reshape(-1)
        for n in SMALL:
            sz = int(np.prod(w[n].shape))
            dst[n] = fl[off:off + sz].reshape(w[n].shape)
            off += sz

    return (loss, grad_x, *[g[n] for n in WEIGHTS], *[delta[n] for n in WEIGHTS], *[new_m[n] for n in WEIGHTS],
            *[new_v[n] for n in WEIGHTS])
```

```python
import functools
import math

import numpy as np
import jax
import jax.numpy as jnp
from jax import lax
from jax.experimental import pallas as pl
from jax.experimental.pallas import tpu as pltpu

f32 = jnp.float32
bf16 = jnp.bfloat16

D_MODEL = 1024
CHUNK = 64
RET_HEADS = 4
RET_KEY_DIM = 128
RET_VAL_DIM = 256
ATT_HEADS = 8
ATT_HEAD_DIM = 64
ATT_W = ATT_HEADS * ATT_HEAD_DIM
BAND_CHUNKS = 8
PAD = BAND_CHUNKS * CHUNK
MAX_REL = 256
N_REL = CHUNK + MAX_REL
D_FF = 2816
N_IN = 6656
ROPE_BASE = 10000.0
EPS = 1e-6
NEG_INF = -1e30
C_RQ, C_RK, C_RV, C_RG, C_AQ, C_AK, C_AV, C_GL = 0, 512, 1024, 2048, 3072, 3584, 4096, 4608

ADAM_LR, ADAM_B1, ADAM_B2, ADAM_EPS, ADAM_WD, ADAM_STEP = 0.001, 0.9, 0.999, 1e-08, 0.01, 10

N_CHIPS = 4
N_DEV = 8
PACK_W = 1024
QBLK = 256
KWIN = PAD + QBLK
TOEP = 1024
VMEM_LIMIT = 56 * 1024 * 1024
MESH = pl.DeviceIdType.MESH

BIG = (
    ("w_in", 1), ("w_ret_out", 0), ("w_att_out", 1), ("w_out", 0), ("w_ffn_gate", 1), ("w_ffn_up", 1), ("w_ffn_down", 0))
WEIGHTS = ("norm_mix", "w_in", "b_gate", "rel_bias", "w_ret_out", "w_att_out", "w_out", "norm_ffn", "w_ffn_gate",
           "w_ffn_up", "w_ffn_down", "norm_final")
SMALL = ("norm_mix", "b_gate", "rel_bias", "norm_ffn", "norm_final")


def _dot(a, b):
    return lax.dot_general(a, b, (((1,), (0,)), ((), ())), preferred_element_type=f32)


def _dot_nt(a, b):
    return lax.dot_general(a, b, (((1,), (1,)), ((), ())), preferred_element_type=f32)


def _dot_tn(a, b):
    return lax.dot_general(a, b, (((0,), (0,)), ((), ())), preferred_element_type=f32)


def _sig(x):
    return 1.0 / (1.0 + jnp.exp(-x))


def _tile(n, pref, mult):
    best = None
    for t in range(mult, min(n, pref) + 1, mult):
        if n % t == 0:
            best = t
    return best if best is not None else n


def _params(sem, vmem=VMEM_LIMIT):
    return pltpu.CompilerParams(dimension_semantics=sem, vmem_limit_bytes=vmem)


def _in_proj(x2, gamma, w):
    T, D = x2.shape
    ns = w.shape[2]
    tm = _tile(T, 512, 8)

    def body(x_ref, g_ref, w_ref, xn_ref, p_ref, xs_ref):
        @pl.when(pl.program_id(1) == 0)
        def _():
            x = x_ref[...]
            r = lax.rsqrt(jnp.mean(x * x, axis=-1, keepdims=True) + EPS)
            xn = (x * r * g_ref[...]).astype(bf16)
            xs_ref[...] = xn
            xn_ref[...] = xn

        p_ref[...] = _dot(xs_ref[...], w_ref[0]).astype(bf16)

    return pl.pallas_call(
        body, name="in_proj", grid=(T // tm, N_CHIPS),
        in_specs=[pl.BlockSpec((tm, D), lambda i, j: (i, 0)), pl.BlockSpec((1, D), lambda i, j: (0, 0)),
                  pl.BlockSpec((1, D, ns), lambda i, j: (j, 0, 0))],
        out_specs=[pl.BlockSpec((tm, D), lambda i, j: (i, 0)), pl.BlockSpec((tm, ns), lambda i, j: (i, j))],
        out_shape=[jax.ShapeDtypeStruct((T, D), bf16), jax.ShapeDtypeStruct((T, N_CHIPS * ns), bf16)],
        scratch_shapes=[pltpu.VMEM((tm, D), bf16)],
        compiler_params=_params(("parallel", "arbitrary")),
    )(x2, gamma, w)


def _rope_tables(S):
    d = RET_KEY_DIM
    freqs = ROPE_BASE ** (-jnp.arange(0, d, 2, dtype=f32) / d)
    ang = jnp.arange(S, dtype=f32)[:, None] * freqs[None, :]
    cos, sin = jnp.cos(ang), jnp.sin(ang)
    return jnp.concatenate([cos, cos], axis=1), jnp.concatenate([-sin, sin], axis=1)


def _decay_tables():
    H = RET_HEADS
    log_g = jnp.log(1.0 - 2.0 ** (-5.0 - jnp.arange(H, dtype=f32)))
    p = jnp.arange(CHUNK, dtype=f32)
    intra = jnp.exp(log_g[:, None, None] * jnp.abs(p[:, None] - p[None, :]))
    q_dec = jnp.exp(log_g[:, None] * (p[None, :] + 1.0))
    k_dec = jnp.exp(log_g[:, None] * (CHUNK - 1.0 - p[None, :]))
    c_dec = jnp.exp(log_g * CHUNK)
    q_dec = jnp.broadcast_to(q_dec[:, :, None], (H, CHUNK, RET_KEY_DIM))
    k_dec = jnp.broadcast_to(k_dec[:, :, None], (H, CHUNK, RET_KEY_DIM))
    c_dec = jnp.broadcast_to(c_dec[:, None, None], (H, 1, RET_VAL_DIM))
    return intra, q_dec, k_dec, c_dec


K_SCALE = RET_KEY_DIM ** -0.5


def _ret_tables_specs():
    return [pl.BlockSpec((1, CHUNK, CHUNK), lambda b, h: (h, 0, 0)),
            pl.BlockSpec((1, CHUNK, RET_KEY_DIM), lambda b, h: (h, 0, 0)),
            pl.BlockSpec((1, CHUNK, RET_KEY_DIM), lambda b, h: (h, 0, 0)),
            pl.BlockSpec((1, 1, RET_VAL_DIM), lambda b, h: (h, 0, 0))]


def _ret_fwd(proj, B, S, rope, decay):
    T = B * S
    nc = S // CHUNK
    H, dk, dv = RET_HEADS, RET_KEY_DIM, RET_VAL_DIM

    def body(q_ref, k_ref, v_ref, g_ref, cos_ref, sin_ref, intra_ref, qd_ref, kd_ref, cd_ref,
             qr_ref, kr_ref, o_ref, u_ref, st_ref):
        cos, sn = cos_ref[...], sin_ref[...]
        q = q_ref[...].astype(f32)
        k = k_ref[...].astype(f32)
        qr_ref[...] = (q * cos + pltpu.roll(q, dk // 2, 1) * sn).astype(bf16)
        kr_ref[...] = ((k * cos + pltpu.roll(k, dk // 2, 1) * sn) * K_SCALE).astype(bf16)
        intra, qd, kd, cd = intra_ref[0], qd_ref[0], kd_ref[0], cd_ref[0]

        def step(i, state):
            r = pl.ds(pl.multiple_of(i * CHUNK, CHUNK), CHUNK)
            qi, ki, vi = qr_ref[r, :], kr_ref[r, :], v_ref[r, :]
            sb = state.astype(bf16)
            st_ref[0, 0, i] = sb
            s = (_dot_nt(qi, ki) * intra).astype(bf16)
            o = _dot(s, vi) + _dot((qi.astype(f32) * qd).astype(bf16), sb)
            new_state = state * cd + _dot_tn((ki.astype(f32) * kd).astype(bf16), vi)
            mu = jnp.mean(o, axis=-1, keepdims=True)
            xc = o - mu
            var = jnp.mean(xc * xc, axis=-1, keepdims=True)
            oh = xc * lax.rsqrt(var + EPS)
            g = g_ref[r, :].astype(f32)
            o_ref[r, :] = o.astype(bf16)
            u_ref[r, :] = (g * _sig(g) * oh).astype(bf16)
            return new_state

        lax.fori_loop(0, nc, step, jnp.zeros((dk, dv), f32))

    return pl.pallas_call(
        body, name="ret_fwd", grid=(B, H),
        in_specs=[pl.BlockSpec((S, dk), lambda b, h: (b, C_RQ // dk + h)),
                  pl.BlockSpec((S, dk), lambda b, h: (b, C_RK // dk + h)),
                  pl.BlockSpec((S, dv), lambda b, h: (b, C_RV // dv + h)),
                  pl.BlockSpec((S, dv), lambda b, h: (b, C_RG // dv + h)),
                  pl.BlockSpec((S, dk), lambda b, h: (0, 0)), pl.BlockSpec((S, dk), lambda b, h: (0, 0)),
                  *_ret_tables_specs()],
        out_specs=[pl.BlockSpec((S, dk), lambda b, h: (b, h)), pl.BlockSpec((S, dk), lambda b, h: (b, h)),
                   pl.BlockSpec((S, dv), lambda b, h: (b, h)), pl.BlockSpec((S, dv), lambda b, h: (b, h)),
                   pl.BlockSpec((1, 1, nc, dk, dv), lambda b, h: (b, h, 0, 0, 0))],
        out_shape=[jax.ShapeDtypeStruct((T, H * dk), bf16), jax.ShapeDtypeStruct((T, H * dk), bf16),
                   jax.ShapeDtypeStruct((T, H * dv), bf16), jax.ShapeDtypeStruct((T, H * dv), bf16),
                   jax.ShapeDtypeStruct((B, H, nc, dk, dv), bf16)],
        compiler_params=_params(("parallel", "parallel")),
    )(proj, proj, proj, proj, *rope, *decay)


def _bias_rows(rb):
    last = rb[:, N_REL - 1:]
    return jnp.concatenate([
        jnp.broadcast_to(last, (ATT_HEADS, PAD - MAX_REL + 1)),
        jnp.flip(rb[:, :N_REL - 1], axis=1),
        jnp.broadcast_to(rb[:, :1], (ATT_HEADS, KWIN - PAD - CHUNK)),
        jnp.broadcast_to(last, (ATT_HEADS, TOEP - KWIN)),
    ], axis=1)


def _build_bias(t_ref, bias_ref):
    row = lax.broadcasted_iota(jnp.int32, (QBLK, KWIN), 0) // CHUNK
    col = lax.broadcasted_iota(jnp.int32, (QBLK, KWIN), 1) // CHUNK
    delta = BAND_CHUNKS + row - col
    vis = (delta >= 0) & (delta <= BAND_CHUNKS)
    for h in range(ATT_HEADS):
        t = jnp.broadcast_to(t_ref[h:h + 1, :], (QBLK, TOEP))
        rolled = pltpu.roll(t, 0, 1, stride=1, stride_axis=0)
        bias_ref[h] = jnp.where(vis, rolled[:, :KWIN], NEG_INF)


def _att_probs(qh, kh, bias):
    s = _dot_nt(qh, kh) * (ATT_HEAD_DIM ** -0.5) + bias
    m = jnp.max(s, axis=-1, keepdims=True)
    p = jnp.exp(s - m)
    return p * (1.0 / jnp.sum(p, axis=-1, keepdims=True))


def _by_window(i, step):
    sizes = list(range(QBLK, KWIN, QBLK))
    for n, nk in enumerate(sizes):
        pl.when(i == n)(functools.partial(step, nk))
    pl.when(i >= len(sizes))(functools.partial(step, KWIN))


def _att_fwd(proj, trows, B, S):
    T = B * S
    nq = S // QBLK
    dh = ATT_HEAD_DIM

    def body(q_ref, k_ref, v_ref, t_ref, o_ref, bias_ref):
        i = pl.program_id(1)

        @pl.when((pl.program_id(0) == 0) & (i == 0))
        def _():
            _build_bias(t_ref, bias_ref)

        def step(nk):
            win = pl.ds(pl.multiple_of((i + 1) * QBLK - nk, QBLK), nk)
            kw, vw = k_ref[win, :], v_ref[win, :]
            outs = []
            for h in range(ATT_HEADS):
                hs = slice(h * dh, (h + 1) * dh)
                pr = _att_probs(q_ref[:, hs], kw[:, hs], bias_ref[h, :, KWIN - nk:])
                outs.append(_dot(pr.astype(bf16), vw[:, hs]))
            o_ref[...] = jnp.concatenate(outs, axis=1).astype(bf16)

        _by_window(i, step)

    return pl.pallas_call(
        body, name="att_fwd", grid=(B, nq),
        in_specs=[pl.BlockSpec((QBLK, ATT_W), lambda b, i: (b * nq + i, C_AQ // ATT_W)),
                  pl.BlockSpec((S, ATT_W), lambda b, i: (b, C_AK // ATT_W)),
                  pl.BlockSpec((S, ATT_W), lambda b, i: (b, C_AV // ATT_W)),
                  pl.BlockSpec((ATT_HEADS, TOEP), lambda b, i: (0, 0))],
        out_specs=pl.BlockSpec((QBLK, ATT_W), lambda b, i: (b * nq + i, 0)),
        out_shape=jax.ShapeDtypeStruct((T, ATT_W), bf16),
        scratch_shapes=[pltpu.VMEM((ATT_HEADS, QBLK, KWIN), f32)],
        compiler_params=_params(("arbitrary", "arbitrary")),
    )(proj, proj, proj, trows)


def _gl_specs(tm):
    w = 512
    return [pl.BlockSpec((tm, w), functools.partial(lambda i, j: (i, C_GL // 512 + j), j=j)) for j in range(4)]


def _gates(gl_refs, bg_ref):
    gl = jnp.concatenate([r[...] for r in gl_refs], axis=1).astype(f32) + bg_ref[...]
    g = _sig(gl)
    return g[:, :D_MODEL], g[:, D_MODEL:]


def _mix_fwd(x2, proj, u, ao, b_gate, w_ro, w_ao, w_out):
    T, D = x2.shape
    tm = _tile(T, 512, 8)

    def body(x_ref, u_ref, ao_ref, g0, g1, g2, g3, bg_ref, wro_ref, wao_ref, wo_ref, h1_ref, yr_ref, ya_ref):
        yr = _dot(u_ref[...], wro_ref[...])
        ao = ao_ref[...]
        ya = jnp.concatenate([_dot(ao, wao_ref[k]) for k in range(N_CHIPS)], axis=1)
        gr, ga = _gates((g0, g1, g2, g3), bg_ref)
        mix = gr * yr + ga * ya
        h1_ref[...] = x_ref[...] + _dot(mix.astype(bf16), wo_ref[...])
        yr_ref[...] = yr.astype(bf16)
        ya_ref[...] = ya.astype(bf16)

    full = lambda a: pl.BlockSpec(a.shape, lambda i: (0,) * a.ndim)
    row = lambda n: pl.BlockSpec((tm, n), lambda i: (i, 0))
    return pl.pallas_call(
        body, name="mix_fwd", grid=(T // tm,),
        in_specs=[row(D), row(D), row(ATT_W), *_gl_specs(tm), full(b_gate), full(w_ro), full(w_ao), full(w_out)],
        out_specs=[row(D), row(D), row(D)],
        out_shape=[jax.ShapeDtypeStruct((T, D), f32), jax.ShapeDtypeStruct((T, D), bf16),
                   jax.ShapeDtypeStruct((T, D), bf16)],
        compiler_params=_params(("parallel",)),
    )(x2, u, ao, proj, proj, proj, proj, b_gate, w_ro, w_ao, w_out)


def _ffn_fwd(h1, g_ffn, wg, wu, wd, g_fin, target):
    T, D = h1.shape
    nf, _, tf = wg.shape
    tm = _tile(T, 512, 8)

    def body(h1_ref, g_ref, wg_ref, wu_ref, wd_ref, gf_ref, tg_ref, hn_ref, a_ref, b_ref, f_ref, dh2_ref, part_ref,
             hs_ref, acc_ref):
        j = pl.program_id(1)

        @pl.when(j == 0)
        def _():
            h = h1_ref[...]
            r = lax.rsqrt(jnp.mean(h * h, axis=-1, keepdims=True) + EPS)
            hn = (h * r * g_ref[...]).astype(bf16)
            hs_ref[...] = hn
            hn_ref[...] = hn
            acc_ref[...] = jnp.zeros_like(acc_ref)

        hn = hs_ref[...]
        a = _dot(hn, wg_ref[0])
        b = _dot(hn, wu_ref[0])
        f = ((a * _sig(a)) * b).astype(bf16)
        a_ref[0] = a.astype(bf16)
        b_ref[0] = b.astype(bf16)
        f_ref[0] = f
        acc_ref[...] += _dot(f, wd_ref[0])

        @pl.when(j == nf - 1)
        def _():
            h2 = h1_ref[...] + acc_ref[...]
            r = lax.rsqrt(jnp.mean(h2 * h2, axis=-1, keepdims=True) + EPS)
            n = h2 * r
            gf = gf_ref[...]
            e = n * gf - tg_ref[...]
            dy = e * (1.0 / D)
            dn = dy * gf
            dh2_ref[...] = r * (dn - n * jnp.mean(dn * n, axis=-1, keepdims=True))
            part_ref[...] = jnp.zeros_like(part_ref)
            part_ref[0:1, :] = jnp.sum(dy * n, axis=0, keepdims=True)
            part_ref[1:2, :] = (0.5 / D) * jnp.sum(e * e, axis=0, keepdims=True)

    row = lambda n: pl.BlockSpec((tm, n), lambda i, j: (i, 0))
    vec = pl.BlockSpec((1, D), lambda i, j: (0, 0))
    col = pl.BlockSpec((1, tm, tf), lambda i, j: (j, i, 0))
    wcol = pl.BlockSpec((1, D, tf), lambda i, j: (j, 0, 0))
    act = jax.ShapeDtypeStruct((nf, T, tf), bf16)
    return pl.pallas_call(
        body, name="ffn_fwd", grid=(T // tm, nf),
        in_specs=[row(D), vec, wcol, wcol, pl.BlockSpec((1, tf, D), lambda i, j: (j, 0, 0)), vec, row(D)],
        out_specs=[row(D), col, col, col, row(D), pl.BlockSpec((8, D), lambda i, j: (i, 0))],
        out_shape=[jax.ShapeDtypeStruct((T, D), bf16), act, act, act,
                   jax.ShapeDtypeStruct((T, D), f32), jax.ShapeDtypeStruct((T // tm * 8, D), f32)],
        scratch_shapes=[pltpu.VMEM((tm, D), bf16), pltpu.VMEM((tm, D), f32)],
        compiler_params=_params(("parallel", "arbitrary")),
    )(h1, g_ffn, wg, wu, wd, g_fin, target)


def _ffn_bwd(dh2, h1, g_ffn, a, b, wg, wu, wd):
    T, D = h1.shape
    nf, _, tf = wg.shape
    tm = _tile(T, 512, 8)

    def body(dh2_ref, h1_ref, g_ref, a_ref, b_ref, wg_ref, wu_ref, wd_ref, da_ref, db_ref, dh1_ref, part_ref,
             ds_ref, acc_ref):
        j = pl.program_id(1)

        @pl.when(j == 0)
        def _():
            ds_ref[...] = dh2_ref[...].astype(bf16)
            acc_ref[...] = jnp.zeros_like(acc_ref)

        df = _dot_nt(ds_ref[...], wd_ref[0])
        av = a_ref[0].astype(f32)
        sg = _sig(av)
        db = (df * (av * sg)).astype(bf16)
        da = (df * b_ref[0].astype(f32) * (sg * (1.0 + av * (1.0 - sg)))).astype(bf16)
        da_ref[0] = da
        db_ref[0] = db
        acc_ref[...] += _dot_nt(da, wg_ref[0]) + _dot_nt(db, wu_ref[0])

        @pl.when(j == nf - 1)
        def _():
            h = h1_ref[...]
            r = lax.rsqrt(jnp.mean(h * h, axis=-1, keepdims=True) + EPS)
            n = h * r
            dhn = acc_ref[...]
            dn = dhn * g_ref[...]
            dh1_ref[...] = dh2_ref[...] + r * (dn - n * jnp.mean(dn * n, axis=-1, keepdims=True))
            part_ref[...] = jnp.zeros_like(part_ref)
            part_ref[0:1, :] = jnp.sum(dhn * n, axis=0, keepdims=True)

    row = lambda n: pl.BlockSpec((tm, n), lambda i, j: (i, 0))
    col = pl.BlockSpec((1, tm, tf), lambda i, j: (j, i, 0))
    wcol = pl.BlockSpec((1, D, tf), lambda i, j: (j, 0, 0))
    act = jax.ShapeDtypeStruct((nf, T, tf), bf16)
    return pl.pallas_call(
        body, name="ffn_bwd", grid=(T // tm, nf),
        in_specs=[row(D), row(D), pl.BlockSpec((1, D), lambda i, j: (0, 0)), col, col, wcol, wcol,
                  pl.BlockSpec((1, tf, D), lambda i, j: (j, 0, 0))],
        out_specs=[col, col, row(D), pl.BlockSpec((8, D), lambda i, j: (i, 0))],
        out_shape=[act, act, jax.ShapeDtypeStruct((T, D), f32), jax.ShapeDtypeStruct((T // tm * 8, D), f32)],
        scratch_shapes=[pltpu.VMEM((tm, D), bf16), pltpu.VMEM((tm, D), f32)],
        compiler_params=_params(("parallel", "arbitrary")),
    )(dh2, h1, g_ffn, a, b, wg, wu, wd)


def _mix_bwd(dh1, proj, yr, ya, b_gate, w_ro, w_ao, w_out):
    T, D = dh1.shape
    tm = _tile(T, 512, 8)

    def body(dh1_ref, g0, g1, g2, g3, bg_ref, yr_ref, ya_ref, wro_ref, wao_ref, wo_ref,
             du_ref, dao_ref, dgl_ref, mix_ref, dyr_ref, dya_ref, part_ref):
        dmix = _dot_nt(dh1_ref[...].astype(bf16), wo_ref[...])
        gr, ga = _gates((g0, g1, g2, g3), bg_ref)
        yr = yr_ref[...].astype(f32)
        ya = ya_ref[...].astype(f32)
        dyr = (dmix * gr).astype(bf16)
        dya = (dmix * ga).astype(bf16)
        dgl = jnp.concatenate([dmix * yr * gr * (1.0 - gr), dmix * ya * ga * (1.0 - ga)], axis=1)
        du_ref[...] = _dot_nt(dyr, wro_ref[...]).astype(bf16)
        ns = wao_ref.shape[2]
        dao = _dot_nt(dya[:, :ns], wao_ref[0])
        for k in range(1, N_CHIPS):
            dao = dao + _dot_nt(dya[:, k * ns:(k + 1) * ns], wao_ref[k])
        dao_ref[...] = dao.astype(bf16)
        dgl_ref[...] = dgl.astype(bf16)
        mix_ref[...] = (gr * yr + ga * ya).astype(bf16)
        dyr_ref[...] = dyr
        dya_ref[...] = dya
        part_ref[...] = jnp.zeros_like(part_ref)
        part_ref[0:1, :] = jnp.sum(dgl, axis=0, keepdims=True)

    full = lambda a: pl.BlockSpec(a.shape, lambda i: (0,) * a.ndim)
    row = lambda n: pl.BlockSpec((tm, n), lambda i: (i, 0))
    return pl.pallas_call(
        body, name="mix_bwd", grid=(T // tm,),
        in_specs=[row(D), *_gl_specs(tm), full(b_gate), row(D), row(D), full(w_ro), full(w_ao), full(w_out)],
        out_specs=[row(D), row(ATT_W), row(2 * D), row(D), row(D), row(D), pl.BlockSpec((8, 2 * D), lambda i: (i, 0))],
        out_shape=[jax.ShapeDtypeStruct((T, D), bf16), jax.ShapeDtypeStruct((T, ATT_W), bf16),
                   jax.ShapeDtypeStruct((T, 2 * D), bf16), jax.ShapeDtypeStruct((T, D), bf16),
                   jax.ShapeDtypeStruct((T, D), bf16), jax.ShapeDtypeStruct((T, D), bf16),
                   jax.ShapeDtypeStruct((T // tm * 8, 2 * D), f32)],
        compiler_params=_params(("parallel",)),
    )(dh1, proj, proj, proj, proj, b_gate, yr, ya, w_ro, w_ao, w_out)


def _ret_bwd(proj, qr, kr, o, states, du, B, S, rope, decay):
    T = B * S
    nc = S // CHUNK
    H, dk, dv = RET_HEADS, RET_KEY_DIM, RET_VAL_DIM

    def body(qr_ref, kr_ref, v_ref, g_ref, o_ref, st_ref, du_ref, cos_ref, sin_ref, intra_ref, qd_ref, kd_ref, cd_ref,
             dq_ref, dk_ref, dv_ref, dg_ref, dqs_ref, dks_ref):
        intra, qd, kd, cd = intra_ref[0], qd_ref[0], kd_ref[0], cd_ref[0]

        def step(t, dstate):
            i = nc - 1 - t
            r = pl.ds(pl.multiple_of(i * CHUNK, CHUNK), CHUNK)
            qi, ki, vi = qr_ref[r, :], kr_ref[r, :], v_ref[r, :]
            si = st_ref[0, 0, i]
            o = o_ref[r, :].astype(f32)
            mu = jnp.mean(o, axis=-1, keepdims=True)
            xc = o - mu
            rstd = lax.rsqrt(jnp.mean(xc * xc, axis=-1, keepdims=True) + EPS)
            oh = xc * rstd
            g = g_ref[r, :].astype(f32)
            sg = _sig(g)
            dui = du_ref[r, :].astype(f32)
            dg_ref[r, :] = (dui * oh * (sg * (1.0 + g * (1.0 - sg)))).astype(bf16)
            doh = dui * (g * sg)
            do = rstd * (doh - jnp.mean(doh, axis=-1, keepdims=True)
                         - oh * jnp.mean(doh * oh, axis=-1, keepdims=True))
            dob = do.astype(bf16)
            p = (_dot_nt(qi, ki) * intra).astype(bf16)
            dsb = dstate.astype(bf16)
            kt = (ki.astype(f32) * kd).astype(bf16)
            qt = (qi.astype(f32) * qd).astype(bf16)
            dv_ref[r, :] = (_dot_tn(p, dob) + _dot(kt, dsb)).astype(bf16)
            da = (_dot_nt(dob, vi) * intra).astype(bf16)
            dqs_ref[r, :] = _dot(da, ki) + _dot_nt(dob, si) * qd
            dks_ref[r, :] = (_dot_tn(da, qi) + _dot_nt(vi, dsb) * kd) * K_SCALE
            return dstate * cd + _dot_tn(qt, dob)

        lax.fori_loop(0, nc, step, jnp.zeros((dk, dv), f32))
        cos, snb = cos_ref[...], -sin_ref[...]
        dq = dqs_ref[...]
        dkk = dks_ref[...]
        dq_ref[...] = (dq * cos + pltpu.roll(dq, dk // 2, 1) * snb).astype(bf16)
        dk_ref[...] = (dkk * cos + pltpu.roll(dkk, dk // 2, 1) * snb).astype(bf16)

    hk = lambda b, h: (b, h)
    return pl.pallas_call(
        body, name="ret_bwd", grid=(B, H),
        in_specs=[pl.BlockSpec((S, dk), hk), pl.BlockSpec((S, dk), hk),
                  pl.BlockSpec((S, dv), lambda b, h: (b, C_RV // dv + h)),
                  pl.BlockSpec((S, dv), lambda b, h: (b, C_RG // dv + h)),
                  pl.BlockSpec((S, dv), hk),
                  pl.BlockSpec((1, 1, nc, dk, dv), lambda b, h: (b, h, 0, 0, 0)),
                  pl.BlockSpec((S, dv), hk),
                  pl.BlockSpec((S, dk), lambda b, h: (0, 0)), pl.BlockSpec((S, dk), lambda b, h: (0, 0)),
                  *_ret_tables_specs()],
        out_specs=[pl.BlockSpec((S, dk), hk), pl.BlockSpec((S, dk), hk), pl.BlockSpec((S, dv), hk),
                   pl.BlockSpec((S, dv), hk)],
        out_shape=[jax.ShapeDtypeStruct((T, H * dk), bf16), jax.ShapeDtypeStruct((T, H * dk), bf16),
                   jax.ShapeDtypeStruct((T, H * dv), bf16), jax.ShapeDtypeStruct((T, H * dv), bf16)],
        scratch_shapes=[pltpu.VMEM((S, dk), f32), pltpu.VMEM((S, dk), f32)],
        compiler_params=_params(("parallel", "parallel")),
    )(qr, kr, proj, proj, o, states, du, *rope, *decay)


def _att_bwd(proj, dao, trows, B, S):
    T = B * S
    nq = S // QBLK
    dh = ATT_HEAD_DIM
    scale = ATT_HEAD_DIM ** -0.5

    def body(q_ref, k_ref, v_ref, do_ref, t_ref, dq_ref, dk_ref, dv_ref, vec_ref, bias_ref, dbias_ref, dka_ref, dva_ref):
        b, i = pl.program_id(0), pl.program_id(1)

        @pl.when((b == 0) & (i == 0))
        def _():
            _build_bias(t_ref, bias_ref)
            dbias_ref[...] = jnp.zeros_like(dbias_ref)

        @pl.when(i == 0)
        def _():
            dka_ref[...] = jnp.zeros_like(dka_ref)
            dva_ref[...] = jnp.zeros_like(dva_ref)

        def step(nk):
            win = pl.ds(pl.multiple_of((i + 1) * QBLK - nk, QBLK), nk)
            kw, vw = k_ref[win, :], v_ref[win, :]
            dqs, dks, dvs = [], [], []
            for h in range(ATT_HEADS):
                hs = slice(h * dh, (h + 1) * dh)
                qh, kh, vh, doh = q_ref[:, hs], kw[:, hs], vw[:, hs], do_ref[:, hs]
                pr = _att_probs(qh, kh, bias_ref[h, :, KWIN - nk:])
                dp = _dot_nt(doh, vh)
                ds = pr * (dp - jnp.sum(pr * dp, axis=-1, keepdims=True))
                dbias_ref[h, :, KWIN - nk:] += ds
                dsb = (ds * scale).astype(bf16)
                dqs.append(_dot(dsb, kh))
                dks.append(_dot_tn(dsb, qh))
                dvs.append(_dot_tn(pr.astype(bf16), doh))
            dq_ref[...] = jnp.concatenate(dqs, axis=1).astype(bf16)
            dka_ref[win, :] += jnp.concatenate(dks, axis=1)
            dva_ref[win, :] += jnp.concatenate(dvs, axis=1)

        _by_window(i, step)

        @pl.when(i == nq - 1)
        def _():
            dk_ref[...] = dka_ref[...].astype(bf16)
            dv_ref[...] = dva_ref[...].astype(bf16)

        @pl.when((b == B - 1) & (i == nq - 1))
        def _():
            rr = lax.broadcasted_iota(jnp.int32, (QBLK, QBLK), 0)
            cc = lax.broadcasted_iota(jnp.int32, (QBLK, QBLK), 1)
            flip = jnp.where(rr + cc == QBLK - 1, 1.0, 0.0).astype(bf16)
            for h in range(ATT_HEADS):
                d = dbias_ref[h]
                hi = d.astype(bf16)
                lo = (d - hi.astype(f32)).astype(bf16)
                rev = _dot(flip, hi) + _dot(flip, lo)
                wide = jnp.concatenate([rev, jnp.zeros((QBLK, TOEP - KWIN), f32)], axis=1)
                rolled = pltpu.roll(wide, 0, 1, stride=1, stride_axis=0)
                vec_ref[h:h + 1, :] = jnp.sum(rolled, axis=0, keepdims=True)

    qspec = lambda c: pl.BlockSpec((QBLK, ATT_W), lambda b, i: (b * nq + i, c))
    kspec = lambda c: pl.BlockSpec((S, ATT_W), lambda b, i: (b, c))
    seq = jax.ShapeDtypeStruct((T, ATT_W), bf16)
    return pl.pallas_call(
        body, name="att_bwd", grid=(B, nq),
        in_specs=[qspec(C_AQ // ATT_W), kspec(C_AK // ATT_W), kspec(C_AV // ATT_W), qspec(0),
                  pl.BlockSpec((ATT_HEADS, TOEP), lambda b, i: (0, 0))],
        out_specs=[qspec(0), kspec(0), kspec(0), pl.BlockSpec((ATT_HEADS, TOEP), lambda b, i: (0, 0))],
        out_shape=[seq, seq, seq, jax.ShapeDtypeStruct((ATT_HEADS, TOEP), f32)],
        scratch_shapes=[pltpu.VMEM((ATT_HEADS, QBLK, KWIN), f32), pltpu.VMEM((ATT_HEADS, QBLK, KWIN), f32),
                        pltpu.VMEM((S, ATT_W), f32), pltpu.VMEM((S, ATT_W), f32)],
        compiler_params=_params(("arbitrary", "arbitrary")),
    )(proj, proj, proj, dao, trows)


def _in_proj_bwd(dproj, w_in, x2, gamma, dh1):
    T, D = x2.shape
    nk, _, tk = w_in.shape
    tm = _tile(T, 512, 8)

    def body(dp_ref, w_ref, x_ref, g_ref, dh1_ref, dx_ref, part_ref, acc_ref):
        j = pl.program_id(1)

        @pl.when(j == 0)
        def _():
            acc_ref[...] = jnp.zeros_like(acc_ref)

        acc_ref[...] += _dot_nt(dp_ref[...], w_ref[0])

        @pl.when(j == nk - 1)
        def _():
            x = x_ref[...]
            r = lax.rsqrt(jnp.mean(x * x, axis=-1, keepdims=True) + EPS)
            n = x * r
            dxn = acc_ref[...]
            dn = dxn * g_ref[...]
            dx_ref[...] = dh1_ref[...] + r * (dn - n * jnp.mean(dn * n, axis=-1, keepdims=True))
            part_ref[...] = jnp.zeros_like(part_ref)
            part_ref[0:1, :] = jnp.sum(dxn * n, axis=0, keepdims=True)

    row = lambda n: pl.BlockSpec((tm, n), lambda i, j: (i, 0))
    return pl.pallas_call(
        body, name="in_proj_bwd", grid=(T // tm, nk),
        in_specs=[pl.BlockSpec((tm, tk), lambda i, j: (i, j)), pl.BlockSpec((1, D, tk), lambda i, j: (j, 0, 0)), row(D),
                  pl.BlockSpec((1, D), lambda i, j: (0, 0)), row(D)],
        out_specs=[row(D), pl.BlockSpec((8, D), lambda i, j: (i, 0))],
        out_shape=[jax.ShapeDtypeStruct((T, D), f32), jax.ShapeDtypeStruct((T // tm * 8, D), f32)],
        scratch_shapes=[pltpu.VMEM((tm, D), f32)],
        compiler_params=_params(("parallel", "arbitrary")),
    )(dproj, w_in, x2, gamma, dh1)


def _wgrad(a, b, shard_axis, name):
    def spec(arr, sharded, tt):
        if arr.ndim == 3:
            return arr.shape[2], pl.BlockSpec((1, tt, arr.shape[2]), lambda s, t: (s, t, 0))
        if sharded:
            w = arr.shape[1] // N_CHIPS
            return w, pl.BlockSpec((tt, w), lambda s, t: (t, s))
        return arr.shape[1], pl.BlockSpec((tt, arr.shape[1]), lambda s, t: (t, 0))

    T = a.shape[-2]
    tt = _tile(T, 512, 16)
    nt = T // tt
    K, a_spec = spec(a, shard_axis == 0, tt)
    N, b_spec = spec(b, shard_axis == 1, tt)

    def body(a_ref, b_ref, o_ref, acc_ref):
        t = pl.program_id(1)

        @pl.when(t == 0)
        def _():
            acc_ref[...] = jnp.zeros_like(acc_ref)

        av = a_ref[0] if a.ndim == 3 else a_ref[...]
        bv = b_ref[0] if b.ndim == 3 else b_ref[...]
        acc_ref[...] += _dot_tn(av.astype(bf16), bv.astype(bf16))

        @pl.when(t == nt - 1)
        def _():
            o_ref[0] = acc_ref[...].astype(bf16)

    return pl.pallas_call(
        body, name=name, grid=(N_CHIPS, nt), in_specs=[a_spec, b_spec],
        out_specs=pl.BlockSpec((1, K, N), lambda s, t: (s, 0, 0)),
        out_shape=jax.ShapeDtypeStruct((N_CHIPS, K, N), bf16),
        scratch_shapes=[pltpu.VMEM((K, N), f32)],
        compiler_params=_params(("parallel", "arbitrary")),
    )(a, b)


def _adamw_sum(place, w, m, v, part, from_chips, from_sibling, name):
    R, C = w.shape
    half = R // 2
    tr = _tile(half, max(16, (1 << 18) // C // 16 * 16), 16)
    nr = half // tr

    def body(p_ref, w_ref, m_ref, v_ref, part_ref, fc_ref, fs_ref, g_ref, d_ref, mo_ref, vo_ref):
        up = lambda x: x.astype(f32)
        mine = ((up(part_ref[0]) + up(fc_ref[0])) + up(fc_ref[1])) + up(fc_ref[2])
        sibs = ((up(fs_ref[0]) + up(fs_ref[1])) + up(fs_ref[2])) + up(fs_ref[3])
        g_ = jnp.where(pl.program_id(0) == p_ref[0], mine, sibs)
        m_ = ADAM_B1 * m_ref[...] + (1.0 - ADAM_B1) * g_
        v_ = ADAM_B2 * v_ref[...] + (1.0 - ADAM_B2) * (g_ * g_)
        m_hat = m_ / (1.0 - ADAM_B1 ** ADAM_STEP)
        v_hat = v_ / (1.0 - ADAM_B2 ** ADAM_STEP)
        g_ref[...] = g_
        d_ref[...] = -ADAM_LR * (m_hat / (jnp.sqrt(v_hat) + ADAM_EPS) + ADAM_WD * w_ref[...])
        mo_ref[...] = m_
        vo_ref[...] = v_

    spec = pl.BlockSpec((tr, C), lambda h, r, p: (h * nr + r, 0))
    return pl.pallas_call(
        body, name=name,
        grid_spec=pltpu.PrefetchScalarGridSpec(
            num_scalar_prefetch=1, grid=(2, nr),
            in_specs=[spec, spec, spec, pl.BlockSpec((1, tr, C), lambda h, r, p: (p[1], r, 0)),
                      pl.BlockSpec((3, tr, C), lambda h, r, p: (0, r, 0)),
                      pl.BlockSpec((4, tr, C), lambda h, r, p: (0, r, 0))],
            out_specs=[spec] * 4),
        out_shape=[jax.ShapeDtypeStruct((R, C), f32)] * 4,
        compiler_params=_params(("parallel", "parallel")),
    )(place, w, m, v, part, from_chips, from_sibling)


def _adamw(w, g, m, v, name):
    R, C = w.shape
    tr = _tile(R, max(8, (1 << 18) // C // 8 * 8), 8)

    def body(w_ref, g_ref, m_ref, v_ref, d_ref, mo_ref, vo_ref):
        g_ = g_ref[...]
        m_ = ADAM_B1 * m_ref[...] + (1.0 - ADAM_B1) * g_
        v_ = ADAM_B2 * v_ref[...] + (1.0 - ADAM_B2) * (g_ * g_)
        m_hat = m_ / (1.0 - ADAM_B1 ** ADAM_STEP)
        v_hat = v_ / (1.0 - ADAM_B2 ** ADAM_STEP)
        d_ref[...] = -ADAM_LR * (m_hat / (jnp.sqrt(v_hat) + ADAM_EPS) + ADAM_WD * w_ref[...])
        mo_ref[...] = m_
        vo_ref[...] = v_

    spec = pl.BlockSpec((tr, C), lambda i: (i, 0))
    return pl.pallas_call(
        body, name=name, grid=(R // tr,), in_specs=[spec] * 4, out_specs=[spec] * 3,
        out_shape=[jax.ShapeDtypeStruct((R, C), f32)] * 3,
        compiler_params=_params(("parallel",)),
    )(w, g, m, v)


def _place():
    return lax.axis_index("x"), lax.axis_index("y"), lax.axis_index("c")


def _other_chips(x, y):
    chips = [(1 - x, y), (x, 1 - y), (1 - x, 1 - y)]
    return chips, [2 * cx + cy for cx, cy in chips]


def _exchange_small(blk, name, reduce):
    R, C = blk.shape

    def body(x_ref, out_ref, *rest):
        if reduce:
            all_ref, send_sems, recv_sems = rest
        else:
            all_ref = out_ref
            send_sems, recv_sems = rest
        x, y, c = _place()
        me = 4 * x + 2 * y + c
        all_ref[me] = x_ref[...]
        copies = []
        for k in range(1, N_DEV):
            peer = tuple(1 - p if (k >> s) & 1 else p for p, s in ((x, 2), (y, 1), (c, 0)))
            cp = pltpu.make_async_remote_copy(src_ref=x_ref, dst_ref=all_ref.at[me], send_sem=send_sems.at[k - 1],
                                              recv_sem=recv_sems.at[k - 1], device_id=peer, device_id_type=MESH)
            cp.start()
            copies.append(cp)
        for cp in copies:
            cp.wait()
        if reduce:
            tot = all_ref[0]
            for d in range(1, N_DEV):
                tot = tot + all_ref[d]
            out_ref[...] = tot

    vm = pl.BlockSpec(memory_space=pltpu.VMEM)
    scratch = [pltpu.SemaphoreType.DMA((N_DEV - 1,)), pltpu.SemaphoreType.DMA((N_DEV - 1,))]
    if reduce:
        scratch = [pltpu.VMEM((N_DEV, R, C), f32)] + scratch
    return pl.pallas_call(
        body, name=name, in_specs=[vm], out_specs=vm,
        out_shape=jax.ShapeDtypeStruct((R, C) if reduce else (N_DEV, R, C), f32),
        scratch_shapes=scratch,
    )(blk)


def _cast_shard(place, w, name):
    R, C = w.shape
    tr = _tile(R, max(16, (1 << 19) // C // 16 * 16), 16)

    def body(p_ref, w_ref, o_ref):
        o_ref[0] = w_ref[...].astype(bf16)

    return pl.pallas_call(
        body, name=name,
        grid_spec=pltpu.PrefetchScalarGridSpec(
            num_scalar_prefetch=1, grid=(R // tr,),
            in_specs=[pl.BlockSpec((tr, C), lambda r, p: (r, 0))],
            out_specs=pl.BlockSpec((1, tr, C), lambda r, p: (p[1], r, 0))),
        out_shape=jax.ShapeDtypeStruct((N_CHIPS, R, C), bf16),
        compiler_params=_params(("parallel",)),
    )(place, w)


def _gather_weights(bufs):
    n = len(bufs)

    def body(*refs):
        out, send_sems, recv_sems = refs[n:2 * n], refs[2 * n], refs[2 * n + 1]
        x, y, c = _place()
        k_me = 2 * x + y
        chips, ks = _other_chips(x, y)

        def copy(a, s, rows, k, to):
            return pltpu.make_async_remote_copy(src_ref=out[a].at[k, rows], dst_ref=out[a].at[k, rows],
                                                send_sem=send_sems.at[a, s], recv_sem=recv_sems.at[a, s],
                                                device_id=to, device_id_type=MESH)

        halves = [b.shape[1] // 2 for b in bufs]
        mine = [pl.ds(c * h, h) for h in halves]
        theirs = [pl.ds((1 - c) * h, h) for h in halves]
        first = [copy(a, j, mine[a], k_me, (*chips[j], c)) for a in range(n) for j in range(3)]
        for cp in first:
            cp.start()
        passed = []
        for a in range(n):
            for j in range(3):
                copy(a, j, mine[a], ks[j], (x, y, c)).wait_recv()
                cp = copy(a, 3 + j, mine[a], ks[j], (x, y, 1 - c))
                cp.start()
                passed.append(cp)
        for a in range(n):
            for j in range(3):
                copy(a, 3 + j, theirs[a], ks[j], (x, y, c)).wait_recv()
        for cp in first + passed:
            cp.wait_send()

    anyspace = pl.BlockSpec(memory_space=pl.ANY)
    return pl.pallas_call(
        body, name="gather_weights", in_specs=[anyspace] * n, out_specs=[anyspace] * n,
        out_shape=[jax.ShapeDtypeStruct(b.shape, b.dtype) for b in bufs],
        input_output_aliases={a: a for a in range(n)},
        scratch_shapes=[pltpu.SemaphoreType.DMA((n, 6)), pltpu.SemaphoreType.DMA((n, 6))],
    )(*bufs)


def _rs_sibling(grads):
    n = len(grads)

    def body(*refs):
        g, out, send_sems, recv_sems = refs[:n], refs[n:2 * n], refs[2 * n], refs[2 * n + 1]
        x, y, c = _place()
        copies = []
        for a in range(n):
            half = grads[a].shape[1] // 2
            cp = pltpu.make_async_remote_copy(src_ref=g[a].at[:, pl.ds((1 - c) * half, half)], dst_ref=out[a],
                                              send_sem=send_sems.at[a], recv_sem=recv_sems.at[a],
                                              device_id=(x, y, 1 - c), device_id_type=MESH)
            cp.start()
            copies.append(cp)
        for cp in copies:
            cp.wait()

    anyspace = pl.BlockSpec(memory_space=pl.ANY)
    return pl.pallas_call(
        body, name="rs_sibling", in_specs=[anyspace] * n, out_specs=[anyspace] * n,
        out_shape=[jax.ShapeDtypeStruct((N_CHIPS, g.shape[1] // 2, g.shape[2]), g.dtype) for g in grads],
        scratch_shapes=[pltpu.SemaphoreType.DMA((n,)), pltpu.SemaphoreType.DMA((n,))],
    )(*grads)


def _rs_add_sibling(place, grad, got, name):
    _, R, C = grad.shape
    half = R // 2
    tr = _tile(half, max(16, (1 << 19) // C // 16 * 16), 16)
    nr = half // tr

    def body(p_ref, a_ref, b_ref, o_ref):
        o_ref[...] = (a_ref[...].astype(f32) + b_ref[...].astype(f32)).astype(o_ref.dtype)

    return pl.pallas_call(
        body, name=name,
        grid_spec=pltpu.PrefetchScalarGridSpec(
            num_scalar_prefetch=1, grid=(N_CHIPS, nr),
            in_specs=[pl.BlockSpec((1, tr, C), lambda k, r, p: (k, p[0] * nr + r, 0)),
                      pl.BlockSpec((1, tr, C), lambda k, r, p: (k, r, 0))],
            out_specs=pl.BlockSpec((1, tr, C), lambda k, r, p: (k, r, 0))),
        out_shape=jax.ShapeDtypeStruct((N_CHIPS, half, C), bf16),
        compiler_params=_params(("parallel", "parallel")),
    )(place, grad, got)


def _rs_chips(parts):
    n = len(parts)

    def body(*refs):
        p, fc, fs = refs[:n], refs[n:3 * n:2], refs[n + 1:3 * n:2]
        send_sems, recv_sems = refs[3 * n], refs[3 * n + 1]
        x, y, c = _place()
        k_me = 2 * x + y
        chips, ks = _other_chips(x, y)

        def copy(a, s, src, dst, to):
            return pltpu.make_async_remote_copy(src_ref=src, dst_ref=dst, send_sem=send_sems.at[a, s],
                                                recv_sem=recv_sems.at[a, s], device_id=to, device_id_type=MESH)

        sent = []
        for a in range(n):
            for j in range(3):
                sent.append(copy(a, j, p[a].at[ks[j]], fc[a].at[j], (*chips[j], c)))
            sent.append(copy(a, 3, p[a].at[k_me], fs[a].at[0], (x, y, 1 - c)))
        for cp in sent:
            cp.start()
        for a in range(n):
            for j in range(3):
                copy(a, j, p[a].at[0], fc[a].at[j], (x, y, c)).wait_recv()
                cp = copy(a, 4 + j, fc[a].at[j], fs[a].at[1 + j], (x, y, 1 - c))
                cp.start()
                sent.append(cp)
        for a in range(n):
            for s in range(4):
                copy(a, 3 + s, p[a].at[0], fs[a].at[s], (x, y, c)).wait_recv()
        for cp in sent:
            cp.wait_send()

    anyspace = pl.BlockSpec(memory_space=pl.ANY)
    shapes = []
    for q in parts:
        shapes += [jax.ShapeDtypeStruct((3,) + q.shape[1:], q.dtype), jax.ShapeDtypeStruct((4,) + q.shape[1:], q.dtype)]
    outs = pl.pallas_call(
        body, name="rs_chips", in_specs=[anyspace] * n, out_specs=[anyspace] * (2 * n), out_shape=shapes,
        scratch_shapes=[pltpu.SemaphoreType.DMA((n, 7)), pltpu.SemaphoreType.DMA((n, 7))],
    )(*parts)
    return outs[0::2], outs[1::2]


def _local_step(x, target, norm_mix, b_gate, rb_full, norm_ffn, norm_final, wb):
    B, S, D = x.shape
    T = B * S
    x2 = x.reshape(T, D)
    tg2 = target.reshape(T, D)
    rope, decay = _rope_tables(S), _decay_tables()
    trows = _bias_rows(rb_full)
    g_fin = norm_final.reshape(1, D)
    w_ro, w_out = wb["w_ret_out"].reshape(-1, D), wb["w_out"].reshape(-1, D)

    xn, proj = _in_proj(x2, norm_mix, wb["w_in"])
    qr, kr, o, u, states = _ret_fwd(proj, B, S, rope, decay)
    ao = _att_fwd(proj, trows, B, S)
    h1, yr, ya = _mix_fwd(x2, proj, u, ao, b_gate, w_ro, wb["w_att_out"], w_out)
    hn, a, b, f, dh2, part_fin = _ffn_fwd(h1, norm_ffn, wb["w_ffn_gate"], wb["w_ffn_up"], wb["w_ffn_down"], g_fin, tg2)

    da, db, dh1, part_ffn = _ffn_bwd(dh2, h1, norm_ffn, a, b, wb["w_ffn_gate"], wb["w_ffn_up"], wb["w_ffn_down"])
    du, dao, dgl, mix, dyr, dya, part_bg = _mix_bwd(dh1, proj, yr, ya, b_gate, w_ro, wb["w_att_out"], w_out)
    drq, drk, drv, drg = _ret_bwd(proj, qr, kr, o, states, du, B, S, rope, decay)
    daq, dak, dav, dvec = _att_bwd(proj, dao, trows, B, S)
    dproj = jnp.concatenate([drq, drk, drv, drg, daq, dak, dav, dgl], axis=1)
    gx, part_mix = _in_proj_bwd(dproj, wb["w_in"], x2, norm_mix, dh1)

    gbig = {
        "w_in": _wgrad(xn, dproj, 1, "wgrad_in"),
        "w_ret_out": _wgrad(u, dyr, 0, "wgrad_ret_out"),
        "w_att_out": _wgrad(ao, dya, 1, "wgrad_att_out"),
        "w_out": _wgrad(mix, dh1, 0, "wgrad_out"),
        "w_ffn_gate": _wgrad(hn, da, 1, "wgrad_ffn_gate"),
        "w_ffn_up": _wgrad(hn, db, 1, "wgrad_ffn_up"),
        "w_ffn_down": _wgrad(f, dh2, 0, "wgrad_ffn_down"),
    }
    rows = lambda p, r: p.reshape(-1, 8, p.shape[-1])[:, r, :].sum(axis=0)
    lo = KWIN - 1 - (MAX_REL - 1)
    drb = jnp.concatenate([jnp.flip(dvec[:, lo:lo + N_REL - 1], axis=1), dvec[:, :lo].sum(axis=1, keepdims=True)], axis=1)
    gsmall = {
        "norm_mix": rows(part_mix, 0), "b_gate": rows(part_bg, 0), "rel_bias": drb, "norm_ffn": rows(part_ffn, 0),
        "norm_final": rows(part_fin, 0),
    }
    return rows(part_fin, 1), gx.reshape(B, S, D), gbig, gsmall


SMALL_ROWS = 16


def _pack_small(gs, loss_lanes):
    D = D_MODEL
    rb = jnp.pad(gs["rel_bias"].reshape(-1), (0, 3 * D - ATT_HEADS * N_REL)).reshape(3, D)
    rows = [gs["norm_mix"].reshape(1, D), gs["b_gate"].reshape(2, D), gs["norm_ffn"].reshape(1, D),
            gs["norm_final"].reshape(1, D), rb, loss_lanes.reshape(1, D)]
    used = sum(r.shape[0] for r in rows)
    return jnp.concatenate(rows + [jnp.zeros((SMALL_ROWS - used, D), f32)], axis=0)


def kernel(x, norm_mix, w_in, b_gate, rel_bias, w_ret_out, w_att_out, w_out, norm_ffn, w_ffn_gate, w_ffn_up, w_ffn_down, norm_final, loss_target, m_norm_mix, m_w_in, m_b_gate, m_rel_bias, m_w_ret_out, m_w_att_out, m_w_out, m_norm_ffn, m_w_ffn_gate, m_w_ffn_up, m_w_ffn_down, m_norm_final, v_norm_mix, v_w_in, v_b_gate, v_rel_bias, v_w_ret_out, v_w_att_out, v_w_out, v_norm_ffn, v_w_ffn_gate, v_w_ffn_up, v_w_ffn_down, v_norm_final):
    w = dict(norm_mix=norm_mix, w_in=w_in, b_gate=b_gate, rel_bias=rel_bias, w_ret_out=w_ret_out, w_att_out=w_att_out,
             w_out=w_out, norm_ffn=norm_ffn, w_ffn_gate=w_ffn_gate, w_ffn_up=w_ffn_up, w_ffn_down=w_ffn_down,
             norm_final=norm_final)
    m = dict(norm_mix=m_norm_mix, w_in=m_w_in, b_gate=m_b_gate, rel_bias=m_rel_bias, w_ret_out=m_w_ret_out,
             w_att_out=m_w_att_out, w_out=m_w_out, norm_ffn=m_norm_ffn, w_ffn_gate=m_w_ffn_gate, w_ffn_up=m_w_ffn_up,
             w_ffn_down=m_w_ffn_down, norm_final=m_norm_final)
    v = dict(norm_mix=v_norm_mix, w_in=v_w_in, b_gate=v_b_gate, rel_bias=v_rel_bias, w_ret_out=v_w_ret_out,
             w_att_out=v_w_att_out, w_out=v_w_out, norm_ffn=v_norm_ffn, w_ffn_gate=v_w_ffn_gate, w_ffn_up=v_w_ffn_up,
             w_ffn_down=v_w_ffn_down, norm_final=v_norm_final)
    xi, yi, ci = _place()
    k_me = 2 * xi + yi

    place = jnp.stack([ci, k_me]).astype(jnp.int32)
    big = [n for n, _ in BIG]

    wb = dict(zip(big, _gather_weights([_cast_shard(place, w[n][0], "cast_" + n) for n in big])))
    nrel_loc = rel_bias.shape[-1]
    rb_all = _exchange_small(jnp.pad(rel_bias[0], ((0, 0), (0, 128 - nrel_loc))), "gather_rel_bias", False)
    rb_full = jnp.concatenate([rb_all[2 * k, :, :nrel_loc] for k in range(N_CHIPS)], axis=1)

    loss_lanes, grad_x, gbig, gsmall = _local_step(x, loss_target, norm_mix, b_gate, rb_full, norm_ffn, norm_final, wb)

    small = _exchange_small(_pack_small(gsmall, loss_lanes), "reduce_small", True)
    D = D_MODEL
    loss = jnp.sum(small[8])
    drb_full = small[5:8].reshape(-1)[:ATT_HEADS * N_REL].reshape(ATT_HEADS, N_REL)
    g = {
        "norm_mix": small[0:1], "b_gate": small[1:3].reshape(1, 2 * D), "norm_ffn": small[3:4], "norm_final": small[4],
        "rel_bias": lax.dynamic_slice_in_dim(drb_full, k_me * nrel_loc, nrel_loc, axis=1)[None],
    }

    got = _rs_sibling([gbig[n] for n in big])
    parts = [_rs_add_sibling(place, gbig[n], r, "rs_add_" + n) for n, r in zip(big, got)]
    from_chips, from_sibling = _rs_chips(parts)
    delta, new_m, new_v = {}, {}, {}
    for n, p, fc, fs in zip(big, parts, from_chips, from_sibling):
        g_, d_, m_, v_ = _adamw_sum(place, w[n][0], m[n][0], v[n][0], p, fc, fs, "adamw_" + n)
        g[n], delta[n], new_m[n], new_v[n] = g_[None], d_[None], m_[None], v_[None]
    flat = lambda d: jnp.concatenate([d[n].reshape(-1) for n in SMALL])
    n_small = sum(int(np.prod(w[n].shape)) for n in SMALL)
    n_pad = -n_small % 1024
    packs = [jnp.pad(flat(d), (0, n_pad)).reshape(-1, 128) for d in (w, g, m, v)]
    outs = _adamw(*packs, "adamw_small")
    for res, dst in zip(outs, (delta, new_m, new_v)):
        off = 0
        fl = res.reshape(-1)
        for n in SMALL:
            sz = int(np.prod(w[n].shape))
            dst[n] = fl[off:off + sz].reshape(w[n].shape)
            off += sz

    return (loss, grad_x, *[g[n] for n in WEIGHTS], *[delta[n] for n in WEIGHTS], *[new_m[n] for n in WEIGHTS],
            *[new_v[n] for n in WEIGHTS])
```

```python
import functools

import numpy as np
import jax
import jax.numpy as jnp
from jax import lax
from jax.experimental import pallas as pl
from jax.experimental.pallas import tpu as pltpu

f32 = jnp.float32
bf16 = jnp.bfloat16

D_MODEL = 1024
CHUNK = 64
RET_HEADS = 4
RET_KEY_DIM = 128
RET_VAL_DIM = 256
ATT_HEADS = 8
ATT_HEAD_DIM = 64
ATT_W = ATT_HEADS * ATT_HEAD_DIM
BAND_CHUNKS = 8
PAD = BAND_CHUNKS * CHUNK
MAX_REL = 256
N_REL = CHUNK + MAX_REL
D_FF = 2816
N_IN = 6656
ROPE_BASE = 10000.0
EPS = 1e-6
NEG_INF = -1e30
C_RQ, C_RK, C_RV, C_RG, C_AQ, C_AK, C_AV, C_GL = 0, 512, 1024, 2048, 3072, 3584, 4096, 4608

ADAM_LR, ADAM_B1, ADAM_B2, ADAM_EPS, ADAM_WD, ADAM_STEP = 0.001, 0.9, 0.999, 1e-08, 0.01, 10

N_CHIPS = 4
N_DEV = 8
WGRAD_ACC_BYTES = 8 * 1024 * 1024
QBLK = 256
KWIN = PAD + QBLK
TOEP = 1024
VMEM_LIMIT = 56 * 1024 * 1024
MESH = pl.DeviceIdType.MESH

BIG = (
    ("w_in", 1), ("w_ret_out", 0), ("w_att_out", 1), ("w_out", 0), ("w_ffn_gate", 1), ("w_ffn_up", 1), ("w_ffn_down", 0))
WEIGHTS = ("norm_mix", "w_in", "b_gate", "rel_bias", "w_ret_out", "w_att_out", "w_out", "norm_ffn", "w_ffn_gate",
           "w_ffn_up", "w_ffn_down", "norm_final")
SMALL = ("norm_mix", "b_gate", "rel_bias", "norm_ffn", "norm_final")


def _dot(a, b):
    return lax.dot_general(a, b, (((1,), (0,)), ((), ())), preferred_element_type=f32)


def _dot_nt(a, b):
    return lax.dot_general(a, b, (((1,), (1,)), ((), ())), preferred_element_type=f32)


def _dot_tn(a, b):
    return lax.dot_general(a, b, (((0,), (0,)), ((), ())), preferred_element_type=f32)


def _sig(x):
    return 1.0 / (1.0 + jnp.exp(-x))


def _tile(n, pref, mult):
    best = None
    for t in range(mult, min(n, pref) + 1, mult):
        if n % t == 0:
            best = t
    return best if best is not None else n


def _params(sem, vmem=VMEM_LIMIT):
    return pltpu.CompilerParams(dimension_semantics=sem, vmem_limit_bytes=vmem)


def _in_proj(x2, gamma, w):
    T, D = x2.shape
    ns = w.shape[2]
    tm = _tile(T, 512, 8)

    def body(x_ref, g_ref, w_ref, xn_ref, p_ref, xs_ref):
        @pl.when(pl.program_id(1) == 0)
        def _():
            x = x_ref[...]
            r = lax.rsqrt(jnp.mean(x * x, axis=-1, keepdims=True) + EPS)
            xn = (x * r * g_ref[...]).astype(bf16)
            xs_ref[...] = xn
            xn_ref[...] = xn

        p_ref[...] = _dot(xs_ref[...], w_ref[0]).astype(bf16)

    return pl.pallas_call(
        body, name="in_proj", grid=(T // tm, N_CHIPS),
        in_specs=[pl.BlockSpec((tm, D), lambda i, j: (i, 0)), pl.BlockSpec((1, D), lambda i, j: (0, 0)),
                  pl.BlockSpec((1, D, ns), lambda i, j: (j, 0, 0))],
        out_specs=[pl.BlockSpec((tm, D), lambda i, j: (i, 0)), pl.BlockSpec((tm, ns), lambda i, j: (i, j))],
        out_shape=[jax.ShapeDtypeStruct((T, D), bf16), jax.ShapeDtypeStruct((T, N_CHIPS * ns), bf16)],
        scratch_shapes=[pltpu.VMEM((tm, D), bf16)],
        compiler_params=_params(("parallel", "arbitrary")),
    )(x2, gamma, w)


def _rope_tables(S):
    d = RET_KEY_DIM
    freqs = ROPE_BASE ** (-jnp.arange(0, d, 2, dtype=f32) / d)
    ang = jnp.arange(S, dtype=f32)[:, None] * freqs[None, :]
    cos, sin = jnp.cos(ang), jnp.sin(ang)
    return jnp.concatenate([cos, cos], axis=1), jnp.concatenate([-sin, sin], axis=1)


def _decay_tables():
    H = RET_HEADS
    log_g = jnp.log(1.0 - 2.0 ** (-5.0 - jnp.arange(H, dtype=f32)))
    p = jnp.arange(CHUNK, dtype=f32)
    intra = jnp.exp(log_g[:, None, None] * jnp.abs(p[:, None] - p[None, :]))
    q_dec = jnp.exp(log_g[:, None] * (p[None, :] + 1.0))
    k_dec = jnp.exp(log_g[:, None] * (CHUNK - 1.0 - p[None, :]))
    c_dec = jnp.exp(log_g * CHUNK)
    q_dec = jnp.broadcast_to(q_dec[:, :, None], (H, CHUNK, RET_KEY_DIM))
    k_dec = jnp.broadcast_to(k_dec[:, :, None], (H, CHUNK, RET_KEY_DIM))
    c_dec = jnp.broadcast_to(c_dec[:, None, None], (H, 1, RET_VAL_DIM))
    return intra, q_dec, k_dec, c_dec


K_SCALE = RET_KEY_DIM ** -0.5


def _ret_tables_specs():
    return [pl.BlockSpec((1, CHUNK, CHUNK), lambda b, h: (h, 0, 0)),
            pl.BlockSpec((1, CHUNK, RET_KEY_DIM), lambda b, h: (h, 0, 0)),
            pl.BlockSpec((1, CHUNK, RET_KEY_DIM), lambda b, h: (h, 0, 0)),
            pl.BlockSpec((1, 1, RET_VAL_DIM), lambda b, h: (h, 0, 0))]


def _ret_fwd(proj, B, S, rope, decay, phase=None):
    T = B * S
    nc = S // CHUNK
    H, dk, dv = RET_HEADS, RET_KEY_DIM, RET_VAL_DIM

    def body(q_ref, k_ref, v_ref, g_ref, cos_ref, sin_ref, intra_ref, qd_ref, kd_ref, cd_ref,
             qr_ref, kr_ref, o_ref, u_ref, st_ref):
        cos, sn = cos_ref[...], sin_ref[...]
        q = q_ref[...].astype(f32)
        k = k_ref[...].astype(f32)
        qr_ref[...] = (q * cos + pltpu.roll(q, dk // 2, 1) * sn).astype(bf16)
        kr_ref[...] = ((k * cos + pltpu.roll(k, dk // 2, 1) * sn) * K_SCALE).astype(bf16)
        intra, qd, kd, cd = intra_ref[0], qd_ref[0], kd_ref[0], cd_ref[0]

        def step(i, state):
            r = pl.ds(pl.multiple_of(i * CHUNK, CHUNK), CHUNK)
            qi, ki, vi = qr_ref[r, :], kr_ref[r, :], v_ref[r, :]
            sb = state.astype(bf16)
            st_ref[0, 0, i] = sb
            s = (_dot_nt(qi, ki) * intra).astype(bf16)
            o = _dot(s, vi) + _dot((qi.astype(f32) * qd).astype(bf16), sb)
            new_state = state * cd + _dot_tn((ki.astype(f32) * kd).astype(bf16), vi)
            mu = jnp.mean(o, axis=-1, keepdims=True)
            xc = o - mu
            var = jnp.mean(xc * xc, axis=-1, keepdims=True)
            oh = xc * lax.rsqrt(var + EPS)
            g = g_ref[r, :].astype(f32)
            o_ref[r, :] = o.astype(bf16)
            u_ref[r, :] = (g * _sig(g) * oh).astype(bf16)
            return new_state

        lax.fori_loop(0, nc, step, jnp.zeros((dk, dv), f32))

    return _call(
        body, phase, name="ret_fwd", grid=(B, H), scratch_shapes=[],
        in_specs=[pl.BlockSpec((S, dk), lambda b, h: (b, C_RQ // dk + h)),
                  pl.BlockSpec((S, dk), lambda b, h: (b, C_RK // dk + h)),
                  pl.BlockSpec((S, dv), lambda b, h: (b, C_RV // dv + h)),
                  pl.BlockSpec((S, dv), lambda b, h: (b, C_RG // dv + h)),
                  pl.BlockSpec((S, dk), lambda b, h: (0, 0)), pl.BlockSpec((S, dk), lambda b, h: (0, 0)),
                  *_ret_tables_specs()],
        out_specs=[pl.BlockSpec((S, dk), lambda b, h: (b, h)), pl.BlockSpec((S, dk), lambda b, h: (b, h)),
                   pl.BlockSpec((S, dv), lambda b, h: (b, h)), pl.BlockSpec((S, dv), lambda b, h: (b, h)),
                   pl.BlockSpec((1, 1, nc, dk, dv), lambda b, h: (b, h, 0, 0, 0))],
        out_shape=[jax.ShapeDtypeStruct((T, H * dk), bf16), jax.ShapeDtypeStruct((T, H * dk), bf16),
                   jax.ShapeDtypeStruct((T, H * dv), bf16), jax.ShapeDtypeStruct((T, H * dv), bf16),
                   jax.ShapeDtypeStruct((B, H, nc, dk, dv), bf16)],
        args=(proj, proj, proj, proj, *rope, *decay))


def _bias_rows(rb):
    last = rb[:, N_REL - 1:]
    return jnp.concatenate([
        jnp.broadcast_to(last, (ATT_HEADS, PAD - MAX_REL + 1)),
        jnp.flip(rb[:, :N_REL - 1], axis=1),
        jnp.broadcast_to(rb[:, :1], (ATT_HEADS, KWIN - PAD - CHUNK)),
        jnp.broadcast_to(last, (ATT_HEADS, TOEP - KWIN)),
    ], axis=1)


def _build_bias(t_ref, bias_ref):
    row = lax.broadcasted_iota(jnp.int32, (QBLK, KWIN), 0) // CHUNK
    col = lax.broadcasted_iota(jnp.int32, (QBLK, KWIN), 1) // CHUNK
    delta = BAND_CHUNKS + row - col
    vis = (delta >= 0) & (delta <= BAND_CHUNKS)
    for h in range(ATT_HEADS):
        t = jnp.broadcast_to(t_ref[h:h + 1, :], (QBLK, TOEP))
        rolled = pltpu.roll(t, 0, 1, stride=1, stride_axis=0)
        bias_ref[h] = jnp.where(vis, rolled[:, :KWIN], NEG_INF)


def _att_probs(qh, kh, bias):
    s = _dot_nt(qh, kh) * (ATT_HEAD_DIM ** -0.5) + bias
    m = jnp.max(s, axis=-1, keepdims=True)
    p = jnp.exp(s - m)
    return p * (1.0 / jnp.sum(p, axis=-1, keepdims=True))


def _by_window(i, step):
    sizes = list(range(QBLK, KWIN, QBLK))
    for n, nk in enumerate(sizes):
        pl.when(i == n)(functools.partial(step, nk))
    pl.when(i >= len(sizes))(functools.partial(step, KWIN))


def _att_fwd(proj, trows, B, S, phase=None):
    T = B * S
    nq = S // QBLK
    dh = ATT_HEAD_DIM

    def body(q_ref, k_ref, v_ref, t_ref, o_ref, bias_ref):
        i = pl.program_id(1)

        @pl.when((pl.program_id(0) == 0) & (i == 0))
        def _():
            _build_bias(t_ref, bias_ref)

        def step(nk):
            win = pl.ds(pl.multiple_of((i + 1) * QBLK - nk, QBLK), nk)
            kw, vw = k_ref[win, :], v_ref[win, :]
            outs = []
            for h in range(ATT_HEADS):
                hs = slice(h * dh, (h + 1) * dh)
                pr = _att_probs(q_ref[:, hs], kw[:, hs], bias_ref[h, :, KWIN - nk:])
                outs.append(_dot(pr.astype(bf16), vw[:, hs]))
            o_ref[...] = jnp.concatenate(outs, axis=1).astype(bf16)

        _by_window(i, step)

    return _call(
        body, phase, name="att_fwd", grid=(B, nq),
        in_specs=[pl.BlockSpec((QBLK, ATT_W), lambda b, i: (b * nq + i, C_AQ // ATT_W)),
                  pl.BlockSpec((S, ATT_W), lambda b, i: (b, C_AK // ATT_W)),
                  pl.BlockSpec((S, ATT_W), lambda b, i: (b, C_AV // ATT_W)),
                  pl.BlockSpec((ATT_HEADS, TOEP), lambda b, i: (0, 0))],
        out_specs=[pl.BlockSpec((QBLK, ATT_W), lambda b, i: (b * nq + i, 0))],
        out_shape=[jax.ShapeDtypeStruct((T, ATT_W), bf16)],
        scratch_shapes=[pltpu.VMEM((ATT_HEADS, QBLK, KWIN), f32)],
        args=(proj, proj, proj, trows))


def _gl_specs(tm):
    w = 512
    return [pl.BlockSpec((tm, w), functools.partial(lambda i, j: (i, C_GL // 512 + j), j=j)) for j in range(4)]


def _gates(gl_refs, bg_ref):
    gl = jnp.concatenate([r[...] for r in gl_refs], axis=1).astype(f32) + bg_ref[...]
    g = _sig(gl)
    return g[:, :D_MODEL], g[:, D_MODEL:]


def _mix_fwd(x2, proj, u, ao, b_gate, w_ro, w_ao, w_out):
    T, D = x2.shape
    tm = _tile(T, 512, 8)

    def body(x_ref, u_ref, ao_ref, g0, g1, g2, g3, bg_ref, wro_ref, wao_ref, wo_ref, h1_ref, yr_ref, ya_ref):
        yr = _dot(u_ref[...], wro_ref[...])
        ao = ao_ref[...]
        ya = jnp.concatenate([_dot(ao, wao_ref[k]) for k in range(N_CHIPS)], axis=1)
        gr, ga = _gates((g0, g1, g2, g3), bg_ref)
        mix = gr * yr + ga * ya
        h1_ref[...] = x_ref[...] + _dot(mix.astype(bf16), wo_ref[...])
        yr_ref[...] = yr.astype(bf16)
        ya_ref[...] = ya.astype(bf16)

    full = lambda a: pl.BlockSpec(a.shape, lambda i: (0,) * a.ndim)
    row = lambda n: pl.BlockSpec((tm, n), lambda i: (i, 0))
    return pl.pallas_call(
        body, name="mix_fwd", grid=(T // tm,),
        in_specs=[row(D), row(D), row(ATT_W), *_gl_specs(tm), full(b_gate), full(w_ro), full(w_ao), full(w_out)],
        out_specs=[row(D), row(D), row(D)],
        out_shape=[jax.ShapeDtypeStruct((T, D), f32), jax.ShapeDtypeStruct((T, D), bf16),
                   jax.ShapeDtypeStruct((T, D), bf16)],
        compiler_params=_params(("parallel",)),
    )(x2, u, ao, proj, proj, proj, proj, b_gate, w_ro, w_ao, w_out)


def _ffn_fwd(h1, g_ffn, wg, wu, wd, g_fin, target):
    T, D = h1.shape
    nf, _, tf = wg.shape
    tm = _tile(T, 512, 8)

    def body(h1_ref, g_ref, wg_ref, wu_ref, wd_ref, gf_ref, tg_ref, hn_ref, a_ref, b_ref, f_ref, dh2_ref, part_ref,
             hs_ref, acc_ref):
        j = pl.program_id(1)

        @pl.when(j == 0)
        def _():
            h = h1_ref[...]
            r = lax.rsqrt(jnp.mean(h * h, axis=-1, keepdims=True) + EPS)
            hn = (h * r * g_ref[...]).astype(bf16)
            hs_ref[...] = hn
            hn_ref[...] = hn
            acc_ref[...] = jnp.zeros_like(acc_ref)

        hn = hs_ref[...]
        a = _dot(hn, wg_ref[0])
        b = _dot(hn, wu_ref[0])
        f = ((a * _sig(a)) * b).astype(bf16)
        a_ref[0] = a.astype(bf16)
        b_ref[0] = b.astype(bf16)
        f_ref[0] = f
        acc_ref[...] += _dot(f, wd_ref[0])

        @pl.when(j == nf - 1)
        def _():
            h2 = h1_ref[...] + acc_ref[...]
            r = lax.rsqrt(jnp.mean(h2 * h2, axis=-1, keepdims=True) + EPS)
            n = h2 * r
            gf = gf_ref[...]
            e = n * gf - tg_ref[...]
            dy = e * (1.0 / D)
            dn = dy * gf
            dh2_ref[...] = r * (dn - n * jnp.mean(dn * n, axis=-1, keepdims=True))
            part_ref[...] = jnp.zeros_like(part_ref)
            part_ref[0:1, :] = jnp.sum(dy * n, axis=0, keepdims=True)
            part_ref[1:2, :] = (0.5 / D) * jnp.sum(e * e, axis=0, keepdims=True)

    row = lambda n: pl.BlockSpec((tm, n), lambda i, j: (i, 0))
    vec = pl.BlockSpec((1, D), lambda i, j: (0, 0))
    col = pl.BlockSpec((1, tm, tf), lambda i, j: (j, i, 0))
    wcol = pl.BlockSpec((1, D, tf), lambda i, j: (j, 0, 0))
    act = jax.ShapeDtypeStruct((nf, T, tf), bf16)
    return pl.pallas_call(
        body, name="ffn_fwd", grid=(T // tm, nf),
        in_specs=[row(D), vec, wcol, wcol, pl.BlockSpec((1, tf, D), lambda i, j: (j, 0, 0)), vec, row(D)],
        out_specs=[row(D), col, col, col, row(D), pl.BlockSpec((8, D), lambda i, j: (i, 0))],
        out_shape=[jax.ShapeDtypeStruct((T, D), bf16), act, act, act,
                   jax.ShapeDtypeStruct((T, D), f32), jax.ShapeDtypeStruct((T // tm * 8, D), f32)],
        scratch_shapes=[pltpu.VMEM((tm, D), bf16), pltpu.VMEM((tm, D), f32)],
        compiler_params=_params(("parallel", "arbitrary")),
    )(h1, g_ffn, wg, wu, wd, g_fin, target)


def _ffn_bwd(dh2, h1, g_ffn, a, b, wg, wu, wd):
    T, D = h1.shape
    nf, _, tf = wg.shape
    tm = _tile(T, 512, 8)

    def body(dh2_ref, h1_ref, g_ref, a_ref, b_ref, wg_ref, wu_ref, wd_ref, da_ref, db_ref, dh1_ref, part_ref,
             ds_ref, acc_ref):
        j = pl.program_id(1)

        @pl.when(j == 0)
        def _():
            ds_ref[...] = dh2_ref[...].astype(bf16)
            acc_ref[...] = jnp.zeros_like(acc_ref)

        df = _dot_nt(ds_ref[...], wd_ref[0])
        av = a_ref[0].astype(f32)
        sg = _sig(av)
        db = (df * (av * sg)).astype(bf16)
        da = (df * b_ref[0].astype(f32) * (sg * (1.0 + av * (1.0 - sg)))).astype(bf16)
        da_ref[0] = da
        db_ref[0] = db
        acc_ref[...] += _dot_nt(da, wg_ref[0]) + _dot_nt(db, wu_ref[0])

        @pl.when(j == nf - 1)
        def _():
            h = h1_ref[...]
            r = lax.rsqrt(jnp.mean(h * h, axis=-1, keepdims=True) + EPS)
            n = h * r
            dhn = acc_ref[...]
            dn = dhn * g_ref[...]
            dh1_ref[...] = dh2_ref[...] + r * (dn - n * jnp.mean(dn * n, axis=-1, keepdims=True))
            part_ref[...] = jnp.zeros_like(part_ref)
            part_ref[0:1, :] = jnp.sum(dhn * n, axis=0, keepdims=True)

    row = lambda n: pl.BlockSpec((tm, n), lambda i, j: (i, 0))
    col = pl.BlockSpec((1, tm, tf), lambda i, j: (j, i, 0))
    wcol = pl.BlockSpec((1, D, tf), lambda i, j: (j, 0, 0))
    act = jax.ShapeDtypeStruct((nf, T, tf), bf16)
    return pl.pallas_call(
        body, name="ffn_bwd", grid=(T // tm, nf),
        in_specs=[row(D), row(D), pl.BlockSpec((1, D), lambda i, j: (0, 0)), col, col, wcol, wcol,
                  pl.BlockSpec((1, tf, D), lambda i, j: (j, 0, 0))],
        out_specs=[col, col, row(D), pl.BlockSpec((8, D), lambda i, j: (i, 0))],
        out_shape=[act, act, jax.ShapeDtypeStruct((T, D), f32), jax.ShapeDtypeStruct((T // tm * 8, D), f32)],
        scratch_shapes=[pltpu.VMEM((tm, D), bf16), pltpu.VMEM((tm, D), f32)],
        compiler_params=_params(("parallel", "arbitrary")),
    )(dh2, h1, g_ffn, a, b, wg, wu, wd)


def _mix_bwd(dh1, proj, yr, ya, b_gate, w_ro, w_ao, w_out):
    T, D = dh1.shape
    tm = _tile(T, 512, 8)

    def body(dh1_ref, g0, g1, g2, g3, bg_ref, yr_ref, ya_ref, wro_ref, wao_ref, wo_ref,
             du_ref, dao_ref, dgl_ref, mix_ref, dyr_ref, dya_ref, part_ref):
        dmix = _dot_nt(dh1_ref[...].astype(bf16), wo_ref[...])
        gr, ga = _gates((g0, g1, g2, g3), bg_ref)
        yr = yr_ref[...].astype(f32)
        ya = ya_ref[...].astype(f32)
        dyr = (dmix * gr).astype(bf16)
        dya = (dmix * ga).astype(bf16)
        dgl = jnp.concatenate([dmix * yr * gr * (1.0 - gr), dmix * ya * ga * (1.0 - ga)], axis=1)
        du_ref[...] = _dot_nt(dyr, wro_ref[...]).astype(bf16)
        ns = wao_ref.shape[2]
        dao = _dot_nt(dya[:, :ns], wao_ref[0])
        for k in range(1, N_CHIPS):
            dao = dao + _dot_nt(dya[:, k * ns:(k + 1) * ns], wao_ref[k])
        dao_ref[...] = dao.astype(bf16)
        dgl_ref[...] = dgl.astype(bf16)
        mix_ref[...] = (gr * yr + ga * ya).astype(bf16)
        dyr_ref[...] = dyr
        dya_ref[...] = dya
        part_ref[...] = jnp.zeros_like(part_ref)
        part_ref[0:1, :] = jnp.sum(dgl, axis=0, keepdims=True)

    full = lambda a: pl.BlockSpec(a.shape, lambda i: (0,) * a.ndim)
    row = lambda n: pl.BlockSpec((tm, n), lambda i: (i, 0))
    return pl.pallas_call(
        body, name="mix_bwd", grid=(T // tm,),
        in_specs=[row(D), *_gl_specs(tm), full(b_gate), row(D), row(D), full(w_ro), full(w_ao), full(w_out)],
        out_specs=[row(D), row(ATT_W), row(2 * D), row(D), row(D), row(D), pl.BlockSpec((8, 2 * D), lambda i: (i, 0))],
        out_shape=[jax.ShapeDtypeStruct((T, D), bf16), jax.ShapeDtypeStruct((T, ATT_W), bf16),
                   jax.ShapeDtypeStruct((T, 2 * D), bf16), jax.ShapeDtypeStruct((T, D), bf16),
                   jax.ShapeDtypeStruct((T, D), bf16), jax.ShapeDtypeStruct((T, D), bf16),
                   jax.ShapeDtypeStruct((T // tm * 8, 2 * D), f32)],
        compiler_params=_params(("parallel",)),
    )(dh1, proj, proj, proj, proj, b_gate, yr, ya, w_ro, w_ao, w_out)


def _ret_bwd(proj, qr, kr, o, states, du, B, S, rope, decay, phase=None):
    T = B * S
    nc = S // CHUNK
    H, dk, dv = RET_HEADS, RET_KEY_DIM, RET_VAL_DIM

    def body(qr_ref, kr_ref, v_ref, g_ref, o_ref, st_ref, du_ref, cos_ref, sin_ref, intra_ref, qd_ref, kd_ref, cd_ref,
             dq_ref, dk_ref, dv_ref, dg_ref, dqs_ref, dks_ref):
        intra, qd, kd, cd = intra_ref[0], qd_ref[0], kd_ref[0], cd_ref[0]

        def step(t, dstate):
            i = nc - 1 - t
            r = pl.ds(pl.multiple_of(i * CHUNK, CHUNK), CHUNK)
            qi, ki, vi = qr_ref[r, :], kr_ref[r, :], v_ref[r, :]
            si = st_ref[0, 0, i]
            o = o_ref[r, :].astype(f32)
            mu = jnp.mean(o, axis=-1, keepdims=True)
            xc = o - mu
            rstd = lax.rsqrt(jnp.mean(xc * xc, axis=-1, keepdims=True) + EPS)
            oh = xc * rstd
            g = g_ref[r, :].astype(f32)
            sg = _sig(g)
            dui = du_ref[r, :].astype(f32)
            dg_ref[r, :] = (dui * oh * (sg * (1.0 + g * (1.0 - sg)))).astype(bf16)
            doh = dui * (g * sg)
            do = rstd * (doh - jnp.mean(doh, axis=-1, keepdims=True)
                         - oh * jnp.mean(doh * oh, axis=-1, keepdims=True))
            dob = do.astype(bf16)
            p = (_dot_nt(qi, ki) * intra).astype(bf16)
            dsb = dstate.astype(bf16)
            kt = (ki.astype(f32) * kd).astype(bf16)
            qt = (qi.astype(f32) * qd).astype(bf16)
            dv_ref[r, :] = (_dot_tn(p, dob) + _dot(kt, dsb)).astype(bf16)
            da = (_dot_nt(dob, vi) * intra).astype(bf16)
            dqs_ref[r, :] = _dot(da, ki) + _dot_nt(dob, si) * qd
            dks_ref[r, :] = (_dot_tn(da, qi) + _dot_nt(vi, dsb) * kd) * K_SCALE
            return dstate * cd + _dot_tn(qt, dob)

        lax.fori_loop(0, nc, step, jnp.zeros((dk, dv), f32))
        cos, snb = cos_ref[...], -sin_ref[...]
        dq = dqs_ref[...]
        dkk = dks_ref[...]
        dq_ref[...] = (dq * cos + pltpu.roll(dq, dk // 2, 1) * snb).astype(bf16)
        dk_ref[...] = (dkk * cos + pltpu.roll(dkk, dk // 2, 1) * snb).astype(bf16)

    hk = lambda b, h: (b, h)
    return _call(
        body, phase, name="ret_bwd", grid=(B, H),
        in_specs=[pl.BlockSpec((S, dk), hk), pl.BlockSpec((S, dk), hk),
                  pl.BlockSpec((S, dv), lambda b, h: (b, C_RV // dv + h)),
                  pl.BlockSpec((S, dv), lambda b, h: (b, C_RG // dv + h)),
                  pl.BlockSpec((S, dv), hk),
                  pl.BlockSpec((1, 1, nc, dk, dv), lambda b, h: (b, h, 0, 0, 0)),
                  pl.BlockSpec((S, dv), hk),
                  pl.BlockSpec((S, dk), lambda b, h: (0, 0)), pl.BlockSpec((S, dk), lambda b, h: (0, 0)),
                  *_ret_tables_specs()],
        out_specs=[pl.BlockSpec((S, dk), hk), pl.BlockSpec((S, dk), hk), pl.BlockSpec((S, dv), hk),
                   pl.BlockSpec((S, dv), hk)],
        out_shape=[jax.ShapeDtypeStruct((T, H * dk), bf16), jax.ShapeDtypeStruct((T, H * dk), bf16),
                   jax.ShapeDtypeStruct((T, H * dv), bf16), jax.ShapeDtypeStruct((T, H * dv), bf16)],
        scratch_shapes=[pltpu.VMEM((S, dk), f32), pltpu.VMEM((S, dk), f32)],
        args=(qr, kr, proj, proj, o, states, du, *rope, *decay))


def _att_bwd(proj, dao, trows, B, S, phase=None):
    T = B * S
    nq = S // QBLK
    dh = ATT_HEAD_DIM
    scale = ATT_HEAD_DIM ** -0.5

    def body(q_ref, k_ref, v_ref, do_ref, t_ref, dq_ref, dk_ref, dv_ref, vec_ref, bias_ref, dbias_ref, dka_ref, dva_ref):
        b, i = pl.program_id(0), pl.program_id(1)

        @pl.when((b == 0) & (i == 0))
        def _():
            _build_bias(t_ref, bias_ref)
            dbias_ref[...] = jnp.zeros_like(dbias_ref)

        @pl.when(i == 0)
        def _():
            dka_ref[...] = jnp.zeros_like(dka_ref)
            dva_ref[...] = jnp.zeros_like(dva_ref)

        def step(nk):
            win = pl.ds(pl.multiple_of((i + 1) * QBLK - nk, QBLK), nk)
            kw, vw = k_ref[win, :], v_ref[win, :]
            dqs, dks, dvs = [], [], []
            for h in range(ATT_HEADS):
                hs = slice(h * dh, (h + 1) * dh)
                qh, kh, vh, doh = q_ref[:, hs], kw[:, hs], vw[:, hs], do_ref[:, hs]
                pr = _att_probs(qh, kh, bias_ref[h, :, KWIN - nk:])
                dp = _dot_nt(doh, vh)
                ds = pr * (dp - jnp.sum(pr * dp, axis=-1, keepdims=True))
                dbias_ref[h, :, KWIN - nk:] += ds
                dsb = (ds * scale).astype(bf16)
                dqs.append(_dot(dsb, kh))
                dks.append(_dot_tn(dsb, qh))
                dvs.append(_dot_tn(pr.astype(bf16), doh))
            dq_ref[...] = jnp.concatenate(dqs, axis=1).astype(bf16)
            dka_ref[win, :] += jnp.concatenate(dks, axis=1)
            dva_ref[win, :] += jnp.concatenate(dvs, axis=1)

        _by_window(i, step)

        @pl.when(i == nq - 1)
        def _():
            dk_ref[...] = dka_ref[...].astype(bf16)
            dv_ref[...] = dva_ref[...].astype(bf16)

        @pl.when((b == B - 1) & (i == nq - 1))
        def _():
            rr = lax.broadcasted_iota(jnp.int32, (QBLK, QBLK), 0)
            cc = lax.broadcasted_iota(jnp.int32, (QBLK, QBLK), 1)
            flip = jnp.where(rr + cc == QBLK - 1, 1.0, 0.0).astype(bf16)
            for h in range(ATT_HEADS):
                d = dbias_ref[h]
                hi = d.astype(bf16)
                lo = (d - hi.astype(f32)).astype(bf16)
                rev = _dot(flip, hi) + _dot(flip, lo)
                wide = jnp.concatenate([rev, jnp.zeros((QBLK, TOEP - KWIN), f32)], axis=1)
                rolled = pltpu.roll(wide, 0, 1, stride=1, stride_axis=0)
                vec_ref[h:h + 1, :] = jnp.sum(rolled, axis=0, keepdims=True)

    qspec = lambda c: pl.BlockSpec((QBLK, ATT_W), lambda b, i: (b * nq + i, c))
    kspec = lambda c: pl.BlockSpec((S, ATT_W), lambda b, i: (b, c))
    seq = jax.ShapeDtypeStruct((T, ATT_W), bf16)
    return _call(
        body, phase, name="att_bwd", grid=(B, nq),
        in_specs=[qspec(C_AQ // ATT_W), kspec(C_AK // ATT_W), kspec(C_AV // ATT_W), qspec(0),
                  pl.BlockSpec((ATT_HEADS, TOEP), lambda b, i: (0, 0))],
        out_specs=[qspec(0), kspec(0), kspec(0), pl.BlockSpec((ATT_HEADS, TOEP), lambda b, i: (0, 0))],
        out_shape=[seq, seq, seq, jax.ShapeDtypeStruct((ATT_HEADS, TOEP), f32)],
        scratch_shapes=[pltpu.VMEM((ATT_HEADS, QBLK, KWIN), f32), pltpu.VMEM((ATT_HEADS, QBLK, KWIN), f32),
                        pltpu.VMEM((S, ATT_W), f32), pltpu.VMEM((S, ATT_W), f32)],
        args=(proj, proj, proj, dao, trows))


def _in_proj_bwd(dproj, w_in, x2, gamma, dh1, phase=None):
    T, D = x2.shape
    nk, _, tk = w_in.shape
    tm = _tile(T, 512, 8)

    def body(dp_ref, w_ref, x_ref, g_ref, dh1_ref, dx_ref, part_ref, acc_ref):
        j = pl.program_id(1)

        @pl.when(j == 0)
        def _():
            acc_ref[...] = jnp.zeros_like(acc_ref)

        acc_ref[...] += _dot_nt(dp_ref[...], w_ref[0])

        @pl.when(j == nk - 1)
        def _():
            x = x_ref[...]
            r = lax.rsqrt(jnp.mean(x * x, axis=-1, keepdims=True) + EPS)
            n = x * r
            dxn = acc_ref[...]
            dn = dxn * g_ref[...]
            dx_ref[...] = dh1_ref[...] + r * (dn - n * jnp.mean(dn * n, axis=-1, keepdims=True))
            part_ref[...] = jnp.zeros_like(part_ref)
            part_ref[0:1, :] = jnp.sum(dxn * n, axis=0, keepdims=True)

    row = lambda n: pl.BlockSpec((tm, n), lambda i, j: (i, 0))
    return _call(
        body, phase, name="in_proj_bwd", grid=(T // tm, nk),
        in_specs=[pl.BlockSpec((tm, tk), lambda i, j: (i, j)), pl.BlockSpec((1, D, tk), lambda i, j: (j, 0, 0)), row(D),
                  pl.BlockSpec((1, D), lambda i, j: (0, 0)), row(D)],
        out_specs=[row(D), pl.BlockSpec((8, D), lambda i, j: (i, 0))],
        out_shape=[jax.ShapeDtypeStruct((T, D), f32), jax.ShapeDtypeStruct((T // tm * 8, D), f32)],
        scratch_shapes=[pltpu.VMEM((tm, D), f32)],
        args=(dproj, w_in, x2, gamma, dh1))


def _wgrad(a, b, shard_axis, name):
    def spec(arr, sharded, tt):
        if arr.ndim == 3:
            return arr.shape[2], pl.BlockSpec((1, tt, arr.shape[2]), lambda s, t: (s, t, 0))
        if sharded:
            w = arr.shape[1] // N_CHIPS
            return w, pl.BlockSpec((tt, w), lambda s, t: (t, s))
        return arr.shape[1], pl.BlockSpec((tt, arr.shape[1]), lambda s, t: (t, 0))

    T = a.shape[-2]
    tt = _tile(T, 512, 16)
    nt = T // tt
    whole = a.ndim == 2 and b.ndim == 2 and a.shape[1] * b.shape[1] * 4 <= WGRAD_ACC_BYTES
    if whole:
        K, N = a.shape[1], b.shape[1]
        a_spec, b_spec = pl.BlockSpec((tt, K), lambda s, t: (t, 0)), pl.BlockSpec((tt, N), lambda s, t: (t, 0))
        out_block = (N_CHIPS, K // N_CHIPS, N) if shard_axis == 0 else (N_CHIPS, K, N // N_CHIPS)
        out_spec = pl.BlockSpec(out_block, lambda s, t: (0, 0, 0))
    else:
        K, a_spec = spec(a, shard_axis == 0, tt)
        N, b_spec = spec(b, shard_axis == 1, tt)
        out_block = (N_CHIPS, K, N)
        out_spec = pl.BlockSpec((1, K, N), lambda s, t: (s, 0, 0))

    def body(a_ref, b_ref, o_ref, acc_ref):
        t = pl.program_id(1)

        @pl.when(t == 0)
        def _():
            acc_ref[...] = jnp.zeros_like(acc_ref)

        av = a_ref[0] if a.ndim == 3 else a_ref[...]
        bv = b_ref[0] if b.ndim == 3 else b_ref[...]
        acc_ref[...] += _dot_tn(av.astype(bf16), bv.astype(bf16))

        @pl.when(t == nt - 1)
        def _():
            if not whole:
                o_ref[0] = acc_ref[...].astype(bf16)
            else:
                _, kk, nn = out_block
                for s in range(N_CHIPS):
                    o_ref[s] = (acc_ref[s * kk:(s + 1) * kk, :] if shard_axis == 0
                                else acc_ref[:, s * nn:(s + 1) * nn]).astype(bf16)

    return pl.pallas_call(
        body, name=name, grid=(1 if whole else N_CHIPS, nt), in_specs=[a_spec, b_spec], out_specs=out_spec,
        out_shape=jax.ShapeDtypeStruct(out_block, bf16),
        scratch_shapes=[pltpu.VMEM((K, N), f32)],
        compiler_params=_params(("parallel", "arbitrary")),
    )(a, b)


def _adamw_sum(place, w, m, v, part, from_chips, from_sibling, name):
    R, C = w.shape
    half = R // 2
    tr = _tile(half, max(16, (1 << 18) // C // 16 * 16), 16)
    nr = half // tr

    def body(p_ref, w_ref, m_ref, v_ref, part_ref, fc_ref, fs_ref, g_ref, d_ref, mo_ref, vo_ref):
        up = lambda x: x.astype(f32)
        mine = ((up(part_ref[0]) + up(fc_ref[0])) + up(fc_ref[1])) + up(fc_ref[2])
        sibs = ((up(fs_ref[0]) + up(fs_ref[1])) + up(fs_ref[2])) + up(fs_ref[3])
        g_ = jnp.where(pl.program_id(0) == p_ref[0], mine, sibs)
        m_ = ADAM_B1 * m_ref[...] + (1.0 - ADAM_B1) * g_
        v_ = ADAM_B2 * v_ref[...] + (1.0 - ADAM_B2) * (g_ * g_)
        m_hat = m_ / (1.0 - ADAM_B1 ** ADAM_STEP)
        v_hat = v_ / (1.0 - ADAM_B2 ** ADAM_STEP)
        g_ref[...] = g_
        d_ref[...] = -ADAM_LR * (m_hat / (jnp.sqrt(v_hat) + ADAM_EPS) + ADAM_WD * w_ref[...])
        mo_ref[...] = m_
        vo_ref[...] = v_

    spec = pl.BlockSpec((tr, C), lambda h, r, p: (h * nr + r, 0))
    return pl.pallas_call(
        body, name=name,
        grid_spec=pltpu.PrefetchScalarGridSpec(
            num_scalar_prefetch=1, grid=(2, nr),
            in_specs=[spec, spec, spec, pl.BlockSpec((1, tr, C), lambda h, r, p: (p[1], r, 0)),
                      pl.BlockSpec((3, tr, C), lambda h, r, p: (0, r, 0)),
                      pl.BlockSpec((4, tr, C), lambda h, r, p: (0, r, 0))],
            out_specs=[spec] * 4),
        out_shape=[jax.ShapeDtypeStruct((R, C), f32)] * 4,
        compiler_params=_params(("parallel", "parallel")),
    )(place, w, m, v, part, from_chips, from_sibling)


def _adamw(w, g, m, v, name):
    R, C = w.shape
    tr = _tile(R, max(8, (1 << 18) // C // 8 * 8), 8)

    def body(w_ref, g_ref, m_ref, v_ref, d_ref, mo_ref, vo_ref):
        g_ = g_ref[...]
        m_ = ADAM_B1 * m_ref[...] + (1.0 - ADAM_B1) * g_
        v_ = ADAM_B2 * v_ref[...] + (1.0 - ADAM_B2) * (g_ * g_)
        m_hat = m_ / (1.0 - ADAM_B1 ** ADAM_STEP)
        v_hat = v_ / (1.0 - ADAM_B2 ** ADAM_STEP)
        d_ref[...] = -ADAM_LR * (m_hat / (jnp.sqrt(v_hat) + ADAM_EPS) + ADAM_WD * w_ref[...])
        mo_ref[...] = m_
        vo_ref[...] = v_

    spec = pl.BlockSpec((tr, C), lambda i: (i, 0))
    return pl.pallas_call(
        body, name=name, grid=(R // tr,), in_specs=[spec] * 4, out_specs=[spec] * 3,
        out_shape=[jax.ShapeDtypeStruct((R, C), f32)] * 3,
        compiler_params=_params(("parallel",)),
    )(w, g, m, v)


def _place():
    return lax.axis_index("x"), lax.axis_index("y"), lax.axis_index("c")


def _other_chips(x, y):
    chips = [(1 - x, y), (x, 1 - y), (1 - x, 1 - y)]
    return chips, [2 * cx + cy for cx, cy in chips]


def _exchange_small(blk, name, reduce):
    R, C = blk.shape

    def body(x_ref, out_ref, *rest):
        if reduce:
            all_ref, send_sems, recv_sems = rest
        else:
            all_ref = out_ref
            send_sems, recv_sems = rest
        x, y, c = _place()
        me = 4 * x + 2 * y + c
        all_ref[me] = x_ref[...]
        copies = []
        for k in range(1, N_DEV):
            peer = tuple(1 - p if (k >> s) & 1 else p for p, s in ((x, 2), (y, 1), (c, 0)))
            cp = pltpu.make_async_remote_copy(src_ref=x_ref, dst_ref=all_ref.at[me], send_sem=send_sems.at[k - 1],
                                              recv_sem=recv_sems.at[k - 1], device_id=peer, device_id_type=MESH)
            cp.start()
            copies.append(cp)
        for cp in copies:
            cp.wait()
        if reduce:
            tot = all_ref[0]
            for d in range(1, N_DEV):
                tot = tot + all_ref[d]
            out_ref[...] = tot

    vm = pl.BlockSpec(memory_space=pltpu.VMEM)
    scratch = [pltpu.SemaphoreType.DMA((N_DEV - 1,)), pltpu.SemaphoreType.DMA((N_DEV - 1,))]
    if reduce:
        scratch = [pltpu.VMEM((N_DEV, R, C), f32)] + scratch
    return pl.pallas_call(
        body, name=name, in_specs=[vm], out_specs=vm,
        out_shape=jax.ShapeDtypeStruct((R, C) if reduce else (N_DEV, R, C), f32),
        scratch_shapes=scratch,
    )(blk)


def _cast_shard(place, w, name):
    R, C = w.shape
    tr = _tile(R, max(16, (1 << 19) // C // 16 * 16), 16)

    def body(p_ref, w_ref, o_ref):
        o_ref[0] = w_ref[...].astype(bf16)

    return pl.pallas_call(
        body, name=name,
        grid_spec=pltpu.PrefetchScalarGridSpec(
            num_scalar_prefetch=1, grid=(R // tr,),
            in_specs=[pl.BlockSpec((tr, C), lambda r, p: (r, 0))],
            out_specs=pl.BlockSpec((1, tr, C), lambda r, p: (p[1], r, 0))),
        out_shape=jax.ShapeDtypeStruct((N_CHIPS, R, C), bf16),
        compiler_params=_params(("parallel",)),
    )(place, w)


class _Phase:
    def __init__(self, arrays, out_shapes, aliases, n_copies, copies, arrivals):
        self.arrays, self.out_shapes, self.aliases = list(arrays), list(out_shapes), dict(aliases)
        self.n_copies, self.copies, self.arrivals = n_copies, copies, arrivals

    def sems(self):
        return [pltpu.SemaphoreType.DMA((self.n_copies,)), pltpu.SemaphoreType.DMA((self.n_copies,))]

    def _descriptors(self, pin, pout, send_sems, recv_sems):
        return [pltpu.make_async_remote_copy(src_ref=s, dst_ref=d, send_sem=send_sems.at[i], recv_sem=recv_sems.at[i],
                                             device_id=to, device_id_type=MESH)
                for i, (s, d, to) in enumerate(self.copies(pin, pout))]

    def start(self, pin, pout, send_sems, recv_sems):
        for cp in self._descriptors(pin, pout, send_sems, recv_sems):
            cp.start()

    def finish(self, pin, pout, send_sems, recv_sems):
        x, y, c = _place()
        sent = self._descriptors(pin, pout, send_sems, recv_sems)
        for i, dst in enumerate(self.arrivals(pin, pout)):
            pltpu.make_async_remote_copy(src_ref=dst, dst_ref=dst, send_sem=send_sems.at[i], recv_sem=recv_sems.at[i],
                                         device_id=(x, y, c), device_id_type=MESH).wait_recv()
        for cp in sent:
            cp.wait_send()


def _call(body, phase, *, name, grid, in_specs, out_specs, out_shape, scratch_shapes, args):
    seq = _params(("arbitrary",) * len(grid))
    if phase is None:
        res = pl.pallas_call(body, name=name, grid=grid, in_specs=in_specs, out_specs=out_specs, out_shape=out_shape,
                             scratch_shapes=scratch_shapes, compiler_params=seq)(*args)
        return list(res), []
    ni, no, ns = len(in_specs), len(out_specs), len(scratch_shapes)
    pi, po = len(phase.arrays), len(phase.out_shapes)

    def hosted(*refs):
        cut = np.cumsum([ni, pi, no, po, ns])
        ins, pin, outs, pout, scr, sems = (refs[a:b] for a, b in zip([0, *cut], [*cut, len(refs)]))
        ids = [pl.program_id(d) for d in range(len(grid))]
        first = functools.reduce(lambda p, q: p & q, [i == 0 for i in ids])
        last = functools.reduce(lambda p, q: p & q, [i == g - 1 for i, g in zip(ids, grid)])
        pl.when(first)(lambda: phase.start(pin, pout, *sems))
        body(*ins, *outs, *scr)
        pl.when(last)(lambda: phase.finish(pin, pout, *sems))

    anyspace = pl.BlockSpec(memory_space=pl.ANY)
    res = pl.pallas_call(
        hosted, name=name, grid=grid, in_specs=list(in_specs) + [anyspace] * pi,
        out_specs=list(out_specs) + [anyspace] * po, out_shape=list(out_shape) + phase.out_shapes,
        input_output_aliases={ni + i: no + j for i, j in phase.aliases.items()},
        scratch_shapes=list(scratch_shapes) + phase.sems(), compiler_params=seq,
    )(*args, *phase.arrays)
    return list(res[:no]), list(res[no:])


def _run_phases(name, phases):
    first = phases[0]
    pi, po = len(first.arrays), len(first.out_shapes)

    def body(*refs):
        pin, pout, sems = refs[:pi], refs[pi:pi + po], refs[pi + po:]
        for n, ph in enumerate(phases):
            ph.start(pin, pout, *sems[2 * n:2 * n + 2])
            ph.finish(pin, pout, *sems[2 * n:2 * n + 2])

    anyspace = pl.BlockSpec(memory_space=pl.ANY)
    return list(pl.pallas_call(
        body, name=name, in_specs=[anyspace] * pi, out_specs=[anyspace] * po, out_shape=first.out_shapes,
        input_output_aliases=first.aliases, scratch_shapes=[s for ph in phases for s in ph.sems()],
    )(*first.arrays))


def _half_rows(buf, c):
    half = buf.shape[1] // 2
    return pl.ds(c * half, half), pl.ds((1 - c) * half, half)


def _gather_phase(bufs, over_ici):
    n = len(bufs)
    shapes = [jax.ShapeDtypeStruct(b.shape, b.dtype) for b in bufs]

    def landed(out, which):
        x, y, c = _place()
        _, ks = _other_chips(x, y)
        return [out[a].at[ks[j], _half_rows(bufs[a], c)[which]] for a in range(n) for j in range(3)]

    def ici(pin, out):
        x, y, c = _place()
        chips, _ = _other_chips(x, y)
        mine = [out[a].at[2 * x + y, _half_rows(bufs[a], c)[0]] for a in range(n)]
        return [(mine[a], mine[a], (*chips[j], c)) for a in range(n) for j in range(3)]

    def d2d(pin, out):
        x, y, c = _place()
        return [(dst, dst, (x, y, 1 - c)) for dst in landed(out, 0)]

    if over_ici:
        return _Phase(bufs, shapes, {a: a for a in range(n)}, 3 * n, ici, lambda pin, out: landed(out, 0))
    return _Phase(bufs, shapes, {a: a for a in range(n)}, 3 * n, d2d, lambda pin, out: landed(out, 1))


def _rs_sibling(grads, name):
    n = len(grads)

    def body(*refs):
        g, out, send_sems, recv_sems = refs[:n], refs[n:2 * n], refs[2 * n], refs[2 * n + 1]
        x, y, c = _place()
        copies = []
        for a in range(n):
            half = grads[a].shape[1] // 2
            cp = pltpu.make_async_remote_copy(src_ref=g[a].at[:, pl.ds((1 - c) * half, half)], dst_ref=out[a],
                                              send_sem=send_sems.at[a], recv_sem=recv_sems.at[a],
                                              device_id=(x, y, 1 - c), device_id_type=MESH)
            cp.start()
            copies.append(cp)
        for cp in copies:
            cp.wait()

    anyspace = pl.BlockSpec(memory_space=pl.ANY)
    return pl.pallas_call(
        body, name=name, in_specs=[anyspace] * n, out_specs=[anyspace] * n,
        out_shape=[jax.ShapeDtypeStruct((N_CHIPS, g.shape[1] // 2, g.shape[2]), g.dtype) for g in grads],
        scratch_shapes=[pltpu.SemaphoreType.DMA((n,)), pltpu.SemaphoreType.DMA((n,))],
    )(*grads)


def _rs_add_sibling(place, grad, got, name):
    _, R, C = grad.shape
    half = R // 2
    tr = _tile(half, max(16, (1 << 19) // C // 16 * 16), 16)
    nr = half // tr

    def body(p_ref, a_ref, b_ref, o_ref):
        o_ref[...] = (a_ref[...].astype(f32) + b_ref[...].astype(f32)).astype(o_ref.dtype)

    return pl.pallas_call(
        body, name=name,
        grid_spec=pltpu.PrefetchScalarGridSpec(
            num_scalar_prefetch=1, grid=(N_CHIPS, nr),
            in_specs=[pl.BlockSpec((1, tr, C), lambda k, r, p: (k, p[0] * nr + r, 0)),
                      pl.BlockSpec((1, tr, C), lambda k, r, p: (k, r, 0))],
            out_specs=pl.BlockSpec((1, tr, C), lambda k, r, p: (k, r, 0))),
        out_shape=jax.ShapeDtypeStruct((N_CHIPS, half, C), bf16),
        compiler_params=_params(("parallel", "parallel")),
    )(place, grad, got)


def _rs_chips_phase(parts):
    n = len(parts)

    def copies(p, fc):
        x, y, c = _place()
        chips, ks = _other_chips(x, y)
        return [(p[a].at[ks[j]], fc[a].at[j], (*chips[j], c)) for a in range(n) for j in range(3)]

    shapes = [jax.ShapeDtypeStruct((3,) + q.shape[1:], q.dtype) for q in parts]
    return _Phase(parts, shapes, {}, 3 * n, copies, lambda p, fc: [fc[a].at[j] for a in range(n) for j in range(3)])


def _rs_hand_phase(parts, from_chips):
    n = len(parts)

    def copies(pin, fs):
        x, y, c = _place()
        sib = (x, y, 1 - c)
        own = [(pin[a].at[2 * x + y], fs[a].at[0], sib) for a in range(n)]
        return own + [(pin[n + a].at[j], fs[a].at[1 + j], sib) for a in range(n) for j in range(3)]

    def arrivals(pin, fs):
        return [fs[a].at[0] for a in range(n)] + [fs[a].at[1 + j] for a in range(n) for j in range(3)]

    shapes = [jax.ShapeDtypeStruct((4,) + q.shape[1:], q.dtype) for q in parts]
    return _Phase(list(parts) + list(from_chips), shapes, {}, 4 * n, copies, arrivals)


class _Exchange:
    def __init__(self, place):
        self.place = place

    def gather(self, bufs, over_ici):
        return _gather_phase(bufs, over_ici)

    def pair_sums(self, names, grads):
        got = _rs_sibling(grads, "rs_sibling_" + names[0])
        return [_rs_add_sibling(self.place, g, r, "rs_add_" + n) for n, g, r in zip(names, grads, got)]

    def to_chips(self, parts):
        return _rs_chips_phase(parts)

    def to_sibling(self, parts, from_chips):
        return _rs_hand_phase(parts, from_chips)

    def hand_over(self, name, parts, from_chips):
        return _run_phases(name, [_rs_hand_phase(parts, from_chips)])


def _local_step(x, target, norm_mix, b_gate, rb_full, norm_ffn, norm_final, w_in, rest, exch):
    B, S, D = x.shape
    T = B * S
    x2 = x.reshape(T, D)
    tg2 = target.reshape(T, D)
    rope, decay = _rope_tables(S), _decay_tables()
    trows = _bias_rows(rb_full)
    g_fin = norm_final.reshape(1, D)

    xn, proj = _in_proj(x2, norm_mix, w_in)
    bufs = list(rest.values())
    (qr, kr, o, u, states), got = _ret_fwd(proj, B, S, rope, decay, exch.gather(bufs, True))
    bufs = got or bufs
    (ao,), got = _att_fwd(proj, trows, B, S, exch.gather(bufs, False))
    wb = dict(zip(rest, got or bufs))
    w_ro, w_out = wb["w_ret_out"].reshape(-1, D), wb["w_out"].reshape(-1, D)
    h1, yr, ya = _mix_fwd(x2, proj, u, ao, b_gate, w_ro, wb["w_att_out"], w_out)
    hn, a, b, f, dh2, part_fin = _ffn_fwd(h1, norm_ffn, wb["w_ffn_gate"], wb["w_ffn_up"], wb["w_ffn_down"], g_fin, tg2)

    da, db, dh1, part_ffn = _ffn_bwd(dh2, h1, norm_ffn, a, b, wb["w_ffn_gate"], wb["w_ffn_up"], wb["w_ffn_down"])
    du, dao, dgl, mix, dyr, dya, part_bg = _mix_bwd(dh1, proj, yr, ya, b_gate, w_ro, wb["w_att_out"], w_out)
    first = {
        "w_ffn_down": _wgrad(f, dh2, 0, "wgrad_ffn_down"),
        "w_ffn_gate": _wgrad(hn, da, 1, "wgrad_ffn_gate"),
        "w_ffn_up": _wgrad(hn, db, 1, "wgrad_ffn_up"),
        "w_out": _wgrad(mix, dh1, 0, "wgrad_out"),
        "w_ret_out": _wgrad(u, dyr, 0, "wgrad_ret_out"),
        "w_att_out": _wgrad(ao, dya, 1, "wgrad_att_out"),
    }
    parts = exch.pair_sums(list(first), list(first.values()))
    (drq, drk, drv, drg), from_chips = _ret_bwd(proj, qr, kr, o, states, du, B, S, rope, decay, exch.to_chips(parts))
    (daq, dak, dav, dvec), from_sib = _att_bwd(proj, dao, trows, B, S, exch.to_sibling(parts, from_chips))
    dproj = jnp.concatenate([drq, drk, drv, drg, daq, dak, dav, dgl], axis=1)
    part_in = exch.pair_sums(["w_in"], [_wgrad(xn, dproj, 1, "wgrad_in")])
    (gx, part_mix), chips_in = _in_proj_bwd(dproj, w_in, x2, norm_mix, dh1, exch.to_chips(part_in))
    sib_in = exch.hand_over("rs_hand_w_in", part_in, chips_in)
    gbig = dict(zip(first, zip(parts, from_chips, from_sib)))
    gbig["w_in"] = (part_in[0], chips_in[0], sib_in[0])
    rows = lambda p, r: p.reshape(-1, 8, p.shape[-1])[:, r, :].sum(axis=0)
    lo = KWIN - 1 - (MAX_REL - 1)
    drb = jnp.concatenate([jnp.flip(dvec[:, lo:lo + N_REL - 1], axis=1), dvec[:, :lo].sum(axis=1, keepdims=True)], axis=1)
    gsmall = {
        "norm_mix": rows(part_mix, 0), "b_gate": rows(part_bg, 0), "rel_bias": drb, "norm_ffn": rows(part_ffn, 0),
        "norm_final": rows(part_fin, 0),
    }
    return rows(part_fin, 1), gx.reshape(B, S, D), gbig, gsmall


SMALL_ROWS = 16


def _pack_small(gs, loss_lanes):
    D = D_MODEL
    rb = jnp.pad(gs["rel_bias"].reshape(-1), (0, 3 * D - ATT_HEADS * N_REL)).reshape(3, D)
    rows = [gs["norm_mix"].reshape(1, D), gs["b_gate"].reshape(2, D), gs["norm_ffn"].reshape(1, D),
            gs["norm_final"].reshape(1, D), rb, loss_lanes.reshape(1, D)]
    used = sum(r.shape[0] for r in rows)
    return jnp.concatenate(rows + [jnp.zeros((SMALL_ROWS - used, D), f32)], axis=0)


def kernel(x, norm_mix, w_in, b_gate, rel_bias, w_ret_out, w_att_out, w_out, norm_ffn, w_ffn_gate, w_ffn_up, w_ffn_down, norm_final, loss_target, m_norm_mix, m_w_in, m_b_gate, m_rel_bias, m_w_ret_out, m_w_att_out, m_w_out, m_norm_ffn, m_w_ffn_gate, m_w_ffn_up, m_w_ffn_down, m_norm_final, v_norm_mix, v_w_in, v_b_gate, v_rel_bias, v_w_ret_out, v_w_att_out, v_w_out, v_norm_ffn, v_w_ffn_gate, v_w_ffn_up, v_w_ffn_down, v_norm_final):
    w = dict(norm_mix=norm_mix, w_in=w_in, b_gate=b_gate, rel_bias=rel_bias, w_ret_out=w_ret_out, w_att_out=w_att_out,
             w_out=w_out, norm_ffn=norm_ffn, w_ffn_gate=w_ffn_gate, w_ffn_up=w_ffn_up, w_ffn_down=w_ffn_down,
             norm_final=norm_final)
    m = dict(norm_mix=m_norm_mix, w_in=m_w_in, b_gate=m_b_gate, rel_bias=m_rel_bias, w_ret_out=m_w_ret_out,
             w_att_out=m_w_att_out, w_out=m_w_out, norm_ffn=m_norm_ffn, w_ffn_gate=m_w_ffn_gate, w_ffn_up=m_w_ffn_up,
             w_ffn_down=m_w_ffn_down, norm_final=m_norm_final)
    v = dict(norm_mix=v_norm_mix, w_in=v_w_in, b_gate=v_b_gate, rel_bias=v_rel_bias, w_ret_out=v_w_ret_out,
             w_att_out=v_w_att_out, w_out=v_w_out, norm_ffn=v_norm_ffn, w_ffn_gate=v_w_ffn_gate, w_ffn_up=v_w_ffn_up,
             w_ffn_down=v_w_ffn_down, norm_final=v_norm_final)
    xi, yi, ci = _place()
    k_me = 2 * xi + yi

    place = jnp.stack([ci, k_me]).astype(jnp.int32)
    big = [n for n, _ in BIG]

    bufs = {n: _cast_shard(place, w[n][0], "cast_" + n) for n in big}
    w_in_all, = _run_phases("gather_w_in", [_gather_phase([bufs["w_in"]], True), _gather_phase([bufs["w_in"]], False)])
    rest = {n: bufs[n] for n in big if n != "w_in"}
    nrel_loc = rel_bias.shape[-1]
    rb_all = _exchange_small(jnp.pad(rel_bias[0], ((0, 0), (0, 128 - nrel_loc))), "gather_rel_bias", False)
    rb_full = jnp.concatenate([rb_all[2 * k, :, :nrel_loc] for k in range(N_CHIPS)], axis=1)

    loss_lanes, grad_x, gbig, gsmall = _local_step(x, loss_target, norm_mix, b_gate, rb_full, norm_ffn, norm_final,
                                                   w_in_all, rest, _Exchange(place))

    small = _exchange_small(_pack_small(gsmall, loss_lanes), "reduce_small", True)
    D = D_MODEL
    loss = jnp.sum(small[8])
    drb_full = small[5:8].reshape(-1)[:ATT_HEADS * N_REL].reshape(ATT_HEADS, N_REL)
    g = {
        "norm_mix": small[0:1], "b_gate": small[1:3].reshape(1, 2 * D), "norm_ffn": small[3:4], "norm_final": small[4],
        "rel_bias": lax.dynamic_slice_in_dim(drb_full, k_me * nrel_loc, nrel_loc, axis=1)[None],
    }

    delta, new_m, new_v = {}, {}, {}
    for n in big:
        g_, d_, m_, v_ = _adamw_sum(place, w[n][0], m[n][0], v[n][0], *gbig[n], "adamw_" + n)
        g[n], delta[n], new_m[n], new_v[n] = g_[None], d_[None], m_[None], v_[None]
    flat = lambda d: jnp.concatenate([d[n].reshape(-1) for n in SMALL])
    n_small = sum(int(np.prod(w[n].shape)) for n in SMALL)
    n_pad = -n_small % 1024
    packs = [jnp.pad(flat(d), (0, n_pad)).reshape(-1, 128) for d in (w, g, m, v)]
    outs = _adamw(*packs, "adamw_small")
    for res, dst in zip(outs, (delta, new_m, new_v)):
        off = 0
        fl = res.reshape(-1)
        for n in SMALL:
            sz = int(np.prod(w[n].shape))
            dst[n] = fl[off:off + sz].reshape(w[n].shape)
            off += sz

    return (loss, grad_x, *[g[n] for n in WEIGHTS], *[delta[n] for n in WEIGHTS], *[new_m[n] for n in WEIGHTS],
            *[new_v[n] for n in WEIGHTS])
```

```python
import functools

import numpy as np
import jax
import jax.numpy as jnp
from jax import lax
from jax.experimental import pallas as pl
from jax.experimental.pallas import tpu as pltpu

f32 = jnp.float32
bf16 = jnp.bfloat16

D_MODEL = 1024
CHUNK = 64
RET_HEADS = 4
RET_KEY_DIM = 128
RET_VAL_DIM = 256
ATT_HEADS = 8
ATT_HEAD_DIM = 64
ATT_W = ATT_HEADS * ATT_HEAD_DIM
BAND_CHUNKS = 8
PAD = BAND_CHUNKS * CHUNK
MAX_REL = 256
N_REL = CHUNK + MAX_REL
D_FF = 2816
N_IN = 6656
ROPE_BASE = 10000.0
EPS = 1e-6
NEG_INF = -1e30
C_RQ, C_RK, C_RV, C_RG, C_AQ, C_AK, C_AV, C_GL = 0, 512, 1024, 2048, 3072, 3584, 4096, 4608

ADAM_LR, ADAM_B1, ADAM_B2, ADAM_EPS, ADAM_WD, ADAM_STEP = 0.001, 0.9, 0.999, 1e-08, 0.01, 10

N_CHIPS = 4
N_DEV = 8
WGRAD_ACC_BYTES = 8 * 1024 * 1024
QBLK = 256
KWIN = PAD + QBLK
TOEP = 1024
VMEM_LIMIT = 56 * 1024 * 1024
MESH = pl.DeviceIdType.MESH

BIG = (
    ("w_in", 1), ("w_ret_out", 0), ("w_att_out", 1), ("w_out", 0), ("w_ffn_gate", 1), ("w_ffn_up", 1), ("w_ffn_down", 0))
WEIGHTS = ("norm_mix", "w_in", "b_gate", "rel_bias", "w_ret_out", "w_att_out", "w_out", "norm_ffn", "w_ffn_gate",
           "w_ffn_up", "w_ffn_down", "norm_final")
SMALL = ("norm_mix", "b_gate", "rel_bias", "norm_ffn", "norm_final")


def _dot(a, b):
    return lax.dot_general(a, b, (((1,), (0,)), ((), ())), preferred_element_type=f32)


def _dot_nt(a, b):
    return lax.dot_general(a, b, (((1,), (1,)), ((), ())), preferred_element_type=f32)


def _dot_tn(a, b):
    return lax.dot_general(a, b, (((0,), (0,)), ((), ())), preferred_element_type=f32)


def _sig(x):
    return 1.0 / (1.0 + jnp.exp(-x))


def _tile(n, pref, mult):
    best = None
    for t in range(mult, min(n, pref) + 1, mult):
        if n % t == 0:
            best = t
    return best if best is not None else n


def _params(sem, vmem=VMEM_LIMIT):
    return pltpu.CompilerParams(dimension_semantics=sem, vmem_limit_bytes=vmem)


def _in_proj(x2, gamma, w):
    T, D = x2.shape
    ns = w.shape[2]
    tm = _tile(T, 512, 8)

    def body(x_ref, g_ref, w_ref, xn_ref, p_ref, xs_ref):
        @pl.when(pl.program_id(1) == 0)
        def _():
            x = x_ref[...]
            r = lax.rsqrt(jnp.mean(x * x, axis=-1, keepdims=True) + EPS)
            xn = (x * r * g_ref[...]).astype(bf16)
            xs_ref[...] = xn
            xn_ref[...] = xn

        p_ref[...] = _dot(xs_ref[...], w_ref[0]).astype(bf16)

    return pl.pallas_call(
        body, name="in_proj", grid=(T // tm, N_CHIPS),
        in_specs=[pl.BlockSpec((tm, D), lambda i, j: (i, 0)), pl.BlockSpec((1, D), lambda i, j: (0, 0)),
                  pl.BlockSpec((1, D, ns), lambda i, j: (j, 0, 0))],
        out_specs=[pl.BlockSpec((tm, D), lambda i, j: (i, 0)), pl.BlockSpec((tm, ns), lambda i, j: (i, j))],
        out_shape=[jax.ShapeDtypeStruct((T, D), bf16), jax.ShapeDtypeStruct((T, N_CHIPS * ns), bf16)],
        scratch_shapes=[pltpu.VMEM((tm, D), bf16)],
        compiler_params=_params(("parallel", "arbitrary")),
    )(x2, gamma, w)


def _rope_tables(S):
    d = RET_KEY_DIM
    freqs = ROPE_BASE ** (-jnp.arange(0, d, 2, dtype=f32) / d)
    ang = jnp.arange(S, dtype=f32)[:, None] * freqs[None, :]
    cos, sin = jnp.cos(ang), jnp.sin(ang)
    return jnp.concatenate([cos, cos], axis=1), jnp.concatenate([-sin, sin], axis=1)


def _decay_tables():
    H = RET_HEADS
    log_g = jnp.log(1.0 - 2.0 ** (-5.0 - jnp.arange(H, dtype=f32)))
    p = jnp.arange(CHUNK, dtype=f32)
    intra = jnp.exp(log_g[:, None, None] * jnp.abs(p[:, None] - p[None, :]))
    q_dec = jnp.exp(log_g[:, None] * (p[None, :] + 1.0))
    k_dec = jnp.exp(log_g[:, None] * (CHUNK - 1.0 - p[None, :]))
    c_dec = jnp.exp(log_g * CHUNK)
    q_dec = jnp.broadcast_to(q_dec[:, :, None], (H, CHUNK, RET_KEY_DIM))
    k_dec = jnp.broadcast_to(k_dec[:, :, None], (H, CHUNK, RET_KEY_DIM))
    c_dec = jnp.broadcast_to(c_dec[:, None, None], (H, 1, RET_VAL_DIM))
    return intra, q_dec, k_dec, c_dec


K_SCALE = RET_KEY_DIM ** -0.5


RET_CHUNKS = 4


def _ret_tables_specs():
    whole = lambda *shape: pl.BlockSpec(shape, lambda b, i: (0,) * len(shape))
    return [whole(RET_HEADS, CHUNK, CHUNK), whole(RET_HEADS, CHUNK, RET_KEY_DIM), whole(RET_HEADS, CHUNK, RET_KEY_DIM),
            whole(RET_HEADS, 1, RET_VAL_DIM)]


def _rotate(x, cos, sn):
    return x * cos + pltpu.roll(x, RET_KEY_DIM // 2, 1) * sn


def _ret_fwd(proj, B, S, rope, decay, phase=None):
    T = B * S
    nc = S // CHUNK
    H, dk, dv = RET_HEADS, RET_KEY_DIM, RET_VAL_DIM
    sb = RET_CHUNKS * CHUNK
    ns = S // sb

    def body(q_ref, k_ref, v_ref, g_ref, cos_ref, sin_ref, intra_ref, qd_ref, kd_ref, cd_ref,
             qr_ref, kr_ref, o_ref, u_ref, st_ref, state_ref):
        @pl.when(pl.program_id(1) == 0)
        def _():
            state_ref[...] = jnp.zeros_like(state_ref)

        cos, sn = cos_ref[...], sin_ref[...]
        for h in range(H):
            hs = slice(h * dk, (h + 1) * dk)
            qr_ref[:, hs] = _rotate(q_ref[:, hs].astype(f32), cos, sn).astype(bf16)
            kr_ref[:, hs] = (_rotate(k_ref[:, hs].astype(f32), cos, sn) * K_SCALE).astype(bf16)
        states = [state_ref[h] for h in range(H)]
        for ci in range(RET_CHUNKS):
            r = slice(ci * CHUNK, (ci + 1) * CHUNK)
            for h in range(H):
                hk, hv = slice(h * dk, (h + 1) * dk), slice(h * dv, (h + 1) * dv)
                qi, ki, vi = qr_ref[r, hk], kr_ref[r, hk], v_ref[r, hv]
                stb = states[h].astype(bf16)
                st_ref[0, h, ci] = stb
                s = (_dot_nt(qi, ki) * intra_ref[h]).astype(bf16)
                o = _dot(s, vi) + _dot((qi.astype(f32) * qd_ref[h]).astype(bf16), stb)
                states[h] = states[h] * cd_ref[h] + _dot_tn((ki.astype(f32) * kd_ref[h]).astype(bf16), vi)
                mu = jnp.mean(o, axis=-1, keepdims=True)
                xc = o - mu
                var = jnp.mean(xc * xc, axis=-1, keepdims=True)
                oh = xc * lax.rsqrt(var + EPS)
                g = g_ref[r, hv].astype(f32)
                o_ref[r, hv] = o.astype(bf16)
                u_ref[r, hv] = (g * _sig(g) * oh).astype(bf16)
        for h in range(H):
            state_ref[h] = states[h]

    blk = lambda w, c: pl.BlockSpec((sb, w), lambda b, i: (b * ns + i, c))
    return _call(
        body, phase, name="ret_fwd", grid=(B, ns), scratch_shapes=[pltpu.VMEM((H, dk, dv), f32)],
        in_specs=[blk(H * dk, C_RQ // (H * dk)), blk(H * dk, C_RK // (H * dk)), blk(H * dv, C_RV // (H * dv)),
                  blk(H * dv, C_RG // (H * dv)),
                  pl.BlockSpec((sb, dk), lambda b, i: (i, 0)), pl.BlockSpec((sb, dk), lambda b, i: (i, 0)),
                  *_ret_tables_specs()],
        out_specs=[blk(H * dk, 0), blk(H * dk, 0), blk(H * dv, 0), blk(H * dv, 0),
                   pl.BlockSpec((1, H, RET_CHUNKS, dk, dv), lambda b, i: (b, 0, i, 0, 0))],
        out_shape=[jax.ShapeDtypeStruct((T, H * dk), bf16), jax.ShapeDtypeStruct((T, H * dk), bf16),
                   jax.ShapeDtypeStruct((T, H * dv), bf16), jax.ShapeDtypeStruct((T, H * dv), bf16),
                   jax.ShapeDtypeStruct((B, H, nc, dk, dv), bf16)],
        args=(proj, proj, proj, proj, *rope, *decay))


def _bias_rows(rb):
    last = rb[:, N_REL - 1:]
    return jnp.concatenate([
        jnp.broadcast_to(last, (ATT_HEADS, PAD - MAX_REL + 1)),
        jnp.flip(rb[:, :N_REL - 1], axis=1),
        jnp.broadcast_to(rb[:, :1], (ATT_HEADS, KWIN - PAD - CHUNK)),
        jnp.broadcast_to(last, (ATT_HEADS, TOEP - KWIN)),
    ], axis=1)


def _build_bias(t_ref, bias_ref):
    row = lax.broadcasted_iota(jnp.int32, (QBLK, KWIN), 0) // CHUNK
    col = lax.broadcasted_iota(jnp.int32, (QBLK, KWIN), 1) // CHUNK
    delta = BAND_CHUNKS + row - col
    vis = (delta >= 0) & (delta <= BAND_CHUNKS)
    for h in range(ATT_HEADS):
        t = jnp.broadcast_to(t_ref[h:h + 1, :], (QBLK, TOEP))
        rolled = pltpu.roll(t, 0, 1, stride=1, stride_axis=0)
        bias_ref[h] = jnp.where(vis, rolled[:, :KWIN], NEG_INF)


def _att_probs(qh, kh, bias):
    s = _dot_nt(qh, kh) * (ATT_HEAD_DIM ** -0.5) + bias
    m = jnp.max(s, axis=-1, keepdims=True)
    p = jnp.exp(s - m)
    return p * (1.0 / jnp.sum(p, axis=-1, keepdims=True))


def _by_window(i, step):
    sizes = list(range(QBLK, KWIN, QBLK))
    for n, nk in enumerate(sizes):
        pl.when(i == n)(functools.partial(step, nk))
    pl.when(i >= len(sizes))(functools.partial(step, KWIN))


def _att_fwd(proj, trows, B, S, phase=None):
    T = B * S
    nq = S // QBLK
    dh = ATT_HEAD_DIM

    def body(q_ref, k_ref, v_ref, t_ref, o_ref, bias_ref):
        i = pl.program_id(1)

        @pl.when((pl.program_id(0) == 0) & (i == 0))
        def _():
            _build_bias(t_ref, bias_ref)

        def step(nk):
            win = pl.ds(pl.multiple_of((i + 1) * QBLK - nk, QBLK), nk)
            kw, vw = k_ref[win, :], v_ref[win, :]
            outs = []
            for h in range(ATT_HEADS):
                hs = slice(h * dh, (h + 1) * dh)
                pr = _att_probs(q_ref[:, hs], kw[:, hs], bias_ref[h, :, KWIN - nk:])
                outs.append(_dot(pr.astype(bf16), vw[:, hs]))
            o_ref[...] = jnp.concatenate(outs, axis=1).astype(bf16)

        _by_window(i, step)

    return _call(
        body, phase, name="att_fwd", grid=(B, nq),
        in_specs=[pl.BlockSpec((QBLK, ATT_W), lambda b, i: (b * nq + i, C_AQ // ATT_W)),
                  pl.BlockSpec((S, ATT_W), lambda b, i: (b, C_AK // ATT_W)),
                  pl.BlockSpec((S, ATT_W), lambda b, i: (b, C_AV // ATT_W)),
                  pl.BlockSpec((ATT_HEADS, TOEP), lambda b, i: (0, 0))],
        out_specs=[pl.BlockSpec((QBLK, ATT_W), lambda b, i: (b * nq + i, 0))],
        out_shape=[jax.ShapeDtypeStruct((T, ATT_W), bf16)],
        scratch_shapes=[pltpu.VMEM((ATT_HEADS, QBLK, KWIN), f32)],
        args=(proj, proj, proj, trows))


def _gl_specs(tm):
    w = 512
    return [pl.BlockSpec((tm, w), functools.partial(lambda i, j: (i, C_GL // 512 + j), j=j)) for j in range(4)]


def _gates(gl_refs, bg_ref):
    gl = jnp.concatenate([r[...] for r in gl_refs], axis=1).astype(f32) + bg_ref[...]
    g = _sig(gl)
    return g[:, :D_MODEL], g[:, D_MODEL:]


def _mix_fwd(x2, proj, u, ao, b_gate, w_ro, w_ao, w_out):
    T, D = x2.shape
    tm = _tile(T, 512, 8)

    def body(x_ref, u_ref, ao_ref, g0, g1, g2, g3, bg_ref, wro_ref, wao_ref, wo_ref, h1_ref, yr_ref, ya_ref):
        yr = _dot(u_ref[...], wro_ref[...])
        ao = ao_ref[...]
        ya = jnp.concatenate([_dot(ao, wao_ref[k]) for k in range(N_CHIPS)], axis=1)
        gr, ga = _gates((g0, g1, g2, g3), bg_ref)
        mix = gr * yr + ga * ya
        h1_ref[...] = x_ref[...] + _dot(mix.astype(bf16), wo_ref[...])
        yr_ref[...] = yr.astype(bf16)
        ya_ref[...] = ya.astype(bf16)

    full = lambda a: pl.BlockSpec(a.shape, lambda i: (0,) * a.ndim)
    row = lambda n: pl.BlockSpec((tm, n), lambda i: (i, 0))
    return pl.pallas_call(
        body, name="mix_fwd", grid=(T // tm,),
        in_specs=[row(D), row(D), row(ATT_W), *_gl_specs(tm), full(b_gate), full(w_ro), full(w_ao), full(w_out)],
        out_specs=[row(D), row(D), row(D)],
        out_shape=[jax.ShapeDtypeStruct((T, D), f32), jax.ShapeDtypeStruct((T, D), bf16),
                   jax.ShapeDtypeStruct((T, D), bf16)],
        compiler_params=_params(("parallel",)),
    )(x2, u, ao, proj, proj, proj, proj, b_gate, w_ro, w_ao, w_out)


def _ffn_fwd(h1, g_ffn, wg, wu, wd, g_fin, target):
    T, D = h1.shape
    nf, _, tf = wg.shape
    tm = _tile(T, 512, 8)

    def body(h1_ref, g_ref, wg_ref, wu_ref, wd_ref, gf_ref, tg_ref, hn_ref, a_ref, b_ref, f_ref, dh2_ref, part_ref,
             hs_ref, acc_ref):
        j = pl.program_id(1)

        @pl.when(j == 0)
        def _():
            h = h1_ref[...]
            r = lax.rsqrt(jnp.mean(h * h, axis=-1, keepdims=True) + EPS)
            hn = (h * r * g_ref[...]).astype(bf16)
            hs_ref[...] = hn
            hn_ref[...] = hn
            acc_ref[...] = jnp.zeros_like(acc_ref)

        hn = hs_ref[...]
        a = _dot(hn, wg_ref[0])
        b = _dot(hn, wu_ref[0])
        f = ((a * _sig(a)) * b).astype(bf16)
        a_ref[0] = a.astype(bf16)
        b_ref[0] = b.astype(bf16)
        f_ref[0] = f
        acc_ref[...] += _dot(f, wd_ref[0])

        @pl.when(j == nf - 1)
        def _():
            h2 = h1_ref[...] + acc_ref[...]
            r = lax.rsqrt(jnp.mean(h2 * h2, axis=-1, keepdims=True) + EPS)
            n = h2 * r
            gf = gf_ref[...]
            e = n * gf - tg_ref[...]
            dy = e * (1.0 / D)
            dn = dy * gf
            dh2_ref[...] = r * (dn - n * jnp.mean(dn * n, axis=-1, keepdims=True))
            part_ref[...] = jnp.zeros_like(part_ref)
            part_ref[0:1, :] = jnp.sum(dy * n, axis=0, keepdims=True)
            part_ref[1:2, :] = (0.5 / D) * jnp.sum(e * e, axis=0, keepdims=True)

    row = lambda n: pl.BlockSpec((tm, n), lambda i, j: (i, 0))
    vec = pl.BlockSpec((1, D), lambda i, j: (0, 0))
    col = pl.BlockSpec((1, tm, tf), lambda i, j: (j, i, 0))
    wcol = pl.BlockSpec((1, D, tf), lambda i, j: (j, 0, 0))
    act = jax.ShapeDtypeStruct((nf, T, tf), bf16)
    return pl.pallas_call(
        body, name="ffn_fwd", grid=(T // tm, nf),
        in_specs=[row(D), vec, wcol, wcol, pl.BlockSpec((1, tf, D), lambda i, j: (j, 0, 0)), vec, row(D)],
        out_specs=[row(D), col, col, col, row(D), pl.BlockSpec((8, D), lambda i, j: (i, 0))],
        out_shape=[jax.ShapeDtypeStruct((T, D), bf16), act, act, act,
                   jax.ShapeDtypeStruct((T, D), f32), jax.ShapeDtypeStruct((T // tm * 8, D), f32)],
        scratch_shapes=[pltpu.VMEM((tm, D), bf16), pltpu.VMEM((tm, D), f32)],
        compiler_params=_params(("parallel", "arbitrary")),
    )(h1, g_ffn, wg, wu, wd, g_fin, target)


def _ffn_bwd(dh2, h1, g_ffn, a, b, wg, wu, wd):
    T, D = h1.shape
    nf, _, tf = wg.shape
    tm = _tile(T, 512, 8)

    def body(dh2_ref, h1_ref, g_ref, a_ref, b_ref, wg_ref, wu_ref, wd_ref, da_ref, db_ref, dh1_ref, part_ref,
             ds_ref, acc_ref):
        j = pl.program_id(1)

        @pl.when(j == 0)
        def _():
            ds_ref[...] = dh2_ref[...].astype(bf16)
            acc_ref[...] = jnp.zeros_like(acc_ref)

        df = _dot_nt(ds_ref[...], wd_ref[0])
        av = a_ref[0].astype(f32)
        sg = _sig(av)
        db = (df * (av * sg)).astype(bf16)
        da = (df * b_ref[0].astype(f32) * (sg * (1.0 + av * (1.0 - sg)))).astype(bf16)
        da_ref[0] = da
        db_ref[0] = db
        acc_ref[...] += _dot_nt(da, wg_ref[0]) + _dot_nt(db, wu_ref[0])

        @pl.when(j == nf - 1)
        def _():
            h = h1_ref[...]
            r = lax.rsqrt(jnp.mean(h * h, axis=-1, keepdims=True) + EPS)
            n = h * r
            dhn = acc_ref[...]
            dn = dhn * g_ref[...]
            dh1_ref[...] = dh2_ref[...] + r * (dn - n * jnp.mean(dn * n, axis=-1, keepdims=True))
            part_ref[...] = jnp.zeros_like(part_ref)
            part_ref[0:1, :] = jnp.sum(dhn * n, axis=0, keepdims=True)

    row = lambda n: pl.BlockSpec((tm, n), lambda i, j: (i, 0))
    col = pl.BlockSpec((1, tm, tf), lambda i, j: (j, i, 0))
    wcol = pl.BlockSpec((1, D, tf), lambda i, j: (j, 0, 0))
    act = jax.ShapeDtypeStruct((nf, T, tf), bf16)
    return pl.pallas_call(
        body, name="ffn_bwd", grid=(T // tm, nf),
        in_specs=[row(D), row(D), pl.BlockSpec((1, D), lambda i, j: (0, 0)), col, col, wcol, wcol,
                  pl.BlockSpec((1, tf, D), lambda i, j: (j, 0, 0))],
        out_specs=[col, col, row(D), pl.BlockSpec((8, D), lambda i, j: (i, 0))],
        out_shape=[act, act, jax.ShapeDtypeStruct((T, D), f32), jax.ShapeDtypeStruct((T // tm * 8, D), f32)],
        scratch_shapes=[pltpu.VMEM((tm, D), bf16), pltpu.VMEM((tm, D), f32)],
        compiler_params=_params(("parallel", "arbitrary")),
    )(dh2, h1, g_ffn, a, b, wg, wu, wd)


def _mix_bwd(dh1, proj, yr, ya, b_gate, w_ro, w_ao, w_out):
    T, D = dh1.shape
    tm = _tile(T, 512, 8)

    def body(dh1_ref, g0, g1, g2, g3, bg_ref, yr_ref, ya_ref, wro_ref, wao_ref, wo_ref,
             du_ref, dao_ref, dgl_ref, mix_ref, dyr_ref, dya_ref, part_ref):
        dmix = _dot_nt(dh1_ref[...].astype(bf16), wo_ref[...])
        gr, ga = _gates((g0, g1, g2, g3), bg_ref)
        yr = yr_ref[...].astype(f32)
        ya = ya_ref[...].astype(f32)
        dyr = (dmix * gr).astype(bf16)
        dya = (dmix * ga).astype(bf16)
        dgl = jnp.concatenate([dmix * yr * gr * (1.0 - gr), dmix * ya * ga * (1.0 - ga)], axis=1)
        du_ref[...] = _dot_nt(dyr, wro_ref[...]).astype(bf16)
        ns = wao_ref.shape[2]
        dao = _dot_nt(dya[:, :ns], wao_ref[0])
        for k in range(1, N_CHIPS):
            dao = dao + _dot_nt(dya[:, k * ns:(k + 1) * ns], wao_ref[k])
        dao_ref[...] = dao.astype(bf16)
        dgl_ref[...] = dgl.astype(bf16)
        mix_ref[...] = (gr * yr + ga * ya).astype(bf16)
        dyr_ref[...] = dyr
        dya_ref[...] = dya
        part_ref[...] = jnp.zeros_like(part_ref)
        part_ref[0:1, :] = jnp.sum(dgl, axis=0, keepdims=True)

    full = lambda a: pl.BlockSpec(a.shape, lambda i: (0,) * a.ndim)
    row = lambda n: pl.BlockSpec((tm, n), lambda i: (i, 0))
    return pl.pallas_call(
        body, name="mix_bwd", grid=(T // tm,),
        in_specs=[row(D), *_gl_specs(tm), full(b_gate), row(D), row(D), full(w_ro), full(w_ao), full(w_out)],
        out_specs=[row(D), row(ATT_W), row(2 * D), row(D), row(D), row(D), pl.BlockSpec((8, 2 * D), lambda i: (i, 0))],
        out_shape=[jax.ShapeDtypeStruct((T, D), bf16), jax.ShapeDtypeStruct((T, ATT_W), bf16),
                   jax.ShapeDtypeStruct((T, 2 * D), bf16), jax.ShapeDtypeStruct((T, D), bf16),
                   jax.ShapeDtypeStruct((T, D), bf16), jax.ShapeDtypeStruct((T, D), bf16),
                   jax.ShapeDtypeStruct((T // tm * 8, 2 * D), f32)],
        compiler_params=_params(("parallel",)),
    )(dh1, proj, proj, proj, proj, b_gate, yr, ya, w_ro, w_ao, w_out)


def _ret_bwd(proj, qr, kr, o, states, du, B, S, rope, decay, phase=None):
    T = B * S
    nc = S // CHUNK
    H, dk, dv = RET_HEADS, RET_KEY_DIM, RET_VAL_DIM

    sb = RET_CHUNKS * CHUNK
    ns = S // sb

    def body(qr_ref, kr_ref, v_ref, g_ref, o_ref, st_ref, du_ref, cos_ref, sin_ref, intra_ref, qd_ref, kd_ref, cd_ref,
             dq_ref, dk_ref, dv_ref, dg_ref, dstate_ref):
        @pl.when(pl.program_id(1) == 0)
        def _():
            dstate_ref[...] = jnp.zeros_like(dstate_ref)

        cos, snb = cos_ref[...], -sin_ref[...]
        dstates = [dstate_ref[h] for h in range(H)]
        for ci in reversed(range(RET_CHUNKS)):
            r = slice(ci * CHUNK, (ci + 1) * CHUNK)
            for h in range(H):
                hk, hv = slice(h * dk, (h + 1) * dk), slice(h * dv, (h + 1) * dv)
                intra, qd, kd = intra_ref[h], qd_ref[h], kd_ref[h]
                qi, ki, vi = qr_ref[r, hk], kr_ref[r, hk], v_ref[r, hv]
                si = st_ref[0, h, ci]
                o = o_ref[r, hv].astype(f32)
                mu = jnp.mean(o, axis=-1, keepdims=True)
                xc = o - mu
                rstd = lax.rsqrt(jnp.mean(xc * xc, axis=-1, keepdims=True) + EPS)
                oh = xc * rstd
                g = g_ref[r, hv].astype(f32)
                sg = _sig(g)
                dui = du_ref[r, hv].astype(f32)
                dg_ref[r, hv] = (dui * oh * (sg * (1.0 + g * (1.0 - sg)))).astype(bf16)
                doh = dui * (g * sg)
                do = rstd * (doh - jnp.mean(doh, axis=-1, keepdims=True)
                             - oh * jnp.mean(doh * oh, axis=-1, keepdims=True))
                dob = do.astype(bf16)
                p = (_dot_nt(qi, ki) * intra).astype(bf16)
                dsb = dstates[h].astype(bf16)
                kt = (ki.astype(f32) * kd).astype(bf16)
                qt = (qi.astype(f32) * qd).astype(bf16)
                dv_ref[r, hv] = (_dot_tn(p, dob) + _dot(kt, dsb)).astype(bf16)
                da = (_dot_nt(dob, vi) * intra).astype(bf16)
                dq = _dot(da, ki) + _dot_nt(dob, si) * qd
                dkk = (_dot_tn(da, qi) + _dot_nt(vi, dsb) * kd) * K_SCALE
                dq_ref[r, hk] = _rotate(dq, cos[r], snb[r]).astype(bf16)
                dk_ref[r, hk] = _rotate(dkk, cos[r], snb[r]).astype(bf16)
                dstates[h] = dstates[h] * cd_ref[h] + _dot_tn(qt, dob)
        for h in range(H):
            dstate_ref[h] = dstates[h]

    blk = lambda w, c: pl.BlockSpec((sb, w), lambda b, i: (b * ns + ns - 1 - i, c))
    return _call(
        body, phase, name="ret_bwd", grid=(B, ns),
        in_specs=[blk(H * dk, 0), blk(H * dk, 0), blk(H * dv, C_RV // (H * dv)), blk(H * dv, C_RG // (H * dv)),
                  blk(H * dv, 0),
                  pl.BlockSpec((1, H, RET_CHUNKS, dk, dv), lambda b, i: (b, 0, ns - 1 - i, 0, 0)),
                  blk(H * dv, 0),
                  pl.BlockSpec((sb, dk), lambda b, i: (ns - 1 - i, 0)), pl.BlockSpec((sb, dk), lambda b, i: (ns - 1 - i, 0)),
                  *_ret_tables_specs()],
        out_specs=[blk(H * dk, 0), blk(H * dk, 0), blk(H * dv, 0), blk(H * dv, 0)],
        out_shape=[jax.ShapeDtypeStruct((T, H * dk), bf16), jax.ShapeDtypeStruct((T, H * dk), bf16),
                   jax.ShapeDtypeStruct((T, H * dv), bf16), jax.ShapeDtypeStruct((T, H * dv), bf16)],
        scratch_shapes=[pltpu.VMEM((H, dk, dv), f32)],
        args=(qr, kr, proj, proj, o, states, du, *rope, *decay))


def _att_bwd(proj, dao, trows, B, S, phase=None):
    T = B * S
    nq = S // QBLK
    dh = ATT_HEAD_DIM
    scale = ATT_HEAD_DIM ** -0.5

    def body(q_ref, k_ref, v_ref, do_ref, t_ref, dq_ref, dk_ref, dv_ref, vec_ref, bias_ref, dbias_ref, dka_ref, dva_ref):
        b, i = pl.program_id(0), pl.program_id(1)

        @pl.when((b == 0) & (i == 0))
        def _():
            _build_bias(t_ref, bias_ref)
            dbias_ref[...] = jnp.zeros_like(dbias_ref)

        @pl.when(i == 0)
        def _():
            dka_ref[...] = jnp.zeros_like(dka_ref)
            dva_ref[...] = jnp.zeros_like(dva_ref)

        def step(nk):
            win = pl.ds(pl.multiple_of((i + 1) * QBLK - nk, QBLK), nk)
            kw, vw = k_ref[win, :], v_ref[win, :]
            dqs, dks, dvs = [], [], []
            for h in range(ATT_HEADS):
                hs = slice(h * dh, (h + 1) * dh)
                qh, kh, vh, doh = q_ref[:, hs], kw[:, hs], vw[:, hs], do_ref[:, hs]
                pr = _att_probs(qh, kh, bias_ref[h, :, KWIN - nk:])
                dp = _dot_nt(doh, vh)
                ds = pr * (dp - jnp.sum(pr * dp, axis=-1, keepdims=True))
                dbias_ref[h, :, KWIN - nk:] += ds
                dsb = (ds * scale).astype(bf16)
                dqs.append(_dot(dsb, kh))
                dks.append(_dot_tn(dsb, qh))
                dvs.append(_dot_tn(pr.astype(bf16), doh))
            dq_ref[...] = jnp.concatenate(dqs, axis=1).astype(bf16)
            dka_ref[win, :] += jnp.concatenate(dks, axis=1)
            dva_ref[win, :] += jnp.concatenate(dvs, axis=1)

        _by_window(i, step)

        @pl.when(i == nq - 1)
        def _():
            dk_ref[...] = dka_ref[...].astype(bf16)
            dv_ref[...] = dva_ref[...].astype(bf16)

        @pl.when((b == B - 1) & (i == nq - 1))
        def _():
            rr = lax.broadcasted_iota(jnp.int32, (QBLK, QBLK), 0)
            cc = lax.broadcasted_iota(jnp.int32, (QBLK, QBLK), 1)
            flip = jnp.where(rr + cc == QBLK - 1, 1.0, 0.0).astype(bf16)
            for h in range(ATT_HEADS):
                d = dbias_ref[h]
                hi = d.astype(bf16)
                lo = (d - hi.astype(f32)).astype(bf16)
                rev = _dot(flip, hi) + _dot(flip, lo)
                wide = jnp.concatenate([rev, jnp.zeros((QBLK, TOEP - KWIN), f32)], axis=1)
                rolled = pltpu.roll(wide, 0, 1, stride=1, stride_axis=0)
                vec_ref[h:h + 1, :] = jnp.sum(rolled, axis=0, keepdims=True)

    qspec = lambda c: pl.BlockSpec((QBLK, ATT_W), lambda b, i: (b * nq + i, c))
    kspec = lambda c: pl.BlockSpec((S, ATT_W), lambda b, i: (b, c))
    seq = jax.ShapeDtypeStruct((T, ATT_W), bf16)
    return _call(
        body, phase, name="att_bwd", grid=(B, nq),
        in_specs=[qspec(C_AQ // ATT_W), kspec(C_AK // ATT_W), kspec(C_AV // ATT_W), qspec(0),
                  pl.BlockSpec((ATT_HEADS, TOEP), lambda b, i: (0, 0))],
        out_specs=[qspec(0), kspec(0), kspec(0), pl.BlockSpec((ATT_HEADS, TOEP), lambda b, i: (0, 0))],
        out_shape=[seq, seq, seq, jax.ShapeDtypeStruct((ATT_HEADS, TOEP), f32)],
        scratch_shapes=[pltpu.VMEM((ATT_HEADS, QBLK, KWIN), f32), pltpu.VMEM((ATT_HEADS, QBLK, KWIN), f32),
                        pltpu.VMEM((S, ATT_W), f32), pltpu.VMEM((S, ATT_W), f32)],
        args=(proj, proj, proj, dao, trows))


def _in_proj_bwd(dproj, w_in, x2, gamma, dh1, phase=None):
    T, D = x2.shape
    nk, _, tk = w_in.shape
    tm = _tile(T, 512, 8)

    def body(dp_ref, w_ref, x_ref, g_ref, dh1_ref, dx_ref, part_ref, acc_ref):
        j = pl.program_id(1)

        @pl.when(j == 0)
        def _():
            acc_ref[...] = jnp.zeros_like(acc_ref)

        acc_ref[...] += _dot_nt(dp_ref[...], w_ref[0])

        @pl.when(j == nk - 1)
        def _():
            x = x_ref[...]
            r = lax.rsqrt(jnp.mean(x * x, axis=-1, keepdims=True) + EPS)
            n = x * r
            dxn = acc_ref[...]
            dn = dxn * g_ref[...]
            dx_ref[...] = dh1_ref[...] + r * (dn - n * jnp.mean(dn * n, axis=-1, keepdims=True))
            part_ref[...] = jnp.zeros_like(part_ref)
            part_ref[0:1, :] = jnp.sum(dxn * n, axis=0, keepdims=True)

    row = lambda n: pl.BlockSpec((tm, n), lambda i, j: (i, 0))
    return _call(
        body, phase, name="in_proj_bwd", grid=(T // tm, nk),
        in_specs=[pl.BlockSpec((tm, tk), lambda i, j: (i, j)), pl.BlockSpec((1, D, tk), lambda i, j: (j, 0, 0)), row(D),
                  pl.BlockSpec((1, D), lambda i, j: (0, 0)), row(D)],
        out_specs=[row(D), pl.BlockSpec((8, D), lambda i, j: (i, 0))],
        out_shape=[jax.ShapeDtypeStruct((T, D), f32), jax.ShapeDtypeStruct((T // tm * 8, D), f32)],
        scratch_shapes=[pltpu.VMEM((tm, D), f32)],
        args=(dproj, w_in, x2, gamma, dh1))


def _wgrad(a, b, shard_axis, name):
    def spec(arr, sharded, tt):
        if arr.ndim == 3:
            return arr.shape[2], pl.BlockSpec((1, tt, arr.shape[2]), lambda s, t: (s, t, 0))
        if sharded:
            w = arr.shape[1] // N_CHIPS
            return w, pl.BlockSpec((tt, w), lambda s, t: (t, s))
        return arr.shape[1], pl.BlockSpec((tt, arr.shape[1]), lambda s, t: (t, 0))

    T = a.shape[-2]
    tt = _tile(T, 512, 16)
    nt = T // tt
    whole = a.ndim == 2 and b.ndim == 2 and a.shape[1] * b.shape[1] * 4 <= WGRAD_ACC_BYTES
    if whole:
        K, N = a.shape[1], b.shape[1]
        a_spec, b_spec = pl.BlockSpec((tt, K), lambda s, t: (t, 0)), pl.BlockSpec((tt, N), lambda s, t: (t, 0))
        out_block = (N_CHIPS, K // N_CHIPS, N) if shard_axis == 0 else (N_CHIPS, K, N // N_CHIPS)
        out_spec = pl.BlockSpec(out_block, lambda s, t: (0, 0, 0))
    else:
        K, a_spec = spec(a, shard_axis == 0, tt)
        N, b_spec = spec(b, shard_axis == 1, tt)
        out_block = (N_CHIPS, K, N)
        out_spec = pl.BlockSpec((1, K, N), lambda s, t: (s, 0, 0))

    def body(a_ref, b_ref, o_ref, acc_ref):
        t = pl.program_id(1)

        @pl.when(t == 0)
        def _():
            acc_ref[...] = jnp.zeros_like(acc_ref)

        av = a_ref[0] if a.ndim == 3 else a_ref[...]
        bv = b_ref[0] if b.ndim == 3 else b_ref[...]
        acc_ref[...] += _dot_tn(av.astype(bf16), bv.astype(bf16))

        @pl.when(t == nt - 1)
        def _():
            if not whole:
                o_ref[0] = acc_ref[...].astype(bf16)
            else:
                _, kk, nn = out_block
                for s in range(N_CHIPS):
                    o_ref[s] = (acc_ref[s * kk:(s + 1) * kk, :] if shard_axis == 0
                                else acc_ref[:, s * nn:(s + 1) * nn]).astype(bf16)

    return pl.pallas_call(
        body, name=name, grid=(1 if whole else N_CHIPS, nt), in_specs=[a_spec, b_spec], out_specs=out_spec,
        out_shape=jax.ShapeDtypeStruct(out_block, bf16),
        scratch_shapes=[pltpu.VMEM((K, N), f32)],
        compiler_params=_params(("parallel", "arbitrary")),
    )(a, b)


def _adamw_sum(place, w, m, v, part, from_chips, from_sibling, name):
    R, C = w.shape
    half = R // 2
    tr = _tile(half, max(16, (1 << 18) // C // 16 * 16), 16)
    nr = half // tr

    def body(p_ref, w_ref, m_ref, v_ref, part_ref, fc_ref, fs_ref, g_ref, d_ref, mo_ref, vo_ref):
        up = lambda x: x.astype(f32)
        mine = ((up(part_ref[0]) + up(fc_ref[0])) + up(fc_ref[1])) + up(fc_ref[2])
        sibs = ((up(fs_ref[0]) + up(fs_ref[1])) + up(fs_ref[2])) + up(fs_ref[3])
        g_ = jnp.where(pl.program_id(0) == p_ref[0], mine, sibs)
        m_ = ADAM_B1 * m_ref[...] + (1.0 - ADAM_B1) * g_
        v_ = ADAM_B2 * v_ref[...] + (1.0 - ADAM_B2) * (g_ * g_)
        m_hat = m_ / (1.0 - ADAM_B1 ** ADAM_STEP)
        v_hat = v_ / (1.0 - ADAM_B2 ** ADAM_STEP)
        g_ref[...] = g_
        d_ref[...] = -ADAM_LR * (m_hat / (jnp.sqrt(v_hat) + ADAM_EPS) + ADAM_WD * w_ref[...])
        mo_ref[...] = m_
        vo_ref[...] = v_

    spec = pl.BlockSpec((tr, C), lambda h, r, p: (h * nr + r, 0))
    return pl.pallas_call(
        body, name=name,
        grid_spec=pltpu.PrefetchScalarGridSpec(
            num_scalar_prefetch=1, grid=(2, nr),
            in_specs=[spec, spec, spec, pl.BlockSpec((1, tr, C), lambda h, r, p: (p[1], r, 0)),
                      pl.BlockSpec((3, tr, C), lambda h, r, p: (0, r, 0)),
                      pl.BlockSpec((4, tr, C), lambda h, r, p: (0, r, 0))],
            out_specs=[spec] * 4),
        out_shape=[jax.ShapeDtypeStruct((R, C), f32)] * 4,
        compiler_params=_params(("parallel", "parallel")),
    )(place, w, m, v, part, from_chips, from_sibling)


def _adamw(w, g, m, v, name):
    R, C = w.shape
    tr = _tile(R, max(8, (1 << 18) // C // 8 * 8), 8)

    def body(w_ref, g_ref, m_ref, v_ref, d_ref, mo_ref, vo_ref):
        g_ = g_ref[...]
        m_ = ADAM_B1 * m_ref[...] + (1.0 - ADAM_B1) * g_
        v_ = ADAM_B2 * v_ref[...] + (1.0 - ADAM_B2) * (g_ * g_)
        m_hat = m_ / (1.0 - ADAM_B1 ** ADAM_STEP)
        v_hat = v_ / (1.0 - ADAM_B2 ** ADAM_STEP)
        d_ref[...] = -ADAM_LR * (m_hat / (jnp.sqrt(v_hat) + ADAM_EPS) + ADAM_WD * w_ref[...])
        mo_ref[...] = m_
        vo_ref[...] = v_

    spec = pl.BlockSpec((tr, C), lambda i: (i, 0))
    return pl.pallas_call(
        body, name=name, grid=(R // tr,), in_specs=[spec] * 4, out_specs=[spec] * 3,
        out_shape=[jax.ShapeDtypeStruct((R, C), f32)] * 3,
        compiler_params=_params(("parallel",)),
    )(w, g, m, v)


def _place():
    return lax.axis_index("x"), lax.axis_index("y"), lax.axis_index("c")


def _other_chips(x, y):
    chips = [(1 - x, y), (x, 1 - y), (1 - x, 1 - y)]
    return chips, [2 * cx + cy for cx, cy in chips]


def _exchange_small(blk, name, reduce):
    R, C = blk.shape

    def body(x_ref, out_ref, *rest):
        if reduce:
            all_ref, send_sems, recv_sems = rest
        else:
            all_ref = out_ref
            send_sems, recv_sems = rest
        x, y, c = _place()
        me = 4 * x + 2 * y + c
        all_ref[me] = x_ref[...]
        copies = []
        for k in range(1, N_DEV):
            peer = tuple(1 - p if (k >> s) & 1 else p for p, s in ((x, 2), (y, 1), (c, 0)))
            cp = pltpu.make_async_remote_copy(src_ref=x_ref, dst_ref=all_ref.at[me], send_sem=send_sems.at[k - 1],
                                              recv_sem=recv_sems.at[k - 1], device_id=peer, device_id_type=MESH)
            cp.start()
            copies.append(cp)
        for cp in copies:
            cp.wait()
        if reduce:
            tot = all_ref[0]
            for d in range(1, N_DEV):
                tot = tot + all_ref[d]
            out_ref[...] = tot

    vm = pl.BlockSpec(memory_space=pltpu.VMEM)
    scratch = [pltpu.SemaphoreType.DMA((N_DEV - 1,)), pltpu.SemaphoreType.DMA((N_DEV - 1,))]
    if reduce:
        scratch = [pltpu.VMEM((N_DEV, R, C), f32)] + scratch
    return pl.pallas_call(
        body, name=name, in_specs=[vm], out_specs=vm,
        out_shape=jax.ShapeDtypeStruct((R, C) if reduce else (N_DEV, R, C), f32),
        scratch_shapes=scratch,
    )(blk)


def _cast_shard(place, w, name):
    R, C = w.shape
    tr = _tile(R, max(16, (1 << 19) // C // 16 * 16), 16)

    def body(p_ref, w_ref, o_ref):
        o_ref[0] = w_ref[...].astype(bf16)

    return pl.pallas_call(
        body, name=name,
        grid_spec=pltpu.PrefetchScalarGridSpec(
            num_scalar_prefetch=1, grid=(R // tr,),
            in_specs=[pl.BlockSpec((tr, C), lambda r, p: (r, 0))],
            out_specs=pl.BlockSpec((1, tr, C), lambda r, p: (p[1], r, 0))),
        out_shape=jax.ShapeDtypeStruct((N_CHIPS, R, C), bf16),
        compiler_params=_params(("parallel",)),
    )(place, w)


class _Phase:
    def __init__(self, arrays, out_shapes, aliases, n_copies, copies, arrivals):
        self.arrays, self.out_shapes, self.aliases = list(arrays), list(out_shapes), dict(aliases)
        self.n_copies, self.copies, self.arrivals = n_copies, copies, arrivals

    def sems(self):
        return [pltpu.SemaphoreType.DMA((self.n_copies,)), pltpu.SemaphoreType.DMA((self.n_copies,))]

    def _descriptors(self, pin, pout, send_sems, recv_sems):
        return [pltpu.make_async_remote_copy(src_ref=s, dst_ref=d, send_sem=send_sems.at[i], recv_sem=recv_sems.at[i],
                                             device_id=to, device_id_type=MESH)
                for i, (s, d, to) in enumerate(self.copies(pin, pout))]

    def start(self, pin, pout, send_sems, recv_sems):
        for cp in self._descriptors(pin, pout, send_sems, recv_sems):
            cp.start()

    def finish(self, pin, pout, send_sems, recv_sems):
        x, y, c = _place()
        sent = self._descriptors(pin, pout, send_sems, recv_sems)
        for i, dst in enumerate(self.arrivals(pin, pout)):
            pltpu.make_async_remote_copy(src_ref=dst, dst_ref=dst, send_sem=send_sems.at[i], recv_sem=recv_sems.at[i],
                                         device_id=(x, y, c), device_id_type=MESH).wait_recv()
        for cp in sent:
            cp.wait_send()


def _call(body, phase, *, name, grid, in_specs, out_specs, out_shape, scratch_shapes, args):
    seq = _params(("arbitrary",) * len(grid))
    if phase is None:
        res = pl.pallas_call(body, name=name, grid=grid, in_specs=in_specs, out_specs=out_specs, out_shape=out_shape,
                             scratch_shapes=scratch_shapes, compiler_params=seq)(*args)
        return list(res), []
    ni, no, ns = len(in_specs), len(out_specs), len(scratch_shapes)
    pi, po = len(phase.arrays), len(phase.out_shapes)

    def hosted(*refs):
        cut = np.cumsum([ni, pi, no, po, ns])
        ins, pin, outs, pout, scr, sems = (refs[a:b] for a, b in zip([0, *cut], [*cut, len(refs)]))
        ids = [pl.program_id(d) for d in range(len(grid))]
        first = functools.reduce(lambda p, q: p & q, [i == 0 for i in ids])
        last = functools.reduce(lambda p, q: p & q, [i == g - 1 for i, g in zip(ids, grid)])
        pl.when(first)(lambda: phase.start(pin, pout, *sems))
        body(*ins, *outs, *scr)
        pl.when(last)(lambda: phase.finish(pin, pout, *sems))

    anyspace = pl.BlockSpec(memory_space=pl.ANY)
    res = pl.pallas_call(
        hosted, name=name, grid=grid, in_specs=list(in_specs) + [anyspace] * pi,
        out_specs=list(out_specs) + [anyspace] * po, out_shape=list(out_shape) + phase.out_shapes,
        input_output_aliases={ni + i: no + j for i, j in phase.aliases.items()},
        scratch_shapes=list(scratch_shapes) + phase.sems(), compiler_params=seq,
    )(*args, *phase.arrays)
    return list(res[:no]), list(res[no:])


def _run_phases(name, phases):
    first = phases[0]
    pi, po = len(first.arrays), len(first.out_shapes)

    def body(*refs):
        pin, pout, sems = refs[:pi], refs[pi:pi + po], refs[pi + po:]
        for n, ph in enumerate(phases):
            ph.start(pin, pout, *sems[2 * n:2 * n + 2])
            ph.finish(pin, pout, *sems[2 * n:2 * n + 2])

    anyspace = pl.BlockSpec(memory_space=pl.ANY)
    return list(pl.pallas_call(
        body, name=name, in_specs=[anyspace] * pi, out_specs=[anyspace] * po, out_shape=first.out_shapes,
        input_output_aliases=first.aliases, scratch_shapes=[s for ph in phases for s in ph.sems()],
    )(*first.arrays))


def _half_rows(buf, c):
    half = buf.shape[1] // 2
    return pl.ds(c * half, half), pl.ds((1 - c) * half, half)


def _gather_phase(bufs, over_ici):
    n = len(bufs)
    shapes = [jax.ShapeDtypeStruct(b.shape, b.dtype) for b in bufs]

    def landed(out, which):
        x, y, c = _place()
        _, ks = _other_chips(x, y)
        return [out[a].at[ks[j], _half_rows(bufs[a], c)[which]] for a in range(n) for j in range(3)]

    def ici(pin, out):
        x, y, c = _place()
        chips, _ = _other_chips(x, y)
        mine = [out[a].at[2 * x + y, _half_rows(bufs[a], c)[0]] for a in range(n)]
        return [(mine[a], mine[a], (*chips[j], c)) for a in range(n) for j in range(3)]

    def d2d(pin, out):
        x, y, c = _place()
        return [(dst, dst, (x, y, 1 - c)) for dst in landed(out, 0)]

    if over_ici:
        return _Phase(bufs, shapes, {a: a for a in range(n)}, 3 * n, ici, lambda pin, out: landed(out, 0))
    return _Phase(bufs, shapes, {a: a for a in range(n)}, 3 * n, d2d, lambda pin, out: landed(out, 1))


def _rs_sibling(grads, name):
    n = len(grads)

    def body(*refs):
        g, out, send_sems, recv_sems = refs[:n], refs[n:2 * n], refs[2 * n], refs[2 * n + 1]
        x, y, c = _place()
        copies = []
        for a in range(n):
            half = grads[a].shape[1] // 2
            cp = pltpu.make_async_remote_copy(src_ref=g[a].at[:, pl.ds((1 - c) * half, half)], dst_ref=out[a],
                                              send_sem=send_sems.at[a], recv_sem=recv_sems.at[a],
                                              device_id=(x, y, 1 - c), device_id_type=MESH)
            cp.start()
            copies.append(cp)
        for cp in copies:
            cp.wait()

    anyspace = pl.BlockSpec(memory_space=pl.ANY)
    return pl.pallas_call(
        body, name=name, in_specs=[anyspace] * n, out_specs=[anyspace] * n,
        out_shape=[jax.ShapeDtypeStruct((N_CHIPS, g.shape[1] // 2, g.shape[2]), g.dtype) for g in grads],
        scratch_shapes=[pltpu.SemaphoreType.DMA((n,)), pltpu.SemaphoreType.DMA((n,))],
    )(*grads)


def _rs_add_sibling(place, grad, got, name):
    _, R, C = grad.shape
    half = R // 2
    tr = _tile(half, max(16, (1 << 19) // C // 16 * 16), 16)
    nr = half // tr

    def body(p_ref, a_ref, b_ref, o_ref):
        o_ref[...] = (a_ref[...].astype(f32) + b_ref[...].astype(f32)).astype(o_ref.dtype)

    return pl.pallas_call(
        body, name=name,
        grid_spec=pltpu.PrefetchScalarGridSpec(
            num_scalar_prefetch=1, grid=(N_CHIPS, nr),
            in_specs=[pl.BlockSpec((1, tr, C), lambda k, r, p: (k, p[0] * nr + r, 0)),
                      pl.BlockSpec((1, tr, C), lambda k, r, p: (k, r, 0))],
            out_specs=pl.BlockSpec((1, tr, C), lambda k, r, p: (k, r, 0))),
        out_shape=jax.ShapeDtypeStruct((N_CHIPS, half, C), bf16),
        compiler_params=_params(("parallel", "parallel")),
    )(place, grad, got)


def _rs_chips_phase(parts):
    n = len(parts)

    def copies(p, fc):
        x, y, c = _place()
        chips, ks = _other_chips(x, y)
        return [(p[a].at[ks[j]], fc[a].at[j], (*chips[j], c)) for a in range(n) for j in range(3)]

    shapes = [jax.ShapeDtypeStruct((3,) + q.shape[1:], q.dtype) for q in parts]
    return _Phase(parts, shapes, {}, 3 * n, copies, lambda p, fc: [fc[a].at[j] for a in range(n) for j in range(3)])


def _rs_hand_phase(parts, from_chips):
    n = len(parts)

    def copies(pin, fs):
        x, y, c = _place()
        sib = (x, y, 1 - c)
        own = [(pin[a].at[2 * x + y], fs[a].at[0], sib) for a in range(n)]
        return own + [(pin[n + a].at[j], fs[a].at[1 + j], sib) for a in range(n) for j in range(3)]

    def arrivals(pin, fs):
        return [fs[a].at[0] for a in range(n)] + [fs[a].at[1 + j] for a in range(n) for j in range(3)]

    shapes = [jax.ShapeDtypeStruct((4,) + q.shape[1:], q.dtype) for q in parts]
    return _Phase(list(parts) + list(from_chips), shapes, {}, 4 * n, copies, arrivals)


class _Exchange:
    def __init__(self, place):
        self.place = place

    def gather(self, bufs, over_ici):
        return _gather_phase(bufs, over_ici)

    def pair_sums(self, names, grads):
        got = _rs_sibling(grads, "rs_sibling_" + names[0])
        return [_rs_add_sibling(self.place, g, r, "rs_add_" + n) for n, g, r in zip(names, grads, got)]

    def to_chips(self, parts):
        return _rs_chips_phase(parts)

    def to_sibling(self, parts, from_chips):
        return _rs_hand_phase(parts, from_chips)

    def hand_over(self, name, parts, from_chips):
        return _run_phases(name, [_rs_hand_phase(parts, from_chips)])


def _local_step(x, target, norm_mix, b_gate, rb_full, norm_ffn, norm_final, w_in, rest, exch):
    B, S, D = x.shape
    T = B * S
    x2 = x.reshape(T, D)
    tg2 = target.reshape(T, D)
    rope, decay = _rope_tables(S), _decay_tables()
    trows = _bias_rows(rb_full)
    g_fin = norm_final.reshape(1, D)

    xn, proj = _in_proj(x2, norm_mix, w_in)
    bufs = list(rest.values())
    (qr, kr, o, u, states), got = _ret_fwd(proj, B, S, rope, decay, exch.gather(bufs, True))
    bufs = got or bufs
    (ao,), got = _att_fwd(proj, trows, B, S, exch.gather(bufs, False))
    wb = dict(zip(rest, got or bufs))
    w_ro, w_out = wb["w_ret_out"].reshape(-1, D), wb["w_out"].reshape(-1, D)
    h1, yr, ya = _mix_fwd(x2, proj, u, ao, b_gate, w_ro, wb["w_att_out"], w_out)
    hn, a, b, f, dh2, part_fin = _ffn_fwd(h1, norm_ffn, wb["w_ffn_gate"], wb["w_ffn_up"], wb["w_ffn_down"], g_fin, tg2)

    da, db, dh1, part_ffn = _ffn_bwd(dh2, h1, norm_ffn, a, b, wb["w_ffn_gate"], wb["w_ffn_up"], wb["w_ffn_down"])
    du, dao, dgl, mix, dyr, dya, part_bg = _mix_bwd(dh1, proj, yr, ya, b_gate, w_ro, wb["w_att_out"], w_out)
    first = {
        "w_ffn_down": _wgrad(f, dh2, 0, "wgrad_ffn_down"),
        "w_ffn_gate": _wgrad(hn, da, 1, "wgrad_ffn_gate"),
        "w_ffn_up": _wgrad(hn, db, 1, "wgrad_ffn_up"),
        "w_out": _wgrad(mix, dh1, 0, "wgrad_out"),
        "w_ret_out": _wgrad(u, dyr, 0, "wgrad_ret_out"),
        "w_att_out": _wgrad(ao, dya, 1, "wgrad_att_out"),
    }
    parts = exch.pair_sums(list(first), list(first.values()))
    (drq, drk, drv, drg), from_chips = _ret_bwd(proj, qr, kr, o, states, du, B, S, rope, decay, exch.to_chips(parts))
    (daq, dak, dav, dvec), from_sib = _att_bwd(proj, dao, trows, B, S, exch.to_sibling(parts, from_chips))
    dproj = jnp.concatenate([drq, drk, drv, drg, daq, dak, dav, dgl], axis=1)
    part_in = exch.pair_sums(["w_in"], [_wgrad(xn, dproj, 1, "wgrad_in")])
    (gx, part_mix), chips_in = _in_proj_bwd(dproj, w_in, x2, norm_mix, dh1, exch.to_chips(part_in))
    sib_in = exch.hand_over("rs_hand_w_in", part_in, chips_in)
    gbig = dict(zip(first, zip(parts, from_chips, from_sib)))
    gbig["w_in"] = (part_in[0], chips_in[0], sib_in[0])
    rows = lambda p, r: p.reshape(-1, 8, p.shape[-1])[:, r, :].sum(axis=0)
    lo = KWIN - 1 - (MAX_REL - 1)
    drb = jnp.concatenate([jnp.flip(dvec[:, lo:lo + N_REL - 1], axis=1), dvec[:, :lo].sum(axis=1, keepdims=True)], axis=1)
    gsmall = {
        "norm_mix": rows(part_mix, 0), "b_gate": rows(part_bg, 0), "rel_bias": drb, "norm_ffn": rows(part_ffn, 0),
        "norm_final": rows(part_fin, 0),
    }
    return rows(part_fin, 1), gx.reshape(B, S, D), gbig, gsmall


SMALL_ROWS = 16


def _pack_small(gs, loss_lanes):
    D = D_MODEL
    rb = jnp.pad(gs["rel_bias"].reshape(-1), (0, 3 * D - ATT_HEADS * N_REL)).reshape(3, D)
    rows = [gs["norm_mix"].reshape(1, D), gs["b_gate"].reshape(2, D), gs["norm_ffn"].reshape(1, D),
            gs["norm_final"].reshape(1, D), rb, loss_lanes.reshape(1, D)]
    used = sum(r.shape[0] for r in rows)
    return jnp.concatenate(rows + [jnp.zeros((SMALL_ROWS - used, D), f32)], axis=0)


def kernel(x, norm_mix, w_in, b_gate, rel_bias, w_ret_out, w_att_out, w_out, norm_ffn, w_ffn_gate, w_ffn_up, w_ffn_down, norm_final, loss_target, m_norm_mix, m_w_in, m_b_gate, m_rel_bias, m_w_ret_out, m_w_att_out, m_w_out, m_norm_ffn, m_w_ffn_gate, m_w_ffn_up, m_w_ffn_down, m_norm_final, v_norm_mix, v_w_in, v_b_gate, v_rel_bias, v_w_ret_out, v_w_att_out, v_w_out, v_norm_ffn, v_w_ffn_gate, v_w_ffn_up, v_w_ffn_down, v_norm_final):
    w = dict(norm_mix=norm_mix, w_in=w_in, b_gate=b_gate, rel_bias=rel_bias, w_ret_out=w_ret_out, w_att_out=w_att_out,
             w_out=w_out, norm_ffn=norm_ffn, w_ffn_gate=w_ffn_gate, w_ffn_up=w_ffn_up, w_ffn_down=w_ffn_down,
             norm_final=norm_final)
    m = dict(norm_mix=m_norm_mix, w_in=m_w_in, b_gate=m_b_gate, rel_bias=m_rel_bias, w_ret_out=m_w_ret_out,
             w_att_out=m_w_att_out, w_out=m_w_out, norm_ffn=m_norm_ffn, w_ffn_gate=m_w_ffn_gate, w_ffn_up=m_w_ffn_up,
             w_ffn_down=m_w_ffn_down, norm_final=m_norm_final)
    v = dict(norm_mix=v_norm_mix, w_in=v_w_in, b_gate=v_b_gate, rel_bias=v_rel_bias, w_ret_out=v_w_ret_out,
             w_att_out=v_w_att_out, w_out=v_w_out, norm_ffn=v_norm_ffn, w_ffn_gate=v_w_ffn_gate, w_ffn_up=v_w_ffn_up,
             w_ffn_down=v_w_ffn_down, norm_final=v_norm_final)
    xi, yi, ci = _place()
    k_me = 2 * xi + yi

    place = jnp.stack([ci, k_me]).astype(jnp.int32)
    big = [n for n, _ in BIG]

    bufs = {n: _cast_shard(place, w[n][0], "cast_" + n) for n in big}
    w_in_all, = _run_phases("gather_w_in", [_gather_phase([bufs["w_in"]], True), _gather_phase([bufs["w_in"]], False)])
    rest = {n: bufs[n] for n in big if n != "w_in"}
    nrel_loc = rel_bias.shape[-1]
    rb_all = _exchange_small(jnp.pad(rel_bias[0], ((0, 0), (0, 128 - nrel_loc))), "gather_rel_bias", False)
    rb_full = jnp.concatenate([rb_all[2 * k, :, :nrel_loc] for k in range(N_CHIPS)], axis=1)

    loss_lanes, grad_x, gbig, gsmall = _local_step(x, loss_target, norm_mix, b_gate, rb_full, norm_ffn, norm_final,
                                                   w_in_all, rest, _Exchange(place))

    small = _exchange_small(_pack_small(gsmall, loss_lanes), "reduce_small", True)
    D = D_MODEL
    loss = jnp.sum(small[8])
    drb_full = small[5:8].reshape(-1)[:ATT_HEADS * N_REL].reshape(ATT_HEADS, N_REL)
    g = {
        "norm_mix": small[0:1], "b_gate": small[1:3].reshape(1, 2 * D), "norm_ffn": small[3:4], "norm_final": small[4],
        "rel_bias": lax.dynamic_slice_in_dim(drb_full, k_me * nrel_loc, nrel_loc, axis=1)[None],
    }

    delta, new_m, new_v = {}, {}, {}
    for n in big:
        g_, d_, m_, v_ = _adamw_sum(place, w[n][0], m[n][0], v[n][0], *gbig[n], "adamw_" + n)
        g[n], delta[n], new_m[n], new_v[n] = g_[None], d_[None], m_[None], v_[None]
    flat = lambda d: jnp.concatenate([d[n].reshape(-1) for n in SMALL])
    n_small = sum(int(np.prod(w[n].shape)) for n in SMALL)
    n_pad = -n_small % 1024
    packs = [jnp.pad(flat(d), (0, n_pad)).reshape(-1, 128) for d in (w, g, m, v)]
    outs = _adamw(*packs, "adamw_small")
    for res, dst in zip(outs, (delta, new_m, new_v)):
        off = 0
        fl = res.reshape(-1)
        for n in SMALL:
            sz = int(np.prod(w[n].shape))
            dst[n] = fl[off:off + sz].reshape(w[n].shape)
            off += sz

    return (loss, grad_x, *[g[n] for n in WEIGHTS], *[delta[n] for n in WEIGHTS], *[new_m[n] for n in WEIGHTS],
            *[new_v[n] for n in WEIGHTS])
```

```python
import functools

import numpy as np
import jax
import jax.numpy as jnp
from jax import lax
from jax.experimental import pallas as pl
from jax.experimental.pallas import tpu as pltpu

f32 = jnp.float32
bf16 = jnp.bfloat16

D_MODEL = 1024
CHUNK = 64
RET_HEADS = 4
RET_KEY_DIM = 128
RET_VAL_DIM = 256
ATT_HEADS = 8
ATT_HEAD_DIM = 64
ATT_W = ATT_HEADS * ATT_HEAD_DIM
BAND_CHUNKS = 8
PAD = BAND_CHUNKS * CHUNK
MAX_REL = 256
N_REL = CHUNK + MAX_REL
D_FF = 2816
N_IN = 6656
ROPE_BASE = 10000.0
EPS = 1e-6
NEG_INF = -1e30
C_RQ, C_RK, C_RV, C_RG, C_AQ, C_AK, C_AV, C_GL = 0, 512, 1024, 2048, 3072, 3584, 4096, 4608

ADAM_LR, ADAM_B1, ADAM_B2, ADAM_EPS, ADAM_WD, ADAM_STEP = 0.001, 0.9, 0.999, 1e-08, 0.01, 10

N_CHIPS = 4
N_DEV = 8
WGRAD_ACC_BYTES = 8 * 1024 * 1024
QBLK = 256
KWIN = PAD + QBLK
TOEP = 1024
VMEM_LIMIT = 56 * 1024 * 1024
MESH = pl.DeviceIdType.MESH

BIG = (
    ("w_in", 1), ("w_ret_out", 0), ("w_att_out", 1), ("w_out", 0), ("w_ffn_gate", 1), ("w_ffn_up", 1), ("w_ffn_down", 0))
WEIGHTS = ("norm_mix", "w_in", "b_gate", "rel_bias", "w_ret_out", "w_att_out", "w_out", "norm_ffn", "w_ffn_gate",
           "w_ffn_up", "w_ffn_down", "norm_final")
SMALL = ("norm_mix", "b_gate", "rel_bias", "norm_ffn", "norm_final")


def _dot(a, b):
    return lax.dot_general(a, b, (((1,), (0,)), ((), ())), preferred_element_type=f32)


def _dot_nt(a, b):
    return lax.dot_general(a, b, (((1,), (1,)), ((), ())), preferred_element_type=f32)


def _dot_tn(a, b):
    return lax.dot_general(a, b, (((0,), (0,)), ((), ())), preferred_element_type=f32)


def _sig(x):
    return 1.0 / (1.0 + jnp.exp(-x))


def _tile(n, pref, mult):
    best = None
    for t in range(mult, min(n, pref) + 1, mult):
        if n % t == 0:
            best = t
    return best if best is not None else n


def _params(sem, vmem=VMEM_LIMIT):
    return pltpu.CompilerParams(dimension_semantics=sem, vmem_limit_bytes=vmem)


def _in_proj(x2, gamma, w, phase=None):
    T, D = x2.shape
    ns = w.shape[2]
    tm = _tile(T, 512, 8)

    def body(x_ref, g_ref, w_ref, xn_ref, p_ref, xs_ref):
        @pl.when(pl.program_id(1) == 0)
        def _():
            x = x_ref[...]
            r = lax.rsqrt(jnp.mean(x * x, axis=-1, keepdims=True) + EPS)
            xn = (x * r * g_ref[...]).astype(bf16)
            xs_ref[...] = xn
            xn_ref[...] = xn

        p_ref[...] = _dot(xs_ref[...], w_ref[0]).astype(bf16)

    return _call(
        body, phase, name="in_proj", grid=(T // tm, N_CHIPS),
        in_specs=[pl.BlockSpec((tm, D), lambda i, j: (i, 0)), pl.BlockSpec((1, D), lambda i, j: (0, 0)),
                  pl.BlockSpec((1, D, ns), lambda i, j: (j, 0, 0))],
        out_specs=[pl.BlockSpec((tm, D), lambda i, j: (i, 0)), pl.BlockSpec((tm, ns), lambda i, j: (i, j))],
        out_shape=[jax.ShapeDtypeStruct((T, D), bf16), jax.ShapeDtypeStruct((T, N_CHIPS * ns), bf16)],
        scratch_shapes=[pltpu.VMEM((tm, D), bf16)],
        args=(x2, gamma, w))


def _rope_tables(S):
    d = RET_KEY_DIM
    freqs = ROPE_BASE ** (-jnp.arange(0, d, 2, dtype=f32) / d)
    ang = jnp.arange(S, dtype=f32)[:, None] * freqs[None, :]
    cos, sin = jnp.cos(ang), jnp.sin(ang)
    return jnp.concatenate([cos, cos], axis=1), jnp.concatenate([-sin, sin], axis=1)


def _decay_tables():
    H = RET_HEADS
    log_g = jnp.log(1.0 - 2.0 ** (-5.0 - jnp.arange(H, dtype=f32)))
    p = jnp.arange(CHUNK, dtype=f32)
    intra = jnp.exp(log_g[:, None, None] * jnp.abs(p[:, None] - p[None, :]))
    q_dec = jnp.exp(log_g[:, None] * (p[None, :] + 1.0))
    k_dec = jnp.exp(log_g[:, None] * (CHUNK - 1.0 - p[None, :]))
    c_dec = jnp.exp(log_g * CHUNK)
    q_dec = jnp.broadcast_to(q_dec[:, :, None], (H, CHUNK, RET_KEY_DIM))
    k_dec = jnp.broadcast_to(k_dec[:, :, None], (H, CHUNK, RET_KEY_DIM))
    c_dec = jnp.broadcast_to(c_dec[:, None, None], (H, 1, RET_VAL_DIM))
    return intra, q_dec, k_dec, c_dec


K_SCALE = RET_KEY_DIM ** -0.5


RET_CHUNKS = 4


def _ret_tables_specs():
    whole = lambda *shape: pl.BlockSpec(shape, lambda b, i: (0,) * len(shape))
    return [whole(RET_HEADS, CHUNK, CHUNK), whole(RET_HEADS, CHUNK, RET_KEY_DIM), whole(RET_HEADS, CHUNK, RET_KEY_DIM),
            whole(RET_HEADS, 1, RET_VAL_DIM)]


def _rotate(x, cos, sn):
    return x * cos + pltpu.roll(x, RET_KEY_DIM // 2, 1) * sn


def _ret_fwd(proj, B, S, rope, decay, phase=None):
    T = B * S
    nc = S // CHUNK
    H, dk, dv = RET_HEADS, RET_KEY_DIM, RET_VAL_DIM
    sb = RET_CHUNKS * CHUNK
    ns = S // sb

    def body(q_ref, k_ref, v_ref, g_ref, cos_ref, sin_ref, intra_ref, qd_ref, kd_ref, cd_ref,
             qr_ref, kr_ref, o_ref, u_ref, st_ref, state_ref):
        @pl.when(pl.program_id(1) == 0)
        def _():
            state_ref[...] = jnp.zeros_like(state_ref)

        cos, sn = cos_ref[...], sin_ref[...]
        for h in range(H):
            hs = slice(h * dk, (h + 1) * dk)
            qr_ref[:, hs] = _rotate(q_ref[:, hs].astype(f32), cos, sn).astype(bf16)
            kr_ref[:, hs] = (_rotate(k_ref[:, hs].astype(f32), cos, sn) * K_SCALE).astype(bf16)
        states = [state_ref[h] for h in range(H)]
        for ci in range(RET_CHUNKS):
            r = slice(ci * CHUNK, (ci + 1) * CHUNK)
            for h in range(H):
                hk, hv = slice(h * dk, (h + 1) * dk), slice(h * dv, (h + 1) * dv)
                qi, ki, vi = qr_ref[r, hk], kr_ref[r, hk], v_ref[r, hv]
                stb = states[h].astype(bf16)
                st_ref[0, h, ci] = stb
                s = (_dot_nt(qi, ki) * intra_ref[h]).astype(bf16)
                o = _dot(s, vi) + _dot((qi.astype(f32) * qd_ref[h]).astype(bf16), stb)
                states[h] = states[h] * cd_ref[h] + _dot_tn((ki.astype(f32) * kd_ref[h]).astype(bf16), vi)
                mu = jnp.mean(o, axis=-1, keepdims=True)
                xc = o - mu
                var = jnp.mean(xc * xc, axis=-1, keepdims=True)
                oh = xc * lax.rsqrt(var + EPS)
                g = g_ref[r, hv].astype(f32)
                o_ref[r, hv] = o.astype(bf16)
                u_ref[r, hv] = (g * _sig(g) * oh).astype(bf16)
        for h in range(H):
            state_ref[h] = states[h]

    blk = lambda w, c: pl.BlockSpec((sb, w), lambda b, i: (b * ns + i, c))
    return _call(
        body, phase, name="ret_fwd", grid=(B, ns), scratch_shapes=[pltpu.VMEM((H, dk, dv), f32)],
        in_specs=[blk(H * dk, C_RQ // (H * dk)), blk(H * dk, C_RK // (H * dk)), blk(H * dv, C_RV // (H * dv)),
                  blk(H * dv, C_RG // (H * dv)),
                  pl.BlockSpec((sb, dk), lambda b, i: (i, 0)), pl.BlockSpec((sb, dk), lambda b, i: (i, 0)),
                  *_ret_tables_specs()],
        out_specs=[blk(H * dk, 0), blk(H * dk, 0), blk(H * dv, 0), blk(H * dv, 0),
                   pl.BlockSpec((1, H, RET_CHUNKS, dk, dv), lambda b, i: (b, 0, i, 0, 0))],
        out_shape=[jax.ShapeDtypeStruct((T, H * dk), bf16), jax.ShapeDtypeStruct((T, H * dk), bf16),
                   jax.ShapeDtypeStruct((T, H * dv), bf16), jax.ShapeDtypeStruct((T, H * dv), bf16),
                   jax.ShapeDtypeStruct((B, H, nc, dk, dv), bf16)],
        args=(proj, proj, proj, proj, *rope, *decay))


def _bias_rows(rb):
    last = rb[:, N_REL - 1:]
    return jnp.concatenate([
        jnp.broadcast_to(last, (ATT_HEADS, PAD - MAX_REL + 1)),
        jnp.flip(rb[:, :N_REL - 1], axis=1),
        jnp.broadcast_to(rb[:, :1], (ATT_HEADS, KWIN - PAD - CHUNK)),
        jnp.broadcast_to(last, (ATT_HEADS, TOEP - KWIN)),
    ], axis=1)


def _build_bias(t_ref, bias_ref):
    row = lax.broadcasted_iota(jnp.int32, (QBLK, KWIN), 0) // CHUNK
    col = lax.broadcasted_iota(jnp.int32, (QBLK, KWIN), 1) // CHUNK
    delta = BAND_CHUNKS + row - col
    vis = (delta >= 0) & (delta <= BAND_CHUNKS)
    for h in range(ATT_HEADS):
        t = jnp.broadcast_to(t_ref[h:h + 1, :], (QBLK, TOEP))
        rolled = pltpu.roll(t, 0, 1, stride=1, stride_axis=0)
        bias_ref[h] = jnp.where(vis, rolled[:, :KWIN], NEG_INF)


def _att_probs(qh, kh, bias):
    s = _dot_nt(qh, kh) * (ATT_HEAD_DIM ** -0.5) + bias
    m = jnp.max(s, axis=-1, keepdims=True)
    p = jnp.exp(s - m)
    return p * (1.0 / jnp.sum(p, axis=-1, keepdims=True))


def _by_window(i, step):
    sizes = list(range(QBLK, KWIN, QBLK))
    for n, nk in enumerate(sizes):
        pl.when(i == n)(functools.partial(step, nk))
    pl.when(i >= len(sizes))(functools.partial(step, KWIN))


def _att_fwd(proj, trows, B, S, phase=None):
    T = B * S
    nq = S // QBLK
    dh = ATT_HEAD_DIM

    def body(q_ref, k_ref, v_ref, t_ref, o_ref, bias_ref):
        i = pl.program_id(1)

        @pl.when((pl.program_id(0) == 0) & (i == 0))
        def _():
            _build_bias(t_ref, bias_ref)

        def step(nk):
            win = pl.ds(pl.multiple_of((i + 1) * QBLK - nk, QBLK), nk)
            kw, vw = k_ref[win, :], v_ref[win, :]
            outs = []
            for h in range(ATT_HEADS):
                hs = slice(h * dh, (h + 1) * dh)
                pr = _att_probs(q_ref[:, hs], kw[:, hs], bias_ref[h, :, KWIN - nk:])
                outs.append(_dot(pr.astype(bf16), vw[:, hs]))
            o_ref[...] = jnp.concatenate(outs, axis=1).astype(bf16)

        _by_window(i, step)

    return _call(
        body, phase, name="att_fwd", grid=(B, nq),
        in_specs=[pl.BlockSpec((QBLK, ATT_W), lambda b, i: (b * nq + i, C_AQ // ATT_W)),
                  pl.BlockSpec((S, ATT_W), lambda b, i: (b, C_AK // ATT_W)),
                  pl.BlockSpec((S, ATT_W), lambda b, i: (b, C_AV // ATT_W)),
                  pl.BlockSpec((ATT_HEADS, TOEP), lambda b, i: (0, 0))],
        out_specs=[pl.BlockSpec((QBLK, ATT_W), lambda b, i: (b * nq + i, 0))],
        out_shape=[jax.ShapeDtypeStruct((T, ATT_W), bf16)],
        scratch_shapes=[pltpu.VMEM((ATT_HEADS, QBLK, KWIN), f32)],
        args=(proj, proj, proj, trows))


def _gl_specs(tm):
    w = 512
    return [pl.BlockSpec((tm, w), functools.partial(lambda i, j: (i, C_GL // 512 + j), j=j)) for j in range(4)]


def _gates(gl_refs, bg_ref):
    gl = jnp.concatenate([r[...] for r in gl_refs], axis=1).astype(f32) + bg_ref[...]
    g = _sig(gl)
    return g[:, :D_MODEL], g[:, D_MODEL:]


def _mix_fwd(x2, proj, u, ao, b_gate, w_ro, w_ao, w_out):
    T, D = x2.shape
    tm = _tile(T, 512, 8)

    def body(x_ref, u_ref, ao_ref, g0, g1, g2, g3, bg_ref, wro_ref, wao_ref, wo_ref, h1_ref, yr_ref, ya_ref):
        yr = _dot(u_ref[...], wro_ref[...])
        ao = ao_ref[...]
        ya = jnp.concatenate([_dot(ao, wao_ref[k]) for k in range(N_CHIPS)], axis=1)
        gr, ga = _gates((g0, g1, g2, g3), bg_ref)
        mix = gr * yr + ga * ya
        h1_ref[...] = x_ref[...] + _dot(mix.astype(bf16), wo_ref[...])
        yr_ref[...] = yr.astype(bf16)
        ya_ref[...] = ya.astype(bf16)

    full = lambda a: pl.BlockSpec(a.shape, lambda i: (0,) * a.ndim)
    row = lambda n: pl.BlockSpec((tm, n), lambda i: (i, 0))
    return pl.pallas_call(
        body, name="mix_fwd", grid=(T // tm,),
        in_specs=[row(D), row(D), row(ATT_W), *_gl_specs(tm), full(b_gate), full(w_ro), full(w_ao), full(w_out)],
        out_specs=[row(D), row(D), row(D)],
        out_shape=[jax.ShapeDtypeStruct((T, D), f32), jax.ShapeDtypeStruct((T, D), bf16),
                   jax.ShapeDtypeStruct((T, D), bf16)],
        compiler_params=_params(("parallel",)),
    )(x2, u, ao, proj, proj, proj, proj, b_gate, w_ro, w_ao, w_out)


def _ffn_fwd(h1, g_ffn, wg, wu, wd, g_fin, target):
    T, D = h1.shape
    nf, _, tf = wg.shape
    tm = _tile(T, 512, 8)

    def body(h1_ref, g_ref, wg_ref, wu_ref, wd_ref, gf_ref, tg_ref, hn_ref, a_ref, b_ref, f_ref, dh2_ref, part_ref,
             hs_ref, acc_ref):
        j = pl.program_id(1)

        @pl.when(j == 0)
        def _():
            h = h1_ref[...]
            r = lax.rsqrt(jnp.mean(h * h, axis=-1, keepdims=True) + EPS)
            hn = (h * r * g_ref[...]).astype(bf16)
            hs_ref[...] = hn
            hn_ref[...] = hn
            acc_ref[...] = jnp.zeros_like(acc_ref)

        hn = hs_ref[...]
        a = _dot(hn, wg_ref[0])
        b = _dot(hn, wu_ref[0])
        f = ((a * _sig(a)) * b).astype(bf16)
        a_ref[0] = a.astype(bf16)
        b_ref[0] = b.astype(bf16)
        f_ref[0] = f
        acc_ref[...] += _dot(f, wd_ref[0])

        @pl.when(j == nf - 1)
        def _():
            h2 = h1_ref[...] + acc_ref[...]
            r = lax.rsqrt(jnp.mean(h2 * h2, axis=-1, keepdims=True) + EPS)
            n = h2 * r
            gf = gf_ref[...]
            e = n * gf - tg_ref[...]
            dy = e * (1.0 / D)
            dn = dy * gf
            dh2_ref[...] = r * (dn - n * jnp.mean(dn * n, axis=-1, keepdims=True))
            part_ref[...] = jnp.zeros_like(part_ref)
            part_ref[0:1, :] = jnp.sum(dy * n, axis=0, keepdims=True)
            part_ref[1:2, :] = (0.5 / D) * jnp.sum(e * e, axis=0, keepdims=True)

    row = lambda n: pl.BlockSpec((tm, n), lambda i, j: (i, 0))
    vec = pl.BlockSpec((1, D), lambda i, j: (0, 0))
    col = pl.BlockSpec((1, tm, tf), lambda i, j: (j, i, 0))
    wcol = pl.BlockSpec((1, D, tf), lambda i, j: (j, 0, 0))
    act = jax.ShapeDtypeStruct((nf, T, tf), bf16)
    return pl.pallas_call(
        body, name="ffn_fwd", grid=(T // tm, nf),
        in_specs=[row(D), vec, wcol, wcol, pl.BlockSpec((1, tf, D), lambda i, j: (j, 0, 0)), vec, row(D)],
        out_specs=[row(D), col, col, col, row(D), pl.BlockSpec((8, D), lambda i, j: (i, 0))],
        out_shape=[jax.ShapeDtypeStruct((T, D), bf16), act, act, act,
                   jax.ShapeDtypeStruct((T, D), f32), jax.ShapeDtypeStruct((T // tm * 8, D), f32)],
        scratch_shapes=[pltpu.VMEM((tm, D), bf16), pltpu.VMEM((tm, D), f32)],
        compiler_params=_params(("parallel", "arbitrary")),
    )(h1, g_ffn, wg, wu, wd, g_fin, target)


def _ffn_bwd(dh2, h1, g_ffn, a, b, wg, wu, wd):
    T, D = h1.shape
    nf, _, tf = wg.shape
    tm = _tile(T, 512, 8)

    def body(dh2_ref, h1_ref, g_ref, a_ref, b_ref, wg_ref, wu_ref, wd_ref, da_ref, db_ref, dh1_ref, part_ref,
             ds_ref, acc_ref):
        j = pl.program_id(1)

        @pl.when(j == 0)
        def _():
            ds_ref[...] = dh2_ref[...].astype(bf16)
            acc_ref[...] = jnp.zeros_like(acc_ref)

        df = _dot_nt(ds_ref[...], wd_ref[0])
        av = a_ref[0].astype(f32)
        sg = _sig(av)
        db = (df * (av * sg)).astype(bf16)
        da = (df * b_ref[0].astype(f32) * (sg * (1.0 + av * (1.0 - sg)))).astype(bf16)
        da_ref[0] = da
        db_ref[0] = db
        acc_ref[...] += _dot_nt(da, wg_ref[0]) + _dot_nt(db, wu_ref[0])

        @pl.when(j == nf - 1)
        def _():
            h = h1_ref[...]
            r = lax.rsqrt(jnp.mean(h * h, axis=-1, keepdims=True) + EPS)
            n = h * r
            dhn = acc_ref[...]
            dn = dhn * g_ref[...]
            dh1_ref[...] = dh2_ref[...] + r * (dn - n * jnp.mean(dn * n, axis=-1, keepdims=True))
            part_ref[...] = jnp.zeros_like(part_ref)
            part_ref[0:1, :] = jnp.sum(dhn * n, axis=0, keepdims=True)

    row = lambda n: pl.BlockSpec((tm, n), lambda i, j: (i, 0))
    col = pl.BlockSpec((1, tm, tf), lambda i, j: (j, i, 0))
    wcol = pl.BlockSpec((1, D, tf), lambda i, j: (j, 0, 0))
    act = jax.ShapeDtypeStruct((nf, T, tf), bf16)
    return pl.pallas_call(
        body, name="ffn_bwd", grid=(T // tm, nf),
        in_specs=[row(D), row(D), pl.BlockSpec((1, D), lambda i, j: (0, 0)), col, col, wcol, wcol,
                  pl.BlockSpec((1, tf, D), lambda i, j: (j, 0, 0))],
        out_specs=[col, col, row(D), pl.BlockSpec((8, D), lambda i, j: (i, 0))],
        out_shape=[act, act, jax.ShapeDtypeStruct((T, D), f32), jax.ShapeDtypeStruct((T // tm * 8, D), f32)],
        scratch_shapes=[pltpu.VMEM((tm, D), bf16), pltpu.VMEM((tm, D), f32)],
        compiler_params=_params(("parallel", "arbitrary")),
    )(dh2, h1, g_ffn, a, b, wg, wu, wd)


def _mix_bwd(dh1, proj, yr, ya, b_gate, w_ro, w_ao, w_out, phase=None):
    T, D = dh1.shape
    tm = _tile(T, 512, 8)

    def body(dh1_ref, g0, g1, g2, g3, bg_ref, yr_ref, ya_ref, wro_ref, wao_ref, wo_ref,
             du_ref, dao_ref, dgl_ref, mix_ref, dyr_ref, dya_ref, part_ref):
        dmix = _dot_nt(dh1_ref[...].astype(bf16), wo_ref[...])
        gr, ga = _gates((g0, g1, g2, g3), bg_ref)
        yr = yr_ref[...].astype(f32)
        ya = ya_ref[...].astype(f32)
        dyr = (dmix * gr).astype(bf16)
        dya = (dmix * ga).astype(bf16)
        dgl = jnp.concatenate([dmix * yr * gr * (1.0 - gr), dmix * ya * ga * (1.0 - ga)], axis=1)
        du_ref[...] = _dot_nt(dyr, wro_ref[...]).astype(bf16)
        ns = wao_ref.shape[2]
        dao = _dot_nt(dya[:, :ns], wao_ref[0])
        for k in range(1, N_CHIPS):
            dao = dao + _dot_nt(dya[:, k * ns:(k + 1) * ns], wao_ref[k])
        dao_ref[...] = dao.astype(bf16)
        dgl_ref[...] = dgl.astype(bf16)
        mix_ref[...] = (gr * yr + ga * ya).astype(bf16)
        dyr_ref[...] = dyr
        dya_ref[...] = dya
        part_ref[...] = jnp.zeros_like(part_ref)
        part_ref[0:1, :] = jnp.sum(dgl, axis=0, keepdims=True)

    full = lambda a: pl.BlockSpec(a.shape, lambda i: (0,) * a.ndim)
    row = lambda n: pl.BlockSpec((tm, n), lambda i: (i, 0))
    return _call(
        body, phase, name="mix_bwd", grid=(T // tm,), scratch_shapes=[],
        in_specs=[row(D), *_gl_specs(tm), full(b_gate), row(D), row(D), full(w_ro), full(w_ao), full(w_out)],
        out_specs=[row(D), row(ATT_W), row(2 * D), row(D), row(D), row(D), pl.BlockSpec((8, 2 * D), lambda i: (i, 0))],
        out_shape=[jax.ShapeDtypeStruct((T, D), bf16), jax.ShapeDtypeStruct((T, ATT_W), bf16),
                   jax.ShapeDtypeStruct((T, 2 * D), bf16), jax.ShapeDtypeStruct((T, D), bf16),
                   jax.ShapeDtypeStruct((T, D), bf16), jax.ShapeDtypeStruct((T, D), bf16),
                   jax.ShapeDtypeStruct((T // tm * 8, 2 * D), f32)],
        args=(dh1, proj, proj, proj, proj, b_gate, yr, ya, w_ro, w_ao, w_out))


def _ret_bwd(proj, qr, kr, o, states, du, B, S, rope, decay, phase=None):
    T = B * S
    nc = S // CHUNK
    H, dk, dv = RET_HEADS, RET_KEY_DIM, RET_VAL_DIM

    sb = RET_CHUNKS * CHUNK
    ns = S // sb

    def body(qr_ref, kr_ref, v_ref, g_ref, o_ref, st_ref, du_ref, cos_ref, sin_ref, intra_ref, qd_ref, kd_ref, cd_ref,
             dq_ref, dk_ref, dv_ref, dg_ref, dstate_ref):
        @pl.when(pl.program_id(1) == 0)
        def _():
            dstate_ref[...] = jnp.zeros_like(dstate_ref)

        cos, snb = cos_ref[...], -sin_ref[...]
        dstates = [dstate_ref[h] for h in range(H)]
        for ci in reversed(range(RET_CHUNKS)):
            r = slice(ci * CHUNK, (ci + 1) * CHUNK)
            for h in range(H):
                hk, hv = slice(h * dk, (h + 1) * dk), slice(h * dv, (h + 1) * dv)
                intra, qd, kd = intra_ref[h], qd_ref[h], kd_ref[h]
                qi, ki, vi = qr_ref[r, hk], kr_ref[r, hk], v_ref[r, hv]
                si = st_ref[0, h, ci]
                o = o_ref[r, hv].astype(f32)
                mu = jnp.mean(o, axis=-1, keepdims=True)
                xc = o - mu
                rstd = lax.rsqrt(jnp.mean(xc * xc, axis=-1, keepdims=True) + EPS)
                oh = xc * rstd
                g = g_ref[r, hv].astype(f32)
                sg = _sig(g)
                dui = du_ref[r, hv].astype(f32)
                dg_ref[r, hv] = (dui * oh * (sg * (1.0 + g * (1.0 - sg)))).astype(bf16)
                doh = dui * (g * sg)
                do = rstd * (doh - jnp.mean(doh, axis=-1, keepdims=True)
                             - oh * jnp.mean(doh * oh, axis=-1, keepdims=True))
                dob = do.astype(bf16)
                p = (_dot_nt(qi, ki) * intra).astype(bf16)
                dsb = dstates[h].astype(bf16)
                kt = (ki.astype(f32) * kd).astype(bf16)
                qt = (qi.astype(f32) * qd).astype(bf16)
                dv_ref[r, hv] = (_dot_tn(p, dob) + _dot(kt, dsb)).astype(bf16)
                da = (_dot_nt(dob, vi) * intra).astype(bf16)
                dq = _dot(da, ki) + _dot_nt(dob, si) * qd
                dkk = (_dot_tn(da, qi) + _dot_nt(vi, dsb) * kd) * K_SCALE
                dq_ref[r, hk] = _rotate(dq, cos[r], snb[r]).astype(bf16)
                dk_ref[r, hk] = _rotate(dkk, cos[r], snb[r]).astype(bf16)
                dstates[h] = dstates[h] * cd_ref[h] + _dot_tn(qt, dob)
        for h in range(H):
            dstate_ref[h] = dstates[h]

    blk = lambda w, c: pl.BlockSpec((sb, w), lambda b, i: (b * ns + ns - 1 - i, c))
    return _call(
        body, phase, name="ret_bwd", grid=(B, ns),
        in_specs=[blk(H * dk, 0), blk(H * dk, 0), blk(H * dv, C_RV // (H * dv)), blk(H * dv, C_RG // (H * dv)),
                  blk(H * dv, 0),
                  pl.BlockSpec((1, H, RET_CHUNKS, dk, dv), lambda b, i: (b, 0, ns - 1 - i, 0, 0)),
                  blk(H * dv, 0),
                  pl.BlockSpec((sb, dk), lambda b, i: (ns - 1 - i, 0)), pl.BlockSpec((sb, dk), lambda b, i: (ns - 1 - i, 0)),
                  *_ret_tables_specs()],
        out_specs=[blk(H * dk, 0), blk(H * dk, 0), blk(H * dv, 0), blk(H * dv, 0)],
        out_shape=[jax.ShapeDtypeStruct((T, H * dk), bf16), jax.ShapeDtypeStruct((T, H * dk), bf16),
                   jax.ShapeDtypeStruct((T, H * dv), bf16), jax.ShapeDtypeStruct((T, H * dv), bf16)],
        scratch_shapes=[pltpu.VMEM((H, dk, dv), f32)],
        args=(qr, kr, proj, proj, o, states, du, *rope, *decay))


def _att_bwd(proj, dao, trows, B, S, phase=None):
    T = B * S
    nq = S // QBLK
    dh = ATT_HEAD_DIM
    scale = ATT_HEAD_DIM ** -0.5

    def body(q_ref, k_ref, v_ref, do_ref, t_ref, dq_ref, dk_ref, dv_ref, vec_ref, bias_ref, dbias_ref, dka_ref, dva_ref):
        b, i = pl.program_id(0), pl.program_id(1)

        @pl.when((b == 0) & (i == 0))
        def _():
            _build_bias(t_ref, bias_ref)
            dbias_ref[...] = jnp.zeros_like(dbias_ref)

        @pl.when(i == 0)
        def _():
            dka_ref[...] = jnp.zeros_like(dka_ref)
            dva_ref[...] = jnp.zeros_like(dva_ref)

        def step(nk):
            win = pl.ds(pl.multiple_of((i + 1) * QBLK - nk, QBLK), nk)
            kw, vw = k_ref[win, :], v_ref[win, :]
            dqs, dks, dvs = [], [], []
            for h in range(ATT_HEADS):
                hs = slice(h * dh, (h + 1) * dh)
                qh, kh, vh, doh = q_ref[:, hs], kw[:, hs], vw[:, hs], do_ref[:, hs]
                pr = _att_probs(qh, kh, bias_ref[h, :, KWIN - nk:])
                dp = _dot_nt(doh, vh)
                ds = pr * (dp - jnp.sum(pr * dp, axis=-1, keepdims=True))
                dbias_ref[h, :, KWIN - nk:] += ds
                dsb = (ds * scale).astype(bf16)
                dqs.append(_dot(dsb, kh))
                dks.append(_dot_tn(dsb, qh))
                dvs.append(_dot_tn(pr.astype(bf16), doh))
            dq_ref[...] = jnp.concatenate(dqs, axis=1).astype(bf16)
            dka_ref[win, :] += jnp.concatenate(dks, axis=1)
            dva_ref[win, :] += jnp.concatenate(dvs, axis=1)

        _by_window(i, step)

        @pl.when(i == nq - 1)
        def _():
            dk_ref[...] = dka_ref[...].astype(bf16)
            dv_ref[...] = dva_ref[...].astype(bf16)

        @pl.when((b == B - 1) & (i == nq - 1))
        def _():
            rr = lax.broadcasted_iota(jnp.int32, (QBLK, QBLK), 0)
            cc = lax.broadcasted_iota(jnp.int32, (QBLK, QBLK), 1)
            flip = jnp.where(rr + cc == QBLK - 1, 1.0, 0.0).astype(bf16)
            for h in range(ATT_HEADS):
                d = dbias_ref[h]
                hi = d.astype(bf16)
                lo = (d - hi.astype(f32)).astype(bf16)
                rev = _dot(flip, hi) + _dot(flip, lo)
                wide = jnp.concatenate([rev, jnp.zeros((QBLK, TOEP - KWIN), f32)], axis=1)
                rolled = pltpu.roll(wide, 0, 1, stride=1, stride_axis=0)
                vec_ref[h:h + 1, :] = jnp.sum(rolled, axis=0, keepdims=True)

    qspec = lambda c: pl.BlockSpec((QBLK, ATT_W), lambda b, i: (b * nq + i, c))
    kspec = lambda c: pl.BlockSpec((S, ATT_W), lambda b, i: (b, c))
    seq = jax.ShapeDtypeStruct((T, ATT_W), bf16)
    return _call(
        body, phase, name="att_bwd", grid=(B, nq),
        in_specs=[qspec(C_AQ // ATT_W), kspec(C_AK // ATT_W), kspec(C_AV // ATT_W), qspec(0),
                  pl.BlockSpec((ATT_HEADS, TOEP), lambda b, i: (0, 0))],
        out_specs=[qspec(0), kspec(0), kspec(0), pl.BlockSpec((ATT_HEADS, TOEP), lambda b, i: (0, 0))],
        out_shape=[seq, seq, seq, jax.ShapeDtypeStruct((ATT_HEADS, TOEP), f32)],
        scratch_shapes=[pltpu.VMEM((ATT_HEADS, QBLK, KWIN), f32), pltpu.VMEM((ATT_HEADS, QBLK, KWIN), f32),
                        pltpu.VMEM((S, ATT_W), f32), pltpu.VMEM((S, ATT_W), f32)],
        args=(proj, proj, proj, dao, trows))


def _in_proj_bwd(dproj, w_in, x2, gamma, dh1, phase=None):
    T, D = x2.shape
    nk, _, tk = w_in.shape
    tm = _tile(T, 512, 8)

    def body(dp_ref, w_ref, x_ref, g_ref, dh1_ref, dx_ref, part_ref, acc_ref):
        j = pl.program_id(1)

        @pl.when(j == 0)
        def _():
            acc_ref[...] = jnp.zeros_like(acc_ref)

        acc_ref[...] += _dot_nt(dp_ref[...], w_ref[0])

        @pl.when(j == nk - 1)
        def _():
            x = x_ref[...]
            r = lax.rsqrt(jnp.mean(x * x, axis=-1, keepdims=True) + EPS)
            n = x * r
            dxn = acc_ref[...]
            dn = dxn * g_ref[...]
            dx_ref[...] = dh1_ref[...] + r * (dn - n * jnp.mean(dn * n, axis=-1, keepdims=True))
            part_ref[...] = jnp.zeros_like(part_ref)
            part_ref[0:1, :] = jnp.sum(dxn * n, axis=0, keepdims=True)

    row = lambda n: pl.BlockSpec((tm, n), lambda i, j: (i, 0))
    return _call(
        body, phase, name="in_proj_bwd", grid=(T // tm, nk),
        in_specs=[pl.BlockSpec((tm, tk), lambda i, j: (i, j)), pl.BlockSpec((1, D, tk), lambda i, j: (j, 0, 0)), row(D),
                  pl.BlockSpec((1, D), lambda i, j: (0, 0)), row(D)],
        out_specs=[row(D), pl.BlockSpec((8, D), lambda i, j: (i, 0))],
        out_shape=[jax.ShapeDtypeStruct((T, D), f32), jax.ShapeDtypeStruct((T // tm * 8, D), f32)],
        scratch_shapes=[pltpu.VMEM((tm, D), f32)],
        args=(dproj, w_in, x2, gamma, dh1))


def _wgrad(a, b, shard_axis, name, phase=None):
    def spec(arr, sharded, tt):
        if arr.ndim == 3:
            return arr.shape[2], pl.BlockSpec((1, tt, arr.shape[2]), lambda s, t: (s, t, 0))
        if sharded:
            w = arr.shape[1] // N_CHIPS
            return w, pl.BlockSpec((tt, w), lambda s, t: (t, s))
        return arr.shape[1], pl.BlockSpec((tt, arr.shape[1]), lambda s, t: (t, 0))

    T = a.shape[-2]
    tt = _tile(T, 512, 16)
    nt = T // tt
    whole = a.ndim == 2 and b.ndim == 2 and a.shape[1] * b.shape[1] * 4 <= WGRAD_ACC_BYTES
    if whole:
        K, N = a.shape[1], b.shape[1]
        a_spec, b_spec = pl.BlockSpec((tt, K), lambda s, t: (t, 0)), pl.BlockSpec((tt, N), lambda s, t: (t, 0))
        out_block = (N_CHIPS, K // N_CHIPS, N) if shard_axis == 0 else (N_CHIPS, K, N // N_CHIPS)
        out_spec = pl.BlockSpec(out_block, lambda s, t: (0, 0, 0))
    else:
        K, a_spec = spec(a, shard_axis == 0, tt)
        N, b_spec = spec(b, shard_axis == 1, tt)
        out_block = (N_CHIPS, K, N)
        out_spec = pl.BlockSpec((1, K, N), lambda s, t: (s, 0, 0))

    def body(a_ref, b_ref, o_ref, acc_ref):
        t = pl.program_id(1)

        @pl.when(t == 0)
        def _():
            acc_ref[...] = jnp.zeros_like(acc_ref)

        av = a_ref[0] if a.ndim == 3 else a_ref[...]
        bv = b_ref[0] if b.ndim == 3 else b_ref[...]
        acc_ref[...] += _dot_tn(av.astype(bf16), bv.astype(bf16))

        @pl.when(t == nt - 1)
        def _():
            if not whole:
                o_ref[0] = acc_ref[...].astype(bf16)
            else:
                _, kk, nn = out_block
                for s in range(N_CHIPS):
                    o_ref[s] = (acc_ref[s * kk:(s + 1) * kk, :] if shard_axis == 0
                                else acc_ref[:, s * nn:(s + 1) * nn]).astype(bf16)

    (grad,), carried = _call(
        body, phase, name=name, grid=(1 if whole else N_CHIPS, nt), in_specs=[a_spec, b_spec], out_specs=[out_spec],
        out_shape=[jax.ShapeDtypeStruct(out_block, bf16)], scratch_shapes=[pltpu.VMEM((K, N), f32)], args=(a, b))
    return grad, carried


def _adamw_sum(place, w, m, v, part, from_chips, from_sibling, name):
    R, C = w.shape
    half = R // 2
    tr = _tile(half, max(16, (1 << 18) // C // 16 * 16), 16)
    nr = half // tr

    def body(p_ref, w_ref, m_ref, v_ref, part_ref, fc_ref, fs_ref, g_ref, d_ref, mo_ref, vo_ref):
        up = lambda x: x.astype(f32)
        mine = ((up(part_ref[0]) + up(fc_ref[0])) + up(fc_ref[1])) + up(fc_ref[2])
        sibs = ((up(fs_ref[0]) + up(fs_ref[1])) + up(fs_ref[2])) + up(fs_ref[3])
        g_ = jnp.where(pl.program_id(0) == p_ref[0], mine, sibs)
        m_ = ADAM_B1 * m_ref[...] + (1.0 - ADAM_B1) * g_
        v_ = ADAM_B2 * v_ref[...] + (1.0 - ADAM_B2) * (g_ * g_)
        m_hat = m_ / (1.0 - ADAM_B1 ** ADAM_STEP)
        v_hat = v_ / (1.0 - ADAM_B2 ** ADAM_STEP)
        g_ref[...] = g_
        d_ref[...] = -ADAM_LR * (m_hat / (jnp.sqrt(v_hat) + ADAM_EPS) + ADAM_WD * w_ref[...])
        mo_ref[...] = m_
        vo_ref[...] = v_

    spec = pl.BlockSpec((tr, C), lambda h, r, p: (h * nr + r, 0))
    return pl.pallas_call(
        body, name=name,
        grid_spec=pltpu.PrefetchScalarGridSpec(
            num_scalar_prefetch=1, grid=(2, nr),
            in_specs=[spec, spec, spec, pl.BlockSpec((1, tr, C), lambda h, r, p: (p[1], r, 0)),
                      pl.BlockSpec((3, tr, C), lambda h, r, p: (0, r, 0)),
                      pl.BlockSpec((4, tr, C), lambda h, r, p: (0, r, 0))],
            out_specs=[spec] * 4),
        out_shape=[jax.ShapeDtypeStruct((R, C), f32)] * 4,
        compiler_params=_params(("parallel", "parallel")),
    )(place, w, m, v, part, from_chips, from_sibling)


def _adamw(w, g, m, v, name):
    R, C = w.shape
    tr = _tile(R, max(8, (1 << 18) // C // 8 * 8), 8)

    def body(w_ref, g_ref, m_ref, v_ref, d_ref, mo_ref, vo_ref):
        g_ = g_ref[...]
        m_ = ADAM_B1 * m_ref[...] + (1.0 - ADAM_B1) * g_
        v_ = ADAM_B2 * v_ref[...] + (1.0 - ADAM_B2) * (g_ * g_)
        m_hat = m_ / (1.0 - ADAM_B1 ** ADAM_STEP)
        v_hat = v_ / (1.0 - ADAM_B2 ** ADAM_STEP)
        d_ref[...] = -ADAM_LR * (m_hat / (jnp.sqrt(v_hat) + ADAM_EPS) + ADAM_WD * w_ref[...])
        mo_ref[...] = m_
        vo_ref[...] = v_

    spec = pl.BlockSpec((tr, C), lambda i: (i, 0))
    return pl.pallas_call(
        body, name=name, grid=(R // tr,), in_specs=[spec] * 4, out_specs=[spec] * 3,
        out_shape=[jax.ShapeDtypeStruct((R, C), f32)] * 3,
        compiler_params=_params(("parallel",)),
    )(w, g, m, v)


def _place():
    return lax.axis_index("x"), lax.axis_index("y"), lax.axis_index("c")


def _other_chips(x, y):
    chips = [(1 - x, y), (x, 1 - y), (1 - x, 1 - y)]
    return chips, [2 * cx + cy for cx, cy in chips]


def _exchange_small(blk, name, reduce):
    R, C = blk.shape

    def body(x_ref, out_ref, *rest):
        if reduce:
            all_ref, send_sems, recv_sems = rest
        else:
            all_ref = out_ref
            send_sems, recv_sems = rest
        x, y, c = _place()
        me = 4 * x + 2 * y + c
        all_ref[me] = x_ref[...]
        copies = []
        for k in range(1, N_DEV):
            peer = tuple(1 - p if (k >> s) & 1 else p for p, s in ((x, 2), (y, 1), (c, 0)))
            cp = pltpu.make_async_remote_copy(src_ref=x_ref, dst_ref=all_ref.at[me], send_sem=send_sems.at[k - 1],
                                              recv_sem=recv_sems.at[k - 1], device_id=peer, device_id_type=MESH)
            cp.start()
            copies.append(cp)
        for cp in copies:
            cp.wait()
        if reduce:
            tot = all_ref[0]
            for d in range(1, N_DEV):
                tot = tot + all_ref[d]
            out_ref[...] = tot

    vm = pl.BlockSpec(memory_space=pltpu.VMEM)
    scratch = [pltpu.SemaphoreType.DMA((N_DEV - 1,)), pltpu.SemaphoreType.DMA((N_DEV - 1,))]
    if reduce:
        scratch = [pltpu.VMEM((N_DEV, R, C), f32)] + scratch
    return pl.pallas_call(
        body, name=name, in_specs=[vm], out_specs=vm,
        out_shape=jax.ShapeDtypeStruct((R, C) if reduce else (N_DEV, R, C), f32),
        scratch_shapes=scratch,
    )(blk)


def _cast_shard(place, w, name):
    R, C = w.shape
    tr = _tile(R, max(16, (1 << 19) // C // 16 * 16), 16)

    def body(p_ref, w_ref, o_ref):
        o_ref[0] = w_ref[...].astype(bf16)

    return pl.pallas_call(
        body, name=name,
        grid_spec=pltpu.PrefetchScalarGridSpec(
            num_scalar_prefetch=1, grid=(R // tr,),
            in_specs=[pl.BlockSpec((tr, C), lambda r, p: (r, 0))],
            out_specs=pl.BlockSpec((1, tr, C), lambda r, p: (p[1], r, 0))),
        out_shape=jax.ShapeDtypeStruct((N_CHIPS, R, C), bf16),
        compiler_params=_params(("parallel",)),
    )(place, w)


class _Phase:
    def __init__(self, arrays, out_shapes, aliases, n_copies, copies, arrivals):
        self.arrays, self.out_shapes, self.aliases = list(arrays), list(out_shapes), dict(aliases)
        self.n_copies, self.copies, self.arrivals = n_copies, copies, arrivals

    def sems(self):
        return [pltpu.SemaphoreType.DMA((self.n_copies,)), pltpu.SemaphoreType.DMA((self.n_copies,))]

    def _descriptors(self, pin, pout, send_sems, recv_sems):
        return [pltpu.make_async_remote_copy(src_ref=s, dst_ref=d, send_sem=send_sems.at[i], recv_sem=recv_sems.at[i],
                                             device_id=to, device_id_type=MESH)
                for i, (s, d, to) in enumerate(self.copies(pin, pout))]

    def start(self, pin, pout, send_sems, recv_sems):
        for cp in self._descriptors(pin, pout, send_sems, recv_sems):
            cp.start()

    def finish(self, pin, pout, send_sems, recv_sems):
        sent = self._descriptors(pin, pout, send_sems, recv_sems)
        for i, dst in enumerate(self.arrivals(pin, pout)):
            pltpu.make_async_remote_copy(src_ref=dst, dst_ref=dst, send_sem=send_sems.at[i], recv_sem=recv_sems.at[i],
                                         device_id=_place(), device_id_type=MESH).wait_recv()
        for cp in sent:
            cp.wait_send()


def _join(phases):
    if len(phases) == 1:
        return phases[0]
    ai = np.cumsum([0] + [len(p.arrays) for p in phases])
    oi = np.cumsum([0] + [len(p.out_shapes) for p in phases])

    def each(fn_name, pin, pout):
        return [item for k, p in enumerate(phases)
                for item in getattr(p, fn_name)(pin[ai[k]:ai[k + 1]], pout[oi[k]:oi[k + 1]])]

    aliases = {int(ai[k]) + i: int(oi[k]) + j for k, p in enumerate(phases) for i, j in p.aliases.items()}
    return _Phase([a for p in phases for a in p.arrays], [s for p in phases for s in p.out_shapes], aliases,
                  sum(p.n_copies for p in phases), functools.partial(each, "copies"), functools.partial(each, "arrivals"))


def _call(body, phase, *, name, grid, in_specs, out_specs, out_shape, scratch_shapes, args):
    seq = _params(("arbitrary",) * len(grid))
    if phase is None:
        res = pl.pallas_call(body, name=name, grid=grid, in_specs=in_specs, out_specs=out_specs, out_shape=out_shape,
                             scratch_shapes=scratch_shapes, compiler_params=seq)(*args)
        return list(res), []
    ni, no, ns = len(in_specs), len(out_specs), len(scratch_shapes)
    pi, po = len(phase.arrays), len(phase.out_shapes)

    def hosted(*refs):
        cut = np.cumsum([ni, pi, no, po, ns])
        ins, pin, outs, pout, scr, sems = (refs[a:b] for a, b in zip([0, *cut], [*cut, len(refs)]))
        ids = [pl.program_id(d) for d in range(len(grid))]
        first = functools.reduce(lambda p, q: p & q, [i == 0 for i in ids])
        last = functools.reduce(lambda p, q: p & q, [i == g - 1 for i, g in zip(ids, grid)])
        pl.when(first)(lambda: phase.start(pin, pout, *sems))
        body(*ins, *outs, *scr)
        pl.when(last)(lambda: phase.finish(pin, pout, *sems))

    anyspace = pl.BlockSpec(memory_space=pl.ANY)
    res = pl.pallas_call(
        hosted, name=name, grid=grid, in_specs=list(in_specs) + [anyspace] * pi,
        out_specs=list(out_specs) + [anyspace] * po, out_shape=list(out_shape) + phase.out_shapes,
        input_output_aliases={ni + i: no + j for i, j in phase.aliases.items()},
        scratch_shapes=list(scratch_shapes) + phase.sems(), compiler_params=seq,
    )(*args, *phase.arrays)
    return list(res[:no]), list(res[no:])


def _run_phases(name, phases):
    first = phases[0]
    pi, po = len(first.arrays), len(first.out_shapes)

    def body(*refs):
        pin, pout, sems = refs[:pi], refs[pi:pi + po], refs[pi + po:]
        for n, ph in enumerate(phases):
            ph.start(pin, pout, *sems[2 * n:2 * n + 2])
            ph.finish(pin, pout, *sems[2 * n:2 * n + 2])

    anyspace = pl.BlockSpec(memory_space=pl.ANY)
    return list(pl.pallas_call(
        body, name=name, in_specs=[anyspace] * pi, out_specs=[anyspace] * po, out_shape=first.out_shapes,
        input_output_aliases=first.aliases, scratch_shapes=[s for ph in phases for s in ph.sems()],
    )(*first.arrays))


def _half_rows(buf, c):
    half = buf.shape[1] // 2
    return pl.ds(c * half, half), pl.ds((1 - c) * half, half)


def _gather_phase(bufs, over_ici):
    n = len(bufs)
    shapes = [jax.ShapeDtypeStruct(b.shape, b.dtype) for b in bufs]

    def landed(out, which):
        x, y, c = _place()
        _, ks = _other_chips(x, y)
        return [out[a].at[ks[j], _half_rows(bufs[a], c)[which]] for a in range(n) for j in range(3)]

    def ici(pin, out):
        x, y, c = _place()
        chips, _ = _other_chips(x, y)
        mine = [out[a].at[2 * x + y, _half_rows(bufs[a], c)[0]] for a in range(n)]
        return [(mine[a], mine[a], (*chips[j], c)) for a in range(n) for j in range(3)]

    def d2d(pin, out):
        x, y, c = _place()
        return [(dst, dst, (x, y, 1 - c)) for dst in landed(out, 0)]

    if over_ici:
        return _Phase(bufs, shapes, {a: a for a in range(n)}, 3 * n, ici, lambda pin, out: landed(out, 0))
    return _Phase(bufs, shapes, {a: a for a in range(n)}, 3 * n, d2d, lambda pin, out: landed(out, 1))


def _rs_sibling(grads, name):
    n = len(grads)

    def body(*refs):
        g, out, send_sems, recv_sems = refs[:n], refs[n:2 * n], refs[2 * n], refs[2 * n + 1]
        x, y, c = _place()
        copies = []
        for a in range(n):
            half = grads[a].shape[1] // 2
            cp = pltpu.make_async_remote_copy(src_ref=g[a].at[:, pl.ds((1 - c) * half, half)], dst_ref=out[a],
                                              send_sem=send_sems.at[a], recv_sem=recv_sems.at[a],
                                              device_id=(x, y, 1 - c), device_id_type=MESH)
            cp.start()
            copies.append(cp)
        for cp in copies:
            cp.wait()

    anyspace = pl.BlockSpec(memory_space=pl.ANY)
    return pl.pallas_call(
        body, name=name, in_specs=[anyspace] * n, out_specs=[anyspace] * n,
        out_shape=[jax.ShapeDtypeStruct((N_CHIPS, g.shape[1] // 2, g.shape[2]), g.dtype) for g in grads],
        scratch_shapes=[pltpu.SemaphoreType.DMA((n,)), pltpu.SemaphoreType.DMA((n,))],
    )(*grads)


def _rs_add_sibling(place, grad, got, name):
    _, R, C = grad.shape
    half = R // 2
    tr = _tile(half, max(16, (1 << 19) // C // 16 * 16), 16)
    nr = half // tr

    def body(p_ref, a_ref, b_ref, o_ref):
        o_ref[...] = (a_ref[...].astype(f32) + b_ref[...].astype(f32)).astype(o_ref.dtype)

    return pl.pallas_call(
        body, name=name,
        grid_spec=pltpu.PrefetchScalarGridSpec(
            num_scalar_prefetch=1, grid=(N_CHIPS, nr),
            in_specs=[pl.BlockSpec((1, tr, C), lambda k, r, p: (k, p[0] * nr + r, 0)),
                      pl.BlockSpec((1, tr, C), lambda k, r, p: (k, r, 0))],
            out_specs=pl.BlockSpec((1, tr, C), lambda k, r, p: (k, r, 0))),
        out_shape=jax.ShapeDtypeStruct((N_CHIPS, half, C), bf16),
        compiler_params=_params(("parallel", "parallel")),
    )(place, grad, got)


def _rs_chips_phase(parts):
    n = len(parts)

    def copies(p, fc):
        x, y, c = _place()
        chips, ks = _other_chips(x, y)
        return [(p[a].at[ks[j]], fc[a].at[j], (*chips[j], c)) for a in range(n) for j in range(3)]

    shapes = [jax.ShapeDtypeStruct((3,) + q.shape[1:], q.dtype) for q in parts]
    return _Phase(parts, shapes, {}, 3 * n, copies, lambda p, fc: [fc[a].at[j] for a in range(n) for j in range(3)])


def _rs_hand_phase(parts, from_chips):
    n = len(parts)

    def copies(pin, fs):
        x, y, c = _place()
        sib = (x, y, 1 - c)
        own = [(pin[a].at[2 * x + y], fs[a].at[0], sib) for a in range(n)]
        return own + [(pin[n + a].at[j], fs[a].at[1 + j], sib) for a in range(n) for j in range(3)]

    def arrivals(pin, fs):
        return [fs[a].at[0] for a in range(n)] + [fs[a].at[1 + j] for a in range(n) for j in range(3)]

    shapes = [jax.ShapeDtypeStruct((4,) + q.shape[1:], q.dtype) for q in parts]
    return _Phase(list(parts) + list(from_chips), shapes, {}, 4 * n, copies, arrivals)


class _Exchange:
    def __init__(self, place):
        self.place = place

    def gather(self, bufs, over_ici):
        return _gather_phase(bufs, over_ici)

    def pair_sums(self, names, grads):
        got = _rs_sibling(grads, "rs_sibling_" + names[0])
        return [_rs_add_sibling(self.place, g, r, "rs_add_" + n) for n, g, r in zip(names, grads, got)]

    def to_chips(self, parts):
        return _rs_chips_phase(parts)

    def to_sibling(self, parts, from_chips):
        return _rs_hand_phase(parts, from_chips)

    def hand_over(self, name, parts, from_chips):
        return _run_phases(name, [_rs_hand_phase(parts, from_chips)])


def _local_step(x, target, norm_mix, b_gate, rb_full, norm_ffn, norm_final, w_in, rest, exch):
    B, S, D = x.shape
    T = B * S
    x2 = x.reshape(T, D)
    tg2 = target.reshape(T, D)
    rope, decay = _rope_tables(S), _decay_tables()
    trows = _bias_rows(rb_full)
    g_fin = norm_final.reshape(1, D)

    early = [n for n in rest if n != "w_ffn_down"]
    (xn, proj), got = _in_proj(x2, norm_mix, w_in, exch.gather([rest[n] for n in early], True))
    (qr, kr, o, u, states), got = _ret_fwd(proj, B, S, rope, decay, _join([exch.gather([rest["w_ffn_down"]], True),
                                                                         exch.gather(got, False)]))
    wb = dict(zip(early, got[1:]))
    (ao,), (wb["w_ffn_down"],) = _att_fwd(proj, trows, B, S, exch.gather(got[:1], False))
    w_ro, w_out = wb["w_ret_out"].reshape(-1, D), wb["w_out"].reshape(-1, D)
    h1, yr, ya = _mix_fwd(x2, proj, u, ao, b_gate, w_ro, wb["w_att_out"], w_out)
    hn, a, b, f, dh2, part_fin = _ffn_fwd(h1, norm_ffn, wb["w_ffn_gate"], wb["w_ffn_up"], wb["w_ffn_down"], g_fin, tg2)

    da, db, dh1, part_ffn = _ffn_bwd(dh2, h1, norm_ffn, a, b, wb["w_ffn_gate"], wb["w_ffn_up"], wb["w_ffn_down"])
    ffn = ["w_ffn_down", "w_ffn_gate", "w_ffn_up"]
    p_ffn = exch.pair_sums(ffn, [_wgrad(f, dh2, 0, "wgrad_ffn_down")[0], _wgrad(hn, da, 1, "wgrad_ffn_gate")[0],
                                 _wgrad(hn, db, 1, "wgrad_ffn_up")[0]])
    (du, dao, dgl, mix, dyr, dya, part_bg), c_down = _mix_bwd(dh1, proj, yr, ya, b_gate, w_ro, wb["w_att_out"], w_out,
                                                               exch.to_chips(p_ffn[:1]))
    mrg = ["w_out", "w_ret_out", "w_att_out"]
    p_mrg = exch.pair_sums(mrg, [_wgrad(mix, dh1, 0, "wgrad_out")[0], _wgrad(u, dyr, 0, "wgrad_ret_out")[0],
                                 _wgrad(ao, dya, 1, "wgrad_att_out")[0]])
    (drq, drk, drv, drg), c_gate_up = _ret_bwd(proj, qr, kr, o, states, du, B, S, rope, decay, exch.to_chips(p_ffn[1:]))
    c_ffn = c_down + c_gate_up
    (daq, dak, dav, dvec), got = _att_bwd(proj, dao, trows, B, S, _join([exch.to_chips(p_mrg),
                                                                        exch.to_sibling(p_ffn, c_ffn)]))
    c_mrg, s_ffn = got[:len(mrg)], got[len(mrg):]
    dproj = jnp.concatenate([drq, drk, drv, drg, daq, dak, dav, dgl], axis=1)
    g_in, s_mrg = _wgrad(xn, dproj, 1, "wgrad_in", exch.to_sibling(p_mrg, c_mrg))
    p_in = exch.pair_sums(["w_in"], [g_in])
    (gx, part_mix), c_in = _in_proj_bwd(dproj, w_in, x2, norm_mix, dh1, exch.to_chips(p_in))
    s_in = exch.hand_over("rs_hand_w_in", p_in, c_in)
    gbig = dict(zip(ffn + mrg + ["w_in"], zip(p_ffn + p_mrg + p_in, c_ffn + c_mrg + c_in, s_ffn + s_mrg + s_in)))
    rows = lambda p, r: p.reshape(-1, 8, p.shape[-1])[:, r, :].sum(axis=0)
    lo = KWIN - 1 - (MAX_REL - 1)
    drb = jnp.concatenate([jnp.flip(dvec[:, lo:lo + N_REL - 1], axis=1), dvec[:, :lo].sum(axis=1, keepdims=True)], axis=1)
    gsmall = {
        "norm_mix": rows(part_mix, 0), "b_gate": rows(part_bg, 0), "rel_bias": drb, "norm_ffn": rows(part_ffn, 0),
        "norm_final": rows(part_fin, 0),
    }
    return rows(part_fin, 1), gx.reshape(B, S, D), gbig, gsmall


SMALL_ROWS = 16


def _pack_small(gs, loss_lanes):
    D = D_MODEL
    rb = jnp.pad(gs["rel_bias"].reshape(-1), (0, 3 * D - ATT_HEADS * N_REL)).reshape(3, D)
    rows = [gs["norm_mix"].reshape(1, D), gs["b_gate"].reshape(2, D), gs["norm_ffn"].reshape(1, D),
            gs["norm_final"].reshape(1, D), rb, loss_lanes.reshape(1, D)]
    used = sum(r.shape[0] for r in rows)
    return jnp.concatenate(rows + [jnp.zeros((SMALL_ROWS - used, D), f32)], axis=0)


def kernel(x, norm_mix, w_in, b_gate, rel_bias, w_ret_out, w_att_out, w_out, norm_ffn, w_ffn_gate, w_ffn_up, w_ffn_down, norm_final, loss_target, m_norm_mix, m_w_in, m_b_gate, m_rel_bias, m_w_ret_out, m_w_att_out, m_w_out, m_norm_ffn, m_w_ffn_gate, m_w_ffn_up, m_w_ffn_down, m_norm_final, v_norm_mix, v_w_in, v_b_gate, v_rel_bias, v_w_ret_out, v_w_att_out, v_w_out, v_norm_ffn, v_w_ffn_gate, v_w_ffn_up, v_w_ffn_down, v_norm_final):
    w = dict(norm_mix=norm_mix, w_in=w_in, b_gate=b_gate, rel_bias=rel_bias, w_ret_out=w_ret_out, w_att_out=w_att_out,
             w_out=w_out, norm_ffn=norm_ffn, w_ffn_gate=w_ffn_gate, w_ffn_up=w_ffn_up, w_ffn_down=w_ffn_down,
             norm_final=norm_final)
    m = dict(norm_mix=m_norm_mix, w_in=m_w_in, b_gate=m_b_gate, rel_bias=m_rel_bias, w_ret_out=m_w_ret_out,
             w_att_out=m_w_att_out, w_out=m_w_out, norm_ffn=m_norm_ffn, w_ffn_gate=m_w_ffn_gate, w_ffn_up=m_w_ffn_up,
             w_ffn_down=m_w_ffn_down, norm_final=m_norm_final)
    v = dict(norm_mix=v_norm_mix, w_in=v_w_in, b_gate=v_b_gate, rel_bias=v_rel_bias, w_ret_out=v_w_ret_out,
             w_att_out=v_w_att_out, w_out=v_w_out, norm_ffn=v_norm_ffn, w_ffn_gate=v_w_ffn_gate, w_ffn_up=v_w_ffn_up,
             w_ffn_down=v_w_ffn_down, norm_final=v_norm_final)
    xi, yi, ci = _place()
    k_me = 2 * xi + yi

    place = jnp.stack([ci, k_me]).astype(jnp.int32)
    big = [n for n, _ in BIG]

    bufs = {n: _cast_shard(place, w[n][0], "cast_" + n) for n in big}
    w_in_all, = _run_phases("gather_w_in", [_gather_phase([bufs["w_in"]], True), _gather_phase([bufs["w_in"]], False)])
    rest = {n: bufs[n] for n in big if n != "w_in"}
    nrel_loc = rel_bias.shape[-1]
    rb_all = _exchange_small(jnp.pad(rel_bias[0], ((0, 0), (0, 128 - nrel_loc))), "gather_rel_bias", False)
    rb_full = jnp.concatenate([rb_all[2 * k, :, :nrel_loc] for k in range(N_CHIPS)], axis=1)

    loss_lanes, grad_x, gbig, gsmall = _local_step(x, loss_target, norm_mix, b_gate, rb_full, norm_ffn, norm_final,
                                                   w_in_all, rest, _Exchange(place))

    small = _exchange_small(_pack_small(gsmall, loss_lanes), "reduce_small", True)
    D = D_MODEL
    loss = jnp.sum(small[8])
    drb_full = small[5:8].reshape(-1)[:ATT_HEADS * N_REL].reshape(ATT_HEADS, N_REL)
    g = {
        "norm_mix": small[0:1], "b_gate": small[1:3].reshape(1, 2 * D), "norm_ffn": small[3:4], "norm_final": small[4],
        "rel_bias": lax.dynamic_slice_in_dim(drb_full, k_me * nrel_loc, nrel_loc, axis=1)[None],
    }

    delta, new_m, new_v = {}, {}, {}
    for n in big:
        g_, d_, m_, v_ = _adamw_sum(place, w[n][0], m[n][0], v[n][0], *gbig[n], "adamw_" + n)
        g[n], delta[n], new_m[n], new_v[n] = g_[None], d_[None], m_[None], v_[None]
    flat = lambda d: jnp.concatenate([d[n].reshape(-1) for n in SMALL])
    n_small = sum(int(np.prod(w[n].shape)) for n in SMALL)
    n_pad = -n_small % 1024
    packs = [jnp.pad(flat(d), (0, n_pad)).reshape(-1, 128) for d in (w, g, m, v)]
    outs = _adamw(*packs, "adamw_small")
    for res, dst in zip(outs, (delta, new_m, new_v)):
        off = 0
        fl = res.reshape(-1)
        for n in SMALL:
            sz = int(np.prod(w[n].shape))
            dst[n] = fl[off:off + sz].reshape(w[n].shape)
            off += sz

    return (loss, grad_x, *[g[n] for n in WEIGHTS], *[delta[n] for n in WEIGHTS], *[new_m[n] for n in WEIGHTS],
            *[new_v[n] for n in WEIGHTS])
```

```python
import functools

import numpy as np
import jax
import jax.numpy as jnp
from jax import lax
from jax.experimental import pallas as pl
from jax.experimental.pallas import tpu as pltpu

f32 = jnp.float32
bf16 = jnp.bfloat16

D_MODEL = 1024
CHUNK = 64
RET_HEADS = 4
RET_KEY_DIM = 128
RET_VAL_DIM = 256
ATT_HEADS = 8
ATT_HEAD_DIM = 64
ATT_W = ATT_HEADS * ATT_HEAD_DIM
BAND_CHUNKS = 8
PAD = BAND_CHUNKS * CHUNK
MAX_REL = 256
N_REL = CHUNK + MAX_REL
D_FF = 2816
N_IN = 6656
ROPE_BASE = 10000.0
EPS = 1e-6
NEG_INF = -1e30
C_RQ, C_RK, C_RV, C_RG, C_AQ, C_AK, C_AV, C_GL = 0, 512, 1024, 2048, 3072, 3584, 4096, 4608

ADAM_LR, ADAM_B1, ADAM_B2, ADAM_EPS, ADAM_WD, ADAM_STEP = 0.001, 0.9, 0.999, 1e-08, 0.01, 10

N_CHIPS = 4
N_DEV = 8
WGRAD_ACC_BYTES = 8 * 1024 * 1024
ROW_TILE = 512
BIG_ROW_TILE = 1024
QBLK = 256
KWIN = PAD + QBLK
TOEP = 1024
VMEM_LIMIT = 56 * 1024 * 1024
MESH = pl.DeviceIdType.MESH

BIG = (
    ("w_in", 1), ("w_ret_out", 0), ("w_att_out", 1), ("w_out", 0), ("w_ffn_gate", 1), ("w_ffn_up", 1), ("w_ffn_down", 0))
WEIGHTS = ("norm_mix", "w_in", "b_gate", "rel_bias", "w_ret_out", "w_att_out", "w_out", "norm_ffn", "w_ffn_gate",
           "w_ffn_up", "w_ffn_down", "norm_final")
SMALL = ("norm_mix", "b_gate", "rel_bias", "norm_ffn", "norm_final")


def _dot(a, b):
    return lax.dot_general(a, b, (((1,), (0,)), ((), ())), preferred_element_type=f32)


def _dot_nt(a, b):
    return lax.dot_general(a, b, (((1,), (1,)), ((), ())), preferred_element_type=f32)


def _dot_tn(a, b):
    return lax.dot_general(a, b, (((0,), (0,)), ((), ())), preferred_element_type=f32)


def _sig(x):
    return 1.0 / (1.0 + jnp.exp(-x))


def _tile(n, pref, mult):
    best = None
    for t in range(mult, min(n, pref) + 1, mult):
        if n % t == 0:
            best = t
    return best if best is not None else n


def _params(sem, vmem=VMEM_LIMIT):
    return pltpu.CompilerParams(dimension_semantics=sem, vmem_limit_bytes=vmem)


def _in_proj(x2, gamma, w, phase=None):
    T, D = x2.shape
    ns = w.shape[2]
    tm = _tile(T, BIG_ROW_TILE, 8)

    def body(x_ref, g_ref, w_ref, xn_ref, p_ref, xs_ref):
        @pl.when(pl.program_id(1) == 0)
        def _():
            x = x_ref[...]
            r = lax.rsqrt(jnp.mean(x * x, axis=-1, keepdims=True) + EPS)
            xn = (x * r * g_ref[...]).astype(bf16)
            xs_ref[...] = xn
            xn_ref[...] = xn

        p_ref[...] = _dot(xs_ref[...], w_ref[0]).astype(bf16)

    return _call(
        body, phase, name="in_proj", grid=(T // tm, N_CHIPS),
        in_specs=[pl.BlockSpec((tm, D), lambda i, j: (i, 0)), pl.BlockSpec((1, D), lambda i, j: (0, 0)),
                  pl.BlockSpec((1, D, ns), lambda i, j: (j, 0, 0))],
        out_specs=[pl.BlockSpec((tm, D), lambda i, j: (i, 0)), pl.BlockSpec((tm, ns), lambda i, j: (i, j))],
        out_shape=[jax.ShapeDtypeStruct((T, D), bf16), jax.ShapeDtypeStruct((T, N_CHIPS * ns), bf16)],
        scratch_shapes=[pltpu.VMEM((tm, D), bf16)],
        args=(x2, gamma, w))


def _rope_tables(S):
    d = RET_KEY_DIM
    freqs = ROPE_BASE ** (-jnp.arange(0, d, 2, dtype=f32) / d)
    ang = jnp.arange(S, dtype=f32)[:, None] * freqs[None, :]
    cos, sin = jnp.cos(ang), jnp.sin(ang)
    return jnp.concatenate([cos, cos], axis=1), jnp.concatenate([-sin, sin], axis=1)


def _decay_tables():
    H = RET_HEADS
    log_g = jnp.log(1.0 - 2.0 ** (-5.0 - jnp.arange(H, dtype=f32)))
    p = jnp.arange(CHUNK, dtype=f32)
    intra = jnp.exp(log_g[:, None, None] * jnp.abs(p[:, None] - p[None, :]))
    q_dec = jnp.exp(log_g[:, None] * (p[None, :] + 1.0))
    k_dec = jnp.exp(log_g[:, None] * (CHUNK - 1.0 - p[None, :]))
    c_dec = jnp.exp(log_g * CHUNK)
    q_dec = jnp.broadcast_to(q_dec[:, :, None], (H, CHUNK, RET_KEY_DIM))
    k_dec = jnp.broadcast_to(k_dec[:, :, None], (H, CHUNK, RET_KEY_DIM))
    c_dec = jnp.broadcast_to(c_dec[:, None, None], (H, 1, RET_VAL_DIM))
    return intra, q_dec, k_dec, c_dec


K_SCALE = RET_KEY_DIM ** -0.5


RET_CHUNKS = 4


def _ret_tables_specs():
    whole = lambda *shape: pl.BlockSpec(shape, lambda b, i: (0,) * len(shape))
    return [whole(RET_HEADS, CHUNK, CHUNK), whole(RET_HEADS, CHUNK, RET_KEY_DIM), whole(RET_HEADS, CHUNK, RET_KEY_DIM),
            whole(RET_HEADS, 1, RET_VAL_DIM)]


def _rotate(x, cos, sn):
    return x * cos + pltpu.roll(x, RET_KEY_DIM // 2, 1) * sn


def _ret_fwd(proj, B, S, rope, decay, phase=None):
    T = B * S
    nc = S // CHUNK
    H, dk, dv = RET_HEADS, RET_KEY_DIM, RET_VAL_DIM
    sb = RET_CHUNKS * CHUNK
    ns = S // sb

    def body(q_ref, k_ref, v_ref, g_ref, cos_ref, sin_ref, intra_ref, qd_ref, kd_ref, cd_ref,
             qr_ref, kr_ref, o_ref, u_ref, st_ref, state_ref):
        @pl.when(pl.program_id(1) == 0)
        def _():
            state_ref[...] = jnp.zeros_like(state_ref)

        cos, sn = cos_ref[...], sin_ref[...]
        for h in range(H):
            hs = slice(h * dk, (h + 1) * dk)
            qr_ref[:, hs] = _rotate(q_ref[:, hs].astype(f32), cos, sn).astype(bf16)
            kr_ref[:, hs] = (_rotate(k_ref[:, hs].astype(f32), cos, sn) * K_SCALE).astype(bf16)
        states = [state_ref[h] for h in range(H)]
        for ci in range(RET_CHUNKS):
            r = slice(ci * CHUNK, (ci + 1) * CHUNK)
            for h in range(H):
                hk, hv = slice(h * dk, (h + 1) * dk), slice(h * dv, (h + 1) * dv)
                qi, ki, vi = qr_ref[r, hk], kr_ref[r, hk], v_ref[r, hv]
                stb = states[h].astype(bf16)
                st_ref[0, h, ci] = stb
                s = (_dot_nt(qi, ki) * intra_ref[h]).astype(bf16)
                o = _dot(s, vi) + _dot((qi.astype(f32) * qd_ref[h]).astype(bf16), stb)
                states[h] = states[h] * cd_ref[h] + _dot_tn((ki.astype(f32) * kd_ref[h]).astype(bf16), vi)
                mu = jnp.mean(o, axis=-1, keepdims=True)
                xc = o - mu
                var = jnp.mean(xc * xc, axis=-1, keepdims=True)
                oh = xc * lax.rsqrt(var + EPS)
                g = g_ref[r, hv].astype(f32)
                o_ref[r, hv] = o.astype(bf16)
                u_ref[r, hv] = (g * _sig(g) * oh).astype(bf16)
        for h in range(H):
            state_ref[h] = states[h]

    blk = lambda w, c: pl.BlockSpec((sb, w), lambda b, i: (b * ns + i, c))
    return _call(
        body, phase, name="ret_fwd", grid=(B, ns), scratch_shapes=[pltpu.VMEM((H, dk, dv), f32)],
        in_specs=[blk(H * dk, C_RQ // (H * dk)), blk(H * dk, C_RK // (H * dk)), blk(H * dv, C_RV // (H * dv)),
                  blk(H * dv, C_RG // (H * dv)),
                  pl.BlockSpec((sb, dk), lambda b, i: (i, 0)), pl.BlockSpec((sb, dk), lambda b, i: (i, 0)),
                  *_ret_tables_specs()],
        out_specs=[blk(H * dk, 0), blk(H * dk, 0), blk(H * dv, 0), blk(H * dv, 0),
                   pl.BlockSpec((1, H, RET_CHUNKS, dk, dv), lambda b, i: (b, 0, i, 0, 0))],
        out_shape=[jax.ShapeDtypeStruct((T, H * dk), bf16), jax.ShapeDtypeStruct((T, H * dk), bf16),
                   jax.ShapeDtypeStruct((T, H * dv), bf16), jax.ShapeDtypeStruct((T, H * dv), bf16),
                   jax.ShapeDtypeStruct((B, H, nc, dk, dv), bf16)],
        args=(proj, proj, proj, proj, *rope, *decay))


def _bias_rows(rb):
    last = rb[:, N_REL - 1:]
    return jnp.concatenate([
        jnp.broadcast_to(last, (ATT_HEADS, PAD - MAX_REL + 1)),
        jnp.flip(rb[:, :N_REL - 1], axis=1),
        jnp.broadcast_to(rb[:, :1], (ATT_HEADS, KWIN - PAD - CHUNK)),
        jnp.broadcast_to(last, (ATT_HEADS, TOEP - KWIN)),
    ], axis=1)


def _build_bias(t_ref, bias_ref):
    row = lax.broadcasted_iota(jnp.int32, (QBLK, KWIN), 0) // CHUNK
    col = lax.broadcasted_iota(jnp.int32, (QBLK, KWIN), 1) // CHUNK
    delta = BAND_CHUNKS + row - col
    vis = (delta >= 0) & (delta <= BAND_CHUNKS)
    for h in range(ATT_HEADS):
        t = jnp.broadcast_to(t_ref[h:h + 1, :], (QBLK, TOEP))
        rolled = pltpu.roll(t, 0, 1, stride=1, stride_axis=0)
        bias_ref[h] = jnp.where(vis, rolled[:, :KWIN], NEG_INF)


def _att_probs(qh, kh, bias):
    s = _dot_nt(qh, kh) * (ATT_HEAD_DIM ** -0.5) + bias
    m = jnp.max(s, axis=-1, keepdims=True)
    p = jnp.exp(s - m)
    return p * (1.0 / jnp.sum(p, axis=-1, keepdims=True))


def _by_window(i, step):
    sizes = list(range(QBLK, KWIN, QBLK))
    for n, nk in enumerate(sizes):
        pl.when(i == n)(functools.partial(step, nk))
    pl.when(i >= len(sizes))(functools.partial(step, KWIN))


def _att_fwd(proj, trows, B, S, phase=None):
    T = B * S
    nq = S // QBLK
    dh = ATT_HEAD_DIM

    def body(q_ref, k_ref, v_ref, t_ref, o_ref, bias_ref):
        i = pl.program_id(1)

        @pl.when((pl.program_id(0) == 0) & (i == 0))
        def _():
            _build_bias(t_ref, bias_ref)

        def step(nk):
            win = pl.ds(pl.multiple_of((i + 1) * QBLK - nk, QBLK), nk)
            kw, vw = k_ref[win, :], v_ref[win, :]
            outs = []
            for h in range(ATT_HEADS):
                hs = slice(h * dh, (h + 1) * dh)
                pr = _att_probs(q_ref[:, hs], kw[:, hs], bias_ref[h, :, KWIN - nk:])
                outs.append(_dot(pr.astype(bf16), vw[:, hs]))
            o_ref[...] = jnp.concatenate(outs, axis=1).astype(bf16)

        _by_window(i, step)

    return _call(
        body, phase, name="att_fwd", grid=(B, nq),
        in_specs=[pl.BlockSpec((QBLK, ATT_W), lambda b, i: (b * nq + i, C_AQ // ATT_W)),
                  pl.BlockSpec((S, ATT_W), lambda b, i: (b, C_AK // ATT_W)),
                  pl.BlockSpec((S, ATT_W), lambda b, i: (b, C_AV // ATT_W)),
                  pl.BlockSpec((ATT_HEADS, TOEP), lambda b, i: (0, 0))],
        out_specs=[pl.BlockSpec((QBLK, ATT_W), lambda b, i: (b * nq + i, 0))],
        out_shape=[jax.ShapeDtypeStruct((T, ATT_W), bf16)],
        scratch_shapes=[pltpu.VMEM((ATT_HEADS, QBLK, KWIN), f32)],
        args=(proj, proj, proj, trows))


def _gl_specs(tm):
    w = 512
    return [pl.BlockSpec((tm, w), functools.partial(lambda i, j: (i, C_GL // 512 + j), j=j)) for j in range(4)]


def _gates(gl_refs, bg_ref):
    gl = jnp.concatenate([r[...] for r in gl_refs], axis=1).astype(f32) + bg_ref[...]
    g = _sig(gl)
    return g[:, :D_MODEL], g[:, D_MODEL:]


def _mix_fwd(x2, proj, u, ao, b_gate, w_ro, w_ao, w_out):
    T, D = x2.shape
    tm = _tile(T, ROW_TILE, 8)

    def body(x_ref, u_ref, ao_ref, g0, g1, g2, g3, bg_ref, wro_ref, wao_ref, wo_ref, h1_ref, yr_ref, ya_ref):
        yr = _dot(u_ref[...], wro_ref[...])
        ao = ao_ref[...]
        ya = jnp.concatenate([_dot(ao, wao_ref[k]) for k in range(N_CHIPS)], axis=1)
        gr, ga = _gates((g0, g1, g2, g3), bg_ref)
        mix = gr * yr + ga * ya
        h1_ref[...] = x_ref[...] + _dot(mix.astype(bf16), wo_ref[...])
        yr_ref[...] = yr.astype(bf16)
        ya_ref[...] = ya.astype(bf16)

    full = lambda a: pl.BlockSpec(a.shape, lambda i: (0,) * a.ndim)
    row = lambda n: pl.BlockSpec((tm, n), lambda i: (i, 0))
    return pl.pallas_call(
        body, name="mix_fwd", grid=(T // tm,),
        in_specs=[row(D), row(D), row(ATT_W), *_gl_specs(tm), full(b_gate), full(w_ro), full(w_ao), full(w_out)],
        out_specs=[row(D), row(D), row(D)],
        out_shape=[jax.ShapeDtypeStruct((T, D), f32), jax.ShapeDtypeStruct((T, D), bf16),
                   jax.ShapeDtypeStruct((T, D), bf16)],
        compiler_params=_params(("parallel",)),
    )(x2, u, ao, proj, proj, proj, proj, b_gate, w_ro, w_ao, w_out)


def _ffn_fwd(h1, g_ffn, wg, wu, wd, g_fin, target):
    T, D = h1.shape
    nf, _, tf = wg.shape
    tm = _tile(T, ROW_TILE, 8)

    def body(h1_ref, g_ref, wg_ref, wu_ref, wd_ref, gf_ref, tg_ref, hn_ref, a_ref, b_ref, f_ref, dh2_ref, part_ref,
             hs_ref, acc_ref):
        j = pl.program_id(1)

        @pl.when(j == 0)
        def _():
            h = h1_ref[...]
            r = lax.rsqrt(jnp.mean(h * h, axis=-1, keepdims=True) + EPS)
            hn = (h * r * g_ref[...]).astype(bf16)
            hs_ref[...] = hn
            hn_ref[...] = hn
            acc_ref[...] = jnp.zeros_like(acc_ref)

        hn = hs_ref[...]
        a = _dot(hn, wg_ref[0])
        b = _dot(hn, wu_ref[0])
        f = ((a * _sig(a)) * b).astype(bf16)
        a_ref[0] = a.astype(bf16)
        b_ref[0] = b.astype(bf16)
        f_ref[0] = f
        acc_ref[...] += _dot(f, wd_ref[0])

        @pl.when(j == nf - 1)
        def _():
            h2 = h1_ref[...] + acc_ref[...]
            r = lax.rsqrt(jnp.mean(h2 * h2, axis=-1, keepdims=True) + EPS)
            n = h2 * r
            gf = gf_ref[...]
            e = n * gf - tg_ref[...]
            dy = e * (1.0 / D)
            dn = dy * gf
            dh2_ref[...] = r * (dn - n * jnp.mean(dn * n, axis=-1, keepdims=True))
            part_ref[...] = jnp.zeros_like(part_ref)
            part_ref[0:1, :] = jnp.sum(dy * n, axis=0, keepdims=True)
            part_ref[1:2, :] = (0.5 / D) * jnp.sum(e * e, axis=0, keepdims=True)

    row = lambda n: pl.BlockSpec((tm, n), lambda i, j: (i, 0))
    vec = pl.BlockSpec((1, D), lambda i, j: (0, 0))
    col = pl.BlockSpec((1, tm, tf), lambda i, j: (j, i, 0))
    wcol = pl.BlockSpec((1, D, tf), lambda i, j: (j, 0, 0))
    act = jax.ShapeDtypeStruct((nf, T, tf), bf16)
    return pl.pallas_call(
        body, name="ffn_fwd", grid=(T // tm, nf),
        in_specs=[row(D), vec, wcol, wcol, pl.BlockSpec((1, tf, D), lambda i, j: (j, 0, 0)), vec, row(D)],
        out_specs=[row(D), col, col, col, row(D), pl.BlockSpec((8, D), lambda i, j: (i, 0))],
        out_shape=[jax.ShapeDtypeStruct((T, D), bf16), act, act, act,
                   jax.ShapeDtypeStruct((T, D), f32), jax.ShapeDtypeStruct((T // tm * 8, D), f32)],
        scratch_shapes=[pltpu.VMEM((tm, D), bf16), pltpu.VMEM((tm, D), f32)],
        compiler_params=_params(("parallel", "arbitrary")),
    )(h1, g_ffn, wg, wu, wd, g_fin, target)


def _ffn_bwd(dh2, h1, g_ffn, a, b, wg, wu, wd):
    T, D = h1.shape
    nf, _, tf = wg.shape
    tm = _tile(T, ROW_TILE, 8)

    def body(dh2_ref, h1_ref, g_ref, a_ref, b_ref, wg_ref, wu_ref, wd_ref, da_ref, db_ref, dh1_ref, part_ref,
             ds_ref, acc_ref):
        j = pl.program_id(1)

        @pl.when(j == 0)
        def _():
            ds_ref[...] = dh2_ref[...].astype(bf16)
            acc_ref[...] = jnp.zeros_like(acc_ref)

        df = _dot_nt(ds_ref[...], wd_ref[0])
        av = a_ref[0].astype(f32)
        sg = _sig(av)
        db = (df * (av * sg)).astype(bf16)
        da = (df * b_ref[0].astype(f32) * (sg * (1.0 + av * (1.0 - sg)))).astype(bf16)
        da_ref[0] = da
        db_ref[0] = db
        acc_ref[...] += _dot_nt(da, wg_ref[0]) + _dot_nt(db, wu_ref[0])

        @pl.when(j == nf - 1)
        def _():
            h = h1_ref[...]
            r = lax.rsqrt(jnp.mean(h * h, axis=-1, keepdims=True) + EPS)
            n = h * r
            dhn = acc_ref[...]
            dn = dhn * g_ref[...]
            dh1_ref[...] = dh2_ref[...] + r * (dn - n * jnp.mean(dn * n, axis=-1, keepdims=True))
            part_ref[...] = jnp.zeros_like(part_ref)
            part_ref[0:1, :] = jnp.sum(dhn * n, axis=0, keepdims=True)

    row = lambda n: pl.BlockSpec((tm, n), lambda i, j: (i, 0))
    col = pl.BlockSpec((1, tm, tf), lambda i, j: (j, i, 0))
    wcol = pl.BlockSpec((1, D, tf), lambda i, j: (j, 0, 0))
    act = jax.ShapeDtypeStruct((nf, T, tf), bf16)
    return pl.pallas_call(
        body, name="ffn_bwd", grid=(T // tm, nf),
        in_specs=[row(D), row(D), pl.BlockSpec((1, D), lambda i, j: (0, 0)), col, col, wcol, wcol,
                  pl.BlockSpec((1, tf, D), lambda i, j: (j, 0, 0))],
        out_specs=[col, col, row(D), pl.BlockSpec((8, D), lambda i, j: (i, 0))],
        out_shape=[act, act, jax.ShapeDtypeStruct((T, D), f32), jax.ShapeDtypeStruct((T // tm * 8, D), f32)],
        scratch_shapes=[pltpu.VMEM((tm, D), bf16), pltpu.VMEM((tm, D), f32)],
        compiler_params=_params(("parallel", "arbitrary")),
    )(dh2, h1, g_ffn, a, b, wg, wu, wd)


def _mix_bwd(dh1, proj, yr, ya, b_gate, w_ro, w_ao, w_out, phase=None):
    T, D = dh1.shape
    tm = _tile(T, ROW_TILE, 8)

    def body(dh1_ref, g0, g1, g2, g3, bg_ref, yr_ref, ya_ref, wro_ref, wao_ref, wo_ref,
             du_ref, dao_ref, dgl_ref, mix_ref, dyr_ref, dya_ref, part_ref):
        dmix = _dot_nt(dh1_ref[...].astype(bf16), wo_ref[...])
        gr, ga = _gates((g0, g1, g2, g3), bg_ref)
        yr = yr_ref[...].astype(f32)
        ya = ya_ref[...].astype(f32)
        dyr = (dmix * gr).astype(bf16)
        dya = (dmix * ga).astype(bf16)
        dgl = jnp.concatenate([dmix * yr * gr * (1.0 - gr), dmix * ya * ga * (1.0 - ga)], axis=1)
        du_ref[...] = _dot_nt(dyr, wro_ref[...]).astype(bf16)
        ns = wao_ref.shape[2]
        dao = _dot_nt(dya[:, :ns], wao_ref[0])
        for k in range(1, N_CHIPS):
            dao = dao + _dot_nt(dya[:, k * ns:(k + 1) * ns], wao_ref[k])
        dao_ref[...] = dao.astype(bf16)
        dgl_ref[...] = dgl.astype(bf16)
        mix_ref[...] = (gr * yr + ga * ya).astype(bf16)
        dyr_ref[...] = dyr
        dya_ref[...] = dya
        part_ref[...] = jnp.zeros_like(part_ref)
        part_ref[0:1, :] = jnp.sum(dgl, axis=0, keepdims=True)

    full = lambda a: pl.BlockSpec(a.shape, lambda i: (0,) * a.ndim)
    row = lambda n: pl.BlockSpec((tm, n), lambda i: (i, 0))
    return _call(
        body, phase, name="mix_bwd", grid=(T // tm,), scratch_shapes=[],
        in_specs=[row(D), *_gl_specs(tm), full(b_gate), row(D), row(D), full(w_ro), full(w_ao), full(w_out)],
        out_specs=[row(D), row(ATT_W), row(2 * D), row(D), row(D), row(D), pl.BlockSpec((8, 2 * D), lambda i: (i, 0))],
        out_shape=[jax.ShapeDtypeStruct((T, D), bf16), jax.ShapeDtypeStruct((T, ATT_W), bf16),
                   jax.ShapeDtypeStruct((T, 2 * D), bf16), jax.ShapeDtypeStruct((T, D), bf16),
                   jax.ShapeDtypeStruct((T, D), bf16), jax.ShapeDtypeStruct((T, D), bf16),
                   jax.ShapeDtypeStruct((T // tm * 8, 2 * D), f32)],
        args=(dh1, proj, proj, proj, proj, b_gate, yr, ya, w_ro, w_ao, w_out))


def _ret_bwd(proj, qr, kr, o, states, du, B, S, rope, decay, phase=None):
    T = B * S
    nc = S // CHUNK
    H, dk, dv = RET_HEADS, RET_KEY_DIM, RET_VAL_DIM

    sb = RET_CHUNKS * CHUNK
    ns = S // sb

    def body(qr_ref, kr_ref, v_ref, g_ref, o_ref, st_ref, du_ref, cos_ref, sin_ref, intra_ref, qd_ref, kd_ref, cd_ref,
             dq_ref, dk_ref, dv_ref, dg_ref, dstate_ref):
        @pl.when(pl.program_id(1) == 0)
        def _():
            dstate_ref[...] = jnp.zeros_like(dstate_ref)

        cos, snb = cos_ref[...], -sin_ref[...]
        dstates = [dstate_ref[h] for h in range(H)]
        for ci in reversed(range(RET_CHUNKS)):
            r = slice(ci * CHUNK, (ci + 1) * CHUNK)
            for h in range(H):
                hk, hv = slice(h * dk, (h + 1) * dk), slice(h * dv, (h + 1) * dv)
                intra, qd, kd = intra_ref[h], qd_ref[h], kd_ref[h]
                qi, ki, vi = qr_ref[r, hk], kr_ref[r, hk], v_ref[r, hv]
                si = st_ref[0, h, ci]
                o = o_ref[r, hv].astype(f32)
                mu = jnp.mean(o, axis=-1, keepdims=True)
                xc = o - mu
                rstd = lax.rsqrt(jnp.mean(xc * xc, axis=-1, keepdims=True) + EPS)
                oh = xc * rstd
                g = g_ref[r, hv].astype(f32)
                sg = _sig(g)
                dui = du_ref[r, hv].astype(f32)
                dg_ref[r, hv] = (dui * oh * (sg * (1.0 + g * (1.0 - sg)))).astype(bf16)
                doh = dui * (g * sg)
                do = rstd * (doh - jnp.mean(doh, axis=-1, keepdims=True)
                             - oh * jnp.mean(doh * oh, axis=-1, keepdims=True))
                dob = do.astype(bf16)
                p = (_dot_nt(qi, ki) * intra).astype(bf16)
                dsb = dstates[h].astype(bf16)
                kt = (ki.astype(f32) * kd).astype(bf16)
                qt = (qi.astype(f32) * qd).astype(bf16)
                dv_ref[r, hv] = (_dot_tn(p, dob) + _dot(kt, dsb)).astype(bf16)
                da = (_dot_nt(dob, vi) * intra).astype(bf16)
                dq = _dot(da, ki) + _dot_nt(dob, si) * qd
                dkk = (_dot_tn(da, qi) + _dot_nt(vi, dsb) * kd) * K_SCALE
                dq_ref[r, hk] = _rotate(dq, cos[r], snb[r]).astype(bf16)
                dk_ref[r, hk] = _rotate(dkk, cos[r], snb[r]).astype(bf16)
                dstates[h] = dstates[h] * cd_ref[h] + _dot_tn(qt, dob)
        for h in range(H):
            dstate_ref[h] = dstates[h]

    blk = lambda w, c: pl.BlockSpec((sb, w), lambda b, i: (b * ns + ns - 1 - i, c))
    return _call(
        body, phase, name="ret_bwd", grid=(B, ns),
        in_specs=[blk(H * dk, 0), blk(H * dk, 0), blk(H * dv, C_RV // (H * dv)), blk(H * dv, C_RG // (H * dv)),
                  blk(H * dv, 0),
                  pl.BlockSpec((1, H, RET_CHUNKS, dk, dv), lambda b, i: (b, 0, ns - 1 - i, 0, 0)),
                  blk(H * dv, 0),
                  pl.BlockSpec((sb, dk), lambda b, i: (ns - 1 - i, 0)), pl.BlockSpec((sb, dk), lambda b, i: (ns - 1 - i, 0)),
                  *_ret_tables_specs()],
        out_specs=[blk(H * dk, 0), blk(H * dk, 0), blk(H * dv, 0), blk(H * dv, 0)],
        out_shape=[jax.ShapeDtypeStruct((T, H * dk), bf16), jax.ShapeDtypeStruct((T, H * dk), bf16),
                   jax.ShapeDtypeStruct((T, H * dv), bf16), jax.ShapeDtypeStruct((T, H * dv), bf16)],
        scratch_shapes=[pltpu.VMEM((H, dk, dv), f32)],
        args=(qr, kr, proj, proj, o, states, du, *rope, *decay))


def _att_bwd(proj, dao, trows, B, S, phase=None):
    T = B * S
    nq = S // QBLK
    dh = ATT_HEAD_DIM
    scale = ATT_HEAD_DIM ** -0.5

    def body(q_ref, k_ref, v_ref, do_ref, t_ref, dq_ref, dk_ref, dv_ref, vec_ref, bias_ref, dbias_ref, dka_ref, dva_ref):
        b, i = pl.program_id(0), pl.program_id(1)

        @pl.when((b == 0) & (i == 0))
        def _():
            _build_bias(t_ref, bias_ref)
            dbias_ref[...] = jnp.zeros_like(dbias_ref)

        @pl.when(i == 0)
        def _():
            dka_ref[...] = jnp.zeros_like(dka_ref)
            dva_ref[...] = jnp.zeros_like(dva_ref)

        def step(nk):
            win = pl.ds(pl.multiple_of((i + 1) * QBLK - nk, QBLK), nk)
            kw, vw = k_ref[win, :], v_ref[win, :]
            dqs, dks, dvs = [], [], []
            for h in range(ATT_HEADS):
                hs = slice(h * dh, (h + 1) * dh)
                qh, kh, vh, doh = q_ref[:, hs], kw[:, hs], vw[:, hs], do_ref[:, hs]
                pr = _att_probs(qh, kh, bias_ref[h, :, KWIN - nk:])
                dp = _dot_nt(doh, vh)
                ds = pr * (dp - jnp.sum(pr * dp, axis=-1, keepdims=True))
                dbias_ref[h, :, KWIN - nk:] += ds
                dsb = (ds * scale).astype(bf16)
                dqs.append(_dot(dsb, kh))
                dks.append(_dot_tn(dsb, qh))
                dvs.append(_dot_tn(pr.astype(bf16), doh))
            dq_ref[...] = jnp.concatenate(dqs, axis=1).astype(bf16)
            dka_ref[win, :] += jnp.concatenate(dks, axis=1)
            dva_ref[win, :] += jnp.concatenate(dvs, axis=1)

        _by_window(i, step)

        @pl.when(i == nq - 1)
        def _():
            dk_ref[...] = dka_ref[...].astype(bf16)
            dv_ref[...] = dva_ref[...].astype(bf16)

        @pl.when((b == B - 1) & (i == nq - 1))
        def _():
            rr = lax.broadcasted_iota(jnp.int32, (QBLK, QBLK), 0)
            cc = lax.broadcasted_iota(jnp.int32, (QBLK, QBLK), 1)
            flip = jnp.where(rr + cc == QBLK - 1, 1.0, 0.0).astype(bf16)
            for h in range(ATT_HEADS):
                d = dbias_ref[h]
                hi = d.astype(bf16)
                lo = (d - hi.astype(f32)).astype(bf16)
                rev = _dot(flip, hi) + _dot(flip, lo)
                wide = jnp.concatenate([rev, jnp.zeros((QBLK, TOEP - KWIN), f32)], axis=1)
                rolled = pltpu.roll(wide, 0, 1, stride=1, stride_axis=0)
                vec_ref[h:h + 1, :] = jnp.sum(rolled, axis=0, keepdims=True)

    qspec = lambda c: pl.BlockSpec((QBLK, ATT_W), lambda b, i: (b * nq + i, c))
    kspec = lambda c: pl.BlockSpec((S, ATT_W), lambda b, i: (b, c))
    seq = jax.ShapeDtypeStruct((T, ATT_W), bf16)
    return _call(
        body, phase, name="att_bwd", grid=(B, nq),
        in_specs=[qspec(C_AQ // ATT_W), kspec(C_AK // ATT_W), kspec(C_AV // ATT_W), qspec(0),
                  pl.BlockSpec((ATT_HEADS, TOEP), lambda b, i: (0, 0))],
        out_specs=[qspec(0), kspec(0), kspec(0), pl.BlockSpec((ATT_HEADS, TOEP), lambda b, i: (0, 0))],
        out_shape=[seq, seq, seq, jax.ShapeDtypeStruct((ATT_HEADS, TOEP), f32)],
        scratch_shapes=[pltpu.VMEM((ATT_HEADS, QBLK, KWIN), f32), pltpu.VMEM((ATT_HEADS, QBLK, KWIN), f32),
                        pltpu.VMEM((S, ATT_W), f32), pltpu.VMEM((S, ATT_W), f32)],
        args=(proj, proj, proj, dao, trows))


def _in_proj_bwd(dproj, w_in, x2, gamma, dh1, phase=None):
    T, D = x2.shape
    nk, _, tk = w_in.shape
    tm = _tile(T, BIG_ROW_TILE, 8)

    def body(dp_ref, w_ref, x_ref, g_ref, dh1_ref, dx_ref, part_ref, acc_ref):
        j = pl.program_id(1)

        @pl.when(j == 0)
        def _():
            acc_ref[...] = jnp.zeros_like(acc_ref)

        acc_ref[...] += _dot_nt(dp_ref[...], w_ref[0])

        @pl.when(j == nk - 1)
        def _():
            x = x_ref[...]
            r = lax.rsqrt(jnp.mean(x * x, axis=-1, keepdims=True) + EPS)
            n = x * r
            dxn = acc_ref[...]
            dn = dxn * g_ref[...]
            dx_ref[...] = dh1_ref[...] + r * (dn - n * jnp.mean(dn * n, axis=-1, keepdims=True))
            part_ref[...] = jnp.zeros_like(part_ref)
            part_ref[0:1, :] = jnp.sum(dxn * n, axis=0, keepdims=True)

    row = lambda n: pl.BlockSpec((tm, n), lambda i, j: (i, 0))
    return _call(
        body, phase, name="in_proj_bwd", grid=(T // tm, nk),
        in_specs=[pl.BlockSpec((tm, tk), lambda i, j: (i, j)), pl.BlockSpec((1, D, tk), lambda i, j: (j, 0, 0)), row(D),
                  pl.BlockSpec((1, D), lambda i, j: (0, 0)), row(D)],
        out_specs=[row(D), pl.BlockSpec((8, D), lambda i, j: (i, 0))],
        out_shape=[jax.ShapeDtypeStruct((T, D), f32), jax.ShapeDtypeStruct((T // tm * 8, D), f32)],
        scratch_shapes=[pltpu.VMEM((tm, D), f32)],
        args=(dproj, w_in, x2, gamma, dh1))


def _wgrad(a, b, shard_axis, name, phase=None):
    def spec(arr, sharded, tt):
        if arr.ndim == 3:
            return arr.shape[2], pl.BlockSpec((1, tt, arr.shape[2]), lambda s, t: (s, t, 0))
        if sharded:
            w = arr.shape[1] // N_CHIPS
            return w, pl.BlockSpec((tt, w), lambda s, t: (t, s))
        return arr.shape[1], pl.BlockSpec((tt, arr.shape[1]), lambda s, t: (t, 0))

    T = a.shape[-2]
    tt = _tile(T, BIG_ROW_TILE, 16)
    nt = T // tt
    whole = a.ndim == 2 and b.ndim == 2 and a.shape[1] * b.shape[1] * 4 <= WGRAD_ACC_BYTES
    if whole:
        K, N = a.shape[1], b.shape[1]
        a_spec, b_spec = pl.BlockSpec((tt, K), lambda s, t: (t, 0)), pl.BlockSpec((tt, N), lambda s, t: (t, 0))
        out_block = (N_CHIPS, K // N_CHIPS, N) if shard_axis == 0 else (N_CHIPS, K, N // N_CHIPS)
        out_spec = pl.BlockSpec(out_block, lambda s, t: (0, 0, 0))
    else:
        K, a_spec = spec(a, shard_axis == 0, tt)
        N, b_spec = spec(b, shard_axis == 1, tt)
        out_block = (N_CHIPS, K, N)
        out_spec = pl.BlockSpec((1, K, N), lambda s, t: (s, 0, 0))

    def body(a_ref, b_ref, o_ref, acc_ref):
        t = pl.program_id(1)

        @pl.when(t == 0)
        def _():
            acc_ref[...] = jnp.zeros_like(acc_ref)

        av = a_ref[0] if a.ndim == 3 else a_ref[...]
        bv = b_ref[0] if b.ndim == 3 else b_ref[...]
        acc_ref[...] += _dot_tn(av.astype(bf16), bv.astype(bf16))

        @pl.when(t == nt - 1)
        def _():
            if not whole:
                o_ref[0] = acc_ref[...].astype(bf16)
            else:
                _, kk, nn = out_block
                for s in range(N_CHIPS):
                    o_ref[s] = (acc_ref[s * kk:(s + 1) * kk, :] if shard_axis == 0
                                else acc_ref[:, s * nn:(s + 1) * nn]).astype(bf16)

    (grad,), carried = _call(
        body, phase, name=name, grid=(1 if whole else N_CHIPS, nt), in_specs=[a_spec, b_spec], out_specs=[out_spec],
        out_shape=[jax.ShapeDtypeStruct(out_block, bf16)], scratch_shapes=[pltpu.VMEM((K, N), f32)], args=(a, b))
    return grad, carried


def _adamw_sum(place, w, m, v, part, from_chips, from_sibling, name):
    R, C = w.shape
    half = R // 2
    tr = _tile(half, max(16, (1 << 18) // C // 16 * 16), 16)
    nr = half // tr

    def body(p_ref, w_ref, m_ref, v_ref, part_ref, fc_ref, fs_ref, g_ref, d_ref, mo_ref, vo_ref):
        up = lambda x: x.astype(f32)
        mine = ((up(part_ref[0]) + up(fc_ref[0])) + up(fc_ref[1])) + up(fc_ref[2])
        sibs = ((up(fs_ref[0]) + up(fs_ref[1])) + up(fs_ref[2])) + up(fs_ref[3])
        g_ = jnp.where(pl.program_id(0) == p_ref[0], mine, sibs)
        m_ = ADAM_B1 * m_ref[...] + (1.0 - ADAM_B1) * g_
        v_ = ADAM_B2 * v_ref[...] + (1.0 - ADAM_B2) * (g_ * g_)
        m_hat = m_ / (1.0 - ADAM_B1 ** ADAM_STEP)
        v_hat = v_ / (1.0 - ADAM_B2 ** ADAM_STEP)
        g_ref[...] = g_
        d_ref[...] = -ADAM_LR * (m_hat / (jnp.sqrt(v_hat) + ADAM_EPS) + ADAM_WD * w_ref[...])
        mo_ref[...] = m_
        vo_ref[...] = v_

    spec = pl.BlockSpec((tr, C), lambda h, r, p: (h * nr + r, 0))
    return pl.pallas_call(
        body, name=name,
        grid_spec=pltpu.PrefetchScalarGridSpec(
            num_scalar_prefetch=1, grid=(2, nr),
            in_specs=[spec, spec, spec, pl.BlockSpec((1, tr, C), lambda h, r, p: (p[1], r, 0)),
                      pl.BlockSpec((3, tr, C), lambda h, r, p: (0, r, 0)),
                      pl.BlockSpec((4, tr, C), lambda h, r, p: (0, r, 0))],
            out_specs=[spec] * 4),
        out_shape=[jax.ShapeDtypeStruct((R, C), f32)] * 4,
        compiler_params=_params(("parallel", "parallel")),
    )(place, w, m, v, part, from_chips, from_sibling)


def _adamw(w, g, m, v, name):
    R, C = w.shape
    tr = _tile(R, max(8, (1 << 18) // C // 8 * 8), 8)

    def body(w_ref, g_ref, m_ref, v_ref, d_ref, mo_ref, vo_ref):
        g_ = g_ref[...]
        m_ = ADAM_B1 * m_ref[...] + (1.0 - ADAM_B1) * g_
        v_ = ADAM_B2 * v_ref[...] + (1.0 - ADAM_B2) * (g_ * g_)
        m_hat = m_ / (1.0 - ADAM_B1 ** ADAM_STEP)
        v_hat = v_ / (1.0 - ADAM_B2 ** ADAM_STEP)
        d_ref[...] = -ADAM_LR * (m_hat / (jnp.sqrt(v_hat) + ADAM_EPS) + ADAM_WD * w_ref[...])
        mo_ref[...] = m_
        vo_ref[...] = v_

    spec = pl.BlockSpec((tr, C), lambda i: (i, 0))
    return pl.pallas_call(
        body, name=name, grid=(R // tr,), in_specs=[spec] * 4, out_specs=[spec] * 3,
        out_shape=[jax.ShapeDtypeStruct((R, C), f32)] * 3,
        compiler_params=_params(("parallel",)),
    )(w, g, m, v)


def _place():
    return lax.axis_index("x"), lax.axis_index("y"), lax.axis_index("c")


def _other_chips(x, y):
    chips = [(1 - x, y), (x, 1 - y), (1 - x, 1 - y)]
    return chips, [2 * cx + cy for cx, cy in chips]


def _exchange_small(blk, name, reduce):
    R, C = blk.shape

    def body(x_ref, out_ref, *rest):
        if reduce:
            all_ref, send_sems, recv_sems = rest
        else:
            all_ref = out_ref
            send_sems, recv_sems = rest
        x, y, c = _place()
        me = 4 * x + 2 * y + c
        all_ref[me] = x_ref[...]
        copies = []
        for k in range(1, N_DEV):
            peer = tuple(1 - p if (k >> s) & 1 else p for p, s in ((x, 2), (y, 1), (c, 0)))
            cp = pltpu.make_async_remote_copy(src_ref=x_ref, dst_ref=all_ref.at[me], send_sem=send_sems.at[k - 1],
                                              recv_sem=recv_sems.at[k - 1], device_id=peer, device_id_type=MESH)
            cp.start()
            copies.append(cp)
        for cp in copies:
            cp.wait()
        if reduce:
            tot = all_ref[0]
            for d in range(1, N_DEV):
                tot = tot + all_ref[d]
            out_ref[...] = tot

    vm = pl.BlockSpec(memory_space=pltpu.VMEM)
    scratch = [pltpu.SemaphoreType.DMA((N_DEV - 1,)), pltpu.SemaphoreType.DMA((N_DEV - 1,))]
    if reduce:
        scratch = [pltpu.VMEM((N_DEV, R, C), f32)] + scratch
    return pl.pallas_call(
        body, name=name, in_specs=[vm], out_specs=vm,
        out_shape=jax.ShapeDtypeStruct((R, C) if reduce else (N_DEV, R, C), f32),
        scratch_shapes=scratch,
    )(blk)


def _cast_shard(place, w, name):
    R, C = w.shape
    tr = _tile(R, max(16, (1 << 19) // C // 16 * 16), 16)

    def body(p_ref, w_ref, o_ref):
        o_ref[0] = w_ref[...].astype(bf16)

    return pl.pallas_call(
        body, name=name,
        grid_spec=pltpu.PrefetchScalarGridSpec(
            num_scalar_prefetch=1, grid=(R // tr,),
            in_specs=[pl.BlockSpec((tr, C), lambda r, p: (r, 0))],
            out_specs=pl.BlockSpec((1, tr, C), lambda r, p: (p[1], r, 0))),
        out_shape=jax.ShapeDtypeStruct((N_CHIPS, R, C), bf16),
        compiler_params=_params(("parallel",)),
    )(place, w)


class _Phase:
    def __init__(self, arrays, out_shapes, aliases, n_copies, copies, arrivals):
        self.arrays, self.out_shapes, self.aliases = list(arrays), list(out_shapes), dict(aliases)
        self.n_copies, self.copies, self.arrivals = n_copies, copies, arrivals

    def sems(self):
        return [pltpu.SemaphoreType.DMA((self.n_copies,)), pltpu.SemaphoreType.DMA((self.n_copies,))]

    def _descriptors(self, pin, pout, send_sems, recv_sems):
        return [pltpu.make_async_remote_copy(src_ref=s, dst_ref=d, send_sem=send_sems.at[i], recv_sem=recv_sems.at[i],
                                             device_id=to, device_id_type=MESH)
                for i, (s, d, to) in enumerate(self.copies(pin, pout))]

    def start(self, pin, pout, send_sems, recv_sems):
        for cp in self._descriptors(pin, pout, send_sems, recv_sems):
            cp.start()

    def finish(self, pin, pout, send_sems, recv_sems):
        sent = self._descriptors(pin, pout, send_sems, recv_sems)
        for i, dst in enumerate(self.arrivals(pin, pout)):
            pltpu.make_async_remote_copy(src_ref=dst, dst_ref=dst, send_sem=send_sems.at[i], recv_sem=recv_sems.at[i],
                                         device_id=_place(), device_id_type=MESH).wait_recv()
        for cp in sent:
            cp.wait_send()


def _join(phases):
    if len(phases) == 1:
        return phases[0]
    ai = np.cumsum([0] + [len(p.arrays) for p in phases])
    oi = np.cumsum([0] + [len(p.out_shapes) for p in phases])

    def each(fn_name, pin, pout):
        return [item for k, p in enumerate(phases)
                for item in getattr(p, fn_name)(pin[ai[k]:ai[k + 1]], pout[oi[k]:oi[k + 1]])]

    aliases = {int(ai[k]) + i: int(oi[k]) + j for k, p in enumerate(phases) for i, j in p.aliases.items()}
    return _Phase([a for p in phases for a in p.arrays], [s for p in phases for s in p.out_shapes], aliases,
                  sum(p.n_copies for p in phases), functools.partial(each, "copies"), functools.partial(each, "arrivals"))


def _call(body, phase, *, name, grid, in_specs, out_specs, out_shape, scratch_shapes, args):
    seq = _params(("arbitrary",) * len(grid))
    if phase is None:
        res = pl.pallas_call(body, name=name, grid=grid, in_specs=in_specs, out_specs=out_specs, out_shape=out_shape,
                             scratch_shapes=scratch_shapes, compiler_params=seq)(*args)
        return list(res), []
    ni, no, ns = len(in_specs), len(out_specs), len(scratch_shapes)
    pi, po = len(phase.arrays), len(phase.out_shapes)

    def hosted(*refs):
        cut = np.cumsum([ni, pi, no, po, ns])
        ins, pin, outs, pout, scr, sems = (refs[a:b] for a, b in zip([0, *cut], [*cut, len(refs)]))
        ids = [pl.program_id(d) for d in range(len(grid))]
        first = functools.reduce(lambda p, q: p & q, [i == 0 for i in ids])
        last = functools.reduce(lambda p, q: p & q, [i == g - 1 for i, g in zip(ids, grid)])
        pl.when(first)(lambda: phase.start(pin, pout, *sems))
        body(*ins, *outs, *scr)
        pl.when(last)(lambda: phase.finish(pin, pout, *sems))

    anyspace = pl.BlockSpec(memory_space=pl.ANY)
    res = pl.pallas_call(
        hosted, name=name, grid=grid, in_specs=list(in_specs) + [anyspace] * pi,
        out_specs=list(out_specs) + [anyspace] * po, out_shape=list(out_shape) + phase.out_shapes,
        input_output_aliases={ni + i: no + j for i, j in phase.aliases.items()},
        scratch_shapes=list(scratch_shapes) + phase.sems(), compiler_params=seq,
    )(*args, *phase.arrays)
    return list(res[:no]), list(res[no:])


def _run_phases(name, phases):
    first = phases[0]
    pi, po = len(first.arrays), len(first.out_shapes)

    def body(*refs):
        pin, pout, sems = refs[:pi], refs[pi:pi + po], refs[pi + po:]
        for n, ph in enumerate(phases):
            ph.start(pin, pout, *sems[2 * n:2 * n + 2])
            ph.finish(pin, pout, *sems[2 * n:2 * n + 2])

    anyspace = pl.BlockSpec(memory_space=pl.ANY)
    return list(pl.pallas_call(
        body, name=name, in_specs=[anyspace] * pi, out_specs=[anyspace] * po, out_shape=first.out_shapes,
        input_output_aliases=first.aliases, scratch_shapes=[s for ph in phases for s in ph.sems()],
    )(*first.arrays))


def _half_rows(buf, c):
    half = buf.shape[1] // 2
    return pl.ds(c * half, half), pl.ds((1 - c) * half, half)


def _gather_phase(bufs, over_ici):
    n = len(bufs)
    shapes = [jax.ShapeDtypeStruct(b.shape, b.dtype) for b in bufs]

    def landed(out, which):
        x, y, c = _place()
        _, ks = _other_chips(x, y)
        return [out[a].at[ks[j], _half_rows(bufs[a], c)[which]] for a in range(n) for j in range(3)]

    def ici(pin, out):
        x, y, c = _place()
        chips, _ = _other_chips(x, y)
        mine = [out[a].at[2 * x + y, _half_rows(bufs[a], c)[0]] for a in range(n)]
        return [(mine[a], mine[a], (*chips[j], c)) for a in range(n) for j in range(3)]

    def d2d(pin, out):
        x, y, c = _place()
        return [(dst, dst, (x, y, 1 - c)) for dst in landed(out, 0)]

    if over_ici:
        return _Phase(bufs, shapes, {a: a for a in range(n)}, 3 * n, ici, lambda pin, out: landed(out, 0))
    return _Phase(bufs, shapes, {a: a for a in range(n)}, 3 * n, d2d, lambda pin, out: landed(out, 1))


def _rs_sibling(grads, name):
    n = len(grads)

    def body(*refs):
        g, out, send_sems, recv_sems = refs[:n], refs[n:2 * n], refs[2 * n], refs[2 * n + 1]
        x, y, c = _place()
        copies = []
        for a in range(n):
            half = grads[a].shape[1] // 2
            cp = pltpu.make_async_remote_copy(src_ref=g[a].at[:, pl.ds((1 - c) * half, half)], dst_ref=out[a],
                                              send_sem=send_sems.at[a], recv_sem=recv_sems.at[a],
                                              device_id=(x, y, 1 - c), device_id_type=MESH)
            cp.start()
            copies.append(cp)
        for cp in copies:
            cp.wait()

    anyspace = pl.BlockSpec(memory_space=pl.ANY)
    return pl.pallas_call(
        body, name=name, in_specs=[anyspace] * n, out_specs=[anyspace] * n,
        out_shape=[jax.ShapeDtypeStruct((N_CHIPS, g.shape[1] // 2, g.shape[2]), g.dtype) for g in grads],
        scratch_shapes=[pltpu.SemaphoreType.DMA((n,)), pltpu.SemaphoreType.DMA((n,))],
    )(*grads)


def _rs_add_sibling(place, grad, got, name):
    _, R, C = grad.shape
    half = R // 2
    tr = _tile(half, max(16, (1 << 19) // C // 16 * 16), 16)
    nr = half // tr

    def body(p_ref, a_ref, b_ref, o_ref):
        o_ref[...] = (a_ref[...].astype(f32) + b_ref[...].astype(f32)).astype(o_ref.dtype)

    return pl.pallas_call(
        body, name=name,
        grid_spec=pltpu.PrefetchScalarGridSpec(
            num_scalar_prefetch=1, grid=(N_CHIPS, nr),
            in_specs=[pl.BlockSpec((1, tr, C), lambda k, r, p: (k, p[0] * nr + r, 0)),
                      pl.BlockSpec((1, tr, C), lambda k, r, p: (k, r, 0))],
            out_specs=pl.BlockSpec((1, tr, C), lambda k, r, p: (k, r, 0))),
        out_shape=jax.ShapeDtypeStruct((N_CHIPS, half, C), bf16),
        compiler_params=_params(("parallel", "parallel")),
    )(place, grad, got)


def _rs_chips_phase(parts):
    n = len(parts)

    def copies(p, fc):
        x, y, c = _place()
        chips, ks = _other_chips(x, y)
        return [(p[a].at[ks[j]], fc[a].at[j], (*chips[j], c)) for a in range(n) for j in range(3)]

    shapes = [jax.ShapeDtypeStruct((3,) + q.shape[1:], q.dtype) for q in parts]
    return _Phase(parts, shapes, {}, 3 * n, copies, lambda p, fc: [fc[a].at[j] for a in range(n) for j in range(3)])


def _rs_hand_phase(parts, from_chips):
    n = len(parts)

    def copies(pin, fs):
        x, y, c = _place()
        sib = (x, y, 1 - c)
        own = [(pin[a].at[2 * x + y], fs[a].at[0], sib) for a in range(n)]
        return own + [(pin[n + a].at[j], fs[a].at[1 + j], sib) for a in range(n) for j in range(3)]

    def arrivals(pin, fs):
        return [fs[a].at[0] for a in range(n)] + [fs[a].at[1 + j] for a in range(n) for j in range(3)]

    shapes = [jax.ShapeDtypeStruct((4,) + q.shape[1:], q.dtype) for q in parts]
    return _Phase(list(parts) + list(from_chips), shapes, {}, 4 * n, copies, arrivals)


class _Exchange:
    def __init__(self, place):
        self.place = place

    def gather(self, bufs, over_ici):
        return _gather_phase(bufs, over_ici)

    def pair_sums(self, names, grads):
        got = _rs_sibling(grads, "rs_sibling_" + names[0])
        return [_rs_add_sibling(self.place, g, r, "rs_add_" + n) for n, g, r in zip(names, grads, got)]

    def to_chips(self, parts):
        return _rs_chips_phase(parts)

    def to_sibling(self, parts, from_chips):
        return _rs_hand_phase(parts, from_chips)

    def hand_over(self, name, parts, from_chips):
        return _run_phases(name, [_rs_hand_phase(parts, from_chips)])


def _local_step(x, target, norm_mix, b_gate, rb_full, norm_ffn, norm_final, w_in, rest, exch):
    B, S, D = x.shape
    T = B * S
    x2 = x.reshape(T, D)
    tg2 = target.reshape(T, D)
    rope, decay = _rope_tables(S), _decay_tables()
    trows = _bias_rows(rb_full)
    g_fin = norm_final.reshape(1, D)

    early = [n for n in rest if n != "w_ffn_down"]
    (xn, proj), got = _in_proj(x2, norm_mix, w_in, exch.gather([rest[n] for n in early], True))
    (qr, kr, o, u, states), got = _ret_fwd(proj, B, S, rope, decay, _join([exch.gather([rest["w_ffn_down"]], True),
                                                                         exch.gather(got, False)]))
    wb = dict(zip(early, got[1:]))
    (ao,), (wb["w_ffn_down"],) = _att_fwd(proj, trows, B, S, exch.gather(got[:1], False))
    w_ro, w_out = wb["w_ret_out"].reshape(-1, D), wb["w_out"].reshape(-1, D)
    h1, yr, ya = _mix_fwd(x2, proj, u, ao, b_gate, w_ro, wb["w_att_out"], w_out)
    hn, a, b, f, dh2, part_fin = _ffn_fwd(h1, norm_ffn, wb["w_ffn_gate"], wb["w_ffn_up"], wb["w_ffn_down"], g_fin, tg2)

    da, db, dh1, part_ffn = _ffn_bwd(dh2, h1, norm_ffn, a, b, wb["w_ffn_gate"], wb["w_ffn_up"], wb["w_ffn_down"])
    ffn = ["w_ffn_down", "w_ffn_gate", "w_ffn_up"]
    p_ffn = exch.pair_sums(ffn, [_wgrad(f, dh2, 0, "wgrad_ffn_down")[0], _wgrad(hn, da, 1, "wgrad_ffn_gate")[0],
                                 _wgrad(hn, db, 1, "wgrad_ffn_up")[0]])
    (du, dao, dgl, mix, dyr, dya, part_bg), c_down = _mix_bwd(dh1, proj, yr, ya, b_gate, w_ro, wb["w_att_out"], w_out,
                                                               exch.to_chips(p_ffn[:1]))
    mrg = ["w_out", "w_ret_out", "w_att_out"]
    p_mrg = exch.pair_sums(mrg, [_wgrad(mix, dh1, 0, "wgrad_out")[0], _wgrad(u, dyr, 0, "wgrad_ret_out")[0],
                                 _wgrad(ao, dya, 1, "wgrad_att_out")[0]])
    (drq, drk, drv, drg), c_gate_up = _ret_bwd(proj, qr, kr, o, states, du, B, S, rope, decay, exch.to_chips(p_ffn[1:]))
    c_ffn = c_down + c_gate_up
    (daq, dak, dav, dvec), got = _att_bwd(proj, dao, trows, B, S, _join([exch.to_chips(p_mrg),
                                                                        exch.to_sibling(p_ffn, c_ffn)]))
    c_mrg, s_ffn = got[:len(mrg)], got[len(mrg):]
    dproj = jnp.concatenate([drq, drk, drv, drg, daq, dak, dav, dgl], axis=1)
    g_in, s_mrg = _wgrad(xn, dproj, 1, "wgrad_in", exch.to_sibling(p_mrg, c_mrg))
    p_in = exch.pair_sums(["w_in"], [g_in])
    (gx, part_mix), c_in = _in_proj_bwd(dproj, w_in, x2, norm_mix, dh1, exch.to_chips(p_in))
    s_in = exch.hand_over("rs_hand_w_in", p_in, c_in)
    gbig = dict(zip(ffn + mrg + ["w_in"], zip(p_ffn + p_mrg + p_in, c_ffn + c_mrg + c_in, s_ffn + s_mrg + s_in)))
    rows = lambda p, r: p.reshape(-1, 8, p.shape[-1])[:, r, :].sum(axis=0)
    lo = KWIN - 1 - (MAX_REL - 1)
    drb = jnp.concatenate([jnp.flip(dvec[:, lo:lo + N_REL - 1], axis=1), dvec[:, :lo].sum(axis=1, keepdims=True)], axis=1)
    gsmall = {
        "norm_mix": rows(part_mix, 0), "b_gate": rows(part_bg, 0), "rel_bias": drb, "norm_ffn": rows(part_ffn, 0),
        "norm_final": rows(part_fin, 0),
    }
    return rows(part_fin, 1), gx.reshape(B, S, D), gbig, gsmall


SMALL_ROWS = 16


def _pack_small(gs, loss_lanes):
    D = D_MODEL
    rb = jnp.pad(gs["rel_bias"].reshape(-1), (0, 3 * D - ATT_HEADS * N_REL)).reshape(3, D)
    rows = [gs["norm_mix"].reshape(1, D), gs["b_gate"].reshape(2, D), gs["norm_ffn"].reshape(1, D),
            gs["norm_final"].reshape(1, D), rb, loss_lanes.reshape(1, D)]
    used = sum(r.shape[0] for r in rows)
    return jnp.concatenate(rows + [jnp.zeros((SMALL_ROWS - used, D), f32)], axis=0)


def kernel(x, norm_mix, w_in, b_gate, rel_bias, w_ret_out, w_att_out, w_out, norm_ffn, w_ffn_gate, w_ffn_up, w_ffn_down, norm_final, loss_target, m_norm_mix, m_w_in, m_b_gate, m_rel_bias, m_w_ret_out, m_w_att_out, m_w_out, m_norm_ffn, m_w_ffn_gate, m_w_ffn_up, m_w_ffn_down, m_norm_final, v_norm_mix, v_w_in, v_b_gate, v_rel_bias, v_w_ret_out, v_w_att_out, v_w_out, v_norm_ffn, v_w_ffn_gate, v_w_ffn_up, v_w_ffn_down, v_norm_final):
    w = dict(norm_mix=norm_mix, w_in=w_in, b_gate=b_gate, rel_bias=rel_bias, w_ret_out=w_ret_out, w_att_out=w_att_out,
             w_out=w_out, norm_ffn=norm_ffn, w_ffn_gate=w_ffn_gate, w_ffn_up=w_ffn_up, w_ffn_down=w_ffn_down,
             norm_final=norm_final)
    m = dict(norm_mix=m_norm_mix, w_in=m_w_in, b_gate=m_b_gate, rel_bias=m_rel_bias, w_ret_out=m_w_ret_out,
             w_att_out=m_w_att_out, w_out=m_w_out, norm_ffn=m_norm_ffn, w_ffn_gate=m_w_ffn_gate, w_ffn_up=m_w_ffn_up,
             w_ffn_down=m_w_ffn_down, norm_final=m_norm_final)
    v = dict(norm_mix=v_norm_mix, w_in=v_w_in, b_gate=v_b_gate, rel_bias=v_rel_bias, w_ret_out=v_w_ret_out,
             w_att_out=v_w_att_out, w_out=v_w_out, norm_ffn=v_norm_ffn, w_ffn_gate=v_w_ffn_gate, w_ffn_up=v_w_ffn_up,
             w_ffn_down=v_w_ffn_down, norm_final=v_norm_final)
    xi, yi, ci = _place()
    k_me = 2 * xi + yi

    place = jnp.stack([ci, k_me]).astype(jnp.int32)
    big = [n for n, _ in BIG]

    bufs = {n: _cast_shard(place, w[n][0], "cast_" + n) for n in big}
    w_in_all, = _run_phases("gather_w_in", [_gather_phase([bufs["w_in"]], True), _gather_phase([bufs["w_in"]], False)])
    rest = {n: bufs[n] for n in big if n != "w_in"}
    nrel_loc = rel_bias.shape[-1]
    rb_all = _exchange_small(jnp.pad(rel_bias[0], ((0, 0), (0, 128 - nrel_loc))), "gather_rel_bias", False)
    rb_full = jnp.concatenate([rb_all[2 * k, :, :nrel_loc] for k in range(N_CHIPS)], axis=1)

    loss_lanes, grad_x, gbig, gsmall = _local_step(x, loss_target, norm_mix, b_gate, rb_full, norm_ffn, norm_final,
                                                   w_in_all, rest, _Exchange(place))

    small = _exchange_small(_pack_small(gsmall, loss_lanes), "reduce_small", True)
    D = D_MODEL
    loss = jnp.sum(small[8])
    drb_full = small[5:8].reshape(-1)[:ATT_HEADS * N_REL].reshape(ATT_HEADS, N_REL)
    g = {
        "norm_mix": small[0:1], "b_gate": small[1:3].reshape(1, 2 * D), "norm_ffn": small[3:4], "norm_final": small[4],
        "rel_bias": lax.dynamic_slice_in_dim(drb_full, k_me * nrel_loc, nrel_loc, axis=1)[None],
    }

    delta, new_m, new_v = {}, {}, {}
    for n in big:
        g_, d_, m_, v_ = _adamw_sum(place, w[n][0], m[n][0], v[n][0], *gbig[n], "adamw_" + n)
        g[n], delta[n], new_m[n], new_v[n] = g_[None], d_[None], m_[None], v_[None]
    flat = lambda d: jnp.concatenate([d[n].reshape(-1) for n in SMALL])
    n_small = sum(int(np.prod(w[n].shape)) for n in SMALL)
    n_pad = -n_small % 1024
    packs = [jnp.pad(flat(d), (0, n_pad)).reshape(-1, 128) for d in (w, g, m, v)]
    outs = _adamw(*packs, "adamw_small")
    for res, dst in zip(outs, (delta, new_m, new_v)):
        off = 0
        fl = res.reshape(-1)
        for n in SMALL:
            sz = int(np.prod(w[n].shape))
            dst[n] = fl[off:off + sz].reshape(w[n].shape)
            off += sz

    return (loss, grad_x, *[g[n] for n in WEIGHTS], *[delta[n] for n in WEIGHTS], *[new_m[n] for n in WEIGHTS],
            *[new_v[n] for n in WEIGHTS])
```

```python
import functools

import numpy as np
import jax
import jax.numpy as jnp
from jax import lax
from jax.experimental import pallas as pl
from jax.experimental.pallas import tpu as pltpu

f32 = jnp.float32
bf16 = jnp.bfloat16

D_MODEL = 1024
CHUNK = 64
RET_HEADS = 4
RET_KEY_DIM = 128
RET_VAL_DIM = 256
ATT_HEADS = 8
ATT_HEAD_DIM = 64
ATT_W = ATT_HEADS * ATT_HEAD_DIM
BAND_CHUNKS = 8
PAD = BAND_CHUNKS * CHUNK
MAX_REL = 256
N_REL = CHUNK + MAX_REL
D_FF = 2816
N_IN = 6656
ROPE_BASE = 10000.0
EPS = 1e-6
NEG_INF = -1e30
C_RQ, C_RK, C_RV, C_RG, C_AQ, C_AK, C_AV, C_GL = 0, 512, 1024, 2048, 3072, 3584, 4096, 4608

ADAM_LR, ADAM_B1, ADAM_B2, ADAM_EPS, ADAM_WD, ADAM_STEP = 0.001, 0.9, 0.999, 1e-08, 0.01, 10

N_CHIPS = 4
N_DEV = 8
WGRAD_ACC_BYTES = 8 * 1024 * 1024
ROW_TILE = 512
BIG_ROW_TILE = 1024
QBLK = 256
KWIN = PAD + QBLK
TOEP = 1024
VMEM_LIMIT = 56 * 1024 * 1024
MESH = pl.DeviceIdType.MESH

BIG = (
    ("w_in", 1), ("w_ret_out", 0), ("w_att_out", 1), ("w_out", 0), ("w_ffn_gate", 1), ("w_ffn_up", 1), ("w_ffn_down", 0))
WEIGHTS = ("norm_mix", "w_in", "b_gate", "rel_bias", "w_ret_out", "w_att_out", "w_out", "norm_ffn", "w_ffn_gate",
           "w_ffn_up", "w_ffn_down", "norm_final")
SMALL = ("norm_mix", "b_gate", "rel_bias", "norm_ffn", "norm_final")


def _dot(a, b):
    return lax.dot_general(a, b, (((1,), (0,)), ((), ())), preferred_element_type=f32)


def _dot_nt(a, b):
    return lax.dot_general(a, b, (((1,), (1,)), ((), ())), preferred_element_type=f32)


def _dot_tn(a, b):
    return lax.dot_general(a, b, (((0,), (0,)), ((), ())), preferred_element_type=f32)


def _sig(x):
    return 1.0 / (1.0 + jnp.exp(-x))


def _tile(n, pref, mult):
    best = None
    for t in range(mult, min(n, pref) + 1, mult):
        if n % t == 0:
            best = t
    return best if best is not None else n


def _params(sem, vmem=VMEM_LIMIT):
    return pltpu.CompilerParams(dimension_semantics=sem, vmem_limit_bytes=vmem)


def _in_proj(place, x2, gamma, phase):
    T, D = x2.shape
    _, _, ns = phase.arrays[0].shape
    tm = _tile(T, BIG_ROW_TILE, 8)
    ni = T // tm

    def body(p_ref, x_ref, g_ref, xn_ref, pr_ref, xs_ref, w_ref, w_sem, carried):
        j, i = pl.program_id(0), pl.program_id(1)
        pin, pout, sems = carried
        rows = pl.ds(pl.multiple_of(i * tm, tm), tm)

        @pl.when(i == 0)
        def _():
            for f in range(1, N_CHIPS):
                @pl.when(j == f)
                def _():
                    phase.arrived(f - 1, pin, pout, *sems)
                    phase.begin(2 + f, pin, pout, *sems)
                    phase.arrived(2 + f, pin, pout, *sems)
            shard = pltpu.make_async_copy(pout[0].at[jnp.bitwise_xor(p_ref[1], j)], w_ref, w_sem)
            shard.start()
            shard.wait()

        @pl.when(j == 0)
        def _():
            x = x_ref[...]
            r = lax.rsqrt(jnp.mean(x * x, axis=-1, keepdims=True) + EPS)
            xn = (x * r * g_ref[...]).astype(bf16)
            xs_ref[rows, :] = xn
            xn_ref[...] = xn

        pr_ref[...] = _dot(xs_ref[rows, :], w_ref[...]).astype(bf16)

    first_pass = lambda j, i, p: (jnp.where(j == 0, i, ni - 1), 0)
    return _call(
        body, phase, name="in_proj", grid=(N_CHIPS, ni), prefetch=(place,), expose=True,
        in_specs=[pl.BlockSpec((tm, D), first_pass), pl.BlockSpec((1, D), lambda j, i, p: (0, 0))],
        out_specs=[pl.BlockSpec((tm, D), first_pass),
                   pl.BlockSpec((tm, ns), lambda j, i, p: (i, jnp.bitwise_xor(p[1], j)))],
        out_shape=[jax.ShapeDtypeStruct((T, D), bf16), jax.ShapeDtypeStruct((T, N_CHIPS * ns), bf16)],
        scratch_shapes=[pltpu.VMEM((T, D), bf16), pltpu.VMEM((D, ns), bf16), pltpu.SemaphoreType.DMA],
        args=(x2, gamma))


def _rope_tables(S):
    d = RET_KEY_DIM
    freqs = ROPE_BASE ** (-jnp.arange(0, d, 2, dtype=f32) / d)
    ang = jnp.arange(S, dtype=f32)[:, None] * freqs[None, :]
    cos, sin = jnp.cos(ang), jnp.sin(ang)
    return jnp.concatenate([cos, cos], axis=1), jnp.concatenate([-sin, sin], axis=1)


def _decay_tables():
    H = RET_HEADS
    log_g = jnp.log(1.0 - 2.0 ** (-5.0 - jnp.arange(H, dtype=f32)))
    p = jnp.arange(CHUNK, dtype=f32)
    intra = jnp.exp(log_g[:, None, None] * jnp.abs(p[:, None] - p[None, :]))
    q_dec = jnp.exp(log_g[:, None] * (p[None, :] + 1.0))
    k_dec = jnp.exp(log_g[:, None] * (CHUNK - 1.0 - p[None, :]))
    c_dec = jnp.exp(log_g * CHUNK)
    q_dec = jnp.broadcast_to(q_dec[:, :, None], (H, CHUNK, RET_KEY_DIM))
    k_dec = jnp.broadcast_to(k_dec[:, :, None], (H, CHUNK, RET_KEY_DIM))
    c_dec = jnp.broadcast_to(c_dec[:, None, None], (H, 1, RET_VAL_DIM))
    return intra, q_dec, k_dec, c_dec


K_SCALE = RET_KEY_DIM ** -0.5


RET_CHUNKS = 4


def _ret_tables_specs():
    whole = lambda *shape: pl.BlockSpec(shape, lambda b, i: (0,) * len(shape))
    return [whole(RET_HEADS, CHUNK, CHUNK), whole(RET_HEADS, CHUNK, RET_KEY_DIM), whole(RET_HEADS, CHUNK, RET_KEY_DIM),
            whole(RET_HEADS, 1, RET_VAL_DIM)]


def _rotate(x, cos, sn):
    return x * cos + pltpu.roll(x, RET_KEY_DIM // 2, 1) * sn


def _ret_fwd(proj, B, S, rope, decay, phase=None):
    T = B * S
    nc = S // CHUNK
    H, dk, dv = RET_HEADS, RET_KEY_DIM, RET_VAL_DIM
    sb = RET_CHUNKS * CHUNK
    ns = S // sb

    def body(q_ref, k_ref, v_ref, g_ref, cos_ref, sin_ref, intra_ref, qd_ref, kd_ref, cd_ref,
             qr_ref, kr_ref, o_ref, u_ref, st_ref, state_ref):
        @pl.when(pl.program_id(1) == 0)
        def _():
            state_ref[...] = jnp.zeros_like(state_ref)

        cos, sn = cos_ref[...], sin_ref[...]
        for h in range(H):
            hs = slice(h * dk, (h + 1) * dk)
            qr_ref[:, hs] = _rotate(q_ref[:, hs].astype(f32), cos, sn).astype(bf16)
            kr_ref[:, hs] = (_rotate(k_ref[:, hs].astype(f32), cos, sn) * K_SCALE).astype(bf16)
        states = [state_ref[h] for h in range(H)]
        for ci in range(RET_CHUNKS):
            r = slice(ci * CHUNK, (ci + 1) * CHUNK)
            for h in range(H):
                hk, hv = slice(h * dk, (h + 1) * dk), slice(h * dv, (h + 1) * dv)
                qi, ki, vi = qr_ref[r, hk], kr_ref[r, hk], v_ref[r, hv]
                stb = states[h].astype(bf16)
                st_ref[0, h, ci] = stb
                s = (_dot_nt(qi, ki) * intra_ref[h]).astype(bf16)
                o = _dot(s, vi) + _dot((qi.astype(f32) * qd_ref[h]).astype(bf16), stb)
                states[h] = states[h] * cd_ref[h] + _dot_tn((ki.astype(f32) * kd_ref[h]).astype(bf16), vi)
                mu = jnp.mean(o, axis=-1, keepdims=True)
                xc = o - mu
                var = jnp.mean(xc * xc, axis=-1, keepdims=True)
                oh = xc * lax.rsqrt(var + EPS)
                g = g_ref[r, hv].astype(f32)
                o_ref[r, hv] = o.astype(bf16)
                u_ref[r, hv] = (g * _sig(g) * oh).astype(bf16)
        for h in range(H):
            state_ref[h] = states[h]

    blk = lambda w, c: pl.BlockSpec((sb, w), lambda b, i: (b * ns + i, c))
    return _call(
        body, phase, name="ret_fwd", grid=(B, ns), scratch_shapes=[pltpu.VMEM((H, dk, dv), f32)],
        in_specs=[blk(H * dk, C_RQ // (H * dk)), blk(H * dk, C_RK // (H * dk)), blk(H * dv, C_RV // (H * dv)),
                  blk(H * dv, C_RG // (H * dv)),
                  pl.BlockSpec((sb, dk), lambda b, i: (i, 0)), pl.BlockSpec((sb, dk), lambda b, i: (i, 0)),
                  *_ret_tables_specs()],
        out_specs=[blk(H * dk, 0), blk(H * dk, 0), blk(H * dv, 0), blk(H * dv, 0),
                   pl.BlockSpec((1, H, RET_CHUNKS, dk, dv), lambda b, i: (b, 0, i, 0, 0))],
        out_shape=[jax.ShapeDtypeStruct((T, H * dk), bf16), jax.ShapeDtypeStruct((T, H * dk), bf16),
                   jax.ShapeDtypeStruct((T, H * dv), bf16), jax.ShapeDtypeStruct((T, H * dv), bf16),
                   jax.ShapeDtypeStruct((B, H, nc, dk, dv), bf16)],
        args=(proj, proj, proj, proj, *rope, *decay))


def _bias_rows(rb):
    last = rb[:, N_REL - 1:]
    return jnp.concatenate([
        jnp.broadcast_to(last, (ATT_HEADS, PAD - MAX_REL + 1)),
        jnp.flip(rb[:, :N_REL - 1], axis=1),
        jnp.broadcast_to(rb[:, :1], (ATT_HEADS, KWIN - PAD - CHUNK)),
        jnp.broadcast_to(last, (ATT_HEADS, TOEP - KWIN)),
    ], axis=1)


def _build_bias(t_ref, bias_ref):
    row = lax.broadcasted_iota(jnp.int32, (QBLK, KWIN), 0) // CHUNK
    col = lax.broadcasted_iota(jnp.int32, (QBLK, KWIN), 1) // CHUNK
    delta = BAND_CHUNKS + row - col
    vis = (delta >= 0) & (delta <= BAND_CHUNKS)
    for h in range(ATT_HEADS):
        t = jnp.broadcast_to(t_ref[h:h + 1, :], (QBLK, TOEP))
        rolled = pltpu.roll(t, 0, 1, stride=1, stride_axis=0)
        bias_ref[h] = jnp.where(vis, rolled[:, :KWIN], NEG_INF)


def _att_probs(qh, kh, bias):
    s = _dot_nt(qh, kh) * (ATT_HEAD_DIM ** -0.5) + bias
    m = jnp.max(s, axis=-1, keepdims=True)
    p = jnp.exp(s - m)
    return p * (1.0 / jnp.sum(p, axis=-1, keepdims=True))


def _by_window(i, step):
    sizes = list(range(QBLK, KWIN, QBLK))
    for n, nk in enumerate(sizes):
        pl.when(i == n)(functools.partial(step, nk))
    pl.when(i >= len(sizes))(functools.partial(step, KWIN))


def _att_fwd(proj, trows, B, S, phase=None):
    T = B * S
    nq = S // QBLK
    dh = ATT_HEAD_DIM

    def body(q_ref, k_ref, v_ref, t_ref, o_ref, bias_ref):
        i = pl.program_id(1)

        @pl.when((pl.program_id(0) == 0) & (i == 0))
        def _():
            _build_bias(t_ref, bias_ref)

        def step(nk):
            win = pl.ds(pl.multiple_of((i + 1) * QBLK - nk, QBLK), nk)
            kw, vw = k_ref[win, :], v_ref[win, :]
            outs = []
            for h in range(ATT_HEADS):
                hs = slice(h * dh, (h + 1) * dh)
                pr = _att_probs(q_ref[:, hs], kw[:, hs], bias_ref[h, :, KWIN - nk:])
                outs.append(_dot(pr.astype(bf16), vw[:, hs]))
            o_ref[...] = jnp.concatenate(outs, axis=1).astype(bf16)

        _by_window(i, step)

    return _call(
        body, phase, name="att_fwd", grid=(B, nq),
        in_specs=[pl.BlockSpec((QBLK, ATT_W), lambda b, i: (b * nq + i, C_AQ // ATT_W)),
                  pl.BlockSpec((S, ATT_W), lambda b, i: (b, C_AK // ATT_W)),
                  pl.BlockSpec((S, ATT_W), lambda b, i: (b, C_AV // ATT_W)),
                  pl.BlockSpec((ATT_HEADS, TOEP), lambda b, i: (0, 0))],
        out_specs=[pl.BlockSpec((QBLK, ATT_W), lambda b, i: (b * nq + i, 0))],
        out_shape=[jax.ShapeDtypeStruct((T, ATT_W), bf16)],
        scratch_shapes=[pltpu.VMEM((ATT_HEADS, QBLK, KWIN), f32)],
        args=(proj, proj, proj, trows))


def _gl_specs(tm):
    w = 512
    return [pl.BlockSpec((tm, w), functools.partial(lambda i, j: (i, C_GL // 512 + j), j=j)) for j in range(4)]


def _gates(gl_refs, bg_ref):
    gl = jnp.concatenate([r[...] for r in gl_refs], axis=1).astype(f32) + bg_ref[...]
    g = _sig(gl)
    return g[:, :D_MODEL], g[:, D_MODEL:]


def _mix_fwd(x2, proj, u, ao, b_gate, w_ro, w_ao, w_out, phase=None):
    T, D = x2.shape
    tm = _tile(T, ROW_TILE, 8)

    def body(x_ref, u_ref, ao_ref, g0, g1, g2, g3, bg_ref, wro_ref, wao_ref, wo_ref, h1_ref, yr_ref, ya_ref):
        yr = _dot(u_ref[...], wro_ref[...])
        ao = ao_ref[...]
        ya = jnp.concatenate([_dot(ao, wao_ref[k]) for k in range(N_CHIPS)], axis=1)
        gr, ga = _gates((g0, g1, g2, g3), bg_ref)
        mix = gr * yr + ga * ya
        h1_ref[...] = x_ref[...] + _dot(mix.astype(bf16), wo_ref[...])
        yr_ref[...] = yr.astype(bf16)
        ya_ref[...] = ya.astype(bf16)

    full = lambda a: pl.BlockSpec(a.shape, lambda i: (0,) * a.ndim)
    row = lambda n: pl.BlockSpec((tm, n), lambda i: (i, 0))
    return _call(
        body, phase, name="mix_fwd", grid=(T // tm,), scratch_shapes=[],
        in_specs=[row(D), row(D), row(ATT_W), *_gl_specs(tm), full(b_gate), full(w_ro), full(w_ao), full(w_out)],
        out_specs=[row(D), row(D), row(D)],
        out_shape=[jax.ShapeDtypeStruct((T, D), f32), jax.ShapeDtypeStruct((T, D), bf16),
                   jax.ShapeDtypeStruct((T, D), bf16)],
        args=(x2, u, ao, proj, proj, proj, proj, b_gate, w_ro, w_ao, w_out))


def _ffn_fwd(h1, g_ffn, wg, wu, wd, g_fin, target):
    T, D = h1.shape
    nf, _, tf = wg.shape
    tm = _tile(T, ROW_TILE, 8)

    def body(h1_ref, g_ref, wg_ref, wu_ref, wd_ref, gf_ref, tg_ref, hn_ref, a_ref, b_ref, f_ref, dh2_ref, part_ref,
             hs_ref, acc_ref):
        j = pl.program_id(1)

        @pl.when(j == 0)
        def _():
            h = h1_ref[...]
            r = lax.rsqrt(jnp.mean(h * h, axis=-1, keepdims=True) + EPS)
            hn = (h * r * g_ref[...]).astype(bf16)
            hs_ref[...] = hn
            hn_ref[...] = hn
            acc_ref[...] = jnp.zeros_like(acc_ref)

        hn = hs_ref[...]
        a = _dot(hn, wg_ref[0])
        b = _dot(hn, wu_ref[0])
        f = ((a * _sig(a)) * b).astype(bf16)
        a_ref[0] = a.astype(bf16)
        b_ref[0] = b.astype(bf16)
        f_ref[0] = f
        acc_ref[...] += _dot(f, wd_ref[0])

        @pl.when(j == nf - 1)
        def _():
            h2 = h1_ref[...] + acc_ref[...]
            r = lax.rsqrt(jnp.mean(h2 * h2, axis=-1, keepdims=True) + EPS)
            n = h2 * r
            gf = gf_ref[...]
            e = n * gf - tg_ref[...]
            dy = e * (1.0 / D)
            dn = dy * gf
            dh2_ref[...] = r * (dn - n * jnp.mean(dn * n, axis=-1, keepdims=True))
            part_ref[...] = jnp.zeros_like(part_ref)
            part_ref[0:1, :] = jnp.sum(dy * n, axis=0, keepdims=True)
            part_ref[1:2, :] = (0.5 / D) * jnp.sum(e * e, axis=0, keepdims=True)

    row = lambda n: pl.BlockSpec((tm, n), lambda i, j: (i, 0))
    vec = pl.BlockSpec((1, D), lambda i, j: (0, 0))
    col = pl.BlockSpec((1, tm, tf), lambda i, j: (j, i, 0))
    wcol = pl.BlockSpec((1, D, tf), lambda i, j: (j, 0, 0))
    act = jax.ShapeDtypeStruct((nf, T, tf), bf16)
    return pl.pallas_call(
        body, name="ffn_fwd", grid=(T // tm, nf),
        in_specs=[row(D), vec, wcol, wcol, pl.BlockSpec((1, tf, D), lambda i, j: (j, 0, 0)), vec, row(D)],
        out_specs=[row(D), col, col, col, row(D), pl.BlockSpec((8, D), lambda i, j: (i, 0))],
        out_shape=[jax.ShapeDtypeStruct((T, D), bf16), act, act, act,
                   jax.ShapeDtypeStruct((T, D), f32), jax.ShapeDtypeStruct((T // tm * 8, D), f32)],
        scratch_shapes=[pltpu.VMEM((tm, D), bf16), pltpu.VMEM((tm, D), f32)],
        compiler_params=_params(("parallel", "arbitrary")),
    )(h1, g_ffn, wg, wu, wd, g_fin, target)


def _ffn_bwd(dh2, h1, g_ffn, a, b, wg, wu, wd):
    T, D = h1.shape
    nf, _, tf = wg.shape
    tm = _tile(T, ROW_TILE, 8)

    def body(dh2_ref, h1_ref, g_ref, a_ref, b_ref, wg_ref, wu_ref, wd_ref, da_ref, db_ref, dh1_ref, part_ref,
             ds_ref, acc_ref):
        j = pl.program_id(1)

        @pl.when(j == 0)
        def _():
            ds_ref[...] = dh2_ref[...].astype(bf16)
            acc_ref[...] = jnp.zeros_like(acc_ref)

        df = _dot_nt(ds_ref[...], wd_ref[0])
        av = a_ref[0].astype(f32)
        sg = _sig(av)
        db = (df * (av * sg)).astype(bf16)
        da = (df * b_ref[0].astype(f32) * (sg * (1.0 + av * (1.0 - sg)))).astype(bf16)
        da_ref[0] = da
        db_ref[0] = db
        acc_ref[...] += _dot_nt(da, wg_ref[0]) + _dot_nt(db, wu_ref[0])

        @pl.when(j == nf - 1)
        def _():
            h = h1_ref[...]
            r = lax.rsqrt(jnp.mean(h * h, axis=-1, keepdims=True) + EPS)
            n = h * r
            dhn = acc_ref[...]
            dn = dhn * g_ref[...]
            dh1_ref[...] = dh2_ref[...] + r * (dn - n * jnp.mean(dn * n, axis=-1, keepdims=True))
            part_ref[...] = jnp.zeros_like(part_ref)
            part_ref[0:1, :] = jnp.sum(dhn * n, axis=0, keepdims=True)

    row = lambda n: pl.BlockSpec((tm, n), lambda i, j: (i, 0))
    col = pl.BlockSpec((1, tm, tf), lambda i, j: (j, i, 0))
    wcol = pl.BlockSpec((1, D, tf), lambda i, j: (j, 0, 0))
    act = jax.ShapeDtypeStruct((nf, T, tf), bf16)
    return pl.pallas_call(
        body, name="ffn_bwd", grid=(T // tm, nf),
        in_specs=[row(D), row(D), pl.BlockSpec((1, D), lambda i, j: (0, 0)), col, col, wcol, wcol,
                  pl.BlockSpec((1, tf, D), lambda i, j: (j, 0, 0))],
        out_specs=[col, col, row(D), pl.BlockSpec((8, D), lambda i, j: (i, 0))],
        out_shape=[act, act, jax.ShapeDtypeStruct((T, D), f32), jax.ShapeDtypeStruct((T // tm * 8, D), f32)],
        scratch_shapes=[pltpu.VMEM((tm, D), bf16), pltpu.VMEM((tm, D), f32)],
        compiler_params=_params(("parallel", "arbitrary")),
    )(dh2, h1, g_ffn, a, b, wg, wu, wd)


def _mix_bwd(dh1, proj, yr, ya, b_gate, w_ro, w_ao, w_out, phase=None):
    T, D = dh1.shape
    tm = _tile(T, ROW_TILE, 8)

    def body(dh1_ref, g0, g1, g2, g3, bg_ref, yr_ref, ya_ref, wro_ref, wao_ref, wo_ref,
             du_ref, dao_ref, dgl_ref, mix_ref, dyr_ref, dya_ref, part_ref):
        dmix = _dot_nt(dh1_ref[...].astype(bf16), wo_ref[...])
        gr, ga = _gates((g0, g1, g2, g3), bg_ref)
        yr = yr_ref[...].astype(f32)
        ya = ya_ref[...].astype(f32)
        dyr = (dmix * gr).astype(bf16)
        dya = (dmix * ga).astype(bf16)
        dgl = jnp.concatenate([dmix * yr * gr * (1.0 - gr), dmix * ya * ga * (1.0 - ga)], axis=1)
        du_ref[...] = _dot_nt(dyr, wro_ref[...]).astype(bf16)
        ns = wao_ref.shape[2]
        dao = _dot_nt(dya[:, :ns], wao_ref[0])
        for k in range(1, N_CHIPS):
            dao = dao + _dot_nt(dya[:, k * ns:(k + 1) * ns], wao_ref[k])
        dao_ref[...] = dao.astype(bf16)
        dgl_ref[...] = dgl.astype(bf16)
        mix_ref[...] = (gr * yr + ga * ya).astype(bf16)
        dyr_ref[...] = dyr
        dya_ref[...] = dya
        part_ref[...] = jnp.zeros_like(part_ref)
        part_ref[0:1, :] = jnp.sum(dgl, axis=0, keepdims=True)

    full = lambda a: pl.BlockSpec(a.shape, lambda i: (0,) * a.ndim)
    row = lambda n: pl.BlockSpec((tm, n), lambda i: (i, 0))
    return _call(
        body, phase, name="mix_bwd", grid=(T // tm,), scratch_shapes=[],
        in_specs=[row(D), *_gl_specs(tm), full(b_gate), row(D), row(D), full(w_ro), full(w_ao), full(w_out)],
        out_specs=[row(D), row(ATT_W), row(2 * D), row(D), row(D), row(D), pl.BlockSpec((8, 2 * D), lambda i: (i, 0))],
        out_shape=[jax.ShapeDtypeStruct((T, D), bf16), jax.ShapeDtypeStruct((T, ATT_W), bf16),
                   jax.ShapeDtypeStruct((T, 2 * D), bf16), jax.ShapeDtypeStruct((T, D), bf16),
                   jax.ShapeDtypeStruct((T, D), bf16), jax.ShapeDtypeStruct((T, D), bf16),
                   jax.ShapeDtypeStruct((T // tm * 8, 2 * D), f32)],
        args=(dh1, proj, proj, proj, proj, b_gate, yr, ya, w_ro, w_ao, w_out))


def _ret_bwd(proj, qr, kr, o, states, du, B, S, rope, decay, phase=None):
    T = B * S
    nc = S // CHUNK
    H, dk, dv = RET_HEADS, RET_KEY_DIM, RET_VAL_DIM

    sb = RET_CHUNKS * CHUNK
    ns = S // sb

    def body(qr_ref, kr_ref, v_ref, g_ref, o_ref, st_ref, du_ref, cos_ref, sin_ref, intra_ref, qd_ref, kd_ref, cd_ref,
             dq_ref, dk_ref, dv_ref, dg_ref, dstate_ref):
        @pl.when(pl.program_id(1) == 0)
        def _():
            dstate_ref[...] = jnp.zeros_like(dstate_ref)

        cos, snb = cos_ref[...], -sin_ref[...]
        dstates = [dstate_ref[h] for h in range(H)]
        for ci in reversed(range(RET_CHUNKS)):
            r = slice(ci * CHUNK, (ci + 1) * CHUNK)
            for h in range(H):
                hk, hv = slice(h * dk, (h + 1) * dk), slice(h * dv, (h + 1) * dv)
                intra, qd, kd = intra_ref[h], qd_ref[h], kd_ref[h]
                qi, ki, vi = qr_ref[r, hk], kr_ref[r, hk], v_ref[r, hv]
                si = st_ref[0, h, ci]
                o = o_ref[r, hv].astype(f32)
                mu = jnp.mean(o, axis=-1, keepdims=True)
                xc = o - mu
                rstd = lax.rsqrt(jnp.mean(xc * xc, axis=-1, keepdims=True) + EPS)
                oh = xc * rstd
                g = g_ref[r, hv].astype(f32)
                sg = _sig(g)
                dui = du_ref[r, hv].astype(f32)
                dg_ref[r, hv] = (dui * oh * (sg * (1.0 + g * (1.0 - sg)))).astype(bf16)
                doh = dui * (g * sg)
                do = rstd * (doh - jnp.mean(doh, axis=-1, keepdims=True)
                             - oh * jnp.mean(doh * oh, axis=-1, keepdims=True))
                dob = do.astype(bf16)
                p = (_dot_nt(qi, ki) * intra).astype(bf16)
                dsb = dstates[h].astype(bf16)
                kt = (ki.astype(f32) * kd).astype(bf16)
                qt = (qi.astype(f32) * qd).astype(bf16)
                dv_ref[r, hv] = (_dot_tn(p, dob) + _dot(kt, dsb)).astype(bf16)
                da = (_dot_nt(dob, vi) * intra).astype(bf16)
                dq = _dot(da, ki) + _dot_nt(dob, si) * qd
                dkk = (_dot_tn(da, qi) + _dot_nt(vi, dsb) * kd) * K_SCALE
                dq_ref[r, hk] = _rotate(dq, cos[r], snb[r]).astype(bf16)
                dk_ref[r, hk] = _rotate(dkk, cos[r], snb[r]).astype(bf16)
                dstates[h] = dstates[h] * cd_ref[h] + _dot_tn(qt, dob)
        for h in range(H):
            dstate_ref[h] = dstates[h]

    blk = lambda w, c: pl.BlockSpec((sb, w), lambda b, i: (b * ns + ns - 1 - i, c))
    return _call(
        body, phase, name="ret_bwd", grid=(B, ns),
        in_specs=[blk(H * dk, 0), blk(H * dk, 0), blk(H * dv, C_RV // (H * dv)), blk(H * dv, C_RG // (H * dv)),
                  blk(H * dv, 0),
                  pl.BlockSpec((1, H, RET_CHUNKS, dk, dv), lambda b, i: (b, 0, ns - 1 - i, 0, 0)),
                  blk(H * dv, 0),
                  pl.BlockSpec((sb, dk), lambda b, i: (ns - 1 - i, 0)), pl.BlockSpec((sb, dk), lambda b, i: (ns - 1 - i, 0)),
                  *_ret_tables_specs()],
        out_specs=[blk(H * dk, 0), blk(H * dk, 0), blk(H * dv, 0), blk(H * dv, 0)],
        out_shape=[jax.ShapeDtypeStruct((T, H * dk), bf16), jax.ShapeDtypeStruct((T, H * dk), bf16),
                   jax.ShapeDtypeStruct((T, H * dv), bf16), jax.ShapeDtypeStruct((T, H * dv), bf16)],
        scratch_shapes=[pltpu.VMEM((H, dk, dv), f32)],
        args=(qr, kr, proj, proj, o, states, du, *rope, *decay))


def _att_bwd(proj, dao, trows, B, S, phase=None):
    T = B * S
    nq = S // QBLK
    dh = ATT_HEAD_DIM
    scale = ATT_HEAD_DIM ** -0.5

    def body(q_ref, k_ref, v_ref, do_ref, t_ref, dq_ref, dk_ref, dv_ref, vec_ref, bias_ref, dbias_ref, dka_ref, dva_ref):
        b, i = pl.program_id(0), pl.program_id(1)

        @pl.when((b == 0) & (i == 0))
        def _():
            _build_bias(t_ref, bias_ref)
            dbias_ref[...] = jnp.zeros_like(dbias_ref)

        @pl.when(i == 0)
        def _():
            dka_ref[...] = jnp.zeros_like(dka_ref)
            dva_ref[...] = jnp.zeros_like(dva_ref)

        def step(nk):
            win = pl.ds(pl.multiple_of((i + 1) * QBLK - nk, QBLK), nk)
            kw, vw = k_ref[win, :], v_ref[win, :]
            dqs, dks, dvs = [], [], []
            for h in range(ATT_HEADS):
                hs = slice(h * dh, (h + 1) * dh)
                qh, kh, vh, doh = q_ref[:, hs], kw[:, hs], vw[:, hs], do_ref[:, hs]
                pr = _att_probs(qh, kh, bias_ref[h, :, KWIN - nk:])
                dp = _dot_nt(doh, vh)
                ds = pr * (dp - jnp.sum(pr * dp, axis=-1, keepdims=True))
                dbias_ref[h, :, KWIN - nk:] += ds
                dsb = (ds * scale).astype(bf16)
                dqs.append(_dot(dsb, kh))
                dks.append(_dot_tn(dsb, qh))
                dvs.append(_dot_tn(pr.astype(bf16), doh))
            dq_ref[...] = jnp.concatenate(dqs, axis=1).astype(bf16)
            dka_ref[win, :] += jnp.concatenate(dks, axis=1)
            dva_ref[win, :] += jnp.concatenate(dvs, axis=1)

        _by_window(i, step)

        @pl.when(i == nq - 1)
        def _():
            dk_ref[...] = dka_ref[...].astype(bf16)
            dv_ref[...] = dva_ref[...].astype(bf16)

        @pl.when((b == B - 1) & (i == nq - 1))
        def _():
            rr = lax.broadcasted_iota(jnp.int32, (QBLK, QBLK), 0)
            cc = lax.broadcasted_iota(jnp.int32, (QBLK, QBLK), 1)
            flip = jnp.where(rr + cc == QBLK - 1, 1.0, 0.0).astype(bf16)
            for h in range(ATT_HEADS):
                d = dbias_ref[h]
                hi = d.astype(bf16)
                lo = (d - hi.astype(f32)).astype(bf16)
                rev = _dot(flip, hi) + _dot(flip, lo)
                wide = jnp.concatenate([rev, jnp.zeros((QBLK, TOEP - KWIN), f32)], axis=1)
                rolled = pltpu.roll(wide, 0, 1, stride=1, stride_axis=0)
                vec_ref[h:h + 1, :] = jnp.sum(rolled, axis=0, keepdims=True)

    qspec = lambda c: pl.BlockSpec((QBLK, ATT_W), lambda b, i: (b * nq + i, c))
    kspec = lambda c: pl.BlockSpec((S, ATT_W), lambda b, i: (b, c))
    seq = jax.ShapeDtypeStruct((T, ATT_W), bf16)
    return _call(
        body, phase, name="att_bwd", grid=(B, nq),
        in_specs=[qspec(C_AQ // ATT_W), kspec(C_AK // ATT_W), kspec(C_AV // ATT_W), qspec(0),
                  pl.BlockSpec((ATT_HEADS, TOEP), lambda b, i: (0, 0))],
        out_specs=[qspec(0), kspec(0), kspec(0), pl.BlockSpec((ATT_HEADS, TOEP), lambda b, i: (0, 0))],
        out_shape=[seq, seq, seq, jax.ShapeDtypeStruct((ATT_HEADS, TOEP), f32)],
        scratch_shapes=[pltpu.VMEM((ATT_HEADS, QBLK, KWIN), f32), pltpu.VMEM((ATT_HEADS, QBLK, KWIN), f32),
                        pltpu.VMEM((S, ATT_W), f32), pltpu.VMEM((S, ATT_W), f32)],
        args=(proj, proj, proj, dao, trows))


def _in_proj_bwd(dproj, w_in, x2, gamma, dh1, phase=None):
    T, D = x2.shape
    nk, _, tk = w_in.shape
    tm = _tile(T, BIG_ROW_TILE, 8)

    def body(dp_ref, w_ref, x_ref, g_ref, dh1_ref, dx_ref, part_ref, acc_ref):
        j = pl.program_id(1)

        @pl.when(j == 0)
        def _():
            acc_ref[...] = jnp.zeros_like(acc_ref)

        acc_ref[...] += _dot_nt(dp_ref[...], w_ref[0])

        @pl.when(j == nk - 1)
        def _():
            x = x_ref[...]
            r = lax.rsqrt(jnp.mean(x * x, axis=-1, keepdims=True) + EPS)
            n = x * r
            dxn = acc_ref[...]
            dn = dxn * g_ref[...]
            dx_ref[...] = dh1_ref[...] + r * (dn - n * jnp.mean(dn * n, axis=-1, keepdims=True))
            part_ref[...] = jnp.zeros_like(part_ref)
            part_ref[0:1, :] = jnp.sum(dxn * n, axis=0, keepdims=True)

    row = lambda n: pl.BlockSpec((tm, n), lambda i, j: (i, 0))
    return _call(
        body, phase, name="in_proj_bwd", grid=(T // tm, nk),
        in_specs=[pl.BlockSpec((tm, tk), lambda i, j: (i, j)), pl.BlockSpec((1, D, tk), lambda i, j: (j, 0, 0)), row(D),
                  pl.BlockSpec((1, D), lambda i, j: (0, 0)), row(D)],
        out_specs=[row(D), pl.BlockSpec((8, D), lambda i, j: (i, 0))],
        out_shape=[jax.ShapeDtypeStruct((T, D), f32), jax.ShapeDtypeStruct((T // tm * 8, D), f32)],
        scratch_shapes=[pltpu.VMEM((tm, D), f32)],
        args=(dproj, w_in, x2, gamma, dh1))


def _wgrad(a, b, shard_axis, name, phase=None):
    def spec(arr, sharded, tt):
        if arr.ndim == 3:
            return arr.shape[2], pl.BlockSpec((1, tt, arr.shape[2]), lambda s, t: (s, t, 0))
        if sharded:
            w = arr.shape[1] // N_CHIPS
            return w, pl.BlockSpec((tt, w), lambda s, t: (t, s))
        return arr.shape[1], pl.BlockSpec((tt, arr.shape[1]), lambda s, t: (t, 0))

    T = a.shape[-2]
    tt = _tile(T, BIG_ROW_TILE, 16)
    nt = T // tt
    whole = a.ndim == 2 and b.ndim == 2 and a.shape[1] * b.shape[1] * 4 <= WGRAD_ACC_BYTES
    if whole:
        K, N = a.shape[1], b.shape[1]
        a_spec, b_spec = pl.BlockSpec((tt, K), lambda s, t: (t, 0)), pl.BlockSpec((tt, N), lambda s, t: (t, 0))
        out_block = (N_CHIPS, K // N_CHIPS, N) if shard_axis == 0 else (N_CHIPS, K, N // N_CHIPS)
        out_spec = pl.BlockSpec(out_block, lambda s, t: (0, 0, 0))
    else:
        K, a_spec = spec(a, shard_axis == 0, tt)
        N, b_spec = spec(b, shard_axis == 1, tt)
        out_block = (N_CHIPS, K, N)
        out_spec = pl.BlockSpec((1, K, N), lambda s, t: (s, 0, 0))

    def body(a_ref, b_ref, o_ref, acc_ref):
        t = pl.program_id(1)

        @pl.when(t == 0)
        def _():
            acc_ref[...] = jnp.zeros_like(acc_ref)

        av = a_ref[0] if a.ndim == 3 else a_ref[...]
        bv = b_ref[0] if b.ndim == 3 else b_ref[...]
        acc_ref[...] += _dot_tn(av.astype(bf16), bv.astype(bf16))

        @pl.when(t == nt - 1)
        def _():
            if not whole:
                o_ref[0] = acc_ref[...].astype(bf16)
            else:
                _, kk, nn = out_block
                for s in range(N_CHIPS):
                    o_ref[s] = (acc_ref[s * kk:(s + 1) * kk, :] if shard_axis == 0
                                else acc_ref[:, s * nn:(s + 1) * nn]).astype(bf16)

    (grad,), carried = _call(
        body, phase, name=name, grid=(1 if whole else N_CHIPS, nt), in_specs=[a_spec, b_spec], out_specs=[out_spec],
        out_shape=[jax.ShapeDtypeStruct(out_block, bf16)], scratch_shapes=[pltpu.VMEM((K, N), f32)], args=(a, b))
    return grad, carried


def _adamw_sum(place, w, m, v, part, from_chips, from_sibling, name):
    R, C = w.shape
    half = R // 2
    tr = _tile(half, max(16, (1 << 18) // C // 16 * 16), 16)
    nr = half // tr

    def body(p_ref, w_ref, m_ref, v_ref, part_ref, fc_ref, fs_ref, g_ref, d_ref, mo_ref, vo_ref):
        up = lambda x: x.astype(f32)
        mine = ((up(part_ref[0]) + up(fc_ref[0])) + up(fc_ref[1])) + up(fc_ref[2])
        sibs = ((up(fs_ref[0]) + up(fs_ref[1])) + up(fs_ref[2])) + up(fs_ref[3])
        g_ = jnp.where(pl.program_id(0) == p_ref[0], mine, sibs)
        m_ = ADAM_B1 * m_ref[...] + (1.0 - ADAM_B1) * g_
        v_ = ADAM_B2 * v_ref[...] + (1.0 - ADAM_B2) * (g_ * g_)
        m_hat = m_ / (1.0 - ADAM_B1 ** ADAM_STEP)
        v_hat = v_ / (1.0 - ADAM_B2 ** ADAM_STEP)
        g_ref[...] = g_
        d_ref[...] = -ADAM_LR * (m_hat / (jnp.sqrt(v_hat) + ADAM_EPS) + ADAM_WD * w_ref[...])
        mo_ref[...] = m_
        vo_ref[...] = v_

    spec = pl.BlockSpec((tr, C), lambda h, r, p: (h * nr + r, 0))
    return pl.pallas_call(
        body, name=name,
        grid_spec=pltpu.PrefetchScalarGridSpec(
            num_scalar_prefetch=1, grid=(2, nr),
            in_specs=[spec, spec, spec, pl.BlockSpec((1, tr, C), lambda h, r, p: (p[1], r, 0)),
                      pl.BlockSpec((3, tr, C), lambda h, r, p: (0, r, 0)),
                      pl.BlockSpec((4, tr, C), lambda h, r, p: (0, r, 0))],
            out_specs=[spec] * 4),
        out_shape=[jax.ShapeDtypeStruct((R, C), f32)] * 4,
        compiler_params=_params(("parallel", "parallel")),
    )(place, w, m, v, part, from_chips, from_sibling)


def _adamw(w, g, m, v, name):
    R, C = w.shape
    tr = _tile(R, max(8, (1 << 18) // C // 8 * 8), 8)

    def body(w_ref, g_ref, m_ref, v_ref, d_ref, mo_ref, vo_ref):
        g_ = g_ref[...]
        m_ = ADAM_B1 * m_ref[...] + (1.0 - ADAM_B1) * g_
        v_ = ADAM_B2 * v_ref[...] + (1.0 - ADAM_B2) * (g_ * g_)
        m_hat = m_ / (1.0 - ADAM_B1 ** ADAM_STEP)
        v_hat = v_ / (1.0 - ADAM_B2 ** ADAM_STEP)
        d_ref[...] = -ADAM_LR * (m_hat / (jnp.sqrt(v_hat) + ADAM_EPS) + ADAM_WD * w_ref[...])
        mo_ref[...] = m_
        vo_ref[...] = v_

    spec = pl.BlockSpec((tr, C), lambda i: (i, 0))
    return pl.pallas_call(
        body, name=name, grid=(R // tr,), in_specs=[spec] * 4, out_specs=[spec] * 3,
        out_shape=[jax.ShapeDtypeStruct((R, C), f32)] * 3,
        compiler_params=_params(("parallel",)),
    )(w, g, m, v)


def _place():
    return lax.axis_index("x"), lax.axis_index("y"), lax.axis_index("c")


def _other_chips(x, y):
    chips = [(1 - x, y), (x, 1 - y), (1 - x, 1 - y)]
    return chips, [2 * cx + cy for cx, cy in chips]


def _exchange_small(blk, name, reduce):
    R, C = blk.shape

    def body(x_ref, out_ref, *rest):
        if reduce:
            all_ref, send_sems, recv_sems = rest
        else:
            all_ref = out_ref
            send_sems, recv_sems = rest
        x, y, c = _place()
        me = 4 * x + 2 * y + c
        all_ref[me] = x_ref[...]
        copies = []
        for k in range(1, N_DEV):
            peer = tuple(1 - p if (k >> s) & 1 else p for p, s in ((x, 2), (y, 1), (c, 0)))
            cp = pltpu.make_async_remote_copy(src_ref=x_ref, dst_ref=all_ref.at[me], send_sem=send_sems.at[k - 1],
                                              recv_sem=recv_sems.at[k - 1], device_id=peer, device_id_type=MESH)
            cp.start()
            copies.append(cp)
        for cp in copies:
            cp.wait()
        if reduce:
            tot = all_ref[0]
            for d in range(1, N_DEV):
                tot = tot + all_ref[d]
            out_ref[...] = tot

    vm = pl.BlockSpec(memory_space=pltpu.VMEM)
    scratch = [pltpu.SemaphoreType.DMA((N_DEV - 1,)), pltpu.SemaphoreType.DMA((N_DEV - 1,))]
    if reduce:
        scratch = [pltpu.VMEM((N_DEV, R, C), f32)] + scratch
    return pl.pallas_call(
        body, name=name, in_specs=[vm], out_specs=vm,
        out_shape=jax.ShapeDtypeStruct((R, C) if reduce else (N_DEV, R, C), f32),
        scratch_shapes=scratch,
    )(blk)


def _cast_shard(place, w, name):
    R, C = w.shape
    tr = _tile(R, max(16, (1 << 19) // C // 16 * 16), 16)

    def body(p_ref, w_ref, o_ref):
        o_ref[0] = w_ref[...].astype(bf16)

    return pl.pallas_call(
        body, name=name,
        grid_spec=pltpu.PrefetchScalarGridSpec(
            num_scalar_prefetch=1, grid=(R // tr,),
            in_specs=[pl.BlockSpec((tr, C), lambda r, p: (r, 0))],
            out_specs=pl.BlockSpec((1, tr, C), lambda r, p: (p[1], r, 0))),
        out_shape=jax.ShapeDtypeStruct((N_CHIPS, R, C), bf16),
        compiler_params=_params(("parallel",)),
    )(place, w)


class _Phase:
    def __init__(self, arrays, out_shapes, aliases, n_copies, copies, arrivals, own_starts=(), own_waits=()):
        self.arrays, self.out_shapes, self.aliases = list(arrays), list(out_shapes), dict(aliases)
        self.n_copies, self.copies, self.arrivals = n_copies, copies, arrivals
        self.own_starts, self.own_waits = tuple(own_starts), tuple(own_waits)

    def sems(self):
        return [pltpu.SemaphoreType.DMA((self.n_copies,)), pltpu.SemaphoreType.DMA((self.n_copies,))]

    def _descriptors(self, pin, pout, send_sems, recv_sems):
        return [pltpu.make_async_remote_copy(src_ref=s, dst_ref=d, send_sem=send_sems.at[i], recv_sem=recv_sems.at[i],
                                             device_id=to, device_id_type=MESH)
                for i, (s, d, to) in enumerate(self.copies(pin, pout))]

    def _arrival(self, i, pin, pout, send_sems, recv_sems):
        dst = self.arrivals(pin, pout)[i]
        return pltpu.make_async_remote_copy(src_ref=dst, dst_ref=dst, send_sem=send_sems.at[i], recv_sem=recv_sems.at[i],
                                            device_id=_place(), device_id_type=MESH)

    def start(self, pin, pout, send_sems, recv_sems):
        for i, cp in enumerate(self._descriptors(pin, pout, send_sems, recv_sems)):
            if i not in self.own_starts:
                cp.start()

    def begin(self, i, pin, pout, send_sems, recv_sems):
        self._descriptors(pin, pout, send_sems, recv_sems)[i].start()

    def arrived(self, i, pin, pout, send_sems, recv_sems):
        self._arrival(i, pin, pout, send_sems, recv_sems).wait_recv()

    def finish(self, pin, pout, send_sems, recv_sems):
        for i in range(self.n_copies):
            if i not in self.own_waits:
                self._arrival(i, pin, pout, send_sems, recv_sems).wait_recv()
        for cp in self._descriptors(pin, pout, send_sems, recv_sems):
            cp.wait_send()


def _join(phases):
    if len(phases) == 1:
        return phases[0]
    ai = np.cumsum([0] + [len(p.arrays) for p in phases])
    oi = np.cumsum([0] + [len(p.out_shapes) for p in phases])

    def each(fn_name, pin, pout):
        return [item for k, p in enumerate(phases)
                for item in getattr(p, fn_name)(pin[ai[k]:ai[k + 1]], pout[oi[k]:oi[k + 1]])]

    aliases = {int(ai[k]) + i: int(oi[k]) + j for k, p in enumerate(phases) for i, j in p.aliases.items()}
    ci = np.cumsum([0] + [p.n_copies for p in phases])
    shifted = lambda attr: [int(ci[k]) + i for k, p in enumerate(phases) for i in getattr(p, attr)]
    return _Phase([a for p in phases for a in p.arrays], [s for p in phases for s in p.out_shapes], aliases,
                  int(ci[-1]), functools.partial(each, "copies"), functools.partial(each, "arrivals"),
                  shifted("own_starts"), shifted("own_waits"))


def _call(body, phase, *, name, grid, in_specs, out_specs, out_shape, scratch_shapes, args, prefetch=(), expose=False):
    seq = _params(("arbitrary",) * len(grid))
    np_ = len(prefetch)
    if phase is None:
        spec = pltpu.PrefetchScalarGridSpec(num_scalar_prefetch=np_, grid=grid, in_specs=in_specs, out_specs=out_specs,
                                            scratch_shapes=scratch_shapes)
        res = pl.pallas_call(body, name=name, grid_spec=spec, out_shape=out_shape, compiler_params=seq)(*prefetch, *args)
        return list(res), []
    ni, no, ns = len(in_specs), len(out_specs), len(scratch_shapes)
    pi, po = len(phase.arrays), len(phase.out_shapes)

    def hosted(*refs):
        cut = np.cumsum([np_, ni, pi, no, po, ns])
        pre, ins, pin, outs, pout, scr, sems = (refs[a:b] for a, b in zip([0, *cut], [*cut, len(refs)]))
        ids = [pl.program_id(d) for d in range(len(grid))]
        first = functools.reduce(lambda p, q: p & q, [i == 0 for i in ids])
        last = functools.reduce(lambda p, q: p & q, [i == g - 1 for i, g in zip(ids, grid)])
        pl.when(first)(lambda: phase.start(pin, pout, *sems))
        body(*pre, *ins, *outs, *scr, **({"carried": (pin, pout, sems)} if expose else {}))
        pl.when(last)(lambda: phase.finish(pin, pout, *sems))

    anyspace = pl.BlockSpec(memory_space=pl.ANY)
    spec = pltpu.PrefetchScalarGridSpec(
        num_scalar_prefetch=np_, grid=grid, in_specs=list(in_specs) + [anyspace] * pi,
        out_specs=list(out_specs) + [anyspace] * po, scratch_shapes=list(scratch_shapes) + phase.sems())
    res = pl.pallas_call(
        hosted, name=name, grid_spec=spec, out_shape=list(out_shape) + phase.out_shapes,
        input_output_aliases={np_ + ni + i: no + j for i, j in phase.aliases.items()}, compiler_params=seq,
    )(*prefetch, *args, *phase.arrays)
    return list(res[:no]), list(res[no:])


def _run_phases(name, phases):
    first = phases[0]
    pi, po = len(first.arrays), len(first.out_shapes)

    def body(*refs):
        pin, pout, sems = refs[:pi], refs[pi:pi + po], refs[pi + po:]
        for n, ph in enumerate(phases):
            ph.start(pin, pout, *sems[2 * n:2 * n + 2])
            ph.finish(pin, pout, *sems[2 * n:2 * n + 2])

    anyspace = pl.BlockSpec(memory_space=pl.ANY)
    return list(pl.pallas_call(
        body, name=name, in_specs=[anyspace] * pi, out_specs=[anyspace] * po, out_shape=first.out_shapes,
        input_output_aliases=first.aliases, scratch_shapes=[s for ph in phases for s in ph.sems()],
    )(*first.arrays))


def _half_rows(buf, c):
    half = buf.shape[1] // 2
    return pl.ds(c * half, half), pl.ds((1 - c) * half, half)


def _gather_phase(bufs, over_ici):
    n = len(bufs)
    shapes = [jax.ShapeDtypeStruct(b.shape, b.dtype) for b in bufs]

    def landed(out, which):
        x, y, c = _place()
        _, ks = _other_chips(x, y)
        return [out[a].at[ks[j], _half_rows(bufs[a], c)[which]] for a in range(n) for j in range(3)]

    def ici(pin, out):
        x, y, c = _place()
        chips, _ = _other_chips(x, y)
        mine = [out[a].at[2 * x + y, _half_rows(bufs[a], c)[0]] for a in range(n)]
        return [(mine[a], mine[a], (*chips[j], c)) for a in range(n) for j in range(3)]

    def d2d(pin, out):
        x, y, c = _place()
        return [(dst, dst, (x, y, 1 - c)) for dst in landed(out, 0)]

    if over_ici:
        return _Phase(bufs, shapes, {a: a for a in range(n)}, 3 * n, ici, lambda pin, out: landed(out, 0))
    return _Phase(bufs, shapes, {a: a for a in range(n)}, 3 * n, d2d, lambda pin, out: landed(out, 1))


def _feed_phase(buf):
    def chips():
        x, y, c = _place()
        return [(x if f < 2 else 1 - x, y if f % 2 == 0 else 1 - y) for f in (1, 2, 3)]

    def copies(pin, out):
        x, y, c = _place()
        mine = _half_rows(buf, c)[0]
        own = out[0].at[2 * x + y, mine]
        sent = [(own, own, (cx, cy, c)) for cx, cy in chips()]
        return sent + [(out[0].at[2 * cx + cy, mine], out[0].at[2 * cx + cy, mine], (x, y, 1 - c)) for cx, cy in chips()]

    def arrivals(pin, out):
        x, y, c = _place()
        mine, theirs = _half_rows(buf, c)
        return [out[0].at[2 * cx + cy, rows] for rows in (mine, theirs) for cx, cy in chips()]

    return _Phase([buf], [jax.ShapeDtypeStruct(buf.shape, buf.dtype)], {0: 0}, 6, copies, arrivals,
                  own_starts=(3, 4, 5), own_waits=range(6))


def _rs_sibling(grads, name):
    n = len(grads)

    def body(*refs):
        g, out, send_sems, recv_sems = refs[:n], refs[n:2 * n], refs[2 * n], refs[2 * n + 1]
        x, y, c = _place()
        copies = []
        for a in range(n):
            half = grads[a].shape[1] // 2
            cp = pltpu.make_async_remote_copy(src_ref=g[a].at[:, pl.ds((1 - c) * half, half)], dst_ref=out[a],
                                              send_sem=send_sems.at[a], recv_sem=recv_sems.at[a],
                                              device_id=(x, y, 1 - c), device_id_type=MESH)
            cp.start()
            copies.append(cp)
        for cp in copies:
            cp.wait()

    anyspace = pl.BlockSpec(memory_space=pl.ANY)
    return pl.pallas_call(
        body, name=name, in_specs=[anyspace] * n, out_specs=[anyspace] * n,
        out_shape=[jax.ShapeDtypeStruct((N_CHIPS, g.shape[1] // 2, g.shape[2]), g.dtype) for g in grads],
        scratch_shapes=[pltpu.SemaphoreType.DMA((n,)), pltpu.SemaphoreType.DMA((n,))],
    )(*grads)


def _rs_add_sibling(place, grad, got, name):
    _, R, C = grad.shape
    half = R // 2
    tr = _tile(half, max(16, (1 << 19) // C // 16 * 16), 16)
    nr = half // tr

    def body(p_ref, a_ref, b_ref, o_ref):
        o_ref[...] = (a_ref[...].astype(f32) + b_ref[...].astype(f32)).astype(o_ref.dtype)

    return pl.pallas_call(
        body, name=name,
        grid_spec=pltpu.PrefetchScalarGridSpec(
            num_scalar_prefetch=1, grid=(N_CHIPS, nr),
            in_specs=[pl.BlockSpec((1, tr, C), lambda k, r, p: (k, p[0] * nr + r, 0)),
                      pl.BlockSpec((1, tr, C), lambda k, r, p: (k, r, 0))],
            out_specs=pl.BlockSpec((1, tr, C), lambda k, r, p: (k, r, 0))),
        out_shape=jax.ShapeDtypeStruct((N_CHIPS, half, C), bf16),
        compiler_params=_params(("parallel", "parallel")),
    )(place, grad, got)


def _rs_chips_phase(parts):
    n = len(parts)

    def copies(p, fc):
        x, y, c = _place()
        chips, ks = _other_chips(x, y)
        return [(p[a].at[ks[j]], fc[a].at[j], (*chips[j], c)) for a in range(n) for j in range(3)]

    shapes = [jax.ShapeDtypeStruct((3,) + q.shape[1:], q.dtype) for q in parts]
    return _Phase(parts, shapes, {}, 3 * n, copies, lambda p, fc: [fc[a].at[j] for a in range(n) for j in range(3)])


def _rs_hand_phase(parts, from_chips):
    n = len(parts)

    def copies(pin, fs):
        x, y, c = _place()
        sib = (x, y, 1 - c)
        own = [(pin[a].at[2 * x + y], fs[a].at[0], sib) for a in range(n)]
        return own + [(pin[n + a].at[j], fs[a].at[1 + j], sib) for a in range(n) for j in range(3)]

    def arrivals(pin, fs):
        return [fs[a].at[0] for a in range(n)] + [fs[a].at[1 + j] for a in range(n) for j in range(3)]

    shapes = [jax.ShapeDtypeStruct((4,) + q.shape[1:], q.dtype) for q in parts]
    return _Phase(list(parts) + list(from_chips), shapes, {}, 4 * n, copies, arrivals)


class _Exchange:
    def __init__(self, place):
        self.place = place

    def feed(self, buf):
        return _feed_phase(buf)

    def gather(self, bufs, over_ici):
        return _gather_phase(bufs, over_ici)

    def pair_sums(self, names, grads):
        got = _rs_sibling(grads, "rs_sibling_" + names[0])
        return [_rs_add_sibling(self.place, g, r, "rs_add_" + n) for n, g, r in zip(names, grads, got)]

    def to_chips(self, parts):
        return _rs_chips_phase(parts)

    def to_sibling(self, parts, from_chips):
        return _rs_hand_phase(parts, from_chips)

    def hand_over(self, name, parts, from_chips):
        return _run_phases(name, [_rs_hand_phase(parts, from_chips)])


def _local_step(place, x, target, norm_mix, b_gate, rb_full, norm_ffn, norm_final, w_in, rest, exch):
    B, S, D = x.shape
    T = B * S
    x2 = x.reshape(T, D)
    tg2 = target.reshape(T, D)
    rope, decay = _rope_tables(S), _decay_tables()
    trows = _bias_rows(rb_full)
    g_fin = norm_final.reshape(1, D)

    mrg, ffn = ["w_ret_out", "w_att_out", "w_out"], ["w_ffn_gate", "w_ffn_up", "w_ffn_down"]
    (xn, proj), got = _in_proj(place, x2, norm_mix, _join([exch.feed(w_in), exch.gather([rest[n] for n in mrg], True)]))
    w_in, wb = got[0], {}
    (qr, kr, o, u, states), got = _ret_fwd(proj, B, S, rope, decay, _join([exch.gather([rest["w_ffn_gate"]], True),
                                                                         exch.gather(got[1:], False)]))
    wb.update(zip(mrg, got[1:]))
    (ao,), got = _att_fwd(proj, trows, B, S, _join([exch.gather([rest["w_ffn_up"], rest["w_ffn_down"]], True),
                                                    exch.gather(got[:1], False)]))
    wb["w_ffn_gate"] = got[2]
    w_ro, w_out = wb["w_ret_out"].reshape(-1, D), wb["w_out"].reshape(-1, D)
    (h1, yr, ya), got = _mix_fwd(x2, proj, u, ao, b_gate, w_ro, wb["w_att_out"], w_out, exch.gather(got[:2], False))
    wb.update(zip(ffn[1:], got))
    hn, a, b, f, dh2, part_fin = _ffn_fwd(h1, norm_ffn, wb["w_ffn_gate"], wb["w_ffn_up"], wb["w_ffn_down"], g_fin, tg2)

    da, db, dh1, part_ffn = _ffn_bwd(dh2, h1, norm_ffn, a, b, wb["w_ffn_gate"], wb["w_ffn_up"], wb["w_ffn_down"])
    ffn = ["w_ffn_down", "w_ffn_gate", "w_ffn_up"]
    p_ffn = exch.pair_sums(ffn, [_wgrad(f, dh2, 0, "wgrad_ffn_down")[0], _wgrad(hn, da, 1, "wgrad_ffn_gate")[0],
                                 _wgrad(hn, db, 1, "wgrad_ffn_up")[0]])
    (du, dao, dgl, mix, dyr, dya, part_bg), c_down = _mix_bwd(dh1, proj, yr, ya, b_gate, w_ro, wb["w_att_out"], w_out,
                                                               exch.to_chips(p_ffn[:1]))
    mrg = ["w_out", "w_ret_out", "w_att_out"]
    p_mrg = exch.pair_sums(mrg, [_wgrad(mix, dh1, 0, "wgrad_out")[0], _wgrad(u, dyr, 0, "wgrad_ret_out")[0],
                                 _wgrad(ao, dya, 1, "wgrad_att_out")[0]])
    (drq, drk, drv, drg), c_gate_up = _ret_bwd(proj, qr, kr, o, states, du, B, S, rope, decay, exch.to_chips(p_ffn[1:]))
    c_ffn = c_down + c_gate_up
    (daq, dak, dav, dvec), got = _att_bwd(proj, dao, trows, B, S, _join([exch.to_chips(p_mrg),
                                                                        exch.to_sibling(p_ffn, c_ffn)]))
    c_mrg, s_ffn = got[:len(mrg)], got[len(mrg):]
    dproj = jnp.concatenate([drq, drk, drv, drg, daq, dak, dav, dgl], axis=1)
    g_in, s_mrg = _wgrad(xn, dproj, 1, "wgrad_in", exch.to_sibling(p_mrg, c_mrg))
    p_in = exch.pair_sums(["w_in"], [g_in])
    (gx, part_mix), c_in = _in_proj_bwd(dproj, w_in, x2, norm_mix, dh1, exch.to_chips(p_in))
    s_in = exch.hand_over("rs_hand_w_in", p_in, c_in)
    gbig = dict(zip(ffn + mrg + ["w_in"], zip(p_ffn + p_mrg + p_in, c_ffn + c_mrg + c_in, s_ffn + s_mrg + s_in)))
    rows = lambda p, r: p.reshape(-1, 8, p.shape[-1])[:, r, :].sum(axis=0)
    lo = KWIN - 1 - (MAX_REL - 1)
    drb = jnp.concatenate([jnp.flip(dvec[:, lo:lo + N_REL - 1], axis=1), dvec[:, :lo].sum(axis=1, keepdims=True)], axis=1)
    gsmall = {
        "norm_mix": rows(part_mix, 0), "b_gate": rows(part_bg, 0), "rel_bias": drb, "norm_ffn": rows(part_ffn, 0),
        "norm_final": rows(part_fin, 0),
    }
    return rows(part_fin, 1), gx.reshape(B, S, D), gbig, gsmall


SMALL_ROWS = 16


def _pack_small(gs, loss_lanes):
    D = D_MODEL
    rb = jnp.pad(gs["rel_bias"].reshape(-1), (0, 3 * D - ATT_HEADS * N_REL)).reshape(3, D)
    rows = [gs["norm_mix"].reshape(1, D), gs["b_gate"].reshape(2, D), gs["norm_ffn"].reshape(1, D),
            gs["norm_final"].reshape(1, D), rb, loss_lanes.reshape(1, D)]
    used = sum(r.shape[0] for r in rows)
    return jnp.concatenate(rows + [jnp.zeros((SMALL_ROWS - used, D), f32)], axis=0)


def kernel(x, norm_mix, w_in, b_gate, rel_bias, w_ret_out, w_att_out, w_out, norm_ffn, w_ffn_gate, w_ffn_up, w_ffn_down, norm_final, loss_target, m_norm_mix, m_w_in, m_b_gate, m_rel_bias, m_w_ret_out, m_w_att_out, m_w_out, m_norm_ffn, m_w_ffn_gate, m_w_ffn_up, m_w_ffn_down, m_norm_final, v_norm_mix, v_w_in, v_b_gate, v_rel_bias, v_w_ret_out, v_w_att_out, v_w_out, v_norm_ffn, v_w_ffn_gate, v_w_ffn_up, v_w_ffn_down, v_norm_final):
    w = dict(norm_mix=norm_mix, w_in=w_in, b_gate=b_gate, rel_bias=rel_bias, w_ret_out=w_ret_out, w_att_out=w_att_out,
             w_out=w_out, norm_ffn=norm_ffn, w_ffn_gate=w_ffn_gate, w_ffn_up=w_ffn_up, w_ffn_down=w_ffn_down,
             norm_final=norm_final)
    m = dict(norm_mix=m_norm_mix, w_in=m_w_in, b_gate=m_b_gate, rel_bias=m_rel_bias, w_ret_out=m_w_ret_out,
             w_att_out=m_w_att_out, w_out=m_w_out, norm_ffn=m_norm_ffn, w_ffn_gate=m_w_ffn_gate, w_ffn_up=m_w_ffn_up,
             w_ffn_down=m_w_ffn_down, norm_final=m_norm_final)
    v = dict(norm_mix=v_norm_mix, w_in=v_w_in, b_gate=v_b_gate, rel_bias=v_rel_bias, w_ret_out=v_w_ret_out,
             w_att_out=v_w_att_out, w_out=v_w_out, norm_ffn=v_norm_ffn, w_ffn_gate=v_w_ffn_gate, w_ffn_up=v_w_ffn_up,
             w_ffn_down=v_w_ffn_down, norm_final=v_norm_final)
    xi, yi, ci = _place()
    k_me = 2 * xi + yi

    place = jnp.stack([ci, k_me]).astype(jnp.int32)
    big = [n for n, _ in BIG]

    bufs = {n: _cast_shard(place, w[n][0], "cast_" + n) for n in big}
    rest = {n: bufs[n] for n in big if n != "w_in"}
    nrel_loc = rel_bias.shape[-1]
    rb_all = _exchange_small(jnp.pad(rel_bias[0], ((0, 0), (0, 128 - nrel_loc))), "gather_rel_bias", False)
    rb_full = jnp.concatenate([rb_all[2 * k, :, :nrel_loc] for k in range(N_CHIPS)], axis=1)

    loss_lanes, grad_x, gbig, gsmall = _local_step(place, x, loss_target, norm_mix, b_gate, rb_full, norm_ffn, norm_final,
                                                   bufs["w_in"], rest, _Exchange(place))

    small = _exchange_small(_pack_small(gsmall, loss_lanes), "reduce_small", True)
    D = D_MODEL
    loss = jnp.sum(small[8])
    drb_full = small[5:8].reshape(-1)[:ATT_HEADS * N_REL].reshape(ATT_HEADS, N_REL)
    g = {
        "norm_mix": small[0:1], "b_gate": small[1:3].reshape(1, 2 * D), "norm_ffn": small[3:4], "norm_final": small[4],
        "rel_bias": lax.dynamic_slice_in_dim(drb_full, k_me * nrel_loc, nrel_loc, axis=1)[None],
    }

    delta, new_m, new_v = {}, {}, {}
    for n in big:
        g_, d_, m_, v_ = _adamw_sum(place, w[n][0], m[n][0], v[n][0], *gbig[n], "adamw_" + n)
        g[n], delta[n], new_m[n], new_v[n] = g_[None], d_[None], m_[None], v_[None]
    flat = lambda d: jnp.concatenate([d[n].reshape(-1) for n in SMALL])
    n_small = sum(int(np.prod(w[n].shape)) for n in SMALL)
    n_pad = -n_small % 1024
    packs = [jnp.pad(flat(d), (0, n_pad)).reshape(-1, 128) for d in (w, g, m, v)]
    outs = _adamw(*packs, "adamw_small")
    for res, dst in zip(outs, (delta, new_m, new_v)):
        off = 0
        fl = res.reshape(-1)
        for n in SMALL:
            sz = int(np.prod(w[n].shape))
            dst[n] = fl[off:off + sz].reshape(w[n].shape)
            off += sz

    return (loss, grad_x, *[g[n] for n in WEIGHTS], *[delta[n] for n in WEIGHTS], *[new_m[n] for n in WEIGHTS],
            *[new_v[n] for n in WEIGHTS])
```

```python
import functools

import numpy as np
import jax
import jax.numpy as jnp
from jax import lax
from jax.experimental import pallas as pl
from jax.experimental.pallas import tpu as pltpu

f32 = jnp.float32
bf16 = jnp.bfloat16

D_MODEL = 1024
CHUNK = 64
RET_HEADS = 4
RET_KEY_DIM = 128
RET_VAL_DIM = 256
ATT_HEADS = 8
ATT_HEAD_DIM = 64
ATT_W = ATT_HEADS * ATT_HEAD_DIM
BAND_CHUNKS = 8
PAD = BAND_CHUNKS * CHUNK
MAX_REL = 256
N_REL = CHUNK + MAX_REL
D_FF = 2816
N_IN = 6656
ROPE_BASE = 10000.0
EPS = 1e-6
NEG_INF = -1e30
C_RQ, C_RK, C_RV, C_RG, C_AQ, C_AK, C_AV, C_GL = 0, 512, 1024, 2048, 3072, 3584, 4096, 4608

ADAM_LR, ADAM_B1, ADAM_B2, ADAM_EPS, ADAM_WD, ADAM_STEP = 0.001, 0.9, 0.999, 1e-08, 0.01, 10

N_CHIPS = 4
N_DEV = 8
WGRAD_ACC_BYTES = 8 * 1024 * 1024
ROW_TILE = 512
BIG_ROW_TILE = 1024
QBLK = 256
KWIN = PAD + QBLK
TOEP = 1024
VMEM_LIMIT = 56 * 1024 * 1024
MESH = pl.DeviceIdType.MESH

BIG = (
    ("w_in", 1), ("w_ret_out", 0), ("w_att_out", 1), ("w_out", 0), ("w_ffn_gate", 1), ("w_ffn_up", 1), ("w_ffn_down", 0))
WEIGHTS = ("norm_mix", "w_in", "b_gate", "rel_bias", "w_ret_out", "w_att_out", "w_out", "norm_ffn", "w_ffn_gate",
           "w_ffn_up", "w_ffn_down", "norm_final")
SMALL = ("norm_mix", "b_gate", "rel_bias", "norm_ffn", "norm_final")


def _dot(a, b):
    return lax.dot_general(a, b, (((1,), (0,)), ((), ())), preferred_element_type=f32)


def _dot_nt(a, b):
    return lax.dot_general(a, b, (((1,), (1,)), ((), ())), preferred_element_type=f32)


def _dot_tn(a, b):
    return lax.dot_general(a, b, (((0,), (0,)), ((), ())), preferred_element_type=f32)


def _sig(x):
    return 1.0 / (1.0 + jnp.exp(-x))


def _tile(n, pref, mult):
    best = None
    for t in range(mult, min(n, pref) + 1, mult):
        if n % t == 0:
            best = t
    return best if best is not None else n


def _params(sem, vmem=VMEM_LIMIT):
    return pltpu.CompilerParams(dimension_semantics=sem, vmem_limit_bytes=vmem)


def _in_proj(place, x2, gamma, phase):
    T, D = x2.shape
    _, _, ns = phase.arrays[0].shape
    tm = _tile(T, BIG_ROW_TILE, 8)
    ni = T // tm

    def body(p_ref, x_ref, g_ref, xn_ref, pr_ref, xs_ref, w_ref, w_sem, carried):
        j, i = pl.program_id(0), pl.program_id(1)
        pin, pout, sems = carried
        rows = pl.ds(pl.multiple_of(i * tm, tm), tm)

        @pl.when(i == 0)
        def _():
            for f in range(1, N_CHIPS):
                @pl.when(j == f)
                def _():
                    phase.arrived(f - 1, pin, pout, *sems)
                    phase.begin(2 + f, pin, pout, *sems)
                    phase.arrived(2 + f, pin, pout, *sems)
            shard = pltpu.make_async_copy(pout[0].at[jnp.bitwise_xor(p_ref[1], j)], w_ref, w_sem)
            shard.start()
            shard.wait()

        @pl.when(j == 0)
        def _():
            x = x_ref[...]
            r = lax.rsqrt(jnp.mean(x * x, axis=-1, keepdims=True) + EPS)
            xn = (x * r * g_ref[...]).astype(bf16)
            xs_ref[rows, :] = xn
            xn_ref[...] = xn

        pr_ref[...] = _dot(xs_ref[rows, :], w_ref[...]).astype(bf16)

    first_pass = lambda j, i, p: (jnp.where(j == 0, i, ni - 1), 0)
    return _call(
        body, phase, name="in_proj", grid=(N_CHIPS, ni), prefetch=(place,), expose=True,
        in_specs=[pl.BlockSpec((tm, D), first_pass), pl.BlockSpec((1, D), lambda j, i, p: (0, 0))],
        out_specs=[pl.BlockSpec((tm, D), first_pass),
                   pl.BlockSpec((tm, ns), lambda j, i, p: (i, jnp.bitwise_xor(p[1], j)))],
        out_shape=[jax.ShapeDtypeStruct((T, D), bf16), jax.ShapeDtypeStruct((T, N_CHIPS * ns), bf16)],
        scratch_shapes=[pltpu.VMEM((T, D), bf16), pltpu.VMEM((D, ns), bf16), pltpu.SemaphoreType.DMA],
        args=(x2, gamma))


def _rope_tables(S):
    d = RET_KEY_DIM
    freqs = ROPE_BASE ** (-jnp.arange(0, d, 2, dtype=f32) / d)
    ang = jnp.arange(S, dtype=f32)[:, None] * freqs[None, :]
    cos, sin = jnp.cos(ang), jnp.sin(ang)
    return jnp.concatenate([cos, cos], axis=1), jnp.concatenate([-sin, sin], axis=1)


def _decay_tables():
    H = RET_HEADS
    log_g = jnp.log(1.0 - 2.0 ** (-5.0 - jnp.arange(H, dtype=f32)))
    p = jnp.arange(CHUNK, dtype=f32)
    intra = jnp.exp(log_g[:, None, None] * jnp.abs(p[:, None] - p[None, :]))
    q_dec = jnp.exp(log_g[:, None] * (p[None, :] + 1.0))
    k_dec = jnp.exp(log_g[:, None] * (CHUNK - 1.0 - p[None, :]))
    c_dec = jnp.exp(log_g * CHUNK)
    q_dec = jnp.broadcast_to(q_dec[:, :, None], (H, CHUNK, RET_KEY_DIM))
    k_dec = jnp.broadcast_to(k_dec[:, :, None], (H, CHUNK, RET_KEY_DIM))
    c_dec = jnp.broadcast_to(c_dec[:, None, None], (H, 1, RET_VAL_DIM))
    return intra, q_dec, k_dec, c_dec


K_SCALE = RET_KEY_DIM ** -0.5


RET_CHUNKS = 4


def _ret_tables_specs():
    whole = lambda *shape: pl.BlockSpec(shape, lambda b, i: (0,) * len(shape))
    return [whole(RET_HEADS, CHUNK, CHUNK), whole(RET_HEADS, CHUNK, RET_KEY_DIM), whole(RET_HEADS, CHUNK, RET_KEY_DIM),
            whole(RET_HEADS, 1, RET_VAL_DIM)]


def _rotate(x, cos, sn):
    return x * cos + pltpu.roll(x, RET_KEY_DIM // 2, 1) * sn


def _ret_fwd(proj, B, S, rope, decay, phase=None):
    T = B * S
    nc = S // CHUNK
    H, dk, dv = RET_HEADS, RET_KEY_DIM, RET_VAL_DIM
    sb = RET_CHUNKS * CHUNK
    ns = S // sb

    def body(q_ref, k_ref, v_ref, g_ref, cos_ref, sin_ref, intra_ref, qd_ref, kd_ref, cd_ref,
             qr_ref, kr_ref, o_ref, u_ref, st_ref, state_ref):
        @pl.when(pl.program_id(1) == 0)
        def _():
            state_ref[...] = jnp.zeros_like(state_ref)

        cos, sn = cos_ref[...], sin_ref[...]
        for h in range(H):
            hs = slice(h * dk, (h + 1) * dk)
            qr_ref[:, hs] = _rotate(q_ref[:, hs].astype(f32), cos, sn).astype(bf16)
            kr_ref[:, hs] = (_rotate(k_ref[:, hs].astype(f32), cos, sn) * K_SCALE).astype(bf16)
        states = [state_ref[h] for h in range(H)]
        for ci in range(RET_CHUNKS):
            r = slice(ci * CHUNK, (ci + 1) * CHUNK)
            for h in range(H):
                hk, hv = slice(h * dk, (h + 1) * dk), slice(h * dv, (h + 1) * dv)
                qi, ki, vi = qr_ref[r, hk], kr_ref[r, hk], v_ref[r, hv]
                stb = states[h].astype(bf16)
                st_ref[0, h, ci] = stb
                s = (_dot_nt(qi, ki) * intra_ref[h]).astype(bf16)
                o = _dot(s, vi) + _dot((qi.astype(f32) * qd_ref[h]).astype(bf16), stb)
                states[h] = states[h] * cd_ref[h] + _dot_tn((ki.astype(f32) * kd_ref[h]).astype(bf16), vi)
                mu = jnp.mean(o, axis=-1, keepdims=True)
                xc = o - mu
                var = jnp.mean(xc * xc, axis=-1, keepdims=True)
                oh = xc * lax.rsqrt(var + EPS)
                g = g_ref[r, hv].astype(f32)
                o_ref[r, hv] = o.astype(bf16)
                u_ref[r, hv] = (g * _sig(g) * oh).astype(bf16)
        for h in range(H):
            state_ref[h] = states[h]

    blk = lambda w, c: pl.BlockSpec((sb, w), lambda b, i: (b * ns + i, c))
    return _call(
        body, phase, name="ret_fwd", grid=(B, ns), scratch_shapes=[pltpu.VMEM((H, dk, dv), f32)],
        in_specs=[blk(H * dk, C_RQ // (H * dk)), blk(H * dk, C_RK // (H * dk)), blk(H * dv, C_RV // (H * dv)),
                  blk(H * dv, C_RG // (H * dv)),
                  pl.BlockSpec((sb, dk), lambda b, i: (i, 0)), pl.BlockSpec((sb, dk), lambda b, i: (i, 0)),
                  *_ret_tables_specs()],
        out_specs=[blk(H * dk, 0), blk(H * dk, 0), blk(H * dv, 0), blk(H * dv, 0),
                   pl.BlockSpec((1, H, RET_CHUNKS, dk, dv), lambda b, i: (b, 0, i, 0, 0))],
        out_shape=[jax.ShapeDtypeStruct((T, H * dk), bf16), jax.ShapeDtypeStruct((T, H * dk), bf16),
                   jax.ShapeDtypeStruct((T, H * dv), bf16), jax.ShapeDtypeStruct((T, H * dv), bf16),
                   jax.ShapeDtypeStruct((B, H, nc, dk, dv), bf16)],
        args=(proj, proj, proj, proj, *rope, *decay))


def _bias_rows(rb):
    last = rb[:, N_REL - 1:]
    return jnp.concatenate([
        jnp.broadcast_to(last, (ATT_HEADS, PAD - MAX_REL + 1)),
        jnp.flip(rb[:, :N_REL - 1], axis=1),
        jnp.broadcast_to(rb[:, :1], (ATT_HEADS, KWIN - PAD - CHUNK)),
        jnp.broadcast_to(last, (ATT_HEADS, TOEP - KWIN)),
    ], axis=1)


def _build_bias(t_ref, bias_ref):
    row = lax.broadcasted_iota(jnp.int32, (QBLK, KWIN), 0) // CHUNK
    col = lax.broadcasted_iota(jnp.int32, (QBLK, KWIN), 1) // CHUNK
    delta = BAND_CHUNKS + row - col
    vis = (delta >= 0) & (delta <= BAND_CHUNKS)
    for h in range(ATT_HEADS):
        t = jnp.broadcast_to(t_ref[h:h + 1, :], (QBLK, TOEP))
        rolled = pltpu.roll(t, 0, 1, stride=1, stride_axis=0)
        bias_ref[h] = jnp.where(vis, rolled[:, :KWIN], NEG_INF)


def _att_probs(qh, kh, bias):
    s = _dot_nt(qh, kh) * (ATT_HEAD_DIM ** -0.5) + bias
    m = jnp.max(s, axis=-1, keepdims=True)
    p = jnp.exp(s - m)
    return p * (1.0 / jnp.sum(p, axis=-1, keepdims=True))


def _by_window(i, step):
    sizes = list(range(QBLK, KWIN, QBLK))
    for n, nk in enumerate(sizes):
        pl.when(i == n)(functools.partial(step, nk))
    pl.when(i >= len(sizes))(functools.partial(step, KWIN))


def _att_fwd(proj, trows, B, S, phase=None):
    T = B * S
    nq = S // QBLK
    dh = ATT_HEAD_DIM

    def body(q_ref, k_ref, v_ref, t_ref, o_ref, bias_ref):
        i = pl.program_id(1)

        @pl.when((pl.program_id(0) == 0) & (i == 0))
        def _():
            _build_bias(t_ref, bias_ref)

        def step(nk):
            win = pl.ds(pl.multiple_of((i + 1) * QBLK - nk, QBLK), nk)
            kw, vw = k_ref[win, :], v_ref[win, :]
            outs = []
            for h in range(ATT_HEADS):
                hs = slice(h * dh, (h + 1) * dh)
                pr = _att_probs(q_ref[:, hs], kw[:, hs], bias_ref[h, :, KWIN - nk:])
                outs.append(_dot(pr.astype(bf16), vw[:, hs]))
            o_ref[...] = jnp.concatenate(outs, axis=1).astype(bf16)

        _by_window(i, step)

    return _call(
        body, phase, name="att_fwd", grid=(B, nq),
        in_specs=[pl.BlockSpec((QBLK, ATT_W), lambda b, i: (b * nq + i, C_AQ // ATT_W)),
                  pl.BlockSpec((S, ATT_W), lambda b, i: (b, C_AK // ATT_W)),
                  pl.BlockSpec((S, ATT_W), lambda b, i: (b, C_AV // ATT_W)),
                  pl.BlockSpec((ATT_HEADS, TOEP), lambda b, i: (0, 0))],
        out_specs=[pl.BlockSpec((QBLK, ATT_W), lambda b, i: (b * nq + i, 0))],
        out_shape=[jax.ShapeDtypeStruct((T, ATT_W), bf16)],
        scratch_shapes=[pltpu.VMEM((ATT_HEADS, QBLK, KWIN), f32)],
        args=(proj, proj, proj, trows))


def _gl_specs(tm):
    w = 512
    return [pl.BlockSpec((tm, w), functools.partial(lambda i, j: (i, C_GL // 512 + j), j=j)) for j in range(4)]


def _gates(gl_refs, bg_ref):
    gl = jnp.concatenate([r[...] for r in gl_refs], axis=1).astype(f32) + bg_ref[...]
    g = _sig(gl)
    return g[:, :D_MODEL], g[:, D_MODEL:]


def _mix_fwd(x2, proj, u, ao, b_gate, w_ro, w_ao, w_out, phase=None):
    T, D = x2.shape
    tm = _tile(T, ROW_TILE, 8)

    def body(x_ref, u_ref, ao_ref, g0, g1, g2, g3, bg_ref, wro_ref, wao_ref, wo_ref, h1_ref, yr_ref, ya_ref):
        yr = _dot(u_ref[...], wro_ref[...])
        ao = ao_ref[...]
        ya = jnp.concatenate([_dot(ao, wao_ref[k]) for k in range(N_CHIPS)], axis=1)
        gr, ga = _gates((g0, g1, g2, g3), bg_ref)
        mix = gr * yr + ga * ya
        h1_ref[...] = x_ref[...] + _dot(mix.astype(bf16), wo_ref[...])
        yr_ref[...] = yr.astype(bf16)
        ya_ref[...] = ya.astype(bf16)

    full = lambda a: pl.BlockSpec(a.shape, lambda i: (0,) * a.ndim)
    row = lambda n: pl.BlockSpec((tm, n), lambda i: (i, 0))
    return _call(
        body, phase, name="mix_fwd", grid=(T // tm,), scratch_shapes=[],
        in_specs=[row(D), row(D), row(ATT_W), *_gl_specs(tm), full(b_gate), full(w_ro), full(w_ao), full(w_out)],
        out_specs=[row(D), row(D), row(D)],
        out_shape=[jax.ShapeDtypeStruct((T, D), f32), jax.ShapeDtypeStruct((T, D), bf16),
                   jax.ShapeDtypeStruct((T, D), bf16)],
        args=(x2, u, ao, proj, proj, proj, proj, b_gate, w_ro, w_ao, w_out))


def _ffn_fwd(h1, g_ffn, wg, wu, wd, g_fin, target):
    T, D = h1.shape
    nf, _, tf = wg.shape
    tm = _tile(T, ROW_TILE, 8)

    def body(h1_ref, g_ref, wg_ref, wu_ref, wd_ref, gf_ref, tg_ref, hn_ref, a_ref, b_ref, f_ref, dh2_ref, part_ref):
        h1v = h1_ref[...]
        r = lax.rsqrt(jnp.mean(h1v * h1v, axis=-1, keepdims=True) + EPS)
        hn = (h1v * r * g_ref[...]).astype(bf16)
        hn_ref[...] = hn
        h2 = h1v
        for k in range(nf):
            a = _dot(hn, wg_ref[k])
            b = _dot(hn, wu_ref[k])
            f = ((a * _sig(a)) * b).astype(bf16)
            a_ref[k] = a.astype(bf16)
            b_ref[k] = b.astype(bf16)
            f_ref[k] = f
            h2 = h2 + _dot(f, wd_ref[k])
        r = lax.rsqrt(jnp.mean(h2 * h2, axis=-1, keepdims=True) + EPS)
        n = h2 * r
        gf = gf_ref[...]
        e = n * gf - tg_ref[...]
        dy = e * (1.0 / D)
        dn = dy * gf
        dh2_ref[...] = r * (dn - n * jnp.mean(dn * n, axis=-1, keepdims=True))
        part_ref[...] = jnp.zeros_like(part_ref)
        part_ref[0:1, :] = jnp.sum(dy * n, axis=0, keepdims=True)
        part_ref[1:2, :] = (0.5 / D) * jnp.sum(e * e, axis=0, keepdims=True)

    row = lambda n: pl.BlockSpec((tm, n), lambda i: (i, 0))
    vec = pl.BlockSpec((1, D), lambda i: (0, 0))
    col = pl.BlockSpec((nf, tm, tf), lambda i: (0, i, 0))
    held = lambda w: pl.BlockSpec(w.shape, lambda i: (0, 0, 0), pipeline_mode=pl.Buffered(1))
    act = jax.ShapeDtypeStruct((nf, T, tf), bf16)
    return pl.pallas_call(
        body, name="ffn_fwd", grid=(T // tm,),
        in_specs=[row(D), vec, held(wg), held(wu), held(wd), vec, row(D)],
        out_specs=[row(D), col, col, col, row(D), pl.BlockSpec((8, D), lambda i: (i, 0))],
        out_shape=[jax.ShapeDtypeStruct((T, D), bf16), act, act, act,
                   jax.ShapeDtypeStruct((T, D), f32), jax.ShapeDtypeStruct((T // tm * 8, D), f32)],
        compiler_params=_params(("parallel",)),
    )(h1, g_ffn, wg, wu, wd, g_fin, target)


def _ffn_bwd(dh2, h1, g_ffn, a, b, wg, wu, wd):
    T, D = h1.shape
    nf, _, tf = wg.shape
    tm = _tile(T, ROW_TILE // 2, 8)

    def body(dh2_ref, h1_ref, g_ref, a_ref, b_ref, wg_ref, wu_ref, wd_ref, da_ref, db_ref, dh1_ref, part_ref):
        dh2v = dh2_ref[...]
        dh2b = dh2v.astype(bf16)
        dhn = jnp.zeros((tm, D), f32)
        for k in range(nf):
            df = _dot_nt(dh2b, wd_ref[k])
            av = a_ref[k].astype(f32)
            sg = _sig(av)
            db = (df * (av * sg)).astype(bf16)
            da = (df * b_ref[k].astype(f32) * (sg * (1.0 + av * (1.0 - sg)))).astype(bf16)
            da_ref[k] = da
            db_ref[k] = db
            dhn = dhn + _dot_nt(da, wg_ref[k]) + _dot_nt(db, wu_ref[k])
        h = h1_ref[...]
        r = lax.rsqrt(jnp.mean(h * h, axis=-1, keepdims=True) + EPS)
        n = h * r
        dn = dhn * g_ref[...]
        dh1_ref[...] = dh2v + r * (dn - n * jnp.mean(dn * n, axis=-1, keepdims=True))
        part_ref[...] = jnp.zeros_like(part_ref)
        part_ref[0:1, :] = jnp.sum(dhn * n, axis=0, keepdims=True)

    row = lambda n: pl.BlockSpec((tm, n), lambda i: (i, 0))
    col = pl.BlockSpec((nf, tm, tf), lambda i: (0, i, 0))
    held = lambda w: pl.BlockSpec(w.shape, lambda i: (0, 0, 0), pipeline_mode=pl.Buffered(1))
    act = jax.ShapeDtypeStruct((nf, T, tf), bf16)
    return pl.pallas_call(
        body, name="ffn_bwd", grid=(T // tm,),
        in_specs=[row(D), row(D), pl.BlockSpec((1, D), lambda i: (0, 0)), col, col, held(wg), held(wu), held(wd)],
        out_specs=[col, col, row(D), pl.BlockSpec((8, D), lambda i: (i, 0))],
        out_shape=[act, act, jax.ShapeDtypeStruct((T, D), f32), jax.ShapeDtypeStruct((T // tm * 8, D), f32)],
        compiler_params=_params(("parallel",)),
    )(dh2, h1, g_ffn, a, b, wg, wu, wd)


def _mix_bwd(dh1, proj, yr, ya, b_gate, w_ro, w_ao, w_out, phase=None):
    T, D = dh1.shape
    tm = _tile(T, ROW_TILE, 8)

    def body(dh1_ref, g0, g1, g2, g3, bg_ref, yr_ref, ya_ref, wro_ref, wao_ref, wo_ref,
             du_ref, dao_ref, dgl_ref, mix_ref, dyr_ref, dya_ref, part_ref):
        dmix = _dot_nt(dh1_ref[...].astype(bf16), wo_ref[...])
        gr, ga = _gates((g0, g1, g2, g3), bg_ref)
        yr = yr_ref[...].astype(f32)
        ya = ya_ref[...].astype(f32)
        dyr = (dmix * gr).astype(bf16)
        dya = (dmix * ga).astype(bf16)
        dgl = jnp.concatenate([dmix * yr * gr * (1.0 - gr), dmix * ya * ga * (1.0 - ga)], axis=1)
        du_ref[...] = _dot_nt(dyr, wro_ref[...]).astype(bf16)
        ns = wao_ref.shape[2]
        dao = _dot_nt(dya[:, :ns], wao_ref[0])
        for k in range(1, N_CHIPS):
            dao = dao + _dot_nt(dya[:, k * ns:(k + 1) * ns], wao_ref[k])
        dao_ref[...] = dao.astype(bf16)
        dgl_ref[...] = dgl.astype(bf16)
        mix_ref[...] = (gr * yr + ga * ya).astype(bf16)
        dyr_ref[...] = dyr
        dya_ref[...] = dya
        part_ref[...] = jnp.zeros_like(part_ref)
        part_ref[0:1, :] = jnp.sum(dgl, axis=0, keepdims=True)

    full = lambda a: pl.BlockSpec(a.shape, lambda i: (0,) * a.ndim)
    row = lambda n: pl.BlockSpec((tm, n), lambda i: (i, 0))
    return _call(
        body, phase, name="mix_bwd", grid=(T // tm,), scratch_shapes=[],
        in_specs=[row(D), *_gl_specs(tm), full(b_gate), row(D), row(D), full(w_ro), full(w_ao), full(w_out)],
        out_specs=[row(D), row(ATT_W), row(2 * D), row(D), row(D), row(D), pl.BlockSpec((8, 2 * D), lambda i: (i, 0))],
        out_shape=[jax.ShapeDtypeStruct((T, D), bf16), jax.ShapeDtypeStruct((T, ATT_W), bf16),
                   jax.ShapeDtypeStruct((T, 2 * D), bf16), jax.ShapeDtypeStruct((T, D), bf16),
                   jax.ShapeDtypeStruct((T, D), bf16), jax.ShapeDtypeStruct((T, D), bf16),
                   jax.ShapeDtypeStruct((T // tm * 8, 2 * D), f32)],
        args=(dh1, proj, proj, proj, proj, b_gate, yr, ya, w_ro, w_ao, w_out))


def _ret_bwd(proj, qr, kr, o, states, du, B, S, rope, decay, phase=None):
    T = B * S
    nc = S // CHUNK
    H, dk, dv = RET_HEADS, RET_KEY_DIM, RET_VAL_DIM

    sb = RET_CHUNKS * CHUNK
    ns = S // sb

    def body(qr_ref, kr_ref, v_ref, g_ref, o_ref, st_ref, du_ref, cos_ref, sin_ref, intra_ref, qd_ref, kd_ref, cd_ref,
             dq_ref, dk_ref, dv_ref, dg_ref, dstate_ref):
        @pl.when(pl.program_id(1) == 0)
        def _():
            dstate_ref[...] = jnp.zeros_like(dstate_ref)

        cos, snb = cos_ref[...], -sin_ref[...]
        dstates = [dstate_ref[h] for h in range(H)]
        for ci in reversed(range(RET_CHUNKS)):
            r = slice(ci * CHUNK, (ci + 1) * CHUNK)
            for h in range(H):
                hk, hv = slice(h * dk, (h + 1) * dk), slice(h * dv, (h + 1) * dv)
                intra, qd, kd = intra_ref[h], qd_ref[h], kd_ref[h]
                qi, ki, vi = qr_ref[r, hk], kr_ref[r, hk], v_ref[r, hv]
                si = st_ref[0, h, ci]
                o = o_ref[r, hv].astype(f32)
                mu = jnp.mean(o, axis=-1, keepdims=True)
                xc = o - mu
                rstd = lax.rsqrt(jnp.mean(xc * xc, axis=-1, keepdims=True) + EPS)
                oh = xc * rstd
                g = g_ref[r, hv].astype(f32)
                sg = _sig(g)
                dui = du_ref[r, hv].astype(f32)
                dg_ref[r, hv] = (dui * oh * (sg * (1.0 + g * (1.0 - sg)))).astype(bf16)
                doh = dui * (g * sg)
                do = rstd * (doh - jnp.mean(doh, axis=-1, keepdims=True)
                             - oh * jnp.mean(doh * oh, axis=-1, keepdims=True))
                dob = do.astype(bf16)
                p = (_dot_nt(qi, ki) * intra).astype(bf16)
                dsb = dstates[h].astype(bf16)
                kt = (ki.astype(f32) * kd).astype(bf16)
                qt = (qi.astype(f32) * qd).astype(bf16)
                dv_ref[r, hv] = (_dot_tn(p, dob) + _dot(kt, dsb)).astype(bf16)
                da = (_dot_nt(dob, vi) * intra).astype(bf16)
                dq = _dot(da, ki) + _dot_nt(dob, si) * qd
                dkk = (_dot_tn(da, qi) + _dot_nt(vi, dsb) * kd) * K_SCALE
                dq_ref[r, hk] = _rotate(dq, cos[r], snb[r]).astype(bf16)
                dk_ref[r, hk] = _rotate(dkk, cos[r], snb[r]).astype(bf16)
                dstates[h] = dstates[h] * cd_ref[h] + _dot_tn(qt, dob)
        for h in range(H):
            dstate_ref[h] = dstates[h]

    blk = lambda w, c: pl.BlockSpec((sb, w), lambda b, i: (b * ns + ns - 1 - i, c))
    return _call(
        body, phase, name="ret_bwd", grid=(B, ns),
        in_specs=[blk(H * dk, 0), blk(H * dk, 0), blk(H * dv, C_RV // (H * dv)), blk(H * dv, C_RG // (H * dv)),
                  blk(H * dv, 0),
                  pl.BlockSpec((1, H, RET_CHUNKS, dk, dv), lambda b, i: (b, 0, ns - 1 - i, 0, 0)),
                  blk(H * dv, 0),
                  pl.BlockSpec((sb, dk), lambda b, i: (ns - 1 - i, 0)), pl.BlockSpec((sb, dk), lambda b, i: (ns - 1 - i, 0)),
                  *_ret_tables_specs()],
        out_specs=[blk(H * dk, 0), blk(H * dk, 0), blk(H * dv, 0), blk(H * dv, 0)],
        out_shape=[jax.ShapeDtypeStruct((T, H * dk), bf16), jax.ShapeDtypeStruct((T, H * dk), bf16),
                   jax.ShapeDtypeStruct((T, H * dv), bf16), jax.ShapeDtypeStruct((T, H * dv), bf16)],
        scratch_shapes=[pltpu.VMEM((H, dk, dv), f32)],
        args=(qr, kr, proj, proj, o, states, du, *rope, *decay))


def _att_bwd(proj, dao, trows, B, S, phase=None):
    T = B * S
    nq = S // QBLK
    dh = ATT_HEAD_DIM
    scale = ATT_HEAD_DIM ** -0.5

    def body(q_ref, k_ref, v_ref, do_ref, t_ref, dq_ref, dk_ref, dv_ref, vec_ref, bias_ref, dbias_ref, dka_ref, dva_ref):
        b, i = pl.program_id(0), pl.program_id(1)

        @pl.when((b == 0) & (i == 0))
        def _():
            _build_bias(t_ref, bias_ref)
            dbias_ref[...] = jnp.zeros_like(dbias_ref)

        @pl.when(i == 0)
        def _():
            dka_ref[...] = jnp.zeros_like(dka_ref)
            dva_ref[...] = jnp.zeros_like(dva_ref)

        def step(nk):
            win = pl.ds(pl.multiple_of((i + 1) * QBLK - nk, QBLK), nk)
            kw, vw = k_ref[win, :], v_ref[win, :]
            dqs, dks, dvs = [], [], []
            for h in range(ATT_HEADS):
                hs = slice(h * dh, (h + 1) * dh)
                qh, kh, vh, doh = q_ref[:, hs], kw[:, hs], vw[:, hs], do_ref[:, hs]
                pr = _att_probs(qh, kh, bias_ref[h, :, KWIN - nk:])
                dp = _dot_nt(doh, vh)
                ds = pr * (dp - jnp.sum(pr * dp, axis=-1, keepdims=True))
                dbias_ref[h, :, KWIN - nk:] += ds
                dsb = (ds * scale).astype(bf16)
                dqs.append(_dot(dsb, kh))
                dks.append(_dot_tn(dsb, qh))
                dvs.append(_dot_tn(pr.astype(bf16), doh))
            dq_ref[...] = jnp.concatenate(dqs, axis=1).astype(bf16)
            dka_ref[win, :] += jnp.concatenate(dks, axis=1)
            dva_ref[win, :] += jnp.concatenate(dvs, axis=1)

        _by_window(i, step)

        @pl.when(i == nq - 1)
        def _():
            dk_ref[...] = dka_ref[...].astype(bf16)
            dv_ref[...] = dva_ref[...].astype(bf16)

        @pl.when((b == B - 1) & (i == nq - 1))
        def _():
            rr = lax.broadcasted_iota(jnp.int32, (QBLK, QBLK), 0)
            cc = lax.broadcasted_iota(jnp.int32, (QBLK, QBLK), 1)
            flip = jnp.where(rr + cc == QBLK - 1, 1.0, 0.0).astype(bf16)
            for h in range(ATT_HEADS):
                d = dbias_ref[h]
                hi = d.astype(bf16)
                lo = (d - hi.astype(f32)).astype(bf16)
                rev = _dot(flip, hi) + _dot(flip, lo)
                wide = jnp.concatenate([rev, jnp.zeros((QBLK, TOEP - KWIN), f32)], axis=1)
                rolled = pltpu.roll(wide, 0, 1, stride=1, stride_axis=0)
                vec_ref[h:h + 1, :] = jnp.sum(rolled, axis=0, keepdims=True)

    qspec = lambda c: pl.BlockSpec((QBLK, ATT_W), lambda b, i: (b * nq + i, c))
    kspec = lambda c: pl.BlockSpec((S, ATT_W), lambda b, i: (b, c))
    seq = jax.ShapeDtypeStruct((T, ATT_W), bf16)
    return _call(
        body, phase, name="att_bwd", grid=(B, nq),
        in_specs=[qspec(C_AQ // ATT_W), kspec(C_AK // ATT_W), kspec(C_AV // ATT_W), qspec(0),
                  pl.BlockSpec((ATT_HEADS, TOEP), lambda b, i: (0, 0))],
        out_specs=[qspec(0), kspec(0), kspec(0), pl.BlockSpec((ATT_HEADS, TOEP), lambda b, i: (0, 0))],
        out_shape=[seq, seq, seq, jax.ShapeDtypeStruct((ATT_HEADS, TOEP), f32)],
        scratch_shapes=[pltpu.VMEM((ATT_HEADS, QBLK, KWIN), f32), pltpu.VMEM((ATT_HEADS, QBLK, KWIN), f32),
                        pltpu.VMEM((S, ATT_W), f32), pltpu.VMEM((S, ATT_W), f32)],
        args=(proj, proj, proj, dao, trows))


def _in_proj_bwd(dproj, w_in, x2, gamma, dh1, phase=None):
    T, D = x2.shape
    nk, _, tk = w_in.shape
    tm = _tile(T, BIG_ROW_TILE, 8)

    def body(dp_ref, w_ref, x_ref, g_ref, dh1_ref, dx_ref, part_ref, acc_ref):
        j = pl.program_id(1)

        @pl.when(j == 0)
        def _():
            acc_ref[...] = jnp.zeros_like(acc_ref)

        acc_ref[...] += _dot_nt(dp_ref[...], w_ref[0])

        @pl.when(j == nk - 1)
        def _():
            x = x_ref[...]
            r = lax.rsqrt(jnp.mean(x * x, axis=-1, keepdims=True) + EPS)
            n = x * r
            dxn = acc_ref[...]
            dn = dxn * g_ref[...]
            dx_ref[...] = dh1_ref[...] + r * (dn - n * jnp.mean(dn * n, axis=-1, keepdims=True))
            part_ref[...] = jnp.zeros_like(part_ref)
            part_ref[0:1, :] = jnp.sum(dxn * n, axis=0, keepdims=True)

    row = lambda n: pl.BlockSpec((tm, n), lambda i, j: (i, 0))
    return _call(
        body, phase, name="in_proj_bwd", grid=(T // tm, nk),
        in_specs=[pl.BlockSpec((tm, tk), lambda i, j: (i, j)), pl.BlockSpec((1, D, tk), lambda i, j: (j, 0, 0)), row(D),
                  pl.BlockSpec((1, D), lambda i, j: (0, 0)), row(D)],
        out_specs=[row(D), pl.BlockSpec((8, D), lambda i, j: (i, 0))],
        out_shape=[jax.ShapeDtypeStruct((T, D), f32), jax.ShapeDtypeStruct((T // tm * 8, D), f32)],
        scratch_shapes=[pltpu.VMEM((tm, D), f32)],
        args=(dproj, w_in, x2, gamma, dh1))


def _wgrad(a, b, shard_axis, name, phase=None):
    def spec(arr, sharded, tt):
        if arr.ndim == 3:
            return arr.shape[2], pl.BlockSpec((1, tt, arr.shape[2]), lambda s, t: (s, t, 0))
        if sharded:
            w = arr.shape[1] // N_CHIPS
            return w, pl.BlockSpec((tt, w), lambda s, t: (t, s))
        return arr.shape[1], pl.BlockSpec((tt, arr.shape[1]), lambda s, t: (t, 0))

    T = a.shape[-2]
    tt = _tile(T, BIG_ROW_TILE, 16)
    nt = T // tt
    whole = a.ndim == 2 and b.ndim == 2 and a.shape[1] * b.shape[1] * 4 <= WGRAD_ACC_BYTES
    if whole:
        K, N = a.shape[1], b.shape[1]
        a_spec, b_spec = pl.BlockSpec((tt, K), lambda s, t: (t, 0)), pl.BlockSpec((tt, N), lambda s, t: (t, 0))
        out_block = (N_CHIPS, K // N_CHIPS, N) if shard_axis == 0 else (N_CHIPS, K, N // N_CHIPS)
        out_spec = pl.BlockSpec(out_block, lambda s, t: (0, 0, 0))
    else:
        K, a_spec = spec(a, shard_axis == 0, tt)
        N, b_spec = spec(b, shard_axis == 1, tt)
        out_block = (N_CHIPS, K, N)
        out_spec = pl.BlockSpec((1, K, N), lambda s, t: (s, 0, 0))

    def body(a_ref, b_ref, o_ref, acc_ref):
        t = pl.program_id(1)

        @pl.when(t == 0)
        def _():
            acc_ref[...] = jnp.zeros_like(acc_ref)

        av = a_ref[0] if a.ndim == 3 else a_ref[...]
        bv = b_ref[0] if b.ndim == 3 else b_ref[...]
        acc_ref[...] += _dot_tn(av.astype(bf16), bv.astype(bf16))

        @pl.when(t == nt - 1)
        def _():
            if not whole:
                o_ref[0] = acc_ref[...].astype(bf16)
            else:
                _, kk, nn = out_block
                for s in range(N_CHIPS):
                    o_ref[s] = (acc_ref[s * kk:(s + 1) * kk, :] if shard_axis == 0
                                else acc_ref[:, s * nn:(s + 1) * nn]).astype(bf16)

    (grad,), carried = _call(
        body, phase, name=name, grid=(1 if whole else N_CHIPS, nt), in_specs=[a_spec, b_spec], out_specs=[out_spec],
        out_shape=[jax.ShapeDtypeStruct(out_block, bf16)], scratch_shapes=[pltpu.VMEM((K, N), f32)], args=(a, b))
    return grad, carried


def _adamw_sum(place, w, m, v, part, from_chips, from_sibling, name):
    R, C = w.shape
    half = R // 2
    tr = _tile(half, max(16, (1 << 18) // C // 16 * 16), 16)
    nr = half // tr

    def body(p_ref, w_ref, m_ref, v_ref, part_ref, fc_ref, fs_ref, g_ref, d_ref, mo_ref, vo_ref):
        up = lambda x: x.astype(f32)
        mine = ((up(part_ref[0]) + up(fc_ref[0])) + up(fc_ref[1])) + up(fc_ref[2])
        sibs = ((up(fs_ref[0]) + up(fs_ref[1])) + up(fs_ref[2])) + up(fs_ref[3])
        g_ = jnp.where(pl.program_id(0) == p_ref[0], mine, sibs)
        m_ = ADAM_B1 * m_ref[...] + (1.0 - ADAM_B1) * g_
        v_ = ADAM_B2 * v_ref[...] + (1.0 - ADAM_B2) * (g_ * g_)
        m_hat = m_ / (1.0 - ADAM_B1 ** ADAM_STEP)
        v_hat = v_ / (1.0 - ADAM_B2 ** ADAM_STEP)
        g_ref[...] = g_
        d_ref[...] = -ADAM_LR * (m_hat / (jnp.sqrt(v_hat) + ADAM_EPS) + ADAM_WD * w_ref[...])
        mo_ref[...] = m_
        vo_ref[...] = v_

    spec = pl.BlockSpec((tr, C), lambda h, r, p: (h * nr + r, 0))
    return pl.pallas_call(
        body, name=name,
        grid_spec=pltpu.PrefetchScalarGridSpec(
            num_scalar_prefetch=1, grid=(2, nr),
            in_specs=[spec, spec, spec, pl.BlockSpec((1, tr, C), lambda h, r, p: (p[1], r, 0)),
                      pl.BlockSpec((3, tr, C), lambda h, r, p: (0, r, 0)),
                      pl.BlockSpec((4, tr, C), lambda h, r, p: (0, r, 0))],
            out_specs=[spec] * 4),
        out_shape=[jax.ShapeDtypeStruct((R, C), f32)] * 4,
        compiler_params=_params(("parallel", "parallel")),
    )(place, w, m, v, part, from_chips, from_sibling)


def _adamw(w, g, m, v, name):
    R, C = w.shape
    tr = _tile(R, max(8, (1 << 18) // C // 8 * 8), 8)

    def body(w_ref, g_ref, m_ref, v_ref, d_ref, mo_ref, vo_ref):
        g_ = g_ref[...]
        m_ = ADAM_B1 * m_ref[...] + (1.0 - ADAM_B1) * g_
        v_ = ADAM_B2 * v_ref[...] + (1.0 - ADAM_B2) * (g_ * g_)
        m_hat = m_ / (1.0 - ADAM_B1 ** ADAM_STEP)
        v_hat = v_ / (1.0 - ADAM_B2 ** ADAM_STEP)
        d_ref[...] = -ADAM_LR * (m_hat / (jnp.sqrt(v_hat) + ADAM_EPS) + ADAM_WD * w_ref[...])
        mo_ref[...] = m_
        vo_ref[...] = v_

    spec = pl.BlockSpec((tr, C), lambda i: (i, 0))
    return pl.pallas_call(
        body, name=name, grid=(R // tr,), in_specs=[spec] * 4, out_specs=[spec] * 3,
        out_shape=[jax.ShapeDtypeStruct((R, C), f32)] * 3,
        compiler_params=_params(("parallel",)),
    )(w, g, m, v)


def _place():
    return lax.axis_index("x"), lax.axis_index("y"), lax.axis_index("c")


def _other_chips(x, y):
    chips = [(1 - x, y), (x, 1 - y), (1 - x, 1 - y)]
    return chips, [2 * cx + cy for cx, cy in chips]


def _exchange_small(blk, name, reduce):
    R, C = blk.shape

    def body(x_ref, out_ref, *rest):
        if reduce:
            all_ref, send_sems, recv_sems = rest
        else:
            all_ref = out_ref
            send_sems, recv_sems = rest
        x, y, c = _place()
        me = 4 * x + 2 * y + c
        all_ref[me] = x_ref[...]
        copies = []
        for k in range(1, N_DEV):
            peer = tuple(1 - p if (k >> s) & 1 else p for p, s in ((x, 2), (y, 1), (c, 0)))
            cp = pltpu.make_async_remote_copy(src_ref=x_ref, dst_ref=all_ref.at[me], send_sem=send_sems.at[k - 1],
                                              recv_sem=recv_sems.at[k - 1], device_id=peer, device_id_type=MESH)
            cp.start()
            copies.append(cp)
        for cp in copies:
            cp.wait()
        if reduce:
            tot = all_ref[0]
            for d in range(1, N_DEV):
                tot = tot + all_ref[d]
            out_ref[...] = tot

    vm = pl.BlockSpec(memory_space=pltpu.VMEM)
    scratch = [pltpu.SemaphoreType.DMA((N_DEV - 1,)), pltpu.SemaphoreType.DMA((N_DEV - 1,))]
    if reduce:
        scratch = [pltpu.VMEM((N_DEV, R, C), f32)] + scratch
    return pl.pallas_call(
        body, name=name, in_specs=[vm], out_specs=vm,
        out_shape=jax.ShapeDtypeStruct((R, C) if reduce else (N_DEV, R, C), f32),
        scratch_shapes=scratch,
    )(blk)


def _cast_shard(place, w, name):
    R, C = w.shape
    tr = _tile(R, max(16, (1 << 19) // C // 16 * 16), 16)

    def body(p_ref, w_ref, o_ref):
        o_ref[0] = w_ref[...].astype(bf16)

    return pl.pallas_call(
        body, name=name,
        grid_spec=pltpu.PrefetchScalarGridSpec(
            num_scalar_prefetch=1, grid=(R // tr,),
            in_specs=[pl.BlockSpec((tr, C), lambda r, p: (r, 0))],
            out_specs=pl.BlockSpec((1, tr, C), lambda r, p: (p[1], r, 0))),
        out_shape=jax.ShapeDtypeStruct((N_CHIPS, R, C), bf16),
        compiler_params=_params(("parallel",)),
    )(place, w)


class _Phase:
    def __init__(self, arrays, out_shapes, aliases, n_copies, copies, arrivals, own_starts=(), own_waits=()):
        self.arrays, self.out_shapes, self.aliases = list(arrays), list(out_shapes), dict(aliases)
        self.n_copies, self.copies, self.arrivals = n_copies, copies, arrivals
        self.own_starts, self.own_waits = tuple(own_starts), tuple(own_waits)

    def sems(self):
        return [pltpu.SemaphoreType.DMA((self.n_copies,)), pltpu.SemaphoreType.DMA((self.n_copies,))]

    def _descriptors(self, pin, pout, send_sems, recv_sems):
        return [pltpu.make_async_remote_copy(src_ref=s, dst_ref=d, send_sem=send_sems.at[i], recv_sem=recv_sems.at[i],
                                             device_id=to, device_id_type=MESH)
                for i, (s, d, to) in enumerate(self.copies(pin, pout))]

    def _arrival(self, i, pin, pout, send_sems, recv_sems):
        dst = self.arrivals(pin, pout)[i]
        return pltpu.make_async_remote_copy(src_ref=dst, dst_ref=dst, send_sem=send_sems.at[i], recv_sem=recv_sems.at[i],
                                            device_id=_place(), device_id_type=MESH)

    def start(self, pin, pout, send_sems, recv_sems):
        for i, cp in enumerate(self._descriptors(pin, pout, send_sems, recv_sems)):
            if i not in self.own_starts:
                cp.start()

    def begin(self, i, pin, pout, send_sems, recv_sems):
        self._descriptors(pin, pout, send_sems, recv_sems)[i].start()

    def arrived(self, i, pin, pout, send_sems, recv_sems):
        self._arrival(i, pin, pout, send_sems, recv_sems).wait_recv()

    def finish(self, pin, pout, send_sems, recv_sems):
        for i in range(self.n_copies):
            if i not in self.own_waits:
                self._arrival(i, pin, pout, send_sems, recv_sems).wait_recv()
        for cp in self._descriptors(pin, pout, send_sems, recv_sems):
            cp.wait_send()


def _join(phases):
    if len(phases) == 1:
        return phases[0]
    ai = np.cumsum([0] + [len(p.arrays) for p in phases])
    oi = np.cumsum([0] + [len(p.out_shapes) for p in phases])

    def each(fn_name, pin, pout):
        return [item for k, p in enumerate(phases)
                for item in getattr(p, fn_name)(pin[ai[k]:ai[k + 1]], pout[oi[k]:oi[k + 1]])]

    aliases = {int(ai[k]) + i: int(oi[k]) + j for k, p in enumerate(phases) for i, j in p.aliases.items()}
    ci = np.cumsum([0] + [p.n_copies for p in phases])
    shifted = lambda attr: [int(ci[k]) + i for k, p in enumerate(phases) for i in getattr(p, attr)]
    return _Phase([a for p in phases for a in p.arrays], [s for p in phases for s in p.out_shapes], aliases,
                  int(ci[-1]), functools.partial(each, "copies"), functools.partial(each, "arrivals"),
                  shifted("own_starts"), shifted("own_waits"))


def _call(body, phase, *, name, grid, in_specs, out_specs, out_shape, scratch_shapes, args, prefetch=(), expose=False):
    seq = _params(("arbitrary",) * len(grid))
    np_ = len(prefetch)
    if phase is None:
        spec = pltpu.PrefetchScalarGridSpec(num_scalar_prefetch=np_, grid=grid, in_specs=in_specs, out_specs=out_specs,
                                            scratch_shapes=scratch_shapes)
        res = pl.pallas_call(body, name=name, grid_spec=spec, out_shape=out_shape, compiler_params=seq)(*prefetch, *args)
        return list(res), []
    ni, no, ns = len(in_specs), len(out_specs), len(scratch_shapes)
    pi, po = len(phase.arrays), len(phase.out_shapes)

    def hosted(*refs):
        cut = np.cumsum([np_, ni, pi, no, po, ns])
        pre, ins, pin, outs, pout, scr, sems = (refs[a:b] for a, b in zip([0, *cut], [*cut, len(refs)]))
        ids = [pl.program_id(d) for d in range(len(grid))]
        first = functools.reduce(lambda p, q: p & q, [i == 0 for i in ids])
        last = functools.reduce(lambda p, q: p & q, [i == g - 1 for i, g in zip(ids, grid)])
        pl.when(first)(lambda: phase.start(pin, pout, *sems))
        body(*pre, *ins, *outs, *scr, **({"carried": (pin, pout, sems)} if expose else {}))
        pl.when(last)(lambda: phase.finish(pin, pout, *sems))

    anyspace = pl.BlockSpec(memory_space=pl.ANY)
    spec = pltpu.PrefetchScalarGridSpec(
        num_scalar_prefetch=np_, grid=grid, in_specs=list(in_specs) + [anyspace] * pi,
        out_specs=list(out_specs) + [anyspace] * po, scratch_shapes=list(scratch_shapes) + phase.sems())
    res = pl.pallas_call(
        hosted, name=name, grid_spec=spec, out_shape=list(out_shape) + phase.out_shapes,
        input_output_aliases={np_ + ni + i: no + j for i, j in phase.aliases.items()}, compiler_params=seq,
    )(*prefetch, *args, *phase.arrays)
    return list(res[:no]), list(res[no:])


def _run_phases(name, phases):
    first = phases[0]
    pi, po = len(first.arrays), len(first.out_shapes)

    def body(*refs):
        pin, pout, sems = refs[:pi], refs[pi:pi + po], refs[pi + po:]
        for n, ph in enumerate(phases):
            ph.start(pin, pout, *sems[2 * n:2 * n + 2])
            ph.finish(pin, pout, *sems[2 * n:2 * n + 2])

    anyspace = pl.BlockSpec(memory_space=pl.ANY)
    return list(pl.pallas_call(
        body, name=name, in_specs=[anyspace] * pi, out_specs=[anyspace] * po, out_shape=first.out_shapes,
        input_output_aliases=first.aliases, scratch_shapes=[s for ph in phases for s in ph.sems()],
    )(*first.arrays))


def _half_rows(buf, c):
    half = buf.shape[1] // 2
    return pl.ds(c * half, half), pl.ds((1 - c) * half, half)


def _gather_phase(bufs, over_ici):
    n = len(bufs)
    shapes = [jax.ShapeDtypeStruct(b.shape, b.dtype) for b in bufs]

    def landed(out, which):
        x, y, c = _place()
        _, ks = _other_chips(x, y)
        return [out[a].at[ks[j], _half_rows(bufs[a], c)[which]] for a in range(n) for j in range(3)]

    def ici(pin, out):
        x, y, c = _place()
        chips, _ = _other_chips(x, y)
        mine = [out[a].at[2 * x + y, _half_rows(bufs[a], c)[0]] for a in range(n)]
        return [(mine[a], mine[a], (*chips[j], c)) for a in range(n) for j in range(3)]

    def d2d(pin, out):
        x, y, c = _place()
        return [(dst, dst, (x, y, 1 - c)) for dst in landed(out, 0)]

    if over_ici:
        return _Phase(bufs, shapes, {a: a for a in range(n)}, 3 * n, ici, lambda pin, out: landed(out, 0))
    return _Phase(bufs, shapes, {a: a for a in range(n)}, 3 * n, d2d, lambda pin, out: landed(out, 1))


def _feed_phase(buf):
    def chips():
        x, y, c = _place()
        return [(x if f < 2 else 1 - x, y if f % 2 == 0 else 1 - y) for f in (1, 2, 3)]

    def copies(pin, out):
        x, y, c = _place()
        mine = _half_rows(buf, c)[0]
        own = out[0].at[2 * x + y, mine]
        sent = [(own, own, (cx, cy, c)) for cx, cy in chips()]
        return sent + [(out[0].at[2 * cx + cy, mine], out[0].at[2 * cx + cy, mine], (x, y, 1 - c)) for cx, cy in chips()]

    def arrivals(pin, out):
        x, y, c = _place()
        mine, theirs = _half_rows(buf, c)
        return [out[0].at[2 * cx + cy, rows] for rows in (mine, theirs) for cx, cy in chips()]

    return _Phase([buf], [jax.ShapeDtypeStruct(buf.shape, buf.dtype)], {0: 0}, 6, copies, arrivals,
                  own_starts=(3, 4, 5), own_waits=range(6))


def _rs_sibling(grads, name):
    n = len(grads)

    def body(*refs):
        g, out, send_sems, recv_sems = refs[:n], refs[n:2 * n], refs[2 * n], refs[2 * n + 1]
        x, y, c = _place()
        copies = []
        for a in range(n):
            half = grads[a].shape[1] // 2
            cp = pltpu.make_async_remote_copy(src_ref=g[a].at[:, pl.ds((1 - c) * half, half)], dst_ref=out[a],
                                              send_sem=send_sems.at[a], recv_sem=recv_sems.at[a],
                                              device_id=(x, y, 1 - c), device_id_type=MESH)
            cp.start()
            copies.append(cp)
        for cp in copies:
            cp.wait()

    anyspace = pl.BlockSpec(memory_space=pl.ANY)
    return pl.pallas_call(
        body, name=name, in_specs=[anyspace] * n, out_specs=[anyspace] * n,
        out_shape=[jax.ShapeDtypeStruct((N_CHIPS, g.shape[1] // 2, g.shape[2]), g.dtype) for g in grads],
        scratch_shapes=[pltpu.SemaphoreType.DMA((n,)), pltpu.SemaphoreType.DMA((n,))],
    )(*grads)


def _rs_add_sibling(place, grad, got, name):
    _, R, C = grad.shape
    half = R // 2
    tr = _tile(half, max(16, (1 << 19) // C // 16 * 16), 16)
    nr = half // tr

    def body(p_ref, a_ref, b_ref, o_ref):
        o_ref[...] = (a_ref[...].astype(f32) + b_ref[...].astype(f32)).astype(o_ref.dtype)

    return pl.pallas_call(
        body, name=name,
        grid_spec=pltpu.PrefetchScalarGridSpec(
            num_scalar_prefetch=1, grid=(N_CHIPS, nr),
            in_specs=[pl.BlockSpec((1, tr, C), lambda k, r, p: (k, p[0] * nr + r, 0)),
                      pl.BlockSpec((1, tr, C), lambda k, r, p: (k, r, 0))],
            out_specs=pl.BlockSpec((1, tr, C), lambda k, r, p: (k, r, 0))),
        out_shape=jax.ShapeDtypeStruct((N_CHIPS, half, C), bf16),
        compiler_params=_params(("parallel", "parallel")),
    )(place, grad, got)


def _rs_chips_phase(parts):
    n = len(parts)

    def copies(p, fc):
        x, y, c = _place()
        chips, ks = _other_chips(x, y)
        return [(p[a].at[ks[j]], fc[a].at[j], (*chips[j], c)) for a in range(n) for j in range(3)]

    shapes = [jax.ShapeDtypeStruct((3,) + q.shape[1:], q.dtype) for q in parts]
    return _Phase(parts, shapes, {}, 3 * n, copies, lambda p, fc: [fc[a].at[j] for a in range(n) for j in range(3)])


def _rs_hand_phase(parts, from_chips):
    n = len(parts)

    def copies(pin, fs):
        x, y, c = _place()
        sib = (x, y, 1 - c)
        own = [(pin[a].at[2 * x + y], fs[a].at[0], sib) for a in range(n)]
        return own + [(pin[n + a].at[j], fs[a].at[1 + j], sib) for a in range(n) for j in range(3)]

    def arrivals(pin, fs):
        return [fs[a].at[0] for a in range(n)] + [fs[a].at[1 + j] for a in range(n) for j in range(3)]

    shapes = [jax.ShapeDtypeStruct((4,) + q.shape[1:], q.dtype) for q in parts]
    return _Phase(list(parts) + list(from_chips), shapes, {}, 4 * n, copies, arrivals)


class _Exchange:
    def __init__(self, place):
        self.place = place

    def feed(self, buf):
        return _feed_phase(buf)

    def gather(self, bufs, over_ici):
        return _gather_phase(bufs, over_ici)

    def pair_sums(self, names, grads):
        got = _rs_sibling(grads, "rs_sibling_" + names[0])
        return [_rs_add_sibling(self.place, g, r, "rs_add_" + n) for n, g, r in zip(names, grads, got)]

    def to_chips(self, parts):
        return _rs_chips_phase(parts)

    def to_sibling(self, parts, from_chips):
        return _rs_hand_phase(parts, from_chips)

    def hand_over(self, name, parts, from_chips):
        return _run_phases(name, [_rs_hand_phase(parts, from_chips)])


def _local_step(place, x, target, norm_mix, b_gate, rb_full, norm_ffn, norm_final, w_in, rest, exch):
    B, S, D = x.shape
    T = B * S
    x2 = x.reshape(T, D)
    tg2 = target.reshape(T, D)
    rope, decay = _rope_tables(S), _decay_tables()
    trows = _bias_rows(rb_full)
    g_fin = norm_final.reshape(1, D)

    mrg, ffn = ["w_ret_out", "w_att_out", "w_out"], ["w_ffn_gate", "w_ffn_up", "w_ffn_down"]
    (xn, proj), got = _in_proj(place, x2, norm_mix, _join([exch.feed(w_in), exch.gather([rest[n] for n in mrg], True)]))
    w_in, wb = got[0], {}
    (qr, kr, o, u, states), got = _ret_fwd(proj, B, S, rope, decay, _join([exch.gather([rest["w_ffn_gate"]], True),
                                                                         exch.gather(got[1:], False)]))
    wb.update(zip(mrg, got[1:]))
    (ao,), got = _att_fwd(proj, trows, B, S, _join([exch.gather([rest["w_ffn_up"], rest["w_ffn_down"]], True),
                                                    exch.gather(got[:1], False)]))
    wb["w_ffn_gate"] = got[2]
    w_ro, w_out = wb["w_ret_out"].reshape(-1, D), wb["w_out"].reshape(-1, D)
    (h1, yr, ya), got = _mix_fwd(x2, proj, u, ao, b_gate, w_ro, wb["w_att_out"], w_out, exch.gather(got[:2], False))
    wb.update(zip(ffn[1:], got))
    hn, a, b, f, dh2, part_fin = _ffn_fwd(h1, norm_ffn, wb["w_ffn_gate"], wb["w_ffn_up"], wb["w_ffn_down"], g_fin, tg2)

    da, db, dh1, part_ffn = _ffn_bwd(dh2, h1, norm_ffn, a, b, wb["w_ffn_gate"], wb["w_ffn_up"], wb["w_ffn_down"])
    ffn = ["w_ffn_down", "w_ffn_gate", "w_ffn_up"]
    p_ffn = exch.pair_sums(ffn, [_wgrad(f, dh2, 0, "wgrad_ffn_down")[0], _wgrad(hn, da, 1, "wgrad_ffn_gate")[0],
                                 _wgrad(hn, db, 1, "wgrad_ffn_up")[0]])
    (du, dao, dgl, mix, dyr, dya, part_bg), c_down = _mix_bwd(dh1, proj, yr, ya, b_gate, w_ro, wb["w_att_out"], w_out,
                                                               exch.to_chips(p_ffn[:1]))
    mrg = ["w_out", "w_ret_out", "w_att_out"]
    p_mrg = exch.pair_sums(mrg, [_wgrad(mix, dh1, 0, "wgrad_out")[0], _wgrad(u, dyr, 0, "wgrad_ret_out")[0],
                                 _wgrad(ao, dya, 1, "wgrad_att_out")[0]])
    (drq, drk, drv, drg), c_gate_up = _ret_bwd(proj, qr, kr, o, states, du, B, S, rope, decay, exch.to_chips(p_ffn[1:]))
    c_ffn = c_down + c_gate_up
    (daq, dak, dav, dvec), got = _att_bwd(proj, dao, trows, B, S, _join([exch.to_chips(p_mrg),
                                                                        exch.to_sibling(p_ffn, c_ffn)]))
    c_mrg, s_ffn = got[:len(mrg)], got[len(mrg):]
    dproj = jnp.concatenate([drq, drk, drv, drg, daq, dak, dav, dgl], axis=1)
    g_in, s_mrg = _wgrad(xn, dproj, 1, "wgrad_in", exch.to_sibling(p_mrg, c_mrg))
    p_in = exch.pair_sums(["w_in"], [g_in])
    (gx, part_mix), c_in = _in_proj_bwd(dproj, w_in, x2, norm_mix, dh1, exch.to_chips(p_in))
    s_in = exch.hand_over("rs_hand_w_in", p_in, c_in)
    gbig = dict(zip(ffn + mrg + ["w_in"], zip(p_ffn + p_mrg + p_in, c_ffn + c_mrg + c_in, s_ffn + s_mrg + s_in)))
    rows = lambda p, r: p.reshape(-1, 8, p.shape[-1])[:, r, :].sum(axis=0)
    lo = KWIN - 1 - (MAX_REL - 1)
    drb = jnp.concatenate([jnp.flip(dvec[:, lo:lo + N_REL - 1], axis=1), dvec[:, :lo].sum(axis=1, keepdims=True)], axis=1)
    gsmall = {
        "norm_mix": rows(part_mix, 0), "b_gate": rows(part_bg, 0), "rel_bias": drb, "norm_ffn": rows(part_ffn, 0),
        "norm_final": rows(part_fin, 0),
    }
    return rows(part_fin, 1), gx.reshape(B, S, D), gbig, gsmall


SMALL_ROWS = 16


def _pack_small(gs, loss_lanes):
    D = D_MODEL
    rb = jnp.pad(gs["rel_bias"].reshape(-1), (0, 3 * D - ATT_HEADS * N_REL)).reshape(3, D)
    rows = [gs["norm_mix"].reshape(1, D), gs["b_gate"].reshape(2, D), gs["norm_ffn"].reshape(1, D),
            gs["norm_final"].reshape(1, D), rb, loss_lanes.reshape(1, D)]
    used = sum(r.shape[0] for r in rows)
    return jnp.concatenate(rows + [jnp.zeros((SMALL_ROWS - used, D), f32)], axis=0)


def kernel(x, norm_mix, w_in, b_gate, rel_bias, w_ret_out, w_att_out, w_out, norm_ffn, w_ffn_gate, w_ffn_up, w_ffn_down, norm_final, loss_target, m_norm_mix, m_w_in, m_b_gate, m_rel_bias, m_w_ret_out, m_w_att_out, m_w_out, m_norm_ffn, m_w_ffn_gate, m_w_ffn_up, m_w_ffn_down, m_norm_final, v_norm_mix, v_w_in, v_b_gate, v_rel_bias, v_w_ret_out, v_w_att_out, v_w_out, v_norm_ffn, v_w_ffn_gate, v_w_ffn_up, v_w_ffn_down, v_norm_final):
    w = dict(norm_mix=norm_mix, w_in=w_in, b_gate=b_gate, rel_bias=rel_bias, w_ret_out=w_ret_out, w_att_out=w_att_out,
             w_out=w_out, norm_ffn=norm_ffn, w_ffn_gate=w_ffn_gate, w_ffn_up=w_ffn_up, w_ffn_down=w_ffn_down,
             norm_final=norm_final)
    m = dict(norm_mix=m_norm_mix, w_in=m_w_in, b_gate=m_b_gate, rel_bias=m_rel_bias, w_ret_out=m_w_ret_out,
             w_att_out=m_w_att_out, w_out=m_w_out, norm_ffn=m_norm_ffn, w_ffn_gate=m_w_ffn_gate, w_ffn_up=m_w_ffn_up,
             w_ffn_down=m_w_ffn_down, norm_final=m_norm_final)
    v = dict(norm_mix=v_norm_mix, w_in=v_w_in, b_gate=v_b_gate, rel_bias=v_rel_bias, w_ret_out=v_w_ret_out,
             w_att_out=v_w_att_out, w_out=v_w_out, norm_ffn=v_norm_ffn, w_ffn_gate=v_w_ffn_gate, w_ffn_up=v_w_ffn_up,
             w_ffn_down=v_w_ffn_down, norm_final=v_norm_final)
    xi, yi, ci = _place()
    k_me = 2 * xi + yi

    place = jnp.stack([ci, k_me]).astype(jnp.int32)
    big = [n for n, _ in BIG]

    bufs = {n: _cast_shard(place, w[n][0], "cast_" + n) for n in big}
    rest = {n: bufs[n] for n in big if n != "w_in"}
    nrel_loc = rel_bias.shape[-1]
    rb_all = _exchange_small(jnp.pad(rel_bias[0], ((0, 0), (0, 128 - nrel_loc))), "gather_rel_bias", False)
    rb_full = jnp.concatenate([rb_all[2 * k, :, :nrel_loc] for k in range(N_CHIPS)], axis=1)

    loss_lanes, grad_x, gbig, gsmall = _local_step(place, x, loss_target, norm_mix, b_gate, rb_full, norm_ffn, norm_final,
                                                   bufs["w_in"], rest, _Exchange(place))

    small = _exchange_small(_pack_small(gsmall, loss_lanes), "reduce_small", True)
    D = D_MODEL
    loss = jnp.sum(small[8])
    drb_full = small[5:8].reshape(-1)[:ATT_HEADS * N_REL].reshape(ATT_HEADS, N_REL)
    g = {
        "norm_mix": small[0:1], "b_gate": small[1:3].reshape(1, 2 * D), "norm_ffn": small[3:4], "norm_final": small[4],
        "rel_bias": lax.dynamic_slice_in_dim(drb_full, k_me * nrel_loc, nrel_loc, axis=1)[None],
    }

    delta, new_m, new_v = {}, {}, {}
    for n in big:
        g_, d_, m_, v_ = _adamw_sum(place, w[n][0], m[n][0], v[n][0], *gbig[n], "adamw_" + n)
        g[n], delta[n], new_m[n], new_v[n] = g_[None], d_[None], m_[None], v_[None]
    flat = lambda d: jnp.concatenate([d[n].reshape(-1) for n in SMALL])
    n_small = sum(int(np.prod(w[n].shape)) for n in SMALL)
    n_pad = -n_small % 1024
    packs = [jnp.pad(flat(d), (0, n_pad)).reshape(-1, 128) for d in (w, g, m, v)]
    outs = _adamw(*packs, "adamw_small")
    for res, dst in zip(outs, (delta, new_m, new_v)):
        off = 0
        fl = res.reshape(-1)
        for n in SMALL:
            sz = int(np.prod(w[n].shape))
            dst[n] = fl[off:off + sz].reshape(w[n].shape)
            off += sz

    return (loss, grad_x, *[g[n] for n in WEIGHTS], *[delta[n] for n in WEIGHTS], *[new_m[n] for n in WEIGHTS],
            *[new_v[n] for n in WEIGHTS])
```

```python
import functools

import numpy as np
import jax
import jax.numpy as jnp
from jax import lax
from jax.experimental import pallas as pl
from jax.experimental.pallas import tpu as pltpu

f32 = jnp.float32
bf16 = jnp.bfloat16

D_MODEL = 1024
CHUNK = 64
RET_HEADS = 4
RET_KEY_DIM = 128
RET_VAL_DIM = 256
ATT_HEADS = 8
ATT_HEAD_DIM = 64
ATT_W = ATT_HEADS * ATT_HEAD_DIM
BAND_CHUNKS = 8
PAD = BAND_CHUNKS * CHUNK
MAX_REL = 256
N_REL = CHUNK + MAX_REL
D_FF = 2816
N_IN = 6656
ROPE_BASE = 10000.0
EPS = 1e-6
NEG_INF = -1e30
C_RQ, C_RK, C_RV, C_RG, C_AQ, C_AK, C_AV, C_GL = 0, 512, 1024, 2048, 3072, 3584, 4096, 4608

ADAM_LR, ADAM_B1, ADAM_B2, ADAM_EPS, ADAM_WD, ADAM_STEP = 0.001, 0.9, 0.999, 1e-08, 0.01, 10

N_CHIPS = 4
N_DEV = 8
WGRAD_ACC_BYTES = 8 * 1024 * 1024
ROW_TILE = 512
BIG_ROW_TILE = 1024
QBLK = 256
KWIN = PAD + QBLK
TOEP = 1024
VMEM_LIMIT = 56 * 1024 * 1024
MESH = pl.DeviceIdType.MESH

BIG = (
    ("w_in", 1), ("w_ret_out", 0), ("w_att_out", 1), ("w_out", 0), ("w_ffn_gate", 1), ("w_ffn_up", 1), ("w_ffn_down", 0))
WEIGHTS = ("norm_mix", "w_in", "b_gate", "rel_bias", "w_ret_out", "w_att_out", "w_out", "norm_ffn", "w_ffn_gate",
           "w_ffn_up", "w_ffn_down", "norm_final")
SMALL = ("norm_mix", "b_gate", "rel_bias", "norm_ffn", "norm_final")


def _dot(a, b):
    return lax.dot_general(a, b, (((1,), (0,)), ((), ())), preferred_element_type=f32)


def _dot_nt(a, b):
    return lax.dot_general(a, b, (((1,), (1,)), ((), ())), preferred_element_type=f32)


def _dot_tn(a, b):
    return lax.dot_general(a, b, (((0,), (0,)), ((), ())), preferred_element_type=f32)


def _sig(x):
    return 1.0 / (1.0 + jnp.exp(-x))


def _tile(n, pref, mult):
    best = None
    for t in range(mult, min(n, pref) + 1, mult):
        if n % t == 0:
            best = t
    return best if best is not None else n


def _params(sem, vmem=VMEM_LIMIT):
    return pltpu.CompilerParams(dimension_semantics=sem, vmem_limit_bytes=vmem)


def _in_proj(place, x2, gamma, phase):
    T, D = x2.shape
    _, _, ns = phase.arrays[0].shape
    tm = _tile(T, BIG_ROW_TILE, 8)
    ni = T // tm

    def body(p_ref, x_ref, g_ref, xn_ref, pr_ref, xs_ref, w_ref, w_sem, carried):
        j, i = pl.program_id(0), pl.program_id(1)
        pin, pout, sems = carried
        rows = pl.ds(pl.multiple_of(i * tm, tm), tm)

        @pl.when(i == 0)
        def _():
            for f in range(1, N_CHIPS):
                @pl.when(j == f)
                def _():
                    phase.arrived(f - 1, pin, pout, *sems)
                    phase.begin(2 + f, pin, pout, *sems)
                    phase.arrived(2 + f, pin, pout, *sems)
            shard = pltpu.make_async_copy(pout[0].at[jnp.bitwise_xor(p_ref[1], j)], w_ref, w_sem)
            shard.start()
            shard.wait()

        @pl.when(j == 0)
        def _():
            x = x_ref[...]
            r = lax.rsqrt(jnp.mean(x * x, axis=-1, keepdims=True) + EPS)
            xn = (x * r * g_ref[...]).astype(bf16)
            xs_ref[rows, :] = xn
            xn_ref[...] = xn

        pr_ref[...] = _dot(xs_ref[rows, :], w_ref[...]).astype(bf16)

    first_pass = lambda j, i, p: (jnp.where(j == 0, i, ni - 1), 0)
    return _call(
        body, phase, name="in_proj", grid=(N_CHIPS, ni), prefetch=(place,), expose=True,
        in_specs=[pl.BlockSpec((tm, D), first_pass), pl.BlockSpec((1, D), lambda j, i, p: (0, 0))],
        out_specs=[pl.BlockSpec((tm, D), first_pass),
                   pl.BlockSpec((tm, ns), lambda j, i, p: (i, jnp.bitwise_xor(p[1], j)))],
        out_shape=[jax.ShapeDtypeStruct((T, D), bf16), jax.ShapeDtypeStruct((T, N_CHIPS * ns), bf16)],
        scratch_shapes=[pltpu.VMEM((T, D), bf16), pltpu.VMEM((D, ns), bf16), pltpu.SemaphoreType.DMA],
        args=(x2, gamma))


def _rope_tables(S):
    d = RET_KEY_DIM
    freqs = ROPE_BASE ** (-jnp.arange(0, d, 2, dtype=f32) / d)
    ang = jnp.arange(S, dtype=f32)[:, None] * freqs[None, :]
    cos, sin = jnp.cos(ang), jnp.sin(ang)
    return jnp.concatenate([cos, cos], axis=1), jnp.concatenate([-sin, sin], axis=1)


def _decay_tables():
    H = RET_HEADS
    log_g = jnp.log(1.0 - 2.0 ** (-5.0 - jnp.arange(H, dtype=f32)))
    p = jnp.arange(CHUNK, dtype=f32)
    intra = jnp.exp(log_g[:, None, None] * jnp.abs(p[:, None] - p[None, :]))
    q_dec = jnp.exp(log_g[:, None] * (p[None, :] + 1.0))
    k_dec = jnp.exp(log_g[:, None] * (CHUNK - 1.0 - p[None, :]))
    c_dec = jnp.exp(log_g * CHUNK)
    q_dec = jnp.broadcast_to(q_dec[:, :, None], (H, CHUNK, RET_KEY_DIM))
    k_dec = jnp.broadcast_to(k_dec[:, :, None], (H, CHUNK, RET_KEY_DIM))
    c_dec = jnp.broadcast_to(c_dec[:, None, None], (H, 1, RET_VAL_DIM))
    return intra, q_dec, k_dec, c_dec


K_SCALE = RET_KEY_DIM ** -0.5


RET_CHUNKS = 4


def _ret_tables_specs():
    whole = lambda *shape: pl.BlockSpec(shape, lambda b, i: (0,) * len(shape))
    return [whole(RET_HEADS, CHUNK, CHUNK), whole(RET_HEADS, CHUNK, RET_KEY_DIM), whole(RET_HEADS, CHUNK, RET_KEY_DIM),
            whole(RET_HEADS, 1, RET_VAL_DIM)]


def _rotate(x, cos, sn):
    return x * cos + pltpu.roll(x, RET_KEY_DIM // 2, 1) * sn


def _ret_fwd(proj, B, S, rope, decay, phase=None):
    T = B * S
    nc = S // CHUNK
    H, dk, dv = RET_HEADS, RET_KEY_DIM, RET_VAL_DIM
    sb = RET_CHUNKS * CHUNK
    ns = S // sb

    def body(q_ref, k_ref, v_ref, g_ref, cos_ref, sin_ref, intra_ref, qd_ref, kd_ref, cd_ref,
             qr_ref, kr_ref, o_ref, u_ref, st_ref, state_ref):
        @pl.when(pl.program_id(1) == 0)
        def _():
            state_ref[...] = jnp.zeros_like(state_ref)

        cos, sn = cos_ref[...], sin_ref[...]
        for h in range(H):
            hs = slice(h * dk, (h + 1) * dk)
            qr_ref[:, hs] = _rotate(q_ref[:, hs].astype(f32), cos, sn).astype(bf16)
            kr_ref[:, hs] = (_rotate(k_ref[:, hs].astype(f32), cos, sn) * K_SCALE).astype(bf16)
        states = [state_ref[h] for h in range(H)]
        for ci in range(RET_CHUNKS):
            r = slice(ci * CHUNK, (ci + 1) * CHUNK)
            for h in range(H):
                hk, hv = slice(h * dk, (h + 1) * dk), slice(h * dv, (h + 1) * dv)
                qi, ki, vi = qr_ref[r, hk], kr_ref[r, hk], v_ref[r, hv]
                stb = states[h].astype(bf16)
                st_ref[0, h, ci] = stb
                s = (_dot_nt(qi, ki) * intra_ref[h]).astype(bf16)
                o = _dot(s, vi) + _dot((qi.astype(f32) * qd_ref[h]).astype(bf16), stb)
                states[h] = states[h] * cd_ref[h] + _dot_tn((ki.astype(f32) * kd_ref[h]).astype(bf16), vi)
                mu = jnp.mean(o, axis=-1, keepdims=True)
                xc = o - mu
                var = jnp.mean(xc * xc, axis=-1, keepdims=True)
                oh = xc * lax.rsqrt(var + EPS)
                g = g_ref[r, hv].astype(f32)
                o_ref[r, hv] = o.astype(bf16)
                u_ref[r, hv] = (g * _sig(g) * oh).astype(bf16)
        for h in range(H):
            state_ref[h] = states[h]

    blk = lambda w, c: pl.BlockSpec((sb, w), lambda b, i: (b * ns + i, c))
    return _call(
        body, phase, name="ret_fwd", grid=(B, ns), scratch_shapes=[pltpu.VMEM((H, dk, dv), f32)],
        in_specs=[blk(H * dk, C_RQ // (H * dk)), blk(H * dk, C_RK // (H * dk)), blk(H * dv, C_RV // (H * dv)),
                  blk(H * dv, C_RG // (H * dv)),
                  pl.BlockSpec((sb, dk), lambda b, i: (i, 0)), pl.BlockSpec((sb, dk), lambda b, i: (i, 0)),
                  *_ret_tables_specs()],
        out_specs=[blk(H * dk, 0), blk(H * dk, 0), blk(H * dv, 0), blk(H * dv, 0),
                   pl.BlockSpec((1, H, RET_CHUNKS, dk, dv), lambda b, i: (b, 0, i, 0, 0))],
        out_shape=[jax.ShapeDtypeStruct((T, H * dk), bf16), jax.ShapeDtypeStruct((T, H * dk), bf16),
                   jax.ShapeDtypeStruct((T, H * dv), bf16), jax.ShapeDtypeStruct((T, H * dv), bf16),
                   jax.ShapeDtypeStruct((B, H, nc, dk, dv), bf16)],
        args=(proj, proj, proj, proj, *rope, *decay))


def _bias_rows(rb):
    last = rb[:, N_REL - 1:]
    return jnp.concatenate([
        jnp.broadcast_to(last, (ATT_HEADS, PAD - MAX_REL + 1)),
        jnp.flip(rb[:, :N_REL - 1], axis=1),
        jnp.broadcast_to(rb[:, :1], (ATT_HEADS, KWIN - PAD - CHUNK)),
        jnp.broadcast_to(last, (ATT_HEADS, TOEP - KWIN)),
    ], axis=1)


def _build_bias(t_ref, bias_ref):
    row = lax.broadcasted_iota(jnp.int32, (QBLK, KWIN), 0) // CHUNK
    col = lax.broadcasted_iota(jnp.int32, (QBLK, KWIN), 1) // CHUNK
    delta = BAND_CHUNKS + row - col
    vis = (delta >= 0) & (delta <= BAND_CHUNKS)
    for h in range(ATT_HEADS):
        t = jnp.broadcast_to(t_ref[h:h + 1, :], (QBLK, TOEP))
        rolled = pltpu.roll(t, 0, 1, stride=1, stride_axis=0)
        bias_ref[h] = jnp.where(vis, rolled[:, :KWIN], NEG_INF)


def _att_probs(qh, kh, bias):
    s = _dot_nt(qh, kh) * (ATT_HEAD_DIM ** -0.5) + bias
    m = jnp.max(s, axis=-1, keepdims=True)
    p = jnp.exp(s - m)
    return p * (1.0 / jnp.sum(p, axis=-1, keepdims=True))


def _first_of_pair():
    return lax.broadcasted_iota(jnp.int32, (1, 2 * ATT_HEAD_DIM), 1) < ATT_HEAD_DIM


def _by_window(i, step):
    sizes = list(range(QBLK, KWIN, QBLK))
    for n, nk in enumerate(sizes):
        pl.when(i == n)(functools.partial(step, nk))
    pl.when(i >= len(sizes))(functools.partial(step, KWIN))


def _att_fwd(proj, trows, B, S, phase=None):
    T = B * S
    nq = S // QBLK
    dh = ATT_HEAD_DIM

    def body(q_ref, k_ref, v_ref, t_ref, o_ref, bias_ref):
        i = pl.program_id(1)

        @pl.when((pl.program_id(0) == 0) & (i == 0))
        def _():
            _build_bias(t_ref, bias_ref)

        def step(nk):
            win = pl.ds(pl.multiple_of((i + 1) * QBLK - nk, QBLK), nk)
            kw, vw = k_ref[win, :], v_ref[win, :]
            first = _first_of_pair()
            outs = []
            for p in range(ATT_HEADS // 2):
                ps = slice(2 * p * dh, 2 * (p + 1) * dh)
                q2, k2, v2 = q_ref[:, ps], kw[:, ps], vw[:, ps]
                both = []
                for e in range(2):
                    qm = jnp.where(first == (e == 0), q2, jnp.zeros_like(q2))
                    pr = _att_probs(qm, k2, bias_ref[2 * p + e, :, KWIN - nk:])
                    both.append(_dot(pr.astype(bf16), v2))
                outs.append(jnp.where(first, both[0], both[1]))
            o_ref[...] = jnp.concatenate(outs, axis=1).astype(bf16)

        _by_window(i, step)

    return _call(
        body, phase, name="att_fwd", grid=(B, nq),
        in_specs=[pl.BlockSpec((QBLK, ATT_W), lambda b, i: (b * nq + i, C_AQ // ATT_W)),
                  pl.BlockSpec((S, ATT_W), lambda b, i: (b, C_AK // ATT_W)),
                  pl.BlockSpec((S, ATT_W), lambda b, i: (b, C_AV // ATT_W)),
                  pl.BlockSpec((ATT_HEADS, TOEP), lambda b, i: (0, 0))],
        out_specs=[pl.BlockSpec((QBLK, ATT_W), lambda b, i: (b * nq + i, 0))],
        out_shape=[jax.ShapeDtypeStruct((T, ATT_W), bf16)],
        scratch_shapes=[pltpu.VMEM((ATT_HEADS, QBLK, KWIN), f32)],
        args=(proj, proj, proj, trows))


def _gl_specs(tm):
    w = 512
    return [pl.BlockSpec((tm, w), functools.partial(lambda i, j: (i, C_GL // 512 + j), j=j)) for j in range(4)]


def _gates(gl_refs, bg_ref):
    gl = jnp.concatenate([r[...] for r in gl_refs], axis=1).astype(f32) + bg_ref[...]
    g = _sig(gl)
    return g[:, :D_MODEL], g[:, D_MODEL:]


def _mix_fwd(x2, proj, u, ao, b_gate, w_ro, w_ao, w_out, phase=None):
    T, D = x2.shape
    tm = _tile(T, ROW_TILE, 8)

    def body(x_ref, u_ref, ao_ref, g0, g1, g2, g3, bg_ref, wro_ref, wao_ref, wo_ref, h1_ref, yr_ref, ya_ref):
        yr = _dot(u_ref[...], wro_ref[...])
        ao = ao_ref[...]
        ya = jnp.concatenate([_dot(ao, wao_ref[k]) for k in range(N_CHIPS)], axis=1)
        gr, ga = _gates((g0, g1, g2, g3), bg_ref)
        mix = gr * yr + ga * ya
        h1_ref[...] = x_ref[...] + _dot(mix.astype(bf16), wo_ref[...])
        yr_ref[...] = yr.astype(bf16)
        ya_ref[...] = ya.astype(bf16)

    full = lambda a: pl.BlockSpec(a.shape, lambda i: (0,) * a.ndim)
    row = lambda n: pl.BlockSpec((tm, n), lambda i: (i, 0))
    return _call(
        body, phase, name="mix_fwd", grid=(T // tm,), scratch_shapes=[],
        in_specs=[row(D), row(D), row(ATT_W), *_gl_specs(tm), full(b_gate), full(w_ro), full(w_ao), full(w_out)],
        out_specs=[row(D), row(D), row(D)],
        out_shape=[jax.ShapeDtypeStruct((T, D), f32), jax.ShapeDtypeStruct((T, D), bf16),
                   jax.ShapeDtypeStruct((T, D), bf16)],
        args=(x2, u, ao, proj, proj, proj, proj, b_gate, w_ro, w_ao, w_out))


def _ffn_fwd(h1, g_ffn, wg, wu, wd, g_fin, target):
    T, D = h1.shape
    nf, _, tf = wg.shape
    tm = _tile(T, ROW_TILE, 8)

    def body(h1_ref, g_ref, wg_ref, wu_ref, wd_ref, gf_ref, tg_ref, hn_ref, a_ref, b_ref, f_ref, dh2_ref, part_ref):
        h1v = h1_ref[...]
        r = lax.rsqrt(jnp.mean(h1v * h1v, axis=-1, keepdims=True) + EPS)
        hn = (h1v * r * g_ref[...]).astype(bf16)
        hn_ref[...] = hn
        h2 = h1v
        for k in range(nf):
            a = _dot(hn, wg_ref[k])
            b = _dot(hn, wu_ref[k])
            f = ((a * _sig(a)) * b).astype(bf16)
            a_ref[k] = a.astype(bf16)
            b_ref[k] = b.astype(bf16)
            f_ref[k] = f
            h2 = h2 + _dot(f, wd_ref[k])
        r = lax.rsqrt(jnp.mean(h2 * h2, axis=-1, keepdims=True) + EPS)
        n = h2 * r
        gf = gf_ref[...]
        e = n * gf - tg_ref[...]
        dy = e * (1.0 / D)
        dn = dy * gf
        dh2_ref[...] = r * (dn - n * jnp.mean(dn * n, axis=-1, keepdims=True))
        part_ref[...] = jnp.zeros_like(part_ref)
        part_ref[0:1, :] = jnp.sum(dy * n, axis=0, keepdims=True)
        part_ref[1:2, :] = (0.5 / D) * jnp.sum(e * e, axis=0, keepdims=True)

    row = lambda n: pl.BlockSpec((tm, n), lambda i: (i, 0))
    vec = pl.BlockSpec((1, D), lambda i: (0, 0))
    col = pl.BlockSpec((nf, tm, tf), lambda i: (0, i, 0))
    held = lambda w: pl.BlockSpec(w.shape, lambda i: (0, 0, 0), pipeline_mode=pl.Buffered(1))
    act = jax.ShapeDtypeStruct((nf, T, tf), bf16)
    return pl.pallas_call(
        body, name="ffn_fwd", grid=(T // tm,),
        in_specs=[row(D), vec, held(wg), held(wu), held(wd), vec, row(D)],
        out_specs=[row(D), col, col, col, row(D), pl.BlockSpec((8, D), lambda i: (i, 0))],
        out_shape=[jax.ShapeDtypeStruct((T, D), bf16), act, act, act,
                   jax.ShapeDtypeStruct((T, D), f32), jax.ShapeDtypeStruct((T // tm * 8, D), f32)],
        compiler_params=_params(("parallel",)),
    )(h1, g_ffn, wg, wu, wd, g_fin, target)


def _ffn_bwd(dh2, h1, g_ffn, a, b, wg, wu, wd):
    T, D = h1.shape
    nf, _, tf = wg.shape
    tm = _tile(T, ROW_TILE // 2, 8)

    def body(dh2_ref, h1_ref, g_ref, a_ref, b_ref, wg_ref, wu_ref, wd_ref, da_ref, db_ref, dh1_ref, part_ref):
        dh2v = dh2_ref[...]
        dh2b = dh2v.astype(bf16)
        dhn = jnp.zeros((tm, D), f32)
        for k in range(nf):
            df = _dot_nt(dh2b, wd_ref[k])
            av = a_ref[k].astype(f32)
            sg = _sig(av)
            db = (df * (av * sg)).astype(bf16)
            da = (df * b_ref[k].astype(f32) * (sg * (1.0 + av * (1.0 - sg)))).astype(bf16)
            da_ref[k] = da
            db_ref[k] = db
            dhn = dhn + _dot_nt(da, wg_ref[k]) + _dot_nt(db, wu_ref[k])
        h = h1_ref[...]
        r = lax.rsqrt(jnp.mean(h * h, axis=-1, keepdims=True) + EPS)
        n = h * r
        dn = dhn * g_ref[...]
        dh1_ref[...] = dh2v + r * (dn - n * jnp.mean(dn * n, axis=-1, keepdims=True))
        part_ref[...] = jnp.zeros_like(part_ref)
        part_ref[0:1, :] = jnp.sum(dhn * n, axis=0, keepdims=True)

    row = lambda n: pl.BlockSpec((tm, n), lambda i: (i, 0))
    col = pl.BlockSpec((nf, tm, tf), lambda i: (0, i, 0))
    held = lambda w: pl.BlockSpec(w.shape, lambda i: (0, 0, 0), pipeline_mode=pl.Buffered(1))
    act = jax.ShapeDtypeStruct((nf, T, tf), bf16)
    return pl.pallas_call(
        body, name="ffn_bwd", grid=(T // tm,),
        in_specs=[row(D), row(D), pl.BlockSpec((1, D), lambda i: (0, 0)), col, col, held(wg), held(wu), held(wd)],
        out_specs=[col, col, row(D), pl.BlockSpec((8, D), lambda i: (i, 0))],
        out_shape=[act, act, jax.ShapeDtypeStruct((T, D), f32), jax.ShapeDtypeStruct((T // tm * 8, D), f32)],
        compiler_params=_params(("parallel",)),
    )(dh2, h1, g_ffn, a, b, wg, wu, wd)


def _mix_bwd(dh1, proj, yr, ya, b_gate, w_ro, w_ao, w_out, phase=None):
    T, D = dh1.shape
    tm = _tile(T, ROW_TILE, 8)

    def body(dh1_ref, g0, g1, g2, g3, bg_ref, yr_ref, ya_ref, wro_ref, wao_ref, wo_ref,
             du_ref, dao_ref, dgl_ref, mix_ref, dyr_ref, dya_ref, part_ref):
        dmix = _dot_nt(dh1_ref[...].astype(bf16), wo_ref[...])
        gr, ga = _gates((g0, g1, g2, g3), bg_ref)
        yr = yr_ref[...].astype(f32)
        ya = ya_ref[...].astype(f32)
        dyr = (dmix * gr).astype(bf16)
        dya = (dmix * ga).astype(bf16)
        dgl = jnp.concatenate([dmix * yr * gr * (1.0 - gr), dmix * ya * ga * (1.0 - ga)], axis=1)
        du_ref[...] = _dot_nt(dyr, wro_ref[...]).astype(bf16)
        ns = wao_ref.shape[2]
        dao = _dot_nt(dya[:, :ns], wao_ref[0])
        for k in range(1, N_CHIPS):
            dao = dao + _dot_nt(dya[:, k * ns:(k + 1) * ns], wao_ref[k])
        dao_ref[...] = dao.astype(bf16)
        dgl_ref[...] = dgl.astype(bf16)
        mix_ref[...] = (gr * yr + ga * ya).astype(bf16)
        dyr_ref[...] = dyr
        dya_ref[...] = dya
        part_ref[...] = jnp.zeros_like(part_ref)
        part_ref[0:1, :] = jnp.sum(dgl, axis=0, keepdims=True)

    full = lambda a: pl.BlockSpec(a.shape, lambda i: (0,) * a.ndim)
    row = lambda n: pl.BlockSpec((tm, n), lambda i: (i, 0))
    return _call(
        body, phase, name="mix_bwd", grid=(T // tm,), scratch_shapes=[],
        in_specs=[row(D), *_gl_specs(tm), full(b_gate), row(D), row(D), full(w_ro), full(w_ao), full(w_out)],
        out_specs=[row(D), row(ATT_W), row(2 * D), row(D), row(D), row(D), pl.BlockSpec((8, 2 * D), lambda i: (i, 0))],
        out_shape=[jax.ShapeDtypeStruct((T, D), bf16), jax.ShapeDtypeStruct((T, ATT_W), bf16),
                   jax.ShapeDtypeStruct((T, 2 * D), bf16), jax.ShapeDtypeStruct((T, D), bf16),
                   jax.ShapeDtypeStruct((T, D), bf16), jax.ShapeDtypeStruct((T, D), bf16),
                   jax.ShapeDtypeStruct((T // tm * 8, 2 * D), f32)],
        args=(dh1, proj, proj, proj, proj, b_gate, yr, ya, w_ro, w_ao, w_out))


def _ret_bwd(proj, qr, kr, o, states, du, B, S, rope, decay, phase=None):
    T = B * S
    nc = S // CHUNK
    H, dk, dv = RET_HEADS, RET_KEY_DIM, RET_VAL_DIM

    sb = RET_CHUNKS * CHUNK
    ns = S // sb

    def body(qr_ref, kr_ref, v_ref, g_ref, o_ref, st_ref, du_ref, cos_ref, sin_ref, intra_ref, qd_ref, kd_ref, cd_ref,
             dq_ref, dk_ref, dv_ref, dg_ref, dstate_ref):
        @pl.when(pl.program_id(1) == 0)
        def _():
            dstate_ref[...] = jnp.zeros_like(dstate_ref)

        cos, snb = cos_ref[...], -sin_ref[...]
        dstates = [dstate_ref[h] for h in range(H)]
        for ci in reversed(range(RET_CHUNKS)):
            r = slice(ci * CHUNK, (ci + 1) * CHUNK)
            for h in range(H):
                hk, hv = slice(h * dk, (h + 1) * dk), slice(h * dv, (h + 1) * dv)
                intra, qd, kd = intra_ref[h], qd_ref[h], kd_ref[h]
                qi, ki, vi = qr_ref[r, hk], kr_ref[r, hk], v_ref[r, hv]
                si = st_ref[0, h, ci]
                o = o_ref[r, hv].astype(f32)
                mu = jnp.mean(o, axis=-1, keepdims=True)
                xc = o - mu
                rstd = lax.rsqrt(jnp.mean(xc * xc, axis=-1, keepdims=True) + EPS)
                oh = xc * rstd
                g = g_ref[r, hv].astype(f32)
                sg = _sig(g)
                dui = du_ref[r, hv].astype(f32)
                dg_ref[r, hv] = (dui * oh * (sg * (1.0 + g * (1.0 - sg)))).astype(bf16)
                doh = dui * (g * sg)
                do = rstd * (doh - jnp.mean(doh, axis=-1, keepdims=True)
                             - oh * jnp.mean(doh * oh, axis=-1, keepdims=True))
                dob = do.astype(bf16)
                p = (_dot_nt(qi, ki) * intra).astype(bf16)
                dsb = dstates[h].astype(bf16)
                kt = (ki.astype(f32) * kd).astype(bf16)
                qt = (qi.astype(f32) * qd).astype(bf16)
                dv_ref[r, hv] = (_dot_tn(p, dob) + _dot(kt, dsb)).astype(bf16)
                da = (_dot_nt(dob, vi) * intra).astype(bf16)
                dq = _dot(da, ki) + _dot_nt(dob, si) * qd
                dkk = (_dot_tn(da, qi) + _dot_nt(vi, dsb) * kd) * K_SCALE
                dq_ref[r, hk] = _rotate(dq, cos[r], snb[r]).astype(bf16)
                dk_ref[r, hk] = _rotate(dkk, cos[r], snb[r]).astype(bf16)
                dstates[h] = dstates[h] * cd_ref[h] + _dot_tn(qt, dob)
        for h in range(H):
            dstate_ref[h] = dstates[h]

    blk = lambda w, c: pl.BlockSpec((sb, w), lambda b, i: (b * ns + ns - 1 - i, c))
    return _call(
        body, phase, name="ret_bwd", grid=(B, ns),
        in_specs=[blk(H * dk, 0), blk(H * dk, 0), blk(H * dv, C_RV // (H * dv)), blk(H * dv, C_RG // (H * dv)),
                  blk(H * dv, 0),
                  pl.BlockSpec((1, H, RET_CHUNKS, dk, dv), lambda b, i: (b, 0, ns - 1 - i, 0, 0)),
                  blk(H * dv, 0),
                  pl.BlockSpec((sb, dk), lambda b, i: (ns - 1 - i, 0)), pl.BlockSpec((sb, dk), lambda b, i: (ns - 1 - i, 0)),
                  *_ret_tables_specs()],
        out_specs=[blk(H * dk, 0), blk(H * dk, 0), blk(H * dv, 0), blk(H * dv, 0)],
        out_shape=[jax.ShapeDtypeStruct((T, H * dk), bf16), jax.ShapeDtypeStruct((T, H * dk), bf16),
                   jax.ShapeDtypeStruct((T, H * dv), bf16), jax.ShapeDtypeStruct((T, H * dv), bf16)],
        scratch_shapes=[pltpu.VMEM((H, dk, dv), f32)],
        args=(qr, kr, proj, proj, o, states, du, *rope, *decay))


def _att_bwd(proj, dao, trows, B, S, phase=None):
    T = B * S
    nq = S // QBLK
    dh = ATT_HEAD_DIM
    scale = ATT_HEAD_DIM ** -0.5

    def body(q_ref, k_ref, v_ref, do_ref, t_ref, dq_ref, dk_ref, dv_ref, vec_ref, bias_ref, dbias_ref, dka_ref, dva_ref):
        b, i = pl.program_id(0), pl.program_id(1)

        @pl.when((b == 0) & (i == 0))
        def _():
            _build_bias(t_ref, bias_ref)
            dbias_ref[...] = jnp.zeros_like(dbias_ref)

        @pl.when(i == 0)
        def _():
            dka_ref[...] = jnp.zeros_like(dka_ref)
            dva_ref[...] = jnp.zeros_like(dva_ref)

        def step(nk):
            win = pl.ds(pl.multiple_of((i + 1) * QBLK - nk, QBLK), nk)
            kw, vw = k_ref[win, :], v_ref[win, :]
            first = _first_of_pair()
            first_rows = lax.broadcasted_iota(jnp.int32, (2 * dh, 1), 0) < dh
            dqs, dks, dvs = [], [], []
            for p in range(ATT_HEADS // 2):
                ps = slice(2 * p * dh, 2 * (p + 1) * dh)
                q2, k2, v2, do2 = q_ref[:, ps], kw[:, ps], vw[:, ps], do_ref[:, ps]
                dq2, dk2, dv2 = [], [], []
                for e in range(2):
                    h = 2 * p + e
                    mine = first == (e == 0)
                    pr = _att_probs(jnp.where(mine, q2, jnp.zeros_like(q2)), k2, bias_ref[h, :, KWIN - nk:])
                    dp = _dot_nt(jnp.where(mine, do2, jnp.zeros_like(do2)), v2)
                    ds = pr * (dp - jnp.sum(pr * dp, axis=-1, keepdims=True))
                    dbias_ref[h, :, KWIN - nk:] += ds
                    dsb = (ds * scale).astype(bf16)
                    dq2.append(_dot(dsb, k2))
                    dk2.append(_dot_tn(q2, dsb))
                    dv2.append(_dot_tn(do2, pr.astype(bf16)))
                dqs.append(jnp.where(first, dq2[0], dq2[1]))
                dks.append(jnp.where(first_rows, dk2[0], dk2[1]))
                dvs.append(jnp.where(first_rows, dv2[0], dv2[1]))
            dq_ref[...] = jnp.concatenate(dqs, axis=1).astype(bf16)
            dka_ref[:, win] += jnp.concatenate(dks, axis=0)
            dva_ref[:, win] += jnp.concatenate(dvs, axis=0)

        _by_window(i, step)

        @pl.when(i == nq - 1)
        def _():
            dk_ref[...] = dka_ref[...].T.astype(bf16)
            dv_ref[...] = dva_ref[...].T.astype(bf16)

        @pl.when((b == B - 1) & (i == nq - 1))
        def _():
            rr = lax.broadcasted_iota(jnp.int32, (QBLK, QBLK), 0)
            cc = lax.broadcasted_iota(jnp.int32, (QBLK, QBLK), 1)
            flip = jnp.where(rr + cc == QBLK - 1, 1.0, 0.0).astype(bf16)
            for h in range(ATT_HEADS):
                d = dbias_ref[h]
                hi = d.astype(bf16)
                lo = (d - hi.astype(f32)).astype(bf16)
                rev = _dot(flip, hi) + _dot(flip, lo)
                wide = jnp.concatenate([rev, jnp.zeros((QBLK, TOEP - KWIN), f32)], axis=1)
                rolled = pltpu.roll(wide, 0, 1, stride=1, stride_axis=0)
                vec_ref[h:h + 1, :] = jnp.sum(rolled, axis=0, keepdims=True)

    qspec = lambda c: pl.BlockSpec((QBLK, ATT_W), lambda b, i: (b * nq + i, c))
    kspec = lambda c: pl.BlockSpec((S, ATT_W), lambda b, i: (b, c))
    seq = jax.ShapeDtypeStruct((T, ATT_W), bf16)
    return _call(
        body, phase, name="att_bwd", grid=(B, nq),
        in_specs=[qspec(C_AQ // ATT_W), kspec(C_AK // ATT_W), kspec(C_AV // ATT_W), qspec(0),
                  pl.BlockSpec((ATT_HEADS, TOEP), lambda b, i: (0, 0))],
        out_specs=[qspec(0), kspec(0), kspec(0), pl.BlockSpec((ATT_HEADS, TOEP), lambda b, i: (0, 0))],
        out_shape=[seq, seq, seq, jax.ShapeDtypeStruct((ATT_HEADS, TOEP), f32)],
        scratch_shapes=[pltpu.VMEM((ATT_HEADS, QBLK, KWIN), f32), pltpu.VMEM((ATT_HEADS, QBLK, KWIN), f32),
                        pltpu.VMEM((ATT_W, S), f32), pltpu.VMEM((ATT_W, S), f32)],
        args=(proj, proj, proj, dao, trows))


def _in_proj_bwd(dproj, w_in, x2, gamma, dh1, phase=None):
    T, D = x2.shape
    nk, _, tk = w_in.shape
    tm = _tile(T, BIG_ROW_TILE, 8)

    def body(dp_ref, w_ref, x_ref, g_ref, dh1_ref, dx_ref, part_ref, acc_ref):
        j = pl.program_id(1)

        @pl.when(j == 0)
        def _():
            acc_ref[...] = jnp.zeros_like(acc_ref)

        acc_ref[...] += _dot_nt(dp_ref[...], w_ref[0])

        @pl.when(j == nk - 1)
        def _():
            x = x_ref[...]
            r = lax.rsqrt(jnp.mean(x * x, axis=-1, keepdims=True) + EPS)
            n = x * r
            dxn = acc_ref[...]
            dn = dxn * g_ref[...]
            dx_ref[...] = dh1_ref[...] + r * (dn - n * jnp.mean(dn * n, axis=-1, keepdims=True))
            part_ref[...] = jnp.zeros_like(part_ref)
            part_ref[0:1, :] = jnp.sum(dxn * n, axis=0, keepdims=True)

    row = lambda n: pl.BlockSpec((tm, n), lambda i, j: (i, 0))
    return _call(
        body, phase, name="in_proj_bwd", grid=(T // tm, nk),
        in_specs=[pl.BlockSpec((tm, tk), lambda i, j: (i, j)), pl.BlockSpec((1, D, tk), lambda i, j: (j, 0, 0)), row(D),
                  pl.BlockSpec((1, D), lambda i, j: (0, 0)), row(D)],
        out_specs=[row(D), pl.BlockSpec((8, D), lambda i, j: (i, 0))],
        out_shape=[jax.ShapeDtypeStruct((T, D), f32), jax.ShapeDtypeStruct((T // tm * 8, D), f32)],
        scratch_shapes=[pltpu.VMEM((tm, D), f32)],
        args=(dproj, w_in, x2, gamma, dh1))


def _wgrad(a, b, shard_axis, name, phase=None):
    def spec(arr, sharded, tt):
        if arr.ndim == 3:
            return arr.shape[2], pl.BlockSpec((1, tt, arr.shape[2]), lambda s, t: (s, t, 0))
        if sharded:
            w = arr.shape[1] // N_CHIPS
            return w, pl.BlockSpec((tt, w), lambda s, t: (t, s))
        return arr.shape[1], pl.BlockSpec((tt, arr.shape[1]), lambda s, t: (t, 0))

    T = a.shape[-2]
    tt = _tile(T, BIG_ROW_TILE, 16)
    nt = T // tt
    whole = a.ndim == 2 and b.ndim == 2 and a.shape[1] * b.shape[1] * 4 <= WGRAD_ACC_BYTES
    if whole:
        K, N = a.shape[1], b.shape[1]
        a_spec, b_spec = pl.BlockSpec((tt, K), lambda s, t: (t, 0)), pl.BlockSpec((tt, N), lambda s, t: (t, 0))
        out_block = (N_CHIPS, K // N_CHIPS, N) if shard_axis == 0 else (N_CHIPS, K, N // N_CHIPS)
        out_spec = pl.BlockSpec(out_block, lambda s, t: (0, 0, 0))
    else:
        K, a_spec = spec(a, shard_axis == 0, tt)
        N, b_spec = spec(b, shard_axis == 1, tt)
        out_block = (N_CHIPS, K, N)
        out_spec = pl.BlockSpec((1, K, N), lambda s, t: (s, 0, 0))

    def body(a_ref, b_ref, o_ref, acc_ref):
        t = pl.program_id(1)

        @pl.when(t == 0)
        def _():
            acc_ref[...] = jnp.zeros_like(acc_ref)

        av = a_ref[0] if a.ndim == 3 else a_ref[...]
        bv = b_ref[0] if b.ndim == 3 else b_ref[...]
        acc_ref[...] += _dot_tn(av.astype(bf16), bv.astype(bf16))

        @pl.when(t == nt - 1)
        def _():
            if not whole:
                o_ref[0] = acc_ref[...].astype(bf16)
            else:
                _, kk, nn = out_block
                for s in range(N_CHIPS):
                    o_ref[s] = (acc_ref[s * kk:(s + 1) * kk, :] if shard_axis == 0
                                else acc_ref[:, s * nn:(s + 1) * nn]).astype(bf16)

    (grad,), carried = _call(
        body, phase, name=name, grid=(1 if whole else N_CHIPS, nt), in_specs=[a_spec, b_spec], out_specs=[out_spec],
        out_shape=[jax.ShapeDtypeStruct(out_block, bf16)], scratch_shapes=[pltpu.VMEM((K, N), f32)], args=(a, b))
    return grad, carried


def _adamw_sum(place, w, m, v, part, from_chips, from_sibling, name):
    R, C = w.shape
    half = R // 2
    tr = _tile(half, max(16, (1 << 18) // C // 16 * 16), 16)
    nr = half // tr

    def body(p_ref, w_ref, m_ref, v_ref, part_ref, fc_ref, fs_ref, g_ref, d_ref, mo_ref, vo_ref):
        up = lambda x: x.astype(f32)
        mine = ((up(part_ref[0]) + up(fc_ref[0])) + up(fc_ref[1])) + up(fc_ref[2])
        sibs = ((up(fs_ref[0]) + up(fs_ref[1])) + up(fs_ref[2])) + up(fs_ref[3])
        g_ = jnp.where(pl.program_id(0) == p_ref[0], mine, sibs)
        m_ = ADAM_B1 * m_ref[...] + (1.0 - ADAM_B1) * g_
        v_ = ADAM_B2 * v_ref[...] + (1.0 - ADAM_B2) * (g_ * g_)
        m_hat = m_ / (1.0 - ADAM_B1 ** ADAM_STEP)
        v_hat = v_ / (1.0 - ADAM_B2 ** ADAM_STEP)
        g_ref[...] = g_
        d_ref[...] = -ADAM_LR * (m_hat / (jnp.sqrt(v_hat) + ADAM_EPS) + ADAM_WD * w_ref[...])
        mo_ref[...] = m_
        vo_ref[...] = v_

    spec = pl.BlockSpec((tr, C), lambda h, r, p: (h * nr + r, 0))
    return pl.pallas_call(
        body, name=name,
        grid_spec=pltpu.PrefetchScalarGridSpec(
            num_scalar_prefetch=1, grid=(2, nr),
            in_specs=[spec, spec, spec, pl.BlockSpec((1, tr, C), lambda h, r, p: (p[1], r, 0)),
                      pl.BlockSpec((3, tr, C), lambda h, r, p: (0, r, 0)),
                      pl.BlockSpec((4, tr, C), lambda h, r, p: (0, r, 0))],
            out_specs=[spec] * 4),
        out_shape=[jax.ShapeDtypeStruct((R, C), f32)] * 4,
        compiler_params=_params(("parallel", "parallel")),
    )(place, w, m, v, part, from_chips, from_sibling)


def _adamw(w, g, m, v, name):
    R, C = w.shape
    tr = _tile(R, max(8, (1 << 18) // C // 8 * 8), 8)

    def body(w_ref, g_ref, m_ref, v_ref, d_ref, mo_ref, vo_ref):
        g_ = g_ref[...]
        m_ = ADAM_B1 * m_ref[...] + (1.0 - ADAM_B1) * g_
        v_ = ADAM_B2 * v_ref[...] + (1.0 - ADAM_B2) * (g_ * g_)
        m_hat = m_ / (1.0 - ADAM_B1 ** ADAM_STEP)
        v_hat = v_ / (1.0 - ADAM_B2 ** ADAM_STEP)
        d_ref[...] = -ADAM_LR * (m_hat / (jnp.sqrt(v_hat) + ADAM_EPS) + ADAM_WD * w_ref[...])
        mo_ref[...] = m_
        vo_ref[...] = v_

    spec = pl.BlockSpec((tr, C), lambda i: (i, 0))
    return pl.pallas_call(
        body, name=name, grid=(R // tr,), in_specs=[spec] * 4, out_specs=[spec] * 3,
        out_shape=[jax.ShapeDtypeStruct((R, C), f32)] * 3,
        compiler_params=_params(("parallel",)),
    )(w, g, m, v)


def _place():
    return lax.axis_index("x"), lax.axis_index("y"), lax.axis_index("c")


def _other_chips(x, y):
    chips = [(1 - x, y), (x, 1 - y), (1 - x, 1 - y)]
    return chips, [2 * cx + cy for cx, cy in chips]


def _exchange_small(blk, name, reduce):
    R, C = blk.shape

    def body(x_ref, out_ref, *rest):
        if reduce:
            all_ref, send_sems, recv_sems = rest
        else:
            all_ref = out_ref
            send_sems, recv_sems = rest
        x, y, c = _place()
        me = 4 * x + 2 * y + c
        all_ref[me] = x_ref[...]
        copies = []
        for k in range(1, N_DEV):
            peer = tuple(1 - p if (k >> s) & 1 else p for p, s in ((x, 2), (y, 1), (c, 0)))
            cp = pltpu.make_async_remote_copy(src_ref=x_ref, dst_ref=all_ref.at[me], send_sem=send_sems.at[k - 1],
                                              recv_sem=recv_sems.at[k - 1], device_id=peer, device_id_type=MESH)
            cp.start()
            copies.append(cp)
        for cp in copies:
            cp.wait()
        if reduce:
            tot = all_ref[0]
            for d in range(1, N_DEV):
                tot = tot + all_ref[d]
            out_ref[...] = tot

    vm = pl.BlockSpec(memory_space=pltpu.VMEM)
    scratch = [pltpu.SemaphoreType.DMA((N_DEV - 1,)), pltpu.SemaphoreType.DMA((N_DEV - 1,))]
    if reduce:
        scratch = [pltpu.VMEM((N_DEV, R, C), f32)] + scratch
    return pl.pallas_call(
        body, name=name, in_specs=[vm], out_specs=vm,
        out_shape=jax.ShapeDtypeStruct((R, C) if reduce else (N_DEV, R, C), f32),
        scratch_shapes=scratch,
    )(blk)


def _cast_shard(place, w, name):
    R, C = w.shape
    tr = _tile(R, max(16, (1 << 19) // C // 16 * 16), 16)

    def body(p_ref, w_ref, o_ref):
        o_ref[0] = w_ref[...].astype(bf16)

    return pl.pallas_call(
        body, name=name,
        grid_spec=pltpu.PrefetchScalarGridSpec(
            num_scalar_prefetch=1, grid=(R // tr,),
            in_specs=[pl.BlockSpec((tr, C), lambda r, p: (r, 0))],
            out_specs=pl.BlockSpec((1, tr, C), lambda r, p: (p[1], r, 0))),
        out_shape=jax.ShapeDtypeStruct((N_CHIPS, R, C), bf16),
        compiler_params=_params(("parallel",)),
    )(place, w)


class _Phase:
    def __init__(self, arrays, out_shapes, aliases, n_copies, copies, arrivals, own_starts=(), own_waits=()):
        self.arrays, self.out_shapes, self.aliases = list(arrays), list(out_shapes), dict(aliases)
        self.n_copies, self.copies, self.arrivals = n_copies, copies, arrivals
        self.own_starts, self.own_waits = tuple(own_starts), tuple(own_waits)

    def sems(self):
        return [pltpu.SemaphoreType.DMA((self.n_copies,)), pltpu.SemaphoreType.DMA((self.n_copies,))]

    def _descriptors(self, pin, pout, send_sems, recv_sems):
        return [pltpu.make_async_remote_copy(src_ref=s, dst_ref=d, send_sem=send_sems.at[i], recv_sem=recv_sems.at[i],
                                             device_id=to, device_id_type=MESH)
                for i, (s, d, to) in enumerate(self.copies(pin, pout))]

    def _arrival(self, i, pin, pout, send_sems, recv_sems):
        dst = self.arrivals(pin, pout)[i]
        return pltpu.make_async_remote_copy(src_ref=dst, dst_ref=dst, send_sem=send_sems.at[i], recv_sem=recv_sems.at[i],
                                            device_id=_place(), device_id_type=MESH)

    def start(self, pin, pout, send_sems, recv_sems):
        for i, cp in enumerate(self._descriptors(pin, pout, send_sems, recv_sems)):
            if i not in self.own_starts:
                cp.start()

    def begin(self, i, pin, pout, send_sems, recv_sems):
        self._descriptors(pin, pout, send_sems, recv_sems)[i].start()

    def arrived(self, i, pin, pout, send_sems, recv_sems):
        self._arrival(i, pin, pout, send_sems, recv_sems).wait_recv()

    def finish(self, pin, pout, send_sems, recv_sems):
        for i in range(self.n_copies):
            if i not in self.own_waits:
                self._arrival(i, pin, pout, send_sems, recv_sems).wait_recv()
        for cp in self._descriptors(pin, pout, send_sems, recv_sems):
            cp.wait_send()


def _join(phases):
    if len(phases) == 1:
        return phases[0]
    ai = np.cumsum([0] + [len(p.arrays) for p in phases])
    oi = np.cumsum([0] + [len(p.out_shapes) for p in phases])

    def each(fn_name, pin, pout):
        return [item for k, p in enumerate(phases)
                for item in getattr(p, fn_name)(pin[ai[k]:ai[k + 1]], pout[oi[k]:oi[k + 1]])]

    aliases = {int(ai[k]) + i: int(oi[k]) + j for k, p in enumerate(phases) for i, j in p.aliases.items()}
    ci = np.cumsum([0] + [p.n_copies for p in phases])
    shifted = lambda attr: [int(ci[k]) + i for k, p in enumerate(phases) for i in getattr(p, attr)]
    return _Phase([a for p in phases for a in p.arrays], [s for p in phases for s in p.out_shapes], aliases,
                  int(ci[-1]), functools.partial(each, "copies"), functools.partial(each, "arrivals"),
                  shifted("own_starts"), shifted("own_waits"))


def _call(body, phase, *, name, grid, in_specs, out_specs, out_shape, scratch_shapes, args, prefetch=(), expose=False):
    seq = _params(("arbitrary",) * len(grid))
    np_ = len(prefetch)
    if phase is None:
        spec = pltpu.PrefetchScalarGridSpec(num_scalar_prefetch=np_, grid=grid, in_specs=in_specs, out_specs=out_specs,
                                            scratch_shapes=scratch_shapes)
        res = pl.pallas_call(body, name=name, grid_spec=spec, out_shape=out_shape, compiler_params=seq)(*prefetch, *args)
        return list(res), []
    ni, no, ns = len(in_specs), len(out_specs), len(scratch_shapes)
    pi, po = len(phase.arrays), len(phase.out_shapes)

    def hosted(*refs):
        cut = np.cumsum([np_, ni, pi, no, po, ns])
        pre, ins, pin, outs, pout, scr, sems = (refs[a:b] for a, b in zip([0, *cut], [*cut, len(refs)]))
        ids = [pl.program_id(d) for d in range(len(grid))]
        first = functools.reduce(lambda p, q: p & q, [i == 0 for i in ids])
        last = functools.reduce(lambda p, q: p & q, [i == g - 1 for i, g in zip(ids, grid)])
        pl.when(first)(lambda: phase.start(pin, pout, *sems))
        body(*pre, *ins, *outs, *scr, **({"carried": (pin, pout, sems)} if expose else {}))
        pl.when(last)(lambda: phase.finish(pin, pout, *sems))

    anyspace = pl.BlockSpec(memory_space=pl.ANY)
    spec = pltpu.PrefetchScalarGridSpec(
        num_scalar_prefetch=np_, grid=grid, in_specs=list(in_specs) + [anyspace] * pi,
        out_specs=list(out_specs) + [anyspace] * po, scratch_shapes=list(scratch_shapes) + phase.sems())
    res = pl.pallas_call(
        hosted, name=name, grid_spec=spec, out_shape=list(out_shape) + phase.out_shapes,
        input_output_aliases={np_ + ni + i: no + j for i, j in phase.aliases.items()}, compiler_params=seq,
    )(*prefetch, *args, *phase.arrays)
    return list(res[:no]), list(res[no:])


def _run_phases(name, phases):
    first = phases[0]
    pi, po = len(first.arrays), len(first.out_shapes)

    def body(*refs):
        pin, pout, sems = refs[:pi], refs[pi:pi + po], refs[pi + po:]
        for n, ph in enumerate(phases):
            ph.start(pin, pout, *sems[2 * n:2 * n + 2])
            ph.finish(pin, pout, *sems[2 * n:2 * n + 2])

    anyspace = pl.BlockSpec(memory_space=pl.ANY)
    return list(pl.pallas_call(
        body, name=name, in_specs=[anyspace] * pi, out_specs=[anyspace] * po, out_shape=first.out_shapes,
        input_output_aliases=first.aliases, scratch_shapes=[s for ph in phases for s in ph.sems()],
    )(*first.arrays))


def _half_rows(buf, c):
    half = buf.shape[1] // 2
    return pl.ds(c * half, half), pl.ds((1 - c) * half, half)


def _gather_phase(bufs, over_ici):
    n = len(bufs)
    shapes = [jax.ShapeDtypeStruct(b.shape, b.dtype) for b in bufs]

    def landed(out, which):
        x, y, c = _place()
        _, ks = _other_chips(x, y)
        return [out[a].at[ks[j], _half_rows(bufs[a], c)[which]] for a in range(n) for j in range(3)]

    def ici(pin, out):
        x, y, c = _place()
        chips, _ = _other_chips(x, y)
        mine = [out[a].at[2 * x + y, _half_rows(bufs[a], c)[0]] for a in range(n)]
        return [(mine[a], mine[a], (*chips[j], c)) for a in range(n) for j in range(3)]

    def d2d(pin, out):
        x, y, c = _place()
        return [(dst, dst, (x, y, 1 - c)) for dst in landed(out, 0)]

    if over_ici:
        return _Phase(bufs, shapes, {a: a for a in range(n)}, 3 * n, ici, lambda pin, out: landed(out, 0))
    return _Phase(bufs, shapes, {a: a for a in range(n)}, 3 * n, d2d, lambda pin, out: landed(out, 1))


def _feed_phase(buf):
    def chips():
        x, y, c = _place()
        return [(x if f < 2 else 1 - x, y if f % 2 == 0 else 1 - y) for f in (1, 2, 3)]

    def copies(pin, out):
        x, y, c = _place()
        mine = _half_rows(buf, c)[0]
        own = out[0].at[2 * x + y, mine]
        sent = [(own, own, (cx, cy, c)) for cx, cy in chips()]
        return sent + [(out[0].at[2 * cx + cy, mine], out[0].at[2 * cx + cy, mine], (x, y, 1 - c)) for cx, cy in chips()]

    def arrivals(pin, out):
        x, y, c = _place()
        mine, theirs = _half_rows(buf, c)
        return [out[0].at[2 * cx + cy, rows] for rows in (mine, theirs) for cx, cy in chips()]

    return _Phase([buf], [jax.ShapeDtypeStruct(buf.shape, buf.dtype)], {0: 0}, 6, copies, arrivals,
                  own_starts=(3, 4, 5), own_waits=range(6))


def _rs_sibling(grads, name):
    n = len(grads)

    def body(*refs):
        g, out, send_sems, recv_sems = refs[:n], refs[n:2 * n], refs[2 * n], refs[2 * n + 1]
        x, y, c = _place()
        copies = []
        for a in range(n):
            half = grads[a].shape[1] // 2
            cp = pltpu.make_async_remote_copy(src_ref=g[a].at[:, pl.ds((1 - c) * half, half)], dst_ref=out[a],
                                              send_sem=send_sems.at[a], recv_sem=recv_sems.at[a],
                                              device_id=(x, y, 1 - c), device_id_type=MESH)
            cp.start()
            copies.append(cp)
        for cp in copies:
            cp.wait()

    anyspace = pl.BlockSpec(memory_space=pl.ANY)
    return pl.pallas_call(
        body, name=name, in_specs=[anyspace] * n, out_specs=[anyspace] * n,
        out_shape=[jax.ShapeDtypeStruct((N_CHIPS, g.shape[1] // 2, g.shape[2]), g.dtype) for g in grads],
        scratch_shapes=[pltpu.SemaphoreType.DMA((n,)), pltpu.SemaphoreType.DMA((n,))],
    )(*grads)


def _rs_add_sibling(place, grad, got, name):
    _, R, C = grad.shape
    half = R // 2
    tr = _tile(half, max(16, (1 << 19) // C // 16 * 16), 16)
    nr = half // tr

    def body(p_ref, a_ref, b_ref, o_ref):
        o_ref[...] = (a_ref[...].astype(f32) + b_ref[...].astype(f32)).astype(o_ref.dtype)

    return pl.pallas_call(
        body, name=name,
        grid_spec=pltpu.PrefetchScalarGridSpec(
            num_scalar_prefetch=1, grid=(N_CHIPS, nr),
            in_specs=[pl.BlockSpec((1, tr, C), lambda k, r, p: (k, p[0] * nr + r, 0)),
                      pl.BlockSpec((1, tr, C), lambda k, r, p: (k, r, 0))],
            out_specs=pl.BlockSpec((1, tr, C), lambda k, r, p: (k, r, 0))),
        out_shape=jax.ShapeDtypeStruct((N_CHIPS, half, C), bf16),
        compiler_params=_params(("parallel", "parallel")),
    )(place, grad, got)


def _rs_chips_phase(parts):
    n = len(parts)

    def copies(p, fc):
        x, y, c = _place()
        chips, ks = _other_chips(x, y)
        return [(p[a].at[ks[j]], fc[a].at[j], (*chips[j], c)) for a in range(n) for j in range(3)]

    shapes = [jax.ShapeDtypeStruct((3,) + q.shape[1:], q.dtype) for q in parts]
    return _Phase(parts, shapes, {}, 3 * n, copies, lambda p, fc: [fc[a].at[j] for a in range(n) for j in range(3)])


def _rs_hand_phase(parts, from_chips):
    n = len(parts)

    def copies(pin, fs):
        x, y, c = _place()
        sib = (x, y, 1 - c)
        own = [(pin[a].at[2 * x + y], fs[a].at[0], sib) for a in range(n)]
        return own + [(pin[n + a].at[j], fs[a].at[1 + j], sib) for a in range(n) for j in range(3)]

    def arrivals(pin, fs):
        return [fs[a].at[0] for a in range(n)] + [fs[a].at[1 + j] for a in range(n) for j in range(3)]

    shapes = [jax.ShapeDtypeStruct((4,) + q.shape[1:], q.dtype) for q in parts]
    return _Phase(list(parts) + list(from_chips), shapes, {}, 4 * n, copies, arrivals)


class _Exchange:
    def __init__(self, place):
        self.place = place

    def feed(self, buf):
        return _feed_phase(buf)

    def gather(self, bufs, over_ici):
        return _gather_phase(bufs, over_ici)

    def pair_sums(self, names, grads):
        got = _rs_sibling(grads, "rs_sibling_" + names[0])
        return [_rs_add_sibling(self.place, g, r, "rs_add_" + n) for n, g, r in zip(names, grads, got)]

    def to_chips(self, parts):
        return _rs_chips_phase(parts)

    def to_sibling(self, parts, from_chips):
        return _rs_hand_phase(parts, from_chips)

    def hand_over(self, name, parts, from_chips):
        return _run_phases(name, [_rs_hand_phase(parts, from_chips)])


def _local_step(place, x, target, norm_mix, b_gate, rb_full, norm_ffn, norm_final, w_in, rest, exch):
    B, S, D = x.shape
    T = B * S
    x2 = x.reshape(T, D)
    tg2 = target.reshape(T, D)
    rope, decay = _rope_tables(S), _decay_tables()
    trows = _bias_rows(rb_full)
    g_fin = norm_final.reshape(1, D)

    mrg, ffn = ["w_ret_out", "w_att_out", "w_out"], ["w_ffn_gate", "w_ffn_up", "w_ffn_down"]
    (xn, proj), got = _in_proj(place, x2, norm_mix, _join([exch.feed(w_in), exch.gather([rest[n] for n in mrg], True)]))
    w_in, wb = got[0], {}
    (qr, kr, o, u, states), got = _ret_fwd(proj, B, S, rope, decay, _join([exch.gather([rest["w_ffn_gate"]], True),
                                                                         exch.gather(got[1:], False)]))
    wb.update(zip(mrg, got[1:]))
    (ao,), got = _att_fwd(proj, trows, B, S, _join([exch.gather([rest["w_ffn_up"], rest["w_ffn_down"]], True),
                                                    exch.gather(got[:1], False)]))
    wb["w_ffn_gate"] = got[2]
    w_ro, w_out = wb["w_ret_out"].reshape(-1, D), wb["w_out"].reshape(-1, D)
    (h1, yr, ya), got = _mix_fwd(x2, proj, u, ao, b_gate, w_ro, wb["w_att_out"], w_out, exch.gather(got[:2], False))
    wb.update(zip(ffn[1:], got))
    hn, a, b, f, dh2, part_fin = _ffn_fwd(h1, norm_ffn, wb["w_ffn_gate"], wb["w_ffn_up"], wb["w_ffn_down"], g_fin, tg2)

    da, db, dh1, part_ffn = _ffn_bwd(dh2, h1, norm_ffn, a, b, wb["w_ffn_gate"], wb["w_ffn_up"], wb["w_ffn_down"])
    ffn = ["w_ffn_down", "w_ffn_gate", "w_ffn_up"]
    p_ffn = exch.pair_sums(ffn, [_wgrad(f, dh2, 0, "wgrad_ffn_down")[0], _wgrad(hn, da, 1, "wgrad_ffn_gate")[0],
                                 _wgrad(hn, db, 1, "wgrad_ffn_up")[0]])
    (du, dao, dgl, mix, dyr, dya, part_bg), c_down = _mix_bwd(dh1, proj, yr, ya, b_gate, w_ro, wb["w_att_out"], w_out,
                                                               exch.to_chips(p_ffn[:1]))
    mrg = ["w_out", "w_ret_out", "w_att_out"]
    p_mrg = exch.pair_sums(mrg, [_wgrad(mix, dh1, 0, "wgrad_out")[0], _wgrad(u, dyr, 0, "wgrad_ret_out")[0],
                                 _wgrad(ao, dya, 1, "wgrad_att_out")[0]])
    (drq, drk, drv, drg), c_gate_up = _ret_bwd(proj, qr, kr, o, states, du, B, S, rope, decay, exch.to_chips(p_ffn[1:]))
    c_ffn = c_down + c_gate_up
    (daq, dak, dav, dvec), got = _att_bwd(proj, dao, trows, B, S, _join([exch.to_chips(p_mrg),
                                                                        exch.to_sibling(p_ffn, c_ffn)]))
    c_mrg, s_ffn = got[:len(mrg)], got[len(mrg):]
    dproj = jnp.concatenate([drq, drk, drv, drg, daq, dak, dav, dgl], axis=1)
    g_in, s_mrg = _wgrad(xn, dproj, 1, "wgrad_in", exch.to_sibling(p_mrg, c_mrg))
    p_in = exch.pair_sums(["w_in"], [g_in])
    (gx, part_mix), c_in = _in_proj_bwd(dproj, w_in, x2, norm_mix, dh1, exch.to_chips(p_in))
    s_in = exch.hand_over("rs_hand_w_in", p_in, c_in)
    gbig = dict(zip(ffn + mrg + ["w_in"], zip(p_ffn + p_mrg + p_in, c_ffn + c_mrg + c_in, s_ffn + s_mrg + s_in)))
    rows = lambda p, r: p.reshape(-1, 8, p.shape[-1])[:, r, :].sum(axis=0)
    lo = KWIN - 1 - (MAX_REL - 1)
    drb = jnp.concatenate([jnp.flip(dvec[:, lo:lo + N_REL - 1], axis=1), dvec[:, :lo].sum(axis=1, keepdims=True)], axis=1)
    gsmall = {
        "norm_mix": rows(part_mix, 0), "b_gate": rows(part_bg, 0), "rel_bias": drb, "norm_ffn": rows(part_ffn, 0),
        "norm_final": rows(part_fin, 0),
    }
    return rows(part_fin, 1), gx.reshape(B, S, D), gbig, gsmall


SMALL_ROWS = 16


def _pack_small(gs, loss_lanes):
    D = D_MODEL
    rb = jnp.pad(gs["rel_bias"].reshape(-1), (0, 3 * D - ATT_HEADS * N_REL)).reshape(3, D)
    rows = [gs["norm_mix"].reshape(1, D), gs["b_gate"].reshape(2, D), gs["norm_ffn"].reshape(1, D),
            gs["norm_final"].reshape(1, D), rb, loss_lanes.reshape(1, D)]
    used = sum(r.shape[0] for r in rows)
    return jnp.concatenate(rows + [jnp.zeros((SMALL_ROWS - used, D), f32)], axis=0)


def kernel(x, norm_mix, w_in, b_gate, rel_bias, w_ret_out, w_att_out, w_out, norm_ffn, w_ffn_gate, w_ffn_up, w_ffn_down, norm_final, loss_target, m_norm_mix, m_w_in, m_b_gate, m_rel_bias, m_w_ret_out, m_w_att_out, m_w_out, m_norm_ffn, m_w_ffn_gate, m_w_ffn_up, m_w_ffn_down, m_norm_final, v_norm_mix, v_w_in, v_b_gate, v_rel_bias, v_w_ret_out, v_w_att_out, v_w_out, v_norm_ffn, v_w_ffn_gate, v_w_ffn_up, v_w_ffn_down, v_norm_final):
    w = dict(norm_mix=norm_mix, w_in=w_in, b_gate=b_gate, rel_bias=rel_bias, w_ret_out=w_ret_out, w_att_out=w_att_out,
             w_out=w_out, norm_ffn=norm_ffn, w_ffn_gate=w_ffn_gate, w_ffn_up=w_ffn_up, w_ffn_down=w_ffn_down,
             norm_final=norm_final)
    m = dict(norm_mix=m_norm_mix, w_in=m_w_in, b_gate=m_b_gate, rel_bias=m_rel_bias, w_ret_out=m_w_ret_out,
             w_att_out=m_w_att_out, w_out=m_w_out, norm_ffn=m_norm_ffn, w_ffn_gate=m_w_ffn_gate, w_ffn_up=m_w_ffn_up,
             w_ffn_down=m_w_ffn_down, norm_final=m_norm_final)
    v = dict(norm_mix=v_norm_mix, w_in=v_w_in, b_gate=v_b_gate, rel_bias=v_rel_bias, w_ret_out=v_w_ret_out,
             w_att_out=v_w_att_out, w_out=v_w_out, norm_ffn=v_norm_ffn, w_ffn_gate=v_w_ffn_gate, w_ffn_up=v_w_ffn_up,
             w_ffn_down=v_w_ffn_down, norm_final=v_norm_final)
    xi, yi, ci = _place()
    k_me = 2 * xi + yi

    place = jnp.stack([ci, k_me]).astype(jnp.int32)
    big = [n for n, _ in BIG]

    bufs = {n: _cast_shard(place, w[n][0], "cast_" + n) for n in big}
    rest = {n: bufs[n] for n in big if n != "w_in"}
    nrel_loc = rel_bias.shape[-1]
    rb_all = _exchange_small(jnp.pad(rel_bias[0], ((0, 0), (0, 128 - nrel_loc))), "gather_rel_bias", False)
    rb_full = jnp.concatenate([rb_all[2 * k, :, :nrel_loc] for k in range(N_CHIPS)], axis=1)

    loss_lanes, grad_x, gbig, gsmall = _local_step(place, x, loss_target, norm_mix, b_gate, rb_full, norm_ffn, norm_final,
                                                   bufs["w_in"], rest, _Exchange(place))

    small = _exchange_small(_pack_small(gsmall, loss_lanes), "reduce_small", True)
    D = D_MODEL
    loss = jnp.sum(small[8])
    drb_full = small[5:8].reshape(-1)[:ATT_HEADS * N_REL].reshape(ATT_HEADS, N_REL)
    g = {
        "norm_mix": small[0:1], "b_gate": small[1:3].reshape(1, 2 * D), "norm_ffn": small[3:4], "norm_final": small[4],
        "rel_bias": lax.dynamic_slice_in_dim(drb_full, k_me * nrel_loc, nrel_loc, axis=1)[None],
    }

    delta, new_m, new_v = {}, {}, {}
    for n in big:
        g_, d_, m_, v_ = _adamw_sum(place, w[n][0], m[n][0], v[n][0], *gbig[n], "adamw_" + n)
        g[n], delta[n], new_m[n], new_v[n] = g_[None], d_[None], m_[None], v_[None]
    flat = lambda d: jnp.concatenate([d[n].reshape(-1) for n in SMALL])
    n_small = sum(int(np.prod(w[n].shape)) for n in SMALL)
    n_pad = -n_small % 1024
    packs = [jnp.pad(flat(d), (0, n_pad)).reshape(-1, 128) for d in (w, g, m, v)]
    outs = _adamw(*packs, "adamw_small")
    for res, dst in zip(outs, (delta, new_m, new_v)):
        off = 0
        fl = res.reshape(-1)
        for n in SMALL:
            sz = int(np.prod(w[n].shape))
            dst[n] = fl[off:off + sz].reshape(w[n].shape)
            off += sz

    return (loss, grad_x, *[g[n] for n in WEIGHTS], *[delta[n] for n in WEIGHTS], *[new_m[n] for n in WEIGHTS],
            *[new_v[n] for n in WEIGHTS])
```

```python
import functools

import numpy as np
import jax
import jax.numpy as jnp
from jax import lax
from jax.experimental import pallas as pl
from jax.experimental.pallas import tpu as pltpu

f32 = jnp.float32
bf16 = jnp.bfloat16

D_MODEL = 1024
CHUNK = 64
RET_HEADS = 4
RET_KEY_DIM = 128
RET_VAL_DIM = 256
ATT_HEADS = 8
ATT_HEAD_DIM = 64
ATT_W = ATT_HEADS * ATT_HEAD_DIM
BAND_CHUNKS = 8
PAD = BAND_CHUNKS * CHUNK
MAX_REL = 256
N_REL = CHUNK + MAX_REL
D_FF = 2816
N_IN = 6656
ROPE_BASE = 10000.0
EPS = 1e-6
NEG_INF = -1e30
C_RQ, C_RK, C_RV, C_RG, C_AQ, C_AK, C_AV, C_GL = 0, 512, 1024, 2048, 3072, 3584, 4096, 4608

ADAM_LR, ADAM_B1, ADAM_B2, ADAM_EPS, ADAM_WD, ADAM_STEP = 0.001, 0.9, 0.999, 1e-08, 0.01, 10

N_CHIPS = 4
N_DEV = 8
WGRAD_ACC_BYTES = 8 * 1024 * 1024
ROW_TILE = 512
BIG_ROW_TILE = 1024
QBLK = 256
KWIN = PAD + QBLK
TOEP = 1024
VMEM_LIMIT = 56 * 1024 * 1024
MESH = pl.DeviceIdType.MESH

BIG = (
    ("w_in", 1), ("w_ret_out", 0), ("w_att_out", 1), ("w_out", 0), ("w_ffn_gate", 1), ("w_ffn_up", 1), ("w_ffn_down", 0))
WEIGHTS = ("norm_mix", "w_in", "b_gate", "rel_bias", "w_ret_out", "w_att_out", "w_out", "norm_ffn", "w_ffn_gate",
           "w_ffn_up", "w_ffn_down", "norm_final")
SMALL = ("norm_mix", "b_gate", "rel_bias", "norm_ffn", "norm_final")


def _dot(a, b):
    return lax.dot_general(a, b, (((1,), (0,)), ((), ())), preferred_element_type=f32)


def _dot_nt(a, b):
    return lax.dot_general(a, b, (((1,), (1,)), ((), ())), preferred_element_type=f32)


def _dot_tn(a, b):
    return lax.dot_general(a, b, (((0,), (0,)), ((), ())), preferred_element_type=f32)


def _sig(x):
    return 1.0 / (1.0 + jnp.exp(-x))


def _tile(n, pref, mult):
    best = None
    for t in range(mult, min(n, pref) + 1, mult):
        if n % t == 0:
            best = t
    return best if best is not None else n


def _params(sem, vmem=VMEM_LIMIT):
    return pltpu.CompilerParams(dimension_semantics=sem, vmem_limit_bytes=vmem)


def _in_proj(place, x2, gamma, phase):
    T, D = x2.shape
    _, _, ns = phase.arrays[0].shape
    tm = _tile(T, BIG_ROW_TILE, 8)
    ni = T // tm

    def body(p_ref, x_ref, g_ref, xn_ref, pr_ref, xs_ref, w_ref, w_sem, carried):
        j, i = pl.program_id(0), pl.program_id(1)
        pin, pout, sems = carried
        rows = pl.ds(pl.multiple_of(i * tm, tm), tm)

        @pl.when(i == 0)
        def _():
            for f in range(1, N_CHIPS):
                @pl.when(j == f)
                def _():
                    phase.arrived(f - 1, pin, pout, *sems)
                    phase.begin(2 + f, pin, pout, *sems)
                    phase.arrived(2 + f, pin, pout, *sems)
            shard = pltpu.make_async_copy(pout[0].at[jnp.bitwise_xor(p_ref[1], j)], w_ref, w_sem)
            shard.start()
            shard.wait()

        @pl.when(j == 0)
        def _():
            x = x_ref[...]
            r = lax.rsqrt(jnp.mean(x * x, axis=-1, keepdims=True) + EPS)
            xn = (x * r * g_ref[...]).astype(bf16)
            xs_ref[rows, :] = xn
            xn_ref[...] = xn

        pr_ref[...] = _dot(xs_ref[rows, :], w_ref[...]).astype(bf16)

    first_pass = lambda j, i, p: (jnp.where(j == 0, i, ni - 1), 0)
    return _call(
        body, phase, name="in_proj", grid=(N_CHIPS, ni), prefetch=(place,), expose=True,
        in_specs=[pl.BlockSpec((tm, D), first_pass), pl.BlockSpec((1, D), lambda j, i, p: (0, 0))],
        out_specs=[pl.BlockSpec((tm, D), first_pass),
                   pl.BlockSpec((tm, ns), lambda j, i, p: (i, jnp.bitwise_xor(p[1], j)))],
        out_shape=[jax.ShapeDtypeStruct((T, D), bf16), jax.ShapeDtypeStruct((T, N_CHIPS * ns), bf16)],
        scratch_shapes=[pltpu.VMEM((T, D), bf16), pltpu.VMEM((D, ns), bf16), pltpu.SemaphoreType.DMA],
        args=(x2, gamma))


def _rope_tables(S):
    d = RET_KEY_DIM
    freqs = ROPE_BASE ** (-jnp.arange(0, d, 2, dtype=f32) / d)
    ang = jnp.arange(S, dtype=f32)[:, None] * freqs[None, :]
    cos, sin = jnp.cos(ang), jnp.sin(ang)
    return jnp.concatenate([cos, cos], axis=1), jnp.concatenate([-sin, sin], axis=1)


def _decay_tables():
    H = RET_HEADS
    log_g = jnp.log(1.0 - 2.0 ** (-5.0 - jnp.arange(H, dtype=f32)))
    p = jnp.arange(CHUNK, dtype=f32)
    intra = jnp.exp(log_g[:, None, None] * jnp.abs(p[:, None] - p[None, :]))
    q_dec = jnp.exp(log_g[:, None] * (p[None, :] + 1.0))
    k_dec = jnp.exp(log_g[:, None] * (CHUNK - 1.0 - p[None, :]))
    c_dec = jnp.exp(log_g * CHUNK)
    q_dec = jnp.broadcast_to(q_dec[:, :, None], (H, CHUNK, RET_KEY_DIM))
    k_dec = jnp.broadcast_to(k_dec[:, :, None], (H, CHUNK, RET_KEY_DIM))
    c_dec = jnp.broadcast_to(c_dec[:, None, None], (H, 1, RET_VAL_DIM))
    return intra, q_dec, k_dec, c_dec


K_SCALE = RET_KEY_DIM ** -0.5


RET_CHUNKS = 4


def _ret_tables_specs():
    whole = lambda *shape: pl.BlockSpec(shape, lambda b, i: (0,) * len(shape))
    return [whole(RET_HEADS, CHUNK, CHUNK), whole(RET_HEADS, CHUNK, RET_KEY_DIM), whole(RET_HEADS, CHUNK, RET_KEY_DIM),
            whole(RET_HEADS, 1, RET_VAL_DIM)]


def _rotate(x, cos, sn):
    return x * cos + pltpu.roll(x, RET_KEY_DIM // 2, 1) * sn


def _ret_fwd(proj, B, S, rope, decay, phase=None):
    T = B * S
    nc = S // CHUNK
    H, dk, dv = RET_HEADS, RET_KEY_DIM, RET_VAL_DIM
    sb = RET_CHUNKS * CHUNK
    ns = S // sb

    def body(q_ref, k_ref, v_ref, g_ref, cos_ref, sin_ref, intra_ref, qd_ref, kd_ref, cd_ref,
             qr_ref, kr_ref, o_ref, u_ref, st_ref, state_ref):
        @pl.when(pl.program_id(1) == 0)
        def _():
            state_ref[...] = jnp.zeros_like(state_ref)

        cos, sn = cos_ref[...], sin_ref[...]
        for h in range(H):
            hs = slice(h * dk, (h + 1) * dk)
            qr_ref[:, hs] = _rotate(q_ref[:, hs].astype(f32), cos, sn).astype(bf16)
            kr_ref[:, hs] = (_rotate(k_ref[:, hs].astype(f32), cos, sn) * K_SCALE).astype(bf16)
        states = [state_ref[h] for h in range(H)]
        for ci in range(RET_CHUNKS):
            r = slice(ci * CHUNK, (ci + 1) * CHUNK)
            for h in range(H):
                hk, hv = slice(h * dk, (h + 1) * dk), slice(h * dv, (h + 1) * dv)
                qi, ki, vi = qr_ref[r, hk], kr_ref[r, hk], v_ref[r, hv]
                stb = states[h].astype(bf16)
                st_ref[0, h, ci] = stb
                s = (_dot_nt(qi, ki) * intra_ref[h]).astype(bf16)
                o = _dot(s, vi) + _dot((qi.astype(f32) * qd_ref[h]).astype(bf16), stb)
                states[h] = states[h] * cd_ref[h] + _dot_tn((ki.astype(f32) * kd_ref[h]).astype(bf16), vi)
                mu = jnp.mean(o, axis=-1, keepdims=True)
                xc = o - mu
                var = jnp.mean(xc * xc, axis=-1, keepdims=True)
                oh = xc * lax.rsqrt(var + EPS)
                g = g_ref[r, hv].astype(f32)
                o_ref[r, hv] = o.astype(bf16)
                u_ref[r, hv] = (g * _sig(g) * oh).astype(bf16)
        for h in range(H):
            state_ref[h] = states[h]

    blk = lambda w, c: pl.BlockSpec((sb, w), lambda b, i: (b * ns + i, c))
    return _call(
        body, phase, name="ret_fwd", grid=(B, ns), scratch_shapes=[pltpu.VMEM((H, dk, dv), f32)],
        in_specs=[blk(H * dk, C_RQ // (H * dk)), blk(H * dk, C_RK // (H * dk)), blk(H * dv, C_RV // (H * dv)),
                  blk(H * dv, C_RG // (H * dv)),
                  pl.BlockSpec((sb, dk), lambda b, i: (i, 0)), pl.BlockSpec((sb, dk), lambda b, i: (i, 0)),
                  *_ret_tables_specs()],
        out_specs=[blk(H * dk, 0), blk(H * dk, 0), blk(H * dv, 0), blk(H * dv, 0),
                   pl.BlockSpec((1, H, RET_CHUNKS, dk, dv), lambda b, i: (b, 0, i, 0, 0))],
        out_shape=[jax.ShapeDtypeStruct((T, H * dk), bf16), jax.ShapeDtypeStruct((T, H * dk), bf16),
                   jax.ShapeDtypeStruct((T, H * dv), bf16), jax.ShapeDtypeStruct((T, H * dv), bf16),
                   jax.ShapeDtypeStruct((B, H, nc, dk, dv), bf16)],
        args=(proj, proj, proj, proj, *rope, *decay))


def _bias_rows(rb):
    last = rb[:, N_REL - 1:]
    return jnp.concatenate([
        jnp.broadcast_to(last, (ATT_HEADS, PAD - MAX_REL + 1)),
        jnp.flip(rb[:, :N_REL - 1], axis=1),
        jnp.broadcast_to(rb[:, :1], (ATT_HEADS, KWIN - PAD - CHUNK)),
        jnp.broadcast_to(last, (ATT_HEADS, TOEP - KWIN)),
    ], axis=1)


def _build_bias(t_ref, bias_ref):
    row = lax.broadcasted_iota(jnp.int32, (QBLK, KWIN), 0) // CHUNK
    col = lax.broadcasted_iota(jnp.int32, (QBLK, KWIN), 1) // CHUNK
    delta = BAND_CHUNKS + row - col
    vis = (delta >= 0) & (delta <= BAND_CHUNKS)
    for h in range(ATT_HEADS):
        t = jnp.broadcast_to(t_ref[h:h + 1, :], (QBLK, TOEP))
        rolled = pltpu.roll(t, 0, 1, stride=1, stride_axis=0)
        bias_ref[h] = jnp.where(vis, rolled[:, :KWIN], NEG_INF)


def _att_probs(qh, kh, bias):
    s = _dot_nt(qh, kh) * (ATT_HEAD_DIM ** -0.5) + bias
    m = jnp.max(s, axis=-1, keepdims=True)
    p = jnp.exp(s - m)
    return p * (1.0 / jnp.sum(p, axis=-1, keepdims=True))


def _first_of_pair():
    return lax.broadcasted_iota(jnp.int32, (1, 2 * ATT_HEAD_DIM), 1) < ATT_HEAD_DIM


def _by_window(i, step):
    sizes = list(range(QBLK, KWIN, QBLK))
    for n, nk in enumerate(sizes):
        pl.when(i == n)(functools.partial(step, nk))
    pl.when(i >= len(sizes))(functools.partial(step, KWIN))


def _att_fwd(proj, trows, B, S, phase=None):
    T = B * S
    nq = S // QBLK
    dh = ATT_HEAD_DIM

    def body(q_ref, k_ref, v_ref, t_ref, o_ref, bias_ref):
        i = pl.program_id(1)

        @pl.when((pl.program_id(0) == 0) & (i == 0))
        def _():
            _build_bias(t_ref, bias_ref)

        def step(nk):
            win = pl.ds(pl.multiple_of((i + 1) * QBLK - nk, QBLK), nk)
            kw, vw = k_ref[win, :], v_ref[win, :]
            first = _first_of_pair()
            outs = []
            for p in range(ATT_HEADS // 2):
                ps = slice(2 * p * dh, 2 * (p + 1) * dh)
                q2, k2, v2 = q_ref[:, ps], kw[:, ps], vw[:, ps]
                both = []
                for e in range(2):
                    qm = jnp.where(first == (e == 0), q2, jnp.zeros_like(q2))
                    pr = _att_probs(qm, k2, bias_ref[2 * p + e, :, KWIN - nk:])
                    both.append(_dot(pr.astype(bf16), v2))
                outs.append(jnp.where(first, both[0], both[1]))
            o_ref[...] = jnp.concatenate(outs, axis=1).astype(bf16)

        _by_window(i, step)

    return _call(
        body, phase, name="att_fwd", grid=(B, nq),
        in_specs=[pl.BlockSpec((QBLK, ATT_W), lambda b, i: (b * nq + i, C_AQ // ATT_W)),
                  pl.BlockSpec((S, ATT_W), lambda b, i: (b, C_AK // ATT_W)),
                  pl.BlockSpec((S, ATT_W), lambda b, i: (b, C_AV // ATT_W)),
                  pl.BlockSpec((ATT_HEADS, TOEP), lambda b, i: (0, 0))],
        out_specs=[pl.BlockSpec((QBLK, ATT_W), lambda b, i: (b * nq + i, 0))],
        out_shape=[jax.ShapeDtypeStruct((T, ATT_W), bf16)],
        scratch_shapes=[pltpu.VMEM((ATT_HEADS, QBLK, KWIN), f32)],
        args=(proj, proj, proj, trows))


def _gl_specs(tm):
    w = 512
    return [pl.BlockSpec((tm, w), functools.partial(lambda i, j: (i, C_GL // 512 + j), j=j)) for j in range(4)]


def _gates(gl_refs, bg_ref):
    gl = jnp.concatenate([r[...] for r in gl_refs], axis=1).astype(f32) + bg_ref[...]
    g = _sig(gl)
    return g[:, :D_MODEL], g[:, D_MODEL:]


def _mix_fwd(x2, proj, u, ao, b_gate, w_ro, w_ao, w_out, phase=None):
    T, D = x2.shape
    tm = _tile(T, ROW_TILE, 8)

    def body(x_ref, u_ref, ao_ref, g0, g1, g2, g3, bg_ref, wro_ref, wao_ref, wo_ref, h1_ref, yr_ref, ya_ref):
        yr = _dot(u_ref[...], wro_ref[...])
        ao = ao_ref[...]
        ya = jnp.concatenate([_dot(ao, wao_ref[k]) for k in range(N_CHIPS)], axis=1)
        gr, ga = _gates((g0, g1, g2, g3), bg_ref)
        mix = gr * yr + ga * ya
        h1_ref[...] = x_ref[...] + _dot(mix.astype(bf16), wo_ref[...])
        yr_ref[...] = yr.astype(bf16)
        ya_ref[...] = ya.astype(bf16)

    full = lambda a: pl.BlockSpec(a.shape, lambda i: (0,) * a.ndim)
    row = lambda n: pl.BlockSpec((tm, n), lambda i: (i, 0))
    return _call(
        body, phase, name="mix_fwd", grid=(T // tm,), scratch_shapes=[],
        in_specs=[row(D), row(D), row(ATT_W), *_gl_specs(tm), full(b_gate), full(w_ro), full(w_ao), full(w_out)],
        out_specs=[row(D), row(D), row(D)],
        out_shape=[jax.ShapeDtypeStruct((T, D), f32), jax.ShapeDtypeStruct((T, D), bf16),
                   jax.ShapeDtypeStruct((T, D), bf16)],
        args=(x2, u, ao, proj, proj, proj, proj, b_gate, w_ro, w_ao, w_out))


def _ffn_fwd(h1, g_ffn, wg, wu, wd, g_fin, target):
    T, D = h1.shape
    nf, _, tf = wg.shape
    tm = _tile(T, ROW_TILE, 8)

    def body(h1_ref, g_ref, wg_ref, wu_ref, wd_ref, gf_ref, tg_ref, hn_ref, a_ref, b_ref, f_ref, dh2_ref, part_ref):
        h1v = h1_ref[...]
        r = lax.rsqrt(jnp.mean(h1v * h1v, axis=-1, keepdims=True) + EPS)
        hn = (h1v * r * g_ref[...]).astype(bf16)
        hn_ref[...] = hn
        h2 = h1v
        for k in range(nf):
            a = _dot(hn, wg_ref[k])
            b = _dot(hn, wu_ref[k])
            f = ((a * _sig(a)) * b).astype(bf16)
            a_ref[k] = a.astype(bf16)
            b_ref[k] = b.astype(bf16)
            f_ref[k] = f
            h2 = h2 + _dot(f, wd_ref[k])
        r = lax.rsqrt(jnp.mean(h2 * h2, axis=-1, keepdims=True) + EPS)
        n = h2 * r
        gf = gf_ref[...]
        e = n * gf - tg_ref[...]
        dy = e * (1.0 / D)
        dn = dy * gf
        dh2_ref[...] = r * (dn - n * jnp.mean(dn * n, axis=-1, keepdims=True))
        part_ref[...] = jnp.zeros_like(part_ref)
        part_ref[0:1, :] = jnp.sum(dy * n, axis=0, keepdims=True)
        part_ref[1:2, :] = (0.5 / D) * jnp.sum(e * e, axis=0, keepdims=True)

    row = lambda n: pl.BlockSpec((tm, n), lambda i: (i, 0))
    vec = pl.BlockSpec((1, D), lambda i: (0, 0))
    col = pl.BlockSpec((nf, tm, tf), lambda i: (0, i, 0))
    held = lambda w: pl.BlockSpec(w.shape, lambda i: (0, 0, 0), pipeline_mode=pl.Buffered(1))
    act = jax.ShapeDtypeStruct((nf, T, tf), bf16)
    return pl.pallas_call(
        body, name="ffn_fwd", grid=(T // tm,),
        in_specs=[row(D), vec, held(wg), held(wu), held(wd), vec, row(D)],
        out_specs=[row(D), col, col, col, row(D), pl.BlockSpec((8, D), lambda i: (i, 0))],
        out_shape=[jax.ShapeDtypeStruct((T, D), bf16), act, act, act,
                   jax.ShapeDtypeStruct((T, D), f32), jax.ShapeDtypeStruct((T // tm * 8, D), f32)],
        compiler_params=_params(("parallel",)),
    )(h1, g_ffn, wg, wu, wd, g_fin, target)


def _ffn_bwd(dh2, h1, g_ffn, a, b, wg, wu, wd):
    T, D = h1.shape
    nf, _, tf = wg.shape
    tm = _tile(T, ROW_TILE // 2, 8)

    def body(dh2_ref, h1_ref, g_ref, a_ref, b_ref, wg_ref, wu_ref, wd_ref, da_ref, db_ref, dh1_ref, part_ref):
        dh2v = dh2_ref[...]
        dh2b = dh2v.astype(bf16)
        dhn = jnp.zeros((tm, D), f32)
        for k in range(nf):
            df = _dot_nt(dh2b, wd_ref[k])
            av = a_ref[k].astype(f32)
            sg = _sig(av)
            db = (df * (av * sg)).astype(bf16)
            da = (df * b_ref[k].astype(f32) * (sg * (1.0 + av * (1.0 - sg)))).astype(bf16)
            da_ref[k] = da
            db_ref[k] = db
            dhn = dhn + _dot_nt(da, wg_ref[k]) + _dot_nt(db, wu_ref[k])
        h = h1_ref[...]
        r = lax.rsqrt(jnp.mean(h * h, axis=-1, keepdims=True) + EPS)
        n = h * r
        dn = dhn * g_ref[...]
        dh1_ref[...] = dh2v + r * (dn - n * jnp.mean(dn * n, axis=-1, keepdims=True))
        part_ref[...] = jnp.zeros_like(part_ref)
        part_ref[0:1, :] = jnp.sum(dhn * n, axis=0, keepdims=True)

    row = lambda n: pl.BlockSpec((tm, n), lambda i: (i, 0))
    col = pl.BlockSpec((nf, tm, tf), lambda i: (0, i, 0))
    held = lambda w: pl.BlockSpec(w.shape, lambda i: (0, 0, 0), pipeline_mode=pl.Buffered(1))
    act = jax.ShapeDtypeStruct((nf, T, tf), bf16)
    return pl.pallas_call(
        body, name="ffn_bwd", grid=(T // tm,),
        in_specs=[row(D), row(D), pl.BlockSpec((1, D), lambda i: (0, 0)), col, col, held(wg), held(wu), held(wd)],
        out_specs=[col, col, row(D), pl.BlockSpec((8, D), lambda i: (i, 0))],
        out_shape=[act, act, jax.ShapeDtypeStruct((T, D), f32), jax.ShapeDtypeStruct((T // tm * 8, D), f32)],
        compiler_params=_params(("parallel",)),
    )(dh2, h1, g_ffn, a, b, wg, wu, wd)


def _mix_bwd(dh1, proj, yr, ya, b_gate, w_ro, w_ao, w_out, phase=None):
    T, D = dh1.shape
    tm = _tile(T, ROW_TILE, 8)

    def body(dh1_ref, g0, g1, g2, g3, bg_ref, yr_ref, ya_ref, wro_ref, wao_ref, wo_ref,
             du_ref, dao_ref, dgl_ref, mix_ref, dyr_ref, dya_ref, part_ref):
        dmix = _dot_nt(dh1_ref[...].astype(bf16), wo_ref[...])
        gr, ga = _gates((g0, g1, g2, g3), bg_ref)
        yr = yr_ref[...].astype(f32)
        ya = ya_ref[...].astype(f32)
        dyr = (dmix * gr).astype(bf16)
        dya = (dmix * ga).astype(bf16)
        dgl = jnp.concatenate([dmix * yr * gr * (1.0 - gr), dmix * ya * ga * (1.0 - ga)], axis=1)
        du_ref[...] = _dot_nt(dyr, wro_ref[...]).astype(bf16)
        ns = wao_ref.shape[2]
        dao = _dot_nt(dya[:, :ns], wao_ref[0])
        for k in range(1, N_CHIPS):
            dao = dao + _dot_nt(dya[:, k * ns:(k + 1) * ns], wao_ref[k])
        dao_ref[...] = dao.astype(bf16)
        dgl_ref[...] = dgl.astype(bf16)
        mix_ref[...] = (gr * yr + ga * ya).astype(bf16)
        dyr_ref[...] = dyr
        dya_ref[...] = dya
        part_ref[...] = jnp.zeros_like(part_ref)
        part_ref[0:1, :] = jnp.sum(dgl, axis=0, keepdims=True)

    full = lambda a: pl.BlockSpec(a.shape, lambda i: (0,) * a.ndim)
    row = lambda n: pl.BlockSpec((tm, n), lambda i: (i, 0))
    return _call(
        body, phase, name="mix_bwd", grid=(T // tm,), scratch_shapes=[],
        in_specs=[row(D), *_gl_specs(tm), full(b_gate), row(D), row(D), full(w_ro), full(w_ao), full(w_out)],
        out_specs=[row(D), row(ATT_W), row(2 * D), row(D), row(D), row(D), pl.BlockSpec((8, 2 * D), lambda i: (i, 0))],
        out_shape=[jax.ShapeDtypeStruct((T, D), bf16), jax.ShapeDtypeStruct((T, ATT_W), bf16),
                   jax.ShapeDtypeStruct((T, 2 * D), bf16), jax.ShapeDtypeStruct((T, D), bf16),
                   jax.ShapeDtypeStruct((T, D), bf16), jax.ShapeDtypeStruct((T, D), bf16),
                   jax.ShapeDtypeStruct((T // tm * 8, 2 * D), f32)],
        args=(dh1, proj, proj, proj, proj, b_gate, yr, ya, w_ro, w_ao, w_out))


def _ret_bwd(proj, qr, kr, o, states, du, B, S, rope, decay, phase=None):
    T = B * S
    nc = S // CHUNK
    H, dk, dv = RET_HEADS, RET_KEY_DIM, RET_VAL_DIM

    sb = RET_CHUNKS * CHUNK
    ns = S // sb

    def body(qr_ref, kr_ref, v_ref, g_ref, o_ref, st_ref, du_ref, cos_ref, sin_ref, intra_ref, qd_ref, kd_ref, cd_ref,
             dp_ref, dstate_ref):
        dq_ref, dk_ref = dp_ref.at[:, pl.ds(C_RQ, H * dk)], dp_ref.at[:, pl.ds(C_RK, H * dk)]
        dv_ref, dg_ref = dp_ref.at[:, pl.ds(C_RV, H * dv)], dp_ref.at[:, pl.ds(C_RG, H * dv)]

        @pl.when(pl.program_id(1) == 0)
        def _():
            dstate_ref[...] = jnp.zeros_like(dstate_ref)

        cos, snb = cos_ref[...], -sin_ref[...]
        dstates = [dstate_ref[h] for h in range(H)]
        for ci in reversed(range(RET_CHUNKS)):
            r = slice(ci * CHUNK, (ci + 1) * CHUNK)
            for h in range(H):
                hk, hv = slice(h * dk, (h + 1) * dk), slice(h * dv, (h + 1) * dv)
                intra, qd, kd = intra_ref[h], qd_ref[h], kd_ref[h]
                qi, ki, vi = qr_ref[r, hk], kr_ref[r, hk], v_ref[r, hv]
                si = st_ref[0, h, ci]
                o = o_ref[r, hv].astype(f32)
                mu = jnp.mean(o, axis=-1, keepdims=True)
                xc = o - mu
                rstd = lax.rsqrt(jnp.mean(xc * xc, axis=-1, keepdims=True) + EPS)
                oh = xc * rstd
                g = g_ref[r, hv].astype(f32)
                sg = _sig(g)
                dui = du_ref[r, hv].astype(f32)
                dg_ref[r, hv] = (dui * oh * (sg * (1.0 + g * (1.0 - sg)))).astype(bf16)
                doh = dui * (g * sg)
                do = rstd * (doh - jnp.mean(doh, axis=-1, keepdims=True)
                             - oh * jnp.mean(doh * oh, axis=-1, keepdims=True))
                dob = do.astype(bf16)
                p = (_dot_nt(qi, ki) * intra).astype(bf16)
                dsb = dstates[h].astype(bf16)
                kt = (ki.astype(f32) * kd).astype(bf16)
                qt = (qi.astype(f32) * qd).astype(bf16)
                dv_ref[r, hv] = (_dot_tn(p, dob) + _dot(kt, dsb)).astype(bf16)
                da = (_dot_nt(dob, vi) * intra).astype(bf16)
                dq = _dot(da, ki) + _dot_nt(dob, si) * qd
                dkk = (_dot_tn(da, qi) + _dot_nt(vi, dsb) * kd) * K_SCALE
                dq_ref[r, hk] = _rotate(dq, cos[r], snb[r]).astype(bf16)
                dk_ref[r, hk] = _rotate(dkk, cos[r], snb[r]).astype(bf16)
                dstates[h] = dstates[h] * cd_ref[h] + _dot_tn(qt, dob)
        for h in range(H):
            dstate_ref[h] = dstates[h]

    blk = lambda w, c: pl.BlockSpec((sb, w), lambda b, i: (b * ns + ns - 1 - i, c))
    return _call(
        body, phase, name="ret_bwd", grid=(B, ns),
        in_specs=[blk(H * dk, 0), blk(H * dk, 0), blk(H * dv, C_RV // (H * dv)), blk(H * dv, C_RG // (H * dv)),
                  blk(H * dv, 0),
                  pl.BlockSpec((1, H, RET_CHUNKS, dk, dv), lambda b, i: (b, 0, ns - 1 - i, 0, 0)),
                  blk(H * dv, 0),
                  pl.BlockSpec((sb, dk), lambda b, i: (ns - 1 - i, 0)), pl.BlockSpec((sb, dk), lambda b, i: (ns - 1 - i, 0)),
                  *_ret_tables_specs()],
        out_specs=[blk(C_AQ, 0)], out_shape=[jax.ShapeDtypeStruct((T, N_IN), bf16)],
        scratch_shapes=[pltpu.VMEM((H, dk, dv), f32)],
        args=(qr, kr, proj, proj, o, states, du, *rope, *decay))


def _att_bwd(proj, dao, trows, dproj, B, S, phase=None):
    T = B * S
    nq = S // QBLK
    dh = ATT_HEAD_DIM
    scale = ATT_HEAD_DIM ** -0.5

    def body(q_ref, k_ref, v_ref, do_ref, t_ref, _, dp_ref, vec_ref, bias_ref, dbias_ref, dka_ref, dva_ref):
        b, i = pl.program_id(0), pl.program_id(1)

        @pl.when((b == 0) & (i == 0))
        def _():
            _build_bias(t_ref, bias_ref)
            dbias_ref[...] = jnp.zeros_like(dbias_ref)

        @pl.when(i == 0)
        def _():
            dka_ref[...] = jnp.zeros_like(dka_ref)
            dva_ref[...] = jnp.zeros_like(dva_ref)

        def step(nk):
            win = pl.ds(pl.multiple_of((i + 1) * QBLK - nk, QBLK), nk)
            kw, vw = k_ref[win, :], v_ref[win, :]
            first = _first_of_pair()
            first_rows = lax.broadcasted_iota(jnp.int32, (2 * dh, 1), 0) < dh
            dqs, dks, dvs = [], [], []
            for p in range(ATT_HEADS // 2):
                ps = slice(2 * p * dh, 2 * (p + 1) * dh)
                q2, k2, v2, do2 = q_ref[:, ps], kw[:, ps], vw[:, ps], do_ref[:, ps]
                dq2, dk2, dv2 = [], [], []
                for e in range(2):
                    h = 2 * p + e
                    mine = first == (e == 0)
                    pr = _att_probs(jnp.where(mine, q2, jnp.zeros_like(q2)), k2, bias_ref[h, :, KWIN - nk:])
                    dp = _dot_nt(jnp.where(mine, do2, jnp.zeros_like(do2)), v2)
                    ds = pr * (dp - jnp.sum(pr * dp, axis=-1, keepdims=True))
                    dbias_ref[h, :, KWIN - nk:] += ds
                    dsb = (ds * scale).astype(bf16)
                    dq2.append(_dot(dsb, k2))
                    dk2.append(_dot_tn(q2, dsb))
                    dv2.append(_dot_tn(do2, pr.astype(bf16)))
                dqs.append(jnp.where(first, dq2[0], dq2[1]))
                dks.append(jnp.where(first_rows, dk2[0], dk2[1]))
                dvs.append(jnp.where(first_rows, dv2[0], dv2[1]))
            dp_ref[pl.ds(pl.multiple_of(i * QBLK, QBLK), QBLK), :ATT_W] = jnp.concatenate(dqs, axis=1).astype(bf16)
            dka_ref[:, win] += jnp.concatenate(dks, axis=0)
            dva_ref[:, win] += jnp.concatenate(dvs, axis=0)

        _by_window(i, step)

        @pl.when(i == nq - 1)
        def _():
            dp_ref[:, ATT_W:2 * ATT_W] = dka_ref[...].T.astype(bf16)
            dp_ref[:, 2 * ATT_W:] = dva_ref[...].T.astype(bf16)

        @pl.when((b == B - 1) & (i == nq - 1))
        def _():
            rr = lax.broadcasted_iota(jnp.int32, (QBLK, QBLK), 0)
            cc = lax.broadcasted_iota(jnp.int32, (QBLK, QBLK), 1)
            flip = jnp.where(rr + cc == QBLK - 1, 1.0, 0.0).astype(bf16)
            for h in range(ATT_HEADS):
                d = dbias_ref[h]
                hi = d.astype(bf16)
                lo = (d - hi.astype(f32)).astype(bf16)
                rev = _dot(flip, hi) + _dot(flip, lo)
                wide = jnp.concatenate([rev, jnp.zeros((QBLK, TOEP - KWIN), f32)], axis=1)
                rolled = pltpu.roll(wide, 0, 1, stride=1, stride_axis=0)
                vec_ref[h:h + 1, :] = jnp.sum(rolled, axis=0, keepdims=True)

    qspec = lambda c: pl.BlockSpec((QBLK, ATT_W), lambda b, i: (b * nq + i, c))
    kspec = lambda c: pl.BlockSpec((S, ATT_W), lambda b, i: (b, c))
    return _call(
        body, phase, name="att_bwd", grid=(B, nq), aliases={5: 0},
        in_specs=[qspec(C_AQ // ATT_W), kspec(C_AK // ATT_W), kspec(C_AV // ATT_W), qspec(0),
                  pl.BlockSpec((ATT_HEADS, TOEP), lambda b, i: (0, 0)), pl.BlockSpec(memory_space=pl.ANY)],
        out_specs=[pl.BlockSpec((S, 3 * ATT_W), lambda b, i: (b, C_AQ // (3 * ATT_W))),
                   pl.BlockSpec((ATT_HEADS, TOEP), lambda b, i: (0, 0))],
        out_shape=[jax.ShapeDtypeStruct((T, N_IN), bf16), jax.ShapeDtypeStruct((ATT_HEADS, TOEP), f32)],
        scratch_shapes=[pltpu.VMEM((ATT_HEADS, QBLK, KWIN), f32), pltpu.VMEM((ATT_HEADS, QBLK, KWIN), f32),
                        pltpu.VMEM((ATT_W, S), f32), pltpu.VMEM((ATT_W, S), f32)],
        args=(proj, proj, proj, dao, trows, dproj))


def _in_proj_bwd(dproj, w_in, x2, gamma, dh1, phase=None):
    T, D = x2.shape
    nk, _, tk = w_in.shape
    tm = _tile(T, BIG_ROW_TILE, 8)

    def body(dp_ref, w_ref, x_ref, g_ref, dh1_ref, dx_ref, part_ref, acc_ref):
        j = pl.program_id(1)

        @pl.when(j == 0)
        def _():
            acc_ref[...] = jnp.zeros_like(acc_ref)

        acc_ref[...] += _dot_nt(dp_ref[...], w_ref[0])

        @pl.when(j == nk - 1)
        def _():
            x = x_ref[...]
            r = lax.rsqrt(jnp.mean(x * x, axis=-1, keepdims=True) + EPS)
            n = x * r
            dxn = acc_ref[...]
            dn = dxn * g_ref[...]
            dx_ref[...] = dh1_ref[...] + r * (dn - n * jnp.mean(dn * n, axis=-1, keepdims=True))
            part_ref[...] = jnp.zeros_like(part_ref)
            part_ref[0:1, :] = jnp.sum(dxn * n, axis=0, keepdims=True)

    row = lambda n: pl.BlockSpec((tm, n), lambda i, j: (i, 0))
    return _call(
        body, phase, name="in_proj_bwd", grid=(T // tm, nk),
        in_specs=[pl.BlockSpec((tm, tk), lambda i, j: (i, j)), pl.BlockSpec((1, D, tk), lambda i, j: (j, 0, 0)), row(D),
                  pl.BlockSpec((1, D), lambda i, j: (0, 0)), row(D)],
        out_specs=[row(D), pl.BlockSpec((8, D), lambda i, j: (i, 0))],
        out_shape=[jax.ShapeDtypeStruct((T, D), f32), jax.ShapeDtypeStruct((T // tm * 8, D), f32)],
        scratch_shapes=[pltpu.VMEM((tm, D), f32)],
        args=(dproj, w_in, x2, gamma, dh1))


def _wgrad(a, b, shard_axis, name, phase=None):
    def spec(arr, sharded, tt):
        if arr.ndim == 3:
            return arr.shape[2], pl.BlockSpec((1, tt, arr.shape[2]), lambda s, t: (s, t, 0))
        if sharded:
            w = arr.shape[1] // N_CHIPS
            return w, pl.BlockSpec((tt, w), lambda s, t: (t, s))
        return arr.shape[1], pl.BlockSpec((tt, arr.shape[1]), lambda s, t: (t, 0))

    T = a.shape[-2]
    tt = _tile(T, BIG_ROW_TILE, 16)
    nt = T // tt
    whole = a.ndim == 2 and b.ndim == 2 and a.shape[1] * b.shape[1] * 4 <= WGRAD_ACC_BYTES
    if whole:
        K, N = a.shape[1], b.shape[1]
        a_spec, b_spec = pl.BlockSpec((tt, K), lambda s, t: (t, 0)), pl.BlockSpec((tt, N), lambda s, t: (t, 0))
        out_block = (N_CHIPS, K // N_CHIPS, N) if shard_axis == 0 else (N_CHIPS, K, N // N_CHIPS)
        out_spec = pl.BlockSpec(out_block, lambda s, t: (0, 0, 0))
    else:
        K, a_spec = spec(a, shard_axis == 0, tt)
        N, b_spec = spec(b, shard_axis == 1, tt)
        out_block = (N_CHIPS, K, N)
        out_spec = pl.BlockSpec((1, K, N), lambda s, t: (s, 0, 0))

    def body(a_ref, b_ref, o_ref, acc_ref):
        t = pl.program_id(1)

        @pl.when(t == 0)
        def _():
            acc_ref[...] = jnp.zeros_like(acc_ref)

        av = a_ref[0] if a.ndim == 3 else a_ref[...]
        bv = b_ref[0] if b.ndim == 3 else b_ref[...]
        acc_ref[...] += _dot_tn(av.astype(bf16), bv.astype(bf16))

        @pl.when(t == nt - 1)
        def _():
            if not whole:
                o_ref[0] = acc_ref[...].astype(bf16)
            else:
                _, kk, nn = out_block
                for s in range(N_CHIPS):
                    o_ref[s] = (acc_ref[s * kk:(s + 1) * kk, :] if shard_axis == 0
                                else acc_ref[:, s * nn:(s + 1) * nn]).astype(bf16)

    (grad,), carried = _call(
        body, phase, name=name, grid=(1 if whole else N_CHIPS, nt), in_specs=[a_spec, b_spec], out_specs=[out_spec],
        out_shape=[jax.ShapeDtypeStruct(out_block, bf16)], scratch_shapes=[pltpu.VMEM((K, N), f32)], args=(a, b))
    return grad, carried


def _adamw_sum(place, w, m, v, part, from_chips, from_sibling, name):
    R, C = w.shape
    half = R // 2
    tr = _tile(half, max(16, (1 << 18) // C // 16 * 16), 16)
    nr = half // tr

    def body(p_ref, w_ref, m_ref, v_ref, part_ref, fc_ref, fs_ref, g_ref, d_ref, mo_ref, vo_ref):
        up = lambda x: x.astype(f32)
        mine = ((up(part_ref[0]) + up(fc_ref[0])) + up(fc_ref[1])) + up(fc_ref[2])
        sibs = ((up(fs_ref[0]) + up(fs_ref[1])) + up(fs_ref[2])) + up(fs_ref[3])
        g_ = jnp.where(pl.program_id(0) == p_ref[0], mine, sibs)
        m_ = ADAM_B1 * m_ref[...] + (1.0 - ADAM_B1) * g_
        v_ = ADAM_B2 * v_ref[...] + (1.0 - ADAM_B2) * (g_ * g_)
        m_hat = m_ / (1.0 - ADAM_B1 ** ADAM_STEP)
        v_hat = v_ / (1.0 - ADAM_B2 ** ADAM_STEP)
        g_ref[...] = g_
        d_ref[...] = -ADAM_LR * (m_hat / (jnp.sqrt(v_hat) + ADAM_EPS) + ADAM_WD * w_ref[...])
        mo_ref[...] = m_
        vo_ref[...] = v_

    spec = pl.BlockSpec((tr, C), lambda h, r, p: (h * nr + r, 0))
    return pl.pallas_call(
        body, name=name,
        grid_spec=pltpu.PrefetchScalarGridSpec(
            num_scalar_prefetch=1, grid=(2, nr),
            in_specs=[spec, spec, spec, pl.BlockSpec((1, tr, C), lambda h, r, p: (p[1], jnp.where(h == p[0], r, 0), 0)),
                      pl.BlockSpec((3, tr, C), lambda h, r, p: (0, jnp.where(h == p[0], r, 0), 0)),
                      pl.BlockSpec((4, tr, C), lambda h, r, p: (0, jnp.where(h == p[0], 0, r), 0))],
            out_specs=[spec] * 4),
        out_shape=[jax.ShapeDtypeStruct((R, C), f32)] * 4,
        compiler_params=_params(("parallel", "parallel")),
    )(place, w, m, v, part, from_chips, from_sibling)


def _adamw(w, g, m, v, name):
    R, C = w.shape
    tr = _tile(R, max(8, (1 << 18) // C // 8 * 8), 8)

    def body(w_ref, g_ref, m_ref, v_ref, d_ref, mo_ref, vo_ref):
        g_ = g_ref[...]
        m_ = ADAM_B1 * m_ref[...] + (1.0 - ADAM_B1) * g_
        v_ = ADAM_B2 * v_ref[...] + (1.0 - ADAM_B2) * (g_ * g_)
        m_hat = m_ / (1.0 - ADAM_B1 ** ADAM_STEP)
        v_hat = v_ / (1.0 - ADAM_B2 ** ADAM_STEP)
        d_ref[...] = -ADAM_LR * (m_hat / (jnp.sqrt(v_hat) + ADAM_EPS) + ADAM_WD * w_ref[...])
        mo_ref[...] = m_
        vo_ref[...] = v_

    spec = pl.BlockSpec((tr, C), lambda i: (i, 0))
    return pl.pallas_call(
        body, name=name, grid=(R // tr,), in_specs=[spec] * 4, out_specs=[spec] * 3,
        out_shape=[jax.ShapeDtypeStruct((R, C), f32)] * 3,
        compiler_params=_params(("parallel",)),
    )(w, g, m, v)


def _place():
    return lax.axis_index("x"), lax.axis_index("y"), lax.axis_index("c")


def _other_chips(x, y):
    chips = [(1 - x, y), (x, 1 - y), (1 - x, 1 - y)]
    return chips, [2 * cx + cy for cx, cy in chips]


def _exchange_small(blk, name, reduce):
    R, C = blk.shape

    def body(x_ref, out_ref, *rest):
        if reduce:
            all_ref, send_sems, recv_sems = rest
        else:
            all_ref = out_ref
            send_sems, recv_sems = rest
        x, y, c = _place()
        me = 4 * x + 2 * y + c
        all_ref[me] = x_ref[...]
        copies = []
        for k in range(1, N_DEV):
            peer = tuple(1 - p if (k >> s) & 1 else p for p, s in ((x, 2), (y, 1), (c, 0)))
            cp = pltpu.make_async_remote_copy(src_ref=x_ref, dst_ref=all_ref.at[me], send_sem=send_sems.at[k - 1],
                                              recv_sem=recv_sems.at[k - 1], device_id=peer, device_id_type=MESH)
            cp.start()
            copies.append(cp)
        for cp in copies:
            cp.wait()
        if reduce:
            tot = all_ref[0]
            for d in range(1, N_DEV):
                tot = tot + all_ref[d]
            out_ref[...] = tot

    vm = pl.BlockSpec(memory_space=pltpu.VMEM)
    scratch = [pltpu.SemaphoreType.DMA((N_DEV - 1,)), pltpu.SemaphoreType.DMA((N_DEV - 1,))]
    if reduce:
        scratch = [pltpu.VMEM((N_DEV, R, C), f32)] + scratch
    return pl.pallas_call(
        body, name=name, in_specs=[vm], out_specs=vm,
        out_shape=jax.ShapeDtypeStruct((R, C) if reduce else (N_DEV, R, C), f32),
        scratch_shapes=scratch,
    )(blk)


def _cast_shard(place, w, name):
    R, C = w.shape
    tr = _tile(R, max(16, (1 << 19) // C // 16 * 16), 16)

    def body(p_ref, w_ref, o_ref):
        o_ref[0] = w_ref[...].astype(bf16)

    return pl.pallas_call(
        body, name=name,
        grid_spec=pltpu.PrefetchScalarGridSpec(
            num_scalar_prefetch=1, grid=(R // tr,),
            in_specs=[pl.BlockSpec((tr, C), lambda r, p: (r, 0))],
            out_specs=pl.BlockSpec((1, tr, C), lambda r, p: (p[1], r, 0))),
        out_shape=jax.ShapeDtypeStruct((N_CHIPS, R, C), bf16),
        compiler_params=_params(("parallel",)),
    )(place, w)


class _Phase:
    def __init__(self, arrays, out_shapes, aliases, n_copies, copies, arrivals, own_starts=(), own_waits=()):
        self.arrays, self.out_shapes, self.aliases = list(arrays), list(out_shapes), dict(aliases)
        self.n_copies, self.copies, self.arrivals = n_copies, copies, arrivals
        self.own_starts, self.own_waits = tuple(own_starts), tuple(own_waits)

    def sems(self):
        return [pltpu.SemaphoreType.DMA((self.n_copies,)), pltpu.SemaphoreType.DMA((self.n_copies,))]

    def _descriptors(self, pin, pout, send_sems, recv_sems):
        return [pltpu.make_async_remote_copy(src_ref=s, dst_ref=d, send_sem=send_sems.at[i], recv_sem=recv_sems.at[i],
                                             device_id=to, device_id_type=MESH)
                for i, (s, d, to) in enumerate(self.copies(pin, pout))]

    def _arrival(self, i, pin, pout, send_sems, recv_sems):
        dst = self.arrivals(pin, pout)[i]
        return pltpu.make_async_remote_copy(src_ref=dst, dst_ref=dst, send_sem=send_sems.at[i], recv_sem=recv_sems.at[i],
                                            device_id=_place(), device_id_type=MESH)

    def start(self, pin, pout, send_sems, recv_sems):
        for i, cp in enumerate(self._descriptors(pin, pout, send_sems, recv_sems)):
            if i not in self.own_starts:
                cp.start()

    def begin(self, i, pin, pout, send_sems, recv_sems):
        self._descriptors(pin, pout, send_sems, recv_sems)[i].start()

    def arrived(self, i, pin, pout, send_sems, recv_sems):
        self._arrival(i, pin, pout, send_sems, recv_sems).wait_recv()

    def finish(self, pin, pout, send_sems, recv_sems):
        for i in range(self.n_copies):
            if i not in self.own_waits:
                self._arrival(i, pin, pout, send_sems, recv_sems).wait_recv()
        for cp in self._descriptors(pin, pout, send_sems, recv_sems):
            cp.wait_send()


def _join(phases):
    if len(phases) == 1:
        return phases[0]
    ai = np.cumsum([0] + [len(p.arrays) for p in phases])
    oi = np.cumsum([0] + [len(p.out_shapes) for p in phases])

    def each(fn_name, pin, pout):
        return [item for k, p in enumerate(phases)
                for item in getattr(p, fn_name)(pin[ai[k]:ai[k + 1]], pout[oi[k]:oi[k + 1]])]

    aliases = {int(ai[k]) + i: int(oi[k]) + j for k, p in enumerate(phases) for i, j in p.aliases.items()}
    ci = np.cumsum([0] + [p.n_copies for p in phases])
    shifted = lambda attr: [int(ci[k]) + i for k, p in enumerate(phases) for i in getattr(p, attr)]
    return _Phase([a for p in phases for a in p.arrays], [s for p in phases for s in p.out_shapes], aliases,
                  int(ci[-1]), functools.partial(each, "copies"), functools.partial(each, "arrivals"),
                  shifted("own_starts"), shifted("own_waits"))


def _call(body, phase, *, name, grid, in_specs, out_specs, out_shape, scratch_shapes, args, prefetch=(), expose=False,
          aliases=None):
    seq = _params(("arbitrary",) * len(grid))
    np_ = len(prefetch)
    own = {np_ + i: j for i, j in (aliases or {}).items()}
    if phase is None:
        spec = pltpu.PrefetchScalarGridSpec(num_scalar_prefetch=np_, grid=grid, in_specs=in_specs, out_specs=out_specs,
                                            scratch_shapes=scratch_shapes)
        res = pl.pallas_call(body, name=name, grid_spec=spec, out_shape=out_shape, input_output_aliases=own,
                             compiler_params=seq)(*prefetch, *args)
        return list(res), []
    ni, no, ns = len(in_specs), len(out_specs), len(scratch_shapes)
    pi, po = len(phase.arrays), len(phase.out_shapes)

    def hosted(*refs):
        cut = np.cumsum([np_, ni, pi, no, po, ns])
        pre, ins, pin, outs, pout, scr, sems = (refs[a:b] for a, b in zip([0, *cut], [*cut, len(refs)]))
        ids = [pl.program_id(d) for d in range(len(grid))]
        first = functools.reduce(lambda p, q: p & q, [i == 0 for i in ids])
        last = functools.reduce(lambda p, q: p & q, [i == g - 1 for i, g in zip(ids, grid)])
        pl.when(first)(lambda: phase.start(pin, pout, *sems))
        body(*pre, *ins, *outs, *scr, **({"carried": (pin, pout, sems)} if expose else {}))
        pl.when(last)(lambda: phase.finish(pin, pout, *sems))

    anyspace = pl.BlockSpec(memory_space=pl.ANY)
    spec = pltpu.PrefetchScalarGridSpec(
        num_scalar_prefetch=np_, grid=grid, in_specs=list(in_specs) + [anyspace] * pi,
        out_specs=list(out_specs) + [anyspace] * po, scratch_shapes=list(scratch_shapes) + phase.sems())
    res = pl.pallas_call(
        hosted, name=name, grid_spec=spec, out_shape=list(out_shape) + phase.out_shapes,
        input_output_aliases={**own, **{np_ + ni + i: no + j for i, j in phase.aliases.items()}}, compiler_params=seq,
    )(*prefetch, *args, *phase.arrays)
    return list(res[:no]), list(res[no:])


def _run_phases(name, phases):
    first = phases[0]
    pi, po = len(first.arrays), len(first.out_shapes)

    def body(*refs):
        pin, pout, sems = refs[:pi], refs[pi:pi + po], refs[pi + po:]
        for n, ph in enumerate(phases):
            ph.start(pin, pout, *sems[2 * n:2 * n + 2])
            ph.finish(pin, pout, *sems[2 * n:2 * n + 2])

    anyspace = pl.BlockSpec(memory_space=pl.ANY)
    return list(pl.pallas_call(
        body, name=name, in_specs=[anyspace] * pi, out_specs=[anyspace] * po, out_shape=first.out_shapes,
        input_output_aliases=first.aliases, scratch_shapes=[s for ph in phases for s in ph.sems()],
    )(*first.arrays))


def _half_rows(buf, c):
    half = buf.shape[1] // 2
    return pl.ds(c * half, half), pl.ds((1 - c) * half, half)


def _gather_phase(bufs, over_ici):
    n = len(bufs)
    shapes = [jax.ShapeDtypeStruct(b.shape, b.dtype) for b in bufs]

    def landed(out, which):
        x, y, c = _place()
        _, ks = _other_chips(x, y)
        return [out[a].at[ks[j], _half_rows(bufs[a], c)[which]] for a in range(n) for j in range(3)]

    def ici(pin, out):
        x, y, c = _place()
        chips, _ = _other_chips(x, y)
        mine = [out[a].at[2 * x + y, _half_rows(bufs[a], c)[0]] for a in range(n)]
        return [(mine[a], mine[a], (*chips[j], c)) for a in range(n) for j in range(3)]

    def d2d(pin, out):
        x, y, c = _place()
        return [(dst, dst, (x, y, 1 - c)) for dst in landed(out, 0)]

    if over_ici:
        return _Phase(bufs, shapes, {a: a for a in range(n)}, 3 * n, ici, lambda pin, out: landed(out, 0))
    return _Phase(bufs, shapes, {a: a for a in range(n)}, 3 * n, d2d, lambda pin, out: landed(out, 1))


def _feed_phase(buf):
    def chips():
        x, y, c = _place()
        return [(x if f < 2 else 1 - x, y if f % 2 == 0 else 1 - y) for f in (1, 2, 3)]

    def copies(pin, out):
        x, y, c = _place()
        mine = _half_rows(buf, c)[0]
        own = out[0].at[2 * x + y, mine]
        sent = [(own, own, (cx, cy, c)) for cx, cy in chips()]
        return sent + [(out[0].at[2 * cx + cy, mine], out[0].at[2 * cx + cy, mine], (x, y, 1 - c)) for cx, cy in chips()]

    def arrivals(pin, out):
        x, y, c = _place()
        mine, theirs = _half_rows(buf, c)
        return [out[0].at[2 * cx + cy, rows] for rows in (mine, theirs) for cx, cy in chips()]

    return _Phase([buf], [jax.ShapeDtypeStruct(buf.shape, buf.dtype)], {0: 0}, 6, copies, arrivals,
                  own_starts=(3, 4, 5), own_waits=range(6))


def _rs_sibling(grads, name):
    n = len(grads)

    def body(*refs):
        g, out, send_sems, recv_sems = refs[:n], refs[n:2 * n], refs[2 * n], refs[2 * n + 1]
        x, y, c = _place()
        copies = []
        for a in range(n):
            half = grads[a].shape[1] // 2
            cp = pltpu.make_async_remote_copy(src_ref=g[a].at[:, pl.ds((1 - c) * half, half)], dst_ref=out[a],
                                              send_sem=send_sems.at[a], recv_sem=recv_sems.at[a],
                                              device_id=(x, y, 1 - c), device_id_type=MESH)
            cp.start()
            copies.append(cp)
        for cp in copies:
            cp.wait()

    anyspace = pl.BlockSpec(memory_space=pl.ANY)
    return pl.pallas_call(
        body, name=name, in_specs=[anyspace] * n, out_specs=[anyspace] * n,
        out_shape=[jax.ShapeDtypeStruct((N_CHIPS, g.shape[1] // 2, g.shape[2]), g.dtype) for g in grads],
        scratch_shapes=[pltpu.SemaphoreType.DMA((n,)), pltpu.SemaphoreType.DMA((n,))],
    )(*grads)


def _rs_add_sibling(place, grad, got, name):
    _, R, C = grad.shape
    half = R // 2
    tr = _tile(half, max(16, (1 << 19) // C // 16 * 16), 16)
    nr = half // tr

    def body(p_ref, a_ref, b_ref, o_ref):
        o_ref[...] = (a_ref[...].astype(f32) + b_ref[...].astype(f32)).astype(o_ref.dtype)

    return pl.pallas_call(
        body, name=name,
        grid_spec=pltpu.PrefetchScalarGridSpec(
            num_scalar_prefetch=1, grid=(N_CHIPS, nr),
            in_specs=[pl.BlockSpec((1, tr, C), lambda k, r, p: (k, p[0] * nr + r, 0)),
                      pl.BlockSpec((1, tr, C), lambda k, r, p: (k, r, 0))],
            out_specs=pl.BlockSpec((1, tr, C), lambda k, r, p: (k, r, 0))),
        out_shape=jax.ShapeDtypeStruct((N_CHIPS, half, C), bf16),
        compiler_params=_params(("parallel", "parallel")),
    )(place, grad, got)


def _rs_chips_phase(parts):
    n = len(parts)

    def copies(p, fc):
        x, y, c = _place()
        chips, ks = _other_chips(x, y)
        return [(p[a].at[ks[j]], fc[a].at[j], (*chips[j], c)) for a in range(n) for j in range(3)]

    shapes = [jax.ShapeDtypeStruct((3,) + q.shape[1:], q.dtype) for q in parts]
    return _Phase(parts, shapes, {}, 3 * n, copies, lambda p, fc: [fc[a].at[j] for a in range(n) for j in range(3)])


def _rs_hand_phase(parts, from_chips):
    n = len(parts)

    def copies(pin, fs):
        x, y, c = _place()
        sib = (x, y, 1 - c)
        own = [(pin[a].at[2 * x + y], fs[a].at[0], sib) for a in range(n)]
        return own + [(pin[n + a].at[j], fs[a].at[1 + j], sib) for a in range(n) for j in range(3)]

    def arrivals(pin, fs):
        return [fs[a].at[0] for a in range(n)] + [fs[a].at[1 + j] for a in range(n) for j in range(3)]

    shapes = [jax.ShapeDtypeStruct((4,) + q.shape[1:], q.dtype) for q in parts]
    return _Phase(list(parts) + list(from_chips), shapes, {}, 4 * n, copies, arrivals)


class _Exchange:
    def __init__(self, place):
        self.place = place

    def feed(self, buf):
        return _feed_phase(buf)

    def gather(self, bufs, over_ici):
        return _gather_phase(bufs, over_ici)

    def pair_sums(self, names, grads):
        got = _rs_sibling(grads, "rs_sibling_" + names[0])
        return [_rs_add_sibling(self.place, g, r, "rs_add_" + n) for n, g, r in zip(names, grads, got)]

    def to_chips(self, parts):
        return _rs_chips_phase(parts)

    def to_sibling(self, parts, from_chips):
        return _rs_hand_phase(parts, from_chips)

    def hand_over(self, name, parts, from_chips):
        return _run_phases(name, [_rs_hand_phase(parts, from_chips)])


def _local_step(place, x, target, norm_mix, b_gate, rb_full, norm_ffn, norm_final, w_in, rest, exch):
    B, S, D = x.shape
    T = B * S
    x2 = x.reshape(T, D)
    tg2 = target.reshape(T, D)
    rope, decay = _rope_tables(S), _decay_tables()
    trows = _bias_rows(rb_full)
    g_fin = norm_final.reshape(1, D)

    mrg, ffn = ["w_ret_out", "w_att_out", "w_out"], ["w_ffn_gate", "w_ffn_up", "w_ffn_down"]
    (xn, proj), got = _in_proj(place, x2, norm_mix, _join([exch.feed(w_in), exch.gather([rest[n] for n in mrg], True)]))
    w_in, wb = got[0], {}
    (qr, kr, o, u, states), got = _ret_fwd(proj, B, S, rope, decay, _join([exch.gather([rest["w_ffn_gate"]], True),
                                                                         exch.gather(got[1:], False)]))
    wb.update(zip(mrg, got[1:]))
    (ao,), got = _att_fwd(proj, trows, B, S, _join([exch.gather([rest["w_ffn_up"], rest["w_ffn_down"]], True),
                                                    exch.gather(got[:1], False)]))
    wb["w_ffn_gate"] = got[2]
    w_ro, w_out = wb["w_ret_out"].reshape(-1, D), wb["w_out"].reshape(-1, D)
    (h1, yr, ya), got = _mix_fwd(x2, proj, u, ao, b_gate, w_ro, wb["w_att_out"], w_out, exch.gather(got[:2], False))
    wb.update(zip(ffn[1:], got))
    hn, a, b, f, dh2, part_fin = _ffn_fwd(h1, norm_ffn, wb["w_ffn_gate"], wb["w_ffn_up"], wb["w_ffn_down"], g_fin, tg2)

    da, db, dh1, part_ffn = _ffn_bwd(dh2, h1, norm_ffn, a, b, wb["w_ffn_gate"], wb["w_ffn_up"], wb["w_ffn_down"])
    ffn = ["w_ffn_down", "w_ffn_gate", "w_ffn_up"]
    p_ffn = exch.pair_sums(ffn, [_wgrad(f, dh2, 0, "wgrad_ffn_down")[0], _wgrad(hn, da, 1, "wgrad_ffn_gate")[0],
                                 _wgrad(hn, db, 1, "wgrad_ffn_up")[0]])
    (du, dao, dgl, mix, dyr, dya, part_bg), c_down = _mix_bwd(dh1, proj, yr, ya, b_gate, w_ro, wb["w_att_out"], w_out,
                                                               exch.to_chips(p_ffn[:1]))
    mrg = ["w_out", "w_ret_out", "w_att_out"]
    p_mrg = exch.pair_sums(mrg, [_wgrad(mix, dh1, 0, "wgrad_out")[0], _wgrad(u, dyr, 0, "wgrad_ret_out")[0],
                                 _wgrad(ao, dya, 1, "wgrad_att_out")[0]])
    (dproj,), c_gate_up = _ret_bwd(proj, qr, kr, o, states, du, B, S, rope, decay, exch.to_chips(p_ffn[1:]))
    c_ffn = c_down + c_gate_up
    (dproj, dvec), got = _att_bwd(proj, dao, trows, dproj, B, S, _join([exch.to_chips(p_mrg),
                                                                        exch.to_sibling(p_ffn, c_ffn)]))
    c_mrg, s_ffn = got[:len(mrg)], got[len(mrg):]
    dproj = lax.dynamic_update_slice(dproj, dgl, (0, C_GL))
    g_in, s_mrg = _wgrad(xn, dproj, 1, "wgrad_in", exch.to_sibling(p_mrg, c_mrg))
    p_in = exch.pair_sums(["w_in"], [g_in])
    (gx, part_mix), c_in = _in_proj_bwd(dproj, w_in, x2, norm_mix, dh1, exch.to_chips(p_in))
    s_in = exch.hand_over("rs_hand_w_in", p_in, c_in)
    gbig = dict(zip(ffn + mrg + ["w_in"], zip(p_ffn + p_mrg + p_in, c_ffn + c_mrg + c_in, s_ffn + s_mrg + s_in)))
    rows = lambda p, r: p.reshape(-1, 8, p.shape[-1])[:, r, :].sum(axis=0)
    lo = KWIN - 1 - (MAX_REL - 1)
    drb = jnp.concatenate([jnp.flip(dvec[:, lo:lo + N_REL - 1], axis=1), dvec[:, :lo].sum(axis=1, keepdims=True)], axis=1)
    gsmall = {
        "norm_mix": rows(part_mix, 0), "b_gate": rows(part_bg, 0), "rel_bias": drb, "norm_ffn": rows(part_ffn, 0),
        "norm_final": rows(part_fin, 0),
    }
    return rows(part_fin, 1), gx.reshape(B, S, D), gbig, gsmall


SMALL_ROWS = 16


def _pack_small(gs, loss_lanes):
    D = D_MODEL
    rb = jnp.pad(gs["rel_bias"].reshape(-1), (0, 3 * D - ATT_HEADS * N_REL)).reshape(3, D)
    rows = [gs["norm_mix"].reshape(1, D), gs["b_gate"].reshape(2, D), gs["norm_ffn"].reshape(1, D),
            gs["norm_final"].reshape(1, D), rb, loss_lanes.reshape(1, D)]
    used = sum(r.shape[0] for r in rows)
    return jnp.concatenate(rows + [jnp.zeros((SMALL_ROWS - used, D), f32)], axis=0)


def kernel(x, norm_mix, w_in, b_gate, rel_bias, w_ret_out, w_att_out, w_out, norm_ffn, w_ffn_gate, w_ffn_up, w_ffn_down, norm_final, loss_target, m_norm_mix, m_w_in, m_b_gate, m_rel_bias, m_w_ret_out, m_w_att_out, m_w_out, m_norm_ffn, m_w_ffn_gate, m_w_ffn_up, m_w_ffn_down, m_norm_final, v_norm_mix, v_w_in, v_b_gate, v_rel_bias, v_w_ret_out, v_w_att_out, v_w_out, v_norm_ffn, v_w_ffn_gate, v_w_ffn_up, v_w_ffn_down, v_norm_final):
    w = dict(norm_mix=norm_mix, w_in=w_in, b_gate=b_gate, rel_bias=rel_bias, w_ret_out=w_ret_out, w_att_out=w_att_out,
             w_out=w_out, norm_ffn=norm_ffn, w_ffn_gate=w_ffn_gate, w_ffn_up=w_ffn_up, w_ffn_down=w_ffn_down,
             norm_final=norm_final)
    m = dict(norm_mix=m_norm_mix, w_in=m_w_in, b_gate=m_b_gate, rel_bias=m_rel_bias, w_ret_out=m_w_ret_out,
             w_att_out=m_w_att_out, w_out=m_w_out, norm_ffn=m_norm_ffn, w_ffn_gate=m_w_ffn_gate, w_ffn_up=m_w_ffn_up,
             w_ffn_down=m_w_ffn_down, norm_final=m_norm_final)
    v = dict(norm_mix=v_norm_mix, w_in=v_w_in, b_gate=v_b_gate, rel_bias=v_rel_bias, w_ret_out=v_w_ret_out,
             w_att_out=v_w_att_out, w_out=v_w_out, norm_ffn=v_norm_ffn, w_ffn_gate=v_w_ffn_gate, w_ffn_up=v_w_ffn_up,
             w_ffn_down=v_w_ffn_down, norm_final=v_norm_final)
    xi, yi, ci = _place()
    k_me = 2 * xi + yi

    place = jnp.stack([ci, k_me]).astype(jnp.int32)
    big = [n for n, _ in BIG]

    bufs = {n: _cast_shard(place, w[n][0], "cast_" + n) for n in big}
    rest = {n: bufs[n] for n in big if n != "w_in"}
    nrel_loc = rel_bias.shape[-1]
    rb_all = _exchange_small(jnp.pad(rel_bias[0], ((0, 0), (0, 128 - nrel_loc))), "gather_rel_bias", False)
    rb_full = jnp.concatenate([rb_all[2 * k, :, :nrel_loc] for k in range(N_CHIPS)], axis=1)

    loss_lanes, grad_x, gbig, gsmall = _local_step(place, x, loss_target, norm_mix, b_gate, rb_full, norm_ffn, norm_final,
                                                   bufs["w_in"], rest, _Exchange(place))

    small = _exchange_small(_pack_small(gsmall, loss_lanes), "reduce_small", True)
    D = D_MODEL
    loss = jnp.sum(small[8])
    drb_full = small[5:8].reshape(-1)[:ATT_HEADS * N_REL].reshape(ATT_HEADS, N_REL)
    g = {
        "norm_mix": small[0:1], "b_gate": small[1:3].reshape(1, 2 * D), "norm_ffn": small[3:4], "norm_final": small[4],
        "rel_bias": lax.dynamic_slice_in_dim(drb_full, k_me * nrel_loc, nrel_loc, axis=1)[None],
    }

    delta, new_m, new_v = {}, {}, {}
    for n in big:
        g_, d_, m_, v_ = _adamw_sum(place, w[n][0], m[n][0], v[n][0], *gbig[n], "adamw_" + n)
        g[n], delta[n], new_m[n], new_v[n] = g_[None], d_[None], m_[None], v_[None]
    flat = lambda d: jnp.concatenate([d[n].reshape(-1) for n in SMALL])
    n_small = sum(int(np.prod(w[n].shape)) for n in SMALL)
    n_pad = -n_small % 1024
    packs = [jnp.pad(flat(d), (0, n_pad)).reshape(-1, 128) for d in (w, g, m, v)]
    outs = _adamw(*packs, "adamw_small")
    for res, dst in zip(outs, (delta, new_m, new_v)):
        off = 0
        fl = res.reshape(-1)
        for n in SMALL:
            sz = int(np.prod(w[n].shape))
            dst[n] = fl[off:off + sz].reshape(w[n].shape)
            off += sz

    return (loss, grad_x, *[g[n] for n in WEIGHTS], *[delta[n] for n in WEIGHTS], *[new_m[n] for n in WEIGHTS],
            *[new_v[n] for n in WEIGHTS])
```

```python
import functools

import numpy as np
import jax
import jax.numpy as jnp
from jax import lax
from jax.experimental import pallas as pl
from jax.experimental.pallas import tpu as pltpu

f32 = jnp.float32
bf16 = jnp.bfloat16

D_MODEL = 1024
CHUNK = 64
RET_HEADS = 4
RET_KEY_DIM = 128
RET_VAL_DIM = 256
ATT_HEADS = 8
ATT_HEAD_DIM = 64
ATT_W = ATT_HEADS * ATT_HEAD_DIM
BAND_CHUNKS = 8
PAD = BAND_CHUNKS * CHUNK
MAX_REL = 256
N_REL = CHUNK + MAX_REL
D_FF = 2816
N_IN = 6656
ROPE_BASE = 10000.0
EPS = 1e-6
NEG_INF = -1e30
C_RQ, C_RK, C_RV, C_RG, C_AQ, C_AK, C_AV, C_GL = 0, 512, 1024, 2048, 3072, 3584, 4096, 4608

ADAM_LR, ADAM_B1, ADAM_B2, ADAM_EPS, ADAM_WD, ADAM_STEP = 0.001, 0.9, 0.999, 1e-08, 0.01, 10

N_CHIPS = 4
N_DEV = 8
WGRAD_ACC_BYTES = 8 * 1024 * 1024
ROW_TILE = 512
BIG_ROW_TILE = 1024
QBLK = 256
KWIN = PAD + QBLK
TOEP = 1024
VMEM_LIMIT = 56 * 1024 * 1024
MESH = pl.DeviceIdType.MESH

BIG = (
    ("w_in", 1), ("w_ret_out", 0), ("w_att_out", 1), ("w_out", 0), ("w_ffn_gate", 1), ("w_ffn_up", 1), ("w_ffn_down", 0))
WEIGHTS = ("norm_mix", "w_in", "b_gate", "rel_bias", "w_ret_out", "w_att_out", "w_out", "norm_ffn", "w_ffn_gate",
           "w_ffn_up", "w_ffn_down", "norm_final")
SMALL = ("norm_mix", "b_gate", "rel_bias", "norm_ffn", "norm_final")


def _dot(a, b):
    return lax.dot_general(a, b, (((1,), (0,)), ((), ())), preferred_element_type=f32)


def _dot_nt(a, b):
    return lax.dot_general(a, b, (((1,), (1,)), ((), ())), preferred_element_type=f32)


def _dot_tn(a, b):
    return lax.dot_general(a, b, (((0,), (0,)), ((), ())), preferred_element_type=f32)


def _sig(x):
    return 1.0 / (1.0 + jnp.exp(-x))


def _tile(n, pref, mult):
    best = None
    for t in range(mult, min(n, pref) + 1, mult):
        if n % t == 0:
            best = t
    return best if best is not None else n


def _params(sem, vmem=VMEM_LIMIT):
    return pltpu.CompilerParams(dimension_semantics=sem, vmem_limit_bytes=vmem)


def _in_proj(place, x2, gamma, phase):
    T, D = x2.shape
    _, _, ns = phase.arrays[0].shape
    tm = _tile(T, BIG_ROW_TILE, 8)
    ni = T // tm

    def body(p_ref, x_ref, g_ref, xn_ref, pr_ref, xs_ref, w_ref, w_sem, carried):
        j, i = pl.program_id(0), pl.program_id(1)
        pin, pout, sems = carried
        rows = pl.ds(pl.multiple_of(i * tm, tm), tm)

        @pl.when(i == 0)
        def _():
            for f in range(1, N_CHIPS):
                @pl.when(j == f)
                def _():
                    phase.arrived(f - 1, pin, pout, *sems)
                    phase.begin(2 + f, pin, pout, *sems)
                    phase.arrived(2 + f, pin, pout, *sems)
            shard = pltpu.make_async_copy(pout[0].at[jnp.bitwise_xor(p_ref[1], j)], w_ref, w_sem)
            shard.start()
            shard.wait()

        @pl.when(j == 0)
        def _():
            x = x_ref[...]
            r = lax.rsqrt(jnp.mean(x * x, axis=-1, keepdims=True) + EPS)
            xn = (x * r * g_ref[...]).astype(bf16)
            xs_ref[rows, :] = xn
            xn_ref[...] = xn

        pr_ref[...] = _dot(xs_ref[rows, :], w_ref[...]).astype(bf16)

    first_pass = lambda j, i, p: (jnp.where(j == 0, i, ni - 1), 0)
    return _call(
        body, phase, name="in_proj", grid=(N_CHIPS, ni), prefetch=(place,), expose=True,
        in_specs=[pl.BlockSpec((tm, D), first_pass), pl.BlockSpec((1, D), lambda j, i, p: (0, 0))],
        out_specs=[pl.BlockSpec((tm, D), first_pass),
                   pl.BlockSpec((tm, ns), lambda j, i, p: (i, jnp.bitwise_xor(p[1], j)))],
        out_shape=[jax.ShapeDtypeStruct((T, D), bf16), jax.ShapeDtypeStruct((T, N_CHIPS * ns), bf16)],
        scratch_shapes=[pltpu.VMEM((T, D), bf16), pltpu.VMEM((D, ns), bf16), pltpu.SemaphoreType.DMA],
        args=(x2, gamma))


def _rope_tables(S):
    d = RET_KEY_DIM
    freqs = ROPE_BASE ** (-jnp.arange(0, d, 2, dtype=f32) / d)
    ang = jnp.arange(S, dtype=f32)[:, None] * freqs[None, :]
    cos, sin = jnp.cos(ang), jnp.sin(ang)
    return jnp.concatenate([cos, cos], axis=1), jnp.concatenate([-sin, sin], axis=1)


def _decay_tables():
    H = RET_HEADS
    log_g = jnp.log(1.0 - 2.0 ** (-5.0 - jnp.arange(H, dtype=f32)))
    p = jnp.arange(CHUNK, dtype=f32)
    intra = jnp.exp(log_g[:, None, None] * jnp.abs(p[:, None] - p[None, :]))
    q_dec = jnp.exp(log_g[:, None] * (p[None, :] + 1.0))
    k_dec = jnp.exp(log_g[:, None] * (CHUNK - 1.0 - p[None, :]))
    c_dec = jnp.exp(log_g * CHUNK)
    q_dec = jnp.broadcast_to(q_dec[:, :, None], (H, CHUNK, RET_KEY_DIM))
    k_dec = jnp.broadcast_to(k_dec[:, :, None], (H, CHUNK, RET_KEY_DIM))
    c_dec = jnp.broadcast_to(c_dec[:, None, None], (H, 1, RET_VAL_DIM))
    return intra, q_dec, k_dec, c_dec


K_SCALE = RET_KEY_DIM ** -0.5


RET_CHUNKS = 4


def _ret_tables_specs():
    whole = lambda *shape: pl.BlockSpec(shape, lambda b, i: (0,) * len(shape))
    return [whole(RET_HEADS, CHUNK, CHUNK), whole(RET_HEADS, CHUNK, RET_KEY_DIM), whole(RET_HEADS, CHUNK, RET_KEY_DIM),
            whole(RET_HEADS, 1, RET_VAL_DIM)]


def _rotate(x, cos, sn):
    return x * cos + pltpu.roll(x, RET_KEY_DIM // 2, 1) * sn


def _ret_fwd(proj, B, S, rope, decay, phase=None):
    T = B * S
    nc = S // CHUNK
    H, dk, dv = RET_HEADS, RET_KEY_DIM, RET_VAL_DIM
    sb = RET_CHUNKS * CHUNK
    ns = S // sb

    def body(q_ref, k_ref, v_ref, g_ref, cos_ref, sin_ref, intra_ref, qd_ref, kd_ref, cd_ref,
             qr_ref, kr_ref, o_ref, u_ref, st_ref, state_ref):
        @pl.when(pl.program_id(1) == 0)
        def _():
            state_ref[...] = jnp.zeros_like(state_ref)

        cos, sn = cos_ref[...], sin_ref[...]
        for h in range(H):
            hs = slice(h * dk, (h + 1) * dk)
            qr_ref[:, hs] = _rotate(q_ref[:, hs].astype(f32), cos, sn).astype(bf16)
            kr_ref[:, hs] = (_rotate(k_ref[:, hs].astype(f32), cos, sn) * K_SCALE).astype(bf16)
        states = [state_ref[h] for h in range(H)]
        for ci in range(RET_CHUNKS):
            r = slice(ci * CHUNK, (ci + 1) * CHUNK)
            for h in range(H):
                hk, hv = slice(h * dk, (h + 1) * dk), slice(h * dv, (h + 1) * dv)
                qi, ki, vi = qr_ref[r, hk], kr_ref[r, hk], v_ref[r, hv]
                stb = states[h].astype(bf16)
                st_ref[0, h, ci] = stb
                s = (_dot_nt(qi, ki) * intra_ref[h]).astype(bf16)
                o = _dot(s, vi) + _dot((qi.astype(f32) * qd_ref[h]).astype(bf16), stb)
                states[h] = states[h] * cd_ref[h] + _dot_tn((ki.astype(f32) * kd_ref[h]).astype(bf16), vi)
                mu = jnp.mean(o, axis=-1, keepdims=True)
                xc = o - mu
                var = jnp.mean(xc * xc, axis=-1, keepdims=True)
                oh = xc * lax.rsqrt(var + EPS)
                g = g_ref[r, hv].astype(f32)
                o_ref[r, hv] = o.astype(bf16)
                u_ref[r, hv] = (g * _sig(g) * oh).astype(bf16)
        for h in range(H):
            state_ref[h] = states[h]

    blk = lambda w, c: pl.BlockSpec((sb, w), lambda b, i: (b * ns + i, c))
    return _call(
        body, phase, name="ret_fwd", grid=(B, ns), scratch_shapes=[pltpu.VMEM((H, dk, dv), f32)],
        in_specs=[blk(H * dk, C_RQ // (H * dk)), blk(H * dk, C_RK // (H * dk)), blk(H * dv, C_RV // (H * dv)),
                  blk(H * dv, C_RG // (H * dv)),
                  pl.BlockSpec((sb, dk), lambda b, i: (i, 0)), pl.BlockSpec((sb, dk), lambda b, i: (i, 0)),
                  *_ret_tables_specs()],
        out_specs=[blk(H * dk, 0), blk(H * dk, 0), blk(H * dv, 0), blk(H * dv, 0),
                   pl.BlockSpec((1, H, RET_CHUNKS, dk, dv), lambda b, i: (b, 0, i, 0, 0))],
        out_shape=[jax.ShapeDtypeStruct((T, H * dk), bf16), jax.ShapeDtypeStruct((T, H * dk), bf16),
                   jax.ShapeDtypeStruct((T, H * dv), bf16), jax.ShapeDtypeStruct((T, H * dv), bf16),
                   jax.ShapeDtypeStruct((B, H, nc, dk, dv), bf16)],
        args=(proj, proj, proj, proj, *rope, *decay))


def _bias_rows(rb):
    last = rb[:, N_REL - 1:]
    return jnp.concatenate([
        jnp.broadcast_to(last, (ATT_HEADS, PAD - MAX_REL + 1)),
        jnp.flip(rb[:, :N_REL - 1], axis=1),
        jnp.broadcast_to(rb[:, :1], (ATT_HEADS, KWIN - PAD - CHUNK)),
        jnp.broadcast_to(last, (ATT_HEADS, TOEP - KWIN)),
    ], axis=1)


def _build_bias(t_ref, bias_ref):
    row = lax.broadcasted_iota(jnp.int32, (QBLK, KWIN), 0) // CHUNK
    col = lax.broadcasted_iota(jnp.int32, (QBLK, KWIN), 1) // CHUNK
    delta = BAND_CHUNKS + row - col
    vis = (delta >= 0) & (delta <= BAND_CHUNKS)
    for h in range(ATT_HEADS):
        t = jnp.broadcast_to(t_ref[h:h + 1, :], (QBLK, TOEP))
        rolled = pltpu.roll(t, 0, 1, stride=1, stride_axis=0)
        bias_ref[h] = jnp.where(vis, rolled[:, :KWIN], NEG_INF)


def _att_probs(qh, kh, bias):
    s = _dot_nt(qh, kh) * (ATT_HEAD_DIM ** -0.5) + bias
    m = jnp.max(s, axis=-1, keepdims=True)
    p = jnp.exp(s - m)
    return p * (1.0 / jnp.sum(p, axis=-1, keepdims=True))


def _first_of_pair():
    return lax.broadcasted_iota(jnp.int32, (1, 2 * ATT_HEAD_DIM), 1) < ATT_HEAD_DIM


def _by_window(i, step):
    sizes = list(range(QBLK, KWIN, QBLK))
    for n, nk in enumerate(sizes):
        pl.when(i == n)(functools.partial(step, nk))
    pl.when(i >= len(sizes))(functools.partial(step, KWIN))


def _att_fwd(proj, trows, B, S, phase=None):
    T = B * S
    nq = S // QBLK
    dh = ATT_HEAD_DIM

    def body(q_ref, k_ref, v_ref, t_ref, o_ref, bias_ref):
        i = pl.program_id(1)

        @pl.when((pl.program_id(0) == 0) & (i == 0))
        def _():
            _build_bias(t_ref, bias_ref)

        def step(nk):
            win = pl.ds(pl.multiple_of((i + 1) * QBLK - nk, QBLK), nk)
            kw, vw = k_ref[win, :], v_ref[win, :]
            first = _first_of_pair()
            outs = []
            for p in range(ATT_HEADS // 2):
                ps = slice(2 * p * dh, 2 * (p + 1) * dh)
                q2, k2, v2 = q_ref[:, ps], kw[:, ps], vw[:, ps]
                both = []
                for e in range(2):
                    qm = jnp.where(first == (e == 0), q2, jnp.zeros_like(q2))
                    pr = _att_probs(qm, k2, bias_ref[2 * p + e, :, KWIN - nk:])
                    both.append(_dot(pr.astype(bf16), v2))
                outs.append(jnp.where(first, both[0], both[1]))
            o_ref[...] = jnp.concatenate(outs, axis=1).astype(bf16)

        _by_window(i, step)

    return _call(
        body, phase, name="att_fwd", grid=(B, nq),
        in_specs=[pl.BlockSpec((QBLK, ATT_W), lambda b, i: (b * nq + i, C_AQ // ATT_W)),
                  pl.BlockSpec((S, ATT_W), lambda b, i: (b, C_AK // ATT_W)),
                  pl.BlockSpec((S, ATT_W), lambda b, i: (b, C_AV // ATT_W)),
                  pl.BlockSpec((ATT_HEADS, TOEP), lambda b, i: (0, 0))],
        out_specs=[pl.BlockSpec((QBLK, ATT_W), lambda b, i: (b * nq + i, 0))],
        out_shape=[jax.ShapeDtypeStruct((T, ATT_W), bf16)],
        scratch_shapes=[pltpu.VMEM((ATT_HEADS, QBLK, KWIN), f32)],
        args=(proj, proj, proj, trows))


def _gl_specs(tm):
    w = 512
    return [pl.BlockSpec((tm, w), functools.partial(lambda i, j: (i, C_GL // 512 + j), j=j)) for j in range(4)]


def _gates(gl_refs, bg_ref):
    gl = jnp.concatenate([r[...] for r in gl_refs], axis=1).astype(f32) + bg_ref[...]
    g = _sig(gl)
    return g[:, :D_MODEL], g[:, D_MODEL:]


def _mix_fwd(x2, proj, u, ao, b_gate, w_ro, w_ao, w_out, phase=None):
    T, D = x2.shape
    tm = _tile(T, ROW_TILE, 8)

    def body(x_ref, u_ref, ao_ref, g0, g1, g2, g3, bg_ref, wro_ref, wao_ref, wo_ref, h1_ref, yr_ref, ya_ref):
        yr = _dot(u_ref[...], wro_ref[...])
        ao = ao_ref[...]
        ya = jnp.concatenate([_dot(ao, wao_ref[k]) for k in range(N_CHIPS)], axis=1)
        gr, ga = _gates((g0, g1, g2, g3), bg_ref)
        mix = gr * yr + ga * ya
        h1_ref[...] = x_ref[...] + _dot(mix.astype(bf16), wo_ref[...])
        yr_ref[...] = yr.astype(bf16)
        ya_ref[...] = ya.astype(bf16)

    full = lambda a: pl.BlockSpec(a.shape, lambda i: (0,) * a.ndim)
    row = lambda n: pl.BlockSpec((tm, n), lambda i: (i, 0))
    return _call(
        body, phase, name="mix_fwd", grid=(T // tm,), scratch_shapes=[],
        in_specs=[row(D), row(D), row(ATT_W), *_gl_specs(tm), full(b_gate), full(w_ro), full(w_ao), full(w_out)],
        out_specs=[row(D), row(D), row(D)],
        out_shape=[jax.ShapeDtypeStruct((T, D), f32), jax.ShapeDtypeStruct((T, D), bf16),
                   jax.ShapeDtypeStruct((T, D), bf16)],
        args=(x2, u, ao, proj, proj, proj, proj, b_gate, w_ro, w_ao, w_out))


def _ffn_fwd(h1, g_ffn, wg, wu, wd, g_fin, target):
    T, D = h1.shape
    nf, tf, _ = wg.shape
    tm = _tile(T, ROW_TILE, 8)

    def body(h1_ref, g_ref, wg_ref, wu_ref, wd_ref, gf_ref, tg_ref, hn_ref, a_ref, b_ref, f_ref, dh2_ref, part_ref):
        h1v = h1_ref[...]
        r = lax.rsqrt(jnp.mean(h1v * h1v, axis=-1, keepdims=True) + EPS)
        hn = (h1v * r * g_ref[...]).astype(bf16)
        hn_ref[...] = hn
        h2 = h1v
        for k in range(nf):
            a = _dot_nt(hn, wg_ref[k])
            b = _dot_nt(hn, wu_ref[k])
            f = ((a * _sig(a)) * b).astype(bf16)
            a_ref[k] = a.astype(bf16)
            b_ref[k] = b.astype(bf16)
            f_ref[k] = f
            h2 = h2 + _dot(f, wd_ref[k])
        r = lax.rsqrt(jnp.mean(h2 * h2, axis=-1, keepdims=True) + EPS)
        n = h2 * r
        gf = gf_ref[...]
        e = n * gf - tg_ref[...]
        dy = e * (1.0 / D)
        dn = dy * gf
        dh2_ref[...] = r * (dn - n * jnp.mean(dn * n, axis=-1, keepdims=True))
        part_ref[...] = jnp.zeros_like(part_ref)
        part_ref[0:1, :] = jnp.sum(dy * n, axis=0, keepdims=True)
        part_ref[1:2, :] = (0.5 / D) * jnp.sum(e * e, axis=0, keepdims=True)

    row = lambda n: pl.BlockSpec((tm, n), lambda i: (i, 0))
    vec = pl.BlockSpec((1, D), lambda i: (0, 0))
    col = pl.BlockSpec((nf, tm, tf), lambda i: (0, i, 0))
    held = lambda w: pl.BlockSpec(w.shape, lambda i: (0, 0, 0), pipeline_mode=pl.Buffered(1))
    act = jax.ShapeDtypeStruct((nf, T, tf), bf16)
    return pl.pallas_call(
        body, name="ffn_fwd", grid=(T // tm,),
        in_specs=[row(D), vec, held(wg), held(wu), held(wd), vec, row(D)],
        out_specs=[row(D), col, col, col, row(D), pl.BlockSpec((8, D), lambda i: (i, 0))],
        out_shape=[jax.ShapeDtypeStruct((T, D), bf16), act, act, act,
                   jax.ShapeDtypeStruct((T, D), f32), jax.ShapeDtypeStruct((T // tm * 8, D), f32)],
        compiler_params=_params(("parallel",)),
    )(h1, g_ffn, wg, wu, wd, g_fin, target)


def _ffn_bwd(dh2, h1, g_ffn, a, b, wg, wu, wd):
    T, D = h1.shape
    nf, tf, _ = wg.shape
    tm = _tile(T, ROW_TILE // 2, 8)

    def body(dh2_ref, h1_ref, g_ref, a_ref, b_ref, wg_ref, wu_ref, wd_ref, da_ref, db_ref, dh1_ref, part_ref):
        dh2v = dh2_ref[...]
        dh2b = dh2v.astype(bf16)
        dhn = jnp.zeros((tm, D), f32)
        for k in range(nf):
            df = _dot_nt(dh2b, wd_ref[k])
            av = a_ref[k].astype(f32)
            sg = _sig(av)
            db = (df * (av * sg)).astype(bf16)
            da = (df * b_ref[k].astype(f32) * (sg * (1.0 + av * (1.0 - sg)))).astype(bf16)
            da_ref[k] = da
            db_ref[k] = db
            dhn = dhn + _dot(da, wg_ref[k]) + _dot(db, wu_ref[k])
        h = h1_ref[...]
        r = lax.rsqrt(jnp.mean(h * h, axis=-1, keepdims=True) + EPS)
        n = h * r
        dn = dhn * g_ref[...]
        dh1_ref[...] = dh2v + r * (dn - n * jnp.mean(dn * n, axis=-1, keepdims=True))
        part_ref[...] = jnp.zeros_like(part_ref)
        part_ref[0:1, :] = jnp.sum(dhn * n, axis=0, keepdims=True)

    row = lambda n: pl.BlockSpec((tm, n), lambda i: (i, 0))
    col = pl.BlockSpec((nf, tm, tf), lambda i: (0, i, 0))
    held = lambda w: pl.BlockSpec(w.shape, lambda i: (0, 0, 0), pipeline_mode=pl.Buffered(1))
    act = jax.ShapeDtypeStruct((nf, T, tf), bf16)
    return pl.pallas_call(
        body, name="ffn_bwd", grid=(T // tm,),
        in_specs=[row(D), row(D), pl.BlockSpec((1, D), lambda i: (0, 0)), col, col, held(wg), held(wu), held(wd)],
        out_specs=[col, col, row(D), pl.BlockSpec((8, D), lambda i: (i, 0))],
        out_shape=[act, act, jax.ShapeDtypeStruct((T, D), f32), jax.ShapeDtypeStruct((T // tm * 8, D), f32)],
        compiler_params=_params(("parallel",)),
    )(dh2, h1, g_ffn, a, b, wg, wu, wd)


def _mix_bwd(dh1, proj, yr, ya, b_gate, w_ro, w_ao, w_out, phase=None):
    T, D = dh1.shape
    tm = _tile(T, ROW_TILE, 8)

    def body(dh1_ref, g0, g1, g2, g3, bg_ref, yr_ref, ya_ref, wro_ref, wao_ref, wo_ref,
             du_ref, dao_ref, dgl_ref, mix_ref, dyr_ref, dya_ref, part_ref):
        dmix = _dot_nt(dh1_ref[...].astype(bf16), wo_ref[...])
        gr, ga = _gates((g0, g1, g2, g3), bg_ref)
        yr = yr_ref[...].astype(f32)
        ya = ya_ref[...].astype(f32)
        dyr = (dmix * gr).astype(bf16)
        dya = (dmix * ga).astype(bf16)
        dgl = jnp.concatenate([dmix * yr * gr * (1.0 - gr), dmix * ya * ga * (1.0 - ga)], axis=1)
        du_ref[...] = _dot_nt(dyr, wro_ref[...]).astype(bf16)
        ns = wao_ref.shape[2]
        dao = _dot_nt(dya[:, :ns], wao_ref[0])
        for k in range(1, N_CHIPS):
            dao = dao + _dot_nt(dya[:, k * ns:(k + 1) * ns], wao_ref[k])
        dao_ref[...] = dao.astype(bf16)
        dgl_ref[...] = dgl.astype(bf16)
        mix_ref[...] = (gr * yr + ga * ya).astype(bf16)
        dyr_ref[...] = dyr
        dya_ref[...] = dya
        part_ref[...] = jnp.zeros_like(part_ref)
        part_ref[0:1, :] = jnp.sum(dgl, axis=0, keepdims=True)

    full = lambda a: pl.BlockSpec(a.shape, lambda i: (0,) * a.ndim)
    row = lambda n: pl.BlockSpec((tm, n), lambda i: (i, 0))
    return _call(
        body, phase, name="mix_bwd", grid=(T // tm,), scratch_shapes=[],
        in_specs=[row(D), *_gl_specs(tm), full(b_gate), row(D), row(D), full(w_ro), full(w_ao), full(w_out)],
        out_specs=[row(D), row(ATT_W), row(2 * D), row(D), row(D), row(D), pl.BlockSpec((8, 2 * D), lambda i: (i, 0))],
        out_shape=[jax.ShapeDtypeStruct((T, D), bf16), jax.ShapeDtypeStruct((T, ATT_W), bf16),
                   jax.ShapeDtypeStruct((T, 2 * D), bf16), jax.ShapeDtypeStruct((T, D), bf16),
                   jax.ShapeDtypeStruct((T, D), bf16), jax.ShapeDtypeStruct((T, D), bf16),
                   jax.ShapeDtypeStruct((T // tm * 8, 2 * D), f32)],
        args=(dh1, proj, proj, proj, proj, b_gate, yr, ya, w_ro, w_ao, w_out))


def _ret_bwd(proj, qr, kr, o, states, du, B, S, rope, decay, phase=None):
    T = B * S
    nc = S // CHUNK
    H, dk, dv = RET_HEADS, RET_KEY_DIM, RET_VAL_DIM

    sb = RET_CHUNKS * CHUNK
    ns = S // sb

    def body(qr_ref, kr_ref, v_ref, g_ref, o_ref, st_ref, du_ref, cos_ref, sin_ref, intra_ref, qd_ref, kd_ref, cd_ref,
             dp_ref, dstate_ref):
        dq_ref, dk_ref = dp_ref.at[:, pl.ds(C_RQ, H * dk)], dp_ref.at[:, pl.ds(C_RK, H * dk)]
        dv_ref, dg_ref = dp_ref.at[:, pl.ds(C_RV, H * dv)], dp_ref.at[:, pl.ds(C_RG, H * dv)]

        @pl.when(pl.program_id(1) == 0)
        def _():
            dstate_ref[...] = jnp.zeros_like(dstate_ref)

        cos, snb = cos_ref[...], -sin_ref[...]
        dstates = [dstate_ref[h] for h in range(H)]
        for ci in reversed(range(RET_CHUNKS)):
            r = slice(ci * CHUNK, (ci + 1) * CHUNK)
            for h in range(H):
                hk, hv = slice(h * dk, (h + 1) * dk), slice(h * dv, (h + 1) * dv)
                intra, qd, kd = intra_ref[h], qd_ref[h], kd_ref[h]
                qi, ki, vi = qr_ref[r, hk], kr_ref[r, hk], v_ref[r, hv]
                si = st_ref[0, h, ci]
                o = o_ref[r, hv].astype(f32)
                mu = jnp.mean(o, axis=-1, keepdims=True)
                xc = o - mu
                rstd = lax.rsqrt(jnp.mean(xc * xc, axis=-1, keepdims=True) + EPS)
                oh = xc * rstd
                g = g_ref[r, hv].astype(f32)
                sg = _sig(g)
                dui = du_ref[r, hv].astype(f32)
                dg_ref[r, hv] = (dui * oh * (sg * (1.0 + g * (1.0 - sg)))).astype(bf16)
                doh = dui * (g * sg)
                do = rstd * (doh - jnp.mean(doh, axis=-1, keepdims=True)
                             - oh * jnp.mean(doh * oh, axis=-1, keepdims=True))
                dob = do.astype(bf16)
                p = (_dot_nt(qi, ki) * intra).astype(bf16)
                dsb = dstates[h].astype(bf16)
                kt = (ki.astype(f32) * kd).astype(bf16)
                qt = (qi.astype(f32) * qd).astype(bf16)
                dv_ref[r, hv] = (_dot_tn(p, dob) + _dot(kt, dsb)).astype(bf16)
                da = (_dot_nt(dob, vi) * intra).astype(bf16)
                dq = _dot(da, ki) + _dot_nt(dob, si) * qd
                dkk = (_dot_tn(da, qi) + _dot_nt(vi, dsb) * kd) * K_SCALE
                dq_ref[r, hk] = _rotate(dq, cos[r], snb[r]).astype(bf16)
                dk_ref[r, hk] = _rotate(dkk, cos[r], snb[r]).astype(bf16)
                dstates[h] = dstates[h] * cd_ref[h] + _dot_tn(qt, dob)
        for h in range(H):
            dstate_ref[h] = dstates[h]

    blk = lambda w, c: pl.BlockSpec((sb, w), lambda b, i: (b * ns + ns - 1 - i, c))
    return _call(
        body, phase, name="ret_bwd", grid=(B, ns),
        in_specs=[blk(H * dk, 0), blk(H * dk, 0), blk(H * dv, C_RV // (H * dv)), blk(H * dv, C_RG // (H * dv)),
                  blk(H * dv, 0),
                  pl.BlockSpec((1, H, RET_CHUNKS, dk, dv), lambda b, i: (b, 0, ns - 1 - i, 0, 0)),
                  blk(H * dv, 0),
                  pl.BlockSpec((sb, dk), lambda b, i: (ns - 1 - i, 0)), pl.BlockSpec((sb, dk), lambda b, i: (ns - 1 - i, 0)),
                  *_ret_tables_specs()],
        out_specs=[blk(C_AQ, 0)], out_shape=[jax.ShapeDtypeStruct((T, N_IN), bf16)],
        scratch_shapes=[pltpu.VMEM((H, dk, dv), f32)],
        args=(qr, kr, proj, proj, o, states, du, *rope, *decay))


def _att_bwd(proj, dao, trows, dproj, B, S, phase=None):
    T = B * S
    nq = S // QBLK
    dh = ATT_HEAD_DIM
    scale = ATT_HEAD_DIM ** -0.5

    def body(q_ref, k_ref, v_ref, do_ref, t_ref, _, dp_ref, vec_ref, bias_ref, dbias_ref, dka_ref, dva_ref):
        b, i = pl.program_id(0), pl.program_id(1)

        @pl.when((b == 0) & (i == 0))
        def _():
            _build_bias(t_ref, bias_ref)
            dbias_ref[...] = jnp.zeros_like(dbias_ref)

        @pl.when(i == 0)
        def _():
            dka_ref[...] = jnp.zeros_like(dka_ref)
            dva_ref[...] = jnp.zeros_like(dva_ref)

        def step(nk):
            win = pl.ds(pl.multiple_of((i + 1) * QBLK - nk, QBLK), nk)
            kw, vw = k_ref[win, :], v_ref[win, :]
            first = _first_of_pair()
            first_rows = lax.broadcasted_iota(jnp.int32, (2 * dh, 1), 0) < dh
            dqs, dks, dvs = [], [], []
            for p in range(ATT_HEADS // 2):
                ps = slice(2 * p * dh, 2 * (p + 1) * dh)
                q2, k2, v2, do2 = q_ref[:, ps], kw[:, ps], vw[:, ps], do_ref[:, ps]
                dq2, dk2, dv2 = [], [], []
                for e in range(2):
                    h = 2 * p + e
                    mine = first == (e == 0)
                    pr = _att_probs(jnp.where(mine, q2, jnp.zeros_like(q2)), k2, bias_ref[h, :, KWIN - nk:])
                    dp = _dot_nt(jnp.where(mine, do2, jnp.zeros_like(do2)), v2)
                    ds = pr * (dp - jnp.sum(pr * dp, axis=-1, keepdims=True))
                    dbias_ref[h, :, KWIN - nk:] += ds
                    dsb = (ds * scale).astype(bf16)
                    dq2.append(_dot(dsb, k2))
                    dk2.append(_dot_tn(q2, dsb))
                    dv2.append(_dot_tn(do2, pr.astype(bf16)))
                dqs.append(jnp.where(first, dq2[0], dq2[1]))
                dks.append(jnp.where(first_rows, dk2[0], dk2[1]))
                dvs.append(jnp.where(first_rows, dv2[0], dv2[1]))
            dp_ref[pl.ds(pl.multiple_of(i * QBLK, QBLK), QBLK), :ATT_W] = jnp.concatenate(dqs, axis=1).astype(bf16)
            dka_ref[:, win] += jnp.concatenate(dks, axis=0)
            dva_ref[:, win] += jnp.concatenate(dvs, axis=0)

        _by_window(i, step)

        @pl.when(i == nq - 1)
        def _():
            dp_ref[:, ATT_W:2 * ATT_W] = dka_ref[...].T.astype(bf16)
            dp_ref[:, 2 * ATT_W:] = dva_ref[...].T.astype(bf16)

        @pl.when((b == B - 1) & (i == nq - 1))
        def _():
            rr = lax.broadcasted_iota(jnp.int32, (QBLK, QBLK), 0)
            cc = lax.broadcasted_iota(jnp.int32, (QBLK, QBLK), 1)
            flip = jnp.where(rr + cc == QBLK - 1, 1.0, 0.0).astype(bf16)
            for h in range(ATT_HEADS):
                d = dbias_ref[h]
                hi = d.astype(bf16)
                lo = (d - hi.astype(f32)).astype(bf16)
                rev = _dot(flip, hi) + _dot(flip, lo)
                wide = jnp.concatenate([rev, jnp.zeros((QBLK, TOEP - KWIN), f32)], axis=1)
                rolled = pltpu.roll(wide, 0, 1, stride=1, stride_axis=0)
                vec_ref[h:h + 1, :] = jnp.sum(rolled, axis=0, keepdims=True)

    qspec = lambda c: pl.BlockSpec((QBLK, ATT_W), lambda b, i: (b * nq + i, c))
    kspec = lambda c: pl.BlockSpec((S, ATT_W), lambda b, i: (b, c))
    return _call(
        body, phase, name="att_bwd", grid=(B, nq), aliases={5: 0},
        in_specs=[qspec(C_AQ // ATT_W), kspec(C_AK // ATT_W), kspec(C_AV // ATT_W), qspec(0),
                  pl.BlockSpec((ATT_HEADS, TOEP), lambda b, i: (0, 0)), pl.BlockSpec(memory_space=pl.ANY)],
        out_specs=[pl.BlockSpec((S, 3 * ATT_W), lambda b, i: (b, C_AQ // (3 * ATT_W))),
                   pl.BlockSpec((ATT_HEADS, TOEP), lambda b, i: (0, 0))],
        out_shape=[jax.ShapeDtypeStruct((T, N_IN), bf16), jax.ShapeDtypeStruct((ATT_HEADS, TOEP), f32)],
        scratch_shapes=[pltpu.VMEM((ATT_HEADS, QBLK, KWIN), f32), pltpu.VMEM((ATT_HEADS, QBLK, KWIN), f32),
                        pltpu.VMEM((ATT_W, S), f32), pltpu.VMEM((ATT_W, S), f32)],
        args=(proj, proj, proj, dao, trows, dproj))


def _in_proj_bwd(dproj, w_in, x2, gamma, dh1, phase=None):
    T, D = x2.shape
    nk, _, tk = w_in.shape
    tm = _tile(T, BIG_ROW_TILE, 8)

    def body(dp_ref, w_ref, x_ref, g_ref, dh1_ref, dx_ref, part_ref, acc_ref):
        j = pl.program_id(1)

        @pl.when(j == 0)
        def _():
            acc_ref[...] = jnp.zeros_like(acc_ref)

        acc_ref[...] += _dot_nt(dp_ref[...], w_ref[0])

        @pl.when(j == nk - 1)
        def _():
            x = x_ref[...]
            r = lax.rsqrt(jnp.mean(x * x, axis=-1, keepdims=True) + EPS)
            n = x * r
            dxn = acc_ref[...]
            dn = dxn * g_ref[...]
            dx_ref[...] = dh1_ref[...] + r * (dn - n * jnp.mean(dn * n, axis=-1, keepdims=True))
            part_ref[...] = jnp.zeros_like(part_ref)
            part_ref[0:1, :] = jnp.sum(dxn * n, axis=0, keepdims=True)

    row = lambda n: pl.BlockSpec((tm, n), lambda i, j: (i, 0))
    return _call(
        body, phase, name="in_proj_bwd", grid=(T // tm, nk),
        in_specs=[pl.BlockSpec((tm, tk), lambda i, j: (i, j)), pl.BlockSpec((1, D, tk), lambda i, j: (j, 0, 0)), row(D),
                  pl.BlockSpec((1, D), lambda i, j: (0, 0)), row(D)],
        out_specs=[row(D), pl.BlockSpec((8, D), lambda i, j: (i, 0))],
        out_shape=[jax.ShapeDtypeStruct((T, D), f32), jax.ShapeDtypeStruct((T // tm * 8, D), f32)],
        scratch_shapes=[pltpu.VMEM((tm, D), f32)],
        args=(dproj, w_in, x2, gamma, dh1))


def _wgrad(a, b, shard_axis, name, phase=None):
    def spec(arr, sharded, tt):
        if arr.ndim == 3:
            return arr.shape[2], pl.BlockSpec((1, tt, arr.shape[2]), lambda s, t: (s, t, 0))
        if sharded:
            w = arr.shape[1] // N_CHIPS
            return w, pl.BlockSpec((tt, w), lambda s, t: (t, s))
        return arr.shape[1], pl.BlockSpec((tt, arr.shape[1]), lambda s, t: (t, 0))

    T = a.shape[-2]
    tt = _tile(T, BIG_ROW_TILE, 16)
    nt = T // tt
    whole = a.ndim == 2 and b.ndim == 2 and a.shape[1] * b.shape[1] * 4 <= WGRAD_ACC_BYTES
    if whole:
        K, N = a.shape[1], b.shape[1]
        a_spec, b_spec = pl.BlockSpec((tt, K), lambda s, t: (t, 0)), pl.BlockSpec((tt, N), lambda s, t: (t, 0))
        out_block = (N_CHIPS, K // N_CHIPS, N) if shard_axis == 0 else (N_CHIPS, K, N // N_CHIPS)
        out_spec = pl.BlockSpec(out_block, lambda s, t: (0, 0, 0))
    else:
        K, a_spec = spec(a, shard_axis == 0, tt)
        N, b_spec = spec(b, shard_axis == 1, tt)
        out_block = (N_CHIPS, K, N)
        out_spec = pl.BlockSpec((1, K, N), lambda s, t: (s, 0, 0))

    def body(a_ref, b_ref, o_ref, acc_ref):
        t = pl.program_id(1)

        @pl.when(t == 0)
        def _():
            acc_ref[...] = jnp.zeros_like(acc_ref)

        av = a_ref[0] if a.ndim == 3 else a_ref[...]
        bv = b_ref[0] if b.ndim == 3 else b_ref[...]
        acc_ref[...] += _dot_tn(av.astype(bf16), bv.astype(bf16))

        @pl.when(t == nt - 1)
        def _():
            if not whole:
                o_ref[0] = acc_ref[...].astype(bf16)
            else:
                _, kk, nn = out_block
                for s in range(N_CHIPS):
                    o_ref[s] = (acc_ref[s * kk:(s + 1) * kk, :] if shard_axis == 0
                                else acc_ref[:, s * nn:(s + 1) * nn]).astype(bf16)

    (grad,), carried = _call(
        body, phase, name=name, grid=(1 if whole else N_CHIPS, nt), in_specs=[a_spec, b_spec], out_specs=[out_spec],
        out_shape=[jax.ShapeDtypeStruct(out_block, bf16)], scratch_shapes=[pltpu.VMEM((K, N), f32)], args=(a, b))
    return grad, carried


def _adamw_sum(place, w, m, v, part, from_chips, from_sibling, name):
    R, C = w.shape
    half = R // 2
    tr = _tile(half, max(16, (1 << 18) // C // 16 * 16), 16)
    nr = half // tr

    def body(p_ref, w_ref, m_ref, v_ref, part_ref, fc_ref, fs_ref, g_ref, d_ref, mo_ref, vo_ref):
        up = lambda x: x.astype(f32)
        mine = ((up(part_ref[0]) + up(fc_ref[0])) + up(fc_ref[1])) + up(fc_ref[2])
        sibs = ((up(fs_ref[0]) + up(fs_ref[1])) + up(fs_ref[2])) + up(fs_ref[3])
        g_ = jnp.where(pl.program_id(0) == p_ref[0], mine, sibs)
        m_ = ADAM_B1 * m_ref[...] + (1.0 - ADAM_B1) * g_
        v_ = ADAM_B2 * v_ref[...] + (1.0 - ADAM_B2) * (g_ * g_)
        m_hat = m_ / (1.0 - ADAM_B1 ** ADAM_STEP)
        v_hat = v_ / (1.0 - ADAM_B2 ** ADAM_STEP)
        g_ref[...] = g_
        d_ref[...] = -ADAM_LR * (m_hat / (jnp.sqrt(v_hat) + ADAM_EPS) + ADAM_WD * w_ref[...])
        mo_ref[...] = m_
        vo_ref[...] = v_

    spec = pl.BlockSpec((tr, C), lambda h, r, p: (h * nr + r, 0))
    return pl.pallas_call(
        body, name=name,
        grid_spec=pltpu.PrefetchScalarGridSpec(
            num_scalar_prefetch=1, grid=(2, nr),
            in_specs=[spec, spec, spec, pl.BlockSpec((1, tr, C), lambda h, r, p: (p[1], jnp.where(h == p[0], r, 0), 0)),
                      pl.BlockSpec((3, tr, C), lambda h, r, p: (0, jnp.where(h == p[0], r, 0), 0)),
                      pl.BlockSpec((4, tr, C), lambda h, r, p: (0, jnp.where(h == p[0], 0, r), 0))],
            out_specs=[spec] * 4),
        out_shape=[jax.ShapeDtypeStruct((R, C), f32)] * 4,
        compiler_params=_params(("parallel", "parallel")),
    )(place, w, m, v, part, from_chips, from_sibling)


def _adamw(w, g, m, v, name):
    R, C = w.shape
    tr = _tile(R, max(8, (1 << 18) // C // 8 * 8), 8)

    def body(w_ref, g_ref, m_ref, v_ref, d_ref, mo_ref, vo_ref):
        g_ = g_ref[...]
        m_ = ADAM_B1 * m_ref[...] + (1.0 - ADAM_B1) * g_
        v_ = ADAM_B2 * v_ref[...] + (1.0 - ADAM_B2) * (g_ * g_)
        m_hat = m_ / (1.0 - ADAM_B1 ** ADAM_STEP)
        v_hat = v_ / (1.0 - ADAM_B2 ** ADAM_STEP)
        d_ref[...] = -ADAM_LR * (m_hat / (jnp.sqrt(v_hat) + ADAM_EPS) + ADAM_WD * w_ref[...])
        mo_ref[...] = m_
        vo_ref[...] = v_

    spec = pl.BlockSpec((tr, C), lambda i: (i, 0))
    return pl.pallas_call(
        body, name=name, grid=(R // tr,), in_specs=[spec] * 4, out_specs=[spec] * 3,
        out_shape=[jax.ShapeDtypeStruct((R, C), f32)] * 3,
        compiler_params=_params(("parallel",)),
    )(w, g, m, v)


def _place():
    return lax.axis_index("x"), lax.axis_index("y"), lax.axis_index("c")


def _other_chips(x, y):
    chips = [(1 - x, y), (x, 1 - y), (1 - x, 1 - y)]
    return chips, [2 * cx + cy for cx, cy in chips]


def _exchange_small(blk, name, reduce):
    R, C = blk.shape

    def body(x_ref, out_ref, *rest):
        if reduce:
            all_ref, send_sems, recv_sems = rest
        else:
            all_ref = out_ref
            send_sems, recv_sems = rest
        x, y, c = _place()
        me = 4 * x + 2 * y + c
        all_ref[me] = x_ref[...]
        copies = []
        for k in range(1, N_DEV):
            peer = tuple(1 - p if (k >> s) & 1 else p for p, s in ((x, 2), (y, 1), (c, 0)))
            cp = pltpu.make_async_remote_copy(src_ref=x_ref, dst_ref=all_ref.at[me], send_sem=send_sems.at[k - 1],
                                              recv_sem=recv_sems.at[k - 1], device_id=peer, device_id_type=MESH)
            cp.start()
            copies.append(cp)
        for cp in copies:
            cp.wait()
        if reduce:
            tot = all_ref[0]
            for d in range(1, N_DEV):
                tot = tot + all_ref[d]
            out_ref[...] = tot

    vm = pl.BlockSpec(memory_space=pltpu.VMEM)
    scratch = [pltpu.SemaphoreType.DMA((N_DEV - 1,)), pltpu.SemaphoreType.DMA((N_DEV - 1,))]
    if reduce:
        scratch = [pltpu.VMEM((N_DEV, R, C), f32)] + scratch
    return pl.pallas_call(
        body, name=name, in_specs=[vm], out_specs=vm,
        out_shape=jax.ShapeDtypeStruct((R, C) if reduce else (N_DEV, R, C), f32),
        scratch_shapes=scratch,
    )(blk)


def _cast_shard(place, w, name):
    R, C = w.shape
    tr = _tile(R, max(16, (1 << 19) // C // 16 * 16), 16)

    def body(p_ref, w_ref, o_ref):
        o_ref[0] = w_ref[...].astype(bf16)

    return pl.pallas_call(
        body, name=name,
        grid_spec=pltpu.PrefetchScalarGridSpec(
            num_scalar_prefetch=1, grid=(R // tr,),
            in_specs=[pl.BlockSpec((tr, C), lambda r, p: (r, 0))],
            out_specs=pl.BlockSpec((1, tr, C), lambda r, p: (p[1], r, 0))),
        out_shape=jax.ShapeDtypeStruct((N_CHIPS, R, C), bf16),
        compiler_params=_params(("parallel",)),
    )(place, w)


class _Phase:
    def __init__(self, arrays, out_shapes, aliases, n_copies, copies, arrivals, own_starts=(), own_waits=()):
        self.arrays, self.out_shapes, self.aliases = list(arrays), list(out_shapes), dict(aliases)
        self.n_copies, self.copies, self.arrivals = n_copies, copies, arrivals
        self.own_starts, self.own_waits = tuple(own_starts), tuple(own_waits)

    def sems(self):
        return [pltpu.SemaphoreType.DMA((self.n_copies,)), pltpu.SemaphoreType.DMA((self.n_copies,))]

    def _descriptors(self, pin, pout, send_sems, recv_sems):
        return [pltpu.make_async_remote_copy(src_ref=s, dst_ref=d, send_sem=send_sems.at[i], recv_sem=recv_sems.at[i],
                                             device_id=to, device_id_type=MESH)
                for i, (s, d, to) in enumerate(self.copies(pin, pout))]

    def _arrival(self, i, pin, pout, send_sems, recv_sems):
        dst = self.arrivals(pin, pout)[i]
        return pltpu.make_async_remote_copy(src_ref=dst, dst_ref=dst, send_sem=send_sems.at[i], recv_sem=recv_sems.at[i],
                                            device_id=_place(), device_id_type=MESH)

    def start(self, pin, pout, send_sems, recv_sems):
        for i, cp in enumerate(self._descriptors(pin, pout, send_sems, recv_sems)):
            if i not in self.own_starts:
                cp.start()

    def begin(self, i, pin, pout, send_sems, recv_sems):
        self._descriptors(pin, pout, send_sems, recv_sems)[i].start()

    def arrived(self, i, pin, pout, send_sems, recv_sems):
        self._arrival(i, pin, pout, send_sems, recv_sems).wait_recv()

    def finish(self, pin, pout, send_sems, recv_sems):
        for i in range(self.n_copies):
            if i not in self.own_waits:
                self._arrival(i, pin, pout, send_sems, recv_sems).wait_recv()
        for cp in self._descriptors(pin, pout, send_sems, recv_sems):
            cp.wait_send()


def _join(phases):
    if len(phases) == 1:
        return phases[0]
    ai = np.cumsum([0] + [len(p.arrays) for p in phases])
    oi = np.cumsum([0] + [len(p.out_shapes) for p in phases])

    def each(fn_name, pin, pout):
        return [item for k, p in enumerate(phases)
                for item in getattr(p, fn_name)(pin[ai[k]:ai[k + 1]], pout[oi[k]:oi[k + 1]])]

    aliases = {int(ai[k]) + i: int(oi[k]) + j for k, p in enumerate(phases) for i, j in p.aliases.items()}
    ci = np.cumsum([0] + [p.n_copies for p in phases])
    shifted = lambda attr: [int(ci[k]) + i for k, p in enumerate(phases) for i in getattr(p, attr)]
    return _Phase([a for p in phases for a in p.arrays], [s for p in phases for s in p.out_shapes], aliases,
                  int(ci[-1]), functools.partial(each, "copies"), functools.partial(each, "arrivals"),
                  shifted("own_starts"), shifted("own_waits"))


def _call(body, phase, *, name, grid, in_specs, out_specs, out_shape, scratch_shapes, args, prefetch=(), expose=False,
          aliases=None):
    seq = _params(("arbitrary",) * len(grid))
    np_ = len(prefetch)
    own = {np_ + i: j for i, j in (aliases or {}).items()}
    if phase is None:
        spec = pltpu.PrefetchScalarGridSpec(num_scalar_prefetch=np_, grid=grid, in_specs=in_specs, out_specs=out_specs,
                                            scratch_shapes=scratch_shapes)
        res = pl.pallas_call(body, name=name, grid_spec=spec, out_shape=out_shape, input_output_aliases=own,
                             compiler_params=seq)(*prefetch, *args)
        return list(res), []
    ni, no, ns = len(in_specs), len(out_specs), len(scratch_shapes)
    pi, po = len(phase.arrays), len(phase.out_shapes)

    def hosted(*refs):
        cut = np.cumsum([np_, ni, pi, no, po, ns])
        pre, ins, pin, outs, pout, scr, sems = (refs[a:b] for a, b in zip([0, *cut], [*cut, len(refs)]))
        ids = [pl.program_id(d) for d in range(len(grid))]
        first = functools.reduce(lambda p, q: p & q, [i == 0 for i in ids])
        last = functools.reduce(lambda p, q: p & q, [i == g - 1 for i, g in zip(ids, grid)])
        pl.when(first)(lambda: phase.start(pin, pout, *sems))
        body(*pre, *ins, *outs, *scr, **({"carried": (pin, pout, sems)} if expose else {}))
        pl.when(last)(lambda: phase.finish(pin, pout, *sems))

    anyspace = pl.BlockSpec(memory_space=pl.ANY)
    spec = pltpu.PrefetchScalarGridSpec(
        num_scalar_prefetch=np_, grid=grid, in_specs=list(in_specs) + [anyspace] * pi,
        out_specs=list(out_specs) + [anyspace] * po, scratch_shapes=list(scratch_shapes) + phase.sems())
    res = pl.pallas_call(
        hosted, name=name, grid_spec=spec, out_shape=list(out_shape) + phase.out_shapes,
        input_output_aliases={**own, **{np_ + ni + i: no + j for i, j in phase.aliases.items()}}, compiler_params=seq,
    )(*prefetch, *args, *phase.arrays)
    return list(res[:no]), list(res[no:])


def _run_phases(name, phases):
    first = phases[0]
    pi, po = len(first.arrays), len(first.out_shapes)

    def body(*refs):
        pin, pout, sems = refs[:pi], refs[pi:pi + po], refs[pi + po:]
        for n, ph in enumerate(phases):
            ph.start(pin, pout, *sems[2 * n:2 * n + 2])
            ph.finish(pin, pout, *sems[2 * n:2 * n + 2])

    anyspace = pl.BlockSpec(memory_space=pl.ANY)
    return list(pl.pallas_call(
        body, name=name, in_specs=[anyspace] * pi, out_specs=[anyspace] * po, out_shape=first.out_shapes,
        input_output_aliases=first.aliases, scratch_shapes=[s for ph in phases for s in ph.sems()],
    )(*first.arrays))


def _half_rows(buf, c):
    half = buf.shape[1] // 2
    return pl.ds(c * half, half), pl.ds((1 - c) * half, half)


def _gather_phase(bufs, over_ici):
    n = len(bufs)
    shapes = [jax.ShapeDtypeStruct(b.shape, b.dtype) for b in bufs]

    def landed(out, which):
        x, y, c = _place()
        _, ks = _other_chips(x, y)
        return [out[a].at[ks[j], _half_rows(bufs[a], c)[which]] for a in range(n) for j in range(3)]

    def ici(pin, out):
        x, y, c = _place()
        chips, _ = _other_chips(x, y)
        mine = [out[a].at[2 * x + y, _half_rows(bufs[a], c)[0]] for a in range(n)]
        return [(mine[a], mine[a], (*chips[j], c)) for a in range(n) for j in range(3)]

    def d2d(pin, out):
        x, y, c = _place()
        return [(dst, dst, (x, y, 1 - c)) for dst in landed(out, 0)]

    if over_ici:
        return _Phase(bufs, shapes, {a: a for a in range(n)}, 3 * n, ici, lambda pin, out: landed(out, 0))
    return _Phase(bufs, shapes, {a: a for a in range(n)}, 3 * n, d2d, lambda pin, out: landed(out, 1))


def _feed_phase(buf):
    def chips():
        x, y, c = _place()
        return [(x if f < 2 else 1 - x, y if f % 2 == 0 else 1 - y) for f in (1, 2, 3)]

    def copies(pin, out):
        x, y, c = _place()
        mine = _half_rows(buf, c)[0]
        own = out[0].at[2 * x + y, mine]
        sent = [(own, own, (cx, cy, c)) for cx, cy in chips()]
        return sent + [(out[0].at[2 * cx + cy, mine], out[0].at[2 * cx + cy, mine], (x, y, 1 - c)) for cx, cy in chips()]

    def arrivals(pin, out):
        x, y, c = _place()
        mine, theirs = _half_rows(buf, c)
        return [out[0].at[2 * cx + cy, rows] for rows in (mine, theirs) for cx, cy in chips()]

    return _Phase([buf], [jax.ShapeDtypeStruct(buf.shape, buf.dtype)], {0: 0}, 6, copies, arrivals,
                  own_starts=(3, 4, 5), own_waits=range(6))


def _rs_sibling(grads, name):
    n = len(grads)

    def body(*refs):
        g, out, send_sems, recv_sems = refs[:n], refs[n:2 * n], refs[2 * n], refs[2 * n + 1]
        x, y, c = _place()
        copies = []
        for a in range(n):
            half = grads[a].shape[1] // 2
            cp = pltpu.make_async_remote_copy(src_ref=g[a].at[:, pl.ds((1 - c) * half, half)], dst_ref=out[a],
                                              send_sem=send_sems.at[a], recv_sem=recv_sems.at[a],
                                              device_id=(x, y, 1 - c), device_id_type=MESH)
            cp.start()
            copies.append(cp)
        for cp in copies:
            cp.wait()

    anyspace = pl.BlockSpec(memory_space=pl.ANY)
    return pl.pallas_call(
        body, name=name, in_specs=[anyspace] * n, out_specs=[anyspace] * n,
        out_shape=[jax.ShapeDtypeStruct((N_CHIPS, g.shape[1] // 2, g.shape[2]), g.dtype) for g in grads],
        scratch_shapes=[pltpu.SemaphoreType.DMA((n,)), pltpu.SemaphoreType.DMA((n,))],
    )(*grads)


def _rs_add_sibling(place, grad, got, name):
    _, R, C = grad.shape
    half = R // 2
    tr = _tile(half, max(16, (1 << 19) // C // 16 * 16), 16)
    nr = half // tr

    def body(p_ref, a_ref, b_ref, o_ref):
        o_ref[...] = (a_ref[...].astype(f32) + b_ref[...].astype(f32)).astype(o_ref.dtype)

    return pl.pallas_call(
        body, name=name,
        grid_spec=pltpu.PrefetchScalarGridSpec(
            num_scalar_prefetch=1, grid=(N_CHIPS, nr),
            in_specs=[pl.BlockSpec((1, tr, C), lambda k, r, p: (k, p[0] * nr + r, 0)),
                      pl.BlockSpec((1, tr, C), lambda k, r, p: (k, r, 0))],
            out_specs=pl.BlockSpec((1, tr, C), lambda k, r, p: (k, r, 0))),
        out_shape=jax.ShapeDtypeStruct((N_CHIPS, half, C), bf16),
        compiler_params=_params(("parallel", "parallel")),
    )(place, grad, got)


def _rs_chips_phase(parts):
    n = len(parts)

    def copies(p, fc):
        x, y, c = _place()
        chips, ks = _other_chips(x, y)
        return [(p[a].at[ks[j]], fc[a].at[j], (*chips[j], c)) for a in range(n) for j in range(3)]

    shapes = [jax.ShapeDtypeStruct((3,) + q.shape[1:], q.dtype) for q in parts]
    return _Phase(parts, shapes, {}, 3 * n, copies, lambda p, fc: [fc[a].at[j] for a in range(n) for j in range(3)])


def _rs_hand_phase(parts, from_chips):
    n = len(parts)

    def copies(pin, fs):
        x, y, c = _place()
        sib = (x, y, 1 - c)
        own = [(pin[a].at[2 * x + y], fs[a].at[0], sib) for a in range(n)]
        return own + [(pin[n + a].at[j], fs[a].at[1 + j], sib) for a in range(n) for j in range(3)]

    def arrivals(pin, fs):
        return [fs[a].at[0] for a in range(n)] + [fs[a].at[1 + j] for a in range(n) for j in range(3)]

    shapes = [jax.ShapeDtypeStruct((4,) + q.shape[1:], q.dtype) for q in parts]
    return _Phase(list(parts) + list(from_chips), shapes, {}, 4 * n, copies, arrivals)


class _Exchange:
    def __init__(self, place):
        self.place = place

    def feed(self, buf):
        return _feed_phase(buf)

    def gather(self, bufs, over_ici):
        return _gather_phase(bufs, over_ici)

    def pair_sums(self, names, grads):
        got = _rs_sibling(grads, "rs_sibling_" + names[0])
        return [_rs_add_sibling(self.place, g, r, "rs_add_" + n) for n, g, r in zip(names, grads, got)]

    def to_chips(self, parts):
        return _rs_chips_phase(parts)

    def to_sibling(self, parts, from_chips):
        return _rs_hand_phase(parts, from_chips)

    def hand_over(self, name, parts, from_chips):
        return _run_phases(name, [_rs_hand_phase(parts, from_chips)])


def _local_step(place, x, target, norm_mix, b_gate, rb_full, norm_ffn, norm_final, w_in, rest, exch):
    B, S, D = x.shape
    T = B * S
    x2 = x.reshape(T, D)
    tg2 = target.reshape(T, D)
    rope, decay = _rope_tables(S), _decay_tables()
    trows = _bias_rows(rb_full)
    g_fin = norm_final.reshape(1, D)

    mrg, ffn = ["w_ret_out", "w_att_out", "w_out"], ["w_ffn_gate", "w_ffn_up", "w_ffn_down"]
    (xn, proj), got = _in_proj(place, x2, norm_mix, _join([exch.feed(w_in), exch.gather([rest[n] for n in mrg], True)]))
    w_in, wb = got[0], {}
    (qr, kr, o, u, states), got = _ret_fwd(proj, B, S, rope, decay, _join([exch.gather([rest["w_ffn_gate"]], True),
                                                                         exch.gather(got[1:], False)]))
    wb.update(zip(mrg, got[1:]))
    (ao,), got = _att_fwd(proj, trows, B, S, _join([exch.gather([rest["w_ffn_up"], rest["w_ffn_down"]], True),
                                                    exch.gather(got[:1], False)]))
    wb["w_ffn_gate"] = got[2]
    w_ro, w_out = wb["w_ret_out"].reshape(-1, D), wb["w_out"].reshape(-1, D)
    (h1, yr, ya), got = _mix_fwd(x2, proj, u, ao, b_gate, w_ro, wb["w_att_out"], w_out, exch.gather(got[:2], False))
    wb.update(zip(ffn[1:], got))
    hn, a, b, f, dh2, part_fin = _ffn_fwd(h1, norm_ffn, wb["w_ffn_gate"], wb["w_ffn_up"], wb["w_ffn_down"], g_fin, tg2)

    da, db, dh1, part_ffn = _ffn_bwd(dh2, h1, norm_ffn, a, b, wb["w_ffn_gate"], wb["w_ffn_up"], wb["w_ffn_down"])
    ffn = ["w_ffn_down", "w_ffn_gate", "w_ffn_up"]
    p_ffn = exch.pair_sums(ffn, [_wgrad(f, dh2, 0, "wgrad_ffn_down")[0], _wgrad(da, hn, 0, "wgrad_ffn_gate")[0],
                                 _wgrad(db, hn, 0, "wgrad_ffn_up")[0]])
    (du, dao, dgl, mix, dyr, dya, part_bg), c_down = _mix_bwd(dh1, proj, yr, ya, b_gate, w_ro, wb["w_att_out"], w_out,
                                                               exch.to_chips(p_ffn[:1]))
    mrg = ["w_out", "w_ret_out", "w_att_out"]
    p_mrg = exch.pair_sums(mrg, [_wgrad(mix, dh1, 0, "wgrad_out")[0], _wgrad(u, dyr, 0, "wgrad_ret_out")[0],
                                 _wgrad(ao, dya, 1, "wgrad_att_out")[0]])
    (dproj,), c_gate_up = _ret_bwd(proj, qr, kr, o, states, du, B, S, rope, decay, exch.to_chips(p_ffn[1:]))
    c_ffn = c_down + c_gate_up
    (dproj, dvec), got = _att_bwd(proj, dao, trows, dproj, B, S, _join([exch.to_chips(p_mrg),
                                                                        exch.to_sibling(p_ffn, c_ffn)]))
    c_mrg, s_ffn = got[:len(mrg)], got[len(mrg):]
    dproj = lax.dynamic_update_slice(dproj, dgl, (0, C_GL))
    g_in, s_mrg = _wgrad(xn, dproj, 1, "wgrad_in", exch.to_sibling(p_mrg, c_mrg))
    p_in = exch.pair_sums(["w_in"], [g_in])
    (gx, part_mix), c_in = _in_proj_bwd(dproj, w_in, x2, norm_mix, dh1, exch.to_chips(p_in))
    s_in = exch.hand_over("rs_hand_w_in", p_in, c_in)
    gbig = dict(zip(ffn + mrg + ["w_in"], zip(p_ffn + p_mrg + p_in, c_ffn + c_mrg + c_in, s_ffn + s_mrg + s_in)))
    rows = lambda p, r: p.reshape(-1, 8, p.shape[-1])[:, r, :].sum(axis=0)
    lo = KWIN - 1 - (MAX_REL - 1)
    drb = jnp.concatenate([jnp.flip(dvec[:, lo:lo + N_REL - 1], axis=1), dvec[:, :lo].sum(axis=1, keepdims=True)], axis=1)
    gsmall = {
        "norm_mix": rows(part_mix, 0), "b_gate": rows(part_bg, 0), "rel_bias": drb, "norm_ffn": rows(part_ffn, 0),
        "norm_final": rows(part_fin, 0),
    }
    return rows(part_fin, 1), gx.reshape(B, S, D), gbig, gsmall


SMALL_ROWS = 16


def _pack_small(gs, loss_lanes):
    D = D_MODEL
    rb = jnp.pad(gs["rel_bias"].reshape(-1), (0, 3 * D - ATT_HEADS * N_REL)).reshape(3, D)
    rows = [gs["norm_mix"].reshape(1, D), gs["b_gate"].reshape(2, D), gs["norm_ffn"].reshape(1, D),
            gs["norm_final"].reshape(1, D), rb, loss_lanes.reshape(1, D)]
    used = sum(r.shape[0] for r in rows)
    return jnp.concatenate(rows + [jnp.zeros((SMALL_ROWS - used, D), f32)], axis=0)


def kernel(x, norm_mix, w_in, b_gate, rel_bias, w_ret_out, w_att_out, w_out, norm_ffn, w_ffn_gate, w_ffn_up, w_ffn_down, norm_final, loss_target, m_norm_mix, m_w_in, m_b_gate, m_rel_bias, m_w_ret_out, m_w_att_out, m_w_out, m_norm_ffn, m_w_ffn_gate, m_w_ffn_up, m_w_ffn_down, m_norm_final, v_norm_mix, v_w_in, v_b_gate, v_rel_bias, v_w_ret_out, v_w_att_out, v_w_out, v_norm_ffn, v_w_ffn_gate, v_w_ffn_up, v_w_ffn_down, v_norm_final):
    w = dict(norm_mix=norm_mix, w_in=w_in, b_gate=b_gate, rel_bias=rel_bias, w_ret_out=w_ret_out, w_att_out=w_att_out,
             w_out=w_out, norm_ffn=norm_ffn, w_ffn_gate=w_ffn_gate, w_ffn_up=w_ffn_up, w_ffn_down=w_ffn_down,
             norm_final=norm_final)
    m = dict(norm_mix=m_norm_mix, w_in=m_w_in, b_gate=m_b_gate, rel_bias=m_rel_bias, w_ret_out=m_w_ret_out,
             w_att_out=m_w_att_out, w_out=m_w_out, norm_ffn=m_norm_ffn, w_ffn_gate=m_w_ffn_gate, w_ffn_up=m_w_ffn_up,
             w_ffn_down=m_w_ffn_down, norm_final=m_norm_final)
    v = dict(norm_mix=v_norm_mix, w_in=v_w_in, b_gate=v_b_gate, rel_bias=v_rel_bias, w_ret_out=v_w_ret_out,
             w_att_out=v_w_att_out, w_out=v_w_out, norm_ffn=v_norm_ffn, w_ffn_gate=v_w_ffn_gate, w_ffn_up=v_w_ffn_up,
             w_ffn_down=v_w_ffn_down, norm_final=v_norm_final)
    xi, yi, ci = _place()
    k_me = 2 * xi + yi

    place = jnp.stack([ci, k_me]).astype(jnp.int32)
    big = [n for n, _ in BIG]

    turned = ("w_ffn_gate", "w_ffn_up")
    shard = lambda d, n: jnp.swapaxes(d[n][0], 0, 1) if n in turned else d[n][0]
    whole = lambda a, n: (jnp.swapaxes(a, 0, 1) if n in turned else a)[None]

    bufs = {n: _cast_shard(place, shard(w, n), "cast_" + n) for n in big}
    rest = {n: bufs[n] for n in big if n != "w_in"}
    nrel_loc = rel_bias.shape[-1]
    rb_all = _exchange_small(jnp.pad(rel_bias[0], ((0, 0), (0, 128 - nrel_loc))), "gather_rel_bias", False)
    rb_full = jnp.concatenate([rb_all[2 * k, :, :nrel_loc] for k in range(N_CHIPS)], axis=1)

    loss_lanes, grad_x, gbig, gsmall = _local_step(place, x, loss_target, norm_mix, b_gate, rb_full, norm_ffn, norm_final,
                                                   bufs["w_in"], rest, _Exchange(place))

    small = _exchange_small(_pack_small(gsmall, loss_lanes), "reduce_small", True)
    D = D_MODEL
    loss = jnp.sum(small[8])
    drb_full = small[5:8].reshape(-1)[:ATT_HEADS * N_REL].reshape(ATT_HEADS, N_REL)
    g = {
        "norm_mix": small[0:1], "b_gate": small[1:3].reshape(1, 2 * D), "norm_ffn": small[3:4], "norm_final": small[4],
        "rel_bias": lax.dynamic_slice_in_dim(drb_full, k_me * nrel_loc, nrel_loc, axis=1)[None],
    }

    delta, new_m, new_v = {}, {}, {}
    for n in big:
        g_, d_, m_, v_ = _adamw_sum(place, shard(w, n), shard(m, n), shard(v, n), *gbig[n], "adamw_" + n)
        g[n], delta[n], new_m[n], new_v[n] = whole(g_, n), whole(d_, n), whole(m_, n), whole(v_, n)
    flat = lambda d: jnp.concatenate([d[n].reshape(-1) for n in SMALL])
    n_small = sum(int(np.prod(w[n].shape)) for n in SMALL)
    n_pad = -n_small % 1024
    packs = [jnp.pad(flat(d), (0, n_pad)).reshape(-1, 128) for d in (w, g, m, v)]
    outs = _adamw(*packs, "adamw_small")
    for res, dst in zip(outs, (delta, new_m, new_v)):
        off = 0
        fl = res.reshape(-1)
        for n in SMALL:
            sz = int(np.prod(w[n].shape))
            dst[n] = fl[off:off + sz].reshape(w[n].shape)
            off += sz

    return (loss, grad_x, *[g[n] for n in WEIGHTS], *[delta[n] for n in WEIGHTS], *[new_m[n] for n in WEIGHTS],
            *[new_v[n] for n in WEIGHTS])
```

```python
import functools

import numpy as np
import jax
import jax.numpy as jnp
from jax import lax
from jax.experimental import pallas as pl
from jax.experimental.pallas import tpu as pltpu

f32 = jnp.float32
bf16 = jnp.bfloat16

D_MODEL = 1024
CHUNK = 64
RET_HEADS = 4
RET_KEY_DIM = 128
RET_VAL_DIM = 256
ATT_HEADS = 8
ATT_HEAD_DIM = 64
ATT_W = ATT_HEADS * ATT_HEAD_DIM
BAND_CHUNKS = 8
PAD = BAND_CHUNKS * CHUNK
MAX_REL = 256
N_REL = CHUNK + MAX_REL
D_FF = 2816
N_IN = 6656
ROPE_BASE = 10000.0
EPS = 1e-6
NEG_INF = -1e30
C_RQ, C_RK, C_RV, C_RG, C_AQ, C_AK, C_AV, C_GL = 0, 512, 1024, 2048, 3072, 3584, 4096, 4608

ADAM_LR, ADAM_B1, ADAM_B2, ADAM_EPS, ADAM_WD, ADAM_STEP = 0.001, 0.9, 0.999, 1e-08, 0.01, 10

N_CHIPS = 4
N_DEV = 8
WGRAD_ACC_BYTES = 8 * 1024 * 1024
ROW_TILE = 512
BIG_ROW_TILE = 1024
IN_ORDER = (0, 2, 3, 1)
QBLK = 256
KWIN = PAD + QBLK
TOEP = 1024
VMEM_LIMIT = 56 * 1024 * 1024
MESH = pl.DeviceIdType.MESH

BIG = (
    ("w_in", 1), ("w_ret_out", 0), ("w_att_out", 1), ("w_out", 0), ("w_ffn_gate", 1), ("w_ffn_up", 1), ("w_ffn_down", 0))
WEIGHTS = ("norm_mix", "w_in", "b_gate", "rel_bias", "w_ret_out", "w_att_out", "w_out", "norm_ffn", "w_ffn_gate",
           "w_ffn_up", "w_ffn_down", "norm_final")
SMALL = ("norm_mix", "b_gate", "rel_bias", "norm_ffn", "norm_final")


def _dot(a, b):
    return lax.dot_general(a, b, (((1,), (0,)), ((), ())), preferred_element_type=f32)


def _dot_nt(a, b):
    return lax.dot_general(a, b, (((1,), (1,)), ((), ())), preferred_element_type=f32)


def _dot_tn(a, b):
    return lax.dot_general(a, b, (((0,), (0,)), ((), ())), preferred_element_type=f32)


def _sig(x):
    return 1.0 / (1.0 + jnp.exp(-x))


def _tile(n, pref, mult):
    best = None
    for t in range(mult, min(n, pref) + 1, mult):
        if n % t == 0:
            best = t
    return best if best is not None else n


def _params(sem, vmem=VMEM_LIMIT):
    return pltpu.CompilerParams(dimension_semantics=sem, vmem_limit_bytes=vmem)


def _in_proj(place, x2, gamma, phase):
    T, D = x2.shape
    _, _, ns = phase.arrays[0].shape
    tm = _tile(T, BIG_ROW_TILE, 8)
    ni = T // tm
    pass_chip = lambda j: sum(jnp.where(j == n, f, 0) for n, f in enumerate(IN_ORDER))

    def body(p_ref, x_ref, g_ref, xn_ref, pr_ref, xs_ref, w_ref, w_sem, carried):
        j, i = pl.program_id(0), pl.program_id(1)
        pin, pout, sems = carried
        rows = pl.ds(pl.multiple_of(i * tm, tm), tm)

        @pl.when(i == 0)
        def _():
            for n, f in enumerate(IN_ORDER):
                if f:
                    @pl.when(j == n)
                    def _():
                        phase.arrived(f - 1, pin, pout, *sems)
                        phase.begin(2 + f, pin, pout, *sems)
                        phase.arrived(2 + f, pin, pout, *sems)
            shard = pltpu.make_async_copy(pout[0].at[jnp.bitwise_xor(p_ref[1], pass_chip(j))], w_ref, w_sem)
            shard.start()
            shard.wait()

        @pl.when(j == 0)
        def _():
            x = x_ref[...]
            r = lax.rsqrt(jnp.mean(x * x, axis=-1, keepdims=True) + EPS)
            xn = (x * r * g_ref[...]).astype(bf16)
            xs_ref[rows, :] = xn
            xn_ref[...] = xn

        pr_ref[...] = _dot(xs_ref[rows, :], w_ref[...]).astype(bf16)

    first_pass = lambda j, i, p: (jnp.where(j == 0, i, ni - 1), 0)
    return _call(
        body, phase, name="in_proj", grid=(N_CHIPS, ni), prefetch=(place,), expose=True,
        in_specs=[pl.BlockSpec((tm, D), first_pass), pl.BlockSpec((1, D), lambda j, i, p: (0, 0))],
        out_specs=[pl.BlockSpec((tm, D), first_pass),
                   pl.BlockSpec((tm, ns), lambda j, i, p: (i, jnp.bitwise_xor(p[1], pass_chip(j))))],
        out_shape=[jax.ShapeDtypeStruct((T, D), bf16), jax.ShapeDtypeStruct((T, N_CHIPS * ns), bf16)],
        scratch_shapes=[pltpu.VMEM((T, D), bf16), pltpu.VMEM((D, ns), bf16), pltpu.SemaphoreType.DMA],
        args=(x2, gamma))


def _rope_tables(S):
    d = RET_KEY_DIM
    freqs = ROPE_BASE ** (-jnp.arange(0, d, 2, dtype=f32) / d)
    ang = jnp.arange(S, dtype=f32)[:, None] * freqs[None, :]
    cos, sin = jnp.cos(ang), jnp.sin(ang)
    return jnp.concatenate([cos, cos], axis=1), jnp.concatenate([-sin, sin], axis=1)


def _decay_tables():
    H = RET_HEADS
    log_g = jnp.log(1.0 - 2.0 ** (-5.0 - jnp.arange(H, dtype=f32)))
    p = jnp.arange(CHUNK, dtype=f32)
    intra = jnp.exp(log_g[:, None, None] * jnp.abs(p[:, None] - p[None, :]))
    q_dec = jnp.exp(log_g[:, None] * (p[None, :] + 1.0))
    k_dec = jnp.exp(log_g[:, None] * (CHUNK - 1.0 - p[None, :]))
    c_dec = jnp.exp(log_g * CHUNK)
    q_dec = jnp.broadcast_to(q_dec[:, :, None], (H, CHUNK, RET_KEY_DIM))
    k_dec = jnp.broadcast_to(k_dec[:, :, None], (H, CHUNK, RET_KEY_DIM))
    c_dec = jnp.broadcast_to(c_dec[:, None, None], (H, 1, RET_VAL_DIM))
    return intra, q_dec, k_dec, c_dec


K_SCALE = RET_KEY_DIM ** -0.5


RET_CHUNKS = 4


def _ret_tables_specs():
    whole = lambda *shape: pl.BlockSpec(shape, lambda b, i: (0,) * len(shape))
    return [whole(RET_HEADS, CHUNK, CHUNK), whole(RET_HEADS, CHUNK, RET_KEY_DIM), whole(RET_HEADS, CHUNK, RET_KEY_DIM),
            whole(RET_HEADS, 1, RET_VAL_DIM)]


def _rotate(x, cos, sn):
    return x * cos + pltpu.roll(x, RET_KEY_DIM // 2, 1) * sn


def _ret_fwd(proj, B, S, rope, decay, phase=None):
    T = B * S
    nc = S // CHUNK
    H, dk, dv = RET_HEADS, RET_KEY_DIM, RET_VAL_DIM
    sb = RET_CHUNKS * CHUNK
    ns = S // sb

    def body(q_ref, k_ref, v_ref, g_ref, cos_ref, sin_ref, intra_ref, qd_ref, kd_ref, cd_ref,
             qr_ref, kr_ref, o_ref, u_ref, st_ref, state_ref):
        @pl.when(pl.program_id(1) == 0)
        def _():
            state_ref[...] = jnp.zeros_like(state_ref)

        cos, sn = cos_ref[...], sin_ref[...]
        for h in range(H):
            hs = slice(h * dk, (h + 1) * dk)
            qr_ref[:, hs] = _rotate(q_ref[:, hs].astype(f32), cos, sn).astype(bf16)
            kr_ref[:, hs] = (_rotate(k_ref[:, hs].astype(f32), cos, sn) * K_SCALE).astype(bf16)
        states = [state_ref[h] for h in range(H)]
        for ci in range(RET_CHUNKS):
            r = slice(ci * CHUNK, (ci + 1) * CHUNK)
            for h in range(H):
                hk, hv = slice(h * dk, (h + 1) * dk), slice(h * dv, (h + 1) * dv)
                qi, ki, vi = qr_ref[r, hk], kr_ref[r, hk], v_ref[r, hv]
                stb = states[h].astype(bf16)
                st_ref[0, h, ci] = stb
                s = (_dot_nt(qi, ki) * intra_ref[h]).astype(bf16)
                o = _dot(s, vi) + _dot((qi.astype(f32) * qd_ref[h]).astype(bf16), stb)
                states[h] = states[h] * cd_ref[h] + _dot_tn((ki.astype(f32) * kd_ref[h]).astype(bf16), vi)
                mu = jnp.mean(o, axis=-1, keepdims=True)
                xc = o - mu
                var = jnp.mean(xc * xc, axis=-1, keepdims=True)
                oh = xc * lax.rsqrt(var + EPS)
                g = g_ref[r, hv].astype(f32)
                o_ref[r, hv] = o.astype(bf16)
                u_ref[r, hv] = (g * _sig(g) * oh).astype(bf16)
        for h in range(H):
            state_ref[h] = states[h]

    blk = lambda w, c: pl.BlockSpec((sb, w), lambda b, i: (b * ns + i, c))
    return _call(
        body, phase, name="ret_fwd", grid=(B, ns), scratch_shapes=[pltpu.VMEM((H, dk, dv), f32)],
        in_specs=[blk(H * dk, C_RQ // (H * dk)), blk(H * dk, C_RK // (H * dk)), blk(H * dv, C_RV // (H * dv)),
                  blk(H * dv, C_RG // (H * dv)),
                  pl.BlockSpec((sb, dk), lambda b, i: (i, 0)), pl.BlockSpec((sb, dk), lambda b, i: (i, 0)),
                  *_ret_tables_specs()],
        out_specs=[blk(H * dk, 0), blk(H * dk, 0), blk(H * dv, 0), blk(H * dv, 0),
                   pl.BlockSpec((1, H, RET_CHUNKS, dk, dv), lambda b, i: (b, 0, i, 0, 0))],
        out_shape=[jax.ShapeDtypeStruct((T, H * dk), bf16), jax.ShapeDtypeStruct((T, H * dk), bf16),
                   jax.ShapeDtypeStruct((T, H * dv), bf16), jax.ShapeDtypeStruct((T, H * dv), bf16),
                   jax.ShapeDtypeStruct((B, H, nc, dk, dv), bf16)],
        args=(proj, proj, proj, proj, *rope, *decay))


def _bias_rows(rb):
    last = rb[:, N_REL - 1:]
    return jnp.concatenate([
        jnp.broadcast_to(last, (ATT_HEADS, PAD - MAX_REL + 1)),
        jnp.flip(rb[:, :N_REL - 1], axis=1),
        jnp.broadcast_to(rb[:, :1], (ATT_HEADS, KWIN - PAD - CHUNK)),
        jnp.broadcast_to(last, (ATT_HEADS, TOEP - KWIN)),
    ], axis=1)


def _build_bias(t_ref, bias_ref):
    row = lax.broadcasted_iota(jnp.int32, (QBLK, KWIN), 0) // CHUNK
    col = lax.broadcasted_iota(jnp.int32, (QBLK, KWIN), 1) // CHUNK
    delta = BAND_CHUNKS + row - col
    vis = (delta >= 0) & (delta <= BAND_CHUNKS)
    for h in range(ATT_HEADS):
        t = jnp.broadcast_to(t_ref[h:h + 1, :], (QBLK, TOEP))
        rolled = pltpu.roll(t, 0, 1, stride=1, stride_axis=0)
        bias_ref[h] = jnp.where(vis, rolled[:, :KWIN], NEG_INF)


ATT_SCALE = ATT_HEAD_DIM ** -0.5


def _att_probs(qh, kh, bias):
    s = _dot_nt(qh, kh) + bias
    m = jnp.max(s, axis=-1, keepdims=True)
    p = jnp.exp(s - m)
    return p * (1.0 / jnp.sum(p, axis=-1, keepdims=True))


def _first_of_pair():
    return lax.broadcasted_iota(jnp.int32, (1, 2 * ATT_HEAD_DIM), 1) < ATT_HEAD_DIM


def _by_window(i, step):
    sizes = list(range(QBLK, KWIN, QBLK))
    for n, nk in enumerate(sizes):
        pl.when(i == n)(functools.partial(step, nk))
    pl.when(i >= len(sizes))(functools.partial(step, KWIN))


def _att_fwd(proj, trows, B, S, phase=None):
    T = B * S
    nq = S // QBLK
    dh = ATT_HEAD_DIM

    def body(q_ref, k_ref, v_ref, t_ref, o_ref, bias_ref):
        i = pl.program_id(1)

        @pl.when((pl.program_id(0) == 0) & (i == 0))
        def _():
            _build_bias(t_ref, bias_ref)

        def step(nk):
            win = pl.ds(pl.multiple_of((i + 1) * QBLK - nk, QBLK), nk)
            kw, vw = k_ref[win, :], v_ref[win, :]
            first = _first_of_pair()
            outs = []
            for p in range(ATT_HEADS // 2):
                ps = slice(2 * p * dh, 2 * (p + 1) * dh)
                q2, k2, v2 = q_ref[:, ps] * ATT_SCALE, kw[:, ps], vw[:, ps]
                both = []
                for e in range(2):
                    qm = jnp.where(first == (e == 0), q2, jnp.zeros_like(q2))
                    pr = _att_probs(qm, k2, bias_ref[2 * p + e, :, KWIN - nk:])
                    both.append(_dot(pr.astype(bf16), v2))
                outs.append(jnp.where(first, both[0], both[1]))
            o_ref[...] = jnp.concatenate(outs, axis=1).astype(bf16)

        _by_window(i, step)

    return _call(
        body, phase, name="att_fwd", grid=(B, nq),
        in_specs=[pl.BlockSpec((QBLK, ATT_W), lambda b, i: (b * nq + i, C_AQ // ATT_W)),
                  pl.BlockSpec((S, ATT_W), lambda b, i: (b, C_AK // ATT_W)),
                  pl.BlockSpec((S, ATT_W), lambda b, i: (b, C_AV // ATT_W)),
                  pl.BlockSpec((ATT_HEADS, TOEP), lambda b, i: (0, 0))],
        out_specs=[pl.BlockSpec((QBLK, ATT_W), lambda b, i: (b * nq + i, 0))],
        out_shape=[jax.ShapeDtypeStruct((T, ATT_W), bf16)],
        scratch_shapes=[pltpu.VMEM((ATT_HEADS, QBLK, KWIN), f32)],
        args=(proj, proj, proj, trows))


def _gl_specs(tm):
    w = 512
    return [pl.BlockSpec((tm, w), functools.partial(lambda i, j: (i, C_GL // 512 + j), j=j)) for j in range(4)]


def _gates(gl_refs, bg_ref):
    gl = jnp.concatenate([r[...] for r in gl_refs], axis=1).astype(f32) + bg_ref[...]
    g = _sig(gl)
    return g[:, :D_MODEL], g[:, D_MODEL:]


def _mix_fwd(x2, proj, u, ao, b_gate, w_ro, w_ao, w_out, phase=None):
    T, D = x2.shape
    tm = _tile(T, ROW_TILE, 8)

    def body(x_ref, u_ref, ao_ref, g0, g1, g2, g3, bg_ref, wro_ref, wao_ref, wo_ref, h1_ref, yr_ref, ya_ref):
        yr = _dot(u_ref[...], wro_ref[...])
        ao = ao_ref[...]
        ya = jnp.concatenate([_dot(ao, wao_ref[k]) for k in range(N_CHIPS)], axis=1)
        gr, ga = _gates((g0, g1, g2, g3), bg_ref)
        mix = gr * yr + ga * ya
        h1_ref[...] = x_ref[...] + _dot(mix.astype(bf16), wo_ref[...])
        yr_ref[...] = yr.astype(bf16)
        ya_ref[...] = ya.astype(bf16)

    full = lambda a: pl.BlockSpec(a.shape, lambda i: (0,) * a.ndim)
    row = lambda n: pl.BlockSpec((tm, n), lambda i: (i, 0))
    return _call(
        body, phase, name="mix_fwd", grid=(T // tm,), scratch_shapes=[],
        in_specs=[row(D), row(D), row(ATT_W), *_gl_specs(tm), full(b_gate), full(w_ro), full(w_ao), full(w_out)],
        out_specs=[row(D), row(D), row(D)],
        out_shape=[jax.ShapeDtypeStruct((T, D), f32), jax.ShapeDtypeStruct((T, D), bf16),
                   jax.ShapeDtypeStruct((T, D), bf16)],
        args=(x2, u, ao, proj, proj, proj, proj, b_gate, w_ro, w_ao, w_out))


def _ffn_fwd(h1, g_ffn, wg, wu, wd, g_fin, target):
    T, D = h1.shape
    nf, tf, _ = wg.shape
    tm = _tile(T, ROW_TILE, 8)

    def body(h1_ref, g_ref, wg_ref, wu_ref, wd_ref, gf_ref, tg_ref, hn_ref, a_ref, b_ref, f_ref, dh2_ref, part_ref):
        h1v = h1_ref[...]
        r = lax.rsqrt(jnp.mean(h1v * h1v, axis=-1, keepdims=True) + EPS)
        hn = (h1v * r * g_ref[...]).astype(bf16)
        hn_ref[...] = hn
        h2 = h1v
        for k in range(nf):
            a = _dot_nt(hn, wg_ref[k])
            b = _dot_nt(hn, wu_ref[k])
            f = ((a * _sig(a)) * b).astype(bf16)
            a_ref[k] = a.astype(bf16)
            b_ref[k] = b.astype(bf16)
            f_ref[k] = f
            h2 = h2 + _dot(f, wd_ref[k])
        r = lax.rsqrt(jnp.mean(h2 * h2, axis=-1, keepdims=True) + EPS)
        n = h2 * r
        gf = gf_ref[...]
        e = n * gf - tg_ref[...]
        dy = e * (1.0 / D)
        dn = dy * gf
        dh2_ref[...] = r * (dn - n * jnp.mean(dn * n, axis=-1, keepdims=True))
        part_ref[...] = jnp.zeros_like(part_ref)
        part_ref[0:1, :] = jnp.sum(dy * n, axis=0, keepdims=True)
        part_ref[1:2, :] = (0.5 / D) * jnp.sum(e * e, axis=0, keepdims=True)

    row = lambda n: pl.BlockSpec((tm, n), lambda i: (i, 0))
    vec = pl.BlockSpec((1, D), lambda i: (0, 0))
    col = pl.BlockSpec((nf, tm, tf), lambda i: (0, i, 0))
    held = lambda w: pl.BlockSpec(w.shape, lambda i: (0, 0, 0), pipeline_mode=pl.Buffered(1))
    act = jax.ShapeDtypeStruct((nf, T, tf), bf16)
    return pl.pallas_call(
        body, name="ffn_fwd", grid=(T // tm,),
        in_specs=[row(D), vec, held(wg), held(wu), held(wd), vec, row(D)],
        out_specs=[row(D), col, col, col, row(D), pl.BlockSpec((8, D), lambda i: (i, 0))],
        out_shape=[jax.ShapeDtypeStruct((T, D), bf16), act, act, act,
                   jax.ShapeDtypeStruct((T, D), f32), jax.ShapeDtypeStruct((T // tm * 8, D), f32)],
        compiler_params=_params(("parallel",)),
    )(h1, g_ffn, wg, wu, wd, g_fin, target)


def _ffn_bwd(dh2, h1, g_ffn, a, b, wg, wu, wd):
    T, D = h1.shape
    nf, tf, _ = wg.shape
    tm = _tile(T, ROW_TILE // 2, 8)

    def body(dh2_ref, h1_ref, g_ref, a_ref, b_ref, wg_ref, wu_ref, wd_ref, da_ref, db_ref, dh1_ref, part_ref):
        dh2v = dh2_ref[...]
        dh2b = dh2v.astype(bf16)
        dhn = jnp.zeros((tm, D), f32)
        for k in range(nf):
            df = _dot_nt(dh2b, wd_ref[k])
            av = a_ref[k].astype(f32)
            sg = _sig(av)
            db = (df * (av * sg)).astype(bf16)
            da = (df * b_ref[k].astype(f32) * (sg * (1.0 + av * (1.0 - sg)))).astype(bf16)
            da_ref[k] = da
            db_ref[k] = db
            dhn = dhn + _dot(da, wg_ref[k]) + _dot(db, wu_ref[k])
        h = h1_ref[...]
        r = lax.rsqrt(jnp.mean(h * h, axis=-1, keepdims=True) + EPS)
        n = h * r
        dn = dhn * g_ref[...]
        dh1_ref[...] = dh2v + r * (dn - n * jnp.mean(dn * n, axis=-1, keepdims=True))
        part_ref[...] = jnp.zeros_like(part_ref)
        part_ref[0:1, :] = jnp.sum(dhn * n, axis=0, keepdims=True)

    row = lambda n: pl.BlockSpec((tm, n), lambda i: (i, 0))
    col = pl.BlockSpec((nf, tm, tf), lambda i: (0, i, 0))
    held = lambda w: pl.BlockSpec(w.shape, lambda i: (0, 0, 0), pipeline_mode=pl.Buffered(1))
    act = jax.ShapeDtypeStruct((nf, T, tf), bf16)
    return pl.pallas_call(
        body, name="ffn_bwd", grid=(T // tm,),
        in_specs=[row(D), row(D), pl.BlockSpec((1, D), lambda i: (0, 0)), col, col, held(wg), held(wu), held(wd)],
        out_specs=[col, col, row(D), pl.BlockSpec((8, D), lambda i: (i, 0))],
        out_shape=[act, act, jax.ShapeDtypeStruct((T, D), f32), jax.ShapeDtypeStruct((T // tm * 8, D), f32)],
        compiler_params=_params(("parallel",)),
    )(dh2, h1, g_ffn, a, b, wg, wu, wd)


def _mix_bwd(dh1, proj, yr, ya, b_gate, w_ro, w_ao, w_out, phase=None):
    T, D = dh1.shape
    tm = _tile(T, ROW_TILE, 8)

    def body(dh1_ref, g0, g1, g2, g3, bg_ref, yr_ref, ya_ref, wro_ref, wao_ref, wo_ref,
             du_ref, dao_ref, dgl_ref, mix_ref, dyr_ref, dya_ref, part_ref):
        dmix = _dot_nt(dh1_ref[...].astype(bf16), wo_ref[...])
        gr, ga = _gates((g0, g1, g2, g3), bg_ref)
        yr = yr_ref[...].astype(f32)
        ya = ya_ref[...].astype(f32)
        dyr = (dmix * gr).astype(bf16)
        dya = (dmix * ga).astype(bf16)
        dgl = jnp.concatenate([dmix * yr * gr * (1.0 - gr), dmix * ya * ga * (1.0 - ga)], axis=1)
        du_ref[...] = _dot_nt(dyr, wro_ref[...]).astype(bf16)
        ns = wao_ref.shape[2]
        dao = _dot_nt(dya[:, :ns], wao_ref[0])
        for k in range(1, N_CHIPS):
            dao = dao + _dot_nt(dya[:, k * ns:(k + 1) * ns], wao_ref[k])
        dao_ref[...] = dao.astype(bf16)
        dgl_ref[...] = dgl.astype(bf16)
        mix_ref[...] = (gr * yr + ga * ya).astype(bf16)
        dyr_ref[...] = dyr
        dya_ref[...] = dya
        part_ref[...] = jnp.zeros_like(part_ref)
        part_ref[0:1, :] = jnp.sum(dgl, axis=0, keepdims=True)

    full = lambda a: pl.BlockSpec(a.shape, lambda i: (0,) * a.ndim)
    row = lambda n: pl.BlockSpec((tm, n), lambda i: (i, 0))
    return _call(
        body, phase, name="mix_bwd", grid=(T // tm,), scratch_shapes=[],
        in_specs=[row(D), *_gl_specs(tm), full(b_gate), row(D), row(D), full(w_ro), full(w_ao), full(w_out)],
        out_specs=[row(D), row(ATT_W), row(2 * D), row(D), row(D), row(D), pl.BlockSpec((8, 2 * D), lambda i: (i, 0))],
        out_shape=[jax.ShapeDtypeStruct((T, D), bf16), jax.ShapeDtypeStruct((T, ATT_W), bf16),
                   jax.ShapeDtypeStruct((T, 2 * D), bf16), jax.ShapeDtypeStruct((T, D), bf16),
                   jax.ShapeDtypeStruct((T, D), bf16), jax.ShapeDtypeStruct((T, D), bf16),
                   jax.ShapeDtypeStruct((T // tm * 8, 2 * D), f32)],
        args=(dh1, proj, proj, proj, proj, b_gate, yr, ya, w_ro, w_ao, w_out))


def _ret_bwd(proj, qr, kr, o, states, du, B, S, rope, decay, phase=None):
    T = B * S
    nc = S // CHUNK
    H, dk, dv = RET_HEADS, RET_KEY_DIM, RET_VAL_DIM

    sb = RET_CHUNKS * CHUNK
    ns = S // sb

    def body(qr_ref, kr_ref, v_ref, g_ref, o_ref, st_ref, du_ref, cos_ref, sin_ref, intra_ref, qd_ref, kd_ref, cd_ref,
             dp_ref, dstate_ref):
        dq_ref, dk_ref = dp_ref.at[:, pl.ds(C_RQ, H * dk)], dp_ref.at[:, pl.ds(C_RK, H * dk)]
        dv_ref, dg_ref = dp_ref.at[:, pl.ds(C_RV, H * dv)], dp_ref.at[:, pl.ds(C_RG, H * dv)]

        @pl.when(pl.program_id(1) == 0)
        def _():
            dstate_ref[...] = jnp.zeros_like(dstate_ref)

        cos, snb = cos_ref[...], -sin_ref[...]
        dstates = [dstate_ref[h] for h in range(H)]
        for ci in reversed(range(RET_CHUNKS)):
            r = slice(ci * CHUNK, (ci + 1) * CHUNK)
            for h in range(H):
                hk, hv = slice(h * dk, (h + 1) * dk), slice(h * dv, (h + 1) * dv)
                intra, qd, kd = intra_ref[h], qd_ref[h], kd_ref[h]
                qi, ki, vi = qr_ref[r, hk], kr_ref[r, hk], v_ref[r, hv]
                si = st_ref[0, h, ci]
                o = o_ref[r, hv].astype(f32)
                mu = jnp.mean(o, axis=-1, keepdims=True)
                xc = o - mu
                rstd = lax.rsqrt(jnp.mean(xc * xc, axis=-1, keepdims=True) + EPS)
                oh = xc * rstd
                g = g_ref[r, hv].astype(f32)
                sg = _sig(g)
                dui = du_ref[r, hv].astype(f32)
                dg_ref[r, hv] = (dui * oh * (sg * (1.0 + g * (1.0 - sg)))).astype(bf16)
                doh = dui * (g * sg)
                do = rstd * (doh - jnp.mean(doh, axis=-1, keepdims=True)
                             - oh * jnp.mean(doh * oh, axis=-1, keepdims=True))
                dob = do.astype(bf16)
                p = (_dot_nt(qi, ki) * intra).astype(bf16)
                dsb = dstates[h].astype(bf16)
                kt = (ki.astype(f32) * kd).astype(bf16)
                qt = (qi.astype(f32) * qd).astype(bf16)
                dv_ref[r, hv] = (_dot_tn(p, dob) + _dot(kt, dsb)).astype(bf16)
                da = (_dot_nt(dob, vi) * intra).astype(bf16)
                dq = _dot(da, ki) + _dot_nt(dob, si) * qd
                dkk = (_dot_tn(da, qi) + _dot_nt(vi, dsb) * kd) * K_SCALE
                dq_ref[r, hk] = _rotate(dq, cos[r], snb[r]).astype(bf16)
                dk_ref[r, hk] = _rotate(dkk, cos[r], snb[r]).astype(bf16)
                dstates[h] = dstates[h] * cd_ref[h] + _dot_tn(qt, dob)
        for h in range(H):
            dstate_ref[h] = dstates[h]

    blk = lambda w, c: pl.BlockSpec((sb, w), lambda b, i: (b * ns + ns - 1 - i, c))
    return _call(
        body, phase, name="ret_bwd", grid=(B, ns),
        in_specs=[blk(H * dk, 0), blk(H * dk, 0), blk(H * dv, C_RV // (H * dv)), blk(H * dv, C_RG // (H * dv)),
                  blk(H * dv, 0),
                  pl.BlockSpec((1, H, RET_CHUNKS, dk, dv), lambda b, i: (b, 0, ns - 1 - i, 0, 0)),
                  blk(H * dv, 0),
                  pl.BlockSpec((sb, dk), lambda b, i: (ns - 1 - i, 0)), pl.BlockSpec((sb, dk), lambda b, i: (ns - 1 - i, 0)),
                  *_ret_tables_specs()],
        out_specs=[blk(C_AQ, 0)], out_shape=[jax.ShapeDtypeStruct((T, N_IN), bf16)],
        scratch_shapes=[pltpu.VMEM((H, dk, dv), f32)],
        args=(qr, kr, proj, proj, o, states, du, *rope, *decay))


def _att_bwd(proj, dao, trows, dproj, B, S, phase=None):
    T = B * S
    nq = S // QBLK
    dh = ATT_HEAD_DIM
    scale = ATT_HEAD_DIM ** -0.5

    def body(q_ref, k_ref, v_ref, do_ref, t_ref, _, dp_ref, vec_ref, bias_ref, dbias_ref, dka_ref, dva_ref):
        b, i = pl.program_id(0), pl.program_id(1)

        @pl.when((b == 0) & (i == 0))
        def _():
            _build_bias(t_ref, bias_ref)
            dbias_ref[...] = jnp.zeros_like(dbias_ref)

        @pl.when(i == 0)
        def _():
            dka_ref[...] = jnp.zeros_like(dka_ref)
            dva_ref[...] = jnp.zeros_like(dva_ref)

        def step(nk):
            win = pl.ds(pl.multiple_of((i + 1) * QBLK - nk, QBLK), nk)
            kw, vw = k_ref[win, :], v_ref[win, :]
            first = _first_of_pair()
            first_rows = lax.broadcasted_iota(jnp.int32, (2 * dh, 1), 0) < dh
            dqs, dks, dvs = [], [], []
            for p in range(ATT_HEADS // 2):
                ps = slice(2 * p * dh, 2 * (p + 1) * dh)
                q2, k2, v2, do2 = q_ref[:, ps] * ATT_SCALE, kw[:, ps], vw[:, ps], do_ref[:, ps]
                dq2, dk2, dv2 = [], [], []
                for e in range(2):
                    h = 2 * p + e
                    mine = first == (e == 0)
                    pr = _att_probs(jnp.where(mine, q2, jnp.zeros_like(q2)), k2, bias_ref[h, :, KWIN - nk:])
                    dp = _dot_nt(jnp.where(mine, do2, jnp.zeros_like(do2)), v2)
                    ds = pr * (dp - jnp.sum(pr * dp, axis=-1, keepdims=True))
                    dbias_ref[h, :, KWIN - nk:] += ds
                    dsb = ds.astype(bf16)
                    dq2.append(_dot(dsb, k2) * ATT_SCALE)
                    dk2.append(_dot_tn(q2, dsb))
                    dv2.append(_dot_tn(do2, pr.astype(bf16)))
                dqs.append(jnp.where(first, dq2[0], dq2[1]))
                dks.append(jnp.where(first_rows, dk2[0], dk2[1]))
                dvs.append(jnp.where(first_rows, dv2[0], dv2[1]))
            dp_ref[pl.ds(pl.multiple_of(i * QBLK, QBLK), QBLK), :ATT_W] = jnp.concatenate(dqs, axis=1).astype(bf16)
            dka_ref[:, win] += jnp.concatenate(dks, axis=0)
            dva_ref[:, win] += jnp.concatenate(dvs, axis=0)

        _by_window(i, step)

        @pl.when(i == nq - 1)
        def _():
            dp_ref[:, ATT_W:2 * ATT_W] = dka_ref[...].T.astype(bf16)
            dp_ref[:, 2 * ATT_W:] = dva_ref[...].T.astype(bf16)

        @pl.when((b == B - 1) & (i == nq - 1))
        def _():
            rr = lax.broadcasted_iota(jnp.int32, (QBLK, QBLK), 0)
            cc = lax.broadcasted_iota(jnp.int32, (QBLK, QBLK), 1)
            flip = jnp.where(rr + cc == QBLK - 1, 1.0, 0.0).astype(bf16)
            for h in range(ATT_HEADS):
                d = dbias_ref[h]
                hi = d.astype(bf16)
                lo = (d - hi.astype(f32)).astype(bf16)
                rev = _dot(flip, hi) + _dot(flip, lo)
                wide = jnp.concatenate([rev, jnp.zeros((QBLK, TOEP - KWIN), f32)], axis=1)
                rolled = pltpu.roll(wide, 0, 1, stride=1, stride_axis=0)
                vec_ref[h:h + 1, :] = jnp.sum(rolled, axis=0, keepdims=True)

    qspec = lambda c: pl.BlockSpec((QBLK, ATT_W), lambda b, i: (b * nq + i, c))
    kspec = lambda c: pl.BlockSpec((S, ATT_W), lambda b, i: (b, c))
    return _call(
        body, phase, name="att_bwd", grid=(B, nq), aliases={5: 0},
        in_specs=[qspec(C_AQ // ATT_W), kspec(C_AK // ATT_W), kspec(C_AV // ATT_W), qspec(0),
                  pl.BlockSpec((ATT_HEADS, TOEP), lambda b, i: (0, 0)), pl.BlockSpec(memory_space=pl.ANY)],
        out_specs=[pl.BlockSpec((S, 3 * ATT_W), lambda b, i: (b, C_AQ // (3 * ATT_W))),
                   pl.BlockSpec((ATT_HEADS, TOEP), lambda b, i: (0, 0))],
        out_shape=[jax.ShapeDtypeStruct((T, N_IN), bf16), jax.ShapeDtypeStruct((ATT_HEADS, TOEP), f32)],
        scratch_shapes=[pltpu.VMEM((ATT_HEADS, QBLK, KWIN), f32), pltpu.VMEM((ATT_HEADS, QBLK, KWIN), f32),
                        pltpu.VMEM((ATT_W, S), f32), pltpu.VMEM((ATT_W, S), f32)],
        args=(proj, proj, proj, dao, trows, dproj))


def _in_proj_bwd(dproj, w_in, x2, gamma, dh1, phase=None):
    T, D = x2.shape
    nk, _, tk = w_in.shape
    tm = _tile(T, BIG_ROW_TILE, 8)

    def body(dp_ref, w_ref, x_ref, g_ref, dh1_ref, dx_ref, part_ref, acc_ref):
        j = pl.program_id(1)

        @pl.when(j == 0)
        def _():
            acc_ref[...] = jnp.zeros_like(acc_ref)

        acc_ref[...] += _dot_nt(dp_ref[...], w_ref[0])

        @pl.when(j == nk - 1)
        def _():
            x = x_ref[...]
            r = lax.rsqrt(jnp.mean(x * x, axis=-1, keepdims=True) + EPS)
            n = x * r
            dxn = acc_ref[...]
            dn = dxn * g_ref[...]
            dx_ref[...] = dh1_ref[...] + r * (dn - n * jnp.mean(dn * n, axis=-1, keepdims=True))
            part_ref[...] = jnp.zeros_like(part_ref)
            part_ref[0:1, :] = jnp.sum(dxn * n, axis=0, keepdims=True)

    row = lambda n: pl.BlockSpec((tm, n), lambda i, j: (i, 0))
    return _call(
        body, phase, name="in_proj_bwd", grid=(T // tm, nk),
        in_specs=[pl.BlockSpec((tm, tk), lambda i, j: (i, j)), pl.BlockSpec((1, D, tk), lambda i, j: (j, 0, 0)), row(D),
                  pl.BlockSpec((1, D), lambda i, j: (0, 0)), row(D)],
        out_specs=[row(D), pl.BlockSpec((8, D), lambda i, j: (i, 0))],
        out_shape=[jax.ShapeDtypeStruct((T, D), f32), jax.ShapeDtypeStruct((T // tm * 8, D), f32)],
        scratch_shapes=[pltpu.VMEM((tm, D), f32)],
        args=(dproj, w_in, x2, gamma, dh1))


def _wgrad(a, b, shard_axis, name, phase=None):
    def spec(arr, sharded, tt):
        if arr.ndim == 3:
            return arr.shape[2], pl.BlockSpec((1, tt, arr.shape[2]), lambda s, t: (s, t, 0))
        if sharded:
            w = arr.shape[1] // N_CHIPS
            return w, pl.BlockSpec((tt, w), lambda s, t: (t, s))
        return arr.shape[1], pl.BlockSpec((tt, arr.shape[1]), lambda s, t: (t, 0))

    T = a.shape[-2]
    tt = _tile(T, BIG_ROW_TILE, 16)
    nt = T // tt
    whole = a.ndim == 2 and b.ndim == 2 and a.shape[1] * b.shape[1] * 4 <= WGRAD_ACC_BYTES
    if whole:
        K, N = a.shape[1], b.shape[1]
        a_spec, b_spec = pl.BlockSpec((tt, K), lambda s, t: (t, 0)), pl.BlockSpec((tt, N), lambda s, t: (t, 0))
        out_block = (N_CHIPS, K // N_CHIPS, N) if shard_axis == 0 else (N_CHIPS, K, N // N_CHIPS)
        out_spec = pl.BlockSpec(out_block, lambda s, t: (0, 0, 0))
    else:
        K, a_spec = spec(a, shard_axis == 0, tt)
        N, b_spec = spec(b, shard_axis == 1, tt)
        out_block = (N_CHIPS, K, N)
        out_spec = pl.BlockSpec((1, K, N), lambda s, t: (s, 0, 0))

    def body(a_ref, b_ref, o_ref, acc_ref):
        t = pl.program_id(1)

        @pl.when(t == 0)
        def _():
            acc_ref[...] = jnp.zeros_like(acc_ref)

        av = a_ref[0] if a.ndim == 3 else a_ref[...]
        bv = b_ref[0] if b.ndim == 3 else b_ref[...]
        acc_ref[...] += _dot_tn(av.astype(bf16), bv.astype(bf16))

        @pl.when(t == nt - 1)
        def _():
            if not whole:
                o_ref[0] = acc_ref[...].astype(bf16)
            else:
                _, kk, nn = out_block
                for s in range(N_CHIPS):
                    o_ref[s] = (acc_ref[s * kk:(s + 1) * kk, :] if shard_axis == 0
                                else acc_ref[:, s * nn:(s + 1) * nn]).astype(bf16)

    (grad,), carried = _call(
        body, phase, name=name, grid=(1 if whole else N_CHIPS, nt), in_specs=[a_spec, b_spec], out_specs=[out_spec],
        out_shape=[jax.ShapeDtypeStruct(out_block, bf16)], scratch_shapes=[pltpu.VMEM((K, N), f32)], args=(a, b))
    return grad, carried


def _adamw_sum(place, w, m, v, part, from_chips, from_sibling, name):
    R, C = w.shape
    half = R // 2
    tr = _tile(half, max(16, (1 << 18) // C // 16 * 16), 16)
    nr = half // tr

    def body(p_ref, w_ref, m_ref, v_ref, part_ref, fc_ref, fs_ref, g_ref, d_ref, mo_ref, vo_ref):
        up = lambda x: x.astype(f32)
        mine = ((up(part_ref[0]) + up(fc_ref[0])) + up(fc_ref[1])) + up(fc_ref[2])
        sibs = ((up(fs_ref[0]) + up(fs_ref[1])) + up(fs_ref[2])) + up(fs_ref[3])
        g_ = jnp.where(pl.program_id(0) == p_ref[0], mine, sibs)
        m_ = ADAM_B1 * m_ref[...] + (1.0 - ADAM_B1) * g_
        v_ = ADAM_B2 * v_ref[...] + (1.0 - ADAM_B2) * (g_ * g_)
        m_hat = m_ / (1.0 - ADAM_B1 ** ADAM_STEP)
        v_hat = v_ / (1.0 - ADAM_B2 ** ADAM_STEP)
        g_ref[...] = g_
        d_ref[...] = -ADAM_LR * (m_hat / (jnp.sqrt(v_hat) + ADAM_EPS) + ADAM_WD * w_ref[...])
        mo_ref[...] = m_
        vo_ref[...] = v_

    spec = pl.BlockSpec((tr, C), lambda h, r, p: (h * nr + r, 0))
    return pl.pallas_call(
        body, name=name,
        grid_spec=pltpu.PrefetchScalarGridSpec(
            num_scalar_prefetch=1, grid=(2, nr),
            in_specs=[spec, spec, spec, pl.BlockSpec((1, tr, C), lambda h, r, p: (p[1], jnp.where(h == p[0], r, 0), 0)),
                      pl.BlockSpec((3, tr, C), lambda h, r, p: (0, jnp.where(h == p[0], r, 0), 0)),
                      pl.BlockSpec((4, tr, C), lambda h, r, p: (0, jnp.where(h == p[0], 0, r), 0))],
            out_specs=[spec] * 4),
        out_shape=[jax.ShapeDtypeStruct((R, C), f32)] * 4,
        compiler_params=_params(("parallel", "parallel")),
    )(place, w, m, v, part, from_chips, from_sibling)


def _adamw(w, g, m, v, name):
    R, C = w.shape
    tr = _tile(R, max(8, (1 << 18) // C // 8 * 8), 8)

    def body(w_ref, g_ref, m_ref, v_ref, d_ref, mo_ref, vo_ref):
        g_ = g_ref[...]
        m_ = ADAM_B1 * m_ref[...] + (1.0 - ADAM_B1) * g_
        v_ = ADAM_B2 * v_ref[...] + (1.0 - ADAM_B2) * (g_ * g_)
        m_hat = m_ / (1.0 - ADAM_B1 ** ADAM_STEP)
        v_hat = v_ / (1.0 - ADAM_B2 ** ADAM_STEP)
        d_ref[...] = -ADAM_LR * (m_hat / (jnp.sqrt(v_hat) + ADAM_EPS) + ADAM_WD * w_ref[...])
        mo_ref[...] = m_
        vo_ref[...] = v_

    spec = pl.BlockSpec((tr, C), lambda i: (i, 0))
    return pl.pallas_call(
        body, name=name, grid=(R // tr,), in_specs=[spec] * 4, out_specs=[spec] * 3,
        out_shape=[jax.ShapeDtypeStruct((R, C), f32)] * 3,
        compiler_params=_params(("parallel",)),
    )(w, g, m, v)


def _place():
    return lax.axis_index("x"), lax.axis_index("y"), lax.axis_index("c")


def _other_chips(x, y):
    chips = [(1 - x, y), (x, 1 - y), (1 - x, 1 - y)]
    return chips, [2 * cx + cy for cx, cy in chips]


def _exchange_small(blk, name, reduce):
    R, C = blk.shape

    def body(x_ref, out_ref, *rest):
        if reduce:
            all_ref, send_sems, recv_sems = rest
        else:
            all_ref = out_ref
            send_sems, recv_sems = rest
        x, y, c = _place()
        me = 4 * x + 2 * y + c
        all_ref[me] = x_ref[...]
        copies = []
        for k in range(1, N_DEV):
            peer = tuple(1 - p if (k >> s) & 1 else p for p, s in ((x, 2), (y, 1), (c, 0)))
            cp = pltpu.make_async_remote_copy(src_ref=x_ref, dst_ref=all_ref.at[me], send_sem=send_sems.at[k - 1],
                                              recv_sem=recv_sems.at[k - 1], device_id=peer, device_id_type=MESH)
            cp.start()
            copies.append(cp)
        for cp in copies:
            cp.wait()
        if reduce:
            tot = all_ref[0]
            for d in range(1, N_DEV):
                tot = tot + all_ref[d]
            out_ref[...] = tot

    vm = pl.BlockSpec(memory_space=pltpu.VMEM)
    scratch = [pltpu.SemaphoreType.DMA((N_DEV - 1,)), pltpu.SemaphoreType.DMA((N_DEV - 1,))]
    if reduce:
        scratch = [pltpu.VMEM((N_DEV, R, C), f32)] + scratch
    return pl.pallas_call(
        body, name=name, in_specs=[vm], out_specs=vm,
        out_shape=jax.ShapeDtypeStruct((R, C) if reduce else (N_DEV, R, C), f32),
        scratch_shapes=scratch,
    )(blk)


def _cast_shard(place, w, name):
    R, C = w.shape
    tr = _tile(R, max(16, (1 << 19) // C // 16 * 16), 16)

    def body(p_ref, w_ref, o_ref):
        o_ref[0] = w_ref[...].astype(bf16)

    return pl.pallas_call(
        body, name=name,
        grid_spec=pltpu.PrefetchScalarGridSpec(
            num_scalar_prefetch=1, grid=(R // tr,),
            in_specs=[pl.BlockSpec((tr, C), lambda r, p: (r, 0))],
            out_specs=pl.BlockSpec((1, tr, C), lambda r, p: (p[1], r, 0))),
        out_shape=jax.ShapeDtypeStruct((N_CHIPS, R, C), bf16),
        compiler_params=_params(("parallel",)),
    )(place, w)


class _Phase:
    def __init__(self, arrays, out_shapes, aliases, n_copies, copies, arrivals, own_starts=(), own_waits=()):
        self.arrays, self.out_shapes, self.aliases = list(arrays), list(out_shapes), dict(aliases)
        self.n_copies, self.copies, self.arrivals = n_copies, copies, arrivals
        self.own_starts, self.own_waits = tuple(own_starts), tuple(own_waits)

    def sems(self):
        return [pltpu.SemaphoreType.DMA((self.n_copies,)), pltpu.SemaphoreType.DMA((self.n_copies,))]

    def _descriptors(self, pin, pout, send_sems, recv_sems):
        return [pltpu.make_async_remote_copy(src_ref=s, dst_ref=d, send_sem=send_sems.at[i], recv_sem=recv_sems.at[i],
                                             device_id=to, device_id_type=MESH)
                for i, (s, d, to) in enumerate(self.copies(pin, pout))]

    def _arrival(self, i, pin, pout, send_sems, recv_sems):
        dst = self.arrivals(pin, pout)[i]
        return pltpu.make_async_remote_copy(src_ref=dst, dst_ref=dst, send_sem=send_sems.at[i], recv_sem=recv_sems.at[i],
                                            device_id=_place(), device_id_type=MESH)

    def start(self, pin, pout, send_sems, recv_sems):
        for i, cp in enumerate(self._descriptors(pin, pout, send_sems, recv_sems)):
            if i not in self.own_starts:
                cp.start()

    def begin(self, i, pin, pout, send_sems, recv_sems):
        self._descriptors(pin, pout, send_sems, recv_sems)[i].start()

    def arrived(self, i, pin, pout, send_sems, recv_sems):
        self._arrival(i, pin, pout, send_sems, recv_sems).wait_recv()

    def finish(self, pin, pout, send_sems, recv_sems):
        for i in range(self.n_copies):
            if i not in self.own_waits:
                self._arrival(i, pin, pout, send_sems, recv_sems).wait_recv()
        for cp in self._descriptors(pin, pout, send_sems, recv_sems):
            cp.wait_send()


def _join(phases):
    if len(phases) == 1:
        return phases[0]
    ai = np.cumsum([0] + [len(p.arrays) for p in phases])
    oi = np.cumsum([0] + [len(p.out_shapes) for p in phases])

    def each(fn_name, pin, pout):
        return [item for k, p in enumerate(phases)
                for item in getattr(p, fn_name)(pin[ai[k]:ai[k + 1]], pout[oi[k]:oi[k + 1]])]

    aliases = {int(ai[k]) + i: int(oi[k]) + j for k, p in enumerate(phases) for i, j in p.aliases.items()}
    ci = np.cumsum([0] + [p.n_copies for p in phases])
    shifted = lambda attr: [int(ci[k]) + i for k, p in enumerate(phases) for i in getattr(p, attr)]
    return _Phase([a for p in phases for a in p.arrays], [s for p in phases for s in p.out_shapes], aliases,
                  int(ci[-1]), functools.partial(each, "copies"), functools.partial(each, "arrivals"),
                  shifted("own_starts"), shifted("own_waits"))


def _call(body, phase, *, name, grid, in_specs, out_specs, out_shape, scratch_shapes, args, prefetch=(), expose=False,
          aliases=None):
    seq = _params(("arbitrary",) * len(grid))
    np_ = len(prefetch)
    own = {np_ + i: j for i, j in (aliases or {}).items()}
    if phase is None:
        spec = pltpu.PrefetchScalarGridSpec(num_scalar_prefetch=np_, grid=grid, in_specs=in_specs, out_specs=out_specs,
                                            scratch_shapes=scratch_shapes)
        res = pl.pallas_call(body, name=name, grid_spec=spec, out_shape=out_shape, input_output_aliases=own,
                             compiler_params=seq)(*prefetch, *args)
        return list(res), []
    ni, no, ns = len(in_specs), len(out_specs), len(scratch_shapes)
    pi, po = len(phase.arrays), len(phase.out_shapes)

    def hosted(*refs):
        cut = np.cumsum([np_, ni, pi, no, po, ns])
        pre, ins, pin, outs, pout, scr, sems = (refs[a:b] for a, b in zip([0, *cut], [*cut, len(refs)]))
        ids = [pl.program_id(d) for d in range(len(grid))]
        first = functools.reduce(lambda p, q: p & q, [i == 0 for i in ids])
        last = functools.reduce(lambda p, q: p & q, [i == g - 1 for i, g in zip(ids, grid)])
        pl.when(first)(lambda: phase.start(pin, pout, *sems))
        body(*pre, *ins, *outs, *scr, **({"carried": (pin, pout, sems)} if expose else {}))
        pl.when(last)(lambda: phase.finish(pin, pout, *sems))

    anyspace = pl.BlockSpec(memory_space=pl.ANY)
    spec = pltpu.PrefetchScalarGridSpec(
        num_scalar_prefetch=np_, grid=grid, in_specs=list(in_specs) + [anyspace] * pi,
        out_specs=list(out_specs) + [anyspace] * po, scratch_shapes=list(scratch_shapes) + phase.sems())
    res = pl.pallas_call(
        hosted, name=name, grid_spec=spec, out_shape=list(out_shape) + phase.out_shapes,
        input_output_aliases={**own, **{np_ + ni + i: no + j for i, j in phase.aliases.items()}}, compiler_params=seq,
    )(*prefetch, *args, *phase.arrays)
    return list(res[:no]), list(res[no:])


def _run_phases(name, phases):
    first = phases[0]
    pi, po = len(first.arrays), len(first.out_shapes)

    def body(*refs):
        pin, pout, sems = refs[:pi], refs[pi:pi + po], refs[pi + po:]
        for n, ph in enumerate(phases):
            ph.start(pin, pout, *sems[2 * n:2 * n + 2])
            ph.finish(pin, pout, *sems[2 * n:2 * n + 2])

    anyspace = pl.BlockSpec(memory_space=pl.ANY)
    return list(pl.pallas_call(
        body, name=name, in_specs=[anyspace] * pi, out_specs=[anyspace] * po, out_shape=first.out_shapes,
        input_output_aliases=first.aliases, scratch_shapes=[s for ph in phases for s in ph.sems()],
    )(*first.arrays))


def _half_rows(buf, c):
    half = buf.shape[1] // 2
    return pl.ds(c * half, half), pl.ds((1 - c) * half, half)


def _gather_phase(bufs, over_ici):
    n = len(bufs)
    shapes = [jax.ShapeDtypeStruct(b.shape, b.dtype) for b in bufs]

    def landed(out, which):
        x, y, c = _place()
        _, ks = _other_chips(x, y)
        return [out[a].at[ks[j], _half_rows(bufs[a], c)[which]] for a in range(n) for j in range(3)]

    def ici(pin, out):
        x, y, c = _place()
        chips, _ = _other_chips(x, y)
        mine = [out[a].at[2 * x + y, _half_rows(bufs[a], c)[0]] for a in range(n)]
        return [(mine[a], mine[a], (*chips[j], c)) for a in range(n) for j in range(3)]

    def d2d(pin, out):
        x, y, c = _place()
        return [(dst, dst, (x, y, 1 - c)) for dst in landed(out, 0)]

    if over_ici:
        return _Phase(bufs, shapes, {a: a for a in range(n)}, 3 * n, ici, lambda pin, out: landed(out, 0))
    return _Phase(bufs, shapes, {a: a for a in range(n)}, 3 * n, d2d, lambda pin, out: landed(out, 1))


def _feed_phase(buf):
    def chips():
        x, y, c = _place()
        return [(x if f < 2 else 1 - x, y if f % 2 == 0 else 1 - y) for f in (1, 2, 3)]

    def copies(pin, out):
        x, y, c = _place()
        mine = _half_rows(buf, c)[0]
        own = out[0].at[2 * x + y, mine]
        sent = [(own, own, (cx, cy, c)) for cx, cy in chips()]
        return sent + [(out[0].at[2 * cx + cy, mine], out[0].at[2 * cx + cy, mine], (x, y, 1 - c)) for cx, cy in chips()]

    def arrivals(pin, out):
        x, y, c = _place()
        mine, theirs = _half_rows(buf, c)
        return [out[0].at[2 * cx + cy, rows] for rows in (mine, theirs) for cx, cy in chips()]

    return _Phase([buf], [jax.ShapeDtypeStruct(buf.shape, buf.dtype)], {0: 0}, 6, copies, arrivals,
                  own_starts=(3, 4, 5), own_waits=range(6))


def _rs_sibling(grads, name):
    n = len(grads)

    def body(*refs):
        g, out, send_sems, recv_sems = refs[:n], refs[n:2 * n], refs[2 * n], refs[2 * n + 1]
        x, y, c = _place()
        copies = []
        for a in range(n):
            half = grads[a].shape[1] // 2
            cp = pltpu.make_async_remote_copy(src_ref=g[a].at[:, pl.ds((1 - c) * half, half)], dst_ref=out[a],
                                              send_sem=send_sems.at[a], recv_sem=recv_sems.at[a],
                                              device_id=(x, y, 1 - c), device_id_type=MESH)
            cp.start()
            copies.append(cp)
        for cp in copies:
            cp.wait()

    anyspace = pl.BlockSpec(memory_space=pl.ANY)
    return pl.pallas_call(
        body, name=name, in_specs=[anyspace] * n, out_specs=[anyspace] * n,
        out_shape=[jax.ShapeDtypeStruct((N_CHIPS, g.shape[1] // 2, g.shape[2]), g.dtype) for g in grads],
        scratch_shapes=[pltpu.SemaphoreType.DMA((n,)), pltpu.SemaphoreType.DMA((n,))],
    )(*grads)


def _rs_add_sibling(place, grad, got, name):
    _, R, C = grad.shape
    half = R // 2
    tr = _tile(half, max(16, (1 << 19) // C // 16 * 16), 16)
    nr = half // tr

    def body(p_ref, a_ref, b_ref, o_ref):
        o_ref[...] = (a_ref[...].astype(f32) + b_ref[...].astype(f32)).astype(o_ref.dtype)

    return pl.pallas_call(
        body, name=name,
        grid_spec=pltpu.PrefetchScalarGridSpec(
            num_scalar_prefetch=1, grid=(N_CHIPS, nr),
            in_specs=[pl.BlockSpec((1, tr, C), lambda k, r, p: (k, p[0] * nr + r, 0)),
                      pl.BlockSpec((1, tr, C), lambda k, r, p: (k, r, 0))],
            out_specs=pl.BlockSpec((1, tr, C), lambda k, r, p: (k, r, 0))),
        out_shape=jax.ShapeDtypeStruct((N_CHIPS, half, C), bf16),
        compiler_params=_params(("parallel", "parallel")),
    )(place, grad, got)


def _rs_chips_phase(parts):
    n = len(parts)

    def copies(p, fc):
        x, y, c = _place()
        chips, ks = _other_chips(x, y)
        return [(p[a].at[ks[j]], fc[a].at[j], (*chips[j], c)) for a in range(n) for j in range(3)]

    shapes = [jax.ShapeDtypeStruct((3,) + q.shape[1:], q.dtype) for q in parts]
    return _Phase(parts, shapes, {}, 3 * n, copies, lambda p, fc: [fc[a].at[j] for a in range(n) for j in range(3)])


def _rs_hand_phase(parts, from_chips):
    n = len(parts)

    def copies(pin, fs):
        x, y, c = _place()
        sib = (x, y, 1 - c)
        own = [(pin[a].at[2 * x + y], fs[a].at[0], sib) for a in range(n)]
        return own + [(pin[n + a].at[j], fs[a].at[1 + j], sib) for a in range(n) for j in range(3)]

    def arrivals(pin, fs):
        return [fs[a].at[0] for a in range(n)] + [fs[a].at[1 + j] for a in range(n) for j in range(3)]

    shapes = [jax.ShapeDtypeStruct((4,) + q.shape[1:], q.dtype) for q in parts]
    return _Phase(list(parts) + list(from_chips), shapes, {}, 4 * n, copies, arrivals)


class _Exchange:
    def __init__(self, place):
        self.place = place

    def feed(self, buf):
        return _feed_phase(buf)

    def gather(self, bufs, over_ici):
        return _gather_phase(bufs, over_ici)

    def pair_sums(self, names, grads):
        got = _rs_sibling(grads, "rs_sibling_" + names[0])
        return [_rs_add_sibling(self.place, g, r, "rs_add_" + n) for n, g, r in zip(names, grads, got)]

    def to_chips(self, parts):
        return _rs_chips_phase(parts)

    def to_sibling(self, parts, from_chips):
        return _rs_hand_phase(parts, from_chips)

    def hand_over(self, name, parts, from_chips):
        return _run_phases(name, [_rs_hand_phase(parts, from_chips)])


def _local_step(place, x, target, norm_mix, b_gate, rb_full, norm_ffn, norm_final, w_in, rest, exch):
    B, S, D = x.shape
    T = B * S
    x2 = x.reshape(T, D)
    tg2 = target.reshape(T, D)
    rope, decay = _rope_tables(S), _decay_tables()
    trows = _bias_rows(rb_full)
    g_fin = norm_final.reshape(1, D)

    mrg, ffn = ["w_ret_out", "w_att_out", "w_out"], ["w_ffn_gate", "w_ffn_up", "w_ffn_down"]
    (xn, proj), got = _in_proj(place, x2, norm_mix, _join([exch.feed(w_in), exch.gather([rest[n] for n in mrg], True)]))
    w_in, wb = got[0], {}
    (qr, kr, o, u, states), got = _ret_fwd(proj, B, S, rope, decay, _join([exch.gather([rest["w_ffn_gate"]], True),
                                                                         exch.gather(got[1:], False)]))
    wb.update(zip(mrg, got[1:]))
    (ao,), got = _att_fwd(proj, trows, B, S, _join([exch.gather([rest["w_ffn_up"], rest["w_ffn_down"]], True),
                                                    exch.gather(got[:1], False)]))
    wb["w_ffn_gate"] = got[2]
    w_ro, w_out = wb["w_ret_out"].reshape(-1, D), wb["w_out"].reshape(-1, D)
    (h1, yr, ya), got = _mix_fwd(x2, proj, u, ao, b_gate, w_ro, wb["w_att_out"], w_out, exch.gather(got[:2], False))
    wb.update(zip(ffn[1:], got))
    hn, a, b, f, dh2, part_fin = _ffn_fwd(h1, norm_ffn, wb["w_ffn_gate"], wb["w_ffn_up"], wb["w_ffn_down"], g_fin, tg2)

    da, db, dh1, part_ffn = _ffn_bwd(dh2, h1, norm_ffn, a, b, wb["w_ffn_gate"], wb["w_ffn_up"], wb["w_ffn_down"])
    ffn = ["w_ffn_down", "w_ffn_gate", "w_ffn_up"]
    p_ffn = exch.pair_sums(ffn, [_wgrad(f, dh2, 0, "wgrad_ffn_down")[0], _wgrad(da, hn, 0, "wgrad_ffn_gate")[0],
                                 _wgrad(db, hn, 0, "wgrad_ffn_up")[0]])
    (du, dao, dgl, mix, dyr, dya, part_bg), c_down = _mix_bwd(dh1, proj, yr, ya, b_gate, w_ro, wb["w_att_out"], w_out,
                                                               exch.to_chips(p_ffn[:1]))
    mrg = ["w_out", "w_ret_out", "w_att_out"]
    p_mrg = exch.pair_sums(mrg, [_wgrad(mix, dh1, 0, "wgrad_out")[0], _wgrad(u, dyr, 0, "wgrad_ret_out")[0],
                                 _wgrad(ao, dya, 1, "wgrad_att_out")[0]])
    (dproj,), c_gate_up = _ret_bwd(proj, qr, kr, o, states, du, B, S, rope, decay, exch.to_chips(p_ffn[1:]))
    c_ffn = c_down + c_gate_up
    (dproj, dvec), got = _att_bwd(proj, dao, trows, dproj, B, S, _join([exch.to_chips(p_mrg),
                                                                        exch.to_sibling(p_ffn, c_ffn)]))
    c_mrg, s_ffn = got[:len(mrg)], got[len(mrg):]
    dproj = lax.dynamic_update_slice(dproj, dgl, (0, C_GL))
    g_in, s_mrg = _wgrad(xn, dproj, 1, "wgrad_in", exch.to_sibling(p_mrg, c_mrg))
    p_in = exch.pair_sums(["w_in"], [g_in])
    (gx, part_mix), c_in = _in_proj_bwd(dproj, w_in, x2, norm_mix, dh1, exch.to_chips(p_in))
    s_in = exch.hand_over("rs_hand_w_in", p_in, c_in)
    gbig = dict(zip(ffn + mrg + ["w_in"], zip(p_ffn + p_mrg + p_in, c_ffn + c_mrg + c_in, s_ffn + s_mrg + s_in)))
    rows = lambda p, r: p.reshape(-1, 8, p.shape[-1])[:, r, :].sum(axis=0)
    lo = KWIN - 1 - (MAX_REL - 1)
    drb = jnp.concatenate([jnp.flip(dvec[:, lo:lo + N_REL - 1], axis=1), dvec[:, :lo].sum(axis=1, keepdims=True)], axis=1)
    gsmall = {
        "norm_mix": rows(part_mix, 0), "b_gate": rows(part_bg, 0), "rel_bias": drb, "norm_ffn": rows(part_ffn, 0),
        "norm_final": rows(part_fin, 0),
    }
    return rows(part_fin, 1), gx.reshape(B, S, D), gbig, gsmall


SMALL_ROWS = 16


def _pack_small(gs, loss_lanes):
    D = D_MODEL
    rb = jnp.pad(gs["rel_bias"].reshape(-1), (0, 3 * D - ATT_HEADS * N_REL)).reshape(3, D)
    rows = [gs["norm_mix"].reshape(1, D), gs["b_gate"].reshape(2, D), gs["norm_ffn"].reshape(1, D),
            gs["norm_final"].reshape(1, D), rb, loss_lanes.reshape(1, D)]
    used = sum(r.shape[0] for r in rows)
    return jnp.concatenate(rows + [jnp.zeros((SMALL_ROWS - used, D), f32)], axis=0)


def kernel(x, norm_mix, w_in, b_gate, rel_bias, w_ret_out, w_att_out, w_out, norm_ffn, w_ffn_gate, w_ffn_up, w_ffn_down, norm_final, loss_target, m_norm_mix, m_w_in, m_b_gate, m_rel_bias, m_w_ret_out, m_w_att_out, m_w_out, m_norm_ffn, m_w_ffn_gate, m_w_ffn_up, m_w_ffn_down, m_norm_final, v_norm_mix, v_w_in, v_b_gate, v_rel_bias, v_w_ret_out, v_w_att_out, v_w_out, v_norm_ffn, v_w_ffn_gate, v_w_ffn_up, v_w_ffn_down, v_norm_final):
    w = dict(norm_mix=norm_mix, w_in=w_in, b_gate=b_gate, rel_bias=rel_bias, w_ret_out=w_ret_out, w_att_out=w_att_out,
             w_out=w_out, norm_ffn=norm_ffn, w_ffn_gate=w_ffn_gate, w_ffn_up=w_ffn_up, w_ffn_down=w_ffn_down,
             norm_final=norm_final)
    m = dict(norm_mix=m_norm_mix, w_in=m_w_in, b_gate=m_b_gate, rel_bias=m_rel_bias, w_ret_out=m_w_ret_out,
             w_att_out=m_w_att_out, w_out=m_w_out, norm_ffn=m_norm_ffn, w_ffn_gate=m_w_ffn_gate, w_ffn_up=m_w_ffn_up,
             w_ffn_down=m_w_ffn_down, norm_final=m_norm_final)
    v = dict(norm_mix=v_norm_mix, w_in=v_w_in, b_gate=v_b_gate, rel_bias=v_rel_bias, w_ret_out=v_w_ret_out,
             w_att_out=v_w_att_out, w_out=v_w_out, norm_ffn=v_norm_ffn, w_ffn_gate=v_w_ffn_gate, w_ffn_up=v_w_ffn_up,
             w_ffn_down=v_w_ffn_down, norm_final=v_norm_final)
    xi, yi, ci = _place()
    k_me = 2 * xi + yi

    place = jnp.stack([ci, k_me]).astype(jnp.int32)
    big = [n for n, _ in BIG]

    turned = ("w_ffn_gate", "w_ffn_up")
    shard = lambda d, n: jnp.swapaxes(d[n][0], 0, 1) if n in turned else d[n][0]
    whole = lambda a, n: (jnp.swapaxes(a, 0, 1) if n in turned else a)[None]

    bufs = {n: _cast_shard(place, shard(w, n), "cast_" + n) for n in big}
    rest = {n: bufs[n] for n in big if n != "w_in"}
    nrel_loc = rel_bias.shape[-1]
    rb_all = _exchange_small(jnp.pad(rel_bias[0], ((0, 0), (0, 128 - nrel_loc))), "gather_rel_bias", False)
    rb_full = jnp.concatenate([rb_all[2 * k, :, :nrel_loc] for k in range(N_CHIPS)], axis=1)

    loss_lanes, grad_x, gbig, gsmall = _local_step(place, x, loss_target, norm_mix, b_gate, rb_full, norm_ffn, norm_final,
                                                   bufs["w_in"], rest, _Exchange(place))

    small = _exchange_small(_pack_small(gsmall, loss_lanes), "reduce_small", True)
    D = D_MODEL
    loss = jnp.sum(small[8])
    drb_full = small[5:8].reshape(-1)[:ATT_HEADS * N_REL].reshape(ATT_HEADS, N_REL)
    g = {
        "norm_mix": small[0:1], "b_gate": small[1:3].reshape(1, 2 * D), "norm_ffn": small[3:4], "norm_final": small[4],
        "rel_bias": lax.dynamic_slice_in_dim(drb_full, k_me * nrel_loc, nrel_loc, axis=1)[None],
    }

    delta, new_m, new_v = {}, {}, {}
    for n in big:
        g_, d_, m_, v_ = _adamw_sum(place, shard(w, n), shard(m, n), shard(v, n), *gbig[n], "adamw_" + n)
        g[n], delta[n], new_m[n], new_v[n] = whole(g_, n), whole(d_, n), whole(m_, n), whole(v_, n)
    flat = lambda d: jnp.concatenate([d[n].reshape(-1) for n in SMALL])
    n_small = sum(int(np.prod(w[n].shape)) for n in SMALL)
    n_pad = -n_small % 1024
    packs = [jnp.pad(flat(d), (0, n_pad)).reshape(-1, 128) for d in (w, g, m, v)]
    outs = _adamw(*packs, "adamw_small")
    for res, dst in zip(outs, (delta, new_m, new_v)):
        off = 0
        fl = res.reshape(-1)
        for n in SMALL:
            sz = int(np.prod(w[n].shape))
            dst[n] = fl[off:off + sz].reshape(w[n].shape)
            off += sz

    return (loss, grad_x, *[g[n] for n in WEIGHTS], *[delta[n] for n in WEIGHTS], *[new_m[n] for n in WEIGHTS],
            *[new_v[n] for n in WEIGHTS])
```

```python
import functools

import numpy as np
import jax
import jax.numpy as jnp
from jax import lax
from jax.experimental import pallas as pl
from jax.experimental.pallas import tpu as pltpu

f32 = jnp.float32
bf16 = jnp.bfloat16

D_MODEL = 1024
CHUNK = 64
RET_HEADS = 4
RET_KEY_DIM = 128
RET_VAL_DIM = 256
ATT_HEADS = 8
ATT_HEAD_DIM = 64
ATT_W = ATT_HEADS * ATT_HEAD_DIM
BAND_CHUNKS = 8
PAD = BAND_CHUNKS * CHUNK
MAX_REL = 256
N_REL = CHUNK + MAX_REL
D_FF = 2816
N_IN = 6656
ROPE_BASE = 10000.0
EPS = 1e-6
NEG_INF = -1e30
C_RQ, C_RK, C_RV, C_RG, C_AQ, C_AK, C_AV, C_GL = 0, 512, 1024, 2048, 3072, 3584, 4096, 4608

ADAM_LR, ADAM_B1, ADAM_B2, ADAM_EPS, ADAM_WD, ADAM_STEP = 0.001, 0.9, 0.999, 1e-08, 0.01, 10

N_CHIPS = 4
N_DEV = 8
WGRAD_ACC_BYTES = 8 * 1024 * 1024
ROW_TILE = 512
BIG_ROW_TILE = 1024
IN_ORDER = (0, 2, 3, 1)
QBLK = 256
KWIN = PAD + QBLK
TOEP = 1024
VMEM_LIMIT = 56 * 1024 * 1024
MESH = pl.DeviceIdType.MESH

BIG = (
    ("w_in", 1), ("w_ret_out", 0), ("w_att_out", 1), ("w_out", 0), ("w_ffn_gate", 1), ("w_ffn_up", 1), ("w_ffn_down", 0))
WEIGHTS = ("norm_mix", "w_in", "b_gate", "rel_bias", "w_ret_out", "w_att_out", "w_out", "norm_ffn", "w_ffn_gate",
           "w_ffn_up", "w_ffn_down", "norm_final")
SMALL = ("norm_mix", "b_gate", "rel_bias", "norm_ffn", "norm_final")


def _dot(a, b):
    return lax.dot_general(a, b, (((1,), (0,)), ((), ())), preferred_element_type=f32)


def _dot_nt(a, b):
    return lax.dot_general(a, b, (((1,), (1,)), ((), ())), preferred_element_type=f32)


def _dot_tn(a, b):
    return lax.dot_general(a, b, (((0,), (0,)), ((), ())), preferred_element_type=f32)


def _sig(x):
    return 1.0 / (1.0 + jnp.exp(-x))


def _tile(n, pref, mult):
    best = None
    for t in range(mult, min(n, pref) + 1, mult):
        if n % t == 0:
            best = t
    return best if best is not None else n


def _params(sem, vmem=VMEM_LIMIT):
    return pltpu.CompilerParams(dimension_semantics=sem, vmem_limit_bytes=vmem)


def _in_proj(place, x2, gamma, phase):
    T, D = x2.shape
    _, _, ns = phase.arrays[0].shape
    tm = _tile(T, BIG_ROW_TILE, 8)
    ni = T // tm
    pass_chip = lambda j: sum(jnp.where(j == n, f, 0) for n, f in enumerate(IN_ORDER))

    def body(p_ref, x_ref, g_ref, xn_ref, pr_ref, xs_ref, w_ref, w_sem, carried):
        j, i = pl.program_id(0), pl.program_id(1)
        pin, pout, sems = carried
        rows = pl.ds(pl.multiple_of(i * tm, tm), tm)

        @pl.when(i == 0)
        def _():
            for n, f in enumerate(IN_ORDER):
                if f:
                    @pl.when(j == n)
                    def _():
                        phase.arrived(f - 1, pin, pout, *sems)
                        phase.begin(2 + f, pin, pout, *sems)
                        phase.arrived(2 + f, pin, pout, *sems)
            shard = pltpu.make_async_copy(pout[0].at[jnp.bitwise_xor(p_ref[1], pass_chip(j))], w_ref, w_sem)
            shard.start()
            shard.wait()

        @pl.when(j == 0)
        def _():
            x = x_ref[...]
            r = lax.rsqrt(jnp.mean(x * x, axis=-1, keepdims=True) + EPS)
            xn = (x * r * g_ref[...]).astype(bf16)
            xs_ref[rows, :] = xn
            xn_ref[...] = xn

        pr_ref[...] = _dot(xs_ref[rows, :], w_ref[...]).astype(bf16)

    first_pass = lambda j, i, p: (jnp.where(j == 0, i, ni - 1), 0)
    return _call(
        body, phase, name="in_proj", grid=(N_CHIPS, ni), prefetch=(place,), expose=True,
        in_specs=[pl.BlockSpec((tm, D), first_pass), pl.BlockSpec((1, D), lambda j, i, p: (0, 0))],
        out_specs=[pl.BlockSpec((tm, D), first_pass),
                   pl.BlockSpec((tm, ns), lambda j, i, p: (i, jnp.bitwise_xor(p[1], pass_chip(j))))],
        out_shape=[jax.ShapeDtypeStruct((T, D), bf16), jax.ShapeDtypeStruct((T, N_CHIPS * ns), bf16)],
        scratch_shapes=[pltpu.VMEM((T, D), bf16), pltpu.VMEM((D, ns), bf16), pltpu.SemaphoreType.DMA],
        args=(x2, gamma))


def _rope_tables(S):
    d = RET_KEY_DIM
    freqs = ROPE_BASE ** (-jnp.arange(0, d, 2, dtype=f32) / d)
    ang = jnp.arange(S, dtype=f32)[:, None] * freqs[None, :]
    cos, sin = jnp.cos(ang), jnp.sin(ang)
    return jnp.concatenate([cos, cos], axis=1), jnp.concatenate([-sin, sin], axis=1)


def _decay_tables():
    H = RET_HEADS
    log_g = jnp.log(1.0 - 2.0 ** (-5.0 - jnp.arange(H, dtype=f32)))
    p = jnp.arange(CHUNK, dtype=f32)
    intra = jnp.exp(log_g[:, None, None] * jnp.abs(p[:, None] - p[None, :]))
    q_dec = jnp.exp(log_g[:, None] * (p[None, :] + 1.0))
    k_dec = jnp.exp(log_g[:, None] * (CHUNK - 1.0 - p[None, :]))
    c_dec = jnp.exp(log_g * CHUNK)
    q_dec = jnp.broadcast_to(q_dec[:, :, None], (H, CHUNK, RET_KEY_DIM))
    k_dec = jnp.broadcast_to(k_dec[:, :, None], (H, CHUNK, RET_KEY_DIM))
    c_dec = jnp.broadcast_to(c_dec[:, None, None], (H, 1, RET_VAL_DIM))
    return intra, q_dec, k_dec, c_dec


K_SCALE = RET_KEY_DIM ** -0.5


RET_CHUNKS = 4


def _ret_tables_specs():
    whole = lambda *shape: pl.BlockSpec(shape, lambda b, i: (0,) * len(shape))
    return [whole(RET_HEADS, CHUNK, CHUNK), whole(RET_HEADS, CHUNK, RET_KEY_DIM), whole(RET_HEADS, CHUNK, RET_KEY_DIM),
            whole(RET_HEADS, 1, RET_VAL_DIM)]


def _rotate(x, cos, sn):
    return x * cos + pltpu.roll(x, RET_KEY_DIM // 2, 1) * sn


def _ret_fwd(proj, B, S, rope, decay, phase=None):
    T = B * S
    nc = S // CHUNK
    H, dk, dv = RET_HEADS, RET_KEY_DIM, RET_VAL_DIM
    sb = RET_CHUNKS * CHUNK
    ns = S // sb

    def body(q_ref, k_ref, v_ref, g_ref, cos_ref, sin_ref, intra_ref, qd_ref, kd_ref, cd_ref,
             qr_ref, kr_ref, o_ref, u_ref, st_ref, state_ref):
        @pl.when(pl.program_id(1) == 0)
        def _():
            state_ref[...] = jnp.zeros_like(state_ref)

        cos, sn = cos_ref[...], sin_ref[...]
        for h in range(H):
            hs = slice(h * dk, (h + 1) * dk)
            qr_ref[:, hs] = _rotate(q_ref[:, hs].astype(f32), cos, sn).astype(bf16)
            kr_ref[:, hs] = (_rotate(k_ref[:, hs].astype(f32), cos, sn) * K_SCALE).astype(bf16)
        states = [state_ref[h] for h in range(H)]
        for ci in range(RET_CHUNKS):
            r = slice(ci * CHUNK, (ci + 1) * CHUNK)
            for h in range(H):
                hk, hv = slice(h * dk, (h + 1) * dk), slice(h * dv, (h + 1) * dv)
                qi, ki, vi = qr_ref[r, hk], kr_ref[r, hk], v_ref[r, hv]
                stb = states[h].astype(bf16)
                st_ref[0, h, ci] = stb
                s = (_dot_nt(qi, ki) * intra_ref[h]).astype(bf16)
                o = _dot(s, vi) + _dot((qi.astype(f32) * qd_ref[h]).astype(bf16), stb)
                states[h] = states[h] * cd_ref[h] + _dot_tn((ki.astype(f32) * kd_ref[h]).astype(bf16), vi)
                mu = jnp.mean(o, axis=-1, keepdims=True)
                xc = o - mu
                var = jnp.mean(xc * xc, axis=-1, keepdims=True)
                oh = xc * lax.rsqrt(var + EPS)
                g = g_ref[r, hv].astype(f32)
                o_ref[r, hv] = o.astype(bf16)
                u_ref[r, hv] = (g * _sig(g) * oh).astype(bf16)
        for h in range(H):
            state_ref[h] = states[h]

    blk = lambda w, c: pl.BlockSpec((sb, w), lambda b, i: (b * ns + i, c))
    return _call(
        body, phase, name="ret_fwd", grid=(B, ns), scratch_shapes=[pltpu.VMEM((H, dk, dv), f32)],
        in_specs=[blk(H * dk, C_RQ // (H * dk)), blk(H * dk, C_RK // (H * dk)), blk(H * dv, C_RV // (H * dv)),
                  blk(H * dv, C_RG // (H * dv)),
                  pl.BlockSpec((sb, dk), lambda b, i: (i, 0)), pl.BlockSpec((sb, dk), lambda b, i: (i, 0)),
                  *_ret_tables_specs()],
        out_specs=[blk(H * dk, 0), blk(H * dk, 0), blk(H * dv, 0), blk(H * dv, 0),
                   pl.BlockSpec((1, H, RET_CHUNKS, dk, dv), lambda b, i: (b, 0, i, 0, 0))],
        out_shape=[jax.ShapeDtypeStruct((T, H * dk), bf16), jax.ShapeDtypeStruct((T, H * dk), bf16),
                   jax.ShapeDtypeStruct((T, H * dv), bf16), jax.ShapeDtypeStruct((T, H * dv), bf16),
                   jax.ShapeDtypeStruct((B, H, nc, dk, dv), bf16)],
        args=(proj, proj, proj, proj, *rope, *decay))


def _bias_rows(rb):
    last = rb[:, N_REL - 1:]
    return jnp.concatenate([
        jnp.broadcast_to(last, (ATT_HEADS, PAD - MAX_REL + 1)),
        jnp.flip(rb[:, :N_REL - 1], axis=1),
        jnp.broadcast_to(rb[:, :1], (ATT_HEADS, KWIN - PAD - CHUNK)),
        jnp.broadcast_to(last, (ATT_HEADS, TOEP - KWIN)),
    ], axis=1)


def _build_bias(t_ref, bias_ref):
    row = lax.broadcasted_iota(jnp.int32, (QBLK, KWIN), 0) // CHUNK
    col = lax.broadcasted_iota(jnp.int32, (QBLK, KWIN), 1) // CHUNK
    delta = BAND_CHUNKS + row - col
    vis = (delta >= 0) & (delta <= BAND_CHUNKS)
    for h in range(ATT_HEADS):
        t = jnp.broadcast_to(t_ref[h:h + 1, :], (QBLK, TOEP))
        rolled = pltpu.roll(t, 0, 1, stride=1, stride_axis=0)
        bias_ref[h] = jnp.where(vis, rolled[:, :KWIN], NEG_INF)


ATT_SCALE = ATT_HEAD_DIM ** -0.5


def _att_probs(qh, kh, bias):
    s = _dot_nt(qh, kh) + bias
    m = jnp.max(s, axis=-1, keepdims=True)
    p = jnp.exp(s - m)
    return p * (1.0 / jnp.sum(p, axis=-1, keepdims=True))


def _first_of_pair():
    return lax.broadcasted_iota(jnp.int32, (1, 2 * ATT_HEAD_DIM), 1) < ATT_HEAD_DIM


def _by_window(i, step):
    sizes = list(range(QBLK, KWIN, QBLK))
    for n, nk in enumerate(sizes):
        pl.when(i == n)(functools.partial(step, nk))
    pl.when(i >= len(sizes))(functools.partial(step, KWIN))


def _att_fwd(proj, trows, B, S, phase=None):
    T = B * S
    nq = S // QBLK
    dh = ATT_HEAD_DIM

    def body(q_ref, k_ref, v_ref, t_ref, o_ref, bias_ref):
        i = pl.program_id(1)

        @pl.when((pl.program_id(0) == 0) & (i == 0))
        def _():
            _build_bias(t_ref, bias_ref)

        def step(nk):
            win = pl.ds(pl.multiple_of((i + 1) * QBLK - nk, QBLK), nk)
            kw, vw = k_ref[win, :], v_ref[win, :]
            first = _first_of_pair()
            outs = []
            for p in range(ATT_HEADS // 2):
                ps = slice(2 * p * dh, 2 * (p + 1) * dh)
                q2, k2, v2 = q_ref[:, ps] * ATT_SCALE, kw[:, ps], vw[:, ps]
                both = []
                for e in range(2):
                    qm = jnp.where(first == (e == 0), q2, jnp.zeros_like(q2))
                    pr = _att_probs(qm, k2, bias_ref[2 * p + e, :, KWIN - nk:])
                    both.append(_dot(pr.astype(bf16), v2))
                outs.append(jnp.where(first, both[0], both[1]))
            o_ref[...] = jnp.concatenate(outs, axis=1).astype(bf16)

        _by_window(i, step)

    return _call(
        body, phase, name="att_fwd", grid=(B, nq),
        in_specs=[pl.BlockSpec((QBLK, ATT_W), lambda b, i: (b * nq + i, C_AQ // ATT_W)),
                  pl.BlockSpec((S, ATT_W), lambda b, i: (b, C_AK // ATT_W)),
                  pl.BlockSpec((S, ATT_W), lambda b, i: (b, C_AV // ATT_W)),
                  pl.BlockSpec((ATT_HEADS, TOEP), lambda b, i: (0, 0))],
        out_specs=[pl.BlockSpec((QBLK, ATT_W), lambda b, i: (b * nq + i, 0))],
        out_shape=[jax.ShapeDtypeStruct((T, ATT_W), bf16)],
        scratch_shapes=[pltpu.VMEM((ATT_HEADS, QBLK, KWIN), f32)],
        args=(proj, proj, proj, trows))


def _gl_specs(tm):
    w = 512
    return [pl.BlockSpec((tm, w), functools.partial(lambda i, j: (i, C_GL // 512 + j), j=j)) for j in range(4)]


def _gates(gl_refs, bg_ref):
    gl = jnp.concatenate([r[...] for r in gl_refs], axis=1).astype(f32) + bg_ref[...]
    g = _sig(gl)
    return g[:, :D_MODEL], g[:, D_MODEL:]


def _mix_fwd(x2, proj, u, ao, b_gate, w_ro, w_ao, w_out, phase=None):
    T, D = x2.shape
    tm = _tile(T, ROW_TILE, 8)

    def body(x_ref, u_ref, ao_ref, g0, g1, g2, g3, bg_ref, wro_ref, wao_ref, wo_ref, h1_ref, yr_ref, ya_ref):
        yr = _dot(u_ref[...], wro_ref[...])
        ao = ao_ref[...]
        ya = jnp.concatenate([_dot(ao, wao_ref[k]) for k in range(N_CHIPS)], axis=1)
        gr, ga = _gates((g0, g1, g2, g3), bg_ref)
        mix = gr * yr + ga * ya
        h1_ref[...] = x_ref[...] + _dot(mix.astype(bf16), wo_ref[...])
        yr_ref[...] = yr.astype(bf16)
        ya_ref[...] = ya.astype(bf16)

    full = lambda a: pl.BlockSpec(a.shape, lambda i: (0,) * a.ndim)
    row = lambda n: pl.BlockSpec((tm, n), lambda i: (i, 0))
    return _call(
        body, phase, name="mix_fwd", grid=(T // tm,), scratch_shapes=[],
        in_specs=[row(D), row(D), row(ATT_W), *_gl_specs(tm), full(b_gate), full(w_ro), full(w_ao), full(w_out)],
        out_specs=[row(D), row(D), row(D)],
        out_shape=[jax.ShapeDtypeStruct((T, D), f32), jax.ShapeDtypeStruct((T, D), bf16),
                   jax.ShapeDtypeStruct((T, D), bf16)],
        args=(x2, u, ao, proj, proj, proj, proj, b_gate, w_ro, w_ao, w_out))


def _ffn_fwd(h1, g_ffn, wg, wu, wd, g_fin, target):
    T, D = h1.shape
    nf, tf, _ = wg.shape
    tm = _tile(T, ROW_TILE, 8)

    def body(h1_ref, g_ref, wg_ref, wu_ref, wd_ref, gf_ref, tg_ref, hn_ref, a_ref, b_ref, f_ref, dh2_ref, part_ref):
        h1v = h1_ref[...]
        r = lax.rsqrt(jnp.mean(h1v * h1v, axis=-1, keepdims=True) + EPS)
        hn = (h1v * r * g_ref[...]).astype(bf16)
        hn_ref[...] = hn
        h2 = h1v
        for k in range(nf):
            a = _dot_nt(hn, wg_ref[k])
            b = _dot_nt(hn, wu_ref[k])
            f = ((a * _sig(a)) * b).astype(bf16)
            a_ref[k] = a.astype(bf16)
            b_ref[k] = b.astype(bf16)
            f_ref[k] = f
            h2 = h2 + _dot(f, wd_ref[k])
        r = lax.rsqrt(jnp.mean(h2 * h2, axis=-1, keepdims=True) + EPS)
        n = h2 * r
        gf = gf_ref[...]
        e = n * gf - tg_ref[...]
        dy = e * (1.0 / D)
        dn = dy * gf
        dh2_ref[...] = r * (dn - n * jnp.mean(dn * n, axis=-1, keepdims=True))
        part_ref[...] = jnp.zeros_like(part_ref)
        part_ref[0:1, :] = jnp.sum(dy * n, axis=0, keepdims=True)
        part_ref[1:2, :] = (0.5 / D) * jnp.sum(e * e, axis=0, keepdims=True)

    row = lambda n: pl.BlockSpec((tm, n), lambda i: (i, 0))
    vec = pl.BlockSpec((1, D), lambda i: (0, 0))
    col = pl.BlockSpec((nf, tm, tf), lambda i: (0, i, 0))
    held = lambda w: pl.BlockSpec(w.shape, lambda i: (0, 0, 0), pipeline_mode=pl.Buffered(1))
    act = jax.ShapeDtypeStruct((nf, T, tf), bf16)
    return pl.pallas_call(
        body, name="ffn_fwd", grid=(T // tm,),
        in_specs=[row(D), vec, held(wg), held(wu), held(wd), vec, row(D)],
        out_specs=[row(D), col, col, col, row(D), pl.BlockSpec((8, D), lambda i: (i, 0))],
        out_shape=[jax.ShapeDtypeStruct((T, D), bf16), act, act, act,
                   jax.ShapeDtypeStruct((T, D), f32), jax.ShapeDtypeStruct((T // tm * 8, D), f32)],
        compiler_params=_params(("parallel",)),
    )(h1, g_ffn, wg, wu, wd, g_fin, target)


def _ffn_bwd(dh2, h1, g_ffn, a, b, wg, wu, wd):
    T, D = h1.shape
    nf, tf, _ = wg.shape
    tm = _tile(T, ROW_TILE // 2, 8)

    def body(dh2_ref, h1_ref, g_ref, a_ref, b_ref, wg_ref, wu_ref, wd_ref, da_ref, db_ref, dh1_ref, part_ref):
        dh2v = dh2_ref[...]
        dh2b = dh2v.astype(bf16)
        dhn = jnp.zeros((tm, D), f32)
        for k in range(nf):
            df = _dot_nt(dh2b, wd_ref[k])
            av = a_ref[k].astype(f32)
            sg = _sig(av)
            db = (df * (av * sg)).astype(bf16)
            da = (df * b_ref[k].astype(f32) * (sg * (1.0 + av * (1.0 - sg)))).astype(bf16)
            da_ref[k] = da
            db_ref[k] = db
            dhn = dhn + _dot(da, wg_ref[k]) + _dot(db, wu_ref[k])
        h = h1_ref[...]
        r = lax.rsqrt(jnp.mean(h * h, axis=-1, keepdims=True) + EPS)
        n = h * r
        dn = dhn * g_ref[...]
        dh1_ref[...] = dh2v + r * (dn - n * jnp.mean(dn * n, axis=-1, keepdims=True))
        part_ref[...] = jnp.zeros_like(part_ref)
        part_ref[0:1, :] = jnp.sum(dhn * n, axis=0, keepdims=True)

    row = lambda n: pl.BlockSpec((tm, n), lambda i: (i, 0))
    col = pl.BlockSpec((nf, tm, tf), lambda i: (0, i, 0))
    held = lambda w: pl.BlockSpec(w.shape, lambda i: (0, 0, 0), pipeline_mode=pl.Buffered(1))
    act = jax.ShapeDtypeStruct((nf, T, tf), bf16)
    return pl.pallas_call(
        body, name="ffn_bwd", grid=(T // tm,),
        in_specs=[row(D), row(D), pl.BlockSpec((1, D), lambda i: (0, 0)), col, col, held(wg), held(wu), held(wd)],
        out_specs=[col, col, row(D), pl.BlockSpec((8, D), lambda i: (i, 0))],
        out_shape=[act, act, jax.ShapeDtypeStruct((T, D), f32), jax.ShapeDtypeStruct((T // tm * 8, D), f32)],
        compiler_params=_params(("parallel",)),
    )(dh2, h1, g_ffn, a, b, wg, wu, wd)


def _mix_bwd(dh1, proj, yr, ya, b_gate, w_ro, w_ao, w_out, phase=None):
    T, D = dh1.shape
    tm = _tile(T, ROW_TILE, 8)

    def body(dh1_ref, g0, g1, g2, g3, bg_ref, yr_ref, ya_ref, wro_ref, wao_ref, wo_ref,
             du_ref, dao_ref, dgl_ref, mix_ref, dyr_ref, dya_ref, part_ref):
        dmix = _dot_nt(dh1_ref[...].astype(bf16), wo_ref[...])
        gr, ga = _gates((g0, g1, g2, g3), bg_ref)
        yr = yr_ref[...].astype(f32)
        ya = ya_ref[...].astype(f32)
        dyr = (dmix * gr).astype(bf16)
        dya = (dmix * ga).astype(bf16)
        dgl = jnp.concatenate([dmix * yr * gr * (1.0 - gr), dmix * ya * ga * (1.0 - ga)], axis=1)
        du_ref[...] = _dot_nt(dyr, wro_ref[...]).astype(bf16)
        ns = wao_ref.shape[2]
        dao = _dot_nt(dya[:, :ns], wao_ref[0])
        for k in range(1, N_CHIPS):
            dao = dao + _dot_nt(dya[:, k * ns:(k + 1) * ns], wao_ref[k])
        dao_ref[...] = dao.astype(bf16)
        dgl_ref[...] = dgl.astype(bf16)
        mix_ref[...] = (gr * yr + ga * ya).astype(bf16)
        dyr_ref[...] = dyr
        dya_ref[...] = dya
        part_ref[...] = jnp.zeros_like(part_ref)
        part_ref[0:1, :] = jnp.sum(dgl, axis=0, keepdims=True)

    full = lambda a: pl.BlockSpec(a.shape, lambda i: (0,) * a.ndim)
    row = lambda n: pl.BlockSpec((tm, n), lambda i: (i, 0))
    return _call(
        body, phase, name="mix_bwd", grid=(T // tm,), scratch_shapes=[],
        in_specs=[row(D), *_gl_specs(tm), full(b_gate), row(D), row(D), full(w_ro), full(w_ao), full(w_out)],
        out_specs=[row(D), row(ATT_W), row(2 * D), row(D), row(D), row(D), pl.BlockSpec((8, 2 * D), lambda i: (i, 0))],
        out_shape=[jax.ShapeDtypeStruct((T, D), bf16), jax.ShapeDtypeStruct((T, ATT_W), bf16),
                   jax.ShapeDtypeStruct((T, 2 * D), bf16), jax.ShapeDtypeStruct((T, D), bf16),
                   jax.ShapeDtypeStruct((T, D), bf16), jax.ShapeDtypeStruct((T, D), bf16),
                   jax.ShapeDtypeStruct((T // tm * 8, 2 * D), f32)],
        args=(dh1, proj, proj, proj, proj, b_gate, yr, ya, w_ro, w_ao, w_out))


def _ret_bwd(proj, qr, kr, o, states, du, B, S, rope, decay, phase=None):
    T = B * S
    nc = S // CHUNK
    H, dk, dv = RET_HEADS, RET_KEY_DIM, RET_VAL_DIM

    sb = RET_CHUNKS * CHUNK
    ns = S // sb

    def body(qr_ref, kr_ref, v_ref, g_ref, o_ref, st_ref, du_ref, cos_ref, sin_ref, intra_ref, qd_ref, kd_ref, cd_ref,
             dp_ref, dstate_ref):
        dq_ref, dk_ref = dp_ref.at[:, pl.ds(C_RQ, H * dk)], dp_ref.at[:, pl.ds(C_RK, H * dk)]
        dv_ref, dg_ref = dp_ref.at[:, pl.ds(C_RV, H * dv)], dp_ref.at[:, pl.ds(C_RG, H * dv)]

        @pl.when(pl.program_id(1) == 0)
        def _():
            dstate_ref[...] = jnp.zeros_like(dstate_ref)

        cos, snb = cos_ref[...], -sin_ref[...]
        dstates = [dstate_ref[h] for h in range(H)]
        for ci in reversed(range(RET_CHUNKS)):
            r = slice(ci * CHUNK, (ci + 1) * CHUNK)
            for h in range(H):
                hk, hv = slice(h * dk, (h + 1) * dk), slice(h * dv, (h + 1) * dv)
                intra, qd, kd = intra_ref[h], qd_ref[h], kd_ref[h]
                qi, ki, vi = qr_ref[r, hk], kr_ref[r, hk], v_ref[r, hv]
                si = st_ref[0, h, ci]
                o = o_ref[r, hv].astype(f32)
                mu = jnp.mean(o, axis=-1, keepdims=True)
                xc = o - mu
                rstd = lax.rsqrt(jnp.mean(xc * xc, axis=-1, keepdims=True) + EPS)
                oh = xc * rstd
                g = g_ref[r, hv].astype(f32)
                sg = _sig(g)
                dui = du_ref[r, hv].astype(f32)
                dg_ref[r, hv] = (dui * oh * (sg * (1.0 + g * (1.0 - sg)))).astype(bf16)
                doh = dui * (g * sg)
                do = rstd * (doh - jnp.mean(doh, axis=-1, keepdims=True)
                             - oh * jnp.mean(doh * oh, axis=-1, keepdims=True))
                dob = do.astype(bf16)
                p = (_dot_nt(qi, ki) * intra).astype(bf16)
                dsb = dstates[h].astype(bf16)
                kt = (ki.astype(f32) * kd).astype(bf16)
                qt = (qi.astype(f32) * qd).astype(bf16)
                dv_ref[r, hv] = (_dot_tn(p, dob) + _dot(kt, dsb)).astype(bf16)
                da = (_dot_nt(dob, vi) * intra).astype(bf16)
                dq = _dot(da, ki) + _dot_nt(dob, si) * qd
                dkk = (_dot_tn(da, qi) + _dot_nt(vi, dsb) * kd) * K_SCALE
                dq_ref[r, hk] = _rotate(dq, cos[r], snb[r]).astype(bf16)
                dk_ref[r, hk] = _rotate(dkk, cos[r], snb[r]).astype(bf16)
                dstates[h] = dstates[h] * cd_ref[h] + _dot_tn(qt, dob)
        for h in range(H):
            dstate_ref[h] = dstates[h]

    blk = lambda w, c: pl.BlockSpec((sb, w), lambda b, i: (b * ns + ns - 1 - i, c))
    return _call(
        body, phase, name="ret_bwd", grid=(B, ns),
        in_specs=[blk(H * dk, 0), blk(H * dk, 0), blk(H * dv, C_RV // (H * dv)), blk(H * dv, C_RG // (H * dv)),
                  blk(H * dv, 0),
                  pl.BlockSpec((1, H, RET_CHUNKS, dk, dv), lambda b, i: (b, 0, ns - 1 - i, 0, 0)),
                  blk(H * dv, 0),
                  pl.BlockSpec((sb, dk), lambda b, i: (ns - 1 - i, 0)), pl.BlockSpec((sb, dk), lambda b, i: (ns - 1 - i, 0)),
                  *_ret_tables_specs()],
        out_specs=[blk(C_AQ, 0)], out_shape=[jax.ShapeDtypeStruct((T, N_IN), bf16)],
        scratch_shapes=[pltpu.VMEM((H, dk, dv), f32)],
        args=(qr, kr, proj, proj, o, states, du, *rope, *decay))


def _att_bwd(proj, dao, trows, dproj, B, S, phase=None):
    T = B * S
    nq = S // QBLK
    dh = ATT_HEAD_DIM
    scale = ATT_HEAD_DIM ** -0.5

    def body(q_ref, k_ref, v_ref, do_ref, t_ref, _, dp_ref, vec_ref, bias_ref, dbias_ref, dka_ref, dva_ref):
        b, i = pl.program_id(0), pl.program_id(1)

        @pl.when((b == 0) & (i == 0))
        def _():
            _build_bias(t_ref, bias_ref)
            dbias_ref[...] = jnp.zeros_like(dbias_ref)

        @pl.when(i == 0)
        def _():
            dka_ref[...] = jnp.zeros_like(dka_ref)
            dva_ref[...] = jnp.zeros_like(dva_ref)

        def step(nk):
            win = pl.ds(pl.multiple_of((i + 1) * QBLK - nk, QBLK), nk)
            kw, vw = k_ref[win, :], v_ref[win, :]
            first = _first_of_pair()
            first_rows = lax.broadcasted_iota(jnp.int32, (2 * dh, 1), 0) < dh
            dqs, dks, dvs = [], [], []
            for p in range(ATT_HEADS // 2):
                ps = slice(2 * p * dh, 2 * (p + 1) * dh)
                q2, k2, v2, do2 = q_ref[:, ps] * ATT_SCALE, kw[:, ps], vw[:, ps], do_ref[:, ps]
                dq2, dk2, dv2 = [], [], []
                for e in range(2):
                    h = 2 * p + e
                    mine = first == (e == 0)
                    pr = _att_probs(jnp.where(mine, q2, jnp.zeros_like(q2)), k2, bias_ref[h, :, KWIN - nk:])
                    dp = _dot_nt(jnp.where(mine, do2, jnp.zeros_like(do2)), v2)
                    ds = pr * (dp - jnp.sum(pr * dp, axis=-1, keepdims=True))
                    dbias_ref[h, :, KWIN - nk:] += ds
                    dsb = ds.astype(bf16)
                    dq2.append(_dot(dsb, k2) * ATT_SCALE)
                    dk2.append(_dot_tn(q2, dsb))
                    dv2.append(_dot_tn(do2, pr.astype(bf16)))
                dqs.append(jnp.where(first, dq2[0], dq2[1]))
                dks.append(jnp.where(first_rows, dk2[0], dk2[1]))
                dvs.append(jnp.where(first_rows, dv2[0], dv2[1]))
            dp_ref[pl.ds(pl.multiple_of(i * QBLK, QBLK), QBLK), :ATT_W] = jnp.concatenate(dqs, axis=1).astype(bf16)
            dka_ref[:, win] += jnp.concatenate(dks, axis=0)
            dva_ref[:, win] += jnp.concatenate(dvs, axis=0)

        _by_window(i, step)

        @pl.when(i == nq - 1)
        def _():
            dp_ref[:, ATT_W:2 * ATT_W] = dka_ref[...].T.astype(bf16)
            dp_ref[:, 2 * ATT_W:] = dva_ref[...].T.astype(bf16)

        @pl.when((b == B - 1) & (i == nq - 1))
        def _():
            rr = lax.broadcasted_iota(jnp.int32, (QBLK, QBLK), 0)
            cc = lax.broadcasted_iota(jnp.int32, (QBLK, QBLK), 1)
            flip = jnp.where(rr + cc == QBLK - 1, 1.0, 0.0).astype(bf16)
            for h in range(ATT_HEADS):
                d = dbias_ref[h]
                hi = d.astype(bf16)
                lo = (d - hi.astype(f32)).astype(bf16)
                rev = _dot(flip, hi) + _dot(flip, lo)
                wide = jnp.concatenate([rev, jnp.zeros((QBLK, TOEP - KWIN), f32)], axis=1)
                rolled = pltpu.roll(wide, 0, 1, stride=1, stride_axis=0)
                vec_ref[h:h + 1, :] = jnp.sum(rolled, axis=0, keepdims=True)

    qspec = lambda c: pl.BlockSpec((QBLK, ATT_W), lambda b, i: (b * nq + i, c))
    kspec = lambda c: pl.BlockSpec((S, ATT_W), lambda b, i: (b, c))
    return _call(
        body, phase, name="att_bwd", grid=(B, nq), aliases={5: 0},
        in_specs=[qspec(C_AQ // ATT_W), kspec(C_AK // ATT_W), kspec(C_AV // ATT_W), qspec(0),
                  pl.BlockSpec((ATT_HEADS, TOEP), lambda b, i: (0, 0)), pl.BlockSpec(memory_space=pl.ANY)],
        out_specs=[pl.BlockSpec((S, 3 * ATT_W), lambda b, i: (b, C_AQ // (3 * ATT_W))),
                   pl.BlockSpec((ATT_HEADS, TOEP), lambda b, i: (0, 0))],
        out_shape=[jax.ShapeDtypeStruct((T, N_IN), bf16), jax.ShapeDtypeStruct((ATT_HEADS, TOEP), f32)],
        scratch_shapes=[pltpu.VMEM((ATT_HEADS, QBLK, KWIN), f32), pltpu.VMEM((ATT_HEADS, QBLK, KWIN), f32),
                        pltpu.VMEM((ATT_W, S), f32), pltpu.VMEM((ATT_W, S), f32)],
        args=(proj, proj, proj, dao, trows, dproj))


def _in_proj_bwd(dproj, w_in, x2, gamma, dh1, phase=None):
    T, D = x2.shape
    nk, _, tk = w_in.shape
    tm = _tile(T, BIG_ROW_TILE, 8)

    def body(dp_ref, w_ref, x_ref, g_ref, dh1_ref, dx_ref, part_ref, acc_ref):
        j = pl.program_id(1)

        @pl.when(j == 0)
        def _():
            acc_ref[...] = jnp.zeros_like(acc_ref)

        acc_ref[...] += _dot_nt(dp_ref[...], w_ref[0])

        @pl.when(j == nk - 1)
        def _():
            x = x_ref[...]
            r = lax.rsqrt(jnp.mean(x * x, axis=-1, keepdims=True) + EPS)
            n = x * r
            dxn = acc_ref[...]
            dn = dxn * g_ref[...]
            dx_ref[...] = dh1_ref[...] + r * (dn - n * jnp.mean(dn * n, axis=-1, keepdims=True))
            part_ref[...] = jnp.zeros_like(part_ref)
            part_ref[0:1, :] = jnp.sum(dxn * n, axis=0, keepdims=True)

    row = lambda n: pl.BlockSpec((tm, n), lambda i, j: (i, 0))
    return _call(
        body, phase, name="in_proj_bwd", grid=(T // tm, nk),
        in_specs=[pl.BlockSpec((tm, tk), lambda i, j: (i, j)), pl.BlockSpec((1, D, tk), lambda i, j: (j, 0, 0)), row(D),
                  pl.BlockSpec((1, D), lambda i, j: (0, 0)), row(D)],
        out_specs=[row(D), pl.BlockSpec((8, D), lambda i, j: (i, 0))],
        out_shape=[jax.ShapeDtypeStruct((T, D), f32), jax.ShapeDtypeStruct((T // tm * 8, D), f32)],
        scratch_shapes=[pltpu.VMEM((tm, D), f32)],
        args=(dproj, w_in, x2, gamma, dh1))


def _wgrad(a, b, shard_axis, name, phase=None):
    def spec(arr, sharded, tt):
        if arr.ndim == 3:
            return arr.shape[2], pl.BlockSpec((1, tt, arr.shape[2]), lambda s, t: (s, t, 0))
        if sharded:
            w = arr.shape[1] // N_CHIPS
            return w, pl.BlockSpec((tt, w), lambda s, t: (t, s))
        return arr.shape[1], pl.BlockSpec((tt, arr.shape[1]), lambda s, t: (t, 0))

    T = a.shape[-2]
    tt = _tile(T, BIG_ROW_TILE, 16)
    nt = T // tt
    whole = a.ndim == 2 and b.ndim == 2 and a.shape[1] * b.shape[1] * 4 <= WGRAD_ACC_BYTES
    if whole:
        K, N = a.shape[1], b.shape[1]
        a_spec, b_spec = pl.BlockSpec((tt, K), lambda s, t: (t, 0)), pl.BlockSpec((tt, N), lambda s, t: (t, 0))
        out_block = (N_CHIPS, K // N_CHIPS, N) if shard_axis == 0 else (N_CHIPS, K, N // N_CHIPS)
        out_spec = pl.BlockSpec(out_block, lambda s, t: (0, 0, 0))
    else:
        K, a_spec = spec(a, shard_axis == 0, tt)
        N, b_spec = spec(b, shard_axis == 1, tt)
        out_block = (N_CHIPS, K, N)
        out_spec = pl.BlockSpec((1, K, N), lambda s, t: (s, 0, 0))

    def body(a_ref, b_ref, o_ref, acc_ref):
        t = pl.program_id(1)

        @pl.when(t == 0)
        def _():
            acc_ref[...] = jnp.zeros_like(acc_ref)

        av = a_ref[0] if a.ndim == 3 else a_ref[...]
        bv = b_ref[0] if b.ndim == 3 else b_ref[...]
        acc_ref[...] += _dot_tn(av.astype(bf16), bv.astype(bf16))

        @pl.when(t == nt - 1)
        def _():
            if not whole:
                o_ref[0] = acc_ref[...].astype(bf16)
            else:
                _, kk, nn = out_block
                for s in range(N_CHIPS):
                    o_ref[s] = (acc_ref[s * kk:(s + 1) * kk, :] if shard_axis == 0
                                else acc_ref[:, s * nn:(s + 1) * nn]).astype(bf16)

    (grad,), carried = _call(
        body, phase, name=name, grid=(1 if whole else N_CHIPS, nt), in_specs=[a_spec, b_spec], out_specs=[out_spec],
        out_shape=[jax.ShapeDtypeStruct(out_block, bf16)], scratch_shapes=[pltpu.VMEM((K, N), f32)], args=(a, b))
    return grad, carried


def _adamw_sum(place, w, m, v, part, from_chips, from_sibling, name):
    R, C = w.shape
    half = R // 2
    tr = _tile(half, max(16, (1 << 18) // C // 16 * 16), 16)
    nr = half // tr

    def body(p_ref, w_ref, m_ref, v_ref, part_ref, fc_ref, fs_ref, g_ref, d_ref, mo_ref, vo_ref):
        up = lambda x: x.astype(f32)
        mine = ((up(part_ref[0]) + up(fc_ref[0])) + up(fc_ref[1])) + up(fc_ref[2])
        sibs = ((up(fs_ref[0]) + up(fs_ref[1])) + up(fs_ref[2])) + up(fs_ref[3])
        g_ = jnp.where(pl.program_id(0) == p_ref[0], mine, sibs)
        m_ = ADAM_B1 * m_ref[...] + (1.0 - ADAM_B1) * g_
        v_ = ADAM_B2 * v_ref[...] + (1.0 - ADAM_B2) * (g_ * g_)
        m_hat = m_ / (1.0 - ADAM_B1 ** ADAM_STEP)
        v_hat = v_ / (1.0 - ADAM_B2 ** ADAM_STEP)
        g_ref[...] = g_
        d_ref[...] = -ADAM_LR * (m_hat / (jnp.sqrt(v_hat) + ADAM_EPS) + ADAM_WD * w_ref[...])
        mo_ref[...] = m_
        vo_ref[...] = v_

    spec = pl.BlockSpec((tr, C), lambda h, r, p: (h * nr + r, 0))
    return pl.pallas_call(
        body, name=name,
        grid_spec=pltpu.PrefetchScalarGridSpec(
            num_scalar_prefetch=1, grid=(2, nr),
            in_specs=[spec, spec, spec, pl.BlockSpec((1, tr, C), lambda h, r, p: (p[1], jnp.where(h == p[0], r, 0), 0)),
                      pl.BlockSpec((3, tr, C), lambda h, r, p: (0, jnp.where(h == p[0], r, 0), 0)),
                      pl.BlockSpec((4, tr, C), lambda h, r, p: (0, jnp.where(h == p[0], 0, r), 0))],
            out_specs=[spec] * 4),
        out_shape=[jax.ShapeDtypeStruct((R, C), f32)] * 4,
        compiler_params=_params(("parallel", "parallel")),
    )(place, w, m, v, part, from_chips, from_sibling)


def _adamw(w, g, m, v, name):
    R, C = w.shape
    tr = _tile(R, max(8, (1 << 18) // C // 8 * 8), 8)

    def body(w_ref, g_ref, m_ref, v_ref, d_ref, mo_ref, vo_ref):
        g_ = g_ref[...]
        m_ = ADAM_B1 * m_ref[...] + (1.0 - ADAM_B1) * g_
        v_ = ADAM_B2 * v_ref[...] + (1.0 - ADAM_B2) * (g_ * g_)
        m_hat = m_ / (1.0 - ADAM_B1 ** ADAM_STEP)
        v_hat = v_ / (1.0 - ADAM_B2 ** ADAM_STEP)
        d_ref[...] = -ADAM_LR * (m_hat / (jnp.sqrt(v_hat) + ADAM_EPS) + ADAM_WD * w_ref[...])
        mo_ref[...] = m_
        vo_ref[...] = v_

    spec = pl.BlockSpec((tr, C), lambda i: (i, 0))
    return pl.pallas_call(
        body, name=name, grid=(R // tr,), in_specs=[spec] * 4, out_specs=[spec] * 3,
        out_shape=[jax.ShapeDtypeStruct((R, C), f32)] * 3,
        compiler_params=_params(("parallel",)),
    )(w, g, m, v)


def _place():
    return lax.axis_index("x"), lax.axis_index("y"), lax.axis_index("c")


def _other_chips(x, y):
    chips = [(1 - x, y), (x, 1 - y), (1 - x, 1 - y)]
    return chips, [2 * cx + cy for cx, cy in chips]


def _spread_phase(blk):
    def peers():
        x, y, c = _place()
        return [tuple(1 - p if (k >> s) & 1 else p for p, s in ((x, 2), (y, 1), (c, 0))) for k in range(1, N_DEV)]

    def copies(pin, out):
        x, y, c = _place()
        mine = out[0].at[4 * x + 2 * y + c]
        return [(mine, mine, peer) for peer in peers()]

    stack = jnp.broadcast_to(blk, (N_DEV,) + blk.shape)
    return _Phase([stack], [jax.ShapeDtypeStruct(stack.shape, stack.dtype)], {0: 0}, N_DEV - 1, copies,
                  lambda pin, out: [out[0].at[4 * px + 2 * py + pc] for px, py, pc in peers()])


def _sum_slots(stack, name):
    def body(s_ref, o_ref):
        tot = s_ref[0]
        for d in range(1, stack.shape[0]):
            tot = tot + s_ref[d]
        o_ref[...] = tot

    vm = pl.BlockSpec(memory_space=pltpu.VMEM)
    return pl.pallas_call(body, name=name, in_specs=[vm], out_specs=vm,
                          out_shape=jax.ShapeDtypeStruct(stack.shape[1:], stack.dtype))(stack)


def _cast_shard(place, w, name):
    R, C = w.shape
    tr = _tile(R, max(16, (1 << 19) // C // 16 * 16), 16)

    def body(p_ref, w_ref, o_ref):
        o_ref[0] = w_ref[...].astype(bf16)

    return pl.pallas_call(
        body, name=name,
        grid_spec=pltpu.PrefetchScalarGridSpec(
            num_scalar_prefetch=1, grid=(R // tr,),
            in_specs=[pl.BlockSpec((tr, C), lambda r, p: (r, 0))],
            out_specs=pl.BlockSpec((1, tr, C), lambda r, p: (p[1], r, 0))),
        out_shape=jax.ShapeDtypeStruct((N_CHIPS, R, C), bf16),
        compiler_params=_params(("parallel",)),
    )(place, w)


class _Phase:
    def __init__(self, arrays, out_shapes, aliases, n_copies, copies, arrivals, own_starts=(), own_waits=()):
        self.arrays, self.out_shapes, self.aliases = list(arrays), list(out_shapes), dict(aliases)
        self.n_copies, self.copies, self.arrivals = n_copies, copies, arrivals
        self.own_starts, self.own_waits = tuple(own_starts), tuple(own_waits)

    def sems(self):
        return [pltpu.SemaphoreType.DMA((self.n_copies,)), pltpu.SemaphoreType.DMA((self.n_copies,))]

    def _descriptors(self, pin, pout, send_sems, recv_sems):
        return [pltpu.make_async_remote_copy(src_ref=s, dst_ref=d, send_sem=send_sems.at[i], recv_sem=recv_sems.at[i],
                                             device_id=to, device_id_type=MESH)
                for i, (s, d, to) in enumerate(self.copies(pin, pout))]

    def _arrival(self, i, pin, pout, send_sems, recv_sems):
        dst = self.arrivals(pin, pout)[i]
        return pltpu.make_async_remote_copy(src_ref=dst, dst_ref=dst, send_sem=send_sems.at[i], recv_sem=recv_sems.at[i],
                                            device_id=_place(), device_id_type=MESH)

    def start(self, pin, pout, send_sems, recv_sems):
        for i, cp in enumerate(self._descriptors(pin, pout, send_sems, recv_sems)):
            if i not in self.own_starts:
                cp.start()

    def begin(self, i, pin, pout, send_sems, recv_sems):
        self._descriptors(pin, pout, send_sems, recv_sems)[i].start()

    def arrived(self, i, pin, pout, send_sems, recv_sems):
        self._arrival(i, pin, pout, send_sems, recv_sems).wait_recv()

    def finish(self, pin, pout, send_sems, recv_sems):
        for i in range(self.n_copies):
            if i not in self.own_waits:
                self._arrival(i, pin, pout, send_sems, recv_sems).wait_recv()
        for cp in self._descriptors(pin, pout, send_sems, recv_sems):
            cp.wait_send()


def _join(phases):
    if len(phases) == 1:
        return phases[0]
    ai = np.cumsum([0] + [len(p.arrays) for p in phases])
    oi = np.cumsum([0] + [len(p.out_shapes) for p in phases])

    def each(fn_name, pin, pout):
        return [item for k, p in enumerate(phases)
                for item in getattr(p, fn_name)(pin[ai[k]:ai[k + 1]], pout[oi[k]:oi[k + 1]])]

    aliases = {int(ai[k]) + i: int(oi[k]) + j for k, p in enumerate(phases) for i, j in p.aliases.items()}
    ci = np.cumsum([0] + [p.n_copies for p in phases])
    shifted = lambda attr: [int(ci[k]) + i for k, p in enumerate(phases) for i in getattr(p, attr)]
    return _Phase([a for p in phases for a in p.arrays], [s for p in phases for s in p.out_shapes], aliases,
                  int(ci[-1]), functools.partial(each, "copies"), functools.partial(each, "arrivals"),
                  shifted("own_starts"), shifted("own_waits"))


def _call(body, phase, *, name, grid, in_specs, out_specs, out_shape, scratch_shapes, args, prefetch=(), expose=False,
          aliases=None):
    seq = _params(("arbitrary",) * len(grid))
    np_ = len(prefetch)
    own = {np_ + i: j for i, j in (aliases or {}).items()}
    if phase is None:
        spec = pltpu.PrefetchScalarGridSpec(num_scalar_prefetch=np_, grid=grid, in_specs=in_specs, out_specs=out_specs,
                                            scratch_shapes=scratch_shapes)
        res = pl.pallas_call(body, name=name, grid_spec=spec, out_shape=out_shape, input_output_aliases=own,
                             compiler_params=seq)(*prefetch, *args)
        return list(res), []
    ni, no, ns = len(in_specs), len(out_specs), len(scratch_shapes)
    pi, po = len(phase.arrays), len(phase.out_shapes)

    def hosted(*refs):
        cut = np.cumsum([np_, ni, pi, no, po, ns])
        pre, ins, pin, outs, pout, scr, sems = (refs[a:b] for a, b in zip([0, *cut], [*cut, len(refs)]))
        ids = [pl.program_id(d) for d in range(len(grid))]
        first = functools.reduce(lambda p, q: p & q, [i == 0 for i in ids])
        last = functools.reduce(lambda p, q: p & q, [i == g - 1 for i, g in zip(ids, grid)])
        pl.when(first)(lambda: phase.start(pin, pout, *sems))
        body(*pre, *ins, *outs, *scr, **({"carried": (pin, pout, sems)} if expose else {}))
        pl.when(last)(lambda: phase.finish(pin, pout, *sems))

    anyspace = pl.BlockSpec(memory_space=pl.ANY)
    spec = pltpu.PrefetchScalarGridSpec(
        num_scalar_prefetch=np_, grid=grid, in_specs=list(in_specs) + [anyspace] * pi,
        out_specs=list(out_specs) + [anyspace] * po, scratch_shapes=list(scratch_shapes) + phase.sems())
    res = pl.pallas_call(
        hosted, name=name, grid_spec=spec, out_shape=list(out_shape) + phase.out_shapes,
        input_output_aliases={**own, **{np_ + ni + i: no + j for i, j in phase.aliases.items()}}, compiler_params=seq,
    )(*prefetch, *args, *phase.arrays)
    return list(res[:no]), list(res[no:])


def _run_phases(name, phases):
    first = phases[0]
    pi, po = len(first.arrays), len(first.out_shapes)

    def body(*refs):
        pin, pout, sems = refs[:pi], refs[pi:pi + po], refs[pi + po:]
        for n, ph in enumerate(phases):
            ph.start(pin, pout, *sems[2 * n:2 * n + 2])
            ph.finish(pin, pout, *sems[2 * n:2 * n + 2])

    anyspace = pl.BlockSpec(memory_space=pl.ANY)
    return list(pl.pallas_call(
        body, name=name, in_specs=[anyspace] * pi, out_specs=[anyspace] * po, out_shape=first.out_shapes,
        input_output_aliases=first.aliases, scratch_shapes=[s for ph in phases for s in ph.sems()],
    )(*first.arrays))


def _half_rows(buf, c):
    half = buf.shape[1] // 2
    return pl.ds(c * half, half), pl.ds((1 - c) * half, half)


def _gather_phase(bufs, over_ici):
    n = len(bufs)
    shapes = [jax.ShapeDtypeStruct(b.shape, b.dtype) for b in bufs]

    def landed(out, which):
        x, y, c = _place()
        _, ks = _other_chips(x, y)
        return [out[a].at[ks[j], _half_rows(bufs[a], c)[which]] for a in range(n) for j in range(3)]

    def ici(pin, out):
        x, y, c = _place()
        chips, _ = _other_chips(x, y)
        mine = [out[a].at[2 * x + y, _half_rows(bufs[a], c)[0]] for a in range(n)]
        return [(mine[a], mine[a], (*chips[j], c)) for a in range(n) for j in range(3)]

    def d2d(pin, out):
        x, y, c = _place()
        return [(dst, dst, (x, y, 1 - c)) for dst in landed(out, 0)]

    if over_ici:
        return _Phase(bufs, shapes, {a: a for a in range(n)}, 3 * n, ici, lambda pin, out: landed(out, 0))
    return _Phase(bufs, shapes, {a: a for a in range(n)}, 3 * n, d2d, lambda pin, out: landed(out, 1))


def _feed_phase(buf):
    def chips():
        x, y, c = _place()
        return [(x if f < 2 else 1 - x, y if f % 2 == 0 else 1 - y) for f in (1, 2, 3)]

    def copies(pin, out):
        x, y, c = _place()
        mine = _half_rows(buf, c)[0]
        own = out[0].at[2 * x + y, mine]
        sent = [(own, own, (cx, cy, c)) for cx, cy in chips()]
        return sent + [(out[0].at[2 * cx + cy, mine], out[0].at[2 * cx + cy, mine], (x, y, 1 - c)) for cx, cy in chips()]

    def arrivals(pin, out):
        x, y, c = _place()
        mine, theirs = _half_rows(buf, c)
        return [out[0].at[2 * cx + cy, rows] for rows in (mine, theirs) for cx, cy in chips()]

    return _Phase([buf], [jax.ShapeDtypeStruct(buf.shape, buf.dtype)], {0: 0}, 6, copies, arrivals,
                  own_starts=(3, 4, 5), own_waits=range(6))


def _rs_sibling(grads, name):
    n = len(grads)

    def body(*refs):
        g, out, send_sems, recv_sems = refs[:n], refs[n:2 * n], refs[2 * n], refs[2 * n + 1]
        x, y, c = _place()
        copies = []
        for a in range(n):
            half = grads[a].shape[1] // 2
            cp = pltpu.make_async_remote_copy(src_ref=g[a].at[:, pl.ds((1 - c) * half, half)], dst_ref=out[a],
                                              send_sem=send_sems.at[a], recv_sem=recv_sems.at[a],
                                              device_id=(x, y, 1 - c), device_id_type=MESH)
            cp.start()
            copies.append(cp)
        for cp in copies:
            cp.wait()

    anyspace = pl.BlockSpec(memory_space=pl.ANY)
    return pl.pallas_call(
        body, name=name, in_specs=[anyspace] * n, out_specs=[anyspace] * n,
        out_shape=[jax.ShapeDtypeStruct((N_CHIPS, g.shape[1] // 2, g.shape[2]), g.dtype) for g in grads],
        scratch_shapes=[pltpu.SemaphoreType.DMA((n,)), pltpu.SemaphoreType.DMA((n,))],
    )(*grads)


def _rs_add_sibling(place, grad, got, name):
    _, R, C = grad.shape
    half = R // 2
    tr = _tile(half, max(16, (1 << 19) // C // 16 * 16), 16)
    nr = half // tr

    def body(p_ref, a_ref, b_ref, o_ref):
        o_ref[...] = (a_ref[...].astype(f32) + b_ref[...].astype(f32)).astype(o_ref.dtype)

    return pl.pallas_call(
        body, name=name,
        grid_spec=pltpu.PrefetchScalarGridSpec(
            num_scalar_prefetch=1, grid=(N_CHIPS, nr),
            in_specs=[pl.BlockSpec((1, tr, C), lambda k, r, p: (k, p[0] * nr + r, 0)),
                      pl.BlockSpec((1, tr, C), lambda k, r, p: (k, r, 0))],
            out_specs=pl.BlockSpec((1, tr, C), lambda k, r, p: (k, r, 0))),
        out_shape=jax.ShapeDtypeStruct((N_CHIPS, half, C), bf16),
        compiler_params=_params(("parallel", "parallel")),
    )(place, grad, got)


def _rs_chips_phase(parts):
    n = len(parts)

    def copies(p, fc):
        x, y, c = _place()
        chips, ks = _other_chips(x, y)
        return [(p[a].at[ks[j]], fc[a].at[j], (*chips[j], c)) for a in range(n) for j in range(3)]

    shapes = [jax.ShapeDtypeStruct((3,) + q.shape[1:], q.dtype) for q in parts]
    return _Phase(parts, shapes, {}, 3 * n, copies, lambda p, fc: [fc[a].at[j] for a in range(n) for j in range(3)])


def _rs_hand_phase(parts, from_chips):
    n = len(parts)

    def copies(pin, fs):
        x, y, c = _place()
        sib = (x, y, 1 - c)
        own = [(pin[a].at[2 * x + y], fs[a].at[0], sib) for a in range(n)]
        return own + [(pin[n + a].at[j], fs[a].at[1 + j], sib) for a in range(n) for j in range(3)]

    def arrivals(pin, fs):
        return [fs[a].at[0] for a in range(n)] + [fs[a].at[1 + j] for a in range(n) for j in range(3)]

    shapes = [jax.ShapeDtypeStruct((4,) + q.shape[1:], q.dtype) for q in parts]
    return _Phase(list(parts) + list(from_chips), shapes, {}, 4 * n, copies, arrivals)


class _Exchange:
    def __init__(self, place):
        self.place = place

    def feed(self, buf):
        return _feed_phase(buf)

    def gather(self, bufs, over_ici):
        return _gather_phase(bufs, over_ici)

    def pair_sums(self, names, grads):
        got = _rs_sibling(grads, "rs_sibling_" + names[0])
        return [_rs_add_sibling(self.place, g, r, "rs_add_" + n) for n, g, r in zip(names, grads, got)]

    def to_chips(self, parts):
        return _rs_chips_phase(parts)

    def to_sibling(self, parts, from_chips):
        return _rs_hand_phase(parts, from_chips)

    def spread(self, blk):
        return _spread_phase(blk)

    def hand_over(self, name, parts, from_chips, blk):
        got = _run_phases(name, [_join([_rs_hand_phase(parts, from_chips), _spread_phase(blk)])])
        return got[:-1], got[-1]


def _local_step(place, x, target, norm_mix, b_gate, rb_chip, norm_ffn, norm_final, w_in, rest, exch):
    B, S, D = x.shape
    T = B * S
    x2 = x.reshape(T, D)
    tg2 = target.reshape(T, D)
    rope, decay = _rope_tables(S), _decay_tables()
    g_fin = norm_final.reshape(1, D)
    nrel = rb_chip.shape[-1]

    mrg, ffn = ["w_ret_out", "w_att_out", "w_out"], ["w_ffn_gate", "w_ffn_up", "w_ffn_down"]
    (xn, proj), got = _in_proj(place, x2, norm_mix, _join([exch.feed(w_in), exch.gather([rest[n] for n in mrg], True),
                                                           exch.spread(jnp.pad(rb_chip, ((0, 0), (0, 128 - nrel))))]))
    w_in, wb, rb_all = got[0], {}, got.pop()
    trows = _bias_rows(jnp.concatenate([rb_all[2 * k, :, :nrel] for k in range(N_CHIPS)], axis=1))
    (qr, kr, o, u, states), got = _ret_fwd(proj, B, S, rope, decay, _join([exch.gather([rest["w_ffn_gate"]], True),
                                                                         exch.gather(got[1:], False)]))
    wb.update(zip(mrg, got[1:]))
    (ao,), got = _att_fwd(proj, trows, B, S, _join([exch.gather([rest["w_ffn_up"], rest["w_ffn_down"]], True),
                                                    exch.gather(got[:1], False)]))
    wb["w_ffn_gate"] = got[2]
    w_ro, w_out = wb["w_ret_out"].reshape(-1, D), wb["w_out"].reshape(-1, D)
    (h1, yr, ya), got = _mix_fwd(x2, proj, u, ao, b_gate, w_ro, wb["w_att_out"], w_out, exch.gather(got[:2], False))
    wb.update(zip(ffn[1:], got))
    hn, a, b, f, dh2, part_fin = _ffn_fwd(h1, norm_ffn, wb["w_ffn_gate"], wb["w_ffn_up"], wb["w_ffn_down"], g_fin, tg2)

    da, db, dh1, part_ffn = _ffn_bwd(dh2, h1, norm_ffn, a, b, wb["w_ffn_gate"], wb["w_ffn_up"], wb["w_ffn_down"])
    ffn = ["w_ffn_down", "w_ffn_gate", "w_ffn_up"]
    p_ffn = exch.pair_sums(ffn, [_wgrad(f, dh2, 0, "wgrad_ffn_down")[0], _wgrad(da, hn, 0, "wgrad_ffn_gate")[0],
                                 _wgrad(db, hn, 0, "wgrad_ffn_up")[0]])
    (du, dao, dgl, mix, dyr, dya, part_bg), c_down = _mix_bwd(dh1, proj, yr, ya, b_gate, w_ro, wb["w_att_out"], w_out,
                                                               exch.to_chips(p_ffn[:1]))
    mrg = ["w_out", "w_ret_out", "w_att_out"]
    p_mrg = exch.pair_sums(mrg, [_wgrad(mix, dh1, 0, "wgrad_out")[0], _wgrad(u, dyr, 0, "wgrad_ret_out")[0],
                                 _wgrad(ao, dya, 1, "wgrad_att_out")[0]])
    (dproj,), c_gate_up = _ret_bwd(proj, qr, kr, o, states, du, B, S, rope, decay, exch.to_chips(p_ffn[1:]))
    c_ffn = c_down + c_gate_up
    (dproj, dvec), got = _att_bwd(proj, dao, trows, dproj, B, S, _join([exch.to_chips(p_mrg),
                                                                        exch.to_sibling(p_ffn, c_ffn)]))
    c_mrg, s_ffn = got[:len(mrg)], got[len(mrg):]
    dproj = lax.dynamic_update_slice(dproj, dgl, (0, C_GL))
    g_in, s_mrg = _wgrad(xn, dproj, 1, "wgrad_in", exch.to_sibling(p_mrg, c_mrg))
    p_in = exch.pair_sums(["w_in"], [g_in])
    (gx, part_mix), c_in = _in_proj_bwd(dproj, w_in, x2, norm_mix, dh1, exch.to_chips(p_in))
    rows = lambda p, r: p.reshape(-1, 8, p.shape[-1])[:, r, :].sum(axis=0)
    lo = KWIN - 1 - (MAX_REL - 1)
    drb = jnp.concatenate([jnp.flip(dvec[:, lo:lo + N_REL - 1], axis=1), dvec[:, :lo].sum(axis=1, keepdims=True)], axis=1)
    gsmall = {
        "norm_mix": rows(part_mix, 0), "b_gate": rows(part_bg, 0), "rel_bias": drb, "norm_ffn": rows(part_ffn, 0),
        "norm_final": rows(part_fin, 0),
    }
    s_in, small_all = exch.hand_over("rs_hand_w_in", p_in, c_in, _pack_small(gsmall, rows(part_fin, 1)))
    gbig = dict(zip(ffn + mrg + ["w_in"], zip(p_ffn + p_mrg + p_in, c_ffn + c_mrg + c_in, s_ffn + s_mrg + s_in)))
    return gx.reshape(B, S, D), gbig, small_all


SMALL_ROWS = 16


def _pack_small(gs, loss_lanes):
    D = D_MODEL
    rb = jnp.pad(gs["rel_bias"].reshape(-1), (0, 3 * D - ATT_HEADS * N_REL)).reshape(3, D)
    rows = [gs["norm_mix"].reshape(1, D), gs["b_gate"].reshape(2, D), gs["norm_ffn"].reshape(1, D),
            gs["norm_final"].reshape(1, D), rb, loss_lanes.reshape(1, D)]
    used = sum(r.shape[0] for r in rows)
    return jnp.concatenate(rows + [jnp.zeros((SMALL_ROWS - used, D), f32)], axis=0)


def kernel(x, norm_mix, w_in, b_gate, rel_bias, w_ret_out, w_att_out, w_out, norm_ffn, w_ffn_gate, w_ffn_up, w_ffn_down, norm_final, loss_target, m_norm_mix, m_w_in, m_b_gate, m_rel_bias, m_w_ret_out, m_w_att_out, m_w_out, m_norm_ffn, m_w_ffn_gate, m_w_ffn_up, m_w_ffn_down, m_norm_final, v_norm_mix, v_w_in, v_b_gate, v_rel_bias, v_w_ret_out, v_w_att_out, v_w_out, v_norm_ffn, v_w_ffn_gate, v_w_ffn_up, v_w_ffn_down, v_norm_final):
    w = dict(norm_mix=norm_mix, w_in=w_in, b_gate=b_gate, rel_bias=rel_bias, w_ret_out=w_ret_out, w_att_out=w_att_out,
             w_out=w_out, norm_ffn=norm_ffn, w_ffn_gate=w_ffn_gate, w_ffn_up=w_ffn_up, w_ffn_down=w_ffn_down,
             norm_final=norm_final)
    m = dict(norm_mix=m_norm_mix, w_in=m_w_in, b_gate=m_b_gate, rel_bias=m_rel_bias, w_ret_out=m_w_ret_out,
             w_att_out=m_w_att_out, w_out=m_w_out, norm_ffn=m_norm_ffn, w_ffn_gate=m_w_ffn_gate, w_ffn_up=m_w_ffn_up,
             w_ffn_down=m_w_ffn_down, norm_final=m_norm_final)
    v = dict(norm_mix=v_norm_mix, w_in=v_w_in, b_gate=v_b_gate, rel_bias=v_rel_bias, w_ret_out=v_w_ret_out,
             w_att_out=v_w_att_out, w_out=v_w_out, norm_ffn=v_norm_ffn, w_ffn_gate=v_w_ffn_gate, w_ffn_up=v_w_ffn_up,
             w_ffn_down=v_w_ffn_down, norm_final=v_norm_final)
    xi, yi, ci = _place()
    k_me = 2 * xi + yi

    place = jnp.stack([ci, k_me]).astype(jnp.int32)
    big = [n for n, _ in BIG]

    turned = ("w_ffn_gate", "w_ffn_up")
    shard = lambda d, n: jnp.swapaxes(d[n][0], 0, 1) if n in turned else d[n][0]
    whole = lambda a, n: (jnp.swapaxes(a, 0, 1) if n in turned else a)[None]

    bufs = {n: _cast_shard(place, shard(w, n), "cast_" + n) for n in big}
    rest = {n: bufs[n] for n in big if n != "w_in"}
    nrel_loc = rel_bias.shape[-1]
    grad_x, gbig, small_all = _local_step(place, x, loss_target, norm_mix, b_gate, rel_bias[0], norm_ffn, norm_final,
                                          bufs["w_in"], rest, _Exchange(place))

    small = _sum_slots(small_all, "reduce_small")
    D = D_MODEL
    loss = jnp.sum(small[8])
    drb_full = small[5:8].reshape(-1)[:ATT_HEADS * N_REL].reshape(ATT_HEADS, N_REL)
    g = {
        "norm_mix": small[0:1], "b_gate": small[1:3].reshape(1, 2 * D), "norm_ffn": small[3:4], "norm_final": small[4],
        "rel_bias": lax.dynamic_slice_in_dim(drb_full, k_me * nrel_loc, nrel_loc, axis=1)[None],
    }

    delta, new_m, new_v = {}, {}, {}
    for n in big:
        g_, d_, m_, v_ = _adamw_sum(place, shard(w, n), shard(m, n), shard(v, n), *gbig[n], "adamw_" + n)
        g[n], delta[n], new_m[n], new_v[n] = whole(g_, n), whole(d_, n), whole(m_, n), whole(v_, n)
    flat = lambda d: jnp.concatenate([d[n].reshape(-1) for n in SMALL])
    n_small = sum(int(np.prod(w[n].shape)) for n in SMALL)
    n_pad = -n_small % 1024
    packs = [jnp.pad(flat(d), (0, n_pad)).reshape(-1, 128) for d in (w, g, m, v)]
    outs = _adamw(*packs, "adamw_small")
    for res, dst in zip(outs, (delta, new_m, new_v)):
        off = 0
        fl = res.reshape(-1)
        for n in SMALL:
            sz = int(np.prod(w[n].shape))
            dst[n] = fl[off:off + sz].reshape(w[n].shape)
            off += sz

    return (loss, grad_x, *[g[n] for n in WEIGHTS], *[delta[n] for n in WEIGHTS], *[new_m[n] for n in WEIGHTS],
            *[new_v[n] for n in WEIGHTS])
```

```python
import functools

import numpy as np
import jax
import jax.numpy as jnp
from jax import lax
from jax.experimental import pallas as pl
from jax.experimental.pallas import tpu as pltpu

f32 = jnp.float32
bf16 = jnp.bfloat16

D_MODEL = 1024
CHUNK = 64
RET_HEADS = 4
RET_KEY_DIM = 128
RET_VAL_DIM = 256
ATT_HEADS = 8
ATT_HEAD_DIM = 64
ATT_W = ATT_HEADS * ATT_HEAD_DIM
BAND_CHUNKS = 8
PAD = BAND_CHUNKS * CHUNK
MAX_REL = 256
N_REL = CHUNK + MAX_REL
D_FF = 2816
N_IN = 6656
ROPE_BASE = 10000.0
EPS = 1e-6
NEG_INF = -1e30
C_RQ, C_RK, C_RV, C_RG, C_AQ, C_AK, C_AV, C_GL = 0, 512, 1024, 2048, 3072, 3584, 4096, 4608

ADAM_LR, ADAM_B1, ADAM_B2, ADAM_EPS, ADAM_WD, ADAM_STEP = 0.001, 0.9, 0.999, 1e-08, 0.01, 10

N_CHIPS = 4
N_DEV = 8
WGRAD_ACC_BYTES = 8 * 1024 * 1024
ROW_TILE = 512
BIG_ROW_TILE = 1024
IN_ORDER = (0, 2, 3, 1)
QBLK = 256
KWIN = PAD + QBLK
TOEP = 1024
VMEM_LIMIT = 56 * 1024 * 1024
MESH = pl.DeviceIdType.MESH

BIG = (
    ("w_in", 1), ("w_ret_out", 0), ("w_att_out", 1), ("w_out", 0), ("w_ffn_gate", 1), ("w_ffn_up", 1), ("w_ffn_down", 0))
WEIGHTS = ("norm_mix", "w_in", "b_gate", "rel_bias", "w_ret_out", "w_att_out", "w_out", "norm_ffn", "w_ffn_gate",
           "w_ffn_up", "w_ffn_down", "norm_final")
SMALL = ("norm_mix", "b_gate", "rel_bias", "norm_ffn", "norm_final")


def _dot(a, b):
    return lax.dot_general(a, b, (((1,), (0,)), ((), ())), preferred_element_type=f32)


def _dot_nt(a, b):
    return lax.dot_general(a, b, (((1,), (1,)), ((), ())), preferred_element_type=f32)


def _dot_tn(a, b):
    return lax.dot_general(a, b, (((0,), (0,)), ((), ())), preferred_element_type=f32)


def _sig(x):
    return 1.0 / (1.0 + jnp.exp(-x))


def _tile(n, pref, mult):
    best = None
    for t in range(mult, min(n, pref) + 1, mult):
        if n % t == 0:
            best = t
    return best if best is not None else n


def _same_shape(arrays):
    groups = {}
    for i, a in enumerate(arrays):
        groups.setdefault(a.shape, []).append(i)
    return list(groups.values())


def _params(sem, vmem=VMEM_LIMIT):
    return pltpu.CompilerParams(dimension_semantics=sem, vmem_limit_bytes=vmem)


def _in_proj(place, x2, gamma, phase):
    T, D = x2.shape
    _, _, ns = phase.arrays[0].shape
    tm = _tile(T, BIG_ROW_TILE, 8)
    ni = T // tm
    pass_chip = lambda j: sum(jnp.where(j == n, f, 0) for n, f in enumerate(IN_ORDER))

    def body(p_ref, x_ref, g_ref, xn_ref, pr_ref, xs_ref, w_ref, w_sem, carried):
        j, i = pl.program_id(0), pl.program_id(1)
        pin, pout, sems = carried
        rows = pl.ds(pl.multiple_of(i * tm, tm), tm)

        @pl.when(i == 0)
        def _():
            for n, f in enumerate(IN_ORDER):
                if f:
                    @pl.when(j == n)
                    def _():
                        phase.arrived(f - 1, pin, pout, *sems)
                        phase.begin(2 + f, pin, pout, *sems)
                        phase.arrived(2 + f, pin, pout, *sems)
            shard = pltpu.make_async_copy(pout[0].at[jnp.bitwise_xor(p_ref[1], pass_chip(j))], w_ref, w_sem)
            shard.start()
            shard.wait()

        @pl.when(j == 0)
        def _():
            x = x_ref[...]
            r = lax.rsqrt(jnp.mean(x * x, axis=-1, keepdims=True) + EPS)
            xn = (x * r * g_ref[...]).astype(bf16)
            xs_ref[rows, :] = xn
            xn_ref[...] = xn

        pr_ref[...] = _dot(xs_ref[rows, :], w_ref[...]).astype(bf16)

    first_pass = lambda j, i, p: (jnp.where(j == 0, i, ni - 1), 0)
    return _call(
        body, phase, name="in_proj", grid=(N_CHIPS, ni), prefetch=(place,), expose=True,
        in_specs=[pl.BlockSpec((tm, D), first_pass), pl.BlockSpec((1, D), lambda j, i, p: (0, 0))],
        out_specs=[pl.BlockSpec((tm, D), first_pass),
                   pl.BlockSpec((tm, ns), lambda j, i, p: (i, jnp.bitwise_xor(p[1], pass_chip(j))))],
        out_shape=[jax.ShapeDtypeStruct((T, D), bf16), jax.ShapeDtypeStruct((T, N_CHIPS * ns), bf16)],
        scratch_shapes=[pltpu.VMEM((T, D), bf16), pltpu.VMEM((D, ns), bf16), pltpu.SemaphoreType.DMA],
        args=(x2, gamma))


def _rope_tables(S):
    d = RET_KEY_DIM
    freqs = ROPE_BASE ** (-jnp.arange(0, d, 2, dtype=f32) / d)
    ang = jnp.arange(S, dtype=f32)[:, None] * freqs[None, :]
    cos, sin = jnp.cos(ang), jnp.sin(ang)
    return jnp.concatenate([cos, cos], axis=1), jnp.concatenate([-sin, sin], axis=1)


def _decay_tables():
    H = RET_HEADS
    log_g = jnp.log(1.0 - 2.0 ** (-5.0 - jnp.arange(H, dtype=f32)))
    p = jnp.arange(CHUNK, dtype=f32)
    intra = jnp.exp(log_g[:, None, None] * jnp.abs(p[:, None] - p[None, :]))
    q_dec = jnp.exp(log_g[:, None] * (p[None, :] + 1.0))
    k_dec = jnp.exp(log_g[:, None] * (CHUNK - 1.0 - p[None, :]))
    c_dec = jnp.exp(log_g * CHUNK)
    q_dec = jnp.broadcast_to(q_dec[:, :, None], (H, CHUNK, RET_KEY_DIM))
    k_dec = jnp.broadcast_to(k_dec[:, :, None], (H, CHUNK, RET_KEY_DIM))
    c_dec = jnp.broadcast_to(c_dec[:, None, None], (H, 1, RET_VAL_DIM))
    return intra, q_dec, k_dec, c_dec


K_SCALE = RET_KEY_DIM ** -0.5


RET_CHUNKS = 4


def _ret_tables_specs():
    whole = lambda *shape: pl.BlockSpec(shape, lambda b, i: (0,) * len(shape))
    return [whole(RET_HEADS, CHUNK, CHUNK), whole(RET_HEADS, CHUNK, RET_KEY_DIM), whole(RET_HEADS, CHUNK, RET_KEY_DIM),
            whole(RET_HEADS, 1, RET_VAL_DIM)]


def _rotate(x, cos, sn):
    return x * cos + pltpu.roll(x, RET_KEY_DIM // 2, 1) * sn


def _ret_fwd(proj, B, S, rope, decay, phase=None):
    T = B * S
    nc = S // CHUNK
    H, dk, dv = RET_HEADS, RET_KEY_DIM, RET_VAL_DIM
    sb = RET_CHUNKS * CHUNK
    ns = S // sb

    def body(q_ref, k_ref, v_ref, g_ref, cos_ref, sin_ref, intra_ref, qd_ref, kd_ref, cd_ref,
             qr_ref, kr_ref, o_ref, u_ref, st_ref, state_ref):
        @pl.when(pl.program_id(1) == 0)
        def _():
            state_ref[...] = jnp.zeros_like(state_ref)

        cos, sn = cos_ref[...], sin_ref[...]
        for h in range(H):
            hs = slice(h * dk, (h + 1) * dk)
            qr_ref[:, hs] = _rotate(q_ref[:, hs].astype(f32), cos, sn).astype(bf16)
            kr_ref[:, hs] = (_rotate(k_ref[:, hs].astype(f32), cos, sn) * K_SCALE).astype(bf16)
        states = [state_ref[h] for h in range(H)]
        for ci in range(RET_CHUNKS):
            r = slice(ci * CHUNK, (ci + 1) * CHUNK)
            for h in range(H):
                hk, hv = slice(h * dk, (h + 1) * dk), slice(h * dv, (h + 1) * dv)
                qi, ki, vi = qr_ref[r, hk], kr_ref[r, hk], v_ref[r, hv]
                stb = states[h].astype(bf16)
                st_ref[0, h, ci] = stb
                s = (_dot_nt(qi, ki) * intra_ref[h]).astype(bf16)
                o = _dot(s, vi) + _dot((qi.astype(f32) * qd_ref[h]).astype(bf16), stb)
                states[h] = states[h] * cd_ref[h] + _dot_tn((ki.astype(f32) * kd_ref[h]).astype(bf16), vi)
                mu = jnp.mean(o, axis=-1, keepdims=True)
                xc = o - mu
                var = jnp.mean(xc * xc, axis=-1, keepdims=True)
                oh = xc * lax.rsqrt(var + EPS)
                g = g_ref[r, hv].astype(f32)
                o_ref[r, hv] = o.astype(bf16)
                u_ref[r, hv] = (g * _sig(g) * oh).astype(bf16)
        for h in range(H):
            state_ref[h] = states[h]

    blk = lambda w, c: pl.BlockSpec((sb, w), lambda b, i: (b * ns + i, c))
    return _call(
        body, phase, name="ret_fwd", grid=(B, ns), scratch_shapes=[pltpu.VMEM((H, dk, dv), f32)],
        in_specs=[blk(H * dk, C_RQ // (H * dk)), blk(H * dk, C_RK // (H * dk)), blk(H * dv, C_RV // (H * dv)),
                  blk(H * dv, C_RG // (H * dv)),
                  pl.BlockSpec((sb, dk), lambda b, i: (i, 0)), pl.BlockSpec((sb, dk), lambda b, i: (i, 0)),
                  *_ret_tables_specs()],
        out_specs=[blk(H * dk, 0), blk(H * dk, 0), blk(H * dv, 0), blk(H * dv, 0),
                   pl.BlockSpec((1, H, RET_CHUNKS, dk, dv), lambda b, i: (b, 0, i, 0, 0))],
        out_shape=[jax.ShapeDtypeStruct((T, H * dk), bf16), jax.ShapeDtypeStruct((T, H * dk), bf16),
                   jax.ShapeDtypeStruct((T, H * dv), bf16), jax.ShapeDtypeStruct((T, H * dv), bf16),
                   jax.ShapeDtypeStruct((B, H, nc, dk, dv), bf16)],
        args=(proj, proj, proj, proj, *rope, *decay))


def _bias_rows(rb):
    last = rb[:, N_REL - 1:]
    return jnp.concatenate([
        jnp.broadcast_to(last, (ATT_HEADS, PAD - MAX_REL + 1)),
        jnp.flip(rb[:, :N_REL - 1], axis=1),
        jnp.broadcast_to(rb[:, :1], (ATT_HEADS, KWIN - PAD - CHUNK)),
        jnp.broadcast_to(last, (ATT_HEADS, TOEP - KWIN)),
    ], axis=1)


def _build_bias(t_ref, bias_ref):
    row = lax.broadcasted_iota(jnp.int32, (QBLK, KWIN), 0) // CHUNK
    col = lax.broadcasted_iota(jnp.int32, (QBLK, KWIN), 1) // CHUNK
    delta = BAND_CHUNKS + row - col
    vis = (delta >= 0) & (delta <= BAND_CHUNKS)
    for h in range(ATT_HEADS):
        t = jnp.broadcast_to(t_ref[h:h + 1, :], (QBLK, TOEP))
        rolled = pltpu.roll(t, 0, 1, stride=1, stride_axis=0)
        bias_ref[h] = jnp.where(vis, rolled[:, :KWIN], NEG_INF)


ATT_SCALE = ATT_HEAD_DIM ** -0.5


def _att_probs(qh, kh, bias):
    s = _dot_nt(qh, kh) + bias
    m = jnp.max(s, axis=-1, keepdims=True)
    p = jnp.exp(s - m)
    return p * (1.0 / jnp.sum(p, axis=-1, keepdims=True))


def _first_of_pair():
    return lax.broadcasted_iota(jnp.int32, (1, 2 * ATT_HEAD_DIM), 1) < ATT_HEAD_DIM


def _by_window(i, step):
    sizes = list(range(QBLK, KWIN, QBLK))
    for n, nk in enumerate(sizes):
        pl.when(i == n)(functools.partial(step, nk))
    pl.when(i >= len(sizes))(functools.partial(step, KWIN))


def _att_fwd(proj, trows, B, S, phase=None):
    T = B * S
    nq = S // QBLK
    dh = ATT_HEAD_DIM

    def body(q_ref, k_ref, v_ref, t_ref, o_ref, bias_ref):
        i = pl.program_id(1)

        @pl.when((pl.program_id(0) == 0) & (i == 0))
        def _():
            _build_bias(t_ref, bias_ref)

        def step(nk):
            win = pl.ds(pl.multiple_of((i + 1) * QBLK - nk, QBLK), nk)
            kw, vw = k_ref[win, :], v_ref[win, :]
            first = _first_of_pair()
            outs = []
            for p in range(ATT_HEADS // 2):
                ps = slice(2 * p * dh, 2 * (p + 1) * dh)
                q2, k2, v2 = q_ref[:, ps] * ATT_SCALE, kw[:, ps], vw[:, ps]
                both = []
                for e in range(2):
                    qm = jnp.where(first == (e == 0), q2, jnp.zeros_like(q2))
                    pr = _att_probs(qm, k2, bias_ref[2 * p + e, :, KWIN - nk:])
                    both.append(_dot(pr.astype(bf16), v2))
                outs.append(jnp.where(first, both[0], both[1]))
            o_ref[...] = jnp.concatenate(outs, axis=1).astype(bf16)

        _by_window(i, step)

    return _call(
        body, phase, name="att_fwd", grid=(B, nq),
        in_specs=[pl.BlockSpec((QBLK, ATT_W), lambda b, i: (b * nq + i, C_AQ // ATT_W)),
                  pl.BlockSpec((S, ATT_W), lambda b, i: (b, C_AK // ATT_W)),
                  pl.BlockSpec((S, ATT_W), lambda b, i: (b, C_AV // ATT_W)),
                  pl.BlockSpec((ATT_HEADS, TOEP), lambda b, i: (0, 0))],
        out_specs=[pl.BlockSpec((QBLK, ATT_W), lambda b, i: (b * nq + i, 0))],
        out_shape=[jax.ShapeDtypeStruct((T, ATT_W), bf16)],
        scratch_shapes=[pltpu.VMEM((ATT_HEADS, QBLK, KWIN), f32)],
        args=(proj, proj, proj, trows))


def _gl_specs(tm):
    w = 512
    return [pl.BlockSpec((tm, w), functools.partial(lambda i, j: (i, C_GL // 512 + j), j=j)) for j in range(4)]


def _gates(gl_refs, bg_ref):
    gl = jnp.concatenate([r[...] for r in gl_refs], axis=1).astype(f32) + bg_ref[...]
    g = _sig(gl)
    return g[:, :D_MODEL], g[:, D_MODEL:]


def _mix_fwd(x2, proj, u, ao, b_gate, w_ro, w_ao, w_out, phase=None):
    T, D = x2.shape
    tm = _tile(T, ROW_TILE, 8)

    def body(x_ref, u_ref, ao_ref, g0, g1, g2, g3, bg_ref, wro_ref, wao_ref, wo_ref, h1_ref, yr_ref, ya_ref):
        yr = _dot(u_ref[...], wro_ref[...])
        ao = ao_ref[...]
        ya = jnp.concatenate([_dot(ao, wao_ref[k]) for k in range(N_CHIPS)], axis=1)
        gr, ga = _gates((g0, g1, g2, g3), bg_ref)
        mix = gr * yr + ga * ya
        h1_ref[...] = x_ref[...] + _dot(mix.astype(bf16), wo_ref[...])
        yr_ref[...] = yr.astype(bf16)
        ya_ref[...] = ya.astype(bf16)

    full = lambda a: pl.BlockSpec(a.shape, lambda i: (0,) * a.ndim)
    row = lambda n: pl.BlockSpec((tm, n), lambda i: (i, 0))
    return _call(
        body, phase, name="mix_fwd", grid=(T // tm,), scratch_shapes=[],
        in_specs=[row(D), row(D), row(ATT_W), *_gl_specs(tm), full(b_gate), full(w_ro), full(w_ao), full(w_out)],
        out_specs=[row(D), row(D), row(D)],
        out_shape=[jax.ShapeDtypeStruct((T, D), f32), jax.ShapeDtypeStruct((T, D), bf16),
                   jax.ShapeDtypeStruct((T, D), bf16)],
        args=(x2, u, ao, proj, proj, proj, proj, b_gate, w_ro, w_ao, w_out))


def _ffn_fwd(h1, g_ffn, wg, wu, wd, g_fin, target):
    T, D = h1.shape
    nf, tf, _ = wg.shape
    tm = _tile(T, ROW_TILE, 8)

    def body(h1_ref, g_ref, wg_ref, wu_ref, wd_ref, gf_ref, tg_ref, hn_ref, a_ref, b_ref, f_ref, dh2_ref, part_ref):
        h1v = h1_ref[...]
        r = lax.rsqrt(jnp.mean(h1v * h1v, axis=-1, keepdims=True) + EPS)
        hn = (h1v * r * g_ref[...]).astype(bf16)
        hn_ref[...] = hn
        h2 = h1v
        for k in range(nf):
            a = _dot_nt(hn, wg_ref[k])
            b = _dot_nt(hn, wu_ref[k])
            f = ((a * _sig(a)) * b).astype(bf16)
            a_ref[k] = a.astype(bf16)
            b_ref[k] = b.astype(bf16)
            f_ref[k] = f
            h2 = h2 + _dot(f, wd_ref[k])
        r = lax.rsqrt(jnp.mean(h2 * h2, axis=-1, keepdims=True) + EPS)
        n = h2 * r
        gf = gf_ref[...]
        e = n * gf - tg_ref[...]
        dy = e * (1.0 / D)
        dn = dy * gf
        dh2_ref[...] = r * (dn - n * jnp.mean(dn * n, axis=-1, keepdims=True))
        part_ref[...] = jnp.zeros_like(part_ref)
        part_ref[0:1, :] = jnp.sum(dy * n, axis=0, keepdims=True)
        part_ref[1:2, :] = (0.5 / D) * jnp.sum(e * e, axis=0, keepdims=True)

    row = lambda n: pl.BlockSpec((tm, n), lambda i: (i, 0))
    vec = pl.BlockSpec((1, D), lambda i: (0, 0))
    col = pl.BlockSpec((nf, tm, tf), lambda i: (0, i, 0))
    held = lambda w: pl.BlockSpec(w.shape, lambda i: (0, 0, 0), pipeline_mode=pl.Buffered(1))
    act = jax.ShapeDtypeStruct((nf, T, tf), bf16)
    return pl.pallas_call(
        body, name="ffn_fwd", grid=(T // tm,),
        in_specs=[row(D), vec, held(wg), held(wu), held(wd), vec, row(D)],
        out_specs=[row(D), col, col, col, row(D), pl.BlockSpec((8, D), lambda i: (i, 0))],
        out_shape=[jax.ShapeDtypeStruct((T, D), bf16), act, act, act,
                   jax.ShapeDtypeStruct((T, D), f32), jax.ShapeDtypeStruct((T // tm * 8, D), f32)],
        compiler_params=_params(("parallel",)),
    )(h1, g_ffn, wg, wu, wd, g_fin, target)


def _ffn_bwd(dh2, h1, g_ffn, a, b, wg, wu, wd):
    T, D = h1.shape
    nf, tf, _ = wg.shape
    tm = _tile(T, ROW_TILE // 2, 8)

    def body(dh2_ref, h1_ref, g_ref, a_ref, b_ref, wg_ref, wu_ref, wd_ref, da_ref, db_ref, dh1_ref, part_ref):
        dh2v = dh2_ref[...]
        dh2b = dh2v.astype(bf16)
        dhn = jnp.zeros((tm, D), f32)
        for k in range(nf):
            df = _dot_nt(dh2b, wd_ref[k])
            av = a_ref[k].astype(f32)
            sg = _sig(av)
            db = (df * (av * sg)).astype(bf16)
            da = (df * b_ref[k].astype(f32) * (sg * (1.0 + av * (1.0 - sg)))).astype(bf16)
            da_ref[k] = da
            db_ref[k] = db
            dhn = dhn + _dot(da, wg_ref[k]) + _dot(db, wu_ref[k])
        h = h1_ref[...]
        r = lax.rsqrt(jnp.mean(h * h, axis=-1, keepdims=True) + EPS)
        n = h * r
        dn = dhn * g_ref[...]
        dh1_ref[...] = dh2v + r * (dn - n * jnp.mean(dn * n, axis=-1, keepdims=True))
        part_ref[...] = jnp.zeros_like(part_ref)
        part_ref[0:1, :] = jnp.sum(dhn * n, axis=0, keepdims=True)

    row = lambda n: pl.BlockSpec((tm, n), lambda i: (i, 0))
    col = pl.BlockSpec((nf, tm, tf), lambda i: (0, i, 0))
    held = lambda w: pl.BlockSpec(w.shape, lambda i: (0, 0, 0), pipeline_mode=pl.Buffered(1))
    act = jax.ShapeDtypeStruct((nf, T, tf), bf16)
    return pl.pallas_call(
        body, name="ffn_bwd", grid=(T // tm,),
        in_specs=[row(D), row(D), pl.BlockSpec((1, D), lambda i: (0, 0)), col, col, held(wg), held(wu), held(wd)],
        out_specs=[col, col, row(D), pl.BlockSpec((8, D), lambda i: (i, 0))],
        out_shape=[act, act, jax.ShapeDtypeStruct((T, D), f32), jax.ShapeDtypeStruct((T // tm * 8, D), f32)],
        compiler_params=_params(("parallel",)),
    )(dh2, h1, g_ffn, a, b, wg, wu, wd)


def _mix_bwd(dh1, proj, yr, ya, b_gate, w_ro, w_ao, w_out, phase=None):
    T, D = dh1.shape
    tm = _tile(T, ROW_TILE, 8)

    def body(dh1_ref, g0, g1, g2, g3, bg_ref, yr_ref, ya_ref, wro_ref, wao_ref, wo_ref,
             du_ref, dao_ref, dgl_ref, mix_ref, dyr_ref, dya_ref, part_ref):
        dmix = _dot_nt(dh1_ref[...].astype(bf16), wo_ref[...])
        gr, ga = _gates((g0, g1, g2, g3), bg_ref)
        yr = yr_ref[...].astype(f32)
        ya = ya_ref[...].astype(f32)
        dyr = (dmix * gr).astype(bf16)
        dya = (dmix * ga).astype(bf16)
        dgl = jnp.concatenate([dmix * yr * gr * (1.0 - gr), dmix * ya * ga * (1.0 - ga)], axis=1)
        du_ref[...] = _dot_nt(dyr, wro_ref[...]).astype(bf16)
        ns = wao_ref.shape[2]
        dao = _dot_nt(dya[:, :ns], wao_ref[0])
        for k in range(1, N_CHIPS):
            dao = dao + _dot_nt(dya[:, k * ns:(k + 1) * ns], wao_ref[k])
        dao_ref[...] = dao.astype(bf16)
        dgl_ref[...] = dgl.astype(bf16)
        mix_ref[...] = (gr * yr + ga * ya).astype(bf16)
        dyr_ref[...] = dyr
        dya_ref[...] = dya
        part_ref[...] = jnp.zeros_like(part_ref)
        part_ref[0:1, :] = jnp.sum(dgl, axis=0, keepdims=True)

    full = lambda a: pl.BlockSpec(a.shape, lambda i: (0,) * a.ndim)
    row = lambda n: pl.BlockSpec((tm, n), lambda i: (i, 0))
    return _call(
        body, phase, name="mix_bwd", grid=(T // tm,), scratch_shapes=[],
        in_specs=[row(D), *_gl_specs(tm), full(b_gate), row(D), row(D), full(w_ro), full(w_ao), full(w_out)],
        out_specs=[row(D), row(ATT_W), row(2 * D), row(D), row(D), row(D), pl.BlockSpec((8, 2 * D), lambda i: (i, 0))],
        out_shape=[jax.ShapeDtypeStruct((T, D), bf16), jax.ShapeDtypeStruct((T, ATT_W), bf16),
                   jax.ShapeDtypeStruct((T, 2 * D), bf16), jax.ShapeDtypeStruct((T, D), bf16),
                   jax.ShapeDtypeStruct((T, D), bf16), jax.ShapeDtypeStruct((T, D), bf16),
                   jax.ShapeDtypeStruct((T // tm * 8, 2 * D), f32)],
        args=(dh1, proj, proj, proj, proj, b_gate, yr, ya, w_ro, w_ao, w_out))


def _ret_bwd(proj, qr, kr, o, states, du, B, S, rope, decay, phase=None):
    T = B * S
    nc = S // CHUNK
    H, dk, dv = RET_HEADS, RET_KEY_DIM, RET_VAL_DIM

    sb = RET_CHUNKS * CHUNK
    ns = S // sb

    def body(qr_ref, kr_ref, v_ref, g_ref, o_ref, st_ref, du_ref, cos_ref, sin_ref, intra_ref, qd_ref, kd_ref, cd_ref,
             dp_ref, dstate_ref):
        dq_ref, dk_ref = dp_ref.at[:, pl.ds(C_RQ, H * dk)], dp_ref.at[:, pl.ds(C_RK, H * dk)]
        dv_ref, dg_ref = dp_ref.at[:, pl.ds(C_RV, H * dv)], dp_ref.at[:, pl.ds(C_RG, H * dv)]

        @pl.when(pl.program_id(1) == 0)
        def _():
            dstate_ref[...] = jnp.zeros_like(dstate_ref)

        cos, snb = cos_ref[...], -sin_ref[...]
        dstates = [dstate_ref[h] for h in range(H)]
        for ci in reversed(range(RET_CHUNKS)):
            r = slice(ci * CHUNK, (ci + 1) * CHUNK)
            for h in range(H):
                hk, hv = slice(h * dk, (h + 1) * dk), slice(h * dv, (h + 1) * dv)
                intra, qd, kd = intra_ref[h], qd_ref[h], kd_ref[h]
                qi, ki, vi = qr_ref[r, hk], kr_ref[r, hk], v_ref[r, hv]
                si = st_ref[0, h, ci]
                o = o_ref[r, hv].astype(f32)
                mu = jnp.mean(o, axis=-1, keepdims=True)
                xc = o - mu
                rstd = lax.rsqrt(jnp.mean(xc * xc, axis=-1, keepdims=True) + EPS)
                oh = xc * rstd
                g = g_ref[r, hv].astype(f32)
                sg = _sig(g)
                dui = du_ref[r, hv].astype(f32)
                dg_ref[r, hv] = (dui * oh * (sg * (1.0 + g * (1.0 - sg)))).astype(bf16)
                doh = dui * (g * sg)
                do = rstd * (doh - jnp.mean(doh, axis=-1, keepdims=True)
                             - oh * jnp.mean(doh * oh, axis=-1, keepdims=True))
                dob = do.astype(bf16)
                p = (_dot_nt(qi, ki) * intra).astype(bf16)
                dsb = dstates[h].astype(bf16)
                kt = (ki.astype(f32) * kd).astype(bf16)
                qt = (qi.astype(f32) * qd).astype(bf16)
                dv_ref[r, hv] = (_dot_tn(p, dob) + _dot(kt, dsb)).astype(bf16)
                da = (_dot_nt(dob, vi) * intra).astype(bf16)
                dq = _dot(da, ki) + _dot_nt(dob, si) * qd
                dkk = (_dot_tn(da, qi) + _dot_nt(vi, dsb) * kd) * K_SCALE
                dq_ref[r, hk] = _rotate(dq, cos[r], snb[r]).astype(bf16)
                dk_ref[r, hk] = _rotate(dkk, cos[r], snb[r]).astype(bf16)
                dstates[h] = dstates[h] * cd_ref[h] + _dot_tn(qt, dob)
        for h in range(H):
            dstate_ref[h] = dstates[h]

    blk = lambda w, c: pl.BlockSpec((sb, w), lambda b, i: (b * ns + ns - 1 - i, c))
    return _call(
        body, phase, name="ret_bwd", grid=(B, ns),
        in_specs=[blk(H * dk, 0), blk(H * dk, 0), blk(H * dv, C_RV // (H * dv)), blk(H * dv, C_RG // (H * dv)),
                  blk(H * dv, 0),
                  pl.BlockSpec((1, H, RET_CHUNKS, dk, dv), lambda b, i: (b, 0, ns - 1 - i, 0, 0)),
                  blk(H * dv, 0),
                  pl.BlockSpec((sb, dk), lambda b, i: (ns - 1 - i, 0)), pl.BlockSpec((sb, dk), lambda b, i: (ns - 1 - i, 0)),
                  *_ret_tables_specs()],
        out_specs=[blk(C_AQ, 0)], out_shape=[jax.ShapeDtypeStruct((T, N_IN), bf16)],
        scratch_shapes=[pltpu.VMEM((H, dk, dv), f32)],
        args=(qr, kr, proj, proj, o, states, du, *rope, *decay))


def _att_bwd(proj, dao, trows, dproj, B, S, phase=None):
    T = B * S
    nq = S // QBLK
    dh = ATT_HEAD_DIM
    scale = ATT_HEAD_DIM ** -0.5

    def body(q_ref, k_ref, v_ref, do_ref, t_ref, _, dp_ref, vec_ref, bias_ref, dbias_ref, dka_ref, dva_ref):
        b, i = pl.program_id(0), pl.program_id(1)

        @pl.when((b == 0) & (i == 0))
        def _():
            _build_bias(t_ref, bias_ref)
            dbias_ref[...] = jnp.zeros_like(dbias_ref)

        @pl.when(i == 0)
        def _():
            dka_ref[...] = jnp.zeros_like(dka_ref)
            dva_ref[...] = jnp.zeros_like(dva_ref)

        def step(nk):
            win = pl.ds(pl.multiple_of((i + 1) * QBLK - nk, QBLK), nk)
            kw, vw = k_ref[win, :], v_ref[win, :]
            first = _first_of_pair()
            first_rows = lax.broadcasted_iota(jnp.int32, (2 * dh, 1), 0) < dh
            dqs, dks, dvs = [], [], []
            for p in range(ATT_HEADS // 2):
                ps = slice(2 * p * dh, 2 * (p + 1) * dh)
                q2, k2, v2, do2 = q_ref[:, ps] * ATT_SCALE, kw[:, ps], vw[:, ps], do_ref[:, ps]
                dq2, dk2, dv2 = [], [], []
                for e in range(2):
                    h = 2 * p + e
                    mine = first == (e == 0)
                    pr = _att_probs(jnp.where(mine, q2, jnp.zeros_like(q2)), k2, bias_ref[h, :, KWIN - nk:])
                    dp = _dot_nt(jnp.where(mine, do2, jnp.zeros_like(do2)), v2)
                    ds = pr * (dp - jnp.sum(pr * dp, axis=-1, keepdims=True))
                    dbias_ref[h, :, KWIN - nk:] += ds
                    dsb = ds.astype(bf16)
                    dq2.append(_dot(dsb, k2) * ATT_SCALE)
                    dk2.append(_dot_tn(q2, dsb))
                    dv2.append(_dot_tn(do2, pr.astype(bf16)))
                dqs.append(jnp.where(first, dq2[0], dq2[1]))
                dks.append(jnp.where(first_rows, dk2[0], dk2[1]))
                dvs.append(jnp.where(first_rows, dv2[0], dv2[1]))
            dp_ref[pl.ds(pl.multiple_of(i * QBLK, QBLK), QBLK), :ATT_W] = jnp.concatenate(dqs, axis=1).astype(bf16)
            dka_ref[:, win] += jnp.concatenate(dks, axis=0)
            dva_ref[:, win] += jnp.concatenate(dvs, axis=0)

        _by_window(i, step)

        @pl.when(i == nq - 1)
        def _():
            dp_ref[:, ATT_W:2 * ATT_W] = dka_ref[...].T.astype(bf16)
            dp_ref[:, 2 * ATT_W:] = dva_ref[...].T.astype(bf16)

        @pl.when((b == B - 1) & (i == nq - 1))
        def _():
            rr = lax.broadcasted_iota(jnp.int32, (QBLK, QBLK), 0)
            cc = lax.broadcasted_iota(jnp.int32, (QBLK, QBLK), 1)
            flip = jnp.where(rr + cc == QBLK - 1, 1.0, 0.0).astype(bf16)
            for h in range(ATT_HEADS):
                d = dbias_ref[h]
                hi = d.astype(bf16)
                lo = (d - hi.astype(f32)).astype(bf16)
                rev = _dot(flip, hi) + _dot(flip, lo)
                wide = jnp.concatenate([rev, jnp.zeros((QBLK, TOEP - KWIN), f32)], axis=1)
                rolled = pltpu.roll(wide, 0, 1, stride=1, stride_axis=0)
                vec_ref[h:h + 1, :] = jnp.sum(rolled, axis=0, keepdims=True)

    qspec = lambda c: pl.BlockSpec((QBLK, ATT_W), lambda b, i: (b * nq + i, c))
    kspec = lambda c: pl.BlockSpec((S, ATT_W), lambda b, i: (b, c))
    return _call(
        body, phase, name="att_bwd", grid=(B, nq), aliases={5: 0},
        in_specs=[qspec(C_AQ // ATT_W), kspec(C_AK // ATT_W), kspec(C_AV // ATT_W), qspec(0),
                  pl.BlockSpec((ATT_HEADS, TOEP), lambda b, i: (0, 0)), pl.BlockSpec(memory_space=pl.ANY)],
        out_specs=[pl.BlockSpec((S, 3 * ATT_W), lambda b, i: (b, C_AQ // (3 * ATT_W))),
                   pl.BlockSpec((ATT_HEADS, TOEP), lambda b, i: (0, 0))],
        out_shape=[jax.ShapeDtypeStruct((T, N_IN), bf16), jax.ShapeDtypeStruct((ATT_HEADS, TOEP), f32)],
        scratch_shapes=[pltpu.VMEM((ATT_HEADS, QBLK, KWIN), f32), pltpu.VMEM((ATT_HEADS, QBLK, KWIN), f32),
                        pltpu.VMEM((ATT_W, S), f32), pltpu.VMEM((ATT_W, S), f32)],
        args=(proj, proj, proj, dao, trows, dproj))


def _in_proj_bwd(dproj, w_in, x2, gamma, dh1, phase=None):
    T, D = x2.shape
    nk, _, tk = w_in.shape
    tm = _tile(T, BIG_ROW_TILE, 8)

    def body(dp_ref, w_ref, x_ref, g_ref, dh1_ref, dx_ref, part_ref, acc_ref):
        j = pl.program_id(1)

        @pl.when(j == 0)
        def _():
            acc_ref[...] = jnp.zeros_like(acc_ref)

        acc_ref[...] += _dot_nt(dp_ref[...], w_ref[0])

        @pl.when(j == nk - 1)
        def _():
            x = x_ref[...]
            r = lax.rsqrt(jnp.mean(x * x, axis=-1, keepdims=True) + EPS)
            n = x * r
            dxn = acc_ref[...]
            dn = dxn * g_ref[...]
            dx_ref[...] = dh1_ref[...] + r * (dn - n * jnp.mean(dn * n, axis=-1, keepdims=True))
            part_ref[...] = jnp.zeros_like(part_ref)
            part_ref[0:1, :] = jnp.sum(dxn * n, axis=0, keepdims=True)

    row = lambda n: pl.BlockSpec((tm, n), lambda i, j: (i, 0))
    return _call(
        body, phase, name="in_proj_bwd", grid=(T // tm, nk),
        in_specs=[pl.BlockSpec((tm, tk), lambda i, j: (i, j)), pl.BlockSpec((1, D, tk), lambda i, j: (j, 0, 0)), row(D),
                  pl.BlockSpec((1, D), lambda i, j: (0, 0)), row(D)],
        out_specs=[row(D), pl.BlockSpec((8, D), lambda i, j: (i, 0))],
        out_shape=[jax.ShapeDtypeStruct((T, D), f32), jax.ShapeDtypeStruct((T // tm * 8, D), f32)],
        scratch_shapes=[pltpu.VMEM((tm, D), f32)],
        args=(dproj, w_in, x2, gamma, dh1))


def _wgrad(a, b, shard_axis, name, phase=None):
    def spec(arr, sharded, tt):
        if arr.ndim == 3:
            return arr.shape[2], pl.BlockSpec((1, tt, arr.shape[2]), lambda s, t: (s, t, 0))
        if sharded:
            w = arr.shape[1] // N_CHIPS
            return w, pl.BlockSpec((tt, w), lambda s, t: (t, s))
        return arr.shape[1], pl.BlockSpec((tt, arr.shape[1]), lambda s, t: (t, 0))

    T = a.shape[-2]
    tt = _tile(T, BIG_ROW_TILE, 16)
    nt = T // tt
    whole = a.ndim == 2 and b.ndim == 2 and a.shape[1] * b.shape[1] * 4 <= WGRAD_ACC_BYTES
    if whole:
        K, N = a.shape[1], b.shape[1]
        a_spec, b_spec = pl.BlockSpec((tt, K), lambda s, t: (t, 0)), pl.BlockSpec((tt, N), lambda s, t: (t, 0))
        out_block = (N_CHIPS, K // N_CHIPS, N) if shard_axis == 0 else (N_CHIPS, K, N // N_CHIPS)
        out_spec = pl.BlockSpec(out_block, lambda s, t: (0, 0, 0))
    else:
        K, a_spec = spec(a, shard_axis == 0, tt)
        N, b_spec = spec(b, shard_axis == 1, tt)
        out_block = (N_CHIPS, K, N)
        out_spec = pl.BlockSpec((1, K, N), lambda s, t: (s, 0, 0))

    def body(a_ref, b_ref, o_ref, acc_ref):
        t = pl.program_id(1)

        @pl.when(t == 0)
        def _():
            acc_ref[...] = jnp.zeros_like(acc_ref)

        av = a_ref[0] if a.ndim == 3 else a_ref[...]
        bv = b_ref[0] if b.ndim == 3 else b_ref[...]
        acc_ref[...] += _dot_tn(av.astype(bf16), bv.astype(bf16))

        @pl.when(t == nt - 1)
        def _():
            if not whole:
                o_ref[0] = acc_ref[...].astype(bf16)
            else:
                _, kk, nn = out_block
                for s in range(N_CHIPS):
                    o_ref[s] = (acc_ref[s * kk:(s + 1) * kk, :] if shard_axis == 0
                                else acc_ref[:, s * nn:(s + 1) * nn]).astype(bf16)

    (grad,), carried = _call(
        body, phase, name=name, grid=(1 if whole else N_CHIPS, nt), in_specs=[a_spec, b_spec], out_specs=[out_spec],
        out_shape=[jax.ShapeDtypeStruct(out_block, bf16)], scratch_shapes=[pltpu.VMEM((K, N), f32)], args=(a, b))
    return grad, carried


def _adamw_sum(place, groups, name):
    n = len(groups)
    R, C = groups[0][0].shape
    half = R // 2
    tr = _tile(half, max(16, (1 << 18) // C // 16 * 16), 16)
    nr = half // tr

    def body(p_ref, *refs):
        for a in range(n):
            w_ref, m_ref, v_ref, part_ref, fc_ref, fs_ref = refs[6 * a:6 * a + 6]
            g_ref, d_ref, mo_ref, vo_ref = refs[6 * n + 4 * a:6 * n + 4 * a + 4]
            up = lambda x: x.astype(f32)
            mine = ((up(part_ref[0]) + up(fc_ref[0])) + up(fc_ref[1])) + up(fc_ref[2])
            sibs = ((up(fs_ref[0]) + up(fs_ref[1])) + up(fs_ref[2])) + up(fs_ref[3])
            g_ = jnp.where(pl.program_id(0) == p_ref[0], mine, sibs)
            m_ = ADAM_B1 * m_ref[...] + (1.0 - ADAM_B1) * g_
            v_ = ADAM_B2 * v_ref[...] + (1.0 - ADAM_B2) * (g_ * g_)
            m_hat = m_ / (1.0 - ADAM_B1 ** ADAM_STEP)
            v_hat = v_ / (1.0 - ADAM_B2 ** ADAM_STEP)
            g_ref[...] = g_
            d_ref[...] = -ADAM_LR * (m_hat / (jnp.sqrt(v_hat) + ADAM_EPS) + ADAM_WD * w_ref[...])
            mo_ref[...] = m_
            vo_ref[...] = v_

    spec = pl.BlockSpec((tr, C), lambda h, r, p: (h * nr + r, 0))
    one = [spec, spec, spec, pl.BlockSpec((1, tr, C), lambda h, r, p: (p[1], jnp.where(h == p[0], r, 0), 0)),
           pl.BlockSpec((3, tr, C), lambda h, r, p: (0, jnp.where(h == p[0], r, 0), 0)),
           pl.BlockSpec((4, tr, C), lambda h, r, p: (0, jnp.where(h == p[0], 0, r), 0))]
    res = pl.pallas_call(
        body, name=name,
        grid_spec=pltpu.PrefetchScalarGridSpec(num_scalar_prefetch=1, grid=(2, nr), in_specs=one * n,
                                               out_specs=[spec] * (4 * n)),
        out_shape=[jax.ShapeDtypeStruct((R, C), f32)] * (4 * n),
        compiler_params=_params(("parallel", "parallel")),
    )(place, *[x for g in groups for x in g])
    return [tuple(res[4 * a:4 * a + 4]) for a in range(n)]


def _adamw(w, g, m, v, name):
    R, C = w.shape
    tr = _tile(R, max(8, (1 << 18) // C // 8 * 8), 8)

    def body(w_ref, g_ref, m_ref, v_ref, d_ref, mo_ref, vo_ref):
        g_ = g_ref[...]
        m_ = ADAM_B1 * m_ref[...] + (1.0 - ADAM_B1) * g_
        v_ = ADAM_B2 * v_ref[...] + (1.0 - ADAM_B2) * (g_ * g_)
        m_hat = m_ / (1.0 - ADAM_B1 ** ADAM_STEP)
        v_hat = v_ / (1.0 - ADAM_B2 ** ADAM_STEP)
        d_ref[...] = -ADAM_LR * (m_hat / (jnp.sqrt(v_hat) + ADAM_EPS) + ADAM_WD * w_ref[...])
        mo_ref[...] = m_
        vo_ref[...] = v_

    spec = pl.BlockSpec((tr, C), lambda i: (i, 0))
    return pl.pallas_call(
        body, name=name, grid=(R // tr,), in_specs=[spec] * 4, out_specs=[spec] * 3,
        out_shape=[jax.ShapeDtypeStruct((R, C), f32)] * 3,
        compiler_params=_params(("parallel",)),
    )(w, g, m, v)


def _place():
    return lax.axis_index("x"), lax.axis_index("y"), lax.axis_index("c")


def _other_chips(x, y):
    chips = [(1 - x, y), (x, 1 - y), (1 - x, 1 - y)]
    return chips, [2 * cx + cy for cx, cy in chips]


def _spread_phase(blk):
    def peers():
        x, y, c = _place()
        return [tuple(1 - p if (k >> s) & 1 else p for p, s in ((x, 2), (y, 1), (c, 0))) for k in range(1, N_DEV)]

    def copies(pin, out):
        x, y, c = _place()
        mine = out[0].at[4 * x + 2 * y + c]
        return [(mine, mine, peer) for peer in peers()]

    stack = jnp.broadcast_to(blk, (N_DEV,) + blk.shape)
    return _Phase([stack], [jax.ShapeDtypeStruct(stack.shape, stack.dtype)], {0: 0}, N_DEV - 1, copies,
                  lambda pin, out: [out[0].at[4 * px + 2 * py + pc] for px, py, pc in peers()])


def _sum_slots(stack, name):
    def body(s_ref, o_ref):
        tot = s_ref[0]
        for d in range(1, stack.shape[0]):
            tot = tot + s_ref[d]
        o_ref[...] = tot

    vm = pl.BlockSpec(memory_space=pltpu.VMEM)
    return pl.pallas_call(body, name=name, in_specs=[vm], out_specs=vm,
                          out_shape=jax.ShapeDtypeStruct(stack.shape[1:], stack.dtype))(stack)


def _cast_shards(place, ws, name):
    n = len(ws)
    R, C = ws[0].shape
    tr = _tile(R, max(16, (1 << 19) // C // 16 * 16), 16)

    def body(p_ref, *refs):
        for a in range(n):
            refs[n + a][0] = refs[a][...].astype(bf16)

    return pl.pallas_call(
        body, name=name,
        grid_spec=pltpu.PrefetchScalarGridSpec(
            num_scalar_prefetch=1, grid=(R // tr,),
            in_specs=[pl.BlockSpec((tr, C), lambda r, p: (r, 0))] * n,
            out_specs=[pl.BlockSpec((1, tr, C), lambda r, p: (p[1], r, 0))] * n),
        out_shape=[jax.ShapeDtypeStruct((N_CHIPS, R, C), bf16)] * n,
        compiler_params=_params(("parallel",)),
    )(place, *ws)


class _Phase:
    def __init__(self, arrays, out_shapes, aliases, n_copies, copies, arrivals, own_starts=(), own_waits=()):
        self.arrays, self.out_shapes, self.aliases = list(arrays), list(out_shapes), dict(aliases)
        self.n_copies, self.copies, self.arrivals = n_copies, copies, arrivals
        self.own_starts, self.own_waits = tuple(own_starts), tuple(own_waits)

    def sems(self):
        return [pltpu.SemaphoreType.DMA((self.n_copies,)), pltpu.SemaphoreType.DMA((self.n_copies,))]

    def _descriptors(self, pin, pout, send_sems, recv_sems):
        return [pltpu.make_async_remote_copy(src_ref=s, dst_ref=d, send_sem=send_sems.at[i], recv_sem=recv_sems.at[i],
                                             device_id=to, device_id_type=MESH)
                for i, (s, d, to) in enumerate(self.copies(pin, pout))]

    def _arrival(self, i, pin, pout, send_sems, recv_sems):
        dst = self.arrivals(pin, pout)[i]
        return pltpu.make_async_remote_copy(src_ref=dst, dst_ref=dst, send_sem=send_sems.at[i], recv_sem=recv_sems.at[i],
                                            device_id=_place(), device_id_type=MESH)

    def start(self, pin, pout, send_sems, recv_sems):
        for i, cp in enumerate(self._descriptors(pin, pout, send_sems, recv_sems)):
            if i not in self.own_starts:
                cp.start()

    def begin(self, i, pin, pout, send_sems, recv_sems):
        self._descriptors(pin, pout, send_sems, recv_sems)[i].start()

    def arrived(self, i, pin, pout, send_sems, recv_sems):
        self._arrival(i, pin, pout, send_sems, recv_sems).wait_recv()

    def finish(self, pin, pout, send_sems, recv_sems):
        for i in range(self.n_copies):
            if i not in self.own_waits:
                self._arrival(i, pin, pout, send_sems, recv_sems).wait_recv()
        for cp in self._descriptors(pin, pout, send_sems, recv_sems):
            cp.wait_send()


def _join(phases):
    if len(phases) == 1:
        return phases[0]
    ai = np.cumsum([0] + [len(p.arrays) for p in phases])
    oi = np.cumsum([0] + [len(p.out_shapes) for p in phases])

    def each(fn_name, pin, pout):
        return [item for k, p in enumerate(phases)
                for item in getattr(p, fn_name)(pin[ai[k]:ai[k + 1]], pout[oi[k]:oi[k + 1]])]

    aliases = {int(ai[k]) + i: int(oi[k]) + j for k, p in enumerate(phases) for i, j in p.aliases.items()}
    ci = np.cumsum([0] + [p.n_copies for p in phases])
    shifted = lambda attr: [int(ci[k]) + i for k, p in enumerate(phases) for i in getattr(p, attr)]
    return _Phase([a for p in phases for a in p.arrays], [s for p in phases for s in p.out_shapes], aliases,
                  int(ci[-1]), functools.partial(each, "copies"), functools.partial(each, "arrivals"),
                  shifted("own_starts"), shifted("own_waits"))


def _call(body, phase, *, name, grid, in_specs, out_specs, out_shape, scratch_shapes, args, prefetch=(), expose=False,
          aliases=None):
    seq = _params(("arbitrary",) * len(grid))
    np_ = len(prefetch)
    own = {np_ + i: j for i, j in (aliases or {}).items()}
    if phase is None:
        spec = pltpu.PrefetchScalarGridSpec(num_scalar_prefetch=np_, grid=grid, in_specs=in_specs, out_specs=out_specs,
                                            scratch_shapes=scratch_shapes)
        res = pl.pallas_call(body, name=name, grid_spec=spec, out_shape=out_shape, input_output_aliases=own,
                             compiler_params=seq)(*prefetch, *args)
        return list(res), []
    ni, no, ns = len(in_specs), len(out_specs), len(scratch_shapes)
    pi, po = len(phase.arrays), len(phase.out_shapes)

    def hosted(*refs):
        cut = np.cumsum([np_, ni, pi, no, po, ns])
        pre, ins, pin, outs, pout, scr, sems = (refs[a:b] for a, b in zip([0, *cut], [*cut, len(refs)]))
        ids = [pl.program_id(d) for d in range(len(grid))]
        first = functools.reduce(lambda p, q: p & q, [i == 0 for i in ids])
        last = functools.reduce(lambda p, q: p & q, [i == g - 1 for i, g in zip(ids, grid)])
        pl.when(first)(lambda: phase.start(pin, pout, *sems))
        body(*pre, *ins, *outs, *scr, **({"carried": (pin, pout, sems)} if expose else {}))
        pl.when(last)(lambda: phase.finish(pin, pout, *sems))

    anyspace = pl.BlockSpec(memory_space=pl.ANY)
    spec = pltpu.PrefetchScalarGridSpec(
        num_scalar_prefetch=np_, grid=grid, in_specs=list(in_specs) + [anyspace] * pi,
        out_specs=list(out_specs) + [anyspace] * po, scratch_shapes=list(scratch_shapes) + phase.sems())
    res = pl.pallas_call(
        hosted, name=name, grid_spec=spec, out_shape=list(out_shape) + phase.out_shapes,
        input_output_aliases={**own, **{np_ + ni + i: no + j for i, j in phase.aliases.items()}}, compiler_params=seq,
    )(*prefetch, *args, *phase.arrays)
    return list(res[:no]), list(res[no:])


def _run_phases(name, phases):
    first = phases[0]
    pi, po = len(first.arrays), len(first.out_shapes)

    def body(*refs):
        pin, pout, sems = refs[:pi], refs[pi:pi + po], refs[pi + po:]
        for n, ph in enumerate(phases):
            ph.start(pin, pout, *sems[2 * n:2 * n + 2])
            ph.finish(pin, pout, *sems[2 * n:2 * n + 2])

    anyspace = pl.BlockSpec(memory_space=pl.ANY)
    return list(pl.pallas_call(
        body, name=name, in_specs=[anyspace] * pi, out_specs=[anyspace] * po, out_shape=first.out_shapes,
        input_output_aliases=first.aliases, scratch_shapes=[s for ph in phases for s in ph.sems()],
    )(*first.arrays))


def _half_rows(buf, c):
    half = buf.shape[1] // 2
    return pl.ds(c * half, half), pl.ds((1 - c) * half, half)


def _gather_phase(bufs, over_ici):
    n = len(bufs)
    shapes = [jax.ShapeDtypeStruct(b.shape, b.dtype) for b in bufs]

    def landed(out, which):
        x, y, c = _place()
        _, ks = _other_chips(x, y)
        return [out[a].at[ks[j], _half_rows(bufs[a], c)[which]] for a in range(n) for j in range(3)]

    def ici(pin, out):
        x, y, c = _place()
        chips, _ = _other_chips(x, y)
        mine = [out[a].at[2 * x + y, _half_rows(bufs[a], c)[0]] for a in range(n)]
        return [(mine[a], mine[a], (*chips[j], c)) for a in range(n) for j in range(3)]

    def d2d(pin, out):
        x, y, c = _place()
        return [(dst, dst, (x, y, 1 - c)) for dst in landed(out, 0)]

    if over_ici:
        return _Phase(bufs, shapes, {a: a for a in range(n)}, 3 * n, ici, lambda pin, out: landed(out, 0))
    return _Phase(bufs, shapes, {a: a for a in range(n)}, 3 * n, d2d, lambda pin, out: landed(out, 1))


def _feed_phase(buf):
    def chips():
        x, y, c = _place()
        return [(x if f < 2 else 1 - x, y if f % 2 == 0 else 1 - y) for f in (1, 2, 3)]

    def copies(pin, out):
        x, y, c = _place()
        mine = _half_rows(buf, c)[0]
        own = out[0].at[2 * x + y, mine]
        sent = [(own, own, (cx, cy, c)) for cx, cy in chips()]
        return sent + [(out[0].at[2 * cx + cy, mine], out[0].at[2 * cx + cy, mine], (x, y, 1 - c)) for cx, cy in chips()]

    def arrivals(pin, out):
        x, y, c = _place()
        mine, theirs = _half_rows(buf, c)
        return [out[0].at[2 * cx + cy, rows] for rows in (mine, theirs) for cx, cy in chips()]

    return _Phase([buf], [jax.ShapeDtypeStruct(buf.shape, buf.dtype)], {0: 0}, 6, copies, arrivals,
                  own_starts=(3, 4, 5), own_waits=range(6))


def _rs_sibling(grads, name):
    n = len(grads)

    def body(*refs):
        g, out, send_sems, recv_sems = refs[:n], refs[n:2 * n], refs[2 * n], refs[2 * n + 1]
        x, y, c = _place()
        copies = []
        for a in range(n):
            half = grads[a].shape[1] // 2
            cp = pltpu.make_async_remote_copy(src_ref=g[a].at[:, pl.ds((1 - c) * half, half)], dst_ref=out[a],
                                              send_sem=send_sems.at[a], recv_sem=recv_sems.at[a],
                                              device_id=(x, y, 1 - c), device_id_type=MESH)
            cp.start()
            copies.append(cp)
        for cp in copies:
            cp.wait()

    anyspace = pl.BlockSpec(memory_space=pl.ANY)
    return pl.pallas_call(
        body, name=name, in_specs=[anyspace] * n, out_specs=[anyspace] * n,
        out_shape=[jax.ShapeDtypeStruct((N_CHIPS, g.shape[1] // 2, g.shape[2]), g.dtype) for g in grads],
        scratch_shapes=[pltpu.SemaphoreType.DMA((n,)), pltpu.SemaphoreType.DMA((n,))],
    )(*grads)


def _rs_add_sibling(place, grads, gots, name):
    n = len(grads)
    _, R, C = grads[0].shape
    half = R // 2
    tr = _tile(half, max(16, (1 << 19) // C // 16 * 16), 16)
    nr = half // tr

    def body(p_ref, *refs):
        for a in range(n):
            refs[2 * n + a][...] = (refs[2 * a][...].astype(f32) + refs[2 * a + 1][...].astype(f32)).astype(bf16)

    res = pl.pallas_call(
        body, name=name,
        grid_spec=pltpu.PrefetchScalarGridSpec(
            num_scalar_prefetch=1, grid=(N_CHIPS, nr),
            in_specs=[pl.BlockSpec((1, tr, C), lambda k, r, p: (k, p[0] * nr + r, 0)),
                      pl.BlockSpec((1, tr, C), lambda k, r, p: (k, r, 0))] * n,
            out_specs=[pl.BlockSpec((1, tr, C), lambda k, r, p: (k, r, 0))] * n),
        out_shape=[jax.ShapeDtypeStruct((N_CHIPS, half, C), bf16)] * n,
        compiler_params=_params(("parallel", "parallel")),
    )(place, *[x for pair in zip(grads, gots) for x in pair])
    return list(res)


def _rs_chips_phase(parts):
    n = len(parts)

    def copies(p, fc):
        x, y, c = _place()
        chips, ks = _other_chips(x, y)
        return [(p[a].at[ks[j]], fc[a].at[j], (*chips[j], c)) for a in range(n) for j in range(3)]

    shapes = [jax.ShapeDtypeStruct((3,) + q.shape[1:], q.dtype) for q in parts]
    return _Phase(parts, shapes, {}, 3 * n, copies, lambda p, fc: [fc[a].at[j] for a in range(n) for j in range(3)])


def _rs_hand_phase(parts, from_chips):
    n = len(parts)

    def copies(pin, fs):
        x, y, c = _place()
        sib = (x, y, 1 - c)
        own = [(pin[a].at[2 * x + y], fs[a].at[0], sib) for a in range(n)]
        return own + [(pin[n + a].at[j], fs[a].at[1 + j], sib) for a in range(n) for j in range(3)]

    def arrivals(pin, fs):
        return [fs[a].at[0] for a in range(n)] + [fs[a].at[1 + j] for a in range(n) for j in range(3)]

    shapes = [jax.ShapeDtypeStruct((4,) + q.shape[1:], q.dtype) for q in parts]
    return _Phase(list(parts) + list(from_chips), shapes, {}, 4 * n, copies, arrivals)


class _Exchange:
    def __init__(self, place):
        self.place = place

    def feed(self, buf):
        return _feed_phase(buf)

    def gather(self, bufs, over_ici):
        return _gather_phase(bufs, over_ici)

    def pair_sums(self, names, grads):
        got = _rs_sibling(grads, "rs_sibling_" + names[0])
        parts = {}
        for group in _same_shape(grads):
            res = _rs_add_sibling(self.place, [grads[i] for i in group], [got[i] for i in group],
                                  "rs_add_" + names[group[0]])
            parts.update(zip(group, res))
        return [parts[i] for i in range(len(names))]

    def to_chips(self, parts):
        return _rs_chips_phase(parts)

    def to_sibling(self, parts, from_chips):
        return _rs_hand_phase(parts, from_chips)

    def spread(self, blk):
        return _spread_phase(blk)

    def hand_over(self, name, parts, from_chips, blk):
        got = _run_phases(name, [_join([_rs_hand_phase(parts, from_chips), _spread_phase(blk)])])
        return got[:-1], got[-1]


def _local_step(place, x, target, norm_mix, b_gate, rb_chip, norm_ffn, norm_final, w_in, rest, exch):
    B, S, D = x.shape
    T = B * S
    x2 = x.reshape(T, D)
    tg2 = target.reshape(T, D)
    rope, decay = _rope_tables(S), _decay_tables()
    g_fin = norm_final.reshape(1, D)
    nrel = rb_chip.shape[-1]

    mrg, ffn = ["w_ret_out", "w_att_out", "w_out"], ["w_ffn_gate", "w_ffn_up", "w_ffn_down"]
    (xn, proj), got = _in_proj(place, x2, norm_mix, _join([exch.feed(w_in), exch.gather([rest[n] for n in mrg], True),
                                                           exch.spread(jnp.pad(rb_chip, ((0, 0), (0, 128 - nrel))))]))
    w_in, wb, rb_all = got[0], {}, got.pop()
    trows = _bias_rows(jnp.concatenate([rb_all[2 * k, :, :nrel] for k in range(N_CHIPS)], axis=1))
    (qr, kr, o, u, states), got = _ret_fwd(proj, B, S, rope, decay, _join([exch.gather([rest["w_ffn_gate"]], True),
                                                                         exch.gather(got[1:], False)]))
    wb.update(zip(mrg, got[1:]))
    (ao,), got = _att_fwd(proj, trows, B, S, _join([exch.gather([rest["w_ffn_up"], rest["w_ffn_down"]], True),
                                                    exch.gather(got[:1], False)]))
    wb["w_ffn_gate"] = got[2]
    w_ro, w_out = wb["w_ret_out"].reshape(-1, D), wb["w_out"].reshape(-1, D)
    (h1, yr, ya), got = _mix_fwd(x2, proj, u, ao, b_gate, w_ro, wb["w_att_out"], w_out, exch.gather(got[:2], False))
    wb.update(zip(ffn[1:], got))
    hn, a, b, f, dh2, part_fin = _ffn_fwd(h1, norm_ffn, wb["w_ffn_gate"], wb["w_ffn_up"], wb["w_ffn_down"], g_fin, tg2)

    da, db, dh1, part_ffn = _ffn_bwd(dh2, h1, norm_ffn, a, b, wb["w_ffn_gate"], wb["w_ffn_up"], wb["w_ffn_down"])
    ffn = ["w_ffn_down", "w_ffn_gate", "w_ffn_up"]
    p_ffn = exch.pair_sums(ffn, [_wgrad(f, dh2, 0, "wgrad_ffn_down")[0], _wgrad(da, hn, 0, "wgrad_ffn_gate")[0],
                                 _wgrad(db, hn, 0, "wgrad_ffn_up")[0]])
    (du, dao, dgl, mix, dyr, dya, part_bg), c_down = _mix_bwd(dh1, proj, yr, ya, b_gate, w_ro, wb["w_att_out"], w_out,
                                                               exch.to_chips(p_ffn[:1]))
    mrg = ["w_out", "w_ret_out", "w_att_out"]
    p_mrg = exch.pair_sums(mrg, [_wgrad(mix, dh1, 0, "wgrad_out")[0], _wgrad(u, dyr, 0, "wgrad_ret_out")[0],
                                 _wgrad(ao, dya, 1, "wgrad_att_out")[0]])
    (dproj,), c_gate_up = _ret_bwd(proj, qr, kr, o, states, du, B, S, rope, decay, exch.to_chips(p_ffn[1:]))
    c_ffn = c_down + c_gate_up
    (dproj, dvec), got = _att_bwd(proj, dao, trows, dproj, B, S, _join([exch.to_chips(p_mrg),
                                                                        exch.to_sibling(p_ffn, c_ffn)]))
    c_mrg, s_ffn = got[:len(mrg)], got[len(mrg):]
    dproj = lax.dynamic_update_slice(dproj, dgl, (0, C_GL))
    g_in, s_mrg = _wgrad(xn, dproj, 1, "wgrad_in", exch.to_sibling(p_mrg, c_mrg))
    p_in = exch.pair_sums(["w_in"], [g_in])
    (gx, part_mix), c_in = _in_proj_bwd(dproj, w_in, x2, norm_mix, dh1, exch.to_chips(p_in))
    rows = lambda p, r: p.reshape(-1, 8, p.shape[-1])[:, r, :].sum(axis=0)
    lo = KWIN - 1 - (MAX_REL - 1)
    drb = jnp.concatenate([jnp.flip(dvec[:, lo:lo + N_REL - 1], axis=1), dvec[:, :lo].sum(axis=1, keepdims=True)], axis=1)
    gsmall = {
        "norm_mix": rows(part_mix, 0), "b_gate": rows(part_bg, 0), "rel_bias": drb, "norm_ffn": rows(part_ffn, 0),
        "norm_final": rows(part_fin, 0),
    }
    s_in, small_all = exch.hand_over("rs_hand_w_in", p_in, c_in, _pack_small(gsmall, rows(part_fin, 1)))
    gbig = dict(zip(ffn + mrg + ["w_in"], zip(p_ffn + p_mrg + p_in, c_ffn + c_mrg + c_in, s_ffn + s_mrg + s_in)))
    return gx.reshape(B, S, D), gbig, small_all


SMALL_ROWS = 16


def _pack_small(gs, loss_lanes):
    D = D_MODEL
    rb = jnp.pad(gs["rel_bias"].reshape(-1), (0, 3 * D - ATT_HEADS * N_REL)).reshape(3, D)
    rows = [gs["norm_mix"].reshape(1, D), gs["b_gate"].reshape(2, D), gs["norm_ffn"].reshape(1, D),
            gs["norm_final"].reshape(1, D), rb, loss_lanes.reshape(1, D)]
    used = sum(r.shape[0] for r in rows)
    return jnp.concatenate(rows + [jnp.zeros((SMALL_ROWS - used, D), f32)], axis=0)


def kernel(x, norm_mix, w_in, b_gate, rel_bias, w_ret_out, w_att_out, w_out, norm_ffn, w_ffn_gate, w_ffn_up, w_ffn_down, norm_final, loss_target, m_norm_mix, m_w_in, m_b_gate, m_rel_bias, m_w_ret_out, m_w_att_out, m_w_out, m_norm_ffn, m_w_ffn_gate, m_w_ffn_up, m_w_ffn_down, m_norm_final, v_norm_mix, v_w_in, v_b_gate, v_rel_bias, v_w_ret_out, v_w_att_out, v_w_out, v_norm_ffn, v_w_ffn_gate, v_w_ffn_up, v_w_ffn_down, v_norm_final):
    w = dict(norm_mix=norm_mix, w_in=w_in, b_gate=b_gate, rel_bias=rel_bias, w_ret_out=w_ret_out, w_att_out=w_att_out,
             w_out=w_out, norm_ffn=norm_ffn, w_ffn_gate=w_ffn_gate, w_ffn_up=w_ffn_up, w_ffn_down=w_ffn_down,
             norm_final=norm_final)
    m = dict(norm_mix=m_norm_mix, w_in=m_w_in, b_gate=m_b_gate, rel_bias=m_rel_bias, w_ret_out=m_w_ret_out,
             w_att_out=m_w_att_out, w_out=m_w_out, norm_ffn=m_norm_ffn, w_ffn_gate=m_w_ffn_gate, w_ffn_up=m_w_ffn_up,
             w_ffn_down=m_w_ffn_down, norm_final=m_norm_final)
    v = dict(norm_mix=v_norm_mix, w_in=v_w_in, b_gate=v_b_gate, rel_bias=v_rel_bias, w_ret_out=v_w_ret_out,
             w_att_out=v_w_att_out, w_out=v_w_out, norm_ffn=v_norm_ffn, w_ffn_gate=v_w_ffn_gate, w_ffn_up=v_w_ffn_up,
             w_ffn_down=v_w_ffn_down, norm_final=v_norm_final)
    xi, yi, ci = _place()
    k_me = 2 * xi + yi

    place = jnp.stack([ci, k_me]).astype(jnp.int32)
    big = [n for n, _ in BIG]

    turned = ("w_ffn_gate", "w_ffn_up")
    shard = lambda d, n: jnp.swapaxes(d[n][0], 0, 1) if n in turned else d[n][0]
    whole = lambda a, n: (jnp.swapaxes(a, 0, 1) if n in turned else a)[None]

    by_shape = [[big[i] for i in group] for group in _same_shape([shard(w, n) for n in big])]
    bufs = {}
    for names in by_shape:
        bufs.update(zip(names, _cast_shards(place, [shard(w, n) for n in names], "cast_" + names[0])))
    rest = {n: bufs[n] for n in big if n != "w_in"}
    nrel_loc = rel_bias.shape[-1]
    grad_x, gbig, small_all = _local_step(place, x, loss_target, norm_mix, b_gate, rel_bias[0], norm_ffn, norm_final,
                                          bufs["w_in"], rest, _Exchange(place))

    small = _sum_slots(small_all, "reduce_small")
    D = D_MODEL
    loss = jnp.sum(small[8])
    drb_full = small[5:8].reshape(-1)[:ATT_HEADS * N_REL].reshape(ATT_HEADS, N_REL)
    g = {
        "norm_mix": small[0:1], "b_gate": small[1:3].reshape(1, 2 * D), "norm_ffn": small[3:4], "norm_final": small[4],
        "rel_bias": lax.dynamic_slice_in_dim(drb_full, k_me * nrel_loc, nrel_loc, axis=1)[None],
    }

    delta, new_m, new_v = {}, {}, {}
    for names in by_shape:
        res = _adamw_sum(place, [(shard(w, n), shard(m, n), shard(v, n), *gbig[n]) for n in names], "adamw_" + names[0])
        for n, (g_, d_, m_, v_) in zip(names, res):
            g[n], delta[n], new_m[n], new_v[n] = whole(g_, n), whole(d_, n), whole(m_, n), whole(v_, n)
    flat = lambda d: jnp.concatenate([d[n].reshape(-1) for n in SMALL])
    n_small = sum(int(np.prod(w[n].shape)) for n in SMALL)
    n_pad = -n_small % 1024
    packs = [jnp.pad(flat(d), (0, n_pad)).reshape(-1, 128) for d in (w, g, m, v)]
    outs = _adamw(*packs, "adamw_small")
    for res, dst in zip(outs, (delta, new_m, new_v)):
        off = 0
        fl = res.reshape(-1)
        for n in SMALL:
            sz = int(np.prod(w[n].shape))
            dst[n] = fl[off:off + sz].reshape(w[n].shape)
            off += sz

    return (loss, grad_x, *[g[n] for n in WEIGHTS], *[delta[n] for n in WEIGHTS], *[new_m[n] for n in WEIGHTS],
            *[new_v[n] for n in WEIGHTS])
```

```python
import functools

import numpy as np
import jax
import jax.numpy as jnp
from jax import lax
from jax.experimental import pallas as pl
from jax.experimental.pallas import tpu as pltpu

f32 = jnp.float32
bf16 = jnp.bfloat16

D_MODEL = 1024
CHUNK = 64
RET_HEADS = 4
RET_KEY_DIM = 128
RET_VAL_DIM = 256
ATT_HEADS = 8
ATT_HEAD_DIM = 64
ATT_W = ATT_HEADS * ATT_HEAD_DIM
BAND_CHUNKS = 8
PAD = BAND_CHUNKS * CHUNK
MAX_REL = 256
N_REL = CHUNK + MAX_REL
D_FF = 2816
N_IN = 6656
ROPE_BASE = 10000.0
EPS = 1e-6
NEG_INF = -1e30
C_RQ, C_RK, C_RV, C_RG, C_AQ, C_AK, C_AV, C_GL = 0, 512, 1024, 2048, 3072, 3584, 4096, 4608

ADAM_LR, ADAM_B1, ADAM_B2, ADAM_EPS, ADAM_WD, ADAM_STEP = 0.001, 0.9, 0.999, 1e-08, 0.01, 10

N_CHIPS = 4
N_DEV = 8
WGRAD_ACC_BYTES = 8 * 1024 * 1024
ROW_TILE = 512
BIG_ROW_TILE = 1024
IN_ORDER = (0, 2, 3, 1)
QBLK = 256
KWIN = PAD + QBLK
TOEP = 1024
VMEM_LIMIT = 56 * 1024 * 1024
MESH = pl.DeviceIdType.MESH

BIG = (
    ("w_in", 1), ("w_ret_out", 0), ("w_att_out", 1), ("w_out", 0), ("w_ffn_gate", 1), ("w_ffn_up", 1), ("w_ffn_down", 0))
WEIGHTS = ("norm_mix", "w_in", "b_gate", "rel_bias", "w_ret_out", "w_att_out", "w_out", "norm_ffn", "w_ffn_gate",
           "w_ffn_up", "w_ffn_down", "norm_final")
SMALL = ("norm_mix", "b_gate", "rel_bias", "norm_ffn", "norm_final")


def _dot(a, b):
    return lax.dot_general(a, b, (((1,), (0,)), ((), ())), preferred_element_type=f32)


def _dot_nt(a, b):
    return lax.dot_general(a, b, (((1,), (1,)), ((), ())), preferred_element_type=f32)


def _dot_tn(a, b):
    return lax.dot_general(a, b, (((0,), (0,)), ((), ())), preferred_element_type=f32)


def _sig(x):
    return 1.0 / (1.0 + jnp.exp(-x))


def _tile(n, pref, mult):
    best = None
    for t in range(mult, min(n, pref) + 1, mult):
        if n % t == 0:
            best = t
    return best if best is not None else n


def _same_shape(arrays):
    groups = {}
    for i, a in enumerate(arrays):
        groups.setdefault(a.shape, []).append(i)
    return list(groups.values())


def _params(sem, vmem=VMEM_LIMIT):
    return pltpu.CompilerParams(dimension_semantics=sem, vmem_limit_bytes=vmem)


def _in_proj(place, x2, gamma, phase):
    T, D = x2.shape
    _, _, ns = phase.arrays[0].shape
    tm = _tile(T, BIG_ROW_TILE, 8)
    ni = T // tm
    pass_chip = lambda j: sum(jnp.where(j == n, f, 0) for n, f in enumerate(IN_ORDER))

    def body(p_ref, x_ref, g_ref, xn_ref, pr_ref, xs_ref, w_ref, w_sem, carried):
        j, i = pl.program_id(0), pl.program_id(1)
        pin, pout, sems = carried
        rows = pl.ds(pl.multiple_of(i * tm, tm), tm)

        @pl.when(i == 0)
        def _():
            for n, f in enumerate(IN_ORDER):
                if f:
                    @pl.when(j == n)
                    def _():
                        phase.arrived(f - 1, pin, pout, *sems)
                        phase.begin(2 + f, pin, pout, *sems)
                        phase.arrived(2 + f, pin, pout, *sems)
            shard = pltpu.make_async_copy(pout[0].at[jnp.bitwise_xor(p_ref[1], pass_chip(j))], w_ref, w_sem)
            shard.start()
            shard.wait()

        @pl.when(j == 0)
        def _():
            x = x_ref[...]
            r = lax.rsqrt(jnp.mean(x * x, axis=-1, keepdims=True) + EPS)
            xn = (x * r * g_ref[...]).astype(bf16)
            xs_ref[rows, :] = xn
            xn_ref[...] = xn

        pr_ref[...] = _dot(xs_ref[rows, :], w_ref[...]).astype(bf16)

    first_pass = lambda j, i, p: (jnp.where(j == 0, i, ni - 1), 0)
    return _call(
        body, phase, name="in_proj", grid=(N_CHIPS, ni), prefetch=(place,), expose=True,
        in_specs=[pl.BlockSpec((tm, D), first_pass), pl.BlockSpec((1, D), lambda j, i, p: (0, 0))],
        out_specs=[pl.BlockSpec((tm, D), first_pass),
                   pl.BlockSpec((tm, ns), lambda j, i, p: (i, jnp.bitwise_xor(p[1], pass_chip(j))))],
        out_shape=[jax.ShapeDtypeStruct((T, D), bf16), jax.ShapeDtypeStruct((T, N_CHIPS * ns), bf16)],
        scratch_shapes=[pltpu.VMEM((T, D), bf16), pltpu.VMEM((D, ns), bf16), pltpu.SemaphoreType.DMA],
        args=(x2, gamma))


def _rope_tables(S):
    d = RET_KEY_DIM
    freqs = ROPE_BASE ** (-jnp.arange(0, d, 2, dtype=f32) / d)
    ang = jnp.arange(S, dtype=f32)[:, None] * freqs[None, :]
    cos, sin = jnp.cos(ang), jnp.sin(ang)
    return jnp.concatenate([cos, cos], axis=1), jnp.concatenate([-sin, sin], axis=1)


def _decay_tables():
    H = RET_HEADS
    log_g = jnp.log(1.0 - 2.0 ** (-5.0 - jnp.arange(H, dtype=f32)))
    p = jnp.arange(CHUNK, dtype=f32)
    intra = jnp.exp(log_g[:, None, None] * jnp.abs(p[:, None] - p[None, :]))
    q_dec = jnp.exp(log_g[:, None] * (p[None, :] + 1.0))
    k_dec = jnp.exp(log_g[:, None] * (CHUNK - 1.0 - p[None, :]))
    c_dec = jnp.exp(log_g * CHUNK)
    q_dec = jnp.broadcast_to(q_dec[:, :, None], (H, CHUNK, RET_KEY_DIM))
    k_dec = jnp.broadcast_to(k_dec[:, :, None], (H, CHUNK, RET_KEY_DIM))
    c_dec = jnp.broadcast_to(c_dec[:, None, None], (H, 1, RET_VAL_DIM))
    return intra, q_dec, k_dec, c_dec


K_SCALE = RET_KEY_DIM ** -0.5


RET_CHUNKS = 4


def _ret_tables_specs():
    whole = lambda *shape: pl.BlockSpec(shape, lambda b, i: (0,) * len(shape))
    return [whole(RET_HEADS, CHUNK, CHUNK), whole(RET_HEADS, CHUNK, RET_KEY_DIM), whole(RET_HEADS, CHUNK, RET_KEY_DIM),
            whole(RET_HEADS, 1, RET_VAL_DIM)]


def _rotate(x, cos, sn):
    return x * cos + pltpu.roll(x, RET_KEY_DIM // 2, 1) * sn


def _ret_fwd(proj, B, S, rope, decay, phase=None):
    T = B * S
    nc = S // CHUNK
    H, dk, dv = RET_HEADS, RET_KEY_DIM, RET_VAL_DIM
    sb = RET_CHUNKS * CHUNK
    ns = S // sb

    def body(q_ref, k_ref, v_ref, g_ref, cos_ref, sin_ref, intra_ref, qd_ref, kd_ref, cd_ref,
             qr_ref, kr_ref, o_ref, u_ref, st_ref, state_ref):
        @pl.when(pl.program_id(1) == 0)
        def _():
            state_ref[...] = jnp.zeros_like(state_ref)

        cos, sn = cos_ref[...], sin_ref[...]
        for h in range(H):
            hs = slice(h * dk, (h + 1) * dk)
            qr_ref[:, hs] = _rotate(q_ref[:, hs].astype(f32), cos, sn).astype(bf16)
            kr_ref[:, hs] = (_rotate(k_ref[:, hs].astype(f32), cos, sn) * K_SCALE).astype(bf16)
        states = [state_ref[h] for h in range(H)]
        for ci in range(RET_CHUNKS):
            r = slice(ci * CHUNK, (ci + 1) * CHUNK)
            for h in range(H):
                hk, hv = slice(h * dk, (h + 1) * dk), slice(h * dv, (h + 1) * dv)
                qi, ki, vi = qr_ref[r, hk], kr_ref[r, hk], v_ref[r, hv]
                stb = states[h].astype(bf16)
                st_ref[0, h, ci] = stb
                s = (_dot_nt(qi, ki) * intra_ref[h]).astype(bf16)
                o = _dot(s, vi) + _dot((qi.astype(f32) * qd_ref[h]).astype(bf16), stb)
                states[h] = states[h] * cd_ref[h] + _dot_tn((ki.astype(f32) * kd_ref[h]).astype(bf16), vi)
                mu = jnp.mean(o, axis=-1, keepdims=True)
                xc = o - mu
                var = jnp.mean(xc * xc, axis=-1, keepdims=True)
                oh = xc * lax.rsqrt(var + EPS)
                g = g_ref[r, hv].astype(f32)
                o_ref[r, hv] = o.astype(bf16)
                u_ref[r, hv] = (g * _sig(g) * oh).astype(bf16)
        for h in range(H):
            state_ref[h] = states[h]

    blk = lambda w, c: pl.BlockSpec((sb, w), lambda b, i: (b * ns + i, c))
    return _call(
        body, phase, name="ret_fwd", grid=(B, ns), scratch_shapes=[pltpu.VMEM((H, dk, dv), f32)],
        in_specs=[blk(H * dk, C_RQ // (H * dk)), blk(H * dk, C_RK // (H * dk)), blk(H * dv, C_RV // (H * dv)),
                  blk(H * dv, C_RG // (H * dv)),
                  pl.BlockSpec((sb, dk), lambda b, i: (i, 0)), pl.BlockSpec((sb, dk), lambda b, i: (i, 0)),
                  *_ret_tables_specs()],
        out_specs=[blk(H * dk, 0), blk(H * dk, 0), blk(H * dv, 0), blk(H * dv, 0),
                   pl.BlockSpec((1, H, RET_CHUNKS, dk, dv), lambda b, i: (b, 0, i, 0, 0))],
        out_shape=[jax.ShapeDtypeStruct((T, H * dk), bf16), jax.ShapeDtypeStruct((T, H * dk), bf16),
                   jax.ShapeDtypeStruct((T, H * dv), bf16), jax.ShapeDtypeStruct((T, H * dv), bf16),
                   jax.ShapeDtypeStruct((B, H, nc, dk, dv), bf16)],
        args=(proj, proj, proj, proj, *rope, *decay))


def _bias_rows(rb):
    last = rb[:, N_REL - 1:]
    return jnp.concatenate([
        jnp.broadcast_to(last, (ATT_HEADS, PAD - MAX_REL + 1)),
        jnp.flip(rb[:, :N_REL - 1], axis=1),
        jnp.broadcast_to(rb[:, :1], (ATT_HEADS, KWIN - PAD - CHUNK)),
        jnp.broadcast_to(last, (ATT_HEADS, TOEP - KWIN)),
    ], axis=1)


def _build_bias(t_ref, bias_ref):
    row = lax.broadcasted_iota(jnp.int32, (QBLK, KWIN), 0) // CHUNK
    col = lax.broadcasted_iota(jnp.int32, (QBLK, KWIN), 1) // CHUNK
    delta = BAND_CHUNKS + row - col
    vis = (delta >= 0) & (delta <= BAND_CHUNKS)
    for h in range(ATT_HEADS):
        t = jnp.broadcast_to(t_ref[h:h + 1, :], (QBLK, TOEP))
        rolled = pltpu.roll(t, 0, 1, stride=1, stride_axis=0)
        bias_ref[h] = jnp.where(vis, rolled[:, :KWIN], NEG_INF)


ATT_SCALE = ATT_HEAD_DIM ** -0.5


def _att_probs(qh, kh, bias):
    s = _dot_nt(qh, kh) + bias
    m = jnp.max(s, axis=-1, keepdims=True)
    p = jnp.exp(s - m)
    return p * (1.0 / jnp.sum(p, axis=-1, keepdims=True))


def _first_of_pair():
    return lax.broadcasted_iota(jnp.int32, (1, 2 * ATT_HEAD_DIM), 1) < ATT_HEAD_DIM


def _by_window(i, step):
    sizes = list(range(QBLK, KWIN, QBLK))
    for n, nk in enumerate(sizes):
        pl.when(i == n)(functools.partial(step, nk))
    pl.when(i >= len(sizes))(functools.partial(step, KWIN))


def _att_fwd(proj, trows, B, S, phase=None):
    T = B * S
    nq = S // QBLK
    dh = ATT_HEAD_DIM

    def body(q_ref, k_ref, v_ref, t_ref, o_ref, bias_ref):
        i = pl.program_id(1)

        @pl.when((pl.program_id(0) == 0) & (i == 0))
        def _():
            _build_bias(t_ref, bias_ref)

        def step(nk):
            win = pl.ds(pl.multiple_of((i + 1) * QBLK - nk, QBLK), nk)
            kw, vw = k_ref[win, :], v_ref[win, :]
            first = _first_of_pair()
            outs = []
            for p in range(ATT_HEADS // 2):
                ps = slice(2 * p * dh, 2 * (p + 1) * dh)
                q2, k2, v2 = q_ref[:, ps] * ATT_SCALE, kw[:, ps], vw[:, ps]
                both = []
                for e in range(2):
                    qm = jnp.where(first == (e == 0), q2, jnp.zeros_like(q2))
                    pr = _att_probs(qm, k2, bias_ref[2 * p + e, :, KWIN - nk:])
                    both.append(_dot(pr.astype(bf16), v2))
                outs.append(jnp.where(first, both[0], both[1]))
            o_ref[...] = jnp.concatenate(outs, axis=1).astype(bf16)

        _by_window(i, step)

    return _call(
        body, phase, name="att_fwd", grid=(B, nq),
        in_specs=[pl.BlockSpec((QBLK, ATT_W), lambda b, i: (b * nq + i, C_AQ // ATT_W)),
                  pl.BlockSpec((S, ATT_W), lambda b, i: (b, C_AK // ATT_W)),
                  pl.BlockSpec((S, ATT_W), lambda b, i: (b, C_AV // ATT_W)),
                  pl.BlockSpec((ATT_HEADS, TOEP), lambda b, i: (0, 0))],
        out_specs=[pl.BlockSpec((QBLK, ATT_W), lambda b, i: (b * nq + i, 0))],
        out_shape=[jax.ShapeDtypeStruct((T, ATT_W), bf16)],
        scratch_shapes=[pltpu.VMEM((ATT_HEADS, QBLK, KWIN), f32)],
        args=(proj, proj, proj, trows))


def _gl_specs(tm):
    w = 512
    return [pl.BlockSpec((tm, w), functools.partial(lambda i, j: (i, C_GL // 512 + j), j=j)) for j in range(4)]


def _gates(gl_refs, bg_ref):
    gl = jnp.concatenate([r[...] for r in gl_refs], axis=1).astype(f32) + bg_ref[...]
    g = _sig(gl)
    return g[:, :D_MODEL], g[:, D_MODEL:]


def _mix_fwd(x2, proj, u, ao, b_gate, w_ro, w_ao, w_out, phase=None):
    T, D = x2.shape
    tm = _tile(T, ROW_TILE, 8)

    def body(x_ref, u_ref, ao_ref, g0, g1, g2, g3, bg_ref, wro_ref, wao_ref, wo_ref, h1_ref, yr_ref, ya_ref):
        yr = _dot(u_ref[...], wro_ref[...])
        ao = ao_ref[...]
        ya = jnp.concatenate([_dot(ao, wao_ref[k]) for k in range(N_CHIPS)], axis=1)
        gr, ga = _gates((g0, g1, g2, g3), bg_ref)
        mix = gr * yr + ga * ya
        h1_ref[...] = x_ref[...] + _dot(mix.astype(bf16), wo_ref[...])
        yr_ref[...] = yr.astype(bf16)
        ya_ref[...] = ya.astype(bf16)

    full = lambda a: pl.BlockSpec(a.shape, lambda i: (0,) * a.ndim)
    row = lambda n: pl.BlockSpec((tm, n), lambda i: (i, 0))
    return _call(
        body, phase, name="mix_fwd", grid=(T // tm,), scratch_shapes=[],
        in_specs=[row(D), row(D), row(ATT_W), *_gl_specs(tm), full(b_gate), full(w_ro), full(w_ao), full(w_out)],
        out_specs=[row(D), row(D), row(D)],
        out_shape=[jax.ShapeDtypeStruct((T, D), f32), jax.ShapeDtypeStruct((T, D), bf16),
                   jax.ShapeDtypeStruct((T, D), bf16)],
        args=(x2, u, ao, proj, proj, proj, proj, b_gate, w_ro, w_ao, w_out))


def _ffn_fwd(h1, g_ffn, wg, wu, wd, g_fin, target):
    T, D = h1.shape
    nf, tf, _ = wg.shape
    tm = _tile(T, ROW_TILE, 8)

    def body(h1_ref, g_ref, wg_ref, wu_ref, wd_ref, gf_ref, tg_ref, hn_ref, a_ref, b_ref, f_ref, dh2_ref, part_ref):
        h1v = h1_ref[...]
        r = lax.rsqrt(jnp.mean(h1v * h1v, axis=-1, keepdims=True) + EPS)
        hn = (h1v * r * g_ref[...]).astype(bf16)
        hn_ref[...] = hn
        h2 = h1v
        for k in range(nf):
            a = _dot_nt(hn, wg_ref[k])
            b = _dot_nt(hn, wu_ref[k])
            f = ((a * _sig(a)) * b).astype(bf16)
            a_ref[k] = a.astype(bf16)
            b_ref[k] = b.astype(bf16)
            f_ref[k] = f
            h2 = h2 + _dot(f, wd_ref[k])
        r = lax.rsqrt(jnp.mean(h2 * h2, axis=-1, keepdims=True) + EPS)
        n = h2 * r
        gf = gf_ref[...]
        e = n * gf - tg_ref[...]
        dy = e * (1.0 / D)
        dn = dy * gf
        dh2_ref[...] = r * (dn - n * jnp.mean(dn * n, axis=-1, keepdims=True))
        part_ref[...] = jnp.zeros_like(part_ref)
        part_ref[0:1, :] = jnp.sum(dy * n, axis=0, keepdims=True)
        part_ref[1:2, :] = (0.5 / D) * jnp.sum(e * e, axis=0, keepdims=True)

    row = lambda n: pl.BlockSpec((tm, n), lambda i: (i, 0))
    vec = pl.BlockSpec((1, D), lambda i: (0, 0))
    col = pl.BlockSpec((nf, tm, tf), lambda i: (0, i, 0))
    held = lambda w: pl.BlockSpec(w.shape, lambda i: (0, 0, 0), pipeline_mode=pl.Buffered(1))
    act = jax.ShapeDtypeStruct((nf, T, tf), bf16)
    return pl.pallas_call(
        body, name="ffn_fwd", grid=(T // tm,),
        in_specs=[row(D), vec, held(wg), held(wu), held(wd), vec, row(D)],
        out_specs=[row(D), col, col, col, row(D), pl.BlockSpec((8, D), lambda i: (i, 0))],
        out_shape=[jax.ShapeDtypeStruct((T, D), bf16), act, act, act,
                   jax.ShapeDtypeStruct((T, D), f32), jax.ShapeDtypeStruct((T // tm * 8, D), f32)],
        compiler_params=_params(("parallel",)),
    )(h1, g_ffn, wg, wu, wd, g_fin, target)


def _ffn_bwd(dh2, h1, g_ffn, a, b, wg, wu, wd):
    T, D = h1.shape
    nf, tf, _ = wg.shape
    tm = _tile(T, ROW_TILE // 2, 8)

    def body(dh2_ref, h1_ref, g_ref, a_ref, b_ref, wg_ref, wu_ref, wd_ref, da_ref, db_ref, dh1_ref, part_ref):
        dh2v = dh2_ref[...]
        dh2b = dh2v.astype(bf16)
        dhn = jnp.zeros((tm, D), f32)
        for k in range(nf):
            df = _dot_nt(dh2b, wd_ref[k])
            av = a_ref[k].astype(f32)
            sg = _sig(av)
            db = (df * (av * sg)).astype(bf16)
            da = (df * b_ref[k].astype(f32) * (sg * (1.0 + av * (1.0 - sg)))).astype(bf16)
            da_ref[k] = da
            db_ref[k] = db
            dhn = dhn + _dot(da, wg_ref[k]) + _dot(db, wu_ref[k])
        h = h1_ref[...]
        r = lax.rsqrt(jnp.mean(h * h, axis=-1, keepdims=True) + EPS)
        n = h * r
        dn = dhn * g_ref[...]
        dh1_ref[...] = dh2v + r * (dn - n * jnp.mean(dn * n, axis=-1, keepdims=True))
        part_ref[...] = jnp.zeros_like(part_ref)
        part_ref[0:1, :] = jnp.sum(dhn * n, axis=0, keepdims=True)

    row = lambda n: pl.BlockSpec((tm, n), lambda i: (i, 0))
    col = pl.BlockSpec((nf, tm, tf), lambda i: (0, i, 0))
    held = lambda w: pl.BlockSpec(w.shape, lambda i: (0, 0, 0), pipeline_mode=pl.Buffered(1))
    act = jax.ShapeDtypeStruct((nf, T, tf), bf16)
    return pl.pallas_call(
        body, name="ffn_bwd", grid=(T // tm,),
        in_specs=[row(D), row(D), pl.BlockSpec((1, D), lambda i: (0, 0)), col, col, held(wg), held(wu), held(wd)],
        out_specs=[col, col, row(D), pl.BlockSpec((8, D), lambda i: (i, 0))],
        out_shape=[act, act, jax.ShapeDtypeStruct((T, D), f32), jax.ShapeDtypeStruct((T // tm * 8, D), f32)],
        compiler_params=_params(("parallel",)),
    )(dh2, h1, g_ffn, a, b, wg, wu, wd)


def _mix_bwd(dh1, proj, yr, ya, b_gate, w_ro, w_ao, w_out, phase=None):
    T, D = dh1.shape
    tm = _tile(T, ROW_TILE, 8)

    def body(dh1_ref, g0, g1, g2, g3, bg_ref, yr_ref, ya_ref, wro_ref, wao_ref, wo_ref,
             du_ref, dao_ref, dgl_ref, mix_ref, dyr_ref, dya_ref, part_ref):
        dmix = _dot_nt(dh1_ref[...].astype(bf16), wo_ref[...])
        gr, ga = _gates((g0, g1, g2, g3), bg_ref)
        yr = yr_ref[...].astype(f32)
        ya = ya_ref[...].astype(f32)
        dyr = (dmix * gr).astype(bf16)
        dya = (dmix * ga).astype(bf16)
        dgl = jnp.concatenate([dmix * yr * gr * (1.0 - gr), dmix * ya * ga * (1.0 - ga)], axis=1)
        du_ref[...] = _dot_nt(dyr, wro_ref[...]).astype(bf16)
        ns = wao_ref.shape[2]
        dao = _dot_nt(dya[:, :ns], wao_ref[0])
        for k in range(1, N_CHIPS):
            dao = dao + _dot_nt(dya[:, k * ns:(k + 1) * ns], wao_ref[k])
        dao_ref[...] = dao.astype(bf16)
        dgl_ref[...] = dgl.astype(bf16)
        mix_ref[...] = (gr * yr + ga * ya).astype(bf16)
        dyr_ref[...] = dyr
        dya_ref[...] = dya
        part_ref[...] = jnp.zeros_like(part_ref)
        part_ref[0:1, :] = jnp.sum(dgl, axis=0, keepdims=True)

    full = lambda a: pl.BlockSpec(a.shape, lambda i: (0,) * a.ndim)
    row = lambda n: pl.BlockSpec((tm, n), lambda i: (i, 0))
    return _call(
        body, phase, name="mix_bwd", grid=(T // tm,), scratch_shapes=[],
        in_specs=[row(D), *_gl_specs(tm), full(b_gate), row(D), row(D), full(w_ro), full(w_ao), full(w_out)],
        out_specs=[row(D), row(ATT_W), row(2 * D), row(D), row(D), row(D), pl.BlockSpec((8, 2 * D), lambda i: (i, 0))],
        out_shape=[jax.ShapeDtypeStruct((T, D), bf16), jax.ShapeDtypeStruct((T, ATT_W), bf16),
                   jax.ShapeDtypeStruct((T, 2 * D), bf16), jax.ShapeDtypeStruct((T, D), bf16),
                   jax.ShapeDtypeStruct((T, D), bf16), jax.ShapeDtypeStruct((T, D), bf16),
                   jax.ShapeDtypeStruct((T // tm * 8, 2 * D), f32)],
        args=(dh1, proj, proj, proj, proj, b_gate, yr, ya, w_ro, w_ao, w_out))


def _ret_bwd(proj, qr, kr, o, states, du, B, S, rope, decay, phase=None):
    T = B * S
    nc = S // CHUNK
    H, dk, dv = RET_HEADS, RET_KEY_DIM, RET_VAL_DIM

    sb = RET_CHUNKS * CHUNK
    ns = S // sb

    def body(qr_ref, kr_ref, v_ref, g_ref, o_ref, st_ref, du_ref, cos_ref, sin_ref, intra_ref, qd_ref, kd_ref, cd_ref,
             dp_ref, dstate_ref):
        dq_ref, dk_ref = dp_ref.at[:, pl.ds(C_RQ, H * dk)], dp_ref.at[:, pl.ds(C_RK, H * dk)]
        dv_ref, dg_ref = dp_ref.at[:, pl.ds(C_RV, H * dv)], dp_ref.at[:, pl.ds(C_RG, H * dv)]

        @pl.when(pl.program_id(1) == 0)
        def _():
            dstate_ref[...] = jnp.zeros_like(dstate_ref)

        cos, snb = cos_ref[...], -sin_ref[...]
        dstates = [dstate_ref[h] for h in range(H)]
        for ci in reversed(range(RET_CHUNKS)):
            r = slice(ci * CHUNK, (ci + 1) * CHUNK)
            for h in range(H):
                hk, hv = slice(h * dk, (h + 1) * dk), slice(h * dv, (h + 1) * dv)
                intra, qd, kd = intra_ref[h], qd_ref[h], kd_ref[h]
                qi, ki, vi = qr_ref[r, hk], kr_ref[r, hk], v_ref[r, hv]
                si = st_ref[0, h, ci]
                o = o_ref[r, hv].astype(f32)
                mu = jnp.mean(o, axis=-1, keepdims=True)
                xc = o - mu
                rstd = lax.rsqrt(jnp.mean(xc * xc, axis=-1, keepdims=True) + EPS)
                oh = xc * rstd
                g = g_ref[r, hv].astype(f32)
                sg = _sig(g)
                dui = du_ref[r, hv].astype(f32)
                dg_ref[r, hv] = (dui * oh * (sg * (1.0 + g * (1.0 - sg)))).astype(bf16)
                doh = dui * (g * sg)
                do = rstd * (doh - jnp.mean(doh, axis=-1, keepdims=True)
                             - oh * jnp.mean(doh * oh, axis=-1, keepdims=True))
                dob = do.astype(bf16)
                p = (_dot_nt(qi, ki) * intra).astype(bf16)
                dsb = dstates[h].astype(bf16)
                kt = (ki.astype(f32) * kd).astype(bf16)
                qt = (qi.astype(f32) * qd).astype(bf16)
                dv_ref[r, hv] = (_dot_tn(p, dob) + _dot(kt, dsb)).astype(bf16)
                da = (_dot_nt(dob, vi) * intra).astype(bf16)
                dq = _dot(da, ki) + _dot_nt(dob, si) * qd
                dkk = (_dot_tn(da, qi) + _dot_nt(vi, dsb) * kd) * K_SCALE
                dq_ref[r, hk] = _rotate(dq, cos[r], snb[r]).astype(bf16)
                dk_ref[r, hk] = _rotate(dkk, cos[r], snb[r]).astype(bf16)
                dstates[h] = dstates[h] * cd_ref[h] + _dot_tn(qt, dob)
        for h in range(H):
            dstate_ref[h] = dstates[h]

    blk = lambda w, c: pl.BlockSpec((sb, w), lambda b, i: (b * ns + ns - 1 - i, c))
    return _call(
        body, phase, name="ret_bwd", grid=(B, ns),
        in_specs=[blk(H * dk, 0), blk(H * dk, 0), blk(H * dv, C_RV // (H * dv)), blk(H * dv, C_RG // (H * dv)),
                  blk(H * dv, 0),
                  pl.BlockSpec((1, H, RET_CHUNKS, dk, dv), lambda b, i: (b, 0, ns - 1 - i, 0, 0)),
                  blk(H * dv, 0),
                  pl.BlockSpec((sb, dk), lambda b, i: (ns - 1 - i, 0)), pl.BlockSpec((sb, dk), lambda b, i: (ns - 1 - i, 0)),
                  *_ret_tables_specs()],
        out_specs=[blk(C_AQ, 0)], out_shape=[jax.ShapeDtypeStruct((T, N_IN), bf16)],
        scratch_shapes=[pltpu.VMEM((H, dk, dv), f32)],
        args=(qr, kr, proj, proj, o, states, du, *rope, *decay))


def _att_bwd(proj, dao, trows, dproj, B, S, phase=None):
    T = B * S
    nq = S // QBLK
    dh = ATT_HEAD_DIM
    scale = ATT_HEAD_DIM ** -0.5

    def body(q_ref, k_ref, v_ref, do_ref, t_ref, _, dp_ref, vec_ref, bias_ref, dbias_ref, dka_ref, dva_ref):
        b, i = pl.program_id(0), pl.program_id(1)

        @pl.when((b == 0) & (i == 0))
        def _():
            _build_bias(t_ref, bias_ref)
            dbias_ref[...] = jnp.zeros_like(dbias_ref)

        @pl.when(i == 0)
        def _():
            dka_ref[...] = jnp.zeros_like(dka_ref)
            dva_ref[...] = jnp.zeros_like(dva_ref)

        def step(nk):
            win = pl.ds(pl.multiple_of((i + 1) * QBLK - nk, QBLK), nk)
            kw, vw = k_ref[win, :], v_ref[win, :]
            first = _first_of_pair()
            first_rows = lax.broadcasted_iota(jnp.int32, (2 * dh, 1), 0) < dh
            dqs, dks, dvs = [], [], []
            for p in range(ATT_HEADS // 2):
                ps = slice(2 * p * dh, 2 * (p + 1) * dh)
                q2, k2, v2, do2 = q_ref[:, ps] * ATT_SCALE, kw[:, ps], vw[:, ps], do_ref[:, ps]
                dq2, dk2, dv2 = [], [], []
                for e in range(2):
                    h = 2 * p + e
                    mine = first == (e == 0)
                    pr = _att_probs(jnp.where(mine, q2, jnp.zeros_like(q2)), k2, bias_ref[h, :, KWIN - nk:])
                    dp = _dot_nt(jnp.where(mine, do2, jnp.zeros_like(do2)), v2)
                    ds = pr * (dp - jnp.sum(pr * dp, axis=-1, keepdims=True))
                    dbias_ref[h, :, KWIN - nk:] += ds
                    dsb = ds.astype(bf16)
                    dq2.append(_dot(dsb, k2) * ATT_SCALE)
                    dk2.append(_dot_tn(q2, dsb))
                    dv2.append(_dot_tn(do2, pr.astype(bf16)))
                dqs.append(jnp.where(first, dq2[0], dq2[1]))
                dks.append(jnp.where(first_rows, dk2[0], dk2[1]))
                dvs.append(jnp.where(first_rows, dv2[0], dv2[1]))
            dp_ref[pl.ds(pl.multiple_of(i * QBLK, QBLK), QBLK), :ATT_W] = jnp.concatenate(dqs, axis=1).astype(bf16)
            dka_ref[:, win] += jnp.concatenate(dks, axis=0)
            dva_ref[:, win] += jnp.concatenate(dvs, axis=0)

        _by_window(i, step)

        @pl.when(i == nq - 1)
        def _():
            dp_ref[:, ATT_W:2 * ATT_W] = dka_ref[...].T.astype(bf16)
            dp_ref[:, 2 * ATT_W:] = dva_ref[...].T.astype(bf16)

        @pl.when((b == B - 1) & (i == nq - 1))
        def _():
            rr = lax.broadcasted_iota(jnp.int32, (QBLK, QBLK), 0)
            cc = lax.broadcasted_iota(jnp.int32, (QBLK, QBLK), 1)
            flip = jnp.where(rr + cc == QBLK - 1, 1.0, 0.0).astype(bf16)
            for h in range(ATT_HEADS):
                d = dbias_ref[h]
                hi = d.astype(bf16)
                lo = (d - hi.astype(f32)).astype(bf16)
                rev = _dot(flip, hi) + _dot(flip, lo)
                wide = jnp.concatenate([rev, jnp.zeros((QBLK, TOEP - KWIN), f32)], axis=1)
                rolled = pltpu.roll(wide, 0, 1, stride=1, stride_axis=0)
                vec_ref[h:h + 1, :] = jnp.sum(rolled, axis=0, keepdims=True)

    qspec = lambda c: pl.BlockSpec((QBLK, ATT_W), lambda b, i: (b * nq + i, c))
    kspec = lambda c: pl.BlockSpec((S, ATT_W), lambda b, i: (b, c))
    return _call(
        body, phase, name="att_bwd", grid=(B, nq), aliases={5: 0},
        in_specs=[qspec(C_AQ // ATT_W), kspec(C_AK // ATT_W), kspec(C_AV // ATT_W), qspec(0),
                  pl.BlockSpec((ATT_HEADS, TOEP), lambda b, i: (0, 0)), pl.BlockSpec(memory_space=pl.ANY)],
        out_specs=[pl.BlockSpec((S, 3 * ATT_W), lambda b, i: (b, C_AQ // (3 * ATT_W))),
                   pl.BlockSpec((ATT_HEADS, TOEP), lambda b, i: (0, 0))],
        out_shape=[jax.ShapeDtypeStruct((T, N_IN), bf16), jax.ShapeDtypeStruct((ATT_HEADS, TOEP), f32)],
        scratch_shapes=[pltpu.VMEM((ATT_HEADS, QBLK, KWIN), f32), pltpu.VMEM((ATT_HEADS, QBLK, KWIN), f32),
                        pltpu.VMEM((ATT_W, S), f32), pltpu.VMEM((ATT_W, S), f32)],
        args=(proj, proj, proj, dao, trows, dproj))


def _in_proj_bwd(dproj, w_in, x2, gamma, dh1, phase=None):
    T, D = x2.shape
    nk, _, tk = w_in.shape
    tm = _tile(T, BIG_ROW_TILE, 8)

    def body(dp_ref, w_ref, x_ref, g_ref, dh1_ref, dx_ref, part_ref, acc_ref):
        j = pl.program_id(1)

        @pl.when(j == 0)
        def _():
            acc_ref[...] = jnp.zeros_like(acc_ref)

        acc_ref[...] += _dot_nt(dp_ref[...], w_ref[0])

        @pl.when(j == nk - 1)
        def _():
            x = x_ref[...]
            r = lax.rsqrt(jnp.mean(x * x, axis=-1, keepdims=True) + EPS)
            n = x * r
            dxn = acc_ref[...]
            dn = dxn * g_ref[...]
            dx_ref[...] = dh1_ref[...] + r * (dn - n * jnp.mean(dn * n, axis=-1, keepdims=True))
            part_ref[...] = jnp.zeros_like(part_ref)
            part_ref[0:1, :] = jnp.sum(dxn * n, axis=0, keepdims=True)

    row = lambda n: pl.BlockSpec((tm, n), lambda i, j: (i, 0))
    return _call(
        body, phase, name="in_proj_bwd", grid=(T // tm, nk),
        in_specs=[pl.BlockSpec((tm, tk), lambda i, j: (i, j)), pl.BlockSpec((1, D, tk), lambda i, j: (j, 0, 0)), row(D),
                  pl.BlockSpec((1, D), lambda i, j: (0, 0)), row(D)],
        out_specs=[row(D), pl.BlockSpec((8, D), lambda i, j: (i, 0))],
        out_shape=[jax.ShapeDtypeStruct((T, D), f32), jax.ShapeDtypeStruct((T // tm * 8, D), f32)],
        scratch_shapes=[pltpu.VMEM((tm, D), f32)],
        args=(dproj, w_in, x2, gamma, dh1))


def _wgrad(a, b, shard_axis, name, phase=None):
    def spec(arr, sharded, tt):
        if arr.ndim == 3:
            return arr.shape[2], pl.BlockSpec((1, tt, arr.shape[2]), lambda s, t: (s, t, 0))
        if sharded:
            w = arr.shape[1] // N_CHIPS
            return w, pl.BlockSpec((tt, w), lambda s, t: (t, s))
        return arr.shape[1], pl.BlockSpec((tt, arr.shape[1]), lambda s, t: (t, 0))

    T = a.shape[-2]
    tt = _tile(T, BIG_ROW_TILE, 16)
    nt = T // tt
    whole = a.ndim == 2 and b.ndim == 2 and a.shape[1] * b.shape[1] * 4 <= WGRAD_ACC_BYTES
    if whole:
        K, N = a.shape[1], b.shape[1]
        a_spec, b_spec = pl.BlockSpec((tt, K), lambda s, t: (t, 0)), pl.BlockSpec((tt, N), lambda s, t: (t, 0))
        out_block = (N_CHIPS, K // N_CHIPS, N) if shard_axis == 0 else (N_CHIPS, K, N // N_CHIPS)
        out_spec = pl.BlockSpec(out_block, lambda s, t: (0, 0, 0))
    else:
        K, a_spec = spec(a, shard_axis == 0, tt)
        N, b_spec = spec(b, shard_axis == 1, tt)
        out_block = (N_CHIPS, K, N)
        out_spec = pl.BlockSpec((1, K, N), lambda s, t: (s, 0, 0))

    def body(a_ref, b_ref, o_ref, acc_ref):
        t = pl.program_id(1)

        @pl.when(t == 0)
        def _():
            acc_ref[...] = jnp.zeros_like(acc_ref)

        av = a_ref[0] if a.ndim == 3 else a_ref[...]
        bv = b_ref[0] if b.ndim == 3 else b_ref[...]
        acc_ref[...] += _dot_tn(av.astype(bf16), bv.astype(bf16))

        @pl.when(t == nt - 1)
        def _():
            if not whole:
                o_ref[0] = acc_ref[...].astype(bf16)
            else:
                _, kk, nn = out_block
                for s in range(N_CHIPS):
                    o_ref[s] = (acc_ref[s * kk:(s + 1) * kk, :] if shard_axis == 0
                                else acc_ref[:, s * nn:(s + 1) * nn]).astype(bf16)

    (grad,), carried = _call(
        body, phase, name=name, grid=(1 if whole else N_CHIPS, nt), in_specs=[a_spec, b_spec], out_specs=[out_spec],
        out_shape=[jax.ShapeDtypeStruct(out_block, bf16)], scratch_shapes=[pltpu.VMEM((K, N), f32)], args=(a, b))
    return grad, carried


def _adamw_sum(place, groups, name):
    n = len(groups)
    R, C = groups[0][0].shape
    half = R // 2
    tr = _tile(half, max(16, (1 << 18) // C // 16 * 16), 16)
    nr = half // tr

    def body(p_ref, *refs):
        for a in range(n):
            w_ref, m_ref, v_ref, part_ref, fc_ref, fs_ref = refs[6 * a:6 * a + 6]
            g_ref, d_ref, mo_ref, vo_ref = refs[6 * n + 4 * a:6 * n + 4 * a + 4]
            up = lambda x: x.astype(f32)
            mine = ((up(part_ref[0]) + up(fc_ref[0])) + up(fc_ref[1])) + up(fc_ref[2])
            sibs = ((up(fs_ref[0]) + up(fs_ref[1])) + up(fs_ref[2])) + up(fs_ref[3])
            g_ = jnp.where(pl.program_id(0) == p_ref[0], mine, sibs)
            m_ = ADAM_B1 * m_ref[...] + (1.0 - ADAM_B1) * g_
            v_ = ADAM_B2 * v_ref[...] + (1.0 - ADAM_B2) * (g_ * g_)
            m_hat = m_ / (1.0 - ADAM_B1 ** ADAM_STEP)
            v_hat = v_ / (1.0 - ADAM_B2 ** ADAM_STEP)
            g_ref[...] = g_
            d_ref[...] = -ADAM_LR * (m_hat / (jnp.sqrt(v_hat) + ADAM_EPS) + ADAM_WD * w_ref[...])
            mo_ref[...] = m_
            vo_ref[...] = v_

    spec = pl.BlockSpec((tr, C), lambda h, r, p: (h * nr + r, 0))
    one = [spec, spec, spec, pl.BlockSpec((1, tr, C), lambda h, r, p: (p[1], jnp.where(h == p[0], r, 0), 0)),
           pl.BlockSpec((3, tr, C), lambda h, r, p: (0, jnp.where(h == p[0], r, 0), 0)),
           pl.BlockSpec((4, tr, C), lambda h, r, p: (0, jnp.where(h == p[0], 0, r), 0))]
    res = pl.pallas_call(
        body, name=name,
        grid_spec=pltpu.PrefetchScalarGridSpec(num_scalar_prefetch=1, grid=(2, nr), in_specs=one * n,
                                               out_specs=[spec] * (4 * n)),
        out_shape=[jax.ShapeDtypeStruct((R, C), f32)] * (4 * n),
        compiler_params=_params(("parallel", "parallel")),
    )(place, *[x for g in groups for x in g])
    return [tuple(res[4 * a:4 * a + 4]) for a in range(n)]


def _adamw(w, g, m, v, name):
    R, C = w.shape
    tr = _tile(R, max(8, (1 << 18) // C // 8 * 8), 8)

    def body(w_ref, g_ref, m_ref, v_ref, d_ref, mo_ref, vo_ref):
        g_ = g_ref[...]
        m_ = ADAM_B1 * m_ref[...] + (1.0 - ADAM_B1) * g_
        v_ = ADAM_B2 * v_ref[...] + (1.0 - ADAM_B2) * (g_ * g_)
        m_hat = m_ / (1.0 - ADAM_B1 ** ADAM_STEP)
        v_hat = v_ / (1.0 - ADAM_B2 ** ADAM_STEP)
        d_ref[...] = -ADAM_LR * (m_hat / (jnp.sqrt(v_hat) + ADAM_EPS) + ADAM_WD * w_ref[...])
        mo_ref[...] = m_
        vo_ref[...] = v_

    spec = pl.BlockSpec((tr, C), lambda i: (i, 0))
    return pl.pallas_call(
        body, name=name, grid=(R // tr,), in_specs=[spec] * 4, out_specs=[spec] * 3,
        out_shape=[jax.ShapeDtypeStruct((R, C), f32)] * 3,
        compiler_params=_params(("parallel",)),
    )(w, g, m, v)


def _place():
    return lax.axis_index("x"), lax.axis_index("y"), lax.axis_index("c")


def _other_chips(x, y):
    chips = [(1 - x, y), (x, 1 - y), (1 - x, 1 - y)]
    return chips, [2 * cx + cy for cx, cy in chips]


def _spread_phase(blk):
    def peers():
        x, y, c = _place()
        return [tuple(1 - p if (k >> s) & 1 else p for p, s in ((x, 2), (y, 1), (c, 0))) for k in range(1, N_DEV)]

    def copies(pin, out):
        x, y, c = _place()
        mine = out[0].at[4 * x + 2 * y + c]
        return [(mine, mine, peer) for peer in peers()]

    stack = jnp.broadcast_to(blk, (N_DEV,) + blk.shape)
    return _Phase([stack], [jax.ShapeDtypeStruct(stack.shape, stack.dtype)], {0: 0}, N_DEV - 1, copies,
                  lambda pin, out: [out[0].at[4 * px + 2 * py + pc] for px, py, pc in peers()])


def _sum_slots(stack, name):
    def body(s_ref, o_ref):
        tot = s_ref[0]
        for d in range(1, stack.shape[0]):
            tot = tot + s_ref[d]
        o_ref[...] = tot

    vm = pl.BlockSpec(memory_space=pltpu.VMEM)
    return pl.pallas_call(body, name=name, in_specs=[vm], out_specs=vm,
                          out_shape=jax.ShapeDtypeStruct(stack.shape[1:], stack.dtype))(stack)


def _cast_shards(place, ws, name):
    n = len(ws)
    R, C = ws[0].shape
    tr = _tile(R, max(16, (1 << 19) // C // 16 * 16), 16)

    def body(p_ref, *refs):
        for a in range(n):
            refs[n + a][0] = refs[a][...].astype(bf16)

    return pl.pallas_call(
        body, name=name,
        grid_spec=pltpu.PrefetchScalarGridSpec(
            num_scalar_prefetch=1, grid=(R // tr,),
            in_specs=[pl.BlockSpec((tr, C), lambda r, p: (r, 0))] * n,
            out_specs=[pl.BlockSpec((1, tr, C), lambda r, p: (p[1], r, 0))] * n),
        out_shape=[jax.ShapeDtypeStruct((N_CHIPS, R, C), bf16)] * n,
        compiler_params=_params(("parallel",)),
    )(place, *ws)


class _Phase:
    def __init__(self, arrays, out_shapes, aliases, n_copies, copies, arrivals, own_starts=(), own_waits=()):
        self.arrays, self.out_shapes, self.aliases = list(arrays), list(out_shapes), dict(aliases)
        self.n_copies, self.copies, self.arrivals = n_copies, copies, arrivals
        self.own_starts, self.own_waits = tuple(own_starts), tuple(own_waits)

    def sems(self):
        return [pltpu.SemaphoreType.DMA((self.n_copies,)), pltpu.SemaphoreType.DMA((self.n_copies,))]

    def _descriptors(self, pin, pout, send_sems, recv_sems):
        return [pltpu.make_async_remote_copy(src_ref=s, dst_ref=d, send_sem=send_sems.at[i], recv_sem=recv_sems.at[i],
                                             device_id=to, device_id_type=MESH)
                for i, (s, d, to) in enumerate(self.copies(pin, pout))]

    def _arrival(self, i, pin, pout, send_sems, recv_sems):
        dst = self.arrivals(pin, pout)[i]
        return pltpu.make_async_remote_copy(src_ref=dst, dst_ref=dst, send_sem=send_sems.at[i], recv_sem=recv_sems.at[i],
                                            device_id=_place(), device_id_type=MESH)

    def start(self, pin, pout, send_sems, recv_sems):
        for i, cp in enumerate(self._descriptors(pin, pout, send_sems, recv_sems)):
            if i not in self.own_starts:
                cp.start()

    def begin(self, i, pin, pout, send_sems, recv_sems):
        self._descriptors(pin, pout, send_sems, recv_sems)[i].start()

    def arrived(self, i, pin, pout, send_sems, recv_sems):
        self._arrival(i, pin, pout, send_sems, recv_sems).wait_recv()

    def finish(self, pin, pout, send_sems, recv_sems):
        for i in range(self.n_copies):
            if i not in self.own_waits:
                self._arrival(i, pin, pout, send_sems, recv_sems).wait_recv()
        for cp in self._descriptors(pin, pout, send_sems, recv_sems):
            cp.wait_send()


def _join(phases):
    if len(phases) == 1:
        return phases[0]
    ai = np.cumsum([0] + [len(p.arrays) for p in phases])
    oi = np.cumsum([0] + [len(p.out_shapes) for p in phases])

    def each(fn_name, pin, pout):
        return [item for k, p in enumerate(phases)
                for item in getattr(p, fn_name)(pin[ai[k]:ai[k + 1]], pout[oi[k]:oi[k + 1]])]

    aliases = {int(ai[k]) + i: int(oi[k]) + j for k, p in enumerate(phases) for i, j in p.aliases.items()}
    ci = np.cumsum([0] + [p.n_copies for p in phases])
    shifted = lambda attr: [int(ci[k]) + i for k, p in enumerate(phases) for i in getattr(p, attr)]
    return _Phase([a for p in phases for a in p.arrays], [s for p in phases for s in p.out_shapes], aliases,
                  int(ci[-1]), functools.partial(each, "copies"), functools.partial(each, "arrivals"),
                  shifted("own_starts"), shifted("own_waits"))


def _call(body, phase, *, name, grid, in_specs, out_specs, out_shape, scratch_shapes, args, prefetch=(), expose=False,
          aliases=None):
    seq = _params(("arbitrary",) * len(grid))
    np_ = len(prefetch)
    own = {np_ + i: j for i, j in (aliases or {}).items()}
    if phase is None:
        spec = pltpu.PrefetchScalarGridSpec(num_scalar_prefetch=np_, grid=grid, in_specs=in_specs, out_specs=out_specs,
                                            scratch_shapes=scratch_shapes)
        res = pl.pallas_call(body, name=name, grid_spec=spec, out_shape=out_shape, input_output_aliases=own,
                             compiler_params=seq)(*prefetch, *args)
        return list(res), []
    ni, no, ns = len(in_specs), len(out_specs), len(scratch_shapes)
    pi, po = len(phase.arrays), len(phase.out_shapes)

    def hosted(*refs):
        cut = np.cumsum([np_, ni, pi, no, po, ns])
        pre, ins, pin, outs, pout, scr, sems = (refs[a:b] for a, b in zip([0, *cut], [*cut, len(refs)]))
        ids = [pl.program_id(d) for d in range(len(grid))]
        first = functools.reduce(lambda p, q: p & q, [i == 0 for i in ids])
        last = functools.reduce(lambda p, q: p & q, [i == g - 1 for i, g in zip(ids, grid)])
        pl.when(first)(lambda: phase.start(pin, pout, *sems))
        body(*pre, *ins, *outs, *scr, **({"carried": (pin, pout, sems)} if expose else {}))
        pl.when(last)(lambda: phase.finish(pin, pout, *sems))

    anyspace = pl.BlockSpec(memory_space=pl.ANY)
    spec = pltpu.PrefetchScalarGridSpec(
        num_scalar_prefetch=np_, grid=grid, in_specs=list(in_specs) + [anyspace] * pi,
        out_specs=list(out_specs) + [anyspace] * po, scratch_shapes=list(scratch_shapes) + phase.sems())
    res = pl.pallas_call(
        hosted, name=name, grid_spec=spec, out_shape=list(out_shape) + phase.out_shapes,
        input_output_aliases={**own, **{np_ + ni + i: no + j for i, j in phase.aliases.items()}}, compiler_params=seq,
    )(*prefetch, *args, *phase.arrays)
    return list(res[:no]), list(res[no:])


def _run_phases(name, phases):
    first = phases[0]
    pi, po = len(first.arrays), len(first.out_shapes)

    def body(*refs):
        pin, pout, sems = refs[:pi], refs[pi:pi + po], refs[pi + po:]
        for n, ph in enumerate(phases):
            ph.start(pin, pout, *sems[2 * n:2 * n + 2])
            ph.finish(pin, pout, *sems[2 * n:2 * n + 2])

    anyspace = pl.BlockSpec(memory_space=pl.ANY)
    return list(pl.pallas_call(
        body, name=name, in_specs=[anyspace] * pi, out_specs=[anyspace] * po, out_shape=first.out_shapes,
        input_output_aliases=first.aliases, scratch_shapes=[s for ph in phases for s in ph.sems()],
    )(*first.arrays))


def _half_rows(buf, c):
    half = buf.shape[1] // 2
    return pl.ds(c * half, half), pl.ds((1 - c) * half, half)


def _gather_phase(bufs, over_ici):
    n = len(bufs)
    shapes = [jax.ShapeDtypeStruct(b.shape, b.dtype) for b in bufs]

    def landed(out, which):
        x, y, c = _place()
        _, ks = _other_chips(x, y)
        return [out[a].at[ks[j], _half_rows(bufs[a], c)[which]] for a in range(n) for j in range(3)]

    def ici(pin, out):
        x, y, c = _place()
        chips, _ = _other_chips(x, y)
        mine = [out[a].at[2 * x + y, _half_rows(bufs[a], c)[0]] for a in range(n)]
        return [(mine[a], mine[a], (*chips[j], c)) for a in range(n) for j in range(3)]

    def d2d(pin, out):
        x, y, c = _place()
        return [(dst, dst, (x, y, 1 - c)) for dst in landed(out, 0)]

    if over_ici:
        return _Phase(bufs, shapes, {a: a for a in range(n)}, 3 * n, ici, lambda pin, out: landed(out, 0))
    return _Phase(bufs, shapes, {a: a for a in range(n)}, 3 * n, d2d, lambda pin, out: landed(out, 1))


def _feed_phase(buf):
    def chips():
        x, y, c = _place()
        return [(x if f < 2 else 1 - x, y if f % 2 == 0 else 1 - y) for f in (1, 2, 3)]

    def copies(pin, out):
        x, y, c = _place()
        mine = _half_rows(buf, c)[0]
        own = out[0].at[2 * x + y, mine]
        sent = [(own, own, (cx, cy, c)) for cx, cy in chips()]
        return sent + [(out[0].at[2 * cx + cy, mine], out[0].at[2 * cx + cy, mine], (x, y, 1 - c)) for cx, cy in chips()]

    def arrivals(pin, out):
        x, y, c = _place()
        mine, theirs = _half_rows(buf, c)
        return [out[0].at[2 * cx + cy, rows] for rows in (mine, theirs) for cx, cy in chips()]

    return _Phase([buf], [jax.ShapeDtypeStruct(buf.shape, buf.dtype)], {0: 0}, 6, copies, arrivals,
                  own_starts=(3, 4, 5), own_waits=range(6))


def _rs_swap_phase(grads):
    n = len(grads)

    def copies(g, out):
        x, y, c = _place()
        return [(g[a].at[:, _half_rows(grads[a], c)[1]], out[a], (x, y, 1 - c)) for a in range(n)]

    shapes = [jax.ShapeDtypeStruct((N_CHIPS, g.shape[1] // 2, g.shape[2]), g.dtype) for g in grads]
    return _Phase(grads, shapes, {}, n, copies, lambda g, out: list(out))


def _rs_add_sibling(place, grads, gots, name):
    n = len(grads)
    _, R, C = grads[0].shape
    half = R // 2
    tr = _tile(half, max(16, (1 << 19) // C // 16 * 16), 16)
    nr = half // tr

    def body(p_ref, *refs):
        for a in range(n):
            refs[2 * n + a][...] = (refs[2 * a][...].astype(f32) + refs[2 * a + 1][...].astype(f32)).astype(bf16)

    res = pl.pallas_call(
        body, name=name,
        grid_spec=pltpu.PrefetchScalarGridSpec(
            num_scalar_prefetch=1, grid=(N_CHIPS, nr),
            in_specs=[pl.BlockSpec((1, tr, C), lambda k, r, p: (k, p[0] * nr + r, 0)),
                      pl.BlockSpec((1, tr, C), lambda k, r, p: (k, r, 0))] * n,
            out_specs=[pl.BlockSpec((1, tr, C), lambda k, r, p: (k, r, 0))] * n),
        out_shape=[jax.ShapeDtypeStruct((N_CHIPS, half, C), bf16)] * n,
        compiler_params=_params(("parallel", "parallel")),
    )(place, *[x for pair in zip(grads, gots) for x in pair])
    return list(res)


def _rs_chips_phase(parts):
    n = len(parts)

    def copies(p, fc):
        x, y, c = _place()
        chips, ks = _other_chips(x, y)
        return [(p[a].at[ks[j]], fc[a].at[j], (*chips[j], c)) for a in range(n) for j in range(3)]

    shapes = [jax.ShapeDtypeStruct((3,) + q.shape[1:], q.dtype) for q in parts]
    return _Phase(parts, shapes, {}, 3 * n, copies, lambda p, fc: [fc[a].at[j] for a in range(n) for j in range(3)])


def _rs_hand_phase(parts, from_chips):
    n = len(parts)

    def copies(pin, fs):
        x, y, c = _place()
        sib = (x, y, 1 - c)
        own = [(pin[a].at[2 * x + y], fs[a].at[0], sib) for a in range(n)]
        return own + [(pin[n + a].at[j], fs[a].at[1 + j], sib) for a in range(n) for j in range(3)]

    def arrivals(pin, fs):
        return [fs[a].at[0] for a in range(n)] + [fs[a].at[1 + j] for a in range(n) for j in range(3)]

    shapes = [jax.ShapeDtypeStruct((4,) + q.shape[1:], q.dtype) for q in parts]
    return _Phase(list(parts) + list(from_chips), shapes, {}, 4 * n, copies, arrivals)


class _Exchange:
    def __init__(self, place):
        self.place = place

    def feed(self, buf):
        return _feed_phase(buf)

    def gather(self, bufs, over_ici):
        return _gather_phase(bufs, over_ici)

    def swap(self, grads):
        return _rs_swap_phase(grads)

    def pair_sums(self, names, grads):
        return self.add(names, grads, _run_phases("rs_sibling_" + names[0], [_rs_swap_phase(grads)]))

    def add(self, names, grads, got):
        parts = {}
        for group in _same_shape(grads):
            res = _rs_add_sibling(self.place, [grads[i] for i in group], [got[i] for i in group],
                                  "rs_add_" + names[group[0]])
            parts.update(zip(group, res))
        return [parts[i] for i in range(len(names))]

    def to_chips(self, parts):
        return _rs_chips_phase(parts)

    def to_sibling(self, parts, from_chips):
        return _rs_hand_phase(parts, from_chips)

    def spread(self, blk):
        return _spread_phase(blk)

    def hand_over(self, name, parts, from_chips, blk):
        got = _run_phases(name, [_join([_rs_hand_phase(parts, from_chips), _spread_phase(blk)])])
        return got[:-1], got[-1]


def _local_step(place, x, target, norm_mix, b_gate, rb_chip, norm_ffn, norm_final, w_in, rest, exch):
    B, S, D = x.shape
    T = B * S
    x2 = x.reshape(T, D)
    tg2 = target.reshape(T, D)
    rope, decay = _rope_tables(S), _decay_tables()
    g_fin = norm_final.reshape(1, D)
    nrel = rb_chip.shape[-1]

    mrg, ffn = ["w_ret_out", "w_att_out", "w_out"], ["w_ffn_gate", "w_ffn_up", "w_ffn_down"]
    (xn, proj), got = _in_proj(place, x2, norm_mix, _join([exch.feed(w_in), exch.gather([rest[n] for n in mrg], True),
                                                           exch.spread(jnp.pad(rb_chip, ((0, 0), (0, 128 - nrel))))]))
    w_in, wb, rb_all = got[0], {}, got.pop()
    trows = _bias_rows(jnp.concatenate([rb_all[2 * k, :, :nrel] for k in range(N_CHIPS)], axis=1))
    (qr, kr, o, u, states), got = _ret_fwd(proj, B, S, rope, decay, _join([exch.gather([rest["w_ffn_gate"]], True),
                                                                         exch.gather(got[1:], False)]))
    wb.update(zip(mrg, got[1:]))
    (ao,), got = _att_fwd(proj, trows, B, S, _join([exch.gather([rest["w_ffn_up"], rest["w_ffn_down"]], True),
                                                    exch.gather(got[:1], False)]))
    wb["w_ffn_gate"] = got[2]
    w_ro, w_out = wb["w_ret_out"].reshape(-1, D), wb["w_out"].reshape(-1, D)
    (h1, yr, ya), got = _mix_fwd(x2, proj, u, ao, b_gate, w_ro, wb["w_att_out"], w_out, exch.gather(got[:2], False))
    wb.update(zip(ffn[1:], got))
    hn, a, b, f, dh2, part_fin = _ffn_fwd(h1, norm_ffn, wb["w_ffn_gate"], wb["w_ffn_up"], wb["w_ffn_down"], g_fin, tg2)

    da, db, dh1, part_ffn = _ffn_bwd(dh2, h1, norm_ffn, a, b, wb["w_ffn_gate"], wb["w_ffn_up"], wb["w_ffn_down"])
    ffn = ["w_ffn_down", "w_ffn_gate", "w_ffn_up"]
    g_ffn = [_wgrad(f, dh2, 0, "wgrad_ffn_down")[0], _wgrad(da, hn, 0, "wgrad_ffn_gate")[0],
             _wgrad(db, hn, 0, "wgrad_ffn_up")[0]]
    (du, dao, dgl, mix, dyr, dya, part_bg), x_ffn = _mix_bwd(dh1, proj, yr, ya, b_gate, w_ro, wb["w_att_out"], w_out,
                                                             exch.swap(g_ffn))
    p_ffn = exch.add(ffn, g_ffn, x_ffn)
    mrg = ["w_out", "w_ret_out", "w_att_out"]
    g_mrg = [_wgrad(mix, dh1, 0, "wgrad_out")[0], _wgrad(u, dyr, 0, "wgrad_ret_out")[0],
             _wgrad(ao, dya, 1, "wgrad_att_out")[0]]
    (dproj,), got = _ret_bwd(proj, qr, kr, o, states, du, B, S, rope, decay, _join([exch.to_chips(p_ffn),
                                                                                     exch.swap(g_mrg)]))
    c_ffn, p_mrg = got[:len(ffn)], exch.add(mrg, g_mrg, got[len(ffn):])
    (dproj, dvec), got = _att_bwd(proj, dao, trows, dproj, B, S, _join([exch.to_chips(p_mrg),
                                                                        exch.to_sibling(p_ffn, c_ffn)]))
    c_mrg, s_ffn = got[:len(mrg)], got[len(mrg):]
    dproj = lax.dynamic_update_slice(dproj, dgl, (0, C_GL))
    g_in, s_mrg = _wgrad(xn, dproj, 1, "wgrad_in", exch.to_sibling(p_mrg, c_mrg))
    p_in = exch.pair_sums(["w_in"], [g_in])
    (gx, part_mix), c_in = _in_proj_bwd(dproj, w_in, x2, norm_mix, dh1, exch.to_chips(p_in))
    rows = lambda p, r: p.reshape(-1, 8, p.shape[-1])[:, r, :].sum(axis=0)
    lo = KWIN - 1 - (MAX_REL - 1)
    drb = jnp.concatenate([jnp.flip(dvec[:, lo:lo + N_REL - 1], axis=1), dvec[:, :lo].sum(axis=1, keepdims=True)], axis=1)
    gsmall = {
        "norm_mix": rows(part_mix, 0), "b_gate": rows(part_bg, 0), "rel_bias": drb, "norm_ffn": rows(part_ffn, 0),
        "norm_final": rows(part_fin, 0),
    }
    s_in, small_all = exch.hand_over("rs_hand_w_in", p_in, c_in, _pack_small(gsmall, rows(part_fin, 1)))
    gbig = dict(zip(ffn + mrg + ["w_in"], zip(p_ffn + p_mrg + p_in, c_ffn + c_mrg + c_in, s_ffn + s_mrg + s_in)))
    return gx.reshape(B, S, D), gbig, small_all


SMALL_ROWS = 16


def _pack_small(gs, loss_lanes):
    D = D_MODEL
    rb = jnp.pad(gs["rel_bias"].reshape(-1), (0, 3 * D - ATT_HEADS * N_REL)).reshape(3, D)
    rows = [gs["norm_mix"].reshape(1, D), gs["b_gate"].reshape(2, D), gs["norm_ffn"].reshape(1, D),
            gs["norm_final"].reshape(1, D), rb, loss_lanes.reshape(1, D)]
    used = sum(r.shape[0] for r in rows)
    return jnp.concatenate(rows + [jnp.zeros((SMALL_ROWS - used, D), f32)], axis=0)


def kernel(x, norm_mix, w_in, b_gate, rel_bias, w_ret_out, w_att_out, w_out, norm_ffn, w_ffn_gate, w_ffn_up, w_ffn_down, norm_final, loss_target, m_norm_mix, m_w_in, m_b_gate, m_rel_bias, m_w_ret_out, m_w_att_out, m_w_out, m_norm_ffn, m_w_ffn_gate, m_w_ffn_up, m_w_ffn_down, m_norm_final, v_norm_mix, v_w_in, v_b_gate, v_rel_bias, v_w_ret_out, v_w_att_out, v_w_out, v_norm_ffn, v_w_ffn_gate, v_w_ffn_up, v_w_ffn_down, v_norm_final):
    w = dict(norm_mix=norm_mix, w_in=w_in, b_gate=b_gate, rel_bias=rel_bias, w_ret_out=w_ret_out, w_att_out=w_att_out,
             w_out=w_out, norm_ffn=norm_ffn, w_ffn_gate=w_ffn_gate, w_ffn_up=w_ffn_up, w_ffn_down=w_ffn_down,
             norm_final=norm_final)
    m = dict(norm_mix=m_norm_mix, w_in=m_w_in, b_gate=m_b_gate, rel_bias=m_rel_bias, w_ret_out=m_w_ret_out,
             w_att_out=m_w_att_out, w_out=m_w_out, norm_ffn=m_norm_ffn, w_ffn_gate=m_w_ffn_gate, w_ffn_up=m_w_ffn_up,
             w_ffn_down=m_w_ffn_down, norm_final=m_norm_final)
    v = dict(norm_mix=v_norm_mix, w_in=v_w_in, b_gate=v_b_gate, rel_bias=v_rel_bias, w_ret_out=v_w_ret_out,
             w_att_out=v_w_att_out, w_out=v_w_out, norm_ffn=v_norm_ffn, w_ffn_gate=v_w_ffn_gate, w_ffn_up=v_w_ffn_up,
             w_ffn_down=v_w_ffn_down, norm_final=v_norm_final)
    xi, yi, ci = _place()
    k_me = 2 * xi + yi

    place = jnp.stack([ci, k_me]).astype(jnp.int32)
    big = [n for n, _ in BIG]

    turned = ("w_ffn_gate", "w_ffn_up")
    shard = lambda d, n: jnp.swapaxes(d[n][0], 0, 1) if n in turned else d[n][0]
    whole = lambda a, n: (jnp.swapaxes(a, 0, 1) if n in turned else a)[None]

    by_shape = [[big[i] for i in group] for group in _same_shape([shard(w, n) for n in big])]
    bufs = {}
    for names in by_shape:
        bufs.update(zip(names, _cast_shards(place, [shard(w, n) for n in names], "cast_" + names[0])))
    rest = {n: bufs[n] for n in big if n != "w_in"}
    nrel_loc = rel_bias.shape[-1]
    grad_x, gbig, small_all = _local_step(place, x, loss_target, norm_mix, b_gate, rel_bias[0], norm_ffn, norm_final,
                                          bufs["w_in"], rest, _Exchange(place))

    small = _sum_slots(small_all, "reduce_small")
    D = D_MODEL
    loss = jnp.sum(small[8])
    drb_full = small[5:8].reshape(-1)[:ATT_HEADS * N_REL].reshape(ATT_HEADS, N_REL)
    g = {
        "norm_mix": small[0:1], "b_gate": small[1:3].reshape(1, 2 * D), "norm_ffn": small[3:4], "norm_final": small[4],
        "rel_bias": lax.dynamic_slice_in_dim(drb_full, k_me * nrel_loc, nrel_loc, axis=1)[None],
    }

    delta, new_m, new_v = {}, {}, {}
    for names in by_shape:
        res = _adamw_sum(place, [(shard(w, n), shard(m, n), shard(v, n), *gbig[n]) for n in names], "adamw_" + names[0])
        for n, (g_, d_, m_, v_) in zip(names, res):
            g[n], delta[n], new_m[n], new_v[n] = whole(g_, n), whole(d_, n), whole(m_, n), whole(v_, n)
    flat = lambda d: jnp.concatenate([d[n].reshape(-1) for n in SMALL])
    n_small = sum(int(np.prod(w[n].shape)) for n in SMALL)
    n_pad = -n_small % 1024
    packs = [jnp.pad(flat(d), (0, n_pad)).reshape(-1, 128) for d in (w, g, m, v)]
    outs = _adamw(*packs, "adamw_small")
    for res, dst in zip(outs, (delta, new_m, new_v)):
        off = 0
        fl = res.reshape(-1)
        for n in SMALL:
            sz = int(np.prod(w[n].shape))
            dst[n] = fl[off:off + sz].reshape(w[n].shape)
            off += sz

    return (loss, grad_x, *[g[n] for n in WEIGHTS], *[delta[n] for n in WEIGHTS], *[new_m[n] for n in WEIGHTS],
            *[new_v[n] for n in WEIGHTS])
```

```python
import functools

import numpy as np
import jax
import jax.numpy as jnp
from jax import lax
from jax.experimental import pallas as pl
from jax.experimental.pallas import tpu as pltpu

f32 = jnp.float32
bf16 = jnp.bfloat16

D_MODEL = 1024
CHUNK = 64
RET_HEADS = 4
RET_KEY_DIM = 128
RET_VAL_DIM = 256
ATT_HEADS = 8
ATT_HEAD_DIM = 64
ATT_W = ATT_HEADS * ATT_HEAD_DIM
BAND_CHUNKS = 8
PAD = BAND_CHUNKS * CHUNK
MAX_REL = 256
N_REL = CHUNK + MAX_REL
D_FF = 2816
N_IN = 6656
ROPE_BASE = 10000.0
EPS = 1e-6
NEG_INF = -1e30
C_RQ, C_RK, C_RV, C_RG, C_AQ, C_AK, C_AV, C_GL = 0, 512, 1024, 2048, 3072, 3584, 4096, 4608

ADAM_LR, ADAM_B1, ADAM_B2, ADAM_EPS, ADAM_WD, ADAM_STEP = 0.001, 0.9, 0.999, 1e-08, 0.01, 10

N_CHIPS = 4
N_DEV = 8
WGRAD_ACC_BYTES = 8 * 1024 * 1024
ROW_TILE = 512
BIG_ROW_TILE = 1024
IN_ORDER = (0, 2, 3, 1)
QBLK = 256
KWIN = PAD + QBLK
TOEP = 1024
VMEM_LIMIT = 56 * 1024 * 1024
MESH = pl.DeviceIdType.MESH

BIG = (
    ("w_in", 1), ("w_ret_out", 0), ("w_att_out", 1), ("w_out", 0), ("w_ffn_gate", 1), ("w_ffn_up", 1), ("w_ffn_down", 0))
WEIGHTS = ("norm_mix", "w_in", "b_gate", "rel_bias", "w_ret_out", "w_att_out", "w_out", "norm_ffn", "w_ffn_gate",
           "w_ffn_up", "w_ffn_down", "norm_final")
SMALL = ("norm_mix", "b_gate", "rel_bias", "norm_ffn", "norm_final")


def _dot(a, b):
    return lax.dot_general(a, b, (((1,), (0,)), ((), ())), preferred_element_type=f32)


def _dot_nt(a, b):
    return lax.dot_general(a, b, (((1,), (1,)), ((), ())), preferred_element_type=f32)


def _dot_tn(a, b):
    return lax.dot_general(a, b, (((0,), (0,)), ((), ())), preferred_element_type=f32)


def _sig(x):
    return 1.0 / (1.0 + jnp.exp(-x))


def _tile(n, pref, mult):
    best = None
    for t in range(mult, min(n, pref) + 1, mult):
        if n % t == 0:
            best = t
    return best if best is not None else n


def _same_shape(arrays):
    groups = {}
    for i, a in enumerate(arrays):
        groups.setdefault(a.shape, []).append(i)
    return list(groups.values())


def _params(sem, vmem=VMEM_LIMIT):
    return pltpu.CompilerParams(dimension_semantics=sem, vmem_limit_bytes=vmem)


def _in_proj(place, x2, gamma, phase):
    T, D = x2.shape
    _, _, ns = phase.arrays[0].shape
    tm = _tile(T, BIG_ROW_TILE, 8)
    ni = T // tm
    pass_chip = lambda j: sum(jnp.where(j == n, f, 0) for n, f in enumerate(IN_ORDER))

    def body(p_ref, x_ref, g_ref, xn_ref, pr_ref, xs_ref, w_ref, w_sem, carried):
        j, i = pl.program_id(0), pl.program_id(1)
        pin, pout, sems = carried
        rows = pl.ds(pl.multiple_of(i * tm, tm), tm)

        @pl.when(i == 0)
        def _():
            for n, f in enumerate(IN_ORDER):
                if f:
                    @pl.when(j == n)
                    def _():
                        phase.arrived(f - 1, pin, pout, *sems)
                        phase.begin(2 + f, pin, pout, *sems)
                        phase.arrived(2 + f, pin, pout, *sems)
            shard = pltpu.make_async_copy(pout[0].at[jnp.bitwise_xor(p_ref[1], pass_chip(j))], w_ref, w_sem)
            shard.start()
            shard.wait()

        @pl.when(j == 0)
        def _():
            x = x_ref[...]
            r = lax.rsqrt(jnp.mean(x * x, axis=-1, keepdims=True) + EPS)
            xn = (x * r * g_ref[...]).astype(bf16)
            xs_ref[rows, :] = xn
            xn_ref[...] = xn

        pr_ref[...] = _dot(xs_ref[rows, :], w_ref[...]).astype(bf16)

    first_pass = lambda j, i, p: (jnp.where(j == 0, i, ni - 1), 0)
    return _call(
        body, phase, name="in_proj", grid=(N_CHIPS, ni), prefetch=(place,), expose=True,
        in_specs=[pl.BlockSpec((tm, D), first_pass), pl.BlockSpec((1, D), lambda j, i, p: (0, 0))],
        out_specs=[pl.BlockSpec((tm, D), first_pass),
                   pl.BlockSpec((tm, ns), lambda j, i, p: (i, jnp.bitwise_xor(p[1], pass_chip(j))))],
        out_shape=[jax.ShapeDtypeStruct((T, D), bf16), jax.ShapeDtypeStruct((T, N_CHIPS * ns), bf16)],
        scratch_shapes=[pltpu.VMEM((T, D), bf16), pltpu.VMEM((D, ns), bf16), pltpu.SemaphoreType.DMA],
        args=(x2, gamma))


def _rope_tables(S):
    d = RET_KEY_DIM
    freqs = ROPE_BASE ** (-jnp.arange(0, d, 2, dtype=f32) / d)
    ang = jnp.arange(S, dtype=f32)[:, None] * freqs[None, :]
    cos, sin = jnp.cos(ang), jnp.sin(ang)
    return jnp.concatenate([cos, cos], axis=1), jnp.concatenate([-sin, sin], axis=1)


def _decay_tables():
    H = RET_HEADS
    log_g = jnp.log(1.0 - 2.0 ** (-5.0 - jnp.arange(H, dtype=f32)))
    p = jnp.arange(CHUNK, dtype=f32)
    intra = jnp.exp(log_g[:, None, None] * jnp.abs(p[:, None] - p[None, :]))
    q_dec = jnp.exp(log_g[:, None] * (p[None, :] + 1.0))
    k_dec = jnp.exp(log_g[:, None] * (CHUNK - 1.0 - p[None, :]))
    c_dec = jnp.exp(log_g * CHUNK)
    q_dec = jnp.broadcast_to(q_dec[:, :, None], (H, CHUNK, RET_KEY_DIM))
    k_dec = jnp.broadcast_to(k_dec[:, :, None], (H, CHUNK, RET_KEY_DIM))
    c_dec = jnp.broadcast_to(c_dec[:, None, None], (H, 1, RET_VAL_DIM))
    return intra, q_dec, k_dec, c_dec


K_SCALE = RET_KEY_DIM ** -0.5


RET_CHUNKS = 4


def _ret_tables_specs():
    whole = lambda *shape: pl.BlockSpec(shape, lambda b, i: (0,) * len(shape))
    return [whole(RET_HEADS, CHUNK, CHUNK), whole(RET_HEADS, CHUNK, RET_KEY_DIM), whole(RET_HEADS, CHUNK, RET_KEY_DIM),
            whole(RET_HEADS, 1, RET_VAL_DIM)]


def _rotate(x, cos, sn):
    return x * cos + pltpu.roll(x, RET_KEY_DIM // 2, 1) * sn


def _ret_fwd(proj, B, S, rope, decay, phase=None):
    T = B * S
    nc = S // CHUNK
    H, dk, dv = RET_HEADS, RET_KEY_DIM, RET_VAL_DIM
    sb = RET_CHUNKS * CHUNK
    ns = S // sb

    def body(q_ref, k_ref, v_ref, g_ref, cos_ref, sin_ref, intra_ref, qd_ref, kd_ref, cd_ref,
             qr_ref, kr_ref, o_ref, u_ref, st_ref, state_ref):
        @pl.when(pl.program_id(1) == 0)
        def _():
            state_ref[...] = jnp.zeros_like(state_ref)

        cos, sn = cos_ref[...], sin_ref[...]
        for h in range(H):
            hs = slice(h * dk, (h + 1) * dk)
            qr_ref[:, hs] = _rotate(q_ref[:, hs].astype(f32), cos, sn).astype(bf16)
            kr_ref[:, hs] = (_rotate(k_ref[:, hs].astype(f32), cos, sn) * K_SCALE).astype(bf16)
        states = [state_ref[h] for h in range(H)]
        for ci in range(RET_CHUNKS):
            r = slice(ci * CHUNK, (ci + 1) * CHUNK)
            for h in range(H):
                hk, hv = slice(h * dk, (h + 1) * dk), slice(h * dv, (h + 1) * dv)
                qi, ki, vi = qr_ref[r, hk], kr_ref[r, hk], v_ref[r, hv]
                stb = states[h].astype(bf16)
                st_ref[0, h, ci] = stb
                s = (_dot_nt(qi, ki) * intra_ref[h]).astype(bf16)
                o = _dot(s, vi) + _dot((qi.astype(f32) * qd_ref[h]).astype(bf16), stb)
                states[h] = states[h] * cd_ref[h] + _dot_tn((ki.astype(f32) * kd_ref[h]).astype(bf16), vi)
                mu = jnp.mean(o, axis=-1, keepdims=True)
                xc = o - mu
                var = jnp.mean(xc * xc, axis=-1, keepdims=True)
                oh = xc * lax.rsqrt(var + EPS)
                g = g_ref[r, hv].astype(f32)
                o_ref[r, hv] = o.astype(bf16)
                u_ref[r, hv] = (g * _sig(g) * oh).astype(bf16)
        for h in range(H):
            state_ref[h] = states[h]

    blk = lambda w, c: pl.BlockSpec((sb, w), lambda b, i: (b * ns + i, c))
    return _call(
        body, phase, name="ret_fwd", grid=(B, ns), scratch_shapes=[pltpu.VMEM((H, dk, dv), f32)],
        in_specs=[blk(H * dk, C_RQ // (H * dk)), blk(H * dk, C_RK // (H * dk)), blk(H * dv, C_RV // (H * dv)),
                  blk(H * dv, C_RG // (H * dv)),
                  pl.BlockSpec((sb, dk), lambda b, i: (i, 0)), pl.BlockSpec((sb, dk), lambda b, i: (i, 0)),
                  *_ret_tables_specs()],
        out_specs=[blk(H * dk, 0), blk(H * dk, 0), blk(H * dv, 0), blk(H * dv, 0),
                   pl.BlockSpec((1, H, RET_CHUNKS, dk, dv), lambda b, i: (b, 0, i, 0, 0))],
        out_shape=[jax.ShapeDtypeStruct((T, H * dk), bf16), jax.ShapeDtypeStruct((T, H * dk), bf16),
                   jax.ShapeDtypeStruct((T, H * dv), bf16), jax.ShapeDtypeStruct((T, H * dv), bf16),
                   jax.ShapeDtypeStruct((B, H, nc, dk, dv), bf16)],
        args=(proj, proj, proj, proj, *rope, *decay))


def _bias_rows(rb):
    last = rb[:, N_REL - 1:]
    return jnp.concatenate([
        jnp.broadcast_to(last, (ATT_HEADS, PAD - MAX_REL + 1)),
        jnp.flip(rb[:, :N_REL - 1], axis=1),
        jnp.broadcast_to(rb[:, :1], (ATT_HEADS, KWIN - PAD - CHUNK)),
        jnp.broadcast_to(last, (ATT_HEADS, TOEP - KWIN)),
    ], axis=1)


def _build_bias(t_ref, bias_ref):
    row = lax.broadcasted_iota(jnp.int32, (QBLK, KWIN), 0) // CHUNK
    col = lax.broadcasted_iota(jnp.int32, (QBLK, KWIN), 1) // CHUNK
    delta = BAND_CHUNKS + row - col
    vis = (delta >= 0) & (delta <= BAND_CHUNKS)
    for h in range(ATT_HEADS):
        t = jnp.broadcast_to(t_ref[h:h + 1, :], (QBLK, TOEP))
        rolled = pltpu.roll(t, 0, 1, stride=1, stride_axis=0)
        bias_ref[h] = jnp.where(vis, rolled[:, :KWIN], NEG_INF)


ATT_SCALE = ATT_HEAD_DIM ** -0.5


def _att_probs(qh, kh, bias):
    s = _dot_nt(qh, kh) + bias
    m = jnp.max(s, axis=-1, keepdims=True)
    p = jnp.exp(s - m)
    return p * (1.0 / jnp.sum(p, axis=-1, keepdims=True))


def _first_of_pair():
    return lax.broadcasted_iota(jnp.int32, (1, 2 * ATT_HEAD_DIM), 1) < ATT_HEAD_DIM


def _by_window(i, step):
    sizes = list(range(QBLK, KWIN, QBLK))
    for n, nk in enumerate(sizes):
        pl.when(i == n)(functools.partial(step, nk))
    pl.when(i >= len(sizes))(functools.partial(step, KWIN))


def _att_fwd(proj, trows, B, S, phase=None):
    T = B * S
    nq = S // QBLK
    dh = ATT_HEAD_DIM

    def body(q_ref, k_ref, v_ref, t_ref, o_ref, bias_ref):
        i = pl.program_id(1)

        @pl.when((pl.program_id(0) == 0) & (i == 0))
        def _():
            _build_bias(t_ref, bias_ref)

        def step(nk):
            win = pl.ds(pl.multiple_of((i + 1) * QBLK - nk, QBLK), nk)
            kw, vw = k_ref[win, :], v_ref[win, :]
            first = _first_of_pair()
            outs = []
            for p in range(ATT_HEADS // 2):
                ps = slice(2 * p * dh, 2 * (p + 1) * dh)
                q2, k2, v2 = q_ref[:, ps] * ATT_SCALE, kw[:, ps], vw[:, ps]
                both = []
                for e in range(2):
                    qm = jnp.where(first == (e == 0), q2, jnp.zeros_like(q2))
                    pr = _att_probs(qm, k2, bias_ref[2 * p + e, :, KWIN - nk:])
                    both.append(_dot(pr.astype(bf16), v2))
                outs.append(jnp.where(first, both[0], both[1]))
            o_ref[...] = jnp.concatenate(outs, axis=1).astype(bf16)

        _by_window(i, step)

    return _call(
        body, phase, name="att_fwd", grid=(B, nq),
        in_specs=[pl.BlockSpec((QBLK, ATT_W), lambda b, i: (b * nq + i, C_AQ // ATT_W)),
                  pl.BlockSpec((S, ATT_W), lambda b, i: (b, C_AK // ATT_W)),
                  pl.BlockSpec((S, ATT_W), lambda b, i: (b, C_AV // ATT_W)),
                  pl.BlockSpec((ATT_HEADS, TOEP), lambda b, i: (0, 0))],
        out_specs=[pl.BlockSpec((QBLK, ATT_W), lambda b, i: (b * nq + i, 0))],
        out_shape=[jax.ShapeDtypeStruct((T, ATT_W), bf16)],
        scratch_shapes=[pltpu.VMEM((ATT_HEADS, QBLK, KWIN), f32)],
        args=(proj, proj, proj, trows))


def _gl_specs(tm):
    w = 512
    return [pl.BlockSpec((tm, w), functools.partial(lambda i, j: (i, C_GL // 512 + j), j=j)) for j in range(4)]


def _gates(gl_refs, bg_ref):
    gl = jnp.concatenate([r[...] for r in gl_refs], axis=1).astype(f32) + bg_ref[...]
    g = _sig(gl)
    return g[:, :D_MODEL], g[:, D_MODEL:]


def _mix_fwd(x2, proj, u, ao, b_gate, w_ro, w_ao, w_out, phase=None):
    T, D = x2.shape
    tm = _tile(T, ROW_TILE, 8)

    def body(x_ref, u_ref, ao_ref, g0, g1, g2, g3, bg_ref, wro_ref, wao_ref, wo_ref, h1_ref, yr_ref, ya_ref):
        yr = _dot(u_ref[...], wro_ref[...])
        ao = ao_ref[...]
        ya = jnp.concatenate([_dot(ao, wao_ref[k]) for k in range(N_CHIPS)], axis=1)
        gr, ga = _gates((g0, g1, g2, g3), bg_ref)
        mix = gr * yr + ga * ya
        h1_ref[...] = x_ref[...] + _dot(mix.astype(bf16), wo_ref[...])
        yr_ref[...] = yr.astype(bf16)
        ya_ref[...] = ya.astype(bf16)

    full = lambda a: pl.BlockSpec(a.shape, lambda i: (0,) * a.ndim)
    row = lambda n: pl.BlockSpec((tm, n), lambda i: (i, 0))
    return _call(
        body, phase, name="mix_fwd", grid=(T // tm,), scratch_shapes=[],
        in_specs=[row(D), row(D), row(ATT_W), *_gl_specs(tm), full(b_gate), full(w_ro), full(w_ao), full(w_out)],
        out_specs=[row(D), row(D), row(D)],
        out_shape=[jax.ShapeDtypeStruct((T, D), f32), jax.ShapeDtypeStruct((T, D), bf16),
                   jax.ShapeDtypeStruct((T, D), bf16)],
        args=(x2, u, ao, proj, proj, proj, proj, b_gate, w_ro, w_ao, w_out))


def _ffn_fwd(h1, g_ffn, wg, wu, wd, g_fin, target):
    T, D = h1.shape
    nf, tf, _ = wg.shape
    tm = _tile(T, ROW_TILE, 8)

    def body(h1_ref, g_ref, wg_ref, wu_ref, wd_ref, gf_ref, tg_ref, hn_ref, a_ref, b_ref, f_ref, dh2_ref, part_ref):
        h1v = h1_ref[...]
        r = lax.rsqrt(jnp.mean(h1v * h1v, axis=-1, keepdims=True) + EPS)
        hn = (h1v * r * g_ref[...]).astype(bf16)
        hn_ref[...] = hn
        h2 = h1v
        for k in range(nf):
            a = _dot_nt(hn, wg_ref[k])
            b = _dot_nt(hn, wu_ref[k])
            f = ((a * _sig(a)) * b).astype(bf16)
            a_ref[k] = a.astype(bf16)
            b_ref[k] = b.astype(bf16)
            f_ref[k] = f
            h2 = h2 + _dot(f, wd_ref[k])
        r = lax.rsqrt(jnp.mean(h2 * h2, axis=-1, keepdims=True) + EPS)
        n = h2 * r
        gf = gf_ref[...]
        e = n * gf - tg_ref[...]
        dy = e * (1.0 / D)
        dn = dy * gf
        dh2_ref[...] = r * (dn - n * jnp.mean(dn * n, axis=-1, keepdims=True))
        part_ref[...] = jnp.zeros_like(part_ref)
        part_ref[0:1, :] = jnp.sum(dy * n, axis=0, keepdims=True)
        part_ref[1:2, :] = (0.5 / D) * jnp.sum(e * e, axis=0, keepdims=True)

    row = lambda n: pl.BlockSpec((tm, n), lambda i: (i, 0))
    vec = pl.BlockSpec((1, D), lambda i: (0, 0))
    col = pl.BlockSpec((nf, tm, tf), lambda i: (0, i, 0))
    held = lambda w: pl.BlockSpec(w.shape, lambda i: (0, 0, 0), pipeline_mode=pl.Buffered(1))
    act = jax.ShapeDtypeStruct((nf, T, tf), bf16)
    return pl.pallas_call(
        body, name="ffn_fwd", grid=(T // tm,),
        in_specs=[row(D), vec, held(wg), held(wu), held(wd), vec, row(D)],
        out_specs=[row(D), col, col, col, row(D), pl.BlockSpec((8, D), lambda i: (i, 0))],
        out_shape=[jax.ShapeDtypeStruct((T, D), bf16), act, act, act,
                   jax.ShapeDtypeStruct((T, D), f32), jax.ShapeDtypeStruct((T // tm * 8, D), f32)],
        compiler_params=_params(("parallel",)),
    )(h1, g_ffn, wg, wu, wd, g_fin, target)


def _ffn_bwd(dh2, h1, g_ffn, a, b, wg, wu, wd):
    T, D = h1.shape
    nf, tf, _ = wg.shape
    tm = _tile(T, ROW_TILE // 2, 8)

    def body(dh2_ref, h1_ref, g_ref, a_ref, b_ref, wg_ref, wu_ref, wd_ref, da_ref, db_ref, dh1_ref, part_ref):
        dh2v = dh2_ref[...]
        dh2b = dh2v.astype(bf16)
        dhn = jnp.zeros((tm, D), f32)
        for k in range(nf):
            df = _dot_nt(dh2b, wd_ref[k])
            av = a_ref[k].astype(f32)
            sg = _sig(av)
            db = (df * (av * sg)).astype(bf16)
            da = (df * b_ref[k].astype(f32) * (sg * (1.0 + av * (1.0 - sg)))).astype(bf16)
            da_ref[k] = da
            db_ref[k] = db
            dhn = dhn + _dot(da, wg_ref[k]) + _dot(db, wu_ref[k])
        h = h1_ref[...]
        r = lax.rsqrt(jnp.mean(h * h, axis=-1, keepdims=True) + EPS)
        n = h * r
        dn = dhn * g_ref[...]
        dh1_ref[...] = dh2v + r * (dn - n * jnp.mean(dn * n, axis=-1, keepdims=True))
        part_ref[...] = jnp.zeros_like(part_ref)
        part_ref[0:1, :] = jnp.sum(dhn * n, axis=0, keepdims=True)

    row = lambda n: pl.BlockSpec((tm, n), lambda i: (i, 0))
    col = pl.BlockSpec((nf, tm, tf), lambda i: (0, i, 0))
    held = lambda w: pl.BlockSpec(w.shape, lambda i: (0, 0, 0), pipeline_mode=pl.Buffered(1))
    act = jax.ShapeDtypeStruct((nf, T, tf), bf16)
    return pl.pallas_call(
        body, name="ffn_bwd", grid=(T // tm,),
        in_specs=[row(D), row(D), pl.BlockSpec((1, D), lambda i: (0, 0)), col, col, held(wg), held(wu), held(wd)],
        out_specs=[col, col, row(D), pl.BlockSpec((8, D), lambda i: (i, 0))],
        out_shape=[act, act, jax.ShapeDtypeStruct((T, D), f32), jax.ShapeDtypeStruct((T // tm * 8, D), f32)],
        compiler_params=_params(("parallel",)),
    )(dh2, h1, g_ffn, a, b, wg, wu, wd)


def _mix_bwd(dh1, proj, yr, ya, b_gate, w_ro, w_ao, w_out, phase=None):
    T, D = dh1.shape
    tm = _tile(T, ROW_TILE, 8)

    def body(dh1_ref, g0, g1, g2, g3, bg_ref, yr_ref, ya_ref, wro_ref, wao_ref, wo_ref,
             du_ref, dao_ref, dgl_ref, mix_ref, dyr_ref, dya_ref, part_ref):
        dmix = _dot_nt(dh1_ref[...].astype(bf16), wo_ref[...])
        gr, ga = _gates((g0, g1, g2, g3), bg_ref)
        yr = yr_ref[...].astype(f32)
        ya = ya_ref[...].astype(f32)
        dyr = (dmix * gr).astype(bf16)
        dya = (dmix * ga).astype(bf16)
        dgl = jnp.concatenate([dmix * yr * gr * (1.0 - gr), dmix * ya * ga * (1.0 - ga)], axis=1)
        du_ref[...] = _dot_nt(dyr, wro_ref[...]).astype(bf16)
        ns = wao_ref.shape[2]
        dao = _dot_nt(dya[:, :ns], wao_ref[0])
        for k in range(1, N_CHIPS):
            dao = dao + _dot_nt(dya[:, k * ns:(k + 1) * ns], wao_ref[k])
        dao_ref[...] = dao.astype(bf16)
        dgl_ref[...] = dgl.astype(bf16)
        mix_ref[...] = (gr * yr + ga * ya).astype(bf16)
        dyr_ref[...] = dyr
        dya_ref[...] = dya
        part_ref[...] = jnp.zeros_like(part_ref)
        part_ref[0:1, :] = jnp.sum(dgl, axis=0, keepdims=True)

    full = lambda a: pl.BlockSpec(a.shape, lambda i: (0,) * a.ndim)
    row = lambda n: pl.BlockSpec((tm, n), lambda i: (i, 0))
    return _call(
        body, phase, name="mix_bwd", grid=(T // tm,), scratch_shapes=[],
        in_specs=[row(D), *_gl_specs(tm), full(b_gate), row(D), row(D), full(w_ro), full(w_ao), full(w_out)],
        out_specs=[row(D), row(ATT_W), row(2 * D), row(D), row(D), row(D), pl.BlockSpec((8, 2 * D), lambda i: (i, 0))],
        out_shape=[jax.ShapeDtypeStruct((T, D), bf16), jax.ShapeDtypeStruct((T, ATT_W), bf16),
                   jax.ShapeDtypeStruct((T, 2 * D), bf16), jax.ShapeDtypeStruct((T, D), bf16),
                   jax.ShapeDtypeStruct((T, D), bf16), jax.ShapeDtypeStruct((T, D), bf16),
                   jax.ShapeDtypeStruct((T // tm * 8, 2 * D), f32)],
        args=(dh1, proj, proj, proj, proj, b_gate, yr, ya, w_ro, w_ao, w_out))


def _ret_bwd(proj, qr, kr, o, states, du, B, S, rope, decay, phase=None):
    T = B * S
    nc = S // CHUNK
    H, dk, dv = RET_HEADS, RET_KEY_DIM, RET_VAL_DIM

    sb = RET_CHUNKS * CHUNK
    ns = S // sb

    def body(qr_ref, kr_ref, v_ref, g_ref, o_ref, st_ref, du_ref, cos_ref, sin_ref, intra_ref, qd_ref, kd_ref, cd_ref,
             dp_ref, dstate_ref):
        dq_ref, dk_ref = dp_ref.at[:, pl.ds(C_RQ, H * dk)], dp_ref.at[:, pl.ds(C_RK, H * dk)]
        dv_ref, dg_ref = dp_ref.at[:, pl.ds(C_RV, H * dv)], dp_ref.at[:, pl.ds(C_RG, H * dv)]

        @pl.when(pl.program_id(1) == 0)
        def _():
            dstate_ref[...] = jnp.zeros_like(dstate_ref)

        cos, snb = cos_ref[...], -sin_ref[...]
        dstates = [dstate_ref[h] for h in range(H)]
        for ci in reversed(range(RET_CHUNKS)):
            r = slice(ci * CHUNK, (ci + 1) * CHUNK)
            for h in range(H):
                hk, hv = slice(h * dk, (h + 1) * dk), slice(h * dv, (h + 1) * dv)
                intra, qd, kd = intra_ref[h], qd_ref[h], kd_ref[h]
                qi, ki, vi = qr_ref[r, hk], kr_ref[r, hk], v_ref[r, hv]
                si = st_ref[0, h, ci]
                o = o_ref[r, hv].astype(f32)
                mu = jnp.mean(o, axis=-1, keepdims=True)
                xc = o - mu
                rstd = lax.rsqrt(jnp.mean(xc * xc, axis=-1, keepdims=True) + EPS)
                oh = xc * rstd
                g = g_ref[r, hv].astype(f32)
                sg = _sig(g)
                dui = du_ref[r, hv].astype(f32)
                dg_ref[r, hv] = (dui * oh * (sg * (1.0 + g * (1.0 - sg)))).astype(bf16)
                doh = dui * (g * sg)
                do = rstd * (doh - jnp.mean(doh, axis=-1, keepdims=True)
                             - oh * jnp.mean(doh * oh, axis=-1, keepdims=True))
                dob = do.astype(bf16)
                p = (_dot_nt(qi, ki) * intra).astype(bf16)
                dsb = dstates[h].astype(bf16)
                kt = (ki.astype(f32) * kd).astype(bf16)
                qt = (qi.astype(f32) * qd).astype(bf16)
                dv_ref[r, hv] = (_dot_tn(p, dob) + _dot(kt, dsb)).astype(bf16)
                da = (_dot_nt(dob, vi) * intra).astype(bf16)
                dq = _dot(da, ki) + _dot_nt(dob, si) * qd
                dkk = (_dot_tn(da, qi) + _dot_nt(vi, dsb) * kd) * K_SCALE
                dq_ref[r, hk] = _rotate(dq, cos[r], snb[r]).astype(bf16)
                dk_ref[r, hk] = _rotate(dkk, cos[r], snb[r]).astype(bf16)
                dstates[h] = dstates[h] * cd_ref[h] + _dot_tn(qt, dob)
        for h in range(H):
            dstate_ref[h] = dstates[h]

    blk = lambda w, c: pl.BlockSpec((sb, w), lambda b, i: (b * ns + ns - 1 - i, c))
    return _call(
        body, phase, name="ret_bwd", grid=(B, ns),
        in_specs=[blk(H * dk, 0), blk(H * dk, 0), blk(H * dv, C_RV // (H * dv)), blk(H * dv, C_RG // (H * dv)),
                  blk(H * dv, 0),
                  pl.BlockSpec((1, H, RET_CHUNKS, dk, dv), lambda b, i: (b, 0, ns - 1 - i, 0, 0)),
                  blk(H * dv, 0),
                  pl.BlockSpec((sb, dk), lambda b, i: (ns - 1 - i, 0)), pl.BlockSpec((sb, dk), lambda b, i: (ns - 1 - i, 0)),
                  *_ret_tables_specs()],
        out_specs=[blk(C_AQ, 0)], out_shape=[jax.ShapeDtypeStruct((T, N_IN), bf16)],
        scratch_shapes=[pltpu.VMEM((H, dk, dv), f32)],
        args=(qr, kr, proj, proj, o, states, du, *rope, *decay))


def _att_bwd(proj, dao, trows, dproj, B, S, phase=None):
    T = B * S
    nq = S // QBLK
    dh = ATT_HEAD_DIM
    scale = ATT_HEAD_DIM ** -0.5

    def body(q_ref, k_ref, v_ref, do_ref, t_ref, _, dp_ref, vec_ref, bias_ref, dbias_ref, dka_ref, dva_ref):
        b, i = pl.program_id(0), pl.program_id(1)

        @pl.when((b == 0) & (i == 0))
        def _():
            _build_bias(t_ref, bias_ref)
            dbias_ref[...] = jnp.zeros_like(dbias_ref)

        @pl.when(i == 0)
        def _():
            dka_ref[...] = jnp.zeros_like(dka_ref)
            dva_ref[...] = jnp.zeros_like(dva_ref)

        def step(nk):
            win = pl.ds(pl.multiple_of((i + 1) * QBLK - nk, QBLK), nk)
            kw, vw = k_ref[win, :], v_ref[win, :]
            first = _first_of_pair()
            first_rows = lax.broadcasted_iota(jnp.int32, (2 * dh, 1), 0) < dh
            dqs, dks, dvs = [], [], []
            for p in range(ATT_HEADS // 2):
                ps = slice(2 * p * dh, 2 * (p + 1) * dh)
                q2, k2, v2, do2 = q_ref[:, ps] * ATT_SCALE, kw[:, ps], vw[:, ps], do_ref[:, ps]
                dq2, dk2, dv2 = [], [], []
                for e in range(2):
                    h = 2 * p + e
                    mine = first == (e == 0)
                    pr = _att_probs(jnp.where(mine, q2, jnp.zeros_like(q2)), k2, bias_ref[h, :, KWIN - nk:])
                    dp = _dot_nt(jnp.where(mine, do2, jnp.zeros_like(do2)), v2)
                    ds = pr * (dp - jnp.sum(pr * dp, axis=-1, keepdims=True))
                    dbias_ref[h, :, KWIN - nk:] += ds
                    dsb = ds.astype(bf16)
                    dq2.append(_dot(dsb, k2) * ATT_SCALE)
                    dk2.append(_dot_tn(q2, dsb))
                    dv2.append(_dot_tn(do2, pr.astype(bf16)))
                dqs.append(jnp.where(first, dq2[0], dq2[1]))
                dks.append(jnp.where(first_rows, dk2[0], dk2[1]))
                dvs.append(jnp.where(first_rows, dv2[0], dv2[1]))
            dp_ref[pl.ds(pl.multiple_of(i * QBLK, QBLK), QBLK), :ATT_W] = jnp.concatenate(dqs, axis=1).astype(bf16)
            dka_ref[:, win] += jnp.concatenate(dks, axis=0)
            dva_ref[:, win] += jnp.concatenate(dvs, axis=0)

        _by_window(i, step)

        @pl.when(i == nq - 1)
        def _():
            dp_ref[:, ATT_W:2 * ATT_W] = dka_ref[...].T.astype(bf16)
            dp_ref[:, 2 * ATT_W:] = dva_ref[...].T.astype(bf16)

        @pl.when((b == B - 1) & (i == nq - 1))
        def _():
            rr = lax.broadcasted_iota(jnp.int32, (QBLK, QBLK), 0)
            cc = lax.broadcasted_iota(jnp.int32, (QBLK, QBLK), 1)
            flip = jnp.where(rr + cc == QBLK - 1, 1.0, 0.0).astype(bf16)
            for h in range(ATT_HEADS):
                d = dbias_ref[h]
                hi = d.astype(bf16)
                lo = (d - hi.astype(f32)).astype(bf16)
                rev = _dot(flip, hi) + _dot(flip, lo)
                wide = jnp.concatenate([rev, jnp.zeros((QBLK, TOEP - KWIN), f32)], axis=1)
                rolled = pltpu.roll(wide, 0, 1, stride=1, stride_axis=0)
                vec_ref[h:h + 1, :] = jnp.sum(rolled, axis=0, keepdims=True)

    qspec = lambda c: pl.BlockSpec((QBLK, ATT_W), lambda b, i: (b * nq + i, c))
    kspec = lambda c: pl.BlockSpec((S, ATT_W), lambda b, i: (b, c))
    return _call(
        body, phase, name="att_bwd", grid=(B, nq), aliases={5: 0},
        in_specs=[qspec(C_AQ // ATT_W), kspec(C_AK // ATT_W), kspec(C_AV // ATT_W), qspec(0),
                  pl.BlockSpec((ATT_HEADS, TOEP), lambda b, i: (0, 0)), pl.BlockSpec(memory_space=pl.ANY)],
        out_specs=[pl.BlockSpec((S, 3 * ATT_W), lambda b, i: (b, C_AQ // (3 * ATT_W))),
                   pl.BlockSpec((ATT_HEADS, TOEP), lambda b, i: (0, 0))],
        out_shape=[jax.ShapeDtypeStruct((T, N_IN), bf16), jax.ShapeDtypeStruct((ATT_HEADS, TOEP), f32)],
        scratch_shapes=[pltpu.VMEM((ATT_HEADS, QBLK, KWIN), f32), pltpu.VMEM((ATT_HEADS, QBLK, KWIN), f32),
                        pltpu.VMEM((ATT_W, S), f32), pltpu.VMEM((ATT_W, S), f32)],
        args=(proj, proj, proj, dao, trows, dproj))


def _in_proj_bwd(dproj, w_in, x2, gamma, dh1, phase=None):
    T, D = x2.shape
    nk, _, tk = w_in.shape
    tm = _tile(T, BIG_ROW_TILE, 8)

    def body(dp_ref, w_ref, x_ref, g_ref, dh1_ref, dx_ref, part_ref, acc_ref):
        j = pl.program_id(1)

        @pl.when(j == 0)
        def _():
            acc_ref[...] = jnp.zeros_like(acc_ref)

        acc_ref[...] += _dot_nt(dp_ref[...], w_ref[0])

        @pl.when(j == nk - 1)
        def _():
            x = x_ref[...]
            r = lax.rsqrt(jnp.mean(x * x, axis=-1, keepdims=True) + EPS)
            n = x * r
            dxn = acc_ref[...]
            dn = dxn * g_ref[...]
            dx_ref[...] = dh1_ref[...] + r * (dn - n * jnp.mean(dn * n, axis=-1, keepdims=True))
            part_ref[...] = jnp.zeros_like(part_ref)
            part_ref[0:1, :] = jnp.sum(dxn * n, axis=0, keepdims=True)

    row = lambda n: pl.BlockSpec((tm, n), lambda i, j: (i, 0))
    return _call(
        body, phase, name="in_proj_bwd", grid=(T // tm, nk),
        in_specs=[pl.BlockSpec((tm, tk), lambda i, j: (i, j)), pl.BlockSpec((1, D, tk), lambda i, j: (j, 0, 0)), row(D),
                  pl.BlockSpec((1, D), lambda i, j: (0, 0)), row(D)],
        out_specs=[row(D), pl.BlockSpec((8, D), lambda i, j: (i, 0))],
        out_shape=[jax.ShapeDtypeStruct((T, D), f32), jax.ShapeDtypeStruct((T // tm * 8, D), f32)],
        scratch_shapes=[pltpu.VMEM((tm, D), f32)],
        args=(dproj, w_in, x2, gamma, dh1))


def _wgrad(a, b, shard_axis, name, phase=None):
    def spec(arr, sharded, tt):
        if arr.ndim == 3:
            return arr.shape[2], pl.BlockSpec((1, tt, arr.shape[2]), lambda s, t: (s, t, 0))
        if sharded:
            w = arr.shape[1] // N_CHIPS
            return w, pl.BlockSpec((tt, w), lambda s, t: (t, s))
        return arr.shape[1], pl.BlockSpec((tt, arr.shape[1]), lambda s, t: (t, 0))

    T = a.shape[-2]
    tt = _tile(T, BIG_ROW_TILE, 16)
    nt = T // tt
    whole = a.ndim == 2 and b.ndim == 2 and a.shape[1] * b.shape[1] * 4 <= WGRAD_ACC_BYTES
    if whole:
        K, N = a.shape[1], b.shape[1]
        a_spec, b_spec = pl.BlockSpec((tt, K), lambda s, t: (t, 0)), pl.BlockSpec((tt, N), lambda s, t: (t, 0))
        out_block = (N_CHIPS, K // N_CHIPS, N) if shard_axis == 0 else (N_CHIPS, K, N // N_CHIPS)
        out_spec = pl.BlockSpec(out_block, lambda s, t: (0, 0, 0))
    else:
        K, a_spec = spec(a, shard_axis == 0, tt)
        N, b_spec = spec(b, shard_axis == 1, tt)
        out_block = (N_CHIPS, K, N)
        out_spec = pl.BlockSpec((1, K, N), lambda s, t: (s, 0, 0))

    def body(a_ref, b_ref, o_ref, acc_ref):
        t = pl.program_id(1)

        @pl.when(t == 0)
        def _():
            acc_ref[...] = jnp.zeros_like(acc_ref)

        av = a_ref[0] if a.ndim == 3 else a_ref[...]
        bv = b_ref[0] if b.ndim == 3 else b_ref[...]
        acc_ref[...] += _dot_tn(av.astype(bf16), bv.astype(bf16))

        @pl.when(t == nt - 1)
        def _():
            if not whole:
                o_ref[0] = acc_ref[...].astype(bf16)
            else:
                _, kk, nn = out_block
                for s in range(N_CHIPS):
                    o_ref[s] = (acc_ref[s * kk:(s + 1) * kk, :] if shard_axis == 0
                                else acc_ref[:, s * nn:(s + 1) * nn]).astype(bf16)

    (grad,), carried = _call(
        body, phase, name=name, grid=(1 if whole else N_CHIPS, nt), in_specs=[a_spec, b_spec], out_specs=[out_spec],
        out_shape=[jax.ShapeDtypeStruct(out_block, bf16)], scratch_shapes=[pltpu.VMEM((K, N), f32)], args=(a, b))
    return grad, carried


def _adamw_sum(place, groups, name):
    n = len(groups)
    R, C = groups[0][0].shape
    half = R // 2
    tr = _tile(half, max(16, (1 << 18) // C // 16 * 16), 16)
    nr = half // tr

    def body(p_ref, *refs):
        for a in range(n):
            w_ref, m_ref, v_ref, part_ref, fc_ref, fs_ref = refs[6 * a:6 * a + 6]
            g_ref, d_ref, mo_ref, vo_ref = refs[6 * n + 4 * a:6 * n + 4 * a + 4]
            up = lambda x: x.astype(f32)
            mine = ((up(part_ref[0]) + up(fc_ref[0])) + up(fc_ref[1])) + up(fc_ref[2])
            sibs = ((up(fs_ref[0]) + up(fs_ref[1])) + up(fs_ref[2])) + up(fs_ref[3])
            g_ = jnp.where(pl.program_id(0) == p_ref[0], mine, sibs)
            m_ = ADAM_B1 * m_ref[...] + (1.0 - ADAM_B1) * g_
            v_ = ADAM_B2 * v_ref[...] + (1.0 - ADAM_B2) * (g_ * g_)
            m_hat = m_ / (1.0 - ADAM_B1 ** ADAM_STEP)
            v_hat = v_ / (1.0 - ADAM_B2 ** ADAM_STEP)
            g_ref[...] = g_
            d_ref[...] = -ADAM_LR * (m_hat / (jnp.sqrt(v_hat) + ADAM_EPS) + ADAM_WD * w_ref[...])
            mo_ref[...] = m_
            vo_ref[...] = v_

    spec = pl.BlockSpec((tr, C), lambda h, r, p: (h * nr + r, 0))
    one = [spec, spec, spec, pl.BlockSpec((1, tr, C), lambda h, r, p: (p[1], jnp.where(h == p[0], r, 0), 0)),
           pl.BlockSpec((3, tr, C), lambda h, r, p: (0, jnp.where(h == p[0], r, 0), 0)),
           pl.BlockSpec((4, tr, C), lambda h, r, p: (0, jnp.where(h == p[0], 0, r), 0))]
    res = pl.pallas_call(
        body, name=name,
        grid_spec=pltpu.PrefetchScalarGridSpec(num_scalar_prefetch=1, grid=(2, nr), in_specs=one * n,
                                               out_specs=[spec] * (4 * n)),
        out_shape=[jax.ShapeDtypeStruct((R, C), f32)] * (4 * n),
        compiler_params=_params(("parallel", "parallel")),
    )(place, *[x for g in groups for x in g])
    return [tuple(res[4 * a:4 * a + 4]) for a in range(n)]


def _adamw(w, g, m, v, name):
    R, C = w.shape
    tr = _tile(R, max(8, (1 << 18) // C // 8 * 8), 8)

    def body(w_ref, g_ref, m_ref, v_ref, d_ref, mo_ref, vo_ref):
        g_ = g_ref[...]
        m_ = ADAM_B1 * m_ref[...] + (1.0 - ADAM_B1) * g_
        v_ = ADAM_B2 * v_ref[...] + (1.0 - ADAM_B2) * (g_ * g_)
        m_hat = m_ / (1.0 - ADAM_B1 ** ADAM_STEP)
        v_hat = v_ / (1.0 - ADAM_B2 ** ADAM_STEP)
        d_ref[...] = -ADAM_LR * (m_hat / (jnp.sqrt(v_hat) + ADAM_EPS) + ADAM_WD * w_ref[...])
        mo_ref[...] = m_
        vo_ref[...] = v_

    spec = pl.BlockSpec((tr, C), lambda i: (i, 0))
    return pl.pallas_call(
        body, name=name, grid=(R // tr,), in_specs=[spec] * 4, out_specs=[spec] * 3,
        out_shape=[jax.ShapeDtypeStruct((R, C), f32)] * 3,
        compiler_params=_params(("parallel",)),
    )(w, g, m, v)


def _place():
    return lax.axis_index("x"), lax.axis_index("y"), lax.axis_index("c")


def _other_chips(x, y):
    chips = [(1 - x, y), (x, 1 - y), (1 - x, 1 - y)]
    return chips, [2 * cx + cy for cx, cy in chips]


def _spread_phase(blk):
    def peers():
        x, y, c = _place()
        return [tuple(1 - p if (k >> s) & 1 else p for p, s in ((x, 2), (y, 1), (c, 0))) for k in range(1, N_DEV)]

    def copies(pin, out):
        x, y, c = _place()
        mine = out[0].at[4 * x + 2 * y + c]
        return [(mine, mine, peer) for peer in peers()]

    stack = jnp.broadcast_to(blk, (N_DEV,) + blk.shape)
    return _Phase([stack], [jax.ShapeDtypeStruct(stack.shape, stack.dtype)], {0: 0}, N_DEV - 1, copies,
                  lambda pin, out: [out[0].at[4 * px + 2 * py + pc] for px, py, pc in peers()])


def _sum_slots(stack, name):
    def body(s_ref, o_ref):
        tot = s_ref[0]
        for d in range(1, stack.shape[0]):
            tot = tot + s_ref[d]
        o_ref[...] = tot

    vm = pl.BlockSpec(memory_space=pltpu.VMEM)
    return pl.pallas_call(body, name=name, in_specs=[vm], out_specs=vm,
                          out_shape=jax.ShapeDtypeStruct(stack.shape[1:], stack.dtype))(stack)


def _cast_shards(place, ws, name):
    n = len(ws)
    R, C = ws[0].shape
    tr = _tile(R, max(16, (1 << 19) // C // 16 * 16), 16)

    def body(p_ref, *refs):
        for a in range(n):
            refs[n + a][0] = refs[a][...].astype(bf16)

    return pl.pallas_call(
        body, name=name,
        grid_spec=pltpu.PrefetchScalarGridSpec(
            num_scalar_prefetch=1, grid=(R // tr,),
            in_specs=[pl.BlockSpec((tr, C), lambda r, p: (r, 0))] * n,
            out_specs=[pl.BlockSpec((1, tr, C), lambda r, p: (p[1], r, 0))] * n),
        out_shape=[jax.ShapeDtypeStruct((N_CHIPS, R, C), bf16)] * n,
        compiler_params=_params(("parallel",)),
    )(place, *ws)


class _Phase:
    def __init__(self, arrays, out_shapes, aliases, n_copies, copies, arrivals, own_starts=(), own_waits=()):
        self.arrays, self.out_shapes, self.aliases = list(arrays), list(out_shapes), dict(aliases)
        self.n_copies, self.copies, self.arrivals = n_copies, copies, arrivals
        self.own_starts, self.own_waits = tuple(own_starts), tuple(own_waits)

    def sems(self):
        return [pltpu.SemaphoreType.DMA((self.n_copies,)), pltpu.SemaphoreType.DMA((self.n_copies,))]

    def _descriptors(self, pin, pout, send_sems, recv_sems):
        return [pltpu.make_async_remote_copy(src_ref=s, dst_ref=d, send_sem=send_sems.at[i], recv_sem=recv_sems.at[i],
                                             device_id=to, device_id_type=MESH)
                for i, (s, d, to) in enumerate(self.copies(pin, pout))]

    def _arrival(self, i, pin, pout, send_sems, recv_sems):
        dst = self.arrivals(pin, pout)[i]
        return pltpu.make_async_remote_copy(src_ref=dst, dst_ref=dst, send_sem=send_sems.at[i], recv_sem=recv_sems.at[i],
                                            device_id=_place(), device_id_type=MESH)

    def start(self, pin, pout, send_sems, recv_sems):
        for i, cp in enumerate(self._descriptors(pin, pout, send_sems, recv_sems)):
            if i not in self.own_starts:
                cp.start()

    def begin(self, i, pin, pout, send_sems, recv_sems):
        self._descriptors(pin, pout, send_sems, recv_sems)[i].start()

    def arrived(self, i, pin, pout, send_sems, recv_sems):
        self._arrival(i, pin, pout, send_sems, recv_sems).wait_recv()

    def finish(self, pin, pout, send_sems, recv_sems):
        for i in range(self.n_copies):
            if i not in self.own_waits:
                self._arrival(i, pin, pout, send_sems, recv_sems).wait_recv()
        for cp in self._descriptors(pin, pout, send_sems, recv_sems):
            cp.wait_send()


def _join(phases):
    if len(phases) == 1:
        return phases[0]
    ai = np.cumsum([0] + [len(p.arrays) for p in phases])
    oi = np.cumsum([0] + [len(p.out_shapes) for p in phases])

    def each(fn_name, pin, pout):
        return [item for k, p in enumerate(phases)
                for item in getattr(p, fn_name)(pin[ai[k]:ai[k + 1]], pout[oi[k]:oi[k + 1]])]

    aliases = {int(ai[k]) + i: int(oi[k]) + j for k, p in enumerate(phases) for i, j in p.aliases.items()}
    ci = np.cumsum([0] + [p.n_copies for p in phases])
    shifted = lambda attr: [int(ci[k]) + i for k, p in enumerate(phases) for i in getattr(p, attr)]
    return _Phase([a for p in phases for a in p.arrays], [s for p in phases for s in p.out_shapes], aliases,
                  int(ci[-1]), functools.partial(each, "copies"), functools.partial(each, "arrivals"),
                  shifted("own_starts"), shifted("own_waits"))


def _call(body, phase, *, name, grid, in_specs, out_specs, out_shape, scratch_shapes, args, prefetch=(), expose=False,
          aliases=None):
    seq = _params(("arbitrary",) * len(grid))
    np_ = len(prefetch)
    own = {np_ + i: j for i, j in (aliases or {}).items()}
    if phase is None:
        spec = pltpu.PrefetchScalarGridSpec(num_scalar_prefetch=np_, grid=grid, in_specs=in_specs, out_specs=out_specs,
                                            scratch_shapes=scratch_shapes)
        res = pl.pallas_call(body, name=name, grid_spec=spec, out_shape=out_shape, input_output_aliases=own,
                             compiler_params=seq)(*prefetch, *args)
        return list(res), []
    ni, no, ns = len(in_specs), len(out_specs), len(scratch_shapes)
    pi, po = len(phase.arrays), len(phase.out_shapes)

    def hosted(*refs):
        cut = np.cumsum([np_, ni, pi, no, po, ns])
        pre, ins, pin, outs, pout, scr, sems = (refs[a:b] for a, b in zip([0, *cut], [*cut, len(refs)]))
        ids = [pl.program_id(d) for d in range(len(grid))]
        first = functools.reduce(lambda p, q: p & q, [i == 0 for i in ids])
        last = functools.reduce(lambda p, q: p & q, [i == g - 1 for i, g in zip(ids, grid)])
        pl.when(first)(lambda: phase.start(pin, pout, *sems))
        body(*pre, *ins, *outs, *scr, **({"carried": (pin, pout, sems)} if expose else {}))
        pl.when(last)(lambda: phase.finish(pin, pout, *sems))

    anyspace = pl.BlockSpec(memory_space=pl.ANY)
    spec = pltpu.PrefetchScalarGridSpec(
        num_scalar_prefetch=np_, grid=grid, in_specs=list(in_specs) + [anyspace] * pi,
        out_specs=list(out_specs) + [anyspace] * po, scratch_shapes=list(scratch_shapes) + phase.sems())
    res = pl.pallas_call(
        hosted, name=name, grid_spec=spec, out_shape=list(out_shape) + phase.out_shapes,
        input_output_aliases={**own, **{np_ + ni + i: no + j for i, j in phase.aliases.items()}}, compiler_params=seq,
    )(*prefetch, *args, *phase.arrays)
    return list(res[:no]), list(res[no:])


def _run_phases(name, phases):
    first = phases[0]
    pi, po = len(first.arrays), len(first.out_shapes)

    def body(*refs):
        pin, pout, sems = refs[:pi], refs[pi:pi + po], refs[pi + po:]
        for n, ph in enumerate(phases):
            ph.start(pin, pout, *sems[2 * n:2 * n + 2])
            ph.finish(pin, pout, *sems[2 * n:2 * n + 2])

    anyspace = pl.BlockSpec(memory_space=pl.ANY)
    return list(pl.pallas_call(
        body, name=name, in_specs=[anyspace] * pi, out_specs=[anyspace] * po, out_shape=first.out_shapes,
        input_output_aliases=first.aliases, scratch_shapes=[s for ph in phases for s in ph.sems()],
    )(*first.arrays))


def _half_rows(buf, c):
    half = buf.shape[1] // 2
    return pl.ds(c * half, half), pl.ds((1 - c) * half, half)


def _gather_phase(bufs, over_ici):
    n = len(bufs)
    shapes = [jax.ShapeDtypeStruct(b.shape, b.dtype) for b in bufs]

    def landed(out, which):
        x, y, c = _place()
        _, ks = _other_chips(x, y)
        return [out[a].at[ks[j], _half_rows(bufs[a], c)[which]] for a in range(n) for j in range(3)]

    def ici(pin, out):
        x, y, c = _place()
        chips, _ = _other_chips(x, y)
        mine = [out[a].at[2 * x + y, _half_rows(bufs[a], c)[0]] for a in range(n)]
        return [(mine[a], mine[a], (*chips[j], c)) for a in range(n) for j in range(3)]

    def d2d(pin, out):
        x, y, c = _place()
        return [(dst, dst, (x, y, 1 - c)) for dst in landed(out, 0)]

    if over_ici:
        return _Phase(bufs, shapes, {a: a for a in range(n)}, 3 * n, ici, lambda pin, out: landed(out, 0))
    return _Phase(bufs, shapes, {a: a for a in range(n)}, 3 * n, d2d, lambda pin, out: landed(out, 1))


def _feed_phase(buf):
    def chips():
        x, y, c = _place()
        return [(x if f < 2 else 1 - x, y if f % 2 == 0 else 1 - y) for f in (1, 2, 3)]

    def copies(pin, out):
        x, y, c = _place()
        mine = _half_rows(buf, c)[0]
        own = out[0].at[2 * x + y, mine]
        sent = [(own, own, (cx, cy, c)) for cx, cy in chips()]
        return sent + [(out[0].at[2 * cx + cy, mine], out[0].at[2 * cx + cy, mine], (x, y, 1 - c)) for cx, cy in chips()]

    def arrivals(pin, out):
        x, y, c = _place()
        mine, theirs = _half_rows(buf, c)
        return [out[0].at[2 * cx + cy, rows] for rows in (mine, theirs) for cx, cy in chips()]

    return _Phase([buf], [jax.ShapeDtypeStruct(buf.shape, buf.dtype)], {0: 0}, 6, copies, arrivals,
                  own_starts=(3, 4, 5), own_waits=range(6))


def _rs_swap_phase(grads):
    n = len(grads)

    def copies(g, out):
        x, y, c = _place()
        return [(g[a].at[:, _half_rows(grads[a], c)[1]], out[a], (x, y, 1 - c)) for a in range(n)]

    shapes = [jax.ShapeDtypeStruct((N_CHIPS, g.shape[1] // 2, g.shape[2]), g.dtype) for g in grads]
    return _Phase(grads, shapes, {}, n, copies, lambda g, out: list(out))


def _rs_add_sibling(place, grads, gots, name):
    n = len(grads)
    _, R, C = grads[0].shape
    half = R // 2
    tr = _tile(half, max(16, (1 << 19) // C // 16 * 16), 16)
    nr = half // tr

    def body(p_ref, *refs):
        for a in range(n):
            refs[2 * n + a][...] = (refs[2 * a][...].astype(f32) + refs[2 * a + 1][...].astype(f32)).astype(bf16)

    res = pl.pallas_call(
        body, name=name,
        grid_spec=pltpu.PrefetchScalarGridSpec(
            num_scalar_prefetch=1, grid=(N_CHIPS, nr),
            in_specs=[pl.BlockSpec((1, tr, C), lambda k, r, p: (k, p[0] * nr + r, 0)),
                      pl.BlockSpec((1, tr, C), lambda k, r, p: (k, r, 0))] * n,
            out_specs=[pl.BlockSpec((1, tr, C), lambda k, r, p: (k, r, 0))] * n),
        out_shape=[jax.ShapeDtypeStruct((N_CHIPS, half, C), bf16)] * n,
        compiler_params=_params(("parallel", "parallel")),
    )(place, *[x for pair in zip(grads, gots) for x in pair])
    return list(res)


def _rs_chips_phase(parts):
    n = len(parts)

    def copies(p, fc):
        x, y, c = _place()
        chips, ks = _other_chips(x, y)
        return [(p[a].at[ks[j]], fc[a].at[j], (*chips[j], c)) for a in range(n) for j in range(3)]

    shapes = [jax.ShapeDtypeStruct((3,) + q.shape[1:], q.dtype) for q in parts]
    return _Phase(parts, shapes, {}, 3 * n, copies, lambda p, fc: [fc[a].at[j] for a in range(n) for j in range(3)])


def _rs_hand_phase(parts, from_chips):
    n = len(parts)

    def copies(pin, fs):
        x, y, c = _place()
        sib = (x, y, 1 - c)
        own = [(pin[a].at[2 * x + y], fs[a].at[0], sib) for a in range(n)]
        return own + [(pin[n + a].at[j], fs[a].at[1 + j], sib) for a in range(n) for j in range(3)]

    def arrivals(pin, fs):
        return [fs[a].at[0] for a in range(n)] + [fs[a].at[1 + j] for a in range(n) for j in range(3)]

    shapes = [jax.ShapeDtypeStruct((4,) + q.shape[1:], q.dtype) for q in parts]
    return _Phase(list(parts) + list(from_chips), shapes, {}, 4 * n, copies, arrivals)


class _Exchange:
    def __init__(self, place):
        self.place = place

    def feed(self, buf):
        return _feed_phase(buf)

    def gather(self, bufs, over_ici):
        return _gather_phase(bufs, over_ici)

    def swap(self, grads):
        return _rs_swap_phase(grads)

    def pair_sums(self, names, grads):
        return self.add(names, grads, _run_phases("rs_sibling_" + names[0], [_rs_swap_phase(grads)]))

    def add(self, names, grads, got):
        parts = {}
        for group in _same_shape(grads):
            res = _rs_add_sibling(self.place, [grads[i] for i in group], [got[i] for i in group],
                                  "rs_add_" + names[group[0]])
            parts.update(zip(group, res))
        return [parts[i] for i in range(len(names))]

    def to_chips(self, parts):
        return _rs_chips_phase(parts)

    def to_sibling(self, parts, from_chips):
        return _rs_hand_phase(parts, from_chips)

    def spread(self, blk):
        return _spread_phase(blk)

    def hand_over(self, name, parts, from_chips, blk):
        got = _run_phases(name, [_join([_rs_hand_phase(parts, from_chips), _spread_phase(blk)])])
        return got[:-1], got[-1]


def _local_step(place, x, target, norm_mix, b_gate, rb_chip, norm_ffn, norm_final, w_in, rest, exch):
    B, S, D = x.shape
    T = B * S
    x2 = x.reshape(T, D)
    tg2 = target.reshape(T, D)
    rope, decay = _rope_tables(S), _decay_tables()
    g_fin = norm_final.reshape(1, D)
    nrel = rb_chip.shape[-1]

    mrg, ffn = ["w_ret_out", "w_att_out", "w_out"], ["w_ffn_gate", "w_ffn_up", "w_ffn_down"]
    (xn, proj), got = _in_proj(place, x2, norm_mix, _join([exch.feed(w_in), exch.gather([rest[n] for n in mrg], True),
                                                           exch.spread(jnp.pad(rb_chip, ((0, 0), (0, 128 - nrel))))]))
    w_in, wb, rb_all = got[0], {}, got.pop()
    trows = _bias_rows(jnp.concatenate([rb_all[2 * k, :, :nrel] for k in range(N_CHIPS)], axis=1))
    (qr, kr, o, u, states), got = _ret_fwd(proj, B, S, rope, decay, _join([exch.gather([rest["w_ffn_gate"]], True),
                                                                         exch.gather(got[1:], False)]))
    wb.update(zip(mrg, got[1:]))
    (ao,), got = _att_fwd(proj, trows, B, S, _join([exch.gather([rest["w_ffn_up"], rest["w_ffn_down"]], True),
                                                    exch.gather(got[:1], False)]))
    wb["w_ffn_gate"] = got[2]
    w_ro, w_out = wb["w_ret_out"].reshape(-1, D), wb["w_out"].reshape(-1, D)
    (h1, yr, ya), got = _mix_fwd(x2, proj, u, ao, b_gate, w_ro, wb["w_att_out"], w_out, exch.gather(got[:2], False))
    wb.update(zip(ffn[1:], got))
    hn, a, b, f, dh2, part_fin = _ffn_fwd(h1, norm_ffn, wb["w_ffn_gate"], wb["w_ffn_up"], wb["w_ffn_down"], g_fin, tg2)

    da, db, dh1, part_ffn = _ffn_bwd(dh2, h1, norm_ffn, a, b, wb["w_ffn_gate"], wb["w_ffn_up"], wb["w_ffn_down"])
    ffn = ["w_ffn_down", "w_ffn_gate", "w_ffn_up"]
    g_ffn = [_wgrad(f, dh2, 0, "wgrad_ffn_down")[0], _wgrad(da, hn, 0, "wgrad_ffn_gate")[0],
             _wgrad(db, hn, 0, "wgrad_ffn_up")[0]]
    (du, dao, dgl, mix, dyr, dya, part_bg), x_ffn = _mix_bwd(dh1, proj, yr, ya, b_gate, w_ro, wb["w_att_out"], w_out,
                                                             exch.swap(g_ffn))
    p_ffn = exch.add(ffn, g_ffn, x_ffn)
    mrg = ["w_out", "w_ret_out", "w_att_out"]
    g_mrg = [_wgrad(mix, dh1, 0, "wgrad_out")[0], _wgrad(u, dyr, 0, "wgrad_ret_out")[0],
             _wgrad(ao, dya, 1, "wgrad_att_out")[0]]
    (dproj,), got = _ret_bwd(proj, qr, kr, o, states, du, B, S, rope, decay, _join([exch.to_chips(p_ffn[:2]),
                                                                                     exch.swap(g_mrg)]))
    c_two, p_mrg = got[:2], exch.add(mrg, g_mrg, got[2:])
    (dproj, dvec), got = _att_bwd(proj, dao, trows, dproj, B, S, exch.to_chips(p_ffn[2:] + p_mrg))
    c_ffn, c_mrg = c_two + got[:1], got[1:]
    dproj = lax.dynamic_update_slice(dproj, dgl, (0, C_GL))
    g_in, got = _wgrad(xn, dproj, 1, "wgrad_in", exch.to_sibling(p_ffn + p_mrg, c_ffn + c_mrg))
    s_ffn, s_mrg = got[:len(ffn)], got[len(ffn):]
    p_in = exch.pair_sums(["w_in"], [g_in])
    (gx, part_mix), c_in = _in_proj_bwd(dproj, w_in, x2, norm_mix, dh1, exch.to_chips(p_in))
    rows = lambda p, r: p.reshape(-1, 8, p.shape[-1])[:, r, :].sum(axis=0)
    lo = KWIN - 1 - (MAX_REL - 1)
    drb = jnp.concatenate([jnp.flip(dvec[:, lo:lo + N_REL - 1], axis=1), dvec[:, :lo].sum(axis=1, keepdims=True)], axis=1)
    gsmall = {
        "norm_mix": rows(part_mix, 0), "b_gate": rows(part_bg, 0), "rel_bias": drb, "norm_ffn": rows(part_ffn, 0),
        "norm_final": rows(part_fin, 0),
    }
    s_in, small_all = exch.hand_over("rs_hand_w_in", p_in, c_in, _pack_small(gsmall, rows(part_fin, 1)))
    gbig = dict(zip(ffn + mrg + ["w_in"], zip(p_ffn + p_mrg + p_in, c_ffn + c_mrg + c_in, s_ffn + s_mrg + s_in)))
    return gx.reshape(B, S, D), gbig, small_all


SMALL_ROWS = 16


def _pack_small(gs, loss_lanes):
    D = D_MODEL
    rb = jnp.pad(gs["rel_bias"].reshape(-1), (0, 3 * D - ATT_HEADS * N_REL)).reshape(3, D)
    rows = [gs["norm_mix"].reshape(1, D), gs["b_gate"].reshape(2, D), gs["norm_ffn"].reshape(1, D),
            gs["norm_final"].reshape(1, D), rb, loss_lanes.reshape(1, D)]
    used = sum(r.shape[0] for r in rows)
    return jnp.concatenate(rows + [jnp.zeros((SMALL_ROWS - used, D), f32)], axis=0)


def kernel(x, norm_mix, w_in, b_gate, rel_bias, w_ret_out, w_att_out, w_out, norm_ffn, w_ffn_gate, w_ffn_up, w_ffn_down, norm_final, loss_target, m_norm_mix, m_w_in, m_b_gate, m_rel_bias, m_w_ret_out, m_w_att_out, m_w_out, m_norm_ffn, m_w_ffn_gate, m_w_ffn_up, m_w_ffn_down, m_norm_final, v_norm_mix, v_w_in, v_b_gate, v_rel_bias, v_w_ret_out, v_w_att_out, v_w_out, v_norm_ffn, v_w_ffn_gate, v_w_ffn_up, v_w_ffn_down, v_norm_final):
    w = dict(norm_mix=norm_mix, w_in=w_in, b_gate=b_gate, rel_bias=rel_bias, w_ret_out=w_ret_out, w_att_out=w_att_out,
             w_out=w_out, norm_ffn=norm_ffn, w_ffn_gate=w_ffn_gate, w_ffn_up=w_ffn_up, w_ffn_down=w_ffn_down,
             norm_final=norm_final)
    m = dict(norm_mix=m_norm_mix, w_in=m_w_in, b_gate=m_b_gate, rel_bias=m_rel_bias, w_ret_out=m_w_ret_out,
             w_att_out=m_w_att_out, w_out=m_w_out, norm_ffn=m_norm_ffn, w_ffn_gate=m_w_ffn_gate, w_ffn_up=m_w_ffn_up,
             w_ffn_down=m_w_ffn_down, norm_final=m_norm_final)
    v = dict(norm_mix=v_norm_mix, w_in=v_w_in, b_gate=v_b_gate, rel_bias=v_rel_bias, w_ret_out=v_w_ret_out,
             w_att_out=v_w_att_out, w_out=v_w_out, norm_ffn=v_norm_ffn, w_ffn_gate=v_w_ffn_gate, w_ffn_up=v_w_ffn_up,
             w_ffn_down=v_w_ffn_down, norm_final=v_norm_final)
    xi, yi, ci = _place()
    k_me = 2 * xi + yi

    place = jnp.stack([ci, k_me]).astype(jnp.int32)
    big = [n for n, _ in BIG]

    turned = ("w_ffn_gate", "w_ffn_up")
    shard = lambda d, n: jnp.swapaxes(d[n][0], 0, 1) if n in turned else d[n][0]
    whole = lambda a, n: (jnp.swapaxes(a, 0, 1) if n in turned else a)[None]

    by_shape = [[big[i] for i in group] for group in _same_shape([shard(w, n) for n in big])]
    bufs = {}
    for names in by_shape:
        bufs.update(zip(names, _cast_shards(place, [shard(w, n) for n in names], "cast_" + names[0])))
    rest = {n: bufs[n] for n in big if n != "w_in"}
    nrel_loc = rel_bias.shape[-1]
    grad_x, gbig, small_all = _local_step(place, x, loss_target, norm_mix, b_gate, rel_bias[0], norm_ffn, norm_final,
                                          bufs["w_in"], rest, _Exchange(place))

    small = _sum_slots(small_all, "reduce_small")
    D = D_MODEL
    loss = jnp.sum(small[8])
    drb_full = small[5:8].reshape(-1)[:ATT_HEADS * N_REL].reshape(ATT_HEADS, N_REL)
    g = {
        "norm_mix": small[0:1], "b_gate": small[1:3].reshape(1, 2 * D), "norm_ffn": small[3:4], "norm_final": small[4],
        "rel_bias": lax.dynamic_slice_in_dim(drb_full, k_me * nrel_loc, nrel_loc, axis=1)[None],
    }

    delta, new_m, new_v = {}, {}, {}
    for names in by_shape:
        res = _adamw_sum(place, [(shard(w, n), shard(m, n), shard(v, n), *gbig[n]) for n in names], "adamw_" + names[0])
        for n, (g_, d_, m_, v_) in zip(names, res):
            g[n], delta[n], new_m[n], new_v[n] = whole(g_, n), whole(d_, n), whole(m_, n), whole(v_, n)
    flat = lambda d: jnp.concatenate([d[n].reshape(-1) for n in SMALL])
    n_small = sum(int(np.prod(w[n].shape)) for n in SMALL)
    n_pad = -n_small % 1024
    packs = [jnp.pad(flat(d), (0, n_pad)).reshape(-1, 128) for d in (w, g, m, v)]
    outs = _adamw(*packs, "adamw_small")
    for res, dst in zip(outs, (delta, new_m, new_v)):
        off = 0
        fl = res.reshape(-1)
        for n in SMALL:
            sz = int(np.prod(w[n].shape))
            dst[n] = fl[off:off + sz].reshape(w[n].shape)
            off += sz

    return (loss, grad_x, *[g[n] for n in WEIGHTS], *[delta[n] for n in WEIGHTS], *[new_m[n] for n in WEIGHTS],
            *[new_v[n] for n in WEIGHTS])
```

```python
import functools

import numpy as np
import jax
import jax.numpy as jnp
from jax import lax
from jax.experimental import pallas as pl
from jax.experimental.pallas import tpu as pltpu

f32 = jnp.float32
bf16 = jnp.bfloat16

D_MODEL = 1024
CHUNK = 64
RET_HEADS = 4
RET_KEY_DIM = 128
RET_VAL_DIM = 256
ATT_HEADS = 8
ATT_HEAD_DIM = 64
ATT_W = ATT_HEADS * ATT_HEAD_DIM
BAND_CHUNKS = 8
PAD = BAND_CHUNKS * CHUNK
MAX_REL = 256
N_REL = CHUNK + MAX_REL
D_FF = 2816
N_IN = 6656
ROPE_BASE = 10000.0
EPS = 1e-6
NEG_INF = -1e30
C_RQ, C_RK, C_RV, C_RG, C_AQ, C_AK, C_AV, C_GL = 0, 512, 1024, 2048, 3072, 3584, 4096, 4608

ADAM_LR, ADAM_B1, ADAM_B2, ADAM_EPS, ADAM_WD, ADAM_STEP = 0.001, 0.9, 0.999, 1e-08, 0.01, 10

N_CHIPS = 4
N_DEV = 8
WGRAD_ACC_BYTES = 8 * 1024 * 1024
WGRAD_VMEM_BYTES = 40 * 1024 * 1024
ROW_TILE = 512
BIG_ROW_TILE = 1024
IN_ORDER = (0, 2, 3, 1)
QBLK = 256
KWIN = PAD + QBLK
TOEP = 1024
VMEM_LIMIT = 56 * 1024 * 1024
MESH = pl.DeviceIdType.MESH

BIG = (
    ("w_in", 1), ("w_ret_out", 0), ("w_att_out", 1), ("w_out", 0), ("w_ffn_gate", 1), ("w_ffn_up", 1), ("w_ffn_down", 0))
WEIGHTS = ("norm_mix", "w_in", "b_gate", "rel_bias", "w_ret_out", "w_att_out", "w_out", "norm_ffn", "w_ffn_gate",
           "w_ffn_up", "w_ffn_down", "norm_final")
SMALL = ("norm_mix", "b_gate", "rel_bias", "norm_ffn", "norm_final")


def _dot(a, b):
    return lax.dot_general(a, b, (((1,), (0,)), ((), ())), preferred_element_type=f32)


def _dot_nt(a, b):
    return lax.dot_general(a, b, (((1,), (1,)), ((), ())), preferred_element_type=f32)


def _dot_tn(a, b):
    return lax.dot_general(a, b, (((0,), (0,)), ((), ())), preferred_element_type=f32)


def _sig(x):
    return 1.0 / (1.0 + jnp.exp(-x))


def _tile(n, pref, mult):
    best = None
    for t in range(mult, min(n, pref) + 1, mult):
        if n % t == 0:
            best = t
    return best if best is not None else n


def _same_shape(arrays):
    groups = {}
    for i, a in enumerate(arrays):
        groups.setdefault(a.shape, []).append(i)
    return list(groups.values())


def _params(sem, vmem=VMEM_LIMIT):
    return pltpu.CompilerParams(dimension_semantics=sem, vmem_limit_bytes=vmem)


def _in_proj(place, x2, gamma, phase):
    T, D = x2.shape
    _, _, ns = phase.arrays[0].shape
    tm = _tile(T, BIG_ROW_TILE, 8)
    ni = T // tm
    pass_chip = lambda j: sum(jnp.where(j == n, f, 0) for n, f in enumerate(IN_ORDER))

    def body(p_ref, x_ref, g_ref, xn_ref, pr_ref, xs_ref, w_ref, w_sem, carried):
        j, i = pl.program_id(0), pl.program_id(1)
        pin, pout, sems = carried
        rows = pl.ds(pl.multiple_of(i * tm, tm), tm)

        @pl.when(i == 0)
        def _():
            for n, f in enumerate(IN_ORDER):
                if f:
                    @pl.when(j == n)
                    def _():
                        phase.arrived(f - 1, pin, pout, *sems)
                        phase.begin(2 + f, pin, pout, *sems)
                        phase.arrived(2 + f, pin, pout, *sems)
            shard = pltpu.make_async_copy(pout[0].at[jnp.bitwise_xor(p_ref[1], pass_chip(j))], w_ref, w_sem)
            shard.start()
            shard.wait()

        @pl.when(j == 0)
        def _():
            x = x_ref[...]
            r = lax.rsqrt(jnp.mean(x * x, axis=-1, keepdims=True) + EPS)
            xn = (x * r * g_ref[...]).astype(bf16)
            xs_ref[rows, :] = xn
            xn_ref[...] = xn

        pr_ref[...] = _dot(xs_ref[rows, :], w_ref[...]).astype(bf16)

    first_pass = lambda j, i, p: (jnp.where(j == 0, i, ni - 1), 0)
    return _call(
        body, phase, name="in_proj", grid=(N_CHIPS, ni), prefetch=(place,), expose=True,
        in_specs=[pl.BlockSpec((tm, D), first_pass), pl.BlockSpec((1, D), lambda j, i, p: (0, 0))],
        out_specs=[pl.BlockSpec((tm, D), first_pass),
                   pl.BlockSpec((tm, ns), lambda j, i, p: (i, jnp.bitwise_xor(p[1], pass_chip(j))))],
        out_shape=[jax.ShapeDtypeStruct((T, D), bf16), jax.ShapeDtypeStruct((T, N_CHIPS * ns), bf16)],
        scratch_shapes=[pltpu.VMEM((T, D), bf16), pltpu.VMEM((D, ns), bf16), pltpu.SemaphoreType.DMA],
        args=(x2, gamma))


def _rope_tables(S):
    d = RET_KEY_DIM
    freqs = ROPE_BASE ** (-jnp.arange(0, d, 2, dtype=f32) / d)
    ang = jnp.arange(S, dtype=f32)[:, None] * freqs[None, :]
    cos, sin = jnp.cos(ang), jnp.sin(ang)
    return jnp.concatenate([cos, cos], axis=1), jnp.concatenate([-sin, sin], axis=1)


def _decay_tables():
    H = RET_HEADS
    log_g = jnp.log(1.0 - 2.0 ** (-5.0 - jnp.arange(H, dtype=f32)))
    p = jnp.arange(CHUNK, dtype=f32)
    intra = jnp.exp(log_g[:, None, None] * jnp.abs(p[:, None] - p[None, :]))
    q_dec = jnp.exp(log_g[:, None] * (p[None, :] + 1.0))
    k_dec = jnp.exp(log_g[:, None] * (CHUNK - 1.0 - p[None, :]))
    c_dec = jnp.exp(log_g * CHUNK)
    q_dec = jnp.broadcast_to(q_dec[:, :, None], (H, CHUNK, RET_KEY_DIM))
    k_dec = jnp.broadcast_to(k_dec[:, :, None], (H, CHUNK, RET_KEY_DIM))
    c_dec = jnp.broadcast_to(c_dec[:, None, None], (H, 1, RET_VAL_DIM))
    return intra, q_dec, k_dec, c_dec


K_SCALE = RET_KEY_DIM ** -0.5


RET_CHUNKS = 4


def _ret_tables_specs():
    whole = lambda *shape: pl.BlockSpec(shape, lambda b, i: (0,) * len(shape))
    return [whole(RET_HEADS, CHUNK, CHUNK), whole(RET_HEADS, CHUNK, RET_KEY_DIM), whole(RET_HEADS, CHUNK, RET_KEY_DIM),
            whole(RET_HEADS, 1, RET_VAL_DIM)]


def _rotate(x, cos, sn):
    return x * cos + pltpu.roll(x, RET_KEY_DIM // 2, 1) * sn


def _ret_fwd(proj, B, S, rope, decay, phase=None):
    T = B * S
    nc = S // CHUNK
    H, dk, dv = RET_HEADS, RET_KEY_DIM, RET_VAL_DIM
    sb = RET_CHUNKS * CHUNK
    ns = S // sb

    def body(q_ref, k_ref, v_ref, g_ref, cos_ref, sin_ref, intra_ref, qd_ref, kd_ref, cd_ref,
             qr_ref, kr_ref, o_ref, u_ref, st_ref, state_ref):
        @pl.when(pl.program_id(1) == 0)
        def _():
            state_ref[...] = jnp.zeros_like(state_ref)

        cos, sn = cos_ref[...], sin_ref[...]
        for h in range(H):
            hs = slice(h * dk, (h + 1) * dk)
            qr_ref[:, hs] = _rotate(q_ref[:, hs].astype(f32), cos, sn).astype(bf16)
            kr_ref[:, hs] = (_rotate(k_ref[:, hs].astype(f32), cos, sn) * K_SCALE).astype(bf16)
        states = [state_ref[h] for h in range(H)]
        for ci in range(RET_CHUNKS):
            r = slice(ci * CHUNK, (ci + 1) * CHUNK)
            for h in range(H):
                hk, hv = slice(h * dk, (h + 1) * dk), slice(h * dv, (h + 1) * dv)
                qi, ki, vi = qr_ref[r, hk], kr_ref[r, hk], v_ref[r, hv]
                stb = states[h].astype(bf16)
                st_ref[0, h, ci] = stb
                s = (_dot_nt(qi, ki) * intra_ref[h]).astype(bf16)
                o = _dot(s, vi) + _dot((qi.astype(f32) * qd_ref[h]).astype(bf16), stb)
                states[h] = states[h] * cd_ref[h] + _dot_tn((ki.astype(f32) * kd_ref[h]).astype(bf16), vi)
                mu = jnp.mean(o, axis=-1, keepdims=True)
                xc = o - mu
                var = jnp.mean(xc * xc, axis=-1, keepdims=True)
                oh = xc * lax.rsqrt(var + EPS)
                g = g_ref[r, hv].astype(f32)
                o_ref[r, hv] = o.astype(bf16)
                u_ref[r, hv] = (g * _sig(g) * oh).astype(bf16)
        for h in range(H):
            state_ref[h] = states[h]

    blk = lambda w, c: pl.BlockSpec((sb, w), lambda b, i: (b * ns + i, c))
    return _call(
        body, phase, name="ret_fwd", grid=(B, ns), scratch_shapes=[pltpu.VMEM((H, dk, dv), f32)],
        in_specs=[blk(H * dk, C_RQ // (H * dk)), blk(H * dk, C_RK // (H * dk)), blk(H * dv, C_RV // (H * dv)),
                  blk(H * dv, C_RG // (H * dv)),
                  pl.BlockSpec((sb, dk), lambda b, i: (i, 0)), pl.BlockSpec((sb, dk), lambda b, i: (i, 0)),
                  *_ret_tables_specs()],
        out_specs=[blk(H * dk, 0), blk(H * dk, 0), blk(H * dv, 0), blk(H * dv, 0),
                   pl.BlockSpec((1, H, RET_CHUNKS, dk, dv), lambda b, i: (b, 0, i, 0, 0))],
        out_shape=[jax.ShapeDtypeStruct((T, H * dk), bf16), jax.ShapeDtypeStruct((T, H * dk), bf16),
                   jax.ShapeDtypeStruct((T, H * dv), bf16), jax.ShapeDtypeStruct((T, H * dv), bf16),
                   jax.ShapeDtypeStruct((B, H, nc, dk, dv), bf16)],
        args=(proj, proj, proj, proj, *rope, *decay))


def _bias_rows(rb):
    last = rb[:, N_REL - 1:]
    return jnp.concatenate([
        jnp.broadcast_to(last, (ATT_HEADS, PAD - MAX_REL + 1)),
        jnp.flip(rb[:, :N_REL - 1], axis=1),
        jnp.broadcast_to(rb[:, :1], (ATT_HEADS, KWIN - PAD - CHUNK)),
        jnp.broadcast_to(last, (ATT_HEADS, TOEP - KWIN)),
    ], axis=1)


def _build_bias(t_ref, bias_ref):
    row = lax.broadcasted_iota(jnp.int32, (QBLK, KWIN), 0) // CHUNK
    col = lax.broadcasted_iota(jnp.int32, (QBLK, KWIN), 1) // CHUNK
    delta = BAND_CHUNKS + row - col
    vis = (delta >= 0) & (delta <= BAND_CHUNKS)
    for h in range(ATT_HEADS):
        t = jnp.broadcast_to(t_ref[h:h + 1, :], (QBLK, TOEP))
        rolled = pltpu.roll(t, 0, 1, stride=1, stride_axis=0)
        bias_ref[h] = jnp.where(vis, rolled[:, :KWIN], NEG_INF)


ATT_SCALE = ATT_HEAD_DIM ** -0.5


def _att_probs(qh, kh, bias):
    s = _dot_nt(qh, kh) + bias
    m = jnp.max(s, axis=-1, keepdims=True)
    p = jnp.exp(s - m)
    return p * (1.0 / jnp.sum(p, axis=-1, keepdims=True))


def _first_of_pair():
    return lax.broadcasted_iota(jnp.int32, (1, 2 * ATT_HEAD_DIM), 1) < ATT_HEAD_DIM


def _by_window(i, step):
    sizes = list(range(QBLK, KWIN, QBLK))
    for n, nk in enumerate(sizes):
        pl.when(i == n)(functools.partial(step, nk))
    pl.when(i >= len(sizes))(functools.partial(step, KWIN))


def _att_fwd(proj, trows, B, S, phase=None):
    T = B * S
    nq = S // QBLK
    dh = ATT_HEAD_DIM

    def body(q_ref, k_ref, v_ref, t_ref, o_ref, bias_ref):
        i = pl.program_id(1)

        @pl.when((pl.program_id(0) == 0) & (i == 0))
        def _():
            _build_bias(t_ref, bias_ref)

        def step(nk):
            win = pl.ds(pl.multiple_of((i + 1) * QBLK - nk, QBLK), nk)
            kw, vw = k_ref[win, :], v_ref[win, :]
            first = _first_of_pair()
            outs = []
            for p in range(ATT_HEADS // 2):
                ps = slice(2 * p * dh, 2 * (p + 1) * dh)
                q2, k2, v2 = q_ref[:, ps] * ATT_SCALE, kw[:, ps], vw[:, ps]
                both = []
                for e in range(2):
                    qm = jnp.where(first == (e == 0), q2, jnp.zeros_like(q2))
                    pr = _att_probs(qm, k2, bias_ref[2 * p + e, :, KWIN - nk:])
                    both.append(_dot(pr.astype(bf16), v2))
                outs.append(jnp.where(first, both[0], both[1]))
            o_ref[...] = jnp.concatenate(outs, axis=1).astype(bf16)

        _by_window(i, step)

    return _call(
        body, phase, name="att_fwd", grid=(B, nq),
        in_specs=[pl.BlockSpec((QBLK, ATT_W), lambda b, i: (b * nq + i, C_AQ // ATT_W)),
                  pl.BlockSpec((S, ATT_W), lambda b, i: (b, C_AK // ATT_W)),
                  pl.BlockSpec((S, ATT_W), lambda b, i: (b, C_AV // ATT_W)),
                  pl.BlockSpec((ATT_HEADS, TOEP), lambda b, i: (0, 0))],
        out_specs=[pl.BlockSpec((QBLK, ATT_W), lambda b, i: (b * nq + i, 0))],
        out_shape=[jax.ShapeDtypeStruct((T, ATT_W), bf16)],
        scratch_shapes=[pltpu.VMEM((ATT_HEADS, QBLK, KWIN), f32)],
        args=(proj, proj, proj, trows))


def _gl_specs(tm):
    w = 512
    return [pl.BlockSpec((tm, w), functools.partial(lambda i, j: (i, C_GL // 512 + j), j=j)) for j in range(4)]


def _gates(gl_refs, bg_ref):
    gl = jnp.concatenate([r[...] for r in gl_refs], axis=1).astype(f32) + bg_ref[...]
    g = _sig(gl)
    return g[:, :D_MODEL], g[:, D_MODEL:]


def _mix_fwd(x2, proj, u, ao, b_gate, w_ro, w_ao, w_out, phase=None):
    T, D = x2.shape
    tm = _tile(T, ROW_TILE, 8)

    def body(x_ref, u_ref, ao_ref, g0, g1, g2, g3, bg_ref, wro_ref, wao_ref, wo_ref, h1_ref, yr_ref, ya_ref):
        yr = _dot(u_ref[...], wro_ref[...])
        ao = ao_ref[...]
        ya = jnp.concatenate([_dot(ao, wao_ref[k]) for k in range(N_CHIPS)], axis=1)
        gr, ga = _gates((g0, g1, g2, g3), bg_ref)
        mix = gr * yr + ga * ya
        h1_ref[...] = x_ref[...] + _dot(mix.astype(bf16), wo_ref[...])
        yr_ref[...] = yr.astype(bf16)
        ya_ref[...] = ya.astype(bf16)

    full = lambda a: pl.BlockSpec(a.shape, lambda i: (0,) * a.ndim)
    row = lambda n: pl.BlockSpec((tm, n), lambda i: (i, 0))
    return _call(
        body, phase, name="mix_fwd", grid=(T // tm,), scratch_shapes=[],
        in_specs=[row(D), row(D), row(ATT_W), *_gl_specs(tm), full(b_gate), full(w_ro), full(w_ao), full(w_out)],
        out_specs=[row(D), row(D), row(D)],
        out_shape=[jax.ShapeDtypeStruct((T, D), f32), jax.ShapeDtypeStruct((T, D), bf16),
                   jax.ShapeDtypeStruct((T, D), bf16)],
        args=(x2, u, ao, proj, proj, proj, proj, b_gate, w_ro, w_ao, w_out))


def _ffn_fwd(h1, g_ffn, wg, wu, wd, g_fin, target):
    T, D = h1.shape
    nf, tf, _ = wg.shape
    tm = _tile(T, ROW_TILE, 8)

    def body(h1_ref, g_ref, wg_ref, wu_ref, wd_ref, gf_ref, tg_ref, hn_ref, a_ref, b_ref, f_ref, dh2_ref, part_ref):
        h1v = h1_ref[...]
        r = lax.rsqrt(jnp.mean(h1v * h1v, axis=-1, keepdims=True) + EPS)
        hn = (h1v * r * g_ref[...]).astype(bf16)
        hn_ref[...] = hn
        h2 = h1v
        for k in range(nf):
            a = _dot_nt(hn, wg_ref[k])
            b = _dot_nt(hn, wu_ref[k])
            f = ((a * _sig(a)) * b).astype(bf16)
            a_ref[k] = a.astype(bf16)
            b_ref[k] = b.astype(bf16)
            f_ref[k] = f
            h2 = h2 + _dot(f, wd_ref[k])
        r = lax.rsqrt(jnp.mean(h2 * h2, axis=-1, keepdims=True) + EPS)
        n = h2 * r
        gf = gf_ref[...]
        e = n * gf - tg_ref[...]
        dy = e * (1.0 / D)
        dn = dy * gf
        dh2_ref[...] = r * (dn - n * jnp.mean(dn * n, axis=-1, keepdims=True))
        part_ref[...] = jnp.zeros_like(part_ref)
        part_ref[0:1, :] = jnp.sum(dy * n, axis=0, keepdims=True)
        part_ref[1:2, :] = (0.5 / D) * jnp.sum(e * e, axis=0, keepdims=True)

    row = lambda n: pl.BlockSpec((tm, n), lambda i: (i, 0))
    vec = pl.BlockSpec((1, D), lambda i: (0, 0))
    col = pl.BlockSpec((nf, tm, tf), lambda i: (0, i, 0))
    held = lambda w: pl.BlockSpec(w.shape, lambda i: (0, 0, 0), pipeline_mode=pl.Buffered(1))
    act = jax.ShapeDtypeStruct((nf, T, tf), bf16)
    return pl.pallas_call(
        body, name="ffn_fwd", grid=(T // tm,),
        in_specs=[row(D), vec, held(wg), held(wu), held(wd), vec, row(D)],
        out_specs=[row(D), col, col, col, row(D), pl.BlockSpec((8, D), lambda i: (i, 0))],
        out_shape=[jax.ShapeDtypeStruct((T, D), bf16), act, act, act,
                   jax.ShapeDtypeStruct((T, D), f32), jax.ShapeDtypeStruct((T // tm * 8, D), f32)],
        compiler_params=_params(("parallel",)),
    )(h1, g_ffn, wg, wu, wd, g_fin, target)


def _ffn_bwd(dh2, h1, g_ffn, a, b, wg, wu, wd):
    T, D = h1.shape
    nf, tf, _ = wg.shape
    tm = _tile(T, ROW_TILE // 2, 8)

    def body(dh2_ref, h1_ref, g_ref, a_ref, b_ref, wg_ref, wu_ref, wd_ref, da_ref, db_ref, dh1_ref, part_ref):
        dh2v = dh2_ref[...]
        dh2b = dh2v.astype(bf16)
        dhn = jnp.zeros((tm, D), f32)
        for k in range(nf):
            df = _dot_nt(dh2b, wd_ref[k])
            av = a_ref[k].astype(f32)
            sg = _sig(av)
            db = (df * (av * sg)).astype(bf16)
            da = (df * b_ref[k].astype(f32) * (sg * (1.0 + av * (1.0 - sg)))).astype(bf16)
            da_ref[k] = da
            db_ref[k] = db
            dhn = dhn + _dot(da, wg_ref[k]) + _dot(db, wu_ref[k])
        h = h1_ref[...]
        r = lax.rsqrt(jnp.mean(h * h, axis=-1, keepdims=True) + EPS)
        n = h * r
        dn = dhn * g_ref[...]
        dh1_ref[...] = dh2v + r * (dn - n * jnp.mean(dn * n, axis=-1, keepdims=True))
        part_ref[...] = jnp.zeros_like(part_ref)
        part_ref[0:1, :] = jnp.sum(dhn * n, axis=0, keepdims=True)

    row = lambda n: pl.BlockSpec((tm, n), lambda i: (i, 0))
    col = pl.BlockSpec((nf, tm, tf), lambda i: (0, i, 0))
    held = lambda w: pl.BlockSpec(w.shape, lambda i: (0, 0, 0), pipeline_mode=pl.Buffered(1))
    act = jax.ShapeDtypeStruct((nf, T, tf), bf16)
    return pl.pallas_call(
        body, name="ffn_bwd", grid=(T // tm,),
        in_specs=[row(D), row(D), pl.BlockSpec((1, D), lambda i: (0, 0)), col, col, held(wg), held(wu), held(wd)],
        out_specs=[col, col, row(D), pl.BlockSpec((8, D), lambda i: (i, 0))],
        out_shape=[act, act, jax.ShapeDtypeStruct((T, D), f32), jax.ShapeDtypeStruct((T // tm * 8, D), f32)],
        compiler_params=_params(("parallel",)),
    )(dh2, h1, g_ffn, a, b, wg, wu, wd)


def _mix_bwd(dh1, proj, yr, ya, b_gate, w_ro, w_ao, w_out, phase=None):
    T, D = dh1.shape
    tm = _tile(T, ROW_TILE, 8)

    def body(dh1_ref, g0, g1, g2, g3, bg_ref, yr_ref, ya_ref, wro_ref, wao_ref, wo_ref,
             du_ref, dao_ref, dgl_ref, mix_ref, dyr_ref, dya_ref, part_ref):
        dmix = _dot_nt(dh1_ref[...].astype(bf16), wo_ref[...])
        gr, ga = _gates((g0, g1, g2, g3), bg_ref)
        yr = yr_ref[...].astype(f32)
        ya = ya_ref[...].astype(f32)
        dyr = (dmix * gr).astype(bf16)
        dya = (dmix * ga).astype(bf16)
        dgl = jnp.concatenate([dmix * yr * gr * (1.0 - gr), dmix * ya * ga * (1.0 - ga)], axis=1)
        du_ref[...] = _dot_nt(dyr, wro_ref[...]).astype(bf16)
        ns = wao_ref.shape[2]
        dao = _dot_nt(dya[:, :ns], wao_ref[0])
        for k in range(1, N_CHIPS):
            dao = dao + _dot_nt(dya[:, k * ns:(k + 1) * ns], wao_ref[k])
        dao_ref[...] = dao.astype(bf16)
        dgl_ref[...] = dgl.astype(bf16)
        mix_ref[...] = (gr * yr + ga * ya).astype(bf16)
        dyr_ref[...] = dyr
        dya_ref[...] = dya
        part_ref[...] = jnp.zeros_like(part_ref)
        part_ref[0:1, :] = jnp.sum(dgl, axis=0, keepdims=True)

    full = lambda a: pl.BlockSpec(a.shape, lambda i: (0,) * a.ndim)
    row = lambda n: pl.BlockSpec((tm, n), lambda i: (i, 0))
    return _call(
        body, phase, name="mix_bwd", grid=(T // tm,), scratch_shapes=[],
        in_specs=[row(D), *_gl_specs(tm), full(b_gate), row(D), row(D), full(w_ro), full(w_ao), full(w_out)],
        out_specs=[row(D), row(ATT_W), row(2 * D), row(D), row(D), row(D), pl.BlockSpec((8, 2 * D), lambda i: (i, 0))],
        out_shape=[jax.ShapeDtypeStruct((T, D), bf16), jax.ShapeDtypeStruct((T, ATT_W), bf16),
                   jax.ShapeDtypeStruct((T, 2 * D), bf16), jax.ShapeDtypeStruct((T, D), bf16),
                   jax.ShapeDtypeStruct((T, D), bf16), jax.ShapeDtypeStruct((T, D), bf16),
                   jax.ShapeDtypeStruct((T // tm * 8, 2 * D), f32)],
        args=(dh1, proj, proj, proj, proj, b_gate, yr, ya, w_ro, w_ao, w_out))


def _ret_bwd(proj, qr, kr, o, states, du, B, S, rope, decay, phase=None):
    T = B * S
    nc = S // CHUNK
    H, dk, dv = RET_HEADS, RET_KEY_DIM, RET_VAL_DIM

    sb = RET_CHUNKS * CHUNK
    ns = S // sb

    def body(qr_ref, kr_ref, v_ref, g_ref, o_ref, st_ref, du_ref, cos_ref, sin_ref, intra_ref, qd_ref, kd_ref, cd_ref,
             dp_ref, dstate_ref):
        dq_ref, dk_ref = dp_ref.at[:, pl.ds(C_RQ, H * dk)], dp_ref.at[:, pl.ds(C_RK, H * dk)]
        dv_ref, dg_ref = dp_ref.at[:, pl.ds(C_RV, H * dv)], dp_ref.at[:, pl.ds(C_RG, H * dv)]

        @pl.when(pl.program_id(1) == 0)
        def _():
            dstate_ref[...] = jnp.zeros_like(dstate_ref)

        cos, snb = cos_ref[...], -sin_ref[...]
        dstates = [dstate_ref[h] for h in range(H)]
        for ci in reversed(range(RET_CHUNKS)):
            r = slice(ci * CHUNK, (ci + 1) * CHUNK)
            for h in range(H):
                hk, hv = slice(h * dk, (h + 1) * dk), slice(h * dv, (h + 1) * dv)
                intra, qd, kd = intra_ref[h], qd_ref[h], kd_ref[h]
                qi, ki, vi = qr_ref[r, hk], kr_ref[r, hk], v_ref[r, hv]
                si = st_ref[0, h, ci]
                o = o_ref[r, hv].astype(f32)
                mu = jnp.mean(o, axis=-1, keepdims=True)
                xc = o - mu
                rstd = lax.rsqrt(jnp.mean(xc * xc, axis=-1, keepdims=True) + EPS)
                oh = xc * rstd
                g = g_ref[r, hv].astype(f32)
                sg = _sig(g)
                dui = du_ref[r, hv].astype(f32)
                dg_ref[r, hv] = (dui * oh * (sg * (1.0 + g * (1.0 - sg)))).astype(bf16)
                doh = dui * (g * sg)
                do = rstd * (doh - jnp.mean(doh, axis=-1, keepdims=True)
                             - oh * jnp.mean(doh * oh, axis=-1, keepdims=True))
                dob = do.astype(bf16)
                p = (_dot_nt(qi, ki) * intra).astype(bf16)
                dsb = dstates[h].astype(bf16)
                kt = (ki.astype(f32) * kd).astype(bf16)
                qt = (qi.astype(f32) * qd).astype(bf16)
                dv_ref[r, hv] = (_dot_tn(p, dob) + _dot(kt, dsb)).astype(bf16)
                da = (_dot_nt(dob, vi) * intra).astype(bf16)
                dq = _dot(da, ki) + _dot_nt(dob, si) * qd
                dkk = (_dot_tn(da, qi) + _dot_nt(vi, dsb) * kd) * K_SCALE
                dq_ref[r, hk] = _rotate(dq, cos[r], snb[r]).astype(bf16)
                dk_ref[r, hk] = _rotate(dkk, cos[r], snb[r]).astype(bf16)
                dstates[h] = dstates[h] * cd_ref[h] + _dot_tn(qt, dob)
        for h in range(H):
            dstate_ref[h] = dstates[h]

    blk = lambda w, c: pl.BlockSpec((sb, w), lambda b, i: (b * ns + ns - 1 - i, c))
    return _call(
        body, phase, name="ret_bwd", grid=(B, ns),
        in_specs=[blk(H * dk, 0), blk(H * dk, 0), blk(H * dv, C_RV // (H * dv)), blk(H * dv, C_RG // (H * dv)),
                  blk(H * dv, 0),
                  pl.BlockSpec((1, H, RET_CHUNKS, dk, dv), lambda b, i: (b, 0, ns - 1 - i, 0, 0)),
                  blk(H * dv, 0),
                  pl.BlockSpec((sb, dk), lambda b, i: (ns - 1 - i, 0)), pl.BlockSpec((sb, dk), lambda b, i: (ns - 1 - i, 0)),
                  *_ret_tables_specs()],
        out_specs=[blk(C_AQ, 0)], out_shape=[jax.ShapeDtypeStruct((T, N_IN), bf16)],
        scratch_shapes=[pltpu.VMEM((H, dk, dv), f32)],
        args=(qr, kr, proj, proj, o, states, du, *rope, *decay))


def _att_bwd(proj, dao, trows, dproj, B, S, phase=None):
    T = B * S
    nq = S // QBLK
    dh = ATT_HEAD_DIM
    scale = ATT_HEAD_DIM ** -0.5

    def body(q_ref, k_ref, v_ref, do_ref, t_ref, _, dp_ref, vec_ref, bias_ref, dbias_ref, dka_ref, dva_ref):
        b, i = pl.program_id(0), pl.program_id(1)

        @pl.when((b == 0) & (i == 0))
        def _():
            _build_bias(t_ref, bias_ref)
            dbias_ref[...] = jnp.zeros_like(dbias_ref)

        @pl.when(i == 0)
        def _():
            dka_ref[...] = jnp.zeros_like(dka_ref)
            dva_ref[...] = jnp.zeros_like(dva_ref)

        def step(nk):
            win = pl.ds(pl.multiple_of((i + 1) * QBLK - nk, QBLK), nk)
            kw, vw = k_ref[win, :], v_ref[win, :]
            first = _first_of_pair()
            first_rows = lax.broadcasted_iota(jnp.int32, (2 * dh, 1), 0) < dh
            dqs, dks, dvs = [], [], []
            for p in range(ATT_HEADS // 2):
                ps = slice(2 * p * dh, 2 * (p + 1) * dh)
                q2, k2, v2, do2 = q_ref[:, ps] * ATT_SCALE, kw[:, ps], vw[:, ps], do_ref[:, ps]
                dq2, dk2, dv2 = [], [], []
                for e in range(2):
                    h = 2 * p + e
                    mine = first == (e == 0)
                    pr = _att_probs(jnp.where(mine, q2, jnp.zeros_like(q2)), k2, bias_ref[h, :, KWIN - nk:])
                    dp = _dot_nt(jnp.where(mine, do2, jnp.zeros_like(do2)), v2)
                    ds = pr * (dp - jnp.sum(pr * dp, axis=-1, keepdims=True))
                    dbias_ref[h, :, KWIN - nk:] += ds
                    dsb = ds.astype(bf16)
                    dq2.append(_dot(dsb, k2) * ATT_SCALE)
                    dk2.append(_dot_tn(q2, dsb))
                    dv2.append(_dot_tn(do2, pr.astype(bf16)))
                dqs.append(jnp.where(first, dq2[0], dq2[1]))
                dks.append(jnp.where(first_rows, dk2[0], dk2[1]))
                dvs.append(jnp.where(first_rows, dv2[0], dv2[1]))
            dp_ref[pl.ds(pl.multiple_of(i * QBLK, QBLK), QBLK), :ATT_W] = jnp.concatenate(dqs, axis=1).astype(bf16)
            dka_ref[:, win] += jnp.concatenate(dks, axis=0)
            dva_ref[:, win] += jnp.concatenate(dvs, axis=0)

        _by_window(i, step)

        @pl.when(i == nq - 1)
        def _():
            dp_ref[:, ATT_W:2 * ATT_W] = dka_ref[...].T.astype(bf16)
            dp_ref[:, 2 * ATT_W:] = dva_ref[...].T.astype(bf16)

        @pl.when((b == B - 1) & (i == nq - 1))
        def _():
            rr = lax.broadcasted_iota(jnp.int32, (QBLK, QBLK), 0)
            cc = lax.broadcasted_iota(jnp.int32, (QBLK, QBLK), 1)
            flip = jnp.where(rr + cc == QBLK - 1, 1.0, 0.0).astype(bf16)
            for h in range(ATT_HEADS):
                d = dbias_ref[h]
                hi = d.astype(bf16)
                lo = (d - hi.astype(f32)).astype(bf16)
                rev = _dot(flip, hi) + _dot(flip, lo)
                wide = jnp.concatenate([rev, jnp.zeros((QBLK, TOEP - KWIN), f32)], axis=1)
                rolled = pltpu.roll(wide, 0, 1, stride=1, stride_axis=0)
                vec_ref[h:h + 1, :] = jnp.sum(rolled, axis=0, keepdims=True)

    qspec = lambda c: pl.BlockSpec((QBLK, ATT_W), lambda b, i: (b * nq + i, c))
    kspec = lambda c: pl.BlockSpec((S, ATT_W), lambda b, i: (b, c))
    return _call(
        body, phase, name="att_bwd", grid=(B, nq), aliases={5: 0},
        in_specs=[qspec(C_AQ // ATT_W), kspec(C_AK // ATT_W), kspec(C_AV // ATT_W), qspec(0),
                  pl.BlockSpec((ATT_HEADS, TOEP), lambda b, i: (0, 0)), pl.BlockSpec(memory_space=pl.ANY)],
        out_specs=[pl.BlockSpec((S, 3 * ATT_W), lambda b, i: (b, C_AQ // (3 * ATT_W))),
                   pl.BlockSpec((ATT_HEADS, TOEP), lambda b, i: (0, 0))],
        out_shape=[jax.ShapeDtypeStruct((T, N_IN), bf16), jax.ShapeDtypeStruct((ATT_HEADS, TOEP), f32)],
        scratch_shapes=[pltpu.VMEM((ATT_HEADS, QBLK, KWIN), f32), pltpu.VMEM((ATT_HEADS, QBLK, KWIN), f32),
                        pltpu.VMEM((ATT_W, S), f32), pltpu.VMEM((ATT_W, S), f32)],
        args=(proj, proj, proj, dao, trows, dproj))


def _in_proj_bwd(dproj, w_in, x2, gamma, dh1, phase=None):
    T, D = x2.shape
    nk, _, tk = w_in.shape
    tm = _tile(T, BIG_ROW_TILE, 8)

    def body(dp_ref, w_ref, x_ref, g_ref, dh1_ref, dx_ref, part_ref, acc_ref):
        j = pl.program_id(1)

        @pl.when(j == 0)
        def _():
            acc_ref[...] = jnp.zeros_like(acc_ref)

        acc_ref[...] += _dot_nt(dp_ref[...], w_ref[0])

        @pl.when(j == nk - 1)
        def _():
            x = x_ref[...]
            r = lax.rsqrt(jnp.mean(x * x, axis=-1, keepdims=True) + EPS)
            n = x * r
            dxn = acc_ref[...]
            dn = dxn * g_ref[...]
            dx_ref[...] = dh1_ref[...] + r * (dn - n * jnp.mean(dn * n, axis=-1, keepdims=True))
            part_ref[...] = jnp.zeros_like(part_ref)
            part_ref[0:1, :] = jnp.sum(dxn * n, axis=0, keepdims=True)

    row = lambda n: pl.BlockSpec((tm, n), lambda i, j: (i, 0))
    return _call(
        body, phase, name="in_proj_bwd", grid=(T // tm, nk),
        in_specs=[pl.BlockSpec((tm, tk), lambda i, j: (i, j)), pl.BlockSpec((1, D, tk), lambda i, j: (j, 0, 0)), row(D),
                  pl.BlockSpec((1, D), lambda i, j: (0, 0)), row(D)],
        out_specs=[row(D), pl.BlockSpec((8, D), lambda i, j: (i, 0))],
        out_shape=[jax.ShapeDtypeStruct((T, D), f32), jax.ShapeDtypeStruct((T // tm * 8, D), f32)],
        scratch_shapes=[pltpu.VMEM((tm, D), f32)],
        args=(dproj, w_in, x2, gamma, dh1))


def _wgrad(a, b, shard_axis, name, phase=None):
    def spec(arr, sharded, tt):
        if arr.ndim == 3:
            return arr.shape[2], pl.BlockSpec((1, tt, arr.shape[2]), lambda s, t: (s, t, 0))
        if sharded:
            w = arr.shape[1] // N_CHIPS
            return w, pl.BlockSpec((tt, w), lambda s, t: (t, s))
        return arr.shape[1], pl.BlockSpec((tt, arr.shape[1]), lambda s, t: (t, 0))

    T = a.shape[-2]
    whole = a.ndim == 2 and b.ndim == 2 and a.shape[1] * b.shape[1] * 4 <= WGRAD_ACC_BYTES
    width = lambda arr, sharded: arr.shape[-1] // (1 if whole or arr.ndim == 3 or not sharded else N_CHIPS)
    wa, wb_ = width(a, shard_axis == 0), width(b, shard_axis == 1)
    fixed = wa * wb_ * (4 + 2 * 2)
    tt = T
    while tt > 256 and 2 * tt * (wa * a.dtype.itemsize + wb_ * b.dtype.itemsize) + fixed > WGRAD_VMEM_BYTES:
        tt //= 2
    nt = T // tt
    if whole:
        K, N = a.shape[1], b.shape[1]
        a_spec, b_spec = pl.BlockSpec((tt, K), lambda s, t: (t, 0)), pl.BlockSpec((tt, N), lambda s, t: (t, 0))
        out_block = (N_CHIPS, K // N_CHIPS, N) if shard_axis == 0 else (N_CHIPS, K, N // N_CHIPS)
        out_spec = pl.BlockSpec(out_block, lambda s, t: (0, 0, 0))
    else:
        K, a_spec = spec(a, shard_axis == 0, tt)
        N, b_spec = spec(b, shard_axis == 1, tt)
        out_block = (N_CHIPS, K, N)
        out_spec = pl.BlockSpec((1, K, N), lambda s, t: (s, 0, 0))

    def body(a_ref, b_ref, o_ref, acc_ref):
        t = pl.program_id(1)

        @pl.when(t == 0)
        def _():
            acc_ref[...] = jnp.zeros_like(acc_ref)

        av = a_ref[0] if a.ndim == 3 else a_ref[...]
        bv = b_ref[0] if b.ndim == 3 else b_ref[...]
        acc_ref[...] += _dot_tn(av.astype(bf16), bv.astype(bf16))

        @pl.when(t == nt - 1)
        def _():
            if not whole:
                o_ref[0] = acc_ref[...].astype(bf16)
            else:
                _, kk, nn = out_block
                for s in range(N_CHIPS):
                    o_ref[s] = (acc_ref[s * kk:(s + 1) * kk, :] if shard_axis == 0
                                else acc_ref[:, s * nn:(s + 1) * nn]).astype(bf16)

    (grad,), carried = _call(
        body, phase, name=name, grid=(1 if whole else N_CHIPS, nt), in_specs=[a_spec, b_spec], out_specs=[out_spec],
        out_shape=[jax.ShapeDtypeStruct(out_block, bf16)], scratch_shapes=[pltpu.VMEM((K, N), f32)], args=(a, b))
    return grad, carried


def _adamw_sum(place, groups, name):
    n = len(groups)
    R, C = groups[0][0].shape
    half = R // 2
    tr = _tile(half, max(16, (1 << 18) // C // 16 * 16), 16)
    nr = half // tr

    def body(p_ref, *refs):
        for a in range(n):
            w_ref, m_ref, v_ref, part_ref, fc_ref, fs_ref = refs[6 * a:6 * a + 6]
            g_ref, d_ref, mo_ref, vo_ref = refs[6 * n + 4 * a:6 * n + 4 * a + 4]
            up = lambda x: x.astype(f32)
            mine = ((up(part_ref[0]) + up(fc_ref[0])) + up(fc_ref[1])) + up(fc_ref[2])
            sibs = ((up(fs_ref[0]) + up(fs_ref[1])) + up(fs_ref[2])) + up(fs_ref[3])
            g_ = jnp.where(pl.program_id(0) == p_ref[0], mine, sibs)
            m_ = ADAM_B1 * m_ref[...] + (1.0 - ADAM_B1) * g_
            v_ = ADAM_B2 * v_ref[...] + (1.0 - ADAM_B2) * (g_ * g_)
            m_hat = m_ / (1.0 - ADAM_B1 ** ADAM_STEP)
            v_hat = v_ / (1.0 - ADAM_B2 ** ADAM_STEP)
            g_ref[...] = g_
            d_ref[...] = -ADAM_LR * (m_hat / (jnp.sqrt(v_hat) + ADAM_EPS) + ADAM_WD * w_ref[...])
            mo_ref[...] = m_
            vo_ref[...] = v_

    spec = pl.BlockSpec((tr, C), lambda h, r, p: (h * nr + r, 0))
    one = [spec, spec, spec, pl.BlockSpec((1, tr, C), lambda h, r, p: (p[1], jnp.where(h == p[0], r, 0), 0)),
           pl.BlockSpec((3, tr, C), lambda h, r, p: (0, jnp.where(h == p[0], r, 0), 0)),
           pl.BlockSpec((4, tr, C), lambda h, r, p: (0, jnp.where(h == p[0], 0, r), 0))]
    res = pl.pallas_call(
        body, name=name,
        grid_spec=pltpu.PrefetchScalarGridSpec(num_scalar_prefetch=1, grid=(2, nr), in_specs=one * n,
                                               out_specs=[spec] * (4 * n)),
        out_shape=[jax.ShapeDtypeStruct((R, C), f32)] * (4 * n),
        compiler_params=_params(("parallel", "parallel")),
    )(place, *[x for g in groups for x in g])
    return [tuple(res[4 * a:4 * a + 4]) for a in range(n)]


def _adamw(w, g, m, v, name):
    R, C = w.shape
    tr = _tile(R, max(8, (1 << 18) // C // 8 * 8), 8)

    def body(w_ref, g_ref, m_ref, v_ref, d_ref, mo_ref, vo_ref):
        g_ = g_ref[...]
        m_ = ADAM_B1 * m_ref[...] + (1.0 - ADAM_B1) * g_
        v_ = ADAM_B2 * v_ref[...] + (1.0 - ADAM_B2) * (g_ * g_)
        m_hat = m_ / (1.0 - ADAM_B1 ** ADAM_STEP)
        v_hat = v_ / (1.0 - ADAM_B2 ** ADAM_STEP)
        d_ref[...] = -ADAM_LR * (m_hat / (jnp.sqrt(v_hat) + ADAM_EPS) + ADAM_WD * w_ref[...])
        mo_ref[...] = m_
        vo_ref[...] = v_

    spec = pl.BlockSpec((tr, C), lambda i: (i, 0))
    return pl.pallas_call(
        body, name=name, grid=(R // tr,), in_specs=[spec] * 4, out_specs=[spec] * 3,
        out_shape=[jax.ShapeDtypeStruct((R, C), f32)] * 3,
        compiler_params=_params(("parallel",)),
    )(w, g, m, v)


def _place():
    return lax.axis_index("x"), lax.axis_index("y"), lax.axis_index("c")


def _other_chips(x, y):
    chips = [(1 - x, y), (x, 1 - y), (1 - x, 1 - y)]
    return chips, [2 * cx + cy for cx, cy in chips]


def _spread_phase(blk):
    def peers():
        x, y, c = _place()
        return [tuple(1 - p if (k >> s) & 1 else p for p, s in ((x, 2), (y, 1), (c, 0))) for k in range(1, N_DEV)]

    def copies(pin, out):
        x, y, c = _place()
        mine = out[0].at[4 * x + 2 * y + c]
        return [(mine, mine, peer) for peer in peers()]

    stack = jnp.broadcast_to(blk, (N_DEV,) + blk.shape)
    return _Phase([stack], [jax.ShapeDtypeStruct(stack.shape, stack.dtype)], {0: 0}, N_DEV - 1, copies,
                  lambda pin, out: [out[0].at[4 * px + 2 * py + pc] for px, py, pc in peers()])


def _sum_slots(stack, name):
    def body(s_ref, o_ref):
        tot = s_ref[0]
        for d in range(1, stack.shape[0]):
            tot = tot + s_ref[d]
        o_ref[...] = tot

    vm = pl.BlockSpec(memory_space=pltpu.VMEM)
    return pl.pallas_call(body, name=name, in_specs=[vm], out_specs=vm,
                          out_shape=jax.ShapeDtypeStruct(stack.shape[1:], stack.dtype))(stack)


def _cast_shards(place, ws, name):
    n = len(ws)
    R, C = ws[0].shape
    tr = _tile(R, max(16, (1 << 19) // C // 16 * 16), 16)

    def body(p_ref, *refs):
        for a in range(n):
            refs[n + a][0] = refs[a][...].astype(bf16)

    return pl.pallas_call(
        body, name=name,
        grid_spec=pltpu.PrefetchScalarGridSpec(
            num_scalar_prefetch=1, grid=(R // tr,),
            in_specs=[pl.BlockSpec((tr, C), lambda r, p: (r, 0))] * n,
            out_specs=[pl.BlockSpec((1, tr, C), lambda r, p: (p[1], r, 0))] * n),
        out_shape=[jax.ShapeDtypeStruct((N_CHIPS, R, C), bf16)] * n,
        compiler_params=_params(("parallel",)),
    )(place, *ws)


class _Phase:
    def __init__(self, arrays, out_shapes, aliases, n_copies, copies, arrivals, own_starts=(), own_waits=()):
        self.arrays, self.out_shapes, self.aliases = list(arrays), list(out_shapes), dict(aliases)
        self.n_copies, self.copies, self.arrivals = n_copies, copies, arrivals
        self.own_starts, self.own_waits = tuple(own_starts), tuple(own_waits)

    def sems(self):
        return [pltpu.SemaphoreType.DMA((self.n_copies,)), pltpu.SemaphoreType.DMA((self.n_copies,))]

    def _descriptors(self, pin, pout, send_sems, recv_sems):
        return [pltpu.make_async_remote_copy(src_ref=s, dst_ref=d, send_sem=send_sems.at[i], recv_sem=recv_sems.at[i],
                                             device_id=to, device_id_type=MESH)
                for i, (s, d, to) in enumerate(self.copies(pin, pout))]

    def _arrival(self, i, pin, pout, send_sems, recv_sems):
        dst = self.arrivals(pin, pout)[i]
        return pltpu.make_async_remote_copy(src_ref=dst, dst_ref=dst, send_sem=send_sems.at[i], recv_sem=recv_sems.at[i],
                                            device_id=_place(), device_id_type=MESH)

    def start(self, pin, pout, send_sems, recv_sems):
        for i, cp in enumerate(self._descriptors(pin, pout, send_sems, recv_sems)):
            if i not in self.own_starts:
                cp.start()

    def begin(self, i, pin, pout, send_sems, recv_sems):
        self._descriptors(pin, pout, send_sems, recv_sems)[i].start()

    def arrived(self, i, pin, pout, send_sems, recv_sems):
        self._arrival(i, pin, pout, send_sems, recv_sems).wait_recv()

    def finish(self, pin, pout, send_sems, recv_sems):
        for i in range(self.n_copies):
            if i not in self.own_waits:
                self._arrival(i, pin, pout, send_sems, recv_sems).wait_recv()
        for cp in self._descriptors(pin, pout, send_sems, recv_sems):
            cp.wait_send()


def _join(phases):
    if len(phases) == 1:
        return phases[0]
    ai = np.cumsum([0] + [len(p.arrays) for p in phases])
    oi = np.cumsum([0] + [len(p.out_shapes) for p in phases])

    def each(fn_name, pin, pout):
        return [item for k, p in enumerate(phases)
                for item in getattr(p, fn_name)(pin[ai[k]:ai[k + 1]], pout[oi[k]:oi[k + 1]])]

    aliases = {int(ai[k]) + i: int(oi[k]) + j for k, p in enumerate(phases) for i, j in p.aliases.items()}
    ci = np.cumsum([0] + [p.n_copies for p in phases])
    shifted = lambda attr: [int(ci[k]) + i for k, p in enumerate(phases) for i in getattr(p, attr)]
    return _Phase([a for p in phases for a in p.arrays], [s for p in phases for s in p.out_shapes], aliases,
                  int(ci[-1]), functools.partial(each, "copies"), functools.partial(each, "arrivals"),
                  shifted("own_starts"), shifted("own_waits"))


def _call(body, phase, *, name, grid, in_specs, out_specs, out_shape, scratch_shapes, args, prefetch=(), expose=False,
          aliases=None):
    seq = _params(("arbitrary",) * len(grid))
    np_ = len(prefetch)
    own = {np_ + i: j for i, j in (aliases or {}).items()}
    if phase is None:
        spec = pltpu.PrefetchScalarGridSpec(num_scalar_prefetch=np_, grid=grid, in_specs=in_specs, out_specs=out_specs,
                                            scratch_shapes=scratch_shapes)
        res = pl.pallas_call(body, name=name, grid_spec=spec, out_shape=out_shape, input_output_aliases=own,
                             compiler_params=seq)(*prefetch, *args)
        return list(res), []
    ni, no, ns = len(in_specs), len(out_specs), len(scratch_shapes)
    pi, po = len(phase.arrays), len(phase.out_shapes)

    def hosted(*refs):
        cut = np.cumsum([np_, ni, pi, no, po, ns])
        pre, ins, pin, outs, pout, scr, sems = (refs[a:b] for a, b in zip([0, *cut], [*cut, len(refs)]))
        ids = [pl.program_id(d) for d in range(len(grid))]
        first = functools.reduce(lambda p, q: p & q, [i == 0 for i in ids])
        last = functools.reduce(lambda p, q: p & q, [i == g - 1 for i, g in zip(ids, grid)])
        pl.when(first)(lambda: phase.start(pin, pout, *sems))
        body(*pre, *ins, *outs, *scr, **({"carried": (pin, pout, sems)} if expose else {}))
        pl.when(last)(lambda: phase.finish(pin, pout, *sems))

    anyspace = pl.BlockSpec(memory_space=pl.ANY)
    spec = pltpu.PrefetchScalarGridSpec(
        num_scalar_prefetch=np_, grid=grid, in_specs=list(in_specs) + [anyspace] * pi,
        out_specs=list(out_specs) + [anyspace] * po, scratch_shapes=list(scratch_shapes) + phase.sems())
    res = pl.pallas_call(
        hosted, name=name, grid_spec=spec, out_shape=list(out_shape) + phase.out_shapes,
        input_output_aliases={**own, **{np_ + ni + i: no + j for i, j in phase.aliases.items()}}, compiler_params=seq,
    )(*prefetch, *args, *phase.arrays)
    return list(res[:no]), list(res[no:])


def _run_phases(name, phases):
    first = phases[0]
    pi, po = len(first.arrays), len(first.out_shapes)

    def body(*refs):
        pin, pout, sems = refs[:pi], refs[pi:pi + po], refs[pi + po:]
        for n, ph in enumerate(phases):
            ph.start(pin, pout, *sems[2 * n:2 * n + 2])
            ph.finish(pin, pout, *sems[2 * n:2 * n + 2])

    anyspace = pl.BlockSpec(memory_space=pl.ANY)
    return list(pl.pallas_call(
        body, name=name, in_specs=[anyspace] * pi, out_specs=[anyspace] * po, out_shape=first.out_shapes,
        input_output_aliases=first.aliases, scratch_shapes=[s for ph in phases for s in ph.sems()],
    )(*first.arrays))


def _half_rows(buf, c):
    half = buf.shape[1] // 2
    return pl.ds(c * half, half), pl.ds((1 - c) * half, half)


def _gather_phase(bufs, over_ici):
    n = len(bufs)
    shapes = [jax.ShapeDtypeStruct(b.shape, b.dtype) for b in bufs]

    def landed(out, which):
        x, y, c = _place()
        _, ks = _other_chips(x, y)
        return [out[a].at[ks[j], _half_rows(bufs[a], c)[which]] for a in range(n) for j in range(3)]

    def ici(pin, out):
        x, y, c = _place()
        chips, _ = _other_chips(x, y)
        mine = [out[a].at[2 * x + y, _half_rows(bufs[a], c)[0]] for a in range(n)]
        return [(mine[a], mine[a], (*chips[j], c)) for a in range(n) for j in range(3)]

    def d2d(pin, out):
        x, y, c = _place()
        return [(dst, dst, (x, y, 1 - c)) for dst in landed(out, 0)]

    if over_ici:
        return _Phase(bufs, shapes, {a: a for a in range(n)}, 3 * n, ici, lambda pin, out: landed(out, 0))
    return _Phase(bufs, shapes, {a: a for a in range(n)}, 3 * n, d2d, lambda pin, out: landed(out, 1))


def _feed_phase(buf):
    def chips():
        x, y, c = _place()
        return [(x if f < 2 else 1 - x, y if f % 2 == 0 else 1 - y) for f in (1, 2, 3)]

    def copies(pin, out):
        x, y, c = _place()
        mine = _half_rows(buf, c)[0]
        own = out[0].at[2 * x + y, mine]
        sent = [(own, own, (cx, cy, c)) for cx, cy in chips()]
        return sent + [(out[0].at[2 * cx + cy, mine], out[0].at[2 * cx + cy, mine], (x, y, 1 - c)) for cx, cy in chips()]

    def arrivals(pin, out):
        x, y, c = _place()
        mine, theirs = _half_rows(buf, c)
        return [out[0].at[2 * cx + cy, rows] for rows in (mine, theirs) for cx, cy in chips()]

    return _Phase([buf], [jax.ShapeDtypeStruct(buf.shape, buf.dtype)], {0: 0}, 6, copies, arrivals,
                  own_starts=(3, 4, 5), own_waits=range(6))


def _rs_swap_phase(grads):
    n = len(grads)

    def copies(g, out):
        x, y, c = _place()
        return [(g[a].at[:, _half_rows(grads[a], c)[1]], out[a], (x, y, 1 - c)) for a in range(n)]

    shapes = [jax.ShapeDtypeStruct((N_CHIPS, g.shape[1] // 2, g.shape[2]), g.dtype) for g in grads]
    return _Phase(grads, shapes, {}, n, copies, lambda g, out: list(out))


def _rs_add_sibling(place, grads, gots, name):
    n = len(grads)
    _, R, C = grads[0].shape
    half = R // 2
    tr = _tile(half, max(16, (1 << 19) // C // 16 * 16), 16)
    nr = half // tr

    def body(p_ref, *refs):
        for a in range(n):
            refs[2 * n + a][...] = (refs[2 * a][...].astype(f32) + refs[2 * a + 1][...].astype(f32)).astype(bf16)

    res = pl.pallas_call(
        body, name=name,
        grid_spec=pltpu.PrefetchScalarGridSpec(
            num_scalar_prefetch=1, grid=(N_CHIPS, nr),
            in_specs=[pl.BlockSpec((1, tr, C), lambda k, r, p: (k, p[0] * nr + r, 0)),
                      pl.BlockSpec((1, tr, C), lambda k, r, p: (k, r, 0))] * n,
            out_specs=[pl.BlockSpec((1, tr, C), lambda k, r, p: (k, r, 0))] * n),
        out_shape=[jax.ShapeDtypeStruct((N_CHIPS, half, C), bf16)] * n,
        compiler_params=_params(("parallel", "parallel")),
    )(place, *[x for pair in zip(grads, gots) for x in pair])
    return list(res)


def _rs_chips_phase(parts):
    n = len(parts)

    def copies(p, fc):
        x, y, c = _place()
        chips, ks = _other_chips(x, y)
        return [(p[a].at[ks[j]], fc[a].at[j], (*chips[j], c)) for a in range(n) for j in range(3)]

    shapes = [jax.ShapeDtypeStruct((3,) + q.shape[1:], q.dtype) for q in parts]
    return _Phase(parts, shapes, {}, 3 * n, copies, lambda p, fc: [fc[a].at[j] for a in range(n) for j in range(3)])


def _rs_hand_phase(parts, from_chips):
    n = len(parts)

    def copies(pin, fs):
        x, y, c = _place()
        sib = (x, y, 1 - c)
        own = [(pin[a].at[2 * x + y], fs[a].at[0], sib) for a in range(n)]
        return own + [(pin[n + a].at[j], fs[a].at[1 + j], sib) for a in range(n) for j in range(3)]

    def arrivals(pin, fs):
        return [fs[a].at[0] for a in range(n)] + [fs[a].at[1 + j] for a in range(n) for j in range(3)]

    shapes = [jax.ShapeDtypeStruct((4,) + q.shape[1:], q.dtype) for q in parts]
    return _Phase(list(parts) + list(from_chips), shapes, {}, 4 * n, copies, arrivals)


class _Exchange:
    def __init__(self, place):
        self.place = place

    def feed(self, buf):
        return _feed_phase(buf)

    def gather(self, bufs, over_ici):
        return _gather_phase(bufs, over_ici)

    def swap(self, grads):
        return _rs_swap_phase(grads)

    def pair_sums(self, names, grads):
        return self.add(names, grads, _run_phases("rs_sibling_" + names[0], [_rs_swap_phase(grads)]))

    def add(self, names, grads, got):
        parts = {}
        for group in _same_shape(grads):
            res = _rs_add_sibling(self.place, [grads[i] for i in group], [got[i] for i in group],
                                  "rs_add_" + names[group[0]])
            parts.update(zip(group, res))
        return [parts[i] for i in range(len(names))]

    def to_chips(self, parts):
        return _rs_chips_phase(parts)

    def to_sibling(self, parts, from_chips):
        return _rs_hand_phase(parts, from_chips)

    def spread(self, blk):
        return _spread_phase(blk)

    def hand_over(self, name, parts, from_chips, blk):
        got = _run_phases(name, [_join([_rs_hand_phase(parts, from_chips), _spread_phase(blk)])])
        return got[:-1], got[-1]


def _local_step(place, x, target, norm_mix, b_gate, rb_chip, norm_ffn, norm_final, w_in, rest, exch):
    B, S, D = x.shape
    T = B * S
    x2 = x.reshape(T, D)
    tg2 = target.reshape(T, D)
    rope, decay = _rope_tables(S), _decay_tables()
    g_fin = norm_final.reshape(1, D)
    nrel = rb_chip.shape[-1]

    mrg, ffn = ["w_ret_out", "w_att_out", "w_out"], ["w_ffn_gate", "w_ffn_up", "w_ffn_down"]
    (xn, proj), got = _in_proj(place, x2, norm_mix, _join([exch.feed(w_in), exch.gather([rest[n] for n in mrg], True),
                                                           exch.spread(jnp.pad(rb_chip, ((0, 0), (0, 128 - nrel))))]))
    w_in, wb, rb_all = got[0], {}, got.pop()
    trows = _bias_rows(jnp.concatenate([rb_all[2 * k, :, :nrel] for k in range(N_CHIPS)], axis=1))
    (qr, kr, o, u, states), got = _ret_fwd(proj, B, S, rope, decay, _join([exch.gather([rest["w_ffn_gate"]], True),
                                                                         exch.gather(got[1:], False)]))
    wb.update(zip(mrg, got[1:]))
    (ao,), got = _att_fwd(proj, trows, B, S, _join([exch.gather([rest["w_ffn_up"], rest["w_ffn_down"]], True),
                                                    exch.gather(got[:1], False)]))
    wb["w_ffn_gate"] = got[2]
    w_ro, w_out = wb["w_ret_out"].reshape(-1, D), wb["w_out"].reshape(-1, D)
    (h1, yr, ya), got = _mix_fwd(x2, proj, u, ao, b_gate, w_ro, wb["w_att_out"], w_out, exch.gather(got[:2], False))
    wb.update(zip(ffn[1:], got))
    hn, a, b, f, dh2, part_fin = _ffn_fwd(h1, norm_ffn, wb["w_ffn_gate"], wb["w_ffn_up"], wb["w_ffn_down"], g_fin, tg2)

    da, db, dh1, part_ffn = _ffn_bwd(dh2, h1, norm_ffn, a, b, wb["w_ffn_gate"], wb["w_ffn_up"], wb["w_ffn_down"])
    ffn = ["w_ffn_down", "w_ffn_gate", "w_ffn_up"]
    g_ffn = [_wgrad(f, dh2, 0, "wgrad_ffn_down")[0], _wgrad(da, hn, 0, "wgrad_ffn_gate")[0],
             _wgrad(db, hn, 0, "wgrad_ffn_up")[0]]
    (du, dao, dgl, mix, dyr, dya, part_bg), x_ffn = _mix_bwd(dh1, proj, yr, ya, b_gate, w_ro, wb["w_att_out"], w_out,
                                                             exch.swap(g_ffn))
    p_ffn = exch.add(ffn, g_ffn, x_ffn)
    mrg = ["w_out", "w_ret_out", "w_att_out"]
    g_mrg = [_wgrad(mix, dh1, 0, "wgrad_out")[0], _wgrad(u, dyr, 0, "wgrad_ret_out")[0],
             _wgrad(ao, dya, 1, "wgrad_att_out")[0]]
    (dproj,), got = _ret_bwd(proj, qr, kr, o, states, du, B, S, rope, decay, _join([exch.to_chips(p_ffn[:2]),
                                                                                     exch.swap(g_mrg)]))
    c_two, p_mrg = got[:2], exch.add(mrg, g_mrg, got[2:])
    (dproj, dvec), got = _att_bwd(proj, dao, trows, dproj, B, S, exch.to_chips(p_ffn[2:] + p_mrg))
    c_ffn, c_mrg = c_two + got[:1], got[1:]
    dproj = lax.dynamic_update_slice(dproj, dgl, (0, C_GL))
    g_in, got = _wgrad(xn, dproj, 1, "wgrad_in", exch.to_sibling(p_ffn + p_mrg, c_ffn + c_mrg))
    s_ffn, s_mrg = got[:len(ffn)], got[len(ffn):]
    p_in = exch.pair_sums(["w_in"], [g_in])
    (gx, part_mix), c_in = _in_proj_bwd(dproj, w_in, x2, norm_mix, dh1, exch.to_chips(p_in))
    rows = lambda p, r: p.reshape(-1, 8, p.shape[-1])[:, r, :].sum(axis=0)
    lo = KWIN - 1 - (MAX_REL - 1)
    drb = jnp.concatenate([jnp.flip(dvec[:, lo:lo + N_REL - 1], axis=1), dvec[:, :lo].sum(axis=1, keepdims=True)], axis=1)
    gsmall = {
        "norm_mix": rows(part_mix, 0), "b_gate": rows(part_bg, 0), "rel_bias": drb, "norm_ffn": rows(part_ffn, 0),
        "norm_final": rows(part_fin, 0),
    }
    s_in, small_all = exch.hand_over("rs_hand_w_in", p_in, c_in, _pack_small(gsmall, rows(part_fin, 1)))
    gbig = dict(zip(ffn + mrg + ["w_in"], zip(p_ffn + p_mrg + p_in, c_ffn + c_mrg + c_in, s_ffn + s_mrg + s_in)))
    return gx.reshape(B, S, D), gbig, small_all


SMALL_ROWS = 16


def _pack_small(gs, loss_lanes):
    D = D_MODEL
    rb = jnp.pad(gs["rel_bias"].reshape(-1), (0, 3 * D - ATT_HEADS * N_REL)).reshape(3, D)
    rows = [gs["norm_mix"].reshape(1, D), gs["b_gate"].reshape(2, D), gs["norm_ffn"].reshape(1, D),
            gs["norm_final"].reshape(1, D), rb, loss_lanes.reshape(1, D)]
    used = sum(r.shape[0] for r in rows)
    return jnp.concatenate(rows + [jnp.zeros((SMALL_ROWS - used, D), f32)], axis=0)


def kernel(x, norm_mix, w_in, b_gate, rel_bias, w_ret_out, w_att_out, w_out, norm_ffn, w_ffn_gate, w_ffn_up, w_ffn_down, norm_final, loss_target, m_norm_mix, m_w_in, m_b_gate, m_rel_bias, m_w_ret_out, m_w_att_out, m_w_out, m_norm_ffn, m_w_ffn_gate, m_w_ffn_up, m_w_ffn_down, m_norm_final, v_norm_mix, v_w_in, v_b_gate, v_rel_bias, v_w_ret_out, v_w_att_out, v_w_out, v_norm_ffn, v_w_ffn_gate, v_w_ffn_up, v_w_ffn_down, v_norm_final):
    w = dict(norm_mix=norm_mix, w_in=w_in, b_gate=b_gate, rel_bias=rel_bias, w_ret_out=w_ret_out, w_att_out=w_att_out,
             w_out=w_out, norm_ffn=norm_ffn, w_ffn_gate=w_ffn_gate, w_ffn_up=w_ffn_up, w_ffn_down=w_ffn_down,
             norm_final=norm_final)
    m = dict(norm_mix=m_norm_mix, w_in=m_w_in, b_gate=m_b_gate, rel_bias=m_rel_bias, w_ret_out=m_w_ret_out,
             w_att_out=m_w_att_out, w_out=m_w_out, norm_ffn=m_norm_ffn, w_ffn_gate=m_w_ffn_gate, w_ffn_up=m_w_ffn_up,
             w_ffn_down=m_w_ffn_down, norm_final=m_norm_final)
    v = dict(norm_mix=v_norm_mix, w_in=v_w_in, b_gate=v_b_gate, rel_bias=v_rel_bias, w_ret_out=v_w_ret_out,
             w_att_out=v_w_att_out, w_out=v_w_out, norm_ffn=v_norm_ffn, w_ffn_gate=v_w_ffn_gate, w_ffn_up=v_w_ffn_up,
             w_ffn_down=v_w_ffn_down, norm_final=v_norm_final)
    xi, yi, ci = _place()
    k_me = 2 * xi + yi

    place = jnp.stack([ci, k_me]).astype(jnp.int32)
    big = [n for n, _ in BIG]

    turned = ("w_ffn_gate", "w_ffn_up")
    shard = lambda d, n: jnp.swapaxes(d[n][0], 0, 1) if n in turned else d[n][0]
    whole = lambda a, n: (jnp.swapaxes(a, 0, 1) if n in turned else a)[None]

    by_shape = [[big[i] for i in group] for group in _same_shape([shard(w, n) for n in big])]
    bufs = {}
    for names in by_shape:
        bufs.update(zip(names, _cast_shards(place, [shard(w, n) for n in names], "cast_" + names[0])))
    rest = {n: bufs[n] for n in big if n != "w_in"}
    nrel_loc = rel_bias.shape[-1]
    grad_x, gbig, small_all = _local_step(place, x, loss_target, norm_mix, b_gate, rel_bias[0], norm_ffn, norm_final,
                                          bufs["w_in"], rest, _Exchange(place))

    small = _sum_slots(small_all, "reduce_small")
    D = D_MODEL
    loss = jnp.sum(small[8])
    drb_full = small[5:8].reshape(-1)[:ATT_HEADS * N_REL].reshape(ATT_HEADS, N_REL)
    g = {
        "norm_mix": small[0:1], "b_gate": small[1:3].reshape(1, 2 * D), "norm_ffn": small[3:4], "norm_final": small[4],
        "rel_bias": lax.dynamic_slice_in_dim(drb_full, k_me * nrel_loc, nrel_loc, axis=1)[None],
    }

    delta, new_m, new_v = {}, {}, {}
    for names in by_shape:
        res = _adamw_sum(place, [(shard(w, n), shard(m, n), shard(v, n), *gbig[n]) for n in names], "adamw_" + names[0])
        for n, (g_, d_, m_, v_) in zip(names, res):
            g[n], delta[n], new_m[n], new_v[n] = whole(g_, n), whole(d_, n), whole(m_, n), whole(v_, n)
    flat = lambda d: jnp.concatenate([d[n].reshape(-1) for n in SMALL])
    n_small = sum(int(np.prod(w[n].shape)) for n in SMALL)
    n_pad = -n_small % 1024
    packs = [jnp.pad(flat(d), (0, n_pad)).reshape(-1, 128) for d in (w, g, m, v)]
    outs = _adamw(*packs, "adamw_small")
    for res, dst in zip(outs, (delta, new_m, new_v)):
        off = 0
        fl = res.reshape(-1)
        for n in SMALL:
            sz = int(np.prod(w[n].shape))
            dst[n] = fl[off:off + sz].reshape(w[n].shape)
            off += sz

    return (loss, grad_x, *[g[n] for n in WEIGHTS], *[delta[n] for n in WEIGHTS], *[new_m[n] for n in WEIGHTS],
            *[new_v[n] for n in WEIGHTS])
```

```python
import functools

import numpy as np
import jax
import jax.numpy as jnp
from jax import lax
from jax.experimental import pallas as pl
from jax.experimental.pallas import tpu as pltpu

f32 = jnp.float32
bf16 = jnp.bfloat16

D_MODEL = 1024
CHUNK = 64
RET_HEADS = 4
RET_KEY_DIM = 128
RET_VAL_DIM = 256
ATT_HEADS = 8
ATT_HEAD_DIM = 64
ATT_W = ATT_HEADS * ATT_HEAD_DIM
BAND_CHUNKS = 8
PAD = BAND_CHUNKS * CHUNK
MAX_REL = 256
N_REL = CHUNK + MAX_REL
D_FF = 2816
N_IN = 6656
ROPE_BASE = 10000.0
EPS = 1e-6
NEG_INF = -1e30
C_RQ, C_RK, C_RV, C_RG, C_AQ, C_AK, C_AV, C_GL = 0, 512, 1024, 2048, 3072, 3584, 4096, 4608

ADAM_LR, ADAM_B1, ADAM_B2, ADAM_EPS, ADAM_WD, ADAM_STEP = 0.001, 0.9, 0.999, 1e-08, 0.01, 10

N_CHIPS = 4
N_DEV = 8
WGRAD_ACC_BYTES = 8 * 1024 * 1024
WGRAD_VMEM_BYTES = 40 * 1024 * 1024
ROW_TILE = 512
BIG_ROW_TILE = 1024
IN_ORDER = (0, 2, 3, 1)
QBLK = 256
KWIN = PAD + QBLK
TOEP = 1024
VMEM_LIMIT = 56 * 1024 * 1024
MESH = pl.DeviceIdType.MESH

BIG = (
    ("w_in", 1), ("w_ret_out", 0), ("w_att_out", 1), ("w_out", 0), ("w_ffn_gate", 1), ("w_ffn_up", 1), ("w_ffn_down", 0))
WEIGHTS = ("norm_mix", "w_in", "b_gate", "rel_bias", "w_ret_out", "w_att_out", "w_out", "norm_ffn", "w_ffn_gate",
           "w_ffn_up", "w_ffn_down", "norm_final")
SMALL = ("norm_mix", "b_gate", "rel_bias", "norm_ffn", "norm_final")


def _dot(a, b):
    return lax.dot_general(a, b, (((1,), (0,)), ((), ())), preferred_element_type=f32)


def _dot_nt(a, b):
    return lax.dot_general(a, b, (((1,), (1,)), ((), ())), preferred_element_type=f32)


def _dot_tn(a, b):
    return lax.dot_general(a, b, (((0,), (0,)), ((), ())), preferred_element_type=f32)


def _sig(x):
    return 1.0 / (1.0 + jnp.exp(-x))


def _tile(n, pref, mult):
    best = None
    for t in range(mult, min(n, pref) + 1, mult):
        if n % t == 0:
            best = t
    return best if best is not None else n


def _same_shape(arrays):
    groups = {}
    for i, a in enumerate(arrays):
        groups.setdefault(a.shape, []).append(i)
    return list(groups.values())


def _params(sem, vmem=VMEM_LIMIT):
    return pltpu.CompilerParams(dimension_semantics=sem, vmem_limit_bytes=vmem)


def _in_proj(place, x2, gamma, phase):
    T, D = x2.shape
    _, _, ns = phase.arrays[0].shape
    tm = _tile(T, BIG_ROW_TILE, 8)
    ni = T // tm
    pass_chip = lambda j: sum(jnp.where(j == n, f, 0) for n, f in enumerate(IN_ORDER))

    def body(p_ref, x_ref, g_ref, xn_ref, pr_ref, xs_ref, w_ref, w_sem, carried):
        j, i = pl.program_id(0), pl.program_id(1)
        pin, pout, sems = carried
        rows = pl.ds(pl.multiple_of(i * tm, tm), tm)

        @pl.when(i == 0)
        def _():
            for n, f in enumerate(IN_ORDER):
                if f:
                    @pl.when(j == n)
                    def _():
                        phase.arrived(f - 1, pin, pout, *sems)
                        phase.begin(2 + f, pin, pout, *sems)
                        phase.arrived(2 + f, pin, pout, *sems)
            shard = pltpu.make_async_copy(pout[0].at[jnp.bitwise_xor(p_ref[1], pass_chip(j))], w_ref, w_sem)
            shard.start()
            shard.wait()

        @pl.when(j == 0)
        def _():
            x = x_ref[...]
            r = lax.rsqrt(jnp.mean(x * x, axis=-1, keepdims=True) + EPS)
            xn = (x * r * g_ref[...]).astype(bf16)
            xs_ref[rows, :] = xn
            xn_ref[...] = xn

        pr_ref[...] = _dot(xs_ref[rows, :], w_ref[...]).astype(bf16)

    first_pass = lambda j, i, p: (jnp.where(j == 0, i, ni - 1), 0)
    return _call(
        body, phase, name="in_proj", grid=(N_CHIPS, ni), prefetch=(place,), expose=True,
        in_specs=[pl.BlockSpec((tm, D), first_pass), pl.BlockSpec((1, D), lambda j, i, p: (0, 0))],
        out_specs=[pl.BlockSpec((tm, D), first_pass),
                   pl.BlockSpec((tm, ns), lambda j, i, p: (i, jnp.bitwise_xor(p[1], pass_chip(j))))],
        out_shape=[jax.ShapeDtypeStruct((T, D), bf16), jax.ShapeDtypeStruct((T, N_CHIPS * ns), bf16)],
        scratch_shapes=[pltpu.VMEM((T, D), bf16), pltpu.VMEM((D, ns), bf16), pltpu.SemaphoreType.DMA],
        args=(x2, gamma))


def _rope_tables(S):
    d = RET_KEY_DIM
    freqs = ROPE_BASE ** (-jnp.arange(0, d, 2, dtype=f32) / d)
    ang = jnp.arange(S, dtype=f32)[:, None] * freqs[None, :]
    cos, sin = jnp.cos(ang), jnp.sin(ang)
    return jnp.concatenate([cos, cos], axis=1), jnp.concatenate([-sin, sin], axis=1)


def _decay_tables():
    H = RET_HEADS
    log_g = jnp.log(1.0 - 2.0 ** (-5.0 - jnp.arange(H, dtype=f32)))
    p = jnp.arange(CHUNK, dtype=f32)
    intra = jnp.exp(log_g[:, None, None] * jnp.abs(p[:, None] - p[None, :]))
    q_dec = jnp.exp(log_g[:, None] * (p[None, :] + 1.0))
    k_dec = jnp.exp(log_g[:, None] * (CHUNK - 1.0 - p[None, :]))
    c_dec = jnp.exp(log_g * CHUNK)
    q_dec = jnp.broadcast_to(q_dec[:, :, None], (H, CHUNK, RET_KEY_DIM))
    k_dec = jnp.broadcast_to(k_dec[:, :, None], (H, CHUNK, RET_KEY_DIM))
    c_dec = jnp.broadcast_to(c_dec[:, None, None], (H, 1, RET_VAL_DIM))
    return intra, q_dec, k_dec, c_dec


K_SCALE = RET_KEY_DIM ** -0.5


RET_CHUNKS = 4


def _ret_tables_specs():
    whole = lambda *shape: pl.BlockSpec(shape, lambda b, i: (0,) * len(shape))
    return [whole(RET_HEADS, CHUNK, CHUNK), whole(RET_HEADS, CHUNK, RET_KEY_DIM), whole(RET_HEADS, CHUNK, RET_KEY_DIM),
            whole(RET_HEADS, 1, RET_VAL_DIM)]


def _rotate(x, cos, sn):
    return x * cos + pltpu.roll(x, RET_KEY_DIM // 2, 1) * sn


def _ret_fwd(proj, B, S, rope, decay, phase=None):
    T = B * S
    nc = S // CHUNK
    H, dk, dv = RET_HEADS, RET_KEY_DIM, RET_VAL_DIM
    sb = RET_CHUNKS * CHUNK
    ns = S // sb

    def body(q_ref, k_ref, v_ref, g_ref, cos_ref, sin_ref, intra_ref, qd_ref, kd_ref, cd_ref,
             qr_ref, kr_ref, o_ref, u_ref, st_ref, state_ref):
        @pl.when(pl.program_id(1) == 0)
        def _():
            state_ref[...] = jnp.zeros_like(state_ref)

        cos, sn = cos_ref[...], sin_ref[...]
        for h in range(H):
            hs = slice(h * dk, (h + 1) * dk)
            qr_ref[:, hs] = _rotate(q_ref[:, hs].astype(f32), cos, sn).astype(bf16)
            kr_ref[:, hs] = (_rotate(k_ref[:, hs].astype(f32), cos, sn) * K_SCALE).astype(bf16)
        states = [state_ref[h] for h in range(H)]
        for ci in range(RET_CHUNKS):
            r = slice(ci * CHUNK, (ci + 1) * CHUNK)
            for h in range(H):
                hk, hv = slice(h * dk, (h + 1) * dk), slice(h * dv, (h + 1) * dv)
                qi, ki, vi = qr_ref[r, hk], kr_ref[r, hk], v_ref[r, hv]
                stb = states[h].astype(bf16)
                st_ref[0, h, ci] = stb
                s = (_dot_nt(qi, ki) * intra_ref[h]).astype(bf16)
                o = _dot(s, vi) + _dot((qi.astype(f32) * qd_ref[h]).astype(bf16), stb)
                states[h] = states[h] * cd_ref[h] + _dot_tn((ki.astype(f32) * kd_ref[h]).astype(bf16), vi)
                mu = jnp.mean(o, axis=-1, keepdims=True)
                xc = o - mu
                var = jnp.mean(xc * xc, axis=-1, keepdims=True)
                oh = xc * lax.rsqrt(var + EPS)
                g = g_ref[r, hv].astype(f32)
                o_ref[r, hv] = o.astype(bf16)
                u_ref[r, hv] = (g * _sig(g) * oh).astype(bf16)
        for h in range(H):
            state_ref[h] = states[h]

    blk = lambda w, c: pl.BlockSpec((sb, w), lambda b, i: (b * ns + i, c))
    return _call(
        body, phase, name="ret_fwd", grid=(B, ns), scratch_shapes=[pltpu.VMEM((H, dk, dv), f32)],
        in_specs=[blk(H * dk, C_RQ // (H * dk)), blk(H * dk, C_RK // (H * dk)), blk(H * dv, C_RV // (H * dv)),
                  blk(H * dv, C_RG // (H * dv)),
                  pl.BlockSpec((sb, dk), lambda b, i: (i, 0)), pl.BlockSpec((sb, dk), lambda b, i: (i, 0)),
                  *_ret_tables_specs()],
        out_specs=[blk(H * dk, 0), blk(H * dk, 0), blk(H * dv, 0), blk(H * dv, 0),
                   pl.BlockSpec((1, H, RET_CHUNKS, dk, dv), lambda b, i: (b, 0, i, 0, 0))],
        out_shape=[jax.ShapeDtypeStruct((T, H * dk), bf16), jax.ShapeDtypeStruct((T, H * dk), bf16),
                   jax.ShapeDtypeStruct((T, H * dv), bf16), jax.ShapeDtypeStruct((T, H * dv), bf16),
                   jax.ShapeDtypeStruct((B, H, nc, dk, dv), bf16)],
        args=(proj, proj, proj, proj, *rope, *decay))


def _bias_rows(rb):
    last = rb[:, N_REL - 1:]
    return jnp.concatenate([
        jnp.broadcast_to(last, (ATT_HEADS, PAD - MAX_REL + 1)),
        jnp.flip(rb[:, :N_REL - 1], axis=1),
        jnp.broadcast_to(rb[:, :1], (ATT_HEADS, KWIN - PAD - CHUNK)),
        jnp.broadcast_to(last, (ATT_HEADS, TOEP - KWIN)),
    ], axis=1)


def _build_bias(t_ref, bias_ref):
    row = lax.broadcasted_iota(jnp.int32, (QBLK, KWIN), 0) // CHUNK
    col = lax.broadcasted_iota(jnp.int32, (QBLK, KWIN), 1) // CHUNK
    delta = BAND_CHUNKS + row - col
    vis = (delta >= 0) & (delta <= BAND_CHUNKS)
    for h in range(ATT_HEADS):
        t = jnp.broadcast_to(t_ref[h:h + 1, :], (QBLK, TOEP))
        rolled = pltpu.roll(t, 0, 1, stride=1, stride_axis=0)
        bias_ref[h] = jnp.where(vis, rolled[:, :KWIN], NEG_INF)


ATT_SCALE = ATT_HEAD_DIM ** -0.5


def _att_probs(qh, kh, bias):
    s = _dot_nt(qh, kh) + bias
    m = jnp.max(s, axis=-1, keepdims=True)
    p = jnp.exp(s - m)
    return p * (1.0 / jnp.sum(p, axis=-1, keepdims=True))


def _first_of_pair():
    return lax.broadcasted_iota(jnp.int32, (1, 2 * ATT_HEAD_DIM), 1) < ATT_HEAD_DIM


def _by_window(i, step):
    sizes = list(range(QBLK, KWIN, QBLK))
    for n, nk in enumerate(sizes):
        pl.when(i == n)(functools.partial(step, nk))
    pl.when(i >= len(sizes))(functools.partial(step, KWIN))


def _att_fwd(proj, trows, B, S, phase=None):
    T = B * S
    nq = S // QBLK
    dh = ATT_HEAD_DIM

    def body(q_ref, k_ref, v_ref, t_ref, o_ref, bias_ref):
        i = pl.program_id(1)

        @pl.when((pl.program_id(0) == 0) & (i == 0))
        def _():
            _build_bias(t_ref, bias_ref)

        def step(nk):
            win = pl.ds(pl.multiple_of((i + 1) * QBLK - nk, QBLK), nk)
            kw, vw = k_ref[win, :], v_ref[win, :]
            first = _first_of_pair()
            outs = []
            for p in range(ATT_HEADS // 2):
                ps = slice(2 * p * dh, 2 * (p + 1) * dh)
                q2, k2, v2 = q_ref[:, ps] * ATT_SCALE, kw[:, ps], vw[:, ps]
                both = []
                for e in range(2):
                    qm = jnp.where(first == (e == 0), q2, jnp.zeros_like(q2))
                    pr = _att_probs(qm, k2, bias_ref[2 * p + e, :, KWIN - nk:])
                    both.append(_dot(pr.astype(bf16), v2))
                outs.append(jnp.where(first, both[0], both[1]))
            o_ref[...] = jnp.concatenate(outs, axis=1).astype(bf16)

        _by_window(i, step)

    return _call(
        body, phase, name="att_fwd", grid=(B, nq),
        in_specs=[pl.BlockSpec((QBLK, ATT_W), lambda b, i: (b * nq + i, C_AQ // ATT_W)),
                  pl.BlockSpec((S, ATT_W), lambda b, i: (b, C_AK // ATT_W)),
                  pl.BlockSpec((S, ATT_W), lambda b, i: (b, C_AV // ATT_W)),
                  pl.BlockSpec((ATT_HEADS, TOEP), lambda b, i: (0, 0))],
        out_specs=[pl.BlockSpec((QBLK, ATT_W), lambda b, i: (b * nq + i, 0))],
        out_shape=[jax.ShapeDtypeStruct((T, ATT_W), bf16)],
        scratch_shapes=[pltpu.VMEM((ATT_HEADS, QBLK, KWIN), f32)],
        args=(proj, proj, proj, trows))


def _gl_specs(tm):
    w = 512
    return [pl.BlockSpec((tm, w), functools.partial(lambda i, j: (i, C_GL // 512 + j), j=j)) for j in range(4)]


def _gates(gl_refs, bg_ref):
    gl = jnp.concatenate([r[...] for r in gl_refs], axis=1).astype(f32) + bg_ref[...]
    g = _sig(gl)
    return g[:, :D_MODEL], g[:, D_MODEL:]


def _mix_fwd(x2, proj, u, ao, b_gate, w_ro, w_ao, w_out, phase=None):
    T, D = x2.shape
    tm = _tile(T, ROW_TILE, 8)

    def body(x_ref, u_ref, ao_ref, g0, g1, g2, g3, bg_ref, wro_ref, wao_ref, wo_ref, h1_ref, yr_ref, ya_ref):
        yr = _dot(u_ref[...], wro_ref[...])
        ao = ao_ref[...]
        ya = jnp.concatenate([_dot(ao, wao_ref[k]) for k in range(N_CHIPS)], axis=1)
        gr, ga = _gates((g0, g1, g2, g3), bg_ref)
        mix = gr * yr + ga * ya
        h1_ref[...] = x_ref[...] + _dot(mix.astype(bf16), wo_ref[...])
        yr_ref[...] = yr.astype(bf16)
        ya_ref[...] = ya.astype(bf16)

    full = lambda a: pl.BlockSpec(a.shape, lambda i: (0,) * a.ndim)
    row = lambda n: pl.BlockSpec((tm, n), lambda i: (i, 0))
    return _call(
        body, phase, name="mix_fwd", grid=(T // tm,), scratch_shapes=[],
        in_specs=[row(D), row(D), row(ATT_W), *_gl_specs(tm), full(b_gate), full(w_ro), full(w_ao), full(w_out)],
        out_specs=[row(D), row(D), row(D)],
        out_shape=[jax.ShapeDtypeStruct((T, D), f32), jax.ShapeDtypeStruct((T, D), bf16),
                   jax.ShapeDtypeStruct((T, D), bf16)],
        args=(x2, u, ao, proj, proj, proj, proj, b_gate, w_ro, w_ao, w_out))


def _ffn_fwd(h1, g_ffn, wg, wu, wd, g_fin, target):
    T, D = h1.shape
    nf, tf, _ = wg.shape
    tm = _tile(T, ROW_TILE, 8)

    def body(h1_ref, g_ref, wg_ref, wu_ref, wd_ref, gf_ref, tg_ref, hn_ref, a_ref, b_ref, f_ref, dh2_ref, part_ref):
        h1v = h1_ref[...]
        r = lax.rsqrt(jnp.mean(h1v * h1v, axis=-1, keepdims=True) + EPS)
        hn = (h1v * r * g_ref[...]).astype(bf16)
        hn_ref[...] = hn
        h2 = h1v
        for k in range(nf):
            a = _dot_nt(hn, wg_ref[k])
            b = _dot_nt(hn, wu_ref[k])
            f = ((a * _sig(a)) * b).astype(bf16)
            a_ref[k] = a.astype(bf16)
            b_ref[k] = b.astype(bf16)
            f_ref[k] = f
            h2 = h2 + _dot(f, wd_ref[k])
        r = lax.rsqrt(jnp.mean(h2 * h2, axis=-1, keepdims=True) + EPS)
        n = h2 * r
        gf = gf_ref[...]
        e = n * gf - tg_ref[...]
        dy = e * (1.0 / D)
        dn = dy * gf
        dh2_ref[...] = r * (dn - n * jnp.mean(dn * n, axis=-1, keepdims=True))
        part_ref[...] = jnp.zeros_like(part_ref)
        part_ref[0:1, :] = jnp.sum(dy * n, axis=0, keepdims=True)
        part_ref[1:2, :] = (0.5 / D) * jnp.sum(e * e, axis=0, keepdims=True)

    row = lambda n: pl.BlockSpec((tm, n), lambda i: (i, 0))
    vec = pl.BlockSpec((1, D), lambda i: (0, 0))
    col = pl.BlockSpec((nf, tm, tf), lambda i: (0, i, 0))
    held = lambda w: pl.BlockSpec(w.shape, lambda i: (0, 0, 0), pipeline_mode=pl.Buffered(1))
    act = jax.ShapeDtypeStruct((nf, T, tf), bf16)
    return pl.pallas_call(
        body, name="ffn_fwd", grid=(T // tm,),
        in_specs=[row(D), vec, held(wg), held(wu), held(wd), vec, row(D)],
        out_specs=[row(D), col, col, col, row(D), pl.BlockSpec((8, D), lambda i: (i, 0))],
        out_shape=[jax.ShapeDtypeStruct((T, D), bf16), act, act, act,
                   jax.ShapeDtypeStruct((T, D), f32), jax.ShapeDtypeStruct((T // tm * 8, D), f32)],
        compiler_params=_params(("parallel",)),
    )(h1, g_ffn, wg, wu, wd, g_fin, target)


def _ffn_bwd(dh2, h1, g_ffn, a, b, wg, wu, wd):
    T, D = h1.shape
    nf, tf, _ = wg.shape
    tm = _tile(T, ROW_TILE // 2, 8)

    def body(dh2_ref, h1_ref, g_ref, a_ref, b_ref, wg_ref, wu_ref, wd_ref, da_ref, db_ref, dh1_ref, part_ref):
        dh2v = dh2_ref[...]
        dh2b = dh2v.astype(bf16)
        dhn = jnp.zeros((tm, D), f32)
        for k in range(nf):
            df = _dot_nt(dh2b, wd_ref[k])
            av = a_ref[k].astype(f32)
            sg = _sig(av)
            db = (df * (av * sg)).astype(bf16)
            da = (df * b_ref[k].astype(f32) * (sg * (1.0 + av * (1.0 - sg)))).astype(bf16)
            da_ref[k] = da
            db_ref[k] = db
            dhn = dhn + _dot(da, wg_ref[k]) + _dot(db, wu_ref[k])
        h = h1_ref[...]
        r = lax.rsqrt(jnp.mean(h * h, axis=-1, keepdims=True) + EPS)
        n = h * r
        dn = dhn * g_ref[...]
        dh1_ref[...] = dh2v + r * (dn - n * jnp.mean(dn * n, axis=-1, keepdims=True))
        part_ref[...] = jnp.zeros_like(part_ref)
        part_ref[0:1, :] = jnp.sum(dhn * n, axis=0, keepdims=True)

    row = lambda n: pl.BlockSpec((tm, n), lambda i: (i, 0))
    col = pl.BlockSpec((nf, tm, tf), lambda i: (0, i, 0))
    held = lambda w: pl.BlockSpec(w.shape, lambda i: (0, 0, 0), pipeline_mode=pl.Buffered(1))
    act = jax.ShapeDtypeStruct((nf, T, tf), bf16)
    return pl.pallas_call(
        body, name="ffn_bwd", grid=(T // tm,),
        in_specs=[row(D), row(D), pl.BlockSpec((1, D), lambda i: (0, 0)), col, col, held(wg), held(wu), held(wd)],
        out_specs=[col, col, row(D), pl.BlockSpec((8, D), lambda i: (i, 0))],
        out_shape=[act, act, jax.ShapeDtypeStruct((T, D), f32), jax.ShapeDtypeStruct((T // tm * 8, D), f32)],
        compiler_params=_params(("parallel",)),
    )(dh2, h1, g_ffn, a, b, wg, wu, wd)


def _mix_bwd(dh1, proj, yr, ya, b_gate, w_ro, w_ao, w_out, phase=None):
    T, D = dh1.shape
    tm = _tile(T, ROW_TILE, 8)

    def body(dh1_ref, g0, g1, g2, g3, bg_ref, yr_ref, ya_ref, wro_ref, wao_ref, wo_ref,
             du_ref, dao_ref, dgl_ref, mix_ref, dyr_ref, dya_ref, part_ref):
        dmix = _dot_nt(dh1_ref[...].astype(bf16), wo_ref[...])
        gr, ga = _gates((g0, g1, g2, g3), bg_ref)
        yr = yr_ref[...].astype(f32)
        ya = ya_ref[...].astype(f32)
        dyr = (dmix * gr).astype(bf16)
        dya = (dmix * ga).astype(bf16)
        dgl = jnp.concatenate([dmix * yr * gr * (1.0 - gr), dmix * ya * ga * (1.0 - ga)], axis=1)
        du_ref[...] = _dot_nt(dyr, wro_ref[...]).astype(bf16)
        ns = wao_ref.shape[2]
        dao = _dot_nt(dya[:, :ns], wao_ref[0])
        for k in range(1, N_CHIPS):
            dao = dao + _dot_nt(dya[:, k * ns:(k + 1) * ns], wao_ref[k])
        dao_ref[...] = dao.astype(bf16)
        dgl_ref[...] = dgl.astype(bf16)
        mix_ref[...] = (gr * yr + ga * ya).astype(bf16)
        dyr_ref[...] = dyr
        dya_ref[...] = dya
        part_ref[...] = jnp.zeros_like(part_ref)
        part_ref[0:1, :] = jnp.sum(dgl, axis=0, keepdims=True)

    full = lambda a: pl.BlockSpec(a.shape, lambda i: (0,) * a.ndim)
    row = lambda n: pl.BlockSpec((tm, n), lambda i: (i, 0))
    return _call(
        body, phase, name="mix_bwd", grid=(T // tm,), scratch_shapes=[],
        in_specs=[row(D), *_gl_specs(tm), full(b_gate), row(D), row(D), full(w_ro), full(w_ao), full(w_out)],
        out_specs=[row(D), row(ATT_W), row(2 * D), row(D), row(D), row(D), pl.BlockSpec((8, 2 * D), lambda i: (i, 0))],
        out_shape=[jax.ShapeDtypeStruct((T, D), bf16), jax.ShapeDtypeStruct((T, ATT_W), bf16),
                   jax.ShapeDtypeStruct((T, 2 * D), bf16), jax.ShapeDtypeStruct((T, D), bf16),
                   jax.ShapeDtypeStruct((T, D), bf16), jax.ShapeDtypeStruct((T, D), bf16),
                   jax.ShapeDtypeStruct((T // tm * 8, 2 * D), f32)],
        args=(dh1, proj, proj, proj, proj, b_gate, yr, ya, w_ro, w_ao, w_out))


def _ret_bwd(proj, qr, kr, o, states, du, B, S, rope, decay, phase=None):
    T = B * S
    nc = S // CHUNK
    H, dk, dv = RET_HEADS, RET_KEY_DIM, RET_VAL_DIM

    sb = RET_CHUNKS * CHUNK
    ns = S // sb

    def body(qr_ref, kr_ref, v_ref, g_ref, o_ref, st_ref, du_ref, cos_ref, sin_ref, intra_ref, qd_ref, kd_ref, cd_ref,
             dp_ref, dstate_ref):
        dq_ref, dk_ref = dp_ref.at[:, pl.ds(C_RQ, H * dk)], dp_ref.at[:, pl.ds(C_RK, H * dk)]
        dv_ref, dg_ref = dp_ref.at[:, pl.ds(C_RV, H * dv)], dp_ref.at[:, pl.ds(C_RG, H * dv)]

        @pl.when(pl.program_id(1) == 0)
        def _():
            dstate_ref[...] = jnp.zeros_like(dstate_ref)

        cos, snb = cos_ref[...], -sin_ref[...]
        dstates = [dstate_ref[h] for h in range(H)]
        for ci in reversed(range(RET_CHUNKS)):
            r = slice(ci * CHUNK, (ci + 1) * CHUNK)
            for h in range(H):
                hk, hv = slice(h * dk, (h + 1) * dk), slice(h * dv, (h + 1) * dv)
                intra, qd, kd = intra_ref[h], qd_ref[h], kd_ref[h]
                qi, ki, vi = qr_ref[r, hk], kr_ref[r, hk], v_ref[r, hv]
                si = st_ref[0, h, ci]
                o = o_ref[r, hv].astype(f32)
                mu = jnp.mean(o, axis=-1, keepdims=True)
                xc = o - mu
                rstd = lax.rsqrt(jnp.mean(xc * xc, axis=-1, keepdims=True) + EPS)
                oh = xc * rstd
                g = g_ref[r, hv].astype(f32)
                sg = _sig(g)
                dui = du_ref[r, hv].astype(f32)
                dg_ref[r, hv] = (dui * oh * (sg * (1.0 + g * (1.0 - sg)))).astype(bf16)
                doh = dui * (g * sg)
                do = rstd * (doh - jnp.mean(doh, axis=-1, keepdims=True)
                             - oh * jnp.mean(doh * oh, axis=-1, keepdims=True))
                dob = do.astype(bf16)
                p = (_dot_nt(qi, ki) * intra).astype(bf16)
                dsb = dstates[h].astype(bf16)
                kt = (ki.astype(f32) * kd).astype(bf16)
                qt = (qi.astype(f32) * qd).astype(bf16)
                dv_ref[r, hv] = (_dot_tn(p, dob) + _dot(kt, dsb)).astype(bf16)
                da = (_dot_nt(dob, vi) * intra).astype(bf16)
                dq = _dot(da, ki) + _dot_nt(dob, si) * qd
                dkk = (_dot_tn(da, qi) + _dot_nt(vi, dsb) * kd) * K_SCALE
                dq_ref[r, hk] = _rotate(dq, cos[r], snb[r]).astype(bf16)
                dk_ref[r, hk] = _rotate(dkk, cos[r], snb[r]).astype(bf16)
                dstates[h] = dstates[h] * cd_ref[h] + _dot_tn(qt, dob)
        for h in range(H):
            dstate_ref[h] = dstates[h]

    blk = lambda w, c: pl.BlockSpec((sb, w), lambda b, i: (b * ns + ns - 1 - i, c))
    return _call(
        body, phase, name="ret_bwd", grid=(B, ns),
        in_specs=[blk(H * dk, 0), blk(H * dk, 0), blk(H * dv, C_RV // (H * dv)), blk(H * dv, C_RG // (H * dv)),
                  blk(H * dv, 0),
                  pl.BlockSpec((1, H, RET_CHUNKS, dk, dv), lambda b, i: (b, 0, ns - 1 - i, 0, 0)),
                  blk(H * dv, 0),
                  pl.BlockSpec((sb, dk), lambda b, i: (ns - 1 - i, 0)), pl.BlockSpec((sb, dk), lambda b, i: (ns - 1 - i, 0)),
                  *_ret_tables_specs()],
        out_specs=[blk(C_AQ, 0)], out_shape=[jax.ShapeDtypeStruct((T, N_IN), bf16)],
        scratch_shapes=[pltpu.VMEM((H, dk, dv), f32)],
        args=(qr, kr, proj, proj, o, states, du, *rope, *decay))


def _att_bwd(proj, dao, trows, dproj, B, S, phase=None):
    T = B * S
    nq = S // QBLK
    dh = ATT_HEAD_DIM
    scale = ATT_HEAD_DIM ** -0.5

    def body(q_ref, k_ref, v_ref, do_ref, t_ref, _, dp_ref, vec_ref, bias_ref, dbias_ref, dka_ref, dva_ref):
        b, i = pl.program_id(0), pl.program_id(1)

        @pl.when((b == 0) & (i == 0))
        def _():
            _build_bias(t_ref, bias_ref)
            dbias_ref[...] = jnp.zeros_like(dbias_ref)

        @pl.when(i == 0)
        def _():
            dka_ref[...] = jnp.zeros_like(dka_ref)
            dva_ref[...] = jnp.zeros_like(dva_ref)

        def step(nk):
            win = pl.ds(pl.multiple_of((i + 1) * QBLK - nk, QBLK), nk)
            kw, vw = k_ref[win, :], v_ref[win, :]
            first = _first_of_pair()
            first_rows = lax.broadcasted_iota(jnp.int32, (2 * dh, 1), 0) < dh
            dqs, dks, dvs = [], [], []
            for p in range(ATT_HEADS // 2):
                ps = slice(2 * p * dh, 2 * (p + 1) * dh)
                q2, k2, v2, do2 = q_ref[:, ps] * ATT_SCALE, kw[:, ps], vw[:, ps], do_ref[:, ps]
                dq2, dk2, dv2 = [], [], []
                for e in range(2):
                    h = 2 * p + e
                    mine = first == (e == 0)
                    pr = _att_probs(jnp.where(mine, q2, jnp.zeros_like(q2)), k2, bias_ref[h, :, KWIN - nk:])
                    dp = _dot_nt(jnp.where(mine, do2, jnp.zeros_like(do2)), v2)
                    ds = pr * (dp - jnp.sum(pr * dp, axis=-1, keepdims=True))
                    dbias_ref[h, :, KWIN - nk:] += ds
                    dsb = ds.astype(bf16)
                    dq2.append(_dot(dsb, k2) * ATT_SCALE)
                    dk2.append(_dot_tn(q2, dsb))
                    dv2.append(_dot_tn(do2, pr.astype(bf16)))
                dqs.append(jnp.where(first, dq2[0], dq2[1]))
                dks.append(jnp.where(first_rows, dk2[0], dk2[1]))
                dvs.append(jnp.where(first_rows, dv2[0], dv2[1]))
            dp_ref[pl.ds(pl.multiple_of(i * QBLK, QBLK), QBLK), :ATT_W] = jnp.concatenate(dqs, axis=1).astype(bf16)
            dka_ref[:, win] += jnp.concatenate(dks, axis=0)
            dva_ref[:, win] += jnp.concatenate(dvs, axis=0)

        _by_window(i, step)

        @pl.when(i == nq - 1)
        def _():
            dp_ref[:, ATT_W:2 * ATT_W] = dka_ref[...].T.astype(bf16)
            dp_ref[:, 2 * ATT_W:] = dva_ref[...].T.astype(bf16)

        @pl.when((b == B - 1) & (i == nq - 1))
        def _():
            rr = lax.broadcasted_iota(jnp.int32, (QBLK, QBLK), 0)
            cc = lax.broadcasted_iota(jnp.int32, (QBLK, QBLK), 1)
            flip = jnp.where(rr + cc == QBLK - 1, 1.0, 0.0).astype(bf16)
            for h in range(ATT_HEADS):
                d = dbias_ref[h]
                hi = d.astype(bf16)
                lo = (d - hi.astype(f32)).astype(bf16)
                rev = _dot(flip, hi) + _dot(flip, lo)
                wide = jnp.concatenate([rev, jnp.zeros((QBLK, TOEP - KWIN), f32)], axis=1)
                rolled = pltpu.roll(wide, 0, 1, stride=1, stride_axis=0)
                vec_ref[h:h + 1, :] = jnp.sum(rolled, axis=0, keepdims=True)

    qspec = lambda c: pl.BlockSpec((QBLK, ATT_W), lambda b, i: (b * nq + i, c))
    kspec = lambda c: pl.BlockSpec((S, ATT_W), lambda b, i: (b, c))
    return _call(
        body, phase, name="att_bwd", grid=(B, nq), aliases={5: 0},
        in_specs=[qspec(C_AQ // ATT_W), kspec(C_AK // ATT_W), kspec(C_AV // ATT_W), qspec(0),
                  pl.BlockSpec((ATT_HEADS, TOEP), lambda b, i: (0, 0)), pl.BlockSpec(memory_space=pl.ANY)],
        out_specs=[pl.BlockSpec((S, 3 * ATT_W), lambda b, i: (b, C_AQ // (3 * ATT_W))),
                   pl.BlockSpec((ATT_HEADS, TOEP), lambda b, i: (0, 0))],
        out_shape=[jax.ShapeDtypeStruct((T, N_IN), bf16), jax.ShapeDtypeStruct((ATT_HEADS, TOEP), f32)],
        scratch_shapes=[pltpu.VMEM((ATT_HEADS, QBLK, KWIN), f32), pltpu.VMEM((ATT_HEADS, QBLK, KWIN), f32),
                        pltpu.VMEM((ATT_W, S), f32), pltpu.VMEM((ATT_W, S), f32)],
        args=(proj, proj, proj, dao, trows, dproj))


def _in_proj_bwd(dproj, w_in, x2, gamma, dh1, phase=None):
    T, D = x2.shape
    nk, _, tk = w_in.shape
    tm = _tile(T, BIG_ROW_TILE, 8)

    def body(dp_ref, w_ref, x_ref, g_ref, dh1_ref, dx_ref, part_ref, acc_ref):
        j = pl.program_id(1)

        @pl.when(j == 0)
        def _():
            acc_ref[...] = jnp.zeros_like(acc_ref)

        acc_ref[...] += _dot_nt(dp_ref[...], w_ref[0])

        @pl.when(j == nk - 1)
        def _():
            x = x_ref[...]
            r = lax.rsqrt(jnp.mean(x * x, axis=-1, keepdims=True) + EPS)
            n = x * r
            dxn = acc_ref[...]
            dn = dxn * g_ref[...]
            dx_ref[...] = dh1_ref[...] + r * (dn - n * jnp.mean(dn * n, axis=-1, keepdims=True))
            part_ref[...] = jnp.zeros_like(part_ref)
            part_ref[0:1, :] = jnp.sum(dxn * n, axis=0, keepdims=True)

    row = lambda n: pl.BlockSpec((tm, n), lambda i, j: (i, 0))
    return _call(
        body, phase, name="in_proj_bwd", grid=(T // tm, nk),
        in_specs=[pl.BlockSpec((tm, tk), lambda i, j: (i, j)), pl.BlockSpec((1, D, tk), lambda i, j: (j, 0, 0)), row(D),
                  pl.BlockSpec((1, D), lambda i, j: (0, 0)), row(D)],
        out_specs=[row(D), pl.BlockSpec((8, D), lambda i, j: (i, 0))],
        out_shape=[jax.ShapeDtypeStruct((T, D), f32), jax.ShapeDtypeStruct((T // tm * 8, D), f32)],
        scratch_shapes=[pltpu.VMEM((tm, D), f32)],
        args=(dproj, w_in, x2, gamma, dh1))


def _wgrad(a, b, shard_axis, name, phase=None):
    def spec(arr, sharded, tt):
        if arr.ndim == 3:
            return arr.shape[2], pl.BlockSpec((1, tt, arr.shape[2]), lambda s, t: (s, t, 0))
        if sharded:
            w = arr.shape[1] // N_CHIPS
            return w, pl.BlockSpec((tt, w), lambda s, t: (t, s))
        return arr.shape[1], pl.BlockSpec((tt, arr.shape[1]), lambda s, t: (t, 0))

    T = a.shape[-2]
    whole = a.ndim == 2 and b.ndim == 2 and a.shape[1] * b.shape[1] * 4 <= WGRAD_ACC_BYTES
    width = lambda arr, sharded: arr.shape[-1] // (1 if whole or arr.ndim == 3 or not sharded else N_CHIPS)
    wa, wb_ = width(a, shard_axis == 0), width(b, shard_axis == 1)
    fixed = wa * wb_ * (4 + 2 * 2)
    tt = T // 4 if whole else T
    while tt > 256 and 2 * tt * (wa * a.dtype.itemsize + wb_ * b.dtype.itemsize) + fixed > WGRAD_VMEM_BYTES:
        tt //= 2
    nt = T // tt
    if whole:
        K, N = a.shape[1], b.shape[1]
        a_spec, b_spec = pl.BlockSpec((tt, K), lambda s, t: (t, 0)), pl.BlockSpec((tt, N), lambda s, t: (t, 0))
        out_block = (N_CHIPS, K // N_CHIPS, N) if shard_axis == 0 else (N_CHIPS, K, N // N_CHIPS)
        out_spec = pl.BlockSpec(out_block, lambda s, t: (0, 0, 0))
    else:
        K, a_spec = spec(a, shard_axis == 0, tt)
        N, b_spec = spec(b, shard_axis == 1, tt)
        out_block = (N_CHIPS, K, N)
        out_spec = pl.BlockSpec((1, K, N), lambda s, t: (s, 0, 0))

    def body(a_ref, b_ref, o_ref, acc_ref):
        t = pl.program_id(1)

        @pl.when(t == 0)
        def _():
            acc_ref[...] = jnp.zeros_like(acc_ref)

        av = a_ref[0] if a.ndim == 3 else a_ref[...]
        bv = b_ref[0] if b.ndim == 3 else b_ref[...]
        acc_ref[...] += _dot_tn(av.astype(bf16), bv.astype(bf16))

        @pl.when(t == nt - 1)
        def _():
            if not whole:
                o_ref[0] = acc_ref[...].astype(bf16)
            else:
                _, kk, nn = out_block
                for s in range(N_CHIPS):
                    o_ref[s] = (acc_ref[s * kk:(s + 1) * kk, :] if shard_axis == 0
                                else acc_ref[:, s * nn:(s + 1) * nn]).astype(bf16)

    (grad,), carried = _call(
        body, phase, name=name, grid=(1 if whole else N_CHIPS, nt), in_specs=[a_spec, b_spec], out_specs=[out_spec],
        out_shape=[jax.ShapeDtypeStruct(out_block, bf16)], scratch_shapes=[pltpu.VMEM((K, N), f32)], args=(a, b))
    return grad, carried


def _adamw_sum(place, groups, name):
    n = len(groups)
    R, C = groups[0][0].shape
    half = R // 2
    tr = _tile(half, max(16, (1 << 18) // C // 16 * 16), 16)
    nr = half // tr

    def body(p_ref, *refs):
        for a in range(n):
            w_ref, m_ref, v_ref, part_ref, fc_ref, fs_ref = refs[6 * a:6 * a + 6]
            g_ref, d_ref, mo_ref, vo_ref = refs[6 * n + 4 * a:6 * n + 4 * a + 4]
            up = lambda x: x.astype(f32)
            mine = ((up(part_ref[0]) + up(fc_ref[0])) + up(fc_ref[1])) + up(fc_ref[2])
            sibs = ((up(fs_ref[0]) + up(fs_ref[1])) + up(fs_ref[2])) + up(fs_ref[3])
            g_ = jnp.where(pl.program_id(0) == p_ref[0], mine, sibs)
            m_ = ADAM_B1 * m_ref[...] + (1.0 - ADAM_B1) * g_
            v_ = ADAM_B2 * v_ref[...] + (1.0 - ADAM_B2) * (g_ * g_)
            m_hat = m_ / (1.0 - ADAM_B1 ** ADAM_STEP)
            v_hat = v_ / (1.0 - ADAM_B2 ** ADAM_STEP)
            g_ref[...] = g_
            d_ref[...] = -ADAM_LR * (m_hat / (jnp.sqrt(v_hat) + ADAM_EPS) + ADAM_WD * w_ref[...])
            mo_ref[...] = m_
            vo_ref[...] = v_

    spec = pl.BlockSpec((tr, C), lambda h, r, p: (h * nr + r, 0))
    one = [spec, spec, spec, pl.BlockSpec((1, tr, C), lambda h, r, p: (p[1], jnp.where(h == p[0], r, 0), 0)),
           pl.BlockSpec((3, tr, C), lambda h, r, p: (0, jnp.where(h == p[0], r, 0), 0)),
           pl.BlockSpec((4, tr, C), lambda h, r, p: (0, jnp.where(h == p[0], 0, r), 0))]
    res = pl.pallas_call(
        body, name=name,
        grid_spec=pltpu.PrefetchScalarGridSpec(num_scalar_prefetch=1, grid=(2, nr), in_specs=one * n,
                                               out_specs=[spec] * (4 * n)),
        out_shape=[jax.ShapeDtypeStruct((R, C), f32)] * (4 * n),
        compiler_params=_params(("parallel", "parallel")),
    )(place, *[x for g in groups for x in g])
    return [tuple(res[4 * a:4 * a + 4]) for a in range(n)]


def _adamw(w, g, m, v, name):
    R, C = w.shape
    tr = _tile(R, max(8, (1 << 18) // C // 8 * 8), 8)

    def body(w_ref, g_ref, m_ref, v_ref, d_ref, mo_ref, vo_ref):
        g_ = g_ref[...]
        m_ = ADAM_B1 * m_ref[...] + (1.0 - ADAM_B1) * g_
        v_ = ADAM_B2 * v_ref[...] + (1.0 - ADAM_B2) * (g_ * g_)
        m_hat = m_ / (1.0 - ADAM_B1 ** ADAM_STEP)
        v_hat = v_ / (1.0 - ADAM_B2 ** ADAM_STEP)
        d_ref[...] = -ADAM_LR * (m_hat / (jnp.sqrt(v_hat) + ADAM_EPS) + ADAM_WD * w_ref[...])
        mo_ref[...] = m_
        vo_ref[...] = v_

    spec = pl.BlockSpec((tr, C), lambda i: (i, 0))
    return pl.pallas_call(
        body, name=name, grid=(R // tr,), in_specs=[spec] * 4, out_specs=[spec] * 3,
        out_shape=[jax.ShapeDtypeStruct((R, C), f32)] * 3,
        compiler_params=_params(("parallel",)),
    )(w, g, m, v)


def _place():
    return lax.axis_index("x"), lax.axis_index("y"), lax.axis_index("c")


def _other_chips(x, y):
    chips = [(1 - x, y), (x, 1 - y), (1 - x, 1 - y)]
    return chips, [2 * cx + cy for cx, cy in chips]


def _spread_phase(blk):
    def peers():
        x, y, c = _place()
        return [tuple(1 - p if (k >> s) & 1 else p for p, s in ((x, 2), (y, 1), (c, 0))) for k in range(1, N_DEV)]

    def copies(pin, out):
        x, y, c = _place()
        mine = out[0].at[4 * x + 2 * y + c]
        return [(mine, mine, peer) for peer in peers()]

    stack = jnp.broadcast_to(blk, (N_DEV,) + blk.shape)
    return _Phase([stack], [jax.ShapeDtypeStruct(stack.shape, stack.dtype)], {0: 0}, N_DEV - 1, copies,
                  lambda pin, out: [out[0].at[4 * px + 2 * py + pc] for px, py, pc in peers()])


def _sum_slots(stack, name):
    def body(s_ref, o_ref):
        tot = s_ref[0]
        for d in range(1, stack.shape[0]):
            tot = tot + s_ref[d]
        o_ref[...] = tot

    vm = pl.BlockSpec(memory_space=pltpu.VMEM)
    return pl.pallas_call(body, name=name, in_specs=[vm], out_specs=vm,
                          out_shape=jax.ShapeDtypeStruct(stack.shape[1:], stack.dtype))(stack)


def _cast_shards(place, ws, name):
    n = len(ws)
    R, C = ws[0].shape
    tr = _tile(R, max(16, (1 << 19) // C // 16 * 16), 16)

    def body(p_ref, *refs):
        for a in range(n):
            refs[n + a][0] = refs[a][...].astype(bf16)

    return pl.pallas_call(
        body, name=name,
        grid_spec=pltpu.PrefetchScalarGridSpec(
            num_scalar_prefetch=1, grid=(R // tr,),
            in_specs=[pl.BlockSpec((tr, C), lambda r, p: (r, 0))] * n,
            out_specs=[pl.BlockSpec((1, tr, C), lambda r, p: (p[1], r, 0))] * n),
        out_shape=[jax.ShapeDtypeStruct((N_CHIPS, R, C), bf16)] * n,
        compiler_params=_params(("parallel",)),
    )(place, *ws)


class _Phase:
    def __init__(self, arrays, out_shapes, aliases, n_copies, copies, arrivals, own_starts=(), own_waits=()):
        self.arrays, self.out_shapes, self.aliases = list(arrays), list(out_shapes), dict(aliases)
        self.n_copies, self.copies, self.arrivals = n_copies, copies, arrivals
        self.own_starts, self.own_waits = tuple(own_starts), tuple(own_waits)

    def sems(self):
        return [pltpu.SemaphoreType.DMA((self.n_copies,)), pltpu.SemaphoreType.DMA((self.n_copies,))]

    def _descriptors(self, pin, pout, send_sems, recv_sems):
        return [pltpu.make_async_remote_copy(src_ref=s, dst_ref=d, send_sem=send_sems.at[i], recv_sem=recv_sems.at[i],
                                             device_id=to, device_id_type=MESH)
                for i, (s, d, to) in enumerate(self.copies(pin, pout))]

    def _arrival(self, i, pin, pout, send_sems, recv_sems):
        dst = self.arrivals(pin, pout)[i]
        return pltpu.make_async_remote_copy(src_ref=dst, dst_ref=dst, send_sem=send_sems.at[i], recv_sem=recv_sems.at[i],
                                            device_id=_place(), device_id_type=MESH)

    def start(self, pin, pout, send_sems, recv_sems):
        for i, cp in enumerate(self._descriptors(pin, pout, send_sems, recv_sems)):
            if i not in self.own_starts:
                cp.start()

    def begin(self, i, pin, pout, send_sems, recv_sems):
        self._descriptors(pin, pout, send_sems, recv_sems)[i].start()

    def arrived(self, i, pin, pout, send_sems, recv_sems):
        self._arrival(i, pin, pout, send_sems, recv_sems).wait_recv()

    def finish(self, pin, pout, send_sems, recv_sems):
        for i in range(self.n_copies):
            if i not in self.own_waits:
                self._arrival(i, pin, pout, send_sems, recv_sems).wait_recv()
        for cp in self._descriptors(pin, pout, send_sems, recv_sems):
            cp.wait_send()


def _join(phases):
    if len(phases) == 1:
        return phases[0]
    ai = np.cumsum([0] + [len(p.arrays) for p in phases])
    oi = np.cumsum([0] + [len(p.out_shapes) for p in phases])

    def each(fn_name, pin, pout):
        return [item for k, p in enumerate(phases)
                for item in getattr(p, fn_name)(pin[ai[k]:ai[k + 1]], pout[oi[k]:oi[k + 1]])]

    aliases = {int(ai[k]) + i: int(oi[k]) + j for k, p in enumerate(phases) for i, j in p.aliases.items()}
    ci = np.cumsum([0] + [p.n_copies for p in phases])
    shifted = lambda attr: [int(ci[k]) + i for k, p in enumerate(phases) for i in getattr(p, attr)]
    return _Phase([a for p in phases for a in p.arrays], [s for p in phases for s in p.out_shapes], aliases,
                  int(ci[-1]), functools.partial(each, "copies"), functools.partial(each, "arrivals"),
                  shifted("own_starts"), shifted("own_waits"))


def _call(body, phase, *, name, grid, in_specs, out_specs, out_shape, scratch_shapes, args, prefetch=(), expose=False,
          aliases=None):
    seq = _params(("arbitrary",) * len(grid))
    np_ = len(prefetch)
    own = {np_ + i: j for i, j in (aliases or {}).items()}
    if phase is None:
        spec = pltpu.PrefetchScalarGridSpec(num_scalar_prefetch=np_, grid=grid, in_specs=in_specs, out_specs=out_specs,
                                            scratch_shapes=scratch_shapes)
        res = pl.pallas_call(body, name=name, grid_spec=spec, out_shape=out_shape, input_output_aliases=own,
                             compiler_params=seq)(*prefetch, *args)
        return list(res), []
    ni, no, ns = len(in_specs), len(out_specs), len(scratch_shapes)
    pi, po = len(phase.arrays), len(phase.out_shapes)

    def hosted(*refs):
        cut = np.cumsum([np_, ni, pi, no, po, ns])
        pre, ins, pin, outs, pout, scr, sems = (refs[a:b] for a, b in zip([0, *cut], [*cut, len(refs)]))
        ids = [pl.program_id(d) for d in range(len(grid))]
        first = functools.reduce(lambda p, q: p & q, [i == 0 for i in ids])
        last = functools.reduce(lambda p, q: p & q, [i == g - 1 for i, g in zip(ids, grid)])
        pl.when(first)(lambda: phase.start(pin, pout, *sems))
        body(*pre, *ins, *outs, *scr, **({"carried": (pin, pout, sems)} if expose else {}))
        pl.when(last)(lambda: phase.finish(pin, pout, *sems))

    anyspace = pl.BlockSpec(memory_space=pl.ANY)
    spec = pltpu.PrefetchScalarGridSpec(
        num_scalar_prefetch=np_, grid=grid, in_specs=list(in_specs) + [anyspace] * pi,
        out_specs=list(out_specs) + [anyspace] * po, scratch_shapes=list(scratch_shapes) + phase.sems())
    res = pl.pallas_call(
        hosted, name=name, grid_spec=spec, out_shape=list(out_shape) + phase.out_shapes,
        input_output_aliases={**own, **{np_ + ni + i: no + j for i, j in phase.aliases.items()}}, compiler_params=seq,
    )(*prefetch, *args, *phase.arrays)
    return list(res[:no]), list(res[no:])


def _run_phases(name, phases):
    first = phases[0]
    pi, po = len(first.arrays), len(first.out_shapes)

    def body(*refs):
        pin, pout, sems = refs[:pi], refs[pi:pi + po], refs[pi + po:]
        for n, ph in enumerate(phases):
            ph.start(pin, pout, *sems[2 * n:2 * n + 2])
            ph.finish(pin, pout, *sems[2 * n:2 * n + 2])

    anyspace = pl.BlockSpec(memory_space=pl.ANY)
    return list(pl.pallas_call(
        body, name=name, in_specs=[anyspace] * pi, out_specs=[anyspace] * po, out_shape=first.out_shapes,
        input_output_aliases=first.aliases, scratch_shapes=[s for ph in phases for s in ph.sems()],
    )(*first.arrays))


def _half_rows(buf, c):
    half = buf.shape[1] // 2
    return pl.ds(c * half, half), pl.ds((1 - c) * half, half)


def _gather_phase(bufs, over_ici):
    n = len(bufs)
    shapes = [jax.ShapeDtypeStruct(b.shape, b.dtype) for b in bufs]

    def landed(out, which):
        x, y, c = _place()
        _, ks = _other_chips(x, y)
        return [out[a].at[ks[j], _half_rows(bufs[a], c)[which]] for a in range(n) for j in range(3)]

    def ici(pin, out):
        x, y, c = _place()
        chips, _ = _other_chips(x, y)
        mine = [out[a].at[2 * x + y, _half_rows(bufs[a], c)[0]] for a in range(n)]
        return [(mine[a], mine[a], (*chips[j], c)) for a in range(n) for j in range(3)]

    def d2d(pin, out):
        x, y, c = _place()
        return [(dst, dst, (x, y, 1 - c)) for dst in landed(out, 0)]

    if over_ici:
        return _Phase(bufs, shapes, {a: a for a in range(n)}, 3 * n, ici, lambda pin, out: landed(out, 0))
    return _Phase(bufs, shapes, {a: a for a in range(n)}, 3 * n, d2d, lambda pin, out: landed(out, 1))


def _feed_phase(buf):
    def chips():
        x, y, c = _place()
        return [(x if f < 2 else 1 - x, y if f % 2 == 0 else 1 - y) for f in (1, 2, 3)]

    def copies(pin, out):
        x, y, c = _place()
        mine = _half_rows(buf, c)[0]
        own = out[0].at[2 * x + y, mine]
        sent = [(own, own, (cx, cy, c)) for cx, cy in chips()]
        return sent + [(out[0].at[2 * cx + cy, mine], out[0].at[2 * cx + cy, mine], (x, y, 1 - c)) for cx, cy in chips()]

    def arrivals(pin, out):
        x, y, c = _place()
        mine, theirs = _half_rows(buf, c)
        return [out[0].at[2 * cx + cy, rows] for rows in (mine, theirs) for cx, cy in chips()]

    return _Phase([buf], [jax.ShapeDtypeStruct(buf.shape, buf.dtype)], {0: 0}, 6, copies, arrivals,
                  own_starts=(3, 4, 5), own_waits=range(6))


def _rs_swap_phase(grads):
    n = len(grads)

    def copies(g, out):
        x, y, c = _place()
        return [(g[a].at[:, _half_rows(grads[a], c)[1]], out[a], (x, y, 1 - c)) for a in range(n)]

    shapes = [jax.ShapeDtypeStruct((N_CHIPS, g.shape[1] // 2, g.shape[2]), g.dtype) for g in grads]
    return _Phase(grads, shapes, {}, n, copies, lambda g, out: list(out))


def _rs_add_sibling(place, grads, gots, name):
    n = len(grads)
    _, R, C = grads[0].shape
    half = R // 2
    tr = _tile(half, max(16, (1 << 19) // C // 16 * 16), 16)
    nr = half // tr

    def body(p_ref, *refs):
        for a in range(n):
            refs[2 * n + a][...] = (refs[2 * a][...].astype(f32) + refs[2 * a + 1][...].astype(f32)).astype(bf16)

    res = pl.pallas_call(
        body, name=name,
        grid_spec=pltpu.PrefetchScalarGridSpec(
            num_scalar_prefetch=1, grid=(N_CHIPS, nr),
            in_specs=[pl.BlockSpec((1, tr, C), lambda k, r, p: (k, p[0] * nr + r, 0)),
                      pl.BlockSpec((1, tr, C), lambda k, r, p: (k, r, 0))] * n,
            out_specs=[pl.BlockSpec((1, tr, C), lambda k, r, p: (k, r, 0))] * n),
        out_shape=[jax.ShapeDtypeStruct((N_CHIPS, half, C), bf16)] * n,
        compiler_params=_params(("parallel", "parallel")),
    )(place, *[x for pair in zip(grads, gots) for x in pair])
    return list(res)


def _rs_chips_phase(parts):
    n = len(parts)

    def copies(p, fc):
        x, y, c = _place()
        chips, ks = _other_chips(x, y)
        return [(p[a].at[ks[j]], fc[a].at[j], (*chips[j], c)) for a in range(n) for j in range(3)]

    shapes = [jax.ShapeDtypeStruct((3,) + q.shape[1:], q.dtype) for q in parts]
    return _Phase(parts, shapes, {}, 3 * n, copies, lambda p, fc: [fc[a].at[j] for a in range(n) for j in range(3)])


def _rs_hand_phase(parts, from_chips):
    n = len(parts)

    def copies(pin, fs):
        x, y, c = _place()
        sib = (x, y, 1 - c)
        own = [(pin[a].at[2 * x + y], fs[a].at[0], sib) for a in range(n)]
        return own + [(pin[n + a].at[j], fs[a].at[1 + j], sib) for a in range(n) for j in range(3)]

    def arrivals(pin, fs):
        return [fs[a].at[0] for a in range(n)] + [fs[a].at[1 + j] for a in range(n) for j in range(3)]

    shapes = [jax.ShapeDtypeStruct((4,) + q.shape[1:], q.dtype) for q in parts]
    return _Phase(list(parts) + list(from_chips), shapes, {}, 4 * n, copies, arrivals)


class _Exchange:
    def __init__(self, place):
        self.place = place

    def feed(self, buf):
        return _feed_phase(buf)

    def gather(self, bufs, over_ici):
        return _gather_phase(bufs, over_ici)

    def swap(self, grads):
        return _rs_swap_phase(grads)

    def pair_sums(self, names, grads):
        return self.add(names, grads, _run_phases("rs_sibling_" + names[0], [_rs_swap_phase(grads)]))

    def add(self, names, grads, got):
        parts = {}
        for group in _same_shape(grads):
            res = _rs_add_sibling(self.place, [grads[i] for i in group], [got[i] for i in group],
                                  "rs_add_" + names[group[0]])
            parts.update(zip(group, res))
        return [parts[i] for i in range(len(names))]

    def to_chips(self, parts):
        return _rs_chips_phase(parts)

    def to_sibling(self, parts, from_chips):
        return _rs_hand_phase(parts, from_chips)

    def spread(self, blk):
        return _spread_phase(blk)

    def hand_over(self, name, parts, from_chips, blk):
        got = _run_phases(name, [_join([_rs_hand_phase(parts, from_chips), _spread_phase(blk)])])
        return got[:-1], got[-1]


def _local_step(place, x, target, norm_mix, b_gate, rb_chip, norm_ffn, norm_final, w_in, rest, exch):
    B, S, D = x.shape
    T = B * S
    x2 = x.reshape(T, D)
    tg2 = target.reshape(T, D)
    rope, decay = _rope_tables(S), _decay_tables()
    g_fin = norm_final.reshape(1, D)
    nrel = rb_chip.shape[-1]

    mrg, ffn = ["w_ret_out", "w_att_out", "w_out"], ["w_ffn_gate", "w_ffn_up", "w_ffn_down"]
    (xn, proj), got = _in_proj(place, x2, norm_mix, _join([exch.feed(w_in), exch.gather([rest[n] for n in mrg], True),
                                                           exch.spread(jnp.pad(rb_chip, ((0, 0), (0, 128 - nrel))))]))
    w_in, wb, rb_all = got[0], {}, got.pop()
    trows = _bias_rows(jnp.concatenate([rb_all[2 * k, :, :nrel] for k in range(N_CHIPS)], axis=1))
    (qr, kr, o, u, states), got = _ret_fwd(proj, B, S, rope, decay, _join([exch.gather([rest["w_ffn_gate"]], True),
                                                                         exch.gather(got[1:], False)]))
    wb.update(zip(mrg, got[1:]))
    (ao,), got = _att_fwd(proj, trows, B, S, _join([exch.gather([rest["w_ffn_up"], rest["w_ffn_down"]], True),
                                                    exch.gather(got[:1], False)]))
    wb["w_ffn_gate"] = got[2]
    w_ro, w_out = wb["w_ret_out"].reshape(-1, D), wb["w_out"].reshape(-1, D)
    (h1, yr, ya), got = _mix_fwd(x2, proj, u, ao, b_gate, w_ro, wb["w_att_out"], w_out, exch.gather(got[:2], False))
    wb.update(zip(ffn[1:], got))
    hn, a, b, f, dh2, part_fin = _ffn_fwd(h1, norm_ffn, wb["w_ffn_gate"], wb["w_ffn_up"], wb["w_ffn_down"], g_fin, tg2)

    da, db, dh1, part_ffn = _ffn_bwd(dh2, h1, norm_ffn, a, b, wb["w_ffn_gate"], wb["w_ffn_up"], wb["w_ffn_down"])
    ffn = ["w_ffn_down", "w_ffn_gate", "w_ffn_up"]
    g_ffn = [_wgrad(f, dh2, 0, "wgrad_ffn_down")[0], _wgrad(da, hn, 0, "wgrad_ffn_gate")[0],
             _wgrad(db, hn, 0, "wgrad_ffn_up")[0]]
    (du, dao, dgl, mix, dyr, dya, part_bg), x_ffn = _mix_bwd(dh1, proj, yr, ya, b_gate, w_ro, wb["w_att_out"], w_out,
                                                             exch.swap(g_ffn))
    p_ffn = exch.add(ffn, g_ffn, x_ffn)
    mrg = ["w_out", "w_ret_out", "w_att_out"]
    g_mrg = [_wgrad(mix, dh1, 0, "wgrad_out")[0], _wgrad(u, dyr, 0, "wgrad_ret_out")[0],
             _wgrad(ao, dya, 1, "wgrad_att_out")[0]]
    (dproj,), got = _ret_bwd(proj, qr, kr, o, states, du, B, S, rope, decay, _join([exch.to_chips(p_ffn[:2]),
                                                                                     exch.swap(g_mrg)]))
    c_two, p_mrg = got[:2], exch.add(mrg, g_mrg, got[2:])
    (dproj, dvec), got = _att_bwd(proj, dao, trows, dproj, B, S, exch.to_chips(p_ffn[2:] + p_mrg))
    c_ffn, c_mrg = c_two + got[:1], got[1:]
    dproj = lax.dynamic_update_slice(dproj, dgl, (0, C_GL))
    g_in, got = _wgrad(xn, dproj, 1, "wgrad_in", exch.to_sibling(p_ffn + p_mrg, c_ffn + c_mrg))
    s_ffn, s_mrg = got[:len(ffn)], got[len(ffn):]
    p_in = exch.pair_sums(["w_in"], [g_in])
    (gx, part_mix), c_in = _in_proj_bwd(dproj, w_in, x2, norm_mix, dh1, exch.to_chips(p_in))
    rows = lambda p, r: p.reshape(-1, 8, p.shape[-1])[:, r, :].sum(axis=0)
    lo = KWIN - 1 - (MAX_REL - 1)
    drb = jnp.concatenate([jnp.flip(dvec[:, lo:lo + N_REL - 1], axis=1), dvec[:, :lo].sum(axis=1, keepdims=True)], axis=1)
    gsmall = {
        "norm_mix": rows(part_mix, 0), "b_gate": rows(part_bg, 0), "rel_bias": drb, "norm_ffn": rows(part_ffn, 0),
        "norm_final": rows(part_fin, 0),
    }
    s_in, small_all = exch.hand_over("rs_hand_w_in", p_in, c_in, _pack_small(gsmall, rows(part_fin, 1)))
    gbig = dict(zip(ffn + mrg + ["w_in"], zip(p_ffn + p_mrg + p_in, c_ffn + c_mrg + c_in, s_ffn + s_mrg + s_in)))
    return gx.reshape(B, S, D), gbig, small_all


SMALL_ROWS = 16


def _pack_small(gs, loss_lanes):
    D = D_MODEL
    rb = jnp.pad(gs["rel_bias"].reshape(-1), (0, 3 * D - ATT_HEADS * N_REL)).reshape(3, D)
    rows = [gs["norm_mix"].reshape(1, D), gs["b_gate"].reshape(2, D), gs["norm_ffn"].reshape(1, D),
            gs["norm_final"].reshape(1, D), rb, loss_lanes.reshape(1, D)]
    used = sum(r.shape[0] for r in rows)
    return jnp.concatenate(rows + [jnp.zeros((SMALL_ROWS - used, D), f32)], axis=0)


def kernel(x, norm_mix, w_in, b_gate, rel_bias, w_ret_out, w_att_out, w_out, norm_ffn, w_ffn_gate, w_ffn_up, w_ffn_down, norm_final, loss_target, m_norm_mix, m_w_in, m_b_gate, m_rel_bias, m_w_ret_out, m_w_att_out, m_w_out, m_norm_ffn, m_w_ffn_gate, m_w_ffn_up, m_w_ffn_down, m_norm_final, v_norm_mix, v_w_in, v_b_gate, v_rel_bias, v_w_ret_out, v_w_att_out, v_w_out, v_norm_ffn, v_w_ffn_gate, v_w_ffn_up, v_w_ffn_down, v_norm_final):
    w = dict(norm_mix=norm_mix, w_in=w_in, b_gate=b_gate, rel_bias=rel_bias, w_ret_out=w_ret_out, w_att_out=w_att_out,
             w_out=w_out, norm_ffn=norm_ffn, w_ffn_gate=w_ffn_gate, w_ffn_up=w_ffn_up, w_ffn_down=w_ffn_down,
             norm_final=norm_final)
    m = dict(norm_mix=m_norm_mix, w_in=m_w_in, b_gate=m_b_gate, rel_bias=m_rel_bias, w_ret_out=m_w_ret_out,
             w_att_out=m_w_att_out, w_out=m_w_out, norm_ffn=m_norm_ffn, w_ffn_gate=m_w_ffn_gate, w_ffn_up=m_w_ffn_up,
             w_ffn_down=m_w_ffn_down, norm_final=m_norm_final)
    v = dict(norm_mix=v_norm_mix, w_in=v_w_in, b_gate=v_b_gate, rel_bias=v_rel_bias, w_ret_out=v_w_ret_out,
             w_att_out=v_w_att_out, w_out=v_w_out, norm_ffn=v_norm_ffn, w_ffn_gate=v_w_ffn_gate, w_ffn_up=v_w_ffn_up,
             w_ffn_down=v_w_ffn_down, norm_final=v_norm_final)
    xi, yi, ci = _place()
    k_me = 2 * xi + yi

    place = jnp.stack([ci, k_me]).astype(jnp.int32)
    big = [n for n, _ in BIG]

    turned = ("w_ffn_gate", "w_ffn_up")
    shard = lambda d, n: jnp.swapaxes(d[n][0], 0, 1) if n in turned else d[n][0]
    whole = lambda a, n: (jnp.swapaxes(a, 0, 1) if n in turned else a)[None]

    by_shape = [[big[i] for i in group] for group in _same_shape([shard(w, n) for n in big])]
    bufs = {}
    for names in by_shape:
        bufs.update(zip(names, _cast_shards(place, [shard(w, n) for n in names], "cast_" + names[0])))
    rest = {n: bufs[n] for n in big if n != "w_in"}
    nrel_loc = rel_bias.shape[-1]
    grad_x, gbig, small_all = _local_step(place, x, loss_target, norm_mix, b_gate, rel_bias[0], norm_ffn, norm_final,
                                          bufs["w_in"], rest, _Exchange(place))

    small = _sum_slots(small_all, "reduce_small")
    D = D_MODEL
    loss = jnp.sum(small[8])
    drb_full = small[5:8].reshape(-1)[:ATT_HEADS * N_REL].reshape(ATT_HEADS, N_REL)
    g = {
        "norm_mix": small[0:1], "b_gate": small[1:3].reshape(1, 2 * D), "norm_ffn": small[3:4], "norm_final": small[4],
        "rel_bias": lax.dynamic_slice_in_dim(drb_full, k_me * nrel_loc, nrel_loc, axis=1)[None],
    }

    delta, new_m, new_v = {}, {}, {}
    for names in by_shape:
        res = _adamw_sum(place, [(shard(w, n), shard(m, n), shard(v, n), *gbig[n]) for n in names], "adamw_" + names[0])
        for n, (g_, d_, m_, v_) in zip(names, res):
            g[n], delta[n], new_m[n], new_v[n] = whole(g_, n), whole(d_, n), whole(m_, n), whole(v_, n)
    flat = lambda d: jnp.concatenate([d[n].reshape(-1) for n in SMALL])
    n_small = sum(int(np.prod(w[n].shape)) for n in SMALL)
    n_pad = -n_small % 1024
    packs = [jnp.pad(flat(d), (0, n_pad)).reshape(-1, 128) for d in (w, g, m, v)]
    outs = _adamw(*packs, "adamw_small")
    for res, dst in zip(outs, (delta, new_m, new_v)):
        off = 0
        fl = res.reshape(-1)
        for n in SMALL:
            sz = int(np.prod(w[n].shape))
            dst[n] = fl[off:off + sz].reshape(w[n].shape)
            off += sz

    return (loss, grad_x, *[g[n] for n in WEIGHTS], *[delta[n] for n in WEIGHTS], *[new_m[n] for n in WEIGHTS],
            *[new_v[n] for n in WEIGHTS])
```

```python
import functools

import numpy as np
import jax
import jax.numpy as jnp
from jax import lax
from jax.experimental import pallas as pl
from jax.experimental.pallas import tpu as pltpu

f32 = jnp.float32
bf16 = jnp.bfloat16

D_MODEL = 1024
CHUNK = 64
RET_HEADS = 4
RET_KEY_DIM = 128
RET_VAL_DIM = 256
ATT_HEADS = 8
ATT_HEAD_DIM = 64
ATT_W = ATT_HEADS * ATT_HEAD_DIM
BAND_CHUNKS = 8
PAD = BAND_CHUNKS * CHUNK
MAX_REL = 256
N_REL = CHUNK + MAX_REL
D_FF = 2816
N_IN = 6656
ROPE_BASE = 10000.0
EPS = 1e-6
NEG_INF = -1e30
C_RQ, C_RK, C_RV, C_RG, C_AQ, C_AK, C_AV, C_GL = 0, 512, 1024, 2048, 3072, 3584, 4096, 4608

ADAM_LR, ADAM_B1, ADAM_B2, ADAM_EPS, ADAM_WD, ADAM_STEP = 0.001, 0.9, 0.999, 1e-08, 0.01, 10

N_CHIPS = 4
N_DEV = 8
WGRAD_ACC_BYTES = 8 * 1024 * 1024
WGRAD_VMEM_BYTES = 40 * 1024 * 1024
ROW_TILE = 512
BIG_ROW_TILE = 1024
IN_ORDER = (0, 2, 3, 1)
QBLK = 256
KWIN = PAD + QBLK
TOEP = 1024
VMEM_LIMIT = 56 * 1024 * 1024
MESH = pl.DeviceIdType.MESH

BIG = (
    ("w_in", 1), ("w_ret_out", 0), ("w_att_out", 1), ("w_out", 0), ("w_ffn_gate", 1), ("w_ffn_up", 1), ("w_ffn_down", 0))
WEIGHTS = ("norm_mix", "w_in", "b_gate", "rel_bias", "w_ret_out", "w_att_out", "w_out", "norm_ffn", "w_ffn_gate",
           "w_ffn_up", "w_ffn_down", "norm_final")
SMALL = ("norm_mix", "b_gate", "rel_bias", "norm_ffn", "norm_final")


def _dot(a, b):
    return lax.dot_general(a, b, (((1,), (0,)), ((), ())), preferred_element_type=f32)


def _dot_nt(a, b):
    return lax.dot_general(a, b, (((1,), (1,)), ((), ())), preferred_element_type=f32)


def _dot_tn(a, b):
    return lax.dot_general(a, b, (((0,), (0,)), ((), ())), preferred_element_type=f32)


def _sig(x):
    return 1.0 / (1.0 + jnp.exp(-x))


def _tile(n, pref, mult):
    best = None
    for t in range(mult, min(n, pref) + 1, mult):
        if n % t == 0:
            best = t
    return best if best is not None else n


def _same_shape(arrays):
    groups = {}
    for i, a in enumerate(arrays):
        groups.setdefault(a.shape, []).append(i)
    return list(groups.values())


def _params(sem, vmem=VMEM_LIMIT):
    return pltpu.CompilerParams(dimension_semantics=sem, vmem_limit_bytes=vmem)


def _in_proj(place, x2, gamma, phase):
    T, D = x2.shape
    _, _, ns = phase.arrays[0].shape
    tm = _tile(T, BIG_ROW_TILE, 8)
    ni = T // tm
    pass_chip = lambda j: sum(jnp.where(j == n, f, 0) for n, f in enumerate(IN_ORDER))

    def body(p_ref, x_ref, g_ref, xn_ref, pr_ref, xs_ref, w_ref, w_sem, carried):
        j, i = pl.program_id(0), pl.program_id(1)
        pin, pout, sems = carried
        rows = pl.ds(pl.multiple_of(i * tm, tm), tm)

        @pl.when(i == 0)
        def _():
            for n, f in enumerate(IN_ORDER):
                if f:
                    @pl.when(j == n)
                    def _():
                        phase.arrived(f - 1, pin, pout, *sems)
                        phase.begin(2 + f, pin, pout, *sems)
                        phase.arrived(2 + f, pin, pout, *sems)
            shard = pltpu.make_async_copy(pout[0].at[jnp.bitwise_xor(p_ref[1], pass_chip(j))], w_ref, w_sem)
            shard.start()
            shard.wait()

        @pl.when(j == 0)
        def _():
            x = x_ref[...]
            r = lax.rsqrt(jnp.mean(x * x, axis=-1, keepdims=True) + EPS)
            xn = (x * r * g_ref[...]).astype(bf16)
            xs_ref[rows, :] = xn
            xn_ref[...] = xn

        pr_ref[...] = _dot(xs_ref[rows, :], w_ref[...]).astype(bf16)

    first_pass = lambda j, i, p: (jnp.where(j == 0, i, ni - 1), 0)
    return _call(
        body, phase, name="in_proj", grid=(N_CHIPS, ni), prefetch=(place,), expose=True,
        in_specs=[pl.BlockSpec((tm, D), first_pass), pl.BlockSpec((1, D), lambda j, i, p: (0, 0))],
        out_specs=[pl.BlockSpec((tm, D), first_pass),
                   pl.BlockSpec((tm, ns), lambda j, i, p: (i, jnp.bitwise_xor(p[1], pass_chip(j))))],
        out_shape=[jax.ShapeDtypeStruct((T, D), bf16), jax.ShapeDtypeStruct((T, N_CHIPS * ns), bf16)],
        scratch_shapes=[pltpu.VMEM((T, D), bf16), pltpu.VMEM((D, ns), bf16), pltpu.SemaphoreType.DMA],
        args=(x2, gamma))


def _rope_tables(S):
    d = RET_KEY_DIM
    freqs = ROPE_BASE ** (-jnp.arange(0, d, 2, dtype=f32) / d)
    ang = jnp.arange(S, dtype=f32)[:, None] * freqs[None, :]
    cos, sin = jnp.cos(ang), jnp.sin(ang)
    return jnp.concatenate([cos, cos], axis=1), jnp.concatenate([-sin, sin], axis=1)


def _decay_tables():
    H = RET_HEADS
    log_g = jnp.log(1.0 - 2.0 ** (-5.0 - jnp.arange(H, dtype=f32)))
    p = jnp.arange(CHUNK, dtype=f32)
    intra = jnp.exp(log_g[:, None, None] * jnp.abs(p[:, None] - p[None, :]))
    q_dec = jnp.exp(log_g[:, None] * (p[None, :] + 1.0))
    k_dec = jnp.exp(log_g[:, None] * (CHUNK - 1.0 - p[None, :]))
    c_dec = jnp.exp(log_g * CHUNK)
    q_dec = jnp.broadcast_to(q_dec[:, :, None], (H, CHUNK, RET_KEY_DIM))
    k_dec = jnp.broadcast_to(k_dec[:, :, None], (H, CHUNK, RET_KEY_DIM))
    c_dec = jnp.broadcast_to(c_dec[:, None, None], (H, 1, RET_VAL_DIM))
    return intra, q_dec, k_dec, c_dec


K_SCALE = RET_KEY_DIM ** -0.5


RET_CHUNKS = 4


def _ret_tables_specs():
    whole = lambda *shape: pl.BlockSpec(shape, lambda b, i: (0,) * len(shape))
    return [whole(RET_HEADS, CHUNK, CHUNK), whole(RET_HEADS, CHUNK, RET_KEY_DIM), whole(RET_HEADS, CHUNK, RET_KEY_DIM),
            whole(RET_HEADS, 1, RET_VAL_DIM)]


def _rotate(x, cos, sn):
    return x * cos + pltpu.roll(x, RET_KEY_DIM // 2, 1) * sn


def _ret_fwd(proj, B, S, rope, decay, phase=None):
    T = B * S
    nc = S // CHUNK
    H, dk, dv = RET_HEADS, RET_KEY_DIM, RET_VAL_DIM
    sb = RET_CHUNKS * CHUNK
    ns = S // sb

    def body(q_ref, k_ref, v_ref, g_ref, cos_ref, sin_ref, intra_ref, qd_ref, kd_ref, cd_ref,
             qr_ref, kr_ref, o_ref, u_ref, st_ref, state_ref):
        @pl.when(pl.program_id(1) == 0)
        def _():
            state_ref[...] = jnp.zeros_like(state_ref)

        cos, sn = cos_ref[...], sin_ref[...]
        for h in range(H):
            hs = slice(h * dk, (h + 1) * dk)
            qr_ref[:, hs] = _rotate(q_ref[:, hs].astype(f32), cos, sn).astype(bf16)
            kr_ref[:, hs] = (_rotate(k_ref[:, hs].astype(f32), cos, sn) * K_SCALE).astype(bf16)
        states = [state_ref[h] for h in range(H)]
        for ci in range(RET_CHUNKS):
            r = slice(ci * CHUNK, (ci + 1) * CHUNK)
            for h in range(H):
                hk, hv = slice(h * dk, (h + 1) * dk), slice(h * dv, (h + 1) * dv)
                qi, ki, vi = qr_ref[r, hk], kr_ref[r, hk], v_ref[r, hv]
                stb = states[h].astype(bf16)
                st_ref[0, h, ci] = stb
                s = (_dot_nt(qi, ki) * intra_ref[h]).astype(bf16)
                o = _dot(s, vi) + _dot((qi.astype(f32) * qd_ref[h]).astype(bf16), stb)
                states[h] = states[h] * cd_ref[h] + _dot_tn((ki.astype(f32) * kd_ref[h]).astype(bf16), vi)
                mu = jnp.mean(o, axis=-1, keepdims=True)
                xc = o - mu
                var = jnp.mean(xc * xc, axis=-1, keepdims=True)
                oh = xc * lax.rsqrt(var + EPS)
                g = g_ref[r, hv].astype(f32)
                o_ref[r, hv] = o.astype(bf16)
                u_ref[r, hv] = (g * _sig(g) * oh).astype(bf16)
        for h in range(H):
            state_ref[h] = states[h]

    blk = lambda w, c: pl.BlockSpec((sb, w), lambda b, i: (b * ns + i, c))
    return _call(
        body, phase, name="ret_fwd", grid=(B, ns), scratch_shapes=[pltpu.VMEM((H, dk, dv), f32)],
        in_specs=[blk(H * dk, C_RQ // (H * dk)), blk(H * dk, C_RK // (H * dk)), blk(H * dv, C_RV // (H * dv)),
                  blk(H * dv, C_RG // (H * dv)),
                  pl.BlockSpec((sb, dk), lambda b, i: (i, 0)), pl.BlockSpec((sb, dk), lambda b, i: (i, 0)),
                  *_ret_tables_specs()],
        out_specs=[blk(H * dk, 0), blk(H * dk, 0), blk(H * dv, 0), blk(H * dv, 0),
                   pl.BlockSpec((1, H, RET_CHUNKS, dk, dv), lambda b, i: (b, 0, i, 0, 0))],
        out_shape=[jax.ShapeDtypeStruct((T, H * dk), bf16), jax.ShapeDtypeStruct((T, H * dk), bf16),
                   jax.ShapeDtypeStruct((T, H * dv), bf16), jax.ShapeDtypeStruct((T, H * dv), bf16),
                   jax.ShapeDtypeStruct((B, H, nc, dk, dv), bf16)],
        args=(proj, proj, proj, proj, *rope, *decay))


def _bias_rows(rb):
    last = rb[:, N_REL - 1:]
    return jnp.concatenate([
        jnp.broadcast_to(last, (ATT_HEADS, PAD - MAX_REL + 1)),
        jnp.flip(rb[:, :N_REL - 1], axis=1),
        jnp.broadcast_to(rb[:, :1], (ATT_HEADS, KWIN - PAD - CHUNK)),
        jnp.broadcast_to(last, (ATT_HEADS, TOEP - KWIN)),
    ], axis=1)


def _build_bias(t_ref, bias_ref):
    row = lax.broadcasted_iota(jnp.int32, (QBLK, KWIN), 0) // CHUNK
    col = lax.broadcasted_iota(jnp.int32, (QBLK, KWIN), 1) // CHUNK
    delta = BAND_CHUNKS + row - col
    vis = (delta >= 0) & (delta <= BAND_CHUNKS)
    for h in range(ATT_HEADS):
        t = jnp.broadcast_to(t_ref[h:h + 1, :], (QBLK, TOEP))
        rolled = pltpu.roll(t, 0, 1, stride=1, stride_axis=0)
        bias_ref[h] = jnp.where(vis, rolled[:, :KWIN], NEG_INF)


ATT_SCALE = ATT_HEAD_DIM ** -0.5


def _att_probs(qh, kh, bias):
    s = _dot_nt(qh, kh) + bias
    m = jnp.max(s, axis=-1, keepdims=True)
    p = jnp.exp(s - m)
    return p * (1.0 / jnp.sum(p, axis=-1, keepdims=True))


def _first_of_pair():
    return lax.broadcasted_iota(jnp.int32, (1, 2 * ATT_HEAD_DIM), 1) < ATT_HEAD_DIM


def _by_window(i, step):
    sizes = list(range(QBLK, KWIN, QBLK))
    for n, nk in enumerate(sizes):
        pl.when(i == n)(functools.partial(step, nk))
    pl.when(i >= len(sizes))(functools.partial(step, KWIN))


def _att_fwd(proj, trows, B, S, phase=None):
    T = B * S
    nq = S // QBLK
    dh = ATT_HEAD_DIM

    def body(q_ref, k_ref, v_ref, t_ref, o_ref, bias_ref):
        i = pl.program_id(1)

        @pl.when((pl.program_id(0) == 0) & (i == 0))
        def _():
            _build_bias(t_ref, bias_ref)

        def step(nk):
            win = pl.ds(pl.multiple_of((i + 1) * QBLK - nk, QBLK), nk)
            kw, vw = k_ref[win, :], v_ref[win, :]
            first = _first_of_pair()
            outs = []
            for p in range(ATT_HEADS // 2):
                ps = slice(2 * p * dh, 2 * (p + 1) * dh)
                q2, k2, v2 = q_ref[:, ps] * ATT_SCALE, kw[:, ps], vw[:, ps]
                both = []
                for e in range(2):
                    qm = jnp.where(first == (e == 0), q2, jnp.zeros_like(q2))
                    pr = _att_probs(qm, k2, bias_ref[2 * p + e, :, KWIN - nk:])
                    both.append(_dot(pr.astype(bf16), v2))
                outs.append(jnp.where(first, both[0], both[1]))
            o_ref[...] = jnp.concatenate(outs, axis=1).astype(bf16)

        _by_window(i, step)

    return _call(
        body, phase, name="att_fwd", grid=(B, nq),
        in_specs=[pl.BlockSpec((QBLK, ATT_W), lambda b, i: (b * nq + i, C_AQ // ATT_W)),
                  pl.BlockSpec((S, ATT_W), lambda b, i: (b, C_AK // ATT_W)),
                  pl.BlockSpec((S, ATT_W), lambda b, i: (b, C_AV // ATT_W)),
                  pl.BlockSpec((ATT_HEADS, TOEP), lambda b, i: (0, 0))],
        out_specs=[pl.BlockSpec((QBLK, ATT_W), lambda b, i: (b * nq + i, 0))],
        out_shape=[jax.ShapeDtypeStruct((T, ATT_W), bf16)],
        scratch_shapes=[pltpu.VMEM((ATT_HEADS, QBLK, KWIN), f32)],
        args=(proj, proj, proj, trows))


def _gl_specs(tm):
    w = 512
    return [pl.BlockSpec((tm, w), functools.partial(lambda i, j: (i, C_GL // 512 + j), j=j)) for j in range(4)]


def _gates(gl_refs, bg_ref):
    gl = jnp.concatenate([r[...] for r in gl_refs], axis=1).astype(f32) + bg_ref[...]
    g = _sig(gl)
    return g[:, :D_MODEL], g[:, D_MODEL:]


def _mix_fwd(x2, proj, u, ao, b_gate, w_ro, w_ao, w_out, phase=None):
    T, D = x2.shape
    tm = _tile(T, ROW_TILE, 8)

    def body(x_ref, u_ref, ao_ref, g0, g1, g2, g3, bg_ref, wro_ref, wao_ref, wo_ref, h1_ref, yr_ref, ya_ref):
        yr = _dot(u_ref[...], wro_ref[...])
        ao = ao_ref[...]
        ya = jnp.concatenate([_dot(ao, wao_ref[k]) for k in range(N_CHIPS)], axis=1)
        gr, ga = _gates((g0, g1, g2, g3), bg_ref)
        mix = gr * yr + ga * ya
        h1_ref[...] = x_ref[...] + _dot(mix.astype(bf16), wo_ref[...])
        yr_ref[...] = yr.astype(bf16)
        ya_ref[...] = ya.astype(bf16)

    full = lambda a: pl.BlockSpec(a.shape, lambda i: (0,) * a.ndim)
    row = lambda n: pl.BlockSpec((tm, n), lambda i: (i, 0))
    return _call(
        body, phase, name="mix_fwd", grid=(T // tm,), scratch_shapes=[],
        in_specs=[row(D), row(D), row(ATT_W), *_gl_specs(tm), full(b_gate), full(w_ro), full(w_ao), full(w_out)],
        out_specs=[row(D), row(D), row(D)],
        out_shape=[jax.ShapeDtypeStruct((T, D), f32), jax.ShapeDtypeStruct((T, D), bf16),
                   jax.ShapeDtypeStruct((T, D), bf16)],
        args=(x2, u, ao, proj, proj, proj, proj, b_gate, w_ro, w_ao, w_out))


def _ffn_fwd(h1, g_ffn, wg, wu, wd, g_fin, target):
    T, D = h1.shape
    nf, tf, _ = wg.shape
    tm = _tile(T, ROW_TILE, 8)

    def body(h1_ref, g_ref, wg_ref, wu_ref, wd_ref, gf_ref, tg_ref, hn_ref, a_ref, b_ref, f_ref, dh2_ref, part_ref):
        h1v = h1_ref[...]
        r = lax.rsqrt(jnp.mean(h1v * h1v, axis=-1, keepdims=True) + EPS)
        hn = (h1v * r * g_ref[...]).astype(bf16)
        hn_ref[...] = hn
        h2 = h1v
        for k in range(nf):
            a = _dot_nt(hn, wg_ref[k])
            b = _dot_nt(hn, wu_ref[k])
            f = ((a * _sig(a)) * b).astype(bf16)
            a_ref[k] = a.astype(bf16)
            b_ref[k] = b.astype(bf16)
            f_ref[k] = f
            h2 = h2 + _dot(f, wd_ref[k])
        r = lax.rsqrt(jnp.mean(h2 * h2, axis=-1, keepdims=True) + EPS)
        n = h2 * r
        gf = gf_ref[...]
        e = n * gf - tg_ref[...]
        dy = e * (1.0 / D)
        dn = dy * gf
        dh2_ref[...] = r * (dn - n * jnp.mean(dn * n, axis=-1, keepdims=True))
        part_ref[...] = jnp.zeros_like(part_ref)
        part_ref[0:1, :] = jnp.sum(dy * n, axis=0, keepdims=True)
        part_ref[1:2, :] = (0.5 / D) * jnp.sum(e * e, axis=0, keepdims=True)

    row = lambda n: pl.BlockSpec((tm, n), lambda i: (i, 0))
    vec = pl.BlockSpec((1, D), lambda i: (0, 0))
    col = pl.BlockSpec((nf, tm, tf), lambda i: (0, i, 0))
    held = lambda w: pl.BlockSpec(w.shape, lambda i: (0, 0, 0), pipeline_mode=pl.Buffered(1))
    act = jax.ShapeDtypeStruct((nf, T, tf), bf16)
    return pl.pallas_call(
        body, name="ffn_fwd", grid=(T // tm,),
        in_specs=[row(D), vec, held(wg), held(wu), held(wd), vec, row(D)],
        out_specs=[row(D), col, col, col, row(D), pl.BlockSpec((8, D), lambda i: (i, 0))],
        out_shape=[jax.ShapeDtypeStruct((T, D), bf16), act, act, act,
                   jax.ShapeDtypeStruct((T, D), f32), jax.ShapeDtypeStruct((T // tm * 8, D), f32)],
        compiler_params=_params(("parallel",)),
    )(h1, g_ffn, wg, wu, wd, g_fin, target)


def _ffn_bwd(dh2, h1, g_ffn, a, b, wg, wu, wd):
    T, D = h1.shape
    nf, tf, _ = wg.shape
    tm = _tile(T, ROW_TILE // 2, 8)

    def body(dh2_ref, h1_ref, g_ref, a_ref, b_ref, wg_ref, wu_ref, wd_ref, da_ref, db_ref, dh1_ref, part_ref):
        dh2v = dh2_ref[...]
        dh2b = dh2v.astype(bf16)
        dhn = jnp.zeros((tm, D), f32)
        for k in range(nf):
            df = _dot_nt(dh2b, wd_ref[k])
            av = a_ref[k].astype(f32)
            sg = _sig(av)
            db = (df * (av * sg)).astype(bf16)
            da = (df * b_ref[k].astype(f32) * (sg * (1.0 + av * (1.0 - sg)))).astype(bf16)
            da_ref[k] = da
            db_ref[k] = db
            dhn = dhn + _dot(da, wg_ref[k]) + _dot(db, wu_ref[k])
        h = h1_ref[...]
        r = lax.rsqrt(jnp.mean(h * h, axis=-1, keepdims=True) + EPS)
        n = h * r
        dn = dhn * g_ref[...]
        dh1_ref[...] = dh2v + r * (dn - n * jnp.mean(dn * n, axis=-1, keepdims=True))
        part_ref[...] = jnp.zeros_like(part_ref)
        part_ref[0:1, :] = jnp.sum(dhn * n, axis=0, keepdims=True)

    row = lambda n: pl.BlockSpec((tm, n), lambda i: (i, 0))
    col = pl.BlockSpec((nf, tm, tf), lambda i: (0, i, 0))
    held = lambda w: pl.BlockSpec(w.shape, lambda i: (0, 0, 0), pipeline_mode=pl.Buffered(1))
    act = jax.ShapeDtypeStruct((nf, T, tf), bf16)
    return pl.pallas_call(
        body, name="ffn_bwd", grid=(T // tm,),
        in_specs=[row(D), row(D), pl.BlockSpec((1, D), lambda i: (0, 0)), col, col, held(wg), held(wu), held(wd)],
        out_specs=[col, col, row(D), pl.BlockSpec((8, D), lambda i: (i, 0))],
        out_shape=[act, act, jax.ShapeDtypeStruct((T, D), f32), jax.ShapeDtypeStruct((T // tm * 8, D), f32)],
        compiler_params=_params(("parallel",)),
    )(dh2, h1, g_ffn, a, b, wg, wu, wd)


def _mix_bwd(dh1, proj, yr, ya, b_gate, w_ro, w_ao, w_out, phase=None):
    T, D = dh1.shape
    tm = _tile(T, ROW_TILE, 8)

    def body(dh1_ref, g0, g1, g2, g3, bg_ref, yr_ref, ya_ref, wro_ref, wao_ref, wo_ref,
             du_ref, dao_ref, dgl_ref, mix_ref, dyr_ref, dya_ref, part_ref):
        dmix = _dot_nt(dh1_ref[...].astype(bf16), wo_ref[...])
        gr, ga = _gates((g0, g1, g2, g3), bg_ref)
        yr = yr_ref[...].astype(f32)
        ya = ya_ref[...].astype(f32)
        dyr = (dmix * gr).astype(bf16)
        dya = (dmix * ga).astype(bf16)
        dgl = jnp.concatenate([dmix * yr * gr * (1.0 - gr), dmix * ya * ga * (1.0 - ga)], axis=1)
        du_ref[...] = _dot_nt(dyr, wro_ref[...]).astype(bf16)
        ns = wao_ref.shape[2]
        dao = _dot_nt(dya[:, :ns], wao_ref[0])
        for k in range(1, N_CHIPS):
            dao = dao + _dot_nt(dya[:, k * ns:(k + 1) * ns], wao_ref[k])
        dao_ref[...] = dao.astype(bf16)
        dgl_ref[...] = dgl.astype(bf16)
        mix_ref[...] = (gr * yr + ga * ya).astype(bf16)
        dyr_ref[...] = dyr
        dya_ref[...] = dya
        part_ref[...] = jnp.zeros_like(part_ref)
        part_ref[0:1, :] = jnp.sum(dgl, axis=0, keepdims=True)

    full = lambda a: pl.BlockSpec(a.shape, lambda i: (0,) * a.ndim)
    row = lambda n: pl.BlockSpec((tm, n), lambda i: (i, 0))
    return _call(
        body, phase, name="mix_bwd", grid=(T // tm,), scratch_shapes=[],
        in_specs=[row(D), *_gl_specs(tm), full(b_gate), row(D), row(D), full(w_ro), full(w_ao), full(w_out)],
        out_specs=[row(D), row(ATT_W), row(2 * D), row(D), row(D), row(D), pl.BlockSpec((8, 2 * D), lambda i: (i, 0))],
        out_shape=[jax.ShapeDtypeStruct((T, D), bf16), jax.ShapeDtypeStruct((T, ATT_W), bf16),
                   jax.ShapeDtypeStruct((T, 2 * D), bf16), jax.ShapeDtypeStruct((T, D), bf16),
                   jax.ShapeDtypeStruct((T, D), bf16), jax.ShapeDtypeStruct((T, D), bf16),
                   jax.ShapeDtypeStruct((T // tm * 8, 2 * D), f32)],
        args=(dh1, proj, proj, proj, proj, b_gate, yr, ya, w_ro, w_ao, w_out))


def _ret_bwd(proj, qr, kr, o, states, du, dgl, B, S, rope, decay, phase=None):
    T = B * S
    nc = S // CHUNK
    H, dk, dv = RET_HEADS, RET_KEY_DIM, RET_VAL_DIM

    sb = RET_CHUNKS * CHUNK
    ns = S // sb

    def body(qr_ref, kr_ref, v_ref, g_ref, o_ref, st_ref, du_ref, dgl_ref, cos_ref, sin_ref, intra_ref, qd_ref, kd_ref,
             cd_ref, dp_ref, dstate_ref):
        dq_ref, dk_ref = dp_ref.at[:, pl.ds(C_RQ, H * dk)], dp_ref.at[:, pl.ds(C_RK, H * dk)]
        dv_ref, dg_ref = dp_ref.at[:, pl.ds(C_RV, H * dv)], dp_ref.at[:, pl.ds(C_RG, H * dv)]
        dp_ref[:, C_GL:] = dgl_ref[...]

        @pl.when(pl.program_id(1) == 0)
        def _():
            dstate_ref[...] = jnp.zeros_like(dstate_ref)

        cos, snb = cos_ref[...], -sin_ref[...]
        dstates = [dstate_ref[h] for h in range(H)]
        for ci in reversed(range(RET_CHUNKS)):
            r = slice(ci * CHUNK, (ci + 1) * CHUNK)
            for h in range(H):
                hk, hv = slice(h * dk, (h + 1) * dk), slice(h * dv, (h + 1) * dv)
                intra, qd, kd = intra_ref[h], qd_ref[h], kd_ref[h]
                qi, ki, vi = qr_ref[r, hk], kr_ref[r, hk], v_ref[r, hv]
                si = st_ref[0, h, ci]
                o = o_ref[r, hv].astype(f32)
                mu = jnp.mean(o, axis=-1, keepdims=True)
                xc = o - mu
                rstd = lax.rsqrt(jnp.mean(xc * xc, axis=-1, keepdims=True) + EPS)
                oh = xc * rstd
                g = g_ref[r, hv].astype(f32)
                sg = _sig(g)
                dui = du_ref[r, hv].astype(f32)
                dg_ref[r, hv] = (dui * oh * (sg * (1.0 + g * (1.0 - sg)))).astype(bf16)
                doh = dui * (g * sg)
                do = rstd * (doh - jnp.mean(doh, axis=-1, keepdims=True)
                             - oh * jnp.mean(doh * oh, axis=-1, keepdims=True))
                dob = do.astype(bf16)
                p = (_dot_nt(qi, ki) * intra).astype(bf16)
                dsb = dstates[h].astype(bf16)
                kt = (ki.astype(f32) * kd).astype(bf16)
                qt = (qi.astype(f32) * qd).astype(bf16)
                dv_ref[r, hv] = (_dot_tn(p, dob) + _dot(kt, dsb)).astype(bf16)
                da = (_dot_nt(dob, vi) * intra).astype(bf16)
                dq = _dot(da, ki) + _dot_nt(dob, si) * qd
                dkk = (_dot_tn(da, qi) + _dot_nt(vi, dsb) * kd) * K_SCALE
                dq_ref[r, hk] = _rotate(dq, cos[r], snb[r]).astype(bf16)
                dk_ref[r, hk] = _rotate(dkk, cos[r], snb[r]).astype(bf16)
                dstates[h] = dstates[h] * cd_ref[h] + _dot_tn(qt, dob)
        for h in range(H):
            dstate_ref[h] = dstates[h]

    blk = lambda w, c: pl.BlockSpec((sb, w), lambda b, i: (b * ns + ns - 1 - i, c))
    return _call(
        body, phase, name="ret_bwd", grid=(B, ns),
        in_specs=[blk(H * dk, 0), blk(H * dk, 0), blk(H * dv, C_RV // (H * dv)), blk(H * dv, C_RG // (H * dv)),
                  blk(H * dv, 0),
                  pl.BlockSpec((1, H, RET_CHUNKS, dk, dv), lambda b, i: (b, 0, ns - 1 - i, 0, 0)),
                  blk(H * dv, 0), blk(N_IN - C_GL, 0),
                  pl.BlockSpec((sb, dk), lambda b, i: (ns - 1 - i, 0)), pl.BlockSpec((sb, dk), lambda b, i: (ns - 1 - i, 0)),
                  *_ret_tables_specs()],
        out_specs=[blk(N_IN, 0)], out_shape=[jax.ShapeDtypeStruct((T, N_IN), bf16)],
        scratch_shapes=[pltpu.VMEM((H, dk, dv), f32)],
        args=(qr, kr, proj, proj, o, states, du, dgl, *rope, *decay))


def _att_bwd(proj, dao, trows, dproj, B, S, phase=None):
    T = B * S
    nq = S // QBLK
    dh = ATT_HEAD_DIM
    scale = ATT_HEAD_DIM ** -0.5

    def body(q_ref, k_ref, v_ref, do_ref, t_ref, _, dp_ref, vec_ref, bias_ref, dbias_ref, dka_ref, dva_ref):
        b, i = pl.program_id(0), pl.program_id(1)

        @pl.when((b == 0) & (i == 0))
        def _():
            _build_bias(t_ref, bias_ref)
            dbias_ref[...] = jnp.zeros_like(dbias_ref)

        @pl.when(i == 0)
        def _():
            dka_ref[...] = jnp.zeros_like(dka_ref)
            dva_ref[...] = jnp.zeros_like(dva_ref)

        def step(nk):
            win = pl.ds(pl.multiple_of((i + 1) * QBLK - nk, QBLK), nk)
            kw, vw = k_ref[win, :], v_ref[win, :]
            first = _first_of_pair()
            first_rows = lax.broadcasted_iota(jnp.int32, (2 * dh, 1), 0) < dh
            dqs, dks, dvs = [], [], []
            for p in range(ATT_HEADS // 2):
                ps = slice(2 * p * dh, 2 * (p + 1) * dh)
                q2, k2, v2, do2 = q_ref[:, ps] * ATT_SCALE, kw[:, ps], vw[:, ps], do_ref[:, ps]
                dq2, dk2, dv2 = [], [], []
                for e in range(2):
                    h = 2 * p + e
                    mine = first == (e == 0)
                    pr = _att_probs(jnp.where(mine, q2, jnp.zeros_like(q2)), k2, bias_ref[h, :, KWIN - nk:])
                    dp = _dot_nt(jnp.where(mine, do2, jnp.zeros_like(do2)), v2)
                    ds = pr * (dp - jnp.sum(pr * dp, axis=-1, keepdims=True))
                    dbias_ref[h, :, KWIN - nk:] += ds
                    dsb = ds.astype(bf16)
                    dq2.append(_dot(dsb, k2) * ATT_SCALE)
                    dk2.append(_dot_tn(q2, dsb))
                    dv2.append(_dot_tn(do2, pr.astype(bf16)))
                dqs.append(jnp.where(first, dq2[0], dq2[1]))
                dks.append(jnp.where(first_rows, dk2[0], dk2[1]))
                dvs.append(jnp.where(first_rows, dv2[0], dv2[1]))
            dp_ref[pl.ds(pl.multiple_of(i * QBLK, QBLK), QBLK), :ATT_W] = jnp.concatenate(dqs, axis=1).astype(bf16)
            dka_ref[:, win] += jnp.concatenate(dks, axis=0)
            dva_ref[:, win] += jnp.concatenate(dvs, axis=0)

        _by_window(i, step)

        @pl.when(i == nq - 1)
        def _():
            dp_ref[:, ATT_W:2 * ATT_W] = dka_ref[...].T.astype(bf16)
            dp_ref[:, 2 * ATT_W:] = dva_ref[...].T.astype(bf16)

        @pl.when((b == B - 1) & (i == nq - 1))
        def _():
            rr = lax.broadcasted_iota(jnp.int32, (QBLK, QBLK), 0)
            cc = lax.broadcasted_iota(jnp.int32, (QBLK, QBLK), 1)
            flip = jnp.where(rr + cc == QBLK - 1, 1.0, 0.0).astype(bf16)
            for h in range(ATT_HEADS):
                d = dbias_ref[h]
                hi = d.astype(bf16)
                lo = (d - hi.astype(f32)).astype(bf16)
                rev = _dot(flip, hi) + _dot(flip, lo)
                wide = jnp.concatenate([rev, jnp.zeros((QBLK, TOEP - KWIN), f32)], axis=1)
                rolled = pltpu.roll(wide, 0, 1, stride=1, stride_axis=0)
                vec_ref[h:h + 1, :] = jnp.sum(rolled, axis=0, keepdims=True)

    qspec = lambda c: pl.BlockSpec((QBLK, ATT_W), lambda b, i: (b * nq + i, c))
    kspec = lambda c: pl.BlockSpec((S, ATT_W), lambda b, i: (b, c))
    return _call(
        body, phase, name="att_bwd", grid=(B, nq), aliases={5: 0},
        in_specs=[qspec(C_AQ // ATT_W), kspec(C_AK // ATT_W), kspec(C_AV // ATT_W), qspec(0),
                  pl.BlockSpec((ATT_HEADS, TOEP), lambda b, i: (0, 0)), pl.BlockSpec(memory_space=pl.ANY)],
        out_specs=[pl.BlockSpec((S, 3 * ATT_W), lambda b, i: (b, C_AQ // (3 * ATT_W))),
                   pl.BlockSpec((ATT_HEADS, TOEP), lambda b, i: (0, 0))],
        out_shape=[jax.ShapeDtypeStruct((T, N_IN), bf16), jax.ShapeDtypeStruct((ATT_HEADS, TOEP), f32)],
        scratch_shapes=[pltpu.VMEM((ATT_HEADS, QBLK, KWIN), f32), pltpu.VMEM((ATT_HEADS, QBLK, KWIN), f32),
                        pltpu.VMEM((ATT_W, S), f32), pltpu.VMEM((ATT_W, S), f32)],
        args=(proj, proj, proj, dao, trows, dproj))


def _in_proj_bwd(dproj, w_in, x2, gamma, dh1, phase=None):
    T, D = x2.shape
    nk, _, tk = w_in.shape
    tm = _tile(T, BIG_ROW_TILE, 8)

    def body(dp_ref, w_ref, x_ref, g_ref, dh1_ref, dx_ref, part_ref, acc_ref):
        j = pl.program_id(1)

        @pl.when(j == 0)
        def _():
            acc_ref[...] = jnp.zeros_like(acc_ref)

        acc_ref[...] += _dot_nt(dp_ref[...], w_ref[0])

        @pl.when(j == nk - 1)
        def _():
            x = x_ref[...]
            r = lax.rsqrt(jnp.mean(x * x, axis=-1, keepdims=True) + EPS)
            n = x * r
            dxn = acc_ref[...]
            dn = dxn * g_ref[...]
            dx_ref[...] = dh1_ref[...] + r * (dn - n * jnp.mean(dn * n, axis=-1, keepdims=True))
            part_ref[...] = jnp.zeros_like(part_ref)
            part_ref[0:1, :] = jnp.sum(dxn * n, axis=0, keepdims=True)

    row = lambda n: pl.BlockSpec((tm, n), lambda i, j: (i, 0))
    return _call(
        body, phase, name="in_proj_bwd", grid=(T // tm, nk),
        in_specs=[pl.BlockSpec((tm, tk), lambda i, j: (i, j)), pl.BlockSpec((1, D, tk), lambda i, j: (j, 0, 0)), row(D),
                  pl.BlockSpec((1, D), lambda i, j: (0, 0)), row(D)],
        out_specs=[row(D), pl.BlockSpec((8, D), lambda i, j: (i, 0))],
        out_shape=[jax.ShapeDtypeStruct((T, D), f32), jax.ShapeDtypeStruct((T // tm * 8, D), f32)],
        scratch_shapes=[pltpu.VMEM((tm, D), f32)],
        args=(dproj, w_in, x2, gamma, dh1))


def _wgrad(a, b, shard_axis, name, phase=None):
    def spec(arr, sharded, tt):
        if arr.ndim == 3:
            return arr.shape[2], pl.BlockSpec((1, tt, arr.shape[2]), lambda s, t: (s, t, 0))
        if sharded:
            w = arr.shape[1] // N_CHIPS
            return w, pl.BlockSpec((tt, w), lambda s, t: (t, s))
        return arr.shape[1], pl.BlockSpec((tt, arr.shape[1]), lambda s, t: (t, 0))

    T = a.shape[-2]
    whole = a.ndim == 2 and b.ndim == 2 and a.shape[1] * b.shape[1] * 4 <= WGRAD_ACC_BYTES
    width = lambda arr, sharded: arr.shape[-1] // (1 if whole or arr.ndim == 3 or not sharded else N_CHIPS)
    wa, wb_ = width(a, shard_axis == 0), width(b, shard_axis == 1)
    fixed = wa * wb_ * (4 + 2 * 2)
    tt = T // 4 if whole else T
    while tt > 256 and 2 * tt * (wa * a.dtype.itemsize + wb_ * b.dtype.itemsize) + fixed > WGRAD_VMEM_BYTES:
        tt //= 2
    nt = T // tt
    if whole:
        K, N = a.shape[1], b.shape[1]
        a_spec, b_spec = pl.BlockSpec((tt, K), lambda s, t: (t, 0)), pl.BlockSpec((tt, N), lambda s, t: (t, 0))
        out_block = (N_CHIPS, K // N_CHIPS, N) if shard_axis == 0 else (N_CHIPS, K, N // N_CHIPS)
        out_spec = pl.BlockSpec(out_block, lambda s, t: (0, 0, 0))
    else:
        K, a_spec = spec(a, shard_axis == 0, tt)
        N, b_spec = spec(b, shard_axis == 1, tt)
        out_block = (N_CHIPS, K, N)
        out_spec = pl.BlockSpec((1, K, N), lambda s, t: (s, 0, 0))

    def body(a_ref, b_ref, o_ref, acc_ref):
        t = pl.program_id(1)

        @pl.when(t == 0)
        def _():
            acc_ref[...] = jnp.zeros_like(acc_ref)

        av = a_ref[0] if a.ndim == 3 else a_ref[...]
        bv = b_ref[0] if b.ndim == 3 else b_ref[...]
        acc_ref[...] += _dot_tn(av.astype(bf16), bv.astype(bf16))

        @pl.when(t == nt - 1)
        def _():
            if not whole:
                o_ref[0] = acc_ref[...].astype(bf16)
            else:
                _, kk, nn = out_block
                for s in range(N_CHIPS):
                    o_ref[s] = (acc_ref[s * kk:(s + 1) * kk, :] if shard_axis == 0
                                else acc_ref[:, s * nn:(s + 1) * nn]).astype(bf16)

    (grad,), carried = _call(
        body, phase, name=name, grid=(1 if whole else N_CHIPS, nt), in_specs=[a_spec, b_spec], out_specs=[out_spec],
        out_shape=[jax.ShapeDtypeStruct(out_block, bf16)], scratch_shapes=[pltpu.VMEM((K, N), f32)], args=(a, b))
    return grad, carried


def _adamw_sum(place, groups, name):
    n = len(groups)
    R, C = groups[0][0].shape
    half = R // 2
    tr = _tile(half, max(16, (1 << 18) // C // 16 * 16), 16)
    nr = half // tr

    def body(p_ref, *refs):
        for a in range(n):
            w_ref, m_ref, v_ref, part_ref, fc_ref, fs_ref = refs[6 * a:6 * a + 6]
            g_ref, d_ref, mo_ref, vo_ref = refs[6 * n + 4 * a:6 * n + 4 * a + 4]
            up = lambda x: x.astype(f32)
            mine = ((up(part_ref[0]) + up(fc_ref[0])) + up(fc_ref[1])) + up(fc_ref[2])
            sibs = ((up(fs_ref[0]) + up(fs_ref[1])) + up(fs_ref[2])) + up(fs_ref[3])
            g_ = jnp.where(pl.program_id(0) == p_ref[0], mine, sibs)
            m_ = ADAM_B1 * m_ref[...] + (1.0 - ADAM_B1) * g_
            v_ = ADAM_B2 * v_ref[...] + (1.0 - ADAM_B2) * (g_ * g_)
            m_hat = m_ / (1.0 - ADAM_B1 ** ADAM_STEP)
            v_hat = v_ / (1.0 - ADAM_B2 ** ADAM_STEP)
            g_ref[...] = g_
            d_ref[...] = -ADAM_LR * (m_hat / (jnp.sqrt(v_hat) + ADAM_EPS) + ADAM_WD * w_ref[...])
            mo_ref[...] = m_
            vo_ref[...] = v_

    spec = pl.BlockSpec((tr, C), lambda h, r, p: (h * nr + r, 0))
    one = [spec, spec, spec, pl.BlockSpec((1, tr, C), lambda h, r, p: (p[1], jnp.where(h == p[0], r, 0), 0)),
           pl.BlockSpec((3, tr, C), lambda h, r, p: (0, jnp.where(h == p[0], r, 0), 0)),
           pl.BlockSpec((4, tr, C), lambda h, r, p: (0, jnp.where(h == p[0], 0, r), 0))]
    res = pl.pallas_call(
        body, name=name,
        grid_spec=pltpu.PrefetchScalarGridSpec(num_scalar_prefetch=1, grid=(2, nr), in_specs=one * n,
                                               out_specs=[spec] * (4 * n)),
        out_shape=[jax.ShapeDtypeStruct((R, C), f32)] * (4 * n),
        compiler_params=_params(("parallel", "parallel")),
    )(place, *[x for g in groups for x in g])
    return [tuple(res[4 * a:4 * a + 4]) for a in range(n)]


def _adamw(w, g, m, v, name):
    R, C = w.shape
    tr = _tile(R, max(8, (1 << 18) // C // 8 * 8), 8)

    def body(w_ref, g_ref, m_ref, v_ref, d_ref, mo_ref, vo_ref):
        g_ = g_ref[...]
        m_ = ADAM_B1 * m_ref[...] + (1.0 - ADAM_B1) * g_
        v_ = ADAM_B2 * v_ref[...] + (1.0 - ADAM_B2) * (g_ * g_)
        m_hat = m_ / (1.0 - ADAM_B1 ** ADAM_STEP)
        v_hat = v_ / (1.0 - ADAM_B2 ** ADAM_STEP)
        d_ref[...] = -ADAM_LR * (m_hat / (jnp.sqrt(v_hat) + ADAM_EPS) + ADAM_WD * w_ref[...])
        mo_ref[...] = m_
        vo_ref[...] = v_

    spec = pl.BlockSpec((tr, C), lambda i: (i, 0))
    return pl.pallas_call(
        body, name=name, grid=(R // tr,), in_specs=[spec] * 4, out_specs=[spec] * 3,
        out_shape=[jax.ShapeDtypeStruct((R, C), f32)] * 3,
        compiler_params=_params(("parallel",)),
    )(w, g, m, v)


def _place():
    return lax.axis_index("x"), lax.axis_index("y"), lax.axis_index("c")


def _other_chips(x, y):
    chips = [(1 - x, y), (x, 1 - y), (1 - x, 1 - y)]
    return chips, [2 * cx + cy for cx, cy in chips]


def _spread_phase(blk):
    def peers():
        x, y, c = _place()
        return [tuple(1 - p if (k >> s) & 1 else p for p, s in ((x, 2), (y, 1), (c, 0))) for k in range(1, N_DEV)]

    def copies(pin, out):
        x, y, c = _place()
        mine = out[0].at[4 * x + 2 * y + c]
        return [(mine, mine, peer) for peer in peers()]

    stack = jnp.broadcast_to(blk, (N_DEV,) + blk.shape)
    return _Phase([stack], [jax.ShapeDtypeStruct(stack.shape, stack.dtype)], {0: 0}, N_DEV - 1, copies,
                  lambda pin, out: [out[0].at[4 * px + 2 * py + pc] for px, py, pc in peers()])


def _sum_slots(stack, name):
    def body(s_ref, o_ref):
        tot = s_ref[0]
        for d in range(1, stack.shape[0]):
            tot = tot + s_ref[d]
        o_ref[...] = tot

    vm = pl.BlockSpec(memory_space=pltpu.VMEM)
    return pl.pallas_call(body, name=name, in_specs=[vm], out_specs=vm,
                          out_shape=jax.ShapeDtypeStruct(stack.shape[1:], stack.dtype))(stack)


def _cast_shards(place, ws, name):
    n = len(ws)
    R, C = ws[0].shape
    tr = _tile(R, max(16, (1 << 19) // C // 16 * 16), 16)

    def body(p_ref, *refs):
        for a in range(n):
            refs[n + a][0] = refs[a][...].astype(bf16)

    return pl.pallas_call(
        body, name=name,
        grid_spec=pltpu.PrefetchScalarGridSpec(
            num_scalar_prefetch=1, grid=(R // tr,),
            in_specs=[pl.BlockSpec((tr, C), lambda r, p: (r, 0))] * n,
            out_specs=[pl.BlockSpec((1, tr, C), lambda r, p: (p[1], r, 0))] * n),
        out_shape=[jax.ShapeDtypeStruct((N_CHIPS, R, C), bf16)] * n,
        compiler_params=_params(("parallel",)),
    )(place, *ws)


class _Phase:
    def __init__(self, arrays, out_shapes, aliases, n_copies, copies, arrivals, own_starts=(), own_waits=()):
        self.arrays, self.out_shapes, self.aliases = list(arrays), list(out_shapes), dict(aliases)
        self.n_copies, self.copies, self.arrivals = n_copies, copies, arrivals
        self.own_starts, self.own_waits = tuple(own_starts), tuple(own_waits)

    def sems(self):
        return [pltpu.SemaphoreType.DMA((self.n_copies,)), pltpu.SemaphoreType.DMA((self.n_copies,))]

    def _descriptors(self, pin, pout, send_sems, recv_sems):
        return [pltpu.make_async_remote_copy(src_ref=s, dst_ref=d, send_sem=send_sems.at[i], recv_sem=recv_sems.at[i],
                                             device_id=to, device_id_type=MESH)
                for i, (s, d, to) in enumerate(self.copies(pin, pout))]

    def _arrival(self, i, pin, pout, send_sems, recv_sems):
        dst = self.arrivals(pin, pout)[i]
        return pltpu.make_async_remote_copy(src_ref=dst, dst_ref=dst, send_sem=send_sems.at[i], recv_sem=recv_sems.at[i],
                                            device_id=_place(), device_id_type=MESH)

    def start(self, pin, pout, send_sems, recv_sems):
        for i, cp in enumerate(self._descriptors(pin, pout, send_sems, recv_sems)):
            if i not in self.own_starts:
                cp.start()

    def begin(self, i, pin, pout, send_sems, recv_sems):
        self._descriptors(pin, pout, send_sems, recv_sems)[i].start()

    def arrived(self, i, pin, pout, send_sems, recv_sems):
        self._arrival(i, pin, pout, send_sems, recv_sems).wait_recv()

    def finish(self, pin, pout, send_sems, recv_sems):
        for i in range(self.n_copies):
            if i not in self.own_waits:
                self._arrival(i, pin, pout, send_sems, recv_sems).wait_recv()
        for cp in self._descriptors(pin, pout, send_sems, recv_sems):
            cp.wait_send()


def _join(phases):
    if len(phases) == 1:
        return phases[0]
    ai = np.cumsum([0] + [len(p.arrays) for p in phases])
    oi = np.cumsum([0] + [len(p.out_shapes) for p in phases])

    def each(fn_name, pin, pout):
        return [item for k, p in enumerate(phases)
                for item in getattr(p, fn_name)(pin[ai[k]:ai[k + 1]], pout[oi[k]:oi[k + 1]])]

    aliases = {int(ai[k]) + i: int(oi[k]) + j for k, p in enumerate(phases) for i, j in p.aliases.items()}
    ci = np.cumsum([0] + [p.n_copies for p in phases])
    shifted = lambda attr: [int(ci[k]) + i for k, p in enumerate(phases) for i in getattr(p, attr)]
    return _Phase([a for p in phases for a in p.arrays], [s for p in phases for s in p.out_shapes], aliases,
                  int(ci[-1]), functools.partial(each, "copies"), functools.partial(each, "arrivals"),
                  shifted("own_starts"), shifted("own_waits"))


def _call(body, phase, *, name, grid, in_specs, out_specs, out_shape, scratch_shapes, args, prefetch=(), expose=False,
          aliases=None):
    seq = _params(("arbitrary",) * len(grid))
    np_ = len(prefetch)
    own = {np_ + i: j for i, j in (aliases or {}).items()}
    if phase is None:
        spec = pltpu.PrefetchScalarGridSpec(num_scalar_prefetch=np_, grid=grid, in_specs=in_specs, out_specs=out_specs,
                                            scratch_shapes=scratch_shapes)
        res = pl.pallas_call(body, name=name, grid_spec=spec, out_shape=out_shape, input_output_aliases=own,
                             compiler_params=seq)(*prefetch, *args)
        return list(res), []
    ni, no, ns = len(in_specs), len(out_specs), len(scratch_shapes)
    pi, po = len(phase.arrays), len(phase.out_shapes)

    def hosted(*refs):
        cut = np.cumsum([np_, ni, pi, no, po, ns])
        pre, ins, pin, outs, pout, scr, sems = (refs[a:b] for a, b in zip([0, *cut], [*cut, len(refs)]))
        ids = [pl.program_id(d) for d in range(len(grid))]
        first = functools.reduce(lambda p, q: p & q, [i == 0 for i in ids])
        last = functools.reduce(lambda p, q: p & q, [i == g - 1 for i, g in zip(ids, grid)])
        pl.when(first)(lambda: phase.start(pin, pout, *sems))
        body(*pre, *ins, *outs, *scr, **({"carried": (pin, pout, sems)} if expose else {}))
        pl.when(last)(lambda: phase.finish(pin, pout, *sems))

    anyspace = pl.BlockSpec(memory_space=pl.ANY)
    spec = pltpu.PrefetchScalarGridSpec(
        num_scalar_prefetch=np_, grid=grid, in_specs=list(in_specs) + [anyspace] * pi,
        out_specs=list(out_specs) + [anyspace] * po, scratch_shapes=list(scratch_shapes) + phase.sems())
    res = pl.pallas_call(
        hosted, name=name, grid_spec=spec, out_shape=list(out_shape) + phase.out_shapes,
        input_output_aliases={**own, **{np_ + ni + i: no + j for i, j in phase.aliases.items()}}, compiler_params=seq,
    )(*prefetch, *args, *phase.arrays)
    return list(res[:no]), list(res[no:])


def _run_phases(name, phases):
    first = phases[0]
    pi, po = len(first.arrays), len(first.out_shapes)

    def body(*refs):
        pin, pout, sems = refs[:pi], refs[pi:pi + po], refs[pi + po:]
        for n, ph in enumerate(phases):
            ph.start(pin, pout, *sems[2 * n:2 * n + 2])
            ph.finish(pin, pout, *sems[2 * n:2 * n + 2])

    anyspace = pl.BlockSpec(memory_space=pl.ANY)
    return list(pl.pallas_call(
        body, name=name, in_specs=[anyspace] * pi, out_specs=[anyspace] * po, out_shape=first.out_shapes,
        input_output_aliases=first.aliases, scratch_shapes=[s for ph in phases for s in ph.sems()],
    )(*first.arrays))


def _half_rows(buf, c):
    half = buf.shape[1] // 2
    return pl.ds(c * half, half), pl.ds((1 - c) * half, half)


def _gather_phase(bufs, over_ici):
    n = len(bufs)
    shapes = [jax.ShapeDtypeStruct(b.shape, b.dtype) for b in bufs]

    def landed(out, which):
        x, y, c = _place()
        _, ks = _other_chips(x, y)
        return [out[a].at[ks[j], _half_rows(bufs[a], c)[which]] for a in range(n) for j in range(3)]

    def ici(pin, out):
        x, y, c = _place()
        chips, _ = _other_chips(x, y)
        mine = [out[a].at[2 * x + y, _half_rows(bufs[a], c)[0]] for a in range(n)]
        return [(mine[a], mine[a], (*chips[j], c)) for a in range(n) for j in range(3)]

    def d2d(pin, out):
        x, y, c = _place()
        return [(dst, dst, (x, y, 1 - c)) for dst in landed(out, 0)]

    if over_ici:
        return _Phase(bufs, shapes, {a: a for a in range(n)}, 3 * n, ici, lambda pin, out: landed(out, 0))
    return _Phase(bufs, shapes, {a: a for a in range(n)}, 3 * n, d2d, lambda pin, out: landed(out, 1))


def _feed_phase(buf):
    def chips():
        x, y, c = _place()
        return [(x if f < 2 else 1 - x, y if f % 2 == 0 else 1 - y) for f in (1, 2, 3)]

    def copies(pin, out):
        x, y, c = _place()
        mine = _half_rows(buf, c)[0]
        own = out[0].at[2 * x + y, mine]
        sent = [(own, own, (cx, cy, c)) for cx, cy in chips()]
        return sent + [(out[0].at[2 * cx + cy, mine], out[0].at[2 * cx + cy, mine], (x, y, 1 - c)) for cx, cy in chips()]

    def arrivals(pin, out):
        x, y, c = _place()
        mine, theirs = _half_rows(buf, c)
        return [out[0].at[2 * cx + cy, rows] for rows in (mine, theirs) for cx, cy in chips()]

    return _Phase([buf], [jax.ShapeDtypeStruct(buf.shape, buf.dtype)], {0: 0}, 6, copies, arrivals,
                  own_starts=(3, 4, 5), own_waits=range(6))


def _rs_swap_phase(grads):
    n = len(grads)

    def copies(g, out):
        x, y, c = _place()
        return [(g[a].at[:, _half_rows(grads[a], c)[1]], out[a], (x, y, 1 - c)) for a in range(n)]

    shapes = [jax.ShapeDtypeStruct((N_CHIPS, g.shape[1] // 2, g.shape[2]), g.dtype) for g in grads]
    return _Phase(grads, shapes, {}, n, copies, lambda g, out: list(out))


def _rs_add_sibling(place, grads, gots, name):
    n = len(grads)
    _, R, C = grads[0].shape
    half = R // 2
    tr = _tile(half, max(16, (1 << 19) // C // 16 * 16), 16)
    nr = half // tr

    def body(p_ref, *refs):
        for a in range(n):
            refs[2 * n + a][...] = (refs[2 * a][...].astype(f32) + refs[2 * a + 1][...].astype(f32)).astype(bf16)

    res = pl.pallas_call(
        body, name=name,
        grid_spec=pltpu.PrefetchScalarGridSpec(
            num_scalar_prefetch=1, grid=(N_CHIPS, nr),
            in_specs=[pl.BlockSpec((1, tr, C), lambda k, r, p: (k, p[0] * nr + r, 0)),
                      pl.BlockSpec((1, tr, C), lambda k, r, p: (k, r, 0))] * n,
            out_specs=[pl.BlockSpec((1, tr, C), lambda k, r, p: (k, r, 0))] * n),
        out_shape=[jax.ShapeDtypeStruct((N_CHIPS, half, C), bf16)] * n,
        compiler_params=_params(("parallel", "parallel")),
    )(place, *[x for pair in zip(grads, gots) for x in pair])
    return list(res)


def _rs_chips_phase(parts):
    n = len(parts)

    def copies(p, fc):
        x, y, c = _place()
        chips, ks = _other_chips(x, y)
        return [(p[a].at[ks[j]], fc[a].at[j], (*chips[j], c)) for a in range(n) for j in range(3)]

    shapes = [jax.ShapeDtypeStruct((3,) + q.shape[1:], q.dtype) for q in parts]
    return _Phase(parts, shapes, {}, 3 * n, copies, lambda p, fc: [fc[a].at[j] for a in range(n) for j in range(3)])


def _rs_hand_phase(parts, from_chips):
    n = len(parts)

    def copies(pin, fs):
        x, y, c = _place()
        sib = (x, y, 1 - c)
        own = [(pin[a].at[2 * x + y], fs[a].at[0], sib) for a in range(n)]
        return own + [(pin[n + a].at[j], fs[a].at[1 + j], sib) for a in range(n) for j in range(3)]

    def arrivals(pin, fs):
        return [fs[a].at[0] for a in range(n)] + [fs[a].at[1 + j] for a in range(n) for j in range(3)]

    shapes = [jax.ShapeDtypeStruct((4,) + q.shape[1:], q.dtype) for q in parts]
    return _Phase(list(parts) + list(from_chips), shapes, {}, 4 * n, copies, arrivals)


class _Exchange:
    def __init__(self, place):
        self.place = place

    def feed(self, buf):
        return _feed_phase(buf)

    def gather(self, bufs, over_ici):
        return _gather_phase(bufs, over_ici)

    def swap(self, grads):
        return _rs_swap_phase(grads)

    def pair_sums(self, names, grads):
        return self.add(names, grads, _run_phases("rs_sibling_" + names[0], [_rs_swap_phase(grads)]))

    def add(self, names, grads, got):
        parts = {}
        for group in _same_shape(grads):
            res = _rs_add_sibling(self.place, [grads[i] for i in group], [got[i] for i in group],
                                  "rs_add_" + names[group[0]])
            parts.update(zip(group, res))
        return [parts[i] for i in range(len(names))]

    def to_chips(self, parts):
        return _rs_chips_phase(parts)

    def to_sibling(self, parts, from_chips):
        return _rs_hand_phase(parts, from_chips)

    def spread(self, blk):
        return _spread_phase(blk)

    def hand_over(self, name, parts, from_chips, blk):
        got = _run_phases(name, [_join([_rs_hand_phase(parts, from_chips), _spread_phase(blk)])])
        return got[:-1], got[-1]


def _local_step(place, x, target, norm_mix, b_gate, rb_chip, norm_ffn, norm_final, w_in, rest, exch):
    B, S, D = x.shape
    T = B * S
    x2 = x.reshape(T, D)
    tg2 = target.reshape(T, D)
    rope, decay = _rope_tables(S), _decay_tables()
    g_fin = norm_final.reshape(1, D)
    nrel = rb_chip.shape[-1]

    mrg, ffn = ["w_ret_out", "w_att_out", "w_out"], ["w_ffn_gate", "w_ffn_up", "w_ffn_down"]
    (xn, proj), got = _in_proj(place, x2, norm_mix, _join([exch.feed(w_in), exch.gather([rest[n] for n in mrg], True),
                                                           exch.spread(jnp.pad(rb_chip, ((0, 0), (0, 128 - nrel))))]))
    w_in, wb, rb_all = got[0], {}, got.pop()
    trows = _bias_rows(jnp.concatenate([rb_all[2 * k, :, :nrel] for k in range(N_CHIPS)], axis=1))
    (qr, kr, o, u, states), got = _ret_fwd(proj, B, S, rope, decay, _join([exch.gather([rest["w_ffn_gate"]], True),
                                                                         exch.gather(got[1:], False)]))
    wb.update(zip(mrg, got[1:]))
    (ao,), got = _att_fwd(proj, trows, B, S, _join([exch.gather([rest["w_ffn_up"], rest["w_ffn_down"]], True),
                                                    exch.gather(got[:1], False)]))
    wb["w_ffn_gate"] = got[2]
    w_ro, w_out = wb["w_ret_out"].reshape(-1, D), wb["w_out"].reshape(-1, D)
    (h1, yr, ya), got = _mix_fwd(x2, proj, u, ao, b_gate, w_ro, wb["w_att_out"], w_out, exch.gather(got[:2], False))
    wb.update(zip(ffn[1:], got))
    hn, a, b, f, dh2, part_fin = _ffn_fwd(h1, norm_ffn, wb["w_ffn_gate"], wb["w_ffn_up"], wb["w_ffn_down"], g_fin, tg2)

    da, db, dh1, part_ffn = _ffn_bwd(dh2, h1, norm_ffn, a, b, wb["w_ffn_gate"], wb["w_ffn_up"], wb["w_ffn_down"])
    ffn = ["w_ffn_down", "w_ffn_gate", "w_ffn_up"]
    g_ffn = [_wgrad(f, dh2, 0, "wgrad_ffn_down")[0], _wgrad(da, hn, 0, "wgrad_ffn_gate")[0],
             _wgrad(db, hn, 0, "wgrad_ffn_up")[0]]
    (du, dao, dgl, mix, dyr, dya, part_bg), x_ffn = _mix_bwd(dh1, proj, yr, ya, b_gate, w_ro, wb["w_att_out"], w_out,
                                                             exch.swap(g_ffn))
    p_ffn = exch.add(ffn, g_ffn, x_ffn)
    mrg = ["w_out", "w_ret_out", "w_att_out"]
    g_mrg = [_wgrad(mix, dh1, 0, "wgrad_out")[0], _wgrad(u, dyr, 0, "wgrad_ret_out")[0],
             _wgrad(ao, dya, 1, "wgrad_att_out")[0]]
    (dproj,), got = _ret_bwd(proj, qr, kr, o, states, du, dgl, B, S, rope, decay, _join([exch.to_chips(p_ffn[:2]),
                                                                                          exch.swap(g_mrg)]))
    c_two, p_mrg = got[:2], exch.add(mrg, g_mrg, got[2:])
    (dproj, dvec), got = _att_bwd(proj, dao, trows, dproj, B, S, exch.to_chips(p_ffn[2:] + p_mrg))
    c_ffn, c_mrg = c_two + got[:1], got[1:]
    g_in, got = _wgrad(xn, dproj, 1, "wgrad_in", exch.to_sibling(p_ffn + p_mrg, c_ffn + c_mrg))
    s_ffn, s_mrg = got[:len(ffn)], got[len(ffn):]
    p_in = exch.pair_sums(["w_in"], [g_in])
    (gx, part_mix), c_in = _in_proj_bwd(dproj, w_in, x2, norm_mix, dh1, exch.to_chips(p_in))
    rows = lambda p, r: p.reshape(-1, 8, p.shape[-1])[:, r, :].sum(axis=0)
    lo = KWIN - 1 - (MAX_REL - 1)
    drb = jnp.concatenate([jnp.flip(dvec[:, lo:lo + N_REL - 1], axis=1), dvec[:, :lo].sum(axis=1, keepdims=True)], axis=1)
    gsmall = {
        "norm_mix": rows(part_mix, 0), "b_gate": rows(part_bg, 0), "rel_bias": drb, "norm_ffn": rows(part_ffn, 0),
        "norm_final": rows(part_fin, 0),
    }
    s_in, small_all = exch.hand_over("rs_hand_w_in", p_in, c_in, _pack_small(gsmall, rows(part_fin, 1)))
    gbig = dict(zip(ffn + mrg + ["w_in"], zip(p_ffn + p_mrg + p_in, c_ffn + c_mrg + c_in, s_ffn + s_mrg + s_in)))
    return gx.reshape(B, S, D), gbig, small_all


SMALL_ROWS = 16


def _pack_small(gs, loss_lanes):
    D = D_MODEL
    rb = jnp.pad(gs["rel_bias"].reshape(-1), (0, 3 * D - ATT_HEADS * N_REL)).reshape(3, D)
    rows = [gs["norm_mix"].reshape(1, D), gs["b_gate"].reshape(2, D), gs["norm_ffn"].reshape(1, D),
            gs["norm_final"].reshape(1, D), rb, loss_lanes.reshape(1, D)]
    used = sum(r.shape[0] for r in rows)
    return jnp.concatenate(rows + [jnp.zeros((SMALL_ROWS - used, D), f32)], axis=0)


def kernel(x, norm_mix, w_in, b_gate, rel_bias, w_ret_out, w_att_out, w_out, norm_ffn, w_ffn_gate, w_ffn_up, w_ffn_down, norm_final, loss_target, m_norm_mix, m_w_in, m_b_gate, m_rel_bias, m_w_ret_out, m_w_att_out, m_w_out, m_norm_ffn, m_w_ffn_gate, m_w_ffn_up, m_w_ffn_down, m_norm_final, v_norm_mix, v_w_in, v_b_gate, v_rel_bias, v_w_ret_out, v_w_att_out, v_w_out, v_norm_ffn, v_w_ffn_gate, v_w_ffn_up, v_w_ffn_down, v_norm_final):
    w = dict(norm_mix=norm_mix, w_in=w_in, b_gate=b_gate, rel_bias=rel_bias, w_ret_out=w_ret_out, w_att_out=w_att_out,
             w_out=w_out, norm_ffn=norm_ffn, w_ffn_gate=w_ffn_gate, w_ffn_up=w_ffn_up, w_ffn_down=w_ffn_down,
             norm_final=norm_final)
    m = dict(norm_mix=m_norm_mix, w_in=m_w_in, b_gate=m_b_gate, rel_bias=m_rel_bias, w_ret_out=m_w_ret_out,
             w_att_out=m_w_att_out, w_out=m_w_out, norm_ffn=m_norm_ffn, w_ffn_gate=m_w_ffn_gate, w_ffn_up=m_w_ffn_up,
             w_ffn_down=m_w_ffn_down, norm_final=m_norm_final)
    v = dict(norm_mix=v_norm_mix, w_in=v_w_in, b_gate=v_b_gate, rel_bias=v_rel_bias, w_ret_out=v_w_ret_out,
             w_att_out=v_w_att_out, w_out=v_w_out, norm_ffn=v_norm_ffn, w_ffn_gate=v_w_ffn_gate, w_ffn_up=v_w_ffn_up,
             w_ffn_down=v_w_ffn_down, norm_final=v_norm_final)
    xi, yi, ci = _place()
    k_me = 2 * xi + yi

    place = jnp.stack([ci, k_me]).astype(jnp.int32)
    big = [n for n, _ in BIG]

    turned = ("w_ffn_gate", "w_ffn_up")
    shard = lambda d, n: jnp.swapaxes(d[n][0], 0, 1) if n in turned else d[n][0]
    whole = lambda a, n: (jnp.swapaxes(a, 0, 1) if n in turned else a)[None]

    by_shape = [[big[i] for i in group] for group in _same_shape([shard(w, n) for n in big])]
    bufs = {}
    for names in by_shape:
        bufs.update(zip(names, _cast_shards(place, [shard(w, n) for n in names], "cast_" + names[0])))
    rest = {n: bufs[n] for n in big if n != "w_in"}
    nrel_loc = rel_bias.shape[-1]
    grad_x, gbig, small_all = _local_step(place, x, loss_target, norm_mix, b_gate, rel_bias[0], norm_ffn, norm_final,
                                          bufs["w_in"], rest, _Exchange(place))

    small = _sum_slots(small_all, "reduce_small")
    D = D_MODEL
    loss = jnp.sum(small[8])
    drb_full = small[5:8].reshape(-1)[:ATT_HEADS * N_REL].reshape(ATT_HEADS, N_REL)
    g = {
        "norm_mix": small[0:1], "b_gate": small[1:3].reshape(1, 2 * D), "norm_ffn": small[3:4], "norm_final": small[4],
        "rel_bias": lax.dynamic_slice_in_dim(drb_full, k_me * nrel_loc, nrel_loc, axis=1)[None],
    }

    delta, new_m, new_v = {}, {}, {}
    for names in by_shape:
        res = _adamw_sum(place, [(shard(w, n), shard(m, n), shard(v, n), *gbig[n]) for n in names], "adamw_" + names[0])
        for n, (g_, d_, m_, v_) in zip(names, res):
            g[n], delta[n], new_m[n], new_v[n] = whole(g_, n), whole(d_, n), whole(m_, n), whole(v_, n)
    flat = lambda d: jnp.concatenate([d[n].reshape(-1) for n in SMALL])
    n_small = sum(int(np.prod(w[n].shape)) for n in SMALL)
    n_pad = -n_small % 1024
    packs = [jnp.pad(flat(d), (0, n_pad)).reshape(-1, 128) for d in (w, g, m, v)]
    outs = _adamw(*packs, "adamw_small")
    for res, dst in zip(outs, (delta, new_m, new_v)):
        off = 0
        fl = res.reshape(-1)
        for n in SMALL:
            sz = int(np.prod(w[n].shape))
            dst[n] = fl[off:off + sz].reshape(w[n].shape)
            off += sz

    return (loss, grad_x, *[g[n] for n in WEIGHTS], *[delta[n] for n in WEIGHTS], *[new_m[n] for n in WEIGHTS],
            *[new_v[n] for n in WEIGHTS])
```

```python
import functools

import numpy as np
import jax
import jax.numpy as jnp
from jax import lax
from jax.experimental import pallas as pl
from jax.experimental.pallas import tpu as pltpu

f32 = jnp.float32
bf16 = jnp.bfloat16

D_MODEL = 1024
CHUNK = 64
RET_HEADS = 4
RET_KEY_DIM = 128
RET_VAL_DIM = 256
ATT_HEADS = 8
ATT_HEAD_DIM = 64
ATT_W = ATT_HEADS * ATT_HEAD_DIM
BAND_CHUNKS = 8
PAD = BAND_CHUNKS * CHUNK
MAX_REL = 256
N_REL = CHUNK + MAX_REL
D_FF = 2816
N_IN = 6656
ROPE_BASE = 10000.0
EPS = 1e-6
NEG_INF = -1e30
C_RQ, C_RK, C_RV, C_RG, C_AQ, C_AK, C_AV, C_GL = 0, 512, 1024, 2048, 3072, 3584, 4096, 4608

ADAM_LR, ADAM_B1, ADAM_B2, ADAM_EPS, ADAM_WD, ADAM_STEP = 0.001, 0.9, 0.999, 1e-08, 0.01, 10

N_CHIPS = 4
N_DEV = 8
WGRAD_ACC_BYTES = 8 * 1024 * 1024
WGRAD_VMEM_BYTES = 40 * 1024 * 1024
ROW_TILE = 512
BIG_ROW_TILE = 1024
IN_ORDER = (0, 2, 3, 1)
QBLK = 256
KWIN = PAD + QBLK
TOEP = 1024
VMEM_LIMIT = 56 * 1024 * 1024
MESH = pl.DeviceIdType.MESH

BIG = (
    ("w_in", 1), ("w_ret_out", 0), ("w_att_out", 1), ("w_out", 0), ("w_ffn_gate", 1), ("w_ffn_up", 1), ("w_ffn_down", 0))
WEIGHTS = ("norm_mix", "w_in", "b_gate", "rel_bias", "w_ret_out", "w_att_out", "w_out", "norm_ffn", "w_ffn_gate",
           "w_ffn_up", "w_ffn_down", "norm_final")
SMALL = ("norm_mix", "b_gate", "rel_bias", "norm_ffn", "norm_final")


def _dot(a, b):
    return lax.dot_general(a, b, (((1,), (0,)), ((), ())), preferred_element_type=f32)


def _dot_nt(a, b):
    return lax.dot_general(a, b, (((1,), (1,)), ((), ())), preferred_element_type=f32)


def _dot_tn(a, b):
    return lax.dot_general(a, b, (((0,), (0,)), ((), ())), preferred_element_type=f32)


def _sig(x):
    return 1.0 / (1.0 + jnp.exp(-x))


def _tile(n, pref, mult):
    best = None
    for t in range(mult, min(n, pref) + 1, mult):
        if n % t == 0:
            best = t
    return best if best is not None else n


def _same_shape(arrays):
    groups = {}
    for i, a in enumerate(arrays):
        groups.setdefault(a.shape, []).append(i)
    return list(groups.values())


def _params(sem, vmem=VMEM_LIMIT):
    return pltpu.CompilerParams(dimension_semantics=sem, vmem_limit_bytes=vmem)


def _in_proj(place, x2, gamma, phase):
    T, D = x2.shape
    _, _, ns = phase.arrays[0].shape
    tm = _tile(T, BIG_ROW_TILE, 8)
    ni = T // tm
    pass_chip = lambda j: sum(jnp.where(j == n, f, 0) for n, f in enumerate(IN_ORDER))

    def body(p_ref, x_ref, g_ref, xn_ref, pr_ref, xs_ref, w_ref, w_sem, carried):
        j, i = pl.program_id(0), pl.program_id(1)
        pin, pout, sems = carried
        rows = pl.ds(pl.multiple_of(i * tm, tm), tm)

        @pl.when(i == 0)
        def _():
            for n, f in enumerate(IN_ORDER):
                if f:
                    @pl.when(j == n)
                    def _():
                        phase.arrived(f - 1, pin, pout, *sems)
                        phase.begin(2 + f, pin, pout, *sems)
                        phase.arrived(2 + f, pin, pout, *sems)
            shard = pltpu.make_async_copy(pout[0].at[jnp.bitwise_xor(p_ref[1], pass_chip(j))], w_ref, w_sem)
            shard.start()
            shard.wait()

        @pl.when(j == 0)
        def _():
            x = x_ref[...]
            r = lax.rsqrt(jnp.mean(x * x, axis=-1, keepdims=True) + EPS)
            xn = (x * r * g_ref[...]).astype(bf16)
            xs_ref[rows, :] = xn
            xn_ref[...] = xn

        pr_ref[...] = _dot(xs_ref[rows, :], w_ref[...]).astype(bf16)

    first_pass = lambda j, i, p: (jnp.where(j == 0, i, ni - 1), 0)
    return _call(
        body, phase, name="in_proj", grid=(N_CHIPS, ni), prefetch=(place,), expose=True,
        in_specs=[pl.BlockSpec((tm, D), first_pass), pl.BlockSpec((1, D), lambda j, i, p: (0, 0))],
        out_specs=[pl.BlockSpec((tm, D), first_pass),
                   pl.BlockSpec((tm, ns), lambda j, i, p: (i, jnp.bitwise_xor(p[1], pass_chip(j))))],
        out_shape=[jax.ShapeDtypeStruct((T, D), bf16), jax.ShapeDtypeStruct((T, N_CHIPS * ns), bf16)],
        scratch_shapes=[pltpu.VMEM((T, D), bf16), pltpu.VMEM((D, ns), bf16), pltpu.SemaphoreType.DMA],
        args=(x2, gamma))


def _rope_tables(S):
    d = RET_KEY_DIM
    freqs = ROPE_BASE ** (-jnp.arange(0, d, 2, dtype=f32) / d)
    ang = jnp.arange(S, dtype=f32)[:, None] * freqs[None, :]
    cos, sin = jnp.cos(ang), jnp.sin(ang)
    return jnp.concatenate([cos, cos], axis=1), jnp.concatenate([-sin, sin], axis=1)


def _decay_tables():
    H = RET_HEADS
    log_g = jnp.log(1.0 - 2.0 ** (-5.0 - jnp.arange(H, dtype=f32)))
    p = jnp.arange(CHUNK, dtype=f32)
    intra = jnp.exp(log_g[:, None, None] * jnp.abs(p[:, None] - p[None, :]))
    q_dec = jnp.exp(log_g[:, None] * (p[None, :] + 1.0))
    k_dec = jnp.exp(log_g[:, None] * (CHUNK - 1.0 - p[None, :]))
    c_dec = jnp.exp(log_g * CHUNK)
    q_dec = jnp.broadcast_to(q_dec[:, :, None], (H, CHUNK, RET_KEY_DIM))
    k_dec = jnp.broadcast_to(k_dec[:, :, None], (H, CHUNK, RET_KEY_DIM))
    c_dec = jnp.broadcast_to(c_dec[:, None, None], (H, 1, RET_VAL_DIM))
    return intra, q_dec, k_dec, c_dec


K_SCALE = RET_KEY_DIM ** -0.5


RET_CHUNKS = 4


def _ret_tables_specs():
    whole = lambda *shape: pl.BlockSpec(shape, lambda b, i: (0,) * len(shape))
    return [whole(RET_HEADS, CHUNK, CHUNK), whole(RET_HEADS, CHUNK, RET_KEY_DIM), whole(RET_HEADS, CHUNK, RET_KEY_DIM),
            whole(RET_HEADS, 1, RET_VAL_DIM)]


def _rotate(x, cos, sn):
    return x * cos + pltpu.roll(x, RET_KEY_DIM // 2, 1) * sn


def _ret_fwd(proj, B, S, rope, decay, phase=None):
    T = B * S
    nc = S // CHUNK
    H, dk, dv = RET_HEADS, RET_KEY_DIM, RET_VAL_DIM
    sb = RET_CHUNKS * CHUNK
    ns = S // sb

    def body(q_ref, k_ref, v_ref, g_ref, cos_ref, sin_ref, intra_ref, qd_ref, kd_ref, cd_ref,
             qr_ref, kr_ref, o_ref, u_ref, st_ref, state_ref):
        @pl.when(pl.program_id(1) == 0)
        def _():
            state_ref[...] = jnp.zeros_like(state_ref)

        cos, sn = cos_ref[...], sin_ref[...]
        for h in range(H):
            hs = slice(h * dk, (h + 1) * dk)
            qr_ref[:, hs] = _rotate(q_ref[:, hs].astype(f32), cos, sn).astype(bf16)
            kr_ref[:, hs] = (_rotate(k_ref[:, hs].astype(f32), cos, sn) * K_SCALE).astype(bf16)
        states = [state_ref[h] for h in range(H)]
        for ci in range(RET_CHUNKS):
            r = slice(ci * CHUNK, (ci + 1) * CHUNK)
            for h in range(H):
                hk, hv = slice(h * dk, (h + 1) * dk), slice(h * dv, (h + 1) * dv)
                qi, ki, vi = qr_ref[r, hk], kr_ref[r, hk], v_ref[r, hv]
                stb = states[h].astype(bf16)
                st_ref[0, h, ci] = stb
                s = (_dot_nt(qi, ki) * intra_ref[h]).astype(bf16)
                o = _dot(s, vi) + _dot((qi.astype(f32) * qd_ref[h]).astype(bf16), stb)
                states[h] = states[h] * cd_ref[h] + _dot_tn((ki.astype(f32) * kd_ref[h]).astype(bf16), vi)
                mu = jnp.mean(o, axis=-1, keepdims=True)
                xc = o - mu
                var = jnp.mean(xc * xc, axis=-1, keepdims=True)
                oh = xc * lax.rsqrt(var + EPS)
                g = g_ref[r, hv].astype(f32)
                o_ref[r, hv] = o.astype(bf16)
                u_ref[r, hv] = (g * _sig(g) * oh).astype(bf16)
        for h in range(H):
            state_ref[h] = states[h]

    blk = lambda w, c: pl.BlockSpec((sb, w), lambda b, i: (b * ns + i, c))
    return _call(
        body, phase, name="ret_fwd", grid=(B, ns), scratch_shapes=[pltpu.VMEM((H, dk, dv), f32)],
        in_specs=[blk(H * dk, C_RQ // (H * dk)), blk(H * dk, C_RK // (H * dk)), blk(H * dv, C_RV // (H * dv)),
                  blk(H * dv, C_RG // (H * dv)),
                  pl.BlockSpec((sb, dk), lambda b, i: (i, 0)), pl.BlockSpec((sb, dk), lambda b, i: (i, 0)),
                  *_ret_tables_specs()],
        out_specs=[blk(H * dk, 0), blk(H * dk, 0), blk(H * dv, 0), blk(H * dv, 0),
                   pl.BlockSpec((1, H, RET_CHUNKS, dk, dv), lambda b, i: (b, 0, i, 0, 0))],
        out_shape=[jax.ShapeDtypeStruct((T, H * dk), bf16), jax.ShapeDtypeStruct((T, H * dk), bf16),
                   jax.ShapeDtypeStruct((T, H * dv), bf16), jax.ShapeDtypeStruct((T, H * dv), bf16),
                   jax.ShapeDtypeStruct((B, H, nc, dk, dv), bf16)],
        args=(proj, proj, proj, proj, *rope, *decay))


def _bias_rows(rb):
    last = rb[:, N_REL - 1:]
    return jnp.concatenate([
        jnp.broadcast_to(last, (ATT_HEADS, PAD - MAX_REL + 1)),
        jnp.flip(rb[:, :N_REL - 1], axis=1),
        jnp.broadcast_to(rb[:, :1], (ATT_HEADS, KWIN - PAD - CHUNK)),
        jnp.broadcast_to(last, (ATT_HEADS, TOEP - KWIN)),
    ], axis=1)


def _build_bias(t_ref, bias_ref):
    row = lax.broadcasted_iota(jnp.int32, (QBLK, KWIN), 0) // CHUNK
    col = lax.broadcasted_iota(jnp.int32, (QBLK, KWIN), 1) // CHUNK
    delta = BAND_CHUNKS + row - col
    vis = (delta >= 0) & (delta <= BAND_CHUNKS)
    for h in range(ATT_HEADS):
        t = jnp.broadcast_to(t_ref[h:h + 1, :], (QBLK, TOEP))
        rolled = pltpu.roll(t, 0, 1, stride=1, stride_axis=0)
        bias_ref[h] = jnp.where(vis, rolled[:, :KWIN], NEG_INF)


ATT_SCALE = ATT_HEAD_DIM ** -0.5


def _att_probs(qh, kh, bias):
    s = _dot_nt(qh, kh) + bias
    m = jnp.max(s, axis=-1, keepdims=True)
    p = jnp.exp(s - m)
    return p * (1.0 / jnp.sum(p, axis=-1, keepdims=True))


def _first_of_pair():
    return lax.broadcasted_iota(jnp.int32, (1, 2 * ATT_HEAD_DIM), 1) < ATT_HEAD_DIM


def _by_window(i, step):
    sizes = list(range(QBLK, KWIN, QBLK))
    for n, nk in enumerate(sizes):
        pl.when(i == n)(functools.partial(step, nk))
    pl.when(i >= len(sizes))(functools.partial(step, KWIN))


def _att_fwd(proj, trows, B, S, phase=None):
    T = B * S
    nq = S // QBLK
    dh = ATT_HEAD_DIM

    def body(q_ref, k_ref, v_ref, t_ref, o_ref, p_ref, bias_ref):
        i = pl.program_id(1)

        @pl.when((pl.program_id(0) == 0) & (i == 0))
        def _():
            _build_bias(t_ref, bias_ref)

        def step(nk):
            win = pl.ds(pl.multiple_of((i + 1) * QBLK - nk, QBLK), nk)
            kw, vw = k_ref[win, :], v_ref[win, :]
            first = _first_of_pair()
            outs = []
            for p in range(ATT_HEADS // 2):
                ps = slice(2 * p * dh, 2 * (p + 1) * dh)
                q2, k2, v2 = q_ref[:, ps] * ATT_SCALE, kw[:, ps], vw[:, ps]
                both = []
                for e in range(2):
                    qm = jnp.where(first == (e == 0), q2, jnp.zeros_like(q2))
                    pr = _att_probs(qm, k2, bias_ref[2 * p + e, :, KWIN - nk:]).astype(bf16)
                    p_ref[0, 2 * p + e, :, KWIN - nk:] = pr
                    both.append(_dot(pr, v2))
                outs.append(jnp.where(first, both[0], both[1]))
            o_ref[...] = jnp.concatenate(outs, axis=1).astype(bf16)

        _by_window(i, step)

    return _call(
        body, phase, name="att_fwd", grid=(B, nq),
        in_specs=[pl.BlockSpec((QBLK, ATT_W), lambda b, i: (b * nq + i, C_AQ // ATT_W)),
                  pl.BlockSpec((S, ATT_W), lambda b, i: (b, C_AK // ATT_W)),
                  pl.BlockSpec((S, ATT_W), lambda b, i: (b, C_AV // ATT_W)),
                  pl.BlockSpec((ATT_HEADS, TOEP), lambda b, i: (0, 0))],
        out_specs=[pl.BlockSpec((QBLK, ATT_W), lambda b, i: (b * nq + i, 0)),
                   pl.BlockSpec((1, ATT_HEADS, QBLK, KWIN), lambda b, i: (b * nq + i, 0, 0, 0))],
        out_shape=[jax.ShapeDtypeStruct((T, ATT_W), bf16), jax.ShapeDtypeStruct((B * nq, ATT_HEADS, QBLK, KWIN), bf16)],
        scratch_shapes=[pltpu.VMEM((ATT_HEADS, QBLK, KWIN), f32)],
        args=(proj, proj, proj, trows))


def _gl_specs(tm):
    w = 512
    return [pl.BlockSpec((tm, w), functools.partial(lambda i, j: (i, C_GL // 512 + j), j=j)) for j in range(4)]


def _gates(gl_refs, bg_ref):
    gl = jnp.concatenate([r[...] for r in gl_refs], axis=1).astype(f32) + bg_ref[...]
    g = _sig(gl)
    return g[:, :D_MODEL], g[:, D_MODEL:]


def _mix_fwd(x2, proj, u, ao, b_gate, w_ro, w_ao, w_out, phase=None):
    T, D = x2.shape
    tm = _tile(T, ROW_TILE, 8)

    def body(x_ref, u_ref, ao_ref, g0, g1, g2, g3, bg_ref, wro_ref, wao_ref, wo_ref, h1_ref, yr_ref, ya_ref):
        yr = _dot(u_ref[...], wro_ref[...])
        ao = ao_ref[...]
        ya = jnp.concatenate([_dot(ao, wao_ref[k]) for k in range(N_CHIPS)], axis=1)
        gr, ga = _gates((g0, g1, g2, g3), bg_ref)
        mix = gr * yr + ga * ya
        h1_ref[...] = x_ref[...] + _dot(mix.astype(bf16), wo_ref[...])
        yr_ref[...] = yr.astype(bf16)
        ya_ref[...] = ya.astype(bf16)

    full = lambda a: pl.BlockSpec(a.shape, lambda i: (0,) * a.ndim)
    row = lambda n: pl.BlockSpec((tm, n), lambda i: (i, 0))
    return _call(
        body, phase, name="mix_fwd", grid=(T // tm,), scratch_shapes=[],
        in_specs=[row(D), row(D), row(ATT_W), *_gl_specs(tm), full(b_gate), full(w_ro), full(w_ao), full(w_out)],
        out_specs=[row(D), row(D), row(D)],
        out_shape=[jax.ShapeDtypeStruct((T, D), f32), jax.ShapeDtypeStruct((T, D), bf16),
                   jax.ShapeDtypeStruct((T, D), bf16)],
        args=(x2, u, ao, proj, proj, proj, proj, b_gate, w_ro, w_ao, w_out))


def _ffn_fwd(h1, g_ffn, wg, wu, wd, g_fin, target):
    T, D = h1.shape
    nf, tf, _ = wg.shape
    tm = _tile(T, ROW_TILE, 8)

    def body(h1_ref, g_ref, wg_ref, wu_ref, wd_ref, gf_ref, tg_ref, hn_ref, a_ref, b_ref, f_ref, dh2_ref, part_ref):
        h1v = h1_ref[...]
        r = lax.rsqrt(jnp.mean(h1v * h1v, axis=-1, keepdims=True) + EPS)
        hn = (h1v * r * g_ref[...]).astype(bf16)
        hn_ref[...] = hn
        h2 = h1v
        for k in range(nf):
            a = _dot_nt(hn, wg_ref[k])
            b = _dot_nt(hn, wu_ref[k])
            f = ((a * _sig(a)) * b).astype(bf16)
            a_ref[k] = a.astype(bf16)
            b_ref[k] = b.astype(bf16)
            f_ref[k] = f
            h2 = h2 + _dot(f, wd_ref[k])
        r = lax.rsqrt(jnp.mean(h2 * h2, axis=-1, keepdims=True) + EPS)
        n = h2 * r
        gf = gf_ref[...]
        e = n * gf - tg_ref[...]
        dy = e * (1.0 / D)
        dn = dy * gf
        dh2_ref[...] = r * (dn - n * jnp.mean(dn * n, axis=-1, keepdims=True))
        part_ref[...] = jnp.zeros_like(part_ref)
        part_ref[0:1, :] = jnp.sum(dy * n, axis=0, keepdims=True)
        part_ref[1:2, :] = (0.5 / D) * jnp.sum(e * e, axis=0, keepdims=True)

    row = lambda n: pl.BlockSpec((tm, n), lambda i: (i, 0))
    vec = pl.BlockSpec((1, D), lambda i: (0, 0))
    col = pl.BlockSpec((nf, tm, tf), lambda i: (0, i, 0))
    held = lambda w: pl.BlockSpec(w.shape, lambda i: (0, 0, 0), pipeline_mode=pl.Buffered(1))
    act = jax.ShapeDtypeStruct((nf, T, tf), bf16)
    return pl.pallas_call(
        body, name="ffn_fwd", grid=(T // tm,),
        in_specs=[row(D), vec, held(wg), held(wu), held(wd), vec, row(D)],
        out_specs=[row(D), col, col, col, row(D), pl.BlockSpec((8, D), lambda i: (i, 0))],
        out_shape=[jax.ShapeDtypeStruct((T, D), bf16), act, act, act,
                   jax.ShapeDtypeStruct((T, D), f32), jax.ShapeDtypeStruct((T // tm * 8, D), f32)],
        compiler_params=_params(("parallel",)),
    )(h1, g_ffn, wg, wu, wd, g_fin, target)


def _ffn_bwd(dh2, h1, g_ffn, a, b, wg, wu, wd):
    T, D = h1.shape
    nf, tf, _ = wg.shape
    tm = _tile(T, ROW_TILE // 2, 8)

    def body(dh2_ref, h1_ref, g_ref, a_ref, b_ref, wg_ref, wu_ref, wd_ref, da_ref, db_ref, dh1_ref, part_ref):
        dh2v = dh2_ref[...]
        dh2b = dh2v.astype(bf16)
        dhn = jnp.zeros((tm, D), f32)
        for k in range(nf):
            df = _dot_nt(dh2b, wd_ref[k])
            av = a_ref[k].astype(f32)
            sg = _sig(av)
            db = (df * (av * sg)).astype(bf16)
            da = (df * b_ref[k].astype(f32) * (sg * (1.0 + av * (1.0 - sg)))).astype(bf16)
            da_ref[k] = da
            db_ref[k] = db
            dhn = dhn + _dot(da, wg_ref[k]) + _dot(db, wu_ref[k])
        h = h1_ref[...]
        r = lax.rsqrt(jnp.mean(h * h, axis=-1, keepdims=True) + EPS)
        n = h * r
        dn = dhn * g_ref[...]
        dh1_ref[...] = dh2v + r * (dn - n * jnp.mean(dn * n, axis=-1, keepdims=True))
        part_ref[...] = jnp.zeros_like(part_ref)
        part_ref[0:1, :] = jnp.sum(dhn * n, axis=0, keepdims=True)

    row = lambda n: pl.BlockSpec((tm, n), lambda i: (i, 0))
    col = pl.BlockSpec((nf, tm, tf), lambda i: (0, i, 0))
    held = lambda w: pl.BlockSpec(w.shape, lambda i: (0, 0, 0), pipeline_mode=pl.Buffered(1))
    act = jax.ShapeDtypeStruct((nf, T, tf), bf16)
    return pl.pallas_call(
        body, name="ffn_bwd", grid=(T // tm,),
        in_specs=[row(D), row(D), pl.BlockSpec((1, D), lambda i: (0, 0)), col, col, held(wg), held(wu), held(wd)],
        out_specs=[col, col, row(D), pl.BlockSpec((8, D), lambda i: (i, 0))],
        out_shape=[act, act, jax.ShapeDtypeStruct((T, D), f32), jax.ShapeDtypeStruct((T // tm * 8, D), f32)],
        compiler_params=_params(("parallel",)),
    )(dh2, h1, g_ffn, a, b, wg, wu, wd)


def _mix_bwd(dh1, proj, yr, ya, b_gate, w_ro, w_ao, w_out, phase=None):
    T, D = dh1.shape
    tm = _tile(T, ROW_TILE, 8)

    def body(dh1_ref, g0, g1, g2, g3, bg_ref, yr_ref, ya_ref, wro_ref, wao_ref, wo_ref,
             du_ref, dao_ref, dgl_ref, mix_ref, dyr_ref, dya_ref, part_ref):
        dmix = _dot_nt(dh1_ref[...].astype(bf16), wo_ref[...])
        gr, ga = _gates((g0, g1, g2, g3), bg_ref)
        yr = yr_ref[...].astype(f32)
        ya = ya_ref[...].astype(f32)
        dyr = (dmix * gr).astype(bf16)
        dya = (dmix * ga).astype(bf16)
        dgl = jnp.concatenate([dmix * yr * gr * (1.0 - gr), dmix * ya * ga * (1.0 - ga)], axis=1)
        du_ref[...] = _dot_nt(dyr, wro_ref[...]).astype(bf16)
        ns = wao_ref.shape[2]
        dao = _dot_nt(dya[:, :ns], wao_ref[0])
        for k in range(1, N_CHIPS):
            dao = dao + _dot_nt(dya[:, k * ns:(k + 1) * ns], wao_ref[k])
        dao_ref[...] = dao.astype(bf16)
        dgl_ref[...] = dgl.astype(bf16)
        mix_ref[...] = (gr * yr + ga * ya).astype(bf16)
        dyr_ref[...] = dyr
        dya_ref[...] = dya
        part_ref[...] = jnp.zeros_like(part_ref)
        part_ref[0:1, :] = jnp.sum(dgl, axis=0, keepdims=True)

    full = lambda a: pl.BlockSpec(a.shape, lambda i: (0,) * a.ndim)
    row = lambda n: pl.BlockSpec((tm, n), lambda i: (i, 0))
    return _call(
        body, phase, name="mix_bwd", grid=(T // tm,), scratch_shapes=[],
        in_specs=[row(D), *_gl_specs(tm), full(b_gate), row(D), row(D), full(w_ro), full(w_ao), full(w_out)],
        out_specs=[row(D), row(ATT_W), row(2 * D), row(D), row(D), row(D), pl.BlockSpec((8, 2 * D), lambda i: (i, 0))],
        out_shape=[jax.ShapeDtypeStruct((T, D), bf16), jax.ShapeDtypeStruct((T, ATT_W), bf16),
                   jax.ShapeDtypeStruct((T, 2 * D), bf16), jax.ShapeDtypeStruct((T, D), bf16),
                   jax.ShapeDtypeStruct((T, D), bf16), jax.ShapeDtypeStruct((T, D), bf16),
                   jax.ShapeDtypeStruct((T // tm * 8, 2 * D), f32)],
        args=(dh1, proj, proj, proj, proj, b_gate, yr, ya, w_ro, w_ao, w_out))


def _ret_bwd(proj, qr, kr, o, states, du, dgl, B, S, rope, decay, phase=None):
    T = B * S
    H, dk, dv = RET_HEADS, RET_KEY_DIM, RET_VAL_DIM
    sb = RET_CHUNKS * CHUNK
    ns = S // sb

    def body(qr_ref, kr_ref, v_ref, g_ref, o_ref, st_ref, du_ref, dgl_ref, cos_ref, sin_ref, intra_ref, qd_ref, kd_ref,
             cd_ref, dp_ref, dstate_ref):
        dq_ref, dk_ref = dp_ref.at[:, pl.ds(C_RQ, H * dk)], dp_ref.at[:, pl.ds(C_RK, H * dk)]
        dv_ref, dg_ref = dp_ref.at[:, pl.ds(C_RV, H * dv)], dp_ref.at[:, pl.ds(C_RG, H * dv)]
        dp_ref[:, C_GL:] = dgl_ref[...]

        @pl.when(pl.program_id(1) == 0)
        def _():
            dstate_ref[...] = jnp.zeros_like(dstate_ref)

        cos, snb = cos_ref[...], -sin_ref[...]
        dstates = [dstate_ref[h] for h in range(H)]
        for ci in reversed(range(RET_CHUNKS)):
            r = slice(ci * CHUNK, (ci + 1) * CHUNK)
            for h in range(H):
                hk, hv = slice(h * dk, (h + 1) * dk), slice(h * dv, (h + 1) * dv)
                intra, qd, kd = intra_ref[h], qd_ref[h], kd_ref[h]
                qi, ki, vi = qr_ref[r, hk], kr_ref[r, hk], v_ref[r, hv]
                si = st_ref[0, h, ci]
                o = o_ref[r, hv].astype(f32)
                mu = jnp.mean(o, axis=-1, keepdims=True)
                xc = o - mu
                rstd = lax.rsqrt(jnp.mean(xc * xc, axis=-1, keepdims=True) + EPS)
                oh = xc * rstd
                g = g_ref[r, hv].astype(f32)
                sg = _sig(g)
                dui = du_ref[r, hv].astype(f32)
                dg_ref[r, hv] = (dui * oh * (sg * (1.0 + g * (1.0 - sg)))).astype(bf16)
                doh = dui * (g * sg)
                do = rstd * (doh - jnp.mean(doh, axis=-1, keepdims=True)
                             - oh * jnp.mean(doh * oh, axis=-1, keepdims=True))
                dob = do.astype(bf16)
                p = (_dot_nt(qi, ki) * intra).astype(bf16)
                dsb = dstates[h].astype(bf16)
                kt = (ki.astype(f32) * kd).astype(bf16)
                qt = (qi.astype(f32) * qd).astype(bf16)
                dv_ref[r, hv] = (_dot_tn(p, dob) + _dot(kt, dsb)).astype(bf16)
                da = (_dot_nt(dob, vi) * intra).astype(bf16)
                dq = _dot(da, ki) + _dot_nt(dob, si) * qd
                dkk = (_dot_tn(da, qi) + _dot_nt(vi, dsb) * kd) * K_SCALE
                dq_ref[r, hk] = _rotate(dq, cos[r], snb[r]).astype(bf16)
                dk_ref[r, hk] = _rotate(dkk, cos[r], snb[r]).astype(bf16)
                dstates[h] = dstates[h] * cd_ref[h] + _dot_tn(qt, dob)
        for h in range(H):
            dstate_ref[h] = dstates[h]

    blk = lambda w, c: pl.BlockSpec((sb, w), lambda b, i: (b * ns + ns - 1 - i, c))
    return _call(
        body, phase, name="ret_bwd", grid=(B, ns),
        in_specs=[blk(H * dk, 0), blk(H * dk, 0), blk(H * dv, C_RV // (H * dv)), blk(H * dv, C_RG // (H * dv)),
                  blk(H * dv, 0),
                  pl.BlockSpec((1, H, RET_CHUNKS, dk, dv), lambda b, i: (b, 0, ns - 1 - i, 0, 0)),
                  blk(H * dv, 0), blk(N_IN - C_GL, 0),
                  pl.BlockSpec((sb, dk), lambda b, i: (ns - 1 - i, 0)), pl.BlockSpec((sb, dk), lambda b, i: (ns - 1 - i, 0)),
                  *_ret_tables_specs()],
        out_specs=[blk(N_IN, 0)], out_shape=[jax.ShapeDtypeStruct((T, N_IN), bf16)],
        scratch_shapes=[pltpu.VMEM((H, dk, dv), f32)],
        args=(qr, kr, proj, proj, o, states, du, dgl, *rope, *decay))


def _att_bwd(proj, dao, probs, dproj, B, S, phase=None):
    T = B * S
    nq = S // QBLK
    dh = ATT_HEAD_DIM

    def body(q_ref, k_ref, v_ref, do_ref, p_ref, _, dp_ref, vec_ref, dbias_ref, dka_ref, dva_ref):
        b, i = pl.program_id(0), pl.program_id(1)

        @pl.when((b == 0) & (i == 0))
        def _():
            dbias_ref[...] = jnp.zeros_like(dbias_ref)

        @pl.when(i == 0)
        def _():
            dka_ref[...] = jnp.zeros_like(dka_ref)
            dva_ref[...] = jnp.zeros_like(dva_ref)

        def step(nk):
            win = pl.ds(pl.multiple_of((i + 1) * QBLK - nk, QBLK), nk)
            kw, vw = k_ref[win, :], v_ref[win, :]
            first = _first_of_pair()
            first_rows = lax.broadcasted_iota(jnp.int32, (2 * dh, 1), 0) < dh
            dqs, dks, dvs = [], [], []
            for p in range(ATT_HEADS // 2):
                ps = slice(2 * p * dh, 2 * (p + 1) * dh)
                q2, k2, v2, do2 = q_ref[:, ps] * ATT_SCALE, kw[:, ps], vw[:, ps], do_ref[:, ps]
                dq2, dk2, dv2 = [], [], []
                for e in range(2):
                    h = 2 * p + e
                    prb = p_ref[0, h, :, KWIN - nk:]
                    pr = prb.astype(f32)
                    dp = _dot_nt(jnp.where(first == (e == 0), do2, jnp.zeros_like(do2)), v2)
                    ds = pr * (dp - jnp.sum(pr * dp, axis=-1, keepdims=True))
                    dbias_ref[h, :, KWIN - nk:] += ds
                    dsb = ds.astype(bf16)
                    dq2.append(_dot(dsb, k2) * ATT_SCALE)
                    dk2.append(_dot_tn(q2, dsb))
                    dv2.append(_dot_tn(do2, prb))
                dqs.append(jnp.where(first, dq2[0], dq2[1]))
                dks.append(jnp.where(first_rows, dk2[0], dk2[1]))
                dvs.append(jnp.where(first_rows, dv2[0], dv2[1]))
            dp_ref[pl.ds(pl.multiple_of(i * QBLK, QBLK), QBLK), :ATT_W] = jnp.concatenate(dqs, axis=1).astype(bf16)
            dka_ref[:, win] += jnp.concatenate(dks, axis=0)
            dva_ref[:, win] += jnp.concatenate(dvs, axis=0)

        _by_window(i, step)

        @pl.when(i == nq - 1)
        def _():
            dp_ref[:, ATT_W:2 * ATT_W] = dka_ref[...].T.astype(bf16)
            dp_ref[:, 2 * ATT_W:] = dva_ref[...].T.astype(bf16)

        @pl.when((b == B - 1) & (i == nq - 1))
        def _():
            rr = lax.broadcasted_iota(jnp.int32, (QBLK, QBLK), 0)
            cc = lax.broadcasted_iota(jnp.int32, (QBLK, QBLK), 1)
            flip = jnp.where(rr + cc == QBLK - 1, 1.0, 0.0).astype(bf16)
            for h in range(ATT_HEADS):
                d = dbias_ref[h]
                hi = d.astype(bf16)
                lo = (d - hi.astype(f32)).astype(bf16)
                rev = _dot(flip, hi) + _dot(flip, lo)
                wide = jnp.concatenate([rev, jnp.zeros((QBLK, TOEP - KWIN), f32)], axis=1)
                rolled = pltpu.roll(wide, 0, 1, stride=1, stride_axis=0)
                vec_ref[h:h + 1, :] = jnp.sum(rolled, axis=0, keepdims=True)

    qspec = lambda c: pl.BlockSpec((QBLK, ATT_W), lambda b, i: (b * nq + i, c))
    kspec = lambda c: pl.BlockSpec((S, ATT_W), lambda b, i: (b, c))
    return _call(
        body, phase, name="att_bwd", grid=(B, nq), aliases={5: 0},
        in_specs=[qspec(C_AQ // ATT_W), kspec(C_AK // ATT_W), kspec(C_AV // ATT_W), qspec(0),
                  pl.BlockSpec((1, ATT_HEADS, QBLK, KWIN), lambda b, i: (b * nq + i, 0, 0, 0)),
                  pl.BlockSpec(memory_space=pl.ANY)],
        out_specs=[pl.BlockSpec((S, 3 * ATT_W), lambda b, i: (b, C_AQ // (3 * ATT_W))),
                   pl.BlockSpec((ATT_HEADS, TOEP), lambda b, i: (0, 0))],
        out_shape=[jax.ShapeDtypeStruct((T, N_IN), bf16), jax.ShapeDtypeStruct((ATT_HEADS, TOEP), f32)],
        scratch_shapes=[pltpu.VMEM((ATT_HEADS, QBLK, KWIN), f32), pltpu.VMEM((ATT_W, S), f32),
                        pltpu.VMEM((ATT_W, S), f32)],
        args=(proj, proj, proj, dao, probs, dproj))


def _in_proj_bwd(dproj, w_in, x2, gamma, dh1, phase=None):
    T, D = x2.shape
    nk, _, tk = w_in.shape
    tm = _tile(T, BIG_ROW_TILE, 8)

    def body(dp_ref, w_ref, x_ref, g_ref, dh1_ref, dx_ref, part_ref, acc_ref):
        j = pl.program_id(1)

        @pl.when(j == 0)
        def _():
            acc_ref[...] = jnp.zeros_like(acc_ref)

        acc_ref[...] += _dot_nt(dp_ref[...], w_ref[0])

        @pl.when(j == nk - 1)
        def _():
            x = x_ref[...]
            r = lax.rsqrt(jnp.mean(x * x, axis=-1, keepdims=True) + EPS)
            n = x * r
            dxn = acc_ref[...]
            dn = dxn * g_ref[...]
            dx_ref[...] = dh1_ref[...] + r * (dn - n * jnp.mean(dn * n, axis=-1, keepdims=True))
            part_ref[...] = jnp.zeros_like(part_ref)
            part_ref[0:1, :] = jnp.sum(dxn * n, axis=0, keepdims=True)

    row = lambda n: pl.BlockSpec((tm, n), lambda i, j: (i, 0))
    return _call(
        body, phase, name="in_proj_bwd", grid=(T // tm, nk),
        in_specs=[pl.BlockSpec((tm, tk), lambda i, j: (i, j)), pl.BlockSpec((1, D, tk), lambda i, j: (j, 0, 0)), row(D),
                  pl.BlockSpec((1, D), lambda i, j: (0, 0)), row(D)],
        out_specs=[row(D), pl.BlockSpec((8, D), lambda i, j: (i, 0))],
        out_shape=[jax.ShapeDtypeStruct((T, D), f32), jax.ShapeDtypeStruct((T // tm * 8, D), f32)],
        scratch_shapes=[pltpu.VMEM((tm, D), f32)],
        args=(dproj, w_in, x2, gamma, dh1))


def _wgrad(a, b, shard_axis, name, phase=None):
    def spec(arr, sharded, tt):
        if arr.ndim == 3:
            return arr.shape[2], pl.BlockSpec((1, tt, arr.shape[2]), lambda s, t: (s, t, 0))
        if sharded:
            w = arr.shape[1] // N_CHIPS
            return w, pl.BlockSpec((tt, w), lambda s, t: (t, s))
        return arr.shape[1], pl.BlockSpec((tt, arr.shape[1]), lambda s, t: (t, 0))

    T = a.shape[-2]
    whole = a.ndim == 2 and b.ndim == 2 and a.shape[1] * b.shape[1] * 4 <= WGRAD_ACC_BYTES
    width = lambda arr, sharded: arr.shape[-1] // (1 if whole or arr.ndim == 3 or not sharded else N_CHIPS)
    wa, wb_ = width(a, shard_axis == 0), width(b, shard_axis == 1)
    fixed = wa * wb_ * (4 + 2 * 2)
    tt = T // 4 if whole else T
    while tt > 256 and 2 * tt * (wa * a.dtype.itemsize + wb_ * b.dtype.itemsize) + fixed > WGRAD_VMEM_BYTES:
        tt //= 2
    nt = T // tt
    if whole:
        K, N = a.shape[1], b.shape[1]
        a_spec, b_spec = pl.BlockSpec((tt, K), lambda s, t: (t, 0)), pl.BlockSpec((tt, N), lambda s, t: (t, 0))
        out_block = (N_CHIPS, K // N_CHIPS, N) if shard_axis == 0 else (N_CHIPS, K, N // N_CHIPS)
        out_spec = pl.BlockSpec(out_block, lambda s, t: (0, 0, 0))
    else:
        K, a_spec = spec(a, shard_axis == 0, tt)
        N, b_spec = spec(b, shard_axis == 1, tt)
        out_block = (N_CHIPS, K, N)
        out_spec = pl.BlockSpec((1, K, N), lambda s, t: (s, 0, 0))

    def body(a_ref, b_ref, o_ref, acc_ref):
        t = pl.program_id(1)

        @pl.when(t == 0)
        def _():
            acc_ref[...] = jnp.zeros_like(acc_ref)

        av = a_ref[0] if a.ndim == 3 else a_ref[...]
        bv = b_ref[0] if b.ndim == 3 else b_ref[...]
        acc_ref[...] += _dot_tn(av.astype(bf16), bv.astype(bf16))

        @pl.when(t == nt - 1)
        def _():
            if not whole:
                o_ref[0] = acc_ref[...].astype(bf16)
            else:
                _, kk, nn = out_block
                for s in range(N_CHIPS):
                    o_ref[s] = (acc_ref[s * kk:(s + 1) * kk, :] if shard_axis == 0
                                else acc_ref[:, s * nn:(s + 1) * nn]).astype(bf16)

    (grad,), carried = _call(
        body, phase, name=name, grid=(1 if whole else N_CHIPS, nt), in_specs=[a_spec, b_spec], out_specs=[out_spec],
        out_shape=[jax.ShapeDtypeStruct(out_block, bf16)], scratch_shapes=[pltpu.VMEM((K, N), f32)], args=(a, b))
    return grad, carried


def _adamw_sum(place, groups, name):
    n = len(groups)
    R, C = groups[0][0].shape
    half = R // 2
    tr = _tile(half, max(16, (1 << 18) // C // 16 * 16), 16)
    nr = half // tr

    def body(p_ref, *refs):
        for a in range(n):
            w_ref, m_ref, v_ref, part_ref, fc_ref, fs_ref = refs[6 * a:6 * a + 6]
            g_ref, d_ref, mo_ref, vo_ref = refs[6 * n + 4 * a:6 * n + 4 * a + 4]
            up = lambda x: x.astype(f32)
            mine = ((up(part_ref[0]) + up(fc_ref[0])) + up(fc_ref[1])) + up(fc_ref[2])
            sibs = ((up(fs_ref[0]) + up(fs_ref[1])) + up(fs_ref[2])) + up(fs_ref[3])
            g_ = jnp.where(pl.program_id(0) == p_ref[0], mine, sibs)
            m_ = ADAM_B1 * m_ref[...] + (1.0 - ADAM_B1) * g_
            v_ = ADAM_B2 * v_ref[...] + (1.0 - ADAM_B2) * (g_ * g_)
            m_hat = m_ / (1.0 - ADAM_B1 ** ADAM_STEP)
            v_hat = v_ / (1.0 - ADAM_B2 ** ADAM_STEP)
            g_ref[...] = g_
            d_ref[...] = -ADAM_LR * (m_hat / (jnp.sqrt(v_hat) + ADAM_EPS) + ADAM_WD * w_ref[...])
            mo_ref[...] = m_
            vo_ref[...] = v_

    spec = pl.BlockSpec((tr, C), lambda h, r, p: (h * nr + r, 0))
    one = [spec, spec, spec, pl.BlockSpec((1, tr, C), lambda h, r, p: (p[1], jnp.where(h == p[0], r, 0), 0)),
           pl.BlockSpec((3, tr, C), lambda h, r, p: (0, jnp.where(h == p[0], r, 0), 0)),
           pl.BlockSpec((4, tr, C), lambda h, r, p: (0, jnp.where(h == p[0], 0, r), 0))]
    res = pl.pallas_call(
        body, name=name,
        grid_spec=pltpu.PrefetchScalarGridSpec(num_scalar_prefetch=1, grid=(2, nr), in_specs=one * n,
                                               out_specs=[spec] * (4 * n)),
        out_shape=[jax.ShapeDtypeStruct((R, C), f32)] * (4 * n),
        compiler_params=_params(("parallel", "parallel")),
    )(place, *[x for g in groups for x in g])
    return [tuple(res[4 * a:4 * a + 4]) for a in range(n)]


def _adamw(w, g, m, v, name):
    R, C = w.shape
    tr = _tile(R, max(8, (1 << 18) // C // 8 * 8), 8)

    def body(w_ref, g_ref, m_ref, v_ref, d_ref, mo_ref, vo_ref):
        g_ = g_ref[...]
        m_ = ADAM_B1 * m_ref[...] + (1.0 - ADAM_B1) * g_
        v_ = ADAM_B2 * v_ref[...] + (1.0 - ADAM_B2) * (g_ * g_)
        m_hat = m_ / (1.0 - ADAM_B1 ** ADAM_STEP)
        v_hat = v_ / (1.0 - ADAM_B2 ** ADAM_STEP)
        d_ref[...] = -ADAM_LR * (m_hat / (jnp.sqrt(v_hat) + ADAM_EPS) + ADAM_WD * w_ref[...])
        mo_ref[...] = m_
        vo_ref[...] = v_

    spec = pl.BlockSpec((tr, C), lambda i: (i, 0))
    return pl.pallas_call(
        body, name=name, grid=(R // tr,), in_specs=[spec] * 4, out_specs=[spec] * 3,
        out_shape=[jax.ShapeDtypeStruct((R, C), f32)] * 3,
        compiler_params=_params(("parallel",)),
    )(w, g, m, v)


def _place():
    return lax.axis_index("x"), lax.axis_index("y"), lax.axis_index("c")


def _other_chips(x, y):
    chips = [(1 - x, y), (x, 1 - y), (1 - x, 1 - y)]
    return chips, [2 * cx + cy for cx, cy in chips]


def _spread_phase(blk):
    def peers():
        x, y, c = _place()
        return [tuple(1 - p if (k >> s) & 1 else p for p, s in ((x, 2), (y, 1), (c, 0))) for k in range(1, N_DEV)]

    def copies(pin, out):
        x, y, c = _place()
        mine = out[0].at[4 * x + 2 * y + c]
        return [(mine, mine, peer) for peer in peers()]

    stack = jnp.broadcast_to(blk, (N_DEV,) + blk.shape)
    return _Phase([stack], [jax.ShapeDtypeStruct(stack.shape, stack.dtype)], {0: 0}, N_DEV - 1, copies,
                  lambda pin, out: [out[0].at[4 * px + 2 * py + pc] for px, py, pc in peers()])


def _sum_slots(stack, name):
    def body(s_ref, o_ref):
        tot = s_ref[0]
        for d in range(1, stack.shape[0]):
            tot = tot + s_ref[d]
        o_ref[...] = tot

    vm = pl.BlockSpec(memory_space=pltpu.VMEM)
    return pl.pallas_call(body, name=name, in_specs=[vm], out_specs=vm,
                          out_shape=jax.ShapeDtypeStruct(stack.shape[1:], stack.dtype))(stack)


def _cast_shards(place, ws, name):
    n = len(ws)
    R, C = ws[0].shape
    tr = _tile(R, max(16, (1 << 19) // C // 16 * 16), 16)

    def body(p_ref, *refs):
        for a in range(n):
            refs[n + a][0] = refs[a][...].astype(bf16)

    return pl.pallas_call(
        body, name=name,
        grid_spec=pltpu.PrefetchScalarGridSpec(
            num_scalar_prefetch=1, grid=(R // tr,),
            in_specs=[pl.BlockSpec((tr, C), lambda r, p: (r, 0))] * n,
            out_specs=[pl.BlockSpec((1, tr, C), lambda r, p: (p[1], r, 0))] * n),
        out_shape=[jax.ShapeDtypeStruct((N_CHIPS, R, C), bf16)] * n,
        compiler_params=_params(("parallel",)),
    )(place, *ws)


class _Phase:
    def __init__(self, arrays, out_shapes, aliases, n_copies, copies, arrivals, own_starts=(), own_waits=()):
        self.arrays, self.out_shapes, self.aliases = list(arrays), list(out_shapes), dict(aliases)
        self.n_copies, self.copies, self.arrivals = n_copies, copies, arrivals
        self.own_starts, self.own_waits = tuple(own_starts), tuple(own_waits)

    def sems(self):
        return [pltpu.SemaphoreType.DMA((self.n_copies,)), pltpu.SemaphoreType.DMA((self.n_copies,))]

    def _descriptors(self, pin, pout, send_sems, recv_sems):
        return [pltpu.make_async_remote_copy(src_ref=s, dst_ref=d, send_sem=send_sems.at[i], recv_sem=recv_sems.at[i],
                                             device_id=to, device_id_type=MESH)
                for i, (s, d, to) in enumerate(self.copies(pin, pout))]

    def _arrival(self, i, pin, pout, send_sems, recv_sems):
        dst = self.arrivals(pin, pout)[i]
        return pltpu.make_async_remote_copy(src_ref=dst, dst_ref=dst, send_sem=send_sems.at[i], recv_sem=recv_sems.at[i],
                                            device_id=_place(), device_id_type=MESH)

    def start(self, pin, pout, send_sems, recv_sems):
        for i, cp in enumerate(self._descriptors(pin, pout, send_sems, recv_sems)):
            if i not in self.own_starts:
                cp.start()

    def begin(self, i, pin, pout, send_sems, recv_sems):
        self._descriptors(pin, pout, send_sems, recv_sems)[i].start()

    def arrived(self, i, pin, pout, send_sems, recv_sems):
        self._arrival(i, pin, pout, send_sems, recv_sems).wait_recv()

    def finish(self, pin, pout, send_sems, recv_sems):
        for i in range(self.n_copies):
            if i not in self.own_waits:
                self._arrival(i, pin, pout, send_sems, recv_sems).wait_recv()
        for cp in self._descriptors(pin, pout, send_sems, recv_sems):
            cp.wait_send()


def _join(phases):
    if len(phases) == 1:
        return phases[0]
    ai = np.cumsum([0] + [len(p.arrays) for p in phases])
    oi = np.cumsum([0] + [len(p.out_shapes) for p in phases])

    def each(fn_name, pin, pout):
        return [item for k, p in enumerate(phases)
                for item in getattr(p, fn_name)(pin[ai[k]:ai[k + 1]], pout[oi[k]:oi[k + 1]])]

    aliases = {int(ai[k]) + i: int(oi[k]) + j for k, p in enumerate(phases) for i, j in p.aliases.items()}
    ci = np.cumsum([0] + [p.n_copies for p in phases])
    shifted = lambda attr: [int(ci[k]) + i for k, p in enumerate(phases) for i in getattr(p, attr)]
    return _Phase([a for p in phases for a in p.arrays], [s for p in phases for s in p.out_shapes], aliases,
                  int(ci[-1]), functools.partial(each, "copies"), functools.partial(each, "arrivals"),
                  shifted("own_starts"), shifted("own_waits"))


def _call(body, phase, *, name, grid, in_specs, out_specs, out_shape, scratch_shapes, args, prefetch=(), expose=False,
          aliases=None):
    seq = _params(("arbitrary",) * len(grid))
    np_ = len(prefetch)
    own = {np_ + i: j for i, j in (aliases or {}).items()}
    if phase is None:
        spec = pltpu.PrefetchScalarGridSpec(num_scalar_prefetch=np_, grid=grid, in_specs=in_specs, out_specs=out_specs,
                                            scratch_shapes=scratch_shapes)
        res = pl.pallas_call(body, name=name, grid_spec=spec, out_shape=out_shape, input_output_aliases=own,
                             compiler_params=seq)(*prefetch, *args)
        return list(res), []
    ni, no, ns = len(in_specs), len(out_specs), len(scratch_shapes)
    pi, po = len(phase.arrays), len(phase.out_shapes)

    def hosted(*refs):
        cut = np.cumsum([np_, ni, pi, no, po, ns])
        pre, ins, pin, outs, pout, scr, sems = (refs[a:b] for a, b in zip([0, *cut], [*cut, len(refs)]))
        ids = [pl.program_id(d) for d in range(len(grid))]
        first = functools.reduce(lambda p, q: p & q, [i == 0 for i in ids])
        last = functools.reduce(lambda p, q: p & q, [i == g - 1 for i, g in zip(ids, grid)])
        pl.when(first)(lambda: phase.start(pin, pout, *sems))
        body(*pre, *ins, *outs, *scr, **({"carried": (pin, pout, sems)} if expose else {}))
        pl.when(last)(lambda: phase.finish(pin, pout, *sems))

    anyspace = pl.BlockSpec(memory_space=pl.ANY)
    spec = pltpu.PrefetchScalarGridSpec(
        num_scalar_prefetch=np_, grid=grid, in_specs=list(in_specs) + [anyspace] * pi,
        out_specs=list(out_specs) + [anyspace] * po, scratch_shapes=list(scratch_shapes) + phase.sems())
    res = pl.pallas_call(
        hosted, name=name, grid_spec=spec, out_shape=list(out_shape) + phase.out_shapes,
        input_output_aliases={**own, **{np_ + ni + i: no + j for i, j in phase.aliases.items()}}, compiler_params=seq,
    )(*prefetch, *args, *phase.arrays)
    return list(res[:no]), list(res[no:])


def _run_phases(name, phases):
    first = phases[0]
    pi, po = len(first.arrays), len(first.out_shapes)

    def body(*refs):
        pin, pout, sems = refs[:pi], refs[pi:pi + po], refs[pi + po:]
        for n, ph in enumerate(phases):
            ph.start(pin, pout, *sems[2 * n:2 * n + 2])
            ph.finish(pin, pout, *sems[2 * n:2 * n + 2])

    anyspace = pl.BlockSpec(memory_space=pl.ANY)
    return list(pl.pallas_call(
        body, name=name, in_specs=[anyspace] * pi, out_specs=[anyspace] * po, out_shape=first.out_shapes,
        input_output_aliases=first.aliases, scratch_shapes=[s for ph in phases for s in ph.sems()],
    )(*first.arrays))


def _half_rows(buf, c):
    half = buf.shape[1] // 2
    return pl.ds(c * half, half), pl.ds((1 - c) * half, half)


def _gather_phase(bufs, over_ici):
    n = len(bufs)
    shapes = [jax.ShapeDtypeStruct(b.shape, b.dtype) for b in bufs]

    def landed(out, which):
        x, y, c = _place()
        _, ks = _other_chips(x, y)
        return [out[a].at[ks[j], _half_rows(bufs[a], c)[which]] for a in range(n) for j in range(3)]

    def ici(pin, out):
        x, y, c = _place()
        chips, _ = _other_chips(x, y)
        mine = [out[a].at[2 * x + y, _half_rows(bufs[a], c)[0]] for a in range(n)]
        return [(mine[a], mine[a], (*chips[j], c)) for a in range(n) for j in range(3)]

    def d2d(pin, out):
        x, y, c = _place()
        return [(dst, dst, (x, y, 1 - c)) for dst in landed(out, 0)]

    if over_ici:
        return _Phase(bufs, shapes, {a: a for a in range(n)}, 3 * n, ici, lambda pin, out: landed(out, 0))
    return _Phase(bufs, shapes, {a: a for a in range(n)}, 3 * n, d2d, lambda pin, out: landed(out, 1))


def _feed_phase(buf):
    def chips():
        x, y, _ = _place()
        return [(x if f < 2 else 1 - x, y if f % 2 == 0 else 1 - y) for f in (1, 2, 3)]

    def copies(pin, out):
        x, y, c = _place()
        mine = _half_rows(buf, c)[0]
        own = out[0].at[2 * x + y, mine]
        sent = [(own, own, (cx, cy, c)) for cx, cy in chips()]
        return sent + [(out[0].at[2 * cx + cy, mine], out[0].at[2 * cx + cy, mine], (x, y, 1 - c)) for cx, cy in chips()]

    def arrivals(pin, out):
        mine, theirs = _half_rows(buf, _place()[2])
        return [out[0].at[2 * cx + cy, rows] for rows in (mine, theirs) for cx, cy in chips()]

    return _Phase([buf], [jax.ShapeDtypeStruct(buf.shape, buf.dtype)], {0: 0}, 6, copies, arrivals,
                  own_starts=(3, 4, 5), own_waits=range(6))


def _rs_swap_phase(grads):
    n = len(grads)

    def copies(g, out):
        x, y, c = _place()
        return [(g[a].at[:, _half_rows(grads[a], c)[1]], out[a], (x, y, 1 - c)) for a in range(n)]

    shapes = [jax.ShapeDtypeStruct((N_CHIPS, g.shape[1] // 2, g.shape[2]), g.dtype) for g in grads]
    return _Phase(grads, shapes, {}, n, copies, lambda g, out: list(out))


def _rs_add_sibling(place, grads, gots, name):
    n = len(grads)
    _, R, C = grads[0].shape
    half = R // 2
    tr = _tile(half, max(16, (1 << 19) // C // 16 * 16), 16)
    nr = half // tr

    def body(p_ref, *refs):
        for a in range(n):
            refs[2 * n + a][...] = (refs[2 * a][...].astype(f32) + refs[2 * a + 1][...].astype(f32)).astype(bf16)

    res = pl.pallas_call(
        body, name=name,
        grid_spec=pltpu.PrefetchScalarGridSpec(
            num_scalar_prefetch=1, grid=(N_CHIPS, nr),
            in_specs=[pl.BlockSpec((1, tr, C), lambda k, r, p: (k, p[0] * nr + r, 0)),
                      pl.BlockSpec((1, tr, C), lambda k, r, p: (k, r, 0))] * n,
            out_specs=[pl.BlockSpec((1, tr, C), lambda k, r, p: (k, r, 0))] * n),
        out_shape=[jax.ShapeDtypeStruct((N_CHIPS, half, C), bf16)] * n,
        compiler_params=_params(("parallel", "parallel")),
    )(place, *[x for pair in zip(grads, gots) for x in pair])
    return list(res)


def _rs_chips_phase(parts):
    n = len(parts)

    def copies(p, fc):
        x, y, c = _place()
        chips, ks = _other_chips(x, y)
        return [(p[a].at[ks[j]], fc[a].at[j], (*chips[j], c)) for a in range(n) for j in range(3)]

    shapes = [jax.ShapeDtypeStruct((3,) + q.shape[1:], q.dtype) for q in parts]
    return _Phase(parts, shapes, {}, 3 * n, copies, lambda p, fc: [fc[a].at[j] for a in range(n) for j in range(3)])


def _rs_hand_phase(parts, from_chips):
    n = len(parts)

    def copies(pin, fs):
        x, y, c = _place()
        sib = (x, y, 1 - c)
        own = [(pin[a].at[2 * x + y], fs[a].at[0], sib) for a in range(n)]
        return own + [(pin[n + a].at[j], fs[a].at[1 + j], sib) for a in range(n) for j in range(3)]

    def arrivals(pin, fs):
        return [fs[a].at[0] for a in range(n)] + [fs[a].at[1 + j] for a in range(n) for j in range(3)]

    shapes = [jax.ShapeDtypeStruct((4,) + q.shape[1:], q.dtype) for q in parts]
    return _Phase(list(parts) + list(from_chips), shapes, {}, 4 * n, copies, arrivals)


class _Exchange:
    def __init__(self, place):
        self.place = place

    def feed(self, buf):
        return _feed_phase(buf)

    def gather(self, bufs, over_ici):
        return _gather_phase(bufs, over_ici)

    def swap(self, grads):
        return _rs_swap_phase(grads)

    def pair_sums(self, names, grads):
        return self.add(names, grads, _run_phases("rs_sibling_" + names[0], [_rs_swap_phase(grads)]))

    def add(self, names, grads, got):
        parts = {}
        for group in _same_shape(grads):
            res = _rs_add_sibling(self.place, [grads[i] for i in group], [got[i] for i in group],
                                  "rs_add_" + names[group[0]])
            parts.update(zip(group, res))
        return [parts[i] for i in range(len(names))]

    def to_chips(self, parts):
        return _rs_chips_phase(parts)

    def to_sibling(self, parts, from_chips):
        return _rs_hand_phase(parts, from_chips)

    def spread(self, blk):
        return _spread_phase(blk)

    def hand_over(self, name, parts, from_chips, blk):
        got = _run_phases(name, [_join([_rs_hand_phase(parts, from_chips), _spread_phase(blk)])])
        return got[:-1], got[-1]


def _local_step(place, x, target, norm_mix, b_gate, rb_chip, norm_ffn, norm_final, w_in, rest, exch):
    B, S, D = x.shape
    T = B * S
    x2 = x.reshape(T, D)
    tg2 = target.reshape(T, D)
    rope, decay = _rope_tables(S), _decay_tables()
    g_fin = norm_final.reshape(1, D)
    nrel = rb_chip.shape[-1]

    mrg, ffn = ["w_ret_out", "w_att_out", "w_out"], ["w_ffn_gate", "w_ffn_up", "w_ffn_down"]
    (xn, proj), got = _in_proj(place, x2, norm_mix, _join([exch.feed(w_in), exch.gather([rest[n] for n in mrg], True),
                                                           exch.spread(jnp.pad(rb_chip, ((0, 0), (0, 128 - nrel))))]))
    w_in, wb, rb_all = got[0], {}, got.pop()
    trows = _bias_rows(jnp.concatenate([rb_all[2 * k, :, :nrel] for k in range(N_CHIPS)], axis=1))
    (qr, kr, o, u, states), got = _ret_fwd(proj, B, S, rope, decay, _join([exch.gather([rest["w_ffn_gate"]], True),
                                                                         exch.gather(got[1:], False)]))
    wb.update(zip(mrg, got[1:]))
    (ao, probs), got = _att_fwd(proj, trows, B, S, _join([exch.gather([rest["w_ffn_up"], rest["w_ffn_down"]], True),
                                                    exch.gather(got[:1], False)]))
    wb["w_ffn_gate"] = got[2]
    w_ro, w_out = wb["w_ret_out"].reshape(-1, D), wb["w_out"].reshape(-1, D)
    (h1, yr, ya), got = _mix_fwd(x2, proj, u, ao, b_gate, w_ro, wb["w_att_out"], w_out, exch.gather(got[:2], False))
    wb.update(zip(ffn[1:], got))
    hn, a, b, f, dh2, part_fin = _ffn_fwd(h1, norm_ffn, wb["w_ffn_gate"], wb["w_ffn_up"], wb["w_ffn_down"], g_fin, tg2)

    da, db, dh1, part_ffn = _ffn_bwd(dh2, h1, norm_ffn, a, b, wb["w_ffn_gate"], wb["w_ffn_up"], wb["w_ffn_down"])
    ffn = ["w_ffn_down", "w_ffn_gate", "w_ffn_up"]
    g_ffn = [_wgrad(f, dh2, 0, "wgrad_ffn_down")[0], _wgrad(da, hn, 0, "wgrad_ffn_gate")[0],
             _wgrad(db, hn, 0, "wgrad_ffn_up")[0]]
    (du, dao, dgl, mix, dyr, dya, part_bg), x_ffn = _mix_bwd(dh1, proj, yr, ya, b_gate, w_ro, wb["w_att_out"], w_out,
                                                             exch.swap(g_ffn))
    p_ffn = exch.add(ffn, g_ffn, x_ffn)
    mrg = ["w_out", "w_ret_out", "w_att_out"]
    g_mrg = [_wgrad(mix, dh1, 0, "wgrad_out")[0], _wgrad(u, dyr, 0, "wgrad_ret_out")[0],
             _wgrad(ao, dya, 1, "wgrad_att_out")[0]]
    (dproj,), got = _ret_bwd(proj, qr, kr, o, states, du, dgl, B, S, rope, decay, _join([exch.to_chips(p_ffn[:2]),
                                                                                          exch.swap(g_mrg)]))
    c_two, p_mrg = got[:2], exch.add(mrg, g_mrg, got[2:])
    (dproj, dvec), got = _att_bwd(proj, dao, probs, dproj, B, S, exch.to_chips(p_ffn[2:] + p_mrg))
    c_ffn, c_mrg = c_two + got[:1], got[1:]
    g_in, got = _wgrad(xn, dproj, 1, "wgrad_in", exch.to_sibling(p_ffn + p_mrg, c_ffn + c_mrg))
    s_ffn, s_mrg = got[:len(ffn)], got[len(ffn):]
    p_in = exch.pair_sums(["w_in"], [g_in])
    (gx, part_mix), c_in = _in_proj_bwd(dproj, w_in, x2, norm_mix, dh1, exch.to_chips(p_in))
    rows = lambda p, r: p.reshape(-1, 8, p.shape[-1])[:, r, :].sum(axis=0)
    lo = KWIN - 1 - (MAX_REL - 1)
    drb = jnp.concatenate([jnp.flip(dvec[:, lo:lo + N_REL - 1], axis=1), dvec[:, :lo].sum(axis=1, keepdims=True)], axis=1)
    gsmall = {
        "norm_mix": rows(part_mix, 0), "b_gate": rows(part_bg, 0), "rel_bias": drb, "norm_ffn": rows(part_ffn, 0),
        "norm_final": rows(part_fin, 0),
    }
    s_in, small_all = exch.hand_over("rs_hand_w_in", p_in, c_in, _pack_small(gsmall, rows(part_fin, 1)))
    gbig = dict(zip(ffn + mrg + ["w_in"], zip(p_ffn + p_mrg + p_in, c_ffn + c_mrg + c_in, s_ffn + s_mrg + s_in)))
    return gx.reshape(B, S, D), gbig, small_all


SMALL_ROWS = 16


def _pack_small(gs, loss_lanes):
    D = D_MODEL
    rb = jnp.pad(gs["rel_bias"].reshape(-1), (0, 3 * D - ATT_HEADS * N_REL)).reshape(3, D)
    rows = [gs["norm_mix"].reshape(1, D), gs["b_gate"].reshape(2, D), gs["norm_ffn"].reshape(1, D),
            gs["norm_final"].reshape(1, D), rb, loss_lanes.reshape(1, D)]
    used = sum(r.shape[0] for r in rows)
    return jnp.concatenate(rows + [jnp.zeros((SMALL_ROWS - used, D), f32)], axis=0)


def kernel(x, norm_mix, w_in, b_gate, rel_bias, w_ret_out, w_att_out, w_out, norm_ffn, w_ffn_gate, w_ffn_up, w_ffn_down, norm_final, loss_target, m_norm_mix, m_w_in, m_b_gate, m_rel_bias, m_w_ret_out, m_w_att_out, m_w_out, m_norm_ffn, m_w_ffn_gate, m_w_ffn_up, m_w_ffn_down, m_norm_final, v_norm_mix, v_w_in, v_b_gate, v_rel_bias, v_w_ret_out, v_w_att_out, v_w_out, v_norm_ffn, v_w_ffn_gate, v_w_ffn_up, v_w_ffn_down, v_norm_final):
    w = dict(norm_mix=norm_mix, w_in=w_in, b_gate=b_gate, rel_bias=rel_bias, w_ret_out=w_ret_out, w_att_out=w_att_out,
             w_out=w_out, norm_ffn=norm_ffn, w_ffn_gate=w_ffn_gate, w_ffn_up=w_ffn_up, w_ffn_down=w_ffn_down,
             norm_final=norm_final)
    m = dict(norm_mix=m_norm_mix, w_in=m_w_in, b_gate=m_b_gate, rel_bias=m_rel_bias, w_ret_out=m_w_ret_out,
             w_att_out=m_w_att_out, w_out=m_w_out, norm_ffn=m_norm_ffn, w_ffn_gate=m_w_ffn_gate, w_ffn_up=m_w_ffn_up,
             w_ffn_down=m_w_ffn_down, norm_final=m_norm_final)
    v = dict(norm_mix=v_norm_mix, w_in=v_w_in, b_gate=v_b_gate, rel_bias=v_rel_bias, w_ret_out=v_w_ret_out,
             w_att_out=v_w_att_out, w_out=v_w_out, norm_ffn=v_norm_ffn, w_ffn_gate=v_w_ffn_gate, w_ffn_up=v_w_ffn_up,
             w_ffn_down=v_w_ffn_down, norm_final=v_norm_final)
    xi, yi, ci = _place()
    k_me = 2 * xi + yi

    place = jnp.stack([ci, k_me]).astype(jnp.int32)
    big = [n for n, _ in BIG]

    turned = ("w_ffn_gate", "w_ffn_up")
    shard = lambda d, n: jnp.swapaxes(d[n][0], 0, 1) if n in turned else d[n][0]
    whole = lambda a, n: (jnp.swapaxes(a, 0, 1) if n in turned else a)[None]

    by_shape = [[big[i] for i in group] for group in _same_shape([shard(w, n) for n in big])]
    bufs = {}
    for names in by_shape:
        bufs.update(zip(names, _cast_shards(place, [shard(w, n) for n in names], "cast_" + names[0])))
    rest = {n: bufs[n] for n in big if n != "w_in"}
    nrel_loc = rel_bias.shape[-1]
    grad_x, gbig, small_all = _local_step(place, x, loss_target, norm_mix, b_gate, rel_bias[0], norm_ffn, norm_final,
                                          bufs["w_in"], rest, _Exchange(place))

    small = _sum_slots(small_all, "reduce_small")
    D = D_MODEL
    loss = jnp.sum(small[8])
    drb_full = small[5:8].reshape(-1)[:ATT_HEADS * N_REL].reshape(ATT_HEADS, N_REL)
    g = {
        "norm_mix": small[0:1], "b_gate": small[1:3].reshape(1, 2 * D), "norm_ffn": small[3:4], "norm_final": small[4],
        "rel_bias": lax.dynamic_slice_in_dim(drb_full, k_me * nrel_loc, nrel_loc, axis=1)[None],
    }

    delta, new_m, new_v = {}, {}, {}
    for names in by_shape:
        res = _adamw_sum(place, [(shard(w, n), shard(m, n), shard(v, n), *gbig[n]) for n in names], "adamw_" + names[0])
        for n, (g_, d_, m_, v_) in zip(names, res):
            g[n], delta[n], new_m[n], new_v[n] = whole(g_, n), whole(d_, n), whole(m_, n), whole(v_, n)
    flat = lambda d: jnp.concatenate([d[n].reshape(-1) for n in SMALL])
    n_small = sum(int(np.prod(w[n].shape)) for n in SMALL)
    n_pad = -n_small % 1024
    packs = [jnp.pad(flat(d), (0, n_pad)).reshape(-1, 128) for d in (w, g, m, v)]
    outs = _adamw(*packs, "adamw_small")
    for res, dst in zip(outs, (delta, new_m, new_v)):
        off = 0
        fl = res.reshape(-1)
        for n in SMALL:
            sz = int(np.prod(w[n].shape))
            dst[n] = fl[off:off + sz].reshape(w[n].shape)
            off += sz

    return (loss, grad_x, *[g[n] for n in WEIGHTS], *[delta[n] for n in WEIGHTS], *[new_m[n] for n in WEIGHTS],
            *[new_v[n] for n in WEIGHTS])
```

```python
import functools

import numpy as np
import jax
import jax.numpy as jnp
from jax import lax
from jax.experimental import pallas as pl
from jax.experimental.pallas import tpu as pltpu

f32 = jnp.float32
bf16 = jnp.bfloat16

D_MODEL = 1024
CHUNK = 64
RET_HEADS = 4
RET_KEY_DIM = 128
RET_VAL_DIM = 256
ATT_HEADS = 8
ATT_HEAD_DIM = 64
ATT_W = ATT_HEADS * ATT_HEAD_DIM
BAND_CHUNKS = 8
PAD = BAND_CHUNKS * CHUNK
MAX_REL = 256
N_REL = CHUNK + MAX_REL
D_FF = 2816
N_IN = 6656
ROPE_BASE = 10000.0
EPS = 1e-6
NEG_INF = -1e30
C_RQ, C_RK, C_RV, C_RG, C_AQ, C_AK, C_AV, C_GL = 0, 512, 1024, 2048, 3072, 3584, 4096, 4608

ADAM_LR, ADAM_B1, ADAM_B2, ADAM_EPS, ADAM_WD, ADAM_STEP = 0.001, 0.9, 0.999, 1e-08, 0.01, 10

N_CHIPS = 4
N_DEV = 8
WGRAD_ACC_BYTES = 8 * 1024 * 1024
WGRAD_VMEM_BYTES = 40 * 1024 * 1024
ROW_TILE = 512
BIG_ROW_TILE = 1024
IN_ORDER = (0, 2, 3, 1)
QBLK = 256
KWIN = PAD + QBLK
TOEP = 1024
VMEM_LIMIT = 56 * 1024 * 1024
MESH = pl.DeviceIdType.MESH

BIG = (
    ("w_in", 1), ("w_ret_out", 0), ("w_att_out", 1), ("w_out", 0), ("w_ffn_gate", 1), ("w_ffn_up", 1), ("w_ffn_down", 0))
WEIGHTS = ("norm_mix", "w_in", "b_gate", "rel_bias", "w_ret_out", "w_att_out", "w_out", "norm_ffn", "w_ffn_gate",
           "w_ffn_up", "w_ffn_down", "norm_final")
SMALL = ("norm_mix", "b_gate", "rel_bias", "norm_ffn", "norm_final")


def _dot(a, b):
    return lax.dot_general(a, b, (((1,), (0,)), ((), ())), preferred_element_type=f32)


def _dot_nt(a, b):
    return lax.dot_general(a, b, (((1,), (1,)), ((), ())), preferred_element_type=f32)


def _dot_tn(a, b):
    return lax.dot_general(a, b, (((0,), (0,)), ((), ())), preferred_element_type=f32)


def _sig(x):
    return 1.0 / (1.0 + jnp.exp(-x))


def _tile(n, pref, mult):
    best = None
    for t in range(mult, min(n, pref) + 1, mult):
        if n % t == 0:
            best = t
    return best if best is not None else n


def _same_shape(arrays):
    groups = {}
    for i, a in enumerate(arrays):
        groups.setdefault(a.shape, []).append(i)
    return list(groups.values())


def _params(sem, vmem=VMEM_LIMIT):
    return pltpu.CompilerParams(dimension_semantics=sem, vmem_limit_bytes=vmem)


def _in_proj(place, x2, gamma, phase):
    T, D = x2.shape
    _, _, ns = phase.arrays[0].shape
    tm = _tile(T, BIG_ROW_TILE, 8)
    ni = T // tm
    pass_chip = lambda j: sum(jnp.where(j == n, f, 0) for n, f in enumerate(IN_ORDER))

    def body(p_ref, x_ref, g_ref, xn_ref, pr_ref, xs_ref, w_ref, w_sem, carried):
        j, i = pl.program_id(0), pl.program_id(1)
        pin, pout, sems = carried
        rows = pl.ds(pl.multiple_of(i * tm, tm), tm)

        @pl.when(i == 0)
        def _():
            for n, f in enumerate(IN_ORDER):
                if f:
                    @pl.when(j == n)
                    def _():
                        phase.arrived(f - 1, pin, pout, *sems)
                        phase.begin(2 + f, pin, pout, *sems)
                        phase.arrived(2 + f, pin, pout, *sems)
            shard = pltpu.make_async_copy(pout[0].at[jnp.bitwise_xor(p_ref[1], pass_chip(j))], w_ref, w_sem)
            shard.start()
            shard.wait()

        @pl.when(j == 0)
        def _():
            x = x_ref[...]
            r = lax.rsqrt(jnp.mean(x * x, axis=-1, keepdims=True) + EPS)
            xn = (x * r * g_ref[...]).astype(bf16)
            xs_ref[rows, :] = xn
            xn_ref[...] = xn

        pr_ref[...] = _dot(xs_ref[rows, :], w_ref[...]).astype(bf16)

    first_pass = lambda j, i, p: (jnp.where(j == 0, i, ni - 1), 0)
    return _call(
        body, phase, name="in_proj", grid=(N_CHIPS, ni), prefetch=(place,), expose=True,
        in_specs=[pl.BlockSpec((tm, D), first_pass), pl.BlockSpec((1, D), lambda j, i, p: (0, 0))],
        out_specs=[pl.BlockSpec((tm, D), first_pass),
                   pl.BlockSpec((tm, ns), lambda j, i, p: (i, jnp.bitwise_xor(p[1], pass_chip(j))))],
        out_shape=[jax.ShapeDtypeStruct((T, D), bf16), jax.ShapeDtypeStruct((T, N_CHIPS * ns), bf16)],
        scratch_shapes=[pltpu.VMEM((T, D), bf16), pltpu.VMEM((D, ns), bf16), pltpu.SemaphoreType.DMA],
        args=(x2, gamma))


def _rope_tables(S):
    d = RET_KEY_DIM
    freqs = ROPE_BASE ** (-jnp.arange(0, d, 2, dtype=f32) / d)
    ang = jnp.arange(S, dtype=f32)[:, None] * freqs[None, :]
    cos, sin = jnp.cos(ang), jnp.sin(ang)
    return jnp.concatenate([cos, cos], axis=1), jnp.concatenate([-sin, sin], axis=1)


def _decay_tables():
    H = RET_HEADS
    log_g = jnp.log(1.0 - 2.0 ** (-5.0 - jnp.arange(H, dtype=f32)))
    p = jnp.arange(CHUNK, dtype=f32)
    intra = jnp.exp(log_g[:, None, None] * jnp.abs(p[:, None] - p[None, :]))
    q_dec = jnp.exp(log_g[:, None] * (p[None, :] + 1.0))
    k_dec = jnp.exp(log_g[:, None] * (CHUNK - 1.0 - p[None, :]))
    c_dec = jnp.exp(log_g * CHUNK)
    q_dec = jnp.broadcast_to(q_dec[:, :, None], (H, CHUNK, RET_KEY_DIM))
    k_dec = jnp.broadcast_to(k_dec[:, :, None], (H, CHUNK, RET_KEY_DIM))
    c_dec = jnp.broadcast_to(c_dec[:, None, None], (H, 1, RET_VAL_DIM))
    return intra, q_dec, k_dec, c_dec


K_SCALE = RET_KEY_DIM ** -0.5


RET_CHUNKS = 4


def _ret_tables_specs():
    whole = lambda *shape: pl.BlockSpec(shape, lambda b, i: (0,) * len(shape))
    return [whole(RET_HEADS, CHUNK, CHUNK), whole(RET_HEADS, CHUNK, RET_KEY_DIM), whole(RET_HEADS, CHUNK, RET_KEY_DIM),
            whole(RET_HEADS, 1, RET_VAL_DIM)]


def _rotate(x, cos, sn):
    return x * cos + pltpu.roll(x, RET_KEY_DIM // 2, 1) * sn


def _ret_fwd(proj, B, S, rope, decay, phase=None):
    T = B * S
    nc = S // CHUNK
    H, dk, dv = RET_HEADS, RET_KEY_DIM, RET_VAL_DIM
    sb = RET_CHUNKS * CHUNK
    ns = S // sb

    def body(q_ref, k_ref, v_ref, g_ref, cos_ref, sin_ref, intra_ref, qd_ref, kd_ref, cd_ref,
             qr_ref, kr_ref, o_ref, u_ref, st_ref, state_ref):
        @pl.when(pl.program_id(1) == 0)
        def _():
            state_ref[...] = jnp.zeros_like(state_ref)

        cos, sn = cos_ref[...], sin_ref[...]
        for h in range(H):
            hs = slice(h * dk, (h + 1) * dk)
            qr_ref[:, hs] = _rotate(q_ref[:, hs].astype(f32), cos, sn).astype(bf16)
            kr_ref[:, hs] = (_rotate(k_ref[:, hs].astype(f32), cos, sn) * K_SCALE).astype(bf16)
        states = [state_ref[h] for h in range(H)]
        for ci in range(RET_CHUNKS):
            r = slice(ci * CHUNK, (ci + 1) * CHUNK)
            for h in range(H):
                hk, hv = slice(h * dk, (h + 1) * dk), slice(h * dv, (h + 1) * dv)
                qi, ki, vi = qr_ref[r, hk], kr_ref[r, hk], v_ref[r, hv]
                stb = states[h].astype(bf16)
                st_ref[0, h, ci] = stb
                s = (_dot_nt(qi, ki) * intra_ref[h]).astype(bf16)
                o = _dot(s, vi) + _dot((qi.astype(f32) * qd_ref[h]).astype(bf16), stb)
                states[h] = states[h] * cd_ref[h] + _dot_tn((ki.astype(f32) * kd_ref[h]).astype(bf16), vi)
                mu = jnp.mean(o, axis=-1, keepdims=True)
                xc = o - mu
                var = jnp.mean(xc * xc, axis=-1, keepdims=True)
                oh = xc * lax.rsqrt(var + EPS)
                g = g_ref[r, hv].astype(f32)
                o_ref[r, hv] = o.astype(bf16)
                u_ref[r, hv] = (g * _sig(g) * oh).astype(bf16)
        for h in range(H):
            state_ref[h] = states[h]

    blk = lambda w, c: pl.BlockSpec((sb, w), lambda b, i: (b * ns + i, c))
    return _call(
        body, phase, name="ret_fwd", grid=(B, ns), scratch_shapes=[pltpu.VMEM((H, dk, dv), f32)],
        in_specs=[blk(H * dk, C_RQ // (H * dk)), blk(H * dk, C_RK // (H * dk)), blk(H * dv, C_RV // (H * dv)),
                  blk(H * dv, C_RG // (H * dv)),
                  pl.BlockSpec((sb, dk), lambda b, i: (i, 0)), pl.BlockSpec((sb, dk), lambda b, i: (i, 0)),
                  *_ret_tables_specs()],
        out_specs=[blk(H * dk, 0), blk(H * dk, 0), blk(H * dv, 0), blk(H * dv, 0),
                   pl.BlockSpec((1, H, RET_CHUNKS, dk, dv), lambda b, i: (b, 0, i, 0, 0))],
        out_shape=[jax.ShapeDtypeStruct((T, H * dk), bf16), jax.ShapeDtypeStruct((T, H * dk), bf16),
                   jax.ShapeDtypeStruct((T, H * dv), bf16), jax.ShapeDtypeStruct((T, H * dv), bf16),
                   jax.ShapeDtypeStruct((B, H, nc, dk, dv), bf16)],
        args=(proj, proj, proj, proj, *rope, *decay))


def _bias_rows(rb):
    last = rb[:, N_REL - 1:]
    return jnp.concatenate([
        jnp.broadcast_to(last, (ATT_HEADS, PAD - MAX_REL + 1)),
        jnp.flip(rb[:, :N_REL - 1], axis=1),
        jnp.broadcast_to(rb[:, :1], (ATT_HEADS, KWIN - PAD - CHUNK)),
        jnp.broadcast_to(last, (ATT_HEADS, TOEP - KWIN)),
    ], axis=1)


def _build_bias(t_ref, bias_ref):
    row = lax.broadcasted_iota(jnp.int32, (QBLK, KWIN), 0) // CHUNK
    col = lax.broadcasted_iota(jnp.int32, (QBLK, KWIN), 1) // CHUNK
    delta = BAND_CHUNKS + row - col
    vis = (delta >= 0) & (delta <= BAND_CHUNKS)
    for h in range(ATT_HEADS):
        t = jnp.broadcast_to(t_ref[h:h + 1, :], (QBLK, TOEP))
        rolled = pltpu.roll(t, 0, 1, stride=1, stride_axis=0)
        bias_ref[h] = jnp.where(vis, rolled[:, :KWIN], NEG_INF)


ATT_SCALE = ATT_HEAD_DIM ** -0.5


def _att_probs(qh, kh, bias):
    s = _dot_nt(qh, kh) + bias
    m = jnp.max(s, axis=-1, keepdims=True)
    p = jnp.exp(s - m)
    return p * (1.0 / jnp.sum(p, axis=-1, keepdims=True))


def _first_of_pair():
    return lax.broadcasted_iota(jnp.int32, (1, 2 * ATT_HEAD_DIM), 1) < ATT_HEAD_DIM


def _by_window(i, step):
    sizes = list(range(QBLK, KWIN, QBLK))
    for n, nk in enumerate(sizes):
        pl.when(i == n)(functools.partial(step, nk))
    pl.when(i >= len(sizes))(functools.partial(step, KWIN))


def _att_fwd(proj, trows, B, S, phase=None):
    T = B * S
    nq = S // QBLK
    dh = ATT_HEAD_DIM

    def body(q_ref, k_ref, v_ref, t_ref, o_ref, p_ref, bias_ref):
        i = pl.program_id(1)

        @pl.when((pl.program_id(0) == 0) & (i == 0))
        def _():
            _build_bias(t_ref, bias_ref)

        def step(nk):
            win = pl.ds(pl.multiple_of((i + 1) * QBLK - nk, QBLK), nk)
            kw, vw = k_ref[win, :], v_ref[win, :]
            first = _first_of_pair()
            outs = []
            for p in range(ATT_HEADS // 2):
                ps = slice(2 * p * dh, 2 * (p + 1) * dh)
                q2, k2, v2 = q_ref[:, ps] * ATT_SCALE, kw[:, ps], vw[:, ps]
                both = []
                for e in range(2):
                    qm = jnp.where(first == (e == 0), q2, jnp.zeros_like(q2))
                    pr = _att_probs(qm, k2, bias_ref[2 * p + e, :, KWIN - nk:]).astype(bf16)
                    p_ref[0, 2 * p + e, :, KWIN - nk:] = pr
                    both.append(_dot(pr, v2))
                outs.append(jnp.where(first, both[0], both[1]))
            o_ref[...] = jnp.concatenate(outs, axis=1).astype(bf16)

        _by_window(i, step)

    return _call(
        body, phase, name="att_fwd", grid=(B, nq),
        in_specs=[pl.BlockSpec((QBLK, ATT_W), lambda b, i: (b * nq + i, C_AQ // ATT_W)),
                  pl.BlockSpec((S, ATT_W), lambda b, i: (b, C_AK // ATT_W)),
                  pl.BlockSpec((S, ATT_W), lambda b, i: (b, C_AV // ATT_W)),
                  pl.BlockSpec((ATT_HEADS, TOEP), lambda b, i: (0, 0))],
        out_specs=[pl.BlockSpec((QBLK, ATT_W), lambda b, i: (b * nq + i, 0)),
                   pl.BlockSpec((1, ATT_HEADS, QBLK, KWIN), lambda b, i: (b * nq + i, 0, 0, 0))],
        out_shape=[jax.ShapeDtypeStruct((T, ATT_W), bf16), jax.ShapeDtypeStruct((B * nq, ATT_HEADS, QBLK, KWIN), bf16)],
        scratch_shapes=[pltpu.VMEM((ATT_HEADS, QBLK, KWIN), f32)],
        args=(proj, proj, proj, trows))


def _gl_specs(tm):
    w = 512
    return [pl.BlockSpec((tm, w), functools.partial(lambda i, j: (i, C_GL // 512 + j), j=j)) for j in range(4)]


def _gates(gl_refs, bg_ref):
    gl = jnp.concatenate([r[...] for r in gl_refs], axis=1).astype(f32) + bg_ref[...]
    g = _sig(gl)
    return g[:, :D_MODEL], g[:, D_MODEL:]


def _mix_fwd(x2, proj, u, ao, b_gate, w_ro, w_ao, w_out, phase=None):
    T, D = x2.shape
    tm = _tile(T, ROW_TILE, 8)

    def body(x_ref, u_ref, ao_ref, g0, g1, g2, g3, bg_ref, wro_ref, wao_ref, wo_ref, h1_ref, yr_ref, ya_ref):
        yr = _dot(u_ref[...], wro_ref[...])
        ao = ao_ref[...]
        ya = jnp.concatenate([_dot(ao, wao_ref[k]) for k in range(N_CHIPS)], axis=1)
        gr, ga = _gates((g0, g1, g2, g3), bg_ref)
        mix = gr * yr + ga * ya
        h1_ref[...] = x_ref[...] + _dot(mix.astype(bf16), wo_ref[...])
        yr_ref[...] = yr.astype(bf16)
        ya_ref[...] = ya.astype(bf16)

    full = lambda a: pl.BlockSpec(a.shape, lambda i: (0,) * a.ndim)
    row = lambda n: pl.BlockSpec((tm, n), lambda i: (i, 0))
    return _call(
        body, phase, name="mix_fwd", grid=(T // tm,), scratch_shapes=[],
        in_specs=[row(D), row(D), row(ATT_W), *_gl_specs(tm), full(b_gate), full(w_ro), full(w_ao), full(w_out)],
        out_specs=[row(D), row(D), row(D)],
        out_shape=[jax.ShapeDtypeStruct((T, D), f32), jax.ShapeDtypeStruct((T, D), bf16),
                   jax.ShapeDtypeStruct((T, D), bf16)],
        args=(x2, u, ao, proj, proj, proj, proj, b_gate, w_ro, w_ao, w_out))


def _ffn_fwd(h1, g_ffn, wg, wu, wd, g_fin, target):
    T, D = h1.shape
    nf, tf, _ = wg.shape
    tm = _tile(T, ROW_TILE, 8)

    def body(h1_ref, g_ref, wg_ref, wu_ref, wd_ref, gf_ref, tg_ref, hn_ref, a_ref, b_ref, f_ref, dh2_ref, dh2b_ref,
             part_ref):
        h1v = h1_ref[...]
        r = lax.rsqrt(jnp.mean(h1v * h1v, axis=-1, keepdims=True) + EPS)
        hn = (h1v * r * g_ref[...]).astype(bf16)
        hn_ref[...] = hn
        h2 = h1v
        for k in range(nf):
            a = _dot_nt(hn, wg_ref[k])
            b = _dot_nt(hn, wu_ref[k])
            f = ((a * _sig(a)) * b).astype(bf16)
            a_ref[k] = a.astype(bf16)
            b_ref[k] = b.astype(bf16)
            f_ref[k] = f
            h2 = h2 + _dot(f, wd_ref[k])
        r = lax.rsqrt(jnp.mean(h2 * h2, axis=-1, keepdims=True) + EPS)
        n = h2 * r
        gf = gf_ref[...]
        e = n * gf - tg_ref[...]
        dy = e * (1.0 / D)
        dn = dy * gf
        dh2 = r * (dn - n * jnp.mean(dn * n, axis=-1, keepdims=True))
        dh2_ref[...] = dh2
        dh2b_ref[...] = dh2.astype(bf16)
        part_ref[...] = jnp.zeros_like(part_ref)
        part_ref[0:1, :] = jnp.sum(dy * n, axis=0, keepdims=True)
        part_ref[1:2, :] = (0.5 / D) * jnp.sum(e * e, axis=0, keepdims=True)

    row = lambda n: pl.BlockSpec((tm, n), lambda i: (i, 0))
    vec = pl.BlockSpec((1, D), lambda i: (0, 0))
    col = pl.BlockSpec((nf, tm, tf), lambda i: (0, i, 0))
    held = lambda w: pl.BlockSpec(w.shape, lambda i: (0, 0, 0), pipeline_mode=pl.Buffered(1))
    act = jax.ShapeDtypeStruct((nf, T, tf), bf16)
    return pl.pallas_call(
        body, name="ffn_fwd", grid=(T // tm,),
        in_specs=[row(D), vec, held(wg), held(wu), held(wd), vec, row(D)],
        out_specs=[row(D), col, col, col, row(D), row(D), pl.BlockSpec((8, D), lambda i: (i, 0))],
        out_shape=[jax.ShapeDtypeStruct((T, D), bf16), act, act, act, jax.ShapeDtypeStruct((T, D), f32),
                   jax.ShapeDtypeStruct((T, D), bf16), jax.ShapeDtypeStruct((T // tm * 8, D), f32)],
        compiler_params=_params(("parallel",)),
    )(h1, g_ffn, wg, wu, wd, g_fin, target)


def _ffn_bwd(dh2, h1, g_ffn, a, b, wg, wu, wd):
    T, D = h1.shape
    nf, tf, _ = wg.shape
    tm = _tile(T, ROW_TILE // 2, 8)

    def body(dh2_ref, h1_ref, g_ref, a_ref, b_ref, wg_ref, wu_ref, wd_ref, da_ref, db_ref, dh1_ref, dh1b_ref, part_ref):
        dh2v = dh2_ref[...]
        dh2b = dh2v.astype(bf16)
        dhn = jnp.zeros((tm, D), f32)
        for k in range(nf):
            df = _dot_nt(dh2b, wd_ref[k])
            av = a_ref[k].astype(f32)
            sg = _sig(av)
            db = (df * (av * sg)).astype(bf16)
            da = (df * b_ref[k].astype(f32) * (sg * (1.0 + av * (1.0 - sg)))).astype(bf16)
            da_ref[k] = da
            db_ref[k] = db
            dhn = dhn + _dot(da, wg_ref[k]) + _dot(db, wu_ref[k])
        h = h1_ref[...]
        r = lax.rsqrt(jnp.mean(h * h, axis=-1, keepdims=True) + EPS)
        n = h * r
        dn = dhn * g_ref[...]
        dh1 = dh2v + r * (dn - n * jnp.mean(dn * n, axis=-1, keepdims=True))
        dh1_ref[...] = dh1
        dh1b_ref[...] = dh1.astype(bf16)
        part_ref[...] = jnp.zeros_like(part_ref)
        part_ref[0:1, :] = jnp.sum(dhn * n, axis=0, keepdims=True)

    row = lambda n: pl.BlockSpec((tm, n), lambda i: (i, 0))
    col = pl.BlockSpec((nf, tm, tf), lambda i: (0, i, 0))
    held = lambda w: pl.BlockSpec(w.shape, lambda i: (0, 0, 0), pipeline_mode=pl.Buffered(1))
    act = jax.ShapeDtypeStruct((nf, T, tf), bf16)
    return pl.pallas_call(
        body, name="ffn_bwd", grid=(T // tm,),
        in_specs=[row(D), row(D), pl.BlockSpec((1, D), lambda i: (0, 0)), col, col, held(wg), held(wu), held(wd)],
        out_specs=[col, col, row(D), row(D), pl.BlockSpec((8, D), lambda i: (i, 0))],
        out_shape=[act, act, jax.ShapeDtypeStruct((T, D), f32), jax.ShapeDtypeStruct((T, D), bf16),
                   jax.ShapeDtypeStruct((T // tm * 8, D), f32)],
        compiler_params=_params(("parallel",)),
    )(dh2, h1, g_ffn, a, b, wg, wu, wd)


def _mix_bwd(dh1, proj, yr, ya, b_gate, w_ro, w_ao, w_out, phase=None):
    T, D = dh1.shape
    tm = _tile(T, ROW_TILE, 8)

    def body(dh1_ref, g0, g1, g2, g3, bg_ref, yr_ref, ya_ref, wro_ref, wao_ref, wo_ref,
             du_ref, dao_ref, dgl_ref, mix_ref, dyr_ref, dya_ref, part_ref):
        dmix = _dot_nt(dh1_ref[...], wo_ref[...])
        gr, ga = _gates((g0, g1, g2, g3), bg_ref)
        yr = yr_ref[...].astype(f32)
        ya = ya_ref[...].astype(f32)
        dyr = (dmix * gr).astype(bf16)
        dya = (dmix * ga).astype(bf16)
        dgl = jnp.concatenate([dmix * yr * gr * (1.0 - gr), dmix * ya * ga * (1.0 - ga)], axis=1)
        du_ref[...] = _dot_nt(dyr, wro_ref[...]).astype(bf16)
        ns = wao_ref.shape[2]
        dao = _dot_nt(dya[:, :ns], wao_ref[0])
        for k in range(1, N_CHIPS):
            dao = dao + _dot_nt(dya[:, k * ns:(k + 1) * ns], wao_ref[k])
        dao_ref[...] = dao.astype(bf16)
        dgl_ref[...] = dgl.astype(bf16)
        mix_ref[...] = (gr * yr + ga * ya).astype(bf16)
        dyr_ref[...] = dyr
        dya_ref[...] = dya
        part_ref[...] = jnp.zeros_like(part_ref)
        part_ref[0:1, :] = jnp.sum(dgl, axis=0, keepdims=True)

    full = lambda a: pl.BlockSpec(a.shape, lambda i: (0,) * a.ndim)
    row = lambda n: pl.BlockSpec((tm, n), lambda i: (i, 0))
    return _call(
        body, phase, name="mix_bwd", grid=(T // tm,), scratch_shapes=[],
        in_specs=[row(D), *_gl_specs(tm), full(b_gate), row(D), row(D), full(w_ro), full(w_ao), full(w_out)],
        out_specs=[row(D), row(ATT_W), row(2 * D), row(D), row(D), row(D), pl.BlockSpec((8, 2 * D), lambda i: (i, 0))],
        out_shape=[jax.ShapeDtypeStruct((T, D), bf16), jax.ShapeDtypeStruct((T, ATT_W), bf16),
                   jax.ShapeDtypeStruct((T, 2 * D), bf16), jax.ShapeDtypeStruct((T, D), bf16),
                   jax.ShapeDtypeStruct((T, D), bf16), jax.ShapeDtypeStruct((T, D), bf16),
                   jax.ShapeDtypeStruct((T // tm * 8, 2 * D), f32)],
        args=(dh1, proj, proj, proj, proj, b_gate, yr, ya, w_ro, w_ao, w_out))


def _ret_bwd(proj, qr, kr, o, states, du, dgl, B, S, rope, decay, phase=None):
    T = B * S
    H, dk, dv = RET_HEADS, RET_KEY_DIM, RET_VAL_DIM
    sb = RET_CHUNKS * CHUNK
    ns = S // sb

    def body(qr_ref, kr_ref, v_ref, g_ref, o_ref, st_ref, du_ref, dgl_ref, cos_ref, sin_ref, intra_ref, qd_ref, kd_ref,
             cd_ref, dp_ref, dstate_ref):
        dq_ref, dk_ref = dp_ref.at[:, pl.ds(C_RQ, H * dk)], dp_ref.at[:, pl.ds(C_RK, H * dk)]
        dv_ref, dg_ref = dp_ref.at[:, pl.ds(C_RV, H * dv)], dp_ref.at[:, pl.ds(C_RG, H * dv)]
        dp_ref[:, C_GL:] = dgl_ref[...]

        @pl.when(pl.program_id(1) == 0)
        def _():
            dstate_ref[...] = jnp.zeros_like(dstate_ref)

        cos, snb = cos_ref[...], -sin_ref[...]
        dstates = [dstate_ref[h] for h in range(H)]
        for ci in reversed(range(RET_CHUNKS)):
            r = slice(ci * CHUNK, (ci + 1) * CHUNK)
            for h in range(H):
                hk, hv = slice(h * dk, (h + 1) * dk), slice(h * dv, (h + 1) * dv)
                intra, qd, kd = intra_ref[h], qd_ref[h], kd_ref[h]
                qi, ki, vi = qr_ref[r, hk], kr_ref[r, hk], v_ref[r, hv]
                si = st_ref[0, h, ci]
                o = o_ref[r, hv].astype(f32)
                mu = jnp.mean(o, axis=-1, keepdims=True)
                xc = o - mu
                rstd = lax.rsqrt(jnp.mean(xc * xc, axis=-1, keepdims=True) + EPS)
                oh = xc * rstd
                g = g_ref[r, hv].astype(f32)
                sg = _sig(g)
                dui = du_ref[r, hv].astype(f32)
                dg_ref[r, hv] = (dui * oh * (sg * (1.0 + g * (1.0 - sg)))).astype(bf16)
                doh = dui * (g * sg)
                do = rstd * (doh - jnp.mean(doh, axis=-1, keepdims=True)
                             - oh * jnp.mean(doh * oh, axis=-1, keepdims=True))
                dob = do.astype(bf16)
                p = (_dot_nt(qi, ki) * intra).astype(bf16)
                dsb = dstates[h].astype(bf16)
                kt = (ki.astype(f32) * kd).astype(bf16)
                qt = (qi.astype(f32) * qd).astype(bf16)
                dv_ref[r, hv] = (_dot_tn(p, dob) + _dot(kt, dsb)).astype(bf16)
                da = (_dot_nt(dob, vi) * intra).astype(bf16)
                dq = _dot(da, ki) + _dot_nt(dob, si) * qd
                dkk = (_dot_tn(da, qi) + _dot_nt(vi, dsb) * kd) * K_SCALE
                dq_ref[r, hk] = _rotate(dq, cos[r], snb[r]).astype(bf16)
                dk_ref[r, hk] = _rotate(dkk, cos[r], snb[r]).astype(bf16)
                dstates[h] = dstates[h] * cd_ref[h] + _dot_tn(qt, dob)
        for h in range(H):
            dstate_ref[h] = dstates[h]

    blk = lambda w, c: pl.BlockSpec((sb, w), lambda b, i: (b * ns + ns - 1 - i, c))
    return _call(
        body, phase, name="ret_bwd", grid=(B, ns),
        in_specs=[blk(H * dk, 0), blk(H * dk, 0), blk(H * dv, C_RV // (H * dv)), blk(H * dv, C_RG // (H * dv)),
                  blk(H * dv, 0),
                  pl.BlockSpec((1, H, RET_CHUNKS, dk, dv), lambda b, i: (b, 0, ns - 1 - i, 0, 0)),
                  blk(H * dv, 0), blk(N_IN - C_GL, 0),
                  pl.BlockSpec((sb, dk), lambda b, i: (ns - 1 - i, 0)), pl.BlockSpec((sb, dk), lambda b, i: (ns - 1 - i, 0)),
                  *_ret_tables_specs()],
        out_specs=[blk(N_IN, 0)], out_shape=[jax.ShapeDtypeStruct((T, N_IN), bf16)],
        scratch_shapes=[pltpu.VMEM((H, dk, dv), f32)],
        args=(qr, kr, proj, proj, o, states, du, dgl, *rope, *decay))


def _att_bwd(proj, dao, probs, dproj, B, S, phase=None):
    T = B * S
    nq = S // QBLK
    dh = ATT_HEAD_DIM

    def body(q_ref, k_ref, v_ref, do_ref, p_ref, _, dp_ref, vec_ref, dbias_ref, dka_ref, dva_ref):
        b, i = pl.program_id(0), pl.program_id(1)

        @pl.when((b == 0) & (i == 0))
        def _():
            dbias_ref[...] = jnp.zeros_like(dbias_ref)

        @pl.when(i == 0)
        def _():
            dka_ref[...] = jnp.zeros_like(dka_ref)
            dva_ref[...] = jnp.zeros_like(dva_ref)

        def step(nk):
            win = pl.ds(pl.multiple_of((i + 1) * QBLK - nk, QBLK), nk)
            kw, vw = k_ref[win, :], v_ref[win, :]
            first = _first_of_pair()
            first_rows = lax.broadcasted_iota(jnp.int32, (2 * dh, 1), 0) < dh
            dqs, dks, dvs = [], [], []
            for p in range(ATT_HEADS // 2):
                ps = slice(2 * p * dh, 2 * (p + 1) * dh)
                q2, k2, v2, do2 = q_ref[:, ps] * ATT_SCALE, kw[:, ps], vw[:, ps], do_ref[:, ps]
                dq2, dk2, dv2 = [], [], []
                for e in range(2):
                    h = 2 * p + e
                    prb = p_ref[0, h, :, KWIN - nk:]
                    pr = prb.astype(f32)
                    dp = _dot_nt(jnp.where(first == (e == 0), do2, jnp.zeros_like(do2)), v2)
                    ds = pr * (dp - jnp.sum(pr * dp, axis=-1, keepdims=True))
                    dbias_ref[h, :, KWIN - nk:] += ds
                    dsb = ds.astype(bf16)
                    dq2.append(_dot(dsb, k2) * ATT_SCALE)
                    dk2.append(_dot_tn(q2, dsb))
                    dv2.append(_dot_tn(do2, prb))
                dqs.append(jnp.where(first, dq2[0], dq2[1]))
                dks.append(jnp.where(first_rows, dk2[0], dk2[1]))
                dvs.append(jnp.where(first_rows, dv2[0], dv2[1]))
            dp_ref[pl.ds(pl.multiple_of(i * QBLK, QBLK), QBLK), :ATT_W] = jnp.concatenate(dqs, axis=1).astype(bf16)
            dka_ref[:, win] += jnp.concatenate(dks, axis=0)
            dva_ref[:, win] += jnp.concatenate(dvs, axis=0)

        _by_window(i, step)

        @pl.when(i == nq - 1)
        def _():
            dp_ref[:, ATT_W:2 * ATT_W] = dka_ref[...].T.astype(bf16)
            dp_ref[:, 2 * ATT_W:] = dva_ref[...].T.astype(bf16)

        @pl.when((b == B - 1) & (i == nq - 1))
        def _():
            rr = lax.broadcasted_iota(jnp.int32, (QBLK, QBLK), 0)
            cc = lax.broadcasted_iota(jnp.int32, (QBLK, QBLK), 1)
            flip = jnp.where(rr + cc == QBLK - 1, 1.0, 0.0).astype(bf16)
            for h in range(ATT_HEADS):
                d = dbias_ref[h]
                hi = d.astype(bf16)
                lo = (d - hi.astype(f32)).astype(bf16)
                rev = _dot(flip, hi) + _dot(flip, lo)
                wide = jnp.concatenate([rev, jnp.zeros((QBLK, TOEP - KWIN), f32)], axis=1)
                rolled = pltpu.roll(wide, 0, 1, stride=1, stride_axis=0)
                vec_ref[h:h + 1, :] = jnp.sum(rolled, axis=0, keepdims=True)

    qspec = lambda c: pl.BlockSpec((QBLK, ATT_W), lambda b, i: (b * nq + i, c))
    kspec = lambda c: pl.BlockSpec((S, ATT_W), lambda b, i: (b, c))
    return _call(
        body, phase, name="att_bwd", grid=(B, nq), aliases={5: 0},
        in_specs=[qspec(C_AQ // ATT_W), kspec(C_AK // ATT_W), kspec(C_AV // ATT_W), qspec(0),
                  pl.BlockSpec((1, ATT_HEADS, QBLK, KWIN), lambda b, i: (b * nq + i, 0, 0, 0)),
                  pl.BlockSpec(memory_space=pl.ANY)],
        out_specs=[pl.BlockSpec((S, 3 * ATT_W), lambda b, i: (b, C_AQ // (3 * ATT_W))),
                   pl.BlockSpec((ATT_HEADS, TOEP), lambda b, i: (0, 0))],
        out_shape=[jax.ShapeDtypeStruct((T, N_IN), bf16), jax.ShapeDtypeStruct((ATT_HEADS, TOEP), f32)],
        scratch_shapes=[pltpu.VMEM((ATT_HEADS, QBLK, KWIN), f32), pltpu.VMEM((ATT_W, S), f32),
                        pltpu.VMEM((ATT_W, S), f32)],
        args=(proj, proj, proj, dao, probs, dproj))


def _in_proj_bwd(dproj, w_in, x2, gamma, dh1, phase=None):
    T, D = x2.shape
    nk, _, tk = w_in.shape
    tm = _tile(T, BIG_ROW_TILE, 8)

    def body(dp_ref, w_ref, x_ref, g_ref, dh1_ref, dx_ref, part_ref, acc_ref):
        j = pl.program_id(1)

        @pl.when(j == 0)
        def _():
            acc_ref[...] = jnp.zeros_like(acc_ref)

        acc_ref[...] += _dot_nt(dp_ref[...], w_ref[0])

        @pl.when(j == nk - 1)
        def _():
            x = x_ref[...]
            r = lax.rsqrt(jnp.mean(x * x, axis=-1, keepdims=True) + EPS)
            n = x * r
            dxn = acc_ref[...]
            dn = dxn * g_ref[...]
            dx_ref[...] = dh1_ref[...] + r * (dn - n * jnp.mean(dn * n, axis=-1, keepdims=True))
            part_ref[...] = jnp.zeros_like(part_ref)
            part_ref[0:1, :] = jnp.sum(dxn * n, axis=0, keepdims=True)

    row = lambda n: pl.BlockSpec((tm, n), lambda i, j: (i, 0))
    return _call(
        body, phase, name="in_proj_bwd", grid=(T // tm, nk),
        in_specs=[pl.BlockSpec((tm, tk), lambda i, j: (i, j)), pl.BlockSpec((1, D, tk), lambda i, j: (j, 0, 0)), row(D),
                  pl.BlockSpec((1, D), lambda i, j: (0, 0)), row(D)],
        out_specs=[row(D), pl.BlockSpec((8, D), lambda i, j: (i, 0))],
        out_shape=[jax.ShapeDtypeStruct((T, D), f32), jax.ShapeDtypeStruct((T // tm * 8, D), f32)],
        scratch_shapes=[pltpu.VMEM((tm, D), f32)],
        args=(dproj, w_in, x2, gamma, dh1))


def _wgrad(a, b, shard_axis, name, phase=None):
    def spec(arr, sharded, tt):
        if arr.ndim == 3:
            return arr.shape[2], pl.BlockSpec((1, tt, arr.shape[2]), lambda s, t: (s, t, 0))
        if sharded:
            w = arr.shape[1] // N_CHIPS
            return w, pl.BlockSpec((tt, w), lambda s, t: (t, s))
        return arr.shape[1], pl.BlockSpec((tt, arr.shape[1]), lambda s, t: (t, 0))

    T = a.shape[-2]
    whole = a.ndim == 2 and b.ndim == 2 and a.shape[1] * b.shape[1] * 4 <= WGRAD_ACC_BYTES
    width = lambda arr, sharded: arr.shape[-1] // (1 if whole or arr.ndim == 3 or not sharded else N_CHIPS)
    wa, wb_ = width(a, shard_axis == 0), width(b, shard_axis == 1)
    fixed = wa * wb_ * (4 + 2 * 2)
    tt = T // 4 if whole else T
    while tt > 256 and 2 * tt * (wa * a.dtype.itemsize + wb_ * b.dtype.itemsize) + fixed > WGRAD_VMEM_BYTES:
        tt //= 2
    nt = T // tt
    if whole:
        K, N = a.shape[1], b.shape[1]
        a_spec, b_spec = pl.BlockSpec((tt, K), lambda s, t: (t, 0)), pl.BlockSpec((tt, N), lambda s, t: (t, 0))
        out_block = (N_CHIPS, K // N_CHIPS, N) if shard_axis == 0 else (N_CHIPS, K, N // N_CHIPS)
        out_spec = pl.BlockSpec(out_block, lambda s, t: (0, 0, 0))
    else:
        K, a_spec = spec(a, shard_axis == 0, tt)
        N, b_spec = spec(b, shard_axis == 1, tt)
        out_block = (N_CHIPS, K, N)
        out_spec = pl.BlockSpec((1, K, N), lambda s, t: (s, 0, 0))

    def body(a_ref, b_ref, o_ref, acc_ref):
        t = pl.program_id(1)

        @pl.when(t == 0)
        def _():
            acc_ref[...] = jnp.zeros_like(acc_ref)

        av = a_ref[0] if a.ndim == 3 else a_ref[...]
        bv = b_ref[0] if b.ndim == 3 else b_ref[...]
        acc_ref[...] += _dot_tn(av.astype(bf16), bv.astype(bf16))

        @pl.when(t == nt - 1)
        def _():
            if not whole:
                o_ref[0] = acc_ref[...].astype(bf16)
            else:
                _, kk, nn = out_block
                for s in range(N_CHIPS):
                    o_ref[s] = (acc_ref[s * kk:(s + 1) * kk, :] if shard_axis == 0
                                else acc_ref[:, s * nn:(s + 1) * nn]).astype(bf16)

    (grad,), carried = _call(
        body, phase, name=name, grid=(1 if whole else N_CHIPS, nt), in_specs=[a_spec, b_spec], out_specs=[out_spec],
        out_shape=[jax.ShapeDtypeStruct(out_block, bf16)], scratch_shapes=[pltpu.VMEM((K, N), f32)], args=(a, b))
    return grad, carried


def _adamw_sum(place, groups, name):
    n = len(groups)
    R, C = groups[0][0].shape
    half = R // 2
    tr = _tile(half, max(16, (1 << 18) // C // 16 * 16), 16)
    nr = half // tr

    def body(p_ref, *refs):
        for a in range(n):
            w_ref, m_ref, v_ref, part_ref, fc_ref, fs_ref = refs[6 * a:6 * a + 6]
            g_ref, d_ref, mo_ref, vo_ref = refs[6 * n + 4 * a:6 * n + 4 * a + 4]
            up = lambda x: x.astype(f32)
            mine = ((up(part_ref[0]) + up(fc_ref[0])) + up(fc_ref[1])) + up(fc_ref[2])
            sibs = ((up(fs_ref[0]) + up(fs_ref[1])) + up(fs_ref[2])) + up(fs_ref[3])
            g_ = jnp.where(pl.program_id(0) == p_ref[0], mine, sibs)
            m_ = ADAM_B1 * m_ref[...] + (1.0 - ADAM_B1) * g_
            v_ = ADAM_B2 * v_ref[...] + (1.0 - ADAM_B2) * (g_ * g_)
            m_hat = m_ / (1.0 - ADAM_B1 ** ADAM_STEP)
            v_hat = v_ / (1.0 - ADAM_B2 ** ADAM_STEP)
            g_ref[...] = g_
            d_ref[...] = -ADAM_LR * (m_hat / (jnp.sqrt(v_hat) + ADAM_EPS) + ADAM_WD * w_ref[...])
            mo_ref[...] = m_
            vo_ref[...] = v_

    spec = pl.BlockSpec((tr, C), lambda h, r, p: (h * nr + r, 0))
    one = [spec, spec, spec, pl.BlockSpec((1, tr, C), lambda h, r, p: (p[1], jnp.where(h == p[0], r, 0), 0)),
           pl.BlockSpec((3, tr, C), lambda h, r, p: (0, jnp.where(h == p[0], r, 0), 0)),
           pl.BlockSpec((4, tr, C), lambda h, r, p: (0, jnp.where(h == p[0], 0, r), 0))]
    res = pl.pallas_call(
        body, name=name,
        grid_spec=pltpu.PrefetchScalarGridSpec(num_scalar_prefetch=1, grid=(2, nr), in_specs=one * n,
                                               out_specs=[spec] * (4 * n)),
        out_shape=[jax.ShapeDtypeStruct((R, C), f32)] * (4 * n),
        compiler_params=_params(("parallel", "parallel")),
    )(place, *[x for g in groups for x in g])
    return [tuple(res[4 * a:4 * a + 4]) for a in range(n)]


def _adamw(w, g, m, v, name):
    R, C = w.shape
    tr = _tile(R, max(8, (1 << 18) // C // 8 * 8), 8)

    def body(w_ref, g_ref, m_ref, v_ref, d_ref, mo_ref, vo_ref):
        g_ = g_ref[...]
        m_ = ADAM_B1 * m_ref[...] + (1.0 - ADAM_B1) * g_
        v_ = ADAM_B2 * v_ref[...] + (1.0 - ADAM_B2) * (g_ * g_)
        m_hat = m_ / (1.0 - ADAM_B1 ** ADAM_STEP)
        v_hat = v_ / (1.0 - ADAM_B2 ** ADAM_STEP)
        d_ref[...] = -ADAM_LR * (m_hat / (jnp.sqrt(v_hat) + ADAM_EPS) + ADAM_WD * w_ref[...])
        mo_ref[...] = m_
        vo_ref[...] = v_

    spec = pl.BlockSpec((tr, C), lambda i: (i, 0))
    return pl.pallas_call(
        body, name=name, grid=(R // tr,), in_specs=[spec] * 4, out_specs=[spec] * 3,
        out_shape=[jax.ShapeDtypeStruct((R, C), f32)] * 3,
        compiler_params=_params(("parallel",)),
    )(w, g, m, v)


def _place():
    return lax.axis_index("x"), lax.axis_index("y"), lax.axis_index("c")


def _other_chips(x, y):
    chips = [(1 - x, y), (x, 1 - y), (1 - x, 1 - y)]
    return chips, [2 * cx + cy for cx, cy in chips]


def _spread_phase(blk):
    def peers():
        x, y, c = _place()
        return [tuple(1 - p if (k >> s) & 1 else p for p, s in ((x, 2), (y, 1), (c, 0))) for k in range(1, N_DEV)]

    def copies(pin, out):
        x, y, c = _place()
        mine = out[0].at[4 * x + 2 * y + c]
        return [(mine, mine, peer) for peer in peers()]

    stack = jnp.broadcast_to(blk, (N_DEV,) + blk.shape)
    return _Phase([stack], [jax.ShapeDtypeStruct(stack.shape, stack.dtype)], {0: 0}, N_DEV - 1, copies,
                  lambda pin, out: [out[0].at[4 * px + 2 * py + pc] for px, py, pc in peers()])


def _sum_slots(stack, name):
    def body(s_ref, o_ref):
        tot = s_ref[0]
        for d in range(1, stack.shape[0]):
            tot = tot + s_ref[d]
        o_ref[...] = tot

    vm = pl.BlockSpec(memory_space=pltpu.VMEM)
    return pl.pallas_call(body, name=name, in_specs=[vm], out_specs=vm,
                          out_shape=jax.ShapeDtypeStruct(stack.shape[1:], stack.dtype))(stack)


def _cast_shards(place, ws, name):
    n = len(ws)
    R, C = ws[0].shape
    tr = _tile(R, max(16, (1 << 19) // C // 16 * 16), 16)

    def body(p_ref, *refs):
        for a in range(n):
            refs[n + a][0] = refs[a][...].astype(bf16)

    return pl.pallas_call(
        body, name=name,
        grid_spec=pltpu.PrefetchScalarGridSpec(
            num_scalar_prefetch=1, grid=(R // tr,),
            in_specs=[pl.BlockSpec((tr, C), lambda r, p: (r, 0))] * n,
            out_specs=[pl.BlockSpec((1, tr, C), lambda r, p: (p[1], r, 0))] * n),
        out_shape=[jax.ShapeDtypeStruct((N_CHIPS, R, C), bf16)] * n,
        compiler_params=_params(("parallel",)),
    )(place, *ws)


class _Phase:
    def __init__(self, arrays, out_shapes, aliases, n_copies, copies, arrivals, own_starts=(), own_waits=()):
        self.arrays, self.out_shapes, self.aliases = list(arrays), list(out_shapes), dict(aliases)
        self.n_copies, self.copies, self.arrivals = n_copies, copies, arrivals
        self.own_starts, self.own_waits = tuple(own_starts), tuple(own_waits)

    def sems(self):
        return [pltpu.SemaphoreType.DMA((self.n_copies,)), pltpu.SemaphoreType.DMA((self.n_copies,))]

    def _descriptors(self, pin, pout, send_sems, recv_sems):
        return [pltpu.make_async_remote_copy(src_ref=s, dst_ref=d, send_sem=send_sems.at[i], recv_sem=recv_sems.at[i],
                                             device_id=to, device_id_type=MESH)
                for i, (s, d, to) in enumerate(self.copies(pin, pout))]

    def _arrival(self, i, pin, pout, send_sems, recv_sems):
        dst = self.arrivals(pin, pout)[i]
        return pltpu.make_async_remote_copy(src_ref=dst, dst_ref=dst, send_sem=send_sems.at[i], recv_sem=recv_sems.at[i],
                                            device_id=_place(), device_id_type=MESH)

    def start(self, pin, pout, send_sems, recv_sems):
        for i, cp in enumerate(self._descriptors(pin, pout, send_sems, recv_sems)):
            if i not in self.own_starts:
                cp.start()

    def begin(self, i, pin, pout, send_sems, recv_sems):
        self._descriptors(pin, pout, send_sems, recv_sems)[i].start()

    def arrived(self, i, pin, pout, send_sems, recv_sems):
        self._arrival(i, pin, pout, send_sems, recv_sems).wait_recv()

    def finish(self, pin, pout, send_sems, recv_sems):
        for i in range(self.n_copies):
            if i not in self.own_waits:
                self._arrival(i, pin, pout, send_sems, recv_sems).wait_recv()
        for cp in self._descriptors(pin, pout, send_sems, recv_sems):
            cp.wait_send()


def _join(phases):
    if len(phases) == 1:
        return phases[0]
    ai = np.cumsum([0] + [len(p.arrays) for p in phases])
    oi = np.cumsum([0] + [len(p.out_shapes) for p in phases])

    def each(fn_name, pin, pout):
        return [item for k, p in enumerate(phases)
                for item in getattr(p, fn_name)(pin[ai[k]:ai[k + 1]], pout[oi[k]:oi[k + 1]])]

    aliases = {int(ai[k]) + i: int(oi[k]) + j for k, p in enumerate(phases) for i, j in p.aliases.items()}
    ci = np.cumsum([0] + [p.n_copies for p in phases])
    shifted = lambda attr: [int(ci[k]) + i for k, p in enumerate(phases) for i in getattr(p, attr)]
    return _Phase([a for p in phases for a in p.arrays], [s for p in phases for s in p.out_shapes], aliases,
                  int(ci[-1]), functools.partial(each, "copies"), functools.partial(each, "arrivals"),
                  shifted("own_starts"), shifted("own_waits"))


def _call(body, phase, *, name, grid, in_specs, out_specs, out_shape, scratch_shapes, args, prefetch=(), expose=False,
          aliases=None):
    seq = _params(("arbitrary",) * len(grid))
    np_ = len(prefetch)
    own = {np_ + i: j for i, j in (aliases or {}).items()}
    if phase is None:
        spec = pltpu.PrefetchScalarGridSpec(num_scalar_prefetch=np_, grid=grid, in_specs=in_specs, out_specs=out_specs,
                                            scratch_shapes=scratch_shapes)
        res = pl.pallas_call(body, name=name, grid_spec=spec, out_shape=out_shape, input_output_aliases=own,
                             compiler_params=seq)(*prefetch, *args)
        return list(res), []
    ni, no, ns = len(in_specs), len(out_specs), len(scratch_shapes)
    pi, po = len(phase.arrays), len(phase.out_shapes)

    def hosted(*refs):
        cut = np.cumsum([np_, ni, pi, no, po, ns])
        pre, ins, pin, outs, pout, scr, sems = (refs[a:b] for a, b in zip([0, *cut], [*cut, len(refs)]))
        ids = [pl.program_id(d) for d in range(len(grid))]
        first = functools.reduce(lambda p, q: p & q, [i == 0 for i in ids])
        last = functools.reduce(lambda p, q: p & q, [i == g - 1 for i, g in zip(ids, grid)])
        pl.when(first)(lambda: phase.start(pin, pout, *sems))
        body(*pre, *ins, *outs, *scr, **({"carried": (pin, pout, sems)} if expose else {}))
        pl.when(last)(lambda: phase.finish(pin, pout, *sems))

    anyspace = pl.BlockSpec(memory_space=pl.ANY)
    spec = pltpu.PrefetchScalarGridSpec(
        num_scalar_prefetch=np_, grid=grid, in_specs=list(in_specs) + [anyspace] * pi,
        out_specs=list(out_specs) + [anyspace] * po, scratch_shapes=list(scratch_shapes) + phase.sems())
    res = pl.pallas_call(
        hosted, name=name, grid_spec=spec, out_shape=list(out_shape) + phase.out_shapes,
        input_output_aliases={**own, **{np_ + ni + i: no + j for i, j in phase.aliases.items()}}, compiler_params=seq,
    )(*prefetch, *args, *phase.arrays)
    return list(res[:no]), list(res[no:])


def _run_phases(name, phases):
    first = phases[0]
    pi, po = len(first.arrays), len(first.out_shapes)

    def body(*refs):
        pin, pout, sems = refs[:pi], refs[pi:pi + po], refs[pi + po:]
        for n, ph in enumerate(phases):
            ph.start(pin, pout, *sems[2 * n:2 * n + 2])
            ph.finish(pin, pout, *sems[2 * n:2 * n + 2])

    anyspace = pl.BlockSpec(memory_space=pl.ANY)
    return list(pl.pallas_call(
        body, name=name, in_specs=[anyspace] * pi, out_specs=[anyspace] * po, out_shape=first.out_shapes,
        input_output_aliases=first.aliases, scratch_shapes=[s for ph in phases for s in ph.sems()],
    )(*first.arrays))


def _half_rows(buf, c):
    half = buf.shape[1] // 2
    return pl.ds(c * half, half), pl.ds((1 - c) * half, half)


def _gather_phase(bufs, over_ici):
    n = len(bufs)
    shapes = [jax.ShapeDtypeStruct(b.shape, b.dtype) for b in bufs]

    def landed(out, which):
        x, y, c = _place()
        _, ks = _other_chips(x, y)
        return [out[a].at[ks[j], _half_rows(bufs[a], c)[which]] for a in range(n) for j in range(3)]

    def ici(pin, out):
        x, y, c = _place()
        chips, _ = _other_chips(x, y)
        mine = [out[a].at[2 * x + y, _half_rows(bufs[a], c)[0]] for a in range(n)]
        return [(mine[a], mine[a], (*chips[j], c)) for a in range(n) for j in range(3)]

    def d2d(pin, out):
        x, y, c = _place()
        return [(dst, dst, (x, y, 1 - c)) for dst in landed(out, 0)]

    if over_ici:
        return _Phase(bufs, shapes, {a: a for a in range(n)}, 3 * n, ici, lambda pin, out: landed(out, 0))
    return _Phase(bufs, shapes, {a: a for a in range(n)}, 3 * n, d2d, lambda pin, out: landed(out, 1))


def _feed_phase(buf):
    def chips():
        x, y, _ = _place()
        return [(x if f < 2 else 1 - x, y if f % 2 == 0 else 1 - y) for f in (1, 2, 3)]

    def copies(pin, out):
        x, y, c = _place()
        mine = _half_rows(buf, c)[0]
        own = out[0].at[2 * x + y, mine]
        sent = [(own, own, (cx, cy, c)) for cx, cy in chips()]
        return sent + [(out[0].at[2 * cx + cy, mine], out[0].at[2 * cx + cy, mine], (x, y, 1 - c)) for cx, cy in chips()]

    def arrivals(pin, out):
        mine, theirs = _half_rows(buf, _place()[2])
        return [out[0].at[2 * cx + cy, rows] for rows in (mine, theirs) for cx, cy in chips()]

    return _Phase([buf], [jax.ShapeDtypeStruct(buf.shape, buf.dtype)], {0: 0}, 6, copies, arrivals,
                  own_starts=(3, 4, 5), own_waits=range(6))


def _rs_swap_phase(grads):
    n = len(grads)

    def copies(g, out):
        x, y, c = _place()
        return [(g[a].at[:, _half_rows(grads[a], c)[1]], out[a], (x, y, 1 - c)) for a in range(n)]

    shapes = [jax.ShapeDtypeStruct((N_CHIPS, g.shape[1] // 2, g.shape[2]), g.dtype) for g in grads]
    return _Phase(grads, shapes, {}, n, copies, lambda g, out: list(out))


def _rs_add_sibling(place, grads, gots, name):
    n = len(grads)
    _, R, C = grads[0].shape
    half = R // 2
    tr = _tile(half, max(16, (1 << 19) // C // 16 * 16), 16)
    nr = half // tr

    def body(p_ref, *refs):
        for a in range(n):
            refs[2 * n + a][...] = (refs[2 * a][...].astype(f32) + refs[2 * a + 1][...].astype(f32)).astype(bf16)

    res = pl.pallas_call(
        body, name=name,
        grid_spec=pltpu.PrefetchScalarGridSpec(
            num_scalar_prefetch=1, grid=(N_CHIPS, nr),
            in_specs=[pl.BlockSpec((1, tr, C), lambda k, r, p: (k, p[0] * nr + r, 0)),
                      pl.BlockSpec((1, tr, C), lambda k, r, p: (k, r, 0))] * n,
            out_specs=[pl.BlockSpec((1, tr, C), lambda k, r, p: (k, r, 0))] * n),
        out_shape=[jax.ShapeDtypeStruct((N_CHIPS, half, C), bf16)] * n,
        compiler_params=_params(("parallel", "parallel")),
    )(place, *[x for pair in zip(grads, gots) for x in pair])
    return list(res)


def _rs_chips_phase(parts):
    n = len(parts)

    def copies(p, fc):
        x, y, c = _place()
        chips, ks = _other_chips(x, y)
        return [(p[a].at[ks[j]], fc[a].at[j], (*chips[j], c)) for a in range(n) for j in range(3)]

    shapes = [jax.ShapeDtypeStruct((3,) + q.shape[1:], q.dtype) for q in parts]
    return _Phase(parts, shapes, {}, 3 * n, copies, lambda p, fc: [fc[a].at[j] for a in range(n) for j in range(3)])


def _rs_hand_phase(parts, from_chips):
    n = len(parts)

    def copies(pin, fs):
        x, y, c = _place()
        sib = (x, y, 1 - c)
        own = [(pin[a].at[2 * x + y], fs[a].at[0], sib) for a in range(n)]
        return own + [(pin[n + a].at[j], fs[a].at[1 + j], sib) for a in range(n) for j in range(3)]

    def arrivals(pin, fs):
        return [fs[a].at[0] for a in range(n)] + [fs[a].at[1 + j] for a in range(n) for j in range(3)]

    shapes = [jax.ShapeDtypeStruct((4,) + q.shape[1:], q.dtype) for q in parts]
    return _Phase(list(parts) + list(from_chips), shapes, {}, 4 * n, copies, arrivals)


class _Exchange:
    def __init__(self, place):
        self.place = place

    def feed(self, buf):
        return _feed_phase(buf)

    def gather(self, bufs, over_ici):
        return _gather_phase(bufs, over_ici)

    def swap(self, grads):
        return _rs_swap_phase(grads)

    def pair_sums(self, names, grads):
        return self.add(names, grads, _run_phases("rs_sibling_" + names[0], [_rs_swap_phase(grads)]))

    def add(self, names, grads, got):
        parts = {}
        for group in _same_shape(grads):
            res = _rs_add_sibling(self.place, [grads[i] for i in group], [got[i] for i in group],
                                  "rs_add_" + names[group[0]])
            parts.update(zip(group, res))
        return [parts[i] for i in range(len(names))]

    def to_chips(self, parts):
        return _rs_chips_phase(parts)

    def to_sibling(self, parts, from_chips):
        return _rs_hand_phase(parts, from_chips)

    def spread(self, blk):
        return _spread_phase(blk)

    def hand_over(self, name, parts, from_chips, blk):
        got = _run_phases(name, [_join([_rs_hand_phase(parts, from_chips), _spread_phase(blk)])])
        return got[:-1], got[-1]


def _local_step(place, x, target, norm_mix, b_gate, rb_chip, norm_ffn, norm_final, w_in, rest, exch):
    B, S, D = x.shape
    T = B * S
    x2 = x.reshape(T, D)
    tg2 = target.reshape(T, D)
    rope, decay = _rope_tables(S), _decay_tables()
    g_fin = norm_final.reshape(1, D)
    nrel = rb_chip.shape[-1]

    mrg, ffn = ["w_ret_out", "w_att_out", "w_out"], ["w_ffn_gate", "w_ffn_up", "w_ffn_down"]
    (xn, proj), got = _in_proj(place, x2, norm_mix, _join([exch.feed(w_in), exch.gather([rest[n] for n in mrg], True),
                                                           exch.spread(jnp.pad(rb_chip, ((0, 0), (0, 128 - nrel))))]))
    w_in, wb, rb_all = got[0], {}, got.pop()
    trows = _bias_rows(jnp.concatenate([rb_all[2 * k, :, :nrel] for k in range(N_CHIPS)], axis=1))
    (qr, kr, o, u, states), got = _ret_fwd(proj, B, S, rope, decay, _join([exch.gather([rest["w_ffn_gate"]], True),
                                                                         exch.gather(got[1:], False)]))
    wb.update(zip(mrg, got[1:]))
    (ao, probs), got = _att_fwd(proj, trows, B, S, _join([exch.gather([rest["w_ffn_up"], rest["w_ffn_down"]], True),
                                                    exch.gather(got[:1], False)]))
    wb["w_ffn_gate"] = got[2]
    w_ro, w_out = wb["w_ret_out"].reshape(-1, D), wb["w_out"].reshape(-1, D)
    (h1, yr, ya), got = _mix_fwd(x2, proj, u, ao, b_gate, w_ro, wb["w_att_out"], w_out, exch.gather(got[:2], False))
    wb.update(zip(ffn[1:], got))
    hn, a, b, f, dh2, dh2b, part_fin = _ffn_fwd(h1, norm_ffn, wb["w_ffn_gate"], wb["w_ffn_up"], wb["w_ffn_down"], g_fin, tg2)

    da, db, dh1, dh1b, part_ffn = _ffn_bwd(dh2, h1, norm_ffn, a, b, wb["w_ffn_gate"], wb["w_ffn_up"], wb["w_ffn_down"])
    ffn = ["w_ffn_down", "w_ffn_gate", "w_ffn_up"]
    g_ffn = [_wgrad(f, dh2b, 0, "wgrad_ffn_down")[0], _wgrad(da, hn, 0, "wgrad_ffn_gate")[0],
             _wgrad(db, hn, 0, "wgrad_ffn_up")[0]]
    (du, dao, dgl, mix, dyr, dya, part_bg), x_ffn = _mix_bwd(dh1b, proj, yr, ya, b_gate, w_ro, wb["w_att_out"], w_out,
                                                             exch.swap(g_ffn))
    p_ffn = exch.add(ffn, g_ffn, x_ffn)
    mrg = ["w_out", "w_ret_out", "w_att_out"]
    g_mrg = [_wgrad(mix, dh1b, 0, "wgrad_out")[0], _wgrad(u, dyr, 0, "wgrad_ret_out")[0],
             _wgrad(ao, dya, 1, "wgrad_att_out")[0]]
    (dproj,), got = _ret_bwd(proj, qr, kr, o, states, du, dgl, B, S, rope, decay, _join([exch.to_chips(p_ffn[:2]),
                                                                                          exch.swap(g_mrg)]))
    c_two, p_mrg = got[:2], exch.add(mrg, g_mrg, got[2:])
    (dproj, dvec), got = _att_bwd(proj, dao, probs, dproj, B, S, exch.to_chips(p_ffn[2:] + p_mrg))
    c_ffn, c_mrg = c_two + got[:1], got[1:]
    g_in, got = _wgrad(xn, dproj, 1, "wgrad_in", exch.to_sibling(p_ffn + p_mrg, c_ffn + c_mrg))
    s_ffn, s_mrg = got[:len(ffn)], got[len(ffn):]
    p_in = exch.pair_sums(["w_in"], [g_in])
    (gx, part_mix), c_in = _in_proj_bwd(dproj, w_in, x2, norm_mix, dh1, exch.to_chips(p_in))
    rows = lambda p, r: p.reshape(-1, 8, p.shape[-1])[:, r, :].sum(axis=0)
    lo = KWIN - 1 - (MAX_REL - 1)
    drb = jnp.concatenate([jnp.flip(dvec[:, lo:lo + N_REL - 1], axis=1), dvec[:, :lo].sum(axis=1, keepdims=True)], axis=1)
    gsmall = {
        "norm_mix": rows(part_mix, 0), "b_gate": rows(part_bg, 0), "rel_bias": drb, "norm_ffn": rows(part_ffn, 0),
        "norm_final": rows(part_fin, 0),
    }
    s_in, small_all = exch.hand_over("rs_hand_w_in", p_in, c_in, _pack_small(gsmall, rows(part_fin, 1)))
    gbig = dict(zip(ffn + mrg + ["w_in"], zip(p_ffn + p_mrg + p_in, c_ffn + c_mrg + c_in, s_ffn + s_mrg + s_in)))
    return gx.reshape(B, S, D), gbig, small_all


SMALL_ROWS = 16


def _pack_small(gs, loss_lanes):
    D = D_MODEL
    rb = jnp.pad(gs["rel_bias"].reshape(-1), (0, 3 * D - ATT_HEADS * N_REL)).reshape(3, D)
    rows = [gs["norm_mix"].reshape(1, D), gs["b_gate"].reshape(2, D), gs["norm_ffn"].reshape(1, D),
            gs["norm_final"].reshape(1, D), rb, loss_lanes.reshape(1, D)]
    used = sum(r.shape[0] for r in rows)
    return jnp.concatenate(rows + [jnp.zeros((SMALL_ROWS - used, D), f32)], axis=0)


def kernel(x, norm_mix, w_in, b_gate, rel_bias, w_ret_out, w_att_out, w_out, norm_ffn, w_ffn_gate, w_ffn_up, w_ffn_down, norm_final, loss_target, m_norm_mix, m_w_in, m_b_gate, m_rel_bias, m_w_ret_out, m_w_att_out, m_w_out, m_norm_ffn, m_w_ffn_gate, m_w_ffn_up, m_w_ffn_down, m_norm_final, v_norm_mix, v_w_in, v_b_gate, v_rel_bias, v_w_ret_out, v_w_att_out, v_w_out, v_norm_ffn, v_w_ffn_gate, v_w_ffn_up, v_w_ffn_down, v_norm_final):
    w = dict(norm_mix=norm_mix, w_in=w_in, b_gate=b_gate, rel_bias=rel_bias, w_ret_out=w_ret_out, w_att_out=w_att_out,
             w_out=w_out, norm_ffn=norm_ffn, w_ffn_gate=w_ffn_gate, w_ffn_up=w_ffn_up, w_ffn_down=w_ffn_down,
             norm_final=norm_final)
    m = dict(norm_mix=m_norm_mix, w_in=m_w_in, b_gate=m_b_gate, rel_bias=m_rel_bias, w_ret_out=m_w_ret_out,
             w_att_out=m_w_att_out, w_out=m_w_out, norm_ffn=m_norm_ffn, w_ffn_gate=m_w_ffn_gate, w_ffn_up=m_w_ffn_up,
             w_ffn_down=m_w_ffn_down, norm_final=m_norm_final)
    v = dict(norm_mix=v_norm_mix, w_in=v_w_in, b_gate=v_b_gate, rel_bias=v_rel_bias, w_ret_out=v_w_ret_out,
             w_att_out=v_w_att_out, w_out=v_w_out, norm_ffn=v_norm_ffn, w_ffn_gate=v_w_ffn_gate, w_ffn_up=v_w_ffn_up,
             w_ffn_down=v_w_ffn_down, norm_final=v_norm_final)
    xi, yi, ci = _place()
    k_me = 2 * xi + yi

    place = jnp.stack([ci, k_me]).astype(jnp.int32)
    big = [n for n, _ in BIG]

    turned = ("w_ffn_gate", "w_ffn_up")
    shard = lambda d, n: jnp.swapaxes(d[n][0], 0, 1) if n in turned else d[n][0]
    whole = lambda a, n: (jnp.swapaxes(a, 0, 1) if n in turned else a)[None]

    by_shape = [[big[i] for i in group] for group in _same_shape([shard(w, n) for n in big])]
    bufs = {}
    for names in by_shape:
        bufs.update(zip(names, _cast_shards(place, [shard(w, n) for n in names], "cast_" + names[0])))
    rest = {n: bufs[n] for n in big if n != "w_in"}
    nrel_loc = rel_bias.shape[-1]
    grad_x, gbig, small_all = _local_step(place, x, loss_target, norm_mix, b_gate, rel_bias[0], norm_ffn, norm_final,
                                          bufs["w_in"], rest, _Exchange(place))

    small = _sum_slots(small_all, "reduce_small")
    D = D_MODEL
    loss = jnp.sum(small[8])
    drb_full = small[5:8].reshape(-1)[:ATT_HEADS * N_REL].reshape(ATT_HEADS, N_REL)
    g = {
        "norm_mix": small[0:1], "b_gate": small[1:3].reshape(1, 2 * D), "norm_ffn": small[3:4], "norm_final": small[4],
        "rel_bias": lax.dynamic_slice_in_dim(drb_full, k_me * nrel_loc, nrel_loc, axis=1)[None],
    }

    delta, new_m, new_v = {}, {}, {}
    for names in by_shape:
        res = _adamw_sum(place, [(shard(w, n), shard(m, n), shard(v, n), *gbig[n]) for n in names], "adamw_" + names[0])
        for n, (g_, d_, m_, v_) in zip(names, res):
            g[n], delta[n], new_m[n], new_v[n] = whole(g_, n), whole(d_, n), whole(m_, n), whole(v_, n)
    flat = lambda d: jnp.concatenate([d[n].reshape(-1) for n in SMALL])
    n_small = sum(int(np.prod(w[n].shape)) for n in SMALL)
    n_pad = -n_small % 1024
    packs = [jnp.pad(flat(d), (0, n_pad)).reshape(-1, 128) for d in (w, g, m, v)]
    outs = _adamw(*packs, "adamw_small")
    for res, dst in zip(outs, (delta, new_m, new_v)):
        off = 0
        fl = res.reshape(-1)
        for n in SMALL:
            sz = int(np.prod(w[n].shape))
            dst[n] = fl[off:off + sz].reshape(w[n].shape)
            off += sz

    return (loss, grad_x, *[g[n] for n in WEIGHTS], *[delta[n] for n in WEIGHTS], *[new_m[n] for n in WEIGHTS],
            *[new_v[n] for n in WEIGHTS])
```

```python
import functools

import numpy as np
import jax
import jax.numpy as jnp
from jax import lax
from jax.experimental import pallas as pl
from jax.experimental.pallas import tpu as pltpu

f32 = jnp.float32
bf16 = jnp.bfloat16

D_MODEL = 1024
CHUNK = 64
RET_HEADS = 4
RET_KEY_DIM = 128
RET_VAL_DIM = 256
ATT_HEADS = 8
ATT_HEAD_DIM = 64
ATT_W = ATT_HEADS * ATT_HEAD_DIM
BAND_CHUNKS = 8
PAD = BAND_CHUNKS * CHUNK
MAX_REL = 256
N_REL = CHUNK + MAX_REL
D_FF = 2816
N_IN = 6656
ROPE_BASE = 10000.0
EPS = 1e-6
NEG_INF = -1e30
C_RQ, C_RK, C_RV, C_RG, C_AQ, C_AK, C_AV, C_GL = 0, 512, 1024, 2048, 3072, 3584, 4096, 4608

ADAM_LR, ADAM_B1, ADAM_B2, ADAM_EPS, ADAM_WD, ADAM_STEP = 0.001, 0.9, 0.999, 1e-08, 0.01, 10

N_CHIPS = 4
N_DEV = 8
WGRAD_ACC_BYTES = 8 * 1024 * 1024
WGRAD_VMEM_BYTES = 40 * 1024 * 1024
ROW_TILE = 512
BIG_ROW_TILE = 1024
IN_ORDER = (0, 2, 3, 1)
QBLK = 256
KWIN = PAD + QBLK
TOEP = 1024
VMEM_LIMIT = 56 * 1024 * 1024
MESH = pl.DeviceIdType.MESH

BIG = (
    ("w_in", 1), ("w_ret_out", 0), ("w_att_out", 1), ("w_out", 0), ("w_ffn_gate", 1), ("w_ffn_up", 1), ("w_ffn_down", 0))
WEIGHTS = ("norm_mix", "w_in", "b_gate", "rel_bias", "w_ret_out", "w_att_out", "w_out", "norm_ffn", "w_ffn_gate",
           "w_ffn_up", "w_ffn_down", "norm_final")
SMALL = ("norm_mix", "b_gate", "rel_bias", "norm_ffn", "norm_final")


def _dot(a, b):
    return lax.dot_general(a, b, (((1,), (0,)), ((), ())), preferred_element_type=f32)


def _dot_nt(a, b):
    return lax.dot_general(a, b, (((1,), (1,)), ((), ())), preferred_element_type=f32)


def _dot_tn(a, b):
    return lax.dot_general(a, b, (((0,), (0,)), ((), ())), preferred_element_type=f32)


def _sig(x):
    return 1.0 / (1.0 + jnp.exp(-x))


def _tile(n, pref, mult):
    best = None
    for t in range(mult, min(n, pref) + 1, mult):
        if n % t == 0:
            best = t
    return best if best is not None else n


def _same_shape(arrays):
    groups = {}
    for i, a in enumerate(arrays):
        groups.setdefault(a.shape, []).append(i)
    return list(groups.values())


def _params(sem, vmem=VMEM_LIMIT):
    return pltpu.CompilerParams(dimension_semantics=sem, vmem_limit_bytes=vmem)


def _in_proj(place, x2, gamma, phase):
    T, D = x2.shape
    _, _, ns = phase.arrays[0].shape
    tm = _tile(T, BIG_ROW_TILE, 8)
    ni = T // tm
    pass_chip = lambda j: sum(jnp.where(j == n, f, 0) for n, f in enumerate(IN_ORDER))

    def body(p_ref, x_ref, g_ref, xn_ref, pr_ref, xs_ref, w_ref, w_sem, carried):
        j, i = pl.program_id(0), pl.program_id(1)
        pin, pout, sems = carried
        rows = pl.ds(pl.multiple_of(i * tm, tm), tm)

        @pl.when(i == 0)
        def _():
            for n, f in enumerate(IN_ORDER):
                if f:
                    @pl.when(j == n)
                    def _():
                        phase.arrived(f - 1, pin, pout, *sems)
                        phase.begin(2 + f, pin, pout, *sems)
                        phase.arrived(2 + f, pin, pout, *sems)
            shard = pltpu.make_async_copy(pout[0].at[jnp.bitwise_xor(p_ref[1], pass_chip(j))], w_ref, w_sem)
            shard.start()
            shard.wait()

        @pl.when(j == 0)
        def _():
            x = x_ref[...]
            r = lax.rsqrt(jnp.mean(x * x, axis=-1, keepdims=True) + EPS)
            xn = (x * r * g_ref[...]).astype(bf16)
            xs_ref[rows, :] = xn
            xn_ref[...] = xn

        pr_ref[...] = _dot(xs_ref[rows, :], w_ref[...]).astype(bf16)

    first_pass = lambda j, i, p: (jnp.where(j == 0, i, ni - 1), 0)
    return _call(
        body, phase, name="in_proj", grid=(N_CHIPS, ni), prefetch=(place,), expose=True,
        in_specs=[pl.BlockSpec((tm, D), first_pass), pl.BlockSpec((1, D), lambda j, i, p: (0, 0))],
        out_specs=[pl.BlockSpec((tm, D), first_pass),
                   pl.BlockSpec((tm, ns), lambda j, i, p: (i, jnp.bitwise_xor(p[1], pass_chip(j))))],
        out_shape=[jax.ShapeDtypeStruct((T, D), bf16), jax.ShapeDtypeStruct((T, N_CHIPS * ns), bf16)],
        scratch_shapes=[pltpu.VMEM((T, D), bf16), pltpu.VMEM((D, ns), bf16), pltpu.SemaphoreType.DMA],
        args=(x2, gamma))


def _rope_tables(S):
    d = RET_KEY_DIM
    freqs = (np.float32(ROPE_BASE) ** (-np.arange(0, d, 2, dtype=np.float32) / np.float32(d))).astype(np.float32)
    ang = np.arange(S, dtype=np.float32)[:, None] * freqs[None, :]
    cos, sin = np.cos(ang).astype(np.float32), np.sin(ang).astype(np.float32)
    return jnp.asarray(np.concatenate([cos, cos], axis=1)), jnp.asarray(np.concatenate([-sin, sin], axis=1))


def _decay_tables():
    H = RET_HEADS
    log_g = jnp.log(1.0 - 2.0 ** (-5.0 - jnp.arange(H, dtype=f32)))
    p = jnp.arange(CHUNK, dtype=f32)
    intra = jnp.exp(log_g[:, None, None] * jnp.abs(p[:, None] - p[None, :]))
    q_dec = jnp.exp(log_g[:, None] * (p[None, :] + 1.0))
    k_dec = jnp.exp(log_g[:, None] * (CHUNK - 1.0 - p[None, :]))
    c_dec = jnp.exp(log_g * CHUNK)
    q_dec = jnp.broadcast_to(q_dec[:, :, None], (H, CHUNK, RET_KEY_DIM))
    k_dec = jnp.broadcast_to(k_dec[:, :, None], (H, CHUNK, RET_KEY_DIM))
    c_dec = jnp.broadcast_to(c_dec[:, None, None], (H, 1, RET_VAL_DIM))
    return intra, q_dec, k_dec, c_dec


K_SCALE = RET_KEY_DIM ** -0.5


RET_CHUNKS = 4


def _ret_tables_specs():
    whole = lambda *shape: pl.BlockSpec(shape, lambda b, i: (0,) * len(shape))
    return [whole(RET_HEADS, CHUNK, CHUNK), whole(RET_HEADS, CHUNK, RET_KEY_DIM), whole(RET_HEADS, CHUNK, RET_KEY_DIM),
            whole(RET_HEADS, 1, RET_VAL_DIM)]


def _rotate(x, cos, sn):
    return x * cos + pltpu.roll(x, RET_KEY_DIM // 2, 1) * sn


def _ret_fwd(proj, B, S, rope, decay, phase=None):
    T = B * S
    nc = S // CHUNK
    H, dk, dv = RET_HEADS, RET_KEY_DIM, RET_VAL_DIM
    sb = RET_CHUNKS * CHUNK
    ns = S // sb

    def body(q_ref, k_ref, v_ref, g_ref, cos_ref, sin_ref, intra_ref, qd_ref, kd_ref, cd_ref,
             qr_ref, kr_ref, o_ref, u_ref, st_ref, state_ref):
        @pl.when(pl.program_id(1) == 0)
        def _():
            state_ref[...] = jnp.zeros_like(state_ref)

        cos, sn = cos_ref[...], sin_ref[...]
        for h in range(H):
            hs = slice(h * dk, (h + 1) * dk)
            qr_ref[:, hs] = _rotate(q_ref[:, hs].astype(f32), cos, sn).astype(bf16)
            kr_ref[:, hs] = (_rotate(k_ref[:, hs].astype(f32), cos, sn) * K_SCALE).astype(bf16)
        states = [state_ref[h] for h in range(H)]
        for ci in range(RET_CHUNKS):
            r = slice(ci * CHUNK, (ci + 1) * CHUNK)
            for h in range(H):
                hk, hv = slice(h * dk, (h + 1) * dk), slice(h * dv, (h + 1) * dv)
                qi, ki, vi = qr_ref[r, hk], kr_ref[r, hk], v_ref[r, hv]
                stb = states[h].astype(bf16)
                st_ref[0, h, ci] = stb
                s = (_dot_nt(qi, ki) * intra_ref[h]).astype(bf16)
                o = _dot(s, vi) + _dot((qi.astype(f32) * qd_ref[h]).astype(bf16), stb)
                states[h] = states[h] * cd_ref[h] + _dot_tn((ki.astype(f32) * kd_ref[h]).astype(bf16), vi)
                mu = jnp.mean(o, axis=-1, keepdims=True)
                xc = o - mu
                var = jnp.mean(xc * xc, axis=-1, keepdims=True)
                oh = xc * lax.rsqrt(var + EPS)
                g = g_ref[r, hv].astype(f32)
                o_ref[r, hv] = o.astype(bf16)
                u_ref[r, hv] = (g * _sig(g) * oh).astype(bf16)
        for h in range(H):
            state_ref[h] = states[h]

    blk = lambda w, c: pl.BlockSpec((sb, w), lambda b, i: (b * ns + i, c))
    return _call(
        body, phase, name="ret_fwd", grid=(B, ns), scratch_shapes=[pltpu.VMEM((H, dk, dv), f32)],
        in_specs=[blk(H * dk, C_RQ // (H * dk)), blk(H * dk, C_RK // (H * dk)), blk(H * dv, C_RV // (H * dv)),
                  blk(H * dv, C_RG // (H * dv)),
                  pl.BlockSpec((sb, dk), lambda b, i: (i, 0)), pl.BlockSpec((sb, dk), lambda b, i: (i, 0)),
                  *_ret_tables_specs()],
        out_specs=[blk(H * dk, 0), blk(H * dk, 0), blk(H * dv, 0), blk(H * dv, 0),
                   pl.BlockSpec((1, H, RET_CHUNKS, dk, dv), lambda b, i: (b, 0, i, 0, 0))],
        out_shape=[jax.ShapeDtypeStruct((T, H * dk), bf16), jax.ShapeDtypeStruct((T, H * dk), bf16),
                   jax.ShapeDtypeStruct((T, H * dv), bf16), jax.ShapeDtypeStruct((T, H * dv), bf16),
                   jax.ShapeDtypeStruct((B, H, nc, dk, dv), bf16)],
        args=(proj, proj, proj, proj, *rope, *decay))


def _bias_rows(rb):
    last = rb[:, N_REL - 1:]
    return jnp.concatenate([
        jnp.broadcast_to(last, (ATT_HEADS, PAD - MAX_REL + 1)),
        jnp.flip(rb[:, :N_REL - 1], axis=1),
        jnp.broadcast_to(rb[:, :1], (ATT_HEADS, KWIN - PAD - CHUNK)),
        jnp.broadcast_to(last, (ATT_HEADS, TOEP - KWIN)),
    ], axis=1)


def _build_bias(t_ref, bias_ref):
    row = lax.broadcasted_iota(jnp.int32, (QBLK, KWIN), 0) // CHUNK
    col = lax.broadcasted_iota(jnp.int32, (QBLK, KWIN), 1) // CHUNK
    delta = BAND_CHUNKS + row - col
    vis = (delta >= 0) & (delta <= BAND_CHUNKS)
    for h in range(ATT_HEADS):
        t = jnp.broadcast_to(t_ref[h:h + 1, :], (QBLK, TOEP))
        rolled = pltpu.roll(t, 0, 1, stride=1, stride_axis=0)
        bias_ref[h] = jnp.where(vis, rolled[:, :KWIN], NEG_INF)


ATT_SCALE = ATT_HEAD_DIM ** -0.5


def _att_probs(qh, kh, bias):
    s = _dot_nt(qh, kh) + bias
    m = jnp.max(s, axis=-1, keepdims=True)
    p = jnp.exp(s - m)
    return p * (1.0 / jnp.sum(p, axis=-1, keepdims=True))


def _first_of_pair():
    return lax.broadcasted_iota(jnp.int32, (1, 2 * ATT_HEAD_DIM), 1) < ATT_HEAD_DIM


def _by_window(i, step):
    sizes = list(range(QBLK, KWIN, QBLK))
    for n, nk in enumerate(sizes):
        pl.when(i == n)(functools.partial(step, nk))
    pl.when(i >= len(sizes))(functools.partial(step, KWIN))


def _att_fwd(proj, trows, B, S, phase=None):
    T = B * S
    nq = S // QBLK
    dh = ATT_HEAD_DIM

    def body(q_ref, k_ref, v_ref, t_ref, o_ref, p_ref, bias_ref):
        i = pl.program_id(1)

        @pl.when((pl.program_id(0) == 0) & (i == 0))
        def _():
            _build_bias(t_ref, bias_ref)

        def step(nk):
            win = pl.ds(pl.multiple_of((i + 1) * QBLK - nk, QBLK), nk)
            kw, vw = k_ref[win, :], v_ref[win, :]
            first = _first_of_pair()
            outs = []
            for p in range(ATT_HEADS // 2):
                ps = slice(2 * p * dh, 2 * (p + 1) * dh)
                q2, k2, v2 = q_ref[:, ps] * ATT_SCALE, kw[:, ps], vw[:, ps]
                both = []
                for e in range(2):
                    qm = jnp.where(first == (e == 0), q2, jnp.zeros_like(q2))
                    pr = _att_probs(qm, k2, bias_ref[2 * p + e, :, KWIN - nk:]).astype(bf16)
                    p_ref[0, 2 * p + e, :, KWIN - nk:] = pr
                    both.append(_dot(pr, v2))
                outs.append(jnp.where(first, both[0], both[1]))
            o_ref[...] = jnp.concatenate(outs, axis=1).astype(bf16)

        _by_window(i, step)

    return _call(
        body, phase, name="att_fwd", grid=(B, nq),
        in_specs=[pl.BlockSpec((QBLK, ATT_W), lambda b, i: (b * nq + i, C_AQ // ATT_W)),
                  pl.BlockSpec((S, ATT_W), lambda b, i: (b, C_AK // ATT_W)),
                  pl.BlockSpec((S, ATT_W), lambda b, i: (b, C_AV // ATT_W)),
                  pl.BlockSpec((ATT_HEADS, TOEP), lambda b, i: (0, 0))],
        out_specs=[pl.BlockSpec((QBLK, ATT_W), lambda b, i: (b * nq + i, 0)),
                   pl.BlockSpec((1, ATT_HEADS, QBLK, KWIN), lambda b, i: (b * nq + i, 0, 0, 0))],
        out_shape=[jax.ShapeDtypeStruct((T, ATT_W), bf16), jax.ShapeDtypeStruct((B * nq, ATT_HEADS, QBLK, KWIN), bf16)],
        scratch_shapes=[pltpu.VMEM((ATT_HEADS, QBLK, KWIN), f32)],
        args=(proj, proj, proj, trows))


def _gl_specs(tm):
    w = 512
    return [pl.BlockSpec((tm, w), functools.partial(lambda i, j: (i, C_GL // 512 + j), j=j)) for j in range(4)]


def _gates(gl_refs, bg_ref):
    gl = jnp.concatenate([r[...] for r in gl_refs], axis=1).astype(f32) + bg_ref[...]
    g = _sig(gl)
    return g[:, :D_MODEL], g[:, D_MODEL:]


def _mix_fwd(x2, proj, u, ao, b_gate, w_ro, w_ao, w_out, phase=None):
    T, D = x2.shape
    tm = _tile(T, ROW_TILE, 8)

    def body(x_ref, u_ref, ao_ref, g0, g1, g2, g3, bg_ref, wro_ref, wao_ref, wo_ref, h1_ref, yr_ref, ya_ref):
        yr = _dot(u_ref[...], wro_ref[...])
        ao = ao_ref[...]
        ya = jnp.concatenate([_dot(ao, wao_ref[k]) for k in range(N_CHIPS)], axis=1)
        gr, ga = _gates((g0, g1, g2, g3), bg_ref)
        mix = gr * yr + ga * ya
        h1_ref[...] = x_ref[...] + _dot(mix.astype(bf16), wo_ref[...])
        yr_ref[...] = yr.astype(bf16)
        ya_ref[...] = ya.astype(bf16)

    full = lambda a: pl.BlockSpec(a.shape, lambda i: (0,) * a.ndim)
    row = lambda n: pl.BlockSpec((tm, n), lambda i: (i, 0))
    return _call(
        body, phase, name="mix_fwd", grid=(T // tm,), scratch_shapes=[],
        in_specs=[row(D), row(D), row(ATT_W), *_gl_specs(tm), full(b_gate), full(w_ro), full(w_ao), full(w_out)],
        out_specs=[row(D), row(D), row(D)],
        out_shape=[jax.ShapeDtypeStruct((T, D), f32), jax.ShapeDtypeStruct((T, D), bf16),
                   jax.ShapeDtypeStruct((T, D), bf16)],
        args=(x2, u, ao, proj, proj, proj, proj, b_gate, w_ro, w_ao, w_out))


def _ffn_fwd(h1, g_ffn, wg, wu, wd, g_fin, target):
    T, D = h1.shape
    nf, tf, _ = wg.shape
    tm = _tile(T, ROW_TILE, 8)

    def body(h1_ref, g_ref, wg_ref, wu_ref, wd_ref, gf_ref, tg_ref, hn_ref, a_ref, b_ref, f_ref, dh2_ref, dh2b_ref,
             part_ref):
        h1v = h1_ref[...]
        r = lax.rsqrt(jnp.mean(h1v * h1v, axis=-1, keepdims=True) + EPS)
        hn = (h1v * r * g_ref[...]).astype(bf16)
        hn_ref[...] = hn
        h2 = h1v
        for k in range(nf):
            a = _dot_nt(hn, wg_ref[k])
            b = _dot_nt(hn, wu_ref[k])
            f = ((a * _sig(a)) * b).astype(bf16)
            a_ref[k] = a.astype(bf16)
            b_ref[k] = b.astype(bf16)
            f_ref[k] = f
            h2 = h2 + _dot(f, wd_ref[k])
        r = lax.rsqrt(jnp.mean(h2 * h2, axis=-1, keepdims=True) + EPS)
        n = h2 * r
        gf = gf_ref[...]
        e = n * gf - tg_ref[...]
        dy = e * (1.0 / D)
        dn = dy * gf
        dh2 = r * (dn - n * jnp.mean(dn * n, axis=-1, keepdims=True))
        dh2_ref[...] = dh2
        dh2b_ref[...] = dh2.astype(bf16)
        part_ref[...] = jnp.zeros_like(part_ref)
        part_ref[0:1, :] = jnp.sum(dy * n, axis=0, keepdims=True)
        part_ref[1:2, :] = (0.5 / D) * jnp.sum(e * e, axis=0, keepdims=True)

    row = lambda n: pl.BlockSpec((tm, n), lambda i: (i, 0))
    vec = pl.BlockSpec((1, D), lambda i: (0, 0))
    col = pl.BlockSpec((nf, tm, tf), lambda i: (0, i, 0))
    held = lambda w: pl.BlockSpec(w.shape, lambda i: (0, 0, 0), pipeline_mode=pl.Buffered(1))
    act = jax.ShapeDtypeStruct((nf, T, tf), bf16)
    return pl.pallas_call(
        body, name="ffn_fwd", grid=(T // tm,),
        in_specs=[row(D), vec, held(wg), held(wu), held(wd), vec, row(D)],
        out_specs=[row(D), col, col, col, row(D), row(D), pl.BlockSpec((8, D), lambda i: (i, 0))],
        out_shape=[jax.ShapeDtypeStruct((T, D), bf16), act, act, act, jax.ShapeDtypeStruct((T, D), f32),
                   jax.ShapeDtypeStruct((T, D), bf16), jax.ShapeDtypeStruct((T // tm * 8, D), f32)],
        compiler_params=_params(("parallel",)),
    )(h1, g_ffn, wg, wu, wd, g_fin, target)


def _ffn_bwd(dh2, h1, g_ffn, a, b, wg, wu, wd):
    T, D = h1.shape
    nf, tf, _ = wg.shape
    tm = _tile(T, ROW_TILE // 2, 8)

    def body(dh2_ref, h1_ref, g_ref, a_ref, b_ref, wg_ref, wu_ref, wd_ref, da_ref, db_ref, dh1_ref, dh1b_ref, part_ref):
        dh2v = dh2_ref[...]
        dh2b = dh2v.astype(bf16)
        dhn = jnp.zeros((tm, D), f32)
        for k in range(nf):
            df = _dot_nt(dh2b, wd_ref[k])
            av = a_ref[k].astype(f32)
            sg = _sig(av)
            db = (df * (av * sg)).astype(bf16)
            da = (df * b_ref[k].astype(f32) * (sg * (1.0 + av * (1.0 - sg)))).astype(bf16)
            da_ref[k] = da
            db_ref[k] = db
            dhn = dhn + _dot(da, wg_ref[k]) + _dot(db, wu_ref[k])
        h = h1_ref[...]
        r = lax.rsqrt(jnp.mean(h * h, axis=-1, keepdims=True) + EPS)
        n = h * r
        dn = dhn * g_ref[...]
        dh1 = dh2v + r * (dn - n * jnp.mean(dn * n, axis=-1, keepdims=True))
        dh1_ref[...] = dh1
        dh1b_ref[...] = dh1.astype(bf16)
        part_ref[...] = jnp.zeros_like(part_ref)
        part_ref[0:1, :] = jnp.sum(dhn * n, axis=0, keepdims=True)

    row = lambda n: pl.BlockSpec((tm, n), lambda i: (i, 0))
    col = pl.BlockSpec((nf, tm, tf), lambda i: (0, i, 0))
    held = lambda w: pl.BlockSpec(w.shape, lambda i: (0, 0, 0), pipeline_mode=pl.Buffered(1))
    act = jax.ShapeDtypeStruct((nf, T, tf), bf16)
    return pl.pallas_call(
        body, name="ffn_bwd", grid=(T // tm,),
        in_specs=[row(D), row(D), pl.BlockSpec((1, D), lambda i: (0, 0)), col, col, held(wg), held(wu), held(wd)],
        out_specs=[col, col, row(D), row(D), pl.BlockSpec((8, D), lambda i: (i, 0))],
        out_shape=[act, act, jax.ShapeDtypeStruct((T, D), f32), jax.ShapeDtypeStruct((T, D), bf16),
                   jax.ShapeDtypeStruct((T // tm * 8, D), f32)],
        compiler_params=_params(("parallel",)),
    )(dh2, h1, g_ffn, a, b, wg, wu, wd)


def _mix_bwd(dh1, proj, yr, ya, b_gate, w_ro, w_ao, w_out, phase=None):
    T, D = dh1.shape
    tm = _tile(T, ROW_TILE, 8)

    def body(dh1_ref, g0, g1, g2, g3, bg_ref, yr_ref, ya_ref, wro_ref, wao_ref, wo_ref,
             du_ref, dao_ref, dgl_ref, mix_ref, dyr_ref, dya_ref, part_ref):
        dmix = _dot_nt(dh1_ref[...], wo_ref[...])
        gr, ga = _gates((g0, g1, g2, g3), bg_ref)
        yr = yr_ref[...].astype(f32)
        ya = ya_ref[...].astype(f32)
        dyr = (dmix * gr).astype(bf16)
        dya = (dmix * ga).astype(bf16)
        dgl = jnp.concatenate([dmix * yr * gr * (1.0 - gr), dmix * ya * ga * (1.0 - ga)], axis=1)
        du_ref[...] = _dot_nt(dyr, wro_ref[...]).astype(bf16)
        ns = wao_ref.shape[2]
        dao = _dot_nt(dya[:, :ns], wao_ref[0])
        for k in range(1, N_CHIPS):
            dao = dao + _dot_nt(dya[:, k * ns:(k + 1) * ns], wao_ref[k])
        dao_ref[...] = dao.astype(bf16)
        dgl_ref[...] = dgl.astype(bf16)
        mix_ref[...] = (gr * yr + ga * ya).astype(bf16)
        dyr_ref[...] = dyr
        dya_ref[...] = dya
        part_ref[...] = jnp.zeros_like(part_ref)
        part_ref[0:1, :] = jnp.sum(dgl, axis=0, keepdims=True)

    full = lambda a: pl.BlockSpec(a.shape, lambda i: (0,) * a.ndim)
    row = lambda n: pl.BlockSpec((tm, n), lambda i: (i, 0))
    return _call(
        body, phase, name="mix_bwd", grid=(T // tm,), scratch_shapes=[],
        in_specs=[row(D), *_gl_specs(tm), full(b_gate), row(D), row(D), full(w_ro), full(w_ao), full(w_out)],
        out_specs=[row(D), row(ATT_W), row(2 * D), row(D), row(D), row(D), pl.BlockSpec((8, 2 * D), lambda i: (i, 0))],
        out_shape=[jax.ShapeDtypeStruct((T, D), bf16), jax.ShapeDtypeStruct((T, ATT_W), bf16),
                   jax.ShapeDtypeStruct((T, 2 * D), bf16), jax.ShapeDtypeStruct((T, D), bf16),
                   jax.ShapeDtypeStruct((T, D), bf16), jax.ShapeDtypeStruct((T, D), bf16),
                   jax.ShapeDtypeStruct((T // tm * 8, 2 * D), f32)],
        args=(dh1, proj, proj, proj, proj, b_gate, yr, ya, w_ro, w_ao, w_out))


def _ret_bwd(proj, qr, kr, o, states, du, dgl, B, S, rope, decay, phase=None):
    T = B * S
    H, dk, dv = RET_HEADS, RET_KEY_DIM, RET_VAL_DIM
    sb = RET_CHUNKS * CHUNK
    ns = S // sb

    def body(qr_ref, kr_ref, v_ref, g_ref, o_ref, st_ref, du_ref, dgl_ref, cos_ref, sin_ref, intra_ref, qd_ref, kd_ref,
             cd_ref, dp_ref, dstate_ref):
        dq_ref, dk_ref = dp_ref.at[:, pl.ds(C_RQ, H * dk)], dp_ref.at[:, pl.ds(C_RK, H * dk)]
        dv_ref, dg_ref = dp_ref.at[:, pl.ds(C_RV, H * dv)], dp_ref.at[:, pl.ds(C_RG, H * dv)]
        dp_ref[:, C_GL:] = dgl_ref[...]

        @pl.when(pl.program_id(1) == 0)
        def _():
            dstate_ref[...] = jnp.zeros_like(dstate_ref)

        cos, snb = cos_ref[...], -sin_ref[...]
        dstates = [dstate_ref[h] for h in range(H)]
        for ci in reversed(range(RET_CHUNKS)):
            r = slice(ci * CHUNK, (ci + 1) * CHUNK)
            for h in range(H):
                hk, hv = slice(h * dk, (h + 1) * dk), slice(h * dv, (h + 1) * dv)
                intra, qd, kd = intra_ref[h], qd_ref[h], kd_ref[h]
                qi, ki, vi = qr_ref[r, hk], kr_ref[r, hk], v_ref[r, hv]
                si = st_ref[0, h, ci]
                o = o_ref[r, hv].astype(f32)
                mu = jnp.mean(o, axis=-1, keepdims=True)
                xc = o - mu
                rstd = lax.rsqrt(jnp.mean(xc * xc, axis=-1, keepdims=True) + EPS)
                oh = xc * rstd
                g = g_ref[r, hv].astype(f32)
                sg = _sig(g)
                dui = du_ref[r, hv].astype(f32)
                dg_ref[r, hv] = (dui * oh * (sg * (1.0 + g * (1.0 - sg)))).astype(bf16)
                doh = dui * (g * sg)
                do = rstd * (doh - jnp.mean(doh, axis=-1, keepdims=True)
                             - oh * jnp.mean(doh * oh, axis=-1, keepdims=True))
                dob = do.astype(bf16)
                p = (_dot_nt(qi, ki) * intra).astype(bf16)
                dsb = dstates[h].astype(bf16)
                kt = (ki.astype(f32) * kd).astype(bf16)
                qt = (qi.astype(f32) * qd).astype(bf16)
                dv_ref[r, hv] = (_dot_tn(p, dob) + _dot(kt, dsb)).astype(bf16)
                da = (_dot_nt(dob, vi) * intra).astype(bf16)
                dq = _dot(da, ki) + _dot_nt(dob, si) * qd
                dkk = (_dot_tn(da, qi) + _dot_nt(vi, dsb) * kd) * K_SCALE
                dq_ref[r, hk] = _rotate(dq, cos[r], snb[r]).astype(bf16)
                dk_ref[r, hk] = _rotate(dkk, cos[r], snb[r]).astype(bf16)
                dstates[h] = dstates[h] * cd_ref[h] + _dot_tn(qt, dob)
        for h in range(H):
            dstate_ref[h] = dstates[h]

    blk = lambda w, c: pl.BlockSpec((sb, w), lambda b, i: (b * ns + ns - 1 - i, c))
    return _call(
        body, phase, name="ret_bwd", grid=(B, ns),
        in_specs=[blk(H * dk, 0), blk(H * dk, 0), blk(H * dv, C_RV // (H * dv)), blk(H * dv, C_RG // (H * dv)),
                  blk(H * dv, 0),
                  pl.BlockSpec((1, H, RET_CHUNKS, dk, dv), lambda b, i: (b, 0, ns - 1 - i, 0, 0)),
                  blk(H * dv, 0), blk(N_IN - C_GL, 0),
                  pl.BlockSpec((sb, dk), lambda b, i: (ns - 1 - i, 0)), pl.BlockSpec((sb, dk), lambda b, i: (ns - 1 - i, 0)),
                  *_ret_tables_specs()],
        out_specs=[blk(N_IN, 0)], out_shape=[jax.ShapeDtypeStruct((T, N_IN), bf16)],
        scratch_shapes=[pltpu.VMEM((H, dk, dv), f32)],
        args=(qr, kr, proj, proj, o, states, du, dgl, *rope, *decay))


def _att_bwd(proj, dao, probs, dproj, B, S, phase=None):
    T = B * S
    nq = S // QBLK
    dh = ATT_HEAD_DIM

    def body(q_ref, k_ref, v_ref, do_ref, p_ref, _, dp_ref, vec_ref, dbias_ref, dka_ref, dva_ref):
        b, i = pl.program_id(0), pl.program_id(1)

        @pl.when((b == 0) & (i == 0))
        def _():
            dbias_ref[...] = jnp.zeros_like(dbias_ref)

        @pl.when(i == 0)
        def _():
            dka_ref[...] = jnp.zeros_like(dka_ref)
            dva_ref[...] = jnp.zeros_like(dva_ref)

        def step(nk):
            win = pl.ds(pl.multiple_of((i + 1) * QBLK - nk, QBLK), nk)
            kw, vw = k_ref[win, :], v_ref[win, :]
            first = _first_of_pair()
            first_rows = lax.broadcasted_iota(jnp.int32, (2 * dh, 1), 0) < dh
            dqs, dks, dvs = [], [], []
            for p in range(ATT_HEADS // 2):
                ps = slice(2 * p * dh, 2 * (p + 1) * dh)
                q2, k2, v2, do2 = q_ref[:, ps] * ATT_SCALE, kw[:, ps], vw[:, ps], do_ref[:, ps]
                dq2, dk2, dv2 = [], [], []
                for e in range(2):
                    h = 2 * p + e
                    prb = p_ref[0, h, :, KWIN - nk:]
                    pr = prb.astype(f32)
                    dp = _dot_nt(jnp.where(first == (e == 0), do2, jnp.zeros_like(do2)), v2)
                    ds = pr * (dp - jnp.sum(pr * dp, axis=-1, keepdims=True))
                    dbias_ref[h, :, KWIN - nk:] += ds
                    dsb = ds.astype(bf16)
                    dq2.append(_dot(dsb, k2) * ATT_SCALE)
                    dk2.append(_dot_tn(q2, dsb))
                    dv2.append(_dot_tn(do2, prb))
                dqs.append(jnp.where(first, dq2[0], dq2[1]))
                dks.append(jnp.where(first_rows, dk2[0], dk2[1]))
                dvs.append(jnp.where(first_rows, dv2[0], dv2[1]))
            dp_ref[pl.ds(pl.multiple_of(i * QBLK, QBLK), QBLK), :ATT_W] = jnp.concatenate(dqs, axis=1).astype(bf16)
            dka_ref[:, win] += jnp.concatenate(dks, axis=0)
            dva_ref[:, win] += jnp.concatenate(dvs, axis=0)

        _by_window(i, step)

        @pl.when(i == nq - 1)
        def _():
            dp_ref[:, ATT_W:2 * ATT_W] = dka_ref[...].T.astype(bf16)
            dp_ref[:, 2 * ATT_W:] = dva_ref[...].T.astype(bf16)

        @pl.when((b == B - 1) & (i == nq - 1))
        def _():
            rr = lax.broadcasted_iota(jnp.int32, (QBLK, QBLK), 0)
            cc = lax.broadcasted_iota(jnp.int32, (QBLK, QBLK), 1)
            flip = jnp.where(rr + cc == QBLK - 1, 1.0, 0.0).astype(bf16)
            for h in range(ATT_HEADS):
                d = dbias_ref[h]
                hi = d.astype(bf16)
                lo = (d - hi.astype(f32)).astype(bf16)
                rev = _dot(flip, hi) + _dot(flip, lo)
                wide = jnp.concatenate([rev, jnp.zeros((QBLK, TOEP - KWIN), f32)], axis=1)
                rolled = pltpu.roll(wide, 0, 1, stride=1, stride_axis=0)
                vec_ref[h:h + 1, :] = jnp.sum(rolled, axis=0, keepdims=True)

    qspec = lambda c: pl.BlockSpec((QBLK, ATT_W), lambda b, i: (b * nq + i, c))
    kspec = lambda c: pl.BlockSpec((S, ATT_W), lambda b, i: (b, c))
    return _call(
        body, phase, name="att_bwd", grid=(B, nq), aliases={5: 0},
        in_specs=[qspec(C_AQ // ATT_W), kspec(C_AK // ATT_W), kspec(C_AV // ATT_W), qspec(0),
                  pl.BlockSpec((1, ATT_HEADS, QBLK, KWIN), lambda b, i: (b * nq + i, 0, 0, 0)),
                  pl.BlockSpec(memory_space=pl.ANY)],
        out_specs=[pl.BlockSpec((S, 3 * ATT_W), lambda b, i: (b, C_AQ // (3 * ATT_W))),
                   pl.BlockSpec((ATT_HEADS, TOEP), lambda b, i: (0, 0))],
        out_shape=[jax.ShapeDtypeStruct((T, N_IN), bf16), jax.ShapeDtypeStruct((ATT_HEADS, TOEP), f32)],
        scratch_shapes=[pltpu.VMEM((ATT_HEADS, QBLK, KWIN), f32), pltpu.VMEM((ATT_W, S), f32),
                        pltpu.VMEM((ATT_W, S), f32)],
        args=(proj, proj, proj, dao, probs, dproj))


def _in_proj_bwd(dproj, w_in, x2, gamma, dh1, phase=None):
    T, D = x2.shape
    nk, _, tk = w_in.shape
    tm = _tile(T, BIG_ROW_TILE, 8)

    def body(dp_ref, w_ref, x_ref, g_ref, dh1_ref, dx_ref, part_ref, acc_ref):
        j = pl.program_id(1)

        @pl.when(j == 0)
        def _():
            acc_ref[...] = jnp.zeros_like(acc_ref)

        acc_ref[...] += _dot_nt(dp_ref[...], w_ref[0])

        @pl.when(j == nk - 1)
        def _():
            x = x_ref[...]
            r = lax.rsqrt(jnp.mean(x * x, axis=-1, keepdims=True) + EPS)
            n = x * r
            dxn = acc_ref[...]
            dn = dxn * g_ref[...]
            dx_ref[...] = dh1_ref[...] + r * (dn - n * jnp.mean(dn * n, axis=-1, keepdims=True))
            part_ref[...] = jnp.zeros_like(part_ref)
            part_ref[0:1, :] = jnp.sum(dxn * n, axis=0, keepdims=True)

    row = lambda n: pl.BlockSpec((tm, n), lambda i, j: (i, 0))
    return _call(
        body, phase, name="in_proj_bwd", grid=(T // tm, nk),
        in_specs=[pl.BlockSpec((tm, tk), lambda i, j: (i, j)), pl.BlockSpec((1, D, tk), lambda i, j: (j, 0, 0)), row(D),
                  pl.BlockSpec((1, D), lambda i, j: (0, 0)), row(D)],
        out_specs=[row(D), pl.BlockSpec((8, D), lambda i, j: (i, 0))],
        out_shape=[jax.ShapeDtypeStruct((T, D), f32), jax.ShapeDtypeStruct((T // tm * 8, D), f32)],
        scratch_shapes=[pltpu.VMEM((tm, D), f32)],
        args=(dproj, w_in, x2, gamma, dh1))


def _wgrad(a, b, shard_axis, name, phase=None):
    def spec(arr, sharded, tt):
        if arr.ndim == 3:
            return arr.shape[2], pl.BlockSpec((1, tt, arr.shape[2]), lambda s, t: (s, t, 0))
        if sharded:
            w = arr.shape[1] // N_CHIPS
            return w, pl.BlockSpec((tt, w), lambda s, t: (t, s))
        return arr.shape[1], pl.BlockSpec((tt, arr.shape[1]), lambda s, t: (t, 0))

    T = a.shape[-2]
    whole = a.ndim == 2 and b.ndim == 2 and a.shape[1] * b.shape[1] * 4 <= WGRAD_ACC_BYTES
    width = lambda arr, sharded: arr.shape[-1] // (1 if whole or arr.ndim == 3 or not sharded else N_CHIPS)
    wa, wb_ = width(a, shard_axis == 0), width(b, shard_axis == 1)
    fixed = wa * wb_ * (4 + 2 * 2)
    tt = T // 4 if whole else T
    while tt > 256 and 2 * tt * (wa * a.dtype.itemsize + wb_ * b.dtype.itemsize) + fixed > WGRAD_VMEM_BYTES:
        tt //= 2
    nt = T // tt
    if whole:
        K, N = a.shape[1], b.shape[1]
        a_spec, b_spec = pl.BlockSpec((tt, K), lambda s, t: (t, 0)), pl.BlockSpec((tt, N), lambda s, t: (t, 0))
        out_block = (N_CHIPS, K // N_CHIPS, N) if shard_axis == 0 else (N_CHIPS, K, N // N_CHIPS)
        out_spec = pl.BlockSpec(out_block, lambda s, t: (0, 0, 0))
    else:
        K, a_spec = spec(a, shard_axis == 0, tt)
        N, b_spec = spec(b, shard_axis == 1, tt)
        out_block = (N_CHIPS, K, N)
        out_spec = pl.BlockSpec((1, K, N), lambda s, t: (s, 0, 0))

    def body(a_ref, b_ref, o_ref, acc_ref):
        t = pl.program_id(1)

        @pl.when(t == 0)
        def _():
            acc_ref[...] = jnp.zeros_like(acc_ref)

        av = a_ref[0] if a.ndim == 3 else a_ref[...]
        bv = b_ref[0] if b.ndim == 3 else b_ref[...]
        acc_ref[...] += _dot_tn(av.astype(bf16), bv.astype(bf16))

        @pl.when(t == nt - 1)
        def _():
            if not whole:
                o_ref[0] = acc_ref[...].astype(bf16)
            else:
                _, kk, nn = out_block
                for s in range(N_CHIPS):
                    o_ref[s] = (acc_ref[s * kk:(s + 1) * kk, :] if shard_axis == 0
                                else acc_ref[:, s * nn:(s + 1) * nn]).astype(bf16)

    (grad,), carried = _call(
        body, phase, name=name, grid=(1 if whole else N_CHIPS, nt), in_specs=[a_spec, b_spec], out_specs=[out_spec],
        out_shape=[jax.ShapeDtypeStruct(out_block, bf16)], scratch_shapes=[pltpu.VMEM((K, N), f32)], args=(a, b))
    return grad, carried


def _adamw_sum(place, groups, name):
    n = len(groups)
    R, C = groups[0][0].shape
    half = R // 2
    tr = _tile(half, max(16, (1 << 18) // C // 16 * 16), 16)
    nr = half // tr

    def body(p_ref, *refs):
        for a in range(n):
            w_ref, m_ref, v_ref, part_ref, fc_ref, fs_ref = refs[6 * a:6 * a + 6]
            g_ref, d_ref, mo_ref, vo_ref = refs[6 * n + 4 * a:6 * n + 4 * a + 4]
            up = lambda x: x.astype(f32)
            mine = ((up(part_ref[0]) + up(fc_ref[0])) + up(fc_ref[1])) + up(fc_ref[2])
            sibs = ((up(fs_ref[0]) + up(fs_ref[1])) + up(fs_ref[2])) + up(fs_ref[3])
            g_ = jnp.where(pl.program_id(0) == p_ref[0], mine, sibs)
            m_ = ADAM_B1 * m_ref[...] + (1.0 - ADAM_B1) * g_
            v_ = ADAM_B2 * v_ref[...] + (1.0 - ADAM_B2) * (g_ * g_)
            m_hat = m_ / (1.0 - ADAM_B1 ** ADAM_STEP)
            v_hat = v_ / (1.0 - ADAM_B2 ** ADAM_STEP)
            g_ref[...] = g_
            d_ref[...] = -ADAM_LR * (m_hat / (jnp.sqrt(v_hat) + ADAM_EPS) + ADAM_WD * w_ref[...])
            mo_ref[...] = m_
            vo_ref[...] = v_

    spec = pl.BlockSpec((tr, C), lambda h, r, p: (h * nr + r, 0))
    one = [spec, spec, spec, pl.BlockSpec((1, tr, C), lambda h, r, p: (p[1], jnp.where(h == p[0], r, 0), 0)),
           pl.BlockSpec((3, tr, C), lambda h, r, p: (0, jnp.where(h == p[0], r, 0), 0)),
           pl.BlockSpec((4, tr, C), lambda h, r, p: (0, jnp.where(h == p[0], 0, r), 0))]
    res = pl.pallas_call(
        body, name=name,
        grid_spec=pltpu.PrefetchScalarGridSpec(num_scalar_prefetch=1, grid=(2, nr), in_specs=one * n,
                                               out_specs=[spec] * (4 * n)),
        out_shape=[jax.ShapeDtypeStruct((R, C), f32)] * (4 * n),
        compiler_params=_params(("parallel", "parallel")),
    )(place, *[x for g in groups for x in g])
    return [tuple(res[4 * a:4 * a + 4]) for a in range(n)]


def _adamw(w, g, m, v, name):
    R, C = w.shape
    tr = _tile(R, max(8, (1 << 18) // C // 8 * 8), 8)

    def body(w_ref, g_ref, m_ref, v_ref, d_ref, mo_ref, vo_ref):
        g_ = g_ref[...]
        m_ = ADAM_B1 * m_ref[...] + (1.0 - ADAM_B1) * g_
        v_ = ADAM_B2 * v_ref[...] + (1.0 - ADAM_B2) * (g_ * g_)
        m_hat = m_ / (1.0 - ADAM_B1 ** ADAM_STEP)
        v_hat = v_ / (1.0 - ADAM_B2 ** ADAM_STEP)
        d_ref[...] = -ADAM_LR * (m_hat / (jnp.sqrt(v_hat) + ADAM_EPS) + ADAM_WD * w_ref[...])
        mo_ref[...] = m_
        vo_ref[...] = v_

    spec = pl.BlockSpec((tr, C), lambda i: (i, 0))
    return pl.pallas_call(
        body, name=name, grid=(R // tr,), in_specs=[spec] * 4, out_specs=[spec] * 3,
        out_shape=[jax.ShapeDtypeStruct((R, C), f32)] * 3,
        compiler_params=_params(("parallel",)),
    )(w, g, m, v)


def _place():
    return lax.axis_index("x"), lax.axis_index("y"), lax.axis_index("c")


def _other_chips(x, y):
    chips = [(1 - x, y), (x, 1 - y), (1 - x, 1 - y)]
    return chips, [2 * cx + cy for cx, cy in chips]


def _spread_phase(blk):
    def peers():
        x, y, c = _place()
        return [tuple(1 - p if (k >> s) & 1 else p for p, s in ((x, 2), (y, 1), (c, 0))) for k in range(1, N_DEV)]

    def copies(pin, out):
        x, y, c = _place()
        mine = out[0].at[4 * x + 2 * y + c]
        return [(mine, mine, peer) for peer in peers()]

    stack = jnp.broadcast_to(blk, (N_DEV,) + blk.shape)
    return _Phase([stack], [jax.ShapeDtypeStruct(stack.shape, stack.dtype)], {0: 0}, N_DEV - 1, copies,
                  lambda pin, out: [out[0].at[4 * px + 2 * py + pc] for px, py, pc in peers()])


def _sum_slots(stack, name):
    def body(s_ref, o_ref):
        tot = s_ref[0]
        for d in range(1, stack.shape[0]):
            tot = tot + s_ref[d]
        o_ref[...] = tot

    vm = pl.BlockSpec(memory_space=pltpu.VMEM)
    return pl.pallas_call(body, name=name, in_specs=[vm], out_specs=vm,
                          out_shape=jax.ShapeDtypeStruct(stack.shape[1:], stack.dtype))(stack)


def _cast_shards(place, ws, name):
    n = len(ws)
    R, C = ws[0].shape
    tr = _tile(R, max(16, (1 << 19) // C // 16 * 16), 16)

    def body(p_ref, *refs):
        for a in range(n):
            refs[n + a][0] = refs[a][...].astype(bf16)

    return pl.pallas_call(
        body, name=name,
        grid_spec=pltpu.PrefetchScalarGridSpec(
            num_scalar_prefetch=1, grid=(R // tr,),
            in_specs=[pl.BlockSpec((tr, C), lambda r, p: (r, 0))] * n,
            out_specs=[pl.BlockSpec((1, tr, C), lambda r, p: (p[1], r, 0))] * n),
        out_shape=[jax.ShapeDtypeStruct((N_CHIPS, R, C), bf16)] * n,
        compiler_params=_params(("parallel",)),
    )(place, *ws)


class _Phase:
    def __init__(self, arrays, out_shapes, aliases, n_copies, copies, arrivals, own_starts=(), own_waits=()):
        self.arrays, self.out_shapes, self.aliases = list(arrays), list(out_shapes), dict(aliases)
        self.n_copies, self.copies, self.arrivals = n_copies, copies, arrivals
        self.own_starts, self.own_waits = tuple(own_starts), tuple(own_waits)

    def sems(self):
        return [pltpu.SemaphoreType.DMA((self.n_copies,)), pltpu.SemaphoreType.DMA((self.n_copies,))]

    def _descriptors(self, pin, pout, send_sems, recv_sems):
        return [pltpu.make_async_remote_copy(src_ref=s, dst_ref=d, send_sem=send_sems.at[i], recv_sem=recv_sems.at[i],
                                             device_id=to, device_id_type=MESH)
                for i, (s, d, to) in enumerate(self.copies(pin, pout))]

    def _arrival(self, i, pin, pout, send_sems, recv_sems):
        dst = self.arrivals(pin, pout)[i]
        return pltpu.make_async_remote_copy(src_ref=dst, dst_ref=dst, send_sem=send_sems.at[i], recv_sem=recv_sems.at[i],
                                            device_id=_place(), device_id_type=MESH)

    def start(self, pin, pout, send_sems, recv_sems):
        for i, cp in enumerate(self._descriptors(pin, pout, send_sems, recv_sems)):
            if i not in self.own_starts:
                cp.start()

    def begin(self, i, pin, pout, send_sems, recv_sems):
        self._descriptors(pin, pout, send_sems, recv_sems)[i].start()

    def arrived(self, i, pin, pout, send_sems, recv_sems):
        self._arrival(i, pin, pout, send_sems, recv_sems).wait_recv()

    def finish(self, pin, pout, send_sems, recv_sems):
        for i in range(self.n_copies):
            if i not in self.own_waits:
                self._arrival(i, pin, pout, send_sems, recv_sems).wait_recv()
        for cp in self._descriptors(pin, pout, send_sems, recv_sems):
            cp.wait_send()


def _join(phases):
    if len(phases) == 1:
        return phases[0]
    ai = np.cumsum([0] + [len(p.arrays) for p in phases])
    oi = np.cumsum([0] + [len(p.out_shapes) for p in phases])

    def each(fn_name, pin, pout):
        return [item for k, p in enumerate(phases)
                for item in getattr(p, fn_name)(pin[ai[k]:ai[k + 1]], pout[oi[k]:oi[k + 1]])]

    aliases = {int(ai[k]) + i: int(oi[k]) + j for k, p in enumerate(phases) for i, j in p.aliases.items()}
    ci = np.cumsum([0] + [p.n_copies for p in phases])
    shifted = lambda attr: [int(ci[k]) + i for k, p in enumerate(phases) for i in getattr(p, attr)]
    return _Phase([a for p in phases for a in p.arrays], [s for p in phases for s in p.out_shapes], aliases,
                  int(ci[-1]), functools.partial(each, "copies"), functools.partial(each, "arrivals"),
                  shifted("own_starts"), shifted("own_waits"))


def _call(body, phase, *, name, grid, in_specs, out_specs, out_shape, scratch_shapes, args, prefetch=(), expose=False,
          aliases=None):
    seq = _params(("arbitrary",) * len(grid))
    np_ = len(prefetch)
    own = {np_ + i: j for i, j in (aliases or {}).items()}
    if phase is None:
        spec = pltpu.PrefetchScalarGridSpec(num_scalar_prefetch=np_, grid=grid, in_specs=in_specs, out_specs=out_specs,
                                            scratch_shapes=scratch_shapes)
        res = pl.pallas_call(body, name=name, grid_spec=spec, out_shape=out_shape, input_output_aliases=own,
                             compiler_params=seq)(*prefetch, *args)
        return list(res), []
    ni, no, ns = len(in_specs), len(out_specs), len(scratch_shapes)
    pi, po = len(phase.arrays), len(phase.out_shapes)

    def hosted(*refs):
        cut = np.cumsum([np_, ni, pi, no, po, ns])
        pre, ins, pin, outs, pout, scr, sems = (refs[a:b] for a, b in zip([0, *cut], [*cut, len(refs)]))
        ids = [pl.program_id(d) for d in range(len(grid))]
        first = functools.reduce(lambda p, q: p & q, [i == 0 for i in ids])
        last = functools.reduce(lambda p, q: p & q, [i == g - 1 for i, g in zip(ids, grid)])
        pl.when(first)(lambda: phase.start(pin, pout, *sems))
        body(*pre, *ins, *outs, *scr, **({"carried": (pin, pout, sems)} if expose else {}))
        pl.when(last)(lambda: phase.finish(pin, pout, *sems))

    anyspace = pl.BlockSpec(memory_space=pl.ANY)
    spec = pltpu.PrefetchScalarGridSpec(
        num_scalar_prefetch=np_, grid=grid, in_specs=list(in_specs) + [anyspace] * pi,
        out_specs=list(out_specs) + [anyspace] * po, scratch_shapes=list(scratch_shapes) + phase.sems())
    res = pl.pallas_call(
        hosted, name=name, grid_spec=spec, out_shape=list(out_shape) + phase.out_shapes,
        input_output_aliases={**own, **{np_ + ni + i: no + j for i, j in phase.aliases.items()}}, compiler_params=seq,
    )(*prefetch, *args, *phase.arrays)
    return list(res[:no]), list(res[no:])


def _run_phases(name, phases):
    first = phases[0]
    pi, po = len(first.arrays), len(first.out_shapes)

    def body(*refs):
        pin, pout, sems = refs[:pi], refs[pi:pi + po], refs[pi + po:]
        for n, ph in enumerate(phases):
            ph.start(pin, pout, *sems[2 * n:2 * n + 2])
            ph.finish(pin, pout, *sems[2 * n:2 * n + 2])

    anyspace = pl.BlockSpec(memory_space=pl.ANY)
    return list(pl.pallas_call(
        body, name=name, in_specs=[anyspace] * pi, out_specs=[anyspace] * po, out_shape=first.out_shapes,
        input_output_aliases=first.aliases, scratch_shapes=[s for ph in phases for s in ph.sems()],
    )(*first.arrays))


def _half_rows(buf, c):
    half = buf.shape[1] // 2
    return pl.ds(c * half, half), pl.ds((1 - c) * half, half)


def _gather_phase(bufs, over_ici):
    n = len(bufs)
    shapes = [jax.ShapeDtypeStruct(b.shape, b.dtype) for b in bufs]

    def landed(out, which):
        x, y, c = _place()
        _, ks = _other_chips(x, y)
        return [out[a].at[ks[j], _half_rows(bufs[a], c)[which]] for a in range(n) for j in range(3)]

    def ici(pin, out):
        x, y, c = _place()
        chips, _ = _other_chips(x, y)
        mine = [out[a].at[2 * x + y, _half_rows(bufs[a], c)[0]] for a in range(n)]
        return [(mine[a], mine[a], (*chips[j], c)) for a in range(n) for j in range(3)]

    def d2d(pin, out):
        x, y, c = _place()
        return [(dst, dst, (x, y, 1 - c)) for dst in landed(out, 0)]

    if over_ici:
        return _Phase(bufs, shapes, {a: a for a in range(n)}, 3 * n, ici, lambda pin, out: landed(out, 0))
    return _Phase(bufs, shapes, {a: a for a in range(n)}, 3 * n, d2d, lambda pin, out: landed(out, 1))


def _feed_phase(buf):
    def chips():
        x, y, _ = _place()
        return [(x if f < 2 else 1 - x, y if f % 2 == 0 else 1 - y) for f in (1, 2, 3)]

    def copies(pin, out):
        x, y, c = _place()
        mine = _half_rows(buf, c)[0]
        own = out[0].at[2 * x + y, mine]
        sent = [(own, own, (cx, cy, c)) for cx, cy in chips()]
        return sent + [(out[0].at[2 * cx + cy, mine], out[0].at[2 * cx + cy, mine], (x, y, 1 - c)) for cx, cy in chips()]

    def arrivals(pin, out):
        mine, theirs = _half_rows(buf, _place()[2])
        return [out[0].at[2 * cx + cy, rows] for rows in (mine, theirs) for cx, cy in chips()]

    return _Phase([buf], [jax.ShapeDtypeStruct(buf.shape, buf.dtype)], {0: 0}, 6, copies, arrivals,
                  own_starts=(3, 4, 5), own_waits=range(6))


def _rs_swap_phase(grads):
    n = len(grads)

    def copies(g, out):
        x, y, c = _place()
        return [(g[a].at[:, _half_rows(grads[a], c)[1]], out[a], (x, y, 1 - c)) for a in range(n)]

    shapes = [jax.ShapeDtypeStruct((N_CHIPS, g.shape[1] // 2, g.shape[2]), g.dtype) for g in grads]
    return _Phase(grads, shapes, {}, n, copies, lambda g, out: list(out))


def _rs_add_sibling(place, grads, gots, name):
    n = len(grads)
    _, R, C = grads[0].shape
    half = R // 2
    tr = _tile(half, max(16, (1 << 19) // C // 16 * 16), 16)
    nr = half // tr

    def body(p_ref, *refs):
        for a in range(n):
            refs[2 * n + a][...] = (refs[2 * a][...].astype(f32) + refs[2 * a + 1][...].astype(f32)).astype(bf16)

    res = pl.pallas_call(
        body, name=name,
        grid_spec=pltpu.PrefetchScalarGridSpec(
            num_scalar_prefetch=1, grid=(N_CHIPS, nr),
            in_specs=[pl.BlockSpec((1, tr, C), lambda k, r, p: (k, p[0] * nr + r, 0)),
                      pl.BlockSpec((1, tr, C), lambda k, r, p: (k, r, 0))] * n,
            out_specs=[pl.BlockSpec((1, tr, C), lambda k, r, p: (k, r, 0))] * n),
        out_shape=[jax.ShapeDtypeStruct((N_CHIPS, half, C), bf16)] * n,
        compiler_params=_params(("parallel", "parallel")),
    )(place, *[x for pair in zip(grads, gots) for x in pair])
    return list(res)


def _rs_chips_phase(parts):
    n = len(parts)

    def copies(p, fc):
        x, y, c = _place()
        chips, ks = _other_chips(x, y)
        return [(p[a].at[ks[j]], fc[a].at[j], (*chips[j], c)) for a in range(n) for j in range(3)]

    shapes = [jax.ShapeDtypeStruct((3,) + q.shape[1:], q.dtype) for q in parts]
    return _Phase(parts, shapes, {}, 3 * n, copies, lambda p, fc: [fc[a].at[j] for a in range(n) for j in range(3)])


def _rs_hand_phase(parts, from_chips):
    n = len(parts)

    def copies(pin, fs):
        x, y, c = _place()
        sib = (x, y, 1 - c)
        own = [(pin[a].at[2 * x + y], fs[a].at[0], sib) for a in range(n)]
        return own + [(pin[n + a].at[j], fs[a].at[1 + j], sib) for a in range(n) for j in range(3)]

    def arrivals(pin, fs):
        return [fs[a].at[0] for a in range(n)] + [fs[a].at[1 + j] for a in range(n) for j in range(3)]

    shapes = [jax.ShapeDtypeStruct((4,) + q.shape[1:], q.dtype) for q in parts]
    return _Phase(list(parts) + list(from_chips), shapes, {}, 4 * n, copies, arrivals)


class _Exchange:
    def __init__(self, place):
        self.place = place

    def feed(self, buf):
        return _feed_phase(buf)

    def gather(self, bufs, over_ici):
        return _gather_phase(bufs, over_ici)

    def swap(self, grads):
        return _rs_swap_phase(grads)

    def pair_sums(self, names, grads):
        return self.add(names, grads, _run_phases("rs_sibling_" + names[0], [_rs_swap_phase(grads)]))

    def add(self, names, grads, got):
        parts = {}
        for group in _same_shape(grads):
            res = _rs_add_sibling(self.place, [grads[i] for i in group], [got[i] for i in group],
                                  "rs_add_" + names[group[0]])
            parts.update(zip(group, res))
        return [parts[i] for i in range(len(names))]

    def to_chips(self, parts):
        return _rs_chips_phase(parts)

    def to_sibling(self, parts, from_chips):
        return _rs_hand_phase(parts, from_chips)

    def spread(self, blk):
        return _spread_phase(blk)

    def hand_over(self, name, parts, from_chips, blk):
        got = _run_phases(name, [_join([_rs_hand_phase(parts, from_chips), _spread_phase(blk)])])
        return got[:-1], got[-1]


def _local_step(place, x, target, norm_mix, b_gate, rb_chip, norm_ffn, norm_final, w_in, rest, exch):
    B, S, D = x.shape
    T = B * S
    x2 = x.reshape(T, D)
    tg2 = target.reshape(T, D)
    rope, decay = _rope_tables(S), _decay_tables()
    g_fin = norm_final.reshape(1, D)
    nrel = rb_chip.shape[-1]

    mrg, ffn = ["w_ret_out", "w_att_out", "w_out"], ["w_ffn_gate", "w_ffn_up", "w_ffn_down"]
    (xn, proj), got = _in_proj(place, x2, norm_mix, _join([exch.feed(w_in), exch.gather([rest[n] for n in mrg], True),
                                                           exch.spread(jnp.pad(rb_chip, ((0, 0), (0, 128 - nrel))))]))
    w_in, wb, rb_all = got[0], {}, got.pop()
    trows = _bias_rows(jnp.concatenate([rb_all[2 * k, :, :nrel] for k in range(N_CHIPS)], axis=1))
    (qr, kr, o, u, states), got = _ret_fwd(proj, B, S, rope, decay, _join([exch.gather([rest["w_ffn_gate"]], True),
                                                                         exch.gather(got[1:], False)]))
    wb.update(zip(mrg, got[1:]))
    (ao, probs), got = _att_fwd(proj, trows, B, S, _join([exch.gather([rest["w_ffn_up"], rest["w_ffn_down"]], True),
                                                    exch.gather(got[:1], False)]))
    wb["w_ffn_gate"] = got[2]
    w_ro, w_out = wb["w_ret_out"].reshape(-1, D), wb["w_out"].reshape(-1, D)
    (h1, yr, ya), got = _mix_fwd(x2, proj, u, ao, b_gate, w_ro, wb["w_att_out"], w_out, exch.gather(got[:2], False))
    wb.update(zip(ffn[1:], got))
    hn, a, b, f, dh2, dh2b, part_fin = _ffn_fwd(h1, norm_ffn, wb["w_ffn_gate"], wb["w_ffn_up"], wb["w_ffn_down"], g_fin, tg2)

    da, db, dh1, dh1b, part_ffn = _ffn_bwd(dh2, h1, norm_ffn, a, b, wb["w_ffn_gate"], wb["w_ffn_up"], wb["w_ffn_down"])
    ffn = ["w_ffn_down", "w_ffn_gate", "w_ffn_up"]
    g_ffn = [_wgrad(f, dh2b, 0, "wgrad_ffn_down")[0], _wgrad(da, hn, 0, "wgrad_ffn_gate")[0],
             _wgrad(db, hn, 0, "wgrad_ffn_up")[0]]
    (du, dao, dgl, mix, dyr, dya, part_bg), x_ffn = _mix_bwd(dh1b, proj, yr, ya, b_gate, w_ro, wb["w_att_out"], w_out,
                                                             exch.swap(g_ffn))
    p_ffn = exch.add(ffn, g_ffn, x_ffn)
    mrg = ["w_out", "w_ret_out", "w_att_out"]
    g_mrg = [_wgrad(mix, dh1b, 0, "wgrad_out")[0], _wgrad(u, dyr, 0, "wgrad_ret_out")[0],
             _wgrad(ao, dya, 1, "wgrad_att_out")[0]]
    (dproj,), got = _ret_bwd(proj, qr, kr, o, states, du, dgl, B, S, rope, decay, _join([exch.to_chips(p_ffn[:2]),
                                                                                          exch.swap(g_mrg)]))
    c_two, p_mrg = got[:2], exch.add(mrg, g_mrg, got[2:])
    (dproj, dvec), got = _att_bwd(proj, dao, probs, dproj, B, S, exch.to_chips(p_ffn[2:] + p_mrg))
    c_ffn, c_mrg = c_two + got[:1], got[1:]
    g_in, got = _wgrad(xn, dproj, 1, "wgrad_in", exch.to_sibling(p_ffn + p_mrg, c_ffn + c_mrg))
    s_ffn, s_mrg = got[:len(ffn)], got[len(ffn):]
    p_in = exch.pair_sums(["w_in"], [g_in])
    (gx, part_mix), c_in = _in_proj_bwd(dproj, w_in, x2, norm_mix, dh1, exch.to_chips(p_in))
    rows = lambda p, r: p.reshape(-1, 8, p.shape[-1])[:, r, :].sum(axis=0)
    lo = KWIN - 1 - (MAX_REL - 1)
    drb = jnp.concatenate([jnp.flip(dvec[:, lo:lo + N_REL - 1], axis=1), dvec[:, :lo].sum(axis=1, keepdims=True)], axis=1)
    gsmall = {
        "norm_mix": rows(part_mix, 0), "b_gate": rows(part_bg, 0), "rel_bias": drb, "norm_ffn": rows(part_ffn, 0),
        "norm_final": rows(part_fin, 0),
    }
    s_in, small_all = exch.hand_over("rs_hand_w_in", p_in, c_in, _pack_small(gsmall, rows(part_fin, 1)))
    gbig = dict(zip(ffn + mrg + ["w_in"], zip(p_ffn + p_mrg + p_in, c_ffn + c_mrg + c_in, s_ffn + s_mrg + s_in)))
    return gx.reshape(B, S, D), gbig, small_all


SMALL_ROWS = 16


def _pack_small(gs, loss_lanes):
    D = D_MODEL
    rb = jnp.pad(gs["rel_bias"].reshape(-1), (0, 3 * D - ATT_HEADS * N_REL)).reshape(3, D)
    rows = [gs["norm_mix"].reshape(1, D), gs["b_gate"].reshape(2, D), gs["norm_ffn"].reshape(1, D),
            gs["norm_final"].reshape(1, D), rb, loss_lanes.reshape(1, D)]
    used = sum(r.shape[0] for r in rows)
    return jnp.concatenate(rows + [jnp.zeros((SMALL_ROWS - used, D), f32)], axis=0)


def kernel(x, norm_mix, w_in, b_gate, rel_bias, w_ret_out, w_att_out, w_out, norm_ffn, w_ffn_gate, w_ffn_up, w_ffn_down, norm_final, loss_target, m_norm_mix, m_w_in, m_b_gate, m_rel_bias, m_w_ret_out, m_w_att_out, m_w_out, m_norm_ffn, m_w_ffn_gate, m_w_ffn_up, m_w_ffn_down, m_norm_final, v_norm_mix, v_w_in, v_b_gate, v_rel_bias, v_w_ret_out, v_w_att_out, v_w_out, v_norm_ffn, v_w_ffn_gate, v_w_ffn_up, v_w_ffn_down, v_norm_final):
    w = dict(norm_mix=norm_mix, w_in=w_in, b_gate=b_gate, rel_bias=rel_bias, w_ret_out=w_ret_out, w_att_out=w_att_out,
             w_out=w_out, norm_ffn=norm_ffn, w_ffn_gate=w_ffn_gate, w_ffn_up=w_ffn_up, w_ffn_down=w_ffn_down,
             norm_final=norm_final)
    m = dict(norm_mix=m_norm_mix, w_in=m_w_in, b_gate=m_b_gate, rel_bias=m_rel_bias, w_ret_out=m_w_ret_out,
             w_att_out=m_w_att_out, w_out=m_w_out, norm_ffn=m_norm_ffn, w_ffn_gate=m_w_ffn_gate, w_ffn_up=m_w_ffn_up,
             w_ffn_down=m_w_ffn_down, norm_final=m_norm_final)
    v = dict(norm_mix=v_norm_mix, w_in=v_w_in, b_gate=v_b_gate, rel_bias=v_rel_bias, w_ret_out=v_w_ret_out,
             w_att_out=v_w_att_out, w_out=v_w_out, norm_ffn=v_norm_ffn, w_ffn_gate=v_w_ffn_gate, w_ffn_up=v_w_ffn_up,
             w_ffn_down=v_w_ffn_down, norm_final=v_norm_final)
    xi, yi, ci = _place()
    k_me = 2 * xi + yi

    place = jnp.stack([ci, k_me]).astype(jnp.int32)
    big = [n for n, _ in BIG]

    turned = ("w_ffn_gate", "w_ffn_up")
    shard = lambda d, n: jnp.swapaxes(d[n][0], 0, 1) if n in turned else d[n][0]
    whole = lambda a, n: (jnp.swapaxes(a, 0, 1) if n in turned else a)[None]

    by_shape = [[big[i] for i in group] for group in _same_shape([shard(w, n) for n in big])]
    bufs = {}
    for names in by_shape:
        bufs.update(zip(names, _cast_shards(place, [shard(w, n) for n in names], "cast_" + names[0])))
    rest = {n: bufs[n] for n in big if n != "w_in"}
    nrel_loc = rel_bias.shape[-1]
    grad_x, gbig, small_all = _local_step(place, x, loss_target, norm_mix, b_gate, rel_bias[0], norm_ffn, norm_final,
                                          bufs["w_in"], rest, _Exchange(place))

    small = _sum_slots(small_all, "reduce_small")
    D = D_MODEL
    loss = jnp.sum(small[8])
    drb_full = small[5:8].reshape(-1)[:ATT_HEADS * N_REL].reshape(ATT_HEADS, N_REL)
    g = {
        "norm_mix": small[0:1], "b_gate": small[1:3].reshape(1, 2 * D), "norm_ffn": small[3:4], "norm_final": small[4],
        "rel_bias": lax.dynamic_slice_in_dim(drb_full, k_me * nrel_loc, nrel_loc, axis=1)[None],
    }

    delta, new_m, new_v = {}, {}, {}
    for names in by_shape:
        res = _adamw_sum(place, [(shard(w, n), shard(m, n), shard(v, n), *gbig[n]) for n in names], "adamw_" + names[0])
        for n, (g_, d_, m_, v_) in zip(names, res):
            g[n], delta[n], new_m[n], new_v[n] = whole(g_, n), whole(d_, n), whole(m_, n), whole(v_, n)
    flat = lambda d: jnp.concatenate([d[n].reshape(-1) for n in SMALL])
    n_small = sum(int(np.prod(w[n].shape)) for n in SMALL)
    n_pad = -n_small % 1024
    packs = [jnp.pad(flat(d), (0, n_pad)).reshape(-1, 128) for d in (w, g, m, v)]
    outs = _adamw(*packs, "adamw_small")
    for res, dst in zip(outs, (delta, new_m, new_v)):
        off = 0
        fl = res.reshape(-1)
        for n in SMALL:
            sz = int(np.prod(w[n].shape))
            dst[n] = fl[off:off + sz].reshape(w[n].shape)
            off += sz

    return (loss, grad_x, *[g[n] for n in WEIGHTS], *[delta[n] for n in WEIGHTS], *[new_m[n] for n in WEIGHTS],
            *[new_v[n] for n in WEIGHTS])
```

```python
import functools

import numpy as np
import jax
import jax.numpy as jnp
from jax import lax
from jax.experimental import pallas as pl
from jax.experimental.pallas import tpu as pltpu

f32 = jnp.float32
bf16 = jnp.bfloat16

D_MODEL = 1024
CHUNK = 64
RET_HEADS = 4
RET_KEY_DIM = 128
RET_VAL_DIM = 256
ATT_HEADS = 8
ATT_HEAD_DIM = 64
ATT_W = ATT_HEADS * ATT_HEAD_DIM
BAND_CHUNKS = 8
PAD = BAND_CHUNKS * CHUNK
MAX_REL = 256
N_REL = CHUNK + MAX_REL
D_FF = 2816
N_IN = 6656
ROPE_BASE = 10000.0
EPS = 1e-6
NEG_INF = -1e30
C_RQ, C_RK, C_RV, C_RG, C_AQ, C_AK, C_AV, C_GL = 0, 512, 1024, 2048, 3072, 3584, 4096, 4608

ADAM_LR, ADAM_B1, ADAM_B2, ADAM_EPS, ADAM_WD, ADAM_STEP = 0.001, 0.9, 0.999, 1e-08, 0.01, 10

N_CHIPS = 4
N_DEV = 8
WGRAD_ACC_BYTES = 8 * 1024 * 1024
WGRAD_VMEM_BYTES = 40 * 1024 * 1024
ROW_TILE = 512
BIG_ROW_TILE = 1024
IN_ORDER = (0, 2, 3, 1)
QBLK = 256
KWIN = PAD + QBLK
TOEP = 1024
VMEM_LIMIT = 56 * 1024 * 1024
MESH = pl.DeviceIdType.MESH

BIG = (
    ("w_in", 1), ("w_ret_out", 0), ("w_att_out", 1), ("w_out", 0), ("w_ffn_gate", 1), ("w_ffn_up", 1), ("w_ffn_down", 0))
WEIGHTS = ("norm_mix", "w_in", "b_gate", "rel_bias", "w_ret_out", "w_att_out", "w_out", "norm_ffn", "w_ffn_gate",
           "w_ffn_up", "w_ffn_down", "norm_final")
SMALL = ("norm_mix", "b_gate", "rel_bias", "norm_ffn", "norm_final")


def _dot(a, b):
    return lax.dot_general(a, b, (((1,), (0,)), ((), ())), preferred_element_type=f32)


def _dot_nt(a, b):
    return lax.dot_general(a, b, (((1,), (1,)), ((), ())), preferred_element_type=f32)


def _dot_tn(a, b):
    return lax.dot_general(a, b, (((0,), (0,)), ((), ())), preferred_element_type=f32)


def _sig(x):
    return 1.0 / (1.0 + jnp.exp(-x))


def _tile(n, pref, mult):
    best = None
    for t in range(mult, min(n, pref) + 1, mult):
        if n % t == 0:
            best = t
    return best if best is not None else n


def _same_shape(arrays):
    groups = {}
    for i, a in enumerate(arrays):
        groups.setdefault(a.shape, []).append(i)
    return list(groups.values())


def _add_rows(part_ref, first, *rows):
    @pl.when(first)
    def _():
        part_ref[...] = jnp.zeros_like(part_ref)

    for r, row in enumerate(rows):
        part_ref[r:r + 1, :] += row


def _params(sem, vmem=VMEM_LIMIT):
    return pltpu.CompilerParams(dimension_semantics=sem, vmem_limit_bytes=vmem)


def _in_proj(place, x2, gamma, phase):
    T, D = x2.shape
    _, _, ns = phase.arrays[0].shape
    tm = _tile(T, BIG_ROW_TILE, 8)
    ni = T // tm
    pass_chip = lambda j: sum(jnp.where(j == n, f, 0) for n, f in enumerate(IN_ORDER))

    def body(p_ref, x_ref, g_ref, xn_ref, pr_ref, xs_ref, w_ref, w_sem, carried):
        j, i = pl.program_id(0), pl.program_id(1)
        pin, pout, sems = carried
        rows = pl.ds(pl.multiple_of(i * tm, tm), tm)

        @pl.when(i == 0)
        def _():
            for n, f in enumerate(IN_ORDER):
                if f:
                    @pl.when(j == n)
                    def _():
                        phase.arrived(f - 1, pin, pout, *sems)
                        phase.begin(2 + f, pin, pout, *sems)
                        phase.arrived(2 + f, pin, pout, *sems)
            shard = pltpu.make_async_copy(pout[0].at[jnp.bitwise_xor(p_ref[1], pass_chip(j))], w_ref, w_sem)
            shard.start()
            shard.wait()

        @pl.when(j == 0)
        def _():
            x = x_ref[...]
            r = lax.rsqrt(jnp.mean(x * x, axis=-1, keepdims=True) + EPS)
            xn = (x * r * g_ref[...]).astype(bf16)
            xs_ref[rows, :] = xn
            xn_ref[...] = xn

        pr_ref[...] = _dot(xs_ref[rows, :], w_ref[...]).astype(bf16)

    first_pass = lambda j, i, p: (jnp.where(j == 0, i, ni - 1), 0)
    return _call(
        body, phase, name="in_proj", grid=(N_CHIPS, ni), prefetch=(place,), expose=True,
        in_specs=[pl.BlockSpec((tm, D), first_pass), pl.BlockSpec((1, D), lambda j, i, p: (0, 0))],
        out_specs=[pl.BlockSpec((tm, D), first_pass),
                   pl.BlockSpec((tm, ns), lambda j, i, p: (i, jnp.bitwise_xor(p[1], pass_chip(j))))],
        out_shape=[jax.ShapeDtypeStruct((T, D), bf16), jax.ShapeDtypeStruct((T, N_CHIPS * ns), bf16)],
        scratch_shapes=[pltpu.VMEM((T, D), bf16), pltpu.VMEM((D, ns), bf16), pltpu.SemaphoreType.DMA],
        args=(x2, gamma))


def _rope_tables(S):
    d = RET_KEY_DIM
    freqs = (np.float32(ROPE_BASE) ** (-np.arange(0, d, 2, dtype=np.float32) / np.float32(d))).astype(np.float32)
    ang = np.arange(S, dtype=np.float32)[:, None] * freqs[None, :]
    cos, sin = np.cos(ang).astype(np.float32), np.sin(ang).astype(np.float32)
    return jnp.asarray(np.concatenate([cos, cos], axis=1)), jnp.asarray(np.concatenate([-sin, sin], axis=1))


def _decay_tables():
    H = RET_HEADS
    one = np.float32(1.0)
    log_g = np.log(one - np.float32(2.0) ** (-np.float32(5.0) - np.arange(H, dtype=np.float32))).astype(np.float32)
    p = np.arange(CHUNK, dtype=np.float32)
    intra = np.exp(log_g[:, None, None] * np.abs(p[:, None] - p[None, :]))
    q_dec = np.exp(log_g[:, None] * (p[None, :] + one))
    k_dec = np.exp(log_g[:, None] * (np.float32(CHUNK) - one - p[None, :]))
    c_dec = np.exp(log_g * np.float32(CHUNK))
    q_dec = np.broadcast_to(q_dec[:, :, None], (H, CHUNK, RET_KEY_DIM))
    k_dec = np.broadcast_to(k_dec[:, :, None], (H, CHUNK, RET_KEY_DIM))
    c_dec = np.broadcast_to(c_dec[:, None, None], (H, 1, RET_VAL_DIM))
    return tuple(jnp.asarray(np.ascontiguousarray(t, dtype=np.float32)) for t in (intra, q_dec, k_dec, c_dec))


K_SCALE = RET_KEY_DIM ** -0.5


RET_CHUNKS = 4


def _ret_tables_specs():
    whole = lambda *shape: pl.BlockSpec(shape, lambda b, i: (0,) * len(shape))
    return [whole(RET_HEADS, CHUNK, CHUNK), whole(RET_HEADS, CHUNK, RET_KEY_DIM), whole(RET_HEADS, CHUNK, RET_KEY_DIM),
            whole(RET_HEADS, 1, RET_VAL_DIM)]


def _rotate(x, cos, sn):
    return x * cos + pltpu.roll(x, RET_KEY_DIM // 2, 1) * sn


def _ret_fwd(proj, B, S, rope, decay, phase=None):
    T = B * S
    nc = S // CHUNK
    H, dk, dv = RET_HEADS, RET_KEY_DIM, RET_VAL_DIM
    sb = RET_CHUNKS * CHUNK
    ns = S // sb

    def body(q_ref, k_ref, v_ref, g_ref, cos_ref, sin_ref, intra_ref, qd_ref, kd_ref, cd_ref,
             qr_ref, kr_ref, o_ref, u_ref, st_ref, state_ref):
        @pl.when(pl.program_id(1) == 0)
        def _():
            state_ref[...] = jnp.zeros_like(state_ref)

        cos, sn = cos_ref[...], sin_ref[...]
        for h in range(H):
            hs = slice(h * dk, (h + 1) * dk)
            qr_ref[:, hs] = _rotate(q_ref[:, hs].astype(f32), cos, sn).astype(bf16)
            kr_ref[:, hs] = (_rotate(k_ref[:, hs].astype(f32), cos, sn) * K_SCALE).astype(bf16)
        states = [state_ref[h] for h in range(H)]
        for ci in range(RET_CHUNKS):
            r = slice(ci * CHUNK, (ci + 1) * CHUNK)
            for h in range(H):
                hk, hv = slice(h * dk, (h + 1) * dk), slice(h * dv, (h + 1) * dv)
                qi, ki, vi = qr_ref[r, hk], kr_ref[r, hk], v_ref[r, hv]
                stb = states[h].astype(bf16)
                st_ref[0, h, ci] = stb
                s = (_dot_nt(qi, ki) * intra_ref[h]).astype(bf16)
                o = _dot(s, vi) + _dot((qi.astype(f32) * qd_ref[h]).astype(bf16), stb)
                states[h] = states[h] * cd_ref[h] + _dot_tn((ki.astype(f32) * kd_ref[h]).astype(bf16), vi)
                mu = jnp.mean(o, axis=-1, keepdims=True)
                xc = o - mu
                var = jnp.mean(xc * xc, axis=-1, keepdims=True)
                oh = xc * lax.rsqrt(var + EPS)
                g = g_ref[r, hv].astype(f32)
                o_ref[r, hv] = o.astype(bf16)
                u_ref[r, hv] = (g * _sig(g) * oh).astype(bf16)
        for h in range(H):
            state_ref[h] = states[h]

    blk = lambda w, c: pl.BlockSpec((sb, w), lambda b, i: (b * ns + i, c))
    return _call(
        body, phase, name="ret_fwd", grid=(B, ns), scratch_shapes=[pltpu.VMEM((H, dk, dv), f32)],
        in_specs=[blk(H * dk, C_RQ // (H * dk)), blk(H * dk, C_RK // (H * dk)), blk(H * dv, C_RV // (H * dv)),
                  blk(H * dv, C_RG // (H * dv)),
                  pl.BlockSpec((sb, dk), lambda b, i: (i, 0)), pl.BlockSpec((sb, dk), lambda b, i: (i, 0)),
                  *_ret_tables_specs()],
        out_specs=[blk(H * dk, 0), blk(H * dk, 0), blk(H * dv, 0), blk(H * dv, 0),
                   pl.BlockSpec((1, H, RET_CHUNKS, dk, dv), lambda b, i: (b, 0, i, 0, 0))],
        out_shape=[jax.ShapeDtypeStruct((T, H * dk), bf16), jax.ShapeDtypeStruct((T, H * dk), bf16),
                   jax.ShapeDtypeStruct((T, H * dv), bf16), jax.ShapeDtypeStruct((T, H * dv), bf16),
                   jax.ShapeDtypeStruct((B, H, nc, dk, dv), bf16)],
        args=(proj, proj, proj, proj, *rope, *decay))


def _bias_rows(rb):
    last = rb[:, N_REL - 1:]
    return jnp.concatenate([
        jnp.broadcast_to(last, (ATT_HEADS, PAD - MAX_REL + 1)),
        jnp.flip(rb[:, :N_REL - 1], axis=1),
        jnp.broadcast_to(rb[:, :1], (ATT_HEADS, KWIN - PAD - CHUNK)),
        jnp.broadcast_to(last, (ATT_HEADS, TOEP - KWIN)),
    ], axis=1)


def _build_bias(t_ref, bias_ref):
    row = lax.broadcasted_iota(jnp.int32, (QBLK, KWIN), 0) // CHUNK
    col = lax.broadcasted_iota(jnp.int32, (QBLK, KWIN), 1) // CHUNK
    delta = BAND_CHUNKS + row - col
    vis = (delta >= 0) & (delta <= BAND_CHUNKS)
    for h in range(ATT_HEADS):
        t = jnp.broadcast_to(t_ref[h:h + 1, :], (QBLK, TOEP))
        rolled = pltpu.roll(t, 0, 1, stride=1, stride_axis=0)
        bias_ref[h] = jnp.where(vis, rolled[:, :KWIN], NEG_INF)


ATT_SCALE = ATT_HEAD_DIM ** -0.5


def _att_probs(qh, kh, bias):
    s = _dot_nt(qh, kh) + bias
    m = jnp.max(s, axis=-1, keepdims=True)
    p = jnp.exp(s - m)
    return p * (1.0 / jnp.sum(p, axis=-1, keepdims=True))


def _first_of_pair():
    return lax.broadcasted_iota(jnp.int32, (1, 2 * ATT_HEAD_DIM), 1) < ATT_HEAD_DIM


def _by_window(i, step):
    sizes = list(range(QBLK, KWIN, QBLK))
    for n, nk in enumerate(sizes):
        pl.when(i == n)(functools.partial(step, nk))
    pl.when(i >= len(sizes))(functools.partial(step, KWIN))


def _att_fwd(proj, trows, B, S, phase=None):
    T = B * S
    nq = S // QBLK
    dh = ATT_HEAD_DIM

    def body(q_ref, k_ref, v_ref, t_ref, o_ref, p_ref, bias_ref):
        i = pl.program_id(1)

        @pl.when((pl.program_id(0) == 0) & (i == 0))
        def _():
            _build_bias(t_ref, bias_ref)

        def step(nk):
            win = pl.ds(pl.multiple_of((i + 1) * QBLK - nk, QBLK), nk)
            kw, vw = k_ref[win, :], v_ref[win, :]
            first = _first_of_pair()
            outs = []
            for p in range(ATT_HEADS // 2):
                ps = slice(2 * p * dh, 2 * (p + 1) * dh)
                q2, k2, v2 = q_ref[:, ps] * ATT_SCALE, kw[:, ps], vw[:, ps]
                both = []
                for e in range(2):
                    qm = jnp.where(first == (e == 0), q2, jnp.zeros_like(q2))
                    pr = _att_probs(qm, k2, bias_ref[2 * p + e, :, KWIN - nk:]).astype(bf16)
                    p_ref[0, 2 * p + e, :, KWIN - nk:] = pr
                    both.append(_dot(pr, v2))
                outs.append(jnp.where(first, both[0], both[1]))
            o_ref[...] = jnp.concatenate(outs, axis=1).astype(bf16)

        _by_window(i, step)

    return _call(
        body, phase, name="att_fwd", grid=(B, nq),
        in_specs=[pl.BlockSpec((QBLK, ATT_W), lambda b, i: (b * nq + i, C_AQ // ATT_W)),
                  pl.BlockSpec((S, ATT_W), lambda b, i: (b, C_AK // ATT_W)),
                  pl.BlockSpec((S, ATT_W), lambda b, i: (b, C_AV // ATT_W)),
                  pl.BlockSpec((ATT_HEADS, TOEP), lambda b, i: (0, 0))],
        out_specs=[pl.BlockSpec((QBLK, ATT_W), lambda b, i: (b * nq + i, 0)),
                   pl.BlockSpec((1, ATT_HEADS, QBLK, KWIN), lambda b, i: (b * nq + i, 0, 0, 0))],
        out_shape=[jax.ShapeDtypeStruct((T, ATT_W), bf16), jax.ShapeDtypeStruct((B * nq, ATT_HEADS, QBLK, KWIN), bf16)],
        scratch_shapes=[pltpu.VMEM((ATT_HEADS, QBLK, KWIN), f32)],
        args=(proj, proj, proj, trows))


def _gl_specs(tm):
    w = 512
    return [pl.BlockSpec((tm, w), functools.partial(lambda i, j: (i, C_GL // 512 + j), j=j)) for j in range(4)]


def _gates(gl_refs, bg_ref):
    gl = jnp.concatenate([r[...] for r in gl_refs], axis=1).astype(f32) + bg_ref[...]
    g = _sig(gl)
    return g[:, :D_MODEL], g[:, D_MODEL:]


def _mix_fwd(x2, proj, u, ao, b_gate, w_ro, w_ao, w_out, phase=None):
    T, D = x2.shape
    tm = _tile(T, ROW_TILE, 8)

    def body(x_ref, u_ref, ao_ref, g0, g1, g2, g3, bg_ref, wro_ref, wao_ref, wo_ref, h1_ref, yr_ref, ya_ref):
        yr = _dot(u_ref[...], wro_ref[...])
        ao = ao_ref[...]
        ya = jnp.concatenate([_dot(ao, wao_ref[k]) for k in range(N_CHIPS)], axis=1)
        gr, ga = _gates((g0, g1, g2, g3), bg_ref)
        mix = gr * yr + ga * ya
        h1_ref[...] = x_ref[...] + _dot(mix.astype(bf16), wo_ref[...])
        yr_ref[...] = yr.astype(bf16)
        ya_ref[...] = ya.astype(bf16)

    full = lambda a: pl.BlockSpec(a.shape, lambda i: (0,) * a.ndim)
    row = lambda n: pl.BlockSpec((tm, n), lambda i: (i, 0))
    return _call(
        body, phase, name="mix_fwd", grid=(T // tm,), scratch_shapes=[],
        in_specs=[row(D), row(D), row(ATT_W), *_gl_specs(tm), full(b_gate), full(w_ro), full(w_ao), full(w_out)],
        out_specs=[row(D), row(D), row(D)],
        out_shape=[jax.ShapeDtypeStruct((T, D), f32), jax.ShapeDtypeStruct((T, D), bf16),
                   jax.ShapeDtypeStruct((T, D), bf16)],
        args=(x2, u, ao, proj, proj, proj, proj, b_gate, w_ro, w_ao, w_out))


def _ffn_fwd(h1, g_ffn, wg, wu, wd, g_fin, target):
    T, D = h1.shape
    nf, tf, _ = wg.shape
    tm = _tile(T, ROW_TILE, 8)

    def body(h1_ref, g_ref, wg_ref, wu_ref, wd_ref, gf_ref, tg_ref, hn_ref, a_ref, b_ref, f_ref, dh2_ref, dh2b_ref,
             part_ref):
        h1v = h1_ref[...]
        r = lax.rsqrt(jnp.mean(h1v * h1v, axis=-1, keepdims=True) + EPS)
        hn = (h1v * r * g_ref[...]).astype(bf16)
        hn_ref[...] = hn
        h2 = h1v
        for k in range(nf):
            a = _dot_nt(hn, wg_ref[k])
            b = _dot_nt(hn, wu_ref[k])
            f = ((a * _sig(a)) * b).astype(bf16)
            a_ref[k] = a.astype(bf16)
            b_ref[k] = b.astype(bf16)
            f_ref[k] = f
            h2 = h2 + _dot(f, wd_ref[k])
        r = lax.rsqrt(jnp.mean(h2 * h2, axis=-1, keepdims=True) + EPS)
        n = h2 * r
        gf = gf_ref[...]
        e = n * gf - tg_ref[...]
        dy = e * (1.0 / D)
        dn = dy * gf
        dh2 = r * (dn - n * jnp.mean(dn * n, axis=-1, keepdims=True))
        dh2_ref[...] = dh2
        dh2b_ref[...] = dh2.astype(bf16)
        _add_rows(part_ref, pl.program_id(0) == 0, jnp.sum(dy * n, axis=0, keepdims=True),
                  (0.5 / D) * jnp.sum(e * e, axis=0, keepdims=True))

    row = lambda n: pl.BlockSpec((tm, n), lambda i: (i, 0))
    vec = pl.BlockSpec((1, D), lambda i: (0, 0))
    col = pl.BlockSpec((nf, tm, tf), lambda i: (0, i, 0))
    held = lambda w: pl.BlockSpec(w.shape, lambda i: (0, 0, 0), pipeline_mode=pl.Buffered(1))
    act = jax.ShapeDtypeStruct((nf, T, tf), bf16)
    return pl.pallas_call(
        body, name="ffn_fwd", grid=(T // tm,),
        in_specs=[row(D), vec, held(wg), held(wu), held(wd), vec, row(D)],
        out_specs=[row(D), col, col, col, row(D), row(D), pl.BlockSpec((8, D), lambda i: (0, 0))],
        out_shape=[jax.ShapeDtypeStruct((T, D), bf16), act, act, act, jax.ShapeDtypeStruct((T, D), f32),
                   jax.ShapeDtypeStruct((T, D), bf16), jax.ShapeDtypeStruct((8, D), f32)],
        compiler_params=_params(("arbitrary",)),
    )(h1, g_ffn, wg, wu, wd, g_fin, target)


def _ffn_bwd(dh2, h1, g_ffn, a, b, wg, wu, wd):
    T, D = h1.shape
    nf, tf, _ = wg.shape
    tm = _tile(T, ROW_TILE // 2, 8)

    def body(dh2_ref, h1_ref, g_ref, a_ref, b_ref, wg_ref, wu_ref, wd_ref, da_ref, db_ref, dh1_ref, dh1b_ref, part_ref):
        dh2v = dh2_ref[...]
        dh2b = dh2v.astype(bf16)
        dhn = jnp.zeros((tm, D), f32)
        for k in range(nf):
            df = _dot_nt(dh2b, wd_ref[k])
            av = a_ref[k].astype(f32)
            sg = _sig(av)
            db = (df * (av * sg)).astype(bf16)
            da = (df * b_ref[k].astype(f32) * (sg * (1.0 + av * (1.0 - sg)))).astype(bf16)
            da_ref[k] = da
            db_ref[k] = db
            dhn = dhn + _dot(da, wg_ref[k]) + _dot(db, wu_ref[k])
        h = h1_ref[...]
        r = lax.rsqrt(jnp.mean(h * h, axis=-1, keepdims=True) + EPS)
        n = h * r
        dn = dhn * g_ref[...]
        dh1 = dh2v + r * (dn - n * jnp.mean(dn * n, axis=-1, keepdims=True))
        dh1_ref[...] = dh1
        dh1b_ref[...] = dh1.astype(bf16)
        _add_rows(part_ref, pl.program_id(0) == 0, jnp.sum(dhn * n, axis=0, keepdims=True))

    row = lambda n: pl.BlockSpec((tm, n), lambda i: (i, 0))
    col = pl.BlockSpec((nf, tm, tf), lambda i: (0, i, 0))
    held = lambda w: pl.BlockSpec(w.shape, lambda i: (0, 0, 0), pipeline_mode=pl.Buffered(1))
    act = jax.ShapeDtypeStruct((nf, T, tf), bf16)
    return pl.pallas_call(
        body, name="ffn_bwd", grid=(T // tm,),
        in_specs=[row(D), row(D), pl.BlockSpec((1, D), lambda i: (0, 0)), col, col, held(wg), held(wu), held(wd)],
        out_specs=[col, col, row(D), row(D), pl.BlockSpec((8, D), lambda i: (0, 0))],
        out_shape=[act, act, jax.ShapeDtypeStruct((T, D), f32), jax.ShapeDtypeStruct((T, D), bf16),
                   jax.ShapeDtypeStruct((8, D), f32)],
        compiler_params=_params(("arbitrary",)),
    )(dh2, h1, g_ffn, a, b, wg, wu, wd)


def _mix_bwd(dh1, proj, yr, ya, b_gate, w_ro, w_ao, w_out, phase=None):
    T, D = dh1.shape
    tm = _tile(T, ROW_TILE, 8)

    def body(dh1_ref, g0, g1, g2, g3, bg_ref, yr_ref, ya_ref, wro_ref, wao_ref, wo_ref,
             du_ref, dao_ref, dgl_ref, mix_ref, dyr_ref, dya_ref, part_ref):
        dmix = _dot_nt(dh1_ref[...], wo_ref[...])
        gr, ga = _gates((g0, g1, g2, g3), bg_ref)
        yr = yr_ref[...].astype(f32)
        ya = ya_ref[...].astype(f32)
        dyr = (dmix * gr).astype(bf16)
        dya = (dmix * ga).astype(bf16)
        dgl = jnp.concatenate([dmix * yr * gr * (1.0 - gr), dmix * ya * ga * (1.0 - ga)], axis=1)
        du_ref[...] = _dot_nt(dyr, wro_ref[...]).astype(bf16)
        ns = wao_ref.shape[2]
        dao = _dot_nt(dya[:, :ns], wao_ref[0])
        for k in range(1, N_CHIPS):
            dao = dao + _dot_nt(dya[:, k * ns:(k + 1) * ns], wao_ref[k])
        dao_ref[...] = dao.astype(bf16)
        dgl_ref[...] = dgl.astype(bf16)
        mix_ref[...] = (gr * yr + ga * ya).astype(bf16)
        dyr_ref[...] = dyr
        dya_ref[...] = dya
        _add_rows(part_ref, pl.program_id(0) == 0, jnp.sum(dgl, axis=0, keepdims=True))

    full = lambda a: pl.BlockSpec(a.shape, lambda i: (0,) * a.ndim)
    row = lambda n: pl.BlockSpec((tm, n), lambda i: (i, 0))
    return _call(
        body, phase, name="mix_bwd", grid=(T // tm,), scratch_shapes=[],
        in_specs=[row(D), *_gl_specs(tm), full(b_gate), row(D), row(D), full(w_ro), full(w_ao), full(w_out)],
        out_specs=[row(D), row(ATT_W), row(2 * D), row(D), row(D), row(D), pl.BlockSpec((8, 2 * D), lambda i: (0, 0))],
        out_shape=[jax.ShapeDtypeStruct((T, D), bf16), jax.ShapeDtypeStruct((T, ATT_W), bf16),
                   jax.ShapeDtypeStruct((T, 2 * D), bf16), jax.ShapeDtypeStruct((T, D), bf16),
                   jax.ShapeDtypeStruct((T, D), bf16), jax.ShapeDtypeStruct((T, D), bf16),
                   jax.ShapeDtypeStruct((8, 2 * D), f32)],
        args=(dh1, proj, proj, proj, proj, b_gate, yr, ya, w_ro, w_ao, w_out))


def _ret_bwd(proj, qr, kr, o, states, du, dgl, B, S, rope, decay, phase=None):
    T = B * S
    H, dk, dv = RET_HEADS, RET_KEY_DIM, RET_VAL_DIM
    sb = RET_CHUNKS * CHUNK
    ns = S // sb

    def body(qr_ref, kr_ref, v_ref, g_ref, o_ref, st_ref, du_ref, dgl_ref, cos_ref, sin_ref, intra_ref, qd_ref, kd_ref,
             cd_ref, dp_ref, dstate_ref):
        dq_ref, dk_ref = dp_ref.at[:, pl.ds(C_RQ, H * dk)], dp_ref.at[:, pl.ds(C_RK, H * dk)]
        dv_ref, dg_ref = dp_ref.at[:, pl.ds(C_RV, H * dv)], dp_ref.at[:, pl.ds(C_RG, H * dv)]
        dp_ref[:, C_GL:] = dgl_ref[...]

        @pl.when(pl.program_id(1) == 0)
        def _():
            dstate_ref[...] = jnp.zeros_like(dstate_ref)

        cos, snb = cos_ref[...], -sin_ref[...]
        dstates = [dstate_ref[h] for h in range(H)]
        for ci in reversed(range(RET_CHUNKS)):
            r = slice(ci * CHUNK, (ci + 1) * CHUNK)
            for h in range(H):
                hk, hv = slice(h * dk, (h + 1) * dk), slice(h * dv, (h + 1) * dv)
                intra, qd, kd = intra_ref[h], qd_ref[h], kd_ref[h]
                qi, ki, vi = qr_ref[r, hk], kr_ref[r, hk], v_ref[r, hv]
                si = st_ref[0, h, ci]
                o = o_ref[r, hv].astype(f32)
                mu = jnp.mean(o, axis=-1, keepdims=True)
                xc = o - mu
                rstd = lax.rsqrt(jnp.mean(xc * xc, axis=-1, keepdims=True) + EPS)
                oh = xc * rstd
                g = g_ref[r, hv].astype(f32)
                sg = _sig(g)
                dui = du_ref[r, hv].astype(f32)
                dg_ref[r, hv] = (dui * oh * (sg * (1.0 + g * (1.0 - sg)))).astype(bf16)
                doh = dui * (g * sg)
                do = rstd * (doh - jnp.mean(doh, axis=-1, keepdims=True)
                             - oh * jnp.mean(doh * oh, axis=-1, keepdims=True))
                dob = do.astype(bf16)
                p = (_dot_nt(qi, ki) * intra).astype(bf16)
                dsb = dstates[h].astype(bf16)
                kt = (ki.astype(f32) * kd).astype(bf16)
                qt = (qi.astype(f32) * qd).astype(bf16)
                dv_ref[r, hv] = (_dot_tn(p, dob) + _dot(kt, dsb)).astype(bf16)
                da = (_dot_nt(dob, vi) * intra).astype(bf16)
                dq = _dot(da, ki) + _dot_nt(dob, si) * qd
                dkk = (_dot_tn(da, qi) + _dot_nt(vi, dsb) * kd) * K_SCALE
                dq_ref[r, hk] = _rotate(dq, cos[r], snb[r]).astype(bf16)
                dk_ref[r, hk] = _rotate(dkk, cos[r], snb[r]).astype(bf16)
                dstates[h] = dstates[h] * cd_ref[h] + _dot_tn(qt, dob)
        for h in range(H):
            dstate_ref[h] = dstates[h]

    blk = lambda w, c: pl.BlockSpec((sb, w), lambda b, i: (b * ns + ns - 1 - i, c))
    return _call(
        body, phase, name="ret_bwd", grid=(B, ns),
        in_specs=[blk(H * dk, 0), blk(H * dk, 0), blk(H * dv, C_RV // (H * dv)), blk(H * dv, C_RG // (H * dv)),
                  blk(H * dv, 0),
                  pl.BlockSpec((1, H, RET_CHUNKS, dk, dv), lambda b, i: (b, 0, ns - 1 - i, 0, 0)),
                  blk(H * dv, 0), blk(N_IN - C_GL, 0),
                  pl.BlockSpec((sb, dk), lambda b, i: (ns - 1 - i, 0)), pl.BlockSpec((sb, dk), lambda b, i: (ns - 1 - i, 0)),
                  *_ret_tables_specs()],
        out_specs=[blk(N_IN, 0)], out_shape=[jax.ShapeDtypeStruct((T, N_IN), bf16)],
        scratch_shapes=[pltpu.VMEM((H, dk, dv), f32)],
        args=(qr, kr, proj, proj, o, states, du, dgl, *rope, *decay))


def _att_bwd(proj, dao, probs, dproj, B, S, phase=None):
    T = B * S
    nq = S // QBLK
    dh = ATT_HEAD_DIM

    def body(q_ref, k_ref, v_ref, do_ref, p_ref, _, dp_ref, vec_ref, dbias_ref, dka_ref, dva_ref):
        b, i = pl.program_id(0), pl.program_id(1)

        @pl.when((b == 0) & (i == 0))
        def _():
            dbias_ref[...] = jnp.zeros_like(dbias_ref)

        @pl.when(i == 0)
        def _():
            dka_ref[...] = jnp.zeros_like(dka_ref)
            dva_ref[...] = jnp.zeros_like(dva_ref)

        def step(nk):
            win = pl.ds(pl.multiple_of((i + 1) * QBLK - nk, QBLK), nk)
            kw, vw = k_ref[win, :], v_ref[win, :]
            first = _first_of_pair()
            first_rows = lax.broadcasted_iota(jnp.int32, (2 * dh, 1), 0) < dh
            dqs, dks, dvs = [], [], []
            for p in range(ATT_HEADS // 2):
                ps = slice(2 * p * dh, 2 * (p + 1) * dh)
                q2, k2, v2, do2 = q_ref[:, ps] * ATT_SCALE, kw[:, ps], vw[:, ps], do_ref[:, ps]
                dq2, dk2, dv2 = [], [], []
                for e in range(2):
                    h = 2 * p + e
                    prb = p_ref[0, h, :, KWIN - nk:]
                    pr = prb.astype(f32)
                    dp = _dot_nt(jnp.where(first == (e == 0), do2, jnp.zeros_like(do2)), v2)
                    ds = pr * (dp - jnp.sum(pr * dp, axis=-1, keepdims=True))
                    dbias_ref[h, :, KWIN - nk:] += ds
                    dsb = ds.astype(bf16)
                    dq2.append(_dot(dsb, k2) * ATT_SCALE)
                    dk2.append(_dot_tn(q2, dsb))
                    dv2.append(_dot_tn(do2, prb))
                dqs.append(jnp.where(first, dq2[0], dq2[1]))
                dks.append(jnp.where(first_rows, dk2[0], dk2[1]))
                dvs.append(jnp.where(first_rows, dv2[0], dv2[1]))
            dp_ref[pl.ds(pl.multiple_of(i * QBLK, QBLK), QBLK), :ATT_W] = jnp.concatenate(dqs, axis=1).astype(bf16)
            dka_ref[:, win] += jnp.concatenate(dks, axis=0)
            dva_ref[:, win] += jnp.concatenate(dvs, axis=0)

        _by_window(i, step)

        @pl.when(i == nq - 1)
        def _():
            dp_ref[:, ATT_W:2 * ATT_W] = dka_ref[...].T.astype(bf16)
            dp_ref[:, 2 * ATT_W:] = dva_ref[...].T.astype(bf16)

        @pl.when((b == B - 1) & (i == nq - 1))
        def _():
            rr = lax.broadcasted_iota(jnp.int32, (QBLK, QBLK), 0)
            cc = lax.broadcasted_iota(jnp.int32, (QBLK, QBLK), 1)
            flip = jnp.where(rr + cc == QBLK - 1, 1.0, 0.0).astype(bf16)
            for h in range(ATT_HEADS):
                d = dbias_ref[h]
                hi = d.astype(bf16)
                lo = (d - hi.astype(f32)).astype(bf16)
                rev = _dot(flip, hi) + _dot(flip, lo)
                wide = jnp.concatenate([rev, jnp.zeros((QBLK, TOEP - KWIN), f32)], axis=1)
                rolled = pltpu.roll(wide, 0, 1, stride=1, stride_axis=0)
                vec_ref[h:h + 1, :] = jnp.sum(rolled, axis=0, keepdims=True)

    qspec = lambda c: pl.BlockSpec((QBLK, ATT_W), lambda b, i: (b * nq + i, c))
    kspec = lambda c: pl.BlockSpec((S, ATT_W), lambda b, i: (b, c))
    return _call(
        body, phase, name="att_bwd", grid=(B, nq), aliases={5: 0},
        in_specs=[qspec(C_AQ // ATT_W), kspec(C_AK // ATT_W), kspec(C_AV // ATT_W), qspec(0),
                  pl.BlockSpec((1, ATT_HEADS, QBLK, KWIN), lambda b, i: (b * nq + i, 0, 0, 0)),
                  pl.BlockSpec(memory_space=pl.ANY)],
        out_specs=[pl.BlockSpec((S, 3 * ATT_W), lambda b, i: (b, C_AQ // (3 * ATT_W))),
                   pl.BlockSpec((ATT_HEADS, TOEP), lambda b, i: (0, 0))],
        out_shape=[jax.ShapeDtypeStruct((T, N_IN), bf16), jax.ShapeDtypeStruct((ATT_HEADS, TOEP), f32)],
        scratch_shapes=[pltpu.VMEM((ATT_HEADS, QBLK, KWIN), f32), pltpu.VMEM((ATT_W, S), f32),
                        pltpu.VMEM((ATT_W, S), f32)],
        args=(proj, proj, proj, dao, probs, dproj))


def _in_proj_bwd(dproj, w_in, x2, gamma, dh1, phase=None):
    T, D = x2.shape
    nk, _, tk = w_in.shape
    tm = _tile(T, BIG_ROW_TILE, 8)

    def body(dp_ref, w_ref, x_ref, g_ref, dh1_ref, dx_ref, part_ref, acc_ref):
        j = pl.program_id(1)

        @pl.when(j == 0)
        def _():
            acc_ref[...] = jnp.zeros_like(acc_ref)

        acc_ref[...] += _dot_nt(dp_ref[...], w_ref[0])

        @pl.when(j == nk - 1)
        def _():
            x = x_ref[...]
            r = lax.rsqrt(jnp.mean(x * x, axis=-1, keepdims=True) + EPS)
            n = x * r
            dxn = acc_ref[...]
            dn = dxn * g_ref[...]
            dx_ref[...] = dh1_ref[...] + r * (dn - n * jnp.mean(dn * n, axis=-1, keepdims=True))
            _add_rows(part_ref, pl.program_id(0) == 0, jnp.sum(dxn * n, axis=0, keepdims=True))

    row = lambda n: pl.BlockSpec((tm, n), lambda i, j: (i, 0))
    return _call(
        body, phase, name="in_proj_bwd", grid=(T // tm, nk),
        in_specs=[pl.BlockSpec((tm, tk), lambda i, j: (i, j)), pl.BlockSpec((1, D, tk), lambda i, j: (j, 0, 0)), row(D),
                  pl.BlockSpec((1, D), lambda i, j: (0, 0)), row(D)],
        out_specs=[row(D), pl.BlockSpec((8, D), lambda i, j: (0, 0))],
        out_shape=[jax.ShapeDtypeStruct((T, D), f32), jax.ShapeDtypeStruct((8, D), f32)],
        scratch_shapes=[pltpu.VMEM((tm, D), f32)],
        args=(dproj, w_in, x2, gamma, dh1))


def _wgrad(a, b, shard_axis, name, phase=None):
    def spec(arr, sharded, tt):
        if arr.ndim == 3:
            return arr.shape[2], pl.BlockSpec((1, tt, arr.shape[2]), lambda s, t: (s, t, 0))
        if sharded:
            w = arr.shape[1] // N_CHIPS
            return w, pl.BlockSpec((tt, w), lambda s, t: (t, s))
        return arr.shape[1], pl.BlockSpec((tt, arr.shape[1]), lambda s, t: (t, 0))

    T = a.shape[-2]
    whole = a.ndim == 2 and b.ndim == 2 and a.shape[1] * b.shape[1] * 4 <= WGRAD_ACC_BYTES
    width = lambda arr, sharded: arr.shape[-1] // (1 if whole or arr.ndim == 3 or not sharded else N_CHIPS)
    wa, wb_ = width(a, shard_axis == 0), width(b, shard_axis == 1)
    fixed = wa * wb_ * (4 + 2 * 2)
    tt = T // 4 if whole else T
    while tt > 256 and 2 * tt * (wa * a.dtype.itemsize + wb_ * b.dtype.itemsize) + fixed > WGRAD_VMEM_BYTES:
        tt //= 2
    nt = T // tt
    if whole:
        K, N = a.shape[1], b.shape[1]
        a_spec, b_spec = pl.BlockSpec((tt, K), lambda s, t: (t, 0)), pl.BlockSpec((tt, N), lambda s, t: (t, 0))
        out_block = (N_CHIPS, K // N_CHIPS, N) if shard_axis == 0 else (N_CHIPS, K, N // N_CHIPS)
        out_spec = pl.BlockSpec(out_block, lambda s, t: (0, 0, 0))
    else:
        K, a_spec = spec(a, shard_axis == 0, tt)
        N, b_spec = spec(b, shard_axis == 1, tt)
        out_block = (N_CHIPS, K, N)
        out_spec = pl.BlockSpec((1, K, N), lambda s, t: (s, 0, 0))

    def body(a_ref, b_ref, o_ref, acc_ref):
        t = pl.program_id(1)

        @pl.when(t == 0)
        def _():
            acc_ref[...] = jnp.zeros_like(acc_ref)

        av = a_ref[0] if a.ndim == 3 else a_ref[...]
        bv = b_ref[0] if b.ndim == 3 else b_ref[...]
        acc_ref[...] += _dot_tn(av.astype(bf16), bv.astype(bf16))

        @pl.when(t == nt - 1)
        def _():
            if not whole:
                o_ref[0] = acc_ref[...].astype(bf16)
            else:
                _, kk, nn = out_block
                for s in range(N_CHIPS):
                    o_ref[s] = (acc_ref[s * kk:(s + 1) * kk, :] if shard_axis == 0
                                else acc_ref[:, s * nn:(s + 1) * nn]).astype(bf16)

    (grad,), carried = _call(
        body, phase, name=name, grid=(1 if whole else N_CHIPS, nt), in_specs=[a_spec, b_spec], out_specs=[out_spec],
        out_shape=[jax.ShapeDtypeStruct(out_block, bf16)], scratch_shapes=[pltpu.VMEM((K, N), f32)], args=(a, b))
    return grad, carried


def _adamw_sum(place, groups, name):
    n = len(groups)
    R, C = groups[0][0].shape
    half = R // 2
    tr = _tile(half, max(16, (1 << 18) // C // 16 * 16), 16)
    nr = half // tr

    def body(p_ref, *refs):
        for a in range(n):
            w_ref, m_ref, v_ref, part_ref, fc_ref, fs_ref = refs[6 * a:6 * a + 6]
            g_ref, d_ref, mo_ref, vo_ref = refs[6 * n + 4 * a:6 * n + 4 * a + 4]
            up = lambda x: x.astype(f32)
            mine = ((up(part_ref[0]) + up(fc_ref[0])) + up(fc_ref[1])) + up(fc_ref[2])
            sibs = ((up(fs_ref[0]) + up(fs_ref[1])) + up(fs_ref[2])) + up(fs_ref[3])
            g_ = jnp.where(pl.program_id(0) == p_ref[0], mine, sibs)
            m_ = ADAM_B1 * m_ref[...] + (1.0 - ADAM_B1) * g_
            v_ = ADAM_B2 * v_ref[...] + (1.0 - ADAM_B2) * (g_ * g_)
            m_hat = m_ / (1.0 - ADAM_B1 ** ADAM_STEP)
            v_hat = v_ / (1.0 - ADAM_B2 ** ADAM_STEP)
            g_ref[...] = g_
            d_ref[...] = -ADAM_LR * (m_hat / (jnp.sqrt(v_hat) + ADAM_EPS) + ADAM_WD * w_ref[...])
            mo_ref[...] = m_
            vo_ref[...] = v_

    spec = pl.BlockSpec((tr, C), lambda h, r, p: (h * nr + r, 0))
    one = [spec, spec, spec, pl.BlockSpec((1, tr, C), lambda h, r, p: (p[1], jnp.where(h == p[0], r, 0), 0)),
           pl.BlockSpec((3, tr, C), lambda h, r, p: (0, jnp.where(h == p[0], r, 0), 0)),
           pl.BlockSpec((4, tr, C), lambda h, r, p: (0, jnp.where(h == p[0], 0, r), 0))]
    res = pl.pallas_call(
        body, name=name,
        grid_spec=pltpu.PrefetchScalarGridSpec(num_scalar_prefetch=1, grid=(2, nr), in_specs=one * n,
                                               out_specs=[spec] * (4 * n)),
        out_shape=[jax.ShapeDtypeStruct((R, C), f32)] * (4 * n),
        compiler_params=_params(("parallel", "parallel")),
    )(place, *[x for g in groups for x in g])
    return [tuple(res[4 * a:4 * a + 4]) for a in range(n)]


def _adamw(w, g, m, v, name):
    R, C = w.shape
    tr = _tile(R, max(8, (1 << 18) // C // 8 * 8), 8)

    def body(w_ref, g_ref, m_ref, v_ref, d_ref, mo_ref, vo_ref):
        g_ = g_ref[...]
        m_ = ADAM_B1 * m_ref[...] + (1.0 - ADAM_B1) * g_
        v_ = ADAM_B2 * v_ref[...] + (1.0 - ADAM_B2) * (g_ * g_)
        m_hat = m_ / (1.0 - ADAM_B1 ** ADAM_STEP)
        v_hat = v_ / (1.0 - ADAM_B2 ** ADAM_STEP)
        d_ref[...] = -ADAM_LR * (m_hat / (jnp.sqrt(v_hat) + ADAM_EPS) + ADAM_WD * w_ref[...])
        mo_ref[...] = m_
        vo_ref[...] = v_

    spec = pl.BlockSpec((tr, C), lambda i: (i, 0))
    return pl.pallas_call(
        body, name=name, grid=(R // tr,), in_specs=[spec] * 4, out_specs=[spec] * 3,
        out_shape=[jax.ShapeDtypeStruct((R, C), f32)] * 3,
        compiler_params=_params(("parallel",)),
    )(w, g, m, v)


def _place():
    return lax.axis_index("x"), lax.axis_index("y"), lax.axis_index("c")


def _other_chips(x, y):
    chips = [(1 - x, y), (x, 1 - y), (1 - x, 1 - y)]
    return chips, [2 * cx + cy for cx, cy in chips]


def _spread_phase(blk):
    def peers():
        x, y, c = _place()
        return [tuple(1 - p if (k >> s) & 1 else p for p, s in ((x, 2), (y, 1), (c, 0))) for k in range(1, N_DEV)]

    def copies(pin, out):
        x, y, c = _place()
        mine = out[0].at[4 * x + 2 * y + c]
        return [(mine, mine, peer) for peer in peers()]

    stack = jnp.broadcast_to(blk, (N_DEV,) + blk.shape)
    return _Phase([stack], [jax.ShapeDtypeStruct(stack.shape, stack.dtype)], {0: 0}, N_DEV - 1, copies,
                  lambda pin, out: [out[0].at[4 * px + 2 * py + pc] for px, py, pc in peers()])


def _sum_slots(stack, name):
    def body(s_ref, o_ref):
        tot = s_ref[0]
        for d in range(1, stack.shape[0]):
            tot = tot + s_ref[d]
        o_ref[...] = tot

    vm = pl.BlockSpec(memory_space=pltpu.VMEM)
    return pl.pallas_call(body, name=name, in_specs=[vm], out_specs=vm,
                          out_shape=jax.ShapeDtypeStruct(stack.shape[1:], stack.dtype))(stack)


def _cast_shards(place, ws, name):
    n = len(ws)
    R, C = ws[0].shape
    tr = _tile(R, max(16, (1 << 19) // C // 16 * 16), 16)

    def body(p_ref, *refs):
        for a in range(n):
            refs[n + a][0] = refs[a][...].astype(bf16)

    return pl.pallas_call(
        body, name=name,
        grid_spec=pltpu.PrefetchScalarGridSpec(
            num_scalar_prefetch=1, grid=(R // tr,),
            in_specs=[pl.BlockSpec((tr, C), lambda r, p: (r, 0))] * n,
            out_specs=[pl.BlockSpec((1, tr, C), lambda r, p: (p[1], r, 0))] * n),
        out_shape=[jax.ShapeDtypeStruct((N_CHIPS, R, C), bf16)] * n,
        compiler_params=_params(("parallel",)),
    )(place, *ws)


class _Phase:
    def __init__(self, arrays, out_shapes, aliases, n_copies, copies, arrivals, own_starts=(), own_waits=()):
        self.arrays, self.out_shapes, self.aliases = list(arrays), list(out_shapes), dict(aliases)
        self.n_copies, self.copies, self.arrivals = n_copies, copies, arrivals
        self.own_starts, self.own_waits = tuple(own_starts), tuple(own_waits)

    def sems(self):
        return [pltpu.SemaphoreType.DMA((self.n_copies,)), pltpu.SemaphoreType.DMA((self.n_copies,))]

    def _descriptors(self, pin, pout, send_sems, recv_sems):
        return [pltpu.make_async_remote_copy(src_ref=s, dst_ref=d, send_sem=send_sems.at[i], recv_sem=recv_sems.at[i],
                                             device_id=to, device_id_type=MESH)
                for i, (s, d, to) in enumerate(self.copies(pin, pout))]

    def _arrival(self, i, pin, pout, send_sems, recv_sems):
        dst = self.arrivals(pin, pout)[i]
        return pltpu.make_async_remote_copy(src_ref=dst, dst_ref=dst, send_sem=send_sems.at[i], recv_sem=recv_sems.at[i],
                                            device_id=_place(), device_id_type=MESH)

    def start(self, pin, pout, send_sems, recv_sems):
        for i, cp in enumerate(self._descriptors(pin, pout, send_sems, recv_sems)):
            if i not in self.own_starts:
                cp.start()

    def begin(self, i, pin, pout, send_sems, recv_sems):
        self._descriptors(pin, pout, send_sems, recv_sems)[i].start()

    def arrived(self, i, pin, pout, send_sems, recv_sems):
        self._arrival(i, pin, pout, send_sems, recv_sems).wait_recv()

    def finish(self, pin, pout, send_sems, recv_sems):
        for i in range(self.n_copies):
            if i not in self.own_waits:
                self._arrival(i, pin, pout, send_sems, recv_sems).wait_recv()
        for cp in self._descriptors(pin, pout, send_sems, recv_sems):
            cp.wait_send()


def _join(phases):
    if len(phases) == 1:
        return phases[0]
    ai = np.cumsum([0] + [len(p.arrays) for p in phases])
    oi = np.cumsum([0] + [len(p.out_shapes) for p in phases])

    def each(fn_name, pin, pout):
        return [item for k, p in enumerate(phases)
                for item in getattr(p, fn_name)(pin[ai[k]:ai[k + 1]], pout[oi[k]:oi[k + 1]])]

    aliases = {int(ai[k]) + i: int(oi[k]) + j for k, p in enumerate(phases) for i, j in p.aliases.items()}
    ci = np.cumsum([0] + [p.n_copies for p in phases])
    shifted = lambda attr: [int(ci[k]) + i for k, p in enumerate(phases) for i in getattr(p, attr)]
    return _Phase([a for p in phases for a in p.arrays], [s for p in phases for s in p.out_shapes], aliases,
                  int(ci[-1]), functools.partial(each, "copies"), functools.partial(each, "arrivals"),
                  shifted("own_starts"), shifted("own_waits"))


def _call(body, phase, *, name, grid, in_specs, out_specs, out_shape, scratch_shapes, args, prefetch=(), expose=False,
          aliases=None):
    seq = _params(("arbitrary",) * len(grid))
    np_ = len(prefetch)
    own = {np_ + i: j for i, j in (aliases or {}).items()}
    if phase is None:
        spec = pltpu.PrefetchScalarGridSpec(num_scalar_prefetch=np_, grid=grid, in_specs=in_specs, out_specs=out_specs,
                                            scratch_shapes=scratch_shapes)
        res = pl.pallas_call(body, name=name, grid_spec=spec, out_shape=out_shape, input_output_aliases=own,
                             compiler_params=seq)(*prefetch, *args)
        return list(res), []
    ni, no, ns = len(in_specs), len(out_specs), len(scratch_shapes)
    pi, po = len(phase.arrays), len(phase.out_shapes)

    def hosted(*refs):
        cut = np.cumsum([np_, ni, pi, no, po, ns])
        pre, ins, pin, outs, pout, scr, sems = (refs[a:b] for a, b in zip([0, *cut], [*cut, len(refs)]))
        ids = [pl.program_id(d) for d in range(len(grid))]
        first = functools.reduce(lambda p, q: p & q, [i == 0 for i in ids])
        last = functools.reduce(lambda p, q: p & q, [i == g - 1 for i, g in zip(ids, grid)])
        pl.when(first)(lambda: phase.start(pin, pout, *sems))
        body(*pre, *ins, *outs, *scr, **({"carried": (pin, pout, sems)} if expose else {}))
        pl.when(last)(lambda: phase.finish(pin, pout, *sems))

    anyspace = pl.BlockSpec(memory_space=pl.ANY)
    spec = pltpu.PrefetchScalarGridSpec(
        num_scalar_prefetch=np_, grid=grid, in_specs=list(in_specs) + [anyspace] * pi,
        out_specs=list(out_specs) + [anyspace] * po, scratch_shapes=list(scratch_shapes) + phase.sems())
    res = pl.pallas_call(
        hosted, name=name, grid_spec=spec, out_shape=list(out_shape) + phase.out_shapes,
        input_output_aliases={**own, **{np_ + ni + i: no + j for i, j in phase.aliases.items()}}, compiler_params=seq,
    )(*prefetch, *args, *phase.arrays)
    return list(res[:no]), list(res[no:])


def _run_phases(name, phases):
    first = phases[0]
    pi, po = len(first.arrays), len(first.out_shapes)

    def body(*refs):
        pin, pout, sems = refs[:pi], refs[pi:pi + po], refs[pi + po:]
        for n, ph in enumerate(phases):
            ph.start(pin, pout, *sems[2 * n:2 * n + 2])
            ph.finish(pin, pout, *sems[2 * n:2 * n + 2])

    anyspace = pl.BlockSpec(memory_space=pl.ANY)
    return list(pl.pallas_call(
        body, name=name, in_specs=[anyspace] * pi, out_specs=[anyspace] * po, out_shape=first.out_shapes,
        input_output_aliases=first.aliases, scratch_shapes=[s for ph in phases for s in ph.sems()],
    )(*first.arrays))


def _half_rows(buf, c):
    half = buf.shape[1] // 2
    return pl.ds(c * half, half), pl.ds((1 - c) * half, half)


def _gather_phase(bufs, over_ici):
    n = len(bufs)
    shapes = [jax.ShapeDtypeStruct(b.shape, b.dtype) for b in bufs]

    def landed(out, which):
        x, y, c = _place()
        _, ks = _other_chips(x, y)
        return [out[a].at[ks[j], _half_rows(bufs[a], c)[which]] for a in range(n) for j in range(3)]

    def ici(pin, out):
        x, y, c = _place()
        chips, _ = _other_chips(x, y)
        mine = [out[a].at[2 * x + y, _half_rows(bufs[a], c)[0]] for a in range(n)]
        return [(mine[a], mine[a], (*chips[j], c)) for a in range(n) for j in range(3)]

    def d2d(pin, out):
        x, y, c = _place()
        return [(dst, dst, (x, y, 1 - c)) for dst in landed(out, 0)]

    if over_ici:
        return _Phase(bufs, shapes, {a: a for a in range(n)}, 3 * n, ici, lambda pin, out: landed(out, 0))
    return _Phase(bufs, shapes, {a: a for a in range(n)}, 3 * n, d2d, lambda pin, out: landed(out, 1))


def _feed_phase(buf):
    def chips():
        x, y, _ = _place()
        return [(x if f < 2 else 1 - x, y if f % 2 == 0 else 1 - y) for f in (1, 2, 3)]

    def copies(pin, out):
        x, y, c = _place()
        mine = _half_rows(buf, c)[0]
        own = out[0].at[2 * x + y, mine]
        sent = [(own, own, (cx, cy, c)) for cx, cy in chips()]
        return sent + [(out[0].at[2 * cx + cy, mine], out[0].at[2 * cx + cy, mine], (x, y, 1 - c)) for cx, cy in chips()]

    def arrivals(pin, out):
        mine, theirs = _half_rows(buf, _place()[2])
        return [out[0].at[2 * cx + cy, rows] for rows in (mine, theirs) for cx, cy in chips()]

    return _Phase([buf], [jax.ShapeDtypeStruct(buf.shape, buf.dtype)], {0: 0}, 6, copies, arrivals,
                  own_starts=(3, 4, 5), own_waits=range(6))


def _rs_swap_phase(grads):
    n = len(grads)

    def copies(g, out):
        x, y, c = _place()
        return [(g[a].at[:, _half_rows(grads[a], c)[1]], out[a], (x, y, 1 - c)) for a in range(n)]

    shapes = [jax.ShapeDtypeStruct((N_CHIPS, g.shape[1] // 2, g.shape[2]), g.dtype) for g in grads]
    return _Phase(grads, shapes, {}, n, copies, lambda g, out: list(out))


def _rs_add_sibling(place, grads, gots, name):
    n = len(grads)
    _, R, C = grads[0].shape
    half = R // 2
    tr = _tile(half, max(16, (1 << 19) // C // 16 * 16), 16)
    nr = half // tr

    def body(p_ref, *refs):
        for a in range(n):
            refs[2 * n + a][...] = (refs[2 * a][...].astype(f32) + refs[2 * a + 1][...].astype(f32)).astype(bf16)

    res = pl.pallas_call(
        body, name=name,
        grid_spec=pltpu.PrefetchScalarGridSpec(
            num_scalar_prefetch=1, grid=(N_CHIPS, nr),
            in_specs=[pl.BlockSpec((1, tr, C), lambda k, r, p: (k, p[0] * nr + r, 0)),
                      pl.BlockSpec((1, tr, C), lambda k, r, p: (k, r, 0))] * n,
            out_specs=[pl.BlockSpec((1, tr, C), lambda k, r, p: (k, r, 0))] * n),
        out_shape=[jax.ShapeDtypeStruct((N_CHIPS, half, C), bf16)] * n,
        compiler_params=_params(("parallel", "parallel")),
    )(place, *[x for pair in zip(grads, gots) for x in pair])
    return list(res)


def _rs_chips_phase(parts):
    n = len(parts)

    def copies(p, fc):
        x, y, c = _place()
        chips, ks = _other_chips(x, y)
        return [(p[a].at[ks[j]], fc[a].at[j], (*chips[j], c)) for a in range(n) for j in range(3)]

    shapes = [jax.ShapeDtypeStruct((3,) + q.shape[1:], q.dtype) for q in parts]
    return _Phase(parts, shapes, {}, 3 * n, copies, lambda p, fc: [fc[a].at[j] for a in range(n) for j in range(3)])


def _rs_hand_phase(parts, from_chips):
    n = len(parts)

    def copies(pin, fs):
        x, y, c = _place()
        sib = (x, y, 1 - c)
        own = [(pin[a].at[2 * x + y], fs[a].at[0], sib) for a in range(n)]
        return own + [(pin[n + a].at[j], fs[a].at[1 + j], sib) for a in range(n) for j in range(3)]

    def arrivals(pin, fs):
        return [fs[a].at[0] for a in range(n)] + [fs[a].at[1 + j] for a in range(n) for j in range(3)]

    shapes = [jax.ShapeDtypeStruct((4,) + q.shape[1:], q.dtype) for q in parts]
    return _Phase(list(parts) + list(from_chips), shapes, {}, 4 * n, copies, arrivals)


class _Exchange:
    def __init__(self, place):
        self.place = place

    def feed(self, buf):
        return _feed_phase(buf)

    def gather(self, bufs, over_ici):
        return _gather_phase(bufs, over_ici)

    def swap(self, grads):
        return _rs_swap_phase(grads)

    def pair_sums(self, names, grads):
        return self.add(names, grads, _run_phases("rs_sibling_" + names[0], [_rs_swap_phase(grads)]))

    def add(self, names, grads, got):
        parts = {}
        for group in _same_shape(grads):
            res = _rs_add_sibling(self.place, [grads[i] for i in group], [got[i] for i in group],
                                  "rs_add_" + names[group[0]])
            parts.update(zip(group, res))
        return [parts[i] for i in range(len(names))]

    def to_chips(self, parts):
        return _rs_chips_phase(parts)

    def to_sibling(self, parts, from_chips):
        return _rs_hand_phase(parts, from_chips)

    def spread(self, blk):
        return _spread_phase(blk)

    def hand_over(self, name, parts, from_chips, blk):
        got = _run_phases(name, [_join([_rs_hand_phase(parts, from_chips), _spread_phase(blk)])])
        return got[:-1], got[-1]


def _local_step(place, x, target, norm_mix, b_gate, rb_chip, norm_ffn, norm_final, w_in, rest, exch):
    B, S, D = x.shape
    T = B * S
    x2 = x.reshape(T, D)
    tg2 = target.reshape(T, D)
    rope, decay = _rope_tables(S), _decay_tables()
    g_fin = norm_final.reshape(1, D)
    nrel = rb_chip.shape[-1]

    mrg, ffn = ["w_ret_out", "w_att_out", "w_out"], ["w_ffn_gate", "w_ffn_up", "w_ffn_down"]
    (xn, proj), got = _in_proj(place, x2, norm_mix, _join([exch.feed(w_in), exch.gather([rest[n] for n in mrg], True),
                                                           exch.spread(jnp.pad(rb_chip, ((0, 0), (0, 128 - nrel))))]))
    w_in, wb, rb_all = got[0], {}, got.pop()
    trows = _bias_rows(jnp.concatenate([rb_all[2 * k, :, :nrel] for k in range(N_CHIPS)], axis=1))
    (qr, kr, o, u, states), got = _ret_fwd(proj, B, S, rope, decay, _join([exch.gather([rest["w_ffn_gate"]], True),
                                                                         exch.gather(got[1:], False)]))
    wb.update(zip(mrg, got[1:]))
    (ao, probs), got = _att_fwd(proj, trows, B, S, _join([exch.gather([rest["w_ffn_up"], rest["w_ffn_down"]], True),
                                                    exch.gather(got[:1], False)]))
    wb["w_ffn_gate"] = got[2]
    w_ro, w_out = wb["w_ret_out"].reshape(-1, D), wb["w_out"].reshape(-1, D)
    (h1, yr, ya), got = _mix_fwd(x2, proj, u, ao, b_gate, w_ro, wb["w_att_out"], w_out, exch.gather(got[:2], False))
    wb.update(zip(ffn[1:], got))
    hn, a, b, f, dh2, dh2b, part_fin = _ffn_fwd(h1, norm_ffn, wb["w_ffn_gate"], wb["w_ffn_up"], wb["w_ffn_down"], g_fin, tg2)

    da, db, dh1, dh1b, part_ffn = _ffn_bwd(dh2, h1, norm_ffn, a, b, wb["w_ffn_gate"], wb["w_ffn_up"], wb["w_ffn_down"])
    ffn = ["w_ffn_down", "w_ffn_gate", "w_ffn_up"]
    g_ffn = [_wgrad(f, dh2b, 0, "wgrad_ffn_down")[0], _wgrad(da, hn, 0, "wgrad_ffn_gate")[0],
             _wgrad(db, hn, 0, "wgrad_ffn_up")[0]]
    (du, dao, dgl, mix, dyr, dya, part_bg), x_ffn = _mix_bwd(dh1b, proj, yr, ya, b_gate, w_ro, wb["w_att_out"], w_out,
                                                             exch.swap(g_ffn))
    p_ffn = exch.add(ffn, g_ffn, x_ffn)
    mrg = ["w_out", "w_ret_out", "w_att_out"]
    g_mrg = [_wgrad(mix, dh1b, 0, "wgrad_out")[0], _wgrad(u, dyr, 0, "wgrad_ret_out")[0],
             _wgrad(ao, dya, 1, "wgrad_att_out")[0]]
    (dproj,), got = _ret_bwd(proj, qr, kr, o, states, du, dgl, B, S, rope, decay, _join([exch.to_chips(p_ffn[:2]),
                                                                                          exch.swap(g_mrg)]))
    c_two, p_mrg = got[:2], exch.add(mrg, g_mrg, got[2:])
    (dproj, dvec), got = _att_bwd(proj, dao, probs, dproj, B, S, exch.to_chips(p_ffn[2:] + p_mrg))
    c_ffn, c_mrg = c_two + got[:1], got[1:]
    g_in, got = _wgrad(xn, dproj, 1, "wgrad_in", exch.to_sibling(p_ffn + p_mrg, c_ffn + c_mrg))
    s_ffn, s_mrg = got[:len(ffn)], got[len(ffn):]
    p_in = exch.pair_sums(["w_in"], [g_in])
    (gx, part_mix), c_in = _in_proj_bwd(dproj, w_in, x2, norm_mix, dh1, exch.to_chips(p_in))
    rows = lambda p, r: p[r]
    lo = KWIN - 1 - (MAX_REL - 1)
    drb = jnp.concatenate([jnp.flip(dvec[:, lo:lo + N_REL - 1], axis=1), dvec[:, :lo].sum(axis=1, keepdims=True)], axis=1)
    gsmall = {
        "norm_mix": rows(part_mix, 0), "b_gate": rows(part_bg, 0), "rel_bias": drb, "norm_ffn": rows(part_ffn, 0),
        "norm_final": rows(part_fin, 0),
    }
    s_in, small_all = exch.hand_over("rs_hand_w_in", p_in, c_in, _pack_small(gsmall, rows(part_fin, 1)))
    gbig = dict(zip(ffn + mrg + ["w_in"], zip(p_ffn + p_mrg + p_in, c_ffn + c_mrg + c_in, s_ffn + s_mrg + s_in)))
    return gx.reshape(B, S, D), gbig, small_all


SMALL_ROWS = 16


def _pack_small(gs, loss_lanes):
    D = D_MODEL
    rb = jnp.pad(gs["rel_bias"].reshape(-1), (0, 3 * D - ATT_HEADS * N_REL)).reshape(3, D)
    rows = [gs["norm_mix"].reshape(1, D), gs["b_gate"].reshape(2, D), gs["norm_ffn"].reshape(1, D),
            gs["norm_final"].reshape(1, D), rb, loss_lanes.reshape(1, D)]
    used = sum(r.shape[0] for r in rows)
    return jnp.concatenate(rows + [jnp.zeros((SMALL_ROWS - used, D), f32)], axis=0)


def kernel(x, norm_mix, w_in, b_gate, rel_bias, w_ret_out, w_att_out, w_out, norm_ffn, w_ffn_gate, w_ffn_up, w_ffn_down, norm_final, loss_target, m_norm_mix, m_w_in, m_b_gate, m_rel_bias, m_w_ret_out, m_w_att_out, m_w_out, m_norm_ffn, m_w_ffn_gate, m_w_ffn_up, m_w_ffn_down, m_norm_final, v_norm_mix, v_w_in, v_b_gate, v_rel_bias, v_w_ret_out, v_w_att_out, v_w_out, v_norm_ffn, v_w_ffn_gate, v_w_ffn_up, v_w_ffn_down, v_norm_final):
    w = dict(norm_mix=norm_mix, w_in=w_in, b_gate=b_gate, rel_bias=rel_bias, w_ret_out=w_ret_out, w_att_out=w_att_out,
             w_out=w_out, norm_ffn=norm_ffn, w_ffn_gate=w_ffn_gate, w_ffn_up=w_ffn_up, w_ffn_down=w_ffn_down,
             norm_final=norm_final)
    m = dict(norm_mix=m_norm_mix, w_in=m_w_in, b_gate=m_b_gate, rel_bias=m_rel_bias, w_ret_out=m_w_ret_out,
             w_att_out=m_w_att_out, w_out=m_w_out, norm_ffn=m_norm_ffn, w_ffn_gate=m_w_ffn_gate, w_ffn_up=m_w_ffn_up,
             w_ffn_down=m_w_ffn_down, norm_final=m_norm_final)
    v = dict(norm_mix=v_norm_mix, w_in=v_w_in, b_gate=v_b_gate, rel_bias=v_rel_bias, w_ret_out=v_w_ret_out,
             w_att_out=v_w_att_out, w_out=v_w_out, norm_ffn=v_norm_ffn, w_ffn_gate=v_w_ffn_gate, w_ffn_up=v_w_ffn_up,
             w_ffn_down=v_w_ffn_down, norm_final=v_norm_final)
    xi, yi, ci = _place()
    k_me = 2 * xi + yi

    place = jnp.stack([ci, k_me]).astype(jnp.int32)
    big = [n for n, _ in BIG]

    turned = ("w_ffn_gate", "w_ffn_up")
    shard = lambda d, n: jnp.swapaxes(d[n][0], 0, 1) if n in turned else d[n][0]
    whole = lambda a, n: (jnp.swapaxes(a, 0, 1) if n in turned else a)[None]

    by_shape = [[big[i] for i in group] for group in _same_shape([shard(w, n) for n in big])]
    bufs = {}
    for names in by_shape:
        bufs.update(zip(names, _cast_shards(place, [shard(w, n) for n in names], "cast_" + names[0])))
    rest = {n: bufs[n] for n in big if n != "w_in"}
    nrel_loc = rel_bias.shape[-1]
    grad_x, gbig, small_all = _local_step(place, x, loss_target, norm_mix, b_gate, rel_bias[0], norm_ffn, norm_final,
                                          bufs["w_in"], rest, _Exchange(place))

    small = _sum_slots(small_all, "reduce_small")
    D = D_MODEL
    loss = jnp.sum(small[8])
    drb_full = small[5:8].reshape(-1)[:ATT_HEADS * N_REL].reshape(ATT_HEADS, N_REL)
    g = {
        "norm_mix": small[0:1], "b_gate": small[1:3].reshape(1, 2 * D), "norm_ffn": small[3:4], "norm_final": small[4],
        "rel_bias": lax.dynamic_slice_in_dim(drb_full, k_me * nrel_loc, nrel_loc, axis=1)[None],
    }

    delta, new_m, new_v = {}, {}, {}
    for names in by_shape:
        res = _adamw_sum(place, [(shard(w, n), shard(m, n), shard(v, n), *gbig[n]) for n in names], "adamw_" + names[0])
        for n, (g_, d_, m_, v_) in zip(names, res):
            g[n], delta[n], new_m[n], new_v[n] = whole(g_, n), whole(d_, n), whole(m_, n), whole(v_, n)
    flat = lambda d: jnp.concatenate([d[n].reshape(-1) for n in SMALL])
    n_small = sum(int(np.prod(w[n].shape)) for n in SMALL)
    n_pad = -n_small % 1024
    packs = [jnp.pad(flat(d), (0, n_pad)).reshape(-1, 128) for d in (w, g, m, v)]
    outs = _adamw(*packs, "adamw_small")
    for res, dst in zip(outs, (delta, new_m, new_v)):
        off = 0
        fl = res.reshape(-1)
        for n in SMALL:
            sz = int(np.prod(w[n].shape))
            dst[n] = fl[off:off + sz].reshape(w[n].shape)
            off += sz

    return (loss, grad_x, *[g[n] for n in WEIGHTS], *[delta[n] for n in WEIGHTS], *[new_m[n] for n in WEIGHTS],
            *[new_v[n] for n in WEIGHTS])
```

```python
import functools

import numpy as np
import jax
import jax.numpy as jnp
from jax import lax
from jax.experimental import pallas as pl
from jax.experimental.pallas import tpu as pltpu

f32 = jnp.float32
bf16 = jnp.bfloat16

D_MODEL = 1024
CHUNK = 64
RET_HEADS = 4
RET_KEY_DIM = 128
RET_VAL_DIM = 256
ATT_HEADS = 8
ATT_HEAD_DIM = 64
ATT_W = ATT_HEADS * ATT_HEAD_DIM
BAND_CHUNKS = 8
PAD = BAND_CHUNKS * CHUNK
MAX_REL = 256
N_REL = CHUNK + MAX_REL
D_FF = 2816
N_IN = 6656
ROPE_BASE = 10000.0
EPS = 1e-6
NEG_INF = -1e30
C_RQ, C_RK, C_RV, C_RG, C_AQ, C_AK, C_AV, C_GL = 0, 512, 1024, 2048, 3072, 3584, 4096, 4608

ADAM_LR, ADAM_B1, ADAM_B2, ADAM_EPS, ADAM_WD, ADAM_STEP = 0.001, 0.9, 0.999, 1e-08, 0.01, 10

N_CHIPS = 4
N_DEV = 8
WGRAD_ACC_BYTES = 8 * 1024 * 1024
WGRAD_VMEM_BYTES = 40 * 1024 * 1024
ROW_TILE = 512
BIG_ROW_TILE = 1024
IN_ORDER = (0, 2, 3, 1)
QBLK = 256
KWIN = PAD + QBLK
TOEP = 1024
VMEM_LIMIT = 56 * 1024 * 1024
MESH = pl.DeviceIdType.MESH

BIG = (
    ("w_in", 1), ("w_ret_out", 0), ("w_att_out", 1), ("w_out", 0), ("w_ffn_gate", 1), ("w_ffn_up", 1), ("w_ffn_down", 0))
WEIGHTS = ("norm_mix", "w_in", "b_gate", "rel_bias", "w_ret_out", "w_att_out", "w_out", "norm_ffn", "w_ffn_gate",
           "w_ffn_up", "w_ffn_down", "norm_final")
SMALL = ("norm_mix", "b_gate", "rel_bias", "norm_ffn", "norm_final")


def _dot(a, b):
    return lax.dot_general(a, b, (((1,), (0,)), ((), ())), preferred_element_type=f32)


def _dot_nt(a, b):
    return lax.dot_general(a, b, (((1,), (1,)), ((), ())), preferred_element_type=f32)


def _dot_tn(a, b):
    return lax.dot_general(a, b, (((0,), (0,)), ((), ())), preferred_element_type=f32)


def _sig(x):
    return 1.0 / (1.0 + jnp.exp(-x))


def _tile(n, pref, mult):
    best = None
    for t in range(mult, min(n, pref) + 1, mult):
        if n % t == 0:
            best = t
    return best if best is not None else n


def _same_shape(arrays):
    groups = {}
    for i, a in enumerate(arrays):
        groups.setdefault(a.shape, []).append(i)
    return list(groups.values())


def _params(sem, vmem=VMEM_LIMIT):
    return pltpu.CompilerParams(dimension_semantics=sem, vmem_limit_bytes=vmem)


def _in_proj(place, x2, gamma, phase):
    T, D = x2.shape
    _, _, ns = phase.arrays[0].shape
    tm = _tile(T, BIG_ROW_TILE, 8)
    ni = T // tm
    pass_chip = lambda j: sum(jnp.where(j == n, f, 0) for n, f in enumerate(IN_ORDER))

    def body(p_ref, x_ref, g_ref, xn_ref, pr_ref, xs_ref, w_ref, w_sem, carried):
        j, i = pl.program_id(0), pl.program_id(1)
        pin, pout, sems = carried
        rows = pl.ds(pl.multiple_of(i * tm, tm), tm)

        @pl.when(i == 0)
        def _():
            for n, f in enumerate(IN_ORDER):
                if f:
                    @pl.when(j == n)
                    def _():
                        phase.arrived(f - 1, pin, pout, *sems)
                        phase.begin(2 + f, pin, pout, *sems)
                        phase.arrived(2 + f, pin, pout, *sems)
            shard = pltpu.make_async_copy(pout[0].at[jnp.bitwise_xor(p_ref[1], pass_chip(j))], w_ref, w_sem)
            shard.start()
            shard.wait()

        @pl.when(j == 0)
        def _():
            x = x_ref[...]
            r = lax.rsqrt(jnp.mean(x * x, axis=-1, keepdims=True) + EPS)
            xn = (x * r * g_ref[...]).astype(bf16)
            xs_ref[rows, :] = xn
            xn_ref[...] = xn

        pr_ref[...] = _dot(xs_ref[rows, :], w_ref[...]).astype(bf16)

    first_pass = lambda j, i, p: (jnp.where(j == 0, i, ni - 1), 0)
    return _call(
        body, phase, name="in_proj", grid=(N_CHIPS, ni), prefetch=(place,), expose=True,
        in_specs=[pl.BlockSpec((tm, D), first_pass), pl.BlockSpec((1, D), lambda j, i, p: (0, 0))],
        out_specs=[pl.BlockSpec((tm, D), first_pass),
                   pl.BlockSpec((tm, ns), lambda j, i, p: (i, jnp.bitwise_xor(p[1], pass_chip(j))))],
        out_shape=[jax.ShapeDtypeStruct((T, D), bf16), jax.ShapeDtypeStruct((T, N_CHIPS * ns), bf16)],
        scratch_shapes=[pltpu.VMEM((T, D), bf16), pltpu.VMEM((D, ns), bf16), pltpu.SemaphoreType.DMA],
        args=(x2, gamma))


def _rope_tables(S):
    d = RET_KEY_DIM
    freqs = (np.float32(ROPE_BASE) ** (-np.arange(0, d, 2, dtype=np.float32) / np.float32(d))).astype(np.float32)
    ang = np.arange(S, dtype=np.float32)[:, None] * freqs[None, :]
    cos, sin = np.cos(ang).astype(np.float32), np.sin(ang).astype(np.float32)
    return jnp.asarray(np.concatenate([cos, cos], axis=1)), jnp.asarray(np.concatenate([-sin, sin], axis=1))


def _decay_tables():
    H = RET_HEADS
    log_g = jnp.log(1.0 - 2.0 ** (-5.0 - jnp.arange(H, dtype=f32)))
    p = jnp.arange(CHUNK, dtype=f32)
    intra = jnp.exp(log_g[:, None, None] * jnp.abs(p[:, None] - p[None, :]))
    q_dec = jnp.exp(log_g[:, None] * (p[None, :] + 1.0))
    k_dec = jnp.exp(log_g[:, None] * (CHUNK - 1.0 - p[None, :]))
    c_dec = jnp.exp(log_g * CHUNK)
    q_dec = jnp.broadcast_to(q_dec[:, :, None], (H, CHUNK, RET_KEY_DIM))
    k_dec = jnp.broadcast_to(k_dec[:, :, None], (H, CHUNK, RET_KEY_DIM))
    c_dec = jnp.broadcast_to(c_dec[:, None, None], (H, 1, RET_VAL_DIM))
    return intra, q_dec, k_dec, c_dec


K_SCALE = RET_KEY_DIM ** -0.5


RET_CHUNKS = 4


def _ret_tables_specs():
    whole = lambda *shape: pl.BlockSpec(shape, lambda b, i: (0,) * len(shape))
    return [whole(RET_HEADS, CHUNK, CHUNK), whole(RET_HEADS, CHUNK, RET_KEY_DIM), whole(RET_HEADS, CHUNK, RET_KEY_DIM),
            whole(RET_HEADS, 1, RET_VAL_DIM)]


def _rotate(x, cos, sn):
    return x * cos + pltpu.roll(x, RET_KEY_DIM // 2, 1) * sn


def _ret_fwd(proj, B, S, rope, decay, phase=None):
    T = B * S
    nc = S // CHUNK
    H, dk, dv = RET_HEADS, RET_KEY_DIM, RET_VAL_DIM
    sb = RET_CHUNKS * CHUNK
    ns = S // sb

    def body(q_ref, k_ref, v_ref, g_ref, cos_ref, sin_ref, intra_ref, qd_ref, kd_ref, cd_ref,
             qr_ref, kr_ref, o_ref, u_ref, st_ref, state_ref):
        @pl.when(pl.program_id(1) == 0)
        def _():
            state_ref[...] = jnp.zeros_like(state_ref)

        cos, sn = cos_ref[...], sin_ref[...]
        for h in range(H):
            hs = slice(h * dk, (h + 1) * dk)
            qr_ref[:, hs] = _rotate(q_ref[:, hs].astype(f32), cos, sn).astype(bf16)
            kr_ref[:, hs] = (_rotate(k_ref[:, hs].astype(f32), cos, sn) * K_SCALE).astype(bf16)
        states = [state_ref[h] for h in range(H)]
        for ci in range(RET_CHUNKS):
            r = slice(ci * CHUNK, (ci + 1) * CHUNK)
            for h in range(H):
                hk, hv = slice(h * dk, (h + 1) * dk), slice(h * dv, (h + 1) * dv)
                qi, ki, vi = qr_ref[r, hk], kr_ref[r, hk], v_ref[r, hv]
                stb = states[h].astype(bf16)
                st_ref[0, h, ci] = stb
                s = (_dot_nt(qi, ki) * intra_ref[h]).astype(bf16)
                o = _dot(s, vi) + _dot((qi.astype(f32) * qd_ref[h]).astype(bf16), stb)
                states[h] = states[h] * cd_ref[h] + _dot_tn((ki.astype(f32) * kd_ref[h]).astype(bf16), vi)
                mu = jnp.mean(o, axis=-1, keepdims=True)
                xc = o - mu
                var = jnp.mean(xc * xc, axis=-1, keepdims=True)
                oh = xc * lax.rsqrt(var + EPS)
                g = g_ref[r, hv].astype(f32)
                o_ref[r, hv] = o.astype(bf16)
                u_ref[r, hv] = (g * _sig(g) * oh).astype(bf16)
        for h in range(H):
            state_ref[h] = states[h]

    blk = lambda w, c: pl.BlockSpec((sb, w), lambda b, i: (b * ns + i, c))
    return _call(
        body, phase, name="ret_fwd", grid=(B, ns), scratch_shapes=[pltpu.VMEM((H, dk, dv), f32)],
        in_specs=[blk(H * dk, C_RQ // (H * dk)), blk(H * dk, C_RK // (H * dk)), blk(H * dv, C_RV // (H * dv)),
                  blk(H * dv, C_RG // (H * dv)),
                  pl.BlockSpec((sb, dk), lambda b, i: (i, 0)), pl.BlockSpec((sb, dk), lambda b, i: (i, 0)),
                  *_ret_tables_specs()],
        out_specs=[blk(H * dk, 0), blk(H * dk, 0), blk(H * dv, 0), blk(H * dv, 0),
                   pl.BlockSpec((1, H, RET_CHUNKS, dk, dv), lambda b, i: (b, 0, i, 0, 0))],
        out_shape=[jax.ShapeDtypeStruct((T, H * dk), bf16), jax.ShapeDtypeStruct((T, H * dk), bf16),
                   jax.ShapeDtypeStruct((T, H * dv), bf16), jax.ShapeDtypeStruct((T, H * dv), bf16),
                   jax.ShapeDtypeStruct((B, H, nc, dk, dv), bf16)],
        args=(proj, proj, proj, proj, *rope, *decay))


def _bias_rows(rb):
    last = rb[:, N_REL - 1:]
    return jnp.concatenate([
        jnp.broadcast_to(last, (ATT_HEADS, PAD - MAX_REL + 1)),
        jnp.flip(rb[:, :N_REL - 1], axis=1),
        jnp.broadcast_to(rb[:, :1], (ATT_HEADS, KWIN - PAD - CHUNK)),
        jnp.broadcast_to(last, (ATT_HEADS, TOEP - KWIN)),
    ], axis=1)


def _build_bias(t_ref, bias_ref):
    row = lax.broadcasted_iota(jnp.int32, (QBLK, KWIN), 0) // CHUNK
    col = lax.broadcasted_iota(jnp.int32, (QBLK, KWIN), 1) // CHUNK
    delta = BAND_CHUNKS + row - col
    vis = (delta >= 0) & (delta <= BAND_CHUNKS)
    for h in range(ATT_HEADS):
        t = jnp.broadcast_to(t_ref[h:h + 1, :], (QBLK, TOEP))
        rolled = pltpu.roll(t, 0, 1, stride=1, stride_axis=0)
        bias_ref[h] = jnp.where(vis, rolled[:, :KWIN], NEG_INF)


ATT_SCALE = ATT_HEAD_DIM ** -0.5


def _att_probs(qh, kh, bias):
    s = _dot_nt(qh, kh) + bias
    m = jnp.max(s, axis=-1, keepdims=True)
    p = jnp.exp(s - m)
    return p * (1.0 / jnp.sum(p, axis=-1, keepdims=True))


def _first_of_pair():
    return lax.broadcasted_iota(jnp.int32, (1, 2 * ATT_HEAD_DIM), 1) < ATT_HEAD_DIM


def _by_window(i, step):
    sizes = list(range(QBLK, KWIN, QBLK))
    for n, nk in enumerate(sizes):
        pl.when(i == n)(functools.partial(step, nk))
    pl.when(i >= len(sizes))(functools.partial(step, KWIN))


def _att_fwd(proj, trows, B, S, phase=None):
    T = B * S
    nq = S // QBLK
    dh = ATT_HEAD_DIM

    def body(q_ref, k_ref, v_ref, t_ref, o_ref, p_ref, bias_ref):
        i = pl.program_id(1)

        @pl.when((pl.program_id(0) == 0) & (i == 0))
        def _():
            _build_bias(t_ref, bias_ref)

        def step(nk):
            win = pl.ds(pl.multiple_of((i + 1) * QBLK - nk, QBLK), nk)
            kw, vw = k_ref[win, :], v_ref[win, :]
            first = _first_of_pair()
            outs = []
            for p in range(ATT_HEADS // 2):
                ps = slice(2 * p * dh, 2 * (p + 1) * dh)
                q2, k2, v2 = q_ref[:, ps] * ATT_SCALE, kw[:, ps], vw[:, ps]
                both = []
                for e in range(2):
                    qm = jnp.where(first == (e == 0), q2, jnp.zeros_like(q2))
                    pr = _att_probs(qm, k2, bias_ref[2 * p + e, :, KWIN - nk:]).astype(bf16)
                    p_ref[0, 2 * p + e, :, KWIN - nk:] = pr
                    both.append(_dot(pr, v2))
                outs.append(jnp.where(first, both[0], both[1]))
            o_ref[...] = jnp.concatenate(outs, axis=1).astype(bf16)

        _by_window(i, step)

    return _call(
        body, phase, name="att_fwd", grid=(B, nq),
        in_specs=[pl.BlockSpec((QBLK, ATT_W), lambda b, i: (b * nq + i, C_AQ // ATT_W)),
                  pl.BlockSpec((S, ATT_W), lambda b, i: (b, C_AK // ATT_W)),
                  pl.BlockSpec((S, ATT_W), lambda b, i: (b, C_AV // ATT_W)),
                  pl.BlockSpec((ATT_HEADS, TOEP), lambda b, i: (0, 0))],
        out_specs=[pl.BlockSpec((QBLK, ATT_W), lambda b, i: (b * nq + i, 0)),
                   pl.BlockSpec((1, ATT_HEADS, QBLK, KWIN), lambda b, i: (b * nq + i, 0, 0, 0))],
        out_shape=[jax.ShapeDtypeStruct((T, ATT_W), bf16), jax.ShapeDtypeStruct((B * nq, ATT_HEADS, QBLK, KWIN), bf16)],
        scratch_shapes=[pltpu.VMEM((ATT_HEADS, QBLK, KWIN), f32)],
        args=(proj, proj, proj, trows))


def _gl_specs(tm):
    w = 512
    return [pl.BlockSpec((tm, w), functools.partial(lambda i, j: (i, C_GL // 512 + j), j=j)) for j in range(4)]


def _gates(gl_refs, bg_ref):
    gl = jnp.concatenate([r[...] for r in gl_refs], axis=1).astype(f32) + bg_ref[...]
    g = _sig(gl)
    return g[:, :D_MODEL], g[:, D_MODEL:]


def _mix_fwd(x2, proj, u, ao, b_gate, w_ro, w_ao, w_out, phase=None):
    T, D = x2.shape
    tm = _tile(T, ROW_TILE, 8)

    def body(x_ref, u_ref, ao_ref, g0, g1, g2, g3, bg_ref, wro_ref, wao_ref, wo_ref, h1_ref, yr_ref, ya_ref):
        yr = _dot(u_ref[...], wro_ref[...])
        ao = ao_ref[...]
        ya = jnp.concatenate([_dot(ao, wao_ref[k]) for k in range(N_CHIPS)], axis=1)
        gr, ga = _gates((g0, g1, g2, g3), bg_ref)
        mix = gr * yr + ga * ya
        h1_ref[...] = x_ref[...] + _dot(mix.astype(bf16), wo_ref[...])
        yr_ref[...] = yr.astype(bf16)
        ya_ref[...] = ya.astype(bf16)

    full = lambda a: pl.BlockSpec(a.shape, lambda i: (0,) * a.ndim)
    row = lambda n: pl.BlockSpec((tm, n), lambda i: (i, 0))
    return _call(
        body, phase, name="mix_fwd", grid=(T // tm,), scratch_shapes=[],
        in_specs=[row(D), row(D), row(ATT_W), *_gl_specs(tm), full(b_gate), full(w_ro), full(w_ao), full(w_out)],
        out_specs=[row(D), row(D), row(D)],
        out_shape=[jax.ShapeDtypeStruct((T, D), f32), jax.ShapeDtypeStruct((T, D), bf16),
                   jax.ShapeDtypeStruct((T, D), bf16)],
        args=(x2, u, ao, proj, proj, proj, proj, b_gate, w_ro, w_ao, w_out))


def _ffn_fwd(h1, g_ffn, wg, wu, wd, g_fin, target):
    T, D = h1.shape
    nf, tf, _ = wg.shape
    tm = _tile(T, ROW_TILE, 8)

    def body(h1_ref, g_ref, wg_ref, wu_ref, wd_ref, gf_ref, tg_ref, hn_ref, a_ref, b_ref, f_ref, dh2_ref, dh2b_ref,
             part_ref):
        h1v = h1_ref[...]
        r = lax.rsqrt(jnp.mean(h1v * h1v, axis=-1, keepdims=True) + EPS)
        hn = (h1v * r * g_ref[...]).astype(bf16)
        hn_ref[...] = hn
        h2 = h1v
        for k in range(nf):
            a = _dot_nt(hn, wg_ref[k])
            b = _dot_nt(hn, wu_ref[k])
            f = ((a * _sig(a)) * b).astype(bf16)
            a_ref[k] = a.astype(bf16)
            b_ref[k] = b.astype(bf16)
            f_ref[k] = f
            h2 = h2 + _dot(f, wd_ref[k])
        r = lax.rsqrt(jnp.mean(h2 * h2, axis=-1, keepdims=True) + EPS)
        n = h2 * r
        gf = gf_ref[...]
        e = n * gf - tg_ref[...]
        dy = e * (1.0 / D)
        dn = dy * gf
        dh2 = r * (dn - n * jnp.mean(dn * n, axis=-1, keepdims=True))
        dh2_ref[...] = dh2
        dh2b_ref[...] = dh2.astype(bf16)
        part_ref[...] = jnp.zeros_like(part_ref)
        part_ref[0:1, :] = jnp.sum(dy * n, axis=0, keepdims=True)
        part_ref[1:2, :] = (0.5 / D) * jnp.sum(e * e, axis=0, keepdims=True)

    row = lambda n: pl.BlockSpec((tm, n), lambda i: (i, 0))
    vec = pl.BlockSpec((1, D), lambda i: (0, 0))
    col = pl.BlockSpec((nf, tm, tf), lambda i: (0, i, 0))
    held = lambda w: pl.BlockSpec(w.shape, lambda i: (0, 0, 0), pipeline_mode=pl.Buffered(1))
    act = jax.ShapeDtypeStruct((nf, T, tf), bf16)
    return pl.pallas_call(
        body, name="ffn_fwd", grid=(T // tm,),
        in_specs=[row(D), vec, held(wg), held(wu), held(wd), vec, row(D)],
        out_specs=[row(D), col, col, col, row(D), row(D), pl.BlockSpec((8, D), lambda i: (i, 0))],
        out_shape=[jax.ShapeDtypeStruct((T, D), bf16), act, act, act, jax.ShapeDtypeStruct((T, D), f32),
                   jax.ShapeDtypeStruct((T, D), bf16), jax.ShapeDtypeStruct((T // tm * 8, D), f32)],
        compiler_params=_params(("parallel",)),
    )(h1, g_ffn, wg, wu, wd, g_fin, target)


def _ffn_bwd(dh2, h1, g_ffn, a, b, wg, wu, wd):
    T, D = h1.shape
    nf, tf, _ = wg.shape
    tm = _tile(T, ROW_TILE // 2, 8)

    def body(dh2_ref, h1_ref, g_ref, a_ref, b_ref, wg_ref, wu_ref, wd_ref, da_ref, db_ref, dh1_ref, dh1b_ref, part_ref):
        dh2v = dh2_ref[...]
        dh2b = dh2v.astype(bf16)
        dhn = jnp.zeros((tm, D), f32)
        for k in range(nf):
            df = _dot_nt(dh2b, wd_ref[k])
            av = a_ref[k].astype(f32)
            sg = _sig(av)
            db = (df * (av * sg)).astype(bf16)
            da = (df * b_ref[k].astype(f32) * (sg * (1.0 + av * (1.0 - sg)))).astype(bf16)
            da_ref[k] = da
            db_ref[k] = db
            dhn = dhn + _dot(da, wg_ref[k]) + _dot(db, wu_ref[k])
        h = h1_ref[...]
        r = lax.rsqrt(jnp.mean(h * h, axis=-1, keepdims=True) + EPS)
        n = h * r
        dn = dhn * g_ref[...]
        dh1 = dh2v + r * (dn - n * jnp.mean(dn * n, axis=-1, keepdims=True))
        dh1_ref[...] = dh1
        dh1b_ref[...] = dh1.astype(bf16)
        part_ref[...] = jnp.zeros_like(part_ref)
        part_ref[0:1, :] = jnp.sum(dhn * n, axis=0, keepdims=True)

    row = lambda n: pl.BlockSpec((tm, n), lambda i: (i, 0))
    col = pl.BlockSpec((nf, tm, tf), lambda i: (0, i, 0))
    held = lambda w: pl.BlockSpec(w.shape, lambda i: (0, 0, 0), pipeline_mode=pl.Buffered(1))
    act = jax.ShapeDtypeStruct((nf, T, tf), bf16)
    return pl.pallas_call(
        body, name="ffn_bwd", grid=(T // tm,),
        in_specs=[row(D), row(D), pl.BlockSpec((1, D), lambda i: (0, 0)), col, col, held(wg), held(wu), held(wd)],
        out_specs=[col, col, row(D), row(D), pl.BlockSpec((8, D), lambda i: (i, 0))],
        out_shape=[act, act, jax.ShapeDtypeStruct((T, D), f32), jax.ShapeDtypeStruct((T, D), bf16),
                   jax.ShapeDtypeStruct((T // tm * 8, D), f32)],
        compiler_params=_params(("parallel",)),
    )(dh2, h1, g_ffn, a, b, wg, wu, wd)


def _mix_bwd(dh1, proj, yr, ya, b_gate, w_ro, w_ao, w_out, phase=None):
    T, D = dh1.shape
    tm = _tile(T, ROW_TILE, 8)

    def body(dh1_ref, g0, g1, g2, g3, bg_ref, yr_ref, ya_ref, wro_ref, wao_ref, wo_ref,
             du_ref, dao_ref, dgl_ref, mix_ref, dyr_ref, dya_ref, part_ref):
        dmix = _dot_nt(dh1_ref[...], wo_ref[...])
        gr, ga = _gates((g0, g1, g2, g3), bg_ref)
        yr = yr_ref[...].astype(f32)
        ya = ya_ref[...].astype(f32)
        dyr = (dmix * gr).astype(bf16)
        dya = (dmix * ga).astype(bf16)
        dgl = jnp.concatenate([dmix * yr * gr * (1.0 - gr), dmix * ya * ga * (1.0 - ga)], axis=1)
        du_ref[...] = _dot_nt(dyr, wro_ref[...]).astype(bf16)
        ns = wao_ref.shape[2]
        dao = _dot_nt(dya[:, :ns], wao_ref[0])
        for k in range(1, N_CHIPS):
            dao = dao + _dot_nt(dya[:, k * ns:(k + 1) * ns], wao_ref[k])
        dao_ref[...] = dao.astype(bf16)
        dgl_ref[...] = dgl.astype(bf16)
        mix_ref[...] = (gr * yr + ga * ya).astype(bf16)
        dyr_ref[...] = dyr
        dya_ref[...] = dya
        part_ref[...] = jnp.zeros_like(part_ref)
        part_ref[0:1, :] = jnp.sum(dgl, axis=0, keepdims=True)

    full = lambda a: pl.BlockSpec(a.shape, lambda i: (0,) * a.ndim)
    row = lambda n: pl.BlockSpec((tm, n), lambda i: (i, 0))
    return _call(
        body, phase, name="mix_bwd", grid=(T // tm,), scratch_shapes=[],
        in_specs=[row(D), *_gl_specs(tm), full(b_gate), row(D), row(D), full(w_ro), full(w_ao), full(w_out)],
        out_specs=[row(D), row(ATT_W), row(2 * D), row(D), row(D), row(D), pl.BlockSpec((8, 2 * D), lambda i: (i, 0))],
        out_shape=[jax.ShapeDtypeStruct((T, D), bf16), jax.ShapeDtypeStruct((T, ATT_W), bf16),
                   jax.ShapeDtypeStruct((T, 2 * D), bf16), jax.ShapeDtypeStruct((T, D), bf16),
                   jax.ShapeDtypeStruct((T, D), bf16), jax.ShapeDtypeStruct((T, D), bf16),
                   jax.ShapeDtypeStruct((T // tm * 8, 2 * D), f32)],
        args=(dh1, proj, proj, proj, proj, b_gate, yr, ya, w_ro, w_ao, w_out))


def _ret_bwd(proj, qr, kr, o, states, du, dgl, B, S, rope, decay, phase=None):
    T = B * S
    H, dk, dv = RET_HEADS, RET_KEY_DIM, RET_VAL_DIM
    sb = RET_CHUNKS * CHUNK
    ns = S // sb

    def body(qr_ref, kr_ref, v_ref, g_ref, o_ref, st_ref, du_ref, dgl_ref, cos_ref, sin_ref, intra_ref, qd_ref, kd_ref,
             cd_ref, dp_ref, dstate_ref):
        dq_ref, dk_ref = dp_ref.at[:, pl.ds(C_RQ, H * dk)], dp_ref.at[:, pl.ds(C_RK, H * dk)]
        dv_ref, dg_ref = dp_ref.at[:, pl.ds(C_RV, H * dv)], dp_ref.at[:, pl.ds(C_RG, H * dv)]
        dp_ref[:, C_GL:] = dgl_ref[...]

        @pl.when(pl.program_id(1) == 0)
        def _():
            dstate_ref[...] = jnp.zeros_like(dstate_ref)

        cos, snb = cos_ref[...], -sin_ref[...]
        dstates = [dstate_ref[h] for h in range(H)]
        for ci in reversed(range(RET_CHUNKS)):
            r = slice(ci * CHUNK, (ci + 1) * CHUNK)
            for h in range(H):
                hk, hv = slice(h * dk, (h + 1) * dk), slice(h * dv, (h + 1) * dv)
                intra, qd, kd = intra_ref[h], qd_ref[h], kd_ref[h]
                qi, ki, vi = qr_ref[r, hk], kr_ref[r, hk], v_ref[r, hv]
                si = st_ref[0, h, ci]
                o = o_ref[r, hv].astype(f32)
                mu = jnp.mean(o, axis=-1, keepdims=True)
                xc = o - mu
                rstd = lax.rsqrt(jnp.mean(xc * xc, axis=-1, keepdims=True) + EPS)
                oh = xc * rstd
                g = g_ref[r, hv].astype(f32)
                sg = _sig(g)
                dui = du_ref[r, hv].astype(f32)
                dg_ref[r, hv] = (dui * oh * (sg * (1.0 + g * (1.0 - sg)))).astype(bf16)
                doh = dui * (g * sg)
                do = rstd * (doh - jnp.mean(doh, axis=-1, keepdims=True)
                             - oh * jnp.mean(doh * oh, axis=-1, keepdims=True))
                dob = do.astype(bf16)
                p = (_dot_nt(qi, ki) * intra).astype(bf16)
                dsb = dstates[h].astype(bf16)
                kt = (ki.astype(f32) * kd).astype(bf16)
                qt = (qi.astype(f32) * qd).astype(bf16)
                dv_ref[r, hv] = (_dot_tn(p, dob) + _dot(kt, dsb)).astype(bf16)
                da = (_dot_nt(dob, vi) * intra).astype(bf16)
                dq = _dot(da, ki) + _dot_nt(dob, si) * qd
                dkk = (_dot_tn(da, qi) + _dot_nt(vi, dsb) * kd) * K_SCALE
                dq_ref[r, hk] = _rotate(dq, cos[r], snb[r]).astype(bf16)
                dk_ref[r, hk] = _rotate(dkk, cos[r], snb[r]).astype(bf16)
                dstates[h] = dstates[h] * cd_ref[h] + _dot_tn(qt, dob)
        for h in range(H):
            dstate_ref[h] = dstates[h]

    blk = lambda w, c: pl.BlockSpec((sb, w), lambda b, i: (b * ns + ns - 1 - i, c))
    return _call(
        body, phase, name="ret_bwd", grid=(B, ns),
        in_specs=[blk(H * dk, 0), blk(H * dk, 0), blk(H * dv, C_RV // (H * dv)), blk(H * dv, C_RG // (H * dv)),
                  blk(H * dv, 0),
                  pl.BlockSpec((1, H, RET_CHUNKS, dk, dv), lambda b, i: (b, 0, ns - 1 - i, 0, 0)),
                  blk(H * dv, 0), blk(N_IN - C_GL, 0),
                  pl.BlockSpec((sb, dk), lambda b, i: (ns - 1 - i, 0)), pl.BlockSpec((sb, dk), lambda b, i: (ns - 1 - i, 0)),
                  *_ret_tables_specs()],
        out_specs=[blk(N_IN, 0)], out_shape=[jax.ShapeDtypeStruct((T, N_IN), bf16)],
        scratch_shapes=[pltpu.VMEM((H, dk, dv), f32)],
        args=(qr, kr, proj, proj, o, states, du, dgl, *rope, *decay))


def _att_bwd(proj, dao, probs, dproj, B, S, phase=None):
    T = B * S
    nq = S // QBLK
    dh = ATT_HEAD_DIM

    def body(q_ref, k_ref, v_ref, do_ref, p_ref, _, dp_ref, vec_ref, dbias_ref, dka_ref, dva_ref):
        b, i = pl.program_id(0), pl.program_id(1)

        @pl.when((b == 0) & (i == 0))
        def _():
            dbias_ref[...] = jnp.zeros_like(dbias_ref)

        @pl.when(i == 0)
        def _():
            dka_ref[...] = jnp.zeros_like(dka_ref)
            dva_ref[...] = jnp.zeros_like(dva_ref)

        def step(nk):
            win = pl.ds(pl.multiple_of((i + 1) * QBLK - nk, QBLK), nk)
            kw, vw = k_ref[win, :], v_ref[win, :]
            first = _first_of_pair()
            first_rows = lax.broadcasted_iota(jnp.int32, (2 * dh, 1), 0) < dh
            dqs, dks, dvs = [], [], []
            for p in range(ATT_HEADS // 2):
                ps = slice(2 * p * dh, 2 * (p + 1) * dh)
                q2, k2, v2, do2 = q_ref[:, ps] * ATT_SCALE, kw[:, ps], vw[:, ps], do_ref[:, ps]
                dq2, dk2, dv2 = [], [], []
                for e in range(2):
                    h = 2 * p + e
                    prb = p_ref[0, h, :, KWIN - nk:]
                    pr = prb.astype(f32)
                    dp = _dot_nt(jnp.where(first == (e == 0), do2, jnp.zeros_like(do2)), v2)
                    ds = pr * (dp - jnp.sum(pr * dp, axis=-1, keepdims=True))
                    dbias_ref[h, :, KWIN - nk:] += ds
                    dsb = ds.astype(bf16)
                    dq2.append(_dot(dsb, k2) * ATT_SCALE)
                    dk2.append(_dot_tn(q2, dsb))
                    dv2.append(_dot_tn(do2, prb))
                dqs.append(jnp.where(first, dq2[0], dq2[1]))
                dks.append(jnp.where(first_rows, dk2[0], dk2[1]))
                dvs.append(jnp.where(first_rows, dv2[0], dv2[1]))
            dp_ref[pl.ds(pl.multiple_of(i * QBLK, QBLK), QBLK), :ATT_W] = jnp.concatenate(dqs, axis=1).astype(bf16)
            dka_ref[:, win] += jnp.concatenate(dks, axis=0)
            dva_ref[:, win] += jnp.concatenate(dvs, axis=0)

        _by_window(i, step)

        @pl.when(i == nq - 1)
        def _():
            dp_ref[:, ATT_W:2 * ATT_W] = dka_ref[...].T.astype(bf16)
            dp_ref[:, 2 * ATT_W:] = dva_ref[...].T.astype(bf16)

        @pl.when((b == B - 1) & (i == nq - 1))
        def _():
            rr = lax.broadcasted_iota(jnp.int32, (QBLK, QBLK), 0)
            cc = lax.broadcasted_iota(jnp.int32, (QBLK, QBLK), 1)
            flip = jnp.where(rr + cc == QBLK - 1, 1.0, 0.0).astype(bf16)
            for h in range(ATT_HEADS):
                d = dbias_ref[h]
                hi = d.astype(bf16)
                lo = (d - hi.astype(f32)).astype(bf16)
                rev = _dot(flip, hi) + _dot(flip, lo)
                wide = jnp.concatenate([rev, jnp.zeros((QBLK, TOEP - KWIN), f32)], axis=1)
                rolled = pltpu.roll(wide, 0, 1, stride=1, stride_axis=0)
                vec_ref[h:h + 1, :] = jnp.sum(rolled, axis=0, keepdims=True)

    qspec = lambda c: pl.BlockSpec((QBLK, ATT_W), lambda b, i: (b * nq + i, c))
    kspec = lambda c: pl.BlockSpec((S, ATT_W), lambda b, i: (b, c))
    return _call(
        body, phase, name="att_bwd", grid=(B, nq), aliases={5: 0},
        in_specs=[qspec(C_AQ // ATT_W), kspec(C_AK // ATT_W), kspec(C_AV // ATT_W), qspec(0),
                  pl.BlockSpec((1, ATT_HEADS, QBLK, KWIN), lambda b, i: (b * nq + i, 0, 0, 0)),
                  pl.BlockSpec(memory_space=pl.ANY)],
        out_specs=[pl.BlockSpec((S, 3 * ATT_W), lambda b, i: (b, C_AQ // (3 * ATT_W))),
                   pl.BlockSpec((ATT_HEADS, TOEP), lambda b, i: (0, 0))],
        out_shape=[jax.ShapeDtypeStruct((T, N_IN), bf16), jax.ShapeDtypeStruct((ATT_HEADS, TOEP), f32)],
        scratch_shapes=[pltpu.VMEM((ATT_HEADS, QBLK, KWIN), f32), pltpu.VMEM((ATT_W, S), f32),
                        pltpu.VMEM((ATT_W, S), f32)],
        args=(proj, proj, proj, dao, probs, dproj))


def _in_proj_bwd(dproj, w_in, x2, gamma, dh1, phase=None):
    T, D = x2.shape
    nk, _, tk = w_in.shape
    tm = _tile(T, BIG_ROW_TILE, 8)

    def body(dp_ref, w_ref, x_ref, g_ref, dh1_ref, dx_ref, part_ref, acc_ref):
        j = pl.program_id(1)

        @pl.when(j == 0)
        def _():
            acc_ref[...] = jnp.zeros_like(acc_ref)

        acc_ref[...] += _dot_nt(dp_ref[...], w_ref[0])

        @pl.when(j == nk - 1)
        def _():
            x = x_ref[...]
            r = lax.rsqrt(jnp.mean(x * x, axis=-1, keepdims=True) + EPS)
            n = x * r
            dxn = acc_ref[...]
            dn = dxn * g_ref[...]
            dx_ref[...] = dh1_ref[...] + r * (dn - n * jnp.mean(dn * n, axis=-1, keepdims=True))
            part_ref[...] = jnp.zeros_like(part_ref)
            part_ref[0:1, :] = jnp.sum(dxn * n, axis=0, keepdims=True)

    row = lambda n: pl.BlockSpec((tm, n), lambda i, j: (i, 0))
    return _call(
        body, phase, name="in_proj_bwd", grid=(T // tm, nk), aliases={4: 0},
        in_specs=[pl.BlockSpec((tm, tk), lambda i, j: (i, j)), pl.BlockSpec((1, D, tk), lambda i, j: (j, 0, 0)), row(D),
                  pl.BlockSpec((1, D), lambda i, j: (0, 0)), row(D)],
        out_specs=[row(D), pl.BlockSpec((8, D), lambda i, j: (i, 0))],
        out_shape=[jax.ShapeDtypeStruct((T, D), f32), jax.ShapeDtypeStruct((T // tm * 8, D), f32)],
        scratch_shapes=[pltpu.VMEM((tm, D), f32)],
        args=(dproj, w_in, x2, gamma, dh1))


def _wgrad(a, b, shard_axis, name, phase=None):
    def spec(arr, sharded, tt):
        if arr.ndim == 3:
            return arr.shape[2], pl.BlockSpec((1, tt, arr.shape[2]), lambda s, t: (s, t, 0))
        if sharded:
            w = arr.shape[1] // N_CHIPS
            return w, pl.BlockSpec((tt, w), lambda s, t: (t, s))
        return arr.shape[1], pl.BlockSpec((tt, arr.shape[1]), lambda s, t: (t, 0))

    T = a.shape[-2]
    whole = a.ndim == 2 and b.ndim == 2 and a.shape[1] * b.shape[1] * 4 <= WGRAD_ACC_BYTES
    width = lambda arr, sharded: arr.shape[-1] // (1 if whole or arr.ndim == 3 or not sharded else N_CHIPS)
    wa, wb_ = width(a, shard_axis == 0), width(b, shard_axis == 1)
    fixed = wa * wb_ * (4 + 2 * 2)
    tt = T // 4 if whole else T
    while tt > 256 and 2 * tt * (wa * a.dtype.itemsize + wb_ * b.dtype.itemsize) + fixed > WGRAD_VMEM_BYTES:
        tt //= 2
    nt = T // tt
    if whole:
        K, N = a.shape[1], b.shape[1]
        a_spec, b_spec = pl.BlockSpec((tt, K), lambda s, t: (t, 0)), pl.BlockSpec((tt, N), lambda s, t: (t, 0))
        out_block = (N_CHIPS, K // N_CHIPS, N) if shard_axis == 0 else (N_CHIPS, K, N // N_CHIPS)
        out_spec = pl.BlockSpec(out_block, lambda s, t: (0, 0, 0))
    else:
        K, a_spec = spec(a, shard_axis == 0, tt)
        N, b_spec = spec(b, shard_axis == 1, tt)
        out_block = (N_CHIPS, K, N)
        out_spec = pl.BlockSpec((1, K, N), lambda s, t: (s, 0, 0))

    def body(a_ref, b_ref, o_ref, acc_ref):
        t = pl.program_id(1)

        @pl.when(t == 0)
        def _():
            acc_ref[...] = jnp.zeros_like(acc_ref)

        av = a_ref[0] if a.ndim == 3 else a_ref[...]
        bv = b_ref[0] if b.ndim == 3 else b_ref[...]
        acc_ref[...] += _dot_tn(av.astype(bf16), bv.astype(bf16))

        @pl.when(t == nt - 1)
        def _():
            if not whole:
                o_ref[0] = acc_ref[...].astype(bf16)
            else:
                _, kk, nn = out_block
                for s in range(N_CHIPS):
                    o_ref[s] = (acc_ref[s * kk:(s + 1) * kk, :] if shard_axis == 0
                                else acc_ref[:, s * nn:(s + 1) * nn]).astype(bf16)

    (grad,), carried = _call(
        body, phase, name=name, grid=(1 if whole else N_CHIPS, nt), in_specs=[a_spec, b_spec], out_specs=[out_spec],
        out_shape=[jax.ShapeDtypeStruct(out_block, bf16)], scratch_shapes=[pltpu.VMEM((K, N), f32)], args=(a, b))
    return grad, carried


def _adamw_sum(place, groups, name):
    n = len(groups)
    R, C = groups[0][0].shape
    half = R // 2
    tr = _tile(half, max(16, (1 << 18) // C // 16 * 16), 16)
    nr = half // tr

    def body(p_ref, *refs):
        for a in range(n):
            w_ref, m_ref, v_ref, part_ref, fc_ref, fs_ref = refs[6 * a:6 * a + 6]
            g_ref, d_ref, mo_ref, vo_ref = refs[6 * n + 4 * a:6 * n + 4 * a + 4]
            up = lambda x: x.astype(f32)
            mine = ((up(part_ref[0]) + up(fc_ref[0])) + up(fc_ref[1])) + up(fc_ref[2])
            sibs = ((up(fs_ref[0]) + up(fs_ref[1])) + up(fs_ref[2])) + up(fs_ref[3])
            g_ = jnp.where(pl.program_id(0) == p_ref[0], mine, sibs)
            m_ = ADAM_B1 * m_ref[...] + (1.0 - ADAM_B1) * g_
            v_ = ADAM_B2 * v_ref[...] + (1.0 - ADAM_B2) * (g_ * g_)
            m_hat = m_ / (1.0 - ADAM_B1 ** ADAM_STEP)
            v_hat = v_ / (1.0 - ADAM_B2 ** ADAM_STEP)
            g_ref[...] = g_
            d_ref[...] = -ADAM_LR * (m_hat / (jnp.sqrt(v_hat) + ADAM_EPS) + ADAM_WD * w_ref[...])
            mo_ref[...] = m_
            vo_ref[...] = v_

    spec = pl.BlockSpec((tr, C), lambda h, r, p: (h * nr + r, 0))
    one = [spec, spec, spec, pl.BlockSpec((1, tr, C), lambda h, r, p: (p[1], jnp.where(h == p[0], r, 0), 0)),
           pl.BlockSpec((3, tr, C), lambda h, r, p: (0, jnp.where(h == p[0], r, 0), 0)),
           pl.BlockSpec((4, tr, C), lambda h, r, p: (0, jnp.where(h == p[0], 0, r), 0))]
    res = pl.pallas_call(
        body, name=name,
        grid_spec=pltpu.PrefetchScalarGridSpec(num_scalar_prefetch=1, grid=(2, nr), in_specs=one * n,
                                               out_specs=[spec] * (4 * n)),
        out_shape=[jax.ShapeDtypeStruct((R, C), f32)] * (4 * n),
        compiler_params=_params(("parallel", "parallel")),
    )(place, *[x for g in groups for x in g])
    return [tuple(res[4 * a:4 * a + 4]) for a in range(n)]


def _adamw(w, g, m, v, name):
    R, C = w.shape
    tr = _tile(R, max(8, (1 << 18) // C // 8 * 8), 8)

    def body(w_ref, g_ref, m_ref, v_ref, d_ref, mo_ref, vo_ref):
        g_ = g_ref[...]
        m_ = ADAM_B1 * m_ref[...] + (1.0 - ADAM_B1) * g_
        v_ = ADAM_B2 * v_ref[...] + (1.0 - ADAM_B2) * (g_ * g_)
        m_hat = m_ / (1.0 - ADAM_B1 ** ADAM_STEP)
        v_hat = v_ / (1.0 - ADAM_B2 ** ADAM_STEP)
        d_ref[...] = -ADAM_LR * (m_hat / (jnp.sqrt(v_hat) + ADAM_EPS) + ADAM_WD * w_ref[...])
        mo_ref[...] = m_
        vo_ref[...] = v_

    spec = pl.BlockSpec((tr, C), lambda i: (i, 0))
    return pl.pallas_call(
        body, name=name, grid=(R // tr,), in_specs=[spec] * 4, out_specs=[spec] * 3,
        out_shape=[jax.ShapeDtypeStruct((R, C), f32)] * 3,
        compiler_params=_params(("parallel",)),
    )(w, g, m, v)


def _place():
    return lax.axis_index("x"), lax.axis_index("y"), lax.axis_index("c")


def _other_chips(x, y):
    chips = [(1 - x, y), (x, 1 - y), (1 - x, 1 - y)]
    return chips, [2 * cx + cy for cx, cy in chips]


def _spread_phase(blk):
    def peers():
        x, y, c = _place()
        return [tuple(1 - p if (k >> s) & 1 else p for p, s in ((x, 2), (y, 1), (c, 0))) for k in range(1, N_DEV)]

    def copies(pin, out):
        x, y, c = _place()
        mine = out[0].at[4 * x + 2 * y + c]
        return [(mine, mine, peer) for peer in peers()]

    stack = jnp.broadcast_to(blk, (N_DEV,) + blk.shape)
    return _Phase([stack], [jax.ShapeDtypeStruct(stack.shape, stack.dtype)], {0: 0}, N_DEV - 1, copies,
                  lambda pin, out: [out[0].at[4 * px + 2 * py + pc] for px, py, pc in peers()])


def _sum_slots(stack, name):
    def body(s_ref, o_ref):
        tot = s_ref[0]
        for d in range(1, stack.shape[0]):
            tot = tot + s_ref[d]
        o_ref[...] = tot

    vm = pl.BlockSpec(memory_space=pltpu.VMEM)
    return pl.pallas_call(body, name=name, in_specs=[vm], out_specs=vm,
                          out_shape=jax.ShapeDtypeStruct(stack.shape[1:], stack.dtype))(stack)


def _cast_shards(place, ws, name):
    n = len(ws)
    R, C = ws[0].shape
    tr = _tile(R, max(16, (1 << 19) // C // 16 * 16), 16)

    def body(p_ref, *refs):
        for a in range(n):
            refs[n + a][0] = refs[a][...].astype(bf16)

    return pl.pallas_call(
        body, name=name,
        grid_spec=pltpu.PrefetchScalarGridSpec(
            num_scalar_prefetch=1, grid=(R // tr,),
            in_specs=[pl.BlockSpec((tr, C), lambda r, p: (r, 0))] * n,
            out_specs=[pl.BlockSpec((1, tr, C), lambda r, p: (p[1], r, 0))] * n),
        out_shape=[jax.ShapeDtypeStruct((N_CHIPS, R, C), bf16)] * n,
        compiler_params=_params(("parallel",)),
    )(place, *ws)


class _Phase:
    def __init__(self, arrays, out_shapes, aliases, n_copies, copies, arrivals, own_starts=(), own_waits=()):
        self.arrays, self.out_shapes, self.aliases = list(arrays), list(out_shapes), dict(aliases)
        self.n_copies, self.copies, self.arrivals = n_copies, copies, arrivals
        self.own_starts, self.own_waits = tuple(own_starts), tuple(own_waits)

    def sems(self):
        return [pltpu.SemaphoreType.DMA((self.n_copies,)), pltpu.SemaphoreType.DMA((self.n_copies,))]

    def _descriptors(self, pin, pout, send_sems, recv_sems):
        return [pltpu.make_async_remote_copy(src_ref=s, dst_ref=d, send_sem=send_sems.at[i], recv_sem=recv_sems.at[i],
                                             device_id=to, device_id_type=MESH)
                for i, (s, d, to) in enumerate(self.copies(pin, pout))]

    def _arrival(self, i, pin, pout, send_sems, recv_sems):
        dst = self.arrivals(pin, pout)[i]
        return pltpu.make_async_remote_copy(src_ref=dst, dst_ref=dst, send_sem=send_sems.at[i], recv_sem=recv_sems.at[i],
                                            device_id=_place(), device_id_type=MESH)

    def start(self, pin, pout, send_sems, recv_sems):
        for i, cp in enumerate(self._descriptors(pin, pout, send_sems, recv_sems)):
            if i not in self.own_starts:
                cp.start()

    def begin(self, i, pin, pout, send_sems, recv_sems):
        self._descriptors(pin, pout, send_sems, recv_sems)[i].start()

    def arrived(self, i, pin, pout, send_sems, recv_sems):
        self._arrival(i, pin, pout, send_sems, recv_sems).wait_recv()

    def finish(self, pin, pout, send_sems, recv_sems):
        for i in range(self.n_copies):
            if i not in self.own_waits:
                self._arrival(i, pin, pout, send_sems, recv_sems).wait_recv()
        for cp in self._descriptors(pin, pout, send_sems, recv_sems):
            cp.wait_send()


def _join(phases):
    if len(phases) == 1:
        return phases[0]
    ai = np.cumsum([0] + [len(p.arrays) for p in phases])
    oi = np.cumsum([0] + [len(p.out_shapes) for p in phases])

    def each(fn_name, pin, pout):
        return [item for k, p in enumerate(phases)
                for item in getattr(p, fn_name)(pin[ai[k]:ai[k + 1]], pout[oi[k]:oi[k + 1]])]

    aliases = {int(ai[k]) + i: int(oi[k]) + j for k, p in enumerate(phases) for i, j in p.aliases.items()}
    ci = np.cumsum([0] + [p.n_copies for p in phases])
    shifted = lambda attr: [int(ci[k]) + i for k, p in enumerate(phases) for i in getattr(p, attr)]
    return _Phase([a for p in phases for a in p.arrays], [s for p in phases for s in p.out_shapes], aliases,
                  int(ci[-1]), functools.partial(each, "copies"), functools.partial(each, "arrivals"),
                  shifted("own_starts"), shifted("own_waits"))


def _call(body, phase, *, name, grid, in_specs, out_specs, out_shape, scratch_shapes, args, prefetch=(), expose=False,
          aliases=None):
    seq = _params(("arbitrary",) * len(grid))
    np_ = len(prefetch)
    own = {np_ + i: j for i, j in (aliases or {}).items()}
    if phase is None:
        spec = pltpu.PrefetchScalarGridSpec(num_scalar_prefetch=np_, grid=grid, in_specs=in_specs, out_specs=out_specs,
                                            scratch_shapes=scratch_shapes)
        res = pl.pallas_call(body, name=name, grid_spec=spec, out_shape=out_shape, input_output_aliases=own,
                             compiler_params=seq)(*prefetch, *args)
        return list(res), []
    ni, no, ns = len(in_specs), len(out_specs), len(scratch_shapes)
    pi, po = len(phase.arrays), len(phase.out_shapes)

    def hosted(*refs):
        cut = np.cumsum([np_, ni, pi, no, po, ns])
        pre, ins, pin, outs, pout, scr, sems = (refs[a:b] for a, b in zip([0, *cut], [*cut, len(refs)]))
        ids = [pl.program_id(d) for d in range(len(grid))]
        first = functools.reduce(lambda p, q: p & q, [i == 0 for i in ids])
        last = functools.reduce(lambda p, q: p & q, [i == g - 1 for i, g in zip(ids, grid)])
        pl.when(first)(lambda: phase.start(pin, pout, *sems))
        body(*pre, *ins, *outs, *scr, **({"carried": (pin, pout, sems)} if expose else {}))
        pl.when(last)(lambda: phase.finish(pin, pout, *sems))

    anyspace = pl.BlockSpec(memory_space=pl.ANY)
    spec = pltpu.PrefetchScalarGridSpec(
        num_scalar_prefetch=np_, grid=grid, in_specs=list(in_specs) + [anyspace] * pi,
        out_specs=list(out_specs) + [anyspace] * po, scratch_shapes=list(scratch_shapes) + phase.sems())
    res = pl.pallas_call(
        hosted, name=name, grid_spec=spec, out_shape=list(out_shape) + phase.out_shapes,
        input_output_aliases={**own, **{np_ + ni + i: no + j for i, j in phase.aliases.items()}}, compiler_params=seq,
    )(*prefetch, *args, *phase.arrays)
    return list(res[:no]), list(res[no:])


def _run_phases(name, phases):
    first = phases[0]
    pi, po = len(first.arrays), len(first.out_shapes)

    def body(*refs):
        pin, pout, sems = refs[:pi], refs[pi:pi + po], refs[pi + po:]
        for n, ph in enumerate(phases):
            ph.start(pin, pout, *sems[2 * n:2 * n + 2])
            ph.finish(pin, pout, *sems[2 * n:2 * n + 2])

    anyspace = pl.BlockSpec(memory_space=pl.ANY)
    return list(pl.pallas_call(
        body, name=name, in_specs=[anyspace] * pi, out_specs=[anyspace] * po, out_shape=first.out_shapes,
        input_output_aliases=first.aliases, scratch_shapes=[s for ph in phases for s in ph.sems()],
    )(*first.arrays))


def _half_rows(buf, c):
    half = buf.shape[1] // 2
    return pl.ds(c * half, half), pl.ds((1 - c) * half, half)


def _gather_phase(bufs, over_ici):
    n = len(bufs)
    shapes = [jax.ShapeDtypeStruct(b.shape, b.dtype) for b in bufs]

    def landed(out, which):
        x, y, c = _place()
        _, ks = _other_chips(x, y)
        return [out[a].at[ks[j], _half_rows(bufs[a], c)[which]] for a in range(n) for j in range(3)]

    def ici(pin, out):
        x, y, c = _place()
        chips, _ = _other_chips(x, y)
        mine = [out[a].at[2 * x + y, _half_rows(bufs[a], c)[0]] for a in range(n)]
        return [(mine[a], mine[a], (*chips[j], c)) for a in range(n) for j in range(3)]

    def d2d(pin, out):
        x, y, c = _place()
        return [(dst, dst, (x, y, 1 - c)) for dst in landed(out, 0)]

    if over_ici:
        return _Phase(bufs, shapes, {a: a for a in range(n)}, 3 * n, ici, lambda pin, out: landed(out, 0))
    return _Phase(bufs, shapes, {a: a for a in range(n)}, 3 * n, d2d, lambda pin, out: landed(out, 1))


def _feed_phase(buf):
    def chips():
        x, y, _ = _place()
        return [(x if f < 2 else 1 - x, y if f % 2 == 0 else 1 - y) for f in (1, 2, 3)]

    def copies(pin, out):
        x, y, c = _place()
        mine = _half_rows(buf, c)[0]
        own = out[0].at[2 * x + y, mine]
        sent = [(own, own, (cx, cy, c)) for cx, cy in chips()]
        return sent + [(out[0].at[2 * cx + cy, mine], out[0].at[2 * cx + cy, mine], (x, y, 1 - c)) for cx, cy in chips()]

    def arrivals(pin, out):
        mine, theirs = _half_rows(buf, _place()[2])
        return [out[0].at[2 * cx + cy, rows] for rows in (mine, theirs) for cx, cy in chips()]

    return _Phase([buf], [jax.ShapeDtypeStruct(buf.shape, buf.dtype)], {0: 0}, 6, copies, arrivals,
                  own_starts=(3, 4, 5), own_waits=range(6))


def _rs_swap_phase(grads):
    n = len(grads)

    def copies(g, out):
        x, y, c = _place()
        return [(g[a].at[:, _half_rows(grads[a], c)[1]], out[a], (x, y, 1 - c)) for a in range(n)]

    shapes = [jax.ShapeDtypeStruct((N_CHIPS, g.shape[1] // 2, g.shape[2]), g.dtype) for g in grads]
    return _Phase(grads, shapes, {}, n, copies, lambda g, out: list(out))


def _rs_add_sibling(place, grads, gots, name):
    n = len(grads)
    _, R, C = grads[0].shape
    half = R // 2
    tr = _tile(half, max(16, (1 << 19) // C // 16 * 16), 16)
    nr = half // tr

    def body(p_ref, *refs):
        for a in range(n):
            refs[2 * n + a][...] = (refs[2 * a][...].astype(f32) + refs[2 * a + 1][...].astype(f32)).astype(bf16)

    res = pl.pallas_call(
        body, name=name,
        grid_spec=pltpu.PrefetchScalarGridSpec(
            num_scalar_prefetch=1, grid=(N_CHIPS, nr),
            in_specs=[pl.BlockSpec((1, tr, C), lambda k, r, p: (k, p[0] * nr + r, 0)),
                      pl.BlockSpec((1, tr, C), lambda k, r, p: (k, r, 0))] * n,
            out_specs=[pl.BlockSpec((1, tr, C), lambda k, r, p: (k, r, 0))] * n),
        out_shape=[jax.ShapeDtypeStruct((N_CHIPS, half, C), bf16)] * n,
        compiler_params=_params(("parallel", "parallel")),
    )(place, *[x for pair in zip(grads, gots) for x in pair])
    return list(res)


def _rs_chips_phase(parts):
    n = len(parts)

    def copies(p, fc):
        x, y, c = _place()
        chips, ks = _other_chips(x, y)
        return [(p[a].at[ks[j]], fc[a].at[j], (*chips[j], c)) for a in range(n) for j in range(3)]

    shapes = [jax.ShapeDtypeStruct((3,) + q.shape[1:], q.dtype) for q in parts]
    return _Phase(parts, shapes, {}, 3 * n, copies, lambda p, fc: [fc[a].at[j] for a in range(n) for j in range(3)])


def _rs_hand_phase(parts, from_chips):
    n = len(parts)

    def copies(pin, fs):
        x, y, c = _place()
        sib = (x, y, 1 - c)
        own = [(pin[a].at[2 * x + y], fs[a].at[0], sib) for a in range(n)]
        return own + [(pin[n + a].at[j], fs[a].at[1 + j], sib) for a in range(n) for j in range(3)]

    def arrivals(pin, fs):
        return [fs[a].at[0] for a in range(n)] + [fs[a].at[1 + j] for a in range(n) for j in range(3)]

    shapes = [jax.ShapeDtypeStruct((4,) + q.shape[1:], q.dtype) for q in parts]
    return _Phase(list(parts) + list(from_chips), shapes, {}, 4 * n, copies, arrivals)


class _Exchange:
    def __init__(self, place):
        self.place = place

    def feed(self, buf):
        return _feed_phase(buf)

    def gather(self, bufs, over_ici):
        return _gather_phase(bufs, over_ici)

    def swap(self, grads):
        return _rs_swap_phase(grads)

    def pair_sums(self, names, grads):
        return self.add(names, grads, _run_phases("rs_sibling_" + names[0], [_rs_swap_phase(grads)]))

    def add(self, names, grads, got):
        parts = {}
        for group in _same_shape(grads):
            res = _rs_add_sibling(self.place, [grads[i] for i in group], [got[i] for i in group],
                                  "rs_add_" + names[group[0]])
            parts.update(zip(group, res))
        return [parts[i] for i in range(len(names))]

    def to_chips(self, parts):
        return _rs_chips_phase(parts)

    def to_sibling(self, parts, from_chips):
        return _rs_hand_phase(parts, from_chips)

    def spread(self, blk):
        return _spread_phase(blk)

    def hand_over(self, name, parts, from_chips, blk):
        got = _run_phases(name, [_join([_rs_hand_phase(parts, from_chips), _spread_phase(blk)])])
        return got[:-1], got[-1]


def _local_step(place, x, target, norm_mix, b_gate, rb_chip, norm_ffn, norm_final, w_in, rest, exch):
    B, S, D = x.shape
    T = B * S
    x2 = x.reshape(T, D)
    tg2 = target.reshape(T, D)
    rope, decay = _rope_tables(S), _decay_tables()
    g_fin = norm_final.reshape(1, D)
    nrel = rb_chip.shape[-1]

    mrg, ffn = ["w_ret_out", "w_att_out", "w_out"], ["w_ffn_gate", "w_ffn_up", "w_ffn_down"]
    (xn, proj), got = _in_proj(place, x2, norm_mix, _join([exch.feed(w_in), exch.gather([rest[n] for n in mrg], True),
                                                           exch.spread(jnp.pad(rb_chip, ((0, 0), (0, 128 - nrel))))]))
    w_in, wb, rb_all = got[0], {}, got.pop()
    trows = _bias_rows(jnp.concatenate([rb_all[2 * k, :, :nrel] for k in range(N_CHIPS)], axis=1))
    (qr, kr, o, u, states), got = _ret_fwd(proj, B, S, rope, decay, _join([exch.gather([rest["w_ffn_gate"]], True),
                                                                         exch.gather(got[1:], False)]))
    wb.update(zip(mrg, got[1:]))
    (ao, probs), got = _att_fwd(proj, trows, B, S, _join([exch.gather([rest["w_ffn_up"], rest["w_ffn_down"]], True),
                                                    exch.gather(got[:1], False)]))
    wb["w_ffn_gate"] = got[2]
    w_ro, w_out = wb["w_ret_out"].reshape(-1, D), wb["w_out"].reshape(-1, D)
    (h1, yr, ya), got = _mix_fwd(x2, proj, u, ao, b_gate, w_ro, wb["w_att_out"], w_out, exch.gather(got[:2], False))
    wb.update(zip(ffn[1:], got))
    hn, a, b, f, dh2, dh2b, part_fin = _ffn_fwd(h1, norm_ffn, wb["w_ffn_gate"], wb["w_ffn_up"], wb["w_ffn_down"], g_fin, tg2)

    da, db, dh1, dh1b, part_ffn = _ffn_bwd(dh2, h1, norm_ffn, a, b, wb["w_ffn_gate"], wb["w_ffn_up"], wb["w_ffn_down"])
    ffn = ["w_ffn_down", "w_ffn_gate", "w_ffn_up"]
    g_ffn = [_wgrad(f, dh2b, 0, "wgrad_ffn_down")[0], _wgrad(da, hn, 0, "wgrad_ffn_gate")[0],
             _wgrad(db, hn, 0, "wgrad_ffn_up")[0]]
    (du, dao, dgl, mix, dyr, dya, part_bg), x_ffn = _mix_bwd(dh1b, proj, yr, ya, b_gate, w_ro, wb["w_att_out"], w_out,
                                                             exch.swap(g_ffn))
    p_ffn = exch.add(ffn, g_ffn, x_ffn)
    mrg = ["w_out", "w_ret_out", "w_att_out"]
    g_mrg = [_wgrad(mix, dh1b, 0, "wgrad_out")[0], _wgrad(u, dyr, 0, "wgrad_ret_out")[0],
             _wgrad(ao, dya, 1, "wgrad_att_out")[0]]
    (dproj,), got = _ret_bwd(proj, qr, kr, o, states, du, dgl, B, S, rope, decay, _join([exch.to_chips(p_ffn[:2]),
                                                                                          exch.swap(g_mrg)]))
    c_two, p_mrg = got[:2], exch.add(mrg, g_mrg, got[2:])
    (dproj, dvec), got = _att_bwd(proj, dao, probs, dproj, B, S, exch.to_chips(p_ffn[2:] + p_mrg))
    c_ffn, c_mrg = c_two + got[:1], got[1:]
    g_in, got = _wgrad(xn, dproj, 1, "wgrad_in", exch.to_sibling(p_ffn + p_mrg, c_ffn + c_mrg))
    s_ffn, s_mrg = got[:len(ffn)], got[len(ffn):]
    p_in = exch.pair_sums(["w_in"], [g_in])
    (gx, part_mix), c_in = _in_proj_bwd(dproj, w_in, x2, norm_mix, dh1, exch.to_chips(p_in))
    rows = lambda p, r: p.reshape(-1, 8, p.shape[-1])[:, r, :].sum(axis=0)
    lo = KWIN - 1 - (MAX_REL - 1)
    drb = jnp.concatenate([jnp.flip(dvec[:, lo:lo + N_REL - 1], axis=1), dvec[:, :lo].sum(axis=1, keepdims=True)], axis=1)
    gsmall = {
        "norm_mix": rows(part_mix, 0), "b_gate": rows(part_bg, 0), "rel_bias": drb, "norm_ffn": rows(part_ffn, 0),
        "norm_final": rows(part_fin, 0),
    }
    s_in, small_all = exch.hand_over("rs_hand_w_in", p_in, c_in, _pack_small(gsmall, rows(part_fin, 1)))
    gbig = dict(zip(ffn + mrg + ["w_in"], zip(p_ffn + p_mrg + p_in, c_ffn + c_mrg + c_in, s_ffn + s_mrg + s_in)))
    return gx.reshape(B, S, D), gbig, small_all


SMALL_ROWS = 16


def _pack_small(gs, loss_lanes):
    D = D_MODEL
    rb = jnp.pad(gs["rel_bias"].reshape(-1), (0, 3 * D - ATT_HEADS * N_REL)).reshape(3, D)
    rows = [gs["norm_mix"].reshape(1, D), gs["b_gate"].reshape(2, D), gs["norm_ffn"].reshape(1, D),
            gs["norm_final"].reshape(1, D), rb, loss_lanes.reshape(1, D)]
    used = sum(r.shape[0] for r in rows)
    return jnp.concatenate(rows + [jnp.zeros((SMALL_ROWS - used, D), f32)], axis=0)


def kernel(x, norm_mix, w_in, b_gate, rel_bias, w_ret_out, w_att_out, w_out, norm_ffn, w_ffn_gate, w_ffn_up, w_ffn_down, norm_final, loss_target, m_norm_mix, m_w_in, m_b_gate, m_rel_bias, m_w_ret_out, m_w_att_out, m_w_out, m_norm_ffn, m_w_ffn_gate, m_w_ffn_up, m_w_ffn_down, m_norm_final, v_norm_mix, v_w_in, v_b_gate, v_rel_bias, v_w_ret_out, v_w_att_out, v_w_out, v_norm_ffn, v_w_ffn_gate, v_w_ffn_up, v_w_ffn_down, v_norm_final):
    w = dict(norm_mix=norm_mix, w_in=w_in, b_gate=b_gate, rel_bias=rel_bias, w_ret_out=w_ret_out, w_att_out=w_att_out,
             w_out=w_out, norm_ffn=norm_ffn, w_ffn_gate=w_ffn_gate, w_ffn_up=w_ffn_up, w_ffn_down=w_ffn_down,
             norm_final=norm_final)
    m = dict(norm_mix=m_norm_mix, w_in=m_w_in, b_gate=m_b_gate, rel_bias=m_rel_bias, w_ret_out=m_w_ret_out,
             w_att_out=m_w_att_out, w_out=m_w_out, norm_ffn=m_norm_ffn, w_ffn_gate=m_w_ffn_gate, w_ffn_up=m_w_ffn_up,
             w_ffn_down=m_w_ffn_down, norm_final=m_norm_final)
    v = dict(norm_mix=v_norm_mix, w_in=v_w_in, b_gate=v_b_gate, rel_bias=v_rel_bias, w_ret_out=v_w_ret_out,
             w_att_out=v_w_att_out, w_out=v_w_out, norm_ffn=v_norm_ffn, w_ffn_gate=v_w_ffn_gate, w_ffn_up=v_w_ffn_up,
             w_ffn_down=v_w_ffn_down, norm_final=v_norm_final)
    xi, yi, ci = _place()
    k_me = 2 * xi + yi

    place = jnp.stack([ci, k_me]).astype(jnp.int32)
    big = [n for n, _ in BIG]

    turned = ("w_ffn_gate", "w_ffn_up")
    shard = lambda d, n: jnp.swapaxes(d[n][0], 0, 1) if n in turned else d[n][0]
    whole = lambda a, n: (jnp.swapaxes(a, 0, 1) if n in turned else a)[None]

    by_shape = [[big[i] for i in group] for group in _same_shape([shard(w, n) for n in big])]
    bufs = {}
    for names in by_shape:
        bufs.update(zip(names, _cast_shards(place, [shard(w, n) for n in names], "cast_" + names[0])))
    rest = {n: bufs[n] for n in big if n != "w_in"}
    nrel_loc = rel_bias.shape[-1]
    grad_x, gbig, small_all = _local_step(place, x, loss_target, norm_mix, b_gate, rel_bias[0], norm_ffn, norm_final,
                                          bufs["w_in"], rest, _Exchange(place))

    small = _sum_slots(small_all, "reduce_small")
    D = D_MODEL
    loss = jnp.sum(small[8])
    drb_full = small[5:8].reshape(-1)[:ATT_HEADS * N_REL].reshape(ATT_HEADS, N_REL)
    g = {
        "norm_mix": small[0:1], "b_gate": small[1:3].reshape(1, 2 * D), "norm_ffn": small[3:4], "norm_final": small[4],
        "rel_bias": lax.dynamic_slice_in_dim(drb_full, k_me * nrel_loc, nrel_loc, axis=1)[None],
    }

    delta, new_m, new_v = {}, {}, {}
    for names in by_shape:
        res = _adamw_sum(place, [(shard(w, n), shard(m, n), shard(v, n), *gbig[n]) for n in names], "adamw_" + names[0])
        for n, (g_, d_, m_, v_) in zip(names, res):
            g[n], delta[n], new_m[n], new_v[n] = whole(g_, n), whole(d_, n), whole(m_, n), whole(v_, n)
    flat = lambda d: jnp.concatenate([d[n].reshape(-1) for n in SMALL])
    n_small = sum(int(np.prod(w[n].shape)) for n in SMALL)
    n_pad = -n_small % 1024
    packs = [jnp.pad(flat(d), (0, n_pad)).reshape(-1, 128) for d in (w, g, m, v)]
    outs = _adamw(*packs, "adamw_small")
    for res, dst in zip(outs, (delta, new_m, new_v)):
        off = 0
        fl = res.reshape(-1)
        for n in SMALL:
            sz = int(np.prod(w[n].shape))
            dst[n] = fl[off:off + sz].reshape(w[n].shape)
            off += sz

    return (loss, grad_x, *[g[n] for n in WEIGHTS], *[delta[n] for n in WEIGHTS], *[new_m[n] for n in WEIGHTS],
            *[new_v[n] for n in WEIGHTS])
```

```python
import functools

import numpy as np
import jax
import jax.numpy as jnp
from jax import lax
from jax.experimental import pallas as pl
from jax.experimental.pallas import tpu as pltpu

f32 = jnp.float32
bf16 = jnp.bfloat16

D_MODEL = 1024
CHUNK = 64
RET_HEADS = 4
RET_KEY_DIM = 128
RET_VAL_DIM = 256
ATT_HEADS = 8
ATT_HEAD_DIM = 64
ATT_W = ATT_HEADS * ATT_HEAD_DIM
BAND_CHUNKS = 8
PAD = BAND_CHUNKS * CHUNK
MAX_REL = 256
N_REL = CHUNK + MAX_REL
D_FF = 2816
N_IN = 6656
ROPE_BASE = 10000.0
EPS = 1e-6
NEG_INF = -1e30
C_RQ, C_RK, C_RV, C_RG, C_AQ, C_AK, C_AV, C_GL = 0, 512, 1024, 2048, 3072, 3584, 4096, 4608

ADAM_LR, ADAM_B1, ADAM_B2, ADAM_EPS, ADAM_WD, ADAM_STEP = 0.001, 0.9, 0.999, 1e-08, 0.01, 10

N_CHIPS = 4
N_DEV = 8
WGRAD_ACC_BYTES = 8 * 1024 * 1024
WGRAD_VMEM_BYTES = 40 * 1024 * 1024
ROW_TILE = 512
BIG_ROW_TILE = 1024
IN_ORDER = (0, 2, 3, 1)
QBLK = 256
KWIN = PAD + QBLK
TOEP = 1024
VMEM_LIMIT = 56 * 1024 * 1024
MESH = pl.DeviceIdType.MESH

BIG = (
    ("w_in", 1), ("w_ret_out", 0), ("w_att_out", 1), ("w_out", 0), ("w_ffn_gate", 1), ("w_ffn_up", 1), ("w_ffn_down", 0))
WEIGHTS = ("norm_mix", "w_in", "b_gate", "rel_bias", "w_ret_out", "w_att_out", "w_out", "norm_ffn", "w_ffn_gate",
           "w_ffn_up", "w_ffn_down", "norm_final")
SMALL = ("norm_mix", "b_gate", "rel_bias", "norm_ffn", "norm_final")


def _dot(a, b):
    return lax.dot_general(a, b, (((1,), (0,)), ((), ())), preferred_element_type=f32)


def _dot_nt(a, b):
    return lax.dot_general(a, b, (((1,), (1,)), ((), ())), preferred_element_type=f32)


def _dot_tn(a, b):
    return lax.dot_general(a, b, (((0,), (0,)), ((), ())), preferred_element_type=f32)


def _sig(x):
    return 1.0 / (1.0 + jnp.exp(-x))


def _tile(n, pref, mult):
    best = None
    for t in range(mult, min(n, pref) + 1, mult):
        if n % t == 0:
            best = t
    return best if best is not None else n


def _same_shape(arrays):
    groups = {}
    for i, a in enumerate(arrays):
        groups.setdefault(a.shape, []).append(i)
    return list(groups.values())


def _params(sem, vmem=VMEM_LIMIT):
    return pltpu.CompilerParams(dimension_semantics=sem, vmem_limit_bytes=vmem)


def _in_proj(place, x2, gamma, phase):
    T, D = x2.shape
    _, _, ns = phase.arrays[0].shape
    tm = _tile(T, BIG_ROW_TILE, 8)
    ni = T // tm
    pass_chip = lambda j: sum(jnp.where(j == n, f, 0) for n, f in enumerate(IN_ORDER))

    def body(p_ref, x_ref, g_ref, xn_ref, pr_ref, xs_ref, w_ref, w_sem, carried):
        j, i = pl.program_id(0), pl.program_id(1)
        pin, pout, sems = carried
        rows = pl.ds(pl.multiple_of(i * tm, tm), tm)

        @pl.when(i == 0)
        def _():
            for n, f in enumerate(IN_ORDER):
                if f:
                    @pl.when(j == n)
                    def _():
                        phase.arrived(f - 1, pin, pout, *sems)
                        phase.begin(2 + f, pin, pout, *sems)
                        phase.arrived(2 + f, pin, pout, *sems)
            shard = pltpu.make_async_copy(pout[0].at[jnp.bitwise_xor(p_ref[1], pass_chip(j))], w_ref, w_sem)
            shard.start()
            shard.wait()

        @pl.when(j == 0)
        def _():
            x = x_ref[...]
            r = lax.rsqrt(jnp.mean(x * x, axis=-1, keepdims=True) + EPS)
            xn = (x * r * g_ref[...]).astype(bf16)
            xs_ref[rows, :] = xn
            xn_ref[...] = xn

        pr_ref[...] = _dot(xs_ref[rows, :], w_ref[...]).astype(bf16)

    first_pass = lambda j, i, p: (jnp.where(j == 0, i, ni - 1), 0)
    return _call(
        body, phase, name="in_proj", grid=(N_CHIPS, ni), prefetch=(place,), expose=True,
        in_specs=[pl.BlockSpec((tm, D), first_pass), pl.BlockSpec((1, D), lambda j, i, p: (0, 0))],
        out_specs=[pl.BlockSpec((tm, D), first_pass),
                   pl.BlockSpec((tm, ns), lambda j, i, p: (i, jnp.bitwise_xor(p[1], pass_chip(j))))],
        out_shape=[jax.ShapeDtypeStruct((T, D), bf16), jax.ShapeDtypeStruct((T, N_CHIPS * ns), bf16)],
        scratch_shapes=[pltpu.VMEM((T, D), bf16), pltpu.VMEM((D, ns), bf16), pltpu.SemaphoreType.DMA],
        args=(x2, gamma))


def _rope_tables(S):
    d = RET_KEY_DIM
    freqs = (np.float32(ROPE_BASE) ** (-np.arange(0, d, 2, dtype=np.float32) / np.float32(d))).astype(np.float32)
    ang = np.arange(S, dtype=np.float32)[:, None] * freqs[None, :]
    cos, sin = np.cos(ang).astype(np.float32), np.sin(ang).astype(np.float32)
    return jnp.asarray(np.concatenate([cos, cos], axis=1)), jnp.asarray(np.concatenate([-sin, sin], axis=1))


def _decay_tables():
    H = RET_HEADS
    log_g = jnp.log(1.0 - 2.0 ** (-5.0 - jnp.arange(H, dtype=f32)))
    p = jnp.arange(CHUNK, dtype=f32)
    intra = jnp.exp(log_g[:, None, None] * jnp.abs(p[:, None] - p[None, :]))
    q_dec = jnp.exp(log_g[:, None] * (p[None, :] + 1.0))
    k_dec = jnp.exp(log_g[:, None] * (CHUNK - 1.0 - p[None, :]))
    c_dec = jnp.exp(log_g * CHUNK)
    q_dec = jnp.broadcast_to(q_dec[:, :, None], (H, CHUNK, RET_KEY_DIM))
    k_dec = jnp.broadcast_to(k_dec[:, :, None], (H, CHUNK, RET_KEY_DIM))
    c_dec = jnp.broadcast_to(c_dec[:, None, None], (H, 1, RET_VAL_DIM))
    return intra, q_dec, k_dec, c_dec


K_SCALE = RET_KEY_DIM ** -0.5


RET_CHUNKS = 4


def _ret_tables_specs():
    whole = lambda *shape: pl.BlockSpec(shape, lambda b, i: (0,) * len(shape))
    return [whole(RET_HEADS, CHUNK, CHUNK), whole(RET_HEADS, CHUNK, RET_KEY_DIM), whole(RET_HEADS, CHUNK, RET_KEY_DIM),
            whole(RET_HEADS, 1, RET_VAL_DIM)]


def _rotate(x, cos, sn):
    return x * cos + pltpu.roll(x, RET_KEY_DIM // 2, 1) * sn


def _ret_fwd(proj, B, S, rope, decay, phase=None):
    T = B * S
    nc = S // CHUNK
    H, dk, dv = RET_HEADS, RET_KEY_DIM, RET_VAL_DIM
    sb = RET_CHUNKS * CHUNK
    ns = S // sb

    def body(q_ref, k_ref, v_ref, g_ref, cos_ref, sin_ref, intra_ref, qd_ref, kd_ref, cd_ref,
             qr_ref, kr_ref, o_ref, u_ref, st_ref, state_ref):
        @pl.when(pl.program_id(1) == 0)
        def _():
            state_ref[...] = jnp.zeros_like(state_ref)

        cos, sn = cos_ref[...], sin_ref[...]
        for h in range(H):
            hs = slice(h * dk, (h + 1) * dk)
            qr_ref[:, hs] = _rotate(q_ref[:, hs].astype(f32), cos, sn).astype(bf16)
            kr_ref[:, hs] = (_rotate(k_ref[:, hs].astype(f32), cos, sn) * K_SCALE).astype(bf16)
        states = [state_ref[h] for h in range(H)]
        for ci in range(RET_CHUNKS):
            r = slice(ci * CHUNK, (ci + 1) * CHUNK)
            for h in range(H):
                hk, hv = slice(h * dk, (h + 1) * dk), slice(h * dv, (h + 1) * dv)
                qi, ki, vi = qr_ref[r, hk], kr_ref[r, hk], v_ref[r, hv]
                stb = states[h].astype(bf16)
                st_ref[0, h, ci] = stb
                s = (_dot_nt(qi, ki) * intra_ref[h]).astype(bf16)
                o = _dot(s, vi) + _dot((qi.astype(f32) * qd_ref[h]).astype(bf16), stb)
                states[h] = states[h] * cd_ref[h] + _dot_tn((ki.astype(f32) * kd_ref[h]).astype(bf16), vi)
                mu = jnp.mean(o, axis=-1, keepdims=True)
                xc = o - mu
                var = jnp.mean(xc * xc, axis=-1, keepdims=True)
                oh = xc * lax.rsqrt(var + EPS)
                g = g_ref[r, hv].astype(f32)
                o_ref[r, hv] = o.astype(bf16)
                u_ref[r, hv] = (g * _sig(g) * oh).astype(bf16)
        for h in range(H):
            state_ref[h] = states[h]

    blk = lambda w, c: pl.BlockSpec((sb, w), lambda b, i: (b * ns + i, c))
    return _call(
        body, phase, name="ret_fwd", grid=(B, ns), scratch_shapes=[pltpu.VMEM((H, dk, dv), f32)],
        in_specs=[blk(H * dk, C_RQ // (H * dk)), blk(H * dk, C_RK // (H * dk)), blk(H * dv, C_RV // (H * dv)),
                  blk(H * dv, C_RG // (H * dv)),
                  pl.BlockSpec((sb, dk), lambda b, i: (i, 0)), pl.BlockSpec((sb, dk), lambda b, i: (i, 0)),
                  *_ret_tables_specs()],
        out_specs=[blk(H * dk, 0), blk(H * dk, 0), blk(H * dv, 0), blk(H * dv, 0),
                   pl.BlockSpec((1, H, RET_CHUNKS, dk, dv), lambda b, i: (b, 0, i, 0, 0))],
        out_shape=[jax.ShapeDtypeStruct((T, H * dk), bf16), jax.ShapeDtypeStruct((T, H * dk), bf16),
                   jax.ShapeDtypeStruct((T, H * dv), bf16), jax.ShapeDtypeStruct((T, H * dv), bf16),
                   jax.ShapeDtypeStruct((B, H, nc, dk, dv), bf16)],
        args=(proj, proj, proj, proj, *rope, *decay))


def _bias_rows(rb):
    last = rb[:, N_REL - 1:]
    return jnp.concatenate([
        jnp.broadcast_to(last, (ATT_HEADS, PAD - MAX_REL + 1)),
        jnp.flip(rb[:, :N_REL - 1], axis=1),
        jnp.broadcast_to(rb[:, :1], (ATT_HEADS, KWIN - PAD - CHUNK)),
        jnp.broadcast_to(last, (ATT_HEADS, TOEP - KWIN)),
    ], axis=1)


def _build_bias(t_ref, bias_ref):
    row = lax.broadcasted_iota(jnp.int32, (QBLK, KWIN), 0) // CHUNK
    col = lax.broadcasted_iota(jnp.int32, (QBLK, KWIN), 1) // CHUNK
    delta = BAND_CHUNKS + row - col
    vis = (delta >= 0) & (delta <= BAND_CHUNKS)
    for h in range(ATT_HEADS):
        t = jnp.broadcast_to(t_ref[h:h + 1, :], (QBLK, TOEP))
        rolled = pltpu.roll(t, 0, 1, stride=1, stride_axis=0)
        bias_ref[h] = jnp.where(vis, rolled[:, :KWIN], NEG_INF)


ATT_SCALE = ATT_HEAD_DIM ** -0.5


def _att_probs(qh, kh, bias):
    s = _dot_nt(qh, kh) + bias
    m = jnp.max(s, axis=-1, keepdims=True)
    p = jnp.exp(s - m)
    return p * (1.0 / jnp.sum(p, axis=-1, keepdims=True))


def _first_of_pair():
    return lax.broadcasted_iota(jnp.int32, (1, 2 * ATT_HEAD_DIM), 1) < ATT_HEAD_DIM


def _by_window(i, step):
    sizes = list(range(QBLK, KWIN, QBLK))
    for n, nk in enumerate(sizes):
        pl.when(i == n)(functools.partial(step, nk))
    pl.when(i >= len(sizes))(functools.partial(step, KWIN))


def _att_fwd(proj, trows, B, S, phase=None):
    T = B * S
    nq = S // QBLK
    dh = ATT_HEAD_DIM

    def body(q_ref, k_ref, v_ref, t_ref, o_ref, p_ref, bias_ref):
        i = pl.program_id(1)

        @pl.when((pl.program_id(0) == 0) & (i == 0))
        def _():
            _build_bias(t_ref, bias_ref)

        def step(nk):
            win = pl.ds(pl.multiple_of((i + 1) * QBLK - nk, QBLK), nk)
            kw, vw = k_ref[win, :], v_ref[win, :]
            first = _first_of_pair()
            outs = []
            for p in range(ATT_HEADS // 2):
                ps = slice(2 * p * dh, 2 * (p + 1) * dh)
                q2, k2, v2 = q_ref[:, ps] * ATT_SCALE, kw[:, ps], vw[:, ps]
                both = []
                for e in range(2):
                    qm = jnp.where(first == (e == 0), q2, jnp.zeros_like(q2))
                    pr = _att_probs(qm, k2, bias_ref[2 * p + e, :, KWIN - nk:]).astype(bf16)
                    p_ref[0, 2 * p + e, :, KWIN - nk:] = pr
                    both.append(_dot(pr, v2))
                outs.append(jnp.where(first, both[0], both[1]))
            o_ref[...] = jnp.concatenate(outs, axis=1).astype(bf16)

        _by_window(i, step)

    return _call(
        body, phase, name="att_fwd", grid=(B, nq),
        in_specs=[pl.BlockSpec((QBLK, ATT_W), lambda b, i: (b * nq + i, C_AQ // ATT_W)),
                  pl.BlockSpec((S, ATT_W), lambda b, i: (b, C_AK // ATT_W)),
                  pl.BlockSpec((S, ATT_W), lambda b, i: (b, C_AV // ATT_W)),
                  pl.BlockSpec((ATT_HEADS, TOEP), lambda b, i: (0, 0))],
        out_specs=[pl.BlockSpec((QBLK, ATT_W), lambda b, i: (b * nq + i, 0)),
                   pl.BlockSpec((1, ATT_HEADS, QBLK, KWIN), lambda b, i: (b * nq + i, 0, 0, 0))],
        out_shape=[jax.ShapeDtypeStruct((T, ATT_W), bf16), jax.ShapeDtypeStruct((B * nq, ATT_HEADS, QBLK, KWIN), bf16)],
        scratch_shapes=[pltpu.VMEM((ATT_HEADS, QBLK, KWIN), f32)],
        args=(proj, proj, proj, trows))


def _gl_specs(tm):
    w = 512
    return [pl.BlockSpec((tm, w), functools.partial(lambda i, j: (i, C_GL // 512 + j), j=j)) for j in range(4)]


def _gates(gl_refs, bg_ref):
    gl = jnp.concatenate([r[...] for r in gl_refs], axis=1).astype(f32) + bg_ref[...]
    g = _sig(gl)
    return g[:, :D_MODEL], g[:, D_MODEL:]


def _mix_fwd(x2, proj, u, ao, b_gate, w_ro, w_ao, w_out, phase=None):
    T, D = x2.shape
    tm = _tile(T, ROW_TILE, 8)

    def body(x_ref, u_ref, ao_ref, g0, g1, g2, g3, bg_ref, wro_ref, wao_ref, wo_ref, h1_ref, yr_ref, ya_ref):
        yr = _dot(u_ref[...], wro_ref[...])
        ao = ao_ref[...]
        ya = jnp.concatenate([_dot(ao, wao_ref[k]) for k in range(N_CHIPS)], axis=1)
        gr, ga = _gates((g0, g1, g2, g3), bg_ref)
        mix = gr * yr + ga * ya
        h1_ref[...] = x_ref[...] + _dot(mix.astype(bf16), wo_ref[...])
        yr_ref[...] = yr.astype(bf16)
        ya_ref[...] = ya.astype(bf16)

    full = lambda a: pl.BlockSpec(a.shape, lambda i: (0,) * a.ndim)
    row = lambda n: pl.BlockSpec((tm, n), lambda i: (i, 0))
    return _call(
        body, phase, name="mix_fwd", grid=(T // tm,), scratch_shapes=[],
        in_specs=[row(D), row(D), row(ATT_W), *_gl_specs(tm), full(b_gate), full(w_ro), full(w_ao), full(w_out)],
        out_specs=[row(D), row(D), row(D)],
        out_shape=[jax.ShapeDtypeStruct((T, D), f32), jax.ShapeDtypeStruct((T, D), bf16),
                   jax.ShapeDtypeStruct((T, D), bf16)],
        args=(x2, u, ao, proj, proj, proj, proj, b_gate, w_ro, w_ao, w_out))


def _ffn_fwd(h1, g_ffn, wg, wu, wd, g_fin, target):
    T, D = h1.shape
    nf, tf, _ = wg.shape
    tm = _tile(T, ROW_TILE, 8)

    def body(h1_ref, g_ref, wg_ref, wu_ref, wd_ref, gf_ref, tg_ref, hn_ref, a_ref, b_ref, f_ref, dh2_ref, dh2b_ref,
             part_ref):
        h1v = h1_ref[...]
        r = lax.rsqrt(jnp.mean(h1v * h1v, axis=-1, keepdims=True) + EPS)
        hn = (h1v * r * g_ref[...]).astype(bf16)
        hn_ref[...] = hn
        h2 = h1v
        for k in range(nf):
            a = _dot_nt(hn, wg_ref[k])
            b = _dot_nt(hn, wu_ref[k])
            f = ((a * _sig(a)) * b).astype(bf16)
            a_ref[k] = a.astype(bf16)
            b_ref[k] = b.astype(bf16)
            f_ref[k] = f
            h2 = h2 + _dot(f, wd_ref[k])
        r = lax.rsqrt(jnp.mean(h2 * h2, axis=-1, keepdims=True) + EPS)
        n = h2 * r
        gf = gf_ref[...]
        e = n * gf - tg_ref[...]
        dy = e * (1.0 / D)
        dn = dy * gf
        dh2 = r * (dn - n * jnp.mean(dn * n, axis=-1, keepdims=True))
        dh2_ref[...] = dh2
        dh2b_ref[...] = dh2.astype(bf16)
        part_ref[...] = jnp.zeros_like(part_ref)
        part_ref[0:1, :] = jnp.sum(dy * n, axis=0, keepdims=True)
        part_ref[1:2, :] = (0.5 / D) * jnp.sum(e * e, axis=0, keepdims=True)

    row = lambda n: pl.BlockSpec((tm, n), lambda i: (i, 0))
    vec = pl.BlockSpec((1, D), lambda i: (0, 0))
    col = pl.BlockSpec((nf, tm, tf), lambda i: (0, i, 0))
    held = lambda w: pl.BlockSpec(w.shape, lambda i: (0, 0, 0), pipeline_mode=pl.Buffered(1))
    act = jax.ShapeDtypeStruct((nf, T, tf), bf16)
    return pl.pallas_call(
        body, name="ffn_fwd", grid=(T // tm,),
        in_specs=[row(D), vec, held(wg), held(wu), held(wd), vec, row(D)],
        out_specs=[row(D), col, col, col, row(D), row(D), pl.BlockSpec((8, D), lambda i: (i, 0))],
        out_shape=[jax.ShapeDtypeStruct((T, D), bf16), act, act, act, jax.ShapeDtypeStruct((T, D), f32),
                   jax.ShapeDtypeStruct((T, D), bf16), jax.ShapeDtypeStruct((T // tm * 8, D), f32)],
        compiler_params=_params(("parallel",)),
    )(h1, g_ffn, wg, wu, wd, g_fin, target)


def _ffn_bwd(dh2, h1, g_ffn, a, b, wg, wu, wd):
    T, D = h1.shape
    nf, tf, _ = wg.shape
    tm = _tile(T, ROW_TILE // 2, 8)

    def body(dh2_ref, h1_ref, g_ref, a_ref, b_ref, wg_ref, wu_ref, wd_ref, da_ref, db_ref, dh1_ref, dh1b_ref, part_ref):
        dh2v = dh2_ref[...]
        dh2b = dh2v.astype(bf16)
        dhn = jnp.zeros((tm, D), f32)
        for k in range(nf):
            df = _dot_nt(dh2b, wd_ref[k])
            av = a_ref[k].astype(f32)
            sg = _sig(av)
            db = (df * (av * sg)).astype(bf16)
            da = (df * b_ref[k].astype(f32) * (sg * (1.0 + av * (1.0 - sg)))).astype(bf16)
            da_ref[k] = da
            db_ref[k] = db
            dhn = dhn + _dot(da, wg_ref[k]) + _dot(db, wu_ref[k])
        h = h1_ref[...]
        r = lax.rsqrt(jnp.mean(h * h, axis=-1, keepdims=True) + EPS)
        n = h * r
        dn = dhn * g_ref[...]
        dh1 = dh2v + r * (dn - n * jnp.mean(dn * n, axis=-1, keepdims=True))
        dh1_ref[...] = dh1
        dh1b_ref[...] = dh1.astype(bf16)
        part_ref[...] = jnp.zeros_like(part_ref)
        part_ref[0:1, :] = jnp.sum(dhn * n, axis=0, keepdims=True)

    row = lambda n: pl.BlockSpec((tm, n), lambda i: (i, 0))
    col = pl.BlockSpec((nf, tm, tf), lambda i: (0, i, 0))
    held = lambda w: pl.BlockSpec(w.shape, lambda i: (0, 0, 0), pipeline_mode=pl.Buffered(1))
    act = jax.ShapeDtypeStruct((nf, T, tf), bf16)
    return pl.pallas_call(
        body, name="ffn_bwd", grid=(T // tm,),
        in_specs=[row(D), row(D), pl.BlockSpec((1, D), lambda i: (0, 0)), col, col, held(wg), held(wu), held(wd)],
        out_specs=[col, col, row(D), row(D), pl.BlockSpec((8, D), lambda i: (i, 0))],
        out_shape=[act, act, jax.ShapeDtypeStruct((T, D), f32), jax.ShapeDtypeStruct((T, D), bf16),
                   jax.ShapeDtypeStruct((T // tm * 8, D), f32)],
        compiler_params=_params(("parallel",)),
    )(dh2, h1, g_ffn, a, b, wg, wu, wd)


def _mix_bwd(dh1, proj, yr, ya, b_gate, w_ro, w_ao, w_out, phase=None):
    T, D = dh1.shape
    tm = _tile(T, ROW_TILE, 8)

    def body(dh1_ref, g0, g1, g2, g3, bg_ref, yr_ref, ya_ref, wro_ref, wao_ref, wo_ref,
             du_ref, dao_ref, dgl_ref, mix_ref, dyr_ref, dya_ref, part_ref):
        dmix = _dot_nt(dh1_ref[...], wo_ref[...])
        gr, ga = _gates((g0, g1, g2, g3), bg_ref)
        yr = yr_ref[...].astype(f32)
        ya = ya_ref[...].astype(f32)
        dyr = (dmix * gr).astype(bf16)
        dya = (dmix * ga).astype(bf16)
        dgl = jnp.concatenate([dmix * yr * gr * (1.0 - gr), dmix * ya * ga * (1.0 - ga)], axis=1)
        du_ref[...] = _dot_nt(dyr, wro_ref[...]).astype(bf16)
        ns = wao_ref.shape[2]
        dao = _dot_nt(dya[:, :ns], wao_ref[0])
        for k in range(1, N_CHIPS):
            dao = dao + _dot_nt(dya[:, k * ns:(k + 1) * ns], wao_ref[k])
        dao_ref[...] = dao.astype(bf16)
        dgl_ref[...] = dgl.astype(bf16)
        mix_ref[...] = (gr * yr + ga * ya).astype(bf16)
        dyr_ref[...] = dyr
        dya_ref[...] = dya
        part_ref[...] = jnp.zeros_like(part_ref)
        part_ref[0:1, :] = jnp.sum(dgl, axis=0, keepdims=True)

    full = lambda a: pl.BlockSpec(a.shape, lambda i: (0,) * a.ndim)
    row = lambda n: pl.BlockSpec((tm, n), lambda i: (i, 0))
    return _call(
        body, phase, name="mix_bwd", grid=(T // tm,), scratch_shapes=[],
        in_specs=[row(D), *_gl_specs(tm), full(b_gate), row(D), row(D), full(w_ro), full(w_ao), full(w_out)],
        out_specs=[row(D), row(ATT_W), row(2 * D), row(D), row(D), row(D), pl.BlockSpec((8, 2 * D), lambda i: (i, 0))],
        out_shape=[jax.ShapeDtypeStruct((T, D), bf16), jax.ShapeDtypeStruct((T, ATT_W), bf16),
                   jax.ShapeDtypeStruct((T, 2 * D), bf16), jax.ShapeDtypeStruct((T, D), bf16),
                   jax.ShapeDtypeStruct((T, D), bf16), jax.ShapeDtypeStruct((T, D), bf16),
                   jax.ShapeDtypeStruct((T // tm * 8, 2 * D), f32)],
        args=(dh1, proj, proj, proj, proj, b_gate, yr, ya, w_ro, w_ao, w_out))


def _ret_bwd(proj, qr, kr, o, states, du, dgl, B, S, rope, decay, phase=None):
    T = B * S
    H, dk, dv = RET_HEADS, RET_KEY_DIM, RET_VAL_DIM
    sb = RET_CHUNKS * CHUNK
    ns = S // sb

    def body(qr_ref, kr_ref, v_ref, g_ref, o_ref, st_ref, du_ref, dgl_ref, cos_ref, sin_ref, intra_ref, qd_ref, kd_ref,
             cd_ref, dp_ref, dstate_ref):
        dq_ref, dk_ref = dp_ref.at[:, pl.ds(C_RQ, H * dk)], dp_ref.at[:, pl.ds(C_RK, H * dk)]
        dv_ref, dg_ref = dp_ref.at[:, pl.ds(C_RV, H * dv)], dp_ref.at[:, pl.ds(C_RG, H * dv)]
        dp_ref[:, C_GL:] = dgl_ref[...]

        @pl.when(pl.program_id(1) == 0)
        def _():
            dstate_ref[...] = jnp.zeros_like(dstate_ref)

        cos, snb = cos_ref[...], -sin_ref[...]
        dstates = [dstate_ref[h] for h in range(H)]
        for ci in reversed(range(RET_CHUNKS)):
            r = slice(ci * CHUNK, (ci + 1) * CHUNK)
            for h in range(H):
                hk, hv = slice(h * dk, (h + 1) * dk), slice(h * dv, (h + 1) * dv)
                intra, qd, kd = intra_ref[h], qd_ref[h], kd_ref[h]
                qi, ki, vi = qr_ref[r, hk], kr_ref[r, hk], v_ref[r, hv]
                si = st_ref[0, h, ci]
                o = o_ref[r, hv].astype(f32)
                mu = jnp.mean(o, axis=-1, keepdims=True)
                xc = o - mu
                rstd = lax.rsqrt(jnp.mean(xc * xc, axis=-1, keepdims=True) + EPS)
                oh = xc * rstd
                g = g_ref[r, hv].astype(f32)
                sg = _sig(g)
                dui = du_ref[r, hv].astype(f32)
                dg_ref[r, hv] = (dui * oh * (sg * (1.0 + g * (1.0 - sg)))).astype(bf16)
                doh = dui * (g * sg)
                do = rstd * (doh - jnp.mean(doh, axis=-1, keepdims=True)
                             - oh * jnp.mean(doh * oh, axis=-1, keepdims=True))
                dob = do.astype(bf16)
                p = (_dot_nt(qi, ki) * intra).astype(bf16)
                dsb = dstates[h].astype(bf16)
                kt = (ki.astype(f32) * kd).astype(bf16)
                qt = (qi.astype(f32) * qd).astype(bf16)
                dv_ref[r, hv] = (_dot_tn(p, dob) + _dot(kt, dsb)).astype(bf16)
                da = (_dot_nt(dob, vi) * intra).astype(bf16)
                dq = _dot(da, ki) + _dot_nt(dob, si) * qd
                dkk = (_dot_tn(da, qi) + _dot_nt(vi, dsb) * kd) * K_SCALE
                dq_ref[r, hk] = _rotate(dq, cos[r], snb[r]).astype(bf16)
                dk_ref[r, hk] = _rotate(dkk, cos[r], snb[r]).astype(bf16)
                dstates[h] = dstates[h] * cd_ref[h] + _dot_tn(qt, dob)
        for h in range(H):
            dstate_ref[h] = dstates[h]

    blk = lambda w, c: pl.BlockSpec((sb, w), lambda b, i: (b * ns + ns - 1 - i, c))
    return _call(
        body, phase, name="ret_bwd", grid=(B, ns),
        in_specs=[blk(H * dk, 0), blk(H * dk, 0), blk(H * dv, C_RV // (H * dv)), blk(H * dv, C_RG // (H * dv)),
                  blk(H * dv, 0),
                  pl.BlockSpec((1, H, RET_CHUNKS, dk, dv), lambda b, i: (b, 0, ns - 1 - i, 0, 0)),
                  blk(H * dv, 0), blk(N_IN - C_GL, 0),
                  pl.BlockSpec((sb, dk), lambda b, i: (ns - 1 - i, 0)), pl.BlockSpec((sb, dk), lambda b, i: (ns - 1 - i, 0)),
                  *_ret_tables_specs()],
        out_specs=[blk(N_IN, 0)], out_shape=[jax.ShapeDtypeStruct((T, N_IN), bf16)],
        scratch_shapes=[pltpu.VMEM((H, dk, dv), f32)],
        args=(qr, kr, proj, proj, o, states, du, dgl, *rope, *decay))


def _att_bwd(proj, dao, probs, dproj, B, S, phase=None):
    T = B * S
    nq = S // QBLK
    dh = ATT_HEAD_DIM

    def body(q_ref, k_ref, v_ref, do_ref, p_ref, _, dp_ref, vec_ref, dbias_ref, dka_ref, dva_ref):
        b, i = pl.program_id(0), pl.program_id(1)

        @pl.when((b == 0) & (i == 0))
        def _():
            dbias_ref[...] = jnp.zeros_like(dbias_ref)

        @pl.when(i == 0)
        def _():
            dka_ref[...] = jnp.zeros_like(dka_ref)
            dva_ref[...] = jnp.zeros_like(dva_ref)

        def step(nk):
            win = pl.ds(pl.multiple_of((i + 1) * QBLK - nk, QBLK), nk)
            kw, vw = k_ref[win, :], v_ref[win, :]
            first = _first_of_pair()
            first_rows = lax.broadcasted_iota(jnp.int32, (2 * dh, 1), 0) < dh
            dqs, dks, dvs = [], [], []
            for p in range(ATT_HEADS // 2):
                ps = slice(2 * p * dh, 2 * (p + 1) * dh)
                q2, k2, v2, do2 = q_ref[:, ps] * ATT_SCALE, kw[:, ps], vw[:, ps], do_ref[:, ps]
                dq2, dk2, dv2 = [], [], []
                for e in range(2):
                    h = 2 * p + e
                    prb = p_ref[0, h, :, KWIN - nk:]
                    pr = prb.astype(f32)
                    dp = _dot_nt(jnp.where(first == (e == 0), do2, jnp.zeros_like(do2)), v2)
                    ds = pr * (dp - jnp.sum(pr * dp, axis=-1, keepdims=True))
                    dbias_ref[h, :, KWIN - nk:] += ds
                    dsb = ds.astype(bf16)
                    dq2.append(_dot(dsb, k2) * ATT_SCALE)
                    dk2.append(_dot_tn(q2, dsb))
                    dv2.append(_dot_tn(do2, prb))
                dqs.append(jnp.where(first, dq2[0], dq2[1]))
                dks.append(jnp.where(first_rows, dk2[0], dk2[1]))
                dvs.append(jnp.where(first_rows, dv2[0], dv2[1]))
            dp_ref[pl.ds(pl.multiple_of(i * QBLK, QBLK), QBLK), :ATT_W] = jnp.concatenate(dqs, axis=1).astype(bf16)
            dka_ref[:, win] += jnp.concatenate(dks, axis=0)
            dva_ref[:, win] += jnp.concatenate(dvs, axis=0)

        _by_window(i, step)

        @pl.when(i == nq - 1)
        def _():
            dp_ref[:, ATT_W:2 * ATT_W] = dka_ref[...].T.astype(bf16)
            dp_ref[:, 2 * ATT_W:] = dva_ref[...].T.astype(bf16)

        @pl.when((b == B - 1) & (i == nq - 1))
        def _():
            rr = lax.broadcasted_iota(jnp.int32, (QBLK, QBLK), 0)
            cc = lax.broadcasted_iota(jnp.int32, (QBLK, QBLK), 1)
            flip = jnp.where(rr + cc == QBLK - 1, 1.0, 0.0).astype(bf16)
            for h in range(ATT_HEADS):
                d = dbias_ref[h]
                hi = d.astype(bf16)
                lo = (d - hi.astype(f32)).astype(bf16)
                rev = _dot(flip, hi) + _dot(flip, lo)
                wide = jnp.concatenate([rev, jnp.zeros((QBLK, TOEP - KWIN), f32)], axis=1)
                rolled = pltpu.roll(wide, 0, 1, stride=1, stride_axis=0)
                vec_ref[h:h + 1, :] = jnp.sum(rolled, axis=0, keepdims=True)

    qspec = lambda c: pl.BlockSpec((QBLK, ATT_W), lambda b, i: (b * nq + i, c))
    kspec = lambda c: pl.BlockSpec((S, ATT_W), lambda b, i: (b, c))
    return _call(
        body, phase, name="att_bwd", grid=(B, nq), aliases={5: 0},
        in_specs=[qspec(C_AQ // ATT_W), kspec(C_AK // ATT_W), kspec(C_AV // ATT_W), qspec(0),
                  pl.BlockSpec((1, ATT_HEADS, QBLK, KWIN), lambda b, i: (b * nq + i, 0, 0, 0)),
                  pl.BlockSpec(memory_space=pl.ANY)],
        out_specs=[pl.BlockSpec((S, 3 * ATT_W), lambda b, i: (b, C_AQ // (3 * ATT_W))),
                   pl.BlockSpec((ATT_HEADS, TOEP), lambda b, i: (0, 0))],
        out_shape=[jax.ShapeDtypeStruct((T, N_IN), bf16), jax.ShapeDtypeStruct((ATT_HEADS, TOEP), f32)],
        scratch_shapes=[pltpu.VMEM((ATT_HEADS, QBLK, KWIN), f32), pltpu.VMEM((ATT_W, S), f32),
                        pltpu.VMEM((ATT_W, S), f32)],
        args=(proj, proj, proj, dao, probs, dproj))


def _in_proj_bwd(dproj, w_in, x2, gamma, dh1, phase=None):
    T, D = x2.shape
    nk, _, tk = w_in.shape
    tm = _tile(T, BIG_ROW_TILE, 8)

    def body(dp_ref, w_ref, x_ref, g_ref, dh1_ref, dx_ref, part_ref, acc_ref):
        j = pl.program_id(1)

        @pl.when(j == 0)
        def _():
            acc_ref[...] = jnp.zeros_like(acc_ref)

        acc_ref[...] += _dot_nt(dp_ref[...], w_ref[0])

        @pl.when(j == nk - 1)
        def _():
            x = x_ref[...]
            r = lax.rsqrt(jnp.mean(x * x, axis=-1, keepdims=True) + EPS)
            n = x * r
            dxn = acc_ref[...]
            dn = dxn * g_ref[...]
            dx_ref[...] = dh1_ref[...] + r * (dn - n * jnp.mean(dn * n, axis=-1, keepdims=True))
            part_ref[...] = jnp.zeros_like(part_ref)
            part_ref[0:1, :] = jnp.sum(dxn * n, axis=0, keepdims=True)

    row = lambda n: pl.BlockSpec((tm, n), lambda i, j: (i, 0))
    return _call(
        body, phase, name="in_proj_bwd", grid=(T // tm, nk),
        in_specs=[pl.BlockSpec((tm, tk), lambda i, j: (i, j)), pl.BlockSpec((1, D, tk), lambda i, j: (j, 0, 0)), row(D),
                  pl.BlockSpec((1, D), lambda i, j: (0, 0)), row(D)],
        out_specs=[row(D), pl.BlockSpec((8, D), lambda i, j: (i, 0))],
        out_shape=[jax.ShapeDtypeStruct((T, D), f32), jax.ShapeDtypeStruct((T // tm * 8, D), f32)],
        scratch_shapes=[pltpu.VMEM((tm, D), f32)],
        args=(dproj, w_in, x2, gamma, dh1))


def _wgrad(a, b, shard_axis, name, phase=None):
    def spec(arr, sharded, tt):
        if arr.ndim == 3:
            return arr.shape[2], pl.BlockSpec((1, tt, arr.shape[2]), lambda s, t: (s, t, 0))
        if sharded:
            w = arr.shape[1] // N_CHIPS
            return w, pl.BlockSpec((tt, w), lambda s, t: (t, s))
        return arr.shape[1], pl.BlockSpec((tt, arr.shape[1]), lambda s, t: (t, 0))

    T = a.shape[-2]
    whole = a.ndim == 2 and b.ndim == 2 and a.shape[1] * b.shape[1] * 4 <= WGRAD_ACC_BYTES
    width = lambda arr, sharded: arr.shape[-1] // (1 if whole or arr.ndim == 3 or not sharded else N_CHIPS)
    wa, wb_ = width(a, shard_axis == 0), width(b, shard_axis == 1)
    fixed = wa * wb_ * (4 + 2 * 2)
    tt = T // 4 if whole else T
    while tt > 256 and 2 * tt * (wa * a.dtype.itemsize + wb_ * b.dtype.itemsize) + fixed > WGRAD_VMEM_BYTES:
        tt //= 2
    nt = T // tt
    if whole:
        K, N = a.shape[1], b.shape[1]
        a_spec, b_spec = pl.BlockSpec((tt, K), lambda s, t: (t, 0)), pl.BlockSpec((tt, N), lambda s, t: (t, 0))
        out_block = (N_CHIPS, K // N_CHIPS, N) if shard_axis == 0 else (N_CHIPS, K, N // N_CHIPS)
        out_spec = pl.BlockSpec(out_block, lambda s, t: (0, 0, 0))
    else:
        K, a_spec = spec(a, shard_axis == 0, tt)
        N, b_spec = spec(b, shard_axis == 1, tt)
        out_block = (N_CHIPS, K, N)
        out_spec = pl.BlockSpec((1, K, N), lambda s, t: (s, 0, 0))

    def body(a_ref, b_ref, o_ref, acc_ref):
        t = pl.program_id(1)

        @pl.when(t == 0)
        def _():
            acc_ref[...] = jnp.zeros_like(acc_ref)

        av = a_ref[0] if a.ndim == 3 else a_ref[...]
        bv = b_ref[0] if b.ndim == 3 else b_ref[...]
        acc_ref[...] += _dot_tn(av.astype(bf16), bv.astype(bf16))

        @pl.when(t == nt - 1)
        def _():
            if not whole:
                o_ref[0] = acc_ref[...].astype(bf16)
            else:
                _, kk, nn = out_block
                for s in range(N_CHIPS):
                    o_ref[s] = (acc_ref[s * kk:(s + 1) * kk, :] if shard_axis == 0
                                else acc_ref[:, s * nn:(s + 1) * nn]).astype(bf16)

    (grad,), carried = _call(
        body, phase, name=name, grid=(1 if whole else N_CHIPS, nt), in_specs=[a_spec, b_spec], out_specs=[out_spec],
        out_shape=[jax.ShapeDtypeStruct(out_block, bf16)], scratch_shapes=[pltpu.VMEM((K, N), f32)], args=(a, b))
    return grad, carried


def _adamw_sum(place, groups, name):
    n = len(groups)
    R, C = groups[0][0].shape
    half = R // 2
    tr = _tile(half, max(16, (1 << 18) // C // 16 * 16), 16)
    nr = half // tr

    def body(p_ref, *refs):
        for a in range(n):
            w_ref, m_ref, v_ref, part_ref, fc_ref, fs_ref = refs[6 * a:6 * a + 6]
            g_ref, d_ref, mo_ref, vo_ref = refs[6 * n + 4 * a:6 * n + 4 * a + 4]
            up = lambda x: x.astype(f32)
            mine = ((up(part_ref[0]) + up(fc_ref[0])) + up(fc_ref[1])) + up(fc_ref[2])
            sibs = ((up(fs_ref[0]) + up(fs_ref[1])) + up(fs_ref[2])) + up(fs_ref[3])
            g_ = jnp.where(pl.program_id(0) == p_ref[0], mine, sibs)
            m_ = ADAM_B1 * m_ref[...] + (1.0 - ADAM_B1) * g_
            v_ = ADAM_B2 * v_ref[...] + (1.0 - ADAM_B2) * (g_ * g_)
            m_hat = m_ / (1.0 - ADAM_B1 ** ADAM_STEP)
            v_hat = v_ / (1.0 - ADAM_B2 ** ADAM_STEP)
            g_ref[...] = g_
            d_ref[...] = -ADAM_LR * (m_hat / (jnp.sqrt(v_hat) + ADAM_EPS) + ADAM_WD * w_ref[...])
            mo_ref[...] = m_
            vo_ref[...] = v_

    spec = pl.BlockSpec((tr, C), lambda h, r, p: (h * nr + r, 0))
    one = [spec, spec, spec, pl.BlockSpec((1, tr, C), lambda h, r, p: (p[1], jnp.where(h == p[0], r, 0), 0)),
           pl.BlockSpec((3, tr, C), lambda h, r, p: (0, jnp.where(h == p[0], r, 0), 0)),
           pl.BlockSpec((4, tr, C), lambda h, r, p: (0, jnp.where(h == p[0], 0, r), 0))]
    res = pl.pallas_call(
        body, name=name,
        grid_spec=pltpu.PrefetchScalarGridSpec(num_scalar_prefetch=1, grid=(2, nr), in_specs=one * n,
                                               out_specs=[spec] * (4 * n)),
        out_shape=[jax.ShapeDtypeStruct((R, C), f32)] * (4 * n),
        compiler_params=_params(("parallel", "parallel")),
    )(place, *[x for g in groups for x in g])
    return [tuple(res[4 * a:4 * a + 4]) for a in range(n)]


def _adamw(w, g, m, v, name):
    R, C = w.shape
    tr = _tile(R, max(8, (1 << 18) // C // 8 * 8), 8)

    def body(w_ref, g_ref, m_ref, v_ref, d_ref, mo_ref, vo_ref):
        g_ = g_ref[...]
        m_ = ADAM_B1 * m_ref[...] + (1.0 - ADAM_B1) * g_
        v_ = ADAM_B2 * v_ref[...] + (1.0 - ADAM_B2) * (g_ * g_)
        m_hat = m_ / (1.0 - ADAM_B1 ** ADAM_STEP)
        v_hat = v_ / (1.0 - ADAM_B2 ** ADAM_STEP)
        d_ref[...] = -ADAM_LR * (m_hat / (jnp.sqrt(v_hat) + ADAM_EPS) + ADAM_WD * w_ref[...])
        mo_ref[...] = m_
        vo_ref[...] = v_

    spec = pl.BlockSpec((tr, C), lambda i: (i, 0))
    return pl.pallas_call(
        body, name=name, grid=(R // tr,), in_specs=[spec] * 4, out_specs=[spec] * 3,
        out_shape=[jax.ShapeDtypeStruct((R, C), f32)] * 3,
        compiler_params=_params(("parallel",)),
    )(w, g, m, v)


def _place():
    return lax.axis_index("x"), lax.axis_index("y"), lax.axis_index("c")


def _other_chips(x, y):
    chips = [(1 - x, y), (x, 1 - y), (1 - x, 1 - y)]
    return chips, [2 * cx + cy for cx, cy in chips]


def _spread_phase(blk):
    def peers():
        x, y, c = _place()
        return [tuple(1 - p if (k >> s) & 1 else p for p, s in ((x, 2), (y, 1), (c, 0))) for k in range(1, N_DEV)]

    def copies(pin, out):
        x, y, c = _place()
        mine = out[0].at[4 * x + 2 * y + c]
        return [(mine, mine, peer) for peer in peers()]

    stack = jnp.broadcast_to(blk, (N_DEV,) + blk.shape)
    return _Phase([stack], [jax.ShapeDtypeStruct(stack.shape, stack.dtype)], {0: 0}, N_DEV - 1, copies,
                  lambda pin, out: [out[0].at[4 * px + 2 * py + pc] for px, py, pc in peers()])


def _sum_slots(stack, name):
    def body(s_ref, o_ref):
        tot = s_ref[0]
        for d in range(1, stack.shape[0]):
            tot = tot + s_ref[d]
        o_ref[...] = tot

    vm = pl.BlockSpec(memory_space=pltpu.VMEM)
    return pl.pallas_call(body, name=name, in_specs=[vm], out_specs=vm,
                          out_shape=jax.ShapeDtypeStruct(stack.shape[1:], stack.dtype))(stack)


def _cast_shards(place, ws, name):
    n = len(ws)
    R, C = ws[0].shape
    tr = _tile(R, max(16, (1 << 19) // C // 16 * 16), 16)

    def body(p_ref, *refs):
        for a in range(n):
            refs[n + a][0] = refs[a][...].astype(bf16)

    return pl.pallas_call(
        body, name=name,
        grid_spec=pltpu.PrefetchScalarGridSpec(
            num_scalar_prefetch=1, grid=(R // tr,),
            in_specs=[pl.BlockSpec((tr, C), lambda r, p: (r, 0))] * n,
            out_specs=[pl.BlockSpec((1, tr, C), lambda r, p: (p[1], r, 0))] * n),
        out_shape=[jax.ShapeDtypeStruct((N_CHIPS, R, C), bf16)] * n,
        compiler_params=_params(("parallel",)),
    )(place, *ws)


class _Phase:
    def __init__(self, arrays, out_shapes, aliases, n_copies, copies, arrivals, own_starts=(), own_waits=()):
        self.arrays, self.out_shapes, self.aliases = list(arrays), list(out_shapes), dict(aliases)
        self.n_copies, self.copies, self.arrivals = n_copies, copies, arrivals
        self.own_starts, self.own_waits = tuple(own_starts), tuple(own_waits)

    def sems(self):
        return [pltpu.SemaphoreType.DMA((self.n_copies,)), pltpu.SemaphoreType.DMA((self.n_copies,))]

    def _descriptors(self, pin, pout, send_sems, recv_sems):
        return [pltpu.make_async_remote_copy(src_ref=s, dst_ref=d, send_sem=send_sems.at[i], recv_sem=recv_sems.at[i],
                                             device_id=to, device_id_type=MESH)
                for i, (s, d, to) in enumerate(self.copies(pin, pout))]

    def _arrival(self, i, pin, pout, send_sems, recv_sems):
        dst = self.arrivals(pin, pout)[i]
        return pltpu.make_async_remote_copy(src_ref=dst, dst_ref=dst, send_sem=send_sems.at[i], recv_sem=recv_sems.at[i],
                                            device_id=_place(), device_id_type=MESH)

    def start(self, pin, pout, send_sems, recv_sems):
        for i, cp in enumerate(self._descriptors(pin, pout, send_sems, recv_sems)):
            if i not in self.own_starts:
                cp.start()

    def begin(self, i, pin, pout, send_sems, recv_sems):
        self._descriptors(pin, pout, send_sems, recv_sems)[i].start()

    def arrived(self, i, pin, pout, send_sems, recv_sems):
        self._arrival(i, pin, pout, send_sems, recv_sems).wait_recv()

    def finish(self, pin, pout, send_sems, recv_sems):
        for i in range(self.n_copies):
            if i not in self.own_waits:
                self._arrival(i, pin, pout, send_sems, recv_sems).wait_recv()
        for cp in self._descriptors(pin, pout, send_sems, recv_sems):
            cp.wait_send()


def _join(phases):
    if len(phases) == 1:
        return phases[0]
    ai = np.cumsum([0] + [len(p.arrays) for p in phases])
    oi = np.cumsum([0] + [len(p.out_shapes) for p in phases])

    def each(fn_name, pin, pout):
        return [item for k, p in enumerate(phases)
                for item in getattr(p, fn_name)(pin[ai[k]:ai[k + 1]], pout[oi[k]:oi[k + 1]])]

    aliases = {int(ai[k]) + i: int(oi[k]) + j for k, p in enumerate(phases) for i, j in p.aliases.items()}
    ci = np.cumsum([0] + [p.n_copies for p in phases])
    shifted = lambda attr: [int(ci[k]) + i for k, p in enumerate(phases) for i in getattr(p, attr)]
    return _Phase([a for p in phases for a in p.arrays], [s for p in phases for s in p.out_shapes], aliases,
                  int(ci[-1]), functools.partial(each, "copies"), functools.partial(each, "arrivals"),
                  shifted("own_starts"), shifted("own_waits"))


def _call(body, phase, *, name, grid, in_specs, out_specs, out_shape, scratch_shapes, args, prefetch=(), expose=False,
          aliases=None):
    seq = _params(("arbitrary",) * len(grid))
    np_ = len(prefetch)
    own = {np_ + i: j for i, j in (aliases or {}).items()}
    if phase is None:
        spec = pltpu.PrefetchScalarGridSpec(num_scalar_prefetch=np_, grid=grid, in_specs=in_specs, out_specs=out_specs,
                                            scratch_shapes=scratch_shapes)
        res = pl.pallas_call(body, name=name, grid_spec=spec, out_shape=out_shape, input_output_aliases=own,
                             compiler_params=seq)(*prefetch, *args)
        return list(res), []
    ni, no, ns = len(in_specs), len(out_specs), len(scratch_shapes)
    pi, po = len(phase.arrays), len(phase.out_shapes)

    def hosted(*refs):
        cut = np.cumsum([np_, ni, pi, no, po, ns])
        pre, ins, pin, outs, pout, scr, sems = (refs[a:b] for a, b in zip([0, *cut], [*cut, len(refs)]))
        ids = [pl.program_id(d) for d in range(len(grid))]
        first = functools.reduce(lambda p, q: p & q, [i == 0 for i in ids])
        last = functools.reduce(lambda p, q: p & q, [i == g - 1 for i, g in zip(ids, grid)])
        pl.when(first)(lambda: phase.start(pin, pout, *sems))
        body(*pre, *ins, *outs, *scr, **({"carried": (pin, pout, sems)} if expose else {}))
        pl.when(last)(lambda: phase.finish(pin, pout, *sems))

    anyspace = pl.BlockSpec(memory_space=pl.ANY)
    spec = pltpu.PrefetchScalarGridSpec(
        num_scalar_prefetch=np_, grid=grid, in_specs=list(in_specs) + [anyspace] * pi,
        out_specs=list(out_specs) + [anyspace] * po, scratch_shapes=list(scratch_shapes) + phase.sems())
    res = pl.pallas_call(
        hosted, name=name, grid_spec=spec, out_shape=list(out_shape) + phase.out_shapes,
        input_output_aliases={**own, **{np_ + ni + i: no + j for i, j in phase.aliases.items()}}, compiler_params=seq,
    )(*prefetch, *args, *phase.arrays)
    return list(res[:no]), list(res[no:])


def _run_phases(name, phases):
    first = phases[0]
    pi, po = len(first.arrays), len(first.out_shapes)

    def body(*refs):
        pin, pout, sems = refs[:pi], refs[pi:pi + po], refs[pi + po:]
        for n, ph in enumerate(phases):
            ph.start(pin, pout, *sems[2 * n:2 * n + 2])
            ph.finish(pin, pout, *sems[2 * n:2 * n + 2])

    anyspace = pl.BlockSpec(memory_space=pl.ANY)
    return list(pl.pallas_call(
        body, name=name, in_specs=[anyspace] * pi, out_specs=[anyspace] * po, out_shape=first.out_shapes,
        input_output_aliases=first.aliases, scratch_shapes=[s for ph in phases for s in ph.sems()],
    )(*first.arrays))


def _half_rows(buf, c):
    half = buf.shape[1] // 2
    return pl.ds(c * half, half), pl.ds((1 - c) * half, half)


def _gather_phase(bufs, over_ici):
    n = len(bufs)
    shapes = [jax.ShapeDtypeStruct(b.shape, b.dtype) for b in bufs]

    def landed(out, which):
        x, y, c = _place()
        _, ks = _other_chips(x, y)
        return [out[a].at[ks[j], _half_rows(bufs[a], c)[which]] for a in range(n) for j in range(3)]

    def ici(pin, out):
        x, y, c = _place()
        chips, _ = _other_chips(x, y)
        mine = [out[a].at[2 * x + y, _half_rows(bufs[a], c)[0]] for a in range(n)]
        return [(mine[a], mine[a], (*chips[j], c)) for a in range(n) for j in range(3)]

    def d2d(pin, out):
        x, y, c = _place()
        return [(dst, dst, (x, y, 1 - c)) for dst in landed(out, 0)]

    if over_ici:
        return _Phase(bufs, shapes, {a: a for a in range(n)}, 3 * n, ici, lambda pin, out: landed(out, 0))
    return _Phase(bufs, shapes, {a: a for a in range(n)}, 3 * n, d2d, lambda pin, out: landed(out, 1))


def _feed_phase(buf):
    def chips():
        x, y, _ = _place()
        return [(x if f < 2 else 1 - x, y if f % 2 == 0 else 1 - y) for f in (1, 2, 3)]

    def copies(pin, out):
        x, y, c = _place()
        mine = _half_rows(buf, c)[0]
        own = out[0].at[2 * x + y, mine]
        sent = [(own, own, (cx, cy, c)) for cx, cy in chips()]
        return sent + [(out[0].at[2 * cx + cy, mine], out[0].at[2 * cx + cy, mine], (x, y, 1 - c)) for cx, cy in chips()]

    def arrivals(pin, out):
        mine, theirs = _half_rows(buf, _place()[2])
        return [out[0].at[2 * cx + cy, rows] for rows in (mine, theirs) for cx, cy in chips()]

    return _Phase([buf], [jax.ShapeDtypeStruct(buf.shape, buf.dtype)], {0: 0}, 6, copies, arrivals,
                  own_starts=(3, 4, 5), own_waits=range(6))


def _rs_swap_phase(grads):
    n = len(grads)

    def copies(g, out):
        x, y, c = _place()
        return [(g[a].at[:, _half_rows(grads[a], c)[1]], out[a], (x, y, 1 - c)) for a in range(n)]

    shapes = [jax.ShapeDtypeStruct((N_CHIPS, g.shape[1] // 2, g.shape[2]), g.dtype) for g in grads]
    return _Phase(grads, shapes, {}, n, copies, lambda g, out: list(out))


def _rs_add_sibling(place, grads, gots, name):
    n = len(grads)
    _, R, C = grads[0].shape
    half = R // 2
    tr = _tile(half, max(16, (1 << 19) // C // 16 * 16), 16)
    nr = half // tr

    def body(p_ref, *refs):
        for a in range(n):
            refs[2 * n + a][...] = (refs[2 * a][...].astype(f32) + refs[2 * a + 1][...].astype(f32)).astype(bf16)

    res = pl.pallas_call(
        body, name=name,
        grid_spec=pltpu.PrefetchScalarGridSpec(
            num_scalar_prefetch=1, grid=(N_CHIPS, nr),
            in_specs=[pl.BlockSpec((1, tr, C), lambda k, r, p: (k, p[0] * nr + r, 0)),
                      pl.BlockSpec((1, tr, C), lambda k, r, p: (k, r, 0))] * n,
            out_specs=[pl.BlockSpec((1, tr, C), lambda k, r, p: (k, r, 0))] * n),
        out_shape=[jax.ShapeDtypeStruct((N_CHIPS, half, C), bf16)] * n,
        compiler_params=_params(("parallel", "parallel")),
    )(place, *[x for pair in zip(grads, gots) for x in pair])
    return list(res)


def _rs_chips_phase(parts):
    n = len(parts)

    def copies(p, fc):
        x, y, c = _place()
        chips, ks = _other_chips(x, y)
        return [(p[a].at[ks[j]], fc[a].at[j], (*chips[j], c)) for a in range(n) for j in range(3)]

    shapes = [jax.ShapeDtypeStruct((3,) + q.shape[1:], q.dtype) for q in parts]
    return _Phase(parts, shapes, {}, 3 * n, copies, lambda p, fc: [fc[a].at[j] for a in range(n) for j in range(3)])


def _rs_hand_phase(parts, from_chips):
    n = len(parts)

    def copies(pin, fs):
        x, y, c = _place()
        sib = (x, y, 1 - c)
        own = [(pin[a].at[2 * x + y], fs[a].at[0], sib) for a in range(n)]
        return own + [(pin[n + a].at[j], fs[a].at[1 + j], sib) for a in range(n) for j in range(3)]

    def arrivals(pin, fs):
        return [fs[a].at[0] for a in range(n)] + [fs[a].at[1 + j] for a in range(n) for j in range(3)]

    shapes = [jax.ShapeDtypeStruct((4,) + q.shape[1:], q.dtype) for q in parts]
    return _Phase(list(parts) + list(from_chips), shapes, {}, 4 * n, copies, arrivals)


class _Exchange:
    def __init__(self, place):
        self.place = place

    def feed(self, buf):
        return _feed_phase(buf)

    def gather(self, bufs, over_ici):
        return _gather_phase(bufs, over_ici)

    def swap(self, grads):
        return _rs_swap_phase(grads)

    def pair_sums(self, names, grads):
        return self.add(names, grads, _run_phases("rs_sibling_" + names[0], [_rs_swap_phase(grads)]))

    def add(self, names, grads, got):
        parts = {}
        for group in _same_shape(grads):
            res = _rs_add_sibling(self.place, [grads[i] for i in group], [got[i] for i in group],
                                  "rs_add_" + names[group[0]])
            parts.update(zip(group, res))
        return [parts[i] for i in range(len(names))]

    def to_chips(self, parts):
        return _rs_chips_phase(parts)

    def to_sibling(self, parts, from_chips):
        return _rs_hand_phase(parts, from_chips)

    def spread(self, blk):
        return _spread_phase(blk)

    def hand_over(self, name, parts, from_chips, blk):
        got = _run_phases(name, [_join([_rs_hand_phase(parts, from_chips), _spread_phase(blk)])])
        return got[:-1], got[-1]


def _local_step(place, x, target, norm_mix, b_gate, rb_chip, norm_ffn, norm_final, w_in, rest, exch):
    B, S, D = x.shape
    T = B * S
    x2 = x.reshape(T, D)
    tg2 = target.reshape(T, D)
    rope, decay = _rope_tables(S), _decay_tables()
    g_fin = norm_final.reshape(1, D)
    nrel = rb_chip.shape[-1]

    mrg, ffn = ["w_ret_out", "w_att_out", "w_out"], ["w_ffn_gate", "w_ffn_up", "w_ffn_down"]
    (xn, proj), got = _in_proj(place, x2, norm_mix, _join([exch.feed(w_in),
                                                           exch.spread(jnp.pad(rb_chip, ((0, 0), (0, 128 - nrel))))]))
    w_in, wb, rb_all = got[0], {}, got.pop()
    trows = _bias_rows(jnp.concatenate([rb_all[2 * k, :, :nrel] for k in range(N_CHIPS)], axis=1))
    (qr, kr, o, u, states), got = _ret_fwd(proj, B, S, rope, decay,
                                           exch.gather([rest["w_ffn_gate"]] + [rest[n] for n in mrg], True))
    (ao, probs), got = _att_fwd(proj, trows, B, S, _join([exch.gather([rest["w_ffn_up"], rest["w_ffn_down"]], True),
                                                    exch.gather(got, False)]))
    wb["w_ffn_gate"] = got[2]
    wb.update(zip(mrg, got[3:]))
    w_ro, w_out = wb["w_ret_out"].reshape(-1, D), wb["w_out"].reshape(-1, D)
    (h1, yr, ya), got = _mix_fwd(x2, proj, u, ao, b_gate, w_ro, wb["w_att_out"], w_out, exch.gather(got[:2], False))
    wb.update(zip(ffn[1:], got))
    hn, a, b, f, dh2, dh2b, part_fin = _ffn_fwd(h1, norm_ffn, wb["w_ffn_gate"], wb["w_ffn_up"], wb["w_ffn_down"], g_fin, tg2)

    da, db, dh1, dh1b, part_ffn = _ffn_bwd(dh2, h1, norm_ffn, a, b, wb["w_ffn_gate"], wb["w_ffn_up"], wb["w_ffn_down"])
    ffn = ["w_ffn_down", "w_ffn_gate", "w_ffn_up"]
    g_ffn = [_wgrad(f, dh2b, 0, "wgrad_ffn_down")[0], _wgrad(da, hn, 0, "wgrad_ffn_gate")[0],
             _wgrad(db, hn, 0, "wgrad_ffn_up")[0]]
    (du, dao, dgl, mix, dyr, dya, part_bg), x_ffn = _mix_bwd(dh1b, proj, yr, ya, b_gate, w_ro, wb["w_att_out"], w_out,
                                                             exch.swap(g_ffn))
    p_ffn = exch.add(ffn, g_ffn, x_ffn)
    mrg = ["w_out", "w_ret_out", "w_att_out"]
    g_mrg = [_wgrad(mix, dh1b, 0, "wgrad_out")[0], _wgrad(u, dyr, 0, "wgrad_ret_out")[0],
             _wgrad(ao, dya, 1, "wgrad_att_out")[0]]
    (dproj,), got = _ret_bwd(proj, qr, kr, o, states, du, dgl, B, S, rope, decay, _join([exch.to_chips(p_ffn[:2]),
                                                                                          exch.swap(g_mrg)]))
    c_two, p_mrg = got[:2], exch.add(mrg, g_mrg, got[2:])
    (dproj, dvec), got = _att_bwd(proj, dao, probs, dproj, B, S, exch.to_chips(p_ffn[2:] + p_mrg))
    c_ffn, c_mrg = c_two + got[:1], got[1:]
    g_in, got = _wgrad(xn, dproj, 1, "wgrad_in", exch.to_sibling(p_ffn + p_mrg, c_ffn + c_mrg))
    s_ffn, s_mrg = got[:len(ffn)], got[len(ffn):]
    p_in = exch.pair_sums(["w_in"], [g_in])
    (gx, part_mix), c_in = _in_proj_bwd(dproj, w_in, x2, norm_mix, dh1, exch.to_chips(p_in))
    rows = lambda p, r: p.reshape(-1, 8, p.shape[-1])[:, r, :].sum(axis=0)
    lo = KWIN - 1 - (MAX_REL - 1)
    drb = jnp.concatenate([jnp.flip(dvec[:, lo:lo + N_REL - 1], axis=1), dvec[:, :lo].sum(axis=1, keepdims=True)], axis=1)
    gsmall = {
        "norm_mix": rows(part_mix, 0), "b_gate": rows(part_bg, 0), "rel_bias": drb, "norm_ffn": rows(part_ffn, 0),
        "norm_final": rows(part_fin, 0),
    }
    s_in, small_all = exch.hand_over("rs_hand_w_in", p_in, c_in, _pack_small(gsmall, rows(part_fin, 1)))
    gbig = dict(zip(ffn + mrg + ["w_in"], zip(p_ffn + p_mrg + p_in, c_ffn + c_mrg + c_in, s_ffn + s_mrg + s_in)))
    return gx.reshape(B, S, D), gbig, small_all


SMALL_ROWS = 16


def _pack_small(gs, loss_lanes):
    D = D_MODEL
    rb = jnp.pad(gs["rel_bias"].reshape(-1), (0, 3 * D - ATT_HEADS * N_REL)).reshape(3, D)
    rows = [gs["norm_mix"].reshape(1, D), gs["b_gate"].reshape(2, D), gs["norm_ffn"].reshape(1, D),
            gs["norm_final"].reshape(1, D), rb, loss_lanes.reshape(1, D)]
    used = sum(r.shape[0] for r in rows)
    return jnp.concatenate(rows + [jnp.zeros((SMALL_ROWS - used, D), f32)], axis=0)


def kernel(x, norm_mix, w_in, b_gate, rel_bias, w_ret_out, w_att_out, w_out, norm_ffn, w_ffn_gate, w_ffn_up, w_ffn_down, norm_final, loss_target, m_norm_mix, m_w_in, m_b_gate, m_rel_bias, m_w_ret_out, m_w_att_out, m_w_out, m_norm_ffn, m_w_ffn_gate, m_w_ffn_up, m_w_ffn_down, m_norm_final, v_norm_mix, v_w_in, v_b_gate, v_rel_bias, v_w_ret_out, v_w_att_out, v_w_out, v_norm_ffn, v_w_ffn_gate, v_w_ffn_up, v_w_ffn_down, v_norm_final):
    w = dict(norm_mix=norm_mix, w_in=w_in, b_gate=b_gate, rel_bias=rel_bias, w_ret_out=w_ret_out, w_att_out=w_att_out,
             w_out=w_out, norm_ffn=norm_ffn, w_ffn_gate=w_ffn_gate, w_ffn_up=w_ffn_up, w_ffn_down=w_ffn_down,
             norm_final=norm_final)
    m = dict(norm_mix=m_norm_mix, w_in=m_w_in, b_gate=m_b_gate, rel_bias=m_rel_bias, w_ret_out=m_w_ret_out,
             w_att_out=m_w_att_out, w_out=m_w_out, norm_ffn=m_norm_ffn, w_ffn_gate=m_w_ffn_gate, w_ffn_up=m_w_ffn_up,
             w_ffn_down=m_w_ffn_down, norm_final=m_norm_final)
    v = dict(norm_mix=v_norm_mix, w_in=v_w_in, b_gate=v_b_gate, rel_bias=v_rel_bias, w_ret_out=v_w_ret_out,
             w_att_out=v_w_att_out, w_out=v_w_out, norm_ffn=v_norm_ffn, w_ffn_gate=v_w_ffn_gate, w_ffn_up=v_w_ffn_up,
             w_ffn_down=v_w_ffn_down, norm_final=v_norm_final)
    xi, yi, ci = _place()
    k_me = 2 * xi + yi

    place = jnp.stack([ci, k_me]).astype(jnp.int32)
    big = [n for n, _ in BIG]

    turned = ("w_ffn_gate", "w_ffn_up")
    shard = lambda d, n: jnp.swapaxes(d[n][0], 0, 1) if n in turned else d[n][0]
    whole = lambda a, n: (jnp.swapaxes(a, 0, 1) if n in turned else a)[None]

    by_shape = [[big[i] for i in group] for group in _same_shape([shard(w, n) for n in big])]
    bufs = {}
    for names in by_shape:
        bufs.update(zip(names, _cast_shards(place, [shard(w, n) for n in names], "cast_" + names[0])))
    rest = {n: bufs[n] for n in big if n != "w_in"}
    nrel_loc = rel_bias.shape[-1]
    grad_x, gbig, small_all = _local_step(place, x, loss_target, norm_mix, b_gate, rel_bias[0], norm_ffn, norm_final,
                                          bufs["w_in"], rest, _Exchange(place))

    small = _sum_slots(small_all, "reduce_small")
    D = D_MODEL
    loss = jnp.sum(small[8])
    drb_full = small[5:8].reshape(-1)[:ATT_HEADS * N_REL].reshape(ATT_HEADS, N_REL)
    g = {
        "norm_mix": small[0:1], "b_gate": small[1:3].reshape(1, 2 * D), "norm_ffn": small[3:4], "norm_final": small[4],
        "rel_bias": lax.dynamic_slice_in_dim(drb_full, k_me * nrel_loc, nrel_loc, axis=1)[None],
    }

    delta, new_m, new_v = {}, {}, {}
    for names in by_shape:
        res = _adamw_sum(place, [(shard(w, n), shard(m, n), shard(v, n), *gbig[n]) for n in names], "adamw_" + names[0])
        for n, (g_, d_, m_, v_) in zip(names, res):
            g[n], delta[n], new_m[n], new_v[n] = whole(g_, n), whole(d_, n), whole(m_, n), whole(v_, n)
    flat = lambda d: jnp.concatenate([d[n].reshape(-1) for n in SMALL])
    n_small = sum(int(np.prod(w[n].shape)) for n in SMALL)
    n_pad = -n_small % 1024
    packs = [jnp.pad(flat(d), (0, n_pad)).reshape(-1, 128) for d in (w, g, m, v)]
    outs = _adamw(*packs, "adamw_small")
    for res, dst in zip(outs, (delta, new_m, new_v)):
        off = 0
        fl = res.reshape(-1)
        for n in SMALL:
            sz = int(np.prod(w[n].shape))
            dst[n] = fl[off:off + sz].reshape(w[n].shape)
            off += sz

    return (loss, grad_x, *[g[n] for n in WEIGHTS], *[delta[n] for n in WEIGHTS], *[new_m[n] for n in WEIGHTS],
            *[new_v[n] for n in WEIGHTS])
```

```python
import functools

import numpy as np
import jax
import jax.numpy as jnp
from jax import lax
from jax.experimental import pallas as pl
from jax.experimental.pallas import tpu as pltpu

f32 = jnp.float32
bf16 = jnp.bfloat16

D_MODEL = 1024
CHUNK = 64
RET_HEADS = 4
RET_KEY_DIM = 128
RET_VAL_DIM = 256
ATT_HEADS = 8
ATT_HEAD_DIM = 64
ATT_W = ATT_HEADS * ATT_HEAD_DIM
BAND_CHUNKS = 8
PAD = BAND_CHUNKS * CHUNK
MAX_REL = 256
N_REL = CHUNK + MAX_REL
D_FF = 2816
N_IN = 6656
ROPE_BASE = 10000.0
EPS = 1e-6
NEG_INF = -1e30
C_RQ, C_RK, C_RV, C_RG, C_AQ, C_AK, C_AV, C_GL = 0, 512, 1024, 2048, 3072, 3584, 4096, 4608

ADAM_LR, ADAM_B1, ADAM_B2, ADAM_EPS, ADAM_WD, ADAM_STEP = 0.001, 0.9, 0.999, 1e-08, 0.01, 10

N_CHIPS = 4
N_DEV = 8
WGRAD_ACC_BYTES = 8 * 1024 * 1024
WGRAD_VMEM_BYTES = 40 * 1024 * 1024
ROW_TILE = 512
BIG_ROW_TILE = 1024
IN_ORDER = (0, 2, 3, 1)
QBLK = 256
KWIN = PAD + QBLK
TOEP = 1024
VMEM_LIMIT = 56 * 1024 * 1024
MESH = pl.DeviceIdType.MESH

BIG = (
    ("w_in", 1), ("w_ret_out", 0), ("w_att_out", 1), ("w_out", 0), ("w_ffn_gate", 1), ("w_ffn_up", 1), ("w_ffn_down", 0))
WEIGHTS = ("norm_mix", "w_in", "b_gate", "rel_bias", "w_ret_out", "w_att_out", "w_out", "norm_ffn", "w_ffn_gate",
           "w_ffn_up", "w_ffn_down", "norm_final")
SMALL = ("norm_mix", "b_gate", "rel_bias", "norm_ffn", "norm_final")


def _dot(a, b):
    return lax.dot_general(a, b, (((1,), (0,)), ((), ())), preferred_element_type=f32)


def _dot_nt(a, b):
    return lax.dot_general(a, b, (((1,), (1,)), ((), ())), preferred_element_type=f32)


def _dot_tn(a, b):
    return lax.dot_general(a, b, (((0,), (0,)), ((), ())), preferred_element_type=f32)


def _sig(x):
    return 1.0 / (1.0 + jnp.exp(-x))


def _tile(n, pref, mult):
    best = None
    for t in range(mult, min(n, pref) + 1, mult):
        if n % t == 0:
            best = t
    return best if best is not None else n


def _same_shape(arrays):
    groups = {}
    for i, a in enumerate(arrays):
        groups.setdefault(a.shape, []).append(i)
    return list(groups.values())


def _params(sem, vmem=VMEM_LIMIT):
    return pltpu.CompilerParams(dimension_semantics=sem, vmem_limit_bytes=vmem)


def _in_proj(place, x2, gamma, phase):
    T, D = x2.shape
    _, _, ns = phase.arrays[0].shape
    tm = _tile(T, BIG_ROW_TILE, 8)
    ni = T // tm
    pass_chip = lambda j: sum(jnp.where(j == n, f, 0) for n, f in enumerate(IN_ORDER))

    def body(p_ref, x_ref, g_ref, xn_ref, pr_ref, xs_ref, w_ref, w_sem, carried):
        j, i = pl.program_id(0), pl.program_id(1)
        pin, pout, sems = carried
        rows = pl.ds(pl.multiple_of(i * tm, tm), tm)

        @pl.when(i == 0)
        def _():
            for n, f in enumerate(IN_ORDER):
                if f:
                    @pl.when(j == n)
                    def _():
                        phase.arrived(f - 1, pin, pout, *sems)
                        phase.begin(2 + f, pin, pout, *sems)
                        phase.arrived(2 + f, pin, pout, *sems)
            shard = pltpu.make_async_copy(pout[0].at[jnp.bitwise_xor(p_ref[1], pass_chip(j))], w_ref, w_sem)
            shard.start()
            shard.wait()

        @pl.when(j == 0)
        def _():
            x = x_ref[...]
            r = lax.rsqrt(jnp.mean(x * x, axis=-1, keepdims=True) + EPS)
            xn = (x * r * g_ref[...]).astype(bf16)
            xs_ref[rows, :] = xn
            xn_ref[...] = xn

        pr_ref[...] = _dot(xs_ref[rows, :], w_ref[...]).astype(bf16)

    first_pass = lambda j, i, p: (jnp.where(j == 0, i, ni - 1), 0)
    return _call(
        body, phase, name="in_proj", grid=(N_CHIPS, ni), prefetch=(place,), expose=True,
        in_specs=[pl.BlockSpec((tm, D), first_pass), pl.BlockSpec((1, D), lambda j, i, p: (0, 0))],
        out_specs=[pl.BlockSpec((tm, D), first_pass),
                   pl.BlockSpec((tm, ns), lambda j, i, p: (i, jnp.bitwise_xor(p[1], pass_chip(j))))],
        out_shape=[jax.ShapeDtypeStruct((T, D), bf16), jax.ShapeDtypeStruct((T, N_CHIPS * ns), bf16)],
        scratch_shapes=[pltpu.VMEM((T, D), bf16), pltpu.VMEM((D, ns), bf16), pltpu.SemaphoreType.DMA],
        args=(x2, gamma))


def _rope_tables(S):
    d = RET_KEY_DIM
    freqs = (np.float32(ROPE_BASE) ** (-np.arange(0, d, 2, dtype=np.float32) / np.float32(d))).astype(np.float32)
    ang = np.arange(S, dtype=np.float32)[:, None] * freqs[None, :]
    cos, sin = np.cos(ang).astype(np.float32), np.sin(ang).astype(np.float32)
    return jnp.asarray(np.concatenate([cos, cos], axis=1)), jnp.asarray(np.concatenate([-sin, sin], axis=1))


def _decay_tables():
    H = RET_HEADS
    log_g = jnp.log(1.0 - 2.0 ** (-5.0 - jnp.arange(H, dtype=f32)))
    p = jnp.arange(CHUNK, dtype=f32)
    intra = jnp.exp(log_g[:, None, None] * jnp.abs(p[:, None] - p[None, :]))
    q_dec = jnp.exp(log_g[:, None] * (p[None, :] + 1.0))
    k_dec = jnp.exp(log_g[:, None] * (CHUNK - 1.0 - p[None, :]))
    c_dec = jnp.exp(log_g * CHUNK)
    q_dec = jnp.broadcast_to(q_dec[:, :, None], (H, CHUNK, RET_KEY_DIM))
    k_dec = jnp.broadcast_to(k_dec[:, :, None], (H, CHUNK, RET_KEY_DIM))
    c_dec = jnp.broadcast_to(c_dec[:, None, None], (H, 1, RET_VAL_DIM))
    return intra, q_dec, k_dec, c_dec


K_SCALE = RET_KEY_DIM ** -0.5


RET_CHUNKS = 4


def _ret_tables_specs():
    whole = lambda *shape: pl.BlockSpec(shape, lambda b, i: (0,) * len(shape))
    return [whole(RET_HEADS, CHUNK, CHUNK), whole(RET_HEADS, CHUNK, RET_KEY_DIM), whole(RET_HEADS, CHUNK, RET_KEY_DIM),
            whole(RET_HEADS, 1, RET_VAL_DIM)]


def _rotate(x, cos, sn):
    return x * cos + pltpu.roll(x, RET_KEY_DIM // 2, 1) * sn


def _ret_fwd(proj, B, S, rope, decay, phase=None):
    T = B * S
    nc = S // CHUNK
    H, dk, dv = RET_HEADS, RET_KEY_DIM, RET_VAL_DIM
    sb = RET_CHUNKS * CHUNK
    ns = S // sb

    def body(q_ref, k_ref, v_ref, g_ref, cos_ref, sin_ref, intra_ref, qd_ref, kd_ref, cd_ref,
             qr_ref, kr_ref, o_ref, u_ref, st_ref, state_ref):
        @pl.when(pl.program_id(1) == 0)
        def _():
            state_ref[...] = jnp.zeros_like(state_ref)

        cos, sn = cos_ref[...], sin_ref[...]
        for h in range(H):
            hs = slice(h * dk, (h + 1) * dk)
            qr_ref[:, hs] = _rotate(q_ref[:, hs].astype(f32), cos, sn).astype(bf16)
            kr_ref[:, hs] = (_rotate(k_ref[:, hs].astype(f32), cos, sn) * K_SCALE).astype(bf16)
        states = [state_ref[h] for h in range(H)]
        for ci in range(RET_CHUNKS):
            r = slice(ci * CHUNK, (ci + 1) * CHUNK)
            for h in range(H):
                hk, hv = slice(h * dk, (h + 1) * dk), slice(h * dv, (h + 1) * dv)
                qi, ki, vi = qr_ref[r, hk], kr_ref[r, hk], v_ref[r, hv]
                stb = states[h].astype(bf16)
                st_ref[0, h, ci] = stb
                s = (_dot_nt(qi, ki) * intra_ref[h]).astype(bf16)
                o = _dot(s, vi) + _dot((qi.astype(f32) * qd_ref[h]).astype(bf16), stb)
                states[h] = states[h] * cd_ref[h] + _dot_tn((ki.astype(f32) * kd_ref[h]).astype(bf16), vi)
                mu = jnp.mean(o, axis=-1, keepdims=True)
                xc = o - mu
                var = jnp.mean(xc * xc, axis=-1, keepdims=True)
                oh = xc * lax.rsqrt(var + EPS)
                g = g_ref[r, hv].astype(f32)
                o_ref[r, hv] = o.astype(bf16)
                u_ref[r, hv] = (g * _sig(g) * oh).astype(bf16)
        for h in range(H):
            state_ref[h] = states[h]

    blk = lambda w, c: pl.BlockSpec((sb, w), lambda b, i: (b * ns + i, c))
    return _call(
        body, phase, name="ret_fwd", grid=(B, ns), scratch_shapes=[pltpu.VMEM((H, dk, dv), f32)],
        in_specs=[blk(H * dk, C_RQ // (H * dk)), blk(H * dk, C_RK // (H * dk)), blk(H * dv, C_RV // (H * dv)),
                  blk(H * dv, C_RG // (H * dv)),
                  pl.BlockSpec((sb, dk), lambda b, i: (i, 0)), pl.BlockSpec((sb, dk), lambda b, i: (i, 0)),
                  *_ret_tables_specs()],
        out_specs=[blk(H * dk, 0), blk(H * dk, 0), blk(H * dv, 0), blk(H * dv, 0),
                   pl.BlockSpec((1, H, RET_CHUNKS, dk, dv), lambda b, i: (b, 0, i, 0, 0))],
        out_shape=[jax.ShapeDtypeStruct((T, H * dk), bf16), jax.ShapeDtypeStruct((T, H * dk), bf16),
                   jax.ShapeDtypeStruct((T, H * dv), bf16), jax.ShapeDtypeStruct((T, H * dv), bf16),
                   jax.ShapeDtypeStruct((B, H, nc, dk, dv), bf16)],
        args=(proj, proj, proj, proj, *rope, *decay))


def _bias_rows(rb):
    last = rb[:, N_REL - 1:]
    return jnp.concatenate([
        jnp.broadcast_to(last, (ATT_HEADS, PAD - MAX_REL + 1)),
        jnp.flip(rb[:, :N_REL - 1], axis=1),
        jnp.broadcast_to(rb[:, :1], (ATT_HEADS, KWIN - PAD - CHUNK)),
        jnp.broadcast_to(last, (ATT_HEADS, TOEP - KWIN)),
    ], axis=1)


def _build_bias(t_ref, bias_ref):
    row = lax.broadcasted_iota(jnp.int32, (QBLK, KWIN), 0) // CHUNK
    col = lax.broadcasted_iota(jnp.int32, (QBLK, KWIN), 1) // CHUNK
    delta = BAND_CHUNKS + row - col
    vis = (delta >= 0) & (delta <= BAND_CHUNKS)
    for h in range(ATT_HEADS):
        t = jnp.broadcast_to(t_ref[h:h + 1, :], (QBLK, TOEP))
        rolled = pltpu.roll(t, 0, 1, stride=1, stride_axis=0)
        bias_ref[h] = jnp.where(vis, rolled[:, :KWIN], NEG_INF)


ATT_SCALE = ATT_HEAD_DIM ** -0.5


def _att_probs(qh, kh, bias):
    s = _dot_nt(qh, kh) + bias
    m = jnp.max(s, axis=-1, keepdims=True)
    p = jnp.exp(s - m)
    return p * (1.0 / jnp.sum(p, axis=-1, keepdims=True))


def _first_of_pair():
    return lax.broadcasted_iota(jnp.int32, (1, 2 * ATT_HEAD_DIM), 1) < ATT_HEAD_DIM


def _by_window(i, step):
    sizes = list(range(QBLK, KWIN, QBLK))
    for n, nk in enumerate(sizes):
        pl.when(i == n)(functools.partial(step, nk))
    pl.when(i >= len(sizes))(functools.partial(step, KWIN))


def _att_fwd(proj, trows, B, S, phase=None):
    T = B * S
    nq = S // QBLK
    dh = ATT_HEAD_DIM

    def body(q_ref, k_ref, v_ref, t_ref, o_ref, p_ref, bias_ref):
        i = pl.program_id(1)

        @pl.when((pl.program_id(0) == 0) & (i == 0))
        def _():
            _build_bias(t_ref, bias_ref)

        def step(nk):
            win = pl.ds(pl.multiple_of((i + 1) * QBLK - nk, QBLK), nk)
            kw, vw = k_ref[win, :], v_ref[win, :]
            first = _first_of_pair()
            outs = []
            for p in range(ATT_HEADS // 2):
                ps = slice(2 * p * dh, 2 * (p + 1) * dh)
                q2, k2, v2 = q_ref[:, ps] * ATT_SCALE, kw[:, ps], vw[:, ps]
                both = []
                for e in range(2):
                    qm = jnp.where(first == (e == 0), q2, jnp.zeros_like(q2))
                    pr = _att_probs(qm, k2, bias_ref[2 * p + e, :, KWIN - nk:]).astype(bf16)
                    p_ref[0, 2 * p + e, :, KWIN - nk:] = pr
                    both.append(_dot(pr, v2))
                outs.append(jnp.where(first, both[0], both[1]))
            o_ref[...] = jnp.concatenate(outs, axis=1).astype(bf16)

        _by_window(i, step)

    return _call(
        body, phase, name="att_fwd", grid=(B, nq),
        in_specs=[pl.BlockSpec((QBLK, ATT_W), lambda b, i: (b * nq + i, C_AQ // ATT_W)),
                  pl.BlockSpec((S, ATT_W), lambda b, i: (b, C_AK // ATT_W)),
                  pl.BlockSpec((S, ATT_W), lambda b, i: (b, C_AV // ATT_W)),
                  pl.BlockSpec((ATT_HEADS, TOEP), lambda b, i: (0, 0))],
        out_specs=[pl.BlockSpec((QBLK, ATT_W), lambda b, i: (b * nq + i, 0)),
                   pl.BlockSpec((1, ATT_HEADS, QBLK, KWIN), lambda b, i: (b * nq + i, 0, 0, 0))],
        out_shape=[jax.ShapeDtypeStruct((T, ATT_W), bf16), jax.ShapeDtypeStruct((B * nq, ATT_HEADS, QBLK, KWIN), bf16)],
        scratch_shapes=[pltpu.VMEM((ATT_HEADS, QBLK, KWIN), f32)],
        args=(proj, proj, proj, trows))


def _gl_specs(tm):
    w = 512
    return [pl.BlockSpec((tm, w), functools.partial(lambda i, j: (i, C_GL // 512 + j), j=j)) for j in range(4)]


def _gates(gl_refs, bg_ref):
    gl = jnp.concatenate([r[...] for r in gl_refs], axis=1).astype(f32) + bg_ref[...]
    g = _sig(gl)
    return g[:, :D_MODEL], g[:, D_MODEL:]


def _mix_fwd(x2, proj, u, ao, b_gate, w_ro, w_ao, w_out, phase=None):
    T, D = x2.shape
    tm = _tile(T, ROW_TILE, 8)

    def body(x_ref, u_ref, ao_ref, g0, g1, g2, g3, bg_ref, wro_ref, wao_ref, wo_ref, h1_ref, yr_ref, ya_ref):
        yr = _dot(u_ref[...], wro_ref[...])
        ao = ao_ref[...]
        ya = jnp.concatenate([_dot(ao, wao_ref[k]) for k in range(N_CHIPS)], axis=1)
        gr, ga = _gates((g0, g1, g2, g3), bg_ref)
        mix = gr * yr + ga * ya
        h1_ref[...] = x_ref[...] + _dot(mix.astype(bf16), wo_ref[...])
        yr_ref[...] = yr.astype(bf16)
        ya_ref[...] = ya.astype(bf16)

    full = lambda a: pl.BlockSpec(a.shape, lambda i: (0,) * a.ndim)
    row = lambda n: pl.BlockSpec((tm, n), lambda i: (i, 0))
    return _call(
        body, phase, name="mix_fwd", grid=(T // tm,), scratch_shapes=[],
        in_specs=[row(D), row(D), row(ATT_W), *_gl_specs(tm), full(b_gate), full(w_ro), full(w_ao), full(w_out)],
        out_specs=[row(D), row(D), row(D)],
        out_shape=[jax.ShapeDtypeStruct((T, D), f32), jax.ShapeDtypeStruct((T, D), bf16),
                   jax.ShapeDtypeStruct((T, D), bf16)],
        args=(x2, u, ao, proj, proj, proj, proj, b_gate, w_ro, w_ao, w_out))


def _ffn_fwd(h1, g_ffn, wg, wu, wd, g_fin, target):
    T, D = h1.shape
    nf, tf, _ = wg.shape
    tm = _tile(T, ROW_TILE, 8)

    def body(h1_ref, g_ref, wg_ref, wu_ref, wd_ref, gf_ref, tg_ref, hn_ref, a_ref, b_ref, f_ref, dh2_ref, dh2b_ref,
             part_ref):
        h1v = h1_ref[...]
        r = lax.rsqrt(jnp.mean(h1v * h1v, axis=-1, keepdims=True) + EPS)
        hn = (h1v * r * g_ref[...]).astype(bf16)
        hn_ref[...] = hn
        h2 = h1v
        for k in range(nf):
            a = _dot_nt(hn, wg_ref[k])
            b = _dot_nt(hn, wu_ref[k])
            f = ((a * _sig(a)) * b).astype(bf16)
            a_ref[k] = a.astype(bf16)
            b_ref[k] = b.astype(bf16)
            f_ref[k] = f
            h2 = h2 + _dot(f, wd_ref[k])
        r = lax.rsqrt(jnp.mean(h2 * h2, axis=-1, keepdims=True) + EPS)
        n = h2 * r
        gf = gf_ref[...]
        e = n * gf - tg_ref[...]
        dy = e * (1.0 / D)
        dn = dy * gf
        dh2 = r * (dn - n * jnp.mean(dn * n, axis=-1, keepdims=True))
        dh2_ref[...] = dh2
        dh2b_ref[...] = dh2.astype(bf16)
        part_ref[...] = jnp.zeros_like(part_ref)
        part_ref[0:1, :] = jnp.sum(dy * n, axis=0, keepdims=True)
        part_ref[1:2, :] = (0.5 / D) * jnp.sum(e * e, axis=0, keepdims=True)

    row = lambda n: pl.BlockSpec((tm, n), lambda i: (i, 0))
    vec = pl.BlockSpec((1, D), lambda i: (0, 0))
    col = pl.BlockSpec((nf, tm, tf), lambda i: (0, i, 0))
    held = lambda w: pl.BlockSpec(w.shape, lambda i: (0, 0, 0), pipeline_mode=pl.Buffered(1))
    act = jax.ShapeDtypeStruct((nf, T, tf), bf16)
    return pl.pallas_call(
        body, name="ffn_fwd", grid=(T // tm,),
        in_specs=[row(D), vec, held(wg), held(wu), held(wd), vec, row(D)],
        out_specs=[row(D), col, col, col, row(D), row(D), pl.BlockSpec((8, D), lambda i: (i, 0))],
        out_shape=[jax.ShapeDtypeStruct((T, D), bf16), act, act, act, jax.ShapeDtypeStruct((T, D), f32),
                   jax.ShapeDtypeStruct((T, D), bf16), jax.ShapeDtypeStruct((T // tm * 8, D), f32)],
        compiler_params=_params(("parallel",)),
    )(h1, g_ffn, wg, wu, wd, g_fin, target)


def _ffn_bwd(dh2, h1, g_ffn, a, b, wg, wu, wd):
    T, D = h1.shape
    nf, tf, _ = wg.shape
    tm = _tile(T, ROW_TILE // 2, 8)

    def body(dh2_ref, h1_ref, g_ref, a_ref, b_ref, wg_ref, wu_ref, wd_ref, da_ref, db_ref, dh1_ref, dh1b_ref, part_ref):
        dh2v = dh2_ref[...]
        dh2b = dh2v.astype(bf16)
        dhn = jnp.zeros((tm, D), f32)
        for k in range(nf):
            df = _dot_nt(dh2b, wd_ref[k])
            av = a_ref[k].astype(f32)
            sg = _sig(av)
            db = (df * (av * sg)).astype(bf16)
            da = (df * b_ref[k].astype(f32) * (sg * (1.0 + av * (1.0 - sg)))).astype(bf16)
            da_ref[k] = da
            db_ref[k] = db
            dhn = dhn + _dot(da, wg_ref[k]) + _dot(db, wu_ref[k])
        h = h1_ref[...]
        r = lax.rsqrt(jnp.mean(h * h, axis=-1, keepdims=True) + EPS)
        n = h * r
        dn = dhn * g_ref[...]
        dh1 = dh2v + r * (dn - n * jnp.mean(dn * n, axis=-1, keepdims=True))
        dh1_ref[...] = dh1
        dh1b_ref[...] = dh1.astype(bf16)
        part_ref[...] = jnp.zeros_like(part_ref)
        part_ref[0:1, :] = jnp.sum(dhn * n, axis=0, keepdims=True)

    row = lambda n: pl.BlockSpec((tm, n), lambda i: (i, 0))
    col = pl.BlockSpec((nf, tm, tf), lambda i: (0, i, 0))
    held = lambda w: pl.BlockSpec(w.shape, lambda i: (0, 0, 0), pipeline_mode=pl.Buffered(1))
    act = jax.ShapeDtypeStruct((nf, T, tf), bf16)
    return pl.pallas_call(
        body, name="ffn_bwd", grid=(T // tm,),
        in_specs=[row(D), row(D), pl.BlockSpec((1, D), lambda i: (0, 0)), col, col, held(wg), held(wu), held(wd)],
        out_specs=[col, col, row(D), row(D), pl.BlockSpec((8, D), lambda i: (i, 0))],
        out_shape=[act, act, jax.ShapeDtypeStruct((T, D), f32), jax.ShapeDtypeStruct((T, D), bf16),
                   jax.ShapeDtypeStruct((T // tm * 8, D), f32)],
        compiler_params=_params(("parallel",)),
    )(dh2, h1, g_ffn, a, b, wg, wu, wd)


def _mix_bwd(dh1, proj, yr, ya, b_gate, w_ro, w_ao, w_out, phase=None):
    T, D = dh1.shape
    tm = _tile(T, ROW_TILE, 8)

    def body(dh1_ref, g0, g1, g2, g3, bg_ref, yr_ref, ya_ref, wro_ref, wao_ref, wo_ref,
             du_ref, dao_ref, dgl_ref, mix_ref, dyr_ref, dya_ref, part_ref):
        dmix = _dot_nt(dh1_ref[...], wo_ref[...])
        gr, ga = _gates((g0, g1, g2, g3), bg_ref)
        yr = yr_ref[...].astype(f32)
        ya = ya_ref[...].astype(f32)
        dyr = (dmix * gr).astype(bf16)
        dya = (dmix * ga).astype(bf16)
        dgl = jnp.concatenate([dmix * yr * gr * (1.0 - gr), dmix * ya * ga * (1.0 - ga)], axis=1)
        du_ref[...] = _dot_nt(dyr, wro_ref[...]).astype(bf16)
        ns = wao_ref.shape[2]
        dao = _dot_nt(dya[:, :ns], wao_ref[0])
        for k in range(1, N_CHIPS):
            dao = dao + _dot_nt(dya[:, k * ns:(k + 1) * ns], wao_ref[k])
        dao_ref[...] = dao.astype(bf16)
        dgl_ref[...] = dgl.astype(bf16)
        mix_ref[...] = (gr * yr + ga * ya).astype(bf16)
        dyr_ref[...] = dyr
        dya_ref[...] = dya
        part_ref[...] = jnp.zeros_like(part_ref)
        part_ref[0:1, :] = jnp.sum(dgl, axis=0, keepdims=True)

    full = lambda a: pl.BlockSpec(a.shape, lambda i: (0,) * a.ndim)
    row = lambda n: pl.BlockSpec((tm, n), lambda i: (i, 0))
    return _call(
        body, phase, name="mix_bwd", grid=(T // tm,), scratch_shapes=[],
        in_specs=[row(D), *_gl_specs(tm), full(b_gate), row(D), row(D), full(w_ro), full(w_ao), full(w_out)],
        out_specs=[row(D), row(ATT_W), row(2 * D), row(D), row(D), row(D), pl.BlockSpec((8, 2 * D), lambda i: (i, 0))],
        out_shape=[jax.ShapeDtypeStruct((T, D), bf16), jax.ShapeDtypeStruct((T, ATT_W), bf16),
                   jax.ShapeDtypeStruct((T, 2 * D), bf16), jax.ShapeDtypeStruct((T, D), bf16),
                   jax.ShapeDtypeStruct((T, D), bf16), jax.ShapeDtypeStruct((T, D), bf16),
                   jax.ShapeDtypeStruct((T // tm * 8, 2 * D), f32)],
        args=(dh1, proj, proj, proj, proj, b_gate, yr, ya, w_ro, w_ao, w_out))


def _ret_bwd(proj, qr, kr, o, states, du, dgl, B, S, rope, decay, phase=None):
    T = B * S
    H, dk, dv = RET_HEADS, RET_KEY_DIM, RET_VAL_DIM
    sb = RET_CHUNKS * CHUNK
    ns = S // sb

    def body(qr_ref, kr_ref, v_ref, g_ref, o_ref, st_ref, du_ref, dgl_ref, cos_ref, sin_ref, intra_ref, qd_ref, kd_ref,
             cd_ref, dp_ref, dstate_ref):
        dq_ref, dk_ref = dp_ref.at[:, pl.ds(C_RQ, H * dk)], dp_ref.at[:, pl.ds(C_RK, H * dk)]
        dv_ref, dg_ref = dp_ref.at[:, pl.ds(C_RV, H * dv)], dp_ref.at[:, pl.ds(C_RG, H * dv)]
        dp_ref[:, C_GL:] = dgl_ref[...]

        @pl.when(pl.program_id(1) == 0)
        def _():
            dstate_ref[...] = jnp.zeros_like(dstate_ref)

        cos, snb = cos_ref[...], -sin_ref[...]
        dstates = [dstate_ref[h] for h in range(H)]
        for ci in reversed(range(RET_CHUNKS)):
            r = slice(ci * CHUNK, (ci + 1) * CHUNK)
            for h in range(H):
                hk, hv = slice(h * dk, (h + 1) * dk), slice(h * dv, (h + 1) * dv)
                intra, qd, kd = intra_ref[h], qd_ref[h], kd_ref[h]
                qi, ki, vi = qr_ref[r, hk], kr_ref[r, hk], v_ref[r, hv]
                si = st_ref[0, h, ci]
                o = o_ref[r, hv].astype(f32)
                mu = jnp.mean(o, axis=-1, keepdims=True)
                xc = o - mu
                rstd = lax.rsqrt(jnp.mean(xc * xc, axis=-1, keepdims=True) + EPS)
                oh = xc * rstd
                g = g_ref[r, hv].astype(f32)
                sg = _sig(g)
                dui = du_ref[r, hv].astype(f32)
                dg_ref[r, hv] = (dui * oh * (sg * (1.0 + g * (1.0 - sg)))).astype(bf16)
                doh = dui * (g * sg)
                do = rstd * (doh - jnp.mean(doh, axis=-1, keepdims=True)
                             - oh * jnp.mean(doh * oh, axis=-1, keepdims=True))
                dob = do.astype(bf16)
                p = (_dot_nt(qi, ki) * intra).astype(bf16)
                dsb = dstates[h].astype(bf16)
                kt = (ki.astype(f32) * kd).astype(bf16)
                qt = (qi.astype(f32) * qd).astype(bf16)
                dv_ref[r, hv] = (_dot_tn(p, dob) + _dot(kt, dsb)).astype(bf16)
                da = (_dot_nt(dob, vi) * intra).astype(bf16)
                dq = _dot(da, ki) + _dot_nt(dob, si) * qd
                dkk = (_dot_tn(da, qi) + _dot_nt(vi, dsb) * kd) * K_SCALE
                dq_ref[r, hk] = _rotate(dq, cos[r], snb[r]).astype(bf16)
                dk_ref[r, hk] = _rotate(dkk, cos[r], snb[r]).astype(bf16)
                dstates[h] = dstates[h] * cd_ref[h] + _dot_tn(qt, dob)
        for h in range(H):
            dstate_ref[h] = dstates[h]

    blk = lambda w, c: pl.BlockSpec((sb, w), lambda b, i: (b * ns + ns - 1 - i, c))
    return _call(
        body, phase, name="ret_bwd", grid=(B, ns),
        in_specs=[blk(H * dk, 0), blk(H * dk, 0), blk(H * dv, C_RV // (H * dv)), blk(H * dv, C_RG // (H * dv)),
                  blk(H * dv, 0),
                  pl.BlockSpec((1, H, RET_CHUNKS, dk, dv), lambda b, i: (b, 0, ns - 1 - i, 0, 0)),
                  blk(H * dv, 0), blk(N_IN - C_GL, 0),
                  pl.BlockSpec((sb, dk), lambda b, i: (ns - 1 - i, 0)), pl.BlockSpec((sb, dk), lambda b, i: (ns - 1 - i, 0)),
                  *_ret_tables_specs()],
        out_specs=[blk(N_IN, 0)], out_shape=[jax.ShapeDtypeStruct((T, N_IN), bf16)],
        scratch_shapes=[pltpu.VMEM((H, dk, dv), f32)],
        args=(qr, kr, proj, proj, o, states, du, dgl, *rope, *decay))


def _att_bwd(proj, dao, probs, dproj, B, S, phase=None):
    T = B * S
    nq = S // QBLK
    dh = ATT_HEAD_DIM

    def body(q_ref, k_ref, v_ref, do_ref, p_ref, _, dp_ref, vec_ref, dbias_ref, dka_ref, dva_ref):
        b, i = pl.program_id(0), pl.program_id(1)

        @pl.when((b == 0) & (i == 0))
        def _():
            dbias_ref[...] = jnp.zeros_like(dbias_ref)

        @pl.when(i == 0)
        def _():
            dka_ref[...] = jnp.zeros_like(dka_ref)
            dva_ref[...] = jnp.zeros_like(dva_ref)

        def step(nk):
            win = pl.ds(pl.multiple_of((i + 1) * QBLK - nk, QBLK), nk)
            kw, vw = k_ref[win, :], v_ref[win, :]
            first = _first_of_pair()
            first_rows = lax.broadcasted_iota(jnp.int32, (2 * dh, 1), 0) < dh
            dqs, dks, dvs = [], [], []
            for p in range(ATT_HEADS // 2):
                ps = slice(2 * p * dh, 2 * (p + 1) * dh)
                q2, k2, v2, do2 = q_ref[:, ps] * ATT_SCALE, kw[:, ps], vw[:, ps], do_ref[:, ps]
                dq2, dk2, dv2 = [], [], []
                for e in range(2):
                    h = 2 * p + e
                    prb = p_ref[0, h, :, KWIN - nk:]
                    pr = prb.astype(f32)
                    dp = _dot_nt(jnp.where(first == (e == 0), do2, jnp.zeros_like(do2)), v2)
                    ds = pr * (dp - jnp.sum(pr * dp, axis=-1, keepdims=True))
                    dbias_ref[h, :, KWIN - nk:] += ds
                    dsb = ds.astype(bf16)
                    dq2.append(_dot(dsb, k2) * ATT_SCALE)
                    dk2.append(_dot_tn(q2, dsb))
                    dv2.append(_dot_tn(do2, prb))
                dqs.append(jnp.where(first, dq2[0], dq2[1]))
                dks.append(jnp.where(first_rows, dk2[0], dk2[1]))
                dvs.append(jnp.where(first_rows, dv2[0], dv2[1]))
            dp_ref[pl.ds(pl.multiple_of(i * QBLK, QBLK), QBLK), :ATT_W] = jnp.concatenate(dqs, axis=1).astype(bf16)
            dka_ref[:, win] += jnp.concatenate(dks, axis=0)
            dva_ref[:, win] += jnp.concatenate(dvs, axis=0)

        _by_window(i, step)

        @pl.when(i == nq - 1)
        def _():
            dp_ref[:, ATT_W:2 * ATT_W] = dka_ref[...].T.astype(bf16)
            dp_ref[:, 2 * ATT_W:] = dva_ref[...].T.astype(bf16)

        @pl.when((b == B - 1) & (i == nq - 1))
        def _():
            rr = lax.broadcasted_iota(jnp.int32, (QBLK, QBLK), 0)
            cc = lax.broadcasted_iota(jnp.int32, (QBLK, QBLK), 1)
            flip = jnp.where(rr + cc == QBLK - 1, 1.0, 0.0).astype(bf16)
            for h in range(ATT_HEADS):
                d = dbias_ref[h]
                hi = d.astype(bf16)
                lo = (d - hi.astype(f32)).astype(bf16)
                rev = _dot(flip, hi) + _dot(flip, lo)
                wide = jnp.concatenate([rev, jnp.zeros((QBLK, TOEP - KWIN), f32)], axis=1)
                rolled = pltpu.roll(wide, 0, 1, stride=1, stride_axis=0)
                vec_ref[h:h + 1, :] = jnp.sum(rolled, axis=0, keepdims=True)

    qspec = lambda c: pl.BlockSpec((QBLK, ATT_W), lambda b, i: (b * nq + i, c))
    kspec = lambda c: pl.BlockSpec((S, ATT_W), lambda b, i: (b, c))
    return _call(
        body, phase, name="att_bwd", grid=(B, nq), aliases={5: 0},
        in_specs=[qspec(C_AQ // ATT_W), kspec(C_AK // ATT_W), kspec(C_AV // ATT_W), qspec(0),
                  pl.BlockSpec((1, ATT_HEADS, QBLK, KWIN), lambda b, i: (b * nq + i, 0, 0, 0)),
                  pl.BlockSpec(memory_space=pl.ANY)],
        out_specs=[pl.BlockSpec((S, 3 * ATT_W), lambda b, i: (b, C_AQ // (3 * ATT_W))),
                   pl.BlockSpec((ATT_HEADS, TOEP), lambda b, i: (0, 0))],
        out_shape=[jax.ShapeDtypeStruct((T, N_IN), bf16), jax.ShapeDtypeStruct((ATT_HEADS, TOEP), f32)],
        scratch_shapes=[pltpu.VMEM((ATT_HEADS, QBLK, KWIN), f32), pltpu.VMEM((ATT_W, S), f32),
                        pltpu.VMEM((ATT_W, S), f32)],
        args=(proj, proj, proj, dao, probs, dproj))


def _in_proj_bwd(dproj, w_in, x2, gamma, dh1, phase=None):
    T, D = x2.shape
    nk, _, tk = w_in.shape
    tm = _tile(T, BIG_ROW_TILE, 8)

    def body(dp_ref, w_ref, x_ref, g_ref, dh1_ref, dx_ref, part_ref, acc_ref):
        j = pl.program_id(1)

        @pl.when(j == 0)
        def _():
            acc_ref[...] = jnp.zeros_like(acc_ref)

        acc_ref[...] += _dot_nt(dp_ref[...], w_ref[0])

        @pl.when(j == nk - 1)
        def _():
            x = x_ref[...]
            r = lax.rsqrt(jnp.mean(x * x, axis=-1, keepdims=True) + EPS)
            n = x * r
            dxn = acc_ref[...]
            dn = dxn * g_ref[...]
            dx_ref[...] = dh1_ref[...] + r * (dn - n * jnp.mean(dn * n, axis=-1, keepdims=True))
            part_ref[...] = jnp.zeros_like(part_ref)
            part_ref[0:1, :] = jnp.sum(dxn * n, axis=0, keepdims=True)

    row = lambda n: pl.BlockSpec((tm, n), lambda i, j: (i, 0))
    return _call(
        body, phase, name="in_proj_bwd", grid=(T // tm, nk),
        in_specs=[pl.BlockSpec((tm, tk), lambda i, j: (i, j)), pl.BlockSpec((1, D, tk), lambda i, j: (j, 0, 0)), row(D),
                  pl.BlockSpec((1, D), lambda i, j: (0, 0)), row(D)],
        out_specs=[row(D), pl.BlockSpec((8, D), lambda i, j: (i, 0))],
        out_shape=[jax.ShapeDtypeStruct((T, D), f32), jax.ShapeDtypeStruct((T // tm * 8, D), f32)],
        scratch_shapes=[pltpu.VMEM((tm, D), f32)],
        args=(dproj, w_in, x2, gamma, dh1))


def _wgrad(a, b, shard_axis, name, phase=None):
    def spec(arr, sharded, tt):
        if arr.ndim == 3:
            return arr.shape[2], pl.BlockSpec((1, tt, arr.shape[2]), lambda s, t: (s, t, 0))
        if sharded:
            w = arr.shape[1] // N_CHIPS
            return w, pl.BlockSpec((tt, w), lambda s, t: (t, s))
        return arr.shape[1], pl.BlockSpec((tt, arr.shape[1]), lambda s, t: (t, 0))

    T = a.shape[-2]
    whole = a.ndim == 2 and b.ndim == 2 and a.shape[1] * b.shape[1] * 4 <= WGRAD_ACC_BYTES
    width = lambda arr, sharded: arr.shape[-1] // (1 if whole or arr.ndim == 3 or not sharded else N_CHIPS)
    wa, wb_ = width(a, shard_axis == 0), width(b, shard_axis == 1)
    fixed = wa * wb_ * (4 + 2 * 2)
    tt = T // 4 if whole else T
    while tt > 256 and 2 * tt * (wa * a.dtype.itemsize + wb_ * b.dtype.itemsize) + fixed > WGRAD_VMEM_BYTES:
        tt //= 2
    nt = T // tt
    if whole:
        K, N = a.shape[1], b.shape[1]
        a_spec, b_spec = pl.BlockSpec((tt, K), lambda s, t: (t, 0)), pl.BlockSpec((tt, N), lambda s, t: (t, 0))
        out_block = (N_CHIPS, K // N_CHIPS, N) if shard_axis == 0 else (N_CHIPS, K, N // N_CHIPS)
        out_spec = pl.BlockSpec(out_block, lambda s, t: (0, 0, 0))
    else:
        K, a_spec = spec(a, shard_axis == 0, tt)
        N, b_spec = spec(b, shard_axis == 1, tt)
        out_block = (N_CHIPS, K, N)
        out_spec = pl.BlockSpec((1, K, N), lambda s, t: (s, 0, 0))

    def body(a_ref, b_ref, o_ref, acc_ref):
        t = pl.program_id(1)

        @pl.when(t == 0)
        def _():
            acc_ref[...] = jnp.zeros_like(acc_ref)

        av = a_ref[0] if a.ndim == 3 else a_ref[...]
        bv = b_ref[0] if b.ndim == 3 else b_ref[...]
        acc_ref[...] += _dot_tn(av.astype(bf16), bv.astype(bf16))

        @pl.when(t == nt - 1)
        def _():
            if not whole:
                o_ref[0] = acc_ref[...].astype(bf16)
            else:
                _, kk, nn = out_block
                for s in range(N_CHIPS):
                    o_ref[s] = (acc_ref[s * kk:(s + 1) * kk, :] if shard_axis == 0
                                else acc_ref[:, s * nn:(s + 1) * nn]).astype(bf16)

    (grad,), carried = _call(
        body, phase, name=name, grid=(1 if whole else N_CHIPS, nt), in_specs=[a_spec, b_spec], out_specs=[out_spec],
        out_shape=[jax.ShapeDtypeStruct(out_block, bf16)], scratch_shapes=[pltpu.VMEM((K, N), f32)], args=(a, b))
    return grad, carried


def _adamw_sum(place, groups, name):
    n = len(groups)
    R, C = groups[0][0].shape
    half = R // 2
    tr = _tile(half, max(16, (1 << 18) // C // 16 * 16), 16)
    nr = half // tr

    deep = dict(pipeline_mode=pl.Buffered(3)) if n <= 2 else {}

    def body(p_ref, *refs):
        def stream_half(h):
            own = p_ref[0] == h

            def step(*blk):
                for a in range(n):
                    w_ref, m_ref, v_ref, part_ref, fc_ref, fs_ref = blk[6 * a:6 * a + 6]
                    g_ref, d_ref, mo_ref, vo_ref = blk[6 * n + 4 * a:6 * n + 4 * a + 4]
                    up = lambda x: x.astype(f32)
                    mine = ((up(part_ref[0]) + up(fc_ref[0])) + up(fc_ref[1])) + up(fc_ref[2])
                    sibs = ((up(fs_ref[0]) + up(fs_ref[1])) + up(fs_ref[2])) + up(fs_ref[3])
                    g_ = jnp.where(own, mine, sibs)
                    m_ = ADAM_B1 * m_ref[...] + (1.0 - ADAM_B1) * g_
                    v_ = ADAM_B2 * v_ref[...] + (1.0 - ADAM_B2) * (g_ * g_)
                    m_hat = m_ / (1.0 - ADAM_B1 ** ADAM_STEP)
                    v_hat = v_ / (1.0 - ADAM_B2 ** ADAM_STEP)
                    g_ref[...] = g_
                    d_ref[...] = -ADAM_LR * (m_hat / (jnp.sqrt(v_hat) + ADAM_EPS) + ADAM_WD * w_ref[...])
                    mo_ref[...] = m_
                    vo_ref[...] = v_

            rows = lambda r: (h * nr + r, 0)
            one = [pl.BlockSpec((tr, C), rows, **deep)] * 3 + [
                pl.BlockSpec((1, tr, C), lambda r: (p_ref[1], jnp.where(own, r, 0), 0), **deep),
                pl.BlockSpec((3, tr, C), lambda r: (0, jnp.where(own, r, 0), 0), **deep),
                pl.BlockSpec((4, tr, C), lambda r: (0, jnp.where(own, 0, r), 0), **deep)]
            pltpu.emit_pipeline(step, grid=(nr,), in_specs=one * n,
                                out_specs=[pl.BlockSpec((tr, C), rows)] * (4 * n))(*refs)

        stream_half(0)
        stream_half(1)

    anywhere = pl.BlockSpec(memory_space=pl.ANY)
    res = pl.pallas_call(
        body, name=name,
        in_specs=[pl.BlockSpec(memory_space=pltpu.SMEM)] + [anywhere] * (6 * n), out_specs=[anywhere] * (4 * n),
        out_shape=[jax.ShapeDtypeStruct((R, C), f32)] * (4 * n),
        compiler_params=pltpu.CompilerParams(vmem_limit_bytes=VMEM_LIMIT),
    )(place, *[x for g in groups for x in g])
    return [tuple(res[4 * a:4 * a + 4]) for a in range(n)]


def _adamw(w, g, m, v, name):
    R, C = w.shape
    tr = _tile(R, max(8, (1 << 18) // C // 8 * 8), 8)

    def body(w_ref, g_ref, m_ref, v_ref, d_ref, mo_ref, vo_ref):
        g_ = g_ref[...]
        m_ = ADAM_B1 * m_ref[...] + (1.0 - ADAM_B1) * g_
        v_ = ADAM_B2 * v_ref[...] + (1.0 - ADAM_B2) * (g_ * g_)
        m_hat = m_ / (1.0 - ADAM_B1 ** ADAM_STEP)
        v_hat = v_ / (1.0 - ADAM_B2 ** ADAM_STEP)
        d_ref[...] = -ADAM_LR * (m_hat / (jnp.sqrt(v_hat) + ADAM_EPS) + ADAM_WD * w_ref[...])
        mo_ref[...] = m_
        vo_ref[...] = v_

    spec = pl.BlockSpec((tr, C), lambda i: (i, 0))
    return pl.pallas_call(
        body, name=name, grid=(R // tr,), in_specs=[spec] * 4, out_specs=[spec] * 3,
        out_shape=[jax.ShapeDtypeStruct((R, C), f32)] * 3,
        compiler_params=_params(("parallel",)),
    )(w, g, m, v)


def _place():
    return lax.axis_index("x"), lax.axis_index("y"), lax.axis_index("c")


def _other_chips(x, y):
    chips = [(1 - x, y), (x, 1 - y), (1 - x, 1 - y)]
    return chips, [2 * cx + cy for cx, cy in chips]


def _spread_phase(blk):
    def peers():
        x, y, c = _place()
        return [tuple(1 - p if (k >> s) & 1 else p for p, s in ((x, 2), (y, 1), (c, 0))) for k in range(1, N_DEV)]

    def copies(pin, out):
        x, y, c = _place()
        mine = out[0].at[4 * x + 2 * y + c]
        return [(mine, mine, peer) for peer in peers()]

    stack = jnp.broadcast_to(blk, (N_DEV,) + blk.shape)
    return _Phase([stack], [jax.ShapeDtypeStruct(stack.shape, stack.dtype)], {0: 0}, N_DEV - 1, copies,
                  lambda pin, out: [out[0].at[4 * px + 2 * py + pc] for px, py, pc in peers()])


def _sum_slots(stack, name):
    def body(s_ref, o_ref):
        tot = s_ref[0]
        for d in range(1, stack.shape[0]):
            tot = tot + s_ref[d]
        o_ref[...] = tot

    vm = pl.BlockSpec(memory_space=pltpu.VMEM)
    return pl.pallas_call(body, name=name, in_specs=[vm], out_specs=vm,
                          out_shape=jax.ShapeDtypeStruct(stack.shape[1:], stack.dtype))(stack)


def _cast_shards(place, ws, name):
    n = len(ws)
    R, C = ws[0].shape
    tr = _tile(R, max(16, (1 << 19) // C // 16 * 16), 16)

    def body(p_ref, *refs):
        for a in range(n):
            refs[n + a][0] = refs[a][...].astype(bf16)

    return pl.pallas_call(
        body, name=name,
        grid_spec=pltpu.PrefetchScalarGridSpec(
            num_scalar_prefetch=1, grid=(R // tr,),
            in_specs=[pl.BlockSpec((tr, C), lambda r, p: (r, 0))] * n,
            out_specs=[pl.BlockSpec((1, tr, C), lambda r, p: (p[1], r, 0))] * n),
        out_shape=[jax.ShapeDtypeStruct((N_CHIPS, R, C), bf16)] * n,
        compiler_params=_params(("parallel",)),
    )(place, *ws)


class _Phase:
    def __init__(self, arrays, out_shapes, aliases, n_copies, copies, arrivals, own_starts=(), own_waits=()):
        self.arrays, self.out_shapes, self.aliases = list(arrays), list(out_shapes), dict(aliases)
        self.n_copies, self.copies, self.arrivals = n_copies, copies, arrivals
        self.own_starts, self.own_waits = tuple(own_starts), tuple(own_waits)

    def sems(self):
        return [pltpu.SemaphoreType.DMA((self.n_copies,)), pltpu.SemaphoreType.DMA((self.n_copies,))]

    def _descriptors(self, pin, pout, send_sems, recv_sems):
        return [pltpu.make_async_remote_copy(src_ref=s, dst_ref=d, send_sem=send_sems.at[i], recv_sem=recv_sems.at[i],
                                             device_id=to, device_id_type=MESH)
                for i, (s, d, to) in enumerate(self.copies(pin, pout))]

    def _arrival(self, i, pin, pout, send_sems, recv_sems):
        dst = self.arrivals(pin, pout)[i]
        return pltpu.make_async_remote_copy(src_ref=dst, dst_ref=dst, send_sem=send_sems.at[i], recv_sem=recv_sems.at[i],
                                            device_id=_place(), device_id_type=MESH)

    def start(self, pin, pout, send_sems, recv_sems):
        for i, cp in enumerate(self._descriptors(pin, pout, send_sems, recv_sems)):
            if i not in self.own_starts:
                cp.start()

    def begin(self, i, pin, pout, send_sems, recv_sems):
        self._descriptors(pin, pout, send_sems, recv_sems)[i].start()

    def arrived(self, i, pin, pout, send_sems, recv_sems):
        self._arrival(i, pin, pout, send_sems, recv_sems).wait_recv()

    def finish(self, pin, pout, send_sems, recv_sems):
        for i in range(self.n_copies):
            if i not in self.own_waits:
                self._arrival(i, pin, pout, send_sems, recv_sems).wait_recv()
        for cp in self._descriptors(pin, pout, send_sems, recv_sems):
            cp.wait_send()


def _join(phases):
    if len(phases) == 1:
        return phases[0]
    ai = np.cumsum([0] + [len(p.arrays) for p in phases])
    oi = np.cumsum([0] + [len(p.out_shapes) for p in phases])

    def each(fn_name, pin, pout):
        return [item for k, p in enumerate(phases)
                for item in getattr(p, fn_name)(pin[ai[k]:ai[k + 1]], pout[oi[k]:oi[k + 1]])]

    aliases = {int(ai[k]) + i: int(oi[k]) + j for k, p in enumerate(phases) for i, j in p.aliases.items()}
    ci = np.cumsum([0] + [p.n_copies for p in phases])
    shifted = lambda attr: [int(ci[k]) + i for k, p in enumerate(phases) for i in getattr(p, attr)]
    return _Phase([a for p in phases for a in p.arrays], [s for p in phases for s in p.out_shapes], aliases,
                  int(ci[-1]), functools.partial(each, "copies"), functools.partial(each, "arrivals"),
                  shifted("own_starts"), shifted("own_waits"))


def _call(body, phase, *, name, grid, in_specs, out_specs, out_shape, scratch_shapes, args, prefetch=(), expose=False,
          aliases=None):
    seq = _params(("arbitrary",) * len(grid))
    np_ = len(prefetch)
    own = {np_ + i: j for i, j in (aliases or {}).items()}
    if phase is None:
        spec = pltpu.PrefetchScalarGridSpec(num_scalar_prefetch=np_, grid=grid, in_specs=in_specs, out_specs=out_specs,
                                            scratch_shapes=scratch_shapes)
        res = pl.pallas_call(body, name=name, grid_spec=spec, out_shape=out_shape, input_output_aliases=own,
                             compiler_params=seq)(*prefetch, *args)
        return list(res), []
    ni, no, ns = len(in_specs), len(out_specs), len(scratch_shapes)
    pi, po = len(phase.arrays), len(phase.out_shapes)

    def hosted(*refs):
        cut = np.cumsum([np_, ni, pi, no, po, ns])
        pre, ins, pin, outs, pout, scr, sems = (refs[a:b] for a, b in zip([0, *cut], [*cut, len(refs)]))
        ids = [pl.program_id(d) for d in range(len(grid))]
        first = functools.reduce(lambda p, q: p & q, [i == 0 for i in ids])
        last = functools.reduce(lambda p, q: p & q, [i == g - 1 for i, g in zip(ids, grid)])
        pl.when(first)(lambda: phase.start(pin, pout, *sems))
        body(*pre, *ins, *outs, *scr, **({"carried": (pin, pout, sems)} if expose else {}))
        pl.when(last)(lambda: phase.finish(pin, pout, *sems))

    anyspace = pl.BlockSpec(memory_space=pl.ANY)
    spec = pltpu.PrefetchScalarGridSpec(
        num_scalar_prefetch=np_, grid=grid, in_specs=list(in_specs) + [anyspace] * pi,
        out_specs=list(out_specs) + [anyspace] * po, scratch_shapes=list(scratch_shapes) + phase.sems())
    res = pl.pallas_call(
        hosted, name=name, grid_spec=spec, out_shape=list(out_shape) + phase.out_shapes,
        input_output_aliases={**own, **{np_ + ni + i: no + j for i, j in phase.aliases.items()}}, compiler_params=seq,
    )(*prefetch, *args, *phase.arrays)
    return list(res[:no]), list(res[no:])


def _run_phases(name, phases):
    first = phases[0]
    pi, po = len(first.arrays), len(first.out_shapes)

    def body(*refs):
        pin, pout, sems = refs[:pi], refs[pi:pi + po], refs[pi + po:]
        for n, ph in enumerate(phases):
            ph.start(pin, pout, *sems[2 * n:2 * n + 2])
            ph.finish(pin, pout, *sems[2 * n:2 * n + 2])

    anyspace = pl.BlockSpec(memory_space=pl.ANY)
    return list(pl.pallas_call(
        body, name=name, in_specs=[anyspace] * pi, out_specs=[anyspace] * po, out_shape=first.out_shapes,
        input_output_aliases=first.aliases, scratch_shapes=[s for ph in phases for s in ph.sems()],
    )(*first.arrays))


def _half_rows(buf, c):
    half = buf.shape[1] // 2
    return pl.ds(c * half, half), pl.ds((1 - c) * half, half)


def _gather_phase(bufs, over_ici):
    n = len(bufs)
    shapes = [jax.ShapeDtypeStruct(b.shape, b.dtype) for b in bufs]

    def landed(out, which):
        x, y, c = _place()
        _, ks = _other_chips(x, y)
        return [out[a].at[ks[j], _half_rows(bufs[a], c)[which]] for a in range(n) for j in range(3)]

    def ici(pin, out):
        x, y, c = _place()
        chips, _ = _other_chips(x, y)
        mine = [out[a].at[2 * x + y, _half_rows(bufs[a], c)[0]] for a in range(n)]
        return [(mine[a], mine[a], (*chips[j], c)) for a in range(n) for j in range(3)]

    def d2d(pin, out):
        x, y, c = _place()
        return [(dst, dst, (x, y, 1 - c)) for dst in landed(out, 0)]

    if over_ici:
        return _Phase(bufs, shapes, {a: a for a in range(n)}, 3 * n, ici, lambda pin, out: landed(out, 0))
    return _Phase(bufs, shapes, {a: a for a in range(n)}, 3 * n, d2d, lambda pin, out: landed(out, 1))


def _feed_phase(buf):
    def chips():
        x, y, _ = _place()
        return [(x if f < 2 else 1 - x, y if f % 2 == 0 else 1 - y) for f in (1, 2, 3)]

    def copies(pin, out):
        x, y, c = _place()
        mine = _half_rows(buf, c)[0]
        own = out[0].at[2 * x + y, mine]
        sent = [(own, own, (cx, cy, c)) for cx, cy in chips()]
        return sent + [(out[0].at[2 * cx + cy, mine], out[0].at[2 * cx + cy, mine], (x, y, 1 - c)) for cx, cy in chips()]

    def arrivals(pin, out):
        mine, theirs = _half_rows(buf, _place()[2])
        return [out[0].at[2 * cx + cy, rows] for rows in (mine, theirs) for cx, cy in chips()]

    return _Phase([buf], [jax.ShapeDtypeStruct(buf.shape, buf.dtype)], {0: 0}, 6, copies, arrivals,
                  own_starts=(3, 4, 5), own_waits=range(6))


def _rs_swap_phase(grads):
    n = len(grads)

    def copies(g, out):
        x, y, c = _place()
        return [(g[a].at[:, _half_rows(grads[a], c)[1]], out[a], (x, y, 1 - c)) for a in range(n)]

    shapes = [jax.ShapeDtypeStruct((N_CHIPS, g.shape[1] // 2, g.shape[2]), g.dtype) for g in grads]
    return _Phase(grads, shapes, {}, n, copies, lambda g, out: list(out))


def _rs_add_sibling(place, grads, gots, name):
    n = len(grads)
    _, R, C = grads[0].shape
    half = R // 2
    tr = _tile(half, max(16, (1 << 19) // C // 16 * 16), 16)
    nr = half // tr

    def body(p_ref, *refs):
        for a in range(n):
            refs[2 * n + a][...] = (refs[2 * a][...].astype(f32) + refs[2 * a + 1][...].astype(f32)).astype(bf16)

    res = pl.pallas_call(
        body, name=name,
        grid_spec=pltpu.PrefetchScalarGridSpec(
            num_scalar_prefetch=1, grid=(N_CHIPS, nr),
            in_specs=[pl.BlockSpec((1, tr, C), lambda k, r, p: (k, p[0] * nr + r, 0)),
                      pl.BlockSpec((1, tr, C), lambda k, r, p: (k, r, 0))] * n,
            out_specs=[pl.BlockSpec((1, tr, C), lambda k, r, p: (k, r, 0))] * n),
        out_shape=[jax.ShapeDtypeStruct((N_CHIPS, half, C), bf16)] * n,
        compiler_params=_params(("parallel", "parallel")),
    )(place, *[x for pair in zip(grads, gots) for x in pair])
    return list(res)


def _rs_chips_phase(parts):
    n = len(parts)

    def copies(p, fc):
        x, y, c = _place()
        chips, ks = _other_chips(x, y)
        return [(p[a].at[ks[j]], fc[a].at[j], (*chips[j], c)) for a in range(n) for j in range(3)]

    shapes = [jax.ShapeDtypeStruct((3,) + q.shape[1:], q.dtype) for q in parts]
    return _Phase(parts, shapes, {}, 3 * n, copies, lambda p, fc: [fc[a].at[j] for a in range(n) for j in range(3)])


def _rs_hand_phase(parts, from_chips):
    n = len(parts)

    def copies(pin, fs):
        x, y, c = _place()
        sib = (x, y, 1 - c)
        own = [(pin[a].at[2 * x + y], fs[a].at[0], sib) for a in range(n)]
        return own + [(pin[n + a].at[j], fs[a].at[1 + j], sib) for a in range(n) for j in range(3)]

    def arrivals(pin, fs):
        return [fs[a].at[0] for a in range(n)] + [fs[a].at[1 + j] for a in range(n) for j in range(3)]

    shapes = [jax.ShapeDtypeStruct((4,) + q.shape[1:], q.dtype) for q in parts]
    return _Phase(list(parts) + list(from_chips), shapes, {}, 4 * n, copies, arrivals)


class _Exchange:
    def __init__(self, place):
        self.place = place

    def feed(self, buf):
        return _feed_phase(buf)

    def gather(self, bufs, over_ici):
        return _gather_phase(bufs, over_ici)

    def swap(self, grads):
        return _rs_swap_phase(grads)

    def pair_sums(self, names, grads):
        return self.add(names, grads, _run_phases("rs_sibling_" + names[0], [_rs_swap_phase(grads)]))

    def add(self, names, grads, got):
        parts = {}
        for group in _same_shape(grads):
            res = _rs_add_sibling(self.place, [grads[i] for i in group], [got[i] for i in group],
                                  "rs_add_" + names[group[0]])
            parts.update(zip(group, res))
        return [parts[i] for i in range(len(names))]

    def to_chips(self, parts):
        return _rs_chips_phase(parts)

    def to_sibling(self, parts, from_chips):
        return _rs_hand_phase(parts, from_chips)

    def spread(self, blk):
        return _spread_phase(blk)

    def hand_over(self, name, parts, from_chips, blk):
        got = _run_phases(name, [_join([_rs_hand_phase(parts, from_chips), _spread_phase(blk)])])
        return got[:-1], got[-1]


def _local_step(place, x, target, norm_mix, b_gate, rb_chip, norm_ffn, norm_final, w_in, rest, exch):
    B, S, D = x.shape
    T = B * S
    x2 = x.reshape(T, D)
    tg2 = target.reshape(T, D)
    rope, decay = _rope_tables(S), _decay_tables()
    g_fin = norm_final.reshape(1, D)
    nrel = rb_chip.shape[-1]

    mrg, ffn = ["w_ret_out", "w_att_out", "w_out"], ["w_ffn_gate", "w_ffn_up", "w_ffn_down"]
    (xn, proj), got = _in_proj(place, x2, norm_mix, _join([exch.feed(w_in), exch.gather([rest[n] for n in mrg], True),
                                                           exch.spread(jnp.pad(rb_chip, ((0, 0), (0, 128 - nrel))))]))
    w_in, wb, rb_all = got[0], {}, got.pop()
    trows = _bias_rows(jnp.concatenate([rb_all[2 * k, :, :nrel] for k in range(N_CHIPS)], axis=1))
    (qr, kr, o, u, states), got = _ret_fwd(proj, B, S, rope, decay, _join([exch.gather([rest["w_ffn_gate"]], True),
                                                                         exch.gather(got[1:], False)]))
    wb.update(zip(mrg, got[1:]))
    (ao, probs), got = _att_fwd(proj, trows, B, S, _join([exch.gather([rest["w_ffn_up"], rest["w_ffn_down"]], True),
                                                    exch.gather(got[:1], False)]))
    wb["w_ffn_gate"] = got[2]
    w_ro, w_out = wb["w_ret_out"].reshape(-1, D), wb["w_out"].reshape(-1, D)
    (h1, yr, ya), got = _mix_fwd(x2, proj, u, ao, b_gate, w_ro, wb["w_att_out"], w_out, exch.gather(got[:2], False))
    wb.update(zip(ffn[1:], got))
    hn, a, b, f, dh2, dh2b, part_fin = _ffn_fwd(h1, norm_ffn, wb["w_ffn_gate"], wb["w_ffn_up"], wb["w_ffn_down"], g_fin, tg2)

    da, db, dh1, dh1b, part_ffn = _ffn_bwd(dh2, h1, norm_ffn, a, b, wb["w_ffn_gate"], wb["w_ffn_up"], wb["w_ffn_down"])
    ffn = ["w_ffn_down", "w_ffn_gate", "w_ffn_up"]
    g_ffn = [_wgrad(f, dh2b, 0, "wgrad_ffn_down")[0], _wgrad(da, hn, 0, "wgrad_ffn_gate")[0],
             _wgrad(db, hn, 0, "wgrad_ffn_up")[0]]
    (du, dao, dgl, mix, dyr, dya, part_bg), x_ffn = _mix_bwd(dh1b, proj, yr, ya, b_gate, w_ro, wb["w_att_out"], w_out,
                                                             exch.swap(g_ffn))
    p_ffn = exch.add(ffn, g_ffn, x_ffn)
    mrg = ["w_out", "w_ret_out", "w_att_out"]
    g_mrg = [_wgrad(mix, dh1b, 0, "wgrad_out")[0], _wgrad(u, dyr, 0, "wgrad_ret_out")[0],
             _wgrad(ao, dya, 1, "wgrad_att_out")[0]]
    (dproj,), got = _ret_bwd(proj, qr, kr, o, states, du, dgl, B, S, rope, decay, _join([exch.to_chips(p_ffn[:2]),
                                                                                          exch.swap(g_mrg)]))
    c_two, p_mrg = got[:2], exch.add(mrg, g_mrg, got[2:])
    (dproj, dvec), got = _att_bwd(proj, dao, probs, dproj, B, S, exch.to_chips(p_ffn[2:] + p_mrg))
    c_ffn, c_mrg = c_two + got[:1], got[1:]
    g_in, got = _wgrad(xn, dproj, 1, "wgrad_in", exch.to_sibling(p_ffn + p_mrg, c_ffn + c_mrg))
    s_ffn, s_mrg = got[:len(ffn)], got[len(ffn):]
    p_in = exch.pair_sums(["w_in"], [g_in])
    (gx, part_mix), c_in = _in_proj_bwd(dproj, w_in, x2, norm_mix, dh1, exch.to_chips(p_in))
    rows = lambda p, r: p.reshape(-1, 8, p.shape[-1])[:, r, :].sum(axis=0)
    lo = KWIN - 1 - (MAX_REL - 1)
    drb = jnp.concatenate([jnp.flip(dvec[:, lo:lo + N_REL - 1], axis=1), dvec[:, :lo].sum(axis=1, keepdims=True)], axis=1)
    gsmall = {
        "norm_mix": rows(part_mix, 0), "b_gate": rows(part_bg, 0), "rel_bias": drb, "norm_ffn": rows(part_ffn, 0),
        "norm_final": rows(part_fin, 0),
    }
    s_in, small_all = exch.hand_over("rs_hand_w_in", p_in, c_in, _pack_small(gsmall, rows(part_fin, 1)))
    gbig = dict(zip(ffn + mrg + ["w_in"], zip(p_ffn + p_mrg + p_in, c_ffn + c_mrg + c_in, s_ffn + s_mrg + s_in)))
    return gx.reshape(B, S, D), gbig, small_all


SMALL_ROWS = 16


def _pack_small(gs, loss_lanes):
    D = D_MODEL
    rb = jnp.pad(gs["rel_bias"].reshape(-1), (0, 3 * D - ATT_HEADS * N_REL)).reshape(3, D)
    rows = [gs["norm_mix"].reshape(1, D), gs["b_gate"].reshape(2, D), gs["norm_ffn"].reshape(1, D),
            gs["norm_final"].reshape(1, D), rb, loss_lanes.reshape(1, D)]
    used = sum(r.shape[0] for r in rows)
    return jnp.concatenate(rows + [jnp.zeros((SMALL_ROWS - used, D), f32)], axis=0)


def kernel(x, norm_mix, w_in, b_gate, rel_bias, w_ret_out, w_att_out, w_out, norm_ffn, w_ffn_gate, w_ffn_up, w_ffn_down, norm_final, loss_target, m_norm_mix, m_w_in, m_b_gate, m_rel_bias, m_w_ret_out, m_w_att_out, m_w_out, m_norm_ffn, m_w_ffn_gate, m_w_ffn_up, m_w_ffn_down, m_norm_final, v_norm_mix, v_w_in, v_b_gate, v_rel_bias, v_w_ret_out, v_w_att_out, v_w_out, v_norm_ffn, v_w_ffn_gate, v_w_ffn_up, v_w_ffn_down, v_norm_final):
    w = dict(norm_mix=norm_mix, w_in=w_in, b_gate=b_gate, rel_bias=rel_bias, w_ret_out=w_ret_out, w_att_out=w_att_out,
             w_out=w_out, norm_ffn=norm_ffn, w_ffn_gate=w_ffn_gate, w_ffn_up=w_ffn_up, w_ffn_down=w_ffn_down,
             norm_final=norm_final)
    m = dict(norm_mix=m_norm_mix, w_in=m_w_in, b_gate=m_b_gate, rel_bias=m_rel_bias, w_ret_out=m_w_ret_out,
             w_att_out=m_w_att_out, w_out=m_w_out, norm_ffn=m_norm_ffn, w_ffn_gate=m_w_ffn_gate, w_ffn_up=m_w_ffn_up,
             w_ffn_down=m_w_ffn_down, norm_final=m_norm_final)
    v = dict(norm_mix=v_norm_mix, w_in=v_w_in, b_gate=v_b_gate, rel_bias=v_rel_bias, w_ret_out=v_w_ret_out,
             w_att_out=v_w_att_out, w_out=v_w_out, norm_ffn=v_norm_ffn, w_ffn_gate=v_w_ffn_gate, w_ffn_up=v_w_ffn_up,
             w_ffn_down=v_w_ffn_down, norm_final=v_norm_final)
    xi, yi, ci = _place()
    k_me = 2 * xi + yi

    place = jnp.stack([ci, k_me]).astype(jnp.int32)
    big = [n for n, _ in BIG]

    turned = ("w_ffn_gate", "w_ffn_up")
    shard = lambda d, n: jnp.swapaxes(d[n][0], 0, 1) if n in turned else d[n][0]
    whole = lambda a, n: (jnp.swapaxes(a, 0, 1) if n in turned else a)[None]

    by_shape = [[big[i] for i in group] for group in _same_shape([shard(w, n) for n in big])]
    bufs = {}
    for names in by_shape:
        bufs.update(zip(names, _cast_shards(place, [shard(w, n) for n in names], "cast_" + names[0])))
    rest = {n: bufs[n] for n in big if n != "w_in"}
    nrel_loc = rel_bias.shape[-1]
    grad_x, gbig, small_all = _local_step(place, x, loss_target, norm_mix, b_gate, rel_bias[0], norm_ffn, norm_final,
                                          bufs["w_in"], rest, _Exchange(place))

    small = _sum_slots(small_all, "reduce_small")
    D = D_MODEL
    loss = jnp.sum(small[8])
    drb_full = small[5:8].reshape(-1)[:ATT_HEADS * N_REL].reshape(ATT_HEADS, N_REL)
    g = {
        "norm_mix": small[0:1], "b_gate": small[1:3].reshape(1, 2 * D), "norm_ffn": small[3:4], "norm_final": small[4],
        "rel_bias": lax.dynamic_slice_in_dim(drb_full, k_me * nrel_loc, nrel_loc, axis=1)[None],
    }

    delta, new_m, new_v = {}, {}, {}
    for names in by_shape:
        res = _adamw_sum(place, [(shard(w, n), shard(m, n), shard(v, n), *gbig[n]) for n in names], "adamw_" + names[0])
        for n, (g_, d_, m_, v_) in zip(names, res):
            g[n], delta[n], new_m[n], new_v[n] = whole(g_, n), whole(d_, n), whole(m_, n), whole(v_, n)
    flat = lambda d: jnp.concatenate([d[n].reshape(-1) for n in SMALL])
    n_small = sum(int(np.prod(w[n].shape)) for n in SMALL)
    n_pad = -n_small % 1024
    packs = [jnp.pad(flat(d), (0, n_pad)).reshape(-1, 128) for d in (w, g, m, v)]
    outs = _adamw(*packs, "adamw_small")
    for res, dst in zip(outs, (delta, new_m, new_v)):
        off = 0
        fl = res.reshape(-1)
        for n in SMALL:
            sz = int(np.prod(w[n].shape))
            dst[n] = fl[off:off + sz].reshape(w[n].shape)
            off += sz

    return (loss, grad_x, *[g[n] for n in WEIGHTS], *[delta[n] for n in WEIGHTS], *[new_m[n] for n in WEIGHTS],
            *[new_v[n] for n in WEIGHTS])
```
